```python
import math
import jax, jax.numpy as jnp
from jax import lax
import numpy as np

D_MODEL = 1024
BATCH = 8
SEQ = 8192
DEPTH = 1

N_META = 16
CHUNK = 64
GDN_HEADS = 8
GDN_DK = 128
GDN_DV = 128
GDN_CONV = 4
RET_HEADS = 4
RET_DK = 256
RET_DV = 256
D_FF = 2816
FFN_CONV = 3
ROPE_BASE = 10000.0
EPS = 1e-6

GDN_QK = GDN_HEADS * GDN_DK
GDN_V = GDN_HEADS * GDN_DV
RET_QK = RET_HEADS * RET_DK
RET_V = RET_HEADS * RET_DV
IN_WIDTHS = (GDN_QK, GDN_QK, GDN_V, GDN_V, GDN_HEADS, GDN_HEADS,
             RET_QK, RET_QK, RET_V, RET_V, D_MODEL, D_MODEL)
D_IN = sum(IN_WIDTHS)
SPLIT_IDX = tuple(int(i) for i in np.cumsum(IN_WIDTHS)[:-1])

kernel_name = 'hybrid_gdn_retention_convffn_meta'


def rmsnorm(x, g):
    xf = x.astype(jnp.float32)
    y = xf * lax.rsqrt(jnp.mean(xf * xf, axis=-1, keepdims=True) + EPS)
    return (y * g.astype(jnp.float32)).astype(x.dtype)


def causal_dwconv(x, w):
    K, C = w.shape
    return lax.conv_general_dilated(
        x, w[:, None, :].astype(x.dtype), window_strides=(1,), padding=[(K - 1, 0)],
        dimension_numbers=('NWC', 'WIO', 'NWC'), feature_group_count=C)


def l2norm(x):
    return x * lax.rsqrt(jnp.sum(x * x, axis=-1, keepdims=True) + EPS)


def rotary(x, pos):
    half = x.shape[-1] // 2
    inv = 1.0 / (ROPE_BASE ** (jnp.arange(half, dtype=jnp.float32) / half))
    ang = pos[:, None] * inv[None, :]
    cos = jnp.cos(ang)[None, :, None, :]
    sin = jnp.sin(ang)[None, :, None, :]
    x1, x2 = x[..., :half], x[..., half:]
    return jnp.concatenate([x1 * cos - x2 * sin, x2 * cos + x1 * sin], axis=-1)


def to_chunks(t, pad):
    widths = [(0, 0), (pad, 0)] + [(0, 0)] * (t.ndim - 2)
    t = jnp.pad(t, widths)
    B, Lp = t.shape[0], t.shape[1]
    t = t.reshape((B, Lp // CHUNK, CHUNK) + t.shape[2:])
    perm = (0, 3, 1, 2) + tuple(range(4, t.ndim))
    return t.transpose(perm)


def from_chunks(o, pad):
    N, B, H, C, d = o.shape
    o = o.transpose(1, 0, 3, 2, 4).reshape(B, N * C, H, d)
    return o[:, pad:]


def gated_delta_chunked(q, k, v, log_g, beta):
    L = q.shape[1]
    pad = (-L) % CHUNK
    dk = q.shape[-1]
    q = to_chunks(q * (dk ** -0.5), pad)
    k = to_chunks(k, pad)
    v = to_chunks(v, pad)
    beta = to_chunks(beta, pad)
    G = jnp.cumsum(to_chunks(log_g, pad), axis=-1)
    tril = jnp.asarray(np.tril(np.ones((CHUNK, CHUNK), dtype=bool)))
    strict = jnp.asarray(np.tril(np.ones((CHUNK, CHUNK), dtype=bool), -1))
    diff = G[..., :, None] - G[..., None, :]
    decay = jnp.where(tril, jnp.exp(jnp.where(tril, diff, 0.0)), 0.0)
    kb = k * beta[..., None]
    M = jnp.where(strict, jnp.einsum('bhncd,bhnsd->bhncs', kb, k) * decay, 0.0)
    eye = jnp.eye(CHUNK, dtype=M.dtype)
    T = lax.linalg.triangular_solve(eye + M, jnp.broadcast_to(eye, M.shape),
                                    left_side=True, lower=True, unit_diagonal=True)
    u = jnp.einsum('bhncs,bhnsd->bhncd', T, v * beta[..., None])
    w = jnp.einsum('bhncs,bhnsd->bhncd', T, kb * jnp.exp(G)[..., None])
    attn = jnp.einsum('bhncd,bhnsd->bhncs', q, k) * decay
    q_dec = q * jnp.exp(G)[..., None]
    G_last = G[..., -1]
    k_dec = k * jnp.exp(G_last[..., None] - G)[..., None]
    g_last = jnp.exp(G_last)

    def step(S, xs):
        u_c, w_c, a_c, qd_c, kd_c, gl_c = xs
        v_new = u_c - jnp.einsum('bhcd,bhde->bhce', w_c, S)
        o_c = jnp.einsum('bhcd,bhde->bhce', qd_c, S) + jnp.einsum('bhcs,bhse->bhce', a_c, v_new)
        S = S * gl_c[..., None, None] + jnp.einsum('bhcd,bhce->bhde', kd_c, v_new)
        return S, o_c

    xs = tuple(jnp.moveaxis(t, 2, 0) for t in (u, w, attn, q_dec, k_dec, g_last))
    B, H = q.shape[0], q.shape[1]
    S0 = jnp.zeros((B, H, dk, v.shape[-1]), jnp.float32)
    _, o = lax.scan(step, S0, xs)
    return from_chunks(o, pad)


def retention_chunked(q, k, v):
    H = q.shape[2]
    dk = q.shape[-1]
    lg = jnp.log(1.0 - 2.0 ** (-5.0 - jnp.arange(H, dtype=jnp.float32)))
    L = q.shape[1]
    pad = (-L) % CHUNK
    q = to_chunks(q, pad)
    k = to_chunks(k * (dk ** -0.5), pad)
    v = to_chunks(v, pad)
    idx = jnp.arange(CHUNK, dtype=jnp.float32)
    tril = jnp.asarray(np.tril(np.ones((CHUNK, CHUNK), dtype=bool)))
    dmask = jnp.where(tril, jnp.exp((idx[:, None] - idx[None, :]) * lg[:, None, None]), 0.0)
    inner = jnp.einsum('bhncs,bhnse->bhnce',
                       jnp.einsum('bhncd,bhnsd->bhncs', q, k) * dmask[None, :, None], v)
    q_dec = q * jnp.exp((idx[None, :] + 1.0) * lg[:, None])[None, :, None, :, None]
    k_dec = k * jnp.exp((CHUNK - 1.0 - idx[None, :]) * lg[:, None])[None, :, None, :, None]
    g_chunk = jnp.exp(CHUNK * lg)[None, :, None, None]

    def step(S, xs):
        in_c, qd_c, kd_c, v_c = xs
        o_c = in_c + jnp.einsum('bhcd,bhde->bhce', qd_c, S)
        S = S * g_chunk + jnp.einsum('bhcd,bhce->bhde', kd_c, v_c)
        return S, o_c

    xs = tuple(jnp.moveaxis(t, 2, 0) for t in (inner, q_dec, k_dec, v))
    S0 = jnp.zeros((q.shape[0], H, dk, v.shape[-1]), jnp.float32)
    _, o = lax.scan(step, S0, xs)
    return from_chunks(o, pad)


def mixer_block(hn, pos, w_in, conv_w, a_log, dt_bias, gdn_norm, w_out):
    B, L, _ = hn.shape
    f32 = jnp.float32
    proj = hn @ w_in.astype(hn.dtype)
    gq, gk, gv, gz, ga, gb, rq, rk, rv, rg, gate_a, gate_b = jnp.split(proj, SPLIT_IDX, axis=-1)
    qkv = jax.nn.silu(causal_dwconv(jnp.concatenate([gq, gk, gv], axis=-1), conv_w))
    gq, gk, gv = jnp.split(qkv.astype(f32), (GDN_QK, 2 * GDN_QK), axis=-1)
    gq = l2norm(gq.reshape(B, L, GDN_HEADS, GDN_DK))
    gk = l2norm(gk.reshape(B, L, GDN_HEADS, GDN_DK))
    gv = gv.reshape(B, L, GDN_HEADS, GDN_DV)
    beta = jax.nn.sigmoid(gb.astype(f32))
    log_g = -jnp.exp(a_log.astype(f32)) * jax.nn.softplus(ga.astype(f32) + dt_bias.astype(f32))
    o_a = gated_delta_chunked(gq, gk, gv, log_g, beta)
    o_a = o_a * lax.rsqrt(jnp.mean(o_a * o_a, axis=-1, keepdims=True) + EPS) * gdn_norm.astype(f32)
    y_a = (o_a * jax.nn.silu(gz.astype(f32).reshape(B, L, GDN_HEADS, GDN_DV))).reshape(B, L, GDN_V)
    rq = rotary(rq.astype(f32).reshape(B, L, RET_HEADS, RET_DK), pos)
    rk = rotary(rk.astype(f32).reshape(B, L, RET_HEADS, RET_DK), pos)
    o_b = retention_chunked(rq, rk, rv.astype(f32).reshape(B, L, RET_HEADS, RET_DV))
    o_b = o_b * lax.rsqrt(jnp.mean(o_b * o_b, axis=-1, keepdims=True) + EPS)
    y_b = jax.nn.silu(rg.astype(f32)) * o_b.reshape(B, L, RET_V)
    y = jax.nn.sigmoid(gate_a.astype(f32)) * y_a + jax.nn.sigmoid(gate_b.astype(f32)) * y_b
    return y.astype(hn.dtype) @ w_out.astype(hn.dtype)


def conv_ffn(hn, w_up, conv_w, conv_b, w_down):
    u = causal_dwconv(hn @ w_up.astype(hn.dtype), conv_w) + conv_b.astype(hn.dtype)
    a, b = jnp.split(u, 2, axis=-1)
    return (jax.nn.silu(a) * b) @ w_down.astype(hn.dtype)


def _fwd_setup_inputs(seed: int = 0) -> dict:
    key = jax.random.key(seed)
    ks = jax.random.split(key, 16)
    f32 = jnp.float32
    nrm = lambda k, shape, s: jax.random.normal(k, shape, f32) * s
    dt = jnp.exp(jax.random.uniform(ks[6], (DEPTH, GDN_HEADS), f32, math.log(1e-3), math.log(1e-1)))
    return {
        'x': nrm(ks[0], (BATCH, SEQ, D_MODEL), 1.0),
        'meta': nrm(ks[1], (N_META, D_MODEL), 1.0),
        'norm1': 1.0 + nrm(ks[2], (DEPTH, D_MODEL), 0.02),
        'w_in': nrm(ks[3], (DEPTH, D_MODEL, D_IN), D_MODEL ** -0.5),
        'gdn_conv_w': nrm(ks[4], (DEPTH, GDN_CONV, 2 * GDN_QK + GDN_V), GDN_CONV ** -0.5),
        'gdn_a_log': jnp.log(jax.random.uniform(ks[5], (DEPTH, GDN_HEADS), f32, 1.0, 16.0)),
        'gdn_dt_bias': dt + jnp.log(-jnp.expm1(-dt)),
        'gdn_norm': 1.0 + nrm(ks[7], (DEPTH, GDN_DV), 0.02),
        'w_out': nrm(ks[8], (DEPTH, D_MODEL, D_MODEL), D_MODEL ** -0.5),
        'norm2': 1.0 + nrm(ks[9], (DEPTH, D_MODEL), 0.02),
        'w_ffn_up': nrm(ks[10], (DEPTH, D_MODEL, 2 * D_FF), D_MODEL ** -0.5),
        'ffn_conv_w': nrm(ks[11], (DEPTH, FFN_CONV, 2 * D_FF), FFN_CONV ** -0.5),
        'ffn_conv_b': nrm(ks[12], (DEPTH, 2 * D_FF), 0.02),
        'w_ffn_down': nrm(ks[13], (DEPTH, D_FF, D_MODEL), D_FF ** -0.5),
        'norm_f': 1.0 + nrm(ks[14], (D_MODEL,), 0.02),
    }


def _fwd_reference(x, meta, norm1, w_in, gdn_conv_w, gdn_a_log, gdn_dt_bias, gdn_norm, w_out,
              norm2, w_ffn_up, ffn_conv_w, ffn_conv_b, w_ffn_down, norm_f):
    B = x.shape[0]
    h = jnp.concatenate(
        [jnp.broadcast_to(meta.astype(x.dtype)[None], (B, N_META, D_MODEL)), x], axis=1)
    pos = jnp.arange(h.shape[1], dtype=jnp.float32)
    for layer in range(DEPTH):
        h = h + mixer_block(rmsnorm(h, norm1[layer]), pos, w_in[layer], gdn_conv_w[layer],
                            gdn_a_log[layer], gdn_dt_bias[layer], gdn_norm[layer], w_out[layer])
        h = h + conv_ffn(rmsnorm(h, norm2[layer]), w_ffn_up[layer], ffn_conv_w[layer],
                         ffn_conv_b[layer], w_ffn_down[layer])
    h = rmsnorm(h, norm_f)
    return h[:, N_META:]


import jax as _jax
import jax.numpy as _jnp

TWIN_FORMAT = 'train_step'
FWD_PARAMS = ['x', 'meta', 'norm1', 'w_in', 'gdn_conv_w', 'gdn_a_log', 'gdn_dt_bias', 'gdn_norm', 'w_out', 'norm2', 'w_ffn_up', 'ffn_conv_w', 'ffn_conv_b', 'w_ffn_down', 'norm_f']
TWIN_WEIGHTS = ['meta', 'norm1', 'w_in', 'gdn_conv_w', 'gdn_a_log', 'gdn_dt_bias', 'gdn_norm', 'w_out', 'norm2', 'w_ffn_up', 'ffn_conv_w', 'ffn_conv_b', 'w_ffn_down', 'norm_f']
TWIN_DIFF_INPUT = 'x'
TWIN_INPUTS = ['x', 'meta', 'norm1', 'w_in', 'gdn_conv_w', 'gdn_a_log', 'gdn_dt_bias', 'gdn_norm', 'w_out', 'norm2', 'w_ffn_up', 'ffn_conv_w', 'ffn_conv_b', 'w_ffn_down', 'norm_f', 'loss_target', 'm_meta', 'm_norm1', 'm_w_in', 'm_gdn_conv_w', 'm_gdn_a_log', 'm_gdn_dt_bias', 'm_gdn_norm', 'm_w_out', 'm_norm2', 'm_w_ffn_up', 'm_ffn_conv_w', 'm_ffn_conv_b', 'm_w_ffn_down', 'm_norm_f', 'v_meta', 'v_norm1', 'v_w_in', 'v_gdn_conv_w', 'v_gdn_a_log', 'v_gdn_dt_bias', 'v_gdn_norm', 'v_w_out', 'v_norm2', 'v_w_ffn_up', 'v_ffn_conv_w', 'v_ffn_conv_b', 'v_w_ffn_down', 'v_norm_f']
TWIN_OUTPUTS = ['loss', 'grad_x', 'grad_meta', 'grad_norm1', 'grad_w_in', 'grad_gdn_conv_w', 'grad_gdn_a_log', 'grad_gdn_dt_bias', 'grad_gdn_norm', 'grad_w_out', 'grad_norm2', 'grad_w_ffn_up', 'grad_ffn_conv_w', 'grad_ffn_conv_b', 'grad_w_ffn_down', 'grad_norm_f', 'delta_meta', 'delta_norm1', 'delta_w_in', 'delta_gdn_conv_w', 'delta_gdn_a_log', 'delta_gdn_dt_bias', 'delta_gdn_norm', 'delta_w_out', 'delta_norm2', 'delta_w_ffn_up', 'delta_ffn_conv_w', 'delta_ffn_conv_b', 'delta_w_ffn_down', 'delta_norm_f', 'new_m_meta', 'new_m_norm1', 'new_m_w_in', 'new_m_gdn_conv_w', 'new_m_gdn_a_log', 'new_m_gdn_dt_bias', 'new_m_gdn_norm', 'new_m_w_out', 'new_m_norm2', 'new_m_w_ffn_up', 'new_m_ffn_conv_w', 'new_m_ffn_conv_b', 'new_m_w_ffn_down', 'new_m_norm_f', 'new_v_meta', 'new_v_norm1', 'new_v_w_in', 'new_v_gdn_conv_w', 'new_v_gdn_a_log', 'new_v_gdn_dt_bias', 'new_v_gdn_norm', 'new_v_w_out', 'new_v_norm2', 'new_v_w_ffn_up', 'new_v_ffn_conv_w', 'new_v_ffn_conv_b', 'new_v_w_ffn_down', 'new_v_norm_f']
TWIN_LEAF_KINDS = {'loss': 'loss', 'grad_x': 'grad_x', 'grad_meta': 'grad_w', 'grad_norm1': 'grad_w', 'grad_w_in': 'grad_w', 'grad_gdn_conv_w': 'grad_w', 'grad_gdn_a_log': 'grad_w', 'grad_gdn_dt_bias': 'grad_w', 'grad_gdn_norm': 'grad_w', 'grad_w_out': 'grad_w', 'grad_norm2': 'grad_w', 'grad_w_ffn_up': 'grad_w', 'grad_ffn_conv_w': 'grad_w', 'grad_ffn_conv_b': 'grad_w', 'grad_w_ffn_down': 'grad_w', 'grad_norm_f': 'grad_w', 'delta_meta': 'delta_w', 'delta_norm1': 'delta_w', 'delta_w_in': 'delta_w', 'delta_gdn_conv_w': 'delta_w', 'delta_gdn_a_log': 'delta_w', 'delta_gdn_dt_bias': 'delta_w', 'delta_gdn_norm': 'delta_w', 'delta_w_out': 'delta_w', 'delta_norm2': 'delta_w', 'delta_w_ffn_up': 'delta_w', 'delta_ffn_conv_w': 'delta_w', 'delta_ffn_conv_b': 'delta_w', 'delta_w_ffn_down': 'delta_w', 'delta_norm_f': 'delta_w', 'new_m_meta': 'new_m', 'new_m_norm1': 'new_m', 'new_m_w_in': 'new_m', 'new_m_gdn_conv_w': 'new_m', 'new_m_gdn_a_log': 'new_m', 'new_m_gdn_dt_bias': 'new_m', 'new_m_gdn_norm': 'new_m', 'new_m_w_out': 'new_m', 'new_m_norm2': 'new_m', 'new_m_w_ffn_up': 'new_m', 'new_m_ffn_conv_w': 'new_m', 'new_m_ffn_conv_b': 'new_m', 'new_m_w_ffn_down': 'new_m', 'new_m_norm_f': 'new_m', 'new_v_meta': 'new_v', 'new_v_norm1': 'new_v', 'new_v_w_in': 'new_v', 'new_v_gdn_conv_w': 'new_v', 'new_v_gdn_a_log': 'new_v', 'new_v_gdn_dt_bias': 'new_v', 'new_v_gdn_norm': 'new_v', 'new_v_w_out': 'new_v', 'new_v_norm2': 'new_v', 'new_v_w_ffn_up': 'new_v', 'new_v_ffn_conv_w': 'new_v', 'new_v_ffn_conv_b': 'new_v', 'new_v_w_ffn_down': 'new_v', 'new_v_norm_f': 'new_v'}


def _forward(args):
    return _fwd_reference(*[args[k] for k in FWD_PARAMS])


def _output_shape():
    def fwd():
        inp = _fwd_setup_inputs(0)
        return _fwd_reference(*[inp[k] for k in FWD_PARAMS])
    out = _jax.eval_shape(fwd)
    return out.shape, out.dtype

N_MICROBATCH = 1
ADAM_LR = 0.001
ADAM_B1 = 0.9
ADAM_B2 = 0.999
ADAM_EPS = 1e-08
ADAM_WD = 0.01
ADAM_STEP = 10
PER_EXAMPLE_BATCH_AXIS = {'x': 0, 'loss_target': 0}
SHARED_INPUTS = []
_WEIGHT_DTYPES = {'meta': _jnp.float32, 'norm1': _jnp.float32, 'w_in': _jnp.float32, 'gdn_conv_w': _jnp.float32, 'gdn_a_log': _jnp.float32, 'gdn_dt_bias': _jnp.float32, 'gdn_norm': _jnp.float32, 'w_out': _jnp.float32, 'norm2': _jnp.float32, 'w_ffn_up': _jnp.float32, 'ffn_conv_w': _jnp.float32, 'ffn_conv_b': _jnp.float32, 'w_ffn_down': _jnp.float32, 'norm_f': _jnp.float32}
MOMENT_SCALE = {'meta': 1.003528e-02, 'norm1': 2.334984e-01, 'w_in': 6.985342e-02, 'gdn_conv_w': 6.195292e-02, 'gdn_a_log': 3.312032e-01, 'gdn_dt_bias': 3.258508e-01, 'gdn_norm': 2.241695e-01, 'w_out': 1.154965e-01, 'norm2': 1.866556e-01, 'w_ffn_up': 7.662589e-02, 'ffn_conv_w': 7.690546e-02, 'ffn_conv_b': 7.618601e-02, 'w_ffn_down': 1.251218e-01, 'norm_f': 6.404334e+01}


def _to_microbatches(a, axis):
    t = _jnp.moveaxis(a, axis, 0)
    t = t.reshape((N_MICROBATCH, t.shape[0] // N_MICROBATCH) + t.shape[1:])
    return _jnp.moveaxis(t, 1, axis + 1)


def setup_inputs(seed: int = 0) -> dict:
    inp = _fwd_setup_inputs(seed)
    key = _jax.random.fold_in(_jax.random.key(seed), 7919)
    shape, _ = _output_shape()
    out = dict(inp)
    out["loss_target"] = _jax.random.normal(_jax.random.fold_in(key, 0), shape, _jnp.float32)
    for i, name in enumerate(TWIN_WEIGHTS):
        w = inp[name].astype(_jnp.float32)
        if MOMENT_SCALE is None:
            s = _jnp.sqrt(_jnp.mean(_jnp.square(w)) + 1e-30)
        else:
            s = MOMENT_SCALE[name]
        km, kv = _jax.random.split(_jax.random.fold_in(key, i + 1))
        out[name] = w
        out["m_" + name] = s * _jax.random.normal(km, w.shape, _jnp.float32)
        out["v_" + name] = (s * s) * _jax.random.uniform(kv, w.shape, _jnp.float32, 0.5, 1.5)
    if N_MICROBATCH > 1:
        for name, axis in PER_EXAMPLE_BATCH_AXIS.items():
            out[name] = _to_microbatches(out[name], axis)
    return {'x': out['x'], 'meta': out['meta'], 'norm1': out['norm1'], 'w_in': out['w_in'], 'gdn_conv_w': out['gdn_conv_w'], 'gdn_a_log': out['gdn_a_log'], 'gdn_dt_bias': out['gdn_dt_bias'], 'gdn_norm': out['gdn_norm'], 'w_out': out['w_out'], 'norm2': out['norm2'], 'w_ffn_up': out['w_ffn_up'], 'ffn_conv_w': out['ffn_conv_w'], 'ffn_conv_b': out['ffn_conv_b'], 'w_ffn_down': out['w_ffn_down'], 'norm_f': out['norm_f'], 'loss_target': out['loss_target'], 'm_meta': out['m_meta'], 'm_norm1': out['m_norm1'], 'm_w_in': out['m_w_in'], 'm_gdn_conv_w': out['m_gdn_conv_w'], 'm_gdn_a_log': out['m_gdn_a_log'], 'm_gdn_dt_bias': out['m_gdn_dt_bias'], 'm_gdn_norm': out['m_gdn_norm'], 'm_w_out': out['m_w_out'], 'm_norm2': out['m_norm2'], 'm_w_ffn_up': out['m_w_ffn_up'], 'm_ffn_conv_w': out['m_ffn_conv_w'], 'm_ffn_conv_b': out['m_ffn_conv_b'], 'm_w_ffn_down': out['m_w_ffn_down'], 'm_norm_f': out['m_norm_f'], 'v_meta': out['v_meta'], 'v_norm1': out['v_norm1'], 'v_w_in': out['v_w_in'], 'v_gdn_conv_w': out['v_gdn_conv_w'], 'v_gdn_a_log': out['v_gdn_a_log'], 'v_gdn_dt_bias': out['v_gdn_dt_bias'], 'v_gdn_norm': out['v_gdn_norm'], 'v_w_out': out['v_w_out'], 'v_norm2': out['v_norm2'], 'v_w_ffn_up': out['v_w_ffn_up'], 'v_ffn_conv_w': out['v_ffn_conv_w'], 'v_ffn_conv_b': out['v_ffn_conv_b'], 'v_w_ffn_down': out['v_w_ffn_down'], 'v_norm_f': out['v_norm_f']}


def _loss(weights, diff, rest, loss_target):
    with _jax.named_scope("forward"):
        args = {**rest, TWIN_DIFF_INPUT: diff, **{k: w.astype(_WEIGHT_DTYPES[k]) for k, w in weights.items()}}
        y = _forward(args)
    with _jax.named_scope("loss_head"):
        err = _jnp.square(y.astype(_jnp.float32) - loss_target)
        return 0.5 * _jnp.sum(_jnp.mean(err, axis=-1)) if err.ndim else 0.5 * err


def _adamw(w, g, m, v):
    m = ADAM_B1 * m + (1.0 - ADAM_B1) * g
    v = ADAM_B2 * v + (1.0 - ADAM_B2) * _jnp.square(g)
    m_hat = m / (1.0 - ADAM_B1 ** ADAM_STEP)
    v_hat = v / (1.0 - ADAM_B2 ** ADAM_STEP)
    delta = -ADAM_LR * (m_hat / (_jnp.sqrt(v_hat) + ADAM_EPS) + ADAM_WD * w)
    return delta, m, v


def reference(x, meta, norm1, w_in, gdn_conv_w, gdn_a_log, gdn_dt_bias, gdn_norm, w_out, norm2, w_ffn_up, ffn_conv_w, ffn_conv_b, w_ffn_down, norm_f, loss_target, m_meta, m_norm1, m_w_in, m_gdn_conv_w, m_gdn_a_log, m_gdn_dt_bias, m_gdn_norm, m_w_out, m_norm2, m_w_ffn_up, m_ffn_conv_w, m_ffn_conv_b, m_w_ffn_down, m_norm_f, v_meta, v_norm1, v_w_in, v_gdn_conv_w, v_gdn_a_log, v_gdn_dt_bias, v_gdn_norm, v_w_out, v_norm2, v_w_ffn_up, v_ffn_conv_w, v_ffn_conv_b, v_w_ffn_down, v_norm_f):
    given = dict(x=x, meta=meta, norm1=norm1, w_in=w_in, gdn_conv_w=gdn_conv_w, gdn_a_log=gdn_a_log, gdn_dt_bias=gdn_dt_bias, gdn_norm=gdn_norm, w_out=w_out, norm2=norm2, w_ffn_up=w_ffn_up, ffn_conv_w=ffn_conv_w, ffn_conv_b=ffn_conv_b, w_ffn_down=w_ffn_down, norm_f=norm_f, loss_target=loss_target, m_meta=m_meta, m_norm1=m_norm1, m_w_in=m_w_in, m_gdn_conv_w=m_gdn_conv_w, m_gdn_a_log=m_gdn_a_log, m_gdn_dt_bias=m_gdn_dt_bias, m_gdn_norm=m_gdn_norm, m_w_out=m_w_out, m_norm2=m_norm2, m_w_ffn_up=m_w_ffn_up, m_ffn_conv_w=m_ffn_conv_w, m_ffn_conv_b=m_ffn_conv_b, m_w_ffn_down=m_w_ffn_down, m_norm_f=m_norm_f, v_meta=v_meta, v_norm1=v_norm1, v_w_in=v_w_in, v_gdn_conv_w=v_gdn_conv_w, v_gdn_a_log=v_gdn_a_log, v_gdn_dt_bias=v_gdn_dt_bias, v_gdn_norm=v_gdn_norm, v_w_out=v_w_out, v_norm2=v_norm2, v_w_ffn_up=v_w_ffn_up, v_ffn_conv_w=v_ffn_conv_w, v_ffn_conv_b=v_ffn_conv_b, v_w_ffn_down=v_w_ffn_down, v_norm_f=v_norm_f)
    weights = {n: given[n] for n in TWIN_WEIGHTS}
    shared = {n: given[n] for n in SHARED_INPUTS}
    per_example = {n: given[n] for n in ['x']}
    grad_fn = _jax.value_and_grad(_loss, argnums=(0, 1))

    def one_microbatch(ex, loss_target):
        ex = dict(ex)
        diff = ex.pop(TWIN_DIFF_INPUT)
        return grad_fn(weights, diff, {**shared, **ex}, loss_target)

    if N_MICROBATCH == 1:
        loss, (grad_w, grad_x) = one_microbatch(per_example, given["loss_target"])
    else:
        def body(carry, xs):
            loss_sum, grad_sum = carry
            l_k, (gw_k, gx_k) = one_microbatch(xs[0], xs[1])
            with _jax.named_scope("update"):
                return (loss_sum + l_k, _jax.tree.map(_jnp.add, grad_sum, gw_k)), gx_k

        init = (_jnp.zeros((), _jnp.float32), _jax.tree.map(_jnp.zeros_like, weights))
        (loss, grad_w), grad_x = _jax.lax.scan(body, init, (per_example, given["loss_target"]))
    with _jax.named_scope("update"):
        delta_w, new_m, new_v = {}, {}, {}
        for n in TWIN_WEIGHTS:
            delta_w[n], new_m[n], new_v[n] = _adamw(weights[n], grad_w[n], given["m_" + n], given["v_" + n])
    return (loss, grad_x, *[grad_w[n] for n in TWIN_WEIGHTS], *[delta_w[n] for n in TWIN_WEIGHTS],
            *[new_m[n] for n in TWIN_WEIGHTS], *[new_v[n] for n in TWIN_WEIGHTS])
```

```python
import functools
import math

import numpy as np
import jax
import jax.numpy as jnp
from jax import lax
from jax.experimental import pallas as pl
from jax.experimental.pallas import tpu as pltpu

F32 = jnp.float32
BF16 = jnp.bfloat16
HI = lax.Precision.HIGHEST

D_MODEL = 1024
N_META = 16
CHUNK = 64
GDN_H = 8
GDN_D = 128
RET_H = 4
RET_D = 256
D_FF = 2816
GDN_CONV = 4
FFN_CONV = 3
ROPE_BASE = 10000.0
EPS = 1e-6
N_DEV = 8
LANES = 128
MAIN_W = 10 * 1024
_O_GQ, _O_GZ, _O_GA, _O_RQ, _O_RG, _O_GATE, _O_END = 0, 3072, 4096, 4112, 7184, 8208, 10256

ADAM_LR = 0.001
ADAM_B1 = 0.9
ADAM_B2 = 0.999
ADAM_EPS = 1e-08
ADAM_WD = 0.01
ADAM_STEP = 10

MESH_T = pl.DeviceIdType.MESH


def _tile(n, target, mult):
    best = None
    for d in range(mult, min(n, target) + 1, mult):
        if n % d == 0:
            best = d
    assert best is not None, (n, target, mult)
    return best


def _sig(x):
    return 1.0 / (1.0 + jnp.exp(-x))


def _d(a, b):
    return jnp.dot(a.astype(BF16), b.astype(BF16), preferred_element_type=F32)


def _dnt(a, b):
    return lax.dot_general(a.astype(BF16), b.astype(BF16), (((1,), (1,)), ((), ())), preferred_element_type=F32)


def _dtn(a, b):
    return lax.dot_general(a.astype(BF16), b.astype(BF16), (((0,), (0,)), ((), ())), preferred_element_type=F32)


def _dx(a, b):
    return jnp.dot(a, b, preferred_element_type=F32, precision=HI)


def _dxnt(a, b):
    return lax.dot_general(a, b, (((1,), (1,)), ((), ())), preferred_element_type=F32, precision=HI)


def _dxtn(a, b):
    return lax.dot_general(a, b, (((0,), (0,)), ((), ())), preferred_element_type=F32, precision=HI)


def _rowsum(x):
    return jnp.sum(x, axis=1, keepdims=True)


def _allsum(x):
    return jnp.sum(jnp.sum(x, axis=1, keepdims=True), axis=0, keepdims=True)


def _mm_nn(a, b, res=None, out_dtype=F32, name="mm_nn"):
    M, K = a.shape
    N = b.shape[1]
    tm = _tile(M, 704, 16)
    tn = _tile(N, 2816, 128)

    def body(*refs):
        if res is None:
            a_ref, b_ref, o_ref = refs
        else:
            a_ref, b_ref, r_ref, o_ref = refs
        acc = jnp.dot(a_ref[...], b_ref[...], preferred_element_type=F32)
        if res is not None:
            acc = acc + r_ref[...]
        o_ref[...] = acc.astype(out_dtype)

    in_specs = [pl.BlockSpec((tm, K), lambda j, i: (i, 0)), pl.BlockSpec((K, tn), lambda j, i: (0, j))]
    args = [a, b]
    if res is not None:
        in_specs.append(pl.BlockSpec((tm, tn), lambda j, i: (i, j)))
        args.append(res)
    return pl.pallas_call(
        body, grid=(N // tn, M // tm), in_specs=in_specs,
        out_specs=pl.BlockSpec((tm, tn), lambda j, i: (i, j)),
        out_shape=jax.ShapeDtypeStruct((M, N), out_dtype), name=name)(*args)


def _mm_nt(a, b, res=None, name="mm_nt"):
    M, Nc = a.shape
    K = b.shape[0]
    tm = _tile(M, 704, 16)
    tc = _tile(Nc, 2048, 128)

    def body(*refs):
        if res is None:
            a_ref, b_ref, o_ref = refs
        else:
            a_ref, b_ref, r_ref, o_ref = refs
        c = pl.program_id(1)
        p = lax.dot_general(a_ref[...], b_ref[...], (((1,), (1,)), ((), ())), preferred_element_type=F32)

        @pl.when(c == 0)
        def _():
            if res is None:
                o_ref[...] = p
            else:
                o_ref[...] = p + r_ref[...]

        @pl.when(c > 0)
        def _():
            o_ref[...] += p

    in_specs = [pl.BlockSpec((tm, tc), lambda i, c: (i, c)), pl.BlockSpec((K, tc), lambda i, c: (0, c))]
    args = [a, b]
    if res is not None:
        in_specs.append(pl.BlockSpec((tm, K), lambda i, c: (i, 0)))
        args.append(res)
    return pl.pallas_call(
        body, grid=(M // tm, Nc // tc), in_specs=in_specs,
        out_specs=pl.BlockSpec((tm, K), lambda i, c: (i, 0)),
        out_shape=jax.ShapeDtypeStruct((M, K), F32), name=name)(*args)


def _mm_tn(a, b, name="mm_tn"):
    M, K = a.shape
    N = b.shape[1]
    tm = _tile(M, 704, 16)
    tk = _tile(K, 1408, 128)
    tn = _tile(N, 2048, 128)

    def body(a_ref, b_ref, o_ref):
        m = pl.program_id(2)
        p = lax.dot_general(a_ref[...], b_ref[...], (((0,), (0,)), ((), ())), preferred_element_type=F32)

        @pl.when(m == 0)
        def _():
            o_ref[...] = p

        @pl.when(m > 0)
        def _():
            o_ref[...] += p

    return pl.pallas_call(
        body, grid=(K // tk, N // tn, M // tm),
        in_specs=[pl.BlockSpec((tm, tk), lambda kk, j, m: (m, kk)), pl.BlockSpec((tm, tn), lambda kk, j, m: (m, j))],
        out_specs=pl.BlockSpec((tk, tn), lambda kk, j, m: (kk, j)),
        out_shape=jax.ShapeDtypeStruct((K, N), F32), name=name)(a, b)


def _rms_fwd(x, g, name):
    Lp = x.shape[0]
    tr = _tile(Lp, 256, 16)

    def body(x_ref, g_ref, o_ref):
        xv = x_ref[...]
        r = lax.rsqrt(jnp.mean(xv * xv, axis=-1, keepdims=True) + EPS)
        o_ref[...] = (xv * r * g_ref[...]).astype(BF16)

    return pl.pallas_call(
        body, grid=(Lp // tr,),
        in_specs=[pl.BlockSpec((tr, D_MODEL), lambda i: (i, 0)), pl.BlockSpec((1, D_MODEL), lambda i: (0, 0))],
        out_specs=pl.BlockSpec((tr, D_MODEL), lambda i: (i, 0)),
        out_shape=jax.ShapeDtypeStruct((Lp, D_MODEL), BF16), name=name)(x, g)


def _rms_bwd(x, g, dy, dres, pad, name):
    Lp = x.shape[0]
    tr = _tile(Lp, 256, 16)

    def body(x_ref, g_ref, dy_ref, dr_ref, dx_ref, dxb_ref, dg_ref):
        i = pl.program_id(0)
        xv = x_ref[...]
        r = lax.rsqrt(jnp.mean(xv * xv, axis=-1, keepdims=True) + EPS)
        xh = xv * r
        dyv = dy_ref[...]
        dxh = dyv * g_ref[...]
        dx = r * (dxh - xh * jnp.mean(dxh * xh, axis=-1, keepdims=True)) + dr_ref[...]
        row = i * tr + lax.broadcasted_iota(jnp.int32, (tr, 1), 0)
        dx = jnp.where(row >= pad, dx, 0.0)
        dx_ref[...] = dx
        dxb_ref[...] = dx.astype(BF16)
        part = jnp.sum(dyv * xh, axis=0, keepdims=True)

        @pl.when(i == 0)
        def _():
            dg_ref[...] = part

        @pl.when(i > 0)
        def _():
            dg_ref[...] += part

    blk = pl.BlockSpec((tr, D_MODEL), lambda i: (i, 0))
    vec = pl.BlockSpec((1, D_MODEL), lambda i: (0, 0))
    return pl.pallas_call(
        body, grid=(Lp // tr,), in_specs=[blk, vec, blk, blk], out_specs=[blk, blk, vec],
        out_shape=[jax.ShapeDtypeStruct((Lp, D_MODEL), F32), jax.ShapeDtypeStruct((Lp, D_MODEL), BF16),
                   jax.ShapeDtypeStruct((1, D_MODEL), F32)], name=name)(x, g, dy, dres)


def _final(h2, g, tgt, first_row):
    Lp = h2.shape[0]
    tr = _tile(Lp, 256, 16)

    def body(x_ref, g_ref, t_ref, loss_ref, dx_ref, dxb_ref, dg_ref):
        i = pl.program_id(0)
        xv = x_ref[...]
        gv = g_ref[...]
        r = lax.rsqrt(jnp.mean(xv * xv, axis=-1, keepdims=True) + EPS)
        xh = xv * r
        row = i * tr + lax.broadcasted_iota(jnp.int32, (tr, 1), 0)
        err = jnp.where(row >= first_row, xh * gv - t_ref[...], 0.0)
        lpart = jnp.sum(err * err, axis=0, keepdims=True) * (0.5 / D_MODEL)
        dyv = err * (1.0 / D_MODEL)
        dxh = dyv * gv
        dx = r * (dxh - xh * jnp.mean(dxh * xh, axis=-1, keepdims=True))
        dx_ref[...] = dx
        dxb_ref[...] = dx.astype(BF16)
        part = jnp.sum(dyv * xh, axis=0, keepdims=True)

        @pl.when(i == 0)
        def _():
            dg_ref[...] = part
            loss_ref[...] = lpart

        @pl.when(i > 0)
        def _():
            dg_ref[...] += part
            loss_ref[...] += lpart

    blk = pl.BlockSpec((tr, D_MODEL), lambda i: (i, 0))
    vec = pl.BlockSpec((1, D_MODEL), lambda i: (0, 0))
    return pl.pallas_call(
        body, grid=(Lp // tr,), in_specs=[blk, vec, blk], out_specs=[vec, blk, blk, vec],
        out_shape=[jax.ShapeDtypeStruct((1, D_MODEL), F32), jax.ShapeDtypeStruct((Lp, D_MODEL), F32),
                   jax.ShapeDtypeStruct((Lp, D_MODEL), BF16), jax.ShapeDtypeStruct((1, D_MODEL), F32)],
        name="final_norm_loss")(h2, g, tgt)


def _halo_prev(tr, width, col=0):
    return pl.BlockSpec((8, width), lambda i: (jnp.maximum(i * (tr // 8) - 1, 0), col))


def _halo_next(tr, width, nrows, col=0):
    last = nrows // 8 - 1
    return pl.BlockSpec((8, width), lambda i: (jnp.minimum((i + 1) * (tr // 8), last), col))


def _gdn_conv(ext, w, tr, lo):
    acc = w[0:1, :] * ext[lo:lo + tr, :]
    for kk in range(1, GDN_CONV):
        acc = acc + w[kk:kk + 1, :] * ext[lo + kk:lo + kk + tr, :]
    return acc


def _gdn_pre(proj_m, proj_s, conv_w, gparams, pad):
    Lp = proj_m.shape[0]
    tr = _tile(Lp, 192, 64)
    W3 = 3 * D_MODEL

    def body(main_ref, prev_ref, s_ref, w_ref, gp_ref, qkv_ref, gsm_ref):
        i = pl.program_id(0)
        prev = jnp.where(i > 0, prev_ref[...], 0.0)
        ext = jnp.concatenate([prev, main_ref[...]], axis=0)
        c = _gdn_conv(ext, w_ref[...], tr, 8 - (GDN_CONV - 1))
        s = c * _sig(c)
        scale = GDN_D ** -0.5
        for j in range(2 * GDN_H):
            seg = s[:, j * GDN_D:(j + 1) * GDN_D]
            r = lax.rsqrt(_rowsum(seg * seg) + EPS)
            if j < GDN_H:
                r = r * scale
            qkv_ref[:, j * GDN_D:(j + 1) * GDN_D] = seg * r
        qkv_ref[:, 2 * D_MODEL:] = s[:, 2 * D_MODEL:]
        sm = s_ref[...]
        gp = gp_ref[...]
        lane = lax.broadcasted_iota(jnp.int32, sm.shape, 1)
        z = sm + gp[1:2, :]
        softplus = jnp.maximum(z, 0.0) + jnp.log(1.0 + jnp.exp(-jnp.abs(z)))
        lg = -jnp.exp(gp[0:1, :]) * softplus
        row = i * tr + lax.broadcasted_iota(jnp.int32, (tr, 1), 0)
        out = jnp.where(lane < GDN_H, lg, jnp.where(lane < 2 * GDN_H, _sig(sm), 0.0))
        gsm_ref[...] = jnp.where(row >= pad, out, 0.0)

    return pl.pallas_call(
        body, grid=(Lp // tr,),
        in_specs=[pl.BlockSpec((tr, W3), lambda i: (i, 0)), _halo_prev(tr, W3),
                  pl.BlockSpec((tr, LANES), lambda i: (i, 0)),
                  pl.BlockSpec((GDN_CONV, W3), lambda i: (0, 0)), pl.BlockSpec((8, LANES), lambda i: (0, 0))],
        out_specs=[pl.BlockSpec((tr, W3), lambda i: (i, 0)), pl.BlockSpec((tr, LANES), lambda i: (i, 0))],
        out_shape=[jax.ShapeDtypeStruct((Lp, W3), F32), jax.ShapeDtypeStruct((Lp, LANES), F32)],
        name="gdn_pre")(proj_m, proj_m, proj_s, conv_w, gparams)


def _gdn_pre_bwd(proj_m, proj_s, conv_w, gparams, dq, dk, dv, dgs, pad):
    Lp = proj_m.shape[0]
    tr = _tile(Lp, 192, 64)
    W3 = 3 * D_MODEL
    te = tr + 8

    def body(main_ref, prev_ref, next_ref, s_ref, w_ref, gp_ref,
             dq_ref, dqn_ref, dk_ref, dkn_ref, dv_ref, dvn_ref, dgs_ref,
             da_ref, ds_ref, dw_ref, dgp_ref):
        i = pl.program_id(0)
        w = w_ref[...]
        prev = jnp.where(i > 0, prev_ref[...], 0.0)
        ext = jnp.concatenate([prev, main_ref[...], next_ref[...]], axis=0)
        c = _gdn_conv(ext, w, te, 8 - (GDN_CONV - 1))
        sg = _sig(c)
        s = c * sg
        rowe = i * tr + lax.broadcasted_iota(jnp.int32, (te, 1), 0)
        live = (rowe >= pad) & (rowe < Lp)
        dqe = jnp.concatenate([dq_ref[...], dqn_ref[...]], axis=0)
        dke = jnp.concatenate([dk_ref[...], dkn_ref[...]], axis=0)
        dve = jnp.concatenate([dv_ref[...], dvn_ref[...]], axis=0)
        scale = GDN_D ** -0.5
        parts = []
        for j in range(2 * GDN_H):
            seg = s[:, j * GDN_D:(j + 1) * GDN_D]
            r = lax.rsqrt(_rowsum(seg * seg) + EPS)
            xh = seg * r
            if j < GDN_H:
                dxh = dqe[:, j * GDN_D:(j + 1) * GDN_D] * scale
            else:
                dxh = dke[:, (j - GDN_H) * GDN_D:(j - GDN_H + 1) * GDN_D]
            parts.append(r * (dxh - xh * _rowsum(dxh * xh)))
        parts.append(dve)
        dsv = jnp.concatenate(parts, axis=1)
        dc = jnp.where(live, dsv * (sg * (1.0 + c * (1.0 - sg))), 0.0)
        acc = w[GDN_CONV - 1:GDN_CONV, :] * dc[0:tr, :]
        for kk in range(GDN_CONV - 1):
            sh = GDN_CONV - 1 - kk
            acc = acc + w[kk:kk + 1, :] * dc[sh:sh + tr, :]
        da_ref[...] = acc.astype(BF16)
        dcm = dc[0:tr, :]
        rows = [jnp.sum(dcm * ext[8 - (GDN_CONV - 1) + kk:8 - (GDN_CONV - 1) + kk + tr, :], axis=0, keepdims=True)
                for kk in range(GDN_CONV)]
        dwp = jnp.concatenate(rows + [jnp.zeros((8 - GDN_CONV, W3), F32)], axis=0)

        sm = s_ref[...]
        gp = gp_ref[...]
        lane = lax.broadcasted_iota(jnp.int32, sm.shape, 1)
        rowm = i * tr + lax.broadcasted_iota(jnp.int32, (tr, 1), 0)
        dlg = jnp.zeros(sm.shape, F32)
        dbt = jnp.zeros(sm.shape, F32)
        for hh in range(GDN_H):
            blk = dgs_ref[hh]
            dlg = dlg + jnp.where(lane == hh, blk[:, 0:1], 0.0)
            dbt = dbt + jnp.where(lane == hh + GDN_H, blk[:, 1:2], 0.0)
        dlg = jnp.where(rowm >= pad, dlg, 0.0)
        dbt = jnp.where(rowm >= pad, dbt, 0.0)
        z = sm + gp[1:2, :]
        softplus = jnp.maximum(z, 0.0) + jnp.log(1.0 + jnp.exp(-jnp.abs(z)))
        ea = jnp.exp(gp[0:1, :])
        dz = dlg * (-ea) * _sig(z)
        dal = dlg * (-ea) * softplus
        bt = _sig(sm)
        dgb = dbt * bt * (1.0 - bt)
        ds_ref[...] = (dz + dgb).astype(BF16)
        gpp = jnp.concatenate([jnp.sum(dal, axis=0, keepdims=True), jnp.sum(dz, axis=0, keepdims=True),
                               jnp.zeros((6, LANES), F32)], axis=0)

        @pl.when(i == 0)
        def _():
            dw_ref[...] = dwp
            dgp_ref[...] = gpp

        @pl.when(i > 0)
        def _():
            dw_ref[...] += dwp
            dgp_ref[...] += gpp

    m3 = pl.BlockSpec((tr, W3), lambda i: (i, 0))
    m1 = pl.BlockSpec((tr, D_MODEL), lambda i: (i, 0))
    n1 = _halo_next(tr, D_MODEL, Lp)
    return pl.pallas_call(
        body, grid=(Lp // tr,),
        in_specs=[m3, _halo_prev(tr, W3), _halo_next(tr, W3, Lp), pl.BlockSpec((tr, LANES), lambda i: (i, 0)),
                  pl.BlockSpec((GDN_CONV, W3), lambda i: (0, 0)), pl.BlockSpec((8, LANES), lambda i: (0, 0)),
                  m1, n1, m1, n1, m1, n1, pl.BlockSpec((GDN_H, tr, LANES), lambda i: (0, i, 0))],
        out_specs=[m3, pl.BlockSpec((tr, LANES), lambda i: (i, 0)),
                   pl.BlockSpec((8, W3), lambda i: (0, 0)), pl.BlockSpec((8, LANES), lambda i: (0, 0))],
        out_shape=[jax.ShapeDtypeStruct((Lp, W3), BF16), jax.ShapeDtypeStruct((Lp, LANES), BF16),
                   jax.ShapeDtypeStruct((8, W3), F32), jax.ShapeDtypeStruct((8, LANES), F32)],
        name="gdn_pre_bwd")(proj_m, proj_m, proj_m, proj_s, conv_w, gparams, dq, dq, dk, dk, dv, dv, dgs)


def _gdn_gates(gs, h):
    ri = lax.broadcasted_iota(jnp.int32, (CHUNK, CHUNK), 0)
    ci = lax.broadcasted_iota(jnp.int32, (CHUNK, CHUNK), 1)
    tril = ri >= ci
    strict = ri > ci
    gall = _dx(tril.astype(F32), gs)
    lane = lax.broadcasted_iota(jnp.int32, gs.shape, 1)
    g = _rowsum(jnp.where(lane == h, gall, 0.0))
    beta = _rowsum(jnp.where(lane == h + GDN_H, gs, 0.0))
    lane8 = lax.broadcasted_iota(jnp.int32, (8, LANES), 1)
    grow = _dxnt((lane8 == h).astype(F32), gall)[0:1, :]
    gam = jnp.where(tril, jnp.exp(jnp.where(tril, g - grow, 0.0)), 0.0)
    return g, beta, gam, tril, strict


def _gdn_chunk_fwd(qkv, gsm):
    Lp = qkv.shape[0]
    N = Lp // CHUNK

    def body(q_ref, k_ref, v_ref, gs_ref, o_ref, sin_ref, t_ref, S):
        h = pl.program_id(0)
        n = pl.program_id(1)

        @pl.when(n == 0)
        def _():
            S[...] = jnp.zeros_like(S)

        q = q_ref[...]
        k = k_ref[...]
        v = v_ref[...]
        g, beta, gam, tril, strict = _gdn_gates(gs_ref[...], h)
        eg = jnp.exp(g)
        gl = g[CHUNK - 1:CHUNK, :]
        kb = k * beta
        m = jnp.where(strict, _dnt(kb, k) * gam, 0.0)
        ri = lax.broadcasted_iota(jnp.int32, (CHUNK, CHUNK), 0)
        ci = lax.broadcasted_iota(jnp.int32, (CHUNK, CHUNK), 1)
        pw = -m
        t = (ri == ci).astype(F32) + pw
        for _ in range(5):
            pw = _dx(pw, pw)
            t = t + _dx(t, pw)
        u = _d(t, v * beta)
        w = _d(t, kb * eg)
        p = _dnt(q, k) * gam
        s0 = S[...]
        vnew = u - _d(w, s0)
        o_ref[...] = _d(q * eg, s0) + _d(p, vnew)
        sin_ref[...] = s0
        t_ref[...] = t
        S[...] = s0 * jnp.exp(gl) + _dtn(k * jnp.exp(gl - g), vnew)

    blk = lambda off: pl.BlockSpec((CHUNK, GDN_D), lambda h, n: (n, off + h))
    return pl.pallas_call(
        body, grid=(GDN_H, N),
        in_specs=[blk(0), blk(GDN_H), blk(2 * GDN_H), pl.BlockSpec((CHUNK, LANES), lambda h, n: (n, 0))],
        out_specs=[blk(0), pl.BlockSpec((None, None, GDN_D, GDN_D), lambda h, n: (h, n, 0, 0)),
                   pl.BlockSpec((None, None, CHUNK, CHUNK), lambda h, n: (h, n, 0, 0))],
        out_shape=[jax.ShapeDtypeStruct((Lp, D_MODEL), F32), jax.ShapeDtypeStruct((GDN_H, N, GDN_D, GDN_D), F32),
                   jax.ShapeDtypeStruct((GDN_H, N, CHUNK, CHUNK), F32)],
        scratch_shapes=[pltpu.VMEM((GDN_D, GDN_D), F32)],
        name="gdn_chunk_fwd")(qkv, qkv, qkv, gsm)


def _gdn_chunk_bwd(qkv, gsm, do, s_in, t_in):
    Lp = qkv.shape[0]
    N = Lp // CHUNK

    def body(q_ref, k_ref, v_ref, gs_ref, do_ref, sin_ref, t_ref, dq_ref, dk_ref, dv_ref, dgs_ref, dS):
        h = pl.program_id(0)
        n = pl.program_id(1)

        @pl.when(n == 0)
        def _():
            dS[...] = jnp.zeros_like(dS)

        q = q_ref[...]
        k = k_ref[...]
        v = v_ref[...]
        dov = do_ref[...]
        s0 = sin_ref[...]
        t = t_ref[...]
        g, beta, gam, tril, strict = _gdn_gates(gs_ref[...], h)
        eg = jnp.exp(g)
        gl = g[CHUNK - 1:CHUNK, :]
        egl = jnp.exp(gl)
        e = jnp.exp(gl - g)
        kb = k * beta
        kbg = kb * eg
        vb = v * beta
        m = jnp.where(strict, _dnt(kb, k) * gam, 0.0)
        u = _d(t, vb)
        w = _d(t, kbg)
        p = _dnt(q, k) * gam
        vnew = u - _d(w, s0)
        qg = q * eg
        kd = k * e
        dsv = dS[...]

        dvnew = _dtn(p, dov) + _d(kd, dsv)
        dp = jnp.where(tril, _dnt(dov, vnew), 0.0)
        dqg = _dnt(dov, s0)
        dkd = _dnt(vnew, dsv)
        dw = -_dnt(dvnew, s0)
        dS[...] = _dtn(qg, dov) + egl * dsv - _dtn(w, dvnew)
        dvb = _dtn(t, dvnew)
        dkbg = _dtn(t, dw)
        dt = _dnt(dvnew, vb) + _dnt(dw, kbg)
        dm = jnp.where(strict, -_dxnt(_dxtn(t, dt), t), 0.0)
        dkk = dm * gam
        dqk = dp * gam
        dkb = _d(dkk, k) + eg * dkbg
        dk_ref[...] = _dtn(dkk, kb) + _dtn(dqk, q) + dkd * e + beta * dkb
        dq_ref[...] = _d(dqk, k) + dqg * eg
        dv_ref[...] = beta * dvb
        dbeta = _rowsum(k * dkb) + _rowsum(v * dvb)
        em = dm * m + dp * p
        colsum = _dxtn(em, jnp.ones((CHUNK, LANES), F32))[:, 0:1]
        z = _rowsum(kd * dkd)
        dg = _rowsum(em) - colsum + _rowsum(qg * dqg) + _rowsum(kbg * dkbg) - z
        extra = _allsum(z) + egl * _allsum(s0 * dsv)
        rcol = lax.broadcasted_iota(jnp.int32, (CHUNK, 1), 0)
        dg = dg + jnp.where(rcol == CHUNK - 1, extra, 0.0)
        ri = lax.broadcasted_iota(jnp.int32, (CHUNK, CHUNK), 0)
        ci = lax.broadcasted_iota(jnp.int32, (CHUNK, CHUNK), 1)
        dlg = _dx((ci >= ri).astype(F32), jnp.broadcast_to(dg, (CHUNK, LANES)))
        lane = lax.broadcasted_iota(jnp.int32, (CHUNK, LANES), 1)
        dgs_ref[...] = jnp.where(lane == 0, dlg, jnp.where(lane == 1, dbeta, 0.0))

    blk = lambda off: pl.BlockSpec((CHUNK, GDN_D), lambda h, n: (N - 1 - n, off + h))
    st = lambda a, b: pl.BlockSpec((None, None, a, b), lambda h, n: (h, N - 1 - n, 0, 0))
    return pl.pallas_call(
        body, grid=(GDN_H, N),
        in_specs=[blk(0), blk(GDN_H), blk(2 * GDN_H), pl.BlockSpec((CHUNK, LANES), lambda h, n: (N - 1 - n, 0)),
                  blk(0), st(GDN_D, GDN_D), st(CHUNK, CHUNK)],
        out_specs=[blk(0), blk(0), blk(0), pl.BlockSpec((None, CHUNK, LANES), lambda h, n: (h, N - 1 - n, 0))],
        out_shape=[jax.ShapeDtypeStruct((Lp, D_MODEL), F32)] * 3 + [jax.ShapeDtypeStruct((GDN_H, Lp, LANES), F32)],
        scratch_shapes=[pltpu.VMEM((GDN_D, GDN_D), F32)],
        name="gdn_chunk_bwd")(qkv, qkv, qkv, gsm, do, s_in, t_in)


def _rot(x, c, s):
    half = RET_D // 2
    x1 = x[:, :half]
    x2 = x[:, half:]
    return jnp.concatenate([x1 * c - x2 * s, x2 * c + x1 * s], axis=1)


def _rot_bwd(d, c, s):
    half = RET_D // 2
    d1 = d[:, :half]
    d2 = d[:, half:]
    return jnp.concatenate([d1 * c + d2 * s, d2 * c - d1 * s], axis=1)


def _ret_tables():
    hh = jnp.arange(RET_H, dtype=F32)
    lg = jnp.log(1.0 - 2.0 ** (-5.0 - hh))
    idx = jnp.arange(CHUNK, dtype=F32)
    tril = jnp.asarray(np.tril(np.ones((CHUNK, CHUNK), dtype=bool)))
    dmask = jnp.where(tril, jnp.exp((idx[:, None] - idx[None, :]) * lg[:, None, None]), 0.0)
    qdec = jnp.exp((idx[None, :] + 1.0) * lg[:, None])
    kdec = jnp.exp((CHUNK - 1.0 - idx[None, :]) * lg[:, None])
    gch = jnp.exp(CHUNK * lg)
    qdec = jnp.broadcast_to(qdec[:, :, None], (RET_H, CHUNK, RET_D))
    kdec = jnp.broadcast_to(kdec[:, :, None], (RET_H, CHUNK, RET_D))
    gch = jnp.broadcast_to(gch[:, None, None], (RET_H, 8, LANES))
    return dmask, qdec, kdec, gch


def _ret_specs(N, rev):
    cn = (lambda n: N - 1 - n) if rev else (lambda n: n)
    blk = lambda off: pl.BlockSpec((CHUNK, RET_D), lambda h, n: (cn(n), off + h))
    tab = lambda a, b: pl.BlockSpec((None, a, b), lambda h, n: (h, 0, 0))
    rope = pl.BlockSpec((CHUNK, LANES), lambda h, n: (cn(n), 0))
    st = pl.BlockSpec((None, None, RET_D, RET_D), lambda h, n: (h, cn(n), 0, 0))
    return blk, tab, rope, st


def _ret_chunk_fwd(proj_m, cos, sin, tables):
    Lp = proj_m.shape[0]
    N = Lp // CHUNK
    dmask, qdec, kdec, gch = tables

    def body(q_ref, k_ref, v_ref, c_ref, s_ref, dm_ref, qd_ref, kd_ref, g_ref, o_ref, sin_ref, S):
        n = pl.program_id(1)

        @pl.when(n == 0)
        def _():
            S[...] = jnp.zeros_like(S)

        c = c_ref[...]
        s = s_ref[...]
        qr = _rot(q_ref[...], c, s)
        ks = _rot(k_ref[...], c, s) * (RET_D ** -0.5)
        v = v_ref[...]
        s0 = S[...]
        a = _dnt(qr, ks) * dm_ref[...]
        o_ref[...] = _d(a, v) + _d(qr * qd_ref[...], s0)
        sin_ref[...] = s0
        S[...] = s0 * g_ref[0:1, 0:1] + _dtn(ks * kd_ref[...], v)

    blk, tab, rope, st = _ret_specs(N, False)
    return pl.pallas_call(
        body, grid=(RET_H, N),
        in_specs=[blk(3 * RET_H), blk(4 * RET_H), blk(5 * RET_H), rope, rope,
                  tab(CHUNK, CHUNK), tab(CHUNK, RET_D), tab(CHUNK, RET_D), tab(8, LANES)],
        out_specs=[blk(0), st],
        out_shape=[jax.ShapeDtypeStruct((Lp, D_MODEL), F32), jax.ShapeDtypeStruct((RET_H, N, RET_D, RET_D), F32)],
        scratch_shapes=[pltpu.VMEM((RET_D, RET_D), F32)],
        name="ret_chunk_fwd")(proj_m, proj_m, proj_m, cos, sin, dmask, qdec, kdec, gch)


def _ret_chunk_bwd(proj_m, cos, sin, tables, do, s_in):
    Lp = proj_m.shape[0]
    N = Lp // CHUNK
    dmask, qdec, kdec, gch = tables

    def body(q_ref, k_ref, v_ref, c_ref, s_ref, dm_ref, qd_ref, kd_ref, g_ref, do_ref, sin_ref,
             dq_ref, dk_ref, dv_ref, dS):
        n = pl.program_id(1)

        @pl.when(n == 0)
        def _():
            dS[...] = jnp.zeros_like(dS)

        c = c_ref[...]
        s = s_ref[...]
        kscale = RET_D ** -0.5
        qr = _rot(q_ref[...], c, s)
        ks = _rot(k_ref[...], c, s) * kscale
        v = v_ref[...]
        dov = do_ref[...]
        s0 = sin_ref[...]
        dmk = dm_ref[...]
        qd = qd_ref[...]
        kd = kd_ref[...]
        dsv = dS[...]
        ad = _dnt(qr, ks) * dmk
        dv_ref[...] = (_dtn(ad, dov) + _d(ks * kd, dsv)).astype(BF16)
        da = _dnt(dov, v) * dmk
        dqr = _d(da, ks) + _dnt(dov, s0) * qd
        dks = _dtn(da, qr) + _dnt(v, dsv) * kd
        dS[...] = dsv * g_ref[0:1, 0:1] + _dtn(qr * qd, dov)
        dq_ref[...] = _rot_bwd(dqr, c, s).astype(BF16)
        dk_ref[...] = _rot_bwd(dks * kscale, c, s).astype(BF16)

    blk, tab, rope, st = _ret_specs(N, True)
    return pl.pallas_call(
        body, grid=(RET_H, N),
        in_specs=[blk(3 * RET_H), blk(4 * RET_H), blk(5 * RET_H), rope, rope,
                  tab(CHUNK, CHUNK), tab(CHUNK, RET_D), tab(CHUNK, RET_D), tab(8, LANES), blk(0), st],
        out_specs=[blk(0), blk(0), blk(0)],
        out_shape=[jax.ShapeDtypeStruct((Lp, D_MODEL), BF16)] * 3,
        scratch_shapes=[pltpu.VMEM((RET_D, RET_D), F32)],
        name="ret_chunk_bwd")(proj_m, proj_m, proj_m, cos, sin, dmask, qdec, kdec, gch, do, s_in)


def _merge_specs(tr):
    col = lambda j: pl.BlockSpec((tr, D_MODEL), lambda i: (i, j))
    return col


def _merge_fwd(o_a, o_b, proj_m, gnorm):
    Lp = o_a.shape[0]
    tr = _tile(Lp, 192, 16)

    def body(oa_ref, ob_ref, gz_ref, rg_ref, ga_ref, gb_ref, gn_ref, y_ref):
        gn = gn_ref[...]
        oa = oa_ref[...]
        ob = ob_ref[...]
        gz = gz_ref[...]
        ya = []
        for j in range(GDN_H):
            seg = oa[:, j * GDN_D:(j + 1) * GDN_D]
            r = lax.rsqrt(jnp.mean(seg * seg, axis=-1, keepdims=True) + EPS)
            ya.append(seg * r * gn)
        ya = jnp.concatenate(ya, axis=1) * (gz * _sig(gz))
        yb = []
        for j in range(RET_H):
            seg = ob[:, j * RET_D:(j + 1) * RET_D]
            r = lax.rsqrt(jnp.mean(seg * seg, axis=-1, keepdims=True) + EPS)
            yb.append(seg * r)
        rg = rg_ref[...]
        yb = jnp.concatenate(yb, axis=1) * (rg * _sig(rg))
        y_ref[...] = (_sig(ga_ref[...]) * ya + _sig(gb_ref[...]) * yb).astype(BF16)

    col = _merge_specs(tr)
    return pl.pallas_call(
        body, grid=(Lp // tr,),
        in_specs=[col(0), col(0), col(6), col(7), col(8), col(9), pl.BlockSpec((1, GDN_D), lambda i: (0, 0))],
        out_specs=col(0), out_shape=jax.ShapeDtypeStruct((Lp, D_MODEL), BF16),
        name="merge_fwd")(o_a, o_b, proj_m, proj_m, proj_m, proj_m, gnorm)


def _merge_bwd(dy, o_a, o_b, proj_m, gnorm):
    Lp = o_a.shape[0]
    tr = _tile(Lp, 192, 16)

    def body(dy_ref, oa_ref, ob_ref, gz_ref, rg_ref, ga_ref, gb_ref, gn_ref, dc_ref, doa_ref, dob_ref, dgn_ref):
        i = pl.program_id(0)
        gn = gn_ref[...]
        dyv = dy_ref[...]
        oa = oa_ref[...]
        ob = ob_ref[...]
        gz = gz_ref[...]
        rg = rg_ref[...]
        sa = _sig(ga_ref[...])
        sb = _sig(gb_ref[...])
        dya = dyv * sa
        dyb = dyv * sb
        sgz = _sig(gz)
        szz = gz * sgz
        dgn = jnp.zeros((1, GDN_D), F32)
        ya = []
        dgz = []
        for j in range(GDN_H):
            sl = slice(j * GDN_D, (j + 1) * GDN_D)
            seg = oa[:, sl]
            r = lax.rsqrt(jnp.mean(seg * seg, axis=-1, keepdims=True) + EPS)
            xh = seg * r
            oan = xh * gn
            ya.append(oan * szz[:, sl])
            dgz.append(dya[:, sl] * oan * (sgz[:, sl] * (1.0 + gz[:, sl] * (1.0 - sgz[:, sl]))))
            doan = dya[:, sl] * szz[:, sl]
            dgn = dgn + jnp.sum(doan * xh, axis=0, keepdims=True)
            dxh = doan * gn
            doa_ref[:, sl] = r * (dxh - xh * jnp.mean(dxh * xh, axis=-1, keepdims=True))
        ya = jnp.concatenate(ya, axis=1)
        srg = _sig(rg)
        srr = rg * srg
        yb = []
        drg = []
        for j in range(RET_H):
            sl = slice(j * RET_D, (j + 1) * RET_D)
            seg = ob[:, sl]
            r = lax.rsqrt(jnp.mean(seg * seg, axis=-1, keepdims=True) + EPS)
            xh = seg * r
            yb.append(xh * srr[:, sl])
            drg.append(dyb[:, sl] * xh * (srg[:, sl] * (1.0 + rg[:, sl] * (1.0 - srg[:, sl]))))
            dxh = dyb[:, sl] * srr[:, sl]
            dob_ref[:, sl] = r * (dxh - xh * jnp.mean(dxh * xh, axis=-1, keepdims=True))
        yb = jnp.concatenate(yb, axis=1)
        dc_ref[:, 0:D_MODEL] = jnp.concatenate(dgz, axis=1).astype(BF16)
        dc_ref[:, D_MODEL:2 * D_MODEL] = jnp.concatenate(drg, axis=1).astype(BF16)
        dc_ref[:, 2 * D_MODEL:3 * D_MODEL] = (dyv * ya * sa * (1.0 - sa)).astype(BF16)
        dc_ref[:, 3 * D_MODEL:] = (dyv * yb * sb * (1.0 - sb)).astype(BF16)

        @pl.when(i == 0)
        def _():
            dgn_ref[...] = dgn

        @pl.when(i > 0)
        def _():
            dgn_ref[...] += dgn

    col = _merge_specs(tr)
    return pl.pallas_call(
        body, grid=(Lp // tr,),
        in_specs=[col(0), col(0), col(0), col(6), col(7), col(8), col(9), pl.BlockSpec((1, GDN_D), lambda i: (0, 0))],
        out_specs=[pl.BlockSpec((tr, 4 * D_MODEL), lambda i: (i, 0)), col(0), col(0),
                   pl.BlockSpec((1, GDN_D), lambda i: (0, 0))],
        out_shape=[jax.ShapeDtypeStruct((Lp, 4 * D_MODEL), BF16), jax.ShapeDtypeStruct((Lp, D_MODEL), F32),
                   jax.ShapeDtypeStruct((Lp, D_MODEL), F32), jax.ShapeDtypeStruct((1, GDN_D), F32)],
        name="merge_bwd")(dy, o_a, o_b, proj_m, proj_m, proj_m, proj_m, gnorm)


def _ffn_conv(ext, w, b, tr, lo):
    acc = b + w[0:1, :] * ext[lo:lo + tr, :]
    for kk in range(1, FFN_CONV):
        acc = acc + w[kk:kk + 1, :] * ext[lo + kk:lo + kk + tr, :]
    return acc


def _ffn_act(up, conv_w, conv_b):
    Lp = up.shape[0]
    tr = _tile(Lp, 192, 16)
    W2 = 2 * D_FF

    def body(main_ref, prev_ref, w_ref, b_ref, act_ref):
        i = pl.program_id(0)
        prev = jnp.where(i > 0, prev_ref[...], 0.0)
        ext = jnp.concatenate([prev, main_ref[...]], axis=0)
        u = _ffn_conv(ext, w_ref[...], b_ref[...], tr, 8 - (FFN_CONV - 1))
        a = u[:, :D_FF]
        act_ref[...] = (a * _sig(a) * u[:, D_FF:]).astype(BF16)

    return pl.pallas_call(
        body, grid=(Lp // tr,),
        in_specs=[pl.BlockSpec((tr, W2), lambda i: (i, 0)), _halo_prev(tr, W2),
                  pl.BlockSpec((FFN_CONV, W2), lambda i: (0, 0)), pl.BlockSpec((1, W2), lambda i: (0, 0))],
        out_specs=pl.BlockSpec((tr, D_FF), lambda i: (i, 0)),
        out_shape=jax.ShapeDtypeStruct((Lp, D_FF), BF16), name="ffn_act")(up, up, conv_w, conv_b)


def _ffn_act_bwd(up, dact, conv_w, conv_b):
    Lp = up.shape[0]
    tr = _tile(Lp, 96, 16)
    W2 = 2 * D_FF
    te = tr + 8

    def body(main_ref, prev_ref, next_ref, da_ref, dan_ref, w_ref, b_ref, dup_ref, acc_ref):
        i = pl.program_id(0)
        w = w_ref[...]
        prev = jnp.where(i > 0, prev_ref[...], 0.0)
        ext = jnp.concatenate([prev, main_ref[...], next_ref[...]], axis=0)
        u = _ffn_conv(ext, w, b_ref[...], te, 8 - (FFN_CONV - 1))
        a = u[:, :D_FF]
        b = u[:, D_FF:]
        rowe = i * tr + lax.broadcasted_iota(jnp.int32, (te, 1), 0)
        dae = jnp.where(rowe < Lp, jnp.concatenate([da_ref[...], dan_ref[...]], axis=0), 0.0)
        sg = _sig(a)
        du = jnp.concatenate([dae * b * (sg * (1.0 + a * (1.0 - sg))), dae * (a * sg)], axis=1)
        acc = w[FFN_CONV - 1:FFN_CONV, :] * du[0:tr, :]
        for kk in range(FFN_CONV - 1):
            sh = FFN_CONV - 1 - kk
            acc = acc + w[kk:kk + 1, :] * du[sh:sh + tr, :]
        dup_ref[...] = acc.astype(BF16)
        dum = du[0:tr, :]
        lo = 8 - (FFN_CONV - 1)
        rows = [jnp.sum(dum * ext[lo + kk:lo + kk + tr, :], axis=0, keepdims=True) for kk in range(FFN_CONV)]
        rows.append(jnp.sum(dum, axis=0, keepdims=True))
        part = jnp.concatenate(rows + [jnp.zeros((8 - len(rows), W2), F32)], axis=0)

        @pl.when(i == 0)
        def _():
            acc_ref[...] = part

        @pl.when(i > 0)
        def _():
            acc_ref[...] += part

    return pl.pallas_call(
        body, grid=(Lp // tr,),
        in_specs=[pl.BlockSpec((tr, W2), lambda i: (i, 0)), _halo_prev(tr, W2), _halo_next(tr, W2, Lp),
                  pl.BlockSpec((tr, D_FF), lambda i: (i, 0)), _halo_next(tr, D_FF, Lp),
                  pl.BlockSpec((FFN_CONV, W2), lambda i: (0, 0)), pl.BlockSpec((1, W2), lambda i: (0, 0))],
        out_specs=[pl.BlockSpec((tr, W2), lambda i: (i, 0)), pl.BlockSpec((8, W2), lambda i: (0, 0))],
        out_shape=[jax.ShapeDtypeStruct((Lp, W2), BF16), jax.ShapeDtypeStruct((8, W2), F32)],
        name="ffn_act_bwd")(up, up, up, dact, dact, conv_w, conv_b)


def _local_step(hpad, tgt, pad, wt):
    Lp = hpad.shape[0]
    first = pad + N_META
    pos = jnp.arange(Lp, dtype=F32) - float(pad)
    half = RET_D // 2
    inv = 1.0 / (ROPE_BASE ** (jnp.arange(half, dtype=F32) / half))
    ang = pos[:, None] * inv[None, :]
    cos, sin = jnp.cos(ang), jnp.sin(ang)
    tables = _ret_tables()
    gparams = jnp.zeros((8, LANES), F32).at[0, :GDN_H].set(wt["a_log"]).at[1, :GDN_H].set(wt["dt_bias"])

    hn1 = _rms_fwd(hpad, wt["norm1"], "rms1_fwd")
    proj_m = _mm_nn(hn1, wt["w_main"], name="proj_main")
    proj_s = _mm_nn(hn1, wt["w_small"], name="proj_small")
    qkv, gsm = _gdn_pre(proj_m, proj_s, wt["gdn_conv_w"], gparams, pad)
    o_a, s_a, t_a = _gdn_chunk_fwd(qkv, gsm)
    o_b, s_b = _ret_chunk_fwd(proj_m, cos, sin, tables)
    y = _merge_fwd(o_a, o_b, proj_m, wt["gdn_norm"])
    h1 = _mm_nn(y, wt["w_out"], res=hpad, name="out_proj")
    hn2 = _rms_fwd(h1, wt["norm2"], "rms2_fwd")
    up = _mm_nn(hn2, wt["w_up"], name="ffn_up")
    act = _ffn_act(up, wt["ffn_conv_w"], wt["ffn_conv_b"])
    h2 = _mm_nn(act, wt["w_down"], res=h1, name="ffn_down")
    lossvec, dh2, dh2b, d_norm_f = _final(h2, wt["norm_f"], tgt, first)

    d_w_down = _mm_tn(act, dh2b, name="dw_down")
    dact = _mm_nt(dh2b, wt["w_down"], name="d_act")
    dup, ffn_rows = _ffn_act_bwd(up, dact, wt["ffn_conv_w"], wt["ffn_conv_b"])
    d_w_up = _mm_tn(hn2, dup, name="dw_up")
    dhn2 = _mm_nt(dup, wt["w_up"], name="d_hn2")
    dh1, dh1b, d_norm2 = _rms_bwd(h1, wt["norm2"], dhn2, dh2, pad, "rms2_bwd")

    d_w_out = _mm_tn(y, dh1b, name="dw_out")
    dy = _mm_nt(dh1b, wt["w_out"], name="d_y")
    d_c, do_a, do_b, d_gnorm = _merge_bwd(dy, o_a, o_b, proj_m, wt["gdn_norm"])
    drq, drk, drv = _ret_chunk_bwd(proj_m, cos, sin, tables, do_b, s_b)
    dq, dk, dv, dgs = _gdn_chunk_bwd(qkv, gsm, do_a, s_a, t_a)
    d_a, d_s, conv_rows, gp_rows = _gdn_pre_bwd(proj_m, proj_s, wt["gdn_conv_w"], gparams, dq, dk, dv, dgs, pad)

    wm = wt["w_main"]
    segs = [(d_a, 0, 3 * D_MODEL), (drq, 3 * D_MODEL, D_MODEL), (drk, 4 * D_MODEL, D_MODEL),
            (drv, 5 * D_MODEL, D_MODEL), (d_c, 6 * D_MODEL, 4 * D_MODEL)]
    d_w_main = jnp.concatenate([_mm_tn(hn1, d, name="dw_in_%d" % i) for i, (d, _, _) in enumerate(segs)], axis=1)
    d_w_small = _mm_tn(hn1, d_s, name="dw_in_small")
    dhn1 = _mm_nt(d_s, wt["w_small"], name="d_hn1_small")
    for i, (d, off, width) in enumerate(segs):
        dhn1 = _mm_nt(d, wm[:, off:off + width], res=dhn1, name="d_hn1_%d" % i)
    dh0, _, d_norm1 = _rms_bwd(hpad, wt["norm1"], dhn1, dh1, pad, "rms1_bwd")

    grads = {
        "norm1": d_norm1, "w_main": d_w_main, "w_small": d_w_small, "gdn_conv_w": conv_rows[:GDN_CONV],
        "a_log": gp_rows[0, :GDN_H], "dt_bias": gp_rows[1, :GDN_H], "gdn_norm": d_gnorm, "w_out": d_w_out,
        "norm2": d_norm2, "w_up": d_w_up, "ffn_conv_w": ffn_rows[:FFN_CONV], "ffn_conv_b": ffn_rows[FFN_CONV:FFN_CONV + 1],
        "w_down": d_w_down, "norm_f": d_norm_f,
    }
    return lossvec, dh0, grads


def _peer(k):
    ix, iy, ic = lax.axis_index("x"), lax.axis_index("y"), lax.axis_index("c")
    px = 1 - ix if (k >> 2) & 1 else ix
    py = 1 - iy if (k >> 1) & 1 else iy
    pc = 1 - ic if k & 1 else ic
    return (px, py, pc), 4 * px + 2 * py + pc


def _all_gather(xs, name):
    R = xs.shape[0]

    def body(x_ref, out_ref, send_sems, recv_sems, local_sem):
        _, me = _peer(0)
        local = pltpu.make_async_copy(x_ref, out_ref.at[me], local_sem)
        local.start()
        sends = []
        for k in range(1, N_DEV):
            dev, _ = _peer(k)
            cp = pltpu.make_async_remote_copy(
                src_ref=x_ref, dst_ref=out_ref.at[me], send_sem=send_sems.at[k - 1], recv_sem=recv_sems.at[k - 1],
                device_id=dev, device_id_type=MESH_T)
            cp.start()
            sends.append(cp)
        for k in range(1, N_DEV):
            dev, idx = _peer(k)
            pltpu.make_async_remote_copy(
                src_ref=x_ref, dst_ref=out_ref.at[idx], send_sem=send_sems.at[k - 1], recv_sem=recv_sems.at[k - 1],
                device_id=dev, device_id_type=MESH_T).wait_recv()
        for cp in sends:
            cp.wait_send()
        local.wait()

    return pl.pallas_call(
        body, out_shape=jax.ShapeDtypeStruct((N_DEV, R, LANES), xs.dtype),
        in_specs=[pl.BlockSpec(memory_space=pl.ANY)], out_specs=pl.BlockSpec(memory_space=pl.ANY),
        scratch_shapes=[pltpu.SemaphoreType.DMA((N_DEV - 1,)), pltpu.SemaphoreType.DMA((N_DEV - 1,)),
                        pltpu.SemaphoreType.DMA],
        name=name)(xs)


def _all_to_all(g, name):
    R = g.shape[1]

    def body(g_ref, out_ref, send_sems, recv_sems, local_sem):
        _, me = _peer(0)
        local = pltpu.make_async_copy(g_ref.at[me], out_ref.at[0], local_sem)
        local.start()
        sends = []
        for k in range(1, N_DEV):
            dev, idx = _peer(k)
            cp = pltpu.make_async_remote_copy(
                src_ref=g_ref.at[idx], dst_ref=out_ref.at[k], send_sem=send_sems.at[k - 1],
                recv_sem=recv_sems.at[k - 1], device_id=dev, device_id_type=MESH_T)
            cp.start()
            sends.append(cp)
        for cp in sends:
            cp.wait_recv()
        for cp in sends:
            cp.wait_send()
        local.wait()

    return pl.pallas_call(
        body, out_shape=jax.ShapeDtypeStruct((N_DEV, R, LANES), g.dtype),
        in_specs=[pl.BlockSpec(memory_space=pl.ANY)], out_specs=pl.BlockSpec(memory_space=pl.ANY),
        scratch_shapes=[pltpu.SemaphoreType.DMA((N_DEV - 1,)), pltpu.SemaphoreType.DMA((N_DEV - 1,)),
                        pltpu.SemaphoreType.DMA],
        name=name)(g)


def _adamw(gslabs, w, m, v, name):
    R = w.shape[0]
    tr = _tile(R, 512, 8)
    c1 = 1.0 - ADAM_B1 ** ADAM_STEP
    c2 = 1.0 - ADAM_B2 ** ADAM_STEP

    def body(g_ref, w_ref, m_ref, v_ref, go_ref, d_ref, mo_ref, vo_ref):
        g = g_ref[0]
        for k in range(1, N_DEV):
            g = g + g_ref[k]
        mn = ADAM_B1 * m_ref[...] + (1.0 - ADAM_B1) * g
        vn = ADAM_B2 * v_ref[...] + (1.0 - ADAM_B2) * (g * g)
        m_hat = mn / c1
        v_hat = vn / c2
        go_ref[...] = g
        d_ref[...] = -ADAM_LR * (m_hat / (jnp.sqrt(v_hat) + ADAM_EPS) + ADAM_WD * w_ref[...])
        mo_ref[...] = mn
        vo_ref[...] = vn

    blk = pl.BlockSpec((tr, LANES), lambda i: (i, 0))
    return pl.pallas_call(
        body, grid=(R // tr,),
        in_specs=[pl.BlockSpec((N_DEV, tr, LANES), lambda i: (0, i, 0)), blk, blk, blk],
        out_specs=[blk] * 4, out_shape=[jax.ShapeDtypeStruct((R, LANES), F32)] * 4, name=name)(gslabs, w, m, v)


def _pack(arrs, row_mult, dtype=F32):
    parts = []
    total = 0
    for a in arrs:
        f = a.reshape(-1).astype(dtype)
        n = -(-f.shape[0] // 1024) * 1024
        parts.append(jnp.pad(f, (0, n - f.shape[0])))
        total += n
    rows = total // LANES
    rows_p = -(-rows // row_mult) * row_mult
    flat = jnp.concatenate(parts)
    flat = jnp.pad(flat, (0, rows_p * LANES - total))
    return flat.reshape(rows_p, LANES)


def _unpack(packed, shapes):
    lead = packed.shape[:-2]
    flat = packed.reshape(lead + (-1,))
    out = []
    off = 0
    for s in shapes:
        n = int(np.prod(s))
        out.append(flat[..., off:off + n].reshape(lead + tuple(s)))
        off += -(-n // 1024) * 1024
    return out


def _gather_cols(stacked):
    d, r, c = stacked.shape
    return stacked.transpose(1, 0, 2).reshape(r, d * c)


def _scatter_cols(full):
    r, n = full.shape
    return full.reshape(r, N_DEV, n // N_DEV).transpose(1, 0, 2)


def kernel(x, meta, norm1, w_in, gdn_conv_w, gdn_a_log, gdn_dt_bias, gdn_norm, w_out, norm2, w_ffn_up, ffn_conv_w, ffn_conv_b, w_ffn_down, norm_f, loss_target, m_meta, m_norm1, m_w_in, m_gdn_conv_w, m_gdn_a_log, m_gdn_dt_bias, m_gdn_norm, m_w_out, m_norm2, m_w_ffn_up, m_ffn_conv_w, m_ffn_conv_b, m_w_ffn_down, m_norm_f, v_meta, v_norm1, v_w_in, v_gdn_conv_w, v_gdn_a_log, v_gdn_dt_bias, v_gdn_norm, v_w_out, v_norm2, v_w_ffn_up, v_ffn_conv_w, v_ffn_conv_b, v_w_ffn_down, v_norm_f):
    S = x.shape[1]
    L = N_META + S
    pad = (-L) % CHUNK
    Lp = L + pad

    big = [w_in, w_out, w_ffn_up, w_ffn_down]
    small = [meta, gdn_conv_w, ffn_conv_w]
    big_all = _all_gather(_pack(big, 16, BF16), "gather_matmul_weights")
    small_all = _all_gather(_pack(small, 8), "gather_small_weights")
    w_in_s, w_out_s, w_up_s, w_down_s = _unpack(big_all, [a.shape for a in big])
    meta_s, gconv_s, fconv_s = _unpack(small_all, [a.shape for a in small])
    w_in_f = _gather_cols(w_in_s[:, 0])
    w_main = jnp.concatenate([w_in_f[:, _O_GQ:_O_GZ], w_in_f[:, _O_RQ:_O_RG], w_in_f[:, _O_GZ:_O_GA],
                              w_in_f[:, _O_RG:_O_END]], axis=1)
    w_small = jnp.pad(w_in_f[:, _O_GA:_O_RQ], ((0, 0), (0, LANES - 2 * GDN_H)))
    wt = {
        "norm1": norm1, "w_main": w_main, "w_small": w_small,
        "gdn_conv_w": _gather_cols(gconv_s[:, 0]), "a_log": gdn_a_log[0], "dt_bias": gdn_dt_bias[0],
        "gdn_norm": gdn_norm, "w_out": w_out_s.reshape(D_MODEL, D_MODEL), "norm2": norm2,
        "w_up": _gather_cols(w_up_s[:, 0]), "ffn_conv_w": _gather_cols(fconv_s[:, 0]), "ffn_conv_b": ffn_conv_b,
        "w_down": w_down_s.reshape(D_FF, D_MODEL), "norm_f": norm_f.reshape(1, D_MODEL),
    }
    meta_f = _gather_cols(meta_s)

    hpad = jnp.concatenate([jnp.zeros((pad, D_MODEL), F32), meta_f, x[0]], axis=0)
    tgt = jnp.concatenate([jnp.zeros((pad + N_META, D_MODEL), F32), loss_target[0]], axis=0)
    lossvec, dh0, gr = _local_step(hpad, tgt, pad, wt)

    loss = lax.psum(jnp.sum(lossvec), ("x", "y", "c"))
    grad_x = dh0[pad + N_META:][None]

    gm = gr["w_main"]
    d_w_in = jnp.concatenate([gm[:, 0:3072], gm[:, 6144:7168], gr["w_small"][:, :2 * GDN_H], gm[:, 3072:6144],
                              gm[:, 7168:]], axis=1)
    g_sh = [
        _scatter_cols(dh0[pad:pad + N_META]),
        _scatter_cols(d_w_in)[:, None],
        _scatter_cols(gr["gdn_conv_w"])[:, None],
        gr["w_out"].reshape(N_DEV, 1, D_MODEL // N_DEV, D_MODEL),
        _scatter_cols(gr["w_up"])[:, None],
        _scatter_cols(gr["ffn_conv_w"])[:, None],
        gr["w_down"].reshape(N_DEV, 1, D_FF // N_DEV, D_MODEL),
    ]
    sh_w = [meta, w_in, gdn_conv_w, w_out, w_ffn_up, ffn_conv_w, w_ffn_down]
    sh_m = [m_meta, m_w_in, m_gdn_conv_w, m_w_out, m_w_ffn_up, m_ffn_conv_w, m_w_ffn_down]
    sh_v = [v_meta, v_w_in, v_gdn_conv_w, v_w_out, v_w_ffn_up, v_ffn_conv_w, v_w_ffn_down]
    g_packed = jnp.stack([_pack([g[d] for g in g_sh], 512) for d in range(N_DEV)])
    slabs = _all_to_all(g_packed, "exchange_gradients")
    sh_out = _adamw(slabs, _pack(sh_w, 512), _pack(sh_m, 512), _pack(sh_v, 512), "adamw_sharded")
    sh_shapes = [a.shape for a in sh_w]
    sh_g, sh_d, sh_nm, sh_nv = [_unpack(o, sh_shapes) for o in sh_out]

    rep_w = [norm1, gdn_a_log, gdn_dt_bias, gdn_norm, norm2, ffn_conv_b, norm_f]
    rep_m = [m_norm1, m_gdn_a_log, m_gdn_dt_bias, m_gdn_norm, m_norm2, m_ffn_conv_b, m_norm_f]
    rep_v = [v_norm1, v_gdn_a_log, v_gdn_dt_bias, v_gdn_norm, v_norm2, v_ffn_conv_b, v_norm_f]
    rep_g = [gr["norm1"], gr["a_log"], gr["dt_bias"], gr["gdn_norm"], gr["norm2"], gr["ffn_conv_b"], gr["norm_f"]]
    rep_slabs = _all_gather(_pack(rep_g, 8), "gather_small_gradients")
    rep_out = _adamw(rep_slabs, _pack(rep_w, 8), _pack(rep_m, 8), _pack(rep_v, 8), "adamw_replicated")
    rep_shapes = [a.shape for a in rep_w]
    rp_g, rp_d, rp_nm, rp_nv = [_unpack(o, rep_shapes) for o in rep_out]

    def order(sh, rp):
        return [sh[0], rp[0], sh[1], sh[2], rp[1], rp[2], rp[3], sh[3], rp[4], sh[4], sh[5], rp[5], sh[6], rp[6]]

    return (loss, grad_x, *order(sh_g, rp_g), *order(sh_d, rp_d), *order(sh_nm, rp_nm), *order(sh_nv, rp_nv))
```

```python
import functools
import math

import numpy as np
import jax
import jax.numpy as jnp
from jax import lax
from jax.experimental import pallas as pl
from jax.experimental.pallas import tpu as pltpu

F32 = jnp.float32
BF16 = jnp.bfloat16
HI = lax.Precision.HIGHEST

D_MODEL = 1024
N_META = 16
CHUNK = 64
GDN_H = 8
GDN_D = 128
RET_H = 4
RET_D = 256
D_FF = 2816
GDN_CONV = 4
FFN_CONV = 3
ROPE_BASE = 10000.0
EPS = 1e-6
N_DEV = 8
LANES = 128
MAIN_W = 10 * 1024
_O_GQ, _O_GZ, _O_GA, _O_RQ, _O_RG, _O_GATE, _O_END = 0, 3072, 4096, 4112, 7184, 8208, 10256

ADAM_LR = 0.001
ADAM_B1 = 0.9
ADAM_B2 = 0.999
ADAM_EPS = 1e-08
ADAM_WD = 0.01
ADAM_STEP = 10

MESH_T = pl.DeviceIdType.MESH


def _tile(n, target, mult):
    best = None
    for d in range(mult, min(n, target) + 1, mult):
        if n % d == 0:
            best = d
    assert best is not None, (n, target, mult)
    return best


def _sig(x):
    return 1.0 / (1.0 + jnp.exp(-x))


def _d(a, b):
    return jnp.dot(a.astype(BF16), b.astype(BF16), preferred_element_type=F32)


def _dnt(a, b):
    return lax.dot_general(a.astype(BF16), b.astype(BF16), (((1,), (1,)), ((), ())), preferred_element_type=F32)


def _dtn(a, b):
    return lax.dot_general(a.astype(BF16), b.astype(BF16), (((0,), (0,)), ((), ())), preferred_element_type=F32)


def _dx(a, b):
    return jnp.dot(a, b, preferred_element_type=F32, precision=HI)


def _dxnt(a, b):
    return lax.dot_general(a, b, (((1,), (1,)), ((), ())), preferred_element_type=F32, precision=HI)


def _dxtn(a, b):
    return lax.dot_general(a, b, (((0,), (0,)), ((), ())), preferred_element_type=F32, precision=HI)


def _split(a):
    hi = a.astype(BF16)
    return hi, (a - hi.astype(F32)).astype(BF16)


def _d3g(a, b, dims):
    ah, al = _split(a)
    bh, bl = _split(b)
    f = functools.partial(lax.dot_general, dimension_numbers=dims, preferred_element_type=F32)
    return f(ah, bh) + (f(ah, bl) + f(al, bh))


_NN = (((1,), (0,)), ((), ()))
_NT = (((1,), (1,)), ((), ()))
_TN = (((0,), (0,)), ((), ()))


def _rowsum(x):
    return jnp.sum(x, axis=1, keepdims=True)


def _allsum(x):
    return jnp.sum(jnp.sum(x, axis=1, keepdims=True), axis=0, keepdims=True)


def _mm_nn(a, b, res=None, out_dtype=F32, name="mm_nn"):
    M, K = a.shape
    N = b.shape[1]
    tm = _tile(M, 704, 16)
    tn = _tile(N, 2816, 128)

    def body(*refs):
        if res is None:
            a_ref, b_ref, o_ref = refs
        else:
            a_ref, b_ref, r_ref, o_ref = refs
        acc = jnp.dot(a_ref[...], b_ref[...], preferred_element_type=F32)
        if res is not None:
            acc = acc + r_ref[...]
        o_ref[...] = acc.astype(out_dtype)

    in_specs = [pl.BlockSpec((tm, K), lambda j, i: (i, 0)), pl.BlockSpec((K, tn), lambda j, i: (0, j))]
    args = [a, b]
    if res is not None:
        in_specs.append(pl.BlockSpec((tm, tn), lambda j, i: (i, j)))
        args.append(res)
    return pl.pallas_call(
        body, grid=(N // tn, M // tm), in_specs=in_specs,
        out_specs=pl.BlockSpec((tm, tn), lambda j, i: (i, j)),
        out_shape=jax.ShapeDtypeStruct((M, N), out_dtype), name=name)(*args)


def _mm_nt(a, b, res=None, name="mm_nt"):
    M, Nc = a.shape
    K = b.shape[0]
    tm = _tile(M, 704, 16)
    tc = _tile(Nc, 2048, 128)

    def body(*refs):
        if res is None:
            a_ref, b_ref, o_ref = refs
        else:
            a_ref, b_ref, r_ref, o_ref = refs
        c = pl.program_id(1)
        p = lax.dot_general(a_ref[...], b_ref[...], (((1,), (1,)), ((), ())), preferred_element_type=F32)

        @pl.when(c == 0)
        def _():
            if res is None:
                o_ref[...] = p
            else:
                o_ref[...] = p + r_ref[...]

        @pl.when(c > 0)
        def _():
            o_ref[...] += p

    in_specs = [pl.BlockSpec((tm, tc), lambda i, c: (i, c)), pl.BlockSpec((K, tc), lambda i, c: (0, c))]
    args = [a, b]
    if res is not None:
        in_specs.append(pl.BlockSpec((tm, K), lambda i, c: (i, 0)))
        args.append(res)
    return pl.pallas_call(
        body, grid=(M // tm, Nc // tc), in_specs=in_specs,
        out_specs=pl.BlockSpec((tm, K), lambda i, c: (i, 0)),
        out_shape=jax.ShapeDtypeStruct((M, K), F32), name=name)(*args)


def _mm_tn(a, b, name="mm_tn"):
    M, K = a.shape
    N = b.shape[1]
    tm = _tile(M, 704, 16)
    tk = _tile(K, 1408, 128)
    tn = _tile(N, 2048, 128)

    def body(a_ref, b_ref, o_ref):
        m = pl.program_id(2)
        p = lax.dot_general(a_ref[...], b_ref[...], (((0,), (0,)), ((), ())), preferred_element_type=F32)

        @pl.when(m == 0)
        def _():
            o_ref[...] = p

        @pl.when(m > 0)
        def _():
            o_ref[...] += p

    return pl.pallas_call(
        body, grid=(K // tk, N // tn, M // tm),
        in_specs=[pl.BlockSpec((tm, tk), lambda kk, j, m: (m, kk)), pl.BlockSpec((tm, tn), lambda kk, j, m: (m, j))],
        out_specs=pl.BlockSpec((tk, tn), lambda kk, j, m: (kk, j)),
        out_shape=jax.ShapeDtypeStruct((K, N), F32), name=name)(a, b)


def _rms_fwd(x, g, name):
    Lp = x.shape[0]
    tr = _tile(Lp, 256, 16)

    def body(x_ref, g_ref, o_ref):
        xv = x_ref[...]
        r = lax.rsqrt(jnp.mean(xv * xv, axis=-1, keepdims=True) + EPS)
        o_ref[...] = (xv * r * g_ref[...]).astype(BF16)

    return pl.pallas_call(
        body, grid=(Lp // tr,),
        in_specs=[pl.BlockSpec((tr, D_MODEL), lambda i: (i, 0)), pl.BlockSpec((1, D_MODEL), lambda i: (0, 0))],
        out_specs=pl.BlockSpec((tr, D_MODEL), lambda i: (i, 0)),
        out_shape=jax.ShapeDtypeStruct((Lp, D_MODEL), BF16), name=name)(x, g)


def _rms_bwd(x, g, dy, dres, pad, name):
    Lp = x.shape[0]
    tr = _tile(Lp, 256, 16)

    def body(x_ref, g_ref, dy_ref, dr_ref, dx_ref, dxb_ref, dg_ref):
        i = pl.program_id(0)
        xv = x_ref[...]
        r = lax.rsqrt(jnp.mean(xv * xv, axis=-1, keepdims=True) + EPS)
        xh = xv * r
        dyv = dy_ref[...]
        dxh = dyv * g_ref[...]
        dx = r * (dxh - xh * jnp.mean(dxh * xh, axis=-1, keepdims=True)) + dr_ref[...]
        row = i * tr + lax.broadcasted_iota(jnp.int32, (tr, 1), 0)
        dx = jnp.where(row >= pad, dx, 0.0)
        dx_ref[...] = dx
        dxb_ref[...] = dx.astype(BF16)
        part = jnp.sum(dyv * xh, axis=0, keepdims=True)

        @pl.when(i == 0)
        def _():
            dg_ref[...] = part

        @pl.when(i > 0)
        def _():
            dg_ref[...] += part

    blk = pl.BlockSpec((tr, D_MODEL), lambda i: (i, 0))
    vec = pl.BlockSpec((1, D_MODEL), lambda i: (0, 0))
    return pl.pallas_call(
        body, grid=(Lp // tr,), in_specs=[blk, vec, blk, blk], out_specs=[blk, blk, vec],
        out_shape=[jax.ShapeDtypeStruct((Lp, D_MODEL), F32), jax.ShapeDtypeStruct((Lp, D_MODEL), BF16),
                   jax.ShapeDtypeStruct((1, D_MODEL), F32)], name=name)(x, g, dy, dres)


def _final(h2, g, tgt, first_row):
    Lp = h2.shape[0]
    tr = _tile(Lp, 256, 16)

    def body(x_ref, g_ref, t_ref, loss_ref, dx_ref, dxb_ref, dg_ref):
        i = pl.program_id(0)
        xv = x_ref[...]
        gv = g_ref[...]
        r = lax.rsqrt(jnp.mean(xv * xv, axis=-1, keepdims=True) + EPS)
        xh = xv * r
        row = i * tr + lax.broadcasted_iota(jnp.int32, (tr, 1), 0)
        err = jnp.where(row >= first_row, xh * gv - t_ref[...], 0.0)
        lpart = jnp.sum(err * err, axis=0, keepdims=True) * (0.5 / D_MODEL)
        dyv = err * (1.0 / D_MODEL)
        dxh = dyv * gv
        dx = r * (dxh - xh * jnp.mean(dxh * xh, axis=-1, keepdims=True))
        dx_ref[...] = dx
        dxb_ref[...] = dx.astype(BF16)
        part = jnp.sum(dyv * xh, axis=0, keepdims=True)

        @pl.when(i == 0)
        def _():
            dg_ref[...] = part
            loss_ref[...] = lpart

        @pl.when(i > 0)
        def _():
            dg_ref[...] += part
            loss_ref[...] += lpart

    blk = pl.BlockSpec((tr, D_MODEL), lambda i: (i, 0))
    vec = pl.BlockSpec((1, D_MODEL), lambda i: (0, 0))
    return pl.pallas_call(
        body, grid=(Lp // tr,), in_specs=[blk, vec, blk], out_specs=[vec, blk, blk, vec],
        out_shape=[jax.ShapeDtypeStruct((1, D_MODEL), F32), jax.ShapeDtypeStruct((Lp, D_MODEL), F32),
                   jax.ShapeDtypeStruct((Lp, D_MODEL), BF16), jax.ShapeDtypeStruct((1, D_MODEL), F32)],
        name="final_norm_loss")(h2, g, tgt)


def _halo_prev(tr, width, col=0):
    return pl.BlockSpec((8, width), lambda i: (jnp.maximum(i * (tr // 8) - 1, 0), col))


def _halo_next(tr, width, nrows, col=0):
    last = nrows // 8 - 1
    return pl.BlockSpec((8, width), lambda i: (jnp.minimum((i + 1) * (tr // 8), last), col))


def _gdn_conv(ext, w, tr, lo):
    acc = w[0:1, :] * ext[lo:lo + tr, :]
    for kk in range(1, GDN_CONV):
        acc = acc + w[kk:kk + 1, :] * ext[lo + kk:lo + kk + tr, :]
    return acc


def _gdn_pre(proj_m, proj_s, conv_w, gparams, pad):
    Lp = proj_m.shape[0]
    tr = _tile(Lp, 192, 64)
    W3 = 3 * D_MODEL

    def body(main_ref, prev_ref, s_ref, w_ref, gp_ref, qkv_ref, gsm_ref):
        i = pl.program_id(0)
        prev = jnp.where(i > 0, prev_ref[...], 0.0)
        ext = jnp.concatenate([prev, main_ref[...]], axis=0)
        c = _gdn_conv(ext, w_ref[...], tr, 8 - (GDN_CONV - 1))
        s = c * _sig(c)
        scale = GDN_D ** -0.5
        for j in range(2 * GDN_H):
            seg = s[:, j * GDN_D:(j + 1) * GDN_D]
            r = lax.rsqrt(_rowsum(seg * seg) + EPS)
            if j < GDN_H:
                r = r * scale
            qkv_ref[:, j * GDN_D:(j + 1) * GDN_D] = seg * r
        qkv_ref[:, 2 * D_MODEL:] = s[:, 2 * D_MODEL:]
        sm = s_ref[...]
        gp = gp_ref[...]
        lane = lax.broadcasted_iota(jnp.int32, sm.shape, 1)
        z = sm + gp[1:2, :]
        softplus = jnp.maximum(z, 0.0) + jnp.log(1.0 + jnp.exp(-jnp.abs(z)))
        lg = -jnp.exp(gp[0:1, :]) * softplus
        row = i * tr + lax.broadcasted_iota(jnp.int32, (tr, 1), 0)
        out = jnp.where(lane < GDN_H, lg, jnp.where(lane < 2 * GDN_H, _sig(sm), 0.0))
        gsm_ref[...] = jnp.where(row >= pad, out, 0.0)

    return pl.pallas_call(
        body, grid=(Lp // tr,),
        in_specs=[pl.BlockSpec((tr, W3), lambda i: (i, 0)), _halo_prev(tr, W3),
                  pl.BlockSpec((tr, LANES), lambda i: (i, 0)),
                  pl.BlockSpec((GDN_CONV, W3), lambda i: (0, 0)), pl.BlockSpec((8, LANES), lambda i: (0, 0))],
        out_specs=[pl.BlockSpec((tr, W3), lambda i: (i, 0)), pl.BlockSpec((tr, LANES), lambda i: (i, 0))],
        out_shape=[jax.ShapeDtypeStruct((Lp, W3), F32), jax.ShapeDtypeStruct((Lp, LANES), F32)],
        name="gdn_pre")(proj_m, proj_m, proj_s, conv_w, gparams)


def _gdn_pre_bwd(proj_m, proj_s, conv_w, gparams, dq, dk, dv, dgs, pad):
    Lp = proj_m.shape[0]
    tr = _tile(Lp, 192, 64)
    W3 = 3 * D_MODEL
    te = tr + 8

    def body(main_ref, prev_ref, next_ref, s_ref, w_ref, gp_ref,
             dq_ref, dqn_ref, dk_ref, dkn_ref, dv_ref, dvn_ref, dgs_ref,
             da_ref, ds_ref, dw_ref, dgp_ref):
        i = pl.program_id(0)
        w = w_ref[...]
        prev = jnp.where(i > 0, prev_ref[...], 0.0)
        ext = jnp.concatenate([prev, main_ref[...], next_ref[...]], axis=0)
        c = _gdn_conv(ext, w, te, 8 - (GDN_CONV - 1))
        sg = _sig(c)
        s = c * sg
        rowe = i * tr + lax.broadcasted_iota(jnp.int32, (te, 1), 0)
        live = (rowe >= pad) & (rowe < Lp)
        dqe = jnp.concatenate([dq_ref[...], dqn_ref[...]], axis=0)
        dke = jnp.concatenate([dk_ref[...], dkn_ref[...]], axis=0)
        dve = jnp.concatenate([dv_ref[...], dvn_ref[...]], axis=0)
        scale = GDN_D ** -0.5
        parts = []
        for j in range(2 * GDN_H):
            seg = s[:, j * GDN_D:(j + 1) * GDN_D]
            r = lax.rsqrt(_rowsum(seg * seg) + EPS)
            xh = seg * r
            if j < GDN_H:
                dxh = dqe[:, j * GDN_D:(j + 1) * GDN_D] * scale
            else:
                dxh = dke[:, (j - GDN_H) * GDN_D:(j - GDN_H + 1) * GDN_D]
            parts.append(r * (dxh - xh * _rowsum(dxh * xh)))
        parts.append(dve)
        dsv = jnp.concatenate(parts, axis=1)
        dc = jnp.where(live, dsv * (sg * (1.0 + c * (1.0 - sg))), 0.0)
        acc = w[GDN_CONV - 1:GDN_CONV, :] * dc[0:tr, :]
        for kk in range(GDN_CONV - 1):
            sh = GDN_CONV - 1 - kk
            acc = acc + w[kk:kk + 1, :] * dc[sh:sh + tr, :]
        da_ref[...] = acc.astype(BF16)
        dcm = dc[0:tr, :]
        rows = [jnp.sum(dcm * ext[8 - (GDN_CONV - 1) + kk:8 - (GDN_CONV - 1) + kk + tr, :], axis=0, keepdims=True)
                for kk in range(GDN_CONV)]
        dwp = jnp.concatenate(rows + [jnp.zeros((8 - GDN_CONV, W3), F32)], axis=0)

        sm = s_ref[...]
        gp = gp_ref[...]
        lane = lax.broadcasted_iota(jnp.int32, sm.shape, 1)
        rowm = i * tr + lax.broadcasted_iota(jnp.int32, (tr, 1), 0)
        dgv = jnp.where(rowm >= pad, dgs_ref[...], 0.0)
        dlg = jnp.where(lane < GDN_H, dgv, 0.0)
        dbt = jnp.where((lane >= GDN_H) & (lane < 2 * GDN_H), dgv, 0.0)
        z = sm + gp[1:2, :]
        softplus = jnp.maximum(z, 0.0) + jnp.log(1.0 + jnp.exp(-jnp.abs(z)))
        ea = jnp.exp(gp[0:1, :])
        dz = dlg * (-ea) * _sig(z)
        dal = dlg * (-ea) * softplus
        bt = _sig(sm)
        dgb = dbt * bt * (1.0 - bt)
        ds_ref[...] = (dz + dgb).astype(BF16)
        gpp = jnp.concatenate([jnp.sum(dal, axis=0, keepdims=True), jnp.sum(dz, axis=0, keepdims=True),
                               jnp.zeros((6, LANES), F32)], axis=0)

        @pl.when(i == 0)
        def _():
            dw_ref[...] = dwp
            dgp_ref[...] = gpp

        @pl.when(i > 0)
        def _():
            dw_ref[...] += dwp
            dgp_ref[...] += gpp

    m3 = pl.BlockSpec((tr, W3), lambda i: (i, 0))
    m1 = pl.BlockSpec((tr, D_MODEL), lambda i: (i, 0))
    n1 = _halo_next(tr, D_MODEL, Lp)
    return pl.pallas_call(
        body, grid=(Lp // tr,),
        in_specs=[m3, _halo_prev(tr, W3), _halo_next(tr, W3, Lp), pl.BlockSpec((tr, LANES), lambda i: (i, 0)),
                  pl.BlockSpec((GDN_CONV, W3), lambda i: (0, 0)), pl.BlockSpec((8, LANES), lambda i: (0, 0)),
                  m1, n1, m1, n1, m1, n1, pl.BlockSpec((tr, LANES), lambda i: (i, 0))],
        out_specs=[m3, pl.BlockSpec((tr, LANES), lambda i: (i, 0)),
                   pl.BlockSpec((8, W3), lambda i: (0, 0)), pl.BlockSpec((8, LANES), lambda i: (0, 0))],
        out_shape=[jax.ShapeDtypeStruct((Lp, W3), BF16), jax.ShapeDtypeStruct((Lp, LANES), BF16),
                   jax.ShapeDtypeStruct((8, W3), F32), jax.ShapeDtypeStruct((8, LANES), F32)],
        name="gdn_pre_bwd")(proj_m, proj_m, proj_m, proj_s, conv_w, gparams, dq, dq, dk, dk, dv, dv, dgs)


def _gdn_gates(gs):
    ri = lax.broadcasted_iota(jnp.int32, (CHUNK, CHUNK), 0)
    ci = lax.broadcasted_iota(jnp.int32, (CHUNK, CHUNK), 1)
    tril = ri >= ci
    strict = ri > ci
    gall = _dx(tril.astype(F32), gs)
    lane8 = lax.broadcasted_iota(jnp.int32, (8, LANES), 1)
    sub8 = lax.broadcasted_iota(jnp.int32, (8, LANES), 0)
    grow = _dxnt((lane8 == sub8).astype(F32), gall)
    return gall, grow, tril, strict


def _gdn_decay(gall, grow, tril, h):
    g = gall[:, h:h + 1]
    return g, jnp.where(tril, jnp.exp(jnp.where(tril, g - grow[h:h + 1, :], 0.0)), 0.0)


def _gdn_chunk_specs(N, rev):
    cn = (lambda n: N - 1 - n) if rev else (lambda n: n)
    col = lambda j: pl.BlockSpec((CHUNK, D_MODEL), lambda n: (cn(n), j))
    gate = pl.BlockSpec((CHUNK, LANES), lambda n: (cn(n), 0))
    st = lambda a, b: pl.BlockSpec((GDN_H, None, a, b), lambda n: (0, cn(n), 0, 0))
    return col, gate, st


def _gdn_chunk_fwd(qkv, gsm):
    Lp = qkv.shape[0]
    N = Lp // CHUNK

    def body(q_ref, k_ref, v_ref, gs_ref, o_ref, sin_ref, t_ref, S):
        n = pl.program_id(0)

        @pl.when(n == 0)
        def _():
            S[...] = jnp.zeros_like(S)

        gs = gs_ref[...]
        gall, grow, tril, strict = _gdn_gates(gs)
        ri = lax.broadcasted_iota(jnp.int32, (CHUNK, CHUNK), 0)
        ci = lax.broadcasted_iota(jnp.int32, (CHUNK, CHUNK), 1)
        eye = (ri == ci).astype(F32)
        for h in range(GDN_H):
            sl = slice(h * GDN_D, (h + 1) * GDN_D)
            q = q_ref[:, sl]
            k = k_ref[:, sl]
            v = v_ref[:, sl]
            beta = gs[:, GDN_H + h:GDN_H + h + 1]
            g, gam = _gdn_decay(gall, grow, tril, h)
            eg = jnp.exp(g)
            gl = g[CHUNK - 1:CHUNK, :]
            kb = k * beta
            m = jnp.where(strict, _dnt(kb, k) * gam, 0.0)
            pw = -m
            t = eye + pw
            for _ in range(5):
                pw = _d3g(pw, pw, _NN)
                t = t + _d3g(t, pw, _NN)
            u = _d(t, v * beta)
            w = _d(t, kb * eg)
            p = _dnt(q, k) * gam
            s0 = S[h]
            vnew = u - _d(w, s0)
            o_ref[:, sl] = _d(q * eg, s0) + _d(p, vnew)
            sin_ref[h] = s0
            t_ref[h] = t
            S[h] = s0 * jnp.exp(gl) + _dtn(k * jnp.exp(gl - g), vnew)

    col, gate, st = _gdn_chunk_specs(N, False)
    return pl.pallas_call(
        body, grid=(N,),
        in_specs=[col(0), col(1), col(2), gate],
        out_specs=[col(0), st(GDN_D, GDN_D), st(CHUNK, CHUNK)],
        out_shape=[jax.ShapeDtypeStruct((Lp, D_MODEL), F32), jax.ShapeDtypeStruct((GDN_H, N, GDN_D, GDN_D), F32),
                   jax.ShapeDtypeStruct((GDN_H, N, CHUNK, CHUNK), F32)],
        scratch_shapes=[pltpu.VMEM((GDN_H, GDN_D, GDN_D), F32)],
        name="gdn_chunk_fwd")(qkv, qkv, qkv, gsm)


def _gdn_chunk_bwd(qkv, gsm, do, s_in, t_in):
    Lp = qkv.shape[0]
    N = Lp // CHUNK

    def body(q_ref, k_ref, v_ref, gs_ref, do_ref, sin_ref, t_ref, dq_ref, dk_ref, dv_ref, dgs_ref, dS):
        n = pl.program_id(0)

        @pl.when(n == 0)
        def _():
            dS[...] = jnp.zeros_like(dS)

        gs = gs_ref[...]
        gall, grow, tril, strict = _gdn_gates(gs)
        lane = lax.broadcasted_iota(jnp.int32, (CHUNK, LANES), 1)
        rcol = lax.broadcasted_iota(jnp.int32, (CHUNK, 1), 0)
        ones = jnp.ones((CHUNK, LANES), F32)
        dg_all = jnp.zeros((CHUNK, LANES), F32)
        dbeta_all = jnp.zeros((CHUNK, LANES), F32)
        for h in range(GDN_H):
            sl = slice(h * GDN_D, (h + 1) * GDN_D)
            q = q_ref[:, sl]
            k = k_ref[:, sl]
            v = v_ref[:, sl]
            dov = do_ref[:, sl]
            s0 = sin_ref[h]
            t = t_ref[h]
            beta = gs[:, GDN_H + h:GDN_H + h + 1]
            g, gam = _gdn_decay(gall, grow, tril, h)
            eg = jnp.exp(g)
            gl = g[CHUNK - 1:CHUNK, :]
            egl = jnp.exp(gl)
            e = jnp.exp(gl - g)
            kb = k * beta
            kbg = kb * eg
            vb = v * beta
            m = jnp.where(strict, _dnt(kb, k) * gam, 0.0)
            u = _d(t, vb)
            w = _d(t, kbg)
            p = _dnt(q, k) * gam
            vnew = u - _d(w, s0)
            qg = q * eg
            kd = k * e
            dsv = dS[h]

            dvnew = _dtn(p, dov) + _d(kd, dsv)
            dp = jnp.where(tril, _dnt(dov, vnew), 0.0)
            dqg = _dnt(dov, s0)
            dkd = _dnt(vnew, dsv)
            dw = -_dnt(dvnew, s0)
            dS[h] = _dtn(qg, dov) + egl * dsv - _dtn(w, dvnew)
            dvb = _dtn(t, dvnew)
            dkbg = _dtn(t, dw)
            dt = _dnt(dvnew, vb) + _dnt(dw, kbg)
            dm = jnp.where(strict, -_d3g(_d3g(t, dt, _TN), t, _NT), 0.0)
            dkk = dm * gam
            dqk = dp * gam
            dkb = _d(dkk, k) + eg * dkbg
            dk_ref[:, sl] = _dtn(dkk, kb) + _dtn(dqk, q) + dkd * e + beta * dkb
            dq_ref[:, sl] = _d(dqk, k) + dqg * eg
            dv_ref[:, sl] = beta * dvb
            dbeta = _rowsum(k * dkb) + _rowsum(v * dvb)
            em = dm * m + dp * p
            colsum = _d3g(em, ones, _TN)[:, 0:1]
            z = _rowsum(kd * dkd)
            dg = _rowsum(em) - colsum + _rowsum(qg * dqg) + _rowsum(kbg * dkbg) - z
            extra = _allsum(z) + egl * _allsum(s0 * dsv)
            dg = dg + jnp.where(rcol == CHUNK - 1, extra, 0.0)
            dg_all = dg_all + jnp.where(lane == h, dg, 0.0)
            dbeta_all = dbeta_all + jnp.where(lane == GDN_H + h, dbeta, 0.0)
        ri = lax.broadcasted_iota(jnp.int32, (CHUNK, CHUNK), 0)
        ci = lax.broadcasted_iota(jnp.int32, (CHUNK, CHUNK), 1)
        dgs_ref[...] = _dx((ci >= ri).astype(F32), dg_all) + dbeta_all

    col, gate, st = _gdn_chunk_specs(N, True)
    return pl.pallas_call(
        body, grid=(N,),
        in_specs=[col(0), col(1), col(2), gate, col(0), st(GDN_D, GDN_D), st(CHUNK, CHUNK)],
        out_specs=[col(0), col(0), col(0), gate],
        out_shape=[jax.ShapeDtypeStruct((Lp, D_MODEL), F32)] * 3 + [jax.ShapeDtypeStruct((Lp, LANES), F32)],
        scratch_shapes=[pltpu.VMEM((GDN_H, GDN_D, GDN_D), F32)],
        name="gdn_chunk_bwd")(qkv, qkv, qkv, gsm, do, s_in, t_in)


def _rot(x, c, s):
    half = RET_D // 2
    x1 = x[:, :half]
    x2 = x[:, half:]
    return jnp.concatenate([x1 * c - x2 * s, x2 * c + x1 * s], axis=1)


def _rot_bwd(d, c, s):
    half = RET_D // 2
    d1 = d[:, :half]
    d2 = d[:, half:]
    return jnp.concatenate([d1 * c + d2 * s, d2 * c - d1 * s], axis=1)


def _ret_tables():
    hh = jnp.arange(RET_H, dtype=F32)
    lg = jnp.log(1.0 - 2.0 ** (-5.0 - hh))
    idx = jnp.arange(CHUNK, dtype=F32)
    tril = jnp.asarray(np.tril(np.ones((CHUNK, CHUNK), dtype=bool)))
    dmask = jnp.where(tril, jnp.exp((idx[:, None] - idx[None, :]) * lg[:, None, None]), 0.0)
    qdec = jnp.exp((idx[None, :] + 1.0) * lg[:, None])
    kdec = jnp.exp((CHUNK - 1.0 - idx[None, :]) * lg[:, None])
    gch = jnp.exp(CHUNK * lg)
    qdec = jnp.broadcast_to(qdec[:, :, None], (RET_H, CHUNK, RET_D))
    kdec = jnp.broadcast_to(kdec[:, :, None], (RET_H, CHUNK, RET_D))
    gch = jnp.broadcast_to(gch[:, None, None], (RET_H, 8, LANES))
    return dmask, qdec, kdec, gch


def _ret_specs(N, rev):
    cn = (lambda n: N - 1 - n) if rev else (lambda n: n)
    col = lambda j: pl.BlockSpec((CHUNK, D_MODEL), lambda n: (cn(n), j))
    tab = lambda a, b: pl.BlockSpec((RET_H, a, b), lambda n: (0, 0, 0))
    rope = pl.BlockSpec((CHUNK, LANES), lambda n: (cn(n), 0))
    st = pl.BlockSpec((RET_H, None, RET_D, RET_D), lambda n: (0, cn(n), 0, 0))
    return col, tab, rope, st


def _ret_chunk_fwd(proj_m, cos, sin, tables):
    Lp = proj_m.shape[0]
    N = Lp // CHUNK
    dmask, qdec, kdec, gch = tables

    def body(q_ref, k_ref, v_ref, c_ref, s_ref, dm_ref, qd_ref, kd_ref, g_ref, o_ref, sin_ref, S):
        n = pl.program_id(0)

        @pl.when(n == 0)
        def _():
            S[...] = jnp.zeros_like(S)

        c = c_ref[...]
        s = s_ref[...]
        for h in range(RET_H):
            sl = slice(h * RET_D, (h + 1) * RET_D)
            qr = _rot(q_ref[:, sl], c, s)
            ks = _rot(k_ref[:, sl], c, s) * (RET_D ** -0.5)
            v = v_ref[:, sl]
            s0 = S[h]
            a = _dnt(qr, ks) * dm_ref[h]
            o_ref[:, sl] = _d(a, v) + _d(qr * qd_ref[h], s0)
            sin_ref[h] = s0
            S[h] = s0 * g_ref[h, 0:1, 0:1] + _dtn(ks * kd_ref[h], v)

    col, tab, rope, st = _ret_specs(N, False)
    return pl.pallas_call(
        body, grid=(N,),
        in_specs=[col(3), col(4), col(5), rope, rope,
                  tab(CHUNK, CHUNK), tab(CHUNK, RET_D), tab(CHUNK, RET_D), tab(8, LANES)],
        out_specs=[col(0), st],
        out_shape=[jax.ShapeDtypeStruct((Lp, D_MODEL), F32), jax.ShapeDtypeStruct((RET_H, N, RET_D, RET_D), F32)],
        scratch_shapes=[pltpu.VMEM((RET_H, RET_D, RET_D), F32)],
        name="ret_chunk_fwd")(proj_m, proj_m, proj_m, cos, sin, dmask, qdec, kdec, gch)


def _ret_chunk_bwd(proj_m, cos, sin, tables, do, s_in):
    Lp = proj_m.shape[0]
    N = Lp // CHUNK
    dmask, qdec, kdec, gch = tables

    def body(q_ref, k_ref, v_ref, c_ref, s_ref, dm_ref, qd_ref, kd_ref, g_ref, do_ref, sin_ref,
             dq_ref, dk_ref, dv_ref, dS):
        n = pl.program_id(0)

        @pl.when(n == 0)
        def _():
            dS[...] = jnp.zeros_like(dS)

        c = c_ref[...]
        s = s_ref[...]
        kscale = RET_D ** -0.5
        for h in range(RET_H):
            sl = slice(h * RET_D, (h + 1) * RET_D)
            qr = _rot(q_ref[:, sl], c, s)
            ks = _rot(k_ref[:, sl], c, s) * kscale
            v = v_ref[:, sl]
            dov = do_ref[:, sl]
            s0 = sin_ref[h]
            dmk = dm_ref[h]
            qd = qd_ref[h]
            kd = kd_ref[h]
            dsv = dS[h]
            ad = _dnt(qr, ks) * dmk
            dv_ref[:, sl] = (_dtn(ad, dov) + _d(ks * kd, dsv)).astype(BF16)
            da = _dnt(dov, v) * dmk
            dqr = _d(da, ks) + _dnt(dov, s0) * qd
            dks = _dtn(da, qr) + _dnt(v, dsv) * kd
            dS[h] = dsv * g_ref[h, 0:1, 0:1] + _dtn(qr * qd, dov)
            dq_ref[:, sl] = _rot_bwd(dqr, c, s).astype(BF16)
            dk_ref[:, sl] = _rot_bwd(dks * kscale, c, s).astype(BF16)

    col, tab, rope, st = _ret_specs(N, True)
    return pl.pallas_call(
        body, grid=(N,),
        in_specs=[col(3), col(4), col(5), rope, rope,
                  tab(CHUNK, CHUNK), tab(CHUNK, RET_D), tab(CHUNK, RET_D), tab(8, LANES), col(0), st],
        out_specs=[col(0), col(0), col(0)],
        out_shape=[jax.ShapeDtypeStruct((Lp, D_MODEL), BF16)] * 3,
        scratch_shapes=[pltpu.VMEM((RET_H, RET_D, RET_D), F32)],
        name="ret_chunk_bwd")(proj_m, proj_m, proj_m, cos, sin, dmask, qdec, kdec, gch, do, s_in)


def _merge_specs(tr):
    col = lambda j: pl.BlockSpec((tr, D_MODEL), lambda i: (i, j))
    return col


def _merge_fwd(o_a, o_b, proj_m, gnorm):
    Lp = o_a.shape[0]
    tr = _tile(Lp, 192, 16)

    def body(oa_ref, ob_ref, gz_ref, rg_ref, ga_ref, gb_ref, gn_ref, y_ref):
        gn = gn_ref[...]
        oa = oa_ref[...]
        ob = ob_ref[...]
        gz = gz_ref[...]
        ya = []
        for j in range(GDN_H):
            seg = oa[:, j * GDN_D:(j + 1) * GDN_D]
            r = lax.rsqrt(jnp.mean(seg * seg, axis=-1, keepdims=True) + EPS)
            ya.append(seg * r * gn)
        ya = jnp.concatenate(ya, axis=1) * (gz * _sig(gz))
        yb = []
        for j in range(RET_H):
            seg = ob[:, j * RET_D:(j + 1) * RET_D]
            r = lax.rsqrt(jnp.mean(seg * seg, axis=-1, keepdims=True) + EPS)
            yb.append(seg * r)
        rg = rg_ref[...]
        yb = jnp.concatenate(yb, axis=1) * (rg * _sig(rg))
        y_ref[...] = (_sig(ga_ref[...]) * ya + _sig(gb_ref[...]) * yb).astype(BF16)

    col = _merge_specs(tr)
    return pl.pallas_call(
        body, grid=(Lp // tr,),
        in_specs=[col(0), col(0), col(6), col(7), col(8), col(9), pl.BlockSpec((1, GDN_D), lambda i: (0, 0))],
        out_specs=col(0), out_shape=jax.ShapeDtypeStruct((Lp, D_MODEL), BF16),
        name="merge_fwd")(o_a, o_b, proj_m, proj_m, proj_m, proj_m, gnorm)


def _merge_bwd(dy, o_a, o_b, proj_m, gnorm):
    Lp = o_a.shape[0]
    tr = _tile(Lp, 192, 16)

    def body(dy_ref, oa_ref, ob_ref, gz_ref, rg_ref, ga_ref, gb_ref, gn_ref, dc_ref, doa_ref, dob_ref, dgn_ref):
        i = pl.program_id(0)
        gn = gn_ref[...]
        dyv = dy_ref[...]
        oa = oa_ref[...]
        ob = ob_ref[...]
        gz = gz_ref[...]
        rg = rg_ref[...]
        sa = _sig(ga_ref[...])
        sb = _sig(gb_ref[...])
        dya = dyv * sa
        dyb = dyv * sb
        sgz = _sig(gz)
        szz = gz * sgz
        dgn = jnp.zeros((1, GDN_D), F32)
        ya = []
        dgz = []
        for j in range(GDN_H):
            sl = slice(j * GDN_D, (j + 1) * GDN_D)
            seg = oa[:, sl]
            r = lax.rsqrt(jnp.mean(seg * seg, axis=-1, keepdims=True) + EPS)
            xh = seg * r
            oan = xh * gn
            ya.append(oan * szz[:, sl])
            dgz.append(dya[:, sl] * oan * (sgz[:, sl] * (1.0 + gz[:, sl] * (1.0 - sgz[:, sl]))))
            doan = dya[:, sl] * szz[:, sl]
            dgn = dgn + jnp.sum(doan * xh, axis=0, keepdims=True)
            dxh = doan * gn
            doa_ref[:, sl] = r * (dxh - xh * jnp.mean(dxh * xh, axis=-1, keepdims=True))
        ya = jnp.concatenate(ya, axis=1)
        srg = _sig(rg)
        srr = rg * srg
        yb = []
        drg = []
        for j in range(RET_H):
            sl = slice(j * RET_D, (j + 1) * RET_D)
            seg = ob[:, sl]
            r = lax.rsqrt(jnp.mean(seg * seg, axis=-1, keepdims=True) + EPS)
            xh = seg * r
            yb.append(xh * srr[:, sl])
            drg.append(dyb[:, sl] * xh * (srg[:, sl] * (1.0 + rg[:, sl] * (1.0 - srg[:, sl]))))
            dxh = dyb[:, sl] * srr[:, sl]
            dob_ref[:, sl] = r * (dxh - xh * jnp.mean(dxh * xh, axis=-1, keepdims=True))
        yb = jnp.concatenate(yb, axis=1)
        dc_ref[:, 0:D_MODEL] = jnp.concatenate(dgz, axis=1).astype(BF16)
        dc_ref[:, D_MODEL:2 * D_MODEL] = jnp.concatenate(drg, axis=1).astype(BF16)
        dc_ref[:, 2 * D_MODEL:3 * D_MODEL] = (dyv * ya * sa * (1.0 - sa)).astype(BF16)
        dc_ref[:, 3 * D_MODEL:] = (dyv * yb * sb * (1.0 - sb)).astype(BF16)

        @pl.when(i == 0)
        def _():
            dgn_ref[...] = dgn

        @pl.when(i > 0)
        def _():
            dgn_ref[...] += dgn

    col = _merge_specs(tr)
    return pl.pallas_call(
        body, grid=(Lp // tr,),
        in_specs=[col(0), col(0), col(0), col(6), col(7), col(8), col(9), pl.BlockSpec((1, GDN_D), lambda i: (0, 0))],
        out_specs=[pl.BlockSpec((tr, 4 * D_MODEL), lambda i: (i, 0)), col(0), col(0),
                   pl.BlockSpec((1, GDN_D), lambda i: (0, 0))],
        out_shape=[jax.ShapeDtypeStruct((Lp, 4 * D_MODEL), BF16), jax.ShapeDtypeStruct((Lp, D_MODEL), F32),
                   jax.ShapeDtypeStruct((Lp, D_MODEL), F32), jax.ShapeDtypeStruct((1, GDN_D), F32)],
        name="merge_bwd")(dy, o_a, o_b, proj_m, proj_m, proj_m, proj_m, gnorm)


def _ffn_conv(ext, w, b, tr, lo):
    acc = b + w[0:1, :] * ext[lo:lo + tr, :]
    for kk in range(1, FFN_CONV):
        acc = acc + w[kk:kk + 1, :] * ext[lo + kk:lo + kk + tr, :]
    return acc


def _ffn_act(up, conv_w, conv_b):
    Lp = up.shape[0]
    tr = _tile(Lp, 192, 16)
    W2 = 2 * D_FF

    def body(main_ref, prev_ref, w_ref, b_ref, act_ref):
        i = pl.program_id(0)
        prev = jnp.where(i > 0, prev_ref[...], 0.0)
        ext = jnp.concatenate([prev, main_ref[...]], axis=0)
        u = _ffn_conv(ext, w_ref[...], b_ref[...], tr, 8 - (FFN_CONV - 1))
        a = u[:, :D_FF]
        act_ref[...] = (a * _sig(a) * u[:, D_FF:]).astype(BF16)

    return pl.pallas_call(
        body, grid=(Lp // tr,),
        in_specs=[pl.BlockSpec((tr, W2), lambda i: (i, 0)), _halo_prev(tr, W2),
                  pl.BlockSpec((FFN_CONV, W2), lambda i: (0, 0)), pl.BlockSpec((1, W2), lambda i: (0, 0))],
        out_specs=pl.BlockSpec((tr, D_FF), lambda i: (i, 0)),
        out_shape=jax.ShapeDtypeStruct((Lp, D_FF), BF16), name="ffn_act")(up, up, conv_w, conv_b)


def _ffn_act_bwd(up, dact, conv_w, conv_b):
    Lp = up.shape[0]
    tr = _tile(Lp, 96, 16)
    W2 = 2 * D_FF
    te = tr + 8

    def body(main_ref, prev_ref, next_ref, da_ref, dan_ref, w_ref, b_ref, dup_ref, acc_ref):
        i = pl.program_id(0)
        w = w_ref[...]
        prev = jnp.where(i > 0, prev_ref[...], 0.0)
        ext = jnp.concatenate([prev, main_ref[...], next_ref[...]], axis=0)
        u = _ffn_conv(ext, w, b_ref[...], te, 8 - (FFN_CONV - 1))
        a = u[:, :D_FF]
        b = u[:, D_FF:]
        rowe = i * tr + lax.broadcasted_iota(jnp.int32, (te, 1), 0)
        dae = jnp.where(rowe < Lp, jnp.concatenate([da_ref[...], dan_ref[...]], axis=0), 0.0)
        sg = _sig(a)
        du = jnp.concatenate([dae * b * (sg * (1.0 + a * (1.0 - sg))), dae * (a * sg)], axis=1)
        acc = w[FFN_CONV - 1:FFN_CONV, :] * du[0:tr, :]
        for kk in range(FFN_CONV - 1):
            sh = FFN_CONV - 1 - kk
            acc = acc + w[kk:kk + 1, :] * du[sh:sh + tr, :]
        dup_ref[...] = acc.astype(BF16)
        dum = du[0:tr, :]
        lo = 8 - (FFN_CONV - 1)
        rows = [jnp.sum(dum * ext[lo + kk:lo + kk + tr, :], axis=0, keepdims=True) for kk in range(FFN_CONV)]
        rows.append(jnp.sum(dum, axis=0, keepdims=True))
        part = jnp.concatenate(rows + [jnp.zeros((8 - len(rows), W2), F32)], axis=0)

        @pl.when(i == 0)
        def _():
            acc_ref[...] = part

        @pl.when(i > 0)
        def _():
            acc_ref[...] += part

    return pl.pallas_call(
        body, grid=(Lp // tr,),
        in_specs=[pl.BlockSpec((tr, W2), lambda i: (i, 0)), _halo_prev(tr, W2), _halo_next(tr, W2, Lp),
                  pl.BlockSpec((tr, D_FF), lambda i: (i, 0)), _halo_next(tr, D_FF, Lp),
                  pl.BlockSpec((FFN_CONV, W2), lambda i: (0, 0)), pl.BlockSpec((1, W2), lambda i: (0, 0))],
        out_specs=[pl.BlockSpec((tr, W2), lambda i: (i, 0)), pl.BlockSpec((8, W2), lambda i: (0, 0))],
        out_shape=[jax.ShapeDtypeStruct((Lp, W2), BF16), jax.ShapeDtypeStruct((8, W2), F32)],
        name="ffn_act_bwd")(up, up, up, dact, dact, conv_w, conv_b)


def _local_step(hpad, tgt, pad, wt):
    Lp = hpad.shape[0]
    first = pad + N_META
    pos = jnp.arange(Lp, dtype=F32) - float(pad)
    half = RET_D // 2
    inv = 1.0 / (ROPE_BASE ** (jnp.arange(half, dtype=F32) / half))
    ang = pos[:, None] * inv[None, :]
    cos, sin = jnp.cos(ang), jnp.sin(ang)
    tables = _ret_tables()
    gparams = jnp.zeros((8, LANES), F32).at[0, :GDN_H].set(wt["a_log"]).at[1, :GDN_H].set(wt["dt_bias"])

    hn1 = _rms_fwd(hpad, wt["norm1"], "rms1_fwd")
    proj_m = _mm_nn(hn1, wt["w_main"], name="proj_main")
    proj_s = _mm_nn(hn1, wt["w_small"], name="proj_small")
    qkv, gsm = _gdn_pre(proj_m, proj_s, wt["gdn_conv_w"], gparams, pad)
    o_a, s_a, t_a = _gdn_chunk_fwd(qkv, gsm)
    o_b, s_b = _ret_chunk_fwd(proj_m, cos, sin, tables)
    y = _merge_fwd(o_a, o_b, proj_m, wt["gdn_norm"])
    h1 = _mm_nn(y, wt["w_out"], res=hpad, name="out_proj")
    hn2 = _rms_fwd(h1, wt["norm2"], "rms2_fwd")
    up = _mm_nn(hn2, wt["w_up"], name="ffn_up")
    act = _ffn_act(up, wt["ffn_conv_w"], wt["ffn_conv_b"])
    h2 = _mm_nn(act, wt["w_down"], res=h1, name="ffn_down")
    lossvec, dh2, dh2b, d_norm_f = _final(h2, wt["norm_f"], tgt, first)

    d_w_down = _mm_tn(act, dh2b, name="dw_down")
    dact = _mm_nt(dh2b, wt["w_down"], name="d_act")
    dup, ffn_rows = _ffn_act_bwd(up, dact, wt["ffn_conv_w"], wt["ffn_conv_b"])
    d_w_up = _mm_tn(hn2, dup, name="dw_up")
    dhn2 = _mm_nt(dup, wt["w_up"], name="d_hn2")
    dh1, dh1b, d_norm2 = _rms_bwd(h1, wt["norm2"], dhn2, dh2, pad, "rms2_bwd")

    d_w_out = _mm_tn(y, dh1b, name="dw_out")
    dy = _mm_nt(dh1b, wt["w_out"], name="d_y")
    d_c, do_a, do_b, d_gnorm = _merge_bwd(dy, o_a, o_b, proj_m, wt["gdn_norm"])
    drq, drk, drv = _ret_chunk_bwd(proj_m, cos, sin, tables, do_b, s_b)
    dq, dk, dv, dgs = _gdn_chunk_bwd(qkv, gsm, do_a, s_a, t_a)
    d_a, d_s, conv_rows, gp_rows = _gdn_pre_bwd(proj_m, proj_s, wt["gdn_conv_w"], gparams, dq, dk, dv, dgs, pad)

    wm = wt["w_main"]
    segs = [(d_a, 0, 3 * D_MODEL), (drq, 3 * D_MODEL, D_MODEL), (drk, 4 * D_MODEL, D_MODEL),
            (drv, 5 * D_MODEL, D_MODEL), (d_c, 6 * D_MODEL, 4 * D_MODEL)]
    d_w_main = jnp.concatenate([_mm_tn(hn1, d, name="dw_in_%d" % i) for i, (d, _, _) in enumerate(segs)], axis=1)
    d_w_small = _mm_tn(hn1, d_s, name="dw_in_small")
    dhn1 = _mm_nt(d_s, wt["w_small"], name="d_hn1_small")
    for i, (d, off, width) in enumerate(segs):
        dhn1 = _mm_nt(d, wm[:, off:off + width], res=dhn1, name="d_hn1_%d" % i)
    dh0, _, d_norm1 = _rms_bwd(hpad, wt["norm1"], dhn1, dh1, pad, "rms1_bwd")

    grads = {
        "norm1": d_norm1, "w_main": d_w_main, "w_small": d_w_small, "gdn_conv_w": conv_rows[:GDN_CONV],
        "a_log": gp_rows[0, :GDN_H], "dt_bias": gp_rows[1, :GDN_H], "gdn_norm": d_gnorm, "w_out": d_w_out,
        "norm2": d_norm2, "w_up": d_w_up, "ffn_conv_w": ffn_rows[:FFN_CONV], "ffn_conv_b": ffn_rows[FFN_CONV:FFN_CONV + 1],
        "w_down": d_w_down, "norm_f": d_norm_f,
    }
    return lossvec, dh0, grads


def _peer(k):
    ix, iy, ic = lax.axis_index("x"), lax.axis_index("y"), lax.axis_index("c")
    px = 1 - ix if (k >> 2) & 1 else ix
    py = 1 - iy if (k >> 1) & 1 else iy
    pc = 1 - ic if k & 1 else ic
    return (px, py, pc), 4 * px + 2 * py + pc


def _comm_call(body, n, out_shapes, name, args):
    hbm = pl.BlockSpec(memory_space=pl.ANY)
    return pl.pallas_call(
        body, out_shape=out_shapes, in_specs=[hbm] * n, out_specs=[hbm] * n,
        scratch_shapes=[pltpu.SemaphoreType.DMA((n, N_DEV - 1)), pltpu.SemaphoreType.DMA((n, N_DEV - 1)),
                        pltpu.SemaphoreType.DMA((n,))],
        name=name)(*args)


def _all_gather(xs, name):
    n = len(xs)

    def body(*refs):
        x_refs, out_refs = refs[:n], refs[n:2 * n]
        send_sems, recv_sems, local_sems = refs[2 * n:]
        _, me = _peer(0)
        pending = []
        for i in range(n):
            local = pltpu.make_async_copy(x_refs[i], out_refs[i].at[me], local_sems.at[i])
            local.start()
            pending.append(local)
        sends = []
        for i in range(n):
            for k in range(1, N_DEV):
                dev, _ = _peer(k)
                cp = pltpu.make_async_remote_copy(
                    src_ref=x_refs[i], dst_ref=out_refs[i].at[me], send_sem=send_sems.at[i, k - 1],
                    recv_sem=recv_sems.at[i, k - 1], device_id=dev, device_id_type=MESH_T)
                cp.start()
                sends.append(cp)
        for i in range(n):
            for k in range(1, N_DEV):
                dev, idx = _peer(k)
                pltpu.make_async_remote_copy(
                    src_ref=x_refs[i], dst_ref=out_refs[i].at[idx], send_sem=send_sems.at[i, k - 1],
                    recv_sem=recv_sems.at[i, k - 1], device_id=dev, device_id_type=MESH_T).wait_recv()
        for cp in sends:
            cp.wait_send()
        for local in pending:
            local.wait()

    out_shapes = [jax.ShapeDtypeStruct((N_DEV,) + a.shape, a.dtype) for a in xs]
    return _comm_call(body, n, out_shapes, name, xs)


def _all_to_all(gs, name):
    n = len(gs)

    def body(*refs):
        g_refs, out_refs = refs[:n], refs[n:2 * n]
        send_sems, recv_sems, local_sems = refs[2 * n:]
        _, me = _peer(0)
        pending = []
        for i in range(n):
            local = pltpu.make_async_copy(g_refs[i].at[me], out_refs[i].at[0], local_sems.at[i])
            local.start()
            pending.append(local)
        sends = []
        for i in range(n):
            for k in range(1, N_DEV):
                dev, idx = _peer(k)
                cp = pltpu.make_async_remote_copy(
                    src_ref=g_refs[i].at[idx], dst_ref=out_refs[i].at[k], send_sem=send_sems.at[i, k - 1],
                    recv_sem=recv_sems.at[i, k - 1], device_id=dev, device_id_type=MESH_T)
                cp.start()
                sends.append(cp)
        for cp in sends:
            cp.wait_recv()
        for cp in sends:
            cp.wait_send()
        for local in pending:
            local.wait()

    out_shapes = [jax.ShapeDtypeStruct(g.shape, g.dtype) for g in gs]
    return _comm_call(body, n, out_shapes, name, gs)


def _adamw(gslabs, w, m, v, name):
    R, Cw = w.shape
    tr = _tile(R, 64 if Cw > 1024 else 128, 8)
    c1 = 1.0 - ADAM_B1 ** ADAM_STEP
    c2 = 1.0 - ADAM_B2 ** ADAM_STEP

    def body(g_ref, w_ref, m_ref, v_ref, go_ref, d_ref, mo_ref, vo_ref):
        g = g_ref[0]
        for k in range(1, N_DEV):
            g = g + g_ref[k]
        mn = ADAM_B1 * m_ref[...] + (1.0 - ADAM_B1) * g
        vn = ADAM_B2 * v_ref[...] + (1.0 - ADAM_B2) * (g * g)
        m_hat = mn / c1
        v_hat = vn / c2
        go_ref[...] = g
        d_ref[...] = -ADAM_LR * (m_hat / (jnp.sqrt(v_hat) + ADAM_EPS) + ADAM_WD * w_ref[...])
        mo_ref[...] = mn
        vo_ref[...] = vn

    blk = pl.BlockSpec((tr, Cw), lambda i: (i, 0))
    return pl.pallas_call(
        body, grid=(R // tr,),
        in_specs=[pl.BlockSpec((N_DEV, tr, Cw), lambda i: (0, i, 0)), blk, blk, blk],
        out_specs=[blk] * 4, out_shape=[jax.ShapeDtypeStruct((R, Cw), F32)] * 4, name=name)(gslabs, w, m, v)


def _pack(arrs, row_mult, dtype=F32):
    parts = []
    total = 0
    for a in arrs:
        f = a.reshape(-1).astype(dtype)
        n = -(-f.shape[0] // 1024) * 1024
        parts.append(jnp.pad(f, (0, n - f.shape[0])))
        total += n
    rows = total // LANES
    rows_p = -(-rows // row_mult) * row_mult
    flat = jnp.concatenate(parts)
    flat = jnp.pad(flat, (0, rows_p * LANES - total))
    return flat.reshape(rows_p, LANES)


def _unpack(packed, shapes):
    lead = packed.shape[:-2]
    flat = packed.reshape(lead + (-1,))
    out = []
    off = 0
    for s in shapes:
        n = int(np.prod(s))
        out.append(flat[..., off:off + n].reshape(lead + tuple(s)))
        off += -(-n // 1024) * 1024
    return out


def _gather_cols(stacked):
    d, r, c = stacked.shape
    return stacked.transpose(1, 0, 2).reshape(r, d * c)


def _scatter_cols(full):
    r, n = full.shape
    return full.reshape(r, N_DEV, n // N_DEV).transpose(1, 0, 2)


def kernel(x, meta, norm1, w_in, gdn_conv_w, gdn_a_log, gdn_dt_bias, gdn_norm, w_out, norm2, w_ffn_up, ffn_conv_w, ffn_conv_b, w_ffn_down, norm_f, loss_target, m_meta, m_norm1, m_w_in, m_gdn_conv_w, m_gdn_a_log, m_gdn_dt_bias, m_gdn_norm, m_w_out, m_norm2, m_w_ffn_up, m_ffn_conv_w, m_ffn_conv_b, m_w_ffn_down, m_norm_f, v_meta, v_norm1, v_w_in, v_gdn_conv_w, v_gdn_a_log, v_gdn_dt_bias, v_gdn_norm, v_w_out, v_norm2, v_w_ffn_up, v_ffn_conv_w, v_ffn_conv_b, v_w_ffn_down, v_norm_f):
    S = x.shape[1]
    L = N_META + S
    pad = (-L) % CHUNK
    Lp = L + pad

    big = [w_in[0], w_out[0], w_ffn_up[0], w_ffn_down[0]]
    small = [meta, gdn_conv_w, ffn_conv_w]
    w_in_s, w_out_s, w_up_s, w_down_s, small_all = _all_gather(
        [a.astype(BF16) for a in big] + [_pack(small, 8)], "gather_weights")
    meta_s, gconv_s, fconv_s = _unpack(small_all, [a.shape for a in small])
    w_in_f = _gather_cols(w_in_s)
    w_main = jnp.concatenate([w_in_f[:, _O_GQ:_O_GZ], w_in_f[:, _O_RQ:_O_RG], w_in_f[:, _O_GZ:_O_GA],
                              w_in_f[:, _O_RG:_O_END]], axis=1)
    w_small = jnp.pad(w_in_f[:, _O_GA:_O_RQ], ((0, 0), (0, LANES - 2 * GDN_H)))
    wt = {
        "norm1": norm1, "w_main": w_main, "w_small": w_small,
        "gdn_conv_w": _gather_cols(gconv_s[:, 0]), "a_log": gdn_a_log[0], "dt_bias": gdn_dt_bias[0],
        "gdn_norm": gdn_norm, "w_out": w_out_s.reshape(D_MODEL, D_MODEL), "norm2": norm2,
        "w_up": _gather_cols(w_up_s), "ffn_conv_w": _gather_cols(fconv_s[:, 0]), "ffn_conv_b": ffn_conv_b,
        "w_down": w_down_s.reshape(D_FF, D_MODEL), "norm_f": norm_f.reshape(1, D_MODEL),
    }
    meta_f = _gather_cols(meta_s)

    hpad = jnp.concatenate([jnp.zeros((pad, D_MODEL), F32), meta_f, x[0]], axis=0)
    tgt = jnp.concatenate([jnp.zeros((pad + N_META, D_MODEL), F32), loss_target[0]], axis=0)
    lossvec, dh0, gr = _local_step(hpad, tgt, pad, wt)

    loss = lax.psum(jnp.sum(lossvec), ("x", "y", "c"))
    grad_x = dh0[pad + N_META:][None]

    gm = gr["w_main"]
    d_w_in = jnp.concatenate([gm[:, 0:3072], gm[:, 6144:7168], gr["w_small"][:, :2 * GDN_H], gm[:, 3072:6144],
                              gm[:, 7168:]], axis=1)
    g_big = [_scatter_cols(d_w_in), gr["w_out"].reshape(N_DEV, D_MODEL // N_DEV, D_MODEL),
             _scatter_cols(gr["w_up"]), gr["w_down"].reshape(N_DEV, D_FF // N_DEV, D_MODEL)]
    g_sm = [_scatter_cols(dh0[pad:pad + N_META]), _scatter_cols(gr["gdn_conv_w"]), _scatter_cols(gr["ffn_conv_w"])]
    g_small = jnp.stack([_pack([g[d] for g in g_sm], 8) for d in range(N_DEV)])
    slabs = _all_to_all(g_big + [g_small], "exchange_gradients")
    big_m = [m_w_in[0], m_w_out[0], m_w_ffn_up[0], m_w_ffn_down[0]]
    big_v = [v_w_in[0], v_w_out[0], v_w_ffn_up[0], v_w_ffn_down[0]]
    big_out = [_adamw(slabs[i], big[i], big_m[i], big_v[i], "adamw_big_%d" % i) for i in range(len(big))]
    small_out = _adamw(slabs[-1], _pack(small, 8), _pack([m_meta, m_gdn_conv_w, m_ffn_conv_w], 8),
                       _pack([v_meta, v_gdn_conv_w, v_ffn_conv_w], 8), "adamw_small_sharded")
    small_un = [_unpack(o, [a.shape for a in small]) for o in small_out]
    sh_g, sh_d, sh_nm, sh_nv = [
        [small_un[j][0], big_out[0][j][None], small_un[j][1], big_out[1][j][None], big_out[2][j][None],
         small_un[j][2], big_out[3][j][None]] for j in range(4)]

    rep_w = [norm1, gdn_a_log, gdn_dt_bias, gdn_norm, norm2, ffn_conv_b, norm_f]
    rep_m = [m_norm1, m_gdn_a_log, m_gdn_dt_bias, m_gdn_norm, m_norm2, m_ffn_conv_b, m_norm_f]
    rep_v = [v_norm1, v_gdn_a_log, v_gdn_dt_bias, v_gdn_norm, v_norm2, v_ffn_conv_b, v_norm_f]
    rep_g = [gr["norm1"], gr["a_log"], gr["dt_bias"], gr["gdn_norm"], gr["norm2"], gr["ffn_conv_b"], gr["norm_f"]]
    rep_slabs, = _all_gather([_pack(rep_g, 8)], "gather_small_gradients")
    rep_out = _adamw(rep_slabs, _pack(rep_w, 8), _pack(rep_m, 8), _pack(rep_v, 8), "adamw_replicated")
    rep_shapes = [a.shape for a in rep_w]
    rp_g, rp_d, rp_nm, rp_nv = [_unpack(o, rep_shapes) for o in rep_out]

    def order(sh, rp):
        return [sh[0], rp[0], sh[1], sh[2], rp[1], rp[2], rp[3], sh[3], rp[4], sh[4], sh[5], rp[5], sh[6], rp[6]]

    return (loss, grad_x, *order(sh_g, rp_g), *order(sh_d, rp_d), *order(sh_nm, rp_nm), *order(sh_nv, rp_nv))
```

```python
import functools
import math

import numpy as np
import jax
import jax.numpy as jnp
from jax import lax
from jax.experimental import pallas as pl
from jax.experimental.pallas import tpu as pltpu

F32 = jnp.float32
BF16 = jnp.bfloat16
HI = lax.Precision.HIGHEST

D_MODEL = 1024
N_META = 16
CHUNK = 64
GDN_H = 8
GDN_D = 128
RET_H = 4
RET_D = 256
D_FF = 2816
GDN_CONV = 4
FFN_CONV = 3
ROPE_BASE = 10000.0
EPS = 1e-6
N_DEV = 8
LANES = 128
MAIN_W = 10 * 1024
_O_GQ, _O_GZ, _O_GA, _O_RQ, _O_RG, _O_GATE, _O_END = 0, 3072, 4096, 4112, 7184, 8208, 10256

ADAM_LR = 0.001
ADAM_B1 = 0.9
ADAM_B2 = 0.999
ADAM_EPS = 1e-08
ADAM_WD = 0.01
ADAM_STEP = 10

MESH_T = pl.DeviceIdType.MESH


def _tile(n, target, mult):
    best = None
    for d in range(mult, min(n, target) + 1, mult):
        if n % d == 0:
            best = d
    assert best is not None, (n, target, mult)
    return best


def _sig(x):
    return 1.0 / (1.0 + jnp.exp(-x))


def _d(a, b):
    return jnp.dot(a.astype(BF16), b.astype(BF16), preferred_element_type=F32)


def _dnt(a, b):
    return lax.dot_general(a.astype(BF16), b.astype(BF16), (((1,), (1,)), ((), ())), preferred_element_type=F32)


def _dtn(a, b):
    return lax.dot_general(a.astype(BF16), b.astype(BF16), (((0,), (0,)), ((), ())), preferred_element_type=F32)


def _dx(a, b):
    return jnp.dot(a, b, preferred_element_type=F32, precision=HI)


def _dxnt(a, b):
    return lax.dot_general(a, b, (((1,), (1,)), ((), ())), preferred_element_type=F32, precision=HI)


def _dxtn(a, b):
    return lax.dot_general(a, b, (((0,), (0,)), ((), ())), preferred_element_type=F32, precision=HI)


def _split(a):
    hi = a.astype(BF16)
    return hi, (a - hi.astype(F32)).astype(BF16)


def _d3g(a, b, dims):
    ah, al = _split(a)
    bh, bl = _split(b)
    f = functools.partial(lax.dot_general, dimension_numbers=dims, preferred_element_type=F32)
    return f(ah, bh) + (f(ah, bl) + f(al, bh))


_NN = (((1,), (0,)), ((), ()))
_NT = (((1,), (1,)), ((), ()))
_TN = (((0,), (0,)), ((), ()))


def _rowsum(x):
    return jnp.sum(x, axis=1, keepdims=True)


def _allsum(x):
    return jnp.sum(jnp.sum(x, axis=1, keepdims=True), axis=0, keepdims=True)


def _mm_nn(a, b, res=None, out_dtype=F32, name="mm_nn"):
    M, K = a.shape
    N = b.shape[1]
    tm = _tile(M, 704, 16)
    tn = _tile(N, 2816, 128)

    def body(*refs):
        if res is None:
            a_ref, b_ref, o_ref = refs
        else:
            a_ref, b_ref, r_ref, o_ref = refs
        acc = jnp.dot(a_ref[...], b_ref[...], preferred_element_type=F32)
        if res is not None:
            acc = acc + r_ref[...]
        o_ref[...] = acc.astype(out_dtype)

    in_specs = [pl.BlockSpec((tm, K), lambda j, i: (i, 0)), pl.BlockSpec((K, tn), lambda j, i: (0, j))]
    args = [a, b]
    if res is not None:
        in_specs.append(pl.BlockSpec((tm, tn), lambda j, i: (i, j)))
        args.append(res)
    return pl.pallas_call(
        body, grid=(N // tn, M // tm), in_specs=in_specs,
        out_specs=pl.BlockSpec((tm, tn), lambda j, i: (i, j)),
        out_shape=jax.ShapeDtypeStruct((M, N), out_dtype), name=name)(*args)


def _mm_nt(a, b, res=None, name="mm_nt"):
    M, Nc = a.shape
    K = b.shape[0]
    tm = _tile(M, 704, 16)
    tc = _tile(Nc, 2048, 128)

    def body(*refs):
        if res is None:
            a_ref, b_ref, o_ref = refs
        else:
            a_ref, b_ref, r_ref, o_ref = refs
        c = pl.program_id(1)
        p = lax.dot_general(a_ref[...], b_ref[...], (((1,), (1,)), ((), ())), preferred_element_type=F32)

        @pl.when(c == 0)
        def _():
            if res is None:
                o_ref[...] = p
            else:
                o_ref[...] = p + r_ref[...]

        @pl.when(c > 0)
        def _():
            o_ref[...] += p

    in_specs = [pl.BlockSpec((tm, tc), lambda i, c: (i, c)), pl.BlockSpec((K, tc), lambda i, c: (0, c))]
    args = [a, b]
    if res is not None:
        in_specs.append(pl.BlockSpec((tm, K), lambda i, c: (i, 0)))
        args.append(res)
    return pl.pallas_call(
        body, grid=(M // tm, Nc // tc), in_specs=in_specs,
        out_specs=pl.BlockSpec((tm, K), lambda i, c: (i, 0)),
        out_shape=jax.ShapeDtypeStruct((M, K), F32), name=name)(*args)


def _mm_tn(a, b, name="mm_tn"):
    M, K = a.shape
    N = b.shape[1]
    tm = _tile(M, 704, 16)
    tk = _tile(K, 1408, 128)
    tn = _tile(N, 2048, 128)

    def body(a_ref, b_ref, o_ref):
        m = pl.program_id(2)
        p = lax.dot_general(a_ref[...], b_ref[...], (((0,), (0,)), ((), ())), preferred_element_type=F32)

        @pl.when(m == 0)
        def _():
            o_ref[...] = p

        @pl.when(m > 0)
        def _():
            o_ref[...] += p

    return pl.pallas_call(
        body, grid=(K // tk, N // tn, M // tm),
        in_specs=[pl.BlockSpec((tm, tk), lambda kk, j, m: (m, kk)), pl.BlockSpec((tm, tn), lambda kk, j, m: (m, j))],
        out_specs=pl.BlockSpec((tk, tn), lambda kk, j, m: (kk, j)),
        out_shape=jax.ShapeDtypeStruct((K, N), F32), name=name)(a, b)


def _rms_fwd(x, g, name):
    Lp = x.shape[0]
    tr = _tile(Lp, 256, 16)

    def body(x_ref, g_ref, o_ref):
        xv = x_ref[...]
        r = lax.rsqrt(jnp.mean(xv * xv, axis=-1, keepdims=True) + EPS)
        o_ref[...] = (xv * r * g_ref[...]).astype(BF16)

    return pl.pallas_call(
        body, grid=(Lp // tr,),
        in_specs=[pl.BlockSpec((tr, D_MODEL), lambda i: (i, 0)), pl.BlockSpec((1, D_MODEL), lambda i: (0, 0))],
        out_specs=pl.BlockSpec((tr, D_MODEL), lambda i: (i, 0)),
        out_shape=jax.ShapeDtypeStruct((Lp, D_MODEL), BF16), name=name)(x, g)


def _rms_bwd(x, g, dy, dres, pad, name):
    Lp = x.shape[0]
    tr = _tile(Lp, 256, 16)

    def body(x_ref, g_ref, dy_ref, dr_ref, dx_ref, dxb_ref, dg_ref):
        i = pl.program_id(0)
        xv = x_ref[...]
        r = lax.rsqrt(jnp.mean(xv * xv, axis=-1, keepdims=True) + EPS)
        xh = xv * r
        dyv = dy_ref[...]
        dxh = dyv * g_ref[...]
        dx = r * (dxh - xh * jnp.mean(dxh * xh, axis=-1, keepdims=True)) + dr_ref[...]
        row = i * tr + lax.broadcasted_iota(jnp.int32, (tr, 1), 0)
        dx = jnp.where(row >= pad, dx, 0.0)
        dx_ref[...] = dx
        dxb_ref[...] = dx.astype(BF16)
        part = jnp.sum(dyv * xh, axis=0, keepdims=True)

        @pl.when(i == 0)
        def _():
            dg_ref[...] = part

        @pl.when(i > 0)
        def _():
            dg_ref[...] += part

    blk = pl.BlockSpec((tr, D_MODEL), lambda i: (i, 0))
    vec = pl.BlockSpec((1, D_MODEL), lambda i: (0, 0))
    return pl.pallas_call(
        body, grid=(Lp // tr,), in_specs=[blk, vec, blk, blk], out_specs=[blk, blk, vec],
        out_shape=[jax.ShapeDtypeStruct((Lp, D_MODEL), F32), jax.ShapeDtypeStruct((Lp, D_MODEL), BF16),
                   jax.ShapeDtypeStruct((1, D_MODEL), F32)], name=name)(x, g, dy, dres)


def _final(h2, g, tgt, first_row):
    Lp = h2.shape[0]
    tr = _tile(Lp, 256, 16)

    def body(x_ref, g_ref, t_ref, loss_ref, dx_ref, dxb_ref, dg_ref):
        i = pl.program_id(0)
        xv = x_ref[...]
        gv = g_ref[...]
        r = lax.rsqrt(jnp.mean(xv * xv, axis=-1, keepdims=True) + EPS)
        xh = xv * r
        row = i * tr + lax.broadcasted_iota(jnp.int32, (tr, 1), 0)
        err = jnp.where(row >= first_row, xh * gv - t_ref[...], 0.0)
        lpart = jnp.sum(err * err, axis=0, keepdims=True) * (0.5 / D_MODEL)
        dyv = err * (1.0 / D_MODEL)
        dxh = dyv * gv
        dx = r * (dxh - xh * jnp.mean(dxh * xh, axis=-1, keepdims=True))
        dx_ref[...] = dx
        dxb_ref[...] = dx.astype(BF16)
        part = jnp.sum(dyv * xh, axis=0, keepdims=True)

        @pl.when(i == 0)
        def _():
            dg_ref[...] = part
            loss_ref[...] = lpart

        @pl.when(i > 0)
        def _():
            dg_ref[...] += part
            loss_ref[...] += lpart

    blk = pl.BlockSpec((tr, D_MODEL), lambda i: (i, 0))
    vec = pl.BlockSpec((1, D_MODEL), lambda i: (0, 0))
    return pl.pallas_call(
        body, grid=(Lp // tr,), in_specs=[blk, vec, blk], out_specs=[vec, blk, blk, vec],
        out_shape=[jax.ShapeDtypeStruct((1, D_MODEL), F32), jax.ShapeDtypeStruct((Lp, D_MODEL), F32),
                   jax.ShapeDtypeStruct((Lp, D_MODEL), BF16), jax.ShapeDtypeStruct((1, D_MODEL), F32)],
        name="final_norm_loss")(h2, g, tgt)


def _halo_prev(tr, width, col=0):
    return pl.BlockSpec((8, width), lambda i: (jnp.maximum(i * (tr // 8) - 1, 0), col))


def _halo_next(tr, width, nrows, col=0):
    last = nrows // 8 - 1
    return pl.BlockSpec((8, width), lambda i: (jnp.minimum((i + 1) * (tr // 8), last), col))


def _gdn_conv(ext, w, tr, lo):
    acc = w[0:1, :] * ext[lo:lo + tr, :]
    for kk in range(1, GDN_CONV):
        acc = acc + w[kk:kk + 1, :] * ext[lo + kk:lo + kk + tr, :]
    return acc


def _gdn_pre(proj_m, proj_s, conv_w, gparams, pad):
    Lp = proj_m.shape[0]
    tr = _tile(Lp, 192, 64)
    W3 = 3 * D_MODEL

    def body(main_ref, prev_ref, s_ref, w_ref, gp_ref, qkv_ref, gsm_ref):
        i = pl.program_id(0)
        prev = jnp.where(i > 0, prev_ref[...], 0.0)
        ext = jnp.concatenate([prev, main_ref[...]], axis=0)
        c = _gdn_conv(ext, w_ref[...], tr, 8 - (GDN_CONV - 1))
        s = c * _sig(c)
        scale = GDN_D ** -0.5
        for j in range(2 * GDN_H):
            seg = s[:, j * GDN_D:(j + 1) * GDN_D]
            r = lax.rsqrt(_rowsum(seg * seg) + EPS)
            if j < GDN_H:
                r = r * scale
            qkv_ref[:, j * GDN_D:(j + 1) * GDN_D] = seg * r
        qkv_ref[:, 2 * D_MODEL:] = s[:, 2 * D_MODEL:]
        sm = s_ref[...]
        gp = gp_ref[...]
        lane = lax.broadcasted_iota(jnp.int32, sm.shape, 1)
        z = sm + gp[1:2, :]
        softplus = jnp.maximum(z, 0.0) + jnp.log(1.0 + jnp.exp(-jnp.abs(z)))
        lg = -jnp.exp(gp[0:1, :]) * softplus
        row = i * tr + lax.broadcasted_iota(jnp.int32, (tr, 1), 0)
        out = jnp.where(lane < GDN_H, lg, jnp.where(lane < 2 * GDN_H, _sig(sm), 0.0))
        gsm_ref[...] = jnp.where(row >= pad, out, 0.0)

    return pl.pallas_call(
        body, grid=(Lp // tr,),
        in_specs=[pl.BlockSpec((tr, W3), lambda i: (i, 0)), _halo_prev(tr, W3),
                  pl.BlockSpec((tr, LANES), lambda i: (i, 0)),
                  pl.BlockSpec((GDN_CONV, W3), lambda i: (0, 0)), pl.BlockSpec((8, LANES), lambda i: (0, 0))],
        out_specs=[pl.BlockSpec((tr, W3), lambda i: (i, 0)), pl.BlockSpec((tr, LANES), lambda i: (i, 0))],
        out_shape=[jax.ShapeDtypeStruct((Lp, W3), F32), jax.ShapeDtypeStruct((Lp, LANES), F32)],
        name="gdn_pre")(proj_m, proj_m, proj_s, conv_w, gparams)


def _gdn_pre_bwd(proj_m, proj_s, conv_w, gparams, dq, dk, dv, dgs, pad):
    Lp = proj_m.shape[0]
    tr = _tile(Lp, 192, 64)
    W3 = 3 * D_MODEL
    te = tr + 8

    def body(main_ref, prev_ref, next_ref, s_ref, w_ref, gp_ref,
             dq_ref, dqn_ref, dk_ref, dkn_ref, dv_ref, dvn_ref, dgs_ref,
             da_ref, ds_ref, dw_ref, dgp_ref):
        i = pl.program_id(0)
        w = w_ref[...]
        prev = jnp.where(i > 0, prev_ref[...], 0.0)
        ext = jnp.concatenate([prev, main_ref[...], next_ref[...]], axis=0)
        c = _gdn_conv(ext, w, te, 8 - (GDN_CONV - 1))
        sg = _sig(c)
        s = c * sg
        rowe = i * tr + lax.broadcasted_iota(jnp.int32, (te, 1), 0)
        live = (rowe >= pad) & (rowe < Lp)
        dqe = jnp.concatenate([dq_ref[...], dqn_ref[...]], axis=0)
        dke = jnp.concatenate([dk_ref[...], dkn_ref[...]], axis=0)
        dve = jnp.concatenate([dv_ref[...], dvn_ref[...]], axis=0)
        scale = GDN_D ** -0.5
        parts = []
        for j in range(2 * GDN_H):
            seg = s[:, j * GDN_D:(j + 1) * GDN_D]
            r = lax.rsqrt(_rowsum(seg * seg) + EPS)
            xh = seg * r
            if j < GDN_H:
                dxh = dqe[:, j * GDN_D:(j + 1) * GDN_D] * scale
            else:
                dxh = dke[:, (j - GDN_H) * GDN_D:(j - GDN_H + 1) * GDN_D]
            parts.append(r * (dxh - xh * _rowsum(dxh * xh)))
        parts.append(dve)
        dsv = jnp.concatenate(parts, axis=1)
        dc = jnp.where(live, dsv * (sg * (1.0 + c * (1.0 - sg))), 0.0)
        acc = w[GDN_CONV - 1:GDN_CONV, :] * dc[0:tr, :]
        for kk in range(GDN_CONV - 1):
            sh = GDN_CONV - 1 - kk
            acc = acc + w[kk:kk + 1, :] * dc[sh:sh + tr, :]
        da_ref[...] = acc.astype(BF16)
        dcm = dc[0:tr, :]
        rows = [jnp.sum(dcm * ext[8 - (GDN_CONV - 1) + kk:8 - (GDN_CONV - 1) + kk + tr, :], axis=0, keepdims=True)
                for kk in range(GDN_CONV)]
        dwp = jnp.concatenate(rows + [jnp.zeros((8 - GDN_CONV, W3), F32)], axis=0)

        sm = s_ref[...]
        gp = gp_ref[...]
        lane = lax.broadcasted_iota(jnp.int32, sm.shape, 1)
        rowm = i * tr + lax.broadcasted_iota(jnp.int32, (tr, 1), 0)
        dgv = jnp.where(rowm >= pad, dgs_ref[...], 0.0)
        dlg = jnp.where(lane < GDN_H, dgv, 0.0)
        dbt = jnp.where((lane >= GDN_H) & (lane < 2 * GDN_H), dgv, 0.0)
        z = sm + gp[1:2, :]
        softplus = jnp.maximum(z, 0.0) + jnp.log(1.0 + jnp.exp(-jnp.abs(z)))
        ea = jnp.exp(gp[0:1, :])
        dz = dlg * (-ea) * _sig(z)
        dal = dlg * (-ea) * softplus
        bt = _sig(sm)
        dgb = dbt * bt * (1.0 - bt)
        ds_ref[...] = (dz + dgb).astype(BF16)
        gpp = jnp.concatenate([jnp.sum(dal, axis=0, keepdims=True), jnp.sum(dz, axis=0, keepdims=True),
                               jnp.zeros((6, LANES), F32)], axis=0)

        @pl.when(i == 0)
        def _():
            dw_ref[...] = dwp
            dgp_ref[...] = gpp

        @pl.when(i > 0)
        def _():
            dw_ref[...] += dwp
            dgp_ref[...] += gpp

    m3 = pl.BlockSpec((tr, W3), lambda i: (i, 0))
    m1 = pl.BlockSpec((tr, D_MODEL), lambda i: (i, 0))
    n1 = _halo_next(tr, D_MODEL, Lp)
    return pl.pallas_call(
        body, grid=(Lp // tr,),
        in_specs=[m3, _halo_prev(tr, W3), _halo_next(tr, W3, Lp), pl.BlockSpec((tr, LANES), lambda i: (i, 0)),
                  pl.BlockSpec((GDN_CONV, W3), lambda i: (0, 0)), pl.BlockSpec((8, LANES), lambda i: (0, 0)),
                  m1, n1, m1, n1, m1, n1, pl.BlockSpec((tr, LANES), lambda i: (i, 0))],
        out_specs=[m3, pl.BlockSpec((tr, LANES), lambda i: (i, 0)),
                   pl.BlockSpec((8, W3), lambda i: (0, 0)), pl.BlockSpec((8, LANES), lambda i: (0, 0))],
        out_shape=[jax.ShapeDtypeStruct((Lp, W3), BF16), jax.ShapeDtypeStruct((Lp, LANES), BF16),
                   jax.ShapeDtypeStruct((8, W3), F32), jax.ShapeDtypeStruct((8, LANES), F32)],
        name="gdn_pre_bwd")(proj_m, proj_m, proj_m, proj_s, conv_w, gparams, dq, dq, dk, dk, dv, dv, dgs)


def _gdn_gates(gs):
    ri = lax.broadcasted_iota(jnp.int32, (CHUNK, CHUNK), 0)
    ci = lax.broadcasted_iota(jnp.int32, (CHUNK, CHUNK), 1)
    tril = ri >= ci
    strict = ri > ci
    gall = _dx(tril.astype(F32), gs)
    lane8 = lax.broadcasted_iota(jnp.int32, (8, LANES), 1)
    sub8 = lax.broadcasted_iota(jnp.int32, (8, LANES), 0)
    grow = _dxnt((lane8 == sub8).astype(F32), gall)
    return gall, grow, tril, strict


def _gdn_decay(gall, grow, tril, h):
    g = gall[:, h:h + 1]
    return g, jnp.where(tril, jnp.exp(jnp.where(tril, g - grow[h:h + 1, :], 0.0)), 0.0)


def _gdn_chunk_specs(N, rev):
    cn = (lambda n: N - 1 - n) if rev else (lambda n: n)
    col = lambda j: pl.BlockSpec((CHUNK, D_MODEL), lambda n: (cn(n), j))
    gate = pl.BlockSpec((CHUNK, LANES), lambda n: (cn(n), 0))
    st = lambda a, b: pl.BlockSpec((GDN_H, None, a, b), lambda n: (0, cn(n), 0, 0))
    return col, gate, st


def _gdn_chunk_fwd(qkv, gsm):
    Lp = qkv.shape[0]
    N = Lp // CHUNK

    def body(q_ref, k_ref, v_ref, gs_ref, o_ref, sin_ref, t_ref, S):
        n = pl.program_id(0)

        @pl.when(n == 0)
        def _():
            S[...] = jnp.zeros_like(S)

        gs = gs_ref[...]
        gall, grow, tril, strict = _gdn_gates(gs)
        ri = lax.broadcasted_iota(jnp.int32, (CHUNK, CHUNK), 0)
        ci = lax.broadcasted_iota(jnp.int32, (CHUNK, CHUNK), 1)
        eye = (ri == ci).astype(F32)
        heads = range(GDN_H)
        sls = [slice(h * GDN_D, (h + 1) * GDN_D) for h in heads]
        q = [q_ref[:, sl] for sl in sls]
        k = [k_ref[:, sl] for sl in sls]
        v = [v_ref[:, sl] for sl in sls]
        s0 = [S[h] for h in heads]
        beta = [gs[:, GDN_H + h:GDN_H + h + 1] for h in heads]
        gg = [_gdn_decay(gall, grow, tril, h) for h in heads]
        g = [x[0] for x in gg]
        gam = [x[1] for x in gg]
        eg = [jnp.exp(g[h]) for h in heads]
        gl = [g[h][CHUNK - 1:CHUNK, :] for h in heads]
        kb = [k[h] * beta[h] for h in heads]
        pw = [-jnp.where(strict, _dnt(kb[h], k[h]) * gam[h], 0.0) for h in heads]
        p = [_dnt(q[h], k[h]) * gam[h] for h in heads]
        qs = [_d(q[h] * eg[h], s0[h]) for h in heads]
        t = [eye + pw[h] for h in heads]
        for _ in range(5):
            pw = [_d3g(pw[h], pw[h], _NN) for h in heads]
            t = [t[h] + _d3g(t[h], pw[h], _NN) for h in heads]
        u = [_d(t[h], v[h] * beta[h]) for h in heads]
        w = [_d(t[h], kb[h] * eg[h]) for h in heads]
        vnew = [u[h] - _d(w[h], s0[h]) for h in heads]
        for h in heads:
            o_ref[:, sls[h]] = qs[h] + _d(p[h], vnew[h])
            sin_ref[h] = s0[h]
            t_ref[h] = t[h]
            S[h] = s0[h] * jnp.exp(gl[h]) + _dtn(k[h] * jnp.exp(gl[h] - g[h]), vnew[h])

    col, gate, st = _gdn_chunk_specs(N, False)
    return pl.pallas_call(
        body, grid=(N,),
        in_specs=[col(0), col(1), col(2), gate],
        out_specs=[col(0), st(GDN_D, GDN_D), st(CHUNK, CHUNK)],
        out_shape=[jax.ShapeDtypeStruct((Lp, D_MODEL), F32), jax.ShapeDtypeStruct((GDN_H, N, GDN_D, GDN_D), F32),
                   jax.ShapeDtypeStruct((GDN_H, N, CHUNK, CHUNK), F32)],
        scratch_shapes=[pltpu.VMEM((GDN_H, GDN_D, GDN_D), F32)],
        name="gdn_chunk_fwd")(qkv, qkv, qkv, gsm)


def _gdn_chunk_bwd(qkv, gsm, do, s_in, t_in):
    Lp = qkv.shape[0]
    N = Lp // CHUNK

    def body(q_ref, k_ref, v_ref, gs_ref, do_ref, sin_ref, t_ref, dq_ref, dk_ref, dv_ref, dgs_ref, dS):
        n = pl.program_id(0)

        @pl.when(n == 0)
        def _():
            dS[...] = jnp.zeros_like(dS)

        gs = gs_ref[...]
        gall, grow, tril, strict = _gdn_gates(gs)
        lane = lax.broadcasted_iota(jnp.int32, (CHUNK, LANES), 1)
        rcol = lax.broadcasted_iota(jnp.int32, (CHUNK, 1), 0)
        ones = jnp.ones((CHUNK, LANES), F32)
        dg_all = jnp.zeros((CHUNK, LANES), F32)
        dbeta_all = jnp.zeros((CHUNK, LANES), F32)
        heads = range(GDN_H)
        sls = [slice(h * GDN_D, (h + 1) * GDN_D) for h in heads]
        H = lambda f: [f(h) for h in heads]
        q = H(lambda h: q_ref[:, sls[h]])
        k = H(lambda h: k_ref[:, sls[h]])
        v = H(lambda h: v_ref[:, sls[h]])
        dov = H(lambda h: do_ref[:, sls[h]])
        s0 = H(lambda h: sin_ref[h])
        t = H(lambda h: t_ref[h])
        dsv = H(lambda h: dS[h])
        beta = H(lambda h: gs[:, GDN_H + h:GDN_H + h + 1])
        gg = H(lambda h: _gdn_decay(gall, grow, tril, h))
        g = [x[0] for x in gg]
        gam = [x[1] for x in gg]
        eg = H(lambda h: jnp.exp(g[h]))
        egl = H(lambda h: jnp.exp(g[h][CHUNK - 1:CHUNK, :]))
        e = H(lambda h: jnp.exp(g[h][CHUNK - 1:CHUNK, :] - g[h]))
        kb = H(lambda h: k[h] * beta[h])
        kbg = H(lambda h: kb[h] * eg[h])
        vb = H(lambda h: v[h] * beta[h])
        qg = H(lambda h: q[h] * eg[h])
        kd = H(lambda h: k[h] * e[h])
        m = H(lambda h: jnp.where(strict, _dnt(kb[h], k[h]) * gam[h], 0.0))
        u = H(lambda h: _d(t[h], vb[h]))
        w = H(lambda h: _d(t[h], kbg[h]))
        p = H(lambda h: _dnt(q[h], k[h]) * gam[h])
        dqg = H(lambda h: _dnt(dov[h], s0[h]))
        kdds = H(lambda h: _d(kd[h], dsv[h]))
        qgdo = H(lambda h: _dtn(qg[h], dov[h]))
        vnew = H(lambda h: u[h] - _d(w[h], s0[h]))
        dvnew = H(lambda h: _dtn(p[h], dov[h]) + kdds[h])
        dp = H(lambda h: jnp.where(tril, _dnt(dov[h], vnew[h]), 0.0))
        dkd = H(lambda h: _dnt(vnew[h], dsv[h]))
        dw = H(lambda h: -_dnt(dvnew[h], s0[h]))
        for h in heads:
            dS[h] = qgdo[h] + egl[h] * dsv[h] - _dtn(w[h], dvnew[h])
        dvb = H(lambda h: _dtn(t[h], dvnew[h]))
        dkbg = H(lambda h: _dtn(t[h], dw[h]))
        dt = H(lambda h: _dnt(dvnew[h], vb[h]) + _dnt(dw[h], kbg[h]))
        x1 = H(lambda h: _d3g(t[h], dt[h], _TN))
        dm = H(lambda h: jnp.where(strict, -_d3g(x1[h], t[h], _NT), 0.0))
        dkk = H(lambda h: dm[h] * gam[h])
        dqk = H(lambda h: dp[h] * gam[h])
        dkb = H(lambda h: _d(dkk[h], k[h]) + eg[h] * dkbg[h])
        em = H(lambda h: dm[h] * m[h] + dp[h] * p[h])
        colsum = H(lambda h: _d3g(em[h], ones, _TN)[:, 0:1])
        for h in heads:
            dk_ref[:, sls[h]] = _dtn(dkk[h], kb[h]) + _dtn(dqk[h], q[h]) + dkd[h] * e[h] + beta[h] * dkb[h]
            dq_ref[:, sls[h]] = _d(dqk[h], k[h]) + dqg[h] * eg[h]
            dv_ref[:, sls[h]] = beta[h] * dvb[h]
        for h in heads:
            dbeta = _rowsum(k[h] * dkb[h]) + _rowsum(v[h] * dvb[h])
            z = _rowsum(kd[h] * dkd[h])
            dg = _rowsum(em[h]) - colsum[h] + _rowsum(qg[h] * dqg[h]) + _rowsum(kbg[h] * dkbg[h]) - z
            extra = _allsum(z) + egl[h] * _allsum(s0[h] * dsv[h])
            dg = dg + jnp.where(rcol == CHUNK - 1, extra, 0.0)
            dg_all = dg_all + jnp.where(lane == h, dg, 0.0)
            dbeta_all = dbeta_all + jnp.where(lane == GDN_H + h, dbeta, 0.0)
        ri = lax.broadcasted_iota(jnp.int32, (CHUNK, CHUNK), 0)
        ci = lax.broadcasted_iota(jnp.int32, (CHUNK, CHUNK), 1)
        dgs_ref[...] = _dx((ci >= ri).astype(F32), dg_all) + dbeta_all

    col, gate, st = _gdn_chunk_specs(N, True)
    return pl.pallas_call(
        body, grid=(N,),
        in_specs=[col(0), col(1), col(2), gate, col(0), st(GDN_D, GDN_D), st(CHUNK, CHUNK)],
        out_specs=[col(0), col(0), col(0), gate],
        out_shape=[jax.ShapeDtypeStruct((Lp, D_MODEL), F32)] * 3 + [jax.ShapeDtypeStruct((Lp, LANES), F32)],
        scratch_shapes=[pltpu.VMEM((GDN_H, GDN_D, GDN_D), F32)],
        name="gdn_chunk_bwd")(qkv, qkv, qkv, gsm, do, s_in, t_in)


def _rot(x, c, s):
    half = RET_D // 2
    x1 = x[:, :half]
    x2 = x[:, half:]
    return jnp.concatenate([x1 * c - x2 * s, x2 * c + x1 * s], axis=1)


def _rot_bwd(d, c, s):
    half = RET_D // 2
    d1 = d[:, :half]
    d2 = d[:, half:]
    return jnp.concatenate([d1 * c + d2 * s, d2 * c - d1 * s], axis=1)


def _ret_tables():
    hh = jnp.arange(RET_H, dtype=F32)
    lg = jnp.log(1.0 - 2.0 ** (-5.0 - hh))
    idx = jnp.arange(CHUNK, dtype=F32)
    tril = jnp.asarray(np.tril(np.ones((CHUNK, CHUNK), dtype=bool)))
    dmask = jnp.where(tril, jnp.exp((idx[:, None] - idx[None, :]) * lg[:, None, None]), 0.0)
    qdec = jnp.exp((idx[None, :] + 1.0) * lg[:, None])
    kdec = jnp.exp((CHUNK - 1.0 - idx[None, :]) * lg[:, None])
    gch = jnp.exp(CHUNK * lg)
    qdec = jnp.broadcast_to(qdec[:, :, None], (RET_H, CHUNK, RET_D))
    kdec = jnp.broadcast_to(kdec[:, :, None], (RET_H, CHUNK, RET_D))
    gch = jnp.broadcast_to(gch[:, None, None], (RET_H, 8, LANES))
    return dmask, qdec, kdec, gch


def _ret_specs(N, rev):
    cn = (lambda n: N - 1 - n) if rev else (lambda n: n)
    col = lambda j: pl.BlockSpec((CHUNK, D_MODEL), lambda n: (cn(n), j))
    tab = lambda a, b: pl.BlockSpec((RET_H, a, b), lambda n: (0, 0, 0))
    rope = pl.BlockSpec((CHUNK, LANES), lambda n: (cn(n), 0))
    st = pl.BlockSpec((RET_H, None, RET_D, RET_D), lambda n: (0, cn(n), 0, 0))
    return col, tab, rope, st


def _ret_chunk_fwd(proj_m, cos, sin, tables):
    Lp = proj_m.shape[0]
    N = Lp // CHUNK
    dmask, qdec, kdec, gch = tables

    def body(q_ref, k_ref, v_ref, c_ref, s_ref, dm_ref, qd_ref, kd_ref, g_ref, o_ref, sin_ref, S):
        n = pl.program_id(0)

        @pl.when(n == 0)
        def _():
            S[...] = jnp.zeros_like(S)

        c = c_ref[...]
        s = s_ref[...]
        heads = range(RET_H)
        sls = [slice(h * RET_D, (h + 1) * RET_D) for h in heads]
        H = lambda f: [f(h) for h in heads]
        qr = H(lambda h: _rot(q_ref[:, sls[h]], c, s))
        ks = H(lambda h: _rot(k_ref[:, sls[h]], c, s) * (RET_D ** -0.5))
        v = H(lambda h: v_ref[:, sls[h]])
        s0 = H(lambda h: S[h])
        a = H(lambda h: _dnt(qr[h], ks[h]) * dm_ref[h])
        qs = H(lambda h: _d(qr[h] * qd_ref[h], s0[h]))
        kv = H(lambda h: _dtn(ks[h] * kd_ref[h], v[h]))
        for h in heads:
            o_ref[:, sls[h]] = _d(a[h], v[h]) + qs[h]
            sin_ref[h] = s0[h]
            S[h] = s0[h] * g_ref[h, 0:1, 0:1] + kv[h]

    col, tab, rope, st = _ret_specs(N, False)
    return pl.pallas_call(
        body, grid=(N,),
        in_specs=[col(3), col(4), col(5), rope, rope,
                  tab(CHUNK, CHUNK), tab(CHUNK, RET_D), tab(CHUNK, RET_D), tab(8, LANES)],
        out_specs=[col(0), st],
        out_shape=[jax.ShapeDtypeStruct((Lp, D_MODEL), F32), jax.ShapeDtypeStruct((RET_H, N, RET_D, RET_D), F32)],
        scratch_shapes=[pltpu.VMEM((RET_H, RET_D, RET_D), F32)],
        name="ret_chunk_fwd")(proj_m, proj_m, proj_m, cos, sin, dmask, qdec, kdec, gch)


def _ret_chunk_bwd(proj_m, cos, sin, tables, do, s_in):
    Lp = proj_m.shape[0]
    N = Lp // CHUNK
    dmask, qdec, kdec, gch = tables

    def body(q_ref, k_ref, v_ref, c_ref, s_ref, dm_ref, qd_ref, kd_ref, g_ref, do_ref, sin_ref,
             dq_ref, dk_ref, dv_ref, dS):
        n = pl.program_id(0)

        @pl.when(n == 0)
        def _():
            dS[...] = jnp.zeros_like(dS)

        c = c_ref[...]
        s = s_ref[...]
        kscale = RET_D ** -0.5
        heads = range(RET_H)
        sls = [slice(h * RET_D, (h + 1) * RET_D) for h in heads]
        H = lambda f: [f(h) for h in heads]
        qr = H(lambda h: _rot(q_ref[:, sls[h]], c, s))
        ks = H(lambda h: _rot(k_ref[:, sls[h]], c, s) * kscale)
        v = H(lambda h: v_ref[:, sls[h]])
        dov = H(lambda h: do_ref[:, sls[h]])
        s0 = H(lambda h: sin_ref[h])
        dsv = H(lambda h: dS[h])
        ad = H(lambda h: _dnt(qr[h], ks[h]) * dm_ref[h])
        da = H(lambda h: _dnt(dov[h], v[h]) * dm_ref[h])
        kds = H(lambda h: _d(ks[h] * kd_ref[h], dsv[h]))
        dos = H(lambda h: _dnt(dov[h], s0[h]) * qd_ref[h])
        vds = H(lambda h: _dnt(v[h], dsv[h]) * kd_ref[h])
        qdo = H(lambda h: _dtn(qr[h] * qd_ref[h], dov[h]))
        for h in heads:
            dS[h] = dsv[h] * g_ref[h, 0:1, 0:1] + qdo[h]
        for h in heads:
            dv_ref[:, sls[h]] = (_dtn(ad[h], dov[h]) + kds[h]).astype(BF16)
            dq_ref[:, sls[h]] = _rot_bwd(_d(da[h], ks[h]) + dos[h], c, s).astype(BF16)
            dk_ref[:, sls[h]] = _rot_bwd((_dtn(da[h], qr[h]) + vds[h]) * kscale, c, s).astype(BF16)

    col, tab, rope, st = _ret_specs(N, True)
    return pl.pallas_call(
        body, grid=(N,),
        in_specs=[col(3), col(4), col(5), rope, rope,
                  tab(CHUNK, CHUNK), tab(CHUNK, RET_D), tab(CHUNK, RET_D), tab(8, LANES), col(0), st],
        out_specs=[col(0), col(0), col(0)],
        out_shape=[jax.ShapeDtypeStruct((Lp, D_MODEL), BF16)] * 3,
        scratch_shapes=[pltpu.VMEM((RET_H, RET_D, RET_D), F32)],
        name="ret_chunk_bwd")(proj_m, proj_m, proj_m, cos, sin, dmask, qdec, kdec, gch, do, s_in)


def _merge_specs(tr):
    col = lambda j: pl.BlockSpec((tr, D_MODEL), lambda i: (i, j))
    return col


def _merge_fwd(o_a, o_b, proj_m, gnorm):
    Lp = o_a.shape[0]
    tr = _tile(Lp, 192, 16)

    def body(oa_ref, ob_ref, gz_ref, rg_ref, ga_ref, gb_ref, gn_ref, y_ref):
        gn = gn_ref[...]
        oa = oa_ref[...]
        ob = ob_ref[...]
        gz = gz_ref[...]
        ya = []
        for j in range(GDN_H):
            seg = oa[:, j * GDN_D:(j + 1) * GDN_D]
            r = lax.rsqrt(jnp.mean(seg * seg, axis=-1, keepdims=True) + EPS)
            ya.append(seg * r * gn)
        ya = jnp.concatenate(ya, axis=1) * (gz * _sig(gz))
        yb = []
        for j in range(RET_H):
            seg = ob[:, j * RET_D:(j + 1) * RET_D]
            r = lax.rsqrt(jnp.mean(seg * seg, axis=-1, keepdims=True) + EPS)
            yb.append(seg * r)
        rg = rg_ref[...]
        yb = jnp.concatenate(yb, axis=1) * (rg * _sig(rg))
        y_ref[...] = (_sig(ga_ref[...]) * ya + _sig(gb_ref[...]) * yb).astype(BF16)

    col = _merge_specs(tr)
    return pl.pallas_call(
        body, grid=(Lp // tr,),
        in_specs=[col(0), col(0), col(6), col(7), col(8), col(9), pl.BlockSpec((1, GDN_D), lambda i: (0, 0))],
        out_specs=col(0), out_shape=jax.ShapeDtypeStruct((Lp, D_MODEL), BF16),
        name="merge_fwd")(o_a, o_b, proj_m, proj_m, proj_m, proj_m, gnorm)


def _merge_bwd(dy, o_a, o_b, proj_m, gnorm):
    Lp = o_a.shape[0]
    tr = _tile(Lp, 192, 16)

    def body(dy_ref, oa_ref, ob_ref, gz_ref, rg_ref, ga_ref, gb_ref, gn_ref, dc_ref, doa_ref, dob_ref, dgn_ref):
        i = pl.program_id(0)
        gn = gn_ref[...]
        dyv = dy_ref[...]
        oa = oa_ref[...]
        ob = ob_ref[...]
        gz = gz_ref[...]
        rg = rg_ref[...]
        sa = _sig(ga_ref[...])
        sb = _sig(gb_ref[...])
        dya = dyv * sa
        dyb = dyv * sb
        sgz = _sig(gz)
        szz = gz * sgz
        dgn = jnp.zeros((1, GDN_D), F32)
        ya = []
        dgz = []
        for j in range(GDN_H):
            sl = slice(j * GDN_D, (j + 1) * GDN_D)
            seg = oa[:, sl]
            r = lax.rsqrt(jnp.mean(seg * seg, axis=-1, keepdims=True) + EPS)
            xh = seg * r
            oan = xh * gn
            ya.append(oan * szz[:, sl])
            dgz.append(dya[:, sl] * oan * (sgz[:, sl] * (1.0 + gz[:, sl] * (1.0 - sgz[:, sl]))))
            doan = dya[:, sl] * szz[:, sl]
            dgn = dgn + jnp.sum(doan * xh, axis=0, keepdims=True)
            dxh = doan * gn
            doa_ref[:, sl] = r * (dxh - xh * jnp.mean(dxh * xh, axis=-1, keepdims=True))
        ya = jnp.concatenate(ya, axis=1)
        srg = _sig(rg)
        srr = rg * srg
        yb = []
        drg = []
        for j in range(RET_H):
            sl = slice(j * RET_D, (j + 1) * RET_D)
            seg = ob[:, sl]
            r = lax.rsqrt(jnp.mean(seg * seg, axis=-1, keepdims=True) + EPS)
            xh = seg * r
            yb.append(xh * srr[:, sl])
            drg.append(dyb[:, sl] * xh * (srg[:, sl] * (1.0 + rg[:, sl] * (1.0 - srg[:, sl]))))
            dxh = dyb[:, sl] * srr[:, sl]
            dob_ref[:, sl] = r * (dxh - xh * jnp.mean(dxh * xh, axis=-1, keepdims=True))
        yb = jnp.concatenate(yb, axis=1)
        dc_ref[:, 0:D_MODEL] = jnp.concatenate(dgz, axis=1).astype(BF16)
        dc_ref[:, D_MODEL:2 * D_MODEL] = jnp.concatenate(drg, axis=1).astype(BF16)
        dc_ref[:, 2 * D_MODEL:3 * D_MODEL] = (dyv * ya * sa * (1.0 - sa)).astype(BF16)
        dc_ref[:, 3 * D_MODEL:] = (dyv * yb * sb * (1.0 - sb)).astype(BF16)

        @pl.when(i == 0)
        def _():
            dgn_ref[...] = dgn

        @pl.when(i > 0)
        def _():
            dgn_ref[...] += dgn

    col = _merge_specs(tr)
    return pl.pallas_call(
        body, grid=(Lp // tr,),
        in_specs=[col(0), col(0), col(0), col(6), col(7), col(8), col(9), pl.BlockSpec((1, GDN_D), lambda i: (0, 0))],
        out_specs=[pl.BlockSpec((tr, 4 * D_MODEL), lambda i: (i, 0)), col(0), col(0),
                   pl.BlockSpec((1, GDN_D), lambda i: (0, 0))],
        out_shape=[jax.ShapeDtypeStruct((Lp, 4 * D_MODEL), BF16), jax.ShapeDtypeStruct((Lp, D_MODEL), F32),
                   jax.ShapeDtypeStruct((Lp, D_MODEL), F32), jax.ShapeDtypeStruct((1, GDN_D), F32)],
        name="merge_bwd")(dy, o_a, o_b, proj_m, proj_m, proj_m, proj_m, gnorm)


def _ffn_conv(ext, w, b, tr, lo):
    acc = b + w[0:1, :] * ext[lo:lo + tr, :]
    for kk in range(1, FFN_CONV):
        acc = acc + w[kk:kk + 1, :] * ext[lo + kk:lo + kk + tr, :]
    return acc


def _ffn_act(up, conv_w, conv_b):
    Lp = up.shape[0]
    tr = _tile(Lp, 192, 16)
    W2 = 2 * D_FF

    def body(main_ref, prev_ref, w_ref, b_ref, act_ref):
        i = pl.program_id(0)
        prev = jnp.where(i > 0, prev_ref[...], 0.0)
        ext = jnp.concatenate([prev, main_ref[...]], axis=0)
        u = _ffn_conv(ext, w_ref[...], b_ref[...], tr, 8 - (FFN_CONV - 1))
        a = u[:, :D_FF]
        act_ref[...] = (a * _sig(a) * u[:, D_FF:]).astype(BF16)

    return pl.pallas_call(
        body, grid=(Lp // tr,),
        in_specs=[pl.BlockSpec((tr, W2), lambda i: (i, 0)), _halo_prev(tr, W2),
                  pl.BlockSpec((FFN_CONV, W2), lambda i: (0, 0)), pl.BlockSpec((1, W2), lambda i: (0, 0))],
        out_specs=pl.BlockSpec((tr, D_FF), lambda i: (i, 0)),
        out_shape=jax.ShapeDtypeStruct((Lp, D_FF), BF16), name="ffn_act")(up, up, conv_w, conv_b)


def _ffn_act_bwd(up, dact, conv_w, conv_b):
    Lp = up.shape[0]
    tr = _tile(Lp, 96, 16)
    W2 = 2 * D_FF
    te = tr + 8

    def body(main_ref, prev_ref, next_ref, da_ref, dan_ref, w_ref, b_ref, dup_ref, acc_ref):
        i = pl.program_id(0)
        w = w_ref[...]
        prev = jnp.where(i > 0, prev_ref[...], 0.0)
        ext = jnp.concatenate([prev, main_ref[...], next_ref[...]], axis=0)
        u = _ffn_conv(ext, w, b_ref[...], te, 8 - (FFN_CONV - 1))
        a = u[:, :D_FF]
        b = u[:, D_FF:]
        rowe = i * tr + lax.broadcasted_iota(jnp.int32, (te, 1), 0)
        dae = jnp.where(rowe < Lp, jnp.concatenate([da_ref[...], dan_ref[...]], axis=0), 0.0)
        sg = _sig(a)
        du = jnp.concatenate([dae * b * (sg * (1.0 + a * (1.0 - sg))), dae * (a * sg)], axis=1)
        acc = w[FFN_CONV - 1:FFN_CONV, :] * du[0:tr, :]
        for kk in range(FFN_CONV - 1):
            sh = FFN_CONV - 1 - kk
            acc = acc + w[kk:kk + 1, :] * du[sh:sh + tr, :]
        dup_ref[...] = acc.astype(BF16)
        dum = du[0:tr, :]
        lo = 8 - (FFN_CONV - 1)
        rows = [jnp.sum(dum * ext[lo + kk:lo + kk + tr, :], axis=0, keepdims=True) for kk in range(FFN_CONV)]
        rows.append(jnp.sum(dum, axis=0, keepdims=True))
        part = jnp.concatenate(rows + [jnp.zeros((8 - len(rows), W2), F32)], axis=0)

        @pl.when(i == 0)
        def _():
            acc_ref[...] = part

        @pl.when(i > 0)
        def _():
            acc_ref[...] += part

    return pl.pallas_call(
        body, grid=(Lp // tr,),
        in_specs=[pl.BlockSpec((tr, W2), lambda i: (i, 0)), _halo_prev(tr, W2), _halo_next(tr, W2, Lp),
                  pl.BlockSpec((tr, D_FF), lambda i: (i, 0)), _halo_next(tr, D_FF, Lp),
                  pl.BlockSpec((FFN_CONV, W2), lambda i: (0, 0)), pl.BlockSpec((1, W2), lambda i: (0, 0))],
        out_specs=[pl.BlockSpec((tr, W2), lambda i: (i, 0)), pl.BlockSpec((8, W2), lambda i: (0, 0))],
        out_shape=[jax.ShapeDtypeStruct((Lp, W2), BF16), jax.ShapeDtypeStruct((8, W2), F32)],
        name="ffn_act_bwd")(up, up, up, dact, dact, conv_w, conv_b)


def _local_step(hpad, tgt, pad, wt):
    Lp = hpad.shape[0]
    first = pad + N_META
    pos = jnp.arange(Lp, dtype=F32) - float(pad)
    half = RET_D // 2
    inv = 1.0 / (ROPE_BASE ** (jnp.arange(half, dtype=F32) / half))
    ang = pos[:, None] * inv[None, :]
    cos, sin = jnp.cos(ang), jnp.sin(ang)
    tables = _ret_tables()
    gparams = jnp.zeros((8, LANES), F32).at[0, :GDN_H].set(wt["a_log"]).at[1, :GDN_H].set(wt["dt_bias"])

    hn1 = _rms_fwd(hpad, wt["norm1"], "rms1_fwd")
    proj_m = _mm_nn(hn1, wt["w_main"], name="proj_main")
    proj_s = _mm_nn(hn1, wt["w_small"], name="proj_small")
    qkv, gsm = _gdn_pre(proj_m, proj_s, wt["gdn_conv_w"], gparams, pad)
    o_a, s_a, t_a = _gdn_chunk_fwd(qkv, gsm)
    o_b, s_b = _ret_chunk_fwd(proj_m, cos, sin, tables)
    y = _merge_fwd(o_a, o_b, proj_m, wt["gdn_norm"])
    h1 = _mm_nn(y, wt["w_out"], res=hpad, name="out_proj")
    hn2 = _rms_fwd(h1, wt["norm2"], "rms2_fwd")
    up = _mm_nn(hn2, wt["w_up"], name="ffn_up")
    act = _ffn_act(up, wt["ffn_conv_w"], wt["ffn_conv_b"])
    h2 = _mm_nn(act, wt["w_down"], res=h1, name="ffn_down")
    lossvec, dh2, dh2b, d_norm_f = _final(h2, wt["norm_f"], tgt, first)

    d_w_down = _mm_tn(act, dh2b, name="dw_down")
    dact = _mm_nt(dh2b, wt["w_down"], name="d_act")
    dup, ffn_rows = _ffn_act_bwd(up, dact, wt["ffn_conv_w"], wt["ffn_conv_b"])
    d_w_up = _mm_tn(hn2, dup, name="dw_up")
    dhn2 = _mm_nt(dup, wt["w_up"], name="d_hn2")
    dh1, dh1b, d_norm2 = _rms_bwd(h1, wt["norm2"], dhn2, dh2, pad, "rms2_bwd")

    d_w_out = _mm_tn(y, dh1b, name="dw_out")
    dy = _mm_nt(dh1b, wt["w_out"], name="d_y")
    d_c, do_a, do_b, d_gnorm = _merge_bwd(dy, o_a, o_b, proj_m, wt["gdn_norm"])
    drq, drk, drv = _ret_chunk_bwd(proj_m, cos, sin, tables, do_b, s_b)
    dq, dk, dv, dgs = _gdn_chunk_bwd(qkv, gsm, do_a, s_a, t_a)
    d_a, d_s, conv_rows, gp_rows = _gdn_pre_bwd(proj_m, proj_s, wt["gdn_conv_w"], gparams, dq, dk, dv, dgs, pad)

    wm = wt["w_main"]
    segs = [(d_a, 0, 3 * D_MODEL), (drq, 3 * D_MODEL, D_MODEL), (drk, 4 * D_MODEL, D_MODEL),
            (drv, 5 * D_MODEL, D_MODEL), (d_c, 6 * D_MODEL, 4 * D_MODEL)]
    d_w_main = jnp.concatenate([_mm_tn(hn1, d, name="dw_in_%d" % i) for i, (d, _, _) in enumerate(segs)], axis=1)
    d_w_small = _mm_tn(hn1, d_s, name="dw_in_small")
    dhn1 = _mm_nt(d_s, wt["w_small"], name="d_hn1_small")
    for i, (d, off, width) in enumerate(segs):
        dhn1 = _mm_nt(d, wm[:, off:off + width], res=dhn1, name="d_hn1_%d" % i)
    dh0, _, d_norm1 = _rms_bwd(hpad, wt["norm1"], dhn1, dh1, pad, "rms1_bwd")

    grads = {
        "norm1": d_norm1, "w_main": d_w_main, "w_small": d_w_small, "gdn_conv_w": conv_rows[:GDN_CONV],
        "a_log": gp_rows[0, :GDN_H], "dt_bias": gp_rows[1, :GDN_H], "gdn_norm": d_gnorm, "w_out": d_w_out,
        "norm2": d_norm2, "w_up": d_w_up, "ffn_conv_w": ffn_rows[:FFN_CONV], "ffn_conv_b": ffn_rows[FFN_CONV:FFN_CONV + 1],
        "w_down": d_w_down, "norm_f": d_norm_f,
    }
    return lossvec, dh0, grads


def _peer(k):
    ix, iy, ic = lax.axis_index("x"), lax.axis_index("y"), lax.axis_index("c")
    px = 1 - ix if (k >> 2) & 1 else ix
    py = 1 - iy if (k >> 1) & 1 else iy
    pc = 1 - ic if k & 1 else ic
    return (px, py, pc), 4 * px + 2 * py + pc


def _comm_call(body, n, out_shapes, name, args):
    hbm = pl.BlockSpec(memory_space=pl.ANY)
    return pl.pallas_call(
        body, out_shape=out_shapes, in_specs=[hbm] * n, out_specs=[hbm] * n,
        scratch_shapes=[pltpu.SemaphoreType.DMA((n, N_DEV - 1)), pltpu.SemaphoreType.DMA((n, N_DEV - 1)),
                        pltpu.SemaphoreType.DMA((n,))],
        name=name)(*args)


def _all_gather(xs, name):
    n = len(xs)

    def body(*refs):
        x_refs, out_refs = refs[:n], refs[n:2 * n]
        send_sems, recv_sems, local_sems = refs[2 * n:]
        _, me = _peer(0)
        pending = []
        for i in range(n):
            local = pltpu.make_async_copy(x_refs[i], out_refs[i].at[me], local_sems.at[i])
            local.start()
            pending.append(local)
        sends = []
        for i in range(n):
            for k in range(1, N_DEV):
                dev, _ = _peer(k)
                cp = pltpu.make_async_remote_copy(
                    src_ref=x_refs[i], dst_ref=out_refs[i].at[me], send_sem=send_sems.at[i, k - 1],
                    recv_sem=recv_sems.at[i, k - 1], device_id=dev, device_id_type=MESH_T)
                cp.start()
                sends.append(cp)
        for i in range(n):
            for k in range(1, N_DEV):
                dev, idx = _peer(k)
                pltpu.make_async_remote_copy(
                    src_ref=x_refs[i], dst_ref=out_refs[i].at[idx], send_sem=send_sems.at[i, k - 1],
                    recv_sem=recv_sems.at[i, k - 1], device_id=dev, device_id_type=MESH_T).wait_recv()
        for cp in sends:
            cp.wait_send()
        for local in pending:
            local.wait()

    out_shapes = [jax.ShapeDtypeStruct((N_DEV,) + a.shape, a.dtype) for a in xs]
    return _comm_call(body, n, out_shapes, name, xs)


def _all_to_all(gs, name):
    n = len(gs)

    def body(*refs):
        g_refs, out_refs = refs[:n], refs[n:2 * n]
        send_sems, recv_sems, local_sems = refs[2 * n:]
        _, me = _peer(0)
        pending = []
        for i in range(n):
            local = pltpu.make_async_copy(g_refs[i].at[me], out_refs[i].at[0], local_sems.at[i])
            local.start()
            pending.append(local)
        sends = []
        for i in range(n):
            for k in range(1, N_DEV):
                dev, idx = _peer(k)
                cp = pltpu.make_async_remote_copy(
                    src_ref=g_refs[i].at[idx], dst_ref=out_refs[i].at[k], send_sem=send_sems.at[i, k - 1],
                    recv_sem=recv_sems.at[i, k - 1], device_id=dev, device_id_type=MESH_T)
                cp.start()
                sends.append(cp)
        for cp in sends:
            cp.wait_recv()
        for cp in sends:
            cp.wait_send()
        for local in pending:
            local.wait()

    out_shapes = [jax.ShapeDtypeStruct(g.shape, g.dtype) for g in gs]
    return _comm_call(body, n, out_shapes, name, gs)


def _adamw(gslabs, w, m, v, name):
    R, Cw = w.shape
    tr = _tile(R, 64 if Cw > 1024 else 128, 8)
    c1 = 1.0 - ADAM_B1 ** ADAM_STEP
    c2 = 1.0 - ADAM_B2 ** ADAM_STEP

    def body(g_ref, w_ref, m_ref, v_ref, go_ref, d_ref, mo_ref, vo_ref):
        g = g_ref[0]
        for k in range(1, N_DEV):
            g = g + g_ref[k]
        mn = ADAM_B1 * m_ref[...] + (1.0 - ADAM_B1) * g
        vn = ADAM_B2 * v_ref[...] + (1.0 - ADAM_B2) * (g * g)
        m_hat = mn / c1
        v_hat = vn / c2
        go_ref[...] = g
        d_ref[...] = -ADAM_LR * (m_hat / (jnp.sqrt(v_hat) + ADAM_EPS) + ADAM_WD * w_ref[...])
        mo_ref[...] = mn
        vo_ref[...] = vn

    blk = pl.BlockSpec((tr, Cw), lambda i: (i, 0))
    return pl.pallas_call(
        body, grid=(R // tr,),
        in_specs=[pl.BlockSpec((N_DEV, tr, Cw), lambda i: (0, i, 0)), blk, blk, blk],
        out_specs=[blk] * 4, out_shape=[jax.ShapeDtypeStruct((R, Cw), F32)] * 4, name=name)(gslabs, w, m, v)


def _pack(arrs, row_mult, dtype=F32):
    parts = []
    total = 0
    for a in arrs:
        f = a.reshape(-1).astype(dtype)
        n = -(-f.shape[0] // 1024) * 1024
        parts.append(jnp.pad(f, (0, n - f.shape[0])))
        total += n
    rows = total // LANES
    rows_p = -(-rows // row_mult) * row_mult
    flat = jnp.concatenate(parts)
    flat = jnp.pad(flat, (0, rows_p * LANES - total))
    return flat.reshape(rows_p, LANES)


def _unpack(packed, shapes):
    lead = packed.shape[:-2]
    flat = packed.reshape(lead + (-1,))
    out = []
    off = 0
    for s in shapes:
        n = int(np.prod(s))
        out.append(flat[..., off:off + n].reshape(lead + tuple(s)))
        off += -(-n // 1024) * 1024
    return out


def _gather_cols(stacked):
    d, r, c = stacked.shape
    return stacked.transpose(1, 0, 2).reshape(r, d * c)


def _scatter_cols(full):
    r, n = full.shape
    return full.reshape(r, N_DEV, n // N_DEV).transpose(1, 0, 2)


def kernel(x, meta, norm1, w_in, gdn_conv_w, gdn_a_log, gdn_dt_bias, gdn_norm, w_out, norm2, w_ffn_up, ffn_conv_w, ffn_conv_b, w_ffn_down, norm_f, loss_target, m_meta, m_norm1, m_w_in, m_gdn_conv_w, m_gdn_a_log, m_gdn_dt_bias, m_gdn_norm, m_w_out, m_norm2, m_w_ffn_up, m_ffn_conv_w, m_ffn_conv_b, m_w_ffn_down, m_norm_f, v_meta, v_norm1, v_w_in, v_gdn_conv_w, v_gdn_a_log, v_gdn_dt_bias, v_gdn_norm, v_w_out, v_norm2, v_w_ffn_up, v_ffn_conv_w, v_ffn_conv_b, v_w_ffn_down, v_norm_f):
    S = x.shape[1]
    L = N_META + S
    pad = (-L) % CHUNK
    Lp = L + pad

    big = [w_in[0], w_out[0], w_ffn_up[0], w_ffn_down[0]]
    small = [meta, gdn_conv_w, ffn_conv_w]
    w_in_s, w_out_s, w_up_s, w_down_s, small_all = _all_gather(
        [a.astype(BF16) for a in big] + [_pack(small, 8)], "gather_weights")
    meta_s, gconv_s, fconv_s = _unpack(small_all, [a.shape for a in small])
    w_in_f = _gather_cols(w_in_s)
    w_main = jnp.concatenate([w_in_f[:, _O_GQ:_O_GZ], w_in_f[:, _O_RQ:_O_RG], w_in_f[:, _O_GZ:_O_GA],
                              w_in_f[:, _O_RG:_O_END]], axis=1)
    w_small = jnp.pad(w_in_f[:, _O_GA:_O_RQ], ((0, 0), (0, LANES - 2 * GDN_H)))
    wt = {
        "norm1": norm1, "w_main": w_main, "w_small": w_small,
        "gdn_conv_w": _gather_cols(gconv_s[:, 0]), "a_log": gdn_a_log[0], "dt_bias": gdn_dt_bias[0],
        "gdn_norm": gdn_norm, "w_out": w_out_s.reshape(D_MODEL, D_MODEL), "norm2": norm2,
        "w_up": _gather_cols(w_up_s), "ffn_conv_w": _gather_cols(fconv_s[:, 0]), "ffn_conv_b": ffn_conv_b,
        "w_down": w_down_s.reshape(D_FF, D_MODEL), "norm_f": norm_f.reshape(1, D_MODEL),
    }
    meta_f = _gather_cols(meta_s)

    hpad = jnp.concatenate([jnp.zeros((pad, D_MODEL), F32), meta_f, x[0]], axis=0)
    tgt = jnp.concatenate([jnp.zeros((pad + N_META, D_MODEL), F32), loss_target[0]], axis=0)
    lossvec, dh0, gr = _local_step(hpad, tgt, pad, wt)

    loss = lax.psum(jnp.sum(lossvec), ("x", "y", "c"))
    grad_x = dh0[pad + N_META:][None]

    gm = gr["w_main"]
    d_w_in = jnp.concatenate([gm[:, 0:3072], gm[:, 6144:7168], gr["w_small"][:, :2 * GDN_H], gm[:, 3072:6144],
                              gm[:, 7168:]], axis=1)
    g_big = [_scatter_cols(d_w_in), gr["w_out"].reshape(N_DEV, D_MODEL // N_DEV, D_MODEL),
             _scatter_cols(gr["w_up"]), gr["w_down"].reshape(N_DEV, D_FF // N_DEV, D_MODEL)]
    g_sm = [_scatter_cols(dh0[pad:pad + N_META]), _scatter_cols(gr["gdn_conv_w"]), _scatter_cols(gr["ffn_conv_w"])]
    g_small = jnp.stack([_pack([g[d] for g in g_sm], 8) for d in range(N_DEV)])
    slabs = _all_to_all(g_big + [g_small], "exchange_gradients")
    big_m = [m_w_in[0], m_w_out[0], m_w_ffn_up[0], m_w_ffn_down[0]]
    big_v = [v_w_in[0], v_w_out[0], v_w_ffn_up[0], v_w_ffn_down[0]]
    big_out = [_adamw(slabs[i], big[i], big_m[i], big_v[i], "adamw_big_%d" % i) for i in range(len(big))]
    small_out = _adamw(slabs[-1], _pack(small, 8), _pack([m_meta, m_gdn_conv_w, m_ffn_conv_w], 8),
                       _pack([v_meta, v_gdn_conv_w, v_ffn_conv_w], 8), "adamw_small_sharded")
    small_un = [_unpack(o, [a.shape for a in small]) for o in small_out]
    sh_g, sh_d, sh_nm, sh_nv = [
        [small_un[j][0], big_out[0][j][None], small_un[j][1], big_out[1][j][None], big_out[2][j][None],
         small_un[j][2], big_out[3][j][None]] for j in range(4)]

    rep_w = [norm1, gdn_a_log, gdn_dt_bias, gdn_norm, norm2, ffn_conv_b, norm_f]
    rep_m = [m_norm1, m_gdn_a_log, m_gdn_dt_bias, m_gdn_norm, m_norm2, m_ffn_conv_b, m_norm_f]
    rep_v = [v_norm1, v_gdn_a_log, v_gdn_dt_bias, v_gdn_norm, v_norm2, v_ffn_conv_b, v_norm_f]
    rep_g = [gr["norm1"], gr["a_log"], gr["dt_bias"], gr["gdn_norm"], gr["norm2"], gr["ffn_conv_b"], gr["norm_f"]]
    rep_slabs, = _all_gather([_pack(rep_g, 8)], "gather_small_gradients")
    rep_out = _adamw(rep_slabs, _pack(rep_w, 8), _pack(rep_m, 8), _pack(rep_v, 8), "adamw_replicated")
    rep_shapes = [a.shape for a in rep_w]
    rp_g, rp_d, rp_nm, rp_nv = [_unpack(o, rep_shapes) for o in rep_out]

    def order(sh, rp):
        return [sh[0], rp[0], sh[1], sh[2], rp[1], rp[2], rp[3], sh[3], rp[4], sh[4], sh[5], rp[5], sh[6], rp[6]]

    return (loss, grad_x, *order(sh_g, rp_g), *order(sh_d, rp_d), *order(sh_nm, rp_nm), *order(sh_nv, rp_nv))
```

```python
import functools
import math

import numpy as np
import jax
import jax.numpy as jnp
from jax import lax
from jax.experimental import pallas as pl
from jax.experimental.pallas import tpu as pltpu

F32 = jnp.float32
BF16 = jnp.bfloat16
HI = lax.Precision.HIGHEST

D_MODEL = 1024
N_META = 16
CHUNK = 64
GDN_H = 8
GDN_D = 128
RET_H = 4
RET_D = 256
D_FF = 2816
GDN_CONV = 4
FFN_CONV = 3
ROPE_BASE = 10000.0
EPS = 1e-6
N_DEV = 8
LANES = 128
MAIN_W = 10 * 1024
_O_GQ, _O_GZ, _O_GA, _O_RQ, _O_RG, _O_GATE, _O_END = 0, 3072, 4096, 4112, 7184, 8208, 10256

ADAM_LR = 0.001
ADAM_B1 = 0.9
ADAM_B2 = 0.999
ADAM_EPS = 1e-08
ADAM_WD = 0.01
ADAM_STEP = 10

MESH_T = pl.DeviceIdType.MESH


def _tile(n, target, mult):
    best = None
    for d in range(mult, min(n, target) + 1, mult):
        if n % d == 0:
            best = d
    assert best is not None, (n, target, mult)
    return best


def _sig(x):
    return 1.0 / (1.0 + jnp.exp(-x))


def _d(a, b):
    return jnp.dot(a.astype(BF16), b.astype(BF16), preferred_element_type=F32)


def _dnt(a, b):
    return lax.dot_general(a.astype(BF16), b.astype(BF16), (((1,), (1,)), ((), ())), preferred_element_type=F32)


def _dtn(a, b):
    return lax.dot_general(a.astype(BF16), b.astype(BF16), (((0,), (0,)), ((), ())), preferred_element_type=F32)


def _dx(a, b):
    return jnp.dot(a, b, preferred_element_type=F32, precision=HI)


def _dxnt(a, b):
    return lax.dot_general(a, b, (((1,), (1,)), ((), ())), preferred_element_type=F32, precision=HI)


def _dxtn(a, b):
    return lax.dot_general(a, b, (((0,), (0,)), ((), ())), preferred_element_type=F32, precision=HI)


def _split(a):
    hi = a.astype(BF16)
    return hi, (a - hi.astype(F32)).astype(BF16)


def _d3g(a, b, dims):
    ah, al = _split(a)
    bh, bl = _split(b)
    f = functools.partial(lax.dot_general, dimension_numbers=dims, preferred_element_type=F32)
    return f(ah, bh) + (f(ah, bl) + f(al, bh))


_NN = (((1,), (0,)), ((), ()))
_NT = (((1,), (1,)), ((), ()))
_TN = (((0,), (0,)), ((), ()))


def _rowsum(x):
    return jnp.sum(x, axis=1, keepdims=True)


def _allsum(x):
    return jnp.sum(jnp.sum(x, axis=1, keepdims=True), axis=0, keepdims=True)


def _mm_nn(a, b, res=None, out_dtype=F32, name="mm_nn"):
    M, K = a.shape
    N = b.shape[1]
    tm = _tile(M, 704, 16)
    tn = _tile(N, 2816, 128)

    def body(*refs):
        if res is None:
            a_ref, b_ref, o_ref = refs
        else:
            a_ref, b_ref, r_ref, o_ref = refs
        acc = jnp.dot(a_ref[...], b_ref[...], preferred_element_type=F32)
        if res is not None:
            acc = acc + r_ref[...]
        o_ref[...] = acc.astype(out_dtype)

    in_specs = [pl.BlockSpec((tm, K), lambda j, i: (i, 0)), pl.BlockSpec((K, tn), lambda j, i: (0, j))]
    args = [a, b]
    if res is not None:
        in_specs.append(pl.BlockSpec((tm, tn), lambda j, i: (i, j)))
        args.append(res)
    return pl.pallas_call(
        body, grid=(N // tn, M // tm), in_specs=in_specs,
        out_specs=pl.BlockSpec((tm, tn), lambda j, i: (i, j)),
        out_shape=jax.ShapeDtypeStruct((M, N), out_dtype), name=name)(*args)


def _mm_nt(a, b, res=None, name="mm_nt"):
    M, Nc = a.shape
    K = b.shape[0]
    tm = _tile(M, 704, 16)
    tc = _tile(Nc, 2048, 128)

    def body(*refs):
        if res is None:
            a_ref, b_ref, o_ref = refs
        else:
            a_ref, b_ref, r_ref, o_ref = refs
        c = pl.program_id(1)
        p = lax.dot_general(a_ref[...], b_ref[...], (((1,), (1,)), ((), ())), preferred_element_type=F32)

        @pl.when(c == 0)
        def _():
            if res is None:
                o_ref[...] = p
            else:
                o_ref[...] = p + r_ref[...]

        @pl.when(c > 0)
        def _():
            o_ref[...] += p

    in_specs = [pl.BlockSpec((tm, tc), lambda i, c: (i, c)), pl.BlockSpec((K, tc), lambda i, c: (0, c))]
    args = [a, b]
    if res is not None:
        in_specs.append(pl.BlockSpec((tm, K), lambda i, c: (i, 0)))
        args.append(res)
    return pl.pallas_call(
        body, grid=(M // tm, Nc // tc), in_specs=in_specs,
        out_specs=pl.BlockSpec((tm, K), lambda i, c: (i, 0)),
        out_shape=jax.ShapeDtypeStruct((M, K), F32), name=name)(*args)


def _mm_tn(a, b, name="mm_tn"):
    M, K = a.shape
    N = b.shape[1]
    tm = _tile(M, 704, 16)
    tk = _tile(K, 1408, 128)
    tn = _tile(N, 2048, 128)

    def body(a_ref, b_ref, o_ref):
        m = pl.program_id(2)
        p = lax.dot_general(a_ref[...], b_ref[...], (((0,), (0,)), ((), ())), preferred_element_type=F32)

        @pl.when(m == 0)
        def _():
            o_ref[...] = p

        @pl.when(m > 0)
        def _():
            o_ref[...] += p

    return pl.pallas_call(
        body, grid=(K // tk, N // tn, M // tm),
        in_specs=[pl.BlockSpec((tm, tk), lambda kk, j, m: (m, kk)), pl.BlockSpec((tm, tn), lambda kk, j, m: (m, j))],
        out_specs=pl.BlockSpec((tk, tn), lambda kk, j, m: (kk, j)),
        out_shape=jax.ShapeDtypeStruct((K, N), F32), name=name)(a, b)


def _rms_fwd(x, g, name):
    Lp = x.shape[0]
    tr = _tile(Lp, 256, 16)

    def body(x_ref, g_ref, o_ref):
        xv = x_ref[...]
        r = lax.rsqrt(jnp.mean(xv * xv, axis=-1, keepdims=True) + EPS)
        o_ref[...] = (xv * r * g_ref[...]).astype(BF16)

    return pl.pallas_call(
        body, grid=(Lp // tr,),
        in_specs=[pl.BlockSpec((tr, D_MODEL), lambda i: (i, 0)), pl.BlockSpec((1, D_MODEL), lambda i: (0, 0))],
        out_specs=pl.BlockSpec((tr, D_MODEL), lambda i: (i, 0)),
        out_shape=jax.ShapeDtypeStruct((Lp, D_MODEL), BF16), name=name)(x, g)


def _rms_bwd(x, g, dy, dres, pad, name):
    Lp = x.shape[0]
    tr = _tile(Lp, 256, 16)

    def body(x_ref, g_ref, dy_ref, dr_ref, dx_ref, dxb_ref, dg_ref):
        i = pl.program_id(0)
        xv = x_ref[...]
        r = lax.rsqrt(jnp.mean(xv * xv, axis=-1, keepdims=True) + EPS)
        xh = xv * r
        dyv = dy_ref[...]
        dxh = dyv * g_ref[...]
        dx = r * (dxh - xh * jnp.mean(dxh * xh, axis=-1, keepdims=True)) + dr_ref[...]
        row = i * tr + lax.broadcasted_iota(jnp.int32, (tr, 1), 0)
        dx = jnp.where(row >= pad, dx, 0.0)
        dx_ref[...] = dx
        dxb_ref[...] = dx.astype(BF16)
        part = jnp.sum(dyv * xh, axis=0, keepdims=True)

        @pl.when(i == 0)
        def _():
            dg_ref[...] = part

        @pl.when(i > 0)
        def _():
            dg_ref[...] += part

    blk = pl.BlockSpec((tr, D_MODEL), lambda i: (i, 0))
    vec = pl.BlockSpec((1, D_MODEL), lambda i: (0, 0))
    return pl.pallas_call(
        body, grid=(Lp // tr,), in_specs=[blk, vec, blk, blk], out_specs=[blk, blk, vec],
        out_shape=[jax.ShapeDtypeStruct((Lp, D_MODEL), F32), jax.ShapeDtypeStruct((Lp, D_MODEL), BF16),
                   jax.ShapeDtypeStruct((1, D_MODEL), F32)], name=name)(x, g, dy, dres)


def _final(h2, g, tgt, first_row):
    Lp = h2.shape[0]
    tr = _tile(Lp, 256, 16)

    def body(x_ref, g_ref, t_ref, loss_ref, dx_ref, dxb_ref, dg_ref):
        i = pl.program_id(0)
        xv = x_ref[...]
        gv = g_ref[...]
        r = lax.rsqrt(jnp.mean(xv * xv, axis=-1, keepdims=True) + EPS)
        xh = xv * r
        row = i * tr + lax.broadcasted_iota(jnp.int32, (tr, 1), 0)
        err = jnp.where(row >= first_row, xh * gv - t_ref[...], 0.0)
        lpart = jnp.sum(err * err, axis=0, keepdims=True) * (0.5 / D_MODEL)
        dyv = err * (1.0 / D_MODEL)
        dxh = dyv * gv
        dx = r * (dxh - xh * jnp.mean(dxh * xh, axis=-1, keepdims=True))
        dx_ref[...] = dx
        dxb_ref[...] = dx.astype(BF16)
        part = jnp.sum(dyv * xh, axis=0, keepdims=True)

        @pl.when(i == 0)
        def _():
            dg_ref[...] = part
            loss_ref[...] = lpart

        @pl.when(i > 0)
        def _():
            dg_ref[...] += part
            loss_ref[...] += lpart

    blk = pl.BlockSpec((tr, D_MODEL), lambda i: (i, 0))
    vec = pl.BlockSpec((1, D_MODEL), lambda i: (0, 0))
    return pl.pallas_call(
        body, grid=(Lp // tr,), in_specs=[blk, vec, blk], out_specs=[vec, blk, blk, vec],
        out_shape=[jax.ShapeDtypeStruct((1, D_MODEL), F32), jax.ShapeDtypeStruct((Lp, D_MODEL), F32),
                   jax.ShapeDtypeStruct((Lp, D_MODEL), BF16), jax.ShapeDtypeStruct((1, D_MODEL), F32)],
        name="final_norm_loss")(h2, g, tgt)


def _halo_prev(tr, width, col=0):
    return pl.BlockSpec((8, width), lambda i: (jnp.maximum(i * (tr // 8) - 1, 0), col))


def _halo_next(tr, width, nrows, col=0):
    last = nrows // 8 - 1
    return pl.BlockSpec((8, width), lambda i: (jnp.minimum((i + 1) * (tr // 8), last), col))


def _gdn_conv(ext, w, tr, lo):
    acc = w[0:1, :] * ext[lo:lo + tr, :]
    for kk in range(1, GDN_CONV):
        acc = acc + w[kk:kk + 1, :] * ext[lo + kk:lo + kk + tr, :]
    return acc


def _gdn_pre(proj_m, proj_s, conv_w, gparams, pad):
    Lp = proj_m.shape[0]
    tr = _tile(Lp, 192, 64)
    W3 = 3 * D_MODEL

    def body(main_ref, prev_ref, s_ref, w_ref, gp_ref, qkv_ref, gsm_ref):
        i = pl.program_id(0)
        prev = jnp.where(i > 0, prev_ref[...], 0.0)
        ext = jnp.concatenate([prev, main_ref[...]], axis=0)
        c = _gdn_conv(ext, w_ref[...], tr, 8 - (GDN_CONV - 1))
        s = c * _sig(c)
        scale = GDN_D ** -0.5
        for j in range(2 * GDN_H):
            seg = s[:, j * GDN_D:(j + 1) * GDN_D]
            r = lax.rsqrt(_rowsum(seg * seg) + EPS)
            if j < GDN_H:
                r = r * scale
            qkv_ref[:, j * GDN_D:(j + 1) * GDN_D] = seg * r
        qkv_ref[:, 2 * D_MODEL:] = s[:, 2 * D_MODEL:]
        sm = s_ref[...]
        gp = gp_ref[...]
        lane = lax.broadcasted_iota(jnp.int32, sm.shape, 1)
        z = sm + gp[1:2, :]
        softplus = jnp.maximum(z, 0.0) + jnp.log(1.0 + jnp.exp(-jnp.abs(z)))
        lg = -jnp.exp(gp[0:1, :]) * softplus
        row = i * tr + lax.broadcasted_iota(jnp.int32, (tr, 1), 0)
        out = jnp.where(lane < GDN_H, lg, jnp.where(lane < 2 * GDN_H, _sig(sm), 0.0))
        gsm_ref[...] = jnp.where(row >= pad, out, 0.0)

    return pl.pallas_call(
        body, grid=(Lp // tr,),
        in_specs=[pl.BlockSpec((tr, W3), lambda i: (i, 0)), _halo_prev(tr, W3),
                  pl.BlockSpec((tr, LANES), lambda i: (i, 0)),
                  pl.BlockSpec((GDN_CONV, W3), lambda i: (0, 0)), pl.BlockSpec((8, LANES), lambda i: (0, 0))],
        out_specs=[pl.BlockSpec((tr, W3), lambda i: (i, 0)), pl.BlockSpec((tr, LANES), lambda i: (i, 0))],
        out_shape=[jax.ShapeDtypeStruct((Lp, W3), F32), jax.ShapeDtypeStruct((Lp, LANES), F32)],
        name="gdn_pre")(proj_m, proj_m, proj_s, conv_w, gparams)


def _gdn_pre_bwd(proj_m, proj_s, conv_w, gparams, dq, dk, dv, dgs, pad):
    Lp = proj_m.shape[0]
    tr = _tile(Lp, 192, 64)
    W3 = 3 * D_MODEL
    te = tr + 8

    def body(main_ref, prev_ref, next_ref, s_ref, w_ref, gp_ref,
             dq_ref, dqn_ref, dk_ref, dkn_ref, dv_ref, dvn_ref, dgs_ref,
             da_ref, ds_ref, dw_ref, dgp_ref):
        i = pl.program_id(0)
        w = w_ref[...]
        prev = jnp.where(i > 0, prev_ref[...], 0.0)
        ext = jnp.concatenate([prev, main_ref[...], next_ref[...]], axis=0)
        c = _gdn_conv(ext, w, te, 8 - (GDN_CONV - 1))
        sg = _sig(c)
        s = c * sg
        rowe = i * tr + lax.broadcasted_iota(jnp.int32, (te, 1), 0)
        live = (rowe >= pad) & (rowe < Lp)
        dqe = jnp.concatenate([dq_ref[...], dqn_ref[...]], axis=0)
        dke = jnp.concatenate([dk_ref[...], dkn_ref[...]], axis=0)
        dve = jnp.concatenate([dv_ref[...], dvn_ref[...]], axis=0)
        scale = GDN_D ** -0.5
        parts = []
        for j in range(2 * GDN_H):
            seg = s[:, j * GDN_D:(j + 1) * GDN_D]
            r = lax.rsqrt(_rowsum(seg * seg) + EPS)
            xh = seg * r
            if j < GDN_H:
                dxh = dqe[:, j * GDN_D:(j + 1) * GDN_D] * scale
            else:
                dxh = dke[:, (j - GDN_H) * GDN_D:(j - GDN_H + 1) * GDN_D]
            parts.append(r * (dxh - xh * _rowsum(dxh * xh)))
        parts.append(dve)
        dsv = jnp.concatenate(parts, axis=1)
        dc = jnp.where(live, dsv * (sg * (1.0 + c * (1.0 - sg))), 0.0)
        acc = w[GDN_CONV - 1:GDN_CONV, :] * dc[0:tr, :]
        for kk in range(GDN_CONV - 1):
            sh = GDN_CONV - 1 - kk
            acc = acc + w[kk:kk + 1, :] * dc[sh:sh + tr, :]
        da_ref[...] = acc.astype(BF16)
        dcm = dc[0:tr, :]
        rows = [jnp.sum(dcm * ext[8 - (GDN_CONV - 1) + kk:8 - (GDN_CONV - 1) + kk + tr, :], axis=0, keepdims=True)
                for kk in range(GDN_CONV)]
        dwp = jnp.concatenate(rows + [jnp.zeros((8 - GDN_CONV, W3), F32)], axis=0)

        sm = s_ref[...]
        gp = gp_ref[...]
        lane = lax.broadcasted_iota(jnp.int32, sm.shape, 1)
        rowm = i * tr + lax.broadcasted_iota(jnp.int32, (tr, 1), 0)
        dgv = jnp.where(rowm >= pad, dgs_ref[...], 0.0)
        dlg = jnp.where(lane < GDN_H, dgv, 0.0)
        dbt = jnp.where((lane >= GDN_H) & (lane < 2 * GDN_H), dgv, 0.0)
        z = sm + gp[1:2, :]
        softplus = jnp.maximum(z, 0.0) + jnp.log(1.0 + jnp.exp(-jnp.abs(z)))
        ea = jnp.exp(gp[0:1, :])
        dz = dlg * (-ea) * _sig(z)
        dal = dlg * (-ea) * softplus
        bt = _sig(sm)
        dgb = dbt * bt * (1.0 - bt)
        ds_ref[...] = (dz + dgb).astype(BF16)
        gpp = jnp.concatenate([jnp.sum(dal, axis=0, keepdims=True), jnp.sum(dz, axis=0, keepdims=True),
                               jnp.zeros((6, LANES), F32)], axis=0)

        @pl.when(i == 0)
        def _():
            dw_ref[...] = dwp
            dgp_ref[...] = gpp

        @pl.when(i > 0)
        def _():
            dw_ref[...] += dwp
            dgp_ref[...] += gpp

    m3 = pl.BlockSpec((tr, W3), lambda i: (i, 0))
    m1 = pl.BlockSpec((tr, D_MODEL), lambda i: (i, 0))
    n1 = _halo_next(tr, D_MODEL, Lp)
    return pl.pallas_call(
        body, grid=(Lp // tr,),
        in_specs=[m3, _halo_prev(tr, W3), _halo_next(tr, W3, Lp), pl.BlockSpec((tr, LANES), lambda i: (i, 0)),
                  pl.BlockSpec((GDN_CONV, W3), lambda i: (0, 0)), pl.BlockSpec((8, LANES), lambda i: (0, 0)),
                  m1, n1, m1, n1, m1, n1, pl.BlockSpec((tr, LANES), lambda i: (i, 0))],
        out_specs=[m3, pl.BlockSpec((tr, LANES), lambda i: (i, 0)),
                   pl.BlockSpec((8, W3), lambda i: (0, 0)), pl.BlockSpec((8, LANES), lambda i: (0, 0))],
        out_shape=[jax.ShapeDtypeStruct((Lp, W3), BF16), jax.ShapeDtypeStruct((Lp, LANES), BF16),
                   jax.ShapeDtypeStruct((8, W3), F32), jax.ShapeDtypeStruct((8, LANES), F32)],
        name="gdn_pre_bwd")(proj_m, proj_m, proj_m, proj_s, conv_w, gparams, dq, dq, dk, dk, dv, dv, dgs)


def _gdn_gates(gs):
    ri = lax.broadcasted_iota(jnp.int32, (CHUNK, CHUNK), 0)
    ci = lax.broadcasted_iota(jnp.int32, (CHUNK, CHUNK), 1)
    tril = ri >= ci
    strict = ri > ci
    gall = _dx(tril.astype(F32), gs)
    lane8 = lax.broadcasted_iota(jnp.int32, (8, LANES), 1)
    sub8 = lax.broadcasted_iota(jnp.int32, (8, LANES), 0)
    grow = _dxnt((lane8 == sub8).astype(F32), gall)
    return gall, grow, tril, strict


def _gdn_decay(gall, grow, tril, h):
    g = gall[:, h:h + 1]
    return g, jnp.where(tril, jnp.exp(jnp.where(tril, g - grow[h:h + 1, :], 0.0)), 0.0)


def _gdn_chunk_specs(N, rev):
    cn = (lambda n: N - 1 - n) if rev else (lambda n: n)
    col = lambda j: pl.BlockSpec((CHUNK, D_MODEL), lambda n: (cn(n), j))
    gate = pl.BlockSpec((CHUNK, LANES), lambda n: (cn(n), 0))
    st = lambda a, b: pl.BlockSpec((GDN_H, None, a, b), lambda n: (0, cn(n), 0, 0))
    return col, gate, st


def _gdn_chunk_fwd(qkv, gsm):
    Lp = qkv.shape[0]
    N = Lp // CHUNK

    def body(q_ref, k_ref, v_ref, gs_ref, o_ref, sin_ref, t_ref, S):
        n = pl.program_id(0)

        @pl.when(n == 0)
        def _():
            S[...] = jnp.zeros_like(S)

        gs = gs_ref[...]
        gall, grow, tril, strict = _gdn_gates(gs)
        ri = lax.broadcasted_iota(jnp.int32, (CHUNK, CHUNK), 0)
        ci = lax.broadcasted_iota(jnp.int32, (CHUNK, CHUNK), 1)
        eye = (ri == ci).astype(F32)
        heads = range(GDN_H)
        sls = [slice(h * GDN_D, (h + 1) * GDN_D) for h in heads]
        q = [q_ref[:, sl] for sl in sls]
        k = [k_ref[:, sl] for sl in sls]
        v = [v_ref[:, sl] for sl in sls]
        s0 = [S[h] for h in heads]
        beta = [gs[:, GDN_H + h:GDN_H + h + 1] for h in heads]
        gg = [_gdn_decay(gall, grow, tril, h) for h in heads]
        g = [x[0] for x in gg]
        gam = [x[1] for x in gg]
        eg = [jnp.exp(g[h]) for h in heads]
        gl = [g[h][CHUNK - 1:CHUNK, :] for h in heads]
        kb = [k[h] * beta[h] for h in heads]
        pw = [-jnp.where(strict, _dnt(kb[h], k[h]) * gam[h], 0.0) for h in heads]
        p = [_dnt(q[h], k[h]) * gam[h] for h in heads]
        qs = [_d(q[h] * eg[h], s0[h]) for h in heads]
        t = [eye + pw[h] for h in heads]
        for _ in range(5):
            pw = [_d3g(pw[h], pw[h], _NN) for h in heads]
            t = [t[h] + _d3g(t[h], pw[h], _NN) for h in heads]
        u = [_d(t[h], v[h] * beta[h]) for h in heads]
        w = [_d(t[h], kb[h] * eg[h]) for h in heads]
        vnew = [u[h] - _d(w[h], s0[h]) for h in heads]
        for h in heads:
            o_ref[:, sls[h]] = qs[h] + _d(p[h], vnew[h])
            sin_ref[h] = s0[h]
            t_ref[h] = t[h]
            S[h] = s0[h] * jnp.exp(gl[h]) + _dtn(k[h] * jnp.exp(gl[h] - g[h]), vnew[h])

    col, gate, st = _gdn_chunk_specs(N, False)
    return pl.pallas_call(
        body, grid=(N,),
        in_specs=[col(0), col(1), col(2), gate],
        out_specs=[col(0), st(GDN_D, GDN_D), st(CHUNK, CHUNK)],
        out_shape=[jax.ShapeDtypeStruct((Lp, D_MODEL), F32), jax.ShapeDtypeStruct((GDN_H, N, GDN_D, GDN_D), F32),
                   jax.ShapeDtypeStruct((GDN_H, N, CHUNK, CHUNK), F32)],
        scratch_shapes=[pltpu.VMEM((GDN_H, GDN_D, GDN_D), F32)],
        name="gdn_chunk_fwd")(qkv, qkv, qkv, gsm)


def _gdn_chunk_bwd(qkv, gsm, do, s_in, t_in):
    Lp = qkv.shape[0]
    N = Lp // CHUNK

    def body(q_ref, k_ref, v_ref, gs_ref, do_ref, sin_ref, t_ref, dq_ref, dk_ref, dv_ref, dgs_ref, dS):
        n = pl.program_id(0)

        @pl.when(n == 0)
        def _():
            dS[...] = jnp.zeros_like(dS)

        gs = gs_ref[...]
        gall, grow, tril, strict = _gdn_gates(gs)
        lane = lax.broadcasted_iota(jnp.int32, (CHUNK, LANES), 1)
        rcol = lax.broadcasted_iota(jnp.int32, (CHUNK, 1), 0)
        ones = jnp.ones((CHUNK, LANES), F32)
        dg_all = jnp.zeros((CHUNK, LANES), F32)
        dbeta_all = jnp.zeros((CHUNK, LANES), F32)
        heads = range(GDN_H)
        sls = [slice(h * GDN_D, (h + 1) * GDN_D) for h in heads]
        H = lambda f: [f(h) for h in heads]
        q = H(lambda h: q_ref[:, sls[h]])
        k = H(lambda h: k_ref[:, sls[h]])
        v = H(lambda h: v_ref[:, sls[h]])
        dov = H(lambda h: do_ref[:, sls[h]])
        s0 = H(lambda h: sin_ref[h])
        t = H(lambda h: t_ref[h])
        dsv = H(lambda h: dS[h])
        beta = H(lambda h: gs[:, GDN_H + h:GDN_H + h + 1])
        gg = H(lambda h: _gdn_decay(gall, grow, tril, h))
        g = [x[0] for x in gg]
        gam = [x[1] for x in gg]
        eg = H(lambda h: jnp.exp(g[h]))
        egl = H(lambda h: jnp.exp(g[h][CHUNK - 1:CHUNK, :]))
        e = H(lambda h: jnp.exp(g[h][CHUNK - 1:CHUNK, :] - g[h]))
        kb = H(lambda h: k[h] * beta[h])
        kbg = H(lambda h: kb[h] * eg[h])
        vb = H(lambda h: v[h] * beta[h])
        qg = H(lambda h: q[h] * eg[h])
        kd = H(lambda h: k[h] * e[h])
        m = H(lambda h: jnp.where(strict, _dnt(kb[h], k[h]) * gam[h], 0.0))
        u = H(lambda h: _d(t[h], vb[h]))
        w = H(lambda h: _d(t[h], kbg[h]))
        p = H(lambda h: _dnt(q[h], k[h]) * gam[h])
        dqg = H(lambda h: _dnt(dov[h], s0[h]))
        kdds = H(lambda h: _d(kd[h], dsv[h]))
        qgdo = H(lambda h: _dtn(qg[h], dov[h]))
        vnew = H(lambda h: u[h] - _d(w[h], s0[h]))
        dvnew = H(lambda h: _dtn(p[h], dov[h]) + kdds[h])
        dp = H(lambda h: jnp.where(tril, _dnt(dov[h], vnew[h]), 0.0))
        dkd = H(lambda h: _dnt(vnew[h], dsv[h]))
        dw = H(lambda h: -_dnt(dvnew[h], s0[h]))
        for h in heads:
            dS[h] = qgdo[h] + egl[h] * dsv[h] - _dtn(w[h], dvnew[h])
        dvb = H(lambda h: _dtn(t[h], dvnew[h]))
        dkbg = H(lambda h: _dtn(t[h], dw[h]))
        dt = H(lambda h: _dnt(dvnew[h], vb[h]) + _dnt(dw[h], kbg[h]))
        x1 = H(lambda h: _d3g(t[h], dt[h], _TN))
        dm = H(lambda h: jnp.where(strict, -_d3g(x1[h], t[h], _NT), 0.0))
        dkk = H(lambda h: dm[h] * gam[h])
        dqk = H(lambda h: dp[h] * gam[h])
        dkb = H(lambda h: _d(dkk[h], k[h]) + eg[h] * dkbg[h])
        em = H(lambda h: dm[h] * m[h] + dp[h] * p[h])
        colsum = H(lambda h: _d3g(em[h], ones, _TN)[:, 0:1])
        for h in heads:
            dk_ref[:, sls[h]] = _dtn(dkk[h], kb[h]) + _dtn(dqk[h], q[h]) + dkd[h] * e[h] + beta[h] * dkb[h]
            dq_ref[:, sls[h]] = _d(dqk[h], k[h]) + dqg[h] * eg[h]
            dv_ref[:, sls[h]] = beta[h] * dvb[h]
        for h in heads:
            dbeta = _rowsum(k[h] * dkb[h]) + _rowsum(v[h] * dvb[h])
            z = _rowsum(kd[h] * dkd[h])
            dg = _rowsum(em[h]) - colsum[h] + _rowsum(qg[h] * dqg[h]) + _rowsum(kbg[h] * dkbg[h]) - z
            extra = _allsum(z) + egl[h] * _allsum(s0[h] * dsv[h])
            dg = dg + jnp.where(rcol == CHUNK - 1, extra, 0.0)
            dg_all = dg_all + jnp.where(lane == h, dg, 0.0)
            dbeta_all = dbeta_all + jnp.where(lane == GDN_H + h, dbeta, 0.0)
        ri = lax.broadcasted_iota(jnp.int32, (CHUNK, CHUNK), 0)
        ci = lax.broadcasted_iota(jnp.int32, (CHUNK, CHUNK), 1)
        dgs_ref[...] = _dx((ci >= ri).astype(F32), dg_all) + dbeta_all

    col, gate, st = _gdn_chunk_specs(N, True)
    return pl.pallas_call(
        body, grid=(N,),
        in_specs=[col(0), col(1), col(2), gate, col(0), st(GDN_D, GDN_D), st(CHUNK, CHUNK)],
        out_specs=[col(0), col(0), col(0), gate],
        out_shape=[jax.ShapeDtypeStruct((Lp, D_MODEL), F32)] * 3 + [jax.ShapeDtypeStruct((Lp, LANES), F32)],
        scratch_shapes=[pltpu.VMEM((GDN_H, GDN_D, GDN_D), F32)],
        name="gdn_chunk_bwd")(qkv, qkv, qkv, gsm, do, s_in, t_in)


def _rot(x, c, s):
    half = RET_D // 2
    x1 = x[:, :half]
    x2 = x[:, half:]
    return jnp.concatenate([x1 * c - x2 * s, x2 * c + x1 * s], axis=1)


def _rot_bwd(d, c, s):
    half = RET_D // 2
    d1 = d[:, :half]
    d2 = d[:, half:]
    return jnp.concatenate([d1 * c + d2 * s, d2 * c - d1 * s], axis=1)


def _ret_tables():
    hh = jnp.arange(RET_H, dtype=F32)
    lg = jnp.log(1.0 - 2.0 ** (-5.0 - hh))
    idx = jnp.arange(CHUNK, dtype=F32)
    tril = jnp.asarray(np.tril(np.ones((CHUNK, CHUNK), dtype=bool)))
    dmask = jnp.where(tril, jnp.exp((idx[:, None] - idx[None, :]) * lg[:, None, None]), 0.0)
    qdec = jnp.exp((idx[None, :] + 1.0) * lg[:, None])
    kdec = jnp.exp((CHUNK - 1.0 - idx[None, :]) * lg[:, None])
    gch = jnp.exp(CHUNK * lg)
    qdec = jnp.broadcast_to(qdec[:, :, None], (RET_H, CHUNK, RET_D))
    kdec = jnp.broadcast_to(kdec[:, :, None], (RET_H, CHUNK, RET_D))
    gch = jnp.broadcast_to(gch[:, None, None], (RET_H, 8, LANES))
    return dmask, qdec, kdec, gch


def _ret_specs(N, rev):
    cn = (lambda n: N - 1 - n) if rev else (lambda n: n)
    col = lambda j: pl.BlockSpec((CHUNK, D_MODEL), lambda n: (cn(n), j))
    tab = lambda a, b: pl.BlockSpec((RET_H, a, b), lambda n: (0, 0, 0))
    rope = pl.BlockSpec((CHUNK, LANES), lambda n: (cn(n), 0))
    st = pl.BlockSpec((RET_H, None, RET_D, RET_D), lambda n: (0, cn(n), 0, 0))
    return col, tab, rope, st


def _ret_chunk_fwd(proj_m, cos, sin, tables):
    Lp = proj_m.shape[0]
    N = Lp // CHUNK
    dmask, qdec, kdec, gch = tables

    def body(q_ref, k_ref, v_ref, c_ref, s_ref, dm_ref, qd_ref, kd_ref, g_ref, o_ref, sin_ref, S):
        n = pl.program_id(0)

        @pl.when(n == 0)
        def _():
            S[...] = jnp.zeros_like(S)

        c = c_ref[...]
        s = s_ref[...]
        heads = range(RET_H)
        sls = [slice(h * RET_D, (h + 1) * RET_D) for h in heads]
        H = lambda f: [f(h) for h in heads]
        qr = H(lambda h: _rot(q_ref[:, sls[h]], c, s))
        ks = H(lambda h: _rot(k_ref[:, sls[h]], c, s) * (RET_D ** -0.5))
        v = H(lambda h: v_ref[:, sls[h]])
        s0 = H(lambda h: S[h])
        a = H(lambda h: _dnt(qr[h], ks[h]) * dm_ref[h])
        qs = H(lambda h: _d(qr[h] * qd_ref[h], s0[h]))
        kv = H(lambda h: _dtn(ks[h] * kd_ref[h], v[h]))
        for h in heads:
            o_ref[:, sls[h]] = _d(a[h], v[h]) + qs[h]
            sin_ref[h] = s0[h]
            S[h] = s0[h] * g_ref[h, 0:1, 0:1] + kv[h]

    col, tab, rope, st = _ret_specs(N, False)
    return pl.pallas_call(
        body, grid=(N,),
        in_specs=[col(3), col(4), col(5), rope, rope,
                  tab(CHUNK, CHUNK), tab(CHUNK, RET_D), tab(CHUNK, RET_D), tab(8, LANES)],
        out_specs=[col(0), st],
        out_shape=[jax.ShapeDtypeStruct((Lp, D_MODEL), F32), jax.ShapeDtypeStruct((RET_H, N, RET_D, RET_D), F32)],
        scratch_shapes=[pltpu.VMEM((RET_H, RET_D, RET_D), F32)],
        name="ret_chunk_fwd")(proj_m, proj_m, proj_m, cos, sin, dmask, qdec, kdec, gch)


def _ret_chunk_bwd(proj_m, cos, sin, tables, do, s_in):
    Lp = proj_m.shape[0]
    N = Lp // CHUNK
    dmask, qdec, kdec, gch = tables

    def body(q_ref, k_ref, v_ref, c_ref, s_ref, dm_ref, qd_ref, kd_ref, g_ref, do_ref, sin_ref,
             dq_ref, dk_ref, dv_ref, dS):
        n = pl.program_id(0)

        @pl.when(n == 0)
        def _():
            dS[...] = jnp.zeros_like(dS)

        c = c_ref[...]
        s = s_ref[...]
        kscale = RET_D ** -0.5
        heads = range(RET_H)
        sls = [slice(h * RET_D, (h + 1) * RET_D) for h in heads]
        H = lambda f: [f(h) for h in heads]
        qr = H(lambda h: _rot(q_ref[:, sls[h]], c, s))
        ks = H(lambda h: _rot(k_ref[:, sls[h]], c, s) * kscale)
        v = H(lambda h: v_ref[:, sls[h]])
        dov = H(lambda h: do_ref[:, sls[h]])
        s0 = H(lambda h: sin_ref[h])
        dsv = H(lambda h: dS[h])
        ad = H(lambda h: _dnt(qr[h], ks[h]) * dm_ref[h])
        da = H(lambda h: _dnt(dov[h], v[h]) * dm_ref[h])
        kds = H(lambda h: _d(ks[h] * kd_ref[h], dsv[h]))
        dos = H(lambda h: _dnt(dov[h], s0[h]) * qd_ref[h])
        vds = H(lambda h: _dnt(v[h], dsv[h]) * kd_ref[h])
        qdo = H(lambda h: _dtn(qr[h] * qd_ref[h], dov[h]))
        for h in heads:
            dS[h] = dsv[h] * g_ref[h, 0:1, 0:1] + qdo[h]
        for h in heads:
            dv_ref[:, sls[h]] = (_dtn(ad[h], dov[h]) + kds[h]).astype(BF16)
            dq_ref[:, sls[h]] = _rot_bwd(_d(da[h], ks[h]) + dos[h], c, s).astype(BF16)
            dk_ref[:, sls[h]] = _rot_bwd((_dtn(da[h], qr[h]) + vds[h]) * kscale, c, s).astype(BF16)

    col, tab, rope, st = _ret_specs(N, True)
    return pl.pallas_call(
        body, grid=(N,),
        in_specs=[col(3), col(4), col(5), rope, rope,
                  tab(CHUNK, CHUNK), tab(CHUNK, RET_D), tab(CHUNK, RET_D), tab(8, LANES), col(0), st],
        out_specs=[col(0), col(0), col(0)],
        out_shape=[jax.ShapeDtypeStruct((Lp, D_MODEL), BF16)] * 3,
        scratch_shapes=[pltpu.VMEM((RET_H, RET_D, RET_D), F32)],
        name="ret_chunk_bwd")(proj_m, proj_m, proj_m, cos, sin, dmask, qdec, kdec, gch, do, s_in)


def _merge_specs(tr):
    col = lambda j: pl.BlockSpec((tr, D_MODEL), lambda i: (i, j))
    return col


def _merge_fwd(o_a, o_b, proj_m, gnorm):
    Lp = o_a.shape[0]
    tr = _tile(Lp, 192, 16)

    def body(oa_ref, ob_ref, gz_ref, rg_ref, ga_ref, gb_ref, gn_ref, y_ref):
        gn = gn_ref[...]
        oa = oa_ref[...]
        ob = ob_ref[...]
        gz = gz_ref[...]
        ya = []
        for j in range(GDN_H):
            seg = oa[:, j * GDN_D:(j + 1) * GDN_D]
            r = lax.rsqrt(jnp.mean(seg * seg, axis=-1, keepdims=True) + EPS)
            ya.append(seg * r * gn)
        ya = jnp.concatenate(ya, axis=1) * (gz * _sig(gz))
        yb = []
        for j in range(RET_H):
            seg = ob[:, j * RET_D:(j + 1) * RET_D]
            r = lax.rsqrt(jnp.mean(seg * seg, axis=-1, keepdims=True) + EPS)
            yb.append(seg * r)
        rg = rg_ref[...]
        yb = jnp.concatenate(yb, axis=1) * (rg * _sig(rg))
        y_ref[...] = (_sig(ga_ref[...]) * ya + _sig(gb_ref[...]) * yb).astype(BF16)

    col = _merge_specs(tr)
    return pl.pallas_call(
        body, grid=(Lp // tr,),
        in_specs=[col(0), col(0), col(6), col(7), col(8), col(9), pl.BlockSpec((1, GDN_D), lambda i: (0, 0))],
        out_specs=col(0), out_shape=jax.ShapeDtypeStruct((Lp, D_MODEL), BF16),
        name="merge_fwd")(o_a, o_b, proj_m, proj_m, proj_m, proj_m, gnorm)


def _merge_bwd(dy, o_a, o_b, proj_m, gnorm):
    Lp = o_a.shape[0]
    tr = _tile(Lp, 192, 16)

    def body(dy_ref, oa_ref, ob_ref, gz_ref, rg_ref, ga_ref, gb_ref, gn_ref, dc_ref, doa_ref, dob_ref, dgn_ref):
        i = pl.program_id(0)
        gn = gn_ref[...]
        dyv = dy_ref[...]
        oa = oa_ref[...]
        ob = ob_ref[...]
        gz = gz_ref[...]
        rg = rg_ref[...]
        sa = _sig(ga_ref[...])
        sb = _sig(gb_ref[...])
        dya = dyv * sa
        dyb = dyv * sb
        sgz = _sig(gz)
        szz = gz * sgz
        dgn = jnp.zeros((1, GDN_D), F32)
        ya = []
        dgz = []
        for j in range(GDN_H):
            sl = slice(j * GDN_D, (j + 1) * GDN_D)
            seg = oa[:, sl]
            r = lax.rsqrt(jnp.mean(seg * seg, axis=-1, keepdims=True) + EPS)
            xh = seg * r
            oan = xh * gn
            ya.append(oan * szz[:, sl])
            dgz.append(dya[:, sl] * oan * (sgz[:, sl] * (1.0 + gz[:, sl] * (1.0 - sgz[:, sl]))))
            doan = dya[:, sl] * szz[:, sl]
            dgn = dgn + jnp.sum(doan * xh, axis=0, keepdims=True)
            dxh = doan * gn
            doa_ref[:, sl] = r * (dxh - xh * jnp.mean(dxh * xh, axis=-1, keepdims=True))
        ya = jnp.concatenate(ya, axis=1)
        srg = _sig(rg)
        srr = rg * srg
        yb = []
        drg = []
        for j in range(RET_H):
            sl = slice(j * RET_D, (j + 1) * RET_D)
            seg = ob[:, sl]
            r = lax.rsqrt(jnp.mean(seg * seg, axis=-1, keepdims=True) + EPS)
            xh = seg * r
            yb.append(xh * srr[:, sl])
            drg.append(dyb[:, sl] * xh * (srg[:, sl] * (1.0 + rg[:, sl] * (1.0 - srg[:, sl]))))
            dxh = dyb[:, sl] * srr[:, sl]
            dob_ref[:, sl] = r * (dxh - xh * jnp.mean(dxh * xh, axis=-1, keepdims=True))
        yb = jnp.concatenate(yb, axis=1)
        dc_ref[:, 0:D_MODEL] = jnp.concatenate(dgz, axis=1).astype(BF16)
        dc_ref[:, D_MODEL:2 * D_MODEL] = jnp.concatenate(drg, axis=1).astype(BF16)
        dc_ref[:, 2 * D_MODEL:3 * D_MODEL] = (dyv * ya * sa * (1.0 - sa)).astype(BF16)
        dc_ref[:, 3 * D_MODEL:] = (dyv * yb * sb * (1.0 - sb)).astype(BF16)

        @pl.when(i == 0)
        def _():
            dgn_ref[...] = dgn

        @pl.when(i > 0)
        def _():
            dgn_ref[...] += dgn

    col = _merge_specs(tr)
    return pl.pallas_call(
        body, grid=(Lp // tr,),
        in_specs=[col(0), col(0), col(0), col(6), col(7), col(8), col(9), pl.BlockSpec((1, GDN_D), lambda i: (0, 0))],
        out_specs=[pl.BlockSpec((tr, 4 * D_MODEL), lambda i: (i, 0)), col(0), col(0),
                   pl.BlockSpec((1, GDN_D), lambda i: (0, 0))],
        out_shape=[jax.ShapeDtypeStruct((Lp, 4 * D_MODEL), BF16), jax.ShapeDtypeStruct((Lp, D_MODEL), F32),
                   jax.ShapeDtypeStruct((Lp, D_MODEL), F32), jax.ShapeDtypeStruct((1, GDN_D), F32)],
        name="merge_bwd")(dy, o_a, o_b, proj_m, proj_m, proj_m, proj_m, gnorm)


def _ffn_conv(ext, w, b, tr, lo):
    acc = b + w[0:1, :] * ext[lo:lo + tr, :]
    for kk in range(1, FFN_CONV):
        acc = acc + w[kk:kk + 1, :] * ext[lo + kk:lo + kk + tr, :]
    return acc


def _ffn_act(up, conv_w, conv_b):
    Lp = up.shape[0]
    tr = _tile(Lp, 192, 16)
    W2 = 2 * D_FF

    def body(main_ref, prev_ref, w_ref, b_ref, act_ref):
        i = pl.program_id(0)
        prev = jnp.where(i > 0, prev_ref[...], 0.0)
        ext = jnp.concatenate([prev, main_ref[...]], axis=0)
        u = _ffn_conv(ext, w_ref[...], b_ref[...], tr, 8 - (FFN_CONV - 1))
        a = u[:, :D_FF]
        act_ref[...] = (a * _sig(a) * u[:, D_FF:]).astype(BF16)

    return pl.pallas_call(
        body, grid=(Lp // tr,),
        in_specs=[pl.BlockSpec((tr, W2), lambda i: (i, 0)), _halo_prev(tr, W2),
                  pl.BlockSpec((FFN_CONV, W2), lambda i: (0, 0)), pl.BlockSpec((1, W2), lambda i: (0, 0))],
        out_specs=pl.BlockSpec((tr, D_FF), lambda i: (i, 0)),
        out_shape=jax.ShapeDtypeStruct((Lp, D_FF), BF16), name="ffn_act")(up, up, conv_w, conv_b)


def _ffn_act_bwd(up, dact, conv_w, conv_b):
    Lp = up.shape[0]
    tr = _tile(Lp, 96, 16)
    W2 = 2 * D_FF
    te = tr + 8

    def body(main_ref, prev_ref, next_ref, da_ref, dan_ref, w_ref, b_ref, dup_ref, acc_ref):
        i = pl.program_id(0)
        w = w_ref[...]
        prev = jnp.where(i > 0, prev_ref[...], 0.0)
        ext = jnp.concatenate([prev, main_ref[...], next_ref[...]], axis=0)
        u = _ffn_conv(ext, w, b_ref[...], te, 8 - (FFN_CONV - 1))
        a = u[:, :D_FF]
        b = u[:, D_FF:]
        rowe = i * tr + lax.broadcasted_iota(jnp.int32, (te, 1), 0)
        dae = jnp.where(rowe < Lp, jnp.concatenate([da_ref[...], dan_ref[...]], axis=0), 0.0)
        sg = _sig(a)
        du = jnp.concatenate([dae * b * (sg * (1.0 + a * (1.0 - sg))), dae * (a * sg)], axis=1)
        acc = w[FFN_CONV - 1:FFN_CONV, :] * du[0:tr, :]
        for kk in range(FFN_CONV - 1):
            sh = FFN_CONV - 1 - kk
            acc = acc + w[kk:kk + 1, :] * du[sh:sh + tr, :]
        dup_ref[...] = acc.astype(BF16)
        dum = du[0:tr, :]
        lo = 8 - (FFN_CONV - 1)
        rows = [jnp.sum(dum * ext[lo + kk:lo + kk + tr, :], axis=0, keepdims=True) for kk in range(FFN_CONV)]
        rows.append(jnp.sum(dum, axis=0, keepdims=True))
        part = jnp.concatenate(rows + [jnp.zeros((8 - len(rows), W2), F32)], axis=0)

        @pl.when(i == 0)
        def _():
            acc_ref[...] = part

        @pl.when(i > 0)
        def _():
            acc_ref[...] += part

    return pl.pallas_call(
        body, grid=(Lp // tr,),
        in_specs=[pl.BlockSpec((tr, W2), lambda i: (i, 0)), _halo_prev(tr, W2), _halo_next(tr, W2, Lp),
                  pl.BlockSpec((tr, D_FF), lambda i: (i, 0)), _halo_next(tr, D_FF, Lp),
                  pl.BlockSpec((FFN_CONV, W2), lambda i: (0, 0)), pl.BlockSpec((1, W2), lambda i: (0, 0))],
        out_specs=[pl.BlockSpec((tr, W2), lambda i: (i, 0)), pl.BlockSpec((8, W2), lambda i: (0, 0))],
        out_shape=[jax.ShapeDtypeStruct((Lp, W2), BF16), jax.ShapeDtypeStruct((8, W2), F32)],
        name="ffn_act_bwd")(up, up, up, dact, dact, conv_w, conv_b)


def _local_step(hpad, tgt, pad, wt, late_weights=None, on_ffn_out_grads=None, on_w_in_grads=None):
    Lp = hpad.shape[0]
    first = pad + N_META
    pos = jnp.arange(Lp, dtype=F32) - float(pad)
    half = RET_D // 2
    inv = 1.0 / (ROPE_BASE ** (jnp.arange(half, dtype=F32) / half))
    ang = pos[:, None] * inv[None, :]
    cos, sin = jnp.cos(ang), jnp.sin(ang)
    tables = _ret_tables()
    gparams = jnp.zeros((8, LANES), F32).at[0, :GDN_H].set(wt["a_log"]).at[1, :GDN_H].set(wt["dt_bias"])

    hn1 = _rms_fwd(hpad, wt["norm1"], "rms1_fwd")
    proj_m = _mm_nn(hn1, wt["w_main"], name="proj_main")
    proj_s = _mm_nn(hn1, wt["w_small"], name="proj_small")
    qkv, gsm = _gdn_pre(proj_m, proj_s, wt["gdn_conv_w"], gparams, pad)
    o_a, s_a, t_a = _gdn_chunk_fwd(qkv, gsm)
    o_b, s_b = _ret_chunk_fwd(proj_m, cos, sin, tables)
    y = _merge_fwd(o_a, o_b, proj_m, wt["gdn_norm"])
    if late_weights is not None:
        wt = {**wt, **late_weights(y)}
    h1 = _mm_nn(y, wt["w_out"], res=hpad, name="out_proj")
    hn2 = _rms_fwd(h1, wt["norm2"], "rms2_fwd")
    up = _mm_nn(hn2, wt["w_up"], name="ffn_up")
    act = _ffn_act(up, wt["ffn_conv_w"], wt["ffn_conv_b"])
    h2 = _mm_nn(act, wt["w_down"], res=h1, name="ffn_down")
    lossvec, dh2, dh2b, d_norm_f = _final(h2, wt["norm_f"], tgt, first)

    d_w_down = _mm_tn(act, dh2b, name="dw_down")
    dact = _mm_nt(dh2b, wt["w_down"], name="d_act")
    dup, ffn_rows = _ffn_act_bwd(up, dact, wt["ffn_conv_w"], wt["ffn_conv_b"])
    d_w_up = _mm_tn(hn2, dup, name="dw_up")
    dhn2 = _mm_nt(dup, wt["w_up"], name="d_hn2")
    dh1, dh1b, d_norm2 = _rms_bwd(h1, wt["norm2"], dhn2, dh2, pad, "rms2_bwd")

    d_w_out = _mm_tn(y, dh1b, name="dw_out")
    dy = _mm_nt(dh1b, wt["w_out"], name="d_y")
    gnorm = wt["gdn_norm"]
    if on_ffn_out_grads is not None:
        gnorm = gnorm + on_ffn_out_grads(d_w_down, d_w_up, d_w_out)[0:1, :]
    d_c, do_a, do_b, d_gnorm = _merge_bwd(dy, o_a, o_b, proj_m, gnorm)
    drq, drk, drv = _ret_chunk_bwd(proj_m, cos, sin, tables, do_b, s_b)
    dq, dk, dv, dgs = _gdn_chunk_bwd(qkv, gsm, do_a, s_a, t_a)
    d_a, d_s, conv_rows, gp_rows = _gdn_pre_bwd(proj_m, proj_s, wt["gdn_conv_w"], gparams, dq, dk, dv, dgs, pad)

    wm = wt["w_main"]
    segs = [(d_a, 0, 3 * D_MODEL), (drq, 3 * D_MODEL, D_MODEL), (drk, 4 * D_MODEL, D_MODEL),
            (drv, 5 * D_MODEL, D_MODEL), (d_c, 6 * D_MODEL, 4 * D_MODEL)]
    d_w_main = jnp.concatenate([_mm_tn(hn1, d, name="dw_in_%d" % i) for i, (d, _, _) in enumerate(segs)], axis=1)
    d_w_small = _mm_tn(hn1, d_s, name="dw_in_small")
    w_small = wt["w_small"]
    if on_w_in_grads is not None:
        w_small = w_small + on_w_in_grads(d_w_main, d_w_small)[0:1, :].astype(w_small.dtype)
    dhn1 = _mm_nt(d_s, w_small, name="d_hn1_small")
    for i, (d, off, width) in enumerate(segs):
        dhn1 = _mm_nt(d, wm[:, off:off + width], res=dhn1, name="d_hn1_%d" % i)
    dh0, _, d_norm1 = _rms_bwd(hpad, wt["norm1"], dhn1, dh1, pad, "rms1_bwd")

    grads = {
        "norm1": d_norm1, "w_main": d_w_main, "w_small": d_w_small, "gdn_conv_w": conv_rows[:GDN_CONV],
        "a_log": gp_rows[0, :GDN_H], "dt_bias": gp_rows[1, :GDN_H], "gdn_norm": d_gnorm, "w_out": d_w_out,
        "norm2": d_norm2, "w_up": d_w_up, "ffn_conv_w": ffn_rows[:FFN_CONV], "ffn_conv_b": ffn_rows[FFN_CONV:FFN_CONV + 1],
        "w_down": d_w_down, "norm_f": d_norm_f,
    }
    return lossvec, dh0, grads


def _peer(k):
    ix, iy, ic = lax.axis_index("x"), lax.axis_index("y"), lax.axis_index("c")
    px = 1 - ix if (k >> 2) & 1 else ix
    py = 1 - iy if (k >> 1) & 1 else iy
    pc = 1 - ic if k & 1 else ic
    return (px, py, pc), 4 * px + 2 * py + pc


def _comm_call(body, n, out_shapes, name, args):
    hbm = pl.BlockSpec(memory_space=pl.ANY)
    return pl.pallas_call(
        body, out_shape=out_shapes, in_specs=[hbm] * n, out_specs=[hbm] * n,
        scratch_shapes=[pltpu.SemaphoreType.DMA((n, N_DEV - 1)), pltpu.SemaphoreType.DMA((n, N_DEV - 1)),
                        pltpu.SemaphoreType.DMA((n,))],
        name=name)(*args)


def _all_gather(xs, name):
    n = len(xs)

    def body(*refs):
        x_refs, out_refs = refs[:n], refs[n:2 * n]
        send_sems, recv_sems, local_sems = refs[2 * n:]
        _, me = _peer(0)
        pending = []
        for i in range(n):
            local = pltpu.make_async_copy(x_refs[i], out_refs[i].at[me], local_sems.at[i])
            local.start()
            pending.append(local)
        sends = []
        for i in range(n):
            for k in range(1, N_DEV):
                dev, _ = _peer(k)
                cp = pltpu.make_async_remote_copy(
                    src_ref=x_refs[i], dst_ref=out_refs[i].at[me], send_sem=send_sems.at[i, k - 1],
                    recv_sem=recv_sems.at[i, k - 1], device_id=dev, device_id_type=MESH_T)
                cp.start()
                sends.append(cp)
        for i in range(n):
            for k in range(1, N_DEV):
                dev, idx = _peer(k)
                pltpu.make_async_remote_copy(
                    src_ref=x_refs[i], dst_ref=out_refs[i].at[idx], send_sem=send_sems.at[i, k - 1],
                    recv_sem=recv_sems.at[i, k - 1], device_id=dev, device_id_type=MESH_T).wait_recv()
        for cp in sends:
            cp.wait_send()
        for local in pending:
            local.wait()

    out_shapes = [jax.ShapeDtypeStruct((N_DEV,) + a.shape, a.dtype) for a in xs]
    return _comm_call(body, n, out_shapes, name, xs)


def _all_to_all(gs, name):
    n = len(gs)

    def body(*refs):
        g_refs, out_refs = refs[:n], refs[n:2 * n]
        send_sems, recv_sems, local_sems = refs[2 * n:]
        _, me = _peer(0)
        pending = []
        for i in range(n):
            local = pltpu.make_async_copy(g_refs[i].at[me], out_refs[i].at[0], local_sems.at[i])
            local.start()
            pending.append(local)
        sends = []
        for i in range(n):
            for k in range(1, N_DEV):
                dev, idx = _peer(k)
                cp = pltpu.make_async_remote_copy(
                    src_ref=g_refs[i].at[idx], dst_ref=out_refs[i].at[k], send_sem=send_sems.at[i, k - 1],
                    recv_sem=recv_sems.at[i, k - 1], device_id=dev, device_id_type=MESH_T)
                cp.start()
                sends.append(cp)
        for cp in sends:
            cp.wait_recv()
        for cp in sends:
            cp.wait_send()
        for local in pending:
            local.wait()

    out_shapes = [jax.ShapeDtypeStruct(g.shape, g.dtype) for g in gs]
    return _comm_call(body, n, out_shapes, name, gs)


def _split_copies(kind, src_refs, land_refs, send_sems, recv_sems, local_sems):
    n = len(src_refs)
    _, me = _peer(0)
    locals_, remotes = [], []
    for i in range(n):
        if kind == "gather":
            locals_.append(pltpu.make_async_copy(src_refs[i], land_refs[i].at[me], local_sems.at[i]))
        else:
            locals_.append(pltpu.make_async_copy(src_refs[i].at[me], land_refs[i].at[0], local_sems.at[i]))
        for k in range(1, N_DEV):
            dev, idx = _peer(k)
            if kind == "gather":
                src, dst, mine = src_refs[i], land_refs[i].at[me], land_refs[i].at[idx]
            else:
                src, dst, mine = src_refs[i].at[idx], land_refs[i].at[k], land_refs[i].at[k]
            j = i * (N_DEV - 1) + k - 1
            send = pltpu.make_async_remote_copy(
                src_ref=src, dst_ref=dst, send_sem=send_sems.at[j], recv_sem=recv_sems.at[j],
                device_id=dev, device_id_type=MESH_T)
            recv = pltpu.make_async_remote_copy(
                src_ref=src, dst_ref=mine, send_sem=send_sems.at[j], recv_sem=recv_sems.at[j],
                device_id=dev, device_id_type=MESH_T)
            remotes.append((send, recv))
    return locals_, remotes


_HBM = pl.BlockSpec(memory_space=pltpu.HBM)
_SEM = pl.BlockSpec(memory_space=pltpu.SEMAPHORE)
_ANY = pl.BlockSpec(memory_space=pl.ANY)


def _split_start(srcs, kind, name, after):
    n = len(srcs)
    lands = [lax.empty(((N_DEV,) + a.shape) if kind == "gather" else a.shape, a.dtype) for a in srcs]

    def body(*refs):
        src_refs, land_refs = refs[:n], refs[n:2 * n]
        send_sems, recv_sems, local_sems = refs[2 * n + 1:2 * n + 4]
        token = refs[-1]
        locals_, remotes = _split_copies(kind, src_refs, land_refs, send_sems, recv_sems, local_sems)
        for cp in locals_:
            cp.start()
        for send, _ in remotes:
            send.start()
        token[...] = jnp.zeros_like(token)

    sems = (pltpu.SemaphoreType.DMA((n * (N_DEV - 1),)), pltpu.SemaphoreType.DMA((n * (N_DEV - 1),)),
            pltpu.SemaphoreType.DMA((n,)))
    thru = tuple(pltpu.HBM(a.shape, a.dtype) for a in list(srcs) + lands)
    outs = pl.pallas_call(
        body, name=name,
        out_shape=sems + thru + (jax.ShapeDtypeStruct((8, LANES), F32),),
        in_specs=[_HBM] * (2 * n) + [_ANY],
        out_specs=[_SEM] * 3 + [_HBM] * (2 * n) + [pl.BlockSpec(memory_space=pltpu.VMEM)],
        input_output_aliases={i: 3 + i for i in range(2 * n)},
        compiler_params=pltpu.CompilerParams(has_side_effects=pltpu.SideEffectType.DATAFLOW_SIDE_EFFECTING),
    )(*[pltpu.with_memory_space_constraint(a, pltpu.HBM) for a in list(srcs) + lands], after)
    return (kind, n, outs[:3], outs[3:3 + 2 * n]), outs[-1]


def _split_wait(handle, name, after):
    kind, n, sems, thru = handle

    def body(*refs):
        src_refs, land_refs = refs[:n], refs[n:2 * n]
        send_sems, recv_sems, local_sems = refs[2 * n:2 * n + 3]
        locals_, remotes = _split_copies(kind, src_refs, land_refs, send_sems, recv_sems, local_sems)
        for send, recv in remotes:
            send.wait_send()
            recv.wait_recv()
        for cp in locals_:
            cp.wait()

    outs = pl.pallas_call(
        body, name=name, out_shape=tuple(pltpu.HBM(a.shape, a.dtype) for a in thru),
        in_specs=[_HBM] * (2 * n) + [_SEM] * 3 + [_ANY], out_specs=[_HBM] * (2 * n),
        input_output_aliases={i: i for i in range(2 * n)},
        compiler_params=pltpu.CompilerParams(has_side_effects=pltpu.SideEffectType.DATAFLOW_SIDE_EFFECTING),
    )(*thru, *sems, after)
    return list(outs[n:])


def _adamw(gslabs, w, m, v, name):
    R, Cw = w.shape
    tr = _tile(R, 64 if Cw > 1024 else 128, 8)
    c1 = 1.0 - ADAM_B1 ** ADAM_STEP
    c2 = 1.0 - ADAM_B2 ** ADAM_STEP

    def body(g_ref, w_ref, m_ref, v_ref, go_ref, d_ref, mo_ref, vo_ref):
        g = g_ref[0]
        for k in range(1, N_DEV):
            g = g + g_ref[k]
        mn = ADAM_B1 * m_ref[...] + (1.0 - ADAM_B1) * g
        vn = ADAM_B2 * v_ref[...] + (1.0 - ADAM_B2) * (g * g)
        m_hat = mn / c1
        v_hat = vn / c2
        go_ref[...] = g
        d_ref[...] = -ADAM_LR * (m_hat / (jnp.sqrt(v_hat) + ADAM_EPS) + ADAM_WD * w_ref[...])
        mo_ref[...] = mn
        vo_ref[...] = vn

    blk = pl.BlockSpec((tr, Cw), lambda i: (i, 0))
    return pl.pallas_call(
        body, grid=(R // tr,),
        in_specs=[pl.BlockSpec((N_DEV, tr, Cw), lambda i: (0, i, 0)), blk, blk, blk],
        out_specs=[blk] * 4, out_shape=[jax.ShapeDtypeStruct((R, Cw), F32)] * 4, name=name)(gslabs, w, m, v)


def _pack(arrs, row_mult, dtype=F32):
    parts = []
    total = 0
    for a in arrs:
        f = a.reshape(-1).astype(dtype)
        n = -(-f.shape[0] // 1024) * 1024
        parts.append(jnp.pad(f, (0, n - f.shape[0])))
        total += n
    rows = total // LANES
    rows_p = -(-rows // row_mult) * row_mult
    flat = jnp.concatenate(parts)
    flat = jnp.pad(flat, (0, rows_p * LANES - total))
    return flat.reshape(rows_p, LANES)


def _unpack(packed, shapes):
    lead = packed.shape[:-2]
    flat = packed.reshape(lead + (-1,))
    out = []
    off = 0
    for s in shapes:
        n = int(np.prod(s))
        out.append(flat[..., off:off + n].reshape(lead + tuple(s)))
        off += -(-n // 1024) * 1024
    return out


def _gather_cols(stacked):
    d, r, c = stacked.shape
    return stacked.transpose(1, 0, 2).reshape(r, d * c)


def _scatter_cols(full):
    r, n = full.shape
    return full.reshape(r, N_DEV, n // N_DEV).transpose(1, 0, 2)


def kernel(x, meta, norm1, w_in, gdn_conv_w, gdn_a_log, gdn_dt_bias, gdn_norm, w_out, norm2, w_ffn_up, ffn_conv_w, ffn_conv_b, w_ffn_down, norm_f, loss_target, m_meta, m_norm1, m_w_in, m_gdn_conv_w, m_gdn_a_log, m_gdn_dt_bias, m_gdn_norm, m_w_out, m_norm2, m_w_ffn_up, m_ffn_conv_w, m_ffn_conv_b, m_w_ffn_down, m_norm_f, v_meta, v_norm1, v_w_in, v_gdn_conv_w, v_gdn_a_log, v_gdn_dt_bias, v_gdn_norm, v_w_out, v_norm2, v_w_ffn_up, v_ffn_conv_w, v_ffn_conv_b, v_w_ffn_down, v_norm_f):
    S = x.shape[1]
    L = N_META + S
    pad = (-L) % CHUNK
    Lp = L + pad

    big = [w_in[0], w_out[0], w_ffn_up[0], w_ffn_down[0]]
    small = [meta, gdn_conv_w, ffn_conv_w]
    w_in_s, small_all = _all_gather([big[0].astype(BF16), _pack(small, 8)], "gather_w_in")
    late, late_token = _split_start([a.astype(BF16) for a in big[1:]], "gather", "gather_late_start", small_all)

    def late_weights(after):
        w_out_s, w_up_s, w_down_s = _split_wait(late, "gather_late_wait", after)
        return {"w_out": w_out_s.reshape(D_MODEL, D_MODEL), "w_up": _gather_cols(w_up_s),
                "w_down": w_down_s.reshape(D_FF, D_MODEL)}

    meta_s, gconv_s, fconv_s = _unpack(small_all, [a.shape for a in small])
    w_in_f = _gather_cols(w_in_s)
    w_main = jnp.concatenate([w_in_f[:, _O_GQ:_O_GZ], w_in_f[:, _O_RQ:_O_RG], w_in_f[:, _O_GZ:_O_GA],
                              w_in_f[:, _O_RG:_O_END]], axis=1)
    w_small = jnp.pad(w_in_f[:, _O_GA:_O_RQ], ((0, 0), (0, LANES - 2 * GDN_H)))
    wt = {
        "norm1": norm1 + jnp.tile(late_token[0:1, :], (1, D_MODEL // LANES)),
        "w_main": w_main, "w_small": w_small,
        "gdn_conv_w": _gather_cols(gconv_s[:, 0]), "a_log": gdn_a_log[0], "dt_bias": gdn_dt_bias[0],
        "gdn_norm": gdn_norm, "norm2": norm2, "ffn_conv_w": _gather_cols(fconv_s[:, 0]), "ffn_conv_b": ffn_conv_b,
        "norm_f": norm_f.reshape(1, D_MODEL),
    }
    meta_f = _gather_cols(meta_s)

    pending = {}

    def on_ffn_out_grads(d_w_down, d_w_up, d_w_out):
        srcs = [d_w_out.reshape(N_DEV, D_MODEL // N_DEV, D_MODEL), _scatter_cols(d_w_up),
                d_w_down.reshape(N_DEV, D_FF // N_DEV, D_MODEL)]
        pending["ffn_out"], token = _split_start(srcs, "a2a", "exchange_ffn_out_start", d_w_out)
        return token

    def on_w_in_grads(gm, gs):
        d_w_in = jnp.concatenate([gm[:, 0:3072], gm[:, 6144:7168], gs[:, :2 * GDN_H], gm[:, 3072:6144],
                                  gm[:, 7168:]], axis=1)
        pending["w_in"], token = _split_start([_scatter_cols(d_w_in)], "a2a", "exchange_w_in_start", gs)
        return token

    hpad = jnp.concatenate([jnp.zeros((pad, D_MODEL), F32), meta_f, x[0]], axis=0)
    tgt = jnp.concatenate([jnp.zeros((pad + N_META, D_MODEL), F32), loss_target[0]], axis=0)
    lossvec, dh0, gr = _local_step(hpad, tgt, pad, wt, late_weights, on_ffn_out_grads, on_w_in_grads)

    loss = lax.psum(jnp.sum(lossvec), ("x", "y", "c"))
    grad_x = dh0[pad + N_META:][None]

    big_m = [m_w_in[0], m_w_out[0], m_w_ffn_up[0], m_w_ffn_down[0]]
    big_v = [v_w_in[0], v_w_out[0], v_w_ffn_up[0], v_w_ffn_down[0]]
    slabs_ffn_out = _split_wait(pending["ffn_out"], "exchange_ffn_out_wait", dh0)
    big_out = [None] + [_adamw(slabs_ffn_out[i - 1], big[i], big_m[i], big_v[i], "adamw_big_%d" % i)
                        for i in range(1, len(big))]
    g_sm = [_scatter_cols(dh0[pad:pad + N_META]), _scatter_cols(gr["gdn_conv_w"]), _scatter_cols(gr["ffn_conv_w"])]
    g_small = jnp.stack([_pack([g[d] for g in g_sm], 8) for d in range(N_DEV)])
    slabs_small, = _all_to_all([g_small], "exchange_small_gradients")
    small_out = _adamw(slabs_small, _pack(small, 8), _pack([m_meta, m_gdn_conv_w, m_ffn_conv_w], 8),
                       _pack([v_meta, v_gdn_conv_w, v_ffn_conv_w], 8), "adamw_small_sharded")
    small_un = [_unpack(o, [a.shape for a in small]) for o in small_out]
    rep_w = [norm1, gdn_a_log, gdn_dt_bias, gdn_norm, norm2, ffn_conv_b, norm_f]
    rep_m = [m_norm1, m_gdn_a_log, m_gdn_dt_bias, m_gdn_norm, m_norm2, m_ffn_conv_b, m_norm_f]
    rep_v = [v_norm1, v_gdn_a_log, v_gdn_dt_bias, v_gdn_norm, v_norm2, v_ffn_conv_b, v_norm_f]
    rep_g = [gr["norm1"], gr["a_log"], gr["dt_bias"], gr["gdn_norm"], gr["norm2"], gr["ffn_conv_b"], gr["norm_f"]]
    rep_slabs, = _all_gather([_pack(rep_g, 8)], "gather_small_gradients")
    rep_out = _adamw(rep_slabs, _pack(rep_w, 8), _pack(rep_m, 8), _pack(rep_v, 8), "adamw_replicated")
    rep_shapes = [a.shape for a in rep_w]
    rp_g, rp_d, rp_nm, rp_nv = [_unpack(o, rep_shapes) for o in rep_out]

    slabs_w_in, = _split_wait(pending["w_in"], "exchange_w_in_wait", rep_out[0])
    big_out[0] = _adamw(slabs_w_in, big[0], big_m[0], big_v[0], "adamw_big_0")
    sh_g, sh_d, sh_nm, sh_nv = [
        [small_un[j][0], big_out[0][j][None], small_un[j][1], big_out[1][j][None], big_out[2][j][None],
         small_un[j][2], big_out[3][j][None]] for j in range(4)]

    def order(sh, rp):
        return [sh[0], rp[0], sh[1], sh[2], rp[1], rp[2], rp[3], sh[3], rp[4], sh[4], sh[5], rp[5], sh[6], rp[6]]

    return (loss, grad_x, *order(sh_g, rp_g), *order(sh_d, rp_d), *order(sh_nm, rp_nm), *order(sh_nv, rp_nv))
```

```python
import functools
import math

import numpy as np
import jax
import jax.numpy as jnp
from jax import lax
from jax.experimental import pallas as pl
from jax.experimental.pallas import tpu as pltpu

F32 = jnp.float32
BF16 = jnp.bfloat16
HI = lax.Precision.HIGHEST

D_MODEL = 1024
N_META = 16
CHUNK = 64
GDN_H = 8
GDN_D = 128
RET_H = 4
RET_D = 256
D_FF = 2816
GDN_CONV = 4
FFN_CONV = 3
ROPE_BASE = 10000.0
EPS = 1e-6
N_DEV = 8
LANES = 128
MAIN_W = 10 * 1024
_O_GQ, _O_GZ, _O_GA, _O_RQ, _O_RG, _O_GATE, _O_END = 0, 3072, 4096, 4112, 7184, 8208, 10256

ADAM_LR = 0.001
ADAM_B1 = 0.9
ADAM_B2 = 0.999
ADAM_EPS = 1e-08
ADAM_WD = 0.01
ADAM_STEP = 10

MESH_T = pl.DeviceIdType.MESH


def _tile(n, target, mult):
    best = None
    for d in range(mult, min(n, target) + 1, mult):
        if n % d == 0:
            best = d
    assert best is not None, (n, target, mult)
    return best


def _sig(x):
    return 1.0 / (1.0 + jnp.exp(-x))


def _d(a, b):
    return jnp.dot(a.astype(BF16), b.astype(BF16), preferred_element_type=F32)


def _dnt(a, b):
    return lax.dot_general(a.astype(BF16), b.astype(BF16), (((1,), (1,)), ((), ())), preferred_element_type=F32)


def _dtn(a, b):
    return lax.dot_general(a.astype(BF16), b.astype(BF16), (((0,), (0,)), ((), ())), preferred_element_type=F32)


def _dx(a, b):
    return jnp.dot(a, b, preferred_element_type=F32, precision=HI)


def _dxnt(a, b):
    return lax.dot_general(a, b, (((1,), (1,)), ((), ())), preferred_element_type=F32, precision=HI)


def _dxtn(a, b):
    return lax.dot_general(a, b, (((0,), (0,)), ((), ())), preferred_element_type=F32, precision=HI)


def _split(a):
    hi = a.astype(BF16)
    return hi, (a - hi.astype(F32)).astype(BF16)


def _d3g(a, b, dims):
    ah, al = _split(a)
    bh, bl = _split(b)
    f = functools.partial(lax.dot_general, dimension_numbers=dims, preferred_element_type=F32)
    return f(ah, bh) + (f(ah, bl) + f(al, bh))


_NN = (((1,), (0,)), ((), ()))
_NT = (((1,), (1,)), ((), ()))
_TN = (((0,), (0,)), ((), ()))


def _rowsum(x):
    return jnp.sum(x, axis=1, keepdims=True)


def _allsum(x):
    return jnp.sum(jnp.sum(x, axis=1, keepdims=True), axis=0, keepdims=True)


def _mm_nn(a, b, res=None, out_dtype=F32, name="mm_nn"):
    M, K = a.shape
    N = b.shape[1]
    tm = _tile(M, 704, 16)
    tn = _tile(N, 2816, 128)

    def body(*refs):
        if res is None:
            a_ref, b_ref, o_ref = refs
        else:
            a_ref, b_ref, r_ref, o_ref = refs
        acc = jnp.dot(a_ref[...], b_ref[...], preferred_element_type=F32)
        if res is not None:
            acc = acc + r_ref[...]
        o_ref[...] = acc.astype(out_dtype)

    in_specs = [pl.BlockSpec((tm, K), lambda j, i: (i, 0)), pl.BlockSpec((K, tn), lambda j, i: (0, j))]
    args = [a, b]
    if res is not None:
        in_specs.append(pl.BlockSpec((tm, tn), lambda j, i: (i, j)))
        args.append(res)
    return pl.pallas_call(
        body, grid=(N // tn, M // tm), in_specs=in_specs,
        out_specs=pl.BlockSpec((tm, tn), lambda j, i: (i, j)),
        out_shape=jax.ShapeDtypeStruct((M, N), out_dtype), name=name)(*args)


def _mm_nt(a, b, res=None, name="mm_nt"):
    M, Nc = a.shape
    K = b.shape[0]
    tm = _tile(M, 704, 16)
    tc = _tile(Nc, 2048, 128)

    def body(*refs):
        if res is None:
            a_ref, b_ref, o_ref = refs
        else:
            a_ref, b_ref, r_ref, o_ref = refs
        c = pl.program_id(1)
        p = lax.dot_general(a_ref[...], b_ref[...], (((1,), (1,)), ((), ())), preferred_element_type=F32)

        @pl.when(c == 0)
        def _():
            if res is None:
                o_ref[...] = p
            else:
                o_ref[...] = p + r_ref[...]

        @pl.when(c > 0)
        def _():
            o_ref[...] += p

    in_specs = [pl.BlockSpec((tm, tc), lambda i, c: (i, c)), pl.BlockSpec((K, tc), lambda i, c: (0, c))]
    args = [a, b]
    if res is not None:
        in_specs.append(pl.BlockSpec((tm, K), lambda i, c: (i, 0)))
        args.append(res)
    return pl.pallas_call(
        body, grid=(M // tm, Nc // tc), in_specs=in_specs,
        out_specs=pl.BlockSpec((tm, K), lambda i, c: (i, 0)),
        out_shape=jax.ShapeDtypeStruct((M, K), F32), name=name)(*args)


def _mm_tn(a, b, name="mm_tn"):
    M, K = a.shape
    N = b.shape[1]
    tm = _tile(M, 704, 16)
    tk = _tile(K, 1408, 128)
    tn = _tile(N, 2048, 128)

    def body(a_ref, b_ref, o_ref):
        m = pl.program_id(2)
        p = lax.dot_general(a_ref[...], b_ref[...], (((0,), (0,)), ((), ())), preferred_element_type=F32)

        @pl.when(m == 0)
        def _():
            o_ref[...] = p

        @pl.when(m > 0)
        def _():
            o_ref[...] += p

    return pl.pallas_call(
        body, grid=(K // tk, N // tn, M // tm),
        in_specs=[pl.BlockSpec((tm, tk), lambda kk, j, m: (m, kk)), pl.BlockSpec((tm, tn), lambda kk, j, m: (m, j))],
        out_specs=pl.BlockSpec((tk, tn), lambda kk, j, m: (kk, j)),
        out_shape=jax.ShapeDtypeStruct((K, N), F32), name=name)(a, b)


def _rms_fwd(x, g, name):
    Lp = x.shape[0]
    tr = _tile(Lp, 256, 16)

    def body(x_ref, g_ref, o_ref):
        xv = x_ref[...]
        r = lax.rsqrt(jnp.mean(xv * xv, axis=-1, keepdims=True) + EPS)
        o_ref[...] = (xv * r * g_ref[...]).astype(BF16)

    return pl.pallas_call(
        body, grid=(Lp // tr,),
        in_specs=[pl.BlockSpec((tr, D_MODEL), lambda i: (i, 0)), pl.BlockSpec((1, D_MODEL), lambda i: (0, 0))],
        out_specs=pl.BlockSpec((tr, D_MODEL), lambda i: (i, 0)),
        out_shape=jax.ShapeDtypeStruct((Lp, D_MODEL), BF16), name=name)(x, g)


def _rms_bwd(x, g, dy, dres, pad, name):
    Lp = x.shape[0]
    tr = _tile(Lp, 256, 16)

    def body(x_ref, g_ref, dy_ref, dr_ref, dx_ref, dxb_ref, dg_ref):
        i = pl.program_id(0)
        xv = x_ref[...]
        r = lax.rsqrt(jnp.mean(xv * xv, axis=-1, keepdims=True) + EPS)
        xh = xv * r
        dyv = dy_ref[...]
        dxh = dyv * g_ref[...]
        dx = r * (dxh - xh * jnp.mean(dxh * xh, axis=-1, keepdims=True)) + dr_ref[...]
        row = i * tr + lax.broadcasted_iota(jnp.int32, (tr, 1), 0)
        dx = jnp.where(row >= pad, dx, 0.0)
        dx_ref[...] = dx
        dxb_ref[...] = dx.astype(BF16)
        part = jnp.sum(dyv * xh, axis=0, keepdims=True)

        @pl.when(i == 0)
        def _():
            dg_ref[...] = part

        @pl.when(i > 0)
        def _():
            dg_ref[...] += part

    blk = pl.BlockSpec((tr, D_MODEL), lambda i: (i, 0))
    vec = pl.BlockSpec((1, D_MODEL), lambda i: (0, 0))
    return pl.pallas_call(
        body, grid=(Lp // tr,), in_specs=[blk, vec, blk, blk], out_specs=[blk, blk, vec],
        out_shape=[jax.ShapeDtypeStruct((Lp, D_MODEL), F32), jax.ShapeDtypeStruct((Lp, D_MODEL), BF16),
                   jax.ShapeDtypeStruct((1, D_MODEL), F32)], name=name)(x, g, dy, dres)


def _final(h2, g, tgt, first_row):
    Lp = h2.shape[0]
    tr = _tile(Lp, 256, 16)

    def body(x_ref, g_ref, t_ref, loss_ref, dx_ref, dxb_ref, dg_ref):
        i = pl.program_id(0)
        xv = x_ref[...]
        gv = g_ref[...]
        r = lax.rsqrt(jnp.mean(xv * xv, axis=-1, keepdims=True) + EPS)
        xh = xv * r
        row = i * tr + lax.broadcasted_iota(jnp.int32, (tr, 1), 0)
        err = jnp.where(row >= first_row, xh * gv - t_ref[...], 0.0)
        lpart = jnp.sum(err * err, axis=0, keepdims=True) * (0.5 / D_MODEL)
        dyv = err * (1.0 / D_MODEL)
        dxh = dyv * gv
        dx = r * (dxh - xh * jnp.mean(dxh * xh, axis=-1, keepdims=True))
        dx_ref[...] = dx
        dxb_ref[...] = dx.astype(BF16)
        part = jnp.sum(dyv * xh, axis=0, keepdims=True)

        @pl.when(i == 0)
        def _():
            dg_ref[...] = part
            loss_ref[...] = lpart

        @pl.when(i > 0)
        def _():
            dg_ref[...] += part
            loss_ref[...] += lpart

    blk = pl.BlockSpec((tr, D_MODEL), lambda i: (i, 0))
    vec = pl.BlockSpec((1, D_MODEL), lambda i: (0, 0))
    return pl.pallas_call(
        body, grid=(Lp // tr,), in_specs=[blk, vec, blk], out_specs=[vec, blk, blk, vec],
        out_shape=[jax.ShapeDtypeStruct((1, D_MODEL), F32), jax.ShapeDtypeStruct((Lp, D_MODEL), F32),
                   jax.ShapeDtypeStruct((Lp, D_MODEL), BF16), jax.ShapeDtypeStruct((1, D_MODEL), F32)],
        name="final_norm_loss")(h2, g, tgt)


def _halo_prev(tr, width, col=0):
    return pl.BlockSpec((8, width), lambda i: (jnp.maximum(i * (tr // 8) - 1, 0), col))


def _halo_next(tr, width, nrows, col=0):
    last = nrows // 8 - 1
    return pl.BlockSpec((8, width), lambda i: (jnp.minimum((i + 1) * (tr // 8), last), col))


def _gdn_conv(ext, w, tr, lo):
    acc = w[0:1, :] * ext[lo:lo + tr, :]
    for kk in range(1, GDN_CONV):
        acc = acc + w[kk:kk + 1, :] * ext[lo + kk:lo + kk + tr, :]
    return acc


def _gdn_pre(proj_m, proj_s, conv_w, gparams, pad):
    Lp = proj_m.shape[0]
    tr = _tile(Lp, 192, 64)
    W3 = 3 * D_MODEL

    def body(main_ref, prev_ref, s_ref, w_ref, gp_ref, qkv_ref, gsm_ref):
        i = pl.program_id(0)
        prev = jnp.where(i > 0, prev_ref[...], 0.0)
        ext = jnp.concatenate([prev, main_ref[...]], axis=0)
        c = _gdn_conv(ext, w_ref[...], tr, 8 - (GDN_CONV - 1))
        s = c * _sig(c)
        scale = GDN_D ** -0.5
        for j in range(2 * GDN_H):
            seg = s[:, j * GDN_D:(j + 1) * GDN_D]
            r = lax.rsqrt(_rowsum(seg * seg) + EPS)
            if j < GDN_H:
                r = r * scale
            qkv_ref[:, j * GDN_D:(j + 1) * GDN_D] = seg * r
        qkv_ref[:, 2 * D_MODEL:] = s[:, 2 * D_MODEL:]
        sm = s_ref[...]
        gp = gp_ref[...]
        lane = lax.broadcasted_iota(jnp.int32, sm.shape, 1)
        z = sm + gp[1:2, :]
        softplus = jnp.maximum(z, 0.0) + jnp.log(1.0 + jnp.exp(-jnp.abs(z)))
        lg = -jnp.exp(gp[0:1, :]) * softplus
        row = i * tr + lax.broadcasted_iota(jnp.int32, (tr, 1), 0)
        out = jnp.where(lane < GDN_H, lg, jnp.where(lane < 2 * GDN_H, _sig(sm), 0.0))
        gsm_ref[...] = jnp.where(row >= pad, out, 0.0)

    return pl.pallas_call(
        body, grid=(Lp // tr,),
        in_specs=[pl.BlockSpec((tr, W3), lambda i: (i, 0)), _halo_prev(tr, W3),
                  pl.BlockSpec((tr, LANES), lambda i: (i, 0)),
                  pl.BlockSpec((GDN_CONV, W3), lambda i: (0, 0)), pl.BlockSpec((8, LANES), lambda i: (0, 0))],
        out_specs=[pl.BlockSpec((tr, W3), lambda i: (i, 0)), pl.BlockSpec((tr, LANES), lambda i: (i, 0))],
        out_shape=[jax.ShapeDtypeStruct((Lp, W3), F32), jax.ShapeDtypeStruct((Lp, LANES), F32)],
        name="gdn_pre")(proj_m, proj_m, proj_s, conv_w, gparams)


def _gdn_pre_bwd(proj_m, proj_s, conv_w, gparams, dq, dk, dv, dgs, pad):
    Lp = proj_m.shape[0]
    tr = _tile(Lp, 192, 64)
    W3 = 3 * D_MODEL
    te = tr + 8

    def body(main_ref, prev_ref, next_ref, s_ref, w_ref, gp_ref,
             dq_ref, dqn_ref, dk_ref, dkn_ref, dv_ref, dvn_ref, dgs_ref,
             da_ref, ds_ref, dw_ref, dgp_ref):
        i = pl.program_id(0)
        w = w_ref[...]
        prev = jnp.where(i > 0, prev_ref[...], 0.0)
        ext = jnp.concatenate([prev, main_ref[...], next_ref[...]], axis=0)
        c = _gdn_conv(ext, w, te, 8 - (GDN_CONV - 1))
        sg = _sig(c)
        s = c * sg
        rowe = i * tr + lax.broadcasted_iota(jnp.int32, (te, 1), 0)
        live = (rowe >= pad) & (rowe < Lp)
        dqe = jnp.concatenate([dq_ref[...], dqn_ref[...]], axis=0)
        dke = jnp.concatenate([dk_ref[...], dkn_ref[...]], axis=0)
        dve = jnp.concatenate([dv_ref[...], dvn_ref[...]], axis=0)
        scale = GDN_D ** -0.5
        parts = []
        for j in range(2 * GDN_H):
            seg = s[:, j * GDN_D:(j + 1) * GDN_D]
            r = lax.rsqrt(_rowsum(seg * seg) + EPS)
            xh = seg * r
            if j < GDN_H:
                dxh = dqe[:, j * GDN_D:(j + 1) * GDN_D] * scale
            else:
                dxh = dke[:, (j - GDN_H) * GDN_D:(j - GDN_H + 1) * GDN_D]
            parts.append(r * (dxh - xh * _rowsum(dxh * xh)))
        parts.append(dve)
        dsv = jnp.concatenate(parts, axis=1)
        dc = jnp.where(live, dsv * (sg * (1.0 + c * (1.0 - sg))), 0.0)
        dcs = [dc[GDN_CONV - 1 - kk:GDN_CONV - 1 - kk + tr, :] for kk in range(GDN_CONV)]
        acc = w[0:1, :] * dcs[0]
        for kk in range(1, GDN_CONV):
            acc = acc + w[kk:kk + 1, :] * dcs[kk]
        da_ref[...] = acc.astype(BF16)
        pm = main_ref[...]
        rows = [jnp.sum(dcs[kk] * pm, axis=0, keepdims=True) for kk in range(GDN_CONV)]
        dwp = jnp.concatenate(rows + [jnp.zeros((8 - GDN_CONV, W3), F32)], axis=0)

        sm = s_ref[...]
        gp = gp_ref[...]
        lane = lax.broadcasted_iota(jnp.int32, sm.shape, 1)
        rowm = i * tr + lax.broadcasted_iota(jnp.int32, (tr, 1), 0)
        dgv = jnp.where(rowm >= pad, dgs_ref[...], 0.0)
        dlg = jnp.where(lane < GDN_H, dgv, 0.0)
        dbt = jnp.where((lane >= GDN_H) & (lane < 2 * GDN_H), dgv, 0.0)
        z = sm + gp[1:2, :]
        softplus = jnp.maximum(z, 0.0) + jnp.log(1.0 + jnp.exp(-jnp.abs(z)))
        ea = jnp.exp(gp[0:1, :])
        dz = dlg * (-ea) * _sig(z)
        dal = dlg * (-ea) * softplus
        bt = _sig(sm)
        dgb = dbt * bt * (1.0 - bt)
        ds_ref[...] = (dz + dgb).astype(BF16)
        gpp = jnp.concatenate([jnp.sum(dal, axis=0, keepdims=True), jnp.sum(dz, axis=0, keepdims=True),
                               jnp.zeros((6, LANES), F32)], axis=0)

        @pl.when(i == 0)
        def _():
            dw_ref[...] = dwp
            dgp_ref[...] = gpp

        @pl.when(i > 0)
        def _():
            dw_ref[...] += dwp
            dgp_ref[...] += gpp

    m3 = pl.BlockSpec((tr, W3), lambda i: (i, 0))
    m1 = pl.BlockSpec((tr, D_MODEL), lambda i: (i, 0))
    n1 = _halo_next(tr, D_MODEL, Lp)
    return pl.pallas_call(
        body, grid=(Lp // tr,),
        in_specs=[m3, _halo_prev(tr, W3), _halo_next(tr, W3, Lp), pl.BlockSpec((tr, LANES), lambda i: (i, 0)),
                  pl.BlockSpec((GDN_CONV, W3), lambda i: (0, 0)), pl.BlockSpec((8, LANES), lambda i: (0, 0)),
                  m1, n1, m1, n1, m1, n1, pl.BlockSpec((tr, LANES), lambda i: (i, 0))],
        out_specs=[m3, pl.BlockSpec((tr, LANES), lambda i: (i, 0)),
                   pl.BlockSpec((8, W3), lambda i: (0, 0)), pl.BlockSpec((8, LANES), lambda i: (0, 0))],
        out_shape=[jax.ShapeDtypeStruct((Lp, W3), BF16), jax.ShapeDtypeStruct((Lp, LANES), BF16),
                   jax.ShapeDtypeStruct((8, W3), F32), jax.ShapeDtypeStruct((8, LANES), F32)],
        name="gdn_pre_bwd")(proj_m, proj_m, proj_m, proj_s, conv_w, gparams, dq, dq, dk, dk, dv, dv, dgs)


def _gdn_gates(gs):
    ri = lax.broadcasted_iota(jnp.int32, (CHUNK, CHUNK), 0)
    ci = lax.broadcasted_iota(jnp.int32, (CHUNK, CHUNK), 1)
    tril = ri >= ci
    strict = ri > ci
    gall = _dx(tril.astype(F32), gs)
    lane8 = lax.broadcasted_iota(jnp.int32, (8, LANES), 1)
    sub8 = lax.broadcasted_iota(jnp.int32, (8, LANES), 0)
    grow = _dxnt((lane8 == sub8).astype(F32), gall)
    return gall, grow, tril, strict


def _gdn_decay(gall, grow, tril, h):
    g = gall[:, h:h + 1]
    return g, jnp.where(tril, jnp.exp(jnp.where(tril, g - grow[h:h + 1, :], 0.0)), 0.0)


def _gdn_chunk_specs(N, rev):
    cn = (lambda n: N - 1 - n) if rev else (lambda n: n)
    col = lambda j: pl.BlockSpec((CHUNK, D_MODEL), lambda n: (cn(n), j))
    gate = pl.BlockSpec((CHUNK, LANES), lambda n: (cn(n), 0))
    st = lambda a, b: pl.BlockSpec((GDN_H, None, a, b), lambda n: (0, cn(n), 0, 0))
    return col, gate, st


def _gdn_chunk_fwd(qkv, gsm):
    Lp = qkv.shape[0]
    N = Lp // CHUNK

    def body(q_ref, k_ref, v_ref, gs_ref, o_ref, sin_ref, t_ref, S):
        n = pl.program_id(0)

        @pl.when(n == 0)
        def _():
            S[...] = jnp.zeros_like(S)

        gs = gs_ref[...]
        gall, grow, tril, strict = _gdn_gates(gs)
        ri = lax.broadcasted_iota(jnp.int32, (CHUNK, CHUNK), 0)
        ci = lax.broadcasted_iota(jnp.int32, (CHUNK, CHUNK), 1)
        eye = (ri == ci).astype(F32)
        heads = range(GDN_H)
        sls = [slice(h * GDN_D, (h + 1) * GDN_D) for h in heads]
        q = [q_ref[:, sl] for sl in sls]
        k = [k_ref[:, sl] for sl in sls]
        v = [v_ref[:, sl] for sl in sls]
        s0 = [S[h] for h in heads]
        beta = [gs[:, GDN_H + h:GDN_H + h + 1] for h in heads]
        gg = [_gdn_decay(gall, grow, tril, h) for h in heads]
        g = [x[0] for x in gg]
        gam = [x[1] for x in gg]
        eg = [jnp.exp(g[h]) for h in heads]
        gl = [g[h][CHUNK - 1:CHUNK, :] for h in heads]
        kb = [k[h] * beta[h] for h in heads]
        pw = [-jnp.where(strict, _dnt(kb[h], k[h]) * gam[h], 0.0) for h in heads]
        p = [_dnt(q[h], k[h]) * gam[h] for h in heads]
        qs = [_d(q[h] * eg[h], s0[h]) for h in heads]
        t = [eye + pw[h] for h in heads]
        for _ in range(5):
            pw = [_d3g(pw[h], pw[h], _NN) for h in heads]
            t = [t[h] + _d3g(t[h], pw[h], _NN) for h in heads]
        u = [_d(t[h], v[h] * beta[h]) for h in heads]
        w = [_d(t[h], kb[h] * eg[h]) for h in heads]
        vnew = [u[h] - _d(w[h], s0[h]) for h in heads]
        for h in heads:
            o_ref[:, sls[h]] = qs[h] + _d(p[h], vnew[h])
            sin_ref[h] = s0[h]
            t_ref[h] = t[h]
            S[h] = s0[h] * jnp.exp(gl[h]) + _dtn(k[h] * jnp.exp(gl[h] - g[h]), vnew[h])

    col, gate, st = _gdn_chunk_specs(N, False)
    return pl.pallas_call(
        body, grid=(N,),
        in_specs=[col(0), col(1), col(2), gate],
        out_specs=[col(0), st(GDN_D, GDN_D), st(CHUNK, CHUNK)],
        out_shape=[jax.ShapeDtypeStruct((Lp, D_MODEL), F32), jax.ShapeDtypeStruct((GDN_H, N, GDN_D, GDN_D), F32),
                   jax.ShapeDtypeStruct((GDN_H, N, CHUNK, CHUNK), F32)],
        scratch_shapes=[pltpu.VMEM((GDN_H, GDN_D, GDN_D), F32)],
        name="gdn_chunk_fwd")(qkv, qkv, qkv, gsm)


def _gdn_chunk_bwd(qkv, gsm, do, s_in, t_in):
    Lp = qkv.shape[0]
    N = Lp // CHUNK

    def body(q_ref, k_ref, v_ref, gs_ref, do_ref, sin_ref, t_ref, dq_ref, dk_ref, dv_ref, dgs_ref, dS):
        n = pl.program_id(0)

        @pl.when(n == 0)
        def _():
            dS[...] = jnp.zeros_like(dS)

        gs = gs_ref[...]
        gall, grow, tril, strict = _gdn_gates(gs)
        lane = lax.broadcasted_iota(jnp.int32, (CHUNK, LANES), 1)
        rcol = lax.broadcasted_iota(jnp.int32, (CHUNK, 1), 0)
        ones = jnp.ones((CHUNK, LANES), F32)
        dg_all = jnp.zeros((CHUNK, LANES), F32)
        dbeta_all = jnp.zeros((CHUNK, LANES), F32)
        heads = range(GDN_H)
        sls = [slice(h * GDN_D, (h + 1) * GDN_D) for h in heads]
        H = lambda f: [f(h) for h in heads]
        q = H(lambda h: q_ref[:, sls[h]])
        k = H(lambda h: k_ref[:, sls[h]])
        v = H(lambda h: v_ref[:, sls[h]])
        dov = H(lambda h: do_ref[:, sls[h]])
        s0 = H(lambda h: sin_ref[h])
        t = H(lambda h: t_ref[h])
        dsv = H(lambda h: dS[h])
        beta = H(lambda h: gs[:, GDN_H + h:GDN_H + h + 1])
        gg = H(lambda h: _gdn_decay(gall, grow, tril, h))
        g = [x[0] for x in gg]
        gam = [x[1] for x in gg]
        eg = H(lambda h: jnp.exp(g[h]))
        egl = H(lambda h: jnp.exp(g[h][CHUNK - 1:CHUNK, :]))
        e = H(lambda h: jnp.exp(g[h][CHUNK - 1:CHUNK, :] - g[h]))
        kb = H(lambda h: k[h] * beta[h])
        kbg = H(lambda h: kb[h] * eg[h])
        vb = H(lambda h: v[h] * beta[h])
        qg = H(lambda h: q[h] * eg[h])
        kd = H(lambda h: k[h] * e[h])
        m = H(lambda h: jnp.where(strict, _dnt(kb[h], k[h]) * gam[h], 0.0))
        u = H(lambda h: _d(t[h], vb[h]))
        w = H(lambda h: _d(t[h], kbg[h]))
        p = H(lambda h: _dnt(q[h], k[h]) * gam[h])
        dqg = H(lambda h: _dnt(dov[h], s0[h]))
        kdds = H(lambda h: _d(kd[h], dsv[h]))
        qgdo = H(lambda h: _dtn(qg[h], dov[h]))
        vnew = H(lambda h: u[h] - _d(w[h], s0[h]))
        dvnew = H(lambda h: _dtn(p[h], dov[h]) + kdds[h])
        dp = H(lambda h: jnp.where(tril, _dnt(dov[h], vnew[h]), 0.0))
        dkd = H(lambda h: _dnt(vnew[h], dsv[h]))
        dw = H(lambda h: -_dnt(dvnew[h], s0[h]))
        for h in heads:
            dS[h] = qgdo[h] + egl[h] * dsv[h] - _dtn(w[h], dvnew[h])
        dvb = H(lambda h: _dtn(t[h], dvnew[h]))
        dkbg = H(lambda h: _dtn(t[h], dw[h]))
        dt = H(lambda h: _dnt(dvnew[h], vb[h]) + _dnt(dw[h], kbg[h]))
        x1 = H(lambda h: _d3g(t[h], dt[h], _TN))
        dm = H(lambda h: jnp.where(strict, -_d3g(x1[h], t[h], _NT), 0.0))
        dkk = H(lambda h: dm[h] * gam[h])
        dqk = H(lambda h: dp[h] * gam[h])
        dkb = H(lambda h: _d(dkk[h], k[h]) + eg[h] * dkbg[h])
        em = H(lambda h: dm[h] * m[h] + dp[h] * p[h])
        colsum = H(lambda h: _d3g(em[h], ones, _TN)[:, 0:1])
        for h in heads:
            dk_ref[:, sls[h]] = _dtn(dkk[h], kb[h]) + _dtn(dqk[h], q[h]) + dkd[h] * e[h] + beta[h] * dkb[h]
            dq_ref[:, sls[h]] = _d(dqk[h], k[h]) + dqg[h] * eg[h]
            dv_ref[:, sls[h]] = beta[h] * dvb[h]
        for h in heads:
            dbeta = _rowsum(k[h] * dkb[h]) + _rowsum(v[h] * dvb[h])
            z = _rowsum(kd[h] * dkd[h])
            dg = _rowsum(em[h]) - colsum[h] + _rowsum(qg[h] * dqg[h]) + _rowsum(kbg[h] * dkbg[h]) - z
            extra = _allsum(z) + egl[h] * _allsum(s0[h] * dsv[h])
            dg = dg + jnp.where(rcol == CHUNK - 1, extra, 0.0)
            dg_all = dg_all + jnp.where(lane == h, dg, 0.0)
            dbeta_all = dbeta_all + jnp.where(lane == GDN_H + h, dbeta, 0.0)
        ri = lax.broadcasted_iota(jnp.int32, (CHUNK, CHUNK), 0)
        ci = lax.broadcasted_iota(jnp.int32, (CHUNK, CHUNK), 1)
        dgs_ref[...] = _dx((ci >= ri).astype(F32), dg_all) + dbeta_all

    col, gate, st = _gdn_chunk_specs(N, True)
    return pl.pallas_call(
        body, grid=(N,),
        in_specs=[col(0), col(1), col(2), gate, col(0), st(GDN_D, GDN_D), st(CHUNK, CHUNK)],
        out_specs=[col(0), col(0), col(0), gate],
        out_shape=[jax.ShapeDtypeStruct((Lp, D_MODEL), F32)] * 3 + [jax.ShapeDtypeStruct((Lp, LANES), F32)],
        scratch_shapes=[pltpu.VMEM((GDN_H, GDN_D, GDN_D), F32)],
        name="gdn_chunk_bwd")(qkv, qkv, qkv, gsm, do, s_in, t_in)


def _rot(x, c, s):
    half = RET_D // 2
    x1 = x[:, :half]
    x2 = x[:, half:]
    return jnp.concatenate([x1 * c - x2 * s, x2 * c + x1 * s], axis=1)


def _rot_bwd(d, c, s):
    half = RET_D // 2
    d1 = d[:, :half]
    d2 = d[:, half:]
    return jnp.concatenate([d1 * c + d2 * s, d2 * c - d1 * s], axis=1)


def _ret_tables():
    hh = jnp.arange(RET_H, dtype=F32)
    lg = jnp.log(1.0 - 2.0 ** (-5.0 - hh))
    idx = jnp.arange(CHUNK, dtype=F32)
    tril = jnp.asarray(np.tril(np.ones((CHUNK, CHUNK), dtype=bool)))
    dmask = jnp.where(tril, jnp.exp((idx[:, None] - idx[None, :]) * lg[:, None, None]), 0.0)
    qdec = jnp.exp((idx[None, :] + 1.0) * lg[:, None])
    kdec = jnp.exp((CHUNK - 1.0 - idx[None, :]) * lg[:, None])
    gch = jnp.exp(CHUNK * lg)
    qdec = jnp.broadcast_to(qdec[:, :, None], (RET_H, CHUNK, RET_D))
    kdec = jnp.broadcast_to(kdec[:, :, None], (RET_H, CHUNK, RET_D))
    gch = jnp.broadcast_to(gch[:, None, None], (RET_H, 8, LANES))
    return dmask, qdec, kdec, gch


def _ret_specs(N, rev):
    cn = (lambda n: N - 1 - n) if rev else (lambda n: n)
    col = lambda j: pl.BlockSpec((CHUNK, D_MODEL), lambda n: (cn(n), j))
    tab = lambda a, b: pl.BlockSpec((RET_H, a, b), lambda n: (0, 0, 0))
    rope = pl.BlockSpec((CHUNK, LANES), lambda n: (cn(n), 0))
    st = pl.BlockSpec((RET_H, None, RET_D, RET_D), lambda n: (0, cn(n), 0, 0))
    return col, tab, rope, st


def _ret_chunk_fwd(proj_m, cos, sin, tables):
    Lp = proj_m.shape[0]
    N = Lp // CHUNK
    dmask, qdec, kdec, gch = tables

    def body(q_ref, k_ref, v_ref, c_ref, s_ref, dm_ref, qd_ref, kd_ref, g_ref, o_ref, sin_ref, S):
        n = pl.program_id(0)

        @pl.when(n == 0)
        def _():
            S[...] = jnp.zeros_like(S)

        c = c_ref[...]
        s = s_ref[...]
        heads = range(RET_H)
        sls = [slice(h * RET_D, (h + 1) * RET_D) for h in heads]
        H = lambda f: [f(h) for h in heads]
        qr = H(lambda h: _rot(q_ref[:, sls[h]], c, s))
        ks = H(lambda h: _rot(k_ref[:, sls[h]], c, s) * (RET_D ** -0.5))
        v = H(lambda h: v_ref[:, sls[h]])
        s0 = H(lambda h: S[h])
        a = H(lambda h: _dnt(qr[h], ks[h]) * dm_ref[h])
        qs = H(lambda h: _d(qr[h] * qd_ref[h], s0[h]))
        kv = H(lambda h: _dtn(ks[h] * kd_ref[h], v[h]))
        for h in heads:
            o_ref[:, sls[h]] = _d(a[h], v[h]) + qs[h]
            sin_ref[h] = s0[h]
            S[h] = s0[h] * g_ref[h, 0:1, 0:1] + kv[h]

    col, tab, rope, st = _ret_specs(N, False)
    return pl.pallas_call(
        body, grid=(N,),
        in_specs=[col(3), col(4), col(5), rope, rope,
                  tab(CHUNK, CHUNK), tab(CHUNK, RET_D), tab(CHUNK, RET_D), tab(8, LANES)],
        out_specs=[col(0), st],
        out_shape=[jax.ShapeDtypeStruct((Lp, D_MODEL), F32), jax.ShapeDtypeStruct((RET_H, N, RET_D, RET_D), F32)],
        scratch_shapes=[pltpu.VMEM((RET_H, RET_D, RET_D), F32)],
        name="ret_chunk_fwd")(proj_m, proj_m, proj_m, cos, sin, dmask, qdec, kdec, gch)


def _ret_chunk_bwd(proj_m, cos, sin, tables, do, s_in):
    Lp = proj_m.shape[0]
    N = Lp // CHUNK
    dmask, qdec, kdec, gch = tables

    def body(q_ref, k_ref, v_ref, c_ref, s_ref, dm_ref, qd_ref, kd_ref, g_ref, do_ref, sin_ref,
             dq_ref, dk_ref, dv_ref, dS):
        n = pl.program_id(0)

        @pl.when(n == 0)
        def _():
            dS[...] = jnp.zeros_like(dS)

        c = c_ref[...]
        s = s_ref[...]
        kscale = RET_D ** -0.5
        heads = range(RET_H)
        sls = [slice(h * RET_D, (h + 1) * RET_D) for h in heads]
        H = lambda f: [f(h) for h in heads]
        qr = H(lambda h: _rot(q_ref[:, sls[h]], c, s))
        ks = H(lambda h: _rot(k_ref[:, sls[h]], c, s) * kscale)
        v = H(lambda h: v_ref[:, sls[h]])
        dov = H(lambda h: do_ref[:, sls[h]])
        s0 = H(lambda h: sin_ref[h])
        dsv = H(lambda h: dS[h])
        ad = H(lambda h: _dnt(qr[h], ks[h]) * dm_ref[h])
        da = H(lambda h: _dnt(dov[h], v[h]) * dm_ref[h])
        kds = H(lambda h: _d(ks[h] * kd_ref[h], dsv[h]))
        dos = H(lambda h: _dnt(dov[h], s0[h]) * qd_ref[h])
        vds = H(lambda h: _dnt(v[h], dsv[h]) * kd_ref[h])
        qdo = H(lambda h: _dtn(qr[h] * qd_ref[h], dov[h]))
        for h in heads:
            dS[h] = dsv[h] * g_ref[h, 0:1, 0:1] + qdo[h]
        for h in heads:
            dv_ref[:, sls[h]] = (_dtn(ad[h], dov[h]) + kds[h]).astype(BF16)
            dq_ref[:, sls[h]] = _rot_bwd(_d(da[h], ks[h]) + dos[h], c, s).astype(BF16)
            dk_ref[:, sls[h]] = _rot_bwd((_dtn(da[h], qr[h]) + vds[h]) * kscale, c, s).astype(BF16)

    col, tab, rope, st = _ret_specs(N, True)
    return pl.pallas_call(
        body, grid=(N,),
        in_specs=[col(3), col(4), col(5), rope, rope,
                  tab(CHUNK, CHUNK), tab(CHUNK, RET_D), tab(CHUNK, RET_D), tab(8, LANES), col(0), st],
        out_specs=[col(0), col(0), col(0)],
        out_shape=[jax.ShapeDtypeStruct((Lp, D_MODEL), BF16)] * 3,
        scratch_shapes=[pltpu.VMEM((RET_H, RET_D, RET_D), F32)],
        name="ret_chunk_bwd")(proj_m, proj_m, proj_m, cos, sin, dmask, qdec, kdec, gch, do, s_in)


def _merge_specs(tr):
    col = lambda j: pl.BlockSpec((tr, D_MODEL), lambda i: (i, j))
    return col


def _merge_fwd(o_a, o_b, proj_m, gnorm):
    Lp = o_a.shape[0]
    tr = _tile(Lp, 192, 16)

    def body(oa_ref, ob_ref, gz_ref, rg_ref, ga_ref, gb_ref, gn_ref, y_ref):
        gn = gn_ref[...]
        oa = oa_ref[...]
        ob = ob_ref[...]
        gz = gz_ref[...]
        ya = []
        for j in range(GDN_H):
            seg = oa[:, j * GDN_D:(j + 1) * GDN_D]
            r = lax.rsqrt(jnp.mean(seg * seg, axis=-1, keepdims=True) + EPS)
            ya.append(seg * r * gn)
        ya = jnp.concatenate(ya, axis=1) * (gz * _sig(gz))
        yb = []
        for j in range(RET_H):
            seg = ob[:, j * RET_D:(j + 1) * RET_D]
            r = lax.rsqrt(jnp.mean(seg * seg, axis=-1, keepdims=True) + EPS)
            yb.append(seg * r)
        rg = rg_ref[...]
        yb = jnp.concatenate(yb, axis=1) * (rg * _sig(rg))
        y_ref[...] = (_sig(ga_ref[...]) * ya + _sig(gb_ref[...]) * yb).astype(BF16)

    col = _merge_specs(tr)
    return pl.pallas_call(
        body, grid=(Lp // tr,),
        in_specs=[col(0), col(0), col(6), col(7), col(8), col(9), pl.BlockSpec((1, GDN_D), lambda i: (0, 0))],
        out_specs=col(0), out_shape=jax.ShapeDtypeStruct((Lp, D_MODEL), BF16),
        name="merge_fwd")(o_a, o_b, proj_m, proj_m, proj_m, proj_m, gnorm)


def _merge_bwd(dy, o_a, o_b, proj_m, gnorm):
    Lp = o_a.shape[0]
    tr = _tile(Lp, 192, 16)

    def body(dy_ref, oa_ref, ob_ref, gz_ref, rg_ref, ga_ref, gb_ref, gn_ref, dc_ref, doa_ref, dob_ref, dgn_ref):
        i = pl.program_id(0)
        gn = gn_ref[...]
        dyv = dy_ref[...]
        oa = oa_ref[...]
        ob = ob_ref[...]
        gz = gz_ref[...]
        rg = rg_ref[...]
        sa = _sig(ga_ref[...])
        sb = _sig(gb_ref[...])
        dya = dyv * sa
        dyb = dyv * sb
        sgz = _sig(gz)
        szz = gz * sgz
        dgn = jnp.zeros((1, GDN_D), F32)
        ya = []
        dgz = []
        for j in range(GDN_H):
            sl = slice(j * GDN_D, (j + 1) * GDN_D)
            seg = oa[:, sl]
            r = lax.rsqrt(jnp.mean(seg * seg, axis=-1, keepdims=True) + EPS)
            xh = seg * r
            oan = xh * gn
            ya.append(oan * szz[:, sl])
            dgz.append(dya[:, sl] * oan * (sgz[:, sl] * (1.0 + gz[:, sl] * (1.0 - sgz[:, sl]))))
            doan = dya[:, sl] * szz[:, sl]
            dgn = dgn + jnp.sum(doan * xh, axis=0, keepdims=True)
            dxh = doan * gn
            doa_ref[:, sl] = r * (dxh - xh * jnp.mean(dxh * xh, axis=-1, keepdims=True))
        ya = jnp.concatenate(ya, axis=1)
        srg = _sig(rg)
        srr = rg * srg
        yb = []
        drg = []
        for j in range(RET_H):
            sl = slice(j * RET_D, (j + 1) * RET_D)
            seg = ob[:, sl]
            r = lax.rsqrt(jnp.mean(seg * seg, axis=-1, keepdims=True) + EPS)
            xh = seg * r
            yb.append(xh * srr[:, sl])
            drg.append(dyb[:, sl] * xh * (srg[:, sl] * (1.0 + rg[:, sl] * (1.0 - srg[:, sl]))))
            dxh = dyb[:, sl] * srr[:, sl]
            dob_ref[:, sl] = r * (dxh - xh * jnp.mean(dxh * xh, axis=-1, keepdims=True))
        yb = jnp.concatenate(yb, axis=1)
        dc_ref[:, 0:D_MODEL] = jnp.concatenate(dgz, axis=1).astype(BF16)
        dc_ref[:, D_MODEL:2 * D_MODEL] = jnp.concatenate(drg, axis=1).astype(BF16)
        dc_ref[:, 2 * D_MODEL:3 * D_MODEL] = (dyv * ya * sa * (1.0 - sa)).astype(BF16)
        dc_ref[:, 3 * D_MODEL:] = (dyv * yb * sb * (1.0 - sb)).astype(BF16)

        @pl.when(i == 0)
        def _():
            dgn_ref[...] = dgn

        @pl.when(i > 0)
        def _():
            dgn_ref[...] += dgn

    col = _merge_specs(tr)
    return pl.pallas_call(
        body, grid=(Lp // tr,),
        in_specs=[col(0), col(0), col(0), col(6), col(7), col(8), col(9), pl.BlockSpec((1, GDN_D), lambda i: (0, 0))],
        out_specs=[pl.BlockSpec((tr, 4 * D_MODEL), lambda i: (i, 0)), col(0), col(0),
                   pl.BlockSpec((1, GDN_D), lambda i: (0, 0))],
        out_shape=[jax.ShapeDtypeStruct((Lp, 4 * D_MODEL), BF16), jax.ShapeDtypeStruct((Lp, D_MODEL), F32),
                   jax.ShapeDtypeStruct((Lp, D_MODEL), F32), jax.ShapeDtypeStruct((1, GDN_D), F32)],
        name="merge_bwd")(dy, o_a, o_b, proj_m, proj_m, proj_m, proj_m, gnorm)


def _ffn_conv(ext, w, b, tr, lo):
    acc = b + w[0:1, :] * ext[lo:lo + tr, :]
    for kk in range(1, FFN_CONV):
        acc = acc + w[kk:kk + 1, :] * ext[lo + kk:lo + kk + tr, :]
    return acc


def _ffn_act(up, conv_w, conv_b):
    Lp = up.shape[0]
    tr = _tile(Lp, 192, 16)
    W2 = 2 * D_FF

    def body(main_ref, prev_ref, w_ref, b_ref, act_ref):
        i = pl.program_id(0)
        prev = jnp.where(i > 0, prev_ref[...], 0.0)
        ext = jnp.concatenate([prev, main_ref[...]], axis=0)
        u = _ffn_conv(ext, w_ref[...], b_ref[...], tr, 8 - (FFN_CONV - 1))
        a = u[:, :D_FF]
        act_ref[...] = (a * _sig(a) * u[:, D_FF:]).astype(BF16)

    return pl.pallas_call(
        body, grid=(Lp // tr,),
        in_specs=[pl.BlockSpec((tr, W2), lambda i: (i, 0)), _halo_prev(tr, W2),
                  pl.BlockSpec((FFN_CONV, W2), lambda i: (0, 0)), pl.BlockSpec((1, W2), lambda i: (0, 0))],
        out_specs=pl.BlockSpec((tr, D_FF), lambda i: (i, 0)),
        out_shape=jax.ShapeDtypeStruct((Lp, D_FF), BF16), name="ffn_act")(up, up, conv_w, conv_b)


def _ffn_act_bwd(up, dact, conv_w, conv_b):
    Lp = up.shape[0]
    tr = _tile(Lp, 96, 16)
    W2 = 2 * D_FF
    te = tr + 8

    def body(main_ref, prev_ref, next_ref, da_ref, dan_ref, w_ref, b_ref, dup_ref, acc_ref):
        i = pl.program_id(0)
        w = w_ref[...]
        prev = jnp.where(i > 0, prev_ref[...], 0.0)
        ext = jnp.concatenate([prev, main_ref[...], next_ref[...]], axis=0)
        u = _ffn_conv(ext, w, b_ref[...], te, 8 - (FFN_CONV - 1))
        a = u[:, :D_FF]
        b = u[:, D_FF:]
        rowe = i * tr + lax.broadcasted_iota(jnp.int32, (te, 1), 0)
        dae = jnp.where(rowe < Lp, jnp.concatenate([da_ref[...], dan_ref[...]], axis=0), 0.0)
        sg = _sig(a)
        du = jnp.concatenate([dae * b * (sg * (1.0 + a * (1.0 - sg))), dae * (a * sg)], axis=1)
        dus = [du[FFN_CONV - 1 - kk:FFN_CONV - 1 - kk + tr, :] for kk in range(FFN_CONV)]
        acc = w[0:1, :] * dus[0]
        for kk in range(1, FFN_CONV):
            acc = acc + w[kk:kk + 1, :] * dus[kk]
        dup_ref[...] = acc.astype(BF16)
        upm = main_ref[...]
        rows = [jnp.sum(dus[kk] * upm, axis=0, keepdims=True) for kk in range(FFN_CONV)]
        rows.append(jnp.sum(dus[FFN_CONV - 1], axis=0, keepdims=True))
        part = jnp.concatenate(rows + [jnp.zeros((8 - len(rows), W2), F32)], axis=0)

        @pl.when(i == 0)
        def _():
            acc_ref[...] = part

        @pl.when(i > 0)
        def _():
            acc_ref[...] += part

    return pl.pallas_call(
        body, grid=(Lp // tr,),
        in_specs=[pl.BlockSpec((tr, W2), lambda i: (i, 0)), _halo_prev(tr, W2), _halo_next(tr, W2, Lp),
                  pl.BlockSpec((tr, D_FF), lambda i: (i, 0)), _halo_next(tr, D_FF, Lp),
                  pl.BlockSpec((FFN_CONV, W2), lambda i: (0, 0)), pl.BlockSpec((1, W2), lambda i: (0, 0))],
        out_specs=[pl.BlockSpec((tr, W2), lambda i: (i, 0)), pl.BlockSpec((8, W2), lambda i: (0, 0))],
        out_shape=[jax.ShapeDtypeStruct((Lp, W2), BF16), jax.ShapeDtypeStruct((8, W2), F32)],
        name="ffn_act_bwd")(up, up, up, dact, dact, conv_w, conv_b)


def _local_step(hpad, tgt, pad, wt, first_weights=None, late_weights=None, on_ffn_out_grads=None,
                on_w_in_grads=None):
    Lp = hpad.shape[0]
    first = pad + N_META
    pos = jnp.arange(Lp, dtype=F32) - float(pad)
    half = RET_D // 2
    inv = 1.0 / (ROPE_BASE ** (jnp.arange(half, dtype=F32) / half))
    ang = pos[:, None] * inv[None, :]
    cos, sin = jnp.cos(ang), jnp.sin(ang)
    tables = _ret_tables()
    gparams = jnp.zeros((8, LANES), F32).at[0, :GDN_H].set(wt["a_log"]).at[1, :GDN_H].set(wt["dt_bias"])

    hn1 = _rms_fwd(hpad, wt["norm1"], "rms1_fwd")
    if first_weights is not None:
        wt = {**wt, **first_weights(hn1)}
    proj_m = _mm_nn(hn1, wt["w_main"], name="proj_main")
    proj_s = _mm_nn(hn1, wt["w_small"], name="proj_small")
    qkv, gsm = _gdn_pre(proj_m, proj_s, wt["gdn_conv_w"], gparams, pad)
    o_a, s_a, t_a = _gdn_chunk_fwd(qkv, gsm)
    o_b, s_b = _ret_chunk_fwd(proj_m, cos, sin, tables)
    y = _merge_fwd(o_a, o_b, proj_m, wt["gdn_norm"])
    if late_weights is not None:
        wt = {**wt, **late_weights(y)}
    h1 = _mm_nn(y, wt["w_out"], res=hpad, name="out_proj")
    hn2 = _rms_fwd(h1, wt["norm2"], "rms2_fwd")
    up = _mm_nn(hn2, wt["w_up"], name="ffn_up")
    act = _ffn_act(up, wt["ffn_conv_w"], wt["ffn_conv_b"])
    h2 = _mm_nn(act, wt["w_down"], res=h1, name="ffn_down")
    lossvec, dh2, dh2b, d_norm_f = _final(h2, wt["norm_f"], tgt, first)

    d_w_down = _mm_tn(act, dh2b, name="dw_down")
    dact = _mm_nt(dh2b, wt["w_down"], name="d_act")
    dup, ffn_rows = _ffn_act_bwd(up, dact, wt["ffn_conv_w"], wt["ffn_conv_b"])
    d_w_up = _mm_tn(hn2, dup, name="dw_up")
    dhn2 = _mm_nt(dup, wt["w_up"], name="d_hn2")
    dh1, dh1b, d_norm2 = _rms_bwd(h1, wt["norm2"], dhn2, dh2, pad, "rms2_bwd")

    d_w_out = _mm_tn(y, dh1b, name="dw_out")
    dy = _mm_nt(dh1b, wt["w_out"], name="d_y")
    gnorm = wt["gdn_norm"]
    if on_ffn_out_grads is not None:
        gnorm = gnorm + on_ffn_out_grads(d_w_down, d_w_up, d_w_out)[0:1, :]
    d_c, do_a, do_b, d_gnorm = _merge_bwd(dy, o_a, o_b, proj_m, gnorm)
    drq, drk, drv = _ret_chunk_bwd(proj_m, cos, sin, tables, do_b, s_b)
    dq, dk, dv, dgs = _gdn_chunk_bwd(qkv, gsm, do_a, s_a, t_a)
    d_a, d_s, conv_rows, gp_rows = _gdn_pre_bwd(proj_m, proj_s, wt["gdn_conv_w"], gparams, dq, dk, dv, dgs, pad)

    wm = wt["w_main"]
    segs = [(d_a, 0, 3 * D_MODEL), (drq, 3 * D_MODEL, D_MODEL), (drk, 4 * D_MODEL, D_MODEL),
            (drv, 5 * D_MODEL, D_MODEL), (d_c, 6 * D_MODEL, 4 * D_MODEL)]
    d_w_main = jnp.concatenate([_mm_tn(hn1, d, name="dw_in_%d" % i) for i, (d, _, _) in enumerate(segs)], axis=1)
    d_w_small = _mm_tn(hn1, d_s, name="dw_in_small")
    w_small = wt["w_small"]
    if on_w_in_grads is not None:
        w_small = w_small + on_w_in_grads(d_w_main, d_w_small)[0:1, :].astype(w_small.dtype)
    dhn1 = _mm_nt(d_s, w_small, name="d_hn1_small")
    for i, (d, off, width) in enumerate(segs):
        dhn1 = _mm_nt(d, wm[:, off:off + width], res=dhn1, name="d_hn1_%d" % i)
    dh0, _, d_norm1 = _rms_bwd(hpad, wt["norm1"], dhn1, dh1, pad, "rms1_bwd")

    grads = {
        "norm1": d_norm1, "w_main": d_w_main, "w_small": d_w_small, "gdn_conv_w": conv_rows[:GDN_CONV],
        "a_log": gp_rows[0, :GDN_H], "dt_bias": gp_rows[1, :GDN_H], "gdn_norm": d_gnorm, "w_out": d_w_out,
        "norm2": d_norm2, "w_up": d_w_up, "ffn_conv_w": ffn_rows[:FFN_CONV], "ffn_conv_b": ffn_rows[FFN_CONV:FFN_CONV + 1],
        "w_down": d_w_down, "norm_f": d_norm_f,
    }
    return lossvec, dh0, grads


def _peer(k):
    ix, iy, ic = lax.axis_index("x"), lax.axis_index("y"), lax.axis_index("c")
    px = 1 - ix if (k >> 2) & 1 else ix
    py = 1 - iy if (k >> 1) & 1 else iy
    pc = 1 - ic if k & 1 else ic
    return (px, py, pc), 4 * px + 2 * py + pc


def _comm_call(body, n, out_shapes, name, args):
    hbm = pl.BlockSpec(memory_space=pl.ANY)
    return pl.pallas_call(
        body, out_shape=out_shapes, in_specs=[hbm] * n, out_specs=[hbm] * n,
        scratch_shapes=[pltpu.SemaphoreType.DMA((n, N_DEV - 1)), pltpu.SemaphoreType.DMA((n, N_DEV - 1)),
                        pltpu.SemaphoreType.DMA((n,))],
        name=name)(*args)


def _all_gather(xs, name):
    n = len(xs)

    def body(*refs):
        x_refs, out_refs = refs[:n], refs[n:2 * n]
        send_sems, recv_sems, local_sems = refs[2 * n:]
        _, me = _peer(0)
        pending = []
        for i in range(n):
            local = pltpu.make_async_copy(x_refs[i], out_refs[i].at[me], local_sems.at[i])
            local.start()
            pending.append(local)
        sends = []
        for i in range(n):
            for k in range(1, N_DEV):
                dev, _ = _peer(k)
                cp = pltpu.make_async_remote_copy(
                    src_ref=x_refs[i], dst_ref=out_refs[i].at[me], send_sem=send_sems.at[i, k - 1],
                    recv_sem=recv_sems.at[i, k - 1], device_id=dev, device_id_type=MESH_T)
                cp.start()
                sends.append(cp)
        for i in range(n):
            for k in range(1, N_DEV):
                dev, idx = _peer(k)
                pltpu.make_async_remote_copy(
                    src_ref=x_refs[i], dst_ref=out_refs[i].at[idx], send_sem=send_sems.at[i, k - 1],
                    recv_sem=recv_sems.at[i, k - 1], device_id=dev, device_id_type=MESH_T).wait_recv()
        for cp in sends:
            cp.wait_send()
        for local in pending:
            local.wait()

    out_shapes = [jax.ShapeDtypeStruct((N_DEV,) + a.shape, a.dtype) for a in xs]
    return _comm_call(body, n, out_shapes, name, xs)


def _all_to_all(gs, name):
    n = len(gs)

    def body(*refs):
        g_refs, out_refs = refs[:n], refs[n:2 * n]
        send_sems, recv_sems, local_sems = refs[2 * n:]
        _, me = _peer(0)
        pending = []
        for i in range(n):
            local = pltpu.make_async_copy(g_refs[i].at[me], out_refs[i].at[0], local_sems.at[i])
            local.start()
            pending.append(local)
        sends = []
        for i in range(n):
            for k in range(1, N_DEV):
                dev, idx = _peer(k)
                cp = pltpu.make_async_remote_copy(
                    src_ref=g_refs[i].at[idx], dst_ref=out_refs[i].at[k], send_sem=send_sems.at[i, k - 1],
                    recv_sem=recv_sems.at[i, k - 1], device_id=dev, device_id_type=MESH_T)
                cp.start()
                sends.append(cp)
        for cp in sends:
            cp.wait_recv()
        for cp in sends:
            cp.wait_send()
        for local in pending:
            local.wait()

    out_shapes = [jax.ShapeDtypeStruct(g.shape, g.dtype) for g in gs]
    return _comm_call(body, n, out_shapes, name, gs)


def _split_copies(kind, src_refs, land_refs, send_sems, recv_sems, local_sems, with_recv):
    n = len(src_refs)
    _, me = _peer(0)
    locals_, remotes = [], []
    for i in range(n):
        if kind == "gather":
            locals_.append(pltpu.make_async_copy(src_refs[i], land_refs[i].at[me], local_sems.at[i]))
        else:
            locals_.append(pltpu.make_async_copy(src_refs[i].at[me], land_refs[i].at[0], local_sems.at[i]))
        for k in range(1, N_DEV):
            dev, idx = _peer(k)
            if kind == "gather":
                src, dst, mine = src_refs[i], land_refs[i].at[me], land_refs[i].at[idx]
            else:
                src, dst, mine = src_refs[i].at[idx], land_refs[i].at[k], land_refs[i].at[k]
            j = i * (N_DEV - 1) + k - 1
            send = pltpu.make_async_remote_copy(
                src_ref=src, dst_ref=dst, send_sem=send_sems.at[j], recv_sem=recv_sems.at[j],
                device_id=dev, device_id_type=MESH_T)
            recv = pltpu.make_async_remote_copy(
                src_ref=src, dst_ref=mine, send_sem=send_sems.at[j], recv_sem=recv_sems.at[j],
                device_id=dev, device_id_type=MESH_T) if with_recv else None
            remotes.append((send, recv))
    return locals_, remotes


_HBM = pl.BlockSpec(memory_space=pltpu.HBM)
_SEM = pl.BlockSpec(memory_space=pltpu.SEMAPHORE)
_ANY = pl.BlockSpec(memory_space=pl.ANY)


def _split_start(srcs, kind, name, after):
    n = len(srcs)
    lands = [lax.empty(((N_DEV,) + a.shape) if kind == "gather" else a.shape, a.dtype) for a in srcs]

    def body(*refs):
        src_refs, land_refs = refs[:n], refs[n:2 * n]
        send_sems, recv_sems, local_sems = refs[2 * n + 1:2 * n + 4]
        token = refs[-1]
        locals_, remotes = _split_copies(kind, src_refs, land_refs, send_sems, recv_sems, local_sems, False)
        for cp in locals_:
            cp.start()
        for send, _ in remotes:
            send.start()
        token[...] = jnp.zeros_like(token)

    sems = (pltpu.SemaphoreType.DMA((n * (N_DEV - 1),)), pltpu.SemaphoreType.DMA((n * (N_DEV - 1),)),
            pltpu.SemaphoreType.DMA((n,)))
    thru = tuple(pltpu.HBM(a.shape, a.dtype) for a in list(srcs) + lands)
    outs = pl.pallas_call(
        body, name=name,
        out_shape=sems + thru + (jax.ShapeDtypeStruct((8, LANES), F32),),
        in_specs=[_HBM] * (2 * n) + [_ANY],
        out_specs=[_SEM] * 3 + [_HBM] * (2 * n) + [pl.BlockSpec(memory_space=pltpu.VMEM)],
        input_output_aliases={i: 3 + i for i in range(2 * n)},
        compiler_params=pltpu.CompilerParams(has_side_effects=pltpu.SideEffectType.DATAFLOW_SIDE_EFFECTING),
    )(*[pltpu.with_memory_space_constraint(a, pltpu.HBM) for a in list(srcs) + lands], after)
    return (kind, n, outs[:3], outs[3:3 + 2 * n]), outs[-1]


def _split_wait(handle, name, after):
    kind, n, sems, thru = handle

    def body(*refs):
        src_refs, land_refs = refs[:n], refs[n:2 * n]
        send_sems, recv_sems, local_sems = refs[2 * n:2 * n + 3]
        locals_, remotes = _split_copies(kind, src_refs, land_refs, send_sems, recv_sems, local_sems, True)
        for send, recv in remotes:
            send.wait_send()
            recv.wait_recv()
        for cp in locals_:
            cp.wait()

    outs = pl.pallas_call(
        body, name=name, out_shape=tuple(pltpu.HBM(a.shape, a.dtype) for a in thru),
        in_specs=[_HBM] * (2 * n) + [_SEM] * 3 + [_ANY], out_specs=[_HBM] * (2 * n),
        input_output_aliases={i: i for i in range(2 * n)},
        compiler_params=pltpu.CompilerParams(has_side_effects=pltpu.SideEffectType.DATAFLOW_SIDE_EFFECTING),
    )(*thru, *sems, after)
    return list(outs[n:])


def _adamw(gslabs, w, m, v, name):
    R, Cw = w.shape
    tr = _tile(R, 64 if Cw > 1024 else 128, 8)
    c1 = 1.0 - ADAM_B1 ** ADAM_STEP
    c2 = 1.0 - ADAM_B2 ** ADAM_STEP

    def body(g_ref, w_ref, m_ref, v_ref, go_ref, d_ref, mo_ref, vo_ref):
        g = g_ref[0].astype(F32)
        for k in range(1, N_DEV):
            g = g + g_ref[k].astype(F32)
        mn = ADAM_B1 * m_ref[...] + (1.0 - ADAM_B1) * g
        vn = ADAM_B2 * v_ref[...] + (1.0 - ADAM_B2) * (g * g)
        m_hat = mn / c1
        v_hat = vn / c2
        go_ref[...] = g
        d_ref[...] = -ADAM_LR * (m_hat / (jnp.sqrt(v_hat) + ADAM_EPS) + ADAM_WD * w_ref[...])
        mo_ref[...] = mn
        vo_ref[...] = vn

    blk = pl.BlockSpec((tr, Cw), lambda i: (i, 0))
    return pl.pallas_call(
        body, grid=(R // tr,),
        in_specs=[pl.BlockSpec((N_DEV, tr, Cw), lambda i: (0, i, 0)), blk, blk, blk],
        out_specs=[blk] * 4, out_shape=[jax.ShapeDtypeStruct((R, Cw), F32)] * 4, name=name)(gslabs, w, m, v)


def _pack(arrs, row_mult, dtype=F32):
    parts = []
    total = 0
    for a in arrs:
        f = a.reshape(-1).astype(dtype)
        n = -(-f.shape[0] // 1024) * 1024
        parts.append(jnp.pad(f, (0, n - f.shape[0])))
        total += n
    rows = total // LANES
    rows_p = -(-rows // row_mult) * row_mult
    flat = jnp.concatenate(parts)
    flat = jnp.pad(flat, (0, rows_p * LANES - total))
    return flat.reshape(rows_p, LANES)


def _unpack(packed, shapes):
    lead = packed.shape[:-2]
    flat = packed.reshape(lead + (-1,))
    out = []
    off = 0
    for s in shapes:
        n = int(np.prod(s))
        out.append(flat[..., off:off + n].reshape(lead + tuple(s)))
        off += -(-n // 1024) * 1024
    return out


def _gather_cols(stacked):
    d, r, c = stacked.shape
    return stacked.transpose(1, 0, 2).reshape(r, d * c)


def _scatter_cols(full):
    r, n = full.shape
    return full.reshape(r, N_DEV, n // N_DEV).transpose(1, 0, 2)


def kernel(x, meta, norm1, w_in, gdn_conv_w, gdn_a_log, gdn_dt_bias, gdn_norm, w_out, norm2, w_ffn_up, ffn_conv_w, ffn_conv_b, w_ffn_down, norm_f, loss_target, m_meta, m_norm1, m_w_in, m_gdn_conv_w, m_gdn_a_log, m_gdn_dt_bias, m_gdn_norm, m_w_out, m_norm2, m_w_ffn_up, m_ffn_conv_w, m_ffn_conv_b, m_w_ffn_down, m_norm_f, v_meta, v_norm1, v_w_in, v_gdn_conv_w, v_gdn_a_log, v_gdn_dt_bias, v_gdn_norm, v_w_out, v_norm2, v_w_ffn_up, v_ffn_conv_w, v_ffn_conv_b, v_w_ffn_down, v_norm_f):
    S = x.shape[1]
    L = N_META + S
    pad = (-L) % CHUNK
    Lp = L + pad

    big = [w_in[0], w_out[0], w_ffn_up[0], w_ffn_down[0]]
    small = [meta, gdn_conv_w, ffn_conv_w]
    small_all, = _all_gather([_pack(small, 8)], "gather_small_weights")
    first, first_token = _split_start([big[0].astype(BF16)], "gather", "gather_w_in_start", small_all)
    late, late_token = _split_start([a.astype(BF16) for a in big[1:]], "gather", "gather_late_start", first_token)

    def first_weights(after):
        w_in_s, = _split_wait(first, "gather_w_in_wait", after)
        w_in_f = _gather_cols(w_in_s)
        w_main = jnp.concatenate([w_in_f[:, _O_GQ:_O_GZ], w_in_f[:, _O_RQ:_O_RG], w_in_f[:, _O_GZ:_O_GA],
                                  w_in_f[:, _O_RG:_O_END]], axis=1)
        return {"w_main": w_main, "w_small": jnp.pad(w_in_f[:, _O_GA:_O_RQ], ((0, 0), (0, LANES - 2 * GDN_H)))}

    def late_weights(after):
        w_out_s, w_up_s, w_down_s = _split_wait(late, "gather_late_wait", after)
        return {"w_out": w_out_s.reshape(D_MODEL, D_MODEL), "w_up": _gather_cols(w_up_s),
                "w_down": w_down_s.reshape(D_FF, D_MODEL)}

    meta_s, gconv_s, fconv_s = _unpack(small_all, [a.shape for a in small])
    wt = {
        "norm1": norm1 + jnp.tile(late_token[0:1, :], (1, D_MODEL // LANES)),
        "gdn_conv_w": _gather_cols(gconv_s[:, 0]), "a_log": gdn_a_log[0], "dt_bias": gdn_dt_bias[0],
        "gdn_norm": gdn_norm, "norm2": norm2, "ffn_conv_w": _gather_cols(fconv_s[:, 0]), "ffn_conv_b": ffn_conv_b,
        "norm_f": norm_f.reshape(1, D_MODEL),
    }
    meta_f = _gather_cols(meta_s)

    pending = {}

    def on_ffn_out_grads(d_w_down, d_w_up, d_w_out):
        srcs = [d_w_out.reshape(N_DEV, D_MODEL // N_DEV, D_MODEL), _scatter_cols(d_w_up),
                d_w_down.reshape(N_DEV, D_FF // N_DEV, D_MODEL)]
        pending["ffn_out"], token = _split_start(srcs, "a2a", "exchange_ffn_out_start", d_w_out)
        return token

    def on_w_in_grads(gm, gs):
        d_w_in = jnp.concatenate([gm[:, 0:3072], gm[:, 6144:7168], gs[:, :2 * GDN_H], gm[:, 3072:6144],
                                  gm[:, 7168:]], axis=1)
        pending["w_in"], token = _split_start([_scatter_cols(d_w_in.astype(BF16))], "a2a", "exchange_w_in_start", gs)
        return token

    hpad = jnp.concatenate([jnp.zeros((pad, D_MODEL), F32), meta_f, x[0]], axis=0)
    tgt = jnp.concatenate([jnp.zeros((pad + N_META, D_MODEL), F32), loss_target[0]], axis=0)
    lossvec, dh0, gr = _local_step(hpad, tgt, pad, wt, first_weights, late_weights, on_ffn_out_grads, on_w_in_grads)

    loss = lax.psum(jnp.sum(lossvec), ("x", "y", "c"))
    grad_x = dh0[pad + N_META:][None]

    big_m = [m_w_in[0], m_w_out[0], m_w_ffn_up[0], m_w_ffn_down[0]]
    big_v = [v_w_in[0], v_w_out[0], v_w_ffn_up[0], v_w_ffn_down[0]]
    slabs_ffn_out = _split_wait(pending["ffn_out"], "exchange_ffn_out_wait", dh0)
    big_out = [None] + [_adamw(slabs_ffn_out[i - 1], big[i], big_m[i], big_v[i], "adamw_big_%d" % i)
                        for i in range(1, len(big))]
    g_sm = [_scatter_cols(dh0[pad:pad + N_META]), _scatter_cols(gr["gdn_conv_w"]), _scatter_cols(gr["ffn_conv_w"])]
    g_small = jnp.stack([_pack([g[d] for g in g_sm], 8) for d in range(N_DEV)])
    slabs_small, = _all_to_all([g_small], "exchange_small_gradients")
    small_out = _adamw(slabs_small, _pack(small, 8), _pack([m_meta, m_gdn_conv_w, m_ffn_conv_w], 8),
                       _pack([v_meta, v_gdn_conv_w, v_ffn_conv_w], 8), "adamw_small_sharded")
    small_un = [_unpack(o, [a.shape for a in small]) for o in small_out]
    rep_w = [norm1, gdn_a_log, gdn_dt_bias, gdn_norm, norm2, ffn_conv_b, norm_f]
    rep_m = [m_norm1, m_gdn_a_log, m_gdn_dt_bias, m_gdn_norm, m_norm2, m_ffn_conv_b, m_norm_f]
    rep_v = [v_norm1, v_gdn_a_log, v_gdn_dt_bias, v_gdn_norm, v_norm2, v_ffn_conv_b, v_norm_f]
    rep_g = [gr["norm1"], gr["a_log"], gr["dt_bias"], gr["gdn_norm"], gr["norm2"], gr["ffn_conv_b"], gr["norm_f"]]
    rep_slabs, = _all_gather([_pack(rep_g, 8)], "gather_small_gradients")
    rep_out = _adamw(rep_slabs, _pack(rep_w, 8), _pack(rep_m, 8), _pack(rep_v, 8), "adamw_replicated")
    rep_shapes = [a.shape for a in rep_w]
    rp_g, rp_d, rp_nm, rp_nv = [_unpack(o, rep_shapes) for o in rep_out]

    slabs_w_in, = _split_wait(pending["w_in"], "exchange_w_in_wait", rep_out[0])
    big_out[0] = _adamw(slabs_w_in, big[0], big_m[0], big_v[0], "adamw_big_0")
    sh_g, sh_d, sh_nm, sh_nv = [
        [small_un[j][0], big_out[0][j][None], small_un[j][1], big_out[1][j][None], big_out[2][j][None],
         small_un[j][2], big_out[3][j][None]] for j in range(4)]

    def order(sh, rp):
        return [sh[0], rp[0], sh[1], sh[2], rp[1], rp[2], rp[3], sh[3], rp[4], sh[4], sh[5], rp[5], sh[6], rp[6]]

    return (loss, grad_x, *order(sh_g, rp_g), *order(sh_d, rp_d), *order(sh_nm, rp_nm), *order(sh_nv, rp_nv))
```

```python
import functools
import math

import numpy as np
import jax
import jax.numpy as jnp
from jax import lax
from jax.experimental import pallas as pl
from jax.experimental.pallas import tpu as pltpu

F32 = jnp.float32
BF16 = jnp.bfloat16
HI = lax.Precision.HIGHEST

D_MODEL = 1024
N_META = 16
CHUNK = 64
GDN_H = 8
GDN_D = 128
RET_H = 4
RET_D = 256
D_FF = 2816
GDN_CONV = 4
FFN_CONV = 3
ROPE_BASE = 10000.0
EPS = 1e-6
N_DEV = 8
LANES = 128
MAIN_W = 10 * 1024
_O_GQ, _O_GZ, _O_GA, _O_RQ, _O_RG, _O_GATE, _O_END = 0, 3072, 4096, 4112, 7184, 8208, 10256

ADAM_LR = 0.001
ADAM_B1 = 0.9
ADAM_B2 = 0.999
ADAM_EPS = 1e-08
ADAM_WD = 0.01
ADAM_STEP = 10

MESH_T = pl.DeviceIdType.MESH


def _tile(n, target, mult):
    best = None
    for d in range(mult, min(n, target) + 1, mult):
        if n % d == 0:
            best = d
    assert best is not None, (n, target, mult)
    return best


def _sig(x):
    return 1.0 / (1.0 + jnp.exp(-x))


def _d(a, b):
    return jnp.dot(a.astype(BF16), b.astype(BF16), preferred_element_type=F32)


def _dnt(a, b):
    return lax.dot_general(a.astype(BF16), b.astype(BF16), (((1,), (1,)), ((), ())), preferred_element_type=F32)


def _dtn(a, b):
    return lax.dot_general(a.astype(BF16), b.astype(BF16), (((0,), (0,)), ((), ())), preferred_element_type=F32)


def _dx(a, b):
    return jnp.dot(a, b, preferred_element_type=F32, precision=HI)


def _dxnt(a, b):
    return lax.dot_general(a, b, (((1,), (1,)), ((), ())), preferred_element_type=F32, precision=HI)


def _dxtn(a, b):
    return lax.dot_general(a, b, (((0,), (0,)), ((), ())), preferred_element_type=F32, precision=HI)


def _split(a):
    hi = a.astype(BF16)
    return hi, (a - hi.astype(F32)).astype(BF16)


def _d3g(a, b, dims):
    ah, al = _split(a)
    bh, bl = _split(b)
    f = functools.partial(lax.dot_general, dimension_numbers=dims, preferred_element_type=F32)
    return f(ah, bh) + (f(ah, bl) + f(al, bh))


_NN = (((1,), (0,)), ((), ()))
_NT = (((1,), (1,)), ((), ()))
_TN = (((0,), (0,)), ((), ()))


def _rowsum(x):
    return jnp.sum(x, axis=1, keepdims=True)


def _allsum(x):
    return jnp.sum(jnp.sum(x, axis=1, keepdims=True), axis=0, keepdims=True)


def _mm_nn(a, b, res=None, out_dtype=F32, name="mm_nn"):
    M, K = a.shape
    N = b.shape[1]
    tm = _tile(M, 704, 16)
    tn = _tile(N, 2816, 128)

    def body(*refs):
        if res is None:
            a_ref, b_ref, o_ref = refs
        else:
            a_ref, b_ref, r_ref, o_ref = refs
        acc = jnp.dot(a_ref[...], b_ref[...], preferred_element_type=F32)
        if res is not None:
            acc = acc + r_ref[...]
        o_ref[...] = acc.astype(out_dtype)

    in_specs = [pl.BlockSpec((tm, K), lambda j, i: (i, 0)), pl.BlockSpec((K, tn), lambda j, i: (0, j))]
    args = [a, b]
    if res is not None:
        in_specs.append(pl.BlockSpec((tm, tn), lambda j, i: (i, j)))
        args.append(res)
    return pl.pallas_call(
        body, grid=(N // tn, M // tm), in_specs=in_specs,
        out_specs=pl.BlockSpec((tm, tn), lambda j, i: (i, j)),
        out_shape=jax.ShapeDtypeStruct((M, N), out_dtype), name=name)(*args)


def _mm_nt(a, b, res=None, name="mm_nt"):
    M, Nc = a.shape
    K = b.shape[0]
    tm = _tile(M, 704, 16)
    tc = _tile(Nc, 2048, 128)

    def body(*refs):
        if res is None:
            a_ref, b_ref, o_ref = refs
        else:
            a_ref, b_ref, r_ref, o_ref = refs
        c = pl.program_id(1)
        p = lax.dot_general(a_ref[...], b_ref[...], (((1,), (1,)), ((), ())), preferred_element_type=F32)

        @pl.when(c == 0)
        def _():
            if res is None:
                o_ref[...] = p
            else:
                o_ref[...] = p + r_ref[...]

        @pl.when(c > 0)
        def _():
            o_ref[...] += p

    in_specs = [pl.BlockSpec((tm, tc), lambda i, c: (i, c)), pl.BlockSpec((K, tc), lambda i, c: (0, c))]
    args = [a, b]
    if res is not None:
        in_specs.append(pl.BlockSpec((tm, K), lambda i, c: (i, 0)))
        args.append(res)
    return pl.pallas_call(
        body, grid=(M // tm, Nc // tc), in_specs=in_specs,
        out_specs=pl.BlockSpec((tm, K), lambda i, c: (i, 0)),
        out_shape=jax.ShapeDtypeStruct((M, K), F32), name=name)(*args)


def _mm_tn(a, b, name="mm_tn"):
    M, K = a.shape
    N = b.shape[1]
    tm = _tile(M, 704, 16)
    tk = _tile(K, 1408, 128)
    tn = _tile(N, 2048, 128)

    def body(a_ref, b_ref, o_ref):
        m = pl.program_id(2)
        p = lax.dot_general(a_ref[...], b_ref[...], (((0,), (0,)), ((), ())), preferred_element_type=F32)

        @pl.when(m == 0)
        def _():
            o_ref[...] = p

        @pl.when(m > 0)
        def _():
            o_ref[...] += p

    return pl.pallas_call(
        body, grid=(K // tk, N // tn, M // tm),
        in_specs=[pl.BlockSpec((tm, tk), lambda kk, j, m: (m, kk)), pl.BlockSpec((tm, tn), lambda kk, j, m: (m, j))],
        out_specs=pl.BlockSpec((tk, tn), lambda kk, j, m: (kk, j)),
        out_shape=jax.ShapeDtypeStruct((K, N), F32), name=name)(a, b)


def _rms_fwd(x, g, name):
    Lp = x.shape[0]
    tr = _tile(Lp, 256, 16)

    def body(x_ref, g_ref, o_ref):
        xv = x_ref[...]
        r = lax.rsqrt(jnp.mean(xv * xv, axis=-1, keepdims=True) + EPS)
        o_ref[...] = (xv * r * g_ref[...]).astype(BF16)

    return pl.pallas_call(
        body, grid=(Lp // tr,),
        in_specs=[pl.BlockSpec((tr, D_MODEL), lambda i: (i, 0)), pl.BlockSpec((1, D_MODEL), lambda i: (0, 0))],
        out_specs=pl.BlockSpec((tr, D_MODEL), lambda i: (i, 0)),
        out_shape=jax.ShapeDtypeStruct((Lp, D_MODEL), BF16), name=name)(x, g)


def _rms_bwd(x, g, dy, dres, pad, name):
    Lp = x.shape[0]
    tr = _tile(Lp, 256, 16)

    def body(x_ref, g_ref, dy_ref, dr_ref, dx_ref, dxb_ref, dg_ref):
        i = pl.program_id(0)
        xv = x_ref[...]
        r = lax.rsqrt(jnp.mean(xv * xv, axis=-1, keepdims=True) + EPS)
        xh = xv * r
        dyv = dy_ref[...]
        dxh = dyv * g_ref[...]
        dx = r * (dxh - xh * jnp.mean(dxh * xh, axis=-1, keepdims=True)) + dr_ref[...]
        row = i * tr + lax.broadcasted_iota(jnp.int32, (tr, 1), 0)
        dx = jnp.where(row >= pad, dx, 0.0)
        dx_ref[...] = dx
        dxb_ref[...] = dx.astype(BF16)
        part = jnp.sum(dyv * xh, axis=0, keepdims=True)

        @pl.when(i == 0)
        def _():
            dg_ref[...] = part

        @pl.when(i > 0)
        def _():
            dg_ref[...] += part

    blk = pl.BlockSpec((tr, D_MODEL), lambda i: (i, 0))
    vec = pl.BlockSpec((1, D_MODEL), lambda i: (0, 0))
    return pl.pallas_call(
        body, grid=(Lp // tr,), in_specs=[blk, vec, blk, blk], out_specs=[blk, blk, vec],
        out_shape=[jax.ShapeDtypeStruct((Lp, D_MODEL), F32), jax.ShapeDtypeStruct((Lp, D_MODEL), BF16),
                   jax.ShapeDtypeStruct((1, D_MODEL), F32)], name=name)(x, g, dy, dres)


def _final(h2, g, tgt, first_row):
    Lp = h2.shape[0]
    tr = _tile(Lp, 256, 16)

    def body(x_ref, g_ref, t_ref, loss_ref, dx_ref, dxb_ref, dg_ref):
        i = pl.program_id(0)
        xv = x_ref[...]
        gv = g_ref[...]
        r = lax.rsqrt(jnp.mean(xv * xv, axis=-1, keepdims=True) + EPS)
        xh = xv * r
        row = i * tr + lax.broadcasted_iota(jnp.int32, (tr, 1), 0)
        err = jnp.where(row >= first_row, xh * gv - t_ref[...], 0.0)
        lpart = jnp.sum(err * err, axis=0, keepdims=True) * (0.5 / D_MODEL)
        dyv = err * (1.0 / D_MODEL)
        dxh = dyv * gv
        dx = r * (dxh - xh * jnp.mean(dxh * xh, axis=-1, keepdims=True))
        dx_ref[...] = dx
        dxb_ref[...] = dx.astype(BF16)
        part = jnp.sum(dyv * xh, axis=0, keepdims=True)

        @pl.when(i == 0)
        def _():
            dg_ref[...] = part
            loss_ref[...] = lpart

        @pl.when(i > 0)
        def _():
            dg_ref[...] += part
            loss_ref[...] += lpart

    blk = pl.BlockSpec((tr, D_MODEL), lambda i: (i, 0))
    vec = pl.BlockSpec((1, D_MODEL), lambda i: (0, 0))
    return pl.pallas_call(
        body, grid=(Lp // tr,), in_specs=[blk, vec, blk], out_specs=[vec, blk, blk, vec],
        out_shape=[jax.ShapeDtypeStruct((1, D_MODEL), F32), jax.ShapeDtypeStruct((Lp, D_MODEL), F32),
                   jax.ShapeDtypeStruct((Lp, D_MODEL), BF16), jax.ShapeDtypeStruct((1, D_MODEL), F32)],
        name="final_norm_loss")(h2, g, tgt)


def _halo_prev(tr, width, col=0):
    return pl.BlockSpec((8, width), lambda i: (jnp.maximum(i * (tr // 8) - 1, 0), col))


def _halo_next(tr, width, nrows, col=0):
    last = nrows // 8 - 1
    return pl.BlockSpec((8, width), lambda i: (jnp.minimum((i + 1) * (tr // 8), last), col))


def _shifted(x, offs):
    n = x.shape[0]
    return [x if off == 0 else pltpu.roll(x, n - off, 0) for off in offs]


def _taps(wins, w, rows, bias=None):
    acc = w[0:1, :] * wins[0][0:rows, :]
    if bias is not None:
        acc = acc + bias
    for kk in range(1, len(wins)):
        acc = acc + w[kk:kk + 1, :] * wins[kk][0:rows, :]
    return acc


def _gdn_pre(proj_m, proj_s, conv_w, gparams, pad):
    Lp = proj_m.shape[0]
    tr = _tile(Lp, 192, 64)
    W3 = 3 * D_MODEL

    def body(main_ref, prev_ref, s_ref, w_ref, gp_ref, qkv_ref, gsm_ref):
        i = pl.program_id(0)
        prev = jnp.where(i > 0, prev_ref[...], 0.0)
        ext = jnp.concatenate([prev, main_ref[...]], axis=0)
        c = _taps(_shifted(ext, range(8 - (GDN_CONV - 1), 9)), w_ref[...], tr)
        s = c * _sig(c)
        scale = GDN_D ** -0.5
        for j in range(2 * GDN_H):
            seg = s[:, j * GDN_D:(j + 1) * GDN_D]
            r = lax.rsqrt(_rowsum(seg * seg) + EPS)
            if j < GDN_H:
                r = r * scale
            qkv_ref[:, j * GDN_D:(j + 1) * GDN_D] = seg * r
        qkv_ref[:, 2 * D_MODEL:] = s[:, 2 * D_MODEL:]
        sm = s_ref[...]
        gp = gp_ref[...]
        lane = lax.broadcasted_iota(jnp.int32, sm.shape, 1)
        z = sm + gp[1:2, :]
        softplus = jnp.maximum(z, 0.0) + jnp.log(1.0 + jnp.exp(-jnp.abs(z)))
        lg = -jnp.exp(gp[0:1, :]) * softplus
        row = i * tr + lax.broadcasted_iota(jnp.int32, (tr, 1), 0)
        out = jnp.where(lane < GDN_H, lg, jnp.where(lane < 2 * GDN_H, _sig(sm), 0.0))
        gsm_ref[...] = jnp.where(row >= pad, out, 0.0)

    return pl.pallas_call(
        body, grid=(Lp // tr,),
        in_specs=[pl.BlockSpec((tr, W3), lambda i: (i, 0)), _halo_prev(tr, W3),
                  pl.BlockSpec((tr, LANES), lambda i: (i, 0)),
                  pl.BlockSpec((GDN_CONV, W3), lambda i: (0, 0)), pl.BlockSpec((8, LANES), lambda i: (0, 0))],
        out_specs=[pl.BlockSpec((tr, W3), lambda i: (i, 0)), pl.BlockSpec((tr, LANES), lambda i: (i, 0))],
        out_shape=[jax.ShapeDtypeStruct((Lp, W3), F32), jax.ShapeDtypeStruct((Lp, LANES), F32)],
        name="gdn_pre")(proj_m, proj_m, proj_s, conv_w, gparams)


def _gdn_pre_bwd(proj_m, proj_s, conv_w, gparams, dq, dk, dv, dgs, pad):
    Lp = proj_m.shape[0]
    tr = _tile(Lp, 192, 64)
    W3 = 3 * D_MODEL
    te = tr + 8

    def body(main_ref, prev_ref, next_ref, s_ref, w_ref, gp_ref,
             dq_ref, dqn_ref, dk_ref, dkn_ref, dv_ref, dvn_ref, dgs_ref,
             da_ref, ds_ref, dw_ref, dgp_ref):
        i = pl.program_id(0)
        w = w_ref[...]
        prev = jnp.where(i > 0, prev_ref[...], 0.0)
        ext = jnp.concatenate([prev, main_ref[...], next_ref[...]], axis=0)
        wins = _shifted(ext, range(8 - (GDN_CONV - 1), 9))
        c = _taps(wins, w, te)
        sg = _sig(c)
        s = c * sg
        rowe = i * tr + lax.broadcasted_iota(jnp.int32, (te, 1), 0)
        live = (rowe >= pad) & (rowe < Lp)
        dqe = jnp.concatenate([dq_ref[...], dqn_ref[...]], axis=0)
        dke = jnp.concatenate([dk_ref[...], dkn_ref[...]], axis=0)
        dve = jnp.concatenate([dv_ref[...], dvn_ref[...]], axis=0)
        scale = GDN_D ** -0.5
        parts = []
        for j in range(2 * GDN_H):
            seg = s[:, j * GDN_D:(j + 1) * GDN_D]
            r = lax.rsqrt(_rowsum(seg * seg) + EPS)
            xh = seg * r
            if j < GDN_H:
                dxh = dqe[:, j * GDN_D:(j + 1) * GDN_D] * scale
            else:
                dxh = dke[:, (j - GDN_H) * GDN_D:(j - GDN_H + 1) * GDN_D]
            parts.append(r * (dxh - xh * _rowsum(dxh * xh)))
        parts.append(dve)
        dsv = jnp.concatenate(parts, axis=1)
        dc = jnp.where(live, dsv * (sg * (1.0 + c * (1.0 - sg))), 0.0)
        da_ref[...] = _taps(_shifted(dc, range(GDN_CONV - 1, -1, -1)), w, tr).astype(BF16)
        dcm = dc[0:tr, :]
        rows = [jnp.sum(dcm * wins[kk][0:tr, :], axis=0, keepdims=True) for kk in range(GDN_CONV)]
        dwp = jnp.concatenate(rows + [jnp.zeros((8 - GDN_CONV, W3), F32)], axis=0)

        sm = s_ref[...]
        gp = gp_ref[...]
        lane = lax.broadcasted_iota(jnp.int32, sm.shape, 1)
        rowm = i * tr + lax.broadcasted_iota(jnp.int32, (tr, 1), 0)
        dgv = jnp.where(rowm >= pad, dgs_ref[...], 0.0)
        dlg = jnp.where(lane < GDN_H, dgv, 0.0)
        dbt = jnp.where((lane >= GDN_H) & (lane < 2 * GDN_H), dgv, 0.0)
        z = sm + gp[1:2, :]
        softplus = jnp.maximum(z, 0.0) + jnp.log(1.0 + jnp.exp(-jnp.abs(z)))
        ea = jnp.exp(gp[0:1, :])
        dz = dlg * (-ea) * _sig(z)
        dal = dlg * (-ea) * softplus
        bt = _sig(sm)
        dgb = dbt * bt * (1.0 - bt)
        ds_ref[...] = (dz + dgb).astype(BF16)
        gpp = jnp.concatenate([jnp.sum(dal, axis=0, keepdims=True), jnp.sum(dz, axis=0, keepdims=True),
                               jnp.zeros((6, LANES), F32)], axis=0)

        @pl.when(i == 0)
        def _():
            dw_ref[...] = dwp
            dgp_ref[...] = gpp

        @pl.when(i > 0)
        def _():
            dw_ref[...] += dwp
            dgp_ref[...] += gpp

    m3 = pl.BlockSpec((tr, W3), lambda i: (i, 0))
    m1 = pl.BlockSpec((tr, D_MODEL), lambda i: (i, 0))
    n1 = _halo_next(tr, D_MODEL, Lp)
    return pl.pallas_call(
        body, grid=(Lp // tr,),
        in_specs=[m3, _halo_prev(tr, W3), _halo_next(tr, W3, Lp), pl.BlockSpec((tr, LANES), lambda i: (i, 0)),
                  pl.BlockSpec((GDN_CONV, W3), lambda i: (0, 0)), pl.BlockSpec((8, LANES), lambda i: (0, 0)),
                  m1, n1, m1, n1, m1, n1, pl.BlockSpec((tr, LANES), lambda i: (i, 0))],
        out_specs=[m3, pl.BlockSpec((tr, LANES), lambda i: (i, 0)),
                   pl.BlockSpec((8, W3), lambda i: (0, 0)), pl.BlockSpec((8, LANES), lambda i: (0, 0))],
        out_shape=[jax.ShapeDtypeStruct((Lp, W3), BF16), jax.ShapeDtypeStruct((Lp, LANES), BF16),
                   jax.ShapeDtypeStruct((8, W3), F32), jax.ShapeDtypeStruct((8, LANES), F32)],
        name="gdn_pre_bwd")(proj_m, proj_m, proj_m, proj_s, conv_w, gparams, dq, dq, dk, dk, dv, dv, dgs)


def _gdn_gates(gs):
    ri = lax.broadcasted_iota(jnp.int32, (CHUNK, CHUNK), 0)
    ci = lax.broadcasted_iota(jnp.int32, (CHUNK, CHUNK), 1)
    tril = ri >= ci
    strict = ri > ci
    gall = _dx(tril.astype(F32), gs)
    lane8 = lax.broadcasted_iota(jnp.int32, (8, LANES), 1)
    sub8 = lax.broadcasted_iota(jnp.int32, (8, LANES), 0)
    grow = _dxnt((lane8 == sub8).astype(F32), gall)
    return gall, grow, tril, strict


def _gdn_decay(gall, grow, tril, h):
    g = gall[:, h:h + 1]
    return g, jnp.where(tril, jnp.exp(jnp.where(tril, g - grow[h:h + 1, :], 0.0)), 0.0)


def _gdn_chunk_specs(N, rev):
    cn = (lambda n: N - 1 - n) if rev else (lambda n: n)
    col = lambda j: pl.BlockSpec((CHUNK, D_MODEL), lambda n: (cn(n), j))
    gate = pl.BlockSpec((CHUNK, LANES), lambda n: (cn(n), 0))
    st = lambda a, b: pl.BlockSpec((GDN_H, None, a, b), lambda n: (0, cn(n), 0, 0))
    return col, gate, st


def _gdn_chunk_fwd(qkv, gsm):
    Lp = qkv.shape[0]
    N = Lp // CHUNK

    def body(q_ref, k_ref, v_ref, gs_ref, o_ref, sin_ref, t_ref, S):
        n = pl.program_id(0)

        @pl.when(n == 0)
        def _():
            S[...] = jnp.zeros_like(S)

        gs = gs_ref[...]
        gall, grow, tril, strict = _gdn_gates(gs)
        ri = lax.broadcasted_iota(jnp.int32, (CHUNK, CHUNK), 0)
        ci = lax.broadcasted_iota(jnp.int32, (CHUNK, CHUNK), 1)
        eye = (ri == ci).astype(F32)
        heads = range(GDN_H)
        sls = [slice(h * GDN_D, (h + 1) * GDN_D) for h in heads]
        q = [q_ref[:, sl] for sl in sls]
        k = [k_ref[:, sl] for sl in sls]
        v = [v_ref[:, sl] for sl in sls]
        s0 = [S[h] for h in heads]
        beta = [gs[:, GDN_H + h:GDN_H + h + 1] for h in heads]
        gg = [_gdn_decay(gall, grow, tril, h) for h in heads]
        g = [x[0] for x in gg]
        gam = [x[1] for x in gg]
        eg = [jnp.exp(g[h]) for h in heads]
        gl = [g[h][CHUNK - 1:CHUNK, :] for h in heads]
        kb = [k[h] * beta[h] for h in heads]
        pw = [-jnp.where(strict, _dnt(kb[h], k[h]) * gam[h], 0.0) for h in heads]
        p = [_dnt(q[h], k[h]) * gam[h] for h in heads]
        qs = [_d(q[h] * eg[h], s0[h]) for h in heads]
        t = [eye + pw[h] for h in heads]
        for _ in range(5):
            pw = [_d3g(pw[h], pw[h], _NN) for h in heads]
            t = [t[h] + _d3g(t[h], pw[h], _NN) for h in heads]
        u = [_d(t[h], v[h] * beta[h]) for h in heads]
        w = [_d(t[h], kb[h] * eg[h]) for h in heads]
        vnew = [u[h] - _d(w[h], s0[h]) for h in heads]
        for h in heads:
            o_ref[:, sls[h]] = qs[h] + _d(p[h], vnew[h])
            sin_ref[h] = s0[h]
            t_ref[h] = t[h]
            S[h] = s0[h] * jnp.exp(gl[h]) + _dtn(k[h] * jnp.exp(gl[h] - g[h]), vnew[h])

    col, gate, st = _gdn_chunk_specs(N, False)
    return pl.pallas_call(
        body, grid=(N,),
        in_specs=[col(0), col(1), col(2), gate],
        out_specs=[col(0), st(GDN_D, GDN_D), st(CHUNK, CHUNK)],
        out_shape=[jax.ShapeDtypeStruct((Lp, D_MODEL), F32), jax.ShapeDtypeStruct((GDN_H, N, GDN_D, GDN_D), F32),
                   jax.ShapeDtypeStruct((GDN_H, N, CHUNK, CHUNK), F32)],
        scratch_shapes=[pltpu.VMEM((GDN_H, GDN_D, GDN_D), F32)],
        name="gdn_chunk_fwd")(qkv, qkv, qkv, gsm)


def _gdn_chunk_bwd(qkv, gsm, do, s_in, t_in):
    Lp = qkv.shape[0]
    N = Lp // CHUNK

    def body(q_ref, k_ref, v_ref, gs_ref, do_ref, sin_ref, t_ref, dq_ref, dk_ref, dv_ref, dgs_ref, dS):
        n = pl.program_id(0)

        @pl.when(n == 0)
        def _():
            dS[...] = jnp.zeros_like(dS)

        gs = gs_ref[...]
        gall, grow, tril, strict = _gdn_gates(gs)
        lane = lax.broadcasted_iota(jnp.int32, (CHUNK, LANES), 1)
        rcol = lax.broadcasted_iota(jnp.int32, (CHUNK, 1), 0)
        ones = jnp.ones((CHUNK, LANES), F32)
        dg_all = jnp.zeros((CHUNK, LANES), F32)
        dbeta_all = jnp.zeros((CHUNK, LANES), F32)
        heads = range(GDN_H)
        sls = [slice(h * GDN_D, (h + 1) * GDN_D) for h in heads]
        H = lambda f: [f(h) for h in heads]
        q = H(lambda h: q_ref[:, sls[h]])
        k = H(lambda h: k_ref[:, sls[h]])
        v = H(lambda h: v_ref[:, sls[h]])
        dov = H(lambda h: do_ref[:, sls[h]])
        s0 = H(lambda h: sin_ref[h])
        t = H(lambda h: t_ref[h])
        dsv = H(lambda h: dS[h])
        beta = H(lambda h: gs[:, GDN_H + h:GDN_H + h + 1])
        gg = H(lambda h: _gdn_decay(gall, grow, tril, h))
        g = [x[0] for x in gg]
        gam = [x[1] for x in gg]
        eg = H(lambda h: jnp.exp(g[h]))
        egl = H(lambda h: jnp.exp(g[h][CHUNK - 1:CHUNK, :]))
        e = H(lambda h: jnp.exp(g[h][CHUNK - 1:CHUNK, :] - g[h]))
        kb = H(lambda h: k[h] * beta[h])
        kbg = H(lambda h: kb[h] * eg[h])
        vb = H(lambda h: v[h] * beta[h])
        qg = H(lambda h: q[h] * eg[h])
        kd = H(lambda h: k[h] * e[h])
        m = H(lambda h: jnp.where(strict, _dnt(kb[h], k[h]) * gam[h], 0.0))
        u = H(lambda h: _d(t[h], vb[h]))
        w = H(lambda h: _d(t[h], kbg[h]))
        p = H(lambda h: _dnt(q[h], k[h]) * gam[h])
        dqg = H(lambda h: _dnt(dov[h], s0[h]))
        kdds = H(lambda h: _d(kd[h], dsv[h]))
        qgdo = H(lambda h: _dtn(qg[h], dov[h]))
        vnew = H(lambda h: u[h] - _d(w[h], s0[h]))
        dvnew = H(lambda h: _dtn(p[h], dov[h]) + kdds[h])
        dp = H(lambda h: jnp.where(tril, _dnt(dov[h], vnew[h]), 0.0))
        dkd = H(lambda h: _dnt(vnew[h], dsv[h]))
        dw = H(lambda h: -_dnt(dvnew[h], s0[h]))
        for h in heads:
            dS[h] = qgdo[h] + egl[h] * dsv[h] - _dtn(w[h], dvnew[h])
        dvb = H(lambda h: _dtn(t[h], dvnew[h]))
        dkbg = H(lambda h: _dtn(t[h], dw[h]))
        dt = H(lambda h: _dnt(dvnew[h], vb[h]) + _dnt(dw[h], kbg[h]))
        x1 = H(lambda h: _d3g(t[h], dt[h], _TN))
        dm = H(lambda h: jnp.where(strict, -_d3g(x1[h], t[h], _NT), 0.0))
        dkk = H(lambda h: dm[h] * gam[h])
        dqk = H(lambda h: dp[h] * gam[h])
        dkb = H(lambda h: _d(dkk[h], k[h]) + eg[h] * dkbg[h])
        em = H(lambda h: dm[h] * m[h] + dp[h] * p[h])
        colsum = H(lambda h: _d3g(em[h], ones, _TN)[:, 0:1])
        for h in heads:
            dk_ref[:, sls[h]] = _dtn(dkk[h], kb[h]) + _dtn(dqk[h], q[h]) + dkd[h] * e[h] + beta[h] * dkb[h]
            dq_ref[:, sls[h]] = _d(dqk[h], k[h]) + dqg[h] * eg[h]
            dv_ref[:, sls[h]] = beta[h] * dvb[h]
        for h in heads:
            dbeta = _rowsum(k[h] * dkb[h]) + _rowsum(v[h] * dvb[h])
            z = _rowsum(kd[h] * dkd[h])
            dg = _rowsum(em[h]) - colsum[h] + _rowsum(qg[h] * dqg[h]) + _rowsum(kbg[h] * dkbg[h]) - z
            extra = _allsum(z) + egl[h] * _allsum(s0[h] * dsv[h])
            dg = dg + jnp.where(rcol == CHUNK - 1, extra, 0.0)
            dg_all = dg_all + jnp.where(lane == h, dg, 0.0)
            dbeta_all = dbeta_all + jnp.where(lane == GDN_H + h, dbeta, 0.0)
        ri = lax.broadcasted_iota(jnp.int32, (CHUNK, CHUNK), 0)
        ci = lax.broadcasted_iota(jnp.int32, (CHUNK, CHUNK), 1)
        dgs_ref[...] = _dx((ci >= ri).astype(F32), dg_all) + dbeta_all

    col, gate, st = _gdn_chunk_specs(N, True)
    return pl.pallas_call(
        body, grid=(N,),
        in_specs=[col(0), col(1), col(2), gate, col(0), st(GDN_D, GDN_D), st(CHUNK, CHUNK)],
        out_specs=[col(0), col(0), col(0), gate],
        out_shape=[jax.ShapeDtypeStruct((Lp, D_MODEL), F32)] * 3 + [jax.ShapeDtypeStruct((Lp, LANES), F32)],
        scratch_shapes=[pltpu.VMEM((GDN_H, GDN_D, GDN_D), F32)],
        name="gdn_chunk_bwd")(qkv, qkv, qkv, gsm, do, s_in, t_in)


def _rot(x, c, s):
    half = RET_D // 2
    x1 = x[:, :half]
    x2 = x[:, half:]
    return jnp.concatenate([x1 * c - x2 * s, x2 * c + x1 * s], axis=1)


def _rot_bwd(d, c, s):
    half = RET_D // 2
    d1 = d[:, :half]
    d2 = d[:, half:]
    return jnp.concatenate([d1 * c + d2 * s, d2 * c - d1 * s], axis=1)


def _ret_tables():
    hh = jnp.arange(RET_H, dtype=F32)
    lg = jnp.log(1.0 - 2.0 ** (-5.0 - hh))
    idx = jnp.arange(CHUNK, dtype=F32)
    tril = jnp.asarray(np.tril(np.ones((CHUNK, CHUNK), dtype=bool)))
    dmask = jnp.where(tril, jnp.exp((idx[:, None] - idx[None, :]) * lg[:, None, None]), 0.0)
    qdec = jnp.exp((idx[None, :] + 1.0) * lg[:, None])
    kdec = jnp.exp((CHUNK - 1.0 - idx[None, :]) * lg[:, None])
    gch = jnp.exp(CHUNK * lg)
    qdec = jnp.broadcast_to(qdec[:, :, None], (RET_H, CHUNK, RET_D))
    kdec = jnp.broadcast_to(kdec[:, :, None], (RET_H, CHUNK, RET_D))
    gch = jnp.broadcast_to(gch[:, None, None], (RET_H, 8, LANES))
    return dmask, qdec, kdec, gch


def _ret_specs(N, rev):
    cn = (lambda n: N - 1 - n) if rev else (lambda n: n)
    col = lambda j: pl.BlockSpec((CHUNK, D_MODEL), lambda n: (cn(n), j))
    tab = lambda a, b: pl.BlockSpec((RET_H, a, b), lambda n: (0, 0, 0))
    rope = pl.BlockSpec((CHUNK, LANES), lambda n: (cn(n), 0))
    st = pl.BlockSpec((RET_H, None, RET_D, RET_D), lambda n: (0, cn(n), 0, 0))
    return col, tab, rope, st


def _ret_chunk_fwd(proj_m, cos, sin, tables):
    Lp = proj_m.shape[0]
    N = Lp // CHUNK
    dmask, qdec, kdec, gch = tables

    def body(q_ref, k_ref, v_ref, c_ref, s_ref, dm_ref, qd_ref, kd_ref, g_ref, o_ref, sin_ref, S):
        n = pl.program_id(0)

        @pl.when(n == 0)
        def _():
            S[...] = jnp.zeros_like(S)

        c = c_ref[...]
        s = s_ref[...]
        heads = range(RET_H)
        sls = [slice(h * RET_D, (h + 1) * RET_D) for h in heads]
        H = lambda f: [f(h) for h in heads]
        qr = H(lambda h: _rot(q_ref[:, sls[h]], c, s))
        ks = H(lambda h: _rot(k_ref[:, sls[h]], c, s) * (RET_D ** -0.5))
        v = H(lambda h: v_ref[:, sls[h]])
        s0 = H(lambda h: S[h])
        a = H(lambda h: _dnt(qr[h], ks[h]) * dm_ref[h])
        qs = H(lambda h: _d(qr[h] * qd_ref[h], s0[h]))
        kv = H(lambda h: _dtn(ks[h] * kd_ref[h], v[h]))
        for h in heads:
            o_ref[:, sls[h]] = _d(a[h], v[h]) + qs[h]
            sin_ref[h] = s0[h]
            S[h] = s0[h] * g_ref[h, 0:1, 0:1] + kv[h]

    col, tab, rope, st = _ret_specs(N, False)
    return pl.pallas_call(
        body, grid=(N,),
        in_specs=[col(3), col(4), col(5), rope, rope,
                  tab(CHUNK, CHUNK), tab(CHUNK, RET_D), tab(CHUNK, RET_D), tab(8, LANES)],
        out_specs=[col(0), st],
        out_shape=[jax.ShapeDtypeStruct((Lp, D_MODEL), F32), jax.ShapeDtypeStruct((RET_H, N, RET_D, RET_D), F32)],
        scratch_shapes=[pltpu.VMEM((RET_H, RET_D, RET_D), F32)],
        name="ret_chunk_fwd")(proj_m, proj_m, proj_m, cos, sin, dmask, qdec, kdec, gch)


def _ret_chunk_bwd(proj_m, cos, sin, tables, do, s_in):
    Lp = proj_m.shape[0]
    N = Lp // CHUNK
    dmask, qdec, kdec, gch = tables

    def body(q_ref, k_ref, v_ref, c_ref, s_ref, dm_ref, qd_ref, kd_ref, g_ref, do_ref, sin_ref,
             dq_ref, dk_ref, dv_ref, dS):
        n = pl.program_id(0)

        @pl.when(n == 0)
        def _():
            dS[...] = jnp.zeros_like(dS)

        c = c_ref[...]
        s = s_ref[...]
        kscale = RET_D ** -0.5
        heads = range(RET_H)
        sls = [slice(h * RET_D, (h + 1) * RET_D) for h in heads]
        H = lambda f: [f(h) for h in heads]
        qr = H(lambda h: _rot(q_ref[:, sls[h]], c, s))
        ks = H(lambda h: _rot(k_ref[:, sls[h]], c, s) * kscale)
        v = H(lambda h: v_ref[:, sls[h]])
        dov = H(lambda h: do_ref[:, sls[h]])
        s0 = H(lambda h: sin_ref[h])
        dsv = H(lambda h: dS[h])
        ad = H(lambda h: _dnt(qr[h], ks[h]) * dm_ref[h])
        da = H(lambda h: _dnt(dov[h], v[h]) * dm_ref[h])
        kds = H(lambda h: _d(ks[h] * kd_ref[h], dsv[h]))
        dos = H(lambda h: _dnt(dov[h], s0[h]) * qd_ref[h])
        vds = H(lambda h: _dnt(v[h], dsv[h]) * kd_ref[h])
        qdo = H(lambda h: _dtn(qr[h] * qd_ref[h], dov[h]))
        for h in heads:
            dS[h] = dsv[h] * g_ref[h, 0:1, 0:1] + qdo[h]
        for h in heads:
            dv_ref[:, sls[h]] = (_dtn(ad[h], dov[h]) + kds[h]).astype(BF16)
            dq_ref[:, sls[h]] = _rot_bwd(_d(da[h], ks[h]) + dos[h], c, s).astype(BF16)
            dk_ref[:, sls[h]] = _rot_bwd((_dtn(da[h], qr[h]) + vds[h]) * kscale, c, s).astype(BF16)

    col, tab, rope, st = _ret_specs(N, True)
    return pl.pallas_call(
        body, grid=(N,),
        in_specs=[col(3), col(4), col(5), rope, rope,
                  tab(CHUNK, CHUNK), tab(CHUNK, RET_D), tab(CHUNK, RET_D), tab(8, LANES), col(0), st],
        out_specs=[col(0), col(0), col(0)],
        out_shape=[jax.ShapeDtypeStruct((Lp, D_MODEL), BF16)] * 3,
        scratch_shapes=[pltpu.VMEM((RET_H, RET_D, RET_D), F32)],
        name="ret_chunk_bwd")(proj_m, proj_m, proj_m, cos, sin, dmask, qdec, kdec, gch, do, s_in)


def _merge_specs(tr):
    col = lambda j: pl.BlockSpec((tr, D_MODEL), lambda i: (i, j))
    return col


def _merge_fwd(o_a, o_b, proj_m, gnorm):
    Lp = o_a.shape[0]
    tr = _tile(Lp, 192, 16)

    def body(oa_ref, ob_ref, gz_ref, rg_ref, ga_ref, gb_ref, gn_ref, y_ref):
        gn = gn_ref[...]
        oa = oa_ref[...]
        ob = ob_ref[...]
        gz = gz_ref[...]
        ya = []
        for j in range(GDN_H):
            seg = oa[:, j * GDN_D:(j + 1) * GDN_D]
            r = lax.rsqrt(jnp.mean(seg * seg, axis=-1, keepdims=True) + EPS)
            ya.append(seg * r * gn)
        ya = jnp.concatenate(ya, axis=1) * (gz * _sig(gz))
        yb = []
        for j in range(RET_H):
            seg = ob[:, j * RET_D:(j + 1) * RET_D]
            r = lax.rsqrt(jnp.mean(seg * seg, axis=-1, keepdims=True) + EPS)
            yb.append(seg * r)
        rg = rg_ref[...]
        yb = jnp.concatenate(yb, axis=1) * (rg * _sig(rg))
        y_ref[...] = (_sig(ga_ref[...]) * ya + _sig(gb_ref[...]) * yb).astype(BF16)

    col = _merge_specs(tr)
    return pl.pallas_call(
        body, grid=(Lp // tr,),
        in_specs=[col(0), col(0), col(6), col(7), col(8), col(9), pl.BlockSpec((1, GDN_D), lambda i: (0, 0))],
        out_specs=col(0), out_shape=jax.ShapeDtypeStruct((Lp, D_MODEL), BF16),
        name="merge_fwd")(o_a, o_b, proj_m, proj_m, proj_m, proj_m, gnorm)


def _merge_bwd(dy, o_a, o_b, proj_m, gnorm):
    Lp = o_a.shape[0]
    tr = _tile(Lp, 192, 16)

    def body(dy_ref, oa_ref, ob_ref, gz_ref, rg_ref, ga_ref, gb_ref, gn_ref, dc_ref, doa_ref, dob_ref, dgn_ref):
        i = pl.program_id(0)
        gn = gn_ref[...]
        dyv = dy_ref[...]
        oa = oa_ref[...]
        ob = ob_ref[...]
        gz = gz_ref[...]
        rg = rg_ref[...]
        sa = _sig(ga_ref[...])
        sb = _sig(gb_ref[...])
        dya = dyv * sa
        dyb = dyv * sb
        sgz = _sig(gz)
        szz = gz * sgz
        dgn = jnp.zeros((1, GDN_D), F32)
        ya = []
        dgz = []
        for j in range(GDN_H):
            sl = slice(j * GDN_D, (j + 1) * GDN_D)
            seg = oa[:, sl]
            r = lax.rsqrt(jnp.mean(seg * seg, axis=-1, keepdims=True) + EPS)
            xh = seg * r
            oan = xh * gn
            ya.append(oan * szz[:, sl])
            dgz.append(dya[:, sl] * oan * (sgz[:, sl] * (1.0 + gz[:, sl] * (1.0 - sgz[:, sl]))))
            doan = dya[:, sl] * szz[:, sl]
            dgn = dgn + jnp.sum(doan * xh, axis=0, keepdims=True)
            dxh = doan * gn
            doa_ref[:, sl] = r * (dxh - xh * jnp.mean(dxh * xh, axis=-1, keepdims=True))
        ya = jnp.concatenate(ya, axis=1)
        srg = _sig(rg)
        srr = rg * srg
        yb = []
        drg = []
        for j in range(RET_H):
            sl = slice(j * RET_D, (j + 1) * RET_D)
            seg = ob[:, sl]
            r = lax.rsqrt(jnp.mean(seg * seg, axis=-1, keepdims=True) + EPS)
            xh = seg * r
            yb.append(xh * srr[:, sl])
            drg.append(dyb[:, sl] * xh * (srg[:, sl] * (1.0 + rg[:, sl] * (1.0 - srg[:, sl]))))
            dxh = dyb[:, sl] * srr[:, sl]
            dob_ref[:, sl] = r * (dxh - xh * jnp.mean(dxh * xh, axis=-1, keepdims=True))
        yb = jnp.concatenate(yb, axis=1)
        dc_ref[:, 0:D_MODEL] = jnp.concatenate(dgz, axis=1).astype(BF16)
        dc_ref[:, D_MODEL:2 * D_MODEL] = jnp.concatenate(drg, axis=1).astype(BF16)
        dc_ref[:, 2 * D_MODEL:3 * D_MODEL] = (dyv * ya * sa * (1.0 - sa)).astype(BF16)
        dc_ref[:, 3 * D_MODEL:] = (dyv * yb * sb * (1.0 - sb)).astype(BF16)

        @pl.when(i == 0)
        def _():
            dgn_ref[...] = dgn

        @pl.when(i > 0)
        def _():
            dgn_ref[...] += dgn

    col = _merge_specs(tr)
    return pl.pallas_call(
        body, grid=(Lp // tr,),
        in_specs=[col(0), col(0), col(0), col(6), col(7), col(8), col(9), pl.BlockSpec((1, GDN_D), lambda i: (0, 0))],
        out_specs=[pl.BlockSpec((tr, 4 * D_MODEL), lambda i: (i, 0)), col(0), col(0),
                   pl.BlockSpec((1, GDN_D), lambda i: (0, 0))],
        out_shape=[jax.ShapeDtypeStruct((Lp, 4 * D_MODEL), BF16), jax.ShapeDtypeStruct((Lp, D_MODEL), F32),
                   jax.ShapeDtypeStruct((Lp, D_MODEL), F32), jax.ShapeDtypeStruct((1, GDN_D), F32)],
        name="merge_bwd")(dy, o_a, o_b, proj_m, proj_m, proj_m, proj_m, gnorm)


def _ffn_act(up, conv_w, conv_b):
    Lp = up.shape[0]
    tr = _tile(Lp, 192, 16)
    W2 = 2 * D_FF

    def body(main_ref, prev_ref, w_ref, b_ref, act_ref):
        i = pl.program_id(0)
        prev = jnp.where(i > 0, prev_ref[...], 0.0)
        ext = jnp.concatenate([prev, main_ref[...]], axis=0)
        u = _taps(_shifted(ext, range(8 - (FFN_CONV - 1), 9)), w_ref[...], tr, b_ref[...])
        a = u[:, :D_FF]
        act_ref[...] = (a * _sig(a) * u[:, D_FF:]).astype(BF16)

    return pl.pallas_call(
        body, grid=(Lp // tr,),
        in_specs=[pl.BlockSpec((tr, W2), lambda i: (i, 0)), _halo_prev(tr, W2),
                  pl.BlockSpec((FFN_CONV, W2), lambda i: (0, 0)), pl.BlockSpec((1, W2), lambda i: (0, 0))],
        out_specs=pl.BlockSpec((tr, D_FF), lambda i: (i, 0)),
        out_shape=jax.ShapeDtypeStruct((Lp, D_FF), BF16), name="ffn_act")(up, up, conv_w, conv_b)


def _ffn_act_bwd(up, dact, conv_w, conv_b):
    Lp = up.shape[0]
    tr = _tile(Lp, 96, 16)
    W2 = 2 * D_FF
    te = tr + 8

    def body(main_ref, prev_ref, next_ref, da_ref, dan_ref, w_ref, b_ref, dup_ref, acc_ref):
        i = pl.program_id(0)
        w = w_ref[...]
        prev = jnp.where(i > 0, prev_ref[...], 0.0)
        ext = jnp.concatenate([prev, main_ref[...], next_ref[...]], axis=0)
        wins = _shifted(ext, range(8 - (FFN_CONV - 1), 9))
        u = _taps(wins, w, te, b_ref[...])
        a = u[:, :D_FF]
        b = u[:, D_FF:]
        rowe = i * tr + lax.broadcasted_iota(jnp.int32, (te, 1), 0)
        dae = jnp.where(rowe < Lp, jnp.concatenate([da_ref[...], dan_ref[...]], axis=0), 0.0)
        sg = _sig(a)
        du = jnp.concatenate([dae * b * (sg * (1.0 + a * (1.0 - sg))), dae * (a * sg)], axis=1)
        dup_ref[...] = _taps(_shifted(du, range(FFN_CONV - 1, -1, -1)), w, tr).astype(BF16)
        dum = du[0:tr, :]
        rows = [jnp.sum(dum * wins[kk][0:tr, :], axis=0, keepdims=True) for kk in range(FFN_CONV)]
        rows.append(jnp.sum(dum, axis=0, keepdims=True))
        part = jnp.concatenate(rows + [jnp.zeros((8 - len(rows), W2), F32)], axis=0)

        @pl.when(i == 0)
        def _():
            acc_ref[...] = part

        @pl.when(i > 0)
        def _():
            acc_ref[...] += part

    return pl.pallas_call(
        body, grid=(Lp // tr,),
        in_specs=[pl.BlockSpec((tr, W2), lambda i: (i, 0)), _halo_prev(tr, W2), _halo_next(tr, W2, Lp),
                  pl.BlockSpec((tr, D_FF), lambda i: (i, 0)), _halo_next(tr, D_FF, Lp),
                  pl.BlockSpec((FFN_CONV, W2), lambda i: (0, 0)), pl.BlockSpec((1, W2), lambda i: (0, 0))],
        out_specs=[pl.BlockSpec((tr, W2), lambda i: (i, 0)), pl.BlockSpec((8, W2), lambda i: (0, 0))],
        out_shape=[jax.ShapeDtypeStruct((Lp, W2), BF16), jax.ShapeDtypeStruct((8, W2), F32)],
        name="ffn_act_bwd")(up, up, up, dact, dact, conv_w, conv_b)


def _local_step(hpad, tgt, pad, wt, first_weights=None, late_weights=None, on_ffn_out_grads=None,
                on_w_in_grads=None):
    Lp = hpad.shape[0]
    first = pad + N_META
    pos = jnp.arange(Lp, dtype=F32) - float(pad)
    half = RET_D // 2
    inv = 1.0 / (ROPE_BASE ** (jnp.arange(half, dtype=F32) / half))
    ang = pos[:, None] * inv[None, :]
    cos, sin = jnp.cos(ang), jnp.sin(ang)
    tables = _ret_tables()
    gparams = jnp.zeros((8, LANES), F32).at[0, :GDN_H].set(wt["a_log"]).at[1, :GDN_H].set(wt["dt_bias"])

    hn1 = _rms_fwd(hpad, wt["norm1"], "rms1_fwd")
    if first_weights is not None:
        wt = {**wt, **first_weights(hn1)}
    proj_m = _mm_nn(hn1, wt["w_main"], name="proj_main")
    proj_s = _mm_nn(hn1, wt["w_small"], name="proj_small")
    qkv, gsm = _gdn_pre(proj_m, proj_s, wt["gdn_conv_w"], gparams, pad)
    o_a, s_a, t_a = _gdn_chunk_fwd(qkv, gsm)
    o_b, s_b = _ret_chunk_fwd(proj_m, cos, sin, tables)
    y = _merge_fwd(o_a, o_b, proj_m, wt["gdn_norm"])
    if late_weights is not None:
        wt = {**wt, **late_weights(y)}
    h1 = _mm_nn(y, wt["w_out"], res=hpad, name="out_proj")
    hn2 = _rms_fwd(h1, wt["norm2"], "rms2_fwd")
    up = _mm_nn(hn2, wt["w_up"], name="ffn_up")
    act = _ffn_act(up, wt["ffn_conv_w"], wt["ffn_conv_b"])
    h2 = _mm_nn(act, wt["w_down"], res=h1, name="ffn_down")
    lossvec, dh2, dh2b, d_norm_f = _final(h2, wt["norm_f"], tgt, first)

    d_w_down = _mm_tn(act, dh2b, name="dw_down")
    dact = _mm_nt(dh2b, wt["w_down"], name="d_act")
    dup, ffn_rows = _ffn_act_bwd(up, dact, wt["ffn_conv_w"], wt["ffn_conv_b"])
    d_w_up = _mm_tn(hn2, dup, name="dw_up")
    dhn2 = _mm_nt(dup, wt["w_up"], name="d_hn2")
    dh1, dh1b, d_norm2 = _rms_bwd(h1, wt["norm2"], dhn2, dh2, pad, "rms2_bwd")

    d_w_out = _mm_tn(y, dh1b, name="dw_out")
    dy = _mm_nt(dh1b, wt["w_out"], name="d_y")
    gnorm = wt["gdn_norm"]
    if on_ffn_out_grads is not None:
        gnorm = gnorm + on_ffn_out_grads(d_w_down, d_w_up, d_w_out)[0:1, :]
    d_c, do_a, do_b, d_gnorm = _merge_bwd(dy, o_a, o_b, proj_m, gnorm)
    drq, drk, drv = _ret_chunk_bwd(proj_m, cos, sin, tables, do_b, s_b)
    dq, dk, dv, dgs = _gdn_chunk_bwd(qkv, gsm, do_a, s_a, t_a)
    d_a, d_s, conv_rows, gp_rows = _gdn_pre_bwd(proj_m, proj_s, wt["gdn_conv_w"], gparams, dq, dk, dv, dgs, pad)

    wm = wt["w_main"]
    segs = [(d_a, 0, 3 * D_MODEL), (drq, 3 * D_MODEL, D_MODEL), (drk, 4 * D_MODEL, D_MODEL),
            (drv, 5 * D_MODEL, D_MODEL), (d_c, 6 * D_MODEL, 4 * D_MODEL)]
    d_w_main = jnp.concatenate([_mm_tn(hn1, d, name="dw_in_%d" % i) for i, (d, _, _) in enumerate(segs)], axis=1)
    d_w_small = _mm_tn(hn1, d_s, name="dw_in_small")
    w_small = wt["w_small"]
    if on_w_in_grads is not None:
        w_small = w_small + on_w_in_grads(d_w_main, d_w_small)[0:1, :].astype(w_small.dtype)
    dhn1 = _mm_nt(d_s, w_small, name="d_hn1_small")
    for i, (d, off, width) in enumerate(segs):
        dhn1 = _mm_nt(d, wm[:, off:off + width], res=dhn1, name="d_hn1_%d" % i)
    dh0, _, d_norm1 = _rms_bwd(hpad, wt["norm1"], dhn1, dh1, pad, "rms1_bwd")

    grads = {
        "norm1": d_norm1, "w_main": d_w_main, "w_small": d_w_small, "gdn_conv_w": conv_rows[:GDN_CONV],
        "a_log": gp_rows[0, :GDN_H], "dt_bias": gp_rows[1, :GDN_H], "gdn_norm": d_gnorm, "w_out": d_w_out,
        "norm2": d_norm2, "w_up": d_w_up, "ffn_conv_w": ffn_rows[:FFN_CONV], "ffn_conv_b": ffn_rows[FFN_CONV:FFN_CONV + 1],
        "w_down": d_w_down, "norm_f": d_norm_f,
    }
    return lossvec, dh0, grads


def _peer(k):
    ix, iy, ic = lax.axis_index("x"), lax.axis_index("y"), lax.axis_index("c")
    px = 1 - ix if (k >> 2) & 1 else ix
    py = 1 - iy if (k >> 1) & 1 else iy
    pc = 1 - ic if k & 1 else ic
    return (px, py, pc), 4 * px + 2 * py + pc


def _comm_call(body, n, out_shapes, name, args):
    hbm = pl.BlockSpec(memory_space=pl.ANY)
    return pl.pallas_call(
        body, out_shape=out_shapes, in_specs=[hbm] * n, out_specs=[hbm] * n,
        scratch_shapes=[pltpu.SemaphoreType.DMA((n, N_DEV - 1)), pltpu.SemaphoreType.DMA((n, N_DEV - 1)),
                        pltpu.SemaphoreType.DMA((n,))],
        name=name)(*args)


def _all_gather(xs, name):
    n = len(xs)

    def body(*refs):
        x_refs, out_refs = refs[:n], refs[n:2 * n]
        send_sems, recv_sems, local_sems = refs[2 * n:]
        _, me = _peer(0)
        pending = []
        for i in range(n):
            local = pltpu.make_async_copy(x_refs[i], out_refs[i].at[me], local_sems.at[i])
            local.start()
            pending.append(local)
        sends = []
        for i in range(n):
            for k in range(1, N_DEV):
                dev, _ = _peer(k)
                cp = pltpu.make_async_remote_copy(
                    src_ref=x_refs[i], dst_ref=out_refs[i].at[me], send_sem=send_sems.at[i, k - 1],
                    recv_sem=recv_sems.at[i, k - 1], device_id=dev, device_id_type=MESH_T)
                cp.start()
                sends.append(cp)
        for i in range(n):
            for k in range(1, N_DEV):
                dev, idx = _peer(k)
                pltpu.make_async_remote_copy(
                    src_ref=x_refs[i], dst_ref=out_refs[i].at[idx], send_sem=send_sems.at[i, k - 1],
                    recv_sem=recv_sems.at[i, k - 1], device_id=dev, device_id_type=MESH_T).wait_recv()
        for cp in sends:
            cp.wait_send()
        for local in pending:
            local.wait()

    out_shapes = [jax.ShapeDtypeStruct((N_DEV,) + a.shape, a.dtype) for a in xs]
    return _comm_call(body, n, out_shapes, name, xs)


def _all_to_all(gs, name):
    n = len(gs)

    def body(*refs):
        g_refs, out_refs = refs[:n], refs[n:2 * n]
        send_sems, recv_sems, local_sems = refs[2 * n:]
        _, me = _peer(0)
        pending = []
        for i in range(n):
            local = pltpu.make_async_copy(g_refs[i].at[me], out_refs[i].at[0], local_sems.at[i])
            local.start()
            pending.append(local)
        sends = []
        for i in range(n):
            for k in range(1, N_DEV):
                dev, idx = _peer(k)
                cp = pltpu.make_async_remote_copy(
                    src_ref=g_refs[i].at[idx], dst_ref=out_refs[i].at[k], send_sem=send_sems.at[i, k - 1],
                    recv_sem=recv_sems.at[i, k - 1], device_id=dev, device_id_type=MESH_T)
                cp.start()
                sends.append(cp)
        for cp in sends:
            cp.wait_recv()
        for cp in sends:
            cp.wait_send()
        for local in pending:
            local.wait()

    out_shapes = [jax.ShapeDtypeStruct(g.shape, g.dtype) for g in gs]
    return _comm_call(body, n, out_shapes, name, gs)


def _split_copies(kind, src_refs, land_refs, send_sems, recv_sems, local_sems, with_recv):
    n = len(src_refs)
    _, me = _peer(0)
    locals_, remotes = [], []
    for i in range(n):
        if kind == "gather":
            locals_.append(pltpu.make_async_copy(src_refs[i], land_refs[i].at[me], local_sems.at[i]))
        else:
            locals_.append(pltpu.make_async_copy(src_refs[i].at[me], land_refs[i].at[0], local_sems.at[i]))
        for k in range(1, N_DEV):
            dev, idx = _peer(k)
            if kind == "gather":
                src, dst, mine = src_refs[i], land_refs[i].at[me], land_refs[i].at[idx]
            else:
                src, dst, mine = src_refs[i].at[idx], land_refs[i].at[k], land_refs[i].at[k]
            j = i * (N_DEV - 1) + k - 1
            send = pltpu.make_async_remote_copy(
                src_ref=src, dst_ref=dst, send_sem=send_sems.at[j], recv_sem=recv_sems.at[j],
                device_id=dev, device_id_type=MESH_T)
            recv = pltpu.make_async_remote_copy(
                src_ref=src, dst_ref=mine, send_sem=send_sems.at[j], recv_sem=recv_sems.at[j],
                device_id=dev, device_id_type=MESH_T) if with_recv else None
            remotes.append((send, recv))
    return locals_, remotes


_HBM = pl.BlockSpec(memory_space=pltpu.HBM)
_SEM = pl.BlockSpec(memory_space=pltpu.SEMAPHORE)
_ANY = pl.BlockSpec(memory_space=pl.ANY)


def _split_start(srcs, kind, name, after):
    n = len(srcs)
    lands = [lax.empty(((N_DEV,) + a.shape) if kind == "gather" else a.shape, a.dtype) for a in srcs]

    def body(*refs):
        src_refs, land_refs = refs[:n], refs[n:2 * n]
        send_sems, recv_sems, local_sems = refs[2 * n + 1:2 * n + 4]
        token = refs[-1]
        locals_, remotes = _split_copies(kind, src_refs, land_refs, send_sems, recv_sems, local_sems, False)
        for cp in locals_:
            cp.start()
        for send, _ in remotes:
            send.start()
        token[...] = jnp.zeros_like(token)

    sems = (pltpu.SemaphoreType.DMA((n * (N_DEV - 1),)), pltpu.SemaphoreType.DMA((n * (N_DEV - 1),)),
            pltpu.SemaphoreType.DMA((n,)))
    thru = tuple(pltpu.HBM(a.shape, a.dtype) for a in list(srcs) + lands)
    outs = pl.pallas_call(
        body, name=name,
        out_shape=sems + thru + (jax.ShapeDtypeStruct((8, LANES), F32),),
        in_specs=[_HBM] * (2 * n) + [_ANY],
        out_specs=[_SEM] * 3 + [_HBM] * (2 * n) + [pl.BlockSpec(memory_space=pltpu.VMEM)],
        input_output_aliases={i: 3 + i for i in range(2 * n)},
        compiler_params=pltpu.CompilerParams(has_side_effects=pltpu.SideEffectType.DATAFLOW_SIDE_EFFECTING),
    )(*[pltpu.with_memory_space_constraint(a, pltpu.HBM) for a in list(srcs) + lands], after)
    return (kind, n, outs[:3], outs[3:3 + 2 * n]), outs[-1]


def _split_wait(handle, name, after):
    kind, n, sems, thru = handle

    def body(*refs):
        src_refs, land_refs = refs[:n], refs[n:2 * n]
        send_sems, recv_sems, local_sems = refs[2 * n:2 * n + 3]
        locals_, remotes = _split_copies(kind, src_refs, land_refs, send_sems, recv_sems, local_sems, True)
        for send, recv in remotes:
            send.wait_send()
            recv.wait_recv()
        for cp in locals_:
            cp.wait()

    outs = pl.pallas_call(
        body, name=name, out_shape=tuple(pltpu.HBM(a.shape, a.dtype) for a in thru),
        in_specs=[_HBM] * (2 * n) + [_SEM] * 3 + [_ANY], out_specs=[_HBM] * (2 * n),
        input_output_aliases={i: i for i in range(2 * n)},
        compiler_params=pltpu.CompilerParams(has_side_effects=pltpu.SideEffectType.DATAFLOW_SIDE_EFFECTING),
    )(*thru, *sems, after)
    return list(outs[n:])


def _adamw(gslabs, w, m, v, name):
    R, Cw = w.shape
    tr = _tile(R, 64 if Cw > 1024 else 128, 8)
    c1 = 1.0 - ADAM_B1 ** ADAM_STEP
    c2 = 1.0 - ADAM_B2 ** ADAM_STEP

    def body(g_ref, w_ref, m_ref, v_ref, go_ref, d_ref, mo_ref, vo_ref):
        g = g_ref[0].astype(F32)
        for k in range(1, N_DEV):
            g = g + g_ref[k].astype(F32)
        mn = ADAM_B1 * m_ref[...] + (1.0 - ADAM_B1) * g
        vn = ADAM_B2 * v_ref[...] + (1.0 - ADAM_B2) * (g * g)
        m_hat = mn / c1
        v_hat = vn / c2
        go_ref[...] = g
        d_ref[...] = -ADAM_LR * (m_hat / (jnp.sqrt(v_hat) + ADAM_EPS) + ADAM_WD * w_ref[...])
        mo_ref[...] = mn
        vo_ref[...] = vn

    blk = pl.BlockSpec((tr, Cw), lambda i: (i, 0))
    return pl.pallas_call(
        body, grid=(R // tr,),
        in_specs=[pl.BlockSpec((N_DEV, tr, Cw), lambda i: (0, i, 0)), blk, blk, blk],
        out_specs=[blk] * 4, out_shape=[jax.ShapeDtypeStruct((R, Cw), F32)] * 4, name=name)(gslabs, w, m, v)


def _pack(arrs, row_mult, dtype=F32):
    parts = []
    total = 0
    for a in arrs:
        f = a.reshape(-1).astype(dtype)
        n = -(-f.shape[0] // 1024) * 1024
        parts.append(jnp.pad(f, (0, n - f.shape[0])))
        total += n
    rows = total // LANES
    rows_p = -(-rows // row_mult) * row_mult
    flat = jnp.concatenate(parts)
    flat = jnp.pad(flat, (0, rows_p * LANES - total))
    return flat.reshape(rows_p, LANES)


def _unpack(packed, shapes):
    lead = packed.shape[:-2]
    flat = packed.reshape(lead + (-1,))
    out = []
    off = 0
    for s in shapes:
        n = int(np.prod(s))
        out.append(flat[..., off:off + n].reshape(lead + tuple(s)))
        off += -(-n // 1024) * 1024
    return out


def _gather_cols(stacked):
    d, r, c = stacked.shape
    return stacked.transpose(1, 0, 2).reshape(r, d * c)


def _scatter_cols(full):
    r, n = full.shape
    return full.reshape(r, N_DEV, n // N_DEV).transpose(1, 0, 2)


def kernel(x, meta, norm1, w_in, gdn_conv_w, gdn_a_log, gdn_dt_bias, gdn_norm, w_out, norm2, w_ffn_up, ffn_conv_w, ffn_conv_b, w_ffn_down, norm_f, loss_target, m_meta, m_norm1, m_w_in, m_gdn_conv_w, m_gdn_a_log, m_gdn_dt_bias, m_gdn_norm, m_w_out, m_norm2, m_w_ffn_up, m_ffn_conv_w, m_ffn_conv_b, m_w_ffn_down, m_norm_f, v_meta, v_norm1, v_w_in, v_gdn_conv_w, v_gdn_a_log, v_gdn_dt_bias, v_gdn_norm, v_w_out, v_norm2, v_w_ffn_up, v_ffn_conv_w, v_ffn_conv_b, v_w_ffn_down, v_norm_f):
    S = x.shape[1]
    L = N_META + S
    pad = (-L) % CHUNK
    Lp = L + pad

    big = [w_in[0], w_out[0], w_ffn_up[0], w_ffn_down[0]]
    small = [meta, gdn_conv_w, ffn_conv_w]
    small_all, = _all_gather([_pack(small, 8)], "gather_small_weights")
    first, first_token = _split_start([big[0].astype(BF16)], "gather", "gather_w_in_start", small_all)
    late, late_token = _split_start([a.astype(BF16) for a in big[1:]], "gather", "gather_late_start", first_token)

    def first_weights(after):
        w_in_s, = _split_wait(first, "gather_w_in_wait", after)
        w_in_f = _gather_cols(w_in_s)
        w_main = jnp.concatenate([w_in_f[:, _O_GQ:_O_GZ], w_in_f[:, _O_RQ:_O_RG], w_in_f[:, _O_GZ:_O_GA],
                                  w_in_f[:, _O_RG:_O_END]], axis=1)
        return {"w_main": w_main, "w_small": jnp.pad(w_in_f[:, _O_GA:_O_RQ], ((0, 0), (0, LANES - 2 * GDN_H)))}

    def late_weights(after):
        w_out_s, w_up_s, w_down_s = _split_wait(late, "gather_late_wait", after)
        return {"w_out": w_out_s.reshape(D_MODEL, D_MODEL), "w_up": _gather_cols(w_up_s),
                "w_down": w_down_s.reshape(D_FF, D_MODEL)}

    meta_s, gconv_s, fconv_s = _unpack(small_all, [a.shape for a in small])
    wt = {
        "norm1": norm1 + jnp.tile(late_token[0:1, :], (1, D_MODEL // LANES)),
        "gdn_conv_w": _gather_cols(gconv_s[:, 0]), "a_log": gdn_a_log[0], "dt_bias": gdn_dt_bias[0],
        "gdn_norm": gdn_norm, "norm2": norm2, "ffn_conv_w": _gather_cols(fconv_s[:, 0]), "ffn_conv_b": ffn_conv_b,
        "norm_f": norm_f.reshape(1, D_MODEL),
    }
    meta_f = _gather_cols(meta_s)

    pending = {}

    def on_ffn_out_grads(d_w_down, d_w_up, d_w_out):
        srcs = [d_w_out.reshape(N_DEV, D_MODEL // N_DEV, D_MODEL), _scatter_cols(d_w_up),
                d_w_down.reshape(N_DEV, D_FF // N_DEV, D_MODEL)]
        pending["ffn_out"], token = _split_start(srcs, "a2a", "exchange_ffn_out_start", d_w_out)
        return token

    def on_w_in_grads(gm, gs):
        d_w_in = jnp.concatenate([gm[:, 0:3072], gm[:, 6144:7168], gs[:, :2 * GDN_H], gm[:, 3072:6144],
                                  gm[:, 7168:]], axis=1)
        pending["w_in"], token = _split_start([_scatter_cols(d_w_in.astype(BF16))], "a2a", "exchange_w_in_start", gs)
        return token

    hpad = jnp.concatenate([jnp.zeros((pad, D_MODEL), F32), meta_f, x[0]], axis=0)
    tgt = jnp.concatenate([jnp.zeros((pad + N_META, D_MODEL), F32), loss_target[0]], axis=0)
    lossvec, dh0, gr = _local_step(hpad, tgt, pad, wt, first_weights, late_weights, on_ffn_out_grads, on_w_in_grads)

    loss = lax.psum(jnp.sum(lossvec), ("x", "y", "c"))
    grad_x = dh0[pad + N_META:][None]

    big_m = [m_w_in[0], m_w_out[0], m_w_ffn_up[0], m_w_ffn_down[0]]
    big_v = [v_w_in[0], v_w_out[0], v_w_ffn_up[0], v_w_ffn_down[0]]
    slabs_ffn_out = _split_wait(pending["ffn_out"], "exchange_ffn_out_wait", dh0)
    big_out = [None] + [_adamw(slabs_ffn_out[i - 1], big[i], big_m[i], big_v[i], "adamw_big_%d" % i)
                        for i in range(1, len(big))]
    g_sm = [_scatter_cols(dh0[pad:pad + N_META]), _scatter_cols(gr["gdn_conv_w"]), _scatter_cols(gr["ffn_conv_w"])]
    g_small = jnp.stack([_pack([g[d] for g in g_sm], 8) for d in range(N_DEV)])
    slabs_small, = _all_to_all([g_small], "exchange_small_gradients")
    small_out = _adamw(slabs_small, _pack(small, 8), _pack([m_meta, m_gdn_conv_w, m_ffn_conv_w], 8),
                       _pack([v_meta, v_gdn_conv_w, v_ffn_conv_w], 8), "adamw_small_sharded")
    small_un = [_unpack(o, [a.shape for a in small]) for o in small_out]
    rep_w = [norm1, gdn_a_log, gdn_dt_bias, gdn_norm, norm2, ffn_conv_b, norm_f]
    rep_m = [m_norm1, m_gdn_a_log, m_gdn_dt_bias, m_gdn_norm, m_norm2, m_ffn_conv_b, m_norm_f]
    rep_v = [v_norm1, v_gdn_a_log, v_gdn_dt_bias, v_gdn_norm, v_norm2, v_ffn_conv_b, v_norm_f]
    rep_g = [gr["norm1"], gr["a_log"], gr["dt_bias"], gr["gdn_norm"], gr["norm2"], gr["ffn_conv_b"], gr["norm_f"]]
    rep_slabs, = _all_gather([_pack(rep_g, 8)], "gather_small_gradients")
    rep_out = _adamw(rep_slabs, _pack(rep_w, 8), _pack(rep_m, 8), _pack(rep_v, 8), "adamw_replicated")
    rep_shapes = [a.shape for a in rep_w]
    rp_g, rp_d, rp_nm, rp_nv = [_unpack(o, rep_shapes) for o in rep_out]

    slabs_w_in, = _split_wait(pending["w_in"], "exchange_w_in_wait", rep_out[0])
    big_out[0] = _adamw(slabs_w_in, big[0], big_m[0], big_v[0], "adamw_big_0")
    sh_g, sh_d, sh_nm, sh_nv = [
        [small_un[j][0], big_out[0][j][None], small_un[j][1], big_out[1][j][None], big_out[2][j][None],
         small_un[j][2], big_out[3][j][None]] for j in range(4)]

    def order(sh, rp):
        return [sh[0], rp[0], sh[1], sh[2], rp[1], rp[2], rp[3], sh[3], rp[4], sh[4], sh[5], rp[5], sh[6], rp[6]]

    return (loss, grad_x, *order(sh_g, rp_g), *order(sh_d, rp_d), *order(sh_nm, rp_nm), *order(sh_nv, rp_nv))
```

```python
import functools
import math

import numpy as np
import jax
import jax.numpy as jnp
from jax import lax
from jax.experimental import pallas as pl
from jax.experimental.pallas import tpu as pltpu

F32 = jnp.float32
BF16 = jnp.bfloat16
HI = lax.Precision.HIGHEST

D_MODEL = 1024
N_META = 16
CHUNK = 64
GDN_H = 8
GDN_D = 128
RET_H = 4
RET_D = 256
D_FF = 2816
GDN_CONV = 4
FFN_CONV = 3
ROPE_BASE = 10000.0
EPS = 1e-6
N_DEV = 8
LANES = 128
MAIN_W = 10 * 1024
_O_GQ, _O_GZ, _O_GA, _O_RQ, _O_RG, _O_GATE, _O_END = 0, 3072, 4096, 4112, 7184, 8208, 10256

ADAM_LR = 0.001
ADAM_B1 = 0.9
ADAM_B2 = 0.999
ADAM_EPS = 1e-08
ADAM_WD = 0.01
ADAM_STEP = 10

MESH_T = pl.DeviceIdType.MESH


def _tile(n, target, mult):
    best = None
    for d in range(mult, min(n, target) + 1, mult):
        if n % d == 0:
            best = d
    assert best is not None, (n, target, mult)
    return best


def _sig(x):
    return 1.0 / (1.0 + jnp.exp(-x))


def _d(a, b):
    return jnp.dot(a.astype(BF16), b.astype(BF16), preferred_element_type=F32)


def _dnt(a, b):
    return lax.dot_general(a.astype(BF16), b.astype(BF16), (((1,), (1,)), ((), ())), preferred_element_type=F32)


def _dtn(a, b):
    return lax.dot_general(a.astype(BF16), b.astype(BF16), (((0,), (0,)), ((), ())), preferred_element_type=F32)


def _dx(a, b):
    return jnp.dot(a, b, preferred_element_type=F32, precision=HI)


def _dxnt(a, b):
    return lax.dot_general(a, b, (((1,), (1,)), ((), ())), preferred_element_type=F32, precision=HI)


def _dxtn(a, b):
    return lax.dot_general(a, b, (((0,), (0,)), ((), ())), preferred_element_type=F32, precision=HI)


def _split(a):
    hi = a.astype(BF16)
    return hi, (a - hi.astype(F32)).astype(BF16)


def _d3g(a, b, dims):
    ah, al = _split(a)
    bh, bl = _split(b)
    f = functools.partial(lax.dot_general, dimension_numbers=dims, preferred_element_type=F32)
    return f(ah, bh) + (f(ah, bl) + f(al, bh))


_NN = (((1,), (0,)), ((), ()))
_NT = (((1,), (1,)), ((), ()))
_TN = (((0,), (0,)), ((), ()))


def _rowsum(x):
    return jnp.sum(x, axis=1, keepdims=True)


def _allsum(x):
    return jnp.sum(jnp.sum(x, axis=1, keepdims=True), axis=0, keepdims=True)


def _mm_nn(a, b, res=None, out_dtype=F32, name="mm_nn"):
    M, K = a.shape
    N = b.shape[1]
    tm = _tile(M, 704, 16)
    tn = _tile(N, 2816, 128)

    def body(*refs):
        if res is None:
            a_ref, b_ref, o_ref = refs
        else:
            a_ref, b_ref, r_ref, o_ref = refs
        acc = jnp.dot(a_ref[...], b_ref[...], preferred_element_type=F32)
        if res is not None:
            acc = acc + r_ref[...]
        o_ref[...] = acc.astype(out_dtype)

    in_specs = [pl.BlockSpec((tm, K), lambda j, i: (i, 0)), pl.BlockSpec((K, tn), lambda j, i: (0, j))]
    args = [a, b]
    if res is not None:
        in_specs.append(pl.BlockSpec((tm, tn), lambda j, i: (i, j)))
        args.append(res)
    return pl.pallas_call(
        body, grid=(N // tn, M // tm), in_specs=in_specs,
        out_specs=pl.BlockSpec((tm, tn), lambda j, i: (i, j)),
        out_shape=jax.ShapeDtypeStruct((M, N), out_dtype), name=name)(*args)


def _mm_nt(a, b, res=None, name="mm_nt"):
    M, Nc = a.shape
    K = b.shape[0]
    tm = _tile(M, 704, 16)
    tc = _tile(Nc, 5632, 128)

    def body(*refs):
        if res is None:
            a_ref, b_ref, o_ref = refs
        else:
            a_ref, b_ref, r_ref, o_ref = refs
        c = pl.program_id(1)
        p = lax.dot_general(a_ref[...], b_ref[...], (((1,), (1,)), ((), ())), preferred_element_type=F32)

        @pl.when(c == 0)
        def _():
            if res is None:
                o_ref[...] = p
            else:
                o_ref[...] = p + r_ref[...]

        @pl.when(c > 0)
        def _():
            o_ref[...] += p

    in_specs = [pl.BlockSpec((tm, tc), lambda i, c: (i, c)), pl.BlockSpec((K, tc), lambda i, c: (0, c))]
    args = [a, b]
    if res is not None:
        in_specs.append(pl.BlockSpec((tm, K), lambda i, c: (i, 0)))
        args.append(res)
    return pl.pallas_call(
        body, grid=(M // tm, Nc // tc), in_specs=in_specs,
        out_specs=pl.BlockSpec((tm, K), lambda i, c: (i, 0)),
        out_shape=jax.ShapeDtypeStruct((M, K), F32), name=name)(*args)


def _mm_tn(a, b, name="mm_tn"):
    M, K = a.shape
    N = b.shape[1]
    tm = _tile(M, 2752, 16)
    tk = _tile(K, 1408, 128)
    tn = _tile(N, 1408, 128)

    def body(a_ref, b_ref, o_ref):
        m = pl.program_id(2)
        p = lax.dot_general(a_ref[...], b_ref[...], (((0,), (0,)), ((), ())), preferred_element_type=F32)

        @pl.when(m == 0)
        def _():
            o_ref[...] = p

        @pl.when(m > 0)
        def _():
            o_ref[...] += p

    return pl.pallas_call(
        body, grid=(K // tk, N // tn, M // tm),
        in_specs=[pl.BlockSpec((tm, tk), lambda kk, j, m: (m, kk)), pl.BlockSpec((tm, tn), lambda kk, j, m: (m, j))],
        out_specs=pl.BlockSpec((tk, tn), lambda kk, j, m: (kk, j)),
        out_shape=jax.ShapeDtypeStruct((K, N), F32), name=name)(a, b)


def _rms_fwd(x, g, name):
    Lp = x.shape[0]
    tr = _tile(Lp, 256, 16)

    def body(x_ref, g_ref, o_ref):
        xv = x_ref[...]
        r = lax.rsqrt(jnp.mean(xv * xv, axis=-1, keepdims=True) + EPS)
        o_ref[...] = (xv * r * g_ref[...]).astype(BF16)

    return pl.pallas_call(
        body, grid=(Lp // tr,),
        in_specs=[pl.BlockSpec((tr, D_MODEL), lambda i: (i, 0)), pl.BlockSpec((1, D_MODEL), lambda i: (0, 0))],
        out_specs=pl.BlockSpec((tr, D_MODEL), lambda i: (i, 0)),
        out_shape=jax.ShapeDtypeStruct((Lp, D_MODEL), BF16), name=name)(x, g)


def _rms_bwd(x, g, dy, dres, pad, name):
    Lp = x.shape[0]
    tr = _tile(Lp, 256, 16)

    def body(x_ref, g_ref, dy_ref, dr_ref, dx_ref, dxb_ref, dg_ref):
        i = pl.program_id(0)
        xv = x_ref[...]
        r = lax.rsqrt(jnp.mean(xv * xv, axis=-1, keepdims=True) + EPS)
        xh = xv * r
        dyv = dy_ref[...]
        dxh = dyv * g_ref[...]
        dx = r * (dxh - xh * jnp.mean(dxh * xh, axis=-1, keepdims=True)) + dr_ref[...]
        row = i * tr + lax.broadcasted_iota(jnp.int32, (tr, 1), 0)
        dx = jnp.where(row >= pad, dx, 0.0)
        dx_ref[...] = dx
        dxb_ref[...] = dx.astype(BF16)
        part = jnp.sum(dyv * xh, axis=0, keepdims=True)

        @pl.when(i == 0)
        def _():
            dg_ref[...] = part

        @pl.when(i > 0)
        def _():
            dg_ref[...] += part

    blk = pl.BlockSpec((tr, D_MODEL), lambda i: (i, 0))
    vec = pl.BlockSpec((1, D_MODEL), lambda i: (0, 0))
    return pl.pallas_call(
        body, grid=(Lp // tr,), in_specs=[blk, vec, blk, blk], out_specs=[blk, blk, vec],
        out_shape=[jax.ShapeDtypeStruct((Lp, D_MODEL), F32), jax.ShapeDtypeStruct((Lp, D_MODEL), BF16),
                   jax.ShapeDtypeStruct((1, D_MODEL), F32)], name=name)(x, g, dy, dres)


def _final(h2, g, tgt, first_row):
    Lp = h2.shape[0]
    tr = _tile(Lp, 256, 16)

    def body(x_ref, g_ref, t_ref, loss_ref, dx_ref, dxb_ref, dg_ref):
        i = pl.program_id(0)
        xv = x_ref[...]
        gv = g_ref[...]
        r = lax.rsqrt(jnp.mean(xv * xv, axis=-1, keepdims=True) + EPS)
        xh = xv * r
        row = i * tr + lax.broadcasted_iota(jnp.int32, (tr, 1), 0)
        err = jnp.where(row >= first_row, xh * gv - t_ref[...], 0.0)
        lpart = jnp.sum(err * err, axis=0, keepdims=True) * (0.5 / D_MODEL)
        dyv = err * (1.0 / D_MODEL)
        dxh = dyv * gv
        dx = r * (dxh - xh * jnp.mean(dxh * xh, axis=-1, keepdims=True))
        dx_ref[...] = dx
        dxb_ref[...] = dx.astype(BF16)
        part = jnp.sum(dyv * xh, axis=0, keepdims=True)

        @pl.when(i == 0)
        def _():
            dg_ref[...] = part
            loss_ref[...] = lpart

        @pl.when(i > 0)
        def _():
            dg_ref[...] += part
            loss_ref[...] += lpart

    blk = pl.BlockSpec((tr, D_MODEL), lambda i: (i, 0))
    vec = pl.BlockSpec((1, D_MODEL), lambda i: (0, 0))
    return pl.pallas_call(
        body, grid=(Lp // tr,), in_specs=[blk, vec, blk], out_specs=[vec, blk, blk, vec],
        out_shape=[jax.ShapeDtypeStruct((1, D_MODEL), F32), jax.ShapeDtypeStruct((Lp, D_MODEL), F32),
                   jax.ShapeDtypeStruct((Lp, D_MODEL), BF16), jax.ShapeDtypeStruct((1, D_MODEL), F32)],
        name="final_norm_loss")(h2, g, tgt)


def _halo_prev(tr, width, col=0):
    return pl.BlockSpec((8, width), lambda i: (jnp.maximum(i * (tr // 8) - 1, 0), col))


def _halo_next(tr, width, nrows, col=0):
    last = nrows // 8 - 1
    return pl.BlockSpec((8, width), lambda i: (jnp.minimum((i + 1) * (tr // 8), last), col))


def _shifted(x, offs):
    n = x.shape[0]
    return [x if off == 0 else pltpu.roll(x, n - off, 0) for off in offs]


def _taps(wins, w, rows, bias=None):
    acc = w[0:1, :] * wins[0][0:rows, :]
    if bias is not None:
        acc = acc + bias
    for kk in range(1, len(wins)):
        acc = acc + w[kk:kk + 1, :] * wins[kk][0:rows, :]
    return acc


def _gdn_pre(proj_m, proj_s, conv_w, gparams, pad):
    Lp = proj_m.shape[0]
    tr = _tile(Lp, 192, 64)
    W3 = 3 * D_MODEL

    def body(main_ref, prev_ref, s_ref, w_ref, gp_ref, qkv_ref, gsm_ref):
        i = pl.program_id(0)
        prev = jnp.where(i > 0, prev_ref[...], 0.0)
        ext = jnp.concatenate([prev, main_ref[...]], axis=0)
        c = _taps(_shifted(ext, range(8 - (GDN_CONV - 1), 9)), w_ref[...], tr)
        s = c * _sig(c)
        scale = GDN_D ** -0.5
        for j in range(2 * GDN_H):
            seg = s[:, j * GDN_D:(j + 1) * GDN_D]
            r = lax.rsqrt(_rowsum(seg * seg) + EPS)
            if j < GDN_H:
                r = r * scale
            qkv_ref[:, j * GDN_D:(j + 1) * GDN_D] = seg * r
        qkv_ref[:, 2 * D_MODEL:] = s[:, 2 * D_MODEL:]
        sm = s_ref[...]
        gp = gp_ref[...]
        lane = lax.broadcasted_iota(jnp.int32, sm.shape, 1)
        z = sm + gp[1:2, :]
        softplus = jnp.maximum(z, 0.0) + jnp.log(1.0 + jnp.exp(-jnp.abs(z)))
        lg = -jnp.exp(gp[0:1, :]) * softplus
        row = i * tr + lax.broadcasted_iota(jnp.int32, (tr, 1), 0)
        out = jnp.where(lane < GDN_H, lg, jnp.where(lane < 2 * GDN_H, _sig(sm), 0.0))
        gsm_ref[...] = jnp.where(row >= pad, out, 0.0)

    return pl.pallas_call(
        body, grid=(Lp // tr,),
        in_specs=[pl.BlockSpec((tr, W3), lambda i: (i, 0)), _halo_prev(tr, W3),
                  pl.BlockSpec((tr, LANES), lambda i: (i, 0)),
                  pl.BlockSpec((GDN_CONV, W3), lambda i: (0, 0)), pl.BlockSpec((8, LANES), lambda i: (0, 0))],
        out_specs=[pl.BlockSpec((tr, W3), lambda i: (i, 0)), pl.BlockSpec((tr, LANES), lambda i: (i, 0))],
        out_shape=[jax.ShapeDtypeStruct((Lp, W3), F32), jax.ShapeDtypeStruct((Lp, LANES), F32)],
        name="gdn_pre")(proj_m, proj_m, proj_s, conv_w, gparams)


def _gdn_pre_bwd(proj_m, proj_s, conv_w, gparams, dq, dk, dv, dgs, pad):
    Lp = proj_m.shape[0]
    tr = _tile(Lp, 192, 64)
    W3 = 3 * D_MODEL
    te = tr + 8

    def body(main_ref, prev_ref, next_ref, s_ref, w_ref, gp_ref,
             dq_ref, dqn_ref, dk_ref, dkn_ref, dv_ref, dvn_ref, dgs_ref,
             da_ref, ds_ref, dw_ref, dgp_ref):
        i = pl.program_id(0)
        w = w_ref[...]
        prev = jnp.where(i > 0, prev_ref[...], 0.0)
        ext = jnp.concatenate([prev, main_ref[...], next_ref[...]], axis=0)
        wins = _shifted(ext, range(8 - (GDN_CONV - 1), 9))
        c = _taps(wins, w, te)
        sg = _sig(c)
        s = c * sg
        rowe = i * tr + lax.broadcasted_iota(jnp.int32, (te, 1), 0)
        live = (rowe >= pad) & (rowe < Lp)
        dqe = jnp.concatenate([dq_ref[...], dqn_ref[...]], axis=0)
        dke = jnp.concatenate([dk_ref[...], dkn_ref[...]], axis=0)
        dve = jnp.concatenate([dv_ref[...], dvn_ref[...]], axis=0)
        scale = GDN_D ** -0.5
        parts = []
        for j in range(2 * GDN_H):
            seg = s[:, j * GDN_D:(j + 1) * GDN_D]
            r = lax.rsqrt(_rowsum(seg * seg) + EPS)
            xh = seg * r
            if j < GDN_H:
                dxh = dqe[:, j * GDN_D:(j + 1) * GDN_D] * scale
            else:
                dxh = dke[:, (j - GDN_H) * GDN_D:(j - GDN_H + 1) * GDN_D]
            parts.append(r * (dxh - xh * _rowsum(dxh * xh)))
        parts.append(dve)
        dsv = jnp.concatenate(parts, axis=1)
        dc = jnp.where(live, dsv * (sg * (1.0 + c * (1.0 - sg))), 0.0)
        da_ref[...] = _taps(_shifted(dc, range(GDN_CONV - 1, -1, -1)), w, tr).astype(BF16)
        dcm = dc[0:tr, :]
        rows = [jnp.sum(dcm * wins[kk][0:tr, :], axis=0, keepdims=True) for kk in range(GDN_CONV)]
        dwp = jnp.concatenate(rows + [jnp.zeros((8 - GDN_CONV, W3), F32)], axis=0)

        sm = s_ref[...]
        gp = gp_ref[...]
        lane = lax.broadcasted_iota(jnp.int32, sm.shape, 1)
        rowm = i * tr + lax.broadcasted_iota(jnp.int32, (tr, 1), 0)
        dgv = jnp.where(rowm >= pad, dgs_ref[...], 0.0)
        dlg = jnp.where(lane < GDN_H, dgv, 0.0)
        dbt = jnp.where((lane >= GDN_H) & (lane < 2 * GDN_H), dgv, 0.0)
        z = sm + gp[1:2, :]
        softplus = jnp.maximum(z, 0.0) + jnp.log(1.0 + jnp.exp(-jnp.abs(z)))
        ea = jnp.exp(gp[0:1, :])
        dz = dlg * (-ea) * _sig(z)
        dal = dlg * (-ea) * softplus
        bt = _sig(sm)
        dgb = dbt * bt * (1.0 - bt)
        ds_ref[...] = (dz + dgb).astype(BF16)
        gpp = jnp.concatenate([jnp.sum(dal, axis=0, keepdims=True), jnp.sum(dz, axis=0, keepdims=True),
                               jnp.zeros((6, LANES), F32)], axis=0)

        @pl.when(i == 0)
        def _():
            dw_ref[...] = dwp
            dgp_ref[...] = gpp

        @pl.when(i > 0)
        def _():
            dw_ref[...] += dwp
            dgp_ref[...] += gpp

    m3 = pl.BlockSpec((tr, W3), lambda i: (i, 0))
    m1 = pl.BlockSpec((tr, D_MODEL), lambda i: (i, 0))
    n1 = _halo_next(tr, D_MODEL, Lp)
    return pl.pallas_call(
        body, grid=(Lp // tr,),
        in_specs=[m3, _halo_prev(tr, W3), _halo_next(tr, W3, Lp), pl.BlockSpec((tr, LANES), lambda i: (i, 0)),
                  pl.BlockSpec((GDN_CONV, W3), lambda i: (0, 0)), pl.BlockSpec((8, LANES), lambda i: (0, 0)),
                  m1, n1, m1, n1, m1, n1, pl.BlockSpec((tr, LANES), lambda i: (i, 0))],
        out_specs=[m3, pl.BlockSpec((tr, LANES), lambda i: (i, 0)),
                   pl.BlockSpec((8, W3), lambda i: (0, 0)), pl.BlockSpec((8, LANES), lambda i: (0, 0))],
        out_shape=[jax.ShapeDtypeStruct((Lp, W3), BF16), jax.ShapeDtypeStruct((Lp, LANES), BF16),
                   jax.ShapeDtypeStruct((8, W3), F32), jax.ShapeDtypeStruct((8, LANES), F32)],
        name="gdn_pre_bwd")(proj_m, proj_m, proj_m, proj_s, conv_w, gparams, dq, dq, dk, dk, dv, dv, dgs)


def _gdn_gates(gs):
    ri = lax.broadcasted_iota(jnp.int32, (CHUNK, CHUNK), 0)
    ci = lax.broadcasted_iota(jnp.int32, (CHUNK, CHUNK), 1)
    tril = ri >= ci
    strict = ri > ci
    gall = _dx(tril.astype(F32), gs)
    lane8 = lax.broadcasted_iota(jnp.int32, (8, LANES), 1)
    sub8 = lax.broadcasted_iota(jnp.int32, (8, LANES), 0)
    grow = _dxnt((lane8 == sub8).astype(F32), gall)
    return gall, grow, tril, strict


def _gdn_decay(gall, grow, tril, h):
    g = gall[:, h:h + 1]
    return g, jnp.where(tril, jnp.exp(jnp.where(tril, g - grow[h:h + 1, :], 0.0)), 0.0)


def _gdn_chunk_specs(N, rev):
    cn = (lambda n: N - 1 - n) if rev else (lambda n: n)
    col = lambda j: pl.BlockSpec((CHUNK, D_MODEL), lambda n: (cn(n), j))
    gate = pl.BlockSpec((CHUNK, LANES), lambda n: (cn(n), 0))
    st = lambda a, b: pl.BlockSpec((GDN_H, None, a, b), lambda n: (0, cn(n), 0, 0))
    return col, gate, st


def _gdn_chunk_fwd(qkv, gsm):
    Lp = qkv.shape[0]
    N = Lp // CHUNK

    def body(q_ref, k_ref, v_ref, gs_ref, o_ref, sin_ref, t_ref, S):
        n = pl.program_id(0)

        @pl.when(n == 0)
        def _():
            S[...] = jnp.zeros_like(S)

        gs = gs_ref[...]
        gall, grow, tril, strict = _gdn_gates(gs)
        ri = lax.broadcasted_iota(jnp.int32, (CHUNK, CHUNK), 0)
        ci = lax.broadcasted_iota(jnp.int32, (CHUNK, CHUNK), 1)
        eye = (ri == ci).astype(F32)
        heads = range(GDN_H)
        sls = [slice(h * GDN_D, (h + 1) * GDN_D) for h in heads]
        q = [q_ref[:, sl] for sl in sls]
        k = [k_ref[:, sl] for sl in sls]
        v = [v_ref[:, sl] for sl in sls]
        s0 = [S[h] for h in heads]
        beta = [gs[:, GDN_H + h:GDN_H + h + 1] for h in heads]
        gg = [_gdn_decay(gall, grow, tril, h) for h in heads]
        g = [x[0] for x in gg]
        gam = [x[1] for x in gg]
        eg = [jnp.exp(g[h]) for h in heads]
        gl = [g[h][CHUNK - 1:CHUNK, :] for h in heads]
        kb = [k[h] * beta[h] for h in heads]
        pw = [-jnp.where(strict, _dnt(kb[h], k[h]) * gam[h], 0.0) for h in heads]
        p = [_dnt(q[h], k[h]) * gam[h] for h in heads]
        qs = [_d(q[h] * eg[h], s0[h]) for h in heads]
        t = [eye + pw[h] for h in heads]
        for _ in range(5):
            pw = [_d3g(pw[h], pw[h], _NN) for h in heads]
            t = [t[h] + _d3g(t[h], pw[h], _NN) for h in heads]
        u = [_d(t[h], v[h] * beta[h]) for h in heads]
        w = [_d(t[h], kb[h] * eg[h]) for h in heads]
        vnew = [u[h] - _d(w[h], s0[h]) for h in heads]
        for h in heads:
            o_ref[:, sls[h]] = qs[h] + _d(p[h], vnew[h])
            sin_ref[h] = s0[h]
            t_ref[h] = t[h]
            S[h] = s0[h] * jnp.exp(gl[h]) + _dtn(k[h] * jnp.exp(gl[h] - g[h]), vnew[h])

    col, gate, st = _gdn_chunk_specs(N, False)
    return pl.pallas_call(
        body, grid=(N,),
        in_specs=[col(0), col(1), col(2), gate],
        out_specs=[col(0), st(GDN_D, GDN_D), st(CHUNK, CHUNK)],
        out_shape=[jax.ShapeDtypeStruct((Lp, D_MODEL), F32), jax.ShapeDtypeStruct((GDN_H, N, GDN_D, GDN_D), F32),
                   jax.ShapeDtypeStruct((GDN_H, N, CHUNK, CHUNK), F32)],
        scratch_shapes=[pltpu.VMEM((GDN_H, GDN_D, GDN_D), F32)],
        name="gdn_chunk_fwd")(qkv, qkv, qkv, gsm)


def _gdn_chunk_bwd(qkv, gsm, do, s_in, t_in):
    Lp = qkv.shape[0]
    N = Lp // CHUNK

    def body(q_ref, k_ref, v_ref, gs_ref, do_ref, sin_ref, t_ref, dq_ref, dk_ref, dv_ref, dgs_ref, dS):
        n = pl.program_id(0)

        @pl.when(n == 0)
        def _():
            dS[...] = jnp.zeros_like(dS)

        gs = gs_ref[...]
        gall, grow, tril, strict = _gdn_gates(gs)
        lane = lax.broadcasted_iota(jnp.int32, (CHUNK, LANES), 1)
        rcol = lax.broadcasted_iota(jnp.int32, (CHUNK, 1), 0)
        ones = jnp.ones((CHUNK, LANES), F32)
        dg_all = jnp.zeros((CHUNK, LANES), F32)
        dbeta_all = jnp.zeros((CHUNK, LANES), F32)
        heads = range(GDN_H)
        sls = [slice(h * GDN_D, (h + 1) * GDN_D) for h in heads]
        H = lambda f: [f(h) for h in heads]
        q = H(lambda h: q_ref[:, sls[h]])
        k = H(lambda h: k_ref[:, sls[h]])
        v = H(lambda h: v_ref[:, sls[h]])
        dov = H(lambda h: do_ref[:, sls[h]])
        s0 = H(lambda h: sin_ref[h])
        t = H(lambda h: t_ref[h])
        dsv = H(lambda h: dS[h])
        beta = H(lambda h: gs[:, GDN_H + h:GDN_H + h + 1])
        gg = H(lambda h: _gdn_decay(gall, grow, tril, h))
        g = [x[0] for x in gg]
        gam = [x[1] for x in gg]
        eg = H(lambda h: jnp.exp(g[h]))
        egl = H(lambda h: jnp.exp(g[h][CHUNK - 1:CHUNK, :]))
        e = H(lambda h: jnp.exp(g[h][CHUNK - 1:CHUNK, :] - g[h]))
        kb = H(lambda h: k[h] * beta[h])
        kbg = H(lambda h: kb[h] * eg[h])
        vb = H(lambda h: v[h] * beta[h])
        qg = H(lambda h: q[h] * eg[h])
        kd = H(lambda h: k[h] * e[h])
        m = H(lambda h: jnp.where(strict, _dnt(kb[h], k[h]) * gam[h], 0.0))
        u = H(lambda h: _d(t[h], vb[h]))
        w = H(lambda h: _d(t[h], kbg[h]))
        p = H(lambda h: _dnt(q[h], k[h]) * gam[h])
        dqg = H(lambda h: _dnt(dov[h], s0[h]))
        kdds = H(lambda h: _d(kd[h], dsv[h]))
        qgdo = H(lambda h: _dtn(qg[h], dov[h]))
        vnew = H(lambda h: u[h] - _d(w[h], s0[h]))
        dvnew = H(lambda h: _dtn(p[h], dov[h]) + kdds[h])
        dp = H(lambda h: jnp.where(tril, _dnt(dov[h], vnew[h]), 0.0))
        dkd = H(lambda h: _dnt(vnew[h], dsv[h]))
        dw = H(lambda h: -_dnt(dvnew[h], s0[h]))
        for h in heads:
            dS[h] = qgdo[h] + egl[h] * dsv[h] - _dtn(w[h], dvnew[h])
        dvb = H(lambda h: _dtn(t[h], dvnew[h]))
        dkbg = H(lambda h: _dtn(t[h], dw[h]))
        dt = H(lambda h: _dnt(dvnew[h], vb[h]) + _dnt(dw[h], kbg[h]))
        x1 = H(lambda h: _d3g(t[h], dt[h], _TN))
        dm = H(lambda h: jnp.where(strict, -_d3g(x1[h], t[h], _NT), 0.0))
        dkk = H(lambda h: dm[h] * gam[h])
        dqk = H(lambda h: dp[h] * gam[h])
        dkb = H(lambda h: _d(dkk[h], k[h]) + eg[h] * dkbg[h])
        em = H(lambda h: dm[h] * m[h] + dp[h] * p[h])
        colsum = H(lambda h: _d3g(em[h], ones, _TN)[:, 0:1])
        for h in heads:
            dk_ref[:, sls[h]] = _dtn(dkk[h], kb[h]) + _dtn(dqk[h], q[h]) + dkd[h] * e[h] + beta[h] * dkb[h]
            dq_ref[:, sls[h]] = _d(dqk[h], k[h]) + dqg[h] * eg[h]
            dv_ref[:, sls[h]] = beta[h] * dvb[h]
        for h in heads:
            dbeta = _rowsum(k[h] * dkb[h]) + _rowsum(v[h] * dvb[h])
            z = _rowsum(kd[h] * dkd[h])
            dg = _rowsum(em[h]) - colsum[h] + _rowsum(qg[h] * dqg[h]) + _rowsum(kbg[h] * dkbg[h]) - z
            extra = _allsum(z) + egl[h] * _allsum(s0[h] * dsv[h])
            dg = dg + jnp.where(rcol == CHUNK - 1, extra, 0.0)
            dg_all = dg_all + jnp.where(lane == h, dg, 0.0)
            dbeta_all = dbeta_all + jnp.where(lane == GDN_H + h, dbeta, 0.0)
        ri = lax.broadcasted_iota(jnp.int32, (CHUNK, CHUNK), 0)
        ci = lax.broadcasted_iota(jnp.int32, (CHUNK, CHUNK), 1)
        dgs_ref[...] = _dx((ci >= ri).astype(F32), dg_all) + dbeta_all

    col, gate, st = _gdn_chunk_specs(N, True)
    return pl.pallas_call(
        body, grid=(N,),
        in_specs=[col(0), col(1), col(2), gate, col(0), st(GDN_D, GDN_D), st(CHUNK, CHUNK)],
        out_specs=[col(0), col(0), col(0), gate],
        out_shape=[jax.ShapeDtypeStruct((Lp, D_MODEL), F32)] * 3 + [jax.ShapeDtypeStruct((Lp, LANES), F32)],
        scratch_shapes=[pltpu.VMEM((GDN_H, GDN_D, GDN_D), F32)],
        name="gdn_chunk_bwd")(qkv, qkv, qkv, gsm, do, s_in, t_in)


def _rot(x, c, s):
    half = RET_D // 2
    x1 = x[:, :half]
    x2 = x[:, half:]
    return jnp.concatenate([x1 * c - x2 * s, x2 * c + x1 * s], axis=1)


def _rot_bwd(d, c, s):
    half = RET_D // 2
    d1 = d[:, :half]
    d2 = d[:, half:]
    return jnp.concatenate([d1 * c + d2 * s, d2 * c - d1 * s], axis=1)


def _ret_tables():
    hh = jnp.arange(RET_H, dtype=F32)
    lg = jnp.log(1.0 - 2.0 ** (-5.0 - hh))
    idx = jnp.arange(CHUNK, dtype=F32)
    tril = jnp.asarray(np.tril(np.ones((CHUNK, CHUNK), dtype=bool)))
    dmask = jnp.where(tril, jnp.exp((idx[:, None] - idx[None, :]) * lg[:, None, None]), 0.0)
    qdec = jnp.exp((idx[None, :] + 1.0) * lg[:, None])
    kdec = jnp.exp((CHUNK - 1.0 - idx[None, :]) * lg[:, None])
    gch = jnp.exp(CHUNK * lg)
    qdec = jnp.broadcast_to(qdec[:, :, None], (RET_H, CHUNK, RET_D))
    kdec = jnp.broadcast_to(kdec[:, :, None], (RET_H, CHUNK, RET_D))
    gch = jnp.broadcast_to(gch[:, None, None], (RET_H, 8, LANES))
    return dmask, qdec, kdec, gch


def _ret_specs(N, rev):
    cn = (lambda n: N - 1 - n) if rev else (lambda n: n)
    col = lambda j: pl.BlockSpec((CHUNK, D_MODEL), lambda n: (cn(n), j))
    tab = lambda a, b: pl.BlockSpec((RET_H, a, b), lambda n: (0, 0, 0))
    rope = pl.BlockSpec((CHUNK, LANES), lambda n: (cn(n), 0))
    st = pl.BlockSpec((RET_H, None, RET_D, RET_D), lambda n: (0, cn(n), 0, 0))
    return col, tab, rope, st


def _ret_chunk_fwd(proj_m, cos, sin, tables):
    Lp = proj_m.shape[0]
    N = Lp // CHUNK
    dmask, qdec, kdec, gch = tables

    def body(q_ref, k_ref, v_ref, c_ref, s_ref, dm_ref, qd_ref, kd_ref, g_ref, o_ref, sin_ref, S):
        n = pl.program_id(0)

        @pl.when(n == 0)
        def _():
            S[...] = jnp.zeros_like(S)

        c = c_ref[...]
        s = s_ref[...]
        heads = range(RET_H)
        sls = [slice(h * RET_D, (h + 1) * RET_D) for h in heads]
        H = lambda f: [f(h) for h in heads]
        qr = H(lambda h: _rot(q_ref[:, sls[h]], c, s))
        ks = H(lambda h: _rot(k_ref[:, sls[h]], c, s) * (RET_D ** -0.5))
        v = H(lambda h: v_ref[:, sls[h]])
        s0 = H(lambda h: S[h])
        a = H(lambda h: _dnt(qr[h], ks[h]) * dm_ref[h])
        qs = H(lambda h: _d(qr[h] * qd_ref[h], s0[h]))
        kv = H(lambda h: _dtn(ks[h] * kd_ref[h], v[h]))
        for h in heads:
            o_ref[:, sls[h]] = _d(a[h], v[h]) + qs[h]
            sin_ref[h] = s0[h]
            S[h] = s0[h] * g_ref[h, 0:1, 0:1] + kv[h]

    col, tab, rope, st = _ret_specs(N, False)
    return pl.pallas_call(
        body, grid=(N,),
        in_specs=[col(3), col(4), col(5), rope, rope,
                  tab(CHUNK, CHUNK), tab(CHUNK, RET_D), tab(CHUNK, RET_D), tab(8, LANES)],
        out_specs=[col(0), st],
        out_shape=[jax.ShapeDtypeStruct((Lp, D_MODEL), F32), jax.ShapeDtypeStruct((RET_H, N, RET_D, RET_D), F32)],
        scratch_shapes=[pltpu.VMEM((RET_H, RET_D, RET_D), F32)],
        name="ret_chunk_fwd")(proj_m, proj_m, proj_m, cos, sin, dmask, qdec, kdec, gch)


def _ret_chunk_bwd(proj_m, cos, sin, tables, do, s_in):
    Lp = proj_m.shape[0]
    N = Lp // CHUNK
    dmask, qdec, kdec, gch = tables

    def body(q_ref, k_ref, v_ref, c_ref, s_ref, dm_ref, qd_ref, kd_ref, g_ref, do_ref, sin_ref,
             dq_ref, dk_ref, dv_ref, dS):
        n = pl.program_id(0)

        @pl.when(n == 0)
        def _():
            dS[...] = jnp.zeros_like(dS)

        c = c_ref[...]
        s = s_ref[...]
        kscale = RET_D ** -0.5
        heads = range(RET_H)
        sls = [slice(h * RET_D, (h + 1) * RET_D) for h in heads]
        H = lambda f: [f(h) for h in heads]
        qr = H(lambda h: _rot(q_ref[:, sls[h]], c, s))
        ks = H(lambda h: _rot(k_ref[:, sls[h]], c, s) * kscale)
        v = H(lambda h: v_ref[:, sls[h]])
        dov = H(lambda h: do_ref[:, sls[h]])
        s0 = H(lambda h: sin_ref[h])
        dsv = H(lambda h: dS[h])
        ad = H(lambda h: _dnt(qr[h], ks[h]) * dm_ref[h])
        da = H(lambda h: _dnt(dov[h], v[h]) * dm_ref[h])
        kds = H(lambda h: _d(ks[h] * kd_ref[h], dsv[h]))
        dos = H(lambda h: _dnt(dov[h], s0[h]) * qd_ref[h])
        vds = H(lambda h: _dnt(v[h], dsv[h]) * kd_ref[h])
        qdo = H(lambda h: _dtn(qr[h] * qd_ref[h], dov[h]))
        for h in heads:
            dS[h] = dsv[h] * g_ref[h, 0:1, 0:1] + qdo[h]
        for h in heads:
            dv_ref[:, sls[h]] = (_dtn(ad[h], dov[h]) + kds[h]).astype(BF16)
            dq_ref[:, sls[h]] = _rot_bwd(_d(da[h], ks[h]) + dos[h], c, s).astype(BF16)
            dk_ref[:, sls[h]] = _rot_bwd((_dtn(da[h], qr[h]) + vds[h]) * kscale, c, s).astype(BF16)

    col, tab, rope, st = _ret_specs(N, True)
    return pl.pallas_call(
        body, grid=(N,),
        in_specs=[col(3), col(4), col(5), rope, rope,
                  tab(CHUNK, CHUNK), tab(CHUNK, RET_D), tab(CHUNK, RET_D), tab(8, LANES), col(0), st],
        out_specs=[col(0), col(0), col(0)],
        out_shape=[jax.ShapeDtypeStruct((Lp, D_MODEL), BF16)] * 3,
        scratch_shapes=[pltpu.VMEM((RET_H, RET_D, RET_D), F32)],
        name="ret_chunk_bwd")(proj_m, proj_m, proj_m, cos, sin, dmask, qdec, kdec, gch, do, s_in)


def _merge_specs(tr):
    col = lambda j: pl.BlockSpec((tr, D_MODEL), lambda i: (i, j))
    return col


def _merge_fwd(o_a, o_b, proj_m, gnorm):
    Lp = o_a.shape[0]
    tr = _tile(Lp, 192, 16)

    def body(oa_ref, ob_ref, gz_ref, rg_ref, ga_ref, gb_ref, gn_ref, y_ref):
        gn = gn_ref[...]
        oa = oa_ref[...]
        ob = ob_ref[...]
        gz = gz_ref[...]
        ya = []
        for j in range(GDN_H):
            seg = oa[:, j * GDN_D:(j + 1) * GDN_D]
            r = lax.rsqrt(jnp.mean(seg * seg, axis=-1, keepdims=True) + EPS)
            ya.append(seg * r * gn)
        ya = jnp.concatenate(ya, axis=1) * (gz * _sig(gz))
        yb = []
        for j in range(RET_H):
            seg = ob[:, j * RET_D:(j + 1) * RET_D]
            r = lax.rsqrt(jnp.mean(seg * seg, axis=-1, keepdims=True) + EPS)
            yb.append(seg * r)
        rg = rg_ref[...]
        yb = jnp.concatenate(yb, axis=1) * (rg * _sig(rg))
        y_ref[...] = (_sig(ga_ref[...]) * ya + _sig(gb_ref[...]) * yb).astype(BF16)

    col = _merge_specs(tr)
    return pl.pallas_call(
        body, grid=(Lp // tr,),
        in_specs=[col(0), col(0), col(6), col(7), col(8), col(9), pl.BlockSpec((1, GDN_D), lambda i: (0, 0))],
        out_specs=col(0), out_shape=jax.ShapeDtypeStruct((Lp, D_MODEL), BF16),
        name="merge_fwd")(o_a, o_b, proj_m, proj_m, proj_m, proj_m, gnorm)


def _merge_bwd(dy, o_a, o_b, proj_m, gnorm):
    Lp = o_a.shape[0]
    tr = _tile(Lp, 192, 16)

    def body(dy_ref, oa_ref, ob_ref, gz_ref, rg_ref, ga_ref, gb_ref, gn_ref, dc_ref, doa_ref, dob_ref, dgn_ref):
        i = pl.program_id(0)
        gn = gn_ref[...]
        dyv = dy_ref[...]
        oa = oa_ref[...]
        ob = ob_ref[...]
        gz = gz_ref[...]
        rg = rg_ref[...]
        sa = _sig(ga_ref[...])
        sb = _sig(gb_ref[...])
        dya = dyv * sa
        dyb = dyv * sb
        sgz = _sig(gz)
        szz = gz * sgz
        dgn = jnp.zeros((1, GDN_D), F32)
        ya = []
        dgz = []
        for j in range(GDN_H):
            sl = slice(j * GDN_D, (j + 1) * GDN_D)
            seg = oa[:, sl]
            r = lax.rsqrt(jnp.mean(seg * seg, axis=-1, keepdims=True) + EPS)
            xh = seg * r
            oan = xh * gn
            ya.append(oan * szz[:, sl])
            dgz.append(dya[:, sl] * oan * (sgz[:, sl] * (1.0 + gz[:, sl] * (1.0 - sgz[:, sl]))))
            doan = dya[:, sl] * szz[:, sl]
            dgn = dgn + jnp.sum(doan * xh, axis=0, keepdims=True)
            dxh = doan * gn
            doa_ref[:, sl] = r * (dxh - xh * jnp.mean(dxh * xh, axis=-1, keepdims=True))
        ya = jnp.concatenate(ya, axis=1)
        srg = _sig(rg)
        srr = rg * srg
        yb = []
        drg = []
        for j in range(RET_H):
            sl = slice(j * RET_D, (j + 1) * RET_D)
            seg = ob[:, sl]
            r = lax.rsqrt(jnp.mean(seg * seg, axis=-1, keepdims=True) + EPS)
            xh = seg * r
            yb.append(xh * srr[:, sl])
            drg.append(dyb[:, sl] * xh * (srg[:, sl] * (1.0 + rg[:, sl] * (1.0 - srg[:, sl]))))
            dxh = dyb[:, sl] * srr[:, sl]
            dob_ref[:, sl] = r * (dxh - xh * jnp.mean(dxh * xh, axis=-1, keepdims=True))
        yb = jnp.concatenate(yb, axis=1)
        dc_ref[:, 0:D_MODEL] = jnp.concatenate(dgz, axis=1).astype(BF16)
        dc_ref[:, D_MODEL:2 * D_MODEL] = jnp.concatenate(drg, axis=1).astype(BF16)
        dc_ref[:, 2 * D_MODEL:3 * D_MODEL] = (dyv * ya * sa * (1.0 - sa)).astype(BF16)
        dc_ref[:, 3 * D_MODEL:] = (dyv * yb * sb * (1.0 - sb)).astype(BF16)

        @pl.when(i == 0)
        def _():
            dgn_ref[...] = dgn

        @pl.when(i > 0)
        def _():
            dgn_ref[...] += dgn

    col = _merge_specs(tr)
    return pl.pallas_call(
        body, grid=(Lp // tr,),
        in_specs=[col(0), col(0), col(0), col(6), col(7), col(8), col(9), pl.BlockSpec((1, GDN_D), lambda i: (0, 0))],
        out_specs=[pl.BlockSpec((tr, 4 * D_MODEL), lambda i: (i, 0)), col(0), col(0),
                   pl.BlockSpec((1, GDN_D), lambda i: (0, 0))],
        out_shape=[jax.ShapeDtypeStruct((Lp, 4 * D_MODEL), BF16), jax.ShapeDtypeStruct((Lp, D_MODEL), F32),
                   jax.ShapeDtypeStruct((Lp, D_MODEL), F32), jax.ShapeDtypeStruct((1, GDN_D), F32)],
        name="merge_bwd")(dy, o_a, o_b, proj_m, proj_m, proj_m, proj_m, gnorm)


def _ffn_act(up, conv_w, conv_b):
    Lp = up.shape[0]
    tr = _tile(Lp, 192, 16)
    W2 = 2 * D_FF

    def body(main_ref, prev_ref, w_ref, b_ref, act_ref):
        i = pl.program_id(0)
        prev = jnp.where(i > 0, prev_ref[...], 0.0)
        ext = jnp.concatenate([prev, main_ref[...]], axis=0)
        u = _taps(_shifted(ext, range(8 - (FFN_CONV - 1), 9)), w_ref[...], tr, b_ref[...])
        a = u[:, :D_FF]
        act_ref[...] = (a * _sig(a) * u[:, D_FF:]).astype(BF16)

    return pl.pallas_call(
        body, grid=(Lp // tr,),
        in_specs=[pl.BlockSpec((tr, W2), lambda i: (i, 0)), _halo_prev(tr, W2),
                  pl.BlockSpec((FFN_CONV, W2), lambda i: (0, 0)), pl.BlockSpec((1, W2), lambda i: (0, 0))],
        out_specs=pl.BlockSpec((tr, D_FF), lambda i: (i, 0)),
        out_shape=jax.ShapeDtypeStruct((Lp, D_FF), BF16), name="ffn_act")(up, up, conv_w, conv_b)


def _ffn_act_bwd(up, dact, conv_w, conv_b):
    Lp = up.shape[0]
    tr = _tile(Lp, 96, 16)
    W2 = 2 * D_FF
    te = tr + 8

    def body(main_ref, prev_ref, next_ref, da_ref, dan_ref, w_ref, b_ref, dup_ref, acc_ref):
        i = pl.program_id(0)
        w = w_ref[...]
        prev = jnp.where(i > 0, prev_ref[...], 0.0)
        ext = jnp.concatenate([prev, main_ref[...], next_ref[...]], axis=0)
        wins = _shifted(ext, range(8 - (FFN_CONV - 1), 9))
        u = _taps(wins, w, te, b_ref[...])
        a = u[:, :D_FF]
        b = u[:, D_FF:]
        rowe = i * tr + lax.broadcasted_iota(jnp.int32, (te, 1), 0)
        dae = jnp.where(rowe < Lp, jnp.concatenate([da_ref[...], dan_ref[...]], axis=0), 0.0)
        sg = _sig(a)
        du = jnp.concatenate([dae * b * (sg * (1.0 + a * (1.0 - sg))), dae * (a * sg)], axis=1)
        dup_ref[...] = _taps(_shifted(du, range(FFN_CONV - 1, -1, -1)), w, tr).astype(BF16)
        dum = du[0:tr, :]
        rows = [jnp.sum(dum * wins[kk][0:tr, :], axis=0, keepdims=True) for kk in range(FFN_CONV)]
        rows.append(jnp.sum(dum, axis=0, keepdims=True))
        part = jnp.concatenate(rows + [jnp.zeros((8 - len(rows), W2), F32)], axis=0)

        @pl.when(i == 0)
        def _():
            acc_ref[...] = part

        @pl.when(i > 0)
        def _():
            acc_ref[...] += part

    return pl.pallas_call(
        body, grid=(Lp // tr,),
        in_specs=[pl.BlockSpec((tr, W2), lambda i: (i, 0)), _halo_prev(tr, W2), _halo_next(tr, W2, Lp),
                  pl.BlockSpec((tr, D_FF), lambda i: (i, 0)), _halo_next(tr, D_FF, Lp),
                  pl.BlockSpec((FFN_CONV, W2), lambda i: (0, 0)), pl.BlockSpec((1, W2), lambda i: (0, 0))],
        out_specs=[pl.BlockSpec((tr, W2), lambda i: (i, 0)), pl.BlockSpec((8, W2), lambda i: (0, 0))],
        out_shape=[jax.ShapeDtypeStruct((Lp, W2), BF16), jax.ShapeDtypeStruct((8, W2), F32)],
        name="ffn_act_bwd")(up, up, up, dact, dact, conv_w, conv_b)


def _local_step(hpad, tgt, pad, wt, first_weights=None, late_weights=None, on_ffn_out_grads=None,
                on_w_in_grads=None):
    Lp = hpad.shape[0]
    first = pad + N_META
    pos = jnp.arange(Lp, dtype=F32) - float(pad)
    half = RET_D // 2
    inv = 1.0 / (ROPE_BASE ** (jnp.arange(half, dtype=F32) / half))
    ang = pos[:, None] * inv[None, :]
    cos, sin = jnp.cos(ang), jnp.sin(ang)
    tables = _ret_tables()
    gparams = jnp.zeros((8, LANES), F32).at[0, :GDN_H].set(wt["a_log"]).at[1, :GDN_H].set(wt["dt_bias"])

    hn1 = _rms_fwd(hpad, wt["norm1"], "rms1_fwd")
    if first_weights is not None:
        wt = {**wt, **first_weights(hn1)}
    proj_m = _mm_nn(hn1, wt["w_main"], name="proj_main")
    proj_s = _mm_nn(hn1, wt["w_small"], name="proj_small")
    qkv, gsm = _gdn_pre(proj_m, proj_s, wt["gdn_conv_w"], gparams, pad)
    o_a, s_a, t_a = _gdn_chunk_fwd(qkv, gsm)
    o_b, s_b = _ret_chunk_fwd(proj_m, cos, sin, tables)
    y = _merge_fwd(o_a, o_b, proj_m, wt["gdn_norm"])
    if late_weights is not None:
        wt = {**wt, **late_weights(y)}
    h1 = _mm_nn(y, wt["w_out"], res=hpad, name="out_proj")
    hn2 = _rms_fwd(h1, wt["norm2"], "rms2_fwd")
    up = _mm_nn(hn2, wt["w_up"], name="ffn_up")
    act = _ffn_act(up, wt["ffn_conv_w"], wt["ffn_conv_b"])
    h2 = _mm_nn(act, wt["w_down"], res=h1, name="ffn_down")
    lossvec, dh2, dh2b, d_norm_f = _final(h2, wt["norm_f"], tgt, first)

    d_w_down = _mm_tn(act, dh2b, name="dw_down")
    dact = _mm_nt(dh2b, wt["w_down"], name="d_act")
    dup, ffn_rows = _ffn_act_bwd(up, dact, wt["ffn_conv_w"], wt["ffn_conv_b"])
    d_w_up = _mm_tn(hn2, dup, name="dw_up")
    dhn2 = _mm_nt(dup, wt["w_up"], name="d_hn2")
    dh1, dh1b, d_norm2 = _rms_bwd(h1, wt["norm2"], dhn2, dh2, pad, "rms2_bwd")

    d_w_out = _mm_tn(y, dh1b, name="dw_out")
    dy = _mm_nt(dh1b, wt["w_out"], name="d_y")
    gnorm = wt["gdn_norm"]
    if on_ffn_out_grads is not None:
        gnorm = gnorm + on_ffn_out_grads(d_w_down, d_w_up, d_w_out)[0:1, :]
    d_c, do_a, do_b, d_gnorm = _merge_bwd(dy, o_a, o_b, proj_m, gnorm)
    drq, drk, drv = _ret_chunk_bwd(proj_m, cos, sin, tables, do_b, s_b)
    dq, dk, dv, dgs = _gdn_chunk_bwd(qkv, gsm, do_a, s_a, t_a)
    d_a, d_s, conv_rows, gp_rows = _gdn_pre_bwd(proj_m, proj_s, wt["gdn_conv_w"], gparams, dq, dk, dv, dgs, pad)

    wm = wt["w_main"]
    segs = [(d_a, 0, 3 * D_MODEL), (drq, 3 * D_MODEL, D_MODEL), (drk, 4 * D_MODEL, D_MODEL),
            (drv, 5 * D_MODEL, D_MODEL), (d_c, 6 * D_MODEL, 4 * D_MODEL)]
    d_w_main = jnp.concatenate([_mm_tn(hn1, d, name="dw_in_%d" % i) for i, (d, _, _) in enumerate(segs)], axis=1)
    d_w_small = _mm_tn(hn1, d_s, name="dw_in_small")
    w_small = wt["w_small"]
    if on_w_in_grads is not None:
        w_small = w_small + on_w_in_grads(d_w_main, d_w_small)[0:1, :].astype(w_small.dtype)
    dhn1 = _mm_nt(d_s, w_small, name="d_hn1_small")
    for i, (d, off, width) in enumerate(segs):
        dhn1 = _mm_nt(d, wm[:, off:off + width], res=dhn1, name="d_hn1_%d" % i)
    dh0, _, d_norm1 = _rms_bwd(hpad, wt["norm1"], dhn1, dh1, pad, "rms1_bwd")

    grads = {
        "norm1": d_norm1, "w_main": d_w_main, "w_small": d_w_small, "gdn_conv_w": conv_rows[:GDN_CONV],
        "a_log": gp_rows[0, :GDN_H], "dt_bias": gp_rows[1, :GDN_H], "gdn_norm": d_gnorm, "w_out": d_w_out,
        "norm2": d_norm2, "w_up": d_w_up, "ffn_conv_w": ffn_rows[:FFN_CONV], "ffn_conv_b": ffn_rows[FFN_CONV:FFN_CONV + 1],
        "w_down": d_w_down, "norm_f": d_norm_f,
    }
    return lossvec, dh0, grads


def _peer(k):
    ix, iy, ic = lax.axis_index("x"), lax.axis_index("y"), lax.axis_index("c")
    px = 1 - ix if (k >> 2) & 1 else ix
    py = 1 - iy if (k >> 1) & 1 else iy
    pc = 1 - ic if k & 1 else ic
    return (px, py, pc), 4 * px + 2 * py + pc


def _comm_call(body, n, out_shapes, name, args):
    hbm = pl.BlockSpec(memory_space=pl.ANY)
    return pl.pallas_call(
        body, out_shape=out_shapes, in_specs=[hbm] * n, out_specs=[hbm] * n,
        scratch_shapes=[pltpu.SemaphoreType.DMA((n, N_DEV - 1)), pltpu.SemaphoreType.DMA((n, N_DEV - 1)),
                        pltpu.SemaphoreType.DMA((n,))],
        name=name)(*args)


def _all_gather(xs, name):
    n = len(xs)

    def body(*refs):
        x_refs, out_refs = refs[:n], refs[n:2 * n]
        send_sems, recv_sems, local_sems = refs[2 * n:]
        _, me = _peer(0)
        pending = []
        for i in range(n):
            local = pltpu.make_async_copy(x_refs[i], out_refs[i].at[me], local_sems.at[i])
            local.start()
            pending.append(local)
        sends = []
        for i in range(n):
            for k in range(1, N_DEV):
                dev, _ = _peer(k)
                cp = pltpu.make_async_remote_copy(
                    src_ref=x_refs[i], dst_ref=out_refs[i].at[me], send_sem=send_sems.at[i, k - 1],
                    recv_sem=recv_sems.at[i, k - 1], device_id=dev, device_id_type=MESH_T)
                cp.start()
                sends.append(cp)
        for i in range(n):
            for k in range(1, N_DEV):
                dev, idx = _peer(k)
                pltpu.make_async_remote_copy(
                    src_ref=x_refs[i], dst_ref=out_refs[i].at[idx], send_sem=send_sems.at[i, k - 1],
                    recv_sem=recv_sems.at[i, k - 1], device_id=dev, device_id_type=MESH_T).wait_recv()
        for cp in sends:
            cp.wait_send()
        for local in pending:
            local.wait()

    out_shapes = [jax.ShapeDtypeStruct((N_DEV,) + a.shape, a.dtype) for a in xs]
    return _comm_call(body, n, out_shapes, name, xs)


def _all_to_all(gs, name):
    n = len(gs)

    def body(*refs):
        g_refs, out_refs = refs[:n], refs[n:2 * n]
        send_sems, recv_sems, local_sems = refs[2 * n:]
        _, me = _peer(0)
        pending = []
        for i in range(n):
            local = pltpu.make_async_copy(g_refs[i].at[me], out_refs[i].at[0], local_sems.at[i])
            local.start()
            pending.append(local)
        sends = []
        for i in range(n):
            for k in range(1, N_DEV):
                dev, idx = _peer(k)
                cp = pltpu.make_async_remote_copy(
                    src_ref=g_refs[i].at[idx], dst_ref=out_refs[i].at[k], send_sem=send_sems.at[i, k - 1],
                    recv_sem=recv_sems.at[i, k - 1], device_id=dev, device_id_type=MESH_T)
                cp.start()
                sends.append(cp)
        for cp in sends:
            cp.wait_recv()
        for cp in sends:
            cp.wait_send()
        for local in pending:
            local.wait()

    out_shapes = [jax.ShapeDtypeStruct(g.shape, g.dtype) for g in gs]
    return _comm_call(body, n, out_shapes, name, gs)


def _split_copies(kind, src_refs, land_refs, send_sems, recv_sems, local_sems, with_recv):
    n = len(src_refs)
    _, me = _peer(0)
    locals_, remotes = [], []
    for i in range(n):
        if kind == "gather":
            locals_.append(pltpu.make_async_copy(src_refs[i], land_refs[i].at[me], local_sems.at[i]))
        else:
            locals_.append(pltpu.make_async_copy(src_refs[i].at[me], land_refs[i].at[0], local_sems.at[i]))
        for k in range(1, N_DEV):
            dev, idx = _peer(k)
            if kind == "gather":
                src, dst, mine = src_refs[i], land_refs[i].at[me], land_refs[i].at[idx]
            else:
                src, dst, mine = src_refs[i].at[idx], land_refs[i].at[k], land_refs[i].at[k]
            j = i * (N_DEV - 1) + k - 1
            send = pltpu.make_async_remote_copy(
                src_ref=src, dst_ref=dst, send_sem=send_sems.at[j], recv_sem=recv_sems.at[j],
                device_id=dev, device_id_type=MESH_T)
            recv = pltpu.make_async_remote_copy(
                src_ref=src, dst_ref=mine, send_sem=send_sems.at[j], recv_sem=recv_sems.at[j],
                device_id=dev, device_id_type=MESH_T) if with_recv else None
            remotes.append((send, recv))
    return locals_, remotes


_HBM = pl.BlockSpec(memory_space=pltpu.HBM)
_SEM = pl.BlockSpec(memory_space=pltpu.SEMAPHORE)
_ANY = pl.BlockSpec(memory_space=pl.ANY)


def _split_start(srcs, kind, name, after):
    n = len(srcs)
    lands = [lax.empty(((N_DEV,) + a.shape) if kind == "gather" else a.shape, a.dtype) for a in srcs]

    def body(*refs):
        src_refs, land_refs = refs[:n], refs[n:2 * n]
        send_sems, recv_sems, local_sems = refs[2 * n + 1:2 * n + 4]
        token = refs[-1]
        locals_, remotes = _split_copies(kind, src_refs, land_refs, send_sems, recv_sems, local_sems, False)
        for cp in locals_:
            cp.start()
        for send, _ in remotes:
            send.start()
        token[...] = jnp.zeros_like(token)

    sems = (pltpu.SemaphoreType.DMA((n * (N_DEV - 1),)), pltpu.SemaphoreType.DMA((n * (N_DEV - 1),)),
            pltpu.SemaphoreType.DMA((n,)))
    thru = tuple(pltpu.HBM(a.shape, a.dtype) for a in list(srcs) + lands)
    outs = pl.pallas_call(
        body, name=name,
        out_shape=sems + thru + (jax.ShapeDtypeStruct((8, LANES), F32),),
        in_specs=[_HBM] * (2 * n) + [_ANY],
        out_specs=[_SEM] * 3 + [_HBM] * (2 * n) + [pl.BlockSpec(memory_space=pltpu.VMEM)],
        input_output_aliases={i: 3 + i for i in range(2 * n)},
        compiler_params=pltpu.CompilerParams(has_side_effects=pltpu.SideEffectType.DATAFLOW_SIDE_EFFECTING),
    )(*[pltpu.with_memory_space_constraint(a, pltpu.HBM) for a in list(srcs) + lands], after)
    return (kind, n, outs[:3], outs[3:3 + 2 * n]), outs[-1]


def _split_wait(handle, name, after):
    kind, n, sems, thru = handle

    def body(*refs):
        src_refs, land_refs = refs[:n], refs[n:2 * n]
        send_sems, recv_sems, local_sems = refs[2 * n:2 * n + 3]
        locals_, remotes = _split_copies(kind, src_refs, land_refs, send_sems, recv_sems, local_sems, True)
        for send, recv in remotes:
            send.wait_send()
            recv.wait_recv()
        for cp in locals_:
            cp.wait()

    outs = pl.pallas_call(
        body, name=name, out_shape=tuple(pltpu.HBM(a.shape, a.dtype) for a in thru),
        in_specs=[_HBM] * (2 * n) + [_SEM] * 3 + [_ANY], out_specs=[_HBM] * (2 * n),
        input_output_aliases={i: i for i in range(2 * n)},
        compiler_params=pltpu.CompilerParams(has_side_effects=pltpu.SideEffectType.DATAFLOW_SIDE_EFFECTING),
    )(*thru, *sems, after)
    return list(outs[n:])


def _adamw(gslabs, w, m, v, name):
    R, Cw = w.shape
    tr = _tile(R, 64 if Cw > 1024 else 128, 8)
    c1 = 1.0 - ADAM_B1 ** ADAM_STEP
    c2 = 1.0 - ADAM_B2 ** ADAM_STEP

    def body(g_ref, w_ref, m_ref, v_ref, go_ref, d_ref, mo_ref, vo_ref):
        g = g_ref[0].astype(F32)
        for k in range(1, N_DEV):
            g = g + g_ref[k].astype(F32)
        mn = ADAM_B1 * m_ref[...] + (1.0 - ADAM_B1) * g
        vn = ADAM_B2 * v_ref[...] + (1.0 - ADAM_B2) * (g * g)
        m_hat = mn / c1
        v_hat = vn / c2
        go_ref[...] = g
        d_ref[...] = -ADAM_LR * (m_hat / (jnp.sqrt(v_hat) + ADAM_EPS) + ADAM_WD * w_ref[...])
        mo_ref[...] = mn
        vo_ref[...] = vn

    blk = pl.BlockSpec((tr, Cw), lambda i: (i, 0))
    return pl.pallas_call(
        body, grid=(R // tr,),
        in_specs=[pl.BlockSpec((N_DEV, tr, Cw), lambda i: (0, i, 0)), blk, blk, blk],
        out_specs=[blk] * 4, out_shape=[jax.ShapeDtypeStruct((R, Cw), F32)] * 4, name=name)(gslabs, w, m, v)


def _pack(arrs, row_mult, dtype=F32):
    parts = []
    total = 0
    for a in arrs:
        f = a.reshape(-1).astype(dtype)
        n = -(-f.shape[0] // 1024) * 1024
        parts.append(jnp.pad(f, (0, n - f.shape[0])))
        total += n
    rows = total // LANES
    rows_p = -(-rows // row_mult) * row_mult
    flat = jnp.concatenate(parts)
    flat = jnp.pad(flat, (0, rows_p * LANES - total))
    return flat.reshape(rows_p, LANES)


def _unpack(packed, shapes):
    lead = packed.shape[:-2]
    flat = packed.reshape(lead + (-1,))
    out = []
    off = 0
    for s in shapes:
        n = int(np.prod(s))
        out.append(flat[..., off:off + n].reshape(lead + tuple(s)))
        off += -(-n // 1024) * 1024
    return out


def _gather_cols(stacked):
    d, r, c = stacked.shape
    return stacked.transpose(1, 0, 2).reshape(r, d * c)


def _scatter_cols(full):
    r, n = full.shape
    return full.reshape(r, N_DEV, n // N_DEV).transpose(1, 0, 2)


def kernel(x, meta, norm1, w_in, gdn_conv_w, gdn_a_log, gdn_dt_bias, gdn_norm, w_out, norm2, w_ffn_up, ffn_conv_w, ffn_conv_b, w_ffn_down, norm_f, loss_target, m_meta, m_norm1, m_w_in, m_gdn_conv_w, m_gdn_a_log, m_gdn_dt_bias, m_gdn_norm, m_w_out, m_norm2, m_w_ffn_up, m_ffn_conv_w, m_ffn_conv_b, m_w_ffn_down, m_norm_f, v_meta, v_norm1, v_w_in, v_gdn_conv_w, v_gdn_a_log, v_gdn_dt_bias, v_gdn_norm, v_w_out, v_norm2, v_w_ffn_up, v_ffn_conv_w, v_ffn_conv_b, v_w_ffn_down, v_norm_f):
    S = x.shape[1]
    L = N_META + S
    pad = (-L) % CHUNK
    Lp = L + pad

    big = [w_in[0], w_out[0], w_ffn_up[0], w_ffn_down[0]]
    small = [meta, gdn_conv_w, ffn_conv_w]
    small_all, = _all_gather([_pack(small, 8)], "gather_small_weights")
    first, first_token = _split_start([big[0].astype(BF16)], "gather", "gather_w_in_start", small_all)
    late, late_token = _split_start([a.astype(BF16) for a in big[1:]], "gather", "gather_late_start", first_token)

    def first_weights(after):
        w_in_s, = _split_wait(first, "gather_w_in_wait", after)
        w_in_f = _gather_cols(w_in_s)
        w_main = jnp.concatenate([w_in_f[:, _O_GQ:_O_GZ], w_in_f[:, _O_RQ:_O_RG], w_in_f[:, _O_GZ:_O_GA],
                                  w_in_f[:, _O_RG:_O_END]], axis=1)
        return {"w_main": w_main, "w_small": jnp.pad(w_in_f[:, _O_GA:_O_RQ], ((0, 0), (0, LANES - 2 * GDN_H)))}

    def late_weights(after):
        w_out_s, w_up_s, w_down_s = _split_wait(late, "gather_late_wait", after)
        return {"w_out": w_out_s.reshape(D_MODEL, D_MODEL), "w_up": _gather_cols(w_up_s),
                "w_down": w_down_s.reshape(D_FF, D_MODEL)}

    meta_s, gconv_s, fconv_s = _unpack(small_all, [a.shape for a in small])
    wt = {
        "norm1": norm1 + jnp.tile(late_token[0:1, :], (1, D_MODEL // LANES)),
        "gdn_conv_w": _gather_cols(gconv_s[:, 0]), "a_log": gdn_a_log[0], "dt_bias": gdn_dt_bias[0],
        "gdn_norm": gdn_norm, "norm2": norm2, "ffn_conv_w": _gather_cols(fconv_s[:, 0]), "ffn_conv_b": ffn_conv_b,
        "norm_f": norm_f.reshape(1, D_MODEL),
    }
    meta_f = _gather_cols(meta_s)

    pending = {}

    def on_ffn_out_grads(d_w_down, d_w_up, d_w_out):
        srcs = [d_w_out.reshape(N_DEV, D_MODEL // N_DEV, D_MODEL), _scatter_cols(d_w_up),
                d_w_down.reshape(N_DEV, D_FF // N_DEV, D_MODEL)]
        pending["ffn_out"], token = _split_start(srcs, "a2a", "exchange_ffn_out_start", d_w_out)
        return token

    def on_w_in_grads(gm, gs):
        d_w_in = jnp.concatenate([gm[:, 0:3072], gm[:, 6144:7168], gs[:, :2 * GDN_H], gm[:, 3072:6144],
                                  gm[:, 7168:]], axis=1)
        pending["w_in"], token = _split_start([_scatter_cols(d_w_in.astype(BF16))], "a2a", "exchange_w_in_start", gs)
        return token

    hpad = jnp.concatenate([jnp.zeros((pad, D_MODEL), F32), meta_f, x[0]], axis=0)
    tgt = jnp.concatenate([jnp.zeros((pad + N_META, D_MODEL), F32), loss_target[0]], axis=0)
    lossvec, dh0, gr = _local_step(hpad, tgt, pad, wt, first_weights, late_weights, on_ffn_out_grads, on_w_in_grads)

    loss = lax.psum(jnp.sum(lossvec), ("x", "y", "c"))
    grad_x = dh0[pad + N_META:][None]

    big_m = [m_w_in[0], m_w_out[0], m_w_ffn_up[0], m_w_ffn_down[0]]
    big_v = [v_w_in[0], v_w_out[0], v_w_ffn_up[0], v_w_ffn_down[0]]
    slabs_ffn_out = _split_wait(pending["ffn_out"], "exchange_ffn_out_wait", dh0)
    big_out = [None] + [_adamw(slabs_ffn_out[i - 1], big[i], big_m[i], big_v[i], "adamw_big_%d" % i)
                        for i in range(1, len(big))]
    g_sm = [_scatter_cols(dh0[pad:pad + N_META]), _scatter_cols(gr["gdn_conv_w"]), _scatter_cols(gr["ffn_conv_w"])]
    g_small = jnp.stack([_pack([g[d] for g in g_sm], 8) for d in range(N_DEV)])
    slabs_small, = _all_to_all([g_small], "exchange_small_gradients")
    small_out = _adamw(slabs_small, _pack(small, 8), _pack([m_meta, m_gdn_conv_w, m_ffn_conv_w], 8),
                       _pack([v_meta, v_gdn_conv_w, v_ffn_conv_w], 8), "adamw_small_sharded")
    small_un = [_unpack(o, [a.shape for a in small]) for o in small_out]
    rep_w = [norm1, gdn_a_log, gdn_dt_bias, gdn_norm, norm2, ffn_conv_b, norm_f]
    rep_m = [m_norm1, m_gdn_a_log, m_gdn_dt_bias, m_gdn_norm, m_norm2, m_ffn_conv_b, m_norm_f]
    rep_v = [v_norm1, v_gdn_a_log, v_gdn_dt_bias, v_gdn_norm, v_norm2, v_ffn_conv_b, v_norm_f]
    rep_g = [gr["norm1"], gr["a_log"], gr["dt_bias"], gr["gdn_norm"], gr["norm2"], gr["ffn_conv_b"], gr["norm_f"]]
    rep_slabs, = _all_gather([_pack(rep_g, 8)], "gather_small_gradients")
    rep_out = _adamw(rep_slabs, _pack(rep_w, 8), _pack(rep_m, 8), _pack(rep_v, 8), "adamw_replicated")
    rep_shapes = [a.shape for a in rep_w]
    rp_g, rp_d, rp_nm, rp_nv = [_unpack(o, rep_shapes) for o in rep_out]

    slabs_w_in, = _split_wait(pending["w_in"], "exchange_w_in_wait", rep_out[0])
    big_out[0] = _adamw(slabs_w_in, big[0], big_m[0], big_v[0], "adamw_big_0")
    sh_g, sh_d, sh_nm, sh_nv = [
        [small_un[j][0], big_out[0][j][None], small_un[j][1], big_out[1][j][None], big_out[2][j][None],
         small_un[j][2], big_out[3][j][None]] for j in range(4)]

    def order(sh, rp):
        return [sh[0], rp[0], sh[1], sh[2], rp[1], rp[2], rp[3], sh[3], rp[4], sh[4], sh[5], rp[5], sh[6], rp[6]]

    return (loss, grad_x, *order(sh_g, rp_g), *order(sh_d, rp_d), *order(sh_nm, rp_nm), *order(sh_nv, rp_nv))
```

```python
import functools
import math

import numpy as np
import jax
import jax.numpy as jnp
from jax import lax
from jax.experimental import pallas as pl
from jax.experimental.pallas import tpu as pltpu

F32 = jnp.float32
BF16 = jnp.bfloat16
HI = lax.Precision.HIGHEST

D_MODEL = 1024
N_META = 16
CHUNK = 64
GDN_H = 8
GDN_D = 128
RET_H = 4
RET_D = 256
D_FF = 2816
GDN_CONV = 4
FFN_CONV = 3
ROPE_BASE = 10000.0
EPS = 1e-6
N_DEV = 8
LANES = 128
MAIN_W = 10 * 1024
_O_GQ, _O_GZ, _O_GA, _O_RQ, _O_RG, _O_GATE, _O_END = 0, 3072, 4096, 4112, 7184, 8208, 10256

ADAM_LR = 0.001
ADAM_B1 = 0.9
ADAM_B2 = 0.999
ADAM_EPS = 1e-08
ADAM_WD = 0.01
ADAM_STEP = 10

MESH_T = pl.DeviceIdType.MESH


def _tile(n, target, mult):
    best = None
    for d in range(mult, min(n, target) + 1, mult):
        if n % d == 0:
            best = d
    assert best is not None, (n, target, mult)
    return best


def _sig(x):
    return 1.0 / (1.0 + jnp.exp(-x))


def _d(a, b):
    return jnp.dot(a.astype(BF16), b.astype(BF16), preferred_element_type=F32)


def _dnt(a, b):
    return lax.dot_general(a.astype(BF16), b.astype(BF16), (((1,), (1,)), ((), ())), preferred_element_type=F32)


def _dtn(a, b):
    return lax.dot_general(a.astype(BF16), b.astype(BF16), (((0,), (0,)), ((), ())), preferred_element_type=F32)


def _dx(a, b):
    return jnp.dot(a, b, preferred_element_type=F32, precision=HI)


def _dxnt(a, b):
    return lax.dot_general(a, b, (((1,), (1,)), ((), ())), preferred_element_type=F32, precision=HI)


def _dxtn(a, b):
    return lax.dot_general(a, b, (((0,), (0,)), ((), ())), preferred_element_type=F32, precision=HI)


def _split(a):
    hi = a.astype(BF16)
    return hi, (a - hi.astype(F32)).astype(BF16)


def _d3g(a, b, dims):
    ah, al = _split(a)
    bh, bl = _split(b)
    f = functools.partial(lax.dot_general, dimension_numbers=dims, preferred_element_type=F32)
    return f(ah, bh) + (f(ah, bl) + f(al, bh))


_NN = (((1,), (0,)), ((), ()))
_NT = (((1,), (1,)), ((), ()))
_TN = (((0,), (0,)), ((), ()))


def _rowsum(x):
    return jnp.sum(x, axis=1, keepdims=True)


def _allsum(x):
    return jnp.sum(jnp.sum(x, axis=1, keepdims=True), axis=0, keepdims=True)


def _mm_nn(a, b, res=None, out_dtype=F32, bt=False, name="mm_nn"):
    M, K = a.shape
    N = b.shape[0] if bt else b.shape[1]
    tm = _tile(M, 704, 16)
    tn = _tile(N, 2816, 128)

    def body(*refs):
        if res is None:
            a_ref, b_ref, o_ref = refs
        else:
            a_ref, b_ref, r_ref, o_ref = refs
        acc = lax.dot_general(a_ref[...], b_ref[...], _NT if bt else _NN, preferred_element_type=F32)
        if res is not None:
            acc = acc + r_ref[...]
        o_ref[...] = acc.astype(out_dtype)

    b_spec = pl.BlockSpec((tn, K), lambda j, i: (j, 0)) if bt else pl.BlockSpec((K, tn), lambda j, i: (0, j))
    in_specs = [pl.BlockSpec((tm, K), lambda j, i: (i, 0)), b_spec]
    args = [a, b]
    if res is not None:
        in_specs.append(pl.BlockSpec((tm, tn), lambda j, i: (i, j)))
        args.append(res)
    return pl.pallas_call(
        body, grid=(N // tn, M // tm), in_specs=in_specs,
        out_specs=pl.BlockSpec((tm, tn), lambda j, i: (i, j)),
        out_shape=jax.ShapeDtypeStruct((M, N), out_dtype), name=name)(*args)


def _mm_nt(a, b, res=None, name="mm_nt"):
    M, Nc = a.shape
    K = b.shape[0]
    tm = _tile(M, 704, 16)
    tc = _tile(Nc, 5632, 128)

    def body(*refs):
        if res is None:
            a_ref, b_ref, o_ref = refs
        else:
            a_ref, b_ref, r_ref, o_ref = refs
        c = pl.program_id(1)
        p = lax.dot_general(a_ref[...], b_ref[...], (((1,), (1,)), ((), ())), preferred_element_type=F32)

        @pl.when(c == 0)
        def _():
            if res is None:
                o_ref[...] = p
            else:
                o_ref[...] = p + r_ref[...]

        @pl.when(c > 0)
        def _():
            o_ref[...] += p

    in_specs = [pl.BlockSpec((tm, tc), lambda i, c: (i, c)), pl.BlockSpec((K, tc), lambda i, c: (0, c))]
    args = [a, b]
    if res is not None:
        in_specs.append(pl.BlockSpec((tm, K), lambda i, c: (i, 0)))
        args.append(res)
    return pl.pallas_call(
        body, grid=(M // tm, Nc // tc), in_specs=in_specs,
        out_specs=pl.BlockSpec((tm, K), lambda i, c: (i, 0)),
        out_shape=jax.ShapeDtypeStruct((M, K), F32), name=name)(*args)


def _mm_tn(a, b, name="mm_tn"):
    M, K = a.shape
    N = b.shape[1]
    tm = _tile(M, 2752, 16)
    tk = _tile(K, 1408, 128)
    tn = _tile(N, 1408, 128)

    def body(a_ref, b_ref, o_ref):
        m = pl.program_id(2)
        p = lax.dot_general(a_ref[...], b_ref[...], (((0,), (0,)), ((), ())), preferred_element_type=F32)

        @pl.when(m == 0)
        def _():
            o_ref[...] = p

        @pl.when(m > 0)
        def _():
            o_ref[...] += p

    return pl.pallas_call(
        body, grid=(K // tk, N // tn, M // tm),
        in_specs=[pl.BlockSpec((tm, tk), lambda kk, j, m: (m, kk)), pl.BlockSpec((tm, tn), lambda kk, j, m: (m, j))],
        out_specs=pl.BlockSpec((tk, tn), lambda kk, j, m: (kk, j)),
        out_shape=jax.ShapeDtypeStruct((K, N), F32), name=name)(a, b)


def _rms_fwd(x, g, name):
    Lp = x.shape[0]
    tr = _tile(Lp, 256, 16)

    def body(x_ref, g_ref, o_ref):
        xv = x_ref[...]
        r = lax.rsqrt(jnp.mean(xv * xv, axis=-1, keepdims=True) + EPS)
        o_ref[...] = (xv * r * g_ref[...]).astype(BF16)

    return pl.pallas_call(
        body, grid=(Lp // tr,),
        in_specs=[pl.BlockSpec((tr, D_MODEL), lambda i: (i, 0)), pl.BlockSpec((1, D_MODEL), lambda i: (0, 0))],
        out_specs=pl.BlockSpec((tr, D_MODEL), lambda i: (i, 0)),
        out_shape=jax.ShapeDtypeStruct((Lp, D_MODEL), BF16), name=name)(x, g)


def _rms_bwd(x, g, dy, dres, pad, name):
    Lp = x.shape[0]
    tr = _tile(Lp, 256, 16)

    def body(x_ref, g_ref, dy_ref, dr_ref, dx_ref, dxb_ref, dg_ref):
        i = pl.program_id(0)
        xv = x_ref[...]
        r = lax.rsqrt(jnp.mean(xv * xv, axis=-1, keepdims=True) + EPS)
        xh = xv * r
        dyv = dy_ref[...]
        dxh = dyv * g_ref[...]
        dx = r * (dxh - xh * jnp.mean(dxh * xh, axis=-1, keepdims=True)) + dr_ref[...]
        row = i * tr + lax.broadcasted_iota(jnp.int32, (tr, 1), 0)
        dx = jnp.where(row >= pad, dx, 0.0)
        dx_ref[...] = dx
        dxb_ref[...] = dx.astype(BF16)
        part = jnp.sum(dyv * xh, axis=0, keepdims=True)

        @pl.when(i == 0)
        def _():
            dg_ref[...] = part

        @pl.when(i > 0)
        def _():
            dg_ref[...] += part

    blk = pl.BlockSpec((tr, D_MODEL), lambda i: (i, 0))
    vec = pl.BlockSpec((1, D_MODEL), lambda i: (0, 0))
    return pl.pallas_call(
        body, grid=(Lp // tr,), in_specs=[blk, vec, blk, blk], out_specs=[blk, blk, vec],
        out_shape=[jax.ShapeDtypeStruct((Lp, D_MODEL), F32), jax.ShapeDtypeStruct((Lp, D_MODEL), BF16),
                   jax.ShapeDtypeStruct((1, D_MODEL), F32)], name=name)(x, g, dy, dres)


def _final(h2, g, tgt, first_row):
    Lp = h2.shape[0]
    tr = _tile(Lp, 256, 16)

    def body(x_ref, g_ref, t_ref, loss_ref, dx_ref, dxb_ref, dg_ref):
        i = pl.program_id(0)
        xv = x_ref[...]
        gv = g_ref[...]
        r = lax.rsqrt(jnp.mean(xv * xv, axis=-1, keepdims=True) + EPS)
        xh = xv * r
        row = i * tr + lax.broadcasted_iota(jnp.int32, (tr, 1), 0)
        err = jnp.where(row >= first_row, xh * gv - t_ref[...], 0.0)
        lpart = jnp.sum(err * err, axis=0, keepdims=True) * (0.5 / D_MODEL)
        dyv = err * (1.0 / D_MODEL)
        dxh = dyv * gv
        dx = r * (dxh - xh * jnp.mean(dxh * xh, axis=-1, keepdims=True))
        dx_ref[...] = dx
        dxb_ref[...] = dx.astype(BF16)
        part = jnp.sum(dyv * xh, axis=0, keepdims=True)

        @pl.when(i == 0)
        def _():
            dg_ref[...] = part
            loss_ref[...] = lpart

        @pl.when(i > 0)
        def _():
            dg_ref[...] += part
            loss_ref[...] += lpart

    blk = pl.BlockSpec((tr, D_MODEL), lambda i: (i, 0))
    vec = pl.BlockSpec((1, D_MODEL), lambda i: (0, 0))
    return pl.pallas_call(
        body, grid=(Lp // tr,), in_specs=[blk, vec, blk], out_specs=[vec, blk, blk, vec],
        out_shape=[jax.ShapeDtypeStruct((1, D_MODEL), F32), jax.ShapeDtypeStruct((Lp, D_MODEL), F32),
                   jax.ShapeDtypeStruct((Lp, D_MODEL), BF16), jax.ShapeDtypeStruct((1, D_MODEL), F32)],
        name="final_norm_loss")(h2, g, tgt)


def _halo_prev(tr, width, col=0):
    return pl.BlockSpec((8, width), lambda i: (jnp.maximum(i * (tr // 8) - 1, 0), col))


def _halo_next(tr, width, nrows, col=0):
    last = nrows // 8 - 1
    return pl.BlockSpec((8, width), lambda i: (jnp.minimum((i + 1) * (tr // 8), last), col))


def _shifted(x, offs):
    n = x.shape[0]
    return [x if off == 0 else pltpu.roll(x, n - off, 0) for off in offs]


def _taps(wins, w, rows, bias=None):
    acc = w[0:1, :] * wins[0][0:rows, :]
    if bias is not None:
        acc = acc + bias
    for kk in range(1, len(wins)):
        acc = acc + w[kk:kk + 1, :] * wins[kk][0:rows, :]
    return acc


def _gdn_pre(proj_m, proj_s, conv_w, gparams, pad):
    Lp = proj_m.shape[0]
    tr = _tile(Lp, 192, 64)
    W3 = 3 * D_MODEL

    def body(main_ref, prev_ref, s_ref, w_ref, gp_ref, qkv_ref, gsm_ref):
        i = pl.program_id(0)
        prev = jnp.where(i > 0, prev_ref[...], 0.0)
        ext = jnp.concatenate([prev, main_ref[...]], axis=0)
        c = _taps(_shifted(ext, range(8 - (GDN_CONV - 1), 9)), w_ref[...], tr)
        s = c * _sig(c)
        scale = GDN_D ** -0.5
        for j in range(2 * GDN_H):
            seg = s[:, j * GDN_D:(j + 1) * GDN_D]
            r = lax.rsqrt(_rowsum(seg * seg) + EPS)
            if j < GDN_H:
                r = r * scale
            qkv_ref[:, j * GDN_D:(j + 1) * GDN_D] = seg * r
        qkv_ref[:, 2 * D_MODEL:] = s[:, 2 * D_MODEL:]
        sm = s_ref[...]
        gp = gp_ref[...]
        lane = lax.broadcasted_iota(jnp.int32, sm.shape, 1)
        z = sm + gp[1:2, :]
        softplus = jnp.maximum(z, 0.0) + jnp.log(1.0 + jnp.exp(-jnp.abs(z)))
        lg = -jnp.exp(gp[0:1, :]) * softplus
        row = i * tr + lax.broadcasted_iota(jnp.int32, (tr, 1), 0)
        out = jnp.where(lane < GDN_H, lg, jnp.where(lane < 2 * GDN_H, _sig(sm), 0.0))
        gsm_ref[...] = jnp.where(row >= pad, out, 0.0)

    return pl.pallas_call(
        body, grid=(Lp // tr,),
        in_specs=[pl.BlockSpec((tr, W3), lambda i: (i, 0)), _halo_prev(tr, W3),
                  pl.BlockSpec((tr, LANES), lambda i: (i, 0)),
                  pl.BlockSpec((GDN_CONV, W3), lambda i: (0, 0)), pl.BlockSpec((8, LANES), lambda i: (0, 0))],
        out_specs=[pl.BlockSpec((tr, W3), lambda i: (i, 0)), pl.BlockSpec((tr, LANES), lambda i: (i, 0))],
        out_shape=[jax.ShapeDtypeStruct((Lp, W3), F32), jax.ShapeDtypeStruct((Lp, LANES), F32)],
        name="gdn_pre")(proj_m, proj_m, proj_s, conv_w, gparams)


def _gdn_pre_bwd(proj_m, proj_s, conv_w, gparams, dq, dk, dv, dgs, pad):
    Lp = proj_m.shape[0]
    tr = _tile(Lp, 192, 64)
    W3 = 3 * D_MODEL
    te = tr + 8

    def body(main_ref, prev_ref, next_ref, s_ref, w_ref, gp_ref,
             dq_ref, dqn_ref, dk_ref, dkn_ref, dv_ref, dvn_ref, dgs_ref,
             da_ref, ds_ref, dw_ref, dgp_ref):
        i = pl.program_id(0)
        w = w_ref[...]
        prev = jnp.where(i > 0, prev_ref[...], 0.0)
        ext = jnp.concatenate([prev, main_ref[...], next_ref[...]], axis=0)
        wins = _shifted(ext, range(8 - (GDN_CONV - 1), 9))
        c = _taps(wins, w, te)
        sg = _sig(c)
        s = c * sg
        rowe = i * tr + lax.broadcasted_iota(jnp.int32, (te, 1), 0)
        live = (rowe >= pad) & (rowe < Lp)
        dqe = jnp.concatenate([dq_ref[...], dqn_ref[...]], axis=0)
        dke = jnp.concatenate([dk_ref[...], dkn_ref[...]], axis=0)
        dve = jnp.concatenate([dv_ref[...], dvn_ref[...]], axis=0)
        scale = GDN_D ** -0.5
        parts = []
        for j in range(2 * GDN_H):
            seg = s[:, j * GDN_D:(j + 1) * GDN_D]
            r = lax.rsqrt(_rowsum(seg * seg) + EPS)
            xh = seg * r
            if j < GDN_H:
                dxh = dqe[:, j * GDN_D:(j + 1) * GDN_D] * scale
            else:
                dxh = dke[:, (j - GDN_H) * GDN_D:(j - GDN_H + 1) * GDN_D]
            parts.append(r * (dxh - xh * _rowsum(dxh * xh)))
        parts.append(dve)
        dsv = jnp.concatenate(parts, axis=1)
        dc = jnp.where(live, dsv * (sg * (1.0 + c * (1.0 - sg))), 0.0)
        da_ref[...] = _taps(_shifted(dc, range(GDN_CONV - 1, -1, -1)), w, tr).astype(BF16)
        dcm = dc[0:tr, :]
        rows = [jnp.sum(dcm * wins[kk][0:tr, :], axis=0, keepdims=True) for kk in range(GDN_CONV)]
        dwp = jnp.concatenate(rows + [jnp.zeros((8 - GDN_CONV, W3), F32)], axis=0)

        sm = s_ref[...]
        gp = gp_ref[...]
        lane = lax.broadcasted_iota(jnp.int32, sm.shape, 1)
        rowm = i * tr + lax.broadcasted_iota(jnp.int32, (tr, 1), 0)
        dgv = jnp.where(rowm >= pad, dgs_ref[...], 0.0)
        dlg = jnp.where(lane < GDN_H, dgv, 0.0)
        dbt = jnp.where((lane >= GDN_H) & (lane < 2 * GDN_H), dgv, 0.0)
        z = sm + gp[1:2, :]
        softplus = jnp.maximum(z, 0.0) + jnp.log(1.0 + jnp.exp(-jnp.abs(z)))
        ea = jnp.exp(gp[0:1, :])
        dz = dlg * (-ea) * _sig(z)
        dal = dlg * (-ea) * softplus
        bt = _sig(sm)
        dgb = dbt * bt * (1.0 - bt)
        ds_ref[...] = (dz + dgb).astype(BF16)
        gpp = jnp.concatenate([jnp.sum(dal, axis=0, keepdims=True), jnp.sum(dz, axis=0, keepdims=True),
                               jnp.zeros((6, LANES), F32)], axis=0)

        @pl.when(i == 0)
        def _():
            dw_ref[...] = dwp
            dgp_ref[...] = gpp

        @pl.when(i > 0)
        def _():
            dw_ref[...] += dwp
            dgp_ref[...] += gpp

    m3 = pl.BlockSpec((tr, W3), lambda i: (i, 0))
    m1 = pl.BlockSpec((tr, D_MODEL), lambda i: (i, 0))
    n1 = _halo_next(tr, D_MODEL, Lp)
    return pl.pallas_call(
        body, grid=(Lp // tr,),
        in_specs=[m3, _halo_prev(tr, W3), _halo_next(tr, W3, Lp), pl.BlockSpec((tr, LANES), lambda i: (i, 0)),
                  pl.BlockSpec((GDN_CONV, W3), lambda i: (0, 0)), pl.BlockSpec((8, LANES), lambda i: (0, 0)),
                  m1, n1, m1, n1, m1, n1, pl.BlockSpec((tr, LANES), lambda i: (i, 0))],
        out_specs=[m3, pl.BlockSpec((tr, LANES), lambda i: (i, 0)),
                   pl.BlockSpec((8, W3), lambda i: (0, 0)), pl.BlockSpec((8, LANES), lambda i: (0, 0))],
        out_shape=[jax.ShapeDtypeStruct((Lp, W3), BF16), jax.ShapeDtypeStruct((Lp, LANES), BF16),
                   jax.ShapeDtypeStruct((8, W3), F32), jax.ShapeDtypeStruct((8, LANES), F32)],
        name="gdn_pre_bwd")(proj_m, proj_m, proj_m, proj_s, conv_w, gparams, dq, dq, dk, dk, dv, dv, dgs)


def _gdn_gates(gs):
    ri = lax.broadcasted_iota(jnp.int32, (CHUNK, CHUNK), 0)
    ci = lax.broadcasted_iota(jnp.int32, (CHUNK, CHUNK), 1)
    tril = ri >= ci
    strict = ri > ci
    gall = _dx(tril.astype(F32), gs)
    lane8 = lax.broadcasted_iota(jnp.int32, (8, LANES), 1)
    sub8 = lax.broadcasted_iota(jnp.int32, (8, LANES), 0)
    grow = _dxnt((lane8 == sub8).astype(F32), gall)
    return gall, grow, tril, strict


def _gdn_decay(gall, grow, tril, h):
    g = gall[:, h:h + 1]
    return g, jnp.where(tril, jnp.exp(jnp.where(tril, g - grow[h:h + 1, :], 0.0)), 0.0)


def _gdn_chunk_specs(N, rev):
    cn = (lambda n: N - 1 - n) if rev else (lambda n: n)
    col = lambda j: pl.BlockSpec((CHUNK, D_MODEL), lambda n: (cn(n), j))
    gate = pl.BlockSpec((CHUNK, LANES), lambda n: (cn(n), 0))
    st = lambda a, b: pl.BlockSpec((GDN_H, None, a, b), lambda n: (0, cn(n), 0, 0))
    return col, gate, st


def _gdn_chunk_fwd(qkv, gsm):
    Lp = qkv.shape[0]
    N = Lp // CHUNK

    def body(q_ref, k_ref, v_ref, gs_ref, o_ref, sin_ref, t_ref, S):
        n = pl.program_id(0)

        @pl.when(n == 0)
        def _():
            S[...] = jnp.zeros_like(S)

        gs = gs_ref[...]
        gall, grow, tril, strict = _gdn_gates(gs)
        ri = lax.broadcasted_iota(jnp.int32, (CHUNK, CHUNK), 0)
        ci = lax.broadcasted_iota(jnp.int32, (CHUNK, CHUNK), 1)
        eye = (ri == ci).astype(F32)
        heads = range(GDN_H)
        sls = [slice(h * GDN_D, (h + 1) * GDN_D) for h in heads]
        q = [q_ref[:, sl] for sl in sls]
        k = [k_ref[:, sl] for sl in sls]
        v = [v_ref[:, sl] for sl in sls]
        s0 = [S[h] for h in heads]
        beta = [gs[:, GDN_H + h:GDN_H + h + 1] for h in heads]
        gg = [_gdn_decay(gall, grow, tril, h) for h in heads]
        g = [x[0] for x in gg]
        gam = [x[1] for x in gg]
        eg = [jnp.exp(g[h]) for h in heads]
        gl = [g[h][CHUNK - 1:CHUNK, :] for h in heads]
        kb = [k[h] * beta[h] for h in heads]
        pw = [-jnp.where(strict, _dnt(kb[h], k[h]) * gam[h], 0.0) for h in heads]
        p = [_dnt(q[h], k[h]) * gam[h] for h in heads]
        qs = [_d(q[h] * eg[h], s0[h]) for h in heads]
        t = [eye + pw[h] for h in heads]
        for _ in range(5):
            pw = [_d3g(pw[h], pw[h], _NN) for h in heads]
            t = [t[h] + _d3g(t[h], pw[h], _NN) for h in heads]
        u = [_d(t[h], v[h] * beta[h]) for h in heads]
        w = [_d(t[h], kb[h] * eg[h]) for h in heads]
        vnew = [u[h] - _d(w[h], s0[h]) for h in heads]
        for h in heads:
            o_ref[:, sls[h]] = qs[h] + _d(p[h], vnew[h])
            sin_ref[h] = s0[h]
            t_ref[h] = t[h]
            S[h] = s0[h] * jnp.exp(gl[h]) + _dtn(k[h] * jnp.exp(gl[h] - g[h]), vnew[h])

    col, gate, st = _gdn_chunk_specs(N, False)
    return pl.pallas_call(
        body, grid=(N,),
        in_specs=[col(0), col(1), col(2), gate],
        out_specs=[col(0), st(GDN_D, GDN_D), st(CHUNK, CHUNK)],
        out_shape=[jax.ShapeDtypeStruct((Lp, D_MODEL), F32), jax.ShapeDtypeStruct((GDN_H, N, GDN_D, GDN_D), F32),
                   jax.ShapeDtypeStruct((GDN_H, N, CHUNK, CHUNK), F32)],
        scratch_shapes=[pltpu.VMEM((GDN_H, GDN_D, GDN_D), F32)],
        name="gdn_chunk_fwd")(qkv, qkv, qkv, gsm)


def _gdn_chunk_bwd(qkv, gsm, do, s_in, t_in):
    Lp = qkv.shape[0]
    N = Lp // CHUNK

    def body(q_ref, k_ref, v_ref, gs_ref, do_ref, sin_ref, t_ref, dq_ref, dk_ref, dv_ref, dgs_ref, dS):
        n = pl.program_id(0)

        @pl.when(n == 0)
        def _():
            dS[...] = jnp.zeros_like(dS)

        gs = gs_ref[...]
        gall, grow, tril, strict = _gdn_gates(gs)
        lane = lax.broadcasted_iota(jnp.int32, (CHUNK, LANES), 1)
        rcol = lax.broadcasted_iota(jnp.int32, (CHUNK, 1), 0)
        ones = jnp.ones((CHUNK, LANES), F32)
        dg_all = jnp.zeros((CHUNK, LANES), F32)
        dbeta_all = jnp.zeros((CHUNK, LANES), F32)
        heads = range(GDN_H)
        sls = [slice(h * GDN_D, (h + 1) * GDN_D) for h in heads]
        H = lambda f: [f(h) for h in heads]
        q = H(lambda h: q_ref[:, sls[h]])
        k = H(lambda h: k_ref[:, sls[h]])
        v = H(lambda h: v_ref[:, sls[h]])
        dov = H(lambda h: do_ref[:, sls[h]])
        s0 = H(lambda h: sin_ref[h])
        t = H(lambda h: t_ref[h])
        dsv = H(lambda h: dS[h])
        beta = H(lambda h: gs[:, GDN_H + h:GDN_H + h + 1])
        gg = H(lambda h: _gdn_decay(gall, grow, tril, h))
        g = [x[0] for x in gg]
        gam = [x[1] for x in gg]
        eg = H(lambda h: jnp.exp(g[h]))
        egl = H(lambda h: jnp.exp(g[h][CHUNK - 1:CHUNK, :]))
        e = H(lambda h: jnp.exp(g[h][CHUNK - 1:CHUNK, :] - g[h]))
        kb = H(lambda h: k[h] * beta[h])
        kbg = H(lambda h: kb[h] * eg[h])
        vb = H(lambda h: v[h] * beta[h])
        qg = H(lambda h: q[h] * eg[h])
        kd = H(lambda h: k[h] * e[h])
        m = H(lambda h: jnp.where(strict, _dnt(kb[h], k[h]) * gam[h], 0.0))
        u = H(lambda h: _d(t[h], vb[h]))
        w = H(lambda h: _d(t[h], kbg[h]))
        p = H(lambda h: _dnt(q[h], k[h]) * gam[h])
        dqg = H(lambda h: _dnt(dov[h], s0[h]))
        kdds = H(lambda h: _d(kd[h], dsv[h]))
        qgdo = H(lambda h: _dtn(qg[h], dov[h]))
        vnew = H(lambda h: u[h] - _d(w[h], s0[h]))
        dvnew = H(lambda h: _dtn(p[h], dov[h]) + kdds[h])
        dp = H(lambda h: jnp.where(tril, _dnt(dov[h], vnew[h]), 0.0))
        dkd = H(lambda h: _dnt(vnew[h], dsv[h]))
        dw = H(lambda h: -_dnt(dvnew[h], s0[h]))
        for h in heads:
            dS[h] = qgdo[h] + egl[h] * dsv[h] - _dtn(w[h], dvnew[h])
        dvb = H(lambda h: _dtn(t[h], dvnew[h]))
        dkbg = H(lambda h: _dtn(t[h], dw[h]))
        dt = H(lambda h: _dnt(dvnew[h], vb[h]) + _dnt(dw[h], kbg[h]))
        x1 = H(lambda h: _d3g(t[h], dt[h], _TN))
        dm = H(lambda h: jnp.where(strict, -_d3g(x1[h], t[h], _NT), 0.0))
        dkk = H(lambda h: dm[h] * gam[h])
        dqk = H(lambda h: dp[h] * gam[h])
        dkb = H(lambda h: _d(dkk[h], k[h]) + eg[h] * dkbg[h])
        em = H(lambda h: dm[h] * m[h] + dp[h] * p[h])
        colsum = H(lambda h: _d3g(em[h], ones, _TN)[:, 0:1])
        for h in heads:
            dk_ref[:, sls[h]] = _dtn(dkk[h], kb[h]) + _dtn(dqk[h], q[h]) + dkd[h] * e[h] + beta[h] * dkb[h]
            dq_ref[:, sls[h]] = _d(dqk[h], k[h]) + dqg[h] * eg[h]
            dv_ref[:, sls[h]] = beta[h] * dvb[h]
        for h in heads:
            dbeta = _rowsum(k[h] * dkb[h]) + _rowsum(v[h] * dvb[h])
            z = _rowsum(kd[h] * dkd[h])
            dg = _rowsum(em[h]) - colsum[h] + _rowsum(qg[h] * dqg[h]) + _rowsum(kbg[h] * dkbg[h]) - z
            extra = _allsum(z) + egl[h] * _allsum(s0[h] * dsv[h])
            dg = dg + jnp.where(rcol == CHUNK - 1, extra, 0.0)
            dg_all = dg_all + jnp.where(lane == h, dg, 0.0)
            dbeta_all = dbeta_all + jnp.where(lane == GDN_H + h, dbeta, 0.0)
        ri = lax.broadcasted_iota(jnp.int32, (CHUNK, CHUNK), 0)
        ci = lax.broadcasted_iota(jnp.int32, (CHUNK, CHUNK), 1)
        dgs_ref[...] = _dx((ci >= ri).astype(F32), dg_all) + dbeta_all

    col, gate, st = _gdn_chunk_specs(N, True)
    return pl.pallas_call(
        body, grid=(N,),
        in_specs=[col(0), col(1), col(2), gate, col(0), st(GDN_D, GDN_D), st(CHUNK, CHUNK)],
        out_specs=[col(0), col(0), col(0), gate],
        out_shape=[jax.ShapeDtypeStruct((Lp, D_MODEL), F32)] * 3 + [jax.ShapeDtypeStruct((Lp, LANES), F32)],
        scratch_shapes=[pltpu.VMEM((GDN_H, GDN_D, GDN_D), F32)],
        name="gdn_chunk_bwd")(qkv, qkv, qkv, gsm, do, s_in, t_in)


def _rot(x, c, s):
    half = RET_D // 2
    x1 = x[:, :half]
    x2 = x[:, half:]
    return jnp.concatenate([x1 * c - x2 * s, x2 * c + x1 * s], axis=1)


def _rot_bwd(d, c, s):
    half = RET_D // 2
    d1 = d[:, :half]
    d2 = d[:, half:]
    return jnp.concatenate([d1 * c + d2 * s, d2 * c - d1 * s], axis=1)


def _ret_tables():
    hh = jnp.arange(RET_H, dtype=F32)
    lg = jnp.log(1.0 - 2.0 ** (-5.0 - hh))
    idx = jnp.arange(CHUNK, dtype=F32)
    tril = jnp.asarray(np.tril(np.ones((CHUNK, CHUNK), dtype=bool)))
    dmask = jnp.where(tril, jnp.exp((idx[:, None] - idx[None, :]) * lg[:, None, None]), 0.0)
    qdec = jnp.exp((idx[None, :] + 1.0) * lg[:, None])
    kdec = jnp.exp((CHUNK - 1.0 - idx[None, :]) * lg[:, None])
    gch = jnp.exp(CHUNK * lg)
    qdec = jnp.broadcast_to(qdec[:, :, None], (RET_H, CHUNK, RET_D))
    kdec = jnp.broadcast_to(kdec[:, :, None], (RET_H, CHUNK, RET_D))
    gch = jnp.broadcast_to(gch[:, None, None], (RET_H, 8, LANES))
    return dmask, qdec, kdec, gch


def _ret_specs(N, rev):
    cn = (lambda n: N - 1 - n) if rev else (lambda n: n)
    col = lambda j: pl.BlockSpec((CHUNK, D_MODEL), lambda n: (cn(n), j))
    tab = lambda a, b: pl.BlockSpec((RET_H, a, b), lambda n: (0, 0, 0))
    rope = pl.BlockSpec((CHUNK, LANES), lambda n: (cn(n), 0))
    st = pl.BlockSpec((RET_H, None, RET_D, RET_D), lambda n: (0, cn(n), 0, 0))
    return col, tab, rope, st


def _ret_chunk_fwd(proj_m, cos, sin, tables):
    Lp = proj_m.shape[0]
    N = Lp // CHUNK
    dmask, qdec, kdec, gch = tables

    def body(q_ref, k_ref, v_ref, c_ref, s_ref, dm_ref, qd_ref, kd_ref, g_ref, o_ref, sin_ref, S):
        n = pl.program_id(0)

        @pl.when(n == 0)
        def _():
            S[...] = jnp.zeros_like(S)

        c = c_ref[...]
        s = s_ref[...]
        heads = range(RET_H)
        sls = [slice(h * RET_D, (h + 1) * RET_D) for h in heads]
        H = lambda f: [f(h) for h in heads]
        qr = H(lambda h: _rot(q_ref[:, sls[h]], c, s))
        ks = H(lambda h: _rot(k_ref[:, sls[h]], c, s) * (RET_D ** -0.5))
        v = H(lambda h: v_ref[:, sls[h]])
        s0 = H(lambda h: S[h])
        a = H(lambda h: _dnt(qr[h], ks[h]) * dm_ref[h])
        qs = H(lambda h: _d(qr[h] * qd_ref[h], s0[h]))
        kv = H(lambda h: _dtn(ks[h] * kd_ref[h], v[h]))
        for h in heads:
            o_ref[:, sls[h]] = _d(a[h], v[h]) + qs[h]
            sin_ref[h] = s0[h]
            S[h] = s0[h] * g_ref[h, 0:1, 0:1] + kv[h]

    col, tab, rope, st = _ret_specs(N, False)
    return pl.pallas_call(
        body, grid=(N,),
        in_specs=[col(3), col(4), col(5), rope, rope,
                  tab(CHUNK, CHUNK), tab(CHUNK, RET_D), tab(CHUNK, RET_D), tab(8, LANES)],
        out_specs=[col(0), st],
        out_shape=[jax.ShapeDtypeStruct((Lp, D_MODEL), F32), jax.ShapeDtypeStruct((RET_H, N, RET_D, RET_D), F32)],
        scratch_shapes=[pltpu.VMEM((RET_H, RET_D, RET_D), F32)],
        name="ret_chunk_fwd")(proj_m, proj_m, proj_m, cos, sin, dmask, qdec, kdec, gch)


def _ret_chunk_bwd(proj_m, cos, sin, tables, do, s_in):
    Lp = proj_m.shape[0]
    N = Lp // CHUNK
    dmask, qdec, kdec, gch = tables

    def body(q_ref, k_ref, v_ref, c_ref, s_ref, dm_ref, qd_ref, kd_ref, g_ref, do_ref, sin_ref,
             dq_ref, dk_ref, dv_ref, dS):
        n = pl.program_id(0)

        @pl.when(n == 0)
        def _():
            dS[...] = jnp.zeros_like(dS)

        c = c_ref[...]
        s = s_ref[...]
        kscale = RET_D ** -0.5
        heads = range(RET_H)
        sls = [slice(h * RET_D, (h + 1) * RET_D) for h in heads]
        H = lambda f: [f(h) for h in heads]
        qr = H(lambda h: _rot(q_ref[:, sls[h]], c, s))
        ks = H(lambda h: _rot(k_ref[:, sls[h]], c, s) * kscale)
        v = H(lambda h: v_ref[:, sls[h]])
        dov = H(lambda h: do_ref[:, sls[h]])
        s0 = H(lambda h: sin_ref[h])
        dsv = H(lambda h: dS[h])
        ad = H(lambda h: _dnt(qr[h], ks[h]) * dm_ref[h])
        da = H(lambda h: _dnt(dov[h], v[h]) * dm_ref[h])
        kds = H(lambda h: _d(ks[h] * kd_ref[h], dsv[h]))
        dos = H(lambda h: _dnt(dov[h], s0[h]) * qd_ref[h])
        vds = H(lambda h: _dnt(v[h], dsv[h]) * kd_ref[h])
        qdo = H(lambda h: _dtn(qr[h] * qd_ref[h], dov[h]))
        for h in heads:
            dS[h] = dsv[h] * g_ref[h, 0:1, 0:1] + qdo[h]
        for h in heads:
            dv_ref[:, sls[h]] = (_dtn(ad[h], dov[h]) + kds[h]).astype(BF16)
            dq_ref[:, sls[h]] = _rot_bwd(_d(da[h], ks[h]) + dos[h], c, s).astype(BF16)
            dk_ref[:, sls[h]] = _rot_bwd((_dtn(da[h], qr[h]) + vds[h]) * kscale, c, s).astype(BF16)

    col, tab, rope, st = _ret_specs(N, True)
    return pl.pallas_call(
        body, grid=(N,),
        in_specs=[col(3), col(4), col(5), rope, rope,
                  tab(CHUNK, CHUNK), tab(CHUNK, RET_D), tab(CHUNK, RET_D), tab(8, LANES), col(0), st],
        out_specs=[col(0), col(0), col(0)],
        out_shape=[jax.ShapeDtypeStruct((Lp, D_MODEL), BF16)] * 3,
        scratch_shapes=[pltpu.VMEM((RET_H, RET_D, RET_D), F32)],
        name="ret_chunk_bwd")(proj_m, proj_m, proj_m, cos, sin, dmask, qdec, kdec, gch, do, s_in)


def _merge_specs(tr):
    col = lambda j: pl.BlockSpec((tr, D_MODEL), lambda i: (i, j))
    return col


def _merge_fwd(o_a, o_b, proj_m, gnorm):
    Lp = o_a.shape[0]
    tr = _tile(Lp, 192, 16)

    def body(oa_ref, ob_ref, gz_ref, rg_ref, ga_ref, gb_ref, gn_ref, y_ref):
        gn = gn_ref[...]
        oa = oa_ref[...]
        ob = ob_ref[...]
        gz = gz_ref[...]
        ya = []
        for j in range(GDN_H):
            seg = oa[:, j * GDN_D:(j + 1) * GDN_D]
            r = lax.rsqrt(jnp.mean(seg * seg, axis=-1, keepdims=True) + EPS)
            ya.append(seg * r * gn)
        ya = jnp.concatenate(ya, axis=1) * (gz * _sig(gz))
        yb = []
        for j in range(RET_H):
            seg = ob[:, j * RET_D:(j + 1) * RET_D]
            r = lax.rsqrt(jnp.mean(seg * seg, axis=-1, keepdims=True) + EPS)
            yb.append(seg * r)
        rg = rg_ref[...]
        yb = jnp.concatenate(yb, axis=1) * (rg * _sig(rg))
        y_ref[...] = (_sig(ga_ref[...]) * ya + _sig(gb_ref[...]) * yb).astype(BF16)

    col = _merge_specs(tr)
    return pl.pallas_call(
        body, grid=(Lp // tr,),
        in_specs=[col(0), col(0), col(6), col(7), col(8), col(9), pl.BlockSpec((1, GDN_D), lambda i: (0, 0))],
        out_specs=col(0), out_shape=jax.ShapeDtypeStruct((Lp, D_MODEL), BF16),
        name="merge_fwd")(o_a, o_b, proj_m, proj_m, proj_m, proj_m, gnorm)


def _merge_bwd(dy, o_a, o_b, proj_m, gnorm):
    Lp = o_a.shape[0]
    tr = _tile(Lp, 192, 16)

    def body(dy_ref, oa_ref, ob_ref, gz_ref, rg_ref, ga_ref, gb_ref, gn_ref, dc_ref, doa_ref, dob_ref, dgn_ref):
        i = pl.program_id(0)
        gn = gn_ref[...]
        dyv = dy_ref[...]
        oa = oa_ref[...]
        ob = ob_ref[...]
        gz = gz_ref[...]
        rg = rg_ref[...]
        sa = _sig(ga_ref[...])
        sb = _sig(gb_ref[...])
        dya = dyv * sa
        dyb = dyv * sb
        sgz = _sig(gz)
        szz = gz * sgz
        dgn = jnp.zeros((1, GDN_D), F32)
        ya = []
        dgz = []
        for j in range(GDN_H):
            sl = slice(j * GDN_D, (j + 1) * GDN_D)
            seg = oa[:, sl]
            r = lax.rsqrt(jnp.mean(seg * seg, axis=-1, keepdims=True) + EPS)
            xh = seg * r
            oan = xh * gn
            ya.append(oan * szz[:, sl])
            dgz.append(dya[:, sl] * oan * (sgz[:, sl] * (1.0 + gz[:, sl] * (1.0 - sgz[:, sl]))))
            doan = dya[:, sl] * szz[:, sl]
            dgn = dgn + jnp.sum(doan * xh, axis=0, keepdims=True)
            dxh = doan * gn
            doa_ref[:, sl] = r * (dxh - xh * jnp.mean(dxh * xh, axis=-1, keepdims=True))
        ya = jnp.concatenate(ya, axis=1)
        srg = _sig(rg)
        srr = rg * srg
        yb = []
        drg = []
        for j in range(RET_H):
            sl = slice(j * RET_D, (j + 1) * RET_D)
            seg = ob[:, sl]
            r = lax.rsqrt(jnp.mean(seg * seg, axis=-1, keepdims=True) + EPS)
            xh = seg * r
            yb.append(xh * srr[:, sl])
            drg.append(dyb[:, sl] * xh * (srg[:, sl] * (1.0 + rg[:, sl] * (1.0 - srg[:, sl]))))
            dxh = dyb[:, sl] * srr[:, sl]
            dob_ref[:, sl] = r * (dxh - xh * jnp.mean(dxh * xh, axis=-1, keepdims=True))
        yb = jnp.concatenate(yb, axis=1)
        dc_ref[:, 0:D_MODEL] = jnp.concatenate(dgz, axis=1).astype(BF16)
        dc_ref[:, D_MODEL:2 * D_MODEL] = jnp.concatenate(drg, axis=1).astype(BF16)
        dc_ref[:, 2 * D_MODEL:3 * D_MODEL] = (dyv * ya * sa * (1.0 - sa)).astype(BF16)
        dc_ref[:, 3 * D_MODEL:] = (dyv * yb * sb * (1.0 - sb)).astype(BF16)

        @pl.when(i == 0)
        def _():
            dgn_ref[...] = dgn

        @pl.when(i > 0)
        def _():
            dgn_ref[...] += dgn

    col = _merge_specs(tr)
    return pl.pallas_call(
        body, grid=(Lp // tr,),
        in_specs=[col(0), col(0), col(0), col(6), col(7), col(8), col(9), pl.BlockSpec((1, GDN_D), lambda i: (0, 0))],
        out_specs=[pl.BlockSpec((tr, 4 * D_MODEL), lambda i: (i, 0)), col(0), col(0),
                   pl.BlockSpec((1, GDN_D), lambda i: (0, 0))],
        out_shape=[jax.ShapeDtypeStruct((Lp, 4 * D_MODEL), BF16), jax.ShapeDtypeStruct((Lp, D_MODEL), F32),
                   jax.ShapeDtypeStruct((Lp, D_MODEL), F32), jax.ShapeDtypeStruct((1, GDN_D), F32)],
        name="merge_bwd")(dy, o_a, o_b, proj_m, proj_m, proj_m, proj_m, gnorm)


def _ffn_act(up, conv_w, conv_b):
    Lp = up.shape[0]
    tr = _tile(Lp, 192, 16)
    W2 = 2 * D_FF

    def body(main_ref, prev_ref, w_ref, b_ref, act_ref):
        i = pl.program_id(0)
        prev = jnp.where(i > 0, prev_ref[...], 0.0)
        ext = jnp.concatenate([prev, main_ref[...]], axis=0)
        u = _taps(_shifted(ext, range(8 - (FFN_CONV - 1), 9)), w_ref[...], tr, b_ref[...])
        a = u[:, :D_FF]
        act_ref[...] = (a * _sig(a) * u[:, D_FF:]).astype(BF16)

    return pl.pallas_call(
        body, grid=(Lp // tr,),
        in_specs=[pl.BlockSpec((tr, W2), lambda i: (i, 0)), _halo_prev(tr, W2),
                  pl.BlockSpec((FFN_CONV, W2), lambda i: (0, 0)), pl.BlockSpec((1, W2), lambda i: (0, 0))],
        out_specs=pl.BlockSpec((tr, D_FF), lambda i: (i, 0)),
        out_shape=jax.ShapeDtypeStruct((Lp, D_FF), BF16), name="ffn_act")(up, up, conv_w, conv_b)


def _ffn_act_bwd(up, dact, conv_w, conv_b):
    Lp = up.shape[0]
    tr = _tile(Lp, 96, 16)
    W2 = 2 * D_FF
    te = tr + 8

    def body(main_ref, prev_ref, next_ref, da_ref, dan_ref, w_ref, b_ref, dup_ref, acc_ref):
        i = pl.program_id(0)
        w = w_ref[...]
        prev = jnp.where(i > 0, prev_ref[...], 0.0)
        ext = jnp.concatenate([prev, main_ref[...], next_ref[...]], axis=0)
        wins = _shifted(ext, range(8 - (FFN_CONV - 1), 9))
        u = _taps(wins, w, te, b_ref[...])
        a = u[:, :D_FF]
        b = u[:, D_FF:]
        rowe = i * tr + lax.broadcasted_iota(jnp.int32, (te, 1), 0)
        dae = jnp.where(rowe < Lp, jnp.concatenate([da_ref[...], dan_ref[...]], axis=0), 0.0)
        sg = _sig(a)
        du = jnp.concatenate([dae * b * (sg * (1.0 + a * (1.0 - sg))), dae * (a * sg)], axis=1)
        dup_ref[...] = _taps(_shifted(du, range(FFN_CONV - 1, -1, -1)), w, tr).astype(BF16)
        dum = du[0:tr, :]
        rows = [jnp.sum(dum * wins[kk][0:tr, :], axis=0, keepdims=True) for kk in range(FFN_CONV)]
        rows.append(jnp.sum(dum, axis=0, keepdims=True))
        part = jnp.concatenate(rows + [jnp.zeros((8 - len(rows), W2), F32)], axis=0)

        @pl.when(i == 0)
        def _():
            acc_ref[...] = part

        @pl.when(i > 0)
        def _():
            acc_ref[...] += part

    return pl.pallas_call(
        body, grid=(Lp // tr,),
        in_specs=[pl.BlockSpec((tr, W2), lambda i: (i, 0)), _halo_prev(tr, W2), _halo_next(tr, W2, Lp),
                  pl.BlockSpec((tr, D_FF), lambda i: (i, 0)), _halo_next(tr, D_FF, Lp),
                  pl.BlockSpec((FFN_CONV, W2), lambda i: (0, 0)), pl.BlockSpec((1, W2), lambda i: (0, 0))],
        out_specs=[pl.BlockSpec((tr, W2), lambda i: (i, 0)), pl.BlockSpec((8, W2), lambda i: (0, 0))],
        out_shape=[jax.ShapeDtypeStruct((Lp, W2), BF16), jax.ShapeDtypeStruct((8, W2), F32)],
        name="ffn_act_bwd")(up, up, up, dact, dact, conv_w, conv_b)


def _local_step(hpad, tgt, pad, wt, first_weights=None, late_weights=None, on_ffn_out_grads=None,
                on_w_in_grads=None):
    Lp = hpad.shape[0]
    first = pad + N_META
    pos = jnp.arange(Lp, dtype=F32) - float(pad)
    half = RET_D // 2
    inv = 1.0 / (ROPE_BASE ** (jnp.arange(half, dtype=F32) / half))
    ang = pos[:, None] * inv[None, :]
    cos, sin = jnp.cos(ang), jnp.sin(ang)
    tables = _ret_tables()
    gparams = jnp.zeros((8, LANES), F32).at[0, :GDN_H].set(wt["a_log"]).at[1, :GDN_H].set(wt["dt_bias"])

    hn1 = _rms_fwd(hpad, wt["norm1"], "rms1_fwd")
    if first_weights is not None:
        wt = {**wt, **first_weights(hn1)}
    proj_m = _mm_nn(hn1, wt["w_main_t"], bt=True, name="proj_main")
    proj_s = _mm_nn(hn1, wt["w_small_t"], bt=True, name="proj_small")
    qkv, gsm = _gdn_pre(proj_m, proj_s, wt["gdn_conv_w"], gparams, pad)
    o_a, s_a, t_a = _gdn_chunk_fwd(qkv, gsm)
    o_b, s_b = _ret_chunk_fwd(proj_m, cos, sin, tables)
    y = _merge_fwd(o_a, o_b, proj_m, wt["gdn_norm"])
    if late_weights is not None:
        wt = {**wt, **late_weights(y)}
    h1 = _mm_nn(y, wt["w_out"], res=hpad, name="out_proj")
    hn2 = _rms_fwd(h1, wt["norm2"], "rms2_fwd")
    up = _mm_nn(hn2, wt["w_up_t"], bt=True, name="ffn_up")
    act = _ffn_act(up, wt["ffn_conv_w"], wt["ffn_conv_b"])
    h2 = _mm_nn(act, wt["w_down"], res=h1, name="ffn_down")
    lossvec, dh2, dh2b, d_norm_f = _final(h2, wt["norm_f"], tgt, first)

    d_w_down = _mm_tn(act, dh2b, name="dw_down")
    dact = _mm_nt(dh2b, wt["w_down"], name="d_act")
    dup, ffn_rows = _ffn_act_bwd(up, dact, wt["ffn_conv_w"], wt["ffn_conv_b"])
    d_w_up_t = _mm_tn(dup, hn2, name="dw_up")
    dhn2 = _mm_nn(dup, wt["w_up_t"], name="d_hn2")
    dh1, dh1b, d_norm2 = _rms_bwd(h1, wt["norm2"], dhn2, dh2, pad, "rms2_bwd")

    d_w_out = _mm_tn(y, dh1b, name="dw_out")
    dy = _mm_nt(dh1b, wt["w_out"], name="d_y")
    gnorm = wt["gdn_norm"]
    if on_ffn_out_grads is not None:
        gnorm = gnorm + on_ffn_out_grads(d_w_down, d_w_up_t, d_w_out)[0:1, :]
    d_c, do_a, do_b, d_gnorm = _merge_bwd(dy, o_a, o_b, proj_m, gnorm)
    drq, drk, drv = _ret_chunk_bwd(proj_m, cos, sin, tables, do_b, s_b)
    dq, dk, dv, dgs = _gdn_chunk_bwd(qkv, gsm, do_a, s_a, t_a)
    d_a, d_s, conv_rows, gp_rows = _gdn_pre_bwd(proj_m, proj_s, wt["gdn_conv_w"], gparams, dq, dk, dv, dgs, pad)

    wmt = wt["w_main_t"]
    segs = [(d_a, 0, 3 * D_MODEL), (drq, 3 * D_MODEL, D_MODEL), (drk, 4 * D_MODEL, D_MODEL),
            (drv, 5 * D_MODEL, D_MODEL), (d_c, 6 * D_MODEL, 4 * D_MODEL)]
    pa, prq, prk, prv, pc = [_mm_tn(d, hn1, name="dw_in_%d" % i) for i, (d, _, _) in enumerate(segs)]
    ps = _mm_tn(d_s, hn1, name="dw_in_small")
    d_w_in_t = jnp.concatenate([pa, pc[:D_MODEL], ps[:2 * GDN_H], prq, prk, prv, pc[D_MODEL:]], axis=0)
    w_small_t = wt["w_small_t"]
    if on_w_in_grads is not None:
        w_small_t = w_small_t + on_w_in_grads(d_w_in_t)[0:1, 0:1].astype(w_small_t.dtype)
    dhn1 = _mm_nn(d_s, w_small_t, name="d_hn1_small")
    for i, (d, off, width) in enumerate(segs):
        dhn1 = _mm_nn(d, wmt[off:off + width], res=dhn1, name="d_hn1_%d" % i)
    dh0, _, d_norm1 = _rms_bwd(hpad, wt["norm1"], dhn1, dh1, pad, "rms1_bwd")

    grads = {
        "norm1": d_norm1, "w_in_t": d_w_in_t, "gdn_conv_w": conv_rows[:GDN_CONV],
        "a_log": gp_rows[0, :GDN_H], "dt_bias": gp_rows[1, :GDN_H], "gdn_norm": d_gnorm, "w_out": d_w_out,
        "norm2": d_norm2, "w_up_t": d_w_up_t, "ffn_conv_w": ffn_rows[:FFN_CONV],
        "ffn_conv_b": ffn_rows[FFN_CONV:FFN_CONV + 1], "w_down": d_w_down, "norm_f": d_norm_f,
    }
    return lossvec, dh0, grads


def _peer(k):
    ix, iy, ic = lax.axis_index("x"), lax.axis_index("y"), lax.axis_index("c")
    px = 1 - ix if (k >> 2) & 1 else ix
    py = 1 - iy if (k >> 1) & 1 else iy
    pc = 1 - ic if k & 1 else ic
    return (px, py, pc), 4 * px + 2 * py + pc


def _comm_call(body, n, out_shapes, name, args):
    hbm = pl.BlockSpec(memory_space=pl.ANY)
    return pl.pallas_call(
        body, out_shape=out_shapes, in_specs=[hbm] * n, out_specs=[hbm] * n,
        scratch_shapes=[pltpu.SemaphoreType.DMA((n, N_DEV - 1)), pltpu.SemaphoreType.DMA((n, N_DEV - 1)),
                        pltpu.SemaphoreType.DMA((n,))],
        name=name)(*args)


def _all_gather(xs, name):
    n = len(xs)

    def body(*refs):
        x_refs, out_refs = refs[:n], refs[n:2 * n]
        send_sems, recv_sems, local_sems = refs[2 * n:]
        _, me = _peer(0)
        pending = []
        for i in range(n):
            local = pltpu.make_async_copy(x_refs[i], out_refs[i].at[me], local_sems.at[i])
            local.start()
            pending.append(local)
        sends = []
        for i in range(n):
            for k in range(1, N_DEV):
                dev, _ = _peer(k)
                cp = pltpu.make_async_remote_copy(
                    src_ref=x_refs[i], dst_ref=out_refs[i].at[me], send_sem=send_sems.at[i, k - 1],
                    recv_sem=recv_sems.at[i, k - 1], device_id=dev, device_id_type=MESH_T)
                cp.start()
                sends.append(cp)
        for i in range(n):
            for k in range(1, N_DEV):
                dev, idx = _peer(k)
                pltpu.make_async_remote_copy(
                    src_ref=x_refs[i], dst_ref=out_refs[i].at[idx], send_sem=send_sems.at[i, k - 1],
                    recv_sem=recv_sems.at[i, k - 1], device_id=dev, device_id_type=MESH_T).wait_recv()
        for cp in sends:
            cp.wait_send()
        for local in pending:
            local.wait()

    out_shapes = [jax.ShapeDtypeStruct((N_DEV,) + a.shape, a.dtype) for a in xs]
    return _comm_call(body, n, out_shapes, name, xs)


def _all_to_all(gs, name):
    n = len(gs)

    def body(*refs):
        g_refs, out_refs = refs[:n], refs[n:2 * n]
        send_sems, recv_sems, local_sems = refs[2 * n:]
        _, me = _peer(0)
        pending = []
        for i in range(n):
            local = pltpu.make_async_copy(g_refs[i].at[me], out_refs[i].at[0], local_sems.at[i])
            local.start()
            pending.append(local)
        sends = []
        for i in range(n):
            for k in range(1, N_DEV):
                dev, idx = _peer(k)
                cp = pltpu.make_async_remote_copy(
                    src_ref=g_refs[i].at[idx], dst_ref=out_refs[i].at[k], send_sem=send_sems.at[i, k - 1],
                    recv_sem=recv_sems.at[i, k - 1], device_id=dev, device_id_type=MESH_T)
                cp.start()
                sends.append(cp)
        for cp in sends:
            cp.wait_recv()
        for cp in sends:
            cp.wait_send()
        for local in pending:
            local.wait()

    out_shapes = [jax.ShapeDtypeStruct(g.shape, g.dtype) for g in gs]
    return _comm_call(body, n, out_shapes, name, gs)


def _split_copies(kind, src_refs, land_refs, send_sems, recv_sems, local_sems, with_recv):
    n = len(src_refs)
    _, me = _peer(0)
    locals_, remotes = [], []
    for i in range(n):
        if kind == "gather":
            locals_.append(pltpu.make_async_copy(src_refs[i], land_refs[i].at[me], local_sems.at[i]))
        else:
            locals_.append(pltpu.make_async_copy(src_refs[i].at[me], land_refs[i].at[0], local_sems.at[i]))
        for k in range(1, N_DEV):
            dev, idx = _peer(k)
            if kind == "gather":
                src, dst, mine = src_refs[i], land_refs[i].at[me], land_refs[i].at[idx]
            else:
                src, dst, mine = src_refs[i].at[idx], land_refs[i].at[k], land_refs[i].at[k]
            j = i * (N_DEV - 1) + k - 1
            send = pltpu.make_async_remote_copy(
                src_ref=src, dst_ref=dst, send_sem=send_sems.at[j], recv_sem=recv_sems.at[j],
                device_id=dev, device_id_type=MESH_T)
            recv = pltpu.make_async_remote_copy(
                src_ref=src, dst_ref=mine, send_sem=send_sems.at[j], recv_sem=recv_sems.at[j],
                device_id=dev, device_id_type=MESH_T) if with_recv else None
            remotes.append((send, recv))
    return locals_, remotes


_HBM = pl.BlockSpec(memory_space=pltpu.HBM)
_SEM = pl.BlockSpec(memory_space=pltpu.SEMAPHORE)
_ANY = pl.BlockSpec(memory_space=pl.ANY)


def _split_start(srcs, kind, name, after):
    n = len(srcs)
    lands = [lax.empty(((N_DEV,) + a.shape) if kind == "gather" else a.shape, a.dtype) for a in srcs]

    def body(*refs):
        src_refs, land_refs = refs[:n], refs[n:2 * n]
        send_sems, recv_sems, local_sems = refs[2 * n + 1:2 * n + 4]
        token = refs[-1]
        locals_, remotes = _split_copies(kind, src_refs, land_refs, send_sems, recv_sems, local_sems, False)
        for cp in locals_:
            cp.start()
        for send, _ in remotes:
            send.start()
        token[...] = jnp.zeros_like(token)

    sems = (pltpu.SemaphoreType.DMA((n * (N_DEV - 1),)), pltpu.SemaphoreType.DMA((n * (N_DEV - 1),)),
            pltpu.SemaphoreType.DMA((n,)))
    thru = tuple(pltpu.HBM(a.shape, a.dtype) for a in list(srcs) + lands)
    outs = pl.pallas_call(
        body, name=name,
        out_shape=sems + thru + (jax.ShapeDtypeStruct((8, LANES), F32),),
        in_specs=[_HBM] * (2 * n) + [_ANY],
        out_specs=[_SEM] * 3 + [_HBM] * (2 * n) + [pl.BlockSpec(memory_space=pltpu.VMEM)],
        input_output_aliases={i: 3 + i for i in range(2 * n)},
        compiler_params=pltpu.CompilerParams(has_side_effects=pltpu.SideEffectType.DATAFLOW_SIDE_EFFECTING),
    )(*[pltpu.with_memory_space_constraint(a, pltpu.HBM) for a in list(srcs) + lands], after)
    return (kind, n, outs[:3], outs[3:3 + 2 * n]), outs[-1]


def _split_wait(handle, name, after):
    kind, n, sems, thru = handle

    def body(*refs):
        src_refs, land_refs = refs[:n], refs[n:2 * n]
        send_sems, recv_sems, local_sems = refs[2 * n:2 * n + 3]
        locals_, remotes = _split_copies(kind, src_refs, land_refs, send_sems, recv_sems, local_sems, True)
        for send, recv in remotes:
            send.wait_send()
            recv.wait_recv()
        for cp in locals_:
            cp.wait()

    outs = pl.pallas_call(
        body, name=name, out_shape=tuple(pltpu.HBM(a.shape, a.dtype) for a in thru),
        in_specs=[_HBM] * (2 * n) + [_SEM] * 3 + [_ANY], out_specs=[_HBM] * (2 * n),
        input_output_aliases={i: i for i in range(2 * n)},
        compiler_params=pltpu.CompilerParams(has_side_effects=pltpu.SideEffectType.DATAFLOW_SIDE_EFFECTING),
    )(*thru, *sems, after)
    return list(outs[n:])


def _adamw(gslabs, w, m, v, name):
    R, Cw = w.shape
    if R % 8 == 0:
        tr, tc = _tile(R, 64 if Cw > 1024 else 128, 8), Cw
    else:
        tr, tc = R, LANES
    c1 = 1.0 - ADAM_B1 ** ADAM_STEP
    c2 = 1.0 - ADAM_B2 ** ADAM_STEP

    def body(g_ref, w_ref, m_ref, v_ref, go_ref, d_ref, mo_ref, vo_ref):
        g = g_ref[0].astype(F32)
        for k in range(1, N_DEV):
            g = g + g_ref[k].astype(F32)
        mn = ADAM_B1 * m_ref[...] + (1.0 - ADAM_B1) * g
        vn = ADAM_B2 * v_ref[...] + (1.0 - ADAM_B2) * (g * g)
        m_hat = mn / c1
        v_hat = vn / c2
        go_ref[...] = g
        d_ref[...] = -ADAM_LR * (m_hat / (jnp.sqrt(v_hat) + ADAM_EPS) + ADAM_WD * w_ref[...])
        mo_ref[...] = mn
        vo_ref[...] = vn

    blk = pl.BlockSpec((tr, tc), lambda i, j: (i, j))
    return pl.pallas_call(
        body, grid=(R // tr, Cw // tc),
        in_specs=[pl.BlockSpec((N_DEV, tr, tc), lambda i, j: (0, i, j)), blk, blk, blk],
        out_specs=[blk] * 4, out_shape=[jax.ShapeDtypeStruct((R, Cw), F32)] * 4, name=name)(gslabs, w, m, v)


def _pack(arrs, row_mult, dtype=F32):
    parts = []
    total = 0
    for a in arrs:
        f = a.reshape(-1).astype(dtype)
        n = -(-f.shape[0] // 1024) * 1024
        parts.append(jnp.pad(f, (0, n - f.shape[0])))
        total += n
    rows = total // LANES
    rows_p = -(-rows // row_mult) * row_mult
    flat = jnp.concatenate(parts)
    flat = jnp.pad(flat, (0, rows_p * LANES - total))
    return flat.reshape(rows_p, LANES)


def _unpack(packed, shapes):
    lead = packed.shape[:-2]
    flat = packed.reshape(lead + (-1,))
    out = []
    off = 0
    for s in shapes:
        n = int(np.prod(s))
        out.append(flat[..., off:off + n].reshape(lead + tuple(s)))
        off += -(-n // 1024) * 1024
    return out


def _gather_cols(stacked):
    d, r, c = stacked.shape
    return stacked.transpose(1, 0, 2).reshape(r, d * c)


def _scatter_cols(full):
    r, n = full.shape
    return full.reshape(r, N_DEV, n // N_DEV).transpose(1, 0, 2)


def kernel(x, meta, norm1, w_in, gdn_conv_w, gdn_a_log, gdn_dt_bias, gdn_norm, w_out, norm2, w_ffn_up, ffn_conv_w, ffn_conv_b, w_ffn_down, norm_f, loss_target, m_meta, m_norm1, m_w_in, m_gdn_conv_w, m_gdn_a_log, m_gdn_dt_bias, m_gdn_norm, m_w_out, m_norm2, m_w_ffn_up, m_ffn_conv_w, m_ffn_conv_b, m_w_ffn_down, m_norm_f, v_meta, v_norm1, v_w_in, v_gdn_conv_w, v_gdn_a_log, v_gdn_dt_bias, v_gdn_norm, v_w_out, v_norm2, v_w_ffn_up, v_ffn_conv_w, v_ffn_conv_b, v_w_ffn_down, v_norm_f):
    S = x.shape[1]
    L = N_META + S
    pad = (-L) % CHUNK
    Lp = L + pad

    tr_ = lambda a: jnp.swapaxes(a[0], 0, 1)
    big = [tr_(w_in), w_out[0], tr_(w_ffn_up), w_ffn_down[0]]
    small = [meta, gdn_conv_w, ffn_conv_w]
    small_all, = _all_gather([_pack(small, 8)], "gather_small_weights")
    first, first_token = _split_start([big[0].astype(BF16)], "gather", "gather_w_in_start", small_all)
    late, late_token = _split_start([a.astype(BF16) for a in big[1:]], "gather", "gather_late_start", first_token)

    def first_weights(after):
        w_in_s, = _split_wait(first, "gather_w_in_wait", after)
        w_in_t = w_in_s.reshape(_O_END, D_MODEL)
        w_main_t = jnp.concatenate([w_in_t[_O_GQ:_O_GZ], w_in_t[_O_RQ:_O_RG], w_in_t[_O_GZ:_O_GA],
                                    w_in_t[_O_RG:_O_END]], axis=0)
        return {"w_main_t": w_main_t, "w_small_t": jnp.pad(w_in_t[_O_GA:_O_RQ], ((0, LANES - 2 * GDN_H), (0, 0)))}

    def late_weights(after):
        w_out_s, w_up_s, w_down_s = _split_wait(late, "gather_late_wait", after)
        return {"w_out": w_out_s.reshape(D_MODEL, D_MODEL), "w_up_t": w_up_s.reshape(2 * D_FF, D_MODEL),
                "w_down": w_down_s.reshape(D_FF, D_MODEL)}

    meta_s, gconv_s, fconv_s = _unpack(small_all, [a.shape for a in small])
    wt = {
        "norm1": norm1 + jnp.tile(late_token[0:1, :], (1, D_MODEL // LANES)),
        "gdn_conv_w": _gather_cols(gconv_s[:, 0]), "a_log": gdn_a_log[0], "dt_bias": gdn_dt_bias[0],
        "gdn_norm": gdn_norm, "norm2": norm2, "ffn_conv_w": _gather_cols(fconv_s[:, 0]), "ffn_conv_b": ffn_conv_b,
        "norm_f": norm_f.reshape(1, D_MODEL),
    }
    meta_f = _gather_cols(meta_s)

    pending = {}

    def on_ffn_out_grads(d_w_down, d_w_up_t, d_w_out):
        srcs = [d_w_out.reshape(N_DEV, D_MODEL // N_DEV, D_MODEL), d_w_up_t.reshape(N_DEV, 2 * D_FF // N_DEV, D_MODEL),
                d_w_down.reshape(N_DEV, D_FF // N_DEV, D_MODEL)]
        pending["ffn_out"], token = _split_start(srcs, "a2a", "exchange_ffn_out_start", d_w_out)
        return token

    def on_w_in_grads(d_w_in_t):
        slabs = d_w_in_t.astype(BF16).reshape(N_DEV, _O_END // N_DEV, D_MODEL)
        pending["w_in"], token = _split_start([slabs], "a2a", "exchange_w_in_start", d_w_in_t)
        return token

    hpad = jnp.concatenate([jnp.zeros((pad, D_MODEL), F32), meta_f, x[0]], axis=0)
    tgt = jnp.concatenate([jnp.zeros((pad + N_META, D_MODEL), F32), loss_target[0]], axis=0)
    lossvec, dh0, gr = _local_step(hpad, tgt, pad, wt, first_weights, late_weights, on_ffn_out_grads, on_w_in_grads)

    loss = lax.psum(jnp.sum(lossvec), ("x", "y", "c"))
    grad_x = dh0[pad + N_META:][None]

    big_m = [tr_(m_w_in), m_w_out[0], tr_(m_w_ffn_up), m_w_ffn_down[0]]
    big_v = [tr_(v_w_in), v_w_out[0], tr_(v_w_ffn_up), v_w_ffn_down[0]]
    slabs_ffn_out = _split_wait(pending["ffn_out"], "exchange_ffn_out_wait", dh0)
    big_out = [None] + [_adamw(slabs_ffn_out[i - 1], big[i], big_m[i], big_v[i], "adamw_big_%d" % i)
                        for i in range(1, len(big))]
    g_sm = [_scatter_cols(dh0[pad:pad + N_META]), _scatter_cols(gr["gdn_conv_w"]), _scatter_cols(gr["ffn_conv_w"])]
    g_small = jnp.stack([_pack([g[d] for g in g_sm], 8) for d in range(N_DEV)])
    slabs_small, = _all_to_all([g_small], "exchange_small_gradients")
    small_out = _adamw(slabs_small, _pack(small, 8), _pack([m_meta, m_gdn_conv_w, m_ffn_conv_w], 8),
                       _pack([v_meta, v_gdn_conv_w, v_ffn_conv_w], 8), "adamw_small_sharded")
    small_un = [_unpack(o, [a.shape for a in small]) for o in small_out]
    rep_w = [norm1, gdn_a_log, gdn_dt_bias, gdn_norm, norm2, ffn_conv_b, norm_f]
    rep_m = [m_norm1, m_gdn_a_log, m_gdn_dt_bias, m_gdn_norm, m_norm2, m_ffn_conv_b, m_norm_f]
    rep_v = [v_norm1, v_gdn_a_log, v_gdn_dt_bias, v_gdn_norm, v_norm2, v_ffn_conv_b, v_norm_f]
    rep_g = [gr["norm1"], gr["a_log"], gr["dt_bias"], gr["gdn_norm"], gr["norm2"], gr["ffn_conv_b"], gr["norm_f"]]
    rep_slabs, = _all_gather([_pack(rep_g, 8)], "gather_small_gradients")
    rep_out = _adamw(rep_slabs, _pack(rep_w, 8), _pack(rep_m, 8), _pack(rep_v, 8), "adamw_replicated")
    rep_shapes = [a.shape for a in rep_w]
    rp_g, rp_d, rp_nm, rp_nv = [_unpack(o, rep_shapes) for o in rep_out]

    slabs_w_in, = _split_wait(pending["w_in"], "exchange_w_in_wait", rep_out[0])
    big_out[0] = _adamw(slabs_w_in, big[0], big_m[0], big_v[0], "adamw_big_0")
    back = lambda a: jnp.swapaxes(a, 0, 1)[None]
    sh_g, sh_d, sh_nm, sh_nv = [
        [small_un[j][0], back(big_out[0][j]), small_un[j][1], big_out[1][j][None], back(big_out[2][j]),
         small_un[j][2], big_out[3][j][None]] for j in range(4)]

    def order(sh, rp):
        return [sh[0], rp[0], sh[1], sh[2], rp[1], rp[2], rp[3], sh[3], rp[4], sh[4], sh[5], rp[5], sh[6], rp[6]]

    return (loss, grad_x, *order(sh_g, rp_g), *order(sh_d, rp_d), *order(sh_nm, rp_nm), *order(sh_nv, rp_nv))
```

```python
import functools
import math

import numpy as np
import jax
import jax.numpy as jnp
from jax import lax
from jax.experimental import pallas as pl
from jax.experimental.pallas import tpu as pltpu

F32 = jnp.float32
BF16 = jnp.bfloat16
HI = lax.Precision.HIGHEST

D_MODEL = 1024
N_META = 16
CHUNK = 64
GDN_H = 8
GDN_D = 128
RET_H = 4
RET_D = 256
D_FF = 2816
GDN_CONV = 4
FFN_CONV = 3
ROPE_BASE = 10000.0
EPS = 1e-6
N_DEV = 8
LANES = 128
MAIN_W = 10 * 1024
_O_GQ, _O_GZ, _O_GA, _O_RQ, _O_RG, _O_GATE, _O_END = 0, 3072, 4096, 4112, 7184, 8208, 10256

ADAM_LR = 0.001
ADAM_B1 = 0.9
ADAM_B2 = 0.999
ADAM_EPS = 1e-08
ADAM_WD = 0.01
ADAM_STEP = 10

MESH_T = pl.DeviceIdType.MESH


def _tile(n, target, mult):
    best = None
    for d in range(mult, min(n, target) + 1, mult):
        if n % d == 0:
            best = d
    assert best is not None, (n, target, mult)
    return best


def _sig(x):
    return 1.0 / (1.0 + jnp.exp(-x))


def _d(a, b):
    return jnp.dot(a.astype(BF16), b.astype(BF16), preferred_element_type=F32)


def _dnt(a, b):
    return lax.dot_general(a.astype(BF16), b.astype(BF16), (((1,), (1,)), ((), ())), preferred_element_type=F32)


def _dtn(a, b):
    return lax.dot_general(a.astype(BF16), b.astype(BF16), (((0,), (0,)), ((), ())), preferred_element_type=F32)


def _dx(a, b):
    return jnp.dot(a, b, preferred_element_type=F32, precision=HI)


def _dxnt(a, b):
    return lax.dot_general(a, b, (((1,), (1,)), ((), ())), preferred_element_type=F32, precision=HI)


def _dxtn(a, b):
    return lax.dot_general(a, b, (((0,), (0,)), ((), ())), preferred_element_type=F32, precision=HI)


def _split(a):
    hi = a.astype(BF16)
    return hi, (a - hi.astype(F32)).astype(BF16)


def _d3g(a, b, dims):
    ah, al = _split(a)
    bh, bl = _split(b)
    f = functools.partial(lax.dot_general, dimension_numbers=dims, preferred_element_type=F32)
    return f(ah, bh) + (f(ah, bl) + f(al, bh))


_NN = (((1,), (0,)), ((), ()))
_NT = (((1,), (1,)), ((), ()))
_TN = (((0,), (0,)), ((), ()))


def _rowsum(x):
    return jnp.sum(x, axis=1, keepdims=True)


def _allsum(x):
    return jnp.sum(jnp.sum(x, axis=1, keepdims=True), axis=0, keepdims=True)


def _mm_nn(a, b, res=None, out_dtype=F32, bt=False, name="mm_nn"):
    M, K = a.shape
    N = b.shape[0] if bt else b.shape[1]
    tm = _tile(M, 704, 16)
    tn = _tile(N, 2816, 128)

    def body(*refs):
        if res is None:
            a_ref, b_ref, o_ref = refs
        else:
            a_ref, b_ref, r_ref, o_ref = refs
        acc = lax.dot_general(a_ref[...], b_ref[...], _NT if bt else _NN, preferred_element_type=F32)
        if res is not None:
            acc = acc + r_ref[...]
        o_ref[...] = acc.astype(out_dtype)

    b_spec = pl.BlockSpec((tn, K), lambda j, i: (j, 0)) if bt else pl.BlockSpec((K, tn), lambda j, i: (0, j))
    in_specs = [pl.BlockSpec((tm, K), lambda j, i: (i, 0)), b_spec]
    args = [a, b]
    if res is not None:
        in_specs.append(pl.BlockSpec((tm, tn), lambda j, i: (i, j)))
        args.append(res)
    return pl.pallas_call(
        body, grid=(N // tn, M // tm), in_specs=in_specs,
        out_specs=pl.BlockSpec((tm, tn), lambda j, i: (i, j)),
        out_shape=jax.ShapeDtypeStruct((M, N), out_dtype), name=name)(*args)


def _mm_nt(a, b, res=None, name="mm_nt"):
    M, Nc = a.shape
    K = b.shape[0]
    tm = _tile(M, 704, 16)
    tc = _tile(Nc, 5632, 128)

    def body(*refs):
        if res is None:
            a_ref, b_ref, o_ref = refs
        else:
            a_ref, b_ref, r_ref, o_ref = refs
        c = pl.program_id(1)
        p = lax.dot_general(a_ref[...], b_ref[...], (((1,), (1,)), ((), ())), preferred_element_type=F32)

        @pl.when(c == 0)
        def _():
            if res is None:
                o_ref[...] = p
            else:
                o_ref[...] = p + r_ref[...]

        @pl.when(c > 0)
        def _():
            o_ref[...] += p

    in_specs = [pl.BlockSpec((tm, tc), lambda i, c: (i, c)), pl.BlockSpec((K, tc), lambda i, c: (0, c))]
    args = [a, b]
    if res is not None:
        in_specs.append(pl.BlockSpec((tm, K), lambda i, c: (i, 0)))
        args.append(res)
    return pl.pallas_call(
        body, grid=(M // tm, Nc // tc), in_specs=in_specs,
        out_specs=pl.BlockSpec((tm, K), lambda i, c: (i, 0)),
        out_shape=jax.ShapeDtypeStruct((M, K), F32), name=name)(*args)


def _mm_tn(a, b, name="mm_tn"):
    M, K = a.shape
    N = b.shape[1]
    tm = _tile(M, 2752, 16)
    tk = _tile(K, 1408, 128)
    tn = _tile(N, 1408, 128)

    def body(a_ref, b_ref, o_ref):
        m = pl.program_id(2)
        p = lax.dot_general(a_ref[...], b_ref[...], (((0,), (0,)), ((), ())), preferred_element_type=F32)

        @pl.when(m == 0)
        def _():
            o_ref[...] = p

        @pl.when(m > 0)
        def _():
            o_ref[...] += p

    return pl.pallas_call(
        body, grid=(K // tk, N // tn, M // tm),
        in_specs=[pl.BlockSpec((tm, tk), lambda kk, j, m: (m, kk)), pl.BlockSpec((tm, tn), lambda kk, j, m: (m, j))],
        out_specs=pl.BlockSpec((tk, tn), lambda kk, j, m: (kk, j)),
        out_shape=jax.ShapeDtypeStruct((K, N), F32), name=name)(a, b)


def _rms_fwd(x, g, name):
    Lp = x.shape[0]
    tr = _tile(Lp, 256, 16)

    def body(x_ref, g_ref, o_ref):
        xv = x_ref[...]
        r = lax.rsqrt(jnp.mean(xv * xv, axis=-1, keepdims=True) + EPS)
        o_ref[...] = (xv * r * g_ref[...]).astype(BF16)

    return pl.pallas_call(
        body, grid=(Lp // tr,),
        in_specs=[pl.BlockSpec((tr, D_MODEL), lambda i: (i, 0)), pl.BlockSpec((1, D_MODEL), lambda i: (0, 0))],
        out_specs=pl.BlockSpec((tr, D_MODEL), lambda i: (i, 0)),
        out_shape=jax.ShapeDtypeStruct((Lp, D_MODEL), BF16), name=name)(x, g)


def _rms_bwd(x, g, dy, dres, pad, name):
    Lp = x.shape[0]
    tr = _tile(Lp, 256, 16)

    def body(x_ref, g_ref, dy_ref, dr_ref, dx_ref, dxb_ref, dg_ref):
        i = pl.program_id(0)
        xv = x_ref[...]
        r = lax.rsqrt(jnp.mean(xv * xv, axis=-1, keepdims=True) + EPS)
        xh = xv * r
        dyv = dy_ref[...]
        dxh = dyv * g_ref[...]
        dx = r * (dxh - xh * jnp.mean(dxh * xh, axis=-1, keepdims=True)) + dr_ref[...]
        row = i * tr + lax.broadcasted_iota(jnp.int32, (tr, 1), 0)
        dx = jnp.where(row >= pad, dx, 0.0)
        dx_ref[...] = dx
        dxb_ref[...] = dx.astype(BF16)
        part = jnp.sum(dyv * xh, axis=0, keepdims=True)

        @pl.when(i == 0)
        def _():
            dg_ref[...] = part

        @pl.when(i > 0)
        def _():
            dg_ref[...] += part

    blk = pl.BlockSpec((tr, D_MODEL), lambda i: (i, 0))
    vec = pl.BlockSpec((1, D_MODEL), lambda i: (0, 0))
    return pl.pallas_call(
        body, grid=(Lp // tr,), in_specs=[blk, vec, blk, blk], out_specs=[blk, blk, vec],
        out_shape=[jax.ShapeDtypeStruct((Lp, D_MODEL), F32), jax.ShapeDtypeStruct((Lp, D_MODEL), BF16),
                   jax.ShapeDtypeStruct((1, D_MODEL), F32)], name=name)(x, g, dy, dres)


def _final(h2, g, tgt, first_row):
    Lp = h2.shape[0]
    tr = _tile(Lp, 256, 16)

    def body(x_ref, g_ref, t_ref, loss_ref, dx_ref, dxb_ref, dg_ref):
        i = pl.program_id(0)
        xv = x_ref[...]
        gv = g_ref[...]
        r = lax.rsqrt(jnp.mean(xv * xv, axis=-1, keepdims=True) + EPS)
        xh = xv * r
        row = i * tr + lax.broadcasted_iota(jnp.int32, (tr, 1), 0)
        err = jnp.where(row >= first_row, xh * gv - t_ref[...], 0.0)
        lpart = jnp.sum(err * err, axis=0, keepdims=True) * (0.5 / D_MODEL)
        dyv = err * (1.0 / D_MODEL)
        dxh = dyv * gv
        dx = r * (dxh - xh * jnp.mean(dxh * xh, axis=-1, keepdims=True))
        dx_ref[...] = dx
        dxb_ref[...] = dx.astype(BF16)
        part = jnp.sum(dyv * xh, axis=0, keepdims=True)

        @pl.when(i == 0)
        def _():
            dg_ref[...] = part
            loss_ref[...] = lpart

        @pl.when(i > 0)
        def _():
            dg_ref[...] += part
            loss_ref[...] += lpart

    blk = pl.BlockSpec((tr, D_MODEL), lambda i: (i, 0))
    vec = pl.BlockSpec((1, D_MODEL), lambda i: (0, 0))
    return pl.pallas_call(
        body, grid=(Lp // tr,), in_specs=[blk, vec, blk], out_specs=[vec, blk, blk, vec],
        out_shape=[jax.ShapeDtypeStruct((1, D_MODEL), F32), jax.ShapeDtypeStruct((Lp, D_MODEL), F32),
                   jax.ShapeDtypeStruct((Lp, D_MODEL), BF16), jax.ShapeDtypeStruct((1, D_MODEL), F32)],
        name="final_norm_loss")(h2, g, tgt)


def _halo_prev(tr, width, col=0):
    return pl.BlockSpec((8, width), lambda i: (jnp.maximum(i * (tr // 8) - 1, 0), col))


def _halo_next(tr, width, nrows, col=0):
    last = nrows // 8 - 1
    return pl.BlockSpec((8, width), lambda i: (jnp.minimum((i + 1) * (tr // 8), last), col))


def _shifted(x, offs):
    n = x.shape[0]
    return [x if off == 0 else pltpu.roll(x, n - off, 0) for off in offs]


def _taps(wins, w, rows, bias=None):
    acc = w[0:1, :] * wins[0][0:rows, :]
    if bias is not None:
        acc = acc + bias
    for kk in range(1, len(wins)):
        acc = acc + w[kk:kk + 1, :] * wins[kk][0:rows, :]
    return acc


def _gdn_pre(proj_m, proj_s, conv_w, gparams, pad):
    Lp = proj_m.shape[0]
    tr = _tile(Lp, 192, 64)
    W3 = 3 * D_MODEL

    def body(main_ref, prev_ref, s_ref, w_ref, gp_ref, qkv_ref, gsm_ref):
        i = pl.program_id(0)
        prev = jnp.where(i > 0, prev_ref[...], 0.0)
        ext = jnp.concatenate([prev, main_ref[...]], axis=0)
        c = _taps(_shifted(ext, range(8 - (GDN_CONV - 1), 9)), w_ref[...], tr)
        s = c * _sig(c)
        scale = GDN_D ** -0.5
        for j in range(2 * GDN_H):
            seg = s[:, j * GDN_D:(j + 1) * GDN_D]
            r = lax.rsqrt(_rowsum(seg * seg) + EPS)
            if j < GDN_H:
                r = r * scale
            qkv_ref[:, j * GDN_D:(j + 1) * GDN_D] = seg * r
        qkv_ref[:, 2 * D_MODEL:] = s[:, 2 * D_MODEL:]
        sm = s_ref[...]
        gp = gp_ref[...]
        lane = lax.broadcasted_iota(jnp.int32, sm.shape, 1)
        z = sm + gp[1:2, :]
        softplus = jnp.maximum(z, 0.0) + jnp.log(1.0 + jnp.exp(-jnp.abs(z)))
        lg = -jnp.exp(gp[0:1, :]) * softplus
        row = i * tr + lax.broadcasted_iota(jnp.int32, (tr, 1), 0)
        out = jnp.where(lane < GDN_H, lg, jnp.where(lane < 2 * GDN_H, _sig(sm), 0.0))
        gsm_ref[...] = jnp.where(row >= pad, out, 0.0)

    return pl.pallas_call(
        body, grid=(Lp // tr,),
        in_specs=[pl.BlockSpec((tr, W3), lambda i: (i, 0)), _halo_prev(tr, W3),
                  pl.BlockSpec((tr, LANES), lambda i: (i, 0)),
                  pl.BlockSpec((GDN_CONV, W3), lambda i: (0, 0)), pl.BlockSpec((8, LANES), lambda i: (0, 0))],
        out_specs=[pl.BlockSpec((tr, W3), lambda i: (i, 0)), pl.BlockSpec((tr, LANES), lambda i: (i, 0))],
        out_shape=[jax.ShapeDtypeStruct((Lp, W3), F32), jax.ShapeDtypeStruct((Lp, LANES), F32)],
        name="gdn_pre")(proj_m, proj_m, proj_s, conv_w, gparams)


def _gdn_pre_bwd(proj_m, proj_s, conv_w, gparams, dq, dk, dv, dgs, pad):
    Lp = proj_m.shape[0]
    tr = _tile(Lp, 192, 64)
    W3 = 3 * D_MODEL
    te = tr + 8

    def body(main_ref, prev_ref, next_ref, s_ref, w_ref, gp_ref,
             dq_ref, dqn_ref, dk_ref, dkn_ref, dv_ref, dvn_ref, dgs_ref,
             da_ref, ds_ref, dw_ref, dgp_ref):
        i = pl.program_id(0)
        w = w_ref[...]
        prev = jnp.where(i > 0, prev_ref[...], 0.0)
        ext = jnp.concatenate([prev, main_ref[...], next_ref[...]], axis=0)
        wins = _shifted(ext, range(8 - (GDN_CONV - 1), 9))
        c = _taps(wins, w, te)
        sg = _sig(c)
        s = c * sg
        rowe = i * tr + lax.broadcasted_iota(jnp.int32, (te, 1), 0)
        live = (rowe >= pad) & (rowe < Lp)
        dqe = jnp.concatenate([dq_ref[...], dqn_ref[...]], axis=0)
        dke = jnp.concatenate([dk_ref[...], dkn_ref[...]], axis=0)
        dve = jnp.concatenate([dv_ref[...], dvn_ref[...]], axis=0)
        scale = GDN_D ** -0.5
        parts = []
        for j in range(2 * GDN_H):
            seg = s[:, j * GDN_D:(j + 1) * GDN_D]
            r = lax.rsqrt(_rowsum(seg * seg) + EPS)
            xh = seg * r
            if j < GDN_H:
                dxh = dqe[:, j * GDN_D:(j + 1) * GDN_D] * scale
            else:
                dxh = dke[:, (j - GDN_H) * GDN_D:(j - GDN_H + 1) * GDN_D]
            parts.append(r * (dxh - xh * _rowsum(dxh * xh)))
        parts.append(dve)
        dsv = jnp.concatenate(parts, axis=1)
        dc = jnp.where(live, dsv * (sg * (1.0 + c * (1.0 - sg))), 0.0)
        da_ref[...] = _taps(_shifted(dc, range(GDN_CONV - 1, -1, -1)), w, tr).astype(BF16)
        dcm = dc[0:tr, :]
        rows = [jnp.sum(dcm * wins[kk][0:tr, :], axis=0, keepdims=True) for kk in range(GDN_CONV)]
        dwp = jnp.concatenate(rows + [jnp.zeros((8 - GDN_CONV, W3), F32)], axis=0)

        sm = s_ref[...]
        gp = gp_ref[...]
        lane = lax.broadcasted_iota(jnp.int32, sm.shape, 1)
        rowm = i * tr + lax.broadcasted_iota(jnp.int32, (tr, 1), 0)
        dgv = jnp.where(rowm >= pad, dgs_ref[...], 0.0)
        dlg = jnp.where(lane < GDN_H, dgv, 0.0)
        dbt = jnp.where((lane >= GDN_H) & (lane < 2 * GDN_H), dgv, 0.0)
        z = sm + gp[1:2, :]
        softplus = jnp.maximum(z, 0.0) + jnp.log(1.0 + jnp.exp(-jnp.abs(z)))
        ea = jnp.exp(gp[0:1, :])
        dz = dlg * (-ea) * _sig(z)
        dal = dlg * (-ea) * softplus
        bt = _sig(sm)
        dgb = dbt * bt * (1.0 - bt)
        ds_ref[...] = (dz + dgb).astype(BF16)
        gpp = jnp.concatenate([jnp.sum(dal, axis=0, keepdims=True), jnp.sum(dz, axis=0, keepdims=True),
                               jnp.zeros((6, LANES), F32)], axis=0)

        @pl.when(i == 0)
        def _():
            dw_ref[...] = dwp
            dgp_ref[...] = gpp

        @pl.when(i > 0)
        def _():
            dw_ref[...] += dwp
            dgp_ref[...] += gpp

    m3 = pl.BlockSpec((tr, W3), lambda i: (i, 0))
    m1 = pl.BlockSpec((tr, D_MODEL), lambda i: (i, 0))
    n1 = _halo_next(tr, D_MODEL, Lp)
    return pl.pallas_call(
        body, grid=(Lp // tr,),
        in_specs=[m3, _halo_prev(tr, W3), _halo_next(tr, W3, Lp), pl.BlockSpec((tr, LANES), lambda i: (i, 0)),
                  pl.BlockSpec((GDN_CONV, W3), lambda i: (0, 0)), pl.BlockSpec((8, LANES), lambda i: (0, 0)),
                  m1, n1, m1, n1, m1, n1, pl.BlockSpec((tr, LANES), lambda i: (i, 0))],
        out_specs=[m3, pl.BlockSpec((tr, LANES), lambda i: (i, 0)),
                   pl.BlockSpec((8, W3), lambda i: (0, 0)), pl.BlockSpec((8, LANES), lambda i: (0, 0))],
        out_shape=[jax.ShapeDtypeStruct((Lp, W3), BF16), jax.ShapeDtypeStruct((Lp, LANES), BF16),
                   jax.ShapeDtypeStruct((8, W3), F32), jax.ShapeDtypeStruct((8, LANES), F32)],
        name="gdn_pre_bwd")(proj_m, proj_m, proj_m, proj_s, conv_w, gparams, dq, dq, dk, dk, dv, dv, dgs)


def _gdn_gates(gs):
    ri = lax.broadcasted_iota(jnp.int32, (CHUNK, CHUNK), 0)
    ci = lax.broadcasted_iota(jnp.int32, (CHUNK, CHUNK), 1)
    tril = ri >= ci
    strict = ri > ci
    gall = _dx(tril.astype(F32), gs)
    lane8 = lax.broadcasted_iota(jnp.int32, (8, LANES), 1)
    sub8 = lax.broadcasted_iota(jnp.int32, (8, LANES), 0)
    grow = _dxnt((lane8 == sub8).astype(F32), gall)
    return gall, grow, tril, strict


def _gdn_decay(gall, grow, tril, h):
    g = gall[:, h:h + 1]
    return g, jnp.where(tril, jnp.exp(jnp.where(tril, g - grow[h:h + 1, :], 0.0)), 0.0)


def _gdn_chunk_specs(N, rev):
    cn = (lambda n: N - 1 - n) if rev else (lambda n: n)
    col = lambda j: pl.BlockSpec((CHUNK, D_MODEL), lambda n: (cn(n), j))
    gate = pl.BlockSpec((CHUNK, LANES), lambda n: (cn(n), 0))
    st = lambda a, b: pl.BlockSpec((GDN_H, None, a, b), lambda n: (0, cn(n), 0, 0))
    return col, gate, st


def _gdn_chunk_fwd(qkv, gsm):
    Lp = qkv.shape[0]
    N = Lp // CHUNK

    def body(q_ref, k_ref, v_ref, gs_ref, o_ref, sin_ref, t_ref, S):
        n = pl.program_id(0)

        @pl.when(n == 0)
        def _():
            S[...] = jnp.zeros_like(S)

        gs = gs_ref[...]
        gall, grow, tril, strict = _gdn_gates(gs)
        ri = lax.broadcasted_iota(jnp.int32, (CHUNK, CHUNK), 0)
        ci = lax.broadcasted_iota(jnp.int32, (CHUNK, CHUNK), 1)
        eye = (ri == ci).astype(F32)
        heads = range(GDN_H)
        sls = [slice(h * GDN_D, (h + 1) * GDN_D) for h in heads]
        q = [q_ref[:, sl] for sl in sls]
        k = [k_ref[:, sl] for sl in sls]
        v = [v_ref[:, sl] for sl in sls]
        s0 = [S[h] for h in heads]
        beta = [gs[:, GDN_H + h:GDN_H + h + 1] for h in heads]
        gg = [_gdn_decay(gall, grow, tril, h) for h in heads]
        g = [x[0] for x in gg]
        gam = [x[1] for x in gg]
        eg = [jnp.exp(g[h]) for h in heads]
        gl = [g[h][CHUNK - 1:CHUNK, :] for h in heads]
        kb = [k[h] * beta[h] for h in heads]
        pw = [-jnp.where(strict, _dnt(kb[h], k[h]) * gam[h], 0.0) for h in heads]
        p = [_dnt(q[h], k[h]) * gam[h] for h in heads]
        qs = [_d(q[h] * eg[h], s0[h]) for h in heads]
        t = [eye + pw[h] for h in heads]
        for _ in range(5):
            pw = [_d3g(pw[h], pw[h], _NN) for h in heads]
            t = [t[h] + _d3g(t[h], pw[h], _NN) for h in heads]
        u = [_d(t[h], v[h] * beta[h]) for h in heads]
        w = [_d(t[h], kb[h] * eg[h]) for h in heads]
        vnew = [u[h] - _d(w[h], s0[h]) for h in heads]
        for h in heads:
            o_ref[:, sls[h]] = qs[h] + _d(p[h], vnew[h])
            sin_ref[h] = s0[h]
            t_ref[h] = t[h]
            S[h] = s0[h] * jnp.exp(gl[h]) + _dtn(k[h] * jnp.exp(gl[h] - g[h]), vnew[h])

    col, gate, st = _gdn_chunk_specs(N, False)
    return pl.pallas_call(
        body, grid=(N,),
        in_specs=[col(0), col(1), col(2), gate],
        out_specs=[col(0), st(GDN_D, GDN_D), st(CHUNK, CHUNK)],
        out_shape=[jax.ShapeDtypeStruct((Lp, D_MODEL), F32), jax.ShapeDtypeStruct((GDN_H, N, GDN_D, GDN_D), F32),
                   jax.ShapeDtypeStruct((GDN_H, N, CHUNK, CHUNK), F32)],
        scratch_shapes=[pltpu.VMEM((GDN_H, GDN_D, GDN_D), F32)],
        name="gdn_chunk_fwd")(qkv, qkv, qkv, gsm)


def _gdn_chunk_bwd(qkv, gsm, do, s_in, t_in):
    Lp = qkv.shape[0]
    N = Lp // CHUNK

    def body(q_ref, k_ref, v_ref, gs_ref, do_ref, sin_ref, t_ref, dq_ref, dk_ref, dv_ref, dgs_ref, dS):
        n = pl.program_id(0)

        @pl.when(n == 0)
        def _():
            dS[...] = jnp.zeros_like(dS)

        gs = gs_ref[...]
        gall, grow, tril, strict = _gdn_gates(gs)
        lane = lax.broadcasted_iota(jnp.int32, (CHUNK, LANES), 1)
        rcol = lax.broadcasted_iota(jnp.int32, (CHUNK, 1), 0)
        ones = jnp.ones((CHUNK, LANES), F32)
        dg_all = jnp.zeros((CHUNK, LANES), F32)
        dbeta_all = jnp.zeros((CHUNK, LANES), F32)
        heads = range(GDN_H)
        sls = [slice(h * GDN_D, (h + 1) * GDN_D) for h in heads]
        H = lambda f: [f(h) for h in heads]
        q = H(lambda h: q_ref[:, sls[h]])
        k = H(lambda h: k_ref[:, sls[h]])
        v = H(lambda h: v_ref[:, sls[h]])
        dov = H(lambda h: do_ref[:, sls[h]])
        s0 = H(lambda h: sin_ref[h])
        t = H(lambda h: t_ref[h])
        dsv = H(lambda h: dS[h])
        beta = H(lambda h: gs[:, GDN_H + h:GDN_H + h + 1])
        gg = H(lambda h: _gdn_decay(gall, grow, tril, h))
        g = [x[0] for x in gg]
        gam = [x[1] for x in gg]
        eg = H(lambda h: jnp.exp(g[h]))
        egl = H(lambda h: jnp.exp(g[h][CHUNK - 1:CHUNK, :]))
        e = H(lambda h: jnp.exp(g[h][CHUNK - 1:CHUNK, :] - g[h]))
        kb = H(lambda h: k[h] * beta[h])
        kbg = H(lambda h: kb[h] * eg[h])
        vb = H(lambda h: v[h] * beta[h])
        qg = H(lambda h: q[h] * eg[h])
        kd = H(lambda h: k[h] * e[h])
        m = H(lambda h: jnp.where(strict, _dnt(kb[h], k[h]) * gam[h], 0.0))
        u = H(lambda h: _d(t[h], vb[h]))
        w = H(lambda h: _d(t[h], kbg[h]))
        p = H(lambda h: _dnt(q[h], k[h]) * gam[h])
        dqg = H(lambda h: _dnt(dov[h], s0[h]))
        kdds = H(lambda h: _d(kd[h], dsv[h]))
        qgdo = H(lambda h: _dtn(qg[h], dov[h]))
        vnew = H(lambda h: u[h] - _d(w[h], s0[h]))
        dvnew = H(lambda h: _dtn(p[h], dov[h]) + kdds[h])
        dp = H(lambda h: jnp.where(tril, _dnt(dov[h], vnew[h]), 0.0))
        dkd = H(lambda h: _dnt(vnew[h], dsv[h]))
        dw = H(lambda h: -_dnt(dvnew[h], s0[h]))
        for h in heads:
            dS[h] = qgdo[h] + egl[h] * dsv[h] - _dtn(w[h], dvnew[h])
        dvb = H(lambda h: _dtn(t[h], dvnew[h]))
        dkbg = H(lambda h: _dtn(t[h], dw[h]))
        dt = H(lambda h: _dnt(dvnew[h], vb[h]) + _dnt(dw[h], kbg[h]))
        x1 = H(lambda h: _d3g(t[h], dt[h], _TN))
        dm = H(lambda h: jnp.where(strict, -_d3g(x1[h], t[h], _NT), 0.0))
        dkk = H(lambda h: dm[h] * gam[h])
        dqk = H(lambda h: dp[h] * gam[h])
        dkb = H(lambda h: _d(dkk[h], k[h]) + eg[h] * dkbg[h])
        em = H(lambda h: dm[h] * m[h] + dp[h] * p[h])
        colsum = H(lambda h: _d3g(em[h], ones, _TN)[:, 0:1])
        for h in heads:
            dk_ref[:, sls[h]] = _dtn(dkk[h], kb[h]) + _dtn(dqk[h], q[h]) + dkd[h] * e[h] + beta[h] * dkb[h]
            dq_ref[:, sls[h]] = _d(dqk[h], k[h]) + dqg[h] * eg[h]
            dv_ref[:, sls[h]] = beta[h] * dvb[h]
        for h in heads:
            dbeta = _rowsum(k[h] * dkb[h]) + _rowsum(v[h] * dvb[h])
            z = _rowsum(kd[h] * dkd[h])
            dg = _rowsum(em[h]) - colsum[h] + _rowsum(qg[h] * dqg[h]) + _rowsum(kbg[h] * dkbg[h]) - z
            extra = _allsum(z) + egl[h] * _allsum(s0[h] * dsv[h])
            dg = dg + jnp.where(rcol == CHUNK - 1, extra, 0.0)
            dg_all = dg_all + jnp.where(lane == h, dg, 0.0)
            dbeta_all = dbeta_all + jnp.where(lane == GDN_H + h, dbeta, 0.0)
        ri = lax.broadcasted_iota(jnp.int32, (CHUNK, CHUNK), 0)
        ci = lax.broadcasted_iota(jnp.int32, (CHUNK, CHUNK), 1)
        dgs_ref[...] = _dx((ci >= ri).astype(F32), dg_all) + dbeta_all

    col, gate, st = _gdn_chunk_specs(N, True)
    return pl.pallas_call(
        body, grid=(N,),
        in_specs=[col(0), col(1), col(2), gate, col(0), st(GDN_D, GDN_D), st(CHUNK, CHUNK)],
        out_specs=[col(0), col(0), col(0), gate],
        out_shape=[jax.ShapeDtypeStruct((Lp, D_MODEL), F32)] * 3 + [jax.ShapeDtypeStruct((Lp, LANES), F32)],
        scratch_shapes=[pltpu.VMEM((GDN_H, GDN_D, GDN_D), F32)],
        name="gdn_chunk_bwd")(qkv, qkv, qkv, gsm, do, s_in, t_in)


def _rot(x, c, s):
    half = RET_D // 2
    x1 = x[:, :half]
    x2 = x[:, half:]
    return jnp.concatenate([x1 * c - x2 * s, x2 * c + x1 * s], axis=1)


def _rot_bwd(d, c, s):
    half = RET_D // 2
    d1 = d[:, :half]
    d2 = d[:, half:]
    return jnp.concatenate([d1 * c + d2 * s, d2 * c - d1 * s], axis=1)


def _ret_tables():
    hh = jnp.arange(RET_H, dtype=F32)
    lg = jnp.log(1.0 - 2.0 ** (-5.0 - hh))
    idx = jnp.arange(CHUNK, dtype=F32)
    tril = jnp.asarray(np.tril(np.ones((CHUNK, CHUNK), dtype=bool)))
    dmask = jnp.where(tril, jnp.exp((idx[:, None] - idx[None, :]) * lg[:, None, None]), 0.0)
    qdec = jnp.exp((idx[None, :] + 1.0) * lg[:, None])
    kdec = jnp.exp((CHUNK - 1.0 - idx[None, :]) * lg[:, None])
    gch = jnp.exp(CHUNK * lg)
    qdec = jnp.broadcast_to(qdec[:, :, None], (RET_H, CHUNK, RET_D))
    kdec = jnp.broadcast_to(kdec[:, :, None], (RET_H, CHUNK, RET_D))
    gch = jnp.broadcast_to(gch[:, None, None], (RET_H, 8, LANES))
    return dmask, qdec, kdec, gch


def _ret_specs(N, rev):
    cn = (lambda n: N - 1 - n) if rev else (lambda n: n)
    col = lambda j: pl.BlockSpec((CHUNK, D_MODEL), lambda n: (cn(n), j))
    tab = lambda a, b: pl.BlockSpec((RET_H, a, b), lambda n: (0, 0, 0))
    rope = pl.BlockSpec((CHUNK, LANES), lambda n: (cn(n), 0))
    st = pl.BlockSpec((RET_H, None, RET_D, RET_D), lambda n: (0, cn(n), 0, 0))
    return col, tab, rope, st


def _ret_chunk_fwd(proj_m, cos, sin, tables):
    Lp = proj_m.shape[0]
    N = Lp // CHUNK
    dmask, qdec, kdec, gch = tables

    def body(q_ref, k_ref, v_ref, c_ref, s_ref, dm_ref, qd_ref, kd_ref, g_ref, o_ref, sin_ref, S):
        n = pl.program_id(0)

        @pl.when(n == 0)
        def _():
            S[...] = jnp.zeros_like(S)

        c = c_ref[...]
        s = s_ref[...]
        heads = range(RET_H)
        sls = [slice(h * RET_D, (h + 1) * RET_D) for h in heads]
        H = lambda f: [f(h) for h in heads]
        qr = H(lambda h: _rot(q_ref[:, sls[h]], c, s))
        ks = H(lambda h: _rot(k_ref[:, sls[h]], c, s) * (RET_D ** -0.5))
        v = H(lambda h: v_ref[:, sls[h]])
        s0 = H(lambda h: S[h])
        a = H(lambda h: _dnt(qr[h], ks[h]) * dm_ref[h])
        qs = H(lambda h: _d(qr[h] * qd_ref[h], s0[h]))
        kv = H(lambda h: _dtn(ks[h] * kd_ref[h], v[h]))
        for h in heads:
            o_ref[:, sls[h]] = _d(a[h], v[h]) + qs[h]
            sin_ref[h] = s0[h]
            S[h] = s0[h] * g_ref[h, 0:1, 0:1] + kv[h]

    col, tab, rope, st = _ret_specs(N, False)
    return pl.pallas_call(
        body, grid=(N,),
        in_specs=[col(3), col(4), col(5), rope, rope,
                  tab(CHUNK, CHUNK), tab(CHUNK, RET_D), tab(CHUNK, RET_D), tab(8, LANES)],
        out_specs=[col(0), st],
        out_shape=[jax.ShapeDtypeStruct((Lp, D_MODEL), F32), jax.ShapeDtypeStruct((RET_H, N, RET_D, RET_D), F32)],
        scratch_shapes=[pltpu.VMEM((RET_H, RET_D, RET_D), F32)],
        name="ret_chunk_fwd")(proj_m, proj_m, proj_m, cos, sin, dmask, qdec, kdec, gch)


def _ret_chunk_bwd(proj_m, cos, sin, tables, do, s_in):
    Lp = proj_m.shape[0]
    N = Lp // CHUNK
    dmask, qdec, kdec, gch = tables

    def body(q_ref, k_ref, v_ref, c_ref, s_ref, dm_ref, qd_ref, kd_ref, g_ref, do_ref, sin_ref,
             dq_ref, dk_ref, dv_ref, dS):
        n = pl.program_id(0)

        @pl.when(n == 0)
        def _():
            dS[...] = jnp.zeros_like(dS)

        c = c_ref[...]
        s = s_ref[...]
        kscale = RET_D ** -0.5
        heads = range(RET_H)
        sls = [slice(h * RET_D, (h + 1) * RET_D) for h in heads]
        H = lambda f: [f(h) for h in heads]
        qr = H(lambda h: _rot(q_ref[:, sls[h]], c, s))
        ks = H(lambda h: _rot(k_ref[:, sls[h]], c, s) * kscale)
        v = H(lambda h: v_ref[:, sls[h]])
        dov = H(lambda h: do_ref[:, sls[h]])
        s0 = H(lambda h: sin_ref[h])
        dsv = H(lambda h: dS[h])
        ad = H(lambda h: _dnt(qr[h], ks[h]) * dm_ref[h])
        da = H(lambda h: _dnt(dov[h], v[h]) * dm_ref[h])
        kds = H(lambda h: _d(ks[h] * kd_ref[h], dsv[h]))
        dos = H(lambda h: _dnt(dov[h], s0[h]) * qd_ref[h])
        vds = H(lambda h: _dnt(v[h], dsv[h]) * kd_ref[h])
        qdo = H(lambda h: _dtn(qr[h] * qd_ref[h], dov[h]))
        for h in heads:
            dS[h] = dsv[h] * g_ref[h, 0:1, 0:1] + qdo[h]
        for h in heads:
            dv_ref[:, sls[h]] = (_dtn(ad[h], dov[h]) + kds[h]).astype(BF16)
            dq_ref[:, sls[h]] = _rot_bwd(_d(da[h], ks[h]) + dos[h], c, s).astype(BF16)
            dk_ref[:, sls[h]] = _rot_bwd((_dtn(da[h], qr[h]) + vds[h]) * kscale, c, s).astype(BF16)

    col, tab, rope, st = _ret_specs(N, True)
    return pl.pallas_call(
        body, grid=(N,),
        in_specs=[col(3), col(4), col(5), rope, rope,
                  tab(CHUNK, CHUNK), tab(CHUNK, RET_D), tab(CHUNK, RET_D), tab(8, LANES), col(0), st],
        out_specs=[col(0), col(0), col(0)],
        out_shape=[jax.ShapeDtypeStruct((Lp, D_MODEL), BF16)] * 3,
        scratch_shapes=[pltpu.VMEM((RET_H, RET_D, RET_D), F32)],
        name="ret_chunk_bwd")(proj_m, proj_m, proj_m, cos, sin, dmask, qdec, kdec, gch, do, s_in)


def _merge_specs(tr):
    col = lambda j: pl.BlockSpec((tr, D_MODEL), lambda i: (i, j))
    return col


def _merge_fwd(o_a, o_b, proj_m, gnorm):
    Lp = o_a.shape[0]
    tr = _tile(Lp, 192, 16)

    def body(oa_ref, ob_ref, gz_ref, rg_ref, ga_ref, gb_ref, gn_ref, y_ref):
        gn = gn_ref[...]
        oa = oa_ref[...]
        ob = ob_ref[...]
        gz = gz_ref[...]
        ya = []
        for j in range(GDN_H):
            seg = oa[:, j * GDN_D:(j + 1) * GDN_D]
            r = lax.rsqrt(jnp.mean(seg * seg, axis=-1, keepdims=True) + EPS)
            ya.append(seg * r * gn)
        ya = jnp.concatenate(ya, axis=1) * (gz * _sig(gz))
        yb = []
        for j in range(RET_H):
            seg = ob[:, j * RET_D:(j + 1) * RET_D]
            r = lax.rsqrt(jnp.mean(seg * seg, axis=-1, keepdims=True) + EPS)
            yb.append(seg * r)
        rg = rg_ref[...]
        yb = jnp.concatenate(yb, axis=1) * (rg * _sig(rg))
        y_ref[...] = (_sig(ga_ref[...]) * ya + _sig(gb_ref[...]) * yb).astype(BF16)

    col = _merge_specs(tr)
    return pl.pallas_call(
        body, grid=(Lp // tr,),
        in_specs=[col(0), col(0), col(6), col(7), col(8), col(9), pl.BlockSpec((1, GDN_D), lambda i: (0, 0))],
        out_specs=col(0), out_shape=jax.ShapeDtypeStruct((Lp, D_MODEL), BF16),
        name="merge_fwd")(o_a, o_b, proj_m, proj_m, proj_m, proj_m, gnorm)


def _merge_bwd(dy, o_a, o_b, proj_m, gnorm):
    Lp = o_a.shape[0]
    tr = _tile(Lp, 192, 16)

    def body(dy_ref, oa_ref, ob_ref, gz_ref, rg_ref, ga_ref, gb_ref, gn_ref, dc_ref, doa_ref, dob_ref, dgn_ref):
        i = pl.program_id(0)
        gn = gn_ref[...]
        dyv = dy_ref[...]
        oa = oa_ref[...]
        ob = ob_ref[...]
        gz = gz_ref[...]
        rg = rg_ref[...]
        sa = _sig(ga_ref[...])
        sb = _sig(gb_ref[...])
        dya = dyv * sa
        dyb = dyv * sb
        sgz = _sig(gz)
        szz = gz * sgz
        dgn = jnp.zeros((1, GDN_D), F32)
        ya = []
        dgz = []
        for j in range(GDN_H):
            sl = slice(j * GDN_D, (j + 1) * GDN_D)
            seg = oa[:, sl]
            r = lax.rsqrt(jnp.mean(seg * seg, axis=-1, keepdims=True) + EPS)
            xh = seg * r
            oan = xh * gn
            ya.append(oan * szz[:, sl])
            dgz.append(dya[:, sl] * oan * (sgz[:, sl] * (1.0 + gz[:, sl] * (1.0 - sgz[:, sl]))))
            doan = dya[:, sl] * szz[:, sl]
            dgn = dgn + jnp.sum(doan * xh, axis=0, keepdims=True)
            dxh = doan * gn
            doa_ref[:, sl] = r * (dxh - xh * jnp.mean(dxh * xh, axis=-1, keepdims=True))
        ya = jnp.concatenate(ya, axis=1)
        srg = _sig(rg)
        srr = rg * srg
        yb = []
        drg = []
        for j in range(RET_H):
            sl = slice(j * RET_D, (j + 1) * RET_D)
            seg = ob[:, sl]
            r = lax.rsqrt(jnp.mean(seg * seg, axis=-1, keepdims=True) + EPS)
            xh = seg * r
            yb.append(xh * srr[:, sl])
            drg.append(dyb[:, sl] * xh * (srg[:, sl] * (1.0 + rg[:, sl] * (1.0 - srg[:, sl]))))
            dxh = dyb[:, sl] * srr[:, sl]
            dob_ref[:, sl] = r * (dxh - xh * jnp.mean(dxh * xh, axis=-1, keepdims=True))
        yb = jnp.concatenate(yb, axis=1)
        dc_ref[:, 0:D_MODEL] = jnp.concatenate(dgz, axis=1).astype(BF16)
        dc_ref[:, D_MODEL:2 * D_MODEL] = jnp.concatenate(drg, axis=1).astype(BF16)
        dc_ref[:, 2 * D_MODEL:3 * D_MODEL] = (dyv * ya * sa * (1.0 - sa)).astype(BF16)
        dc_ref[:, 3 * D_MODEL:] = (dyv * yb * sb * (1.0 - sb)).astype(BF16)

        @pl.when(i == 0)
        def _():
            dgn_ref[...] = dgn

        @pl.when(i > 0)
        def _():
            dgn_ref[...] += dgn

    col = _merge_specs(tr)
    return pl.pallas_call(
        body, grid=(Lp // tr,),
        in_specs=[col(0), col(0), col(0), col(6), col(7), col(8), col(9), pl.BlockSpec((1, GDN_D), lambda i: (0, 0))],
        out_specs=[pl.BlockSpec((tr, 4 * D_MODEL), lambda i: (i, 0)), col(0), col(0),
                   pl.BlockSpec((1, GDN_D), lambda i: (0, 0))],
        out_shape=[jax.ShapeDtypeStruct((Lp, 4 * D_MODEL), BF16), jax.ShapeDtypeStruct((Lp, D_MODEL), F32),
                   jax.ShapeDtypeStruct((Lp, D_MODEL), F32), jax.ShapeDtypeStruct((1, GDN_D), F32)],
        name="merge_bwd")(dy, o_a, o_b, proj_m, proj_m, proj_m, proj_m, gnorm)


def _ffn_act(up, conv_w, conv_b):
    Lp = up.shape[0]
    tr = _tile(Lp, 192, 16)
    W2 = 2 * D_FF

    def body(main_ref, prev_ref, w_ref, b_ref, act_ref):
        i = pl.program_id(0)
        prev = jnp.where(i > 0, prev_ref[...], 0.0)
        ext = jnp.concatenate([prev, main_ref[...]], axis=0)
        u = _taps(_shifted(ext, range(8 - (FFN_CONV - 1), 9)), w_ref[...], tr, b_ref[...])
        a = u[:, :D_FF]
        act_ref[...] = (a * _sig(a) * u[:, D_FF:]).astype(BF16)

    return pl.pallas_call(
        body, grid=(Lp // tr,),
        in_specs=[pl.BlockSpec((tr, W2), lambda i: (i, 0)), _halo_prev(tr, W2),
                  pl.BlockSpec((FFN_CONV, W2), lambda i: (0, 0)), pl.BlockSpec((1, W2), lambda i: (0, 0))],
        out_specs=pl.BlockSpec((tr, D_FF), lambda i: (i, 0)),
        out_shape=jax.ShapeDtypeStruct((Lp, D_FF), BF16), name="ffn_act")(up, up, conv_w, conv_b)


def _ffn_act_bwd(up, dact, conv_w, conv_b):
    Lp = up.shape[0]
    tr = _tile(Lp, 96, 16)
    W2 = 2 * D_FF
    te = tr + 8

    def body(main_ref, prev_ref, next_ref, da_ref, dan_ref, w_ref, b_ref, dup_ref, acc_ref):
        i = pl.program_id(0)
        w = w_ref[...]
        prev = jnp.where(i > 0, prev_ref[...], 0.0)
        ext = jnp.concatenate([prev, main_ref[...], next_ref[...]], axis=0)
        wins = _shifted(ext, range(8 - (FFN_CONV - 1), 9))
        u = _taps(wins, w, te, b_ref[...])
        a = u[:, :D_FF]
        b = u[:, D_FF:]
        rowe = i * tr + lax.broadcasted_iota(jnp.int32, (te, 1), 0)
        dae = jnp.where(rowe < Lp, jnp.concatenate([da_ref[...], dan_ref[...]], axis=0), 0.0)
        sg = _sig(a)
        du = jnp.concatenate([dae * b * (sg * (1.0 + a * (1.0 - sg))), dae * (a * sg)], axis=1)
        dup_ref[...] = _taps(_shifted(du, range(FFN_CONV - 1, -1, -1)), w, tr).astype(BF16)
        dum = du[0:tr, :]
        rows = [jnp.sum(dum * wins[kk][0:tr, :], axis=0, keepdims=True) for kk in range(FFN_CONV)]
        rows.append(jnp.sum(dum, axis=0, keepdims=True))
        part = jnp.concatenate(rows + [jnp.zeros((8 - len(rows), W2), F32)], axis=0)

        @pl.when(i == 0)
        def _():
            acc_ref[...] = part

        @pl.when(i > 0)
        def _():
            acc_ref[...] += part

    return pl.pallas_call(
        body, grid=(Lp // tr,),
        in_specs=[pl.BlockSpec((tr, W2), lambda i: (i, 0)), _halo_prev(tr, W2), _halo_next(tr, W2, Lp),
                  pl.BlockSpec((tr, D_FF), lambda i: (i, 0)), _halo_next(tr, D_FF, Lp),
                  pl.BlockSpec((FFN_CONV, W2), lambda i: (0, 0)), pl.BlockSpec((1, W2), lambda i: (0, 0))],
        out_specs=[pl.BlockSpec((tr, W2), lambda i: (i, 0)), pl.BlockSpec((8, W2), lambda i: (0, 0))],
        out_shape=[jax.ShapeDtypeStruct((Lp, W2), BF16), jax.ShapeDtypeStruct((8, W2), F32)],
        name="ffn_act_bwd")(up, up, up, dact, dact, conv_w, conv_b)


def _local_step(hpad, tgt, pad, wt, first_weights=None, late_weights=None, on_ffn_out_grads=None,
                on_w_in_grads=None):
    Lp = hpad.shape[0]
    first = pad + N_META
    pos = jnp.arange(Lp, dtype=F32) - float(pad)
    half = RET_D // 2
    inv = 1.0 / (ROPE_BASE ** (jnp.arange(half, dtype=F32) / half))
    ang = pos[:, None] * inv[None, :]
    cos, sin = jnp.cos(ang), jnp.sin(ang)
    tables = _ret_tables()
    gparams = jnp.zeros((8, LANES), F32).at[0, :GDN_H].set(wt["a_log"]).at[1, :GDN_H].set(wt["dt_bias"])

    hn1 = _rms_fwd(hpad, wt["norm1"], "rms1_fwd")
    if first_weights is not None:
        wt = {**wt, **first_weights(hn1)}
    proj_m = _mm_nn(hn1, wt["w_main_t"], bt=True, name="proj_main")
    proj_s = _mm_nn(hn1, wt["w_small_t"], bt=True, name="proj_small")
    qkv, gsm = _gdn_pre(proj_m, proj_s, wt["gdn_conv_w"], gparams, pad)
    o_a, s_a, t_a = _gdn_chunk_fwd(qkv, gsm)
    o_b, s_b = _ret_chunk_fwd(proj_m, cos, sin, tables)
    y = _merge_fwd(o_a, o_b, proj_m, wt["gdn_norm"])
    if late_weights is not None:
        wt = {**wt, **late_weights(y)}
    h1 = _mm_nn(y, wt["w_out"], res=hpad, name="out_proj")
    hn2 = _rms_fwd(h1, wt["norm2"], "rms2_fwd")
    up = _mm_nn(hn2, wt["w_up_t"], bt=True, name="ffn_up")
    act = _ffn_act(up, wt["ffn_conv_w"], wt["ffn_conv_b"])
    h2 = _mm_nn(act, wt["w_down"], res=h1, name="ffn_down")
    lossvec, dh2, dh2b, d_norm_f = _final(h2, wt["norm_f"], tgt, first)

    d_w_down = _mm_tn(act, dh2b, name="dw_down")
    dact = _mm_nt(dh2b, wt["w_down"], name="d_act")
    dup, ffn_rows = _ffn_act_bwd(up, dact, wt["ffn_conv_w"], wt["ffn_conv_b"])
    d_w_up_t = _mm_tn(dup, hn2, name="dw_up")
    dhn2 = _mm_nn(dup, wt["w_up_t"], name="d_hn2")
    dh1, dh1b, d_norm2 = _rms_bwd(h1, wt["norm2"], dhn2, dh2, pad, "rms2_bwd")

    d_w_out = _mm_tn(y, dh1b, name="dw_out")
    dy = _mm_nt(dh1b, wt["w_out"], name="d_y")
    gnorm = wt["gdn_norm"]
    if on_ffn_out_grads is not None:
        gnorm = gnorm + on_ffn_out_grads(d_w_down, d_w_up_t, d_w_out)[0:1, :]
    d_c, do_a, do_b, d_gnorm = _merge_bwd(dy, o_a, o_b, proj_m, gnorm)
    drq, drk, drv = _ret_chunk_bwd(proj_m, cos, sin, tables, do_b, s_b)
    dq, dk, dv, dgs = _gdn_chunk_bwd(qkv, gsm, do_a, s_a, t_a)
    d_a, d_s, conv_rows, gp_rows = _gdn_pre_bwd(proj_m, proj_s, wt["gdn_conv_w"], gparams, dq, dk, dv, dgs, pad)

    wmt = wt["w_main_t"]
    segs = [(d_a, 0, 3 * D_MODEL), (drq, 3 * D_MODEL, D_MODEL), (drk, 4 * D_MODEL, D_MODEL),
            (drv, 5 * D_MODEL, D_MODEL), (d_c, 6 * D_MODEL, 4 * D_MODEL)]
    pa, prq, prk, prv, pc = [_mm_tn(d, hn1, name="dw_in_%d" % i) for i, (d, _, _) in enumerate(segs)]
    ps = _mm_tn(d_s, hn1, name="dw_in_small")
    d_w_in_t = jnp.concatenate([pa, pc[:D_MODEL], ps[:2 * GDN_H], prq, prk, prv, pc[D_MODEL:]], axis=0)
    w_small_t = wt["w_small_t"]
    if on_w_in_grads is not None:
        w_small_t = w_small_t + on_w_in_grads(d_w_in_t)[0:1, 0:1].astype(w_small_t.dtype)
    dhn1 = _mm_nn(d_s, w_small_t, name="d_hn1_small")
    for i, (d, off, width) in enumerate(segs):
        dhn1 = _mm_nn(d, wmt[off:off + width], res=dhn1, name="d_hn1_%d" % i)
    dh0, _, d_norm1 = _rms_bwd(hpad, wt["norm1"], dhn1, dh1, pad, "rms1_bwd")

    grads = {
        "norm1": d_norm1, "w_in_t": d_w_in_t, "gdn_conv_w": conv_rows[:GDN_CONV],
        "a_log": gp_rows[0, :GDN_H], "dt_bias": gp_rows[1, :GDN_H], "gdn_norm": d_gnorm, "w_out": d_w_out,
        "norm2": d_norm2, "w_up_t": d_w_up_t, "ffn_conv_w": ffn_rows[:FFN_CONV],
        "ffn_conv_b": ffn_rows[FFN_CONV:FFN_CONV + 1], "w_down": d_w_down, "norm_f": d_norm_f,
    }
    return lossvec, dh0, grads


def _peer(k):
    ix, iy, ic = lax.axis_index("x"), lax.axis_index("y"), lax.axis_index("c")
    px = 1 - ix if (k >> 2) & 1 else ix
    py = 1 - iy if (k >> 1) & 1 else iy
    pc = 1 - ic if k & 1 else ic
    return (px, py, pc), 4 * px + 2 * py + pc


def _comm_call(body, n, out_shapes, name, args):
    hbm = pl.BlockSpec(memory_space=pl.ANY)
    return pl.pallas_call(
        body, out_shape=out_shapes, in_specs=[hbm] * n, out_specs=[hbm] * n,
        scratch_shapes=[pltpu.SemaphoreType.DMA((n, N_DEV - 1)), pltpu.SemaphoreType.DMA((n, N_DEV - 1)),
                        pltpu.SemaphoreType.DMA((n,))],
        name=name)(*args)


def _all_gather(xs, name):
    n = len(xs)

    def body(*refs):
        x_refs, out_refs = refs[:n], refs[n:2 * n]
        send_sems, recv_sems, local_sems = refs[2 * n:]
        _, me = _peer(0)
        pending = []
        for i in range(n):
            local = pltpu.make_async_copy(x_refs[i], out_refs[i].at[me], local_sems.at[i])
            local.start()
            pending.append(local)
        sends = []
        for i in range(n):
            for k in range(1, N_DEV):
                dev, _ = _peer(k)
                cp = pltpu.make_async_remote_copy(
                    src_ref=x_refs[i], dst_ref=out_refs[i].at[me], send_sem=send_sems.at[i, k - 1],
                    recv_sem=recv_sems.at[i, k - 1], device_id=dev, device_id_type=MESH_T)
                cp.start()
                sends.append(cp)
        for i in range(n):
            for k in range(1, N_DEV):
                dev, idx = _peer(k)
                pltpu.make_async_remote_copy(
                    src_ref=x_refs[i], dst_ref=out_refs[i].at[idx], send_sem=send_sems.at[i, k - 1],
                    recv_sem=recv_sems.at[i, k - 1], device_id=dev, device_id_type=MESH_T).wait_recv()
        for cp in sends:
            cp.wait_send()
        for local in pending:
            local.wait()

    out_shapes = [jax.ShapeDtypeStruct((N_DEV,) + a.shape, a.dtype) for a in xs]
    return _comm_call(body, n, out_shapes, name, xs)


def _all_to_all(gs, name):
    n = len(gs)

    def body(*refs):
        g_refs, out_refs = refs[:n], refs[n:2 * n]
        send_sems, recv_sems, local_sems = refs[2 * n:]
        _, me = _peer(0)
        pending = []
        for i in range(n):
            local = pltpu.make_async_copy(g_refs[i].at[me], out_refs[i].at[0], local_sems.at[i])
            local.start()
            pending.append(local)
        sends = []
        for i in range(n):
            for k in range(1, N_DEV):
                dev, idx = _peer(k)
                cp = pltpu.make_async_remote_copy(
                    src_ref=g_refs[i].at[idx], dst_ref=out_refs[i].at[k], send_sem=send_sems.at[i, k - 1],
                    recv_sem=recv_sems.at[i, k - 1], device_id=dev, device_id_type=MESH_T)
                cp.start()
                sends.append(cp)
        for cp in sends:
            cp.wait_recv()
        for cp in sends:
            cp.wait_send()
        for local in pending:
            local.wait()

    out_shapes = [jax.ShapeDtypeStruct(g.shape, g.dtype) for g in gs]
    return _comm_call(body, n, out_shapes, name, gs)


_SPLIT_RELATIONS = {"gather": tuple(range(1, N_DEV)), "a2a": tuple(range(1, N_DEV)), "chip": (1, 2, 4, 6),
                    "forward": (2, 4, 6)}


def _split_copies(kind, src_refs, land_refs, send_sems, recv_sems, local_sems, with_recv):
    n = len(land_refs)
    rels = _SPLIT_RELATIONS[kind]
    _, me = _peer(0)
    locals_, remotes = [], []
    for i in range(n):
        if kind in ("gather", "chip"):
            locals_.append(pltpu.make_async_copy(src_refs[i], land_refs[i].at[me], local_sems.at[i]))
        elif kind == "a2a":
            locals_.append(pltpu.make_async_copy(src_refs[i].at[me], land_refs[i].at[0], local_sems.at[i]))
        for jj, k in enumerate(rels):
            dev, idx = _peer(k)
            if kind in ("gather", "chip"):
                src, dst, mine = src_refs[i], land_refs[i].at[me], land_refs[i].at[idx]
            elif kind == "a2a":
                src, dst, mine = src_refs[i].at[idx], land_refs[i].at[k], land_refs[i].at[k]
            else:
                dev, _ = _peer(1)
                _, came = _peer(k + 1)
                src, dst, mine = land_refs[i].at[idx], land_refs[i].at[idx], land_refs[i].at[came]
            j = i * len(rels) + jj
            send = pltpu.make_async_remote_copy(
                src_ref=src, dst_ref=dst, send_sem=send_sems.at[j], recv_sem=recv_sems.at[j],
                device_id=dev, device_id_type=MESH_T)
            recv = pltpu.make_async_remote_copy(
                src_ref=src, dst_ref=mine, send_sem=send_sems.at[j], recv_sem=recv_sems.at[j],
                device_id=dev, device_id_type=MESH_T) if with_recv else None
            remotes.append((send, recv))
    return locals_, remotes


_HBM = pl.BlockSpec(memory_space=pltpu.HBM)
_SEM = pl.BlockSpec(memory_space=pltpu.SEMAPHORE)
_ANY = pl.BlockSpec(memory_space=pl.ANY)


def _split_start(srcs, kind, name, after):
    n = len(srcs)
    if kind == "forward":
        arrays = list(srcs)
    else:
        gathers = kind in ("gather", "chip")
        arrays = list(srcs) + [lax.empty(((N_DEV,) + a.shape) if gathers else a.shape, a.dtype) for a in srcs]
    na = len(arrays)

    def body(*refs):
        src_refs, land_refs = refs[:n], refs[na - n:na]
        send_sems, recv_sems, local_sems = refs[na + 1:na + 4]
        token = refs[-1]
        locals_, remotes = _split_copies(kind, src_refs, land_refs, send_sems, recv_sems, local_sems, False)
        for cp in locals_:
            cp.start()
        for send, _ in remotes:
            send.start()
        token[...] = jnp.zeros_like(token)

    ncp = n * len(_SPLIT_RELATIONS[kind])
    sems = (pltpu.SemaphoreType.DMA((ncp,)), pltpu.SemaphoreType.DMA((ncp,)), pltpu.SemaphoreType.DMA((n,)))
    thru = tuple(pltpu.HBM(a.shape, a.dtype) for a in arrays)
    outs = pl.pallas_call(
        body, name=name,
        out_shape=sems + thru + (jax.ShapeDtypeStruct((8, LANES), F32),),
        in_specs=[_HBM] * na + [_ANY],
        out_specs=[_SEM] * 3 + [_HBM] * na + [pl.BlockSpec(memory_space=pltpu.VMEM)],
        input_output_aliases={i: 3 + i for i in range(na)},
        compiler_params=pltpu.CompilerParams(has_side_effects=pltpu.SideEffectType.DATAFLOW_SIDE_EFFECTING),
    )(*[pltpu.with_memory_space_constraint(a, pltpu.HBM) for a in arrays], after)
    return (kind, n, outs[:3], outs[3:3 + na]), outs[-1]


def _split_wait(handle, name, after):
    kind, n, sems, thru = handle
    na = len(thru)

    def body(*refs):
        src_refs, land_refs = refs[:n], refs[na - n:na]
        send_sems, recv_sems, local_sems = refs[na:na + 3]
        locals_, remotes = _split_copies(kind, src_refs, land_refs, send_sems, recv_sems, local_sems, True)
        for send, recv in remotes:
            send.wait_send()
            recv.wait_recv()
        for cp in locals_:
            cp.wait()

    outs = pl.pallas_call(
        body, name=name, out_shape=tuple(pltpu.HBM(a.shape, a.dtype) for a in thru),
        in_specs=[_HBM] * na + [_SEM] * 3 + [_ANY], out_specs=[_HBM] * na,
        input_output_aliases={i: i for i in range(na)},
        compiler_params=pltpu.CompilerParams(has_side_effects=pltpu.SideEffectType.DATAFLOW_SIDE_EFFECTING),
    )(*thru, *sems, after)
    return list(outs[na - n:])


def _adamw(gslabs, w, m, v, name):
    R, Cw = w.shape
    if R % 8 == 0:
        tr, tc = _tile(R, 64 if Cw > 1024 else 128, 8), Cw
    else:
        tr, tc = R, LANES
    c1 = 1.0 - ADAM_B1 ** ADAM_STEP
    c2 = 1.0 - ADAM_B2 ** ADAM_STEP

    def body(g_ref, w_ref, m_ref, v_ref, go_ref, d_ref, mo_ref, vo_ref):
        g = g_ref[0].astype(F32)
        for k in range(1, N_DEV):
            g = g + g_ref[k].astype(F32)
        mn = ADAM_B1 * m_ref[...] + (1.0 - ADAM_B1) * g
        vn = ADAM_B2 * v_ref[...] + (1.0 - ADAM_B2) * (g * g)
        m_hat = mn / c1
        v_hat = vn / c2
        go_ref[...] = g
        d_ref[...] = -ADAM_LR * (m_hat / (jnp.sqrt(v_hat) + ADAM_EPS) + ADAM_WD * w_ref[...])
        mo_ref[...] = mn
        vo_ref[...] = vn

    blk = pl.BlockSpec((tr, tc), lambda i, j: (i, j))
    return pl.pallas_call(
        body, grid=(R // tr, Cw // tc),
        in_specs=[pl.BlockSpec((N_DEV, tr, tc), lambda i, j: (0, i, j)), blk, blk, blk],
        out_specs=[blk] * 4, out_shape=[jax.ShapeDtypeStruct((R, Cw), F32)] * 4, name=name)(gslabs, w, m, v)


def _pack(arrs, row_mult, dtype=F32):
    parts = []
    total = 0
    for a in arrs:
        f = a.reshape(-1).astype(dtype)
        n = -(-f.shape[0] // 1024) * 1024
        parts.append(jnp.pad(f, (0, n - f.shape[0])))
        total += n
    rows = total // LANES
    rows_p = -(-rows // row_mult) * row_mult
    flat = jnp.concatenate(parts)
    flat = jnp.pad(flat, (0, rows_p * LANES - total))
    return flat.reshape(rows_p, LANES)


def _unpack(packed, shapes):
    lead = packed.shape[:-2]
    flat = packed.reshape(lead + (-1,))
    out = []
    off = 0
    for s in shapes:
        n = int(np.prod(s))
        out.append(flat[..., off:off + n].reshape(lead + tuple(s)))
        off += -(-n // 1024) * 1024
    return out


def _gather_cols(stacked):
    d, r, c = stacked.shape
    return stacked.transpose(1, 0, 2).reshape(r, d * c)


def _scatter_cols(full):
    r, n = full.shape
    return full.reshape(r, N_DEV, n // N_DEV).transpose(1, 0, 2)


def kernel(x, meta, norm1, w_in, gdn_conv_w, gdn_a_log, gdn_dt_bias, gdn_norm, w_out, norm2, w_ffn_up, ffn_conv_w, ffn_conv_b, w_ffn_down, norm_f, loss_target, m_meta, m_norm1, m_w_in, m_gdn_conv_w, m_gdn_a_log, m_gdn_dt_bias, m_gdn_norm, m_w_out, m_norm2, m_w_ffn_up, m_ffn_conv_w, m_ffn_conv_b, m_w_ffn_down, m_norm_f, v_meta, v_norm1, v_w_in, v_gdn_conv_w, v_gdn_a_log, v_gdn_dt_bias, v_gdn_norm, v_w_out, v_norm2, v_w_ffn_up, v_ffn_conv_w, v_ffn_conv_b, v_w_ffn_down, v_norm_f):
    S = x.shape[1]
    L = N_META + S
    pad = (-L) % CHUNK
    Lp = L + pad

    tr_ = lambda a: jnp.swapaxes(a[0], 0, 1)
    big = [tr_(w_in), w_out[0], tr_(w_ffn_up), w_ffn_down[0]]
    small = [meta, gdn_conv_w, ffn_conv_w]
    small_all, = _all_gather([_pack(small, 8)], "gather_small_weights")
    first, first_token = _split_start([big[0].astype(BF16)], "chip", "gather_w_in_start", small_all)
    late, late_token = _split_start([a.astype(BF16) for a in big[1:]], "gather", "gather_late_start", first_token)

    def first_weights(after):
        half = _split_wait(first, "gather_w_in_wait", after)
        second, second_token = _split_start(half, "forward", "gather_w_in_forward_start", after)
        w_in_s, = _split_wait(second, "gather_w_in_forward_wait", second_token)
        w_in_t = w_in_s.reshape(_O_END, D_MODEL)
        w_main_t = jnp.concatenate([w_in_t[_O_GQ:_O_GZ], w_in_t[_O_RQ:_O_RG], w_in_t[_O_GZ:_O_GA],
                                    w_in_t[_O_RG:_O_END]], axis=0)
        return {"w_main_t": w_main_t, "w_small_t": jnp.pad(w_in_t[_O_GA:_O_RQ], ((0, LANES - 2 * GDN_H), (0, 0)))}

    def late_weights(after):
        w_out_s, w_up_s, w_down_s = _split_wait(late, "gather_late_wait", after)
        return {"w_out": w_out_s.reshape(D_MODEL, D_MODEL), "w_up_t": w_up_s.reshape(2 * D_FF, D_MODEL),
                "w_down": w_down_s.reshape(D_FF, D_MODEL)}

    meta_s, gconv_s, fconv_s = _unpack(small_all, [a.shape for a in small])
    wt = {
        "norm1": norm1 + jnp.tile(late_token[0:1, :], (1, D_MODEL // LANES)),
        "gdn_conv_w": _gather_cols(gconv_s[:, 0]), "a_log": gdn_a_log[0], "dt_bias": gdn_dt_bias[0],
        "gdn_norm": gdn_norm, "norm2": norm2, "ffn_conv_w": _gather_cols(fconv_s[:, 0]), "ffn_conv_b": ffn_conv_b,
        "norm_f": norm_f.reshape(1, D_MODEL),
    }
    meta_f = _gather_cols(meta_s)

    pending = {}

    def on_ffn_out_grads(d_w_down, d_w_up_t, d_w_out):
        srcs = [d_w_out.reshape(N_DEV, D_MODEL // N_DEV, D_MODEL), d_w_up_t.reshape(N_DEV, 2 * D_FF // N_DEV, D_MODEL),
                d_w_down.reshape(N_DEV, D_FF // N_DEV, D_MODEL)]
        pending["ffn_out"], token = _split_start(srcs, "a2a", "exchange_ffn_out_start", d_w_out)
        return token

    def on_w_in_grads(d_w_in_t):
        slabs = d_w_in_t.astype(BF16).reshape(N_DEV, _O_END // N_DEV, D_MODEL)
        pending["w_in"], token = _split_start([slabs], "a2a", "exchange_w_in_start", d_w_in_t)
        return token

    hpad = jnp.concatenate([jnp.zeros((pad, D_MODEL), F32), meta_f, x[0]], axis=0)
    tgt = jnp.concatenate([jnp.zeros((pad + N_META, D_MODEL), F32), loss_target[0]], axis=0)
    lossvec, dh0, gr = _local_step(hpad, tgt, pad, wt, first_weights, late_weights, on_ffn_out_grads, on_w_in_grads)

    loss = lax.psum(jnp.sum(lossvec), ("x", "y", "c"))
    grad_x = dh0[pad + N_META:][None]

    big_m = [tr_(m_w_in), m_w_out[0], tr_(m_w_ffn_up), m_w_ffn_down[0]]
    big_v = [tr_(v_w_in), v_w_out[0], tr_(v_w_ffn_up), v_w_ffn_down[0]]
    slabs_ffn_out = _split_wait(pending["ffn_out"], "exchange_ffn_out_wait", dh0)
    big_out = [None] + [_adamw(slabs_ffn_out[i - 1], big[i], big_m[i], big_v[i], "adamw_big_%d" % i)
                        for i in range(1, len(big))]
    g_sm = [_scatter_cols(dh0[pad:pad + N_META]), _scatter_cols(gr["gdn_conv_w"]), _scatter_cols(gr["ffn_conv_w"])]
    g_small = jnp.stack([_pack([g[d] for g in g_sm], 8) for d in range(N_DEV)])
    slabs_small, = _all_to_all([g_small], "exchange_small_gradients")
    small_out = _adamw(slabs_small, _pack(small, 8), _pack([m_meta, m_gdn_conv_w, m_ffn_conv_w], 8),
                       _pack([v_meta, v_gdn_conv_w, v_ffn_conv_w], 8), "adamw_small_sharded")
    small_un = [_unpack(o, [a.shape for a in small]) for o in small_out]
    rep_w = [norm1, gdn_a_log, gdn_dt_bias, gdn_norm, norm2, ffn_conv_b, norm_f]
    rep_m = [m_norm1, m_gdn_a_log, m_gdn_dt_bias, m_gdn_norm, m_norm2, m_ffn_conv_b, m_norm_f]
    rep_v = [v_norm1, v_gdn_a_log, v_gdn_dt_bias, v_gdn_norm, v_norm2, v_ffn_conv_b, v_norm_f]
    rep_g = [gr["norm1"], gr["a_log"], gr["dt_bias"], gr["gdn_norm"], gr["norm2"], gr["ffn_conv_b"], gr["norm_f"]]
    rep_slabs, = _all_gather([_pack(rep_g, 8)], "gather_small_gradients")
    rep_out = _adamw(rep_slabs, _pack(rep_w, 8), _pack(rep_m, 8), _pack(rep_v, 8), "adamw_replicated")
    rep_shapes = [a.shape for a in rep_w]
    rp_g, rp_d, rp_nm, rp_nv = [_unpack(o, rep_shapes) for o in rep_out]

    slabs_w_in, = _split_wait(pending["w_in"], "exchange_w_in_wait", rep_out[0])
    big_out[0] = _adamw(slabs_w_in, big[0], big_m[0], big_v[0], "adamw_big_0")
    back = lambda a: jnp.swapaxes(a, 0, 1)[None]
    sh_g, sh_d, sh_nm, sh_nv = [
        [small_un[j][0], back(big_out[0][j]), small_un[j][1], big_out[1][j][None], back(big_out[2][j]),
         small_un[j][2], big_out[3][j][None]] for j in range(4)]

    def order(sh, rp):
        return [sh[0], rp[0], sh[1], sh[2], rp[1], rp[2], rp[3], sh[3], rp[4], sh[4], sh[5], rp[5], sh[6], rp[6]]

    return (loss, grad_x, *order(sh_g, rp_g), *order(sh_d, rp_d), *order(sh_nm, rp_nm), *order(sh_nv, rp_nv))
```

```python
import functools
import math

import numpy as np
import jax
import jax.numpy as jnp
from jax import lax
from jax.experimental import pallas as pl
from jax.experimental.pallas import tpu as pltpu

F32 = jnp.float32
BF16 = jnp.bfloat16
HI = lax.Precision.HIGHEST

D_MODEL = 1024
N_META = 16
CHUNK = 64
GDN_H = 8
GDN_D = 128
RET_H = 4
RET_D = 256
D_FF = 2816
GDN_CONV = 4
FFN_CONV = 3
ROPE_BASE = 10000.0
EPS = 1e-6
N_DEV = 8
LANES = 128
MAIN_W = 10 * 1024
_O_GQ, _O_GZ, _O_GA, _O_RQ, _O_RG, _O_GATE, _O_END = 0, 3072, 4096, 4112, 7184, 8208, 10256

ADAM_LR = 0.001
ADAM_B1 = 0.9
ADAM_B2 = 0.999
ADAM_EPS = 1e-08
ADAM_WD = 0.01
ADAM_STEP = 10

MESH_T = pl.DeviceIdType.MESH


def _tile(n, target, mult):
    best = None
    for d in range(mult, min(n, target) + 1, mult):
        if n % d == 0:
            best = d
    assert best is not None, (n, target, mult)
    return best


def _sig(x):
    return 1.0 / (1.0 + jnp.exp(-x))


def _d(a, b):
    return jnp.dot(a.astype(BF16), b.astype(BF16), preferred_element_type=F32)


def _dnt(a, b):
    return lax.dot_general(a.astype(BF16), b.astype(BF16), (((1,), (1,)), ((), ())), preferred_element_type=F32)


def _dtn(a, b):
    return lax.dot_general(a.astype(BF16), b.astype(BF16), (((0,), (0,)), ((), ())), preferred_element_type=F32)


def _dx(a, b):
    return jnp.dot(a, b, preferred_element_type=F32, precision=HI)


def _dxnt(a, b):
    return lax.dot_general(a, b, (((1,), (1,)), ((), ())), preferred_element_type=F32, precision=HI)


def _dxtn(a, b):
    return lax.dot_general(a, b, (((0,), (0,)), ((), ())), preferred_element_type=F32, precision=HI)


def _split(a):
    hi = a.astype(BF16)
    return hi, (a - hi.astype(F32)).astype(BF16)


def _d3g(a, b, dims):
    ah, al = _split(a)
    bh, bl = _split(b)
    f = functools.partial(lax.dot_general, dimension_numbers=dims, preferred_element_type=F32)
    return f(ah, bh) + (f(ah, bl) + f(al, bh))


_NN = (((1,), (0,)), ((), ()))
_NT = (((1,), (1,)), ((), ()))
_TN = (((0,), (0,)), ((), ()))


def _rowsum(x):
    return jnp.sum(x, axis=1, keepdims=True)


def _allsum(x):
    return jnp.sum(jnp.sum(x, axis=1, keepdims=True), axis=0, keepdims=True)


def _mm_nn(a, b, res=None, out_dtype=F32, bt=False, name="mm_nn"):
    M, K = a.shape
    N = b.shape[0] if bt else b.shape[1]
    tm = _tile(M, 704, 16)
    tn = _tile(N, 2816, 128)

    def body(*refs):
        if res is None:
            a_ref, b_ref, o_ref = refs
        else:
            a_ref, b_ref, r_ref, o_ref = refs
        acc = lax.dot_general(a_ref[...], b_ref[...], _NT if bt else _NN, preferred_element_type=F32)
        if res is not None:
            acc = acc + r_ref[...]
        o_ref[...] = acc.astype(out_dtype)

    b_spec = pl.BlockSpec((tn, K), lambda j, i: (j, 0)) if bt else pl.BlockSpec((K, tn), lambda j, i: (0, j))
    in_specs = [pl.BlockSpec((tm, K), lambda j, i: (i, 0)), b_spec]
    args = [a, b]
    if res is not None:
        in_specs.append(pl.BlockSpec((tm, tn), lambda j, i: (i, j)))
        args.append(res)
    return pl.pallas_call(
        body, grid=(N // tn, M // tm), in_specs=in_specs,
        out_specs=pl.BlockSpec((tm, tn), lambda j, i: (i, j)),
        out_shape=jax.ShapeDtypeStruct((M, N), out_dtype), name=name)(*args)


def _mm_nt(a, b, res=None, name="mm_nt"):
    M, Nc = a.shape
    K = b.shape[0]
    tm = _tile(M, 704, 16)
    tc = _tile(Nc, 5632, 128)

    def body(*refs):
        if res is None:
            a_ref, b_ref, o_ref = refs
        else:
            a_ref, b_ref, r_ref, o_ref = refs
        c = pl.program_id(1)
        p = lax.dot_general(a_ref[...], b_ref[...], (((1,), (1,)), ((), ())), preferred_element_type=F32)

        @pl.when(c == 0)
        def _():
            if res is None:
                o_ref[...] = p
            else:
                o_ref[...] = p + r_ref[...]

        @pl.when(c > 0)
        def _():
            o_ref[...] += p

    in_specs = [pl.BlockSpec((tm, tc), lambda i, c: (i, c)), pl.BlockSpec((K, tc), lambda i, c: (0, c))]
    args = [a, b]
    if res is not None:
        in_specs.append(pl.BlockSpec((tm, K), lambda i, c: (i, 0)))
        args.append(res)
    return pl.pallas_call(
        body, grid=(M // tm, Nc // tc), in_specs=in_specs,
        out_specs=pl.BlockSpec((tm, K), lambda i, c: (i, 0)),
        out_shape=jax.ShapeDtypeStruct((M, K), F32), name=name)(*args)


def _mm_tn(a, b, name="mm_tn"):
    M, K = a.shape
    N = b.shape[1]
    tm = _tile(M, 2752, 16)
    tk = _tile(K, 1408, 128)
    tn = _tile(N, 1408, 128)

    def body(a_ref, b_ref, o_ref):
        m = pl.program_id(2)
        p = lax.dot_general(a_ref[...], b_ref[...], (((0,), (0,)), ((), ())), preferred_element_type=F32)

        @pl.when(m == 0)
        def _():
            o_ref[...] = p

        @pl.when(m > 0)
        def _():
            o_ref[...] += p

    return pl.pallas_call(
        body, grid=(K // tk, N // tn, M // tm),
        in_specs=[pl.BlockSpec((tm, tk), lambda kk, j, m: (m, kk)), pl.BlockSpec((tm, tn), lambda kk, j, m: (m, j))],
        out_specs=pl.BlockSpec((tk, tn), lambda kk, j, m: (kk, j)),
        out_shape=jax.ShapeDtypeStruct((K, N), F32), name=name)(a, b)


def _rms_fwd(x, g, name):
    Lp = x.shape[0]
    tr = _tile(Lp, 256, 16)

    def body(x_ref, g_ref, o_ref):
        xv = x_ref[...]
        r = lax.rsqrt(jnp.mean(xv * xv, axis=-1, keepdims=True) + EPS)
        o_ref[...] = (xv * r * g_ref[...]).astype(BF16)

    return pl.pallas_call(
        body, grid=(Lp // tr,),
        in_specs=[pl.BlockSpec((tr, D_MODEL), lambda i: (i, 0)), pl.BlockSpec((1, D_MODEL), lambda i: (0, 0))],
        out_specs=pl.BlockSpec((tr, D_MODEL), lambda i: (i, 0)),
        out_shape=jax.ShapeDtypeStruct((Lp, D_MODEL), BF16), name=name)(x, g)


def _rms_bwd(x, g, dy, dres, pad, name):
    Lp = x.shape[0]
    tr = _tile(Lp, 256, 16)

    def body(x_ref, g_ref, dy_ref, dr_ref, dx_ref, dxb_ref, dg_ref):
        i = pl.program_id(0)
        xv = x_ref[...]
        r = lax.rsqrt(jnp.mean(xv * xv, axis=-1, keepdims=True) + EPS)
        xh = xv * r
        dyv = dy_ref[...]
        dxh = dyv * g_ref[...]
        dx = r * (dxh - xh * jnp.mean(dxh * xh, axis=-1, keepdims=True)) + dr_ref[...]
        row = i * tr + lax.broadcasted_iota(jnp.int32, (tr, 1), 0)
        dx = jnp.where(row >= pad, dx, 0.0)
        dx_ref[...] = dx
        dxb_ref[...] = dx.astype(BF16)
        part = jnp.sum(dyv * xh, axis=0, keepdims=True)

        @pl.when(i == 0)
        def _():
            dg_ref[...] = part

        @pl.when(i > 0)
        def _():
            dg_ref[...] += part

    blk = pl.BlockSpec((tr, D_MODEL), lambda i: (i, 0))
    vec = pl.BlockSpec((1, D_MODEL), lambda i: (0, 0))
    return pl.pallas_call(
        body, grid=(Lp // tr,), in_specs=[blk, vec, blk, blk], out_specs=[blk, blk, vec],
        out_shape=[jax.ShapeDtypeStruct((Lp, D_MODEL), F32), jax.ShapeDtypeStruct((Lp, D_MODEL), BF16),
                   jax.ShapeDtypeStruct((1, D_MODEL), F32)], name=name)(x, g, dy, dres)


def _final(h2, g, tgt, first_row):
    Lp = h2.shape[0]
    tr = _tile(Lp, 256, 16)

    def body(x_ref, g_ref, t_ref, loss_ref, dx_ref, dxb_ref, dg_ref):
        i = pl.program_id(0)
        xv = x_ref[...]
        gv = g_ref[...]
        r = lax.rsqrt(jnp.mean(xv * xv, axis=-1, keepdims=True) + EPS)
        xh = xv * r
        row = i * tr + lax.broadcasted_iota(jnp.int32, (tr, 1), 0)
        err = jnp.where(row >= first_row, xh * gv - t_ref[...], 0.0)
        lpart = jnp.sum(err * err, axis=0, keepdims=True) * (0.5 / D_MODEL)
        dyv = err * (1.0 / D_MODEL)
        dxh = dyv * gv
        dx = r * (dxh - xh * jnp.mean(dxh * xh, axis=-1, keepdims=True))
        dx_ref[...] = dx
        dxb_ref[...] = dx.astype(BF16)
        part = jnp.sum(dyv * xh, axis=0, keepdims=True)

        @pl.when(i == 0)
        def _():
            dg_ref[...] = part
            loss_ref[...] = lpart

        @pl.when(i > 0)
        def _():
            dg_ref[...] += part
            loss_ref[...] += lpart

    blk = pl.BlockSpec((tr, D_MODEL), lambda i: (i, 0))
    vec = pl.BlockSpec((1, D_MODEL), lambda i: (0, 0))
    return pl.pallas_call(
        body, grid=(Lp // tr,), in_specs=[blk, vec, blk], out_specs=[vec, blk, blk, vec],
        out_shape=[jax.ShapeDtypeStruct((1, D_MODEL), F32), jax.ShapeDtypeStruct((Lp, D_MODEL), F32),
                   jax.ShapeDtypeStruct((Lp, D_MODEL), BF16), jax.ShapeDtypeStruct((1, D_MODEL), F32)],
        name="final_norm_loss")(h2, g, tgt)


def _halo_prev(tr, width, col=0):
    return pl.BlockSpec((8, width), lambda i: (jnp.maximum(i * (tr // 8) - 1, 0), col))


def _halo_next(tr, width, nrows, col=0):
    last = nrows // 8 - 1
    return pl.BlockSpec((8, width), lambda i: (jnp.minimum((i + 1) * (tr // 8), last), col))


def _shifted(x, offs):
    n = x.shape[0]
    return [x if off == 0 else pltpu.roll(x, n - off, 0) for off in offs]


def _taps(wins, w, rows, bias=None):
    acc = w[0:1, :] * wins[0][0:rows, :]
    if bias is not None:
        acc = acc + bias
    for kk in range(1, len(wins)):
        acc = acc + w[kk:kk + 1, :] * wins[kk][0:rows, :]
    return acc


def _gdn_pre(proj_m, proj_s, conv_w, gparams, pad):
    Lp = proj_m.shape[0]
    tr = _tile(Lp, 192, 64)
    W3 = 3 * D_MODEL

    def body(main_ref, prev_ref, s_ref, w_ref, gp_ref, qkv_ref, gsm_ref, c_ref):
        i = pl.program_id(0)
        prev = jnp.where(i > 0, prev_ref[...], 0.0)
        ext = jnp.concatenate([prev, main_ref[...]], axis=0)
        c = _taps(_shifted(ext, range(8 - (GDN_CONV - 1), 9)), w_ref[...], tr)
        c_ref[...] = c
        s = c * _sig(c)
        scale = GDN_D ** -0.5
        for j in range(2 * GDN_H):
            seg = s[:, j * GDN_D:(j + 1) * GDN_D]
            r = lax.rsqrt(_rowsum(seg * seg) + EPS)
            if j < GDN_H:
                r = r * scale
            qkv_ref[:, j * GDN_D:(j + 1) * GDN_D] = seg * r
        qkv_ref[:, 2 * D_MODEL:] = s[:, 2 * D_MODEL:]
        sm = s_ref[...]
        gp = gp_ref[...]
        lane = lax.broadcasted_iota(jnp.int32, sm.shape, 1)
        z = sm + gp[1:2, :]
        softplus = jnp.maximum(z, 0.0) + jnp.log(1.0 + jnp.exp(-jnp.abs(z)))
        lg = -jnp.exp(gp[0:1, :]) * softplus
        row = i * tr + lax.broadcasted_iota(jnp.int32, (tr, 1), 0)
        out = jnp.where(lane < GDN_H, lg, jnp.where(lane < 2 * GDN_H, _sig(sm), 0.0))
        gsm_ref[...] = jnp.where(row >= pad, out, 0.0)

    return pl.pallas_call(
        body, grid=(Lp // tr,),
        in_specs=[pl.BlockSpec((tr, W3), lambda i: (i, 0)), _halo_prev(tr, W3),
                  pl.BlockSpec((tr, LANES), lambda i: (i, 0)),
                  pl.BlockSpec((GDN_CONV, W3), lambda i: (0, 0)), pl.BlockSpec((8, LANES), lambda i: (0, 0))],
        out_specs=[pl.BlockSpec((tr, W3), lambda i: (i, 0)), pl.BlockSpec((tr, LANES), lambda i: (i, 0)),
                   pl.BlockSpec((tr, W3), lambda i: (i, 0))],
        out_shape=[jax.ShapeDtypeStruct((Lp, W3), F32), jax.ShapeDtypeStruct((Lp, LANES), F32),
                   jax.ShapeDtypeStruct((Lp, W3), F32)],
        name="gdn_pre")(proj_m, proj_m, proj_s, conv_w, gparams)


def _gdn_pre_bwd(proj_m, conv_out, proj_s, conv_w, gparams, dq, dk, dv, dgs, pad):
    Lp = proj_m.shape[0]
    tr = _tile(Lp, 192, 64)
    W3 = 3 * D_MODEL
    te = tr + 8

    def body(main_ref, c_ref, cn_ref, s_ref, w_ref, gp_ref,
             dq_ref, dqn_ref, dk_ref, dkn_ref, dv_ref, dvn_ref, dgs_ref,
             da_ref, ds_ref, dw_ref, dgp_ref):
        i = pl.program_id(0)
        w = w_ref[...]
        c = jnp.concatenate([c_ref[...], cn_ref[...]], axis=0)
        sg = _sig(c)
        s = c * sg
        rowe = i * tr + lax.broadcasted_iota(jnp.int32, (te, 1), 0)
        live = (rowe >= pad) & (rowe < Lp)
        dqe = jnp.concatenate([dq_ref[...], dqn_ref[...]], axis=0)
        dke = jnp.concatenate([dk_ref[...], dkn_ref[...]], axis=0)
        dve = jnp.concatenate([dv_ref[...], dvn_ref[...]], axis=0)
        scale = GDN_D ** -0.5
        parts = []
        for j in range(2 * GDN_H):
            seg = s[:, j * GDN_D:(j + 1) * GDN_D]
            r = lax.rsqrt(_rowsum(seg * seg) + EPS)
            xh = seg * r
            if j < GDN_H:
                dxh = dqe[:, j * GDN_D:(j + 1) * GDN_D] * scale
            else:
                dxh = dke[:, (j - GDN_H) * GDN_D:(j - GDN_H + 1) * GDN_D]
            parts.append(r * (dxh - xh * _rowsum(dxh * xh)))
        parts.append(dve)
        dsv = jnp.concatenate(parts, axis=1)
        dc = jnp.where(live, dsv * (sg * (1.0 + c * (1.0 - sg))), 0.0)
        dcs = _shifted(dc, range(GDN_CONV - 1, -1, -1))
        da_ref[...] = _taps(dcs, w, tr).astype(BF16)
        pm = main_ref[...]
        rows = [jnp.sum(dcs[kk][0:tr, :] * pm, axis=0, keepdims=True) for kk in range(GDN_CONV)]
        dwp = jnp.concatenate(rows + [jnp.zeros((8 - GDN_CONV, W3), F32)], axis=0)

        sm = s_ref[...]
        gp = gp_ref[...]
        lane = lax.broadcasted_iota(jnp.int32, sm.shape, 1)
        rowm = i * tr + lax.broadcasted_iota(jnp.int32, (tr, 1), 0)
        dgv = jnp.where(rowm >= pad, dgs_ref[...], 0.0)
        dlg = jnp.where(lane < GDN_H, dgv, 0.0)
        dbt = jnp.where((lane >= GDN_H) & (lane < 2 * GDN_H), dgv, 0.0)
        z = sm + gp[1:2, :]
        softplus = jnp.maximum(z, 0.0) + jnp.log(1.0 + jnp.exp(-jnp.abs(z)))
        ea = jnp.exp(gp[0:1, :])
        dz = dlg * (-ea) * _sig(z)
        dal = dlg * (-ea) * softplus
        bt = _sig(sm)
        dgb = dbt * bt * (1.0 - bt)
        ds_ref[...] = (dz + dgb).astype(BF16)
        gpp = jnp.concatenate([jnp.sum(dal, axis=0, keepdims=True), jnp.sum(dz, axis=0, keepdims=True),
                               jnp.zeros((6, LANES), F32)], axis=0)

        @pl.when(i == 0)
        def _():
            dw_ref[...] = dwp
            dgp_ref[...] = gpp

        @pl.when(i > 0)
        def _():
            dw_ref[...] += dwp
            dgp_ref[...] += gpp

    m3 = pl.BlockSpec((tr, W3), lambda i: (i, 0))
    m1 = pl.BlockSpec((tr, D_MODEL), lambda i: (i, 0))
    n1 = _halo_next(tr, D_MODEL, Lp)
    return pl.pallas_call(
        body, grid=(Lp // tr,),
        in_specs=[m3, m3, _halo_next(tr, W3, Lp), pl.BlockSpec((tr, LANES), lambda i: (i, 0)),
                  pl.BlockSpec((GDN_CONV, W3), lambda i: (0, 0)), pl.BlockSpec((8, LANES), lambda i: (0, 0)),
                  m1, n1, m1, n1, m1, n1, pl.BlockSpec((tr, LANES), lambda i: (i, 0))],
        out_specs=[m3, pl.BlockSpec((tr, LANES), lambda i: (i, 0)),
                   pl.BlockSpec((8, W3), lambda i: (0, 0)), pl.BlockSpec((8, LANES), lambda i: (0, 0))],
        out_shape=[jax.ShapeDtypeStruct((Lp, W3), BF16), jax.ShapeDtypeStruct((Lp, LANES), BF16),
                   jax.ShapeDtypeStruct((8, W3), F32), jax.ShapeDtypeStruct((8, LANES), F32)],
        name="gdn_pre_bwd")(proj_m, conv_out, conv_out, proj_s, conv_w, gparams, dq, dq, dk, dk, dv, dv, dgs)


def _gdn_gates(gs):
    ri = lax.broadcasted_iota(jnp.int32, (CHUNK, CHUNK), 0)
    ci = lax.broadcasted_iota(jnp.int32, (CHUNK, CHUNK), 1)
    tril = ri >= ci
    strict = ri > ci
    gall = _dx(tril.astype(F32), gs)
    lane8 = lax.broadcasted_iota(jnp.int32, (8, LANES), 1)
    sub8 = lax.broadcasted_iota(jnp.int32, (8, LANES), 0)
    grow = _dxnt((lane8 == sub8).astype(F32), gall)
    return gall, grow, tril, strict


def _gdn_decay(gall, grow, tril, h):
    g = gall[:, h:h + 1]
    return g, jnp.where(tril, jnp.exp(jnp.where(tril, g - grow[h:h + 1, :], 0.0)), 0.0)


def _gdn_chunk_specs(N, rev):
    cn = (lambda n: N - 1 - n) if rev else (lambda n: n)
    col = lambda j: pl.BlockSpec((CHUNK, D_MODEL), lambda n: (cn(n), j))
    gate = pl.BlockSpec((CHUNK, LANES), lambda n: (cn(n), 0))
    st = lambda a, b: pl.BlockSpec((GDN_H, None, a, b), lambda n: (0, cn(n), 0, 0))
    return col, gate, st


def _gdn_chunk_fwd(qkv, gsm):
    Lp = qkv.shape[0]
    N = Lp // CHUNK

    def body(q_ref, k_ref, v_ref, gs_ref, o_ref, sin_ref, t_ref, S):
        n = pl.program_id(0)

        @pl.when(n == 0)
        def _():
            S[...] = jnp.zeros_like(S)

        gs = gs_ref[...]
        gall, grow, tril, strict = _gdn_gates(gs)
        ri = lax.broadcasted_iota(jnp.int32, (CHUNK, CHUNK), 0)
        ci = lax.broadcasted_iota(jnp.int32, (CHUNK, CHUNK), 1)
        eye = (ri == ci).astype(F32)
        heads = range(GDN_H)
        sls = [slice(h * GDN_D, (h + 1) * GDN_D) for h in heads]
        q = [q_ref[:, sl] for sl in sls]
        k = [k_ref[:, sl] for sl in sls]
        v = [v_ref[:, sl] for sl in sls]
        s0 = [S[h] for h in heads]
        beta = [gs[:, GDN_H + h:GDN_H + h + 1] for h in heads]
        gg = [_gdn_decay(gall, grow, tril, h) for h in heads]
        g = [x[0] for x in gg]
        gam = [x[1] for x in gg]
        eg = [jnp.exp(g[h]) for h in heads]
        gl = [g[h][CHUNK - 1:CHUNK, :] for h in heads]
        kb = [k[h] * beta[h] for h in heads]
        pw = [-jnp.where(strict, _dnt(kb[h], k[h]) * gam[h], 0.0) for h in heads]
        p = [_dnt(q[h], k[h]) * gam[h] for h in heads]
        qs = [_d(q[h] * eg[h], s0[h]) for h in heads]
        t = [eye + pw[h] for h in heads]
        for _ in range(5):
            pw = [_d3g(pw[h], pw[h], _NN) for h in heads]
            t = [t[h] + _d3g(t[h], pw[h], _NN) for h in heads]
        u = [_d(t[h], v[h] * beta[h]) for h in heads]
        w = [_d(t[h], kb[h] * eg[h]) for h in heads]
        vnew = [u[h] - _d(w[h], s0[h]) for h in heads]
        for h in heads:
            o_ref[:, sls[h]] = qs[h] + _d(p[h], vnew[h])
            sin_ref[h] = s0[h]
            t_ref[h] = t[h]
            S[h] = s0[h] * jnp.exp(gl[h]) + _dtn(k[h] * jnp.exp(gl[h] - g[h]), vnew[h])

    col, gate, st = _gdn_chunk_specs(N, False)
    return pl.pallas_call(
        body, grid=(N,),
        in_specs=[col(0), col(1), col(2), gate],
        out_specs=[col(0), st(GDN_D, GDN_D), st(CHUNK, CHUNK)],
        out_shape=[jax.ShapeDtypeStruct((Lp, D_MODEL), F32), jax.ShapeDtypeStruct((GDN_H, N, GDN_D, GDN_D), F32),
                   jax.ShapeDtypeStruct((GDN_H, N, CHUNK, CHUNK), F32)],
        scratch_shapes=[pltpu.VMEM((GDN_H, GDN_D, GDN_D), F32)],
        name="gdn_chunk_fwd")(qkv, qkv, qkv, gsm)


def _gdn_chunk_bwd(qkv, gsm, do, s_in, t_in):
    Lp = qkv.shape[0]
    N = Lp // CHUNK

    def body(q_ref, k_ref, v_ref, gs_ref, do_ref, sin_ref, t_ref, dq_ref, dk_ref, dv_ref, dgs_ref, dS):
        n = pl.program_id(0)

        @pl.when(n == 0)
        def _():
            dS[...] = jnp.zeros_like(dS)

        gs = gs_ref[...]
        gall, grow, tril, strict = _gdn_gates(gs)
        lane = lax.broadcasted_iota(jnp.int32, (CHUNK, LANES), 1)
        rcol = lax.broadcasted_iota(jnp.int32, (CHUNK, 1), 0)
        ones = jnp.ones((CHUNK, LANES), F32)
        dg_all = jnp.zeros((CHUNK, LANES), F32)
        dbeta_all = jnp.zeros((CHUNK, LANES), F32)
        heads = range(GDN_H)
        sls = [slice(h * GDN_D, (h + 1) * GDN_D) for h in heads]
        H = lambda f: [f(h) for h in heads]
        q = H(lambda h: q_ref[:, sls[h]])
        k = H(lambda h: k_ref[:, sls[h]])
        v = H(lambda h: v_ref[:, sls[h]])
        dov = H(lambda h: do_ref[:, sls[h]])
        s0 = H(lambda h: sin_ref[h])
        t = H(lambda h: t_ref[h])
        dsv = H(lambda h: dS[h])
        beta = H(lambda h: gs[:, GDN_H + h:GDN_H + h + 1])
        gg = H(lambda h: _gdn_decay(gall, grow, tril, h))
        g = [x[0] for x in gg]
        gam = [x[1] for x in gg]
        eg = H(lambda h: jnp.exp(g[h]))
        egl = H(lambda h: jnp.exp(g[h][CHUNK - 1:CHUNK, :]))
        e = H(lambda h: jnp.exp(g[h][CHUNK - 1:CHUNK, :] - g[h]))
        kb = H(lambda h: k[h] * beta[h])
        kbg = H(lambda h: kb[h] * eg[h])
        vb = H(lambda h: v[h] * beta[h])
        qg = H(lambda h: q[h] * eg[h])
        kd = H(lambda h: k[h] * e[h])
        m = H(lambda h: jnp.where(strict, _dnt(kb[h], k[h]) * gam[h], 0.0))
        u = H(lambda h: _d(t[h], vb[h]))
        w = H(lambda h: _d(t[h], kbg[h]))
        p = H(lambda h: _dnt(q[h], k[h]) * gam[h])
        dqg = H(lambda h: _dnt(dov[h], s0[h]))
        kdds = H(lambda h: _d(kd[h], dsv[h]))
        qgdo = H(lambda h: _dtn(qg[h], dov[h]))
        vnew = H(lambda h: u[h] - _d(w[h], s0[h]))
        dvnew = H(lambda h: _dtn(p[h], dov[h]) + kdds[h])
        dp = H(lambda h: jnp.where(tril, _dnt(dov[h], vnew[h]), 0.0))
        dkd = H(lambda h: _dnt(vnew[h], dsv[h]))
        dw = H(lambda h: -_dnt(dvnew[h], s0[h]))
        for h in heads:
            dS[h] = qgdo[h] + egl[h] * dsv[h] - _dtn(w[h], dvnew[h])
        dvb = H(lambda h: _dtn(t[h], dvnew[h]))
        dkbg = H(lambda h: _dtn(t[h], dw[h]))
        dt = H(lambda h: _dnt(dvnew[h], vb[h]) + _dnt(dw[h], kbg[h]))
        x1 = H(lambda h: _d3g(t[h], dt[h], _TN))
        dm = H(lambda h: jnp.where(strict, -_d3g(x1[h], t[h], _NT), 0.0))
        dkk = H(lambda h: dm[h] * gam[h])
        dqk = H(lambda h: dp[h] * gam[h])
        dkb = H(lambda h: _d(dkk[h], k[h]) + eg[h] * dkbg[h])
        em = H(lambda h: dm[h] * m[h] + dp[h] * p[h])
        colsum = H(lambda h: _d3g(em[h], ones, _TN)[:, 0:1])
        for h in heads:
            dk_ref[:, sls[h]] = _dtn(dkk[h], kb[h]) + _dtn(dqk[h], q[h]) + dkd[h] * e[h] + beta[h] * dkb[h]
            dq_ref[:, sls[h]] = _d(dqk[h], k[h]) + dqg[h] * eg[h]
            dv_ref[:, sls[h]] = beta[h] * dvb[h]
        for h in heads:
            dbeta = _rowsum(k[h] * dkb[h]) + _rowsum(v[h] * dvb[h])
            z = _rowsum(kd[h] * dkd[h])
            dg = _rowsum(em[h]) - colsum[h] + _rowsum(qg[h] * dqg[h]) + _rowsum(kbg[h] * dkbg[h]) - z
            extra = _allsum(z) + egl[h] * _allsum(s0[h] * dsv[h])
            dg = dg + jnp.where(rcol == CHUNK - 1, extra, 0.0)
            dg_all = dg_all + jnp.where(lane == h, dg, 0.0)
            dbeta_all = dbeta_all + jnp.where(lane == GDN_H + h, dbeta, 0.0)
        ri = lax.broadcasted_iota(jnp.int32, (CHUNK, CHUNK), 0)
        ci = lax.broadcasted_iota(jnp.int32, (CHUNK, CHUNK), 1)
        dgs_ref[...] = _dx((ci >= ri).astype(F32), dg_all) + dbeta_all

    col, gate, st = _gdn_chunk_specs(N, True)
    return pl.pallas_call(
        body, grid=(N,),
        in_specs=[col(0), col(1), col(2), gate, col(0), st(GDN_D, GDN_D), st(CHUNK, CHUNK)],
        out_specs=[col(0), col(0), col(0), gate],
        out_shape=[jax.ShapeDtypeStruct((Lp, D_MODEL), F32)] * 3 + [jax.ShapeDtypeStruct((Lp, LANES), F32)],
        scratch_shapes=[pltpu.VMEM((GDN_H, GDN_D, GDN_D), F32)],
        name="gdn_chunk_bwd")(qkv, qkv, qkv, gsm, do, s_in, t_in)


def _rot(x, c, s):
    half = RET_D // 2
    x1 = x[:, :half]
    x2 = x[:, half:]
    return jnp.concatenate([x1 * c - x2 * s, x2 * c + x1 * s], axis=1)


def _rot_bwd(d, c, s):
    half = RET_D // 2
    d1 = d[:, :half]
    d2 = d[:, half:]
    return jnp.concatenate([d1 * c + d2 * s, d2 * c - d1 * s], axis=1)


def _ret_tables():
    hh = jnp.arange(RET_H, dtype=F32)
    lg = jnp.log(1.0 - 2.0 ** (-5.0 - hh))
    idx = jnp.arange(CHUNK, dtype=F32)
    tril = jnp.asarray(np.tril(np.ones((CHUNK, CHUNK), dtype=bool)))
    dmask = jnp.where(tril, jnp.exp((idx[:, None] - idx[None, :]) * lg[:, None, None]), 0.0)
    qdec = jnp.exp((idx[None, :] + 1.0) * lg[:, None])
    kdec = jnp.exp((CHUNK - 1.0 - idx[None, :]) * lg[:, None])
    gch = jnp.exp(CHUNK * lg)
    qdec = jnp.broadcast_to(qdec[:, :, None], (RET_H, CHUNK, RET_D))
    kdec = jnp.broadcast_to(kdec[:, :, None], (RET_H, CHUNK, RET_D))
    gch = jnp.broadcast_to(gch[:, None, None], (RET_H, 8, LANES))
    return dmask, qdec, kdec, gch


def _ret_specs(N, rev):
    cn = (lambda n: N - 1 - n) if rev else (lambda n: n)
    col = lambda j: pl.BlockSpec((CHUNK, D_MODEL), lambda n: (cn(n), j))
    tab = lambda a, b: pl.BlockSpec((RET_H, a, b), lambda n: (0, 0, 0))
    rope = pl.BlockSpec((CHUNK, LANES), lambda n: (cn(n), 0))
    st = pl.BlockSpec((RET_H, None, RET_D, RET_D), lambda n: (0, cn(n), 0, 0))
    return col, tab, rope, st


def _ret_chunk_fwd(proj_m, cos, sin, tables):
    Lp = proj_m.shape[0]
    N = Lp // CHUNK
    dmask, qdec, kdec, gch = tables

    def body(q_ref, k_ref, v_ref, c_ref, s_ref, dm_ref, qd_ref, kd_ref, g_ref, o_ref, sin_ref, S):
        n = pl.program_id(0)

        @pl.when(n == 0)
        def _():
            S[...] = jnp.zeros_like(S)

        c = c_ref[...]
        s = s_ref[...]
        heads = range(RET_H)
        sls = [slice(h * RET_D, (h + 1) * RET_D) for h in heads]
        H = lambda f: [f(h) for h in heads]
        qr = H(lambda h: _rot(q_ref[:, sls[h]], c, s))
        ks = H(lambda h: _rot(k_ref[:, sls[h]], c, s) * (RET_D ** -0.5))
        v = H(lambda h: v_ref[:, sls[h]])
        s0 = H(lambda h: S[h])
        a = H(lambda h: _dnt(qr[h], ks[h]) * dm_ref[h])
        qs = H(lambda h: _d(qr[h] * qd_ref[h], s0[h]))
        kv = H(lambda h: _dtn(ks[h] * kd_ref[h], v[h]))
        for h in heads:
            o_ref[:, sls[h]] = _d(a[h], v[h]) + qs[h]
            sin_ref[h] = s0[h]
            S[h] = s0[h] * g_ref[h, 0:1, 0:1] + kv[h]

    col, tab, rope, st = _ret_specs(N, False)
    return pl.pallas_call(
        body, grid=(N,),
        in_specs=[col(3), col(4), col(5), rope, rope,
                  tab(CHUNK, CHUNK), tab(CHUNK, RET_D), tab(CHUNK, RET_D), tab(8, LANES)],
        out_specs=[col(0), st],
        out_shape=[jax.ShapeDtypeStruct((Lp, D_MODEL), F32), jax.ShapeDtypeStruct((RET_H, N, RET_D, RET_D), F32)],
        scratch_shapes=[pltpu.VMEM((RET_H, RET_D, RET_D), F32)],
        name="ret_chunk_fwd")(proj_m, proj_m, proj_m, cos, sin, dmask, qdec, kdec, gch)


def _ret_chunk_bwd(proj_m, cos, sin, tables, do, s_in):
    Lp = proj_m.shape[0]
    N = Lp // CHUNK
    dmask, qdec, kdec, gch = tables

    def body(q_ref, k_ref, v_ref, c_ref, s_ref, dm_ref, qd_ref, kd_ref, g_ref, do_ref, sin_ref,
             dq_ref, dk_ref, dv_ref, dS):
        n = pl.program_id(0)

        @pl.when(n == 0)
        def _():
            dS[...] = jnp.zeros_like(dS)

        c = c_ref[...]
        s = s_ref[...]
        kscale = RET_D ** -0.5
        heads = range(RET_H)
        sls = [slice(h * RET_D, (h + 1) * RET_D) for h in heads]
        H = lambda f: [f(h) for h in heads]
        qr = H(lambda h: _rot(q_ref[:, sls[h]], c, s))
        ks = H(lambda h: _rot(k_ref[:, sls[h]], c, s) * kscale)
        v = H(lambda h: v_ref[:, sls[h]])
        dov = H(lambda h: do_ref[:, sls[h]])
        s0 = H(lambda h: sin_ref[h])
        dsv = H(lambda h: dS[h])
        ad = H(lambda h: _dnt(qr[h], ks[h]) * dm_ref[h])
        da = H(lambda h: _dnt(dov[h], v[h]) * dm_ref[h])
        kds = H(lambda h: _d(ks[h] * kd_ref[h], dsv[h]))
        dos = H(lambda h: _dnt(dov[h], s0[h]) * qd_ref[h])
        vds = H(lambda h: _dnt(v[h], dsv[h]) * kd_ref[h])
        qdo = H(lambda h: _dtn(qr[h] * qd_ref[h], dov[h]))
        for h in heads:
            dS[h] = dsv[h] * g_ref[h, 0:1, 0:1] + qdo[h]
        for h in heads:
            dv_ref[:, sls[h]] = (_dtn(ad[h], dov[h]) + kds[h]).astype(BF16)
            dq_ref[:, sls[h]] = _rot_bwd(_d(da[h], ks[h]) + dos[h], c, s).astype(BF16)
            dk_ref[:, sls[h]] = _rot_bwd((_dtn(da[h], qr[h]) + vds[h]) * kscale, c, s).astype(BF16)

    col, tab, rope, st = _ret_specs(N, True)
    return pl.pallas_call(
        body, grid=(N,),
        in_specs=[col(3), col(4), col(5), rope, rope,
                  tab(CHUNK, CHUNK), tab(CHUNK, RET_D), tab(CHUNK, RET_D), tab(8, LANES), col(0), st],
        out_specs=[col(0), col(0), col(0)],
        out_shape=[jax.ShapeDtypeStruct((Lp, D_MODEL), BF16)] * 3,
        scratch_shapes=[pltpu.VMEM((RET_H, RET_D, RET_D), F32)],
        name="ret_chunk_bwd")(proj_m, proj_m, proj_m, cos, sin, dmask, qdec, kdec, gch, do, s_in)


def _merge_specs(tr):
    col = lambda j: pl.BlockSpec((tr, D_MODEL), lambda i: (i, j))
    return col


def _merge_fwd(o_a, o_b, proj_m, gnorm):
    Lp = o_a.shape[0]
    tr = _tile(Lp, 192, 16)

    def body(oa_ref, ob_ref, gz_ref, rg_ref, ga_ref, gb_ref, gn_ref, y_ref):
        gn = gn_ref[...]
        oa = oa_ref[...]
        ob = ob_ref[...]
        gz = gz_ref[...]
        ya = []
        for j in range(GDN_H):
            seg = oa[:, j * GDN_D:(j + 1) * GDN_D]
            r = lax.rsqrt(jnp.mean(seg * seg, axis=-1, keepdims=True) + EPS)
            ya.append(seg * r * gn)
        ya = jnp.concatenate(ya, axis=1) * (gz * _sig(gz))
        yb = []
        for j in range(RET_H):
            seg = ob[:, j * RET_D:(j + 1) * RET_D]
            r = lax.rsqrt(jnp.mean(seg * seg, axis=-1, keepdims=True) + EPS)
            yb.append(seg * r)
        rg = rg_ref[...]
        yb = jnp.concatenate(yb, axis=1) * (rg * _sig(rg))
        y_ref[...] = (_sig(ga_ref[...]) * ya + _sig(gb_ref[...]) * yb).astype(BF16)

    col = _merge_specs(tr)
    return pl.pallas_call(
        body, grid=(Lp // tr,),
        in_specs=[col(0), col(0), col(6), col(7), col(8), col(9), pl.BlockSpec((1, GDN_D), lambda i: (0, 0))],
        out_specs=col(0), out_shape=jax.ShapeDtypeStruct((Lp, D_MODEL), BF16),
        name="merge_fwd")(o_a, o_b, proj_m, proj_m, proj_m, proj_m, gnorm)


def _merge_bwd(dy, o_a, o_b, proj_m, gnorm):
    Lp = o_a.shape[0]
    tr = _tile(Lp, 192, 16)

    def body(dy_ref, oa_ref, ob_ref, gz_ref, rg_ref, ga_ref, gb_ref, gn_ref, dc_ref, doa_ref, dob_ref, dgn_ref):
        i = pl.program_id(0)
        gn = gn_ref[...]
        dyv = dy_ref[...]
        oa = oa_ref[...]
        ob = ob_ref[...]
        gz = gz_ref[...]
        rg = rg_ref[...]
        sa = _sig(ga_ref[...])
        sb = _sig(gb_ref[...])
        dya = dyv * sa
        dyb = dyv * sb
        sgz = _sig(gz)
        szz = gz * sgz
        dgn = jnp.zeros((1, GDN_D), F32)
        ya = []
        dgz = []
        for j in range(GDN_H):
            sl = slice(j * GDN_D, (j + 1) * GDN_D)
            seg = oa[:, sl]
            r = lax.rsqrt(jnp.mean(seg * seg, axis=-1, keepdims=True) + EPS)
            xh = seg * r
            oan = xh * gn
            ya.append(oan * szz[:, sl])
            dgz.append(dya[:, sl] * oan * (sgz[:, sl] * (1.0 + gz[:, sl] * (1.0 - sgz[:, sl]))))
            doan = dya[:, sl] * szz[:, sl]
            dgn = dgn + jnp.sum(doan * xh, axis=0, keepdims=True)
            dxh = doan * gn
            doa_ref[:, sl] = r * (dxh - xh * jnp.mean(dxh * xh, axis=-1, keepdims=True))
        ya = jnp.concatenate(ya, axis=1)
        srg = _sig(rg)
        srr = rg * srg
        yb = []
        drg = []
        for j in range(RET_H):
            sl = slice(j * RET_D, (j + 1) * RET_D)
            seg = ob[:, sl]
            r = lax.rsqrt(jnp.mean(seg * seg, axis=-1, keepdims=True) + EPS)
            xh = seg * r
            yb.append(xh * srr[:, sl])
            drg.append(dyb[:, sl] * xh * (srg[:, sl] * (1.0 + rg[:, sl] * (1.0 - srg[:, sl]))))
            dxh = dyb[:, sl] * srr[:, sl]
            dob_ref[:, sl] = r * (dxh - xh * jnp.mean(dxh * xh, axis=-1, keepdims=True))
        yb = jnp.concatenate(yb, axis=1)
        dc_ref[:, 0:D_MODEL] = jnp.concatenate(dgz, axis=1).astype(BF16)
        dc_ref[:, D_MODEL:2 * D_MODEL] = jnp.concatenate(drg, axis=1).astype(BF16)
        dc_ref[:, 2 * D_MODEL:3 * D_MODEL] = (dyv * ya * sa * (1.0 - sa)).astype(BF16)
        dc_ref[:, 3 * D_MODEL:] = (dyv * yb * sb * (1.0 - sb)).astype(BF16)

        @pl.when(i == 0)
        def _():
            dgn_ref[...] = dgn

        @pl.when(i > 0)
        def _():
            dgn_ref[...] += dgn

    col = _merge_specs(tr)
    return pl.pallas_call(
        body, grid=(Lp // tr,),
        in_specs=[col(0), col(0), col(0), col(6), col(7), col(8), col(9), pl.BlockSpec((1, GDN_D), lambda i: (0, 0))],
        out_specs=[pl.BlockSpec((tr, 4 * D_MODEL), lambda i: (i, 0)), col(0), col(0),
                   pl.BlockSpec((1, GDN_D), lambda i: (0, 0))],
        out_shape=[jax.ShapeDtypeStruct((Lp, 4 * D_MODEL), BF16), jax.ShapeDtypeStruct((Lp, D_MODEL), F32),
                   jax.ShapeDtypeStruct((Lp, D_MODEL), F32), jax.ShapeDtypeStruct((1, GDN_D), F32)],
        name="merge_bwd")(dy, o_a, o_b, proj_m, proj_m, proj_m, proj_m, gnorm)


def _ffn_act(up, conv_w, conv_b):
    Lp = up.shape[0]
    tr = _tile(Lp, 192, 16)
    W2 = 2 * D_FF

    def body(main_ref, prev_ref, w_ref, b_ref, act_ref, u_ref):
        i = pl.program_id(0)
        prev = jnp.where(i > 0, prev_ref[...], 0.0)
        ext = jnp.concatenate([prev, main_ref[...]], axis=0)
        u = _taps(_shifted(ext, range(8 - (FFN_CONV - 1), 9)), w_ref[...], tr, b_ref[...])
        a = u[:, :D_FF]
        act_ref[...] = (a * _sig(a) * u[:, D_FF:]).astype(BF16)
        u_ref[...] = u

    return pl.pallas_call(
        body, grid=(Lp // tr,),
        in_specs=[pl.BlockSpec((tr, W2), lambda i: (i, 0)), _halo_prev(tr, W2),
                  pl.BlockSpec((FFN_CONV, W2), lambda i: (0, 0)), pl.BlockSpec((1, W2), lambda i: (0, 0))],
        out_specs=[pl.BlockSpec((tr, D_FF), lambda i: (i, 0)), pl.BlockSpec((tr, W2), lambda i: (i, 0))],
        out_shape=[jax.ShapeDtypeStruct((Lp, D_FF), BF16), jax.ShapeDtypeStruct((Lp, W2), F32)],
        name="ffn_act")(up, up, conv_w, conv_b)


def _ffn_act_bwd(up, u, dact, conv_w):
    Lp = up.shape[0]
    tr = _tile(Lp, 96, 16)
    W2 = 2 * D_FF
    te = tr + 8

    def body(up_ref, u_ref, un_ref, da_ref, dan_ref, w_ref, dup_ref, acc_ref):
        i = pl.program_id(0)
        w = w_ref[...]
        ue = jnp.concatenate([u_ref[...], un_ref[...]], axis=0)
        a = ue[:, :D_FF]
        b = ue[:, D_FF:]
        rowe = i * tr + lax.broadcasted_iota(jnp.int32, (te, 1), 0)
        dae = jnp.where(rowe < Lp, jnp.concatenate([da_ref[...], dan_ref[...]], axis=0), 0.0)
        sg = _sig(a)
        du = jnp.concatenate([dae * b * (sg * (1.0 + a * (1.0 - sg))), dae * (a * sg)], axis=1)
        dus = _shifted(du, range(FFN_CONV - 1, -1, -1))
        dup_ref[...] = _taps(dus, w, tr).astype(BF16)
        upm = up_ref[...]
        rows = [jnp.sum(dus[kk][0:tr, :] * upm, axis=0, keepdims=True) for kk in range(FFN_CONV)]
        rows.append(jnp.sum(du[0:tr, :], axis=0, keepdims=True))
        part = jnp.concatenate(rows + [jnp.zeros((8 - len(rows), W2), F32)], axis=0)

        @pl.when(i == 0)
        def _():
            acc_ref[...] = part

        @pl.when(i > 0)
        def _():
            acc_ref[...] += part

    return pl.pallas_call(
        body, grid=(Lp // tr,),
        in_specs=[pl.BlockSpec((tr, W2), lambda i: (i, 0)), pl.BlockSpec((tr, W2), lambda i: (i, 0)),
                  _halo_next(tr, W2, Lp), pl.BlockSpec((tr, D_FF), lambda i: (i, 0)), _halo_next(tr, D_FF, Lp),
                  pl.BlockSpec((FFN_CONV, W2), lambda i: (0, 0))],
        out_specs=[pl.BlockSpec((tr, W2), lambda i: (i, 0)), pl.BlockSpec((8, W2), lambda i: (0, 0))],
        out_shape=[jax.ShapeDtypeStruct((Lp, W2), BF16), jax.ShapeDtypeStruct((8, W2), F32)],
        name="ffn_act_bwd")(up, u, u, dact, dact, conv_w)


def _local_step(hpad, tgt, pad, wt, first_weights=None, late_weights=None, on_ffn_out_grads=None,
                on_w_in_grads=None):
    Lp = hpad.shape[0]
    first = pad + N_META
    pos = jnp.arange(Lp, dtype=F32) - float(pad)
    half = RET_D // 2
    inv = 1.0 / (ROPE_BASE ** (jnp.arange(half, dtype=F32) / half))
    ang = pos[:, None] * inv[None, :]
    cos, sin = jnp.cos(ang), jnp.sin(ang)
    tables = _ret_tables()
    gparams = jnp.zeros((8, LANES), F32).at[0, :GDN_H].set(wt["a_log"]).at[1, :GDN_H].set(wt["dt_bias"])

    hn1 = _rms_fwd(hpad, wt["norm1"], "rms1_fwd")
    if first_weights is not None:
        wt = {**wt, **first_weights(hn1)}
    proj_m = _mm_nn(hn1, wt["w_main_t"], bt=True, name="proj_main")
    proj_s = _mm_nn(hn1, wt["w_small_t"], bt=True, name="proj_small")
    qkv, gsm, conv_out = _gdn_pre(proj_m, proj_s, wt["gdn_conv_w"], gparams, pad)
    o_a, s_a, t_a = _gdn_chunk_fwd(qkv, gsm)
    o_b, s_b = _ret_chunk_fwd(proj_m, cos, sin, tables)
    y = _merge_fwd(o_a, o_b, proj_m, wt["gdn_norm"])
    if late_weights is not None:
        wt = {**wt, **late_weights(y)}
    h1 = _mm_nn(y, wt["w_out"], res=hpad, name="out_proj")
    hn2 = _rms_fwd(h1, wt["norm2"], "rms2_fwd")
    up = _mm_nn(hn2, wt["w_up_t"], bt=True, name="ffn_up")
    act, u_ffn = _ffn_act(up, wt["ffn_conv_w"], wt["ffn_conv_b"])
    h2 = _mm_nn(act, wt["w_down"], res=h1, name="ffn_down")
    lossvec, dh2, dh2b, d_norm_f = _final(h2, wt["norm_f"], tgt, first)

    d_w_down = _mm_tn(act, dh2b, name="dw_down")
    dact = _mm_nt(dh2b, wt["w_down"], name="d_act")
    dup, ffn_rows = _ffn_act_bwd(up, u_ffn, dact, wt["ffn_conv_w"])
    d_w_up_t = _mm_tn(dup, hn2, name="dw_up")
    dhn2 = _mm_nn(dup, wt["w_up_t"], name="d_hn2")
    dh1, dh1b, d_norm2 = _rms_bwd(h1, wt["norm2"], dhn2, dh2, pad, "rms2_bwd")

    d_w_out = _mm_tn(y, dh1b, name="dw_out")
    dy = _mm_nt(dh1b, wt["w_out"], name="d_y")
    gnorm = wt["gdn_norm"]
    if on_ffn_out_grads is not None:
        gnorm = gnorm + on_ffn_out_grads(d_w_down, d_w_up_t, d_w_out)[0:1, :]
    d_c, do_a, do_b, d_gnorm = _merge_bwd(dy, o_a, o_b, proj_m, gnorm)
    drq, drk, drv = _ret_chunk_bwd(proj_m, cos, sin, tables, do_b, s_b)
    dq, dk, dv, dgs = _gdn_chunk_bwd(qkv, gsm, do_a, s_a, t_a)
    d_a, d_s, conv_rows, gp_rows = _gdn_pre_bwd(proj_m, conv_out, proj_s, wt["gdn_conv_w"], gparams, dq, dk, dv, dgs,
                                                pad)

    wmt = wt["w_main_t"]
    segs = [(d_a, 0, 3 * D_MODEL), (drq, 3 * D_MODEL, D_MODEL), (drk, 4 * D_MODEL, D_MODEL),
            (drv, 5 * D_MODEL, D_MODEL), (d_c, 6 * D_MODEL, 4 * D_MODEL)]
    pa, prq, prk, prv, pc = [_mm_tn(d, hn1, name="dw_in_%d" % i) for i, (d, _, _) in enumerate(segs)]
    ps = _mm_tn(d_s, hn1, name="dw_in_small")
    d_w_in_t = jnp.concatenate([pa, pc[:D_MODEL], ps[:2 * GDN_H], prq, prk, prv, pc[D_MODEL:]], axis=0)
    w_small_t = wt["w_small_t"]
    if on_w_in_grads is not None:
        w_small_t = w_small_t + on_w_in_grads(d_w_in_t)[0:1, 0:1].astype(w_small_t.dtype)
    dhn1 = _mm_nn(d_s, w_small_t, name="d_hn1_small")
    for i, (d, off, width) in enumerate(segs):
        dhn1 = _mm_nn(d, wmt[off:off + width], res=dhn1, name="d_hn1_%d" % i)
    dh0, _, d_norm1 = _rms_bwd(hpad, wt["norm1"], dhn1, dh1, pad, "rms1_bwd")

    grads = {
        "norm1": d_norm1, "w_in_t": d_w_in_t, "gdn_conv_w": conv_rows[:GDN_CONV],
        "a_log": gp_rows[0, :GDN_H], "dt_bias": gp_rows[1, :GDN_H], "gdn_norm": d_gnorm, "w_out": d_w_out,
        "norm2": d_norm2, "w_up_t": d_w_up_t, "ffn_conv_w": ffn_rows[:FFN_CONV],
        "ffn_conv_b": ffn_rows[FFN_CONV:FFN_CONV + 1], "w_down": d_w_down, "norm_f": d_norm_f,
    }
    return lossvec, dh0, grads


def _peer(k):
    ix, iy, ic = lax.axis_index("x"), lax.axis_index("y"), lax.axis_index("c")
    px = 1 - ix if (k >> 2) & 1 else ix
    py = 1 - iy if (k >> 1) & 1 else iy
    pc = 1 - ic if k & 1 else ic
    return (px, py, pc), 4 * px + 2 * py + pc


def _comm_call(body, n, out_shapes, name, args):
    hbm = pl.BlockSpec(memory_space=pl.ANY)
    return pl.pallas_call(
        body, out_shape=out_shapes, in_specs=[hbm] * n, out_specs=[hbm] * n,
        scratch_shapes=[pltpu.SemaphoreType.DMA((n, N_DEV - 1)), pltpu.SemaphoreType.DMA((n, N_DEV - 1)),
                        pltpu.SemaphoreType.DMA((n,))],
        name=name)(*args)


def _all_gather(xs, name):
    n = len(xs)

    def body(*refs):
        x_refs, out_refs = refs[:n], refs[n:2 * n]
        send_sems, recv_sems, local_sems = refs[2 * n:]
        _, me = _peer(0)
        pending = []
        for i in range(n):
            local = pltpu.make_async_copy(x_refs[i], out_refs[i].at[me], local_sems.at[i])
            local.start()
            pending.append(local)
        sends = []
        for i in range(n):
            for k in range(1, N_DEV):
                dev, _ = _peer(k)
                cp = pltpu.make_async_remote_copy(
                    src_ref=x_refs[i], dst_ref=out_refs[i].at[me], send_sem=send_sems.at[i, k - 1],
                    recv_sem=recv_sems.at[i, k - 1], device_id=dev, device_id_type=MESH_T)
                cp.start()
                sends.append(cp)
        for i in range(n):
            for k in range(1, N_DEV):
                dev, idx = _peer(k)
                pltpu.make_async_remote_copy(
                    src_ref=x_refs[i], dst_ref=out_refs[i].at[idx], send_sem=send_sems.at[i, k - 1],
                    recv_sem=recv_sems.at[i, k - 1], device_id=dev, device_id_type=MESH_T).wait_recv()
        for cp in sends:
            cp.wait_send()
        for local in pending:
            local.wait()

    out_shapes = [jax.ShapeDtypeStruct((N_DEV,) + a.shape, a.dtype) for a in xs]
    return _comm_call(body, n, out_shapes, name, xs)


def _all_to_all(gs, name):
    n = len(gs)

    def body(*refs):
        g_refs, out_refs = refs[:n], refs[n:2 * n]
        send_sems, recv_sems, local_sems = refs[2 * n:]
        _, me = _peer(0)
        pending = []
        for i in range(n):
            local = pltpu.make_async_copy(g_refs[i].at[me], out_refs[i].at[0], local_sems.at[i])
            local.start()
            pending.append(local)
        sends = []
        for i in range(n):
            for k in range(1, N_DEV):
                dev, idx = _peer(k)
                cp = pltpu.make_async_remote_copy(
                    src_ref=g_refs[i].at[idx], dst_ref=out_refs[i].at[k], send_sem=send_sems.at[i, k - 1],
                    recv_sem=recv_sems.at[i, k - 1], device_id=dev, device_id_type=MESH_T)
                cp.start()
                sends.append(cp)
        for cp in sends:
            cp.wait_recv()
        for cp in sends:
            cp.wait_send()
        for local in pending:
            local.wait()

    out_shapes = [jax.ShapeDtypeStruct(g.shape, g.dtype) for g in gs]
    return _comm_call(body, n, out_shapes, name, gs)


_SPLIT_RELATIONS = {"gather": tuple(range(1, N_DEV)), "a2a": tuple(range(1, N_DEV)), "chip": (1, 2, 4, 6),
                    "forward": (2, 4, 6)}


def _split_copies(kind, src_refs, land_refs, send_sems, recv_sems, local_sems, with_recv):
    n = len(land_refs)
    rels = _SPLIT_RELATIONS[kind]
    _, me = _peer(0)
    locals_, remotes = [], []
    for i in range(n):
        if kind in ("gather", "chip"):
            locals_.append(pltpu.make_async_copy(src_refs[i], land_refs[i].at[me], local_sems.at[i]))
        elif kind == "a2a":
            locals_.append(pltpu.make_async_copy(src_refs[i].at[me], land_refs[i].at[0], local_sems.at[i]))
        for jj, k in enumerate(rels):
            dev, idx = _peer(k)
            if kind in ("gather", "chip"):
                src, dst, mine = src_refs[i], land_refs[i].at[me], land_refs[i].at[idx]
            elif kind == "a2a":
                src, dst, mine = src_refs[i].at[idx], land_refs[i].at[k], land_refs[i].at[k]
            else:
                dev, _ = _peer(1)
                _, came = _peer(k + 1)
                src, dst, mine = land_refs[i].at[idx], land_refs[i].at[idx], land_refs[i].at[came]
            j = i * len(rels) + jj
            send = pltpu.make_async_remote_copy(
                src_ref=src, dst_ref=dst, send_sem=send_sems.at[j], recv_sem=recv_sems.at[j],
                device_id=dev, device_id_type=MESH_T)
            recv = pltpu.make_async_remote_copy(
                src_ref=src, dst_ref=mine, send_sem=send_sems.at[j], recv_sem=recv_sems.at[j],
                device_id=dev, device_id_type=MESH_T) if with_recv else None
            remotes.append((send, recv))
    return locals_, remotes


_HBM = pl.BlockSpec(memory_space=pltpu.HBM)
_SEM = pl.BlockSpec(memory_space=pltpu.SEMAPHORE)
_ANY = pl.BlockSpec(memory_space=pl.ANY)


def _split_start(srcs, kind, name, after):
    n = len(srcs)
    if kind == "forward":
        arrays = list(srcs)
    else:
        gathers = kind in ("gather", "chip")
        arrays = list(srcs) + [lax.empty(((N_DEV,) + a.shape) if gathers else a.shape, a.dtype) for a in srcs]
    na = len(arrays)

    def body(*refs):
        src_refs, land_refs = refs[:n], refs[na - n:na]
        send_sems, recv_sems, local_sems = refs[na + 1:na + 4]
        token = refs[-1]
        locals_, remotes = _split_copies(kind, src_refs, land_refs, send_sems, recv_sems, local_sems, False)
        for cp in locals_:
            cp.start()
        for send, _ in remotes:
            send.start()
        token[...] = jnp.zeros_like(token)

    ncp = n * len(_SPLIT_RELATIONS[kind])
    sems = (pltpu.SemaphoreType.DMA((ncp,)), pltpu.SemaphoreType.DMA((ncp,)), pltpu.SemaphoreType.DMA((n,)))
    thru = tuple(pltpu.HBM(a.shape, a.dtype) for a in arrays)
    outs = pl.pallas_call(
        body, name=name,
        out_shape=sems + thru + (jax.ShapeDtypeStruct((8, LANES), F32),),
        in_specs=[_HBM] * na + [_ANY],
        out_specs=[_SEM] * 3 + [_HBM] * na + [pl.BlockSpec(memory_space=pltpu.VMEM)],
        input_output_aliases={i: 3 + i for i in range(na)},
        compiler_params=pltpu.CompilerParams(has_side_effects=pltpu.SideEffectType.DATAFLOW_SIDE_EFFECTING),
    )(*[pltpu.with_memory_space_constraint(a, pltpu.HBM) for a in arrays], after)
    return (kind, n, outs[:3], outs[3:3 + na]), outs[-1]


def _split_wait(handle, name, after):
    kind, n, sems, thru = handle
    na = len(thru)

    def body(*refs):
        src_refs, land_refs = refs[:n], refs[na - n:na]
        send_sems, recv_sems, local_sems = refs[na:na + 3]
        locals_, remotes = _split_copies(kind, src_refs, land_refs, send_sems, recv_sems, local_sems, True)
        for send, recv in remotes:
            send.wait_send()
            recv.wait_recv()
        for cp in locals_:
            cp.wait()

    outs = pl.pallas_call(
        body, name=name, out_shape=tuple(pltpu.HBM(a.shape, a.dtype) for a in thru),
        in_specs=[_HBM] * na + [_SEM] * 3 + [_ANY], out_specs=[_HBM] * na,
        input_output_aliases={i: i for i in range(na)},
        compiler_params=pltpu.CompilerParams(has_side_effects=pltpu.SideEffectType.DATAFLOW_SIDE_EFFECTING),
    )(*thru, *sems, after)
    return list(outs[na - n:])


def _adamw(gslabs, w, m, v, name):
    R, Cw = w.shape
    if R % 8 == 0:
        tr, tc = _tile(R, 64 if Cw > 1024 else 128, 8), Cw
    else:
        tr, tc = R, LANES
    c1 = 1.0 - ADAM_B1 ** ADAM_STEP
    c2 = 1.0 - ADAM_B2 ** ADAM_STEP

    def body(g_ref, w_ref, m_ref, v_ref, go_ref, d_ref, mo_ref, vo_ref):
        g = g_ref[0].astype(F32)
        for k in range(1, N_DEV):
            g = g + g_ref[k].astype(F32)
        mn = ADAM_B1 * m_ref[...] + (1.0 - ADAM_B1) * g
        vn = ADAM_B2 * v_ref[...] + (1.0 - ADAM_B2) * (g * g)
        m_hat = mn / c1
        v_hat = vn / c2
        go_ref[...] = g
        d_ref[...] = -ADAM_LR * (m_hat / (jnp.sqrt(v_hat) + ADAM_EPS) + ADAM_WD * w_ref[...])
        mo_ref[...] = mn
        vo_ref[...] = vn

    blk = pl.BlockSpec((tr, tc), lambda i, j: (i, j))
    return pl.pallas_call(
        body, grid=(R // tr, Cw // tc),
        in_specs=[pl.BlockSpec((N_DEV, tr, tc), lambda i, j: (0, i, j)), blk, blk, blk],
        out_specs=[blk] * 4, out_shape=[jax.ShapeDtypeStruct((R, Cw), F32)] * 4, name=name)(gslabs, w, m, v)


def _pack(arrs, row_mult, dtype=F32):
    parts = []
    total = 0
    for a in arrs:
        f = a.reshape(-1).astype(dtype)
        n = -(-f.shape[0] // 1024) * 1024
        parts.append(jnp.pad(f, (0, n - f.shape[0])))
        total += n
    rows = total // LANES
    rows_p = -(-rows // row_mult) * row_mult
    flat = jnp.concatenate(parts)
    flat = jnp.pad(flat, (0, rows_p * LANES - total))
    return flat.reshape(rows_p, LANES)


def _unpack(packed, shapes):
    lead = packed.shape[:-2]
    flat = packed.reshape(lead + (-1,))
    out = []
    off = 0
    for s in shapes:
        n = int(np.prod(s))
        out.append(flat[..., off:off + n].reshape(lead + tuple(s)))
        off += -(-n // 1024) * 1024
    return out


def _gather_cols(stacked):
    d, r, c = stacked.shape
    return stacked.transpose(1, 0, 2).reshape(r, d * c)


def _scatter_cols(full):
    r, n = full.shape
    return full.reshape(r, N_DEV, n // N_DEV).transpose(1, 0, 2)


def kernel(x, meta, norm1, w_in, gdn_conv_w, gdn_a_log, gdn_dt_bias, gdn_norm, w_out, norm2, w_ffn_up, ffn_conv_w, ffn_conv_b, w_ffn_down, norm_f, loss_target, m_meta, m_norm1, m_w_in, m_gdn_conv_w, m_gdn_a_log, m_gdn_dt_bias, m_gdn_norm, m_w_out, m_norm2, m_w_ffn_up, m_ffn_conv_w, m_ffn_conv_b, m_w_ffn_down, m_norm_f, v_meta, v_norm1, v_w_in, v_gdn_conv_w, v_gdn_a_log, v_gdn_dt_bias, v_gdn_norm, v_w_out, v_norm2, v_w_ffn_up, v_ffn_conv_w, v_ffn_conv_b, v_w_ffn_down, v_norm_f):
    S = x.shape[1]
    L = N_META + S
    pad = (-L) % CHUNK
    Lp = L + pad

    tr_ = lambda a: jnp.swapaxes(a[0], 0, 1)
    big = [tr_(w_in), w_out[0], tr_(w_ffn_up), w_ffn_down[0]]
    small = [meta, gdn_conv_w, ffn_conv_w]
    small_all, = _all_gather([_pack(small, 8)], "gather_small_weights")
    first, first_token = _split_start([big[0].astype(BF16)], "chip", "gather_w_in_start", small_all)
    late, late_token = _split_start([a.astype(BF16) for a in big[1:]], "gather", "gather_late_start", first_token)

    def first_weights(after):
        half = _split_wait(first, "gather_w_in_wait", after)
        second, second_token = _split_start(half, "forward", "gather_w_in_forward_start", after)
        w_in_s, = _split_wait(second, "gather_w_in_forward_wait", second_token)
        w_in_t = w_in_s.reshape(_O_END, D_MODEL)
        w_main_t = jnp.concatenate([w_in_t[_O_GQ:_O_GZ], w_in_t[_O_RQ:_O_RG], w_in_t[_O_GZ:_O_GA],
                                    w_in_t[_O_RG:_O_END]], axis=0)
        return {"w_main_t": w_main_t, "w_small_t": jnp.pad(w_in_t[_O_GA:_O_RQ], ((0, LANES - 2 * GDN_H), (0, 0)))}

    def late_weights(after):
        w_out_s, w_up_s, w_down_s = _split_wait(late, "gather_late_wait", after)
        return {"w_out": w_out_s.reshape(D_MODEL, D_MODEL), "w_up_t": w_up_s.reshape(2 * D_FF, D_MODEL),
                "w_down": w_down_s.reshape(D_FF, D_MODEL)}

    meta_s, gconv_s, fconv_s = _unpack(small_all, [a.shape for a in small])
    wt = {
        "norm1": norm1 + jnp.tile(late_token[0:1, :], (1, D_MODEL // LANES)),
        "gdn_conv_w": _gather_cols(gconv_s[:, 0]), "a_log": gdn_a_log[0], "dt_bias": gdn_dt_bias[0],
        "gdn_norm": gdn_norm, "norm2": norm2, "ffn_conv_w": _gather_cols(fconv_s[:, 0]), "ffn_conv_b": ffn_conv_b,
        "norm_f": norm_f.reshape(1, D_MODEL),
    }
    meta_f = _gather_cols(meta_s)

    pending = {}

    def on_ffn_out_grads(d_w_down, d_w_up_t, d_w_out):
        srcs = [d_w_out.reshape(N_DEV, D_MODEL // N_DEV, D_MODEL), d_w_up_t.reshape(N_DEV, 2 * D_FF // N_DEV, D_MODEL),
                d_w_down.reshape(N_DEV, D_FF // N_DEV, D_MODEL)]
        pending["ffn_out"], token = _split_start(srcs, "a2a", "exchange_ffn_out_start", d_w_out)
        return token

    def on_w_in_grads(d_w_in_t):
        slabs = d_w_in_t.astype(BF16).reshape(N_DEV, _O_END // N_DEV, D_MODEL)
        pending["w_in"], token = _split_start([slabs], "a2a", "exchange_w_in_start", d_w_in_t)
        return token

    hpad = jnp.concatenate([jnp.zeros((pad, D_MODEL), F32), meta_f, x[0]], axis=0)
    tgt = jnp.concatenate([jnp.zeros((pad + N_META, D_MODEL), F32), loss_target[0]], axis=0)
    lossvec, dh0, gr = _local_step(hpad, tgt, pad, wt, first_weights, late_weights, on_ffn_out_grads, on_w_in_grads)

    loss = lax.psum(jnp.sum(lossvec), ("x", "y", "c"))
    grad_x = dh0[pad + N_META:][None]

    big_m = [tr_(m_w_in), m_w_out[0], tr_(m_w_ffn_up), m_w_ffn_down[0]]
    big_v = [tr_(v_w_in), v_w_out[0], tr_(v_w_ffn_up), v_w_ffn_down[0]]
    slabs_ffn_out = _split_wait(pending["ffn_out"], "exchange_ffn_out_wait", dh0)
    big_out = [None] + [_adamw(slabs_ffn_out[i - 1], big[i], big_m[i], big_v[i], "adamw_big_%d" % i)
                        for i in range(1, len(big))]
    g_sm = [_scatter_cols(dh0[pad:pad + N_META]), _scatter_cols(gr["gdn_conv_w"]), _scatter_cols(gr["ffn_conv_w"])]
    g_small = jnp.stack([_pack([g[d] for g in g_sm], 8) for d in range(N_DEV)])
    slabs_small, = _all_to_all([g_small], "exchange_small_gradients")
    small_out = _adamw(slabs_small, _pack(small, 8), _pack([m_meta, m_gdn_conv_w, m_ffn_conv_w], 8),
                       _pack([v_meta, v_gdn_conv_w, v_ffn_conv_w], 8), "adamw_small_sharded")
    small_un = [_unpack(o, [a.shape for a in small]) for o in small_out]
    rep_w = [norm1, gdn_a_log, gdn_dt_bias, gdn_norm, norm2, ffn_conv_b, norm_f]
    rep_m = [m_norm1, m_gdn_a_log, m_gdn_dt_bias, m_gdn_norm, m_norm2, m_ffn_conv_b, m_norm_f]
    rep_v = [v_norm1, v_gdn_a_log, v_gdn_dt_bias, v_gdn_norm, v_norm2, v_ffn_conv_b, v_norm_f]
    rep_g = [gr["norm1"], gr["a_log"], gr["dt_bias"], gr["gdn_norm"], gr["norm2"], gr["ffn_conv_b"], gr["norm_f"]]
    rep_slabs, = _all_gather([_pack(rep_g, 8)], "gather_small_gradients")
    rep_out = _adamw(rep_slabs, _pack(rep_w, 8), _pack(rep_m, 8), _pack(rep_v, 8), "adamw_replicated")
    rep_shapes = [a.shape for a in rep_w]
    rp_g, rp_d, rp_nm, rp_nv = [_unpack(o, rep_shapes) for o in rep_out]

    slabs_w_in, = _split_wait(pending["w_in"], "exchange_w_in_wait", rep_out[0])
    big_out[0] = _adamw(slabs_w_in, big[0], big_m[0], big_v[0], "adamw_big_0")
    back = lambda a: jnp.swapaxes(a, 0, 1)[None]
    sh_g, sh_d, sh_nm, sh_nv = [
        [small_un[j][0], back(big_out[0][j]), small_un[j][1], big_out[1][j][None], back(big_out[2][j]),
         small_un[j][2], big_out[3][j][None]] for j in range(4)]

    def order(sh, rp):
        return [sh[0], rp[0], sh[1], sh[2], rp[1], rp[2], rp[3], sh[3], rp[4], sh[4], sh[5], rp[5], sh[6], rp[6]]

    return (loss, grad_x, *order(sh_g, rp_g), *order(sh_d, rp_d), *order(sh_nm, rp_nm), *order(sh_nv, rp_nv))
```

```python
import functools
import math

import numpy as np
import jax
import jax.numpy as jnp
from jax import lax
from jax.experimental import pallas as pl
from jax.experimental.pallas import tpu as pltpu

F32 = jnp.float32
BF16 = jnp.bfloat16
HI = lax.Precision.HIGHEST

D_MODEL = 1024
N_META = 16
CHUNK = 64
GDN_H = 8
GDN_D = 128
RET_H = 4
RET_D = 256
D_FF = 2816
GDN_CONV = 4
FFN_CONV = 3
ROPE_BASE = 10000.0
EPS = 1e-6
N_DEV = 8
LANES = 128
MAIN_W = 10 * 1024
_O_GQ, _O_GZ, _O_GA, _O_RQ, _O_RG, _O_GATE, _O_END = 0, 3072, 4096, 4112, 7184, 8208, 10256

ADAM_LR = 0.001
ADAM_B1 = 0.9
ADAM_B2 = 0.999
ADAM_EPS = 1e-08
ADAM_WD = 0.01
ADAM_STEP = 10

MESH_T = pl.DeviceIdType.MESH


def _tile(n, target, mult):
    best = None
    for d in range(mult, min(n, target) + 1, mult):
        if n % d == 0:
            best = d
    assert best is not None, (n, target, mult)
    return best


def _sig(x):
    return 1.0 / (1.0 + jnp.exp(-x))


def _d(a, b):
    return jnp.dot(a.astype(BF16), b.astype(BF16), preferred_element_type=F32)


def _dnt(a, b):
    return lax.dot_general(a.astype(BF16), b.astype(BF16), (((1,), (1,)), ((), ())), preferred_element_type=F32)


def _dtn(a, b):
    return lax.dot_general(a.astype(BF16), b.astype(BF16), (((0,), (0,)), ((), ())), preferred_element_type=F32)


def _dx(a, b):
    return jnp.dot(a, b, preferred_element_type=F32, precision=HI)


def _dxnt(a, b):
    return lax.dot_general(a, b, (((1,), (1,)), ((), ())), preferred_element_type=F32, precision=HI)


def _dxtn(a, b):
    return lax.dot_general(a, b, (((0,), (0,)), ((), ())), preferred_element_type=F32, precision=HI)


def _split(a):
    hi = a.astype(BF16)
    return hi, (a - hi.astype(F32)).astype(BF16)


def _d3g(a, b, dims):
    ah, al = _split(a)
    bh, bl = _split(b)
    f = functools.partial(lax.dot_general, dimension_numbers=dims, preferred_element_type=F32)
    return f(ah, bh) + (f(ah, bl) + f(al, bh))


_NN = (((1,), (0,)), ((), ()))
_NT = (((1,), (1,)), ((), ()))
_TN = (((0,), (0,)), ((), ()))


def _rowsum(x):
    return jnp.sum(x, axis=1, keepdims=True)


def _allsum(x):
    return jnp.sum(jnp.sum(x, axis=1, keepdims=True), axis=0, keepdims=True)


def _mm_nn(a, b, res=None, out_dtype=F32, bt=False, name="mm_nn"):
    M, K = a.shape
    N = b.shape[0] if bt else b.shape[1]
    tm = _tile(M, 704, 16)
    tn = _tile(N, 2816, 128)

    def body(*refs):
        if res is None:
            a_ref, b_ref, o_ref = refs
        else:
            a_ref, b_ref, r_ref, o_ref = refs
        acc = lax.dot_general(a_ref[...], b_ref[...], _NT if bt else _NN, preferred_element_type=F32)
        if res is not None:
            acc = acc + r_ref[...]
        o_ref[...] = acc.astype(out_dtype)

    b_spec = pl.BlockSpec((tn, K), lambda j, i: (j, 0)) if bt else pl.BlockSpec((K, tn), lambda j, i: (0, j))
    in_specs = [pl.BlockSpec((tm, K), lambda j, i: (i, 0)), b_spec]
    args = [a, b]
    if res is not None:
        in_specs.append(pl.BlockSpec((tm, tn), lambda j, i: (i, j)))
        args.append(res)
    return pl.pallas_call(
        body, grid=(N // tn, M // tm), in_specs=in_specs,
        out_specs=pl.BlockSpec((tm, tn), lambda j, i: (i, j)),
        out_shape=jax.ShapeDtypeStruct((M, N), out_dtype), name=name)(*args)


def _mm_nt(a, b, res=None, name="mm_nt"):
    M, Nc = a.shape
    K = b.shape[0]
    tm = _tile(M, 704, 16)
    tc = _tile(Nc, 5632, 128)

    def body(*refs):
        if res is None:
            a_ref, b_ref, o_ref = refs
        else:
            a_ref, b_ref, r_ref, o_ref = refs
        c = pl.program_id(1)
        p = lax.dot_general(a_ref[...], b_ref[...], (((1,), (1,)), ((), ())), preferred_element_type=F32)

        @pl.when(c == 0)
        def _():
            if res is None:
                o_ref[...] = p
            else:
                o_ref[...] = p + r_ref[...]

        @pl.when(c > 0)
        def _():
            o_ref[...] += p

    in_specs = [pl.BlockSpec((tm, tc), lambda i, c: (i, c)), pl.BlockSpec((K, tc), lambda i, c: (0, c))]
    args = [a, b]
    if res is not None:
        in_specs.append(pl.BlockSpec((tm, K), lambda i, c: (i, 0)))
        args.append(res)
    return pl.pallas_call(
        body, grid=(M // tm, Nc // tc), in_specs=in_specs,
        out_specs=pl.BlockSpec((tm, K), lambda i, c: (i, 0)),
        out_shape=jax.ShapeDtypeStruct((M, K), F32), name=name)(*args)


def _mm_tn(a, b, name="mm_tn"):
    M, K = a.shape
    N = b.shape[1]
    tm = _tile(M, 2752, 16)
    tk = _tile(K, 1408, 128)
    tn = _tile(N, 1408, 128)

    def body(a_ref, b_ref, o_ref):
        m = pl.program_id(2)
        p = lax.dot_general(a_ref[...], b_ref[...], (((0,), (0,)), ((), ())), preferred_element_type=F32)

        @pl.when(m == 0)
        def _():
            o_ref[...] = p

        @pl.when(m > 0)
        def _():
            o_ref[...] += p

    return pl.pallas_call(
        body, grid=(K // tk, N // tn, M // tm),
        in_specs=[pl.BlockSpec((tm, tk), lambda kk, j, m: (m, kk)), pl.BlockSpec((tm, tn), lambda kk, j, m: (m, j))],
        out_specs=pl.BlockSpec((tk, tn), lambda kk, j, m: (kk, j)),
        out_shape=jax.ShapeDtypeStruct((K, N), F32), name=name)(a, b)


def _rms_fwd(x, g, name):
    Lp = x.shape[0]
    tr = _tile(Lp, 256, 16)

    def body(x_ref, g_ref, o_ref):
        xv = x_ref[...]
        r = lax.rsqrt(jnp.mean(xv * xv, axis=-1, keepdims=True) + EPS)
        o_ref[...] = (xv * r * g_ref[...]).astype(BF16)

    return pl.pallas_call(
        body, grid=(Lp // tr,),
        in_specs=[pl.BlockSpec((tr, D_MODEL), lambda i: (i, 0)), pl.BlockSpec((1, D_MODEL), lambda i: (0, 0))],
        out_specs=pl.BlockSpec((tr, D_MODEL), lambda i: (i, 0)),
        out_shape=jax.ShapeDtypeStruct((Lp, D_MODEL), BF16), name=name)(x, g)


class _Producer:
    def __init__(self, a, b, res=None):
        self.a, self.b, self.res = a, b, res
        self.tr = _tile(a.shape[0], 704, 16)
        K = a.shape[1]
        self.args = [a, b] + ([] if res is None else [res])
        self.specs = [pl.BlockSpec((self.tr, K), lambda i: (i, 0)),
                      pl.BlockSpec((K, D_MODEL), lambda i: (0, 0), pipeline_mode=pl.Buffered(1))]
        if res is not None:
            self.specs.append(pl.BlockSpec((self.tr, D_MODEL), lambda i: (i, 0)))

    def tile(self, refs):
        acc = jnp.dot(refs[0][...], refs[1][...], preferred_element_type=F32)
        return acc if self.res is None else acc + refs[2][...]


def _mm_rms_fwd(prod, g, name):
    Lp, tr, n = prod.a.shape[0], prod.tr, len(prod.args)

    def body(*refs):
        g_ref, x_ref, o_ref = refs[n:]
        xv = prod.tile(refs[:n])
        r = lax.rsqrt(jnp.mean(xv * xv, axis=-1, keepdims=True) + EPS)
        x_ref[...] = xv
        o_ref[...] = (xv * r * g_ref[...]).astype(BF16)

    blk = pl.BlockSpec((tr, D_MODEL), lambda i: (i, 0))
    return pl.pallas_call(
        body, grid=(Lp // tr,), in_specs=prod.specs + [pl.BlockSpec((1, D_MODEL), lambda i: (0, 0))],
        out_specs=[blk, blk],
        out_shape=[jax.ShapeDtypeStruct((Lp, D_MODEL), F32), jax.ShapeDtypeStruct((Lp, D_MODEL), BF16)],
        name=name)(*prod.args, g)


def _rms_bwd(x, g, dy, dres, pad, name):
    Lp = x.shape[0]
    fused = isinstance(dy, _Producer)
    tr = dy.tr if fused else _tile(Lp, 256, 16)
    n = len(dy.args) if fused else 1

    def body(*refs):
        x_ref, g_ref, dr_ref, dx_ref, dxb_ref, dg_ref = refs[n:]
        i = pl.program_id(0)
        xv = x_ref[...]
        r = lax.rsqrt(jnp.mean(xv * xv, axis=-1, keepdims=True) + EPS)
        xh = xv * r
        dyv = dy.tile(refs[:n]) if fused else refs[0][...]
        dxh = dyv * g_ref[...]
        dx = r * (dxh - xh * jnp.mean(dxh * xh, axis=-1, keepdims=True)) + dr_ref[...]
        row = i * tr + lax.broadcasted_iota(jnp.int32, (tr, 1), 0)
        dx = jnp.where(row >= pad, dx, 0.0)
        dx_ref[...] = dx
        dxb_ref[...] = dx.astype(BF16)
        part = jnp.sum(dyv * xh, axis=0, keepdims=True)

        @pl.when(i == 0)
        def _():
            dg_ref[...] = part

        @pl.when(i > 0)
        def _():
            dg_ref[...] += part

    blk = pl.BlockSpec((tr, D_MODEL), lambda i: (i, 0))
    vec = pl.BlockSpec((1, D_MODEL), lambda i: (0, 0))
    return pl.pallas_call(
        body, grid=(Lp // tr,), in_specs=(dy.specs if fused else [blk]) + [blk, vec, blk], out_specs=[blk, blk, vec],
        out_shape=[jax.ShapeDtypeStruct((Lp, D_MODEL), F32), jax.ShapeDtypeStruct((Lp, D_MODEL), BF16),
                   jax.ShapeDtypeStruct((1, D_MODEL), F32)], name=name)(*(dy.args if fused else [dy]), x, g, dres)


def _final(h2, g, tgt, first_row):
    fused = isinstance(h2, _Producer)
    Lp = h2.a.shape[0] if fused else h2.shape[0]
    tr = h2.tr if fused else _tile(Lp, 256, 16)
    n = len(h2.args) if fused else 1

    def body(*refs):
        g_ref, t_ref, loss_ref, dx_ref, dxb_ref, dg_ref = refs[n:]
        i = pl.program_id(0)
        xv = h2.tile(refs[:n]) if fused else refs[0][...]
        gv = g_ref[...]
        r = lax.rsqrt(jnp.mean(xv * xv, axis=-1, keepdims=True) + EPS)
        xh = xv * r
        row = i * tr + lax.broadcasted_iota(jnp.int32, (tr, 1), 0)
        err = jnp.where(row >= first_row, xh * gv - t_ref[...], 0.0)
        lpart = jnp.sum(err * err, axis=0, keepdims=True) * (0.5 / D_MODEL)
        dyv = err * (1.0 / D_MODEL)
        dxh = dyv * gv
        dx = r * (dxh - xh * jnp.mean(dxh * xh, axis=-1, keepdims=True))
        dx_ref[...] = dx
        dxb_ref[...] = dx.astype(BF16)
        part = jnp.sum(dyv * xh, axis=0, keepdims=True)

        @pl.when(i == 0)
        def _():
            dg_ref[...] = part
            loss_ref[...] = lpart

        @pl.when(i > 0)
        def _():
            dg_ref[...] += part
            loss_ref[...] += lpart

    blk = pl.BlockSpec((tr, D_MODEL), lambda i: (i, 0))
    vec = pl.BlockSpec((1, D_MODEL), lambda i: (0, 0))
    return pl.pallas_call(
        body, grid=(Lp // tr,), in_specs=(h2.specs if fused else [blk]) + [vec, blk], out_specs=[vec, blk, blk, vec],
        out_shape=[jax.ShapeDtypeStruct((1, D_MODEL), F32), jax.ShapeDtypeStruct((Lp, D_MODEL), F32),
                   jax.ShapeDtypeStruct((Lp, D_MODEL), BF16), jax.ShapeDtypeStruct((1, D_MODEL), F32)],
        name="final_norm_loss")(*(h2.args if fused else [h2]), g, tgt)


def _halo_prev(tr, width, col=0):
    return pl.BlockSpec((8, width), lambda i: (jnp.maximum(i * (tr // 8) - 1, 0), col))


def _halo_next(tr, width, nrows, col=0):
    last = nrows // 8 - 1
    return pl.BlockSpec((8, width), lambda i: (jnp.minimum((i + 1) * (tr // 8), last), col))


def _shifted(x, offs):
    n = x.shape[0]
    return [x if off == 0 else pltpu.roll(x, n - off, 0) for off in offs]


def _taps(wins, w, rows, bias=None):
    acc = w[0:1, :] * wins[0][0:rows, :]
    if bias is not None:
        acc = acc + bias
    for kk in range(1, len(wins)):
        acc = acc + w[kk:kk + 1, :] * wins[kk][0:rows, :]
    return acc


def _gdn_pre(proj_m, proj_s, conv_w, gparams, pad):
    Lp = proj_m.shape[0]
    tr = _tile(Lp, 192, 64)
    W3 = 3 * D_MODEL

    def body(main_ref, prev_ref, s_ref, w_ref, gp_ref, qkv_ref, gsm_ref, c_ref):
        i = pl.program_id(0)
        prev = jnp.where(i > 0, prev_ref[...], 0.0)
        ext = jnp.concatenate([prev, main_ref[...]], axis=0)
        c = _taps(_shifted(ext, range(8 - (GDN_CONV - 1), 9)), w_ref[...], tr)
        c_ref[...] = c
        s = c * _sig(c)
        scale = GDN_D ** -0.5
        for j in range(2 * GDN_H):
            seg = s[:, j * GDN_D:(j + 1) * GDN_D]
            r = lax.rsqrt(_rowsum(seg * seg) + EPS)
            if j < GDN_H:
                r = r * scale
            qkv_ref[:, j * GDN_D:(j + 1) * GDN_D] = seg * r
        qkv_ref[:, 2 * D_MODEL:] = s[:, 2 * D_MODEL:]
        sm = s_ref[...]
        gp = gp_ref[...]
        lane = lax.broadcasted_iota(jnp.int32, sm.shape, 1)
        z = sm + gp[1:2, :]
        softplus = jnp.maximum(z, 0.0) + jnp.log(1.0 + jnp.exp(-jnp.abs(z)))
        lg = -jnp.exp(gp[0:1, :]) * softplus
        row = i * tr + lax.broadcasted_iota(jnp.int32, (tr, 1), 0)
        out = jnp.where(lane < GDN_H, lg, jnp.where(lane < 2 * GDN_H, _sig(sm), 0.0))
        gsm_ref[...] = jnp.where(row >= pad, out, 0.0)

    return pl.pallas_call(
        body, grid=(Lp // tr,),
        in_specs=[pl.BlockSpec((tr, W3), lambda i: (i, 0)), _halo_prev(tr, W3),
                  pl.BlockSpec((tr, LANES), lambda i: (i, 0)),
                  pl.BlockSpec((GDN_CONV, W3), lambda i: (0, 0)), pl.BlockSpec((8, LANES), lambda i: (0, 0))],
        out_specs=[pl.BlockSpec((tr, W3), lambda i: (i, 0)), pl.BlockSpec((tr, LANES), lambda i: (i, 0)),
                   pl.BlockSpec((tr, W3), lambda i: (i, 0))],
        out_shape=[jax.ShapeDtypeStruct((Lp, W3), F32), jax.ShapeDtypeStruct((Lp, LANES), F32),
                   jax.ShapeDtypeStruct((Lp, W3), F32)],
        name="gdn_pre")(proj_m, proj_m, proj_s, conv_w, gparams)


def _gdn_pre_bwd(proj_m, conv_out, proj_s, conv_w, gparams, dq, dk, dv, dgs, pad):
    Lp = proj_m.shape[0]
    tr = _tile(Lp, 192, 64)
    W3 = 3 * D_MODEL
    te = tr + 8

    def body(main_ref, c_ref, cn_ref, s_ref, w_ref, gp_ref,
             dq_ref, dqn_ref, dk_ref, dkn_ref, dv_ref, dvn_ref, dgs_ref,
             da_ref, ds_ref, dw_ref, dgp_ref):
        i = pl.program_id(0)
        w = w_ref[...]
        c = jnp.concatenate([c_ref[...], cn_ref[...]], axis=0)
        sg = _sig(c)
        s = c * sg
        rowe = i * tr + lax.broadcasted_iota(jnp.int32, (te, 1), 0)
        live = (rowe >= pad) & (rowe < Lp)
        dqe = jnp.concatenate([dq_ref[...], dqn_ref[...]], axis=0)
        dke = jnp.concatenate([dk_ref[...], dkn_ref[...]], axis=0)
        dve = jnp.concatenate([dv_ref[...], dvn_ref[...]], axis=0)
        scale = GDN_D ** -0.5
        parts = []
        for j in range(2 * GDN_H):
            seg = s[:, j * GDN_D:(j + 1) * GDN_D]
            r = lax.rsqrt(_rowsum(seg * seg) + EPS)
            xh = seg * r
            if j < GDN_H:
                dxh = dqe[:, j * GDN_D:(j + 1) * GDN_D] * scale
            else:
                dxh = dke[:, (j - GDN_H) * GDN_D:(j - GDN_H + 1) * GDN_D]
            parts.append(r * (dxh - xh * _rowsum(dxh * xh)))
        parts.append(dve)
        dsv = jnp.concatenate(parts, axis=1)
        dc = jnp.where(live, dsv * (sg * (1.0 + c * (1.0 - sg))), 0.0)
        dcs = _shifted(dc, range(GDN_CONV - 1, -1, -1))
        da_ref[...] = _taps(dcs, w, tr).astype(BF16)
        pm = main_ref[...]
        rows = [jnp.sum(dcs[kk][0:tr, :] * pm, axis=0, keepdims=True) for kk in range(GDN_CONV)]
        dwp = jnp.concatenate(rows + [jnp.zeros((8 - GDN_CONV, W3), F32)], axis=0)

        sm = s_ref[...]
        gp = gp_ref[...]
        lane = lax.broadcasted_iota(jnp.int32, sm.shape, 1)
        rowm = i * tr + lax.broadcasted_iota(jnp.int32, (tr, 1), 0)
        dgv = jnp.where(rowm >= pad, dgs_ref[...], 0.0)
        dlg = jnp.where(lane < GDN_H, dgv, 0.0)
        dbt = jnp.where((lane >= GDN_H) & (lane < 2 * GDN_H), dgv, 0.0)
        z = sm + gp[1:2, :]
        softplus = jnp.maximum(z, 0.0) + jnp.log(1.0 + jnp.exp(-jnp.abs(z)))
        ea = jnp.exp(gp[0:1, :])
        dz = dlg * (-ea) * _sig(z)
        dal = dlg * (-ea) * softplus
        bt = _sig(sm)
        dgb = dbt * bt * (1.0 - bt)
        ds_ref[...] = (dz + dgb).astype(BF16)
        gpp = jnp.concatenate([jnp.sum(dal, axis=0, keepdims=True), jnp.sum(dz, axis=0, keepdims=True),
                               jnp.zeros((6, LANES), F32)], axis=0)

        @pl.when(i == 0)
        def _():
            dw_ref[...] = dwp
            dgp_ref[...] = gpp

        @pl.when(i > 0)
        def _():
            dw_ref[...] += dwp
            dgp_ref[...] += gpp

    m3 = pl.BlockSpec((tr, W3), lambda i: (i, 0))
    m1 = pl.BlockSpec((tr, D_MODEL), lambda i: (i, 0))
    n1 = _halo_next(tr, D_MODEL, Lp)
    return pl.pallas_call(
        body, grid=(Lp // tr,),
        in_specs=[m3, m3, _halo_next(tr, W3, Lp), pl.BlockSpec((tr, LANES), lambda i: (i, 0)),
                  pl.BlockSpec((GDN_CONV, W3), lambda i: (0, 0)), pl.BlockSpec((8, LANES), lambda i: (0, 0)),
                  m1, n1, m1, n1, m1, n1, pl.BlockSpec((tr, LANES), lambda i: (i, 0))],
        out_specs=[m3, pl.BlockSpec((tr, LANES), lambda i: (i, 0)),
                   pl.BlockSpec((8, W3), lambda i: (0, 0)), pl.BlockSpec((8, LANES), lambda i: (0, 0))],
        out_shape=[jax.ShapeDtypeStruct((Lp, W3), BF16), jax.ShapeDtypeStruct((Lp, LANES), BF16),
                   jax.ShapeDtypeStruct((8, W3), F32), jax.ShapeDtypeStruct((8, LANES), F32)],
        name="gdn_pre_bwd")(proj_m, conv_out, conv_out, proj_s, conv_w, gparams, dq, dq, dk, dk, dv, dv, dgs)


def _gdn_gates(gs):
    ri = lax.broadcasted_iota(jnp.int32, (CHUNK, CHUNK), 0)
    ci = lax.broadcasted_iota(jnp.int32, (CHUNK, CHUNK), 1)
    tril = ri >= ci
    strict = ri > ci
    gall = _dx(tril.astype(F32), gs)
    lane8 = lax.broadcasted_iota(jnp.int32, (8, LANES), 1)
    sub8 = lax.broadcasted_iota(jnp.int32, (8, LANES), 0)
    grow = _dxnt((lane8 == sub8).astype(F32), gall)
    return gall, grow, tril, strict


def _gdn_decay(gall, grow, tril, h):
    g = gall[:, h:h + 1]
    return g, jnp.where(tril, jnp.exp(jnp.where(tril, g - grow[h:h + 1, :], 0.0)), 0.0)


def _gdn_chunk_specs(N, rev):
    cn = (lambda n: N - 1 - n) if rev else (lambda n: n)
    col = lambda j: pl.BlockSpec((CHUNK, D_MODEL), lambda n: (cn(n), j))
    gate = pl.BlockSpec((CHUNK, LANES), lambda n: (cn(n), 0))
    st = lambda a, b: pl.BlockSpec((GDN_H, None, a, b), lambda n: (0, cn(n), 0, 0))
    return col, gate, st


def _gdn_chunk_fwd(qkv, gsm):
    Lp = qkv.shape[0]
    N = Lp // CHUNK

    def body(q_ref, k_ref, v_ref, gs_ref, o_ref, sin_ref, t_ref, S):
        n = pl.program_id(0)

        @pl.when(n == 0)
        def _():
            S[...] = jnp.zeros_like(S)

        gs = gs_ref[...]
        gall, grow, tril, strict = _gdn_gates(gs)
        ri = lax.broadcasted_iota(jnp.int32, (CHUNK, CHUNK), 0)
        ci = lax.broadcasted_iota(jnp.int32, (CHUNK, CHUNK), 1)
        eye = (ri == ci).astype(F32)
        heads = range(GDN_H)
        sls = [slice(h * GDN_D, (h + 1) * GDN_D) for h in heads]
        q = [q_ref[:, sl] for sl in sls]
        k = [k_ref[:, sl] for sl in sls]
        v = [v_ref[:, sl] for sl in sls]
        s0 = [S[h] for h in heads]
        beta = [gs[:, GDN_H + h:GDN_H + h + 1] for h in heads]
        gg = [_gdn_decay(gall, grow, tril, h) for h in heads]
        g = [x[0] for x in gg]
        gam = [x[1] for x in gg]
        eg = [jnp.exp(g[h]) for h in heads]
        gl = [g[h][CHUNK - 1:CHUNK, :] for h in heads]
        kb = [k[h] * beta[h] for h in heads]
        pw = [-jnp.where(strict, _dnt(kb[h], k[h]) * gam[h], 0.0) for h in heads]
        p = [_dnt(q[h], k[h]) * gam[h] for h in heads]
        qs = [_d(q[h] * eg[h], s0[h]) for h in heads]
        t = [eye + pw[h] for h in heads]
        for _ in range(5):
            pw = [_d3g(pw[h], pw[h], _NN) for h in heads]
            t = [t[h] + _d3g(t[h], pw[h], _NN) for h in heads]
        u = [_d(t[h], v[h] * beta[h]) for h in heads]
        w = [_d(t[h], kb[h] * eg[h]) for h in heads]
        vnew = [u[h] - _d(w[h], s0[h]) for h in heads]
        for h in heads:
            o_ref[:, sls[h]] = qs[h] + _d(p[h], vnew[h])
            sin_ref[h] = s0[h]
            t_ref[h] = t[h]
            S[h] = s0[h] * jnp.exp(gl[h]) + _dtn(k[h] * jnp.exp(gl[h] - g[h]), vnew[h])

    col, gate, st = _gdn_chunk_specs(N, False)
    return pl.pallas_call(
        body, grid=(N,),
        in_specs=[col(0), col(1), col(2), gate],
        out_specs=[col(0), st(GDN_D, GDN_D), st(CHUNK, CHUNK)],
        out_shape=[jax.ShapeDtypeStruct((Lp, D_MODEL), F32), jax.ShapeDtypeStruct((GDN_H, N, GDN_D, GDN_D), F32),
                   jax.ShapeDtypeStruct((GDN_H, N, CHUNK, CHUNK), F32)],
        scratch_shapes=[pltpu.VMEM((GDN_H, GDN_D, GDN_D), F32)],
        name="gdn_chunk_fwd")(qkv, qkv, qkv, gsm)


def _gdn_chunk_bwd(qkv, gsm, do, s_in, t_in):
    Lp = qkv.shape[0]
    N = Lp // CHUNK

    def body(q_ref, k_ref, v_ref, gs_ref, do_ref, sin_ref, t_ref, dq_ref, dk_ref, dv_ref, dgs_ref, dS):
        n = pl.program_id(0)

        @pl.when(n == 0)
        def _():
            dS[...] = jnp.zeros_like(dS)

        gs = gs_ref[...]
        gall, grow, tril, strict = _gdn_gates(gs)
        lane = lax.broadcasted_iota(jnp.int32, (CHUNK, LANES), 1)
        rcol = lax.broadcasted_iota(jnp.int32, (CHUNK, 1), 0)
        ones = jnp.ones((CHUNK, LANES), F32)
        dg_all = jnp.zeros((CHUNK, LANES), F32)
        dbeta_all = jnp.zeros((CHUNK, LANES), F32)
        heads = range(GDN_H)
        sls = [slice(h * GDN_D, (h + 1) * GDN_D) for h in heads]
        H = lambda f: [f(h) for h in heads]
        q = H(lambda h: q_ref[:, sls[h]])
        k = H(lambda h: k_ref[:, sls[h]])
        v = H(lambda h: v_ref[:, sls[h]])
        dov = H(lambda h: do_ref[:, sls[h]])
        s0 = H(lambda h: sin_ref[h])
        t = H(lambda h: t_ref[h])
        dsv = H(lambda h: dS[h])
        beta = H(lambda h: gs[:, GDN_H + h:GDN_H + h + 1])
        gg = H(lambda h: _gdn_decay(gall, grow, tril, h))
        g = [x[0] for x in gg]
        gam = [x[1] for x in gg]
        eg = H(lambda h: jnp.exp(g[h]))
        egl = H(lambda h: jnp.exp(g[h][CHUNK - 1:CHUNK, :]))
        e = H(lambda h: jnp.exp(g[h][CHUNK - 1:CHUNK, :] - g[h]))
        kb = H(lambda h: k[h] * beta[h])
        kbg = H(lambda h: kb[h] * eg[h])
        vb = H(lambda h: v[h] * beta[h])
        qg = H(lambda h: q[h] * eg[h])
        kd = H(lambda h: k[h] * e[h])
        m = H(lambda h: jnp.where(strict, _dnt(kb[h], k[h]) * gam[h], 0.0))
        u = H(lambda h: _d(t[h], vb[h]))
        w = H(lambda h: _d(t[h], kbg[h]))
        p = H(lambda h: _dnt(q[h], k[h]) * gam[h])
        dqg = H(lambda h: _dnt(dov[h], s0[h]))
        kdds = H(lambda h: _d(kd[h], dsv[h]))
        qgdo = H(lambda h: _dtn(qg[h], dov[h]))
        vnew = H(lambda h: u[h] - _d(w[h], s0[h]))
        dvnew = H(lambda h: _dtn(p[h], dov[h]) + kdds[h])
        dp = H(lambda h: jnp.where(tril, _dnt(dov[h], vnew[h]), 0.0))
        dkd = H(lambda h: _dnt(vnew[h], dsv[h]))
        dw = H(lambda h: -_dnt(dvnew[h], s0[h]))
        for h in heads:
            dS[h] = qgdo[h] + egl[h] * dsv[h] - _dtn(w[h], dvnew[h])
        dvb = H(lambda h: _dtn(t[h], dvnew[h]))
        dkbg = H(lambda h: _dtn(t[h], dw[h]))
        dt = H(lambda h: _dnt(dvnew[h], vb[h]) + _dnt(dw[h], kbg[h]))
        x1 = H(lambda h: _d3g(t[h], dt[h], _TN))
        dm = H(lambda h: jnp.where(strict, -_d3g(x1[h], t[h], _NT), 0.0))
        dkk = H(lambda h: dm[h] * gam[h])
        dqk = H(lambda h: dp[h] * gam[h])
        dkb = H(lambda h: _d(dkk[h], k[h]) + eg[h] * dkbg[h])
        em = H(lambda h: dm[h] * m[h] + dp[h] * p[h])
        colsum = H(lambda h: _d3g(em[h], ones, _TN)[:, 0:1])
        for h in heads:
            dk_ref[:, sls[h]] = _dtn(dkk[h], kb[h]) + _dtn(dqk[h], q[h]) + dkd[h] * e[h] + beta[h] * dkb[h]
            dq_ref[:, sls[h]] = _d(dqk[h], k[h]) + dqg[h] * eg[h]
            dv_ref[:, sls[h]] = beta[h] * dvb[h]
        for h in heads:
            dbeta = _rowsum(k[h] * dkb[h]) + _rowsum(v[h] * dvb[h])
            z = _rowsum(kd[h] * dkd[h])
            dg = _rowsum(em[h]) - colsum[h] + _rowsum(qg[h] * dqg[h]) + _rowsum(kbg[h] * dkbg[h]) - z
            extra = _allsum(z) + egl[h] * _allsum(s0[h] * dsv[h])
            dg = dg + jnp.where(rcol == CHUNK - 1, extra, 0.0)
            dg_all = dg_all + jnp.where(lane == h, dg, 0.0)
            dbeta_all = dbeta_all + jnp.where(lane == GDN_H + h, dbeta, 0.0)
        ri = lax.broadcasted_iota(jnp.int32, (CHUNK, CHUNK), 0)
        ci = lax.broadcasted_iota(jnp.int32, (CHUNK, CHUNK), 1)
        dgs_ref[...] = _dx((ci >= ri).astype(F32), dg_all) + dbeta_all

    col, gate, st = _gdn_chunk_specs(N, True)
    return pl.pallas_call(
        body, grid=(N,),
        in_specs=[col(0), col(1), col(2), gate, col(0), st(GDN_D, GDN_D), st(CHUNK, CHUNK)],
        out_specs=[col(0), col(0), col(0), gate],
        out_shape=[jax.ShapeDtypeStruct((Lp, D_MODEL), F32)] * 3 + [jax.ShapeDtypeStruct((Lp, LANES), F32)],
        scratch_shapes=[pltpu.VMEM((GDN_H, GDN_D, GDN_D), F32)],
        name="gdn_chunk_bwd")(qkv, qkv, qkv, gsm, do, s_in, t_in)


def _rot(x, c, s):
    half = RET_D // 2
    x1 = x[:, :half]
    x2 = x[:, half:]
    return jnp.concatenate([x1 * c - x2 * s, x2 * c + x1 * s], axis=1)


def _rot_bwd(d, c, s):
    half = RET_D // 2
    d1 = d[:, :half]
    d2 = d[:, half:]
    return jnp.concatenate([d1 * c + d2 * s, d2 * c - d1 * s], axis=1)


def _ret_tables():
    hh = jnp.arange(RET_H, dtype=F32)
    lg = jnp.log(1.0 - 2.0 ** (-5.0 - hh))
    idx = jnp.arange(CHUNK, dtype=F32)
    tril = jnp.asarray(np.tril(np.ones((CHUNK, CHUNK), dtype=bool)))
    dmask = jnp.where(tril, jnp.exp((idx[:, None] - idx[None, :]) * lg[:, None, None]), 0.0)
    qdec = jnp.exp((idx[None, :] + 1.0) * lg[:, None])
    kdec = jnp.exp((CHUNK - 1.0 - idx[None, :]) * lg[:, None])
    gch = jnp.exp(CHUNK * lg)
    qdec = jnp.broadcast_to(qdec[:, :, None], (RET_H, CHUNK, RET_D))
    kdec = jnp.broadcast_to(kdec[:, :, None], (RET_H, CHUNK, RET_D))
    gch = jnp.broadcast_to(gch[:, None, None], (RET_H, 8, LANES))
    return dmask, qdec, kdec, gch


def _ret_specs(N, rev):
    cn = (lambda n: N - 1 - n) if rev else (lambda n: n)
    col = lambda j: pl.BlockSpec((CHUNK, D_MODEL), lambda n: (cn(n), j))
    tab = lambda a, b: pl.BlockSpec((RET_H, a, b), lambda n: (0, 0, 0))
    rope = pl.BlockSpec((CHUNK, LANES), lambda n: (cn(n), 0))
    st = pl.BlockSpec((RET_H, None, RET_D, RET_D), lambda n: (0, cn(n), 0, 0))
    return col, tab, rope, st


def _ret_chunk_fwd(proj_m, cos, sin, tables):
    Lp = proj_m.shape[0]
    N = Lp // CHUNK
    dmask, qdec, kdec, gch = tables

    def body(q_ref, k_ref, v_ref, c_ref, s_ref, dm_ref, qd_ref, kd_ref, g_ref, o_ref, sin_ref, S):
        n = pl.program_id(0)

        @pl.when(n == 0)
        def _():
            S[...] = jnp.zeros_like(S)

        c = c_ref[...]
        s = s_ref[...]
        heads = range(RET_H)
        sls = [slice(h * RET_D, (h + 1) * RET_D) for h in heads]
        H = lambda f: [f(h) for h in heads]
        qr = H(lambda h: _rot(q_ref[:, sls[h]], c, s))
        ks = H(lambda h: _rot(k_ref[:, sls[h]], c, s) * (RET_D ** -0.5))
        v = H(lambda h: v_ref[:, sls[h]])
        s0 = H(lambda h: S[h])
        a = H(lambda h: _dnt(qr[h], ks[h]) * dm_ref[h])
        qs = H(lambda h: _d(qr[h] * qd_ref[h], s0[h]))
        kv = H(lambda h: _dtn(ks[h] * kd_ref[h], v[h]))
        for h in heads:
            o_ref[:, sls[h]] = _d(a[h], v[h]) + qs[h]
            sin_ref[h] = s0[h]
            S[h] = s0[h] * g_ref[h, 0:1, 0:1] + kv[h]

    col, tab, rope, st = _ret_specs(N, False)
    return pl.pallas_call(
        body, grid=(N,),
        in_specs=[col(3), col(4), col(5), rope, rope,
                  tab(CHUNK, CHUNK), tab(CHUNK, RET_D), tab(CHUNK, RET_D), tab(8, LANES)],
        out_specs=[col(0), st],
        out_shape=[jax.ShapeDtypeStruct((Lp, D_MODEL), F32), jax.ShapeDtypeStruct((RET_H, N, RET_D, RET_D), F32)],
        scratch_shapes=[pltpu.VMEM((RET_H, RET_D, RET_D), F32)],
        name="ret_chunk_fwd")(proj_m, proj_m, proj_m, cos, sin, dmask, qdec, kdec, gch)


def _ret_chunk_bwd(proj_m, cos, sin, tables, do, s_in):
    Lp = proj_m.shape[0]
    N = Lp // CHUNK
    dmask, qdec, kdec, gch = tables

    def body(q_ref, k_ref, v_ref, c_ref, s_ref, dm_ref, qd_ref, kd_ref, g_ref, do_ref, sin_ref,
             dq_ref, dk_ref, dv_ref, dS):
        n = pl.program_id(0)

        @pl.when(n == 0)
        def _():
            dS[...] = jnp.zeros_like(dS)

        c = c_ref[...]
        s = s_ref[...]
        kscale = RET_D ** -0.5
        heads = range(RET_H)
        sls = [slice(h * RET_D, (h + 1) * RET_D) for h in heads]
        H = lambda f: [f(h) for h in heads]
        qr = H(lambda h: _rot(q_ref[:, sls[h]], c, s))
        ks = H(lambda h: _rot(k_ref[:, sls[h]], c, s) * kscale)
        v = H(lambda h: v_ref[:, sls[h]])
        dov = H(lambda h: do_ref[:, sls[h]])
        s0 = H(lambda h: sin_ref[h])
        dsv = H(lambda h: dS[h])
        ad = H(lambda h: _dnt(qr[h], ks[h]) * dm_ref[h])
        da = H(lambda h: _dnt(dov[h], v[h]) * dm_ref[h])
        kds = H(lambda h: _d(ks[h] * kd_ref[h], dsv[h]))
        dos = H(lambda h: _dnt(dov[h], s0[h]) * qd_ref[h])
        vds = H(lambda h: _dnt(v[h], dsv[h]) * kd_ref[h])
        qdo = H(lambda h: _dtn(qr[h] * qd_ref[h], dov[h]))
        for h in heads:
            dS[h] = dsv[h] * g_ref[h, 0:1, 0:1] + qdo[h]
        for h in heads:
            dv_ref[:, sls[h]] = (_dtn(ad[h], dov[h]) + kds[h]).astype(BF16)
            dq_ref[:, sls[h]] = _rot_bwd(_d(da[h], ks[h]) + dos[h], c, s).astype(BF16)
            dk_ref[:, sls[h]] = _rot_bwd((_dtn(da[h], qr[h]) + vds[h]) * kscale, c, s).astype(BF16)

    col, tab, rope, st = _ret_specs(N, True)
    return pl.pallas_call(
        body, grid=(N,),
        in_specs=[col(3), col(4), col(5), rope, rope,
                  tab(CHUNK, CHUNK), tab(CHUNK, RET_D), tab(CHUNK, RET_D), tab(8, LANES), col(0), st],
        out_specs=[col(0), col(0), col(0)],
        out_shape=[jax.ShapeDtypeStruct((Lp, D_MODEL), BF16)] * 3,
        scratch_shapes=[pltpu.VMEM((RET_H, RET_D, RET_D), F32)],
        name="ret_chunk_bwd")(proj_m, proj_m, proj_m, cos, sin, dmask, qdec, kdec, gch, do, s_in)


def _merge_specs(tr):
    col = lambda j: pl.BlockSpec((tr, D_MODEL), lambda i: (i, j))
    return col


def _merge_fwd(o_a, o_b, proj_m, gnorm):
    Lp = o_a.shape[0]
    tr = _tile(Lp, 192, 16)

    def body(oa_ref, ob_ref, gz_ref, rg_ref, ga_ref, gb_ref, gn_ref, y_ref):
        gn = gn_ref[...]
        oa = oa_ref[...]
        ob = ob_ref[...]
        gz = gz_ref[...]
        ya = []
        for j in range(GDN_H):
            seg = oa[:, j * GDN_D:(j + 1) * GDN_D]
            r = lax.rsqrt(jnp.mean(seg * seg, axis=-1, keepdims=True) + EPS)
            ya.append(seg * r * gn)
        ya = jnp.concatenate(ya, axis=1) * (gz * _sig(gz))
        yb = []
        for j in range(RET_H):
            seg = ob[:, j * RET_D:(j + 1) * RET_D]
            r = lax.rsqrt(jnp.mean(seg * seg, axis=-1, keepdims=True) + EPS)
            yb.append(seg * r)
        rg = rg_ref[...]
        yb = jnp.concatenate(yb, axis=1) * (rg * _sig(rg))
        y_ref[...] = (_sig(ga_ref[...]) * ya + _sig(gb_ref[...]) * yb).astype(BF16)

    col = _merge_specs(tr)
    return pl.pallas_call(
        body, grid=(Lp // tr,),
        in_specs=[col(0), col(0), col(6), col(7), col(8), col(9), pl.BlockSpec((1, GDN_D), lambda i: (0, 0))],
        out_specs=col(0), out_shape=jax.ShapeDtypeStruct((Lp, D_MODEL), BF16),
        name="merge_fwd")(o_a, o_b, proj_m, proj_m, proj_m, proj_m, gnorm)


def _merge_bwd(dy, o_a, o_b, proj_m, gnorm):
    Lp = o_a.shape[0]
    tr = _tile(Lp, 192, 16)

    def body(dy_ref, oa_ref, ob_ref, gz_ref, rg_ref, ga_ref, gb_ref, gn_ref, dc_ref, doa_ref, dob_ref, dgn_ref):
        i = pl.program_id(0)
        gn = gn_ref[...]
        dyv = dy_ref[...]
        oa = oa_ref[...]
        ob = ob_ref[...]
        gz = gz_ref[...]
        rg = rg_ref[...]
        sa = _sig(ga_ref[...])
        sb = _sig(gb_ref[...])
        dya = dyv * sa
        dyb = dyv * sb
        sgz = _sig(gz)
        szz = gz * sgz
        dgn = jnp.zeros((1, GDN_D), F32)
        ya = []
        dgz = []
        for j in range(GDN_H):
            sl = slice(j * GDN_D, (j + 1) * GDN_D)
            seg = oa[:, sl]
            r = lax.rsqrt(jnp.mean(seg * seg, axis=-1, keepdims=True) + EPS)
            xh = seg * r
            oan = xh * gn
            ya.append(oan * szz[:, sl])
            dgz.append(dya[:, sl] * oan * (sgz[:, sl] * (1.0 + gz[:, sl] * (1.0 - sgz[:, sl]))))
            doan = dya[:, sl] * szz[:, sl]
            dgn = dgn + jnp.sum(doan * xh, axis=0, keepdims=True)
            dxh = doan * gn
            doa_ref[:, sl] = r * (dxh - xh * jnp.mean(dxh * xh, axis=-1, keepdims=True))
        ya = jnp.concatenate(ya, axis=1)
        srg = _sig(rg)
        srr = rg * srg
        yb = []
        drg = []
        for j in range(RET_H):
            sl = slice(j * RET_D, (j + 1) * RET_D)
            seg = ob[:, sl]
            r = lax.rsqrt(jnp.mean(seg * seg, axis=-1, keepdims=True) + EPS)
            xh = seg * r
            yb.append(xh * srr[:, sl])
            drg.append(dyb[:, sl] * xh * (srg[:, sl] * (1.0 + rg[:, sl] * (1.0 - srg[:, sl]))))
            dxh = dyb[:, sl] * srr[:, sl]
            dob_ref[:, sl] = r * (dxh - xh * jnp.mean(dxh * xh, axis=-1, keepdims=True))
        yb = jnp.concatenate(yb, axis=1)
        dc_ref[:, 0:D_MODEL] = jnp.concatenate(dgz, axis=1).astype(BF16)
        dc_ref[:, D_MODEL:2 * D_MODEL] = jnp.concatenate(drg, axis=1).astype(BF16)
        dc_ref[:, 2 * D_MODEL:3 * D_MODEL] = (dyv * ya * sa * (1.0 - sa)).astype(BF16)
        dc_ref[:, 3 * D_MODEL:] = (dyv * yb * sb * (1.0 - sb)).astype(BF16)

        @pl.when(i == 0)
        def _():
            dgn_ref[...] = dgn

        @pl.when(i > 0)
        def _():
            dgn_ref[...] += dgn

    col = _merge_specs(tr)
    return pl.pallas_call(
        body, grid=(Lp // tr,),
        in_specs=[col(0), col(0), col(0), col(6), col(7), col(8), col(9), pl.BlockSpec((1, GDN_D), lambda i: (0, 0))],
        out_specs=[pl.BlockSpec((tr, 4 * D_MODEL), lambda i: (i, 0)), col(0), col(0),
                   pl.BlockSpec((1, GDN_D), lambda i: (0, 0))],
        out_shape=[jax.ShapeDtypeStruct((Lp, 4 * D_MODEL), BF16), jax.ShapeDtypeStruct((Lp, D_MODEL), F32),
                   jax.ShapeDtypeStruct((Lp, D_MODEL), F32), jax.ShapeDtypeStruct((1, GDN_D), F32)],
        name="merge_bwd")(dy, o_a, o_b, proj_m, proj_m, proj_m, proj_m, gnorm)


def _ffn_act(up, conv_w, conv_b):
    Lp = up.shape[0]
    tr = _tile(Lp, 192, 16)
    W2 = 2 * D_FF

    def body(main_ref, prev_ref, w_ref, b_ref, act_ref, u_ref):
        i = pl.program_id(0)
        prev = jnp.where(i > 0, prev_ref[...], 0.0)
        ext = jnp.concatenate([prev, main_ref[...]], axis=0)
        u = _taps(_shifted(ext, range(8 - (FFN_CONV - 1), 9)), w_ref[...], tr, b_ref[...])
        a = u[:, :D_FF]
        act_ref[...] = (a * _sig(a) * u[:, D_FF:]).astype(BF16)
        u_ref[...] = u

    return pl.pallas_call(
        body, grid=(Lp // tr,),
        in_specs=[pl.BlockSpec((tr, W2), lambda i: (i, 0)), _halo_prev(tr, W2),
                  pl.BlockSpec((FFN_CONV, W2), lambda i: (0, 0)), pl.BlockSpec((1, W2), lambda i: (0, 0))],
        out_specs=[pl.BlockSpec((tr, D_FF), lambda i: (i, 0)), pl.BlockSpec((tr, W2), lambda i: (i, 0))],
        out_shape=[jax.ShapeDtypeStruct((Lp, D_FF), BF16), jax.ShapeDtypeStruct((Lp, W2), F32)],
        name="ffn_act")(up, up, conv_w, conv_b)


def _ffn_act_bwd(up, u, dact, conv_w):
    Lp = up.shape[0]
    tr = _tile(Lp, 96, 16)
    W2 = 2 * D_FF
    te = tr + 8

    def body(up_ref, u_ref, un_ref, da_ref, dan_ref, w_ref, dup_ref, acc_ref):
        i = pl.program_id(0)
        w = w_ref[...]
        ue = jnp.concatenate([u_ref[...], un_ref[...]], axis=0)
        a = ue[:, :D_FF]
        b = ue[:, D_FF:]
        rowe = i * tr + lax.broadcasted_iota(jnp.int32, (te, 1), 0)
        dae = jnp.where(rowe < Lp, jnp.concatenate([da_ref[...], dan_ref[...]], axis=0), 0.0)
        sg = _sig(a)
        du = jnp.concatenate([dae * b * (sg * (1.0 + a * (1.0 - sg))), dae * (a * sg)], axis=1)
        dus = _shifted(du, range(FFN_CONV - 1, -1, -1))
        dup_ref[...] = _taps(dus, w, tr).astype(BF16)
        upm = up_ref[...]
        rows = [jnp.sum(dus[kk][0:tr, :] * upm, axis=0, keepdims=True) for kk in range(FFN_CONV)]
        rows.append(jnp.sum(du[0:tr, :], axis=0, keepdims=True))
        part = jnp.concatenate(rows + [jnp.zeros((8 - len(rows), W2), F32)], axis=0)

        @pl.when(i == 0)
        def _():
            acc_ref[...] = part

        @pl.when(i > 0)
        def _():
            acc_ref[...] += part

    return pl.pallas_call(
        body, grid=(Lp // tr,),
        in_specs=[pl.BlockSpec((tr, W2), lambda i: (i, 0)), pl.BlockSpec((tr, W2), lambda i: (i, 0)),
                  _halo_next(tr, W2, Lp), pl.BlockSpec((tr, D_FF), lambda i: (i, 0)), _halo_next(tr, D_FF, Lp),
                  pl.BlockSpec((FFN_CONV, W2), lambda i: (0, 0))],
        out_specs=[pl.BlockSpec((tr, W2), lambda i: (i, 0)), pl.BlockSpec((8, W2), lambda i: (0, 0))],
        out_shape=[jax.ShapeDtypeStruct((Lp, W2), BF16), jax.ShapeDtypeStruct((8, W2), F32)],
        name="ffn_act_bwd")(up, u, u, dact, dact, conv_w)


def _local_step(hpad, tgt, pad, wt, first_weights=None, late_weights=None, on_ffn_out_grads=None,
                on_w_in_grads=None):
    Lp = hpad.shape[0]
    first = pad + N_META
    pos = jnp.arange(Lp, dtype=F32) - float(pad)
    half = RET_D // 2
    inv = 1.0 / (ROPE_BASE ** (jnp.arange(half, dtype=F32) / half))
    ang = pos[:, None] * inv[None, :]
    cos, sin = jnp.cos(ang), jnp.sin(ang)
    tables = _ret_tables()
    gparams = jnp.zeros((8, LANES), F32).at[0, :GDN_H].set(wt["a_log"]).at[1, :GDN_H].set(wt["dt_bias"])

    hn1 = _rms_fwd(hpad, wt["norm1"], "rms1_fwd")
    if first_weights is not None:
        wt = {**wt, **first_weights(hn1)}
    proj_m = _mm_nn(hn1, wt["w_main_t"], bt=True, name="proj_main")
    proj_s = _mm_nn(hn1, wt["w_small_t"], bt=True, name="proj_small")
    qkv, gsm, conv_out = _gdn_pre(proj_m, proj_s, wt["gdn_conv_w"], gparams, pad)
    o_a, s_a, t_a = _gdn_chunk_fwd(qkv, gsm)
    o_b, s_b = _ret_chunk_fwd(proj_m, cos, sin, tables)
    y = _merge_fwd(o_a, o_b, proj_m, wt["gdn_norm"])
    if late_weights is not None:
        wt = {**wt, **late_weights(y)}
    h1, hn2 = _mm_rms_fwd(_Producer(y, wt["w_out"], hpad), wt["norm2"], "out_proj_rms2")
    up = _mm_nn(hn2, wt["w_up_t"], bt=True, name="ffn_up")
    act, u_ffn = _ffn_act(up, wt["ffn_conv_w"], wt["ffn_conv_b"])
    lossvec, dh2, dh2b, d_norm_f = _final(_Producer(act, wt["w_down"], h1), wt["norm_f"], tgt, first)

    d_w_down = _mm_tn(act, dh2b, name="dw_down")
    dact = _mm_nt(dh2b, wt["w_down"], name="d_act")
    dup, ffn_rows = _ffn_act_bwd(up, u_ffn, dact, wt["ffn_conv_w"])
    d_w_up_t = _mm_tn(dup, hn2, name="dw_up")
    dh1, dh1b, d_norm2 = _rms_bwd(h1, wt["norm2"], _Producer(dup, wt["w_up_t"]), dh2, pad, "d_hn2_rms2_bwd")

    d_w_out = _mm_tn(y, dh1b, name="dw_out")
    dy = _mm_nt(dh1b, wt["w_out"], name="d_y")
    gnorm = wt["gdn_norm"]
    if on_ffn_out_grads is not None:
        gnorm = gnorm + on_ffn_out_grads(d_w_down, d_w_up_t, d_w_out)[0:1, :]
    d_c, do_a, do_b, d_gnorm = _merge_bwd(dy, o_a, o_b, proj_m, gnorm)
    drq, drk, drv = _ret_chunk_bwd(proj_m, cos, sin, tables, do_b, s_b)
    dq, dk, dv, dgs = _gdn_chunk_bwd(qkv, gsm, do_a, s_a, t_a)
    d_a, d_s, conv_rows, gp_rows = _gdn_pre_bwd(proj_m, conv_out, proj_s, wt["gdn_conv_w"], gparams, dq, dk, dv, dgs,
                                                pad)

    wmt = wt["w_main_t"]
    segs = [(d_a, 0, 3 * D_MODEL), (drq, 3 * D_MODEL, D_MODEL), (drk, 4 * D_MODEL, D_MODEL),
            (drv, 5 * D_MODEL, D_MODEL), (d_c, 6 * D_MODEL, 4 * D_MODEL)]
    pa, prq, prk, prv, pc = [_mm_tn(d, hn1, name="dw_in_%d" % i) for i, (d, _, _) in enumerate(segs)]
    ps = _mm_tn(d_s, hn1, name="dw_in_small")
    d_w_in_t = jnp.concatenate([pa, pc[:D_MODEL], ps[:2 * GDN_H], prq, prk, prv, pc[D_MODEL:]], axis=0)
    w_small_t = wt["w_small_t"]
    if on_w_in_grads is not None:
        w_small_t = w_small_t + on_w_in_grads(d_w_in_t)[0:1, 0:1].astype(w_small_t.dtype)
    dhn1 = _mm_nn(d_s, w_small_t, name="d_hn1_small")
    for i, (d, off, width) in enumerate(segs[:-1]):
        dhn1 = _mm_nn(d, wmt[off:off + width], res=dhn1, name="d_hn1_%d" % i)
    d, off, width = segs[-1]
    dh0, _, d_norm1 = _rms_bwd(hpad, wt["norm1"], _Producer(d, wmt[off:off + width], dhn1), dh1, pad,
                               "d_hn1_rms1_bwd")

    grads = {
        "norm1": d_norm1, "w_in_t": d_w_in_t, "gdn_conv_w": conv_rows[:GDN_CONV],
        "a_log": gp_rows[0, :GDN_H], "dt_bias": gp_rows[1, :GDN_H], "gdn_norm": d_gnorm, "w_out": d_w_out,
        "norm2": d_norm2, "w_up_t": d_w_up_t, "ffn_conv_w": ffn_rows[:FFN_CONV],
        "ffn_conv_b": ffn_rows[FFN_CONV:FFN_CONV + 1], "w_down": d_w_down, "norm_f": d_norm_f,
    }
    return lossvec, dh0, grads


def _peer(k):
    ix, iy, ic = lax.axis_index("x"), lax.axis_index("y"), lax.axis_index("c")
    px = 1 - ix if (k >> 2) & 1 else ix
    py = 1 - iy if (k >> 1) & 1 else iy
    pc = 1 - ic if k & 1 else ic
    return (px, py, pc), 4 * px + 2 * py + pc


def _comm_call(body, n, out_shapes, name, args):
    hbm = pl.BlockSpec(memory_space=pl.ANY)
    return pl.pallas_call(
        body, out_shape=out_shapes, in_specs=[hbm] * n, out_specs=[hbm] * n,
        scratch_shapes=[pltpu.SemaphoreType.DMA((n, N_DEV - 1)), pltpu.SemaphoreType.DMA((n, N_DEV - 1)),
                        pltpu.SemaphoreType.DMA((n,))],
        name=name)(*args)


def _all_gather(xs, name):
    n = len(xs)

    def body(*refs):
        x_refs, out_refs = refs[:n], refs[n:2 * n]
        send_sems, recv_sems, local_sems = refs[2 * n:]
        _, me = _peer(0)
        pending = []
        for i in range(n):
            local = pltpu.make_async_copy(x_refs[i], out_refs[i].at[me], local_sems.at[i])
            local.start()
            pending.append(local)
        sends = []
        for i in range(n):
            for k in range(1, N_DEV):
                dev, _ = _peer(k)
                cp = pltpu.make_async_remote_copy(
                    src_ref=x_refs[i], dst_ref=out_refs[i].at[me], send_sem=send_sems.at[i, k - 1],
                    recv_sem=recv_sems.at[i, k - 1], device_id=dev, device_id_type=MESH_T)
                cp.start()
                sends.append(cp)
        for i in range(n):
            for k in range(1, N_DEV):
                dev, idx = _peer(k)
                pltpu.make_async_remote_copy(
                    src_ref=x_refs[i], dst_ref=out_refs[i].at[idx], send_sem=send_sems.at[i, k - 1],
                    recv_sem=recv_sems.at[i, k - 1], device_id=dev, device_id_type=MESH_T).wait_recv()
        for cp in sends:
            cp.wait_send()
        for local in pending:
            local.wait()

    out_shapes = [jax.ShapeDtypeStruct((N_DEV,) + a.shape, a.dtype) for a in xs]
    return _comm_call(body, n, out_shapes, name, xs)


def _all_to_all(gs, name):
    n = len(gs)

    def body(*refs):
        g_refs, out_refs = refs[:n], refs[n:2 * n]
        send_sems, recv_sems, local_sems = refs[2 * n:]
        _, me = _peer(0)
        pending = []
        for i in range(n):
            local = pltpu.make_async_copy(g_refs[i].at[me], out_refs[i].at[0], local_sems.at[i])
            local.start()
            pending.append(local)
        sends = []
        for i in range(n):
            for k in range(1, N_DEV):
                dev, idx = _peer(k)
                cp = pltpu.make_async_remote_copy(
                    src_ref=g_refs[i].at[idx], dst_ref=out_refs[i].at[k], send_sem=send_sems.at[i, k - 1],
                    recv_sem=recv_sems.at[i, k - 1], device_id=dev, device_id_type=MESH_T)
                cp.start()
                sends.append(cp)
        for cp in sends:
            cp.wait_recv()
        for cp in sends:
            cp.wait_send()
        for local in pending:
            local.wait()

    out_shapes = [jax.ShapeDtypeStruct(g.shape, g.dtype) for g in gs]
    return _comm_call(body, n, out_shapes, name, gs)


_SPLIT_RELATIONS = {"gather": tuple(range(1, N_DEV)), "a2a": tuple(range(1, N_DEV)), "chip": (1, 2, 4, 6),
                    "forward": (2, 4, 6)}


def _split_copies(kind, src_refs, land_refs, send_sems, recv_sems, local_sems, with_recv):
    n = len(land_refs)
    rels = _SPLIT_RELATIONS[kind]
    _, me = _peer(0)
    locals_, remotes = [], []
    for i in range(n):
        if kind in ("gather", "chip"):
            locals_.append(pltpu.make_async_copy(src_refs[i], land_refs[i].at[me], local_sems.at[i]))
        elif kind == "a2a":
            locals_.append(pltpu.make_async_copy(src_refs[i].at[me], land_refs[i].at[0], local_sems.at[i]))
        for jj, k in enumerate(rels):
            dev, idx = _peer(k)
            if kind in ("gather", "chip"):
                src, dst, mine = src_refs[i], land_refs[i].at[me], land_refs[i].at[idx]
            elif kind == "a2a":
                src, dst, mine = src_refs[i].at[idx], land_refs[i].at[k], land_refs[i].at[k]
            else:
                dev, _ = _peer(1)
                _, came = _peer(k + 1)
                src, dst, mine = land_refs[i].at[idx], land_refs[i].at[idx], land_refs[i].at[came]
            j = i * len(rels) + jj
            send = pltpu.make_async_remote_copy(
                src_ref=src, dst_ref=dst, send_sem=send_sems.at[j], recv_sem=recv_sems.at[j],
                device_id=dev, device_id_type=MESH_T)
            recv = pltpu.make_async_remote_copy(
                src_ref=src, dst_ref=mine, send_sem=send_sems.at[j], recv_sem=recv_sems.at[j],
                device_id=dev, device_id_type=MESH_T) if with_recv else None
            remotes.append((send, recv))
    return locals_, remotes


_HBM = pl.BlockSpec(memory_space=pltpu.HBM)
_SEM = pl.BlockSpec(memory_space=pltpu.SEMAPHORE)
_ANY = pl.BlockSpec(memory_space=pl.ANY)


def _split_start(srcs, kind, name, after):
    n = len(srcs)
    if kind == "forward":
        arrays = list(srcs)
    else:
        gathers = kind in ("gather", "chip")
        arrays = list(srcs) + [lax.empty(((N_DEV,) + a.shape) if gathers else a.shape, a.dtype) for a in srcs]
    na = len(arrays)

    def body(*refs):
        src_refs, land_refs = refs[:n], refs[na - n:na]
        send_sems, recv_sems, local_sems = refs[na + 1:na + 4]
        token = refs[-1]
        locals_, remotes = _split_copies(kind, src_refs, land_refs, send_sems, recv_sems, local_sems, False)
        for cp in locals_:
            cp.start()
        for send, _ in remotes:
            send.start()
        token[...] = jnp.zeros_like(token)

    ncp = n * len(_SPLIT_RELATIONS[kind])
    sems = (pltpu.SemaphoreType.DMA((ncp,)), pltpu.SemaphoreType.DMA((ncp,)), pltpu.SemaphoreType.DMA((n,)))
    thru = tuple(pltpu.HBM(a.shape, a.dtype) for a in arrays)
    outs = pl.pallas_call(
        body, name=name,
        out_shape=sems + thru + (jax.ShapeDtypeStruct((8, LANES), F32),),
        in_specs=[_HBM] * na + [_ANY],
        out_specs=[_SEM] * 3 + [_HBM] * na + [pl.BlockSpec(memory_space=pltpu.VMEM)],
        input_output_aliases={i: 3 + i for i in range(na)},
        compiler_params=pltpu.CompilerParams(has_side_effects=pltpu.SideEffectType.DATAFLOW_SIDE_EFFECTING),
    )(*[pltpu.with_memory_space_constraint(a, pltpu.HBM) for a in arrays], after)
    return (kind, n, outs[:3], outs[3:3 + na]), outs[-1]


def _split_wait(handle, name, after):
    kind, n, sems, thru = handle
    na = len(thru)

    def body(*refs):
        src_refs, land_refs = refs[:n], refs[na - n:na]
        send_sems, recv_sems, local_sems = refs[na:na + 3]
        locals_, remotes = _split_copies(kind, src_refs, land_refs, send_sems, recv_sems, local_sems, True)
        for send, recv in remotes:
            send.wait_send()
            recv.wait_recv()
        for cp in locals_:
            cp.wait()

    outs = pl.pallas_call(
        body, name=name, out_shape=tuple(pltpu.HBM(a.shape, a.dtype) for a in thru),
        in_specs=[_HBM] * na + [_SEM] * 3 + [_ANY], out_specs=[_HBM] * na,
        input_output_aliases={i: i for i in range(na)},
        compiler_params=pltpu.CompilerParams(has_side_effects=pltpu.SideEffectType.DATAFLOW_SIDE_EFFECTING),
    )(*thru, *sems, after)
    return list(outs[na - n:])


def _adamw(gslabs, w, m, v, name):
    R, Cw = w.shape
    if R % 8 == 0:
        tr, tc = _tile(R, 64 if Cw > 1024 else 128, 8), Cw
    else:
        tr, tc = R, LANES
    c1 = 1.0 - ADAM_B1 ** ADAM_STEP
    c2 = 1.0 - ADAM_B2 ** ADAM_STEP

    def body(g_ref, w_ref, m_ref, v_ref, go_ref, d_ref, mo_ref, vo_ref):
        g = g_ref[0].astype(F32)
        for k in range(1, N_DEV):
            g = g + g_ref[k].astype(F32)
        mn = ADAM_B1 * m_ref[...] + (1.0 - ADAM_B1) * g
        vn = ADAM_B2 * v_ref[...] + (1.0 - ADAM_B2) * (g * g)
        m_hat = mn / c1
        v_hat = vn / c2
        go_ref[...] = g
        d_ref[...] = -ADAM_LR * (m_hat / (jnp.sqrt(v_hat) + ADAM_EPS) + ADAM_WD * w_ref[...])
        mo_ref[...] = mn
        vo_ref[...] = vn

    blk = pl.BlockSpec((tr, tc), lambda i, j: (i, j))
    return pl.pallas_call(
        body, grid=(R // tr, Cw // tc),
        in_specs=[pl.BlockSpec((N_DEV, tr, tc), lambda i, j: (0, i, j)), blk, blk, blk],
        out_specs=[blk] * 4, out_shape=[jax.ShapeDtypeStruct((R, Cw), F32)] * 4, name=name)(gslabs, w, m, v)


def _pack(arrs, row_mult, dtype=F32):
    parts = []
    total = 0
    for a in arrs:
        f = a.reshape(-1).astype(dtype)
        n = -(-f.shape[0] // 1024) * 1024
        parts.append(jnp.pad(f, (0, n - f.shape[0])))
        total += n
    rows = total // LANES
    rows_p = -(-rows // row_mult) * row_mult
    flat = jnp.concatenate(parts)
    flat = jnp.pad(flat, (0, rows_p * LANES - total))
    return flat.reshape(rows_p, LANES)


def _unpack(packed, shapes):
    lead = packed.shape[:-2]
    flat = packed.reshape(lead + (-1,))
    out = []
    off = 0
    for s in shapes:
        n = int(np.prod(s))
        out.append(flat[..., off:off + n].reshape(lead + tuple(s)))
        off += -(-n // 1024) * 1024
    return out


def _gather_cols(stacked):
    d, r, c = stacked.shape
    return stacked.transpose(1, 0, 2).reshape(r, d * c)


def _scatter_cols(full):
    r, n = full.shape
    return full.reshape(r, N_DEV, n // N_DEV).transpose(1, 0, 2)


def kernel(x, meta, norm1, w_in, gdn_conv_w, gdn_a_log, gdn_dt_bias, gdn_norm, w_out, norm2, w_ffn_up, ffn_conv_w, ffn_conv_b, w_ffn_down, norm_f, loss_target, m_meta, m_norm1, m_w_in, m_gdn_conv_w, m_gdn_a_log, m_gdn_dt_bias, m_gdn_norm, m_w_out, m_norm2, m_w_ffn_up, m_ffn_conv_w, m_ffn_conv_b, m_w_ffn_down, m_norm_f, v_meta, v_norm1, v_w_in, v_gdn_conv_w, v_gdn_a_log, v_gdn_dt_bias, v_gdn_norm, v_w_out, v_norm2, v_w_ffn_up, v_ffn_conv_w, v_ffn_conv_b, v_w_ffn_down, v_norm_f):
    S = x.shape[1]
    L = N_META + S
    pad = (-L) % CHUNK
    Lp = L + pad

    tr_ = lambda a: jnp.swapaxes(a[0], 0, 1)
    big = [tr_(w_in), w_out[0], tr_(w_ffn_up), w_ffn_down[0]]
    small = [meta, gdn_conv_w, ffn_conv_w]
    small_all, = _all_gather([_pack(small, 8)], "gather_small_weights")
    first, first_token = _split_start([big[0].astype(BF16)], "chip", "gather_w_in_start", small_all)
    late, late_token = _split_start([a.astype(BF16) for a in big[1:]], "gather", "gather_late_start", first_token)

    def first_weights(after):
        half = _split_wait(first, "gather_w_in_wait", after)
        second, second_token = _split_start(half, "forward", "gather_w_in_forward_start", after)
        w_in_s, = _split_wait(second, "gather_w_in_forward_wait", second_token)
        w_in_t = w_in_s.reshape(_O_END, D_MODEL)
        w_main_t = jnp.concatenate([w_in_t[_O_GQ:_O_GZ], w_in_t[_O_RQ:_O_RG], w_in_t[_O_GZ:_O_GA],
                                    w_in_t[_O_RG:_O_END]], axis=0)
        return {"w_main_t": w_main_t, "w_small_t": jnp.pad(w_in_t[_O_GA:_O_RQ], ((0, LANES - 2 * GDN_H), (0, 0)))}

    def late_weights(after):
        w_out_s, w_up_s, w_down_s = _split_wait(late, "gather_late_wait", after)
        return {"w_out": w_out_s.reshape(D_MODEL, D_MODEL), "w_up_t": w_up_s.reshape(2 * D_FF, D_MODEL),
                "w_down": w_down_s.reshape(D_FF, D_MODEL)}

    meta_s, gconv_s, fconv_s = _unpack(small_all, [a.shape for a in small])
    wt = {
        "norm1": norm1 + jnp.tile(late_token[0:1, :], (1, D_MODEL // LANES)),
        "gdn_conv_w": _gather_cols(gconv_s[:, 0]), "a_log": gdn_a_log[0], "dt_bias": gdn_dt_bias[0],
        "gdn_norm": gdn_norm, "norm2": norm2, "ffn_conv_w": _gather_cols(fconv_s[:, 0]), "ffn_conv_b": ffn_conv_b,
        "norm_f": norm_f.reshape(1, D_MODEL),
    }
    meta_f = _gather_cols(meta_s)

    pending = {}

    def on_ffn_out_grads(d_w_down, d_w_up_t, d_w_out):
        srcs = [d_w_out.reshape(N_DEV, D_MODEL // N_DEV, D_MODEL), d_w_up_t.reshape(N_DEV, 2 * D_FF // N_DEV, D_MODEL),
                d_w_down.reshape(N_DEV, D_FF // N_DEV, D_MODEL)]
        pending["ffn_out"], token = _split_start(srcs, "a2a", "exchange_ffn_out_start", d_w_out)
        return token

    def on_w_in_grads(d_w_in_t):
        slabs = d_w_in_t.astype(BF16).reshape(N_DEV, _O_END // N_DEV, D_MODEL)
        pending["w_in"], token = _split_start([slabs], "a2a", "exchange_w_in_start", d_w_in_t)
        return token

    hpad = jnp.concatenate([jnp.zeros((pad, D_MODEL), F32), meta_f, x[0]], axis=0)
    tgt = jnp.concatenate([jnp.zeros((pad + N_META, D_MODEL), F32), loss_target[0]], axis=0)
    lossvec, dh0, gr = _local_step(hpad, tgt, pad, wt, first_weights, late_weights, on_ffn_out_grads, on_w_in_grads)

    loss = lax.psum(jnp.sum(lossvec), ("x", "y", "c"))
    grad_x = dh0[pad + N_META:][None]

    big_m = [tr_(m_w_in), m_w_out[0], tr_(m_w_ffn_up), m_w_ffn_down[0]]
    big_v = [tr_(v_w_in), v_w_out[0], tr_(v_w_ffn_up), v_w_ffn_down[0]]
    slabs_ffn_out = _split_wait(pending["ffn_out"], "exchange_ffn_out_wait", dh0)
    big_out = [None] + [_adamw(slabs_ffn_out[i - 1], big[i], big_m[i], big_v[i], "adamw_big_%d" % i)
                        for i in range(1, len(big))]
    g_sm = [_scatter_cols(dh0[pad:pad + N_META]), _scatter_cols(gr["gdn_conv_w"]), _scatter_cols(gr["ffn_conv_w"])]
    g_small = jnp.stack([_pack([g[d] for g in g_sm], 8) for d in range(N_DEV)])
    slabs_small, = _all_to_all([g_small], "exchange_small_gradients")
    small_out = _adamw(slabs_small, _pack(small, 8), _pack([m_meta, m_gdn_conv_w, m_ffn_conv_w], 8),
                       _pack([v_meta, v_gdn_conv_w, v_ffn_conv_w], 8), "adamw_small_sharded")
    small_un = [_unpack(o, [a.shape for a in small]) for o in small_out]
    rep_w = [norm1, gdn_a_log, gdn_dt_bias, gdn_norm, norm2, ffn_conv_b, norm_f]
    rep_m = [m_norm1, m_gdn_a_log, m_gdn_dt_bias, m_gdn_norm, m_norm2, m_ffn_conv_b, m_norm_f]
    rep_v = [v_norm1, v_gdn_a_log, v_gdn_dt_bias, v_gdn_norm, v_norm2, v_ffn_conv_b, v_norm_f]
    rep_g = [gr["norm1"], gr["a_log"], gr["dt_bias"], gr["gdn_norm"], gr["norm2"], gr["ffn_conv_b"], gr["norm_f"]]
    rep_slabs, = _all_gather([_pack(rep_g, 8)], "gather_small_gradients")
    rep_out = _adamw(rep_slabs, _pack(rep_w, 8), _pack(rep_m, 8), _pack(rep_v, 8), "adamw_replicated")
    rep_shapes = [a.shape for a in rep_w]
    rp_g, rp_d, rp_nm, rp_nv = [_unpack(o, rep_shapes) for o in rep_out]

    slabs_w_in, = _split_wait(pending["w_in"], "exchange_w_in_wait", rep_out[0])
    big_out[0] = _adamw(slabs_w_in, big[0], big_m[0], big_v[0], "adamw_big_0")
    back = lambda a: jnp.swapaxes(a, 0, 1)[None]
    sh_g, sh_d, sh_nm, sh_nv = [
        [small_un[j][0], back(big_out[0][j]), small_un[j][1], big_out[1][j][None], back(big_out[2][j]),
         small_un[j][2], big_out[3][j][None]] for j in range(4)]

    def order(sh, rp):
        return [sh[0], rp[0], sh[1], sh[2], rp[1], rp[2], rp[3], sh[3], rp[4], sh[4], sh[5], rp[5], sh[6], rp[6]]

    return (loss, grad_x, *order(sh_g, rp_g), *order(sh_d, rp_d), *order(sh_nm, rp_nm), *order(sh_nv, rp_nv))
```

```python
import functools
import math

import numpy as np
import jax
import jax.numpy as jnp
from jax import lax
from jax.experimental import pallas as pl
from jax.experimental.pallas import tpu as pltpu

F32 = jnp.float32
BF16 = jnp.bfloat16
HI = lax.Precision.HIGHEST

D_MODEL = 1024
N_META = 16
CHUNK = 64
GDN_H = 8
GDN_D = 128
RET_H = 4
RET_D = 256
D_FF = 2816
GDN_CONV = 4
FFN_CONV = 3
ROPE_BASE = 10000.0
EPS = 1e-6
N_DEV = 8
LANES = 128
MAIN_W = 10 * 1024
_O_GQ, _O_GZ, _O_GA, _O_RQ, _O_RG, _O_GATE, _O_END = 0, 3072, 4096, 4112, 7184, 8208, 10256

ADAM_LR = 0.001
ADAM_B1 = 0.9
ADAM_B2 = 0.999
ADAM_EPS = 1e-08
ADAM_WD = 0.01
ADAM_STEP = 10

MESH_T = pl.DeviceIdType.MESH


def _tile(n, target, mult):
    best = None
    for d in range(mult, min(n, target) + 1, mult):
        if n % d == 0:
            best = d
    assert best is not None, (n, target, mult)
    return best


def _sig(x):
    return 1.0 / (1.0 + jnp.exp(-x))


def _d(a, b):
    return jnp.dot(a.astype(BF16), b.astype(BF16), preferred_element_type=F32)


def _dnt(a, b):
    return lax.dot_general(a.astype(BF16), b.astype(BF16), (((1,), (1,)), ((), ())), preferred_element_type=F32)


def _dtn(a, b):
    return lax.dot_general(a.astype(BF16), b.astype(BF16), (((0,), (0,)), ((), ())), preferred_element_type=F32)


def _dx(a, b):
    return jnp.dot(a, b, preferred_element_type=F32, precision=HI)


def _dxnt(a, b):
    return lax.dot_general(a, b, (((1,), (1,)), ((), ())), preferred_element_type=F32, precision=HI)


def _dxtn(a, b):
    return lax.dot_general(a, b, (((0,), (0,)), ((), ())), preferred_element_type=F32, precision=HI)


def _split(a):
    hi = a.astype(BF16)
    return hi, (a - hi.astype(F32)).astype(BF16)


def _d3g(a, b, dims):
    ah, al = _split(a)
    bh, bl = _split(b)
    f = functools.partial(lax.dot_general, dimension_numbers=dims, preferred_element_type=F32)
    return f(ah, bh) + (f(ah, bl) + f(al, bh))


_NN = (((1,), (0,)), ((), ()))
_NT = (((1,), (1,)), ((), ()))
_TN = (((0,), (0,)), ((), ()))


def _rowsum(x):
    return jnp.sum(x, axis=1, keepdims=True)


def _allsum(x):
    return jnp.sum(jnp.sum(x, axis=1, keepdims=True), axis=0, keepdims=True)


def _mm_nn(a, b, res=None, out_dtype=F32, bt=False, name="mm_nn"):
    M, K = a.shape
    N = b.shape[0] if bt else b.shape[1]
    tm = _tile(M, 704, 16)
    tn = _tile(N, 2816, 128)

    def body(*refs):
        if res is None:
            a_ref, b_ref, o_ref = refs
        else:
            a_ref, b_ref, r_ref, o_ref = refs
        acc = lax.dot_general(a_ref[...], b_ref[...], _NT if bt else _NN, preferred_element_type=F32)
        if res is not None:
            acc = acc + r_ref[...]
        o_ref[...] = acc.astype(out_dtype)

    b_spec = pl.BlockSpec((tn, K), lambda j, i: (j, 0)) if bt else pl.BlockSpec((K, tn), lambda j, i: (0, j))
    in_specs = [pl.BlockSpec((tm, K), lambda j, i: (i, 0)), b_spec]
    args = [a, b]
    if res is not None:
        in_specs.append(pl.BlockSpec((tm, tn), lambda j, i: (i, j)))
        args.append(res)
    return pl.pallas_call(
        body, grid=(N // tn, M // tm), in_specs=in_specs,
        out_specs=pl.BlockSpec((tm, tn), lambda j, i: (i, j)),
        out_shape=jax.ShapeDtypeStruct((M, N), out_dtype), name=name)(*args)


def _mm_nt(a, b, res=None, name="mm_nt"):
    M, Nc = a.shape
    K = b.shape[0]
    tm = _tile(M, 704, 16)
    tc = _tile(Nc, 5632, 128)

    def body(*refs):
        if res is None:
            a_ref, b_ref, o_ref = refs
        else:
            a_ref, b_ref, r_ref, o_ref = refs
        c = pl.program_id(1)
        p = lax.dot_general(a_ref[...], b_ref[...], (((1,), (1,)), ((), ())), preferred_element_type=F32)

        @pl.when(c == 0)
        def _():
            if res is None:
                o_ref[...] = p
            else:
                o_ref[...] = p + r_ref[...]

        @pl.when(c > 0)
        def _():
            o_ref[...] += p

    in_specs = [pl.BlockSpec((tm, tc), lambda i, c: (i, c)), pl.BlockSpec((K, tc), lambda i, c: (0, c))]
    args = [a, b]
    if res is not None:
        in_specs.append(pl.BlockSpec((tm, K), lambda i, c: (i, 0)))
        args.append(res)
    return pl.pallas_call(
        body, grid=(M // tm, Nc // tc), in_specs=in_specs,
        out_specs=pl.BlockSpec((tm, K), lambda i, c: (i, 0)),
        out_shape=jax.ShapeDtypeStruct((M, K), F32), name=name)(*args)


def _mm_tn(a, b, name="mm_tn"):
    M, K = a.shape
    N = b.shape[1]
    tm = _tile(M, 2752, 16)
    tk = _tile(K, 1408, 128)
    tn = _tile(N, 1408, 128)

    def body(a_ref, b_ref, o_ref):
        m = pl.program_id(2)
        p = lax.dot_general(a_ref[...], b_ref[...], (((0,), (0,)), ((), ())), preferred_element_type=F32)

        @pl.when(m == 0)
        def _():
            o_ref[...] = p

        @pl.when(m > 0)
        def _():
            o_ref[...] += p

    return pl.pallas_call(
        body, grid=(K // tk, N // tn, M // tm),
        in_specs=[pl.BlockSpec((tm, tk), lambda kk, j, m: (m, kk)), pl.BlockSpec((tm, tn), lambda kk, j, m: (m, j))],
        out_specs=pl.BlockSpec((tk, tn), lambda kk, j, m: (kk, j)),
        out_shape=jax.ShapeDtypeStruct((K, N), F32), name=name)(a, b)


def _rms_fwd(x, g, name):
    Lp = x.shape[0]
    tr = _tile(Lp, 256, 16)

    def body(x_ref, g_ref, o_ref):
        xv = x_ref[...]
        r = lax.rsqrt(jnp.mean(xv * xv, axis=-1, keepdims=True) + EPS)
        o_ref[...] = (xv * r * g_ref[...]).astype(BF16)

    return pl.pallas_call(
        body, grid=(Lp // tr,),
        in_specs=[pl.BlockSpec((tr, D_MODEL), lambda i: (i, 0)), pl.BlockSpec((1, D_MODEL), lambda i: (0, 0))],
        out_specs=pl.BlockSpec((tr, D_MODEL), lambda i: (i, 0)),
        out_shape=jax.ShapeDtypeStruct((Lp, D_MODEL), BF16), name=name)(x, g)


class _Producer:
    def __init__(self, a, b, res=None):
        self.a, self.b, self.res = a, b, res
        self.tr = _tile(a.shape[0], 704, 16)
        K = a.shape[1]
        self.args = [a, b] + ([] if res is None else [res])
        self.specs = [pl.BlockSpec((self.tr, K), lambda i: (i, 0)),
                      pl.BlockSpec((K, D_MODEL), lambda i: (0, 0), pipeline_mode=pl.Buffered(1))]
        if res is not None:
            self.specs.append(pl.BlockSpec((self.tr, D_MODEL), lambda i: (i, 0)))

    def tile(self, refs):
        acc = jnp.dot(refs[0][...], refs[1][...], preferred_element_type=F32)
        return acc if self.res is None else acc + refs[2][...]


def _mm_rms_fwd(prod, g, name):
    Lp, tr, n = prod.a.shape[0], prod.tr, len(prod.args)

    def body(*refs):
        g_ref, x_ref, o_ref = refs[n:]
        xv = prod.tile(refs[:n])
        r = lax.rsqrt(jnp.mean(xv * xv, axis=-1, keepdims=True) + EPS)
        x_ref[...] = xv
        o_ref[...] = (xv * r * g_ref[...]).astype(BF16)

    blk = pl.BlockSpec((tr, D_MODEL), lambda i: (i, 0))
    return pl.pallas_call(
        body, grid=(Lp // tr,), in_specs=prod.specs + [pl.BlockSpec((1, D_MODEL), lambda i: (0, 0))],
        out_specs=[blk, blk],
        out_shape=[jax.ShapeDtypeStruct((Lp, D_MODEL), F32), jax.ShapeDtypeStruct((Lp, D_MODEL), BF16)],
        name=name)(*prod.args, g)


def _rms_bwd(x, g, dy, dres, pad, name):
    Lp = x.shape[0]
    fused = isinstance(dy, _Producer)
    tr = dy.tr if fused else _tile(Lp, 256, 16)
    n = len(dy.args) if fused else 1

    def body(*refs):
        x_ref, g_ref, dr_ref, dx_ref, dxb_ref, dg_ref = refs[n:]
        i = pl.program_id(0)
        xv = x_ref[...]
        r = lax.rsqrt(jnp.mean(xv * xv, axis=-1, keepdims=True) + EPS)
        xh = xv * r
        dyv = dy.tile(refs[:n]) if fused else refs[0][...]
        dxh = dyv * g_ref[...]
        dx = r * (dxh - xh * jnp.mean(dxh * xh, axis=-1, keepdims=True)) + dr_ref[...]
        row = i * tr + lax.broadcasted_iota(jnp.int32, (tr, 1), 0)
        dx = jnp.where(row >= pad, dx, 0.0)
        dx_ref[...] = dx
        dxb_ref[...] = dx.astype(BF16)
        part = jnp.sum(dyv * xh, axis=0, keepdims=True)

        @pl.when(i == 0)
        def _():
            dg_ref[...] = part

        @pl.when(i > 0)
        def _():
            dg_ref[...] += part

    blk = pl.BlockSpec((tr, D_MODEL), lambda i: (i, 0))
    vec = pl.BlockSpec((1, D_MODEL), lambda i: (0, 0))
    return pl.pallas_call(
        body, grid=(Lp // tr,), in_specs=(dy.specs if fused else [blk]) + [blk, vec, blk], out_specs=[blk, blk, vec],
        out_shape=[jax.ShapeDtypeStruct((Lp, D_MODEL), F32), jax.ShapeDtypeStruct((Lp, D_MODEL), BF16),
                   jax.ShapeDtypeStruct((1, D_MODEL), F32)], name=name)(*(dy.args if fused else [dy]), x, g, dres)


def _final(h2, g, tgt, first_row):
    fused = isinstance(h2, _Producer)
    Lp = h2.a.shape[0] if fused else h2.shape[0]
    tr = h2.tr if fused else _tile(Lp, 256, 16)
    n = len(h2.args) if fused else 1

    def body(*refs):
        g_ref, t_ref, loss_ref, dx_ref, dxb_ref, dg_ref = refs[n:]
        i = pl.program_id(0)
        xv = h2.tile(refs[:n]) if fused else refs[0][...]
        gv = g_ref[...]
        r = lax.rsqrt(jnp.mean(xv * xv, axis=-1, keepdims=True) + EPS)
        xh = xv * r
        row = i * tr + lax.broadcasted_iota(jnp.int32, (tr, 1), 0)
        err = jnp.where(row >= first_row, xh * gv - t_ref[...], 0.0)
        lpart = jnp.sum(err * err, axis=0, keepdims=True) * (0.5 / D_MODEL)
        dyv = err * (1.0 / D_MODEL)
        dxh = dyv * gv
        dx = r * (dxh - xh * jnp.mean(dxh * xh, axis=-1, keepdims=True))
        dx_ref[...] = dx
        dxb_ref[...] = dx.astype(BF16)
        part = jnp.sum(dyv * xh, axis=0, keepdims=True)

        @pl.when(i == 0)
        def _():
            dg_ref[...] = part
            loss_ref[...] = lpart

        @pl.when(i > 0)
        def _():
            dg_ref[...] += part
            loss_ref[...] += lpart

    blk = pl.BlockSpec((tr, D_MODEL), lambda i: (i, 0))
    vec = pl.BlockSpec((1, D_MODEL), lambda i: (0, 0))
    return pl.pallas_call(
        body, grid=(Lp // tr,), in_specs=(h2.specs if fused else [blk]) + [vec, blk], out_specs=[vec, blk, blk, vec],
        out_shape=[jax.ShapeDtypeStruct((1, D_MODEL), F32), jax.ShapeDtypeStruct((Lp, D_MODEL), F32),
                   jax.ShapeDtypeStruct((Lp, D_MODEL), BF16), jax.ShapeDtypeStruct((1, D_MODEL), F32)],
        name="final_norm_loss")(*(h2.args if fused else [h2]), g, tgt)


def _halo_prev(tr, width, col=0):
    return pl.BlockSpec((8, width), lambda i: (jnp.maximum(i * (tr // 8) - 1, 0), col))


def _halo_next(tr, width, nrows, col=0):
    last = nrows // 8 - 1
    return pl.BlockSpec((8, width), lambda i: (jnp.minimum((i + 1) * (tr // 8), last), col))


def _shifted(x, offs):
    n = x.shape[0]
    return [x if off == 0 else pltpu.roll(x, n - off, 0) for off in offs]


def _taps(wins, w, rows, bias=None):
    acc = w[0:1, :] * wins[0][0:rows, :]
    if bias is not None:
        acc = acc + bias
    for kk in range(1, len(wins)):
        acc = acc + w[kk:kk + 1, :] * wins[kk][0:rows, :]
    return acc


def _gdn_pre(proj_m, proj_s, conv_w, gparams, pad):
    Lp = proj_m.shape[0]
    tr = _tile(Lp, 192, 64)
    W3 = 3 * D_MODEL

    def body(main_ref, prev_ref, s_ref, w_ref, gp_ref, qkv_ref, gsm_ref, c_ref):
        i = pl.program_id(0)
        prev = jnp.where(i > 0, prev_ref[...], 0.0)
        ext = jnp.concatenate([prev, main_ref[...]], axis=0)
        c = _taps(_shifted(ext, range(8 - (GDN_CONV - 1), 9)), w_ref[...], tr)
        c_ref[...] = c
        s = c * _sig(c)
        scale = GDN_D ** -0.5
        for j in range(2 * GDN_H):
            seg = s[:, j * GDN_D:(j + 1) * GDN_D]
            r = lax.rsqrt(_rowsum(seg * seg) + EPS)
            if j < GDN_H:
                r = r * scale
            qkv_ref[:, j * GDN_D:(j + 1) * GDN_D] = seg * r
        qkv_ref[:, 2 * D_MODEL:] = s[:, 2 * D_MODEL:]
        sm = s_ref[...]
        gp = gp_ref[...]
        lane = lax.broadcasted_iota(jnp.int32, sm.shape, 1)
        z = sm + gp[1:2, :]
        softplus = jnp.maximum(z, 0.0) + jnp.log(1.0 + jnp.exp(-jnp.abs(z)))
        lg = -jnp.exp(gp[0:1, :]) * softplus
        row = i * tr + lax.broadcasted_iota(jnp.int32, (tr, 1), 0)
        out = jnp.where(lane < GDN_H, lg, jnp.where(lane < 2 * GDN_H, _sig(sm), 0.0))
        gsm_ref[...] = jnp.where(row >= pad, out, 0.0)

    return pl.pallas_call(
        body, grid=(Lp // tr,),
        in_specs=[pl.BlockSpec((tr, W3), lambda i: (i, 0)), _halo_prev(tr, W3),
                  pl.BlockSpec((tr, LANES), lambda i: (i, 0)),
                  pl.BlockSpec((GDN_CONV, W3), lambda i: (0, 0)), pl.BlockSpec((8, LANES), lambda i: (0, 0))],
        out_specs=[pl.BlockSpec((tr, W3), lambda i: (i, 0)), pl.BlockSpec((tr, LANES), lambda i: (i, 0)),
                   pl.BlockSpec((tr, W3), lambda i: (i, 0))],
        out_shape=[jax.ShapeDtypeStruct((Lp, W3), F32), jax.ShapeDtypeStruct((Lp, LANES), F32),
                   jax.ShapeDtypeStruct((Lp, W3), F32)],
        name="gdn_pre")(proj_m, proj_m, proj_s, conv_w, gparams)


def _gdn_pre_bwd(proj_m, conv_out, proj_s, conv_w, gparams, dq, dk, dv, dgs, pad):
    Lp = proj_m.shape[0]
    tr = _tile(Lp, 192, 64)
    W3 = 3 * D_MODEL
    te = tr + 8

    def body(main_ref, c_ref, cn_ref, s_ref, w_ref, gp_ref,
             dq_ref, dqn_ref, dk_ref, dkn_ref, dv_ref, dvn_ref, dgs_ref,
             da_ref, ds_ref, dw_ref, dgp_ref):
        i = pl.program_id(0)
        w = w_ref[...]
        c = jnp.concatenate([c_ref[...], cn_ref[...]], axis=0)
        sg = _sig(c)
        s = c * sg
        rowe = i * tr + lax.broadcasted_iota(jnp.int32, (te, 1), 0)
        live = (rowe >= pad) & (rowe < Lp)
        dqe = jnp.concatenate([dq_ref[...], dqn_ref[...]], axis=0)
        dke = jnp.concatenate([dk_ref[...], dkn_ref[...]], axis=0)
        dve = jnp.concatenate([dv_ref[...], dvn_ref[...]], axis=0)
        scale = GDN_D ** -0.5
        parts = []
        for j in range(2 * GDN_H):
            seg = s[:, j * GDN_D:(j + 1) * GDN_D]
            r = lax.rsqrt(_rowsum(seg * seg) + EPS)
            xh = seg * r
            if j < GDN_H:
                dxh = dqe[:, j * GDN_D:(j + 1) * GDN_D] * scale
            else:
                dxh = dke[:, (j - GDN_H) * GDN_D:(j - GDN_H + 1) * GDN_D]
            parts.append(r * (dxh - xh * _rowsum(dxh * xh)))
        parts.append(dve)
        dsv = jnp.concatenate(parts, axis=1)
        dc = jnp.where(live, dsv * (sg * (1.0 + c * (1.0 - sg))), 0.0)
        dcs = _shifted(dc, range(GDN_CONV - 1, -1, -1))
        da_ref[...] = _taps(dcs, w, tr).astype(BF16)
        pm = main_ref[...]
        rows = [jnp.sum(dcs[kk][0:tr, :] * pm, axis=0, keepdims=True) for kk in range(GDN_CONV)]
        dwp = jnp.concatenate(rows + [jnp.zeros((8 - GDN_CONV, W3), F32)], axis=0)

        sm = s_ref[...]
        gp = gp_ref[...]
        lane = lax.broadcasted_iota(jnp.int32, sm.shape, 1)
        rowm = i * tr + lax.broadcasted_iota(jnp.int32, (tr, 1), 0)
        dgv = jnp.where(rowm >= pad, dgs_ref[...], 0.0)
        dlg = jnp.where(lane < GDN_H, dgv, 0.0)
        dbt = jnp.where((lane >= GDN_H) & (lane < 2 * GDN_H), dgv, 0.0)
        z = sm + gp[1:2, :]
        softplus = jnp.maximum(z, 0.0) + jnp.log(1.0 + jnp.exp(-jnp.abs(z)))
        ea = jnp.exp(gp[0:1, :])
        dz = dlg * (-ea) * _sig(z)
        dal = dlg * (-ea) * softplus
        bt = _sig(sm)
        dgb = dbt * bt * (1.0 - bt)
        ds_ref[...] = (dz + dgb).astype(BF16)
        gpp = jnp.concatenate([jnp.sum(dal, axis=0, keepdims=True), jnp.sum(dz, axis=0, keepdims=True),
                               jnp.zeros((6, LANES), F32)], axis=0)

        @pl.when(i == 0)
        def _():
            dw_ref[...] = dwp
            dgp_ref[...] = gpp

        @pl.when(i > 0)
        def _():
            dw_ref[...] += dwp
            dgp_ref[...] += gpp

    m3 = pl.BlockSpec((tr, W3), lambda i: (i, 0))
    m1 = pl.BlockSpec((tr, D_MODEL), lambda i: (i, 0))
    n1 = _halo_next(tr, D_MODEL, Lp)
    return pl.pallas_call(
        body, grid=(Lp // tr,),
        in_specs=[m3, m3, _halo_next(tr, W3, Lp), pl.BlockSpec((tr, LANES), lambda i: (i, 0)),
                  pl.BlockSpec((GDN_CONV, W3), lambda i: (0, 0)), pl.BlockSpec((8, LANES), lambda i: (0, 0)),
                  m1, n1, m1, n1, m1, n1, pl.BlockSpec((tr, LANES), lambda i: (i, 0))],
        out_specs=[m3, pl.BlockSpec((tr, LANES), lambda i: (i, 0)),
                   pl.BlockSpec((8, W3), lambda i: (0, 0)), pl.BlockSpec((8, LANES), lambda i: (0, 0))],
        out_shape=[jax.ShapeDtypeStruct((Lp, W3), BF16), jax.ShapeDtypeStruct((Lp, LANES), BF16),
                   jax.ShapeDtypeStruct((8, W3), F32), jax.ShapeDtypeStruct((8, LANES), F32)],
        name="gdn_pre_bwd")(proj_m, conv_out, conv_out, proj_s, conv_w, gparams, dq, dq, dk, dk, dv, dv, dgs)


def _gdn_gates(gs):
    ri = lax.broadcasted_iota(jnp.int32, (CHUNK, CHUNK), 0)
    ci = lax.broadcasted_iota(jnp.int32, (CHUNK, CHUNK), 1)
    tril = ri >= ci
    strict = ri > ci
    gall = _dx(tril.astype(F32), gs)
    lane8 = lax.broadcasted_iota(jnp.int32, (8, LANES), 1)
    sub8 = lax.broadcasted_iota(jnp.int32, (8, LANES), 0)
    grow = _dxnt((lane8 == sub8).astype(F32), gall)
    return gall, grow, tril, strict


def _gdn_decay(gall, grow, tril, h):
    g = gall[:, h:h + 1]
    return g, jnp.where(tril, jnp.exp(jnp.where(tril, g - grow[h:h + 1, :], 0.0)), 0.0)


def _gdn_chunk_specs(N, rev):
    cn = (lambda n: N - 1 - n) if rev else (lambda n: n)
    col = lambda j: pl.BlockSpec((CHUNK, D_MODEL), lambda n: (cn(n), j))
    gate = pl.BlockSpec((CHUNK, LANES), lambda n: (cn(n), 0))
    st = lambda a, b: pl.BlockSpec((GDN_H, None, a, b), lambda n: (0, cn(n), 0, 0))
    return col, gate, st


def _gdn_chunk_fwd(qkv, gsm):
    Lp = qkv.shape[0]
    N = Lp // CHUNK

    def body(q_ref, k_ref, v_ref, gs_ref, o_ref, sin_ref, t_ref, S):
        n = pl.program_id(0)

        @pl.when(n == 0)
        def _():
            S[...] = jnp.zeros_like(S)

        gs = gs_ref[...]
        gall, grow, tril, strict = _gdn_gates(gs)
        ri = lax.broadcasted_iota(jnp.int32, (CHUNK, CHUNK), 0)
        ci = lax.broadcasted_iota(jnp.int32, (CHUNK, CHUNK), 1)
        eye = (ri == ci).astype(F32)
        heads = range(GDN_H)
        sls = [slice(h * GDN_D, (h + 1) * GDN_D) for h in heads]
        q = [q_ref[:, sl] for sl in sls]
        k = [k_ref[:, sl] for sl in sls]
        v = [v_ref[:, sl] for sl in sls]
        s0 = [S[h] for h in heads]
        beta = [gs[:, GDN_H + h:GDN_H + h + 1] for h in heads]
        gg = [_gdn_decay(gall, grow, tril, h) for h in heads]
        g = [x[0] for x in gg]
        gam = [x[1] for x in gg]
        eg = [jnp.exp(g[h]) for h in heads]
        gl = [g[h][CHUNK - 1:CHUNK, :] for h in heads]
        kb = [k[h] * beta[h] for h in heads]
        pw = [-jnp.where(strict, _dnt(kb[h], k[h]) * gam[h], 0.0) for h in heads]
        p = [_dnt(q[h], k[h]) * gam[h] for h in heads]
        qs = [_d(q[h] * eg[h], s0[h]) for h in heads]
        t = [eye + pw[h] for h in heads]
        for _ in range(5):
            pw = [_d3g(pw[h], pw[h], _NN) for h in heads]
            t = [t[h] + _d3g(t[h], pw[h], _NN) for h in heads]
        u = [_d(t[h], v[h] * beta[h]) for h in heads]
        w = [_d(t[h], kb[h] * eg[h]) for h in heads]
        vnew = [u[h] - _d(w[h], s0[h]) for h in heads]
        for h in heads:
            o_ref[:, sls[h]] = qs[h] + _d(p[h], vnew[h])
            sin_ref[h] = s0[h]
            t_ref[h] = t[h]
            S[h] = s0[h] * jnp.exp(gl[h]) + _dtn(k[h] * jnp.exp(gl[h] - g[h]), vnew[h])

    col, gate, st = _gdn_chunk_specs(N, False)
    return pl.pallas_call(
        body, grid=(N,),
        in_specs=[col(0), col(1), col(2), gate],
        out_specs=[col(0), st(GDN_D, GDN_D), st(CHUNK, CHUNK)],
        out_shape=[jax.ShapeDtypeStruct((Lp, D_MODEL), F32), jax.ShapeDtypeStruct((GDN_H, N, GDN_D, GDN_D), F32),
                   jax.ShapeDtypeStruct((GDN_H, N, CHUNK, CHUNK), F32)],
        scratch_shapes=[pltpu.VMEM((GDN_H, GDN_D, GDN_D), F32)],
        name="gdn_chunk_fwd")(qkv, qkv, qkv, gsm)


def _gdn_chunk_bwd(qkv, gsm, do, s_in, t_in):
    Lp = qkv.shape[0]
    N = Lp // CHUNK

    def body(q_ref, k_ref, v_ref, gs_ref, do_ref, sin_ref, t_ref, dq_ref, dk_ref, dv_ref, dgs_ref, dS):
        n = pl.program_id(0)

        @pl.when(n == 0)
        def _():
            dS[...] = jnp.zeros_like(dS)

        gs = gs_ref[...]
        gall, grow, tril, strict = _gdn_gates(gs)
        lane = lax.broadcasted_iota(jnp.int32, (CHUNK, LANES), 1)
        rcol = lax.broadcasted_iota(jnp.int32, (CHUNK, 1), 0)
        ones = jnp.ones((CHUNK, LANES), F32)
        dg_all = jnp.zeros((CHUNK, LANES), F32)
        dbeta_all = jnp.zeros((CHUNK, LANES), F32)
        heads = range(GDN_H)
        sls = [slice(h * GDN_D, (h + 1) * GDN_D) for h in heads]
        H = lambda f: [f(h) for h in heads]
        q = H(lambda h: q_ref[:, sls[h]])
        k = H(lambda h: k_ref[:, sls[h]])
        v = H(lambda h: v_ref[:, sls[h]])
        dov = H(lambda h: do_ref[:, sls[h]])
        s0 = H(lambda h: sin_ref[h])
        t = H(lambda h: t_ref[h])
        dsv = H(lambda h: dS[h])
        beta = H(lambda h: gs[:, GDN_H + h:GDN_H + h + 1])
        gg = H(lambda h: _gdn_decay(gall, grow, tril, h))
        g = [x[0] for x in gg]
        gam = [x[1] for x in gg]
        eg = H(lambda h: jnp.exp(g[h]))
        egl = H(lambda h: jnp.exp(g[h][CHUNK - 1:CHUNK, :]))
        e = H(lambda h: jnp.exp(g[h][CHUNK - 1:CHUNK, :] - g[h]))
        kb = H(lambda h: k[h] * beta[h])
        kbg = H(lambda h: kb[h] * eg[h])
        vb = H(lambda h: v[h] * beta[h])
        qg = H(lambda h: q[h] * eg[h])
        kd = H(lambda h: k[h] * e[h])
        m = H(lambda h: jnp.where(strict, _dnt(kb[h], k[h]) * gam[h], 0.0))
        u = H(lambda h: _d(t[h], vb[h]))
        w = H(lambda h: _d(t[h], kbg[h]))
        p = H(lambda h: _dnt(q[h], k[h]) * gam[h])
        dqg = H(lambda h: _dnt(dov[h], s0[h]))
        kdds = H(lambda h: _d(kd[h], dsv[h]))
        qgdo = H(lambda h: _dtn(qg[h], dov[h]))
        vnew = H(lambda h: u[h] - _d(w[h], s0[h]))
        dvnew = H(lambda h: _dtn(p[h], dov[h]) + kdds[h])
        dp = H(lambda h: jnp.where(tril, _dnt(dov[h], vnew[h]), 0.0))
        dkd = H(lambda h: _dnt(vnew[h], dsv[h]))
        dw = H(lambda h: -_dnt(dvnew[h], s0[h]))
        for h in heads:
            dS[h] = qgdo[h] + egl[h] * dsv[h] - _dtn(w[h], dvnew[h])
        dvb = H(lambda h: _dtn(t[h], dvnew[h]))
        dkbg = H(lambda h: _dtn(t[h], dw[h]))
        dt = H(lambda h: _dnt(dvnew[h], vb[h]) + _dnt(dw[h], kbg[h]))
        x1 = H(lambda h: _d3g(t[h], dt[h], _TN))
        dm = H(lambda h: jnp.where(strict, -_d3g(x1[h], t[h], _NT), 0.0))
        dkk = H(lambda h: dm[h] * gam[h])
        dqk = H(lambda h: dp[h] * gam[h])
        dkb = H(lambda h: _d(dkk[h], k[h]) + eg[h] * dkbg[h])
        em = H(lambda h: dm[h] * m[h] + dp[h] * p[h])
        colsum = H(lambda h: _d3g(em[h], ones, _TN)[:, 0:1])
        for h in heads:
            dk_ref[:, sls[h]] = _dtn(dkk[h], kb[h]) + _dtn(dqk[h], q[h]) + dkd[h] * e[h] + beta[h] * dkb[h]
            dq_ref[:, sls[h]] = _d(dqk[h], k[h]) + dqg[h] * eg[h]
            dv_ref[:, sls[h]] = beta[h] * dvb[h]
        for h in heads:
            dbeta = _rowsum(k[h] * dkb[h]) + _rowsum(v[h] * dvb[h])
            z = _rowsum(kd[h] * dkd[h])
            dg = _rowsum(em[h]) - colsum[h] + _rowsum(qg[h] * dqg[h]) + _rowsum(kbg[h] * dkbg[h]) - z
            extra = _allsum(z) + egl[h] * _allsum(s0[h] * dsv[h])
            dg = dg + jnp.where(rcol == CHUNK - 1, extra, 0.0)
            dg_all = dg_all + jnp.where(lane == h, dg, 0.0)
            dbeta_all = dbeta_all + jnp.where(lane == GDN_H + h, dbeta, 0.0)
        ri = lax.broadcasted_iota(jnp.int32, (CHUNK, CHUNK), 0)
        ci = lax.broadcasted_iota(jnp.int32, (CHUNK, CHUNK), 1)
        dgs_ref[...] = _dx((ci >= ri).astype(F32), dg_all) + dbeta_all

    col, gate, st = _gdn_chunk_specs(N, True)
    return pl.pallas_call(
        body, grid=(N,),
        in_specs=[col(0), col(1), col(2), gate, col(0), st(GDN_D, GDN_D), st(CHUNK, CHUNK)],
        out_specs=[col(0), col(0), col(0), gate],
        out_shape=[jax.ShapeDtypeStruct((Lp, D_MODEL), F32)] * 3 + [jax.ShapeDtypeStruct((Lp, LANES), F32)],
        scratch_shapes=[pltpu.VMEM((GDN_H, GDN_D, GDN_D), F32)],
        name="gdn_chunk_bwd")(qkv, qkv, qkv, gsm, do, s_in, t_in)


def _rot(x, c, s):
    half = RET_D // 2
    x1 = x[:, :half]
    x2 = x[:, half:]
    return jnp.concatenate([x1 * c - x2 * s, x2 * c + x1 * s], axis=1)


def _rot_bwd(d, c, s):
    half = RET_D // 2
    d1 = d[:, :half]
    d2 = d[:, half:]
    return jnp.concatenate([d1 * c + d2 * s, d2 * c - d1 * s], axis=1)


def _ret_tables():
    hh = jnp.arange(RET_H, dtype=F32)
    lg = jnp.log(1.0 - 2.0 ** (-5.0 - hh))
    idx = jnp.arange(CHUNK, dtype=F32)
    tril = jnp.asarray(np.tril(np.ones((CHUNK, CHUNK), dtype=bool)))
    dmask = jnp.where(tril, jnp.exp((idx[:, None] - idx[None, :]) * lg[:, None, None]), 0.0)
    qdec = jnp.exp((idx[None, :] + 1.0) * lg[:, None])
    kdec = jnp.exp((CHUNK - 1.0 - idx[None, :]) * lg[:, None])
    gch = jnp.exp(CHUNK * lg)
    qdec = jnp.broadcast_to(qdec[:, :, None], (RET_H, CHUNK, RET_D))
    kdec = jnp.broadcast_to(kdec[:, :, None], (RET_H, CHUNK, RET_D))
    gch = jnp.broadcast_to(gch[:, None, None], (RET_H, 8, LANES))
    return dmask, qdec, kdec, gch


def _ret_specs(N, rev):
    cn = (lambda n: N - 1 - n) if rev else (lambda n: n)
    col = lambda j: pl.BlockSpec((CHUNK, D_MODEL), lambda n: (cn(n), j))
    tab = lambda a, b: pl.BlockSpec((RET_H, a, b), lambda n: (0, 0, 0))
    rope = pl.BlockSpec((CHUNK, LANES), lambda n: (cn(n), 0))
    st = pl.BlockSpec((RET_H, None, RET_D, RET_D), lambda n: (0, cn(n), 0, 0))
    return col, tab, rope, st


def _ret_chunk_fwd(proj_m, cos, sin, tables):
    Lp = proj_m.shape[0]
    N = Lp // CHUNK
    dmask, qdec, kdec, gch = tables

    def body(q_ref, k_ref, v_ref, c_ref, s_ref, dm_ref, qd_ref, kd_ref, g_ref, o_ref, sin_ref, S):
        n = pl.program_id(0)

        @pl.when(n == 0)
        def _():
            S[...] = jnp.zeros_like(S)

        c = c_ref[...]
        s = s_ref[...]
        heads = range(RET_H)
        sls = [slice(h * RET_D, (h + 1) * RET_D) for h in heads]
        H = lambda f: [f(h) for h in heads]
        qr = H(lambda h: _rot(q_ref[:, sls[h]], c, s))
        ks = H(lambda h: _rot(k_ref[:, sls[h]], c, s) * (RET_D ** -0.5))
        v = H(lambda h: v_ref[:, sls[h]])
        s0 = H(lambda h: S[h])
        a = H(lambda h: _dnt(qr[h], ks[h]) * dm_ref[h])
        qs = H(lambda h: _d(qr[h] * qd_ref[h], s0[h]))
        kv = H(lambda h: _dtn(ks[h] * kd_ref[h], v[h]))
        for h in heads:
            o_ref[:, sls[h]] = _d(a[h], v[h]) + qs[h]
            sin_ref[h] = s0[h]
            S[h] = s0[h] * g_ref[h, 0:1, 0:1] + kv[h]

    col, tab, rope, st = _ret_specs(N, False)
    return pl.pallas_call(
        body, grid=(N,),
        in_specs=[col(3), col(4), col(5), rope, rope,
                  tab(CHUNK, CHUNK), tab(CHUNK, RET_D), tab(CHUNK, RET_D), tab(8, LANES)],
        out_specs=[col(0), st],
        out_shape=[jax.ShapeDtypeStruct((Lp, D_MODEL), F32), jax.ShapeDtypeStruct((RET_H, N, RET_D, RET_D), F32)],
        scratch_shapes=[pltpu.VMEM((RET_H, RET_D, RET_D), F32)],
        name="ret_chunk_fwd")(proj_m, proj_m, proj_m, cos, sin, dmask, qdec, kdec, gch)


def _ret_chunk_bwd(proj_m, cos, sin, tables, do, s_in):
    Lp = proj_m.shape[0]
    N = Lp // CHUNK
    dmask, qdec, kdec, gch = tables

    def body(q_ref, k_ref, v_ref, c_ref, s_ref, dm_ref, qd_ref, kd_ref, g_ref, do_ref, sin_ref,
             dq_ref, dk_ref, dv_ref, dS):
        n = pl.program_id(0)

        @pl.when(n == 0)
        def _():
            dS[...] = jnp.zeros_like(dS)

        c = c_ref[...]
        s = s_ref[...]
        kscale = RET_D ** -0.5
        heads = range(RET_H)
        sls = [slice(h * RET_D, (h + 1) * RET_D) for h in heads]
        H = lambda f: [f(h) for h in heads]
        qr = H(lambda h: _rot(q_ref[:, sls[h]], c, s))
        ks = H(lambda h: _rot(k_ref[:, sls[h]], c, s) * kscale)
        v = H(lambda h: v_ref[:, sls[h]])
        dov = H(lambda h: do_ref[:, sls[h]])
        s0 = H(lambda h: sin_ref[h])
        dsv = H(lambda h: dS[h])
        ad = H(lambda h: _dnt(qr[h], ks[h]) * dm_ref[h])
        da = H(lambda h: _dnt(dov[h], v[h]) * dm_ref[h])
        kds = H(lambda h: _d(ks[h] * kd_ref[h], dsv[h]))
        dos = H(lambda h: _dnt(dov[h], s0[h]) * qd_ref[h])
        vds = H(lambda h: _dnt(v[h], dsv[h]) * kd_ref[h])
        qdo = H(lambda h: _dtn(qr[h] * qd_ref[h], dov[h]))
        for h in heads:
            dS[h] = dsv[h] * g_ref[h, 0:1, 0:1] + qdo[h]
        for h in heads:
            dv_ref[:, sls[h]] = (_dtn(ad[h], dov[h]) + kds[h]).astype(BF16)
            dq_ref[:, sls[h]] = _rot_bwd(_d(da[h], ks[h]) + dos[h], c, s).astype(BF16)
            dk_ref[:, sls[h]] = _rot_bwd((_dtn(da[h], qr[h]) + vds[h]) * kscale, c, s).astype(BF16)

    col, tab, rope, st = _ret_specs(N, True)
    return pl.pallas_call(
        body, grid=(N,),
        in_specs=[col(3), col(4), col(5), rope, rope,
                  tab(CHUNK, CHUNK), tab(CHUNK, RET_D), tab(CHUNK, RET_D), tab(8, LANES), col(0), st],
        out_specs=[col(0), col(0), col(0)],
        out_shape=[jax.ShapeDtypeStruct((Lp, D_MODEL), BF16)] * 3,
        scratch_shapes=[pltpu.VMEM((RET_H, RET_D, RET_D), F32)],
        name="ret_chunk_bwd")(proj_m, proj_m, proj_m, cos, sin, dmask, qdec, kdec, gch, do, s_in)


def _merge_specs(tr):
    col = lambda j: pl.BlockSpec((tr, D_MODEL), lambda i: (i, j))
    return col


def _merge_fwd(o_a, o_b, proj_m, gnorm):
    Lp = o_a.shape[0]
    tr = _tile(Lp, 192, 16)

    def body(oa_ref, ob_ref, gz_ref, rg_ref, ga_ref, gb_ref, gn_ref, y_ref):
        gn = gn_ref[...]
        oa = oa_ref[...]
        ob = ob_ref[...]
        gz = gz_ref[...]
        ya = []
        for j in range(GDN_H):
            seg = oa[:, j * GDN_D:(j + 1) * GDN_D]
            r = lax.rsqrt(jnp.mean(seg * seg, axis=-1, keepdims=True) + EPS)
            ya.append(seg * r * gn)
        ya = jnp.concatenate(ya, axis=1) * (gz * _sig(gz))
        yb = []
        for j in range(RET_H):
            seg = ob[:, j * RET_D:(j + 1) * RET_D]
            r = lax.rsqrt(jnp.mean(seg * seg, axis=-1, keepdims=True) + EPS)
            yb.append(seg * r)
        rg = rg_ref[...]
        yb = jnp.concatenate(yb, axis=1) * (rg * _sig(rg))
        y_ref[...] = (_sig(ga_ref[...]) * ya + _sig(gb_ref[...]) * yb).astype(BF16)

    col = _merge_specs(tr)
    return pl.pallas_call(
        body, grid=(Lp // tr,),
        in_specs=[col(0), col(0), col(6), col(7), col(8), col(9), pl.BlockSpec((1, GDN_D), lambda i: (0, 0))],
        out_specs=col(0), out_shape=jax.ShapeDtypeStruct((Lp, D_MODEL), BF16),
        name="merge_fwd")(o_a, o_b, proj_m, proj_m, proj_m, proj_m, gnorm)


def _merge_bwd(dh1b, w_out, o_a, o_b, proj_m, gnorm):
    Lp = o_a.shape[0]
    tr = _tile(Lp, 192, 16)

    def body(d_ref, wo_ref, oa_ref, ob_ref, gz_ref, rg_ref, ga_ref, gb_ref, gn_ref, dc_ref, doa_ref, dob_ref, dgn_ref):
        i = pl.program_id(0)
        gn = gn_ref[...]
        dyv = lax.dot_general(d_ref[...], wo_ref[...], _NT, preferred_element_type=F32)
        oa = oa_ref[...]
        ob = ob_ref[...]
        gz = gz_ref[...]
        rg = rg_ref[...]
        sa = _sig(ga_ref[...])
        sb = _sig(gb_ref[...])
        dya = dyv * sa
        dyb = dyv * sb
        sgz = _sig(gz)
        szz = gz * sgz
        dgn = jnp.zeros((1, GDN_D), F32)
        ya = []
        dgz = []
        for j in range(GDN_H):
            sl = slice(j * GDN_D, (j + 1) * GDN_D)
            seg = oa[:, sl]
            r = lax.rsqrt(jnp.mean(seg * seg, axis=-1, keepdims=True) + EPS)
            xh = seg * r
            oan = xh * gn
            ya.append(oan * szz[:, sl])
            dgz.append(dya[:, sl] * oan * (sgz[:, sl] * (1.0 + gz[:, sl] * (1.0 - sgz[:, sl]))))
            doan = dya[:, sl] * szz[:, sl]
            dgn = dgn + jnp.sum(doan * xh, axis=0, keepdims=True)
            dxh = doan * gn
            doa_ref[:, sl] = r * (dxh - xh * jnp.mean(dxh * xh, axis=-1, keepdims=True))
        ya = jnp.concatenate(ya, axis=1)
        srg = _sig(rg)
        srr = rg * srg
        yb = []
        drg = []
        for j in range(RET_H):
            sl = slice(j * RET_D, (j + 1) * RET_D)
            seg = ob[:, sl]
            r = lax.rsqrt(jnp.mean(seg * seg, axis=-1, keepdims=True) + EPS)
            xh = seg * r
            yb.append(xh * srr[:, sl])
            drg.append(dyb[:, sl] * xh * (srg[:, sl] * (1.0 + rg[:, sl] * (1.0 - srg[:, sl]))))
            dxh = dyb[:, sl] * srr[:, sl]
            dob_ref[:, sl] = r * (dxh - xh * jnp.mean(dxh * xh, axis=-1, keepdims=True))
        yb = jnp.concatenate(yb, axis=1)
        dc_ref[:, 0:D_MODEL] = jnp.concatenate(dgz, axis=1).astype(BF16)
        dc_ref[:, D_MODEL:2 * D_MODEL] = jnp.concatenate(drg, axis=1).astype(BF16)
        dc_ref[:, 2 * D_MODEL:3 * D_MODEL] = (dyv * ya * sa * (1.0 - sa)).astype(BF16)
        dc_ref[:, 3 * D_MODEL:] = (dyv * yb * sb * (1.0 - sb)).astype(BF16)

        @pl.when(i == 0)
        def _():
            dgn_ref[...] = dgn

        @pl.when(i > 0)
        def _():
            dgn_ref[...] += dgn

    col = _merge_specs(tr)
    return pl.pallas_call(
        body, grid=(Lp // tr,),
        in_specs=[col(0), pl.BlockSpec((D_MODEL, D_MODEL), lambda i: (0, 0), pipeline_mode=pl.Buffered(1)),
                  col(0), col(0), col(6), col(7), col(8), col(9), pl.BlockSpec((1, GDN_D), lambda i: (0, 0))],
        out_specs=[pl.BlockSpec((tr, 4 * D_MODEL), lambda i: (i, 0)), col(0), col(0),
                   pl.BlockSpec((1, GDN_D), lambda i: (0, 0))],
        out_shape=[jax.ShapeDtypeStruct((Lp, 4 * D_MODEL), BF16), jax.ShapeDtypeStruct((Lp, D_MODEL), F32),
                   jax.ShapeDtypeStruct((Lp, D_MODEL), F32), jax.ShapeDtypeStruct((1, GDN_D), F32)],
        name="merge_bwd")(dh1b, w_out, o_a, o_b, proj_m, proj_m, proj_m, proj_m, gnorm)


def _ffn_act(up, conv_w, conv_b):
    Lp = up.shape[0]
    tr = _tile(Lp, 192, 16)
    W2 = 2 * D_FF

    def body(main_ref, prev_ref, w_ref, b_ref, act_ref, u_ref):
        i = pl.program_id(0)
        prev = jnp.where(i > 0, prev_ref[...], 0.0)
        ext = jnp.concatenate([prev, main_ref[...]], axis=0)
        u = _taps(_shifted(ext, range(8 - (FFN_CONV - 1), 9)), w_ref[...], tr, b_ref[...])
        a = u[:, :D_FF]
        act_ref[...] = (a * _sig(a) * u[:, D_FF:]).astype(BF16)
        u_ref[...] = u

    return pl.pallas_call(
        body, grid=(Lp // tr,),
        in_specs=[pl.BlockSpec((tr, W2), lambda i: (i, 0)), _halo_prev(tr, W2),
                  pl.BlockSpec((FFN_CONV, W2), lambda i: (0, 0)), pl.BlockSpec((1, W2), lambda i: (0, 0))],
        out_specs=[pl.BlockSpec((tr, D_FF), lambda i: (i, 0)), pl.BlockSpec((tr, W2), lambda i: (i, 0))],
        out_shape=[jax.ShapeDtypeStruct((Lp, D_FF), BF16), jax.ShapeDtypeStruct((Lp, W2), F32)],
        name="ffn_act")(up, up, conv_w, conv_b)


def _ffn_act_bwd(up, u, dh2b, w_down, conv_w):
    Lp = up.shape[0]
    tr = _tile(Lp, 96, 16)
    W2 = 2 * D_FF
    te = tr + 8

    def body(up_ref, u_ref, un_ref, d_ref, dn_ref, wd_ref, w_ref, dup_ref, acc_ref):
        i = pl.program_id(0)
        w = w_ref[...]
        ue = jnp.concatenate([u_ref[...], un_ref[...]], axis=0)
        a = ue[:, :D_FF]
        b = ue[:, D_FF:]
        rowe = i * tr + lax.broadcasted_iota(jnp.int32, (te, 1), 0)
        dact = lax.dot_general(jnp.concatenate([d_ref[...], dn_ref[...]], axis=0), wd_ref[...], _NT,
                               preferred_element_type=F32)[0:te, :]
        dae = jnp.where(rowe < Lp, dact, 0.0)
        sg = _sig(a)
        du = jnp.concatenate([dae * b * (sg * (1.0 + a * (1.0 - sg))), dae * (a * sg)], axis=1)
        dus = _shifted(du, range(FFN_CONV - 1, -1, -1))
        dup_ref[...] = _taps(dus, w, tr).astype(BF16)
        upm = up_ref[...]
        rows = [jnp.sum(dus[kk][0:tr, :] * upm, axis=0, keepdims=True) for kk in range(FFN_CONV)]
        rows.append(jnp.sum(du[0:tr, :], axis=0, keepdims=True))
        part = jnp.concatenate(rows + [jnp.zeros((8 - len(rows), W2), F32)], axis=0)

        @pl.when(i == 0)
        def _():
            acc_ref[...] = part

        @pl.when(i > 0)
        def _():
            acc_ref[...] += part

    return pl.pallas_call(
        body, grid=(Lp // tr,),
        in_specs=[pl.BlockSpec((tr, W2), lambda i: (i, 0)), pl.BlockSpec((tr, W2), lambda i: (i, 0)),
                  _halo_next(tr, W2, Lp), pl.BlockSpec((tr, D_MODEL), lambda i: (i, 0)),
                  pl.BlockSpec((16, D_MODEL), lambda i: (jnp.minimum((i + 1) * (tr // 16), Lp // 16 - 1), 0)),
                  pl.BlockSpec((D_FF, D_MODEL), lambda i: (0, 0), pipeline_mode=pl.Buffered(1)),
                  pl.BlockSpec((FFN_CONV, W2), lambda i: (0, 0))],
        out_specs=[pl.BlockSpec((tr, W2), lambda i: (i, 0)), pl.BlockSpec((8, W2), lambda i: (0, 0))],
        out_shape=[jax.ShapeDtypeStruct((Lp, W2), BF16), jax.ShapeDtypeStruct((8, W2), F32)],
        name="ffn_act_bwd")(up, u, u, dh2b, dh2b, w_down, conv_w)


def _local_step(hpad, tgt, pad, wt, first_weights=None, late_weights=None, on_ffn_out_grads=None,
                on_w_in_grads=None):
    Lp = hpad.shape[0]
    first = pad + N_META
    pos = jnp.arange(Lp, dtype=F32) - float(pad)
    half = RET_D // 2
    inv = 1.0 / (ROPE_BASE ** (jnp.arange(half, dtype=F32) / half))
    ang = pos[:, None] * inv[None, :]
    cos, sin = jnp.cos(ang), jnp.sin(ang)
    tables = _ret_tables()
    gparams = jnp.zeros((8, LANES), F32).at[0, :GDN_H].set(wt["a_log"]).at[1, :GDN_H].set(wt["dt_bias"])

    hn1 = _rms_fwd(hpad, wt["norm1"], "rms1_fwd")
    if first_weights is not None:
        wt = {**wt, **first_weights(hn1)}
    proj_m = _mm_nn(hn1, wt["w_main_t"], bt=True, name="proj_main")
    proj_s = _mm_nn(hn1, wt["w_small_t"], bt=True, name="proj_small")
    qkv, gsm, conv_out = _gdn_pre(proj_m, proj_s, wt["gdn_conv_w"], gparams, pad)
    o_a, s_a, t_a = _gdn_chunk_fwd(qkv, gsm)
    o_b, s_b = _ret_chunk_fwd(proj_m, cos, sin, tables)
    y = _merge_fwd(o_a, o_b, proj_m, wt["gdn_norm"])
    if late_weights is not None:
        wt = {**wt, **late_weights(y)}
    h1, hn2 = _mm_rms_fwd(_Producer(y, wt["w_out"], hpad), wt["norm2"], "out_proj_rms2")
    up = _mm_nn(hn2, wt["w_up_t"], bt=True, name="ffn_up")
    act, u_ffn = _ffn_act(up, wt["ffn_conv_w"], wt["ffn_conv_b"])
    lossvec, dh2, dh2b, d_norm_f = _final(_Producer(act, wt["w_down"], h1), wt["norm_f"], tgt, first)

    d_w_down = _mm_tn(act, dh2b, name="dw_down")
    dup, ffn_rows = _ffn_act_bwd(up, u_ffn, dh2b, wt["w_down"], wt["ffn_conv_w"])
    d_w_up_t = _mm_tn(dup, hn2, name="dw_up")
    dh1, dh1b, d_norm2 = _rms_bwd(h1, wt["norm2"], _Producer(dup, wt["w_up_t"]), dh2, pad, "d_hn2_rms2_bwd")

    d_w_out = _mm_tn(y, dh1b, name="dw_out")
    gnorm = wt["gdn_norm"]
    if on_ffn_out_grads is not None:
        gnorm = gnorm + on_ffn_out_grads(d_w_down, d_w_up_t, d_w_out)[0:1, :]
    d_c, do_a, do_b, d_gnorm = _merge_bwd(dh1b, wt["w_out"], o_a, o_b, proj_m, gnorm)
    drq, drk, drv = _ret_chunk_bwd(proj_m, cos, sin, tables, do_b, s_b)
    dq, dk, dv, dgs = _gdn_chunk_bwd(qkv, gsm, do_a, s_a, t_a)
    d_a, d_s, conv_rows, gp_rows = _gdn_pre_bwd(proj_m, conv_out, proj_s, wt["gdn_conv_w"], gparams, dq, dk, dv, dgs,
                                                pad)

    wmt = wt["w_main_t"]
    segs = [(d_a, 0, 3 * D_MODEL), (drq, 3 * D_MODEL, D_MODEL), (drk, 4 * D_MODEL, D_MODEL),
            (drv, 5 * D_MODEL, D_MODEL), (d_c, 6 * D_MODEL, 4 * D_MODEL)]
    pa, prq, prk, prv, pc = [_mm_tn(d, hn1, name="dw_in_%d" % i) for i, (d, _, _) in enumerate(segs)]
    ps = _mm_tn(d_s, hn1, name="dw_in_small")
    d_w_in_t = jnp.concatenate([pa, pc[:D_MODEL], ps[:2 * GDN_H], prq, prk, prv, pc[D_MODEL:]], axis=0)
    w_small_t = wt["w_small_t"]
    if on_w_in_grads is not None:
        w_small_t = w_small_t + on_w_in_grads(d_w_in_t)[0:1, 0:1].astype(w_small_t.dtype)
    dhn1 = _mm_nn(d_s, w_small_t, name="d_hn1_small")
    for i, (d, off, width) in enumerate(segs[:-1]):
        dhn1 = _mm_nn(d, wmt[off:off + width], res=dhn1, name="d_hn1_%d" % i)
    d, off, width = segs[-1]
    dh0, _, d_norm1 = _rms_bwd(hpad, wt["norm1"], _Producer(d, wmt[off:off + width], dhn1), dh1, pad,
                               "d_hn1_rms1_bwd")

    grads = {
        "norm1": d_norm1, "w_in_t": d_w_in_t, "gdn_conv_w": conv_rows[:GDN_CONV],
        "a_log": gp_rows[0, :GDN_H], "dt_bias": gp_rows[1, :GDN_H], "gdn_norm": d_gnorm, "w_out": d_w_out,
        "norm2": d_norm2, "w_up_t": d_w_up_t, "ffn_conv_w": ffn_rows[:FFN_CONV],
        "ffn_conv_b": ffn_rows[FFN_CONV:FFN_CONV + 1], "w_down": d_w_down, "norm_f": d_norm_f,
    }
    return lossvec, dh0, grads


def _peer(k):
    ix, iy, ic = lax.axis_index("x"), lax.axis_index("y"), lax.axis_index("c")
    px = 1 - ix if (k >> 2) & 1 else ix
    py = 1 - iy if (k >> 1) & 1 else iy
    pc = 1 - ic if k & 1 else ic
    return (px, py, pc), 4 * px + 2 * py + pc


def _comm_call(body, n, out_shapes, name, args):
    hbm = pl.BlockSpec(memory_space=pl.ANY)
    return pl.pallas_call(
        body, out_shape=out_shapes, in_specs=[hbm] * n, out_specs=[hbm] * n,
        scratch_shapes=[pltpu.SemaphoreType.DMA((n, N_DEV - 1)), pltpu.SemaphoreType.DMA((n, N_DEV - 1)),
                        pltpu.SemaphoreType.DMA((n,))],
        name=name)(*args)


def _all_gather(xs, name):
    n = len(xs)

    def body(*refs):
        x_refs, out_refs = refs[:n], refs[n:2 * n]
        send_sems, recv_sems, local_sems = refs[2 * n:]
        _, me = _peer(0)
        pending = []
        for i in range(n):
            local = pltpu.make_async_copy(x_refs[i], out_refs[i].at[me], local_sems.at[i])
            local.start()
            pending.append(local)
        sends = []
        for i in range(n):
            for k in range(1, N_DEV):
                dev, _ = _peer(k)
                cp = pltpu.make_async_remote_copy(
                    src_ref=x_refs[i], dst_ref=out_refs[i].at[me], send_sem=send_sems.at[i, k - 1],
                    recv_sem=recv_sems.at[i, k - 1], device_id=dev, device_id_type=MESH_T)
                cp.start()
                sends.append(cp)
        for i in range(n):
            for k in range(1, N_DEV):
                dev, idx = _peer(k)
                pltpu.make_async_remote_copy(
                    src_ref=x_refs[i], dst_ref=out_refs[i].at[idx], send_sem=send_sems.at[i, k - 1],
                    recv_sem=recv_sems.at[i, k - 1], device_id=dev, device_id_type=MESH_T).wait_recv()
        for cp in sends:
            cp.wait_send()
        for local in pending:
            local.wait()

    out_shapes = [jax.ShapeDtypeStruct((N_DEV,) + a.shape, a.dtype) for a in xs]
    return _comm_call(body, n, out_shapes, name, xs)


def _all_to_all(gs, name):
    n = len(gs)

    def body(*refs):
        g_refs, out_refs = refs[:n], refs[n:2 * n]
        send_sems, recv_sems, local_sems = refs[2 * n:]
        _, me = _peer(0)
        pending = []
        for i in range(n):
            local = pltpu.make_async_copy(g_refs[i].at[me], out_refs[i].at[0], local_sems.at[i])
            local.start()
            pending.append(local)
        sends = []
        for i in range(n):
            for k in range(1, N_DEV):
                dev, idx = _peer(k)
                cp = pltpu.make_async_remote_copy(
                    src_ref=g_refs[i].at[idx], dst_ref=out_refs[i].at[k], send_sem=send_sems.at[i, k - 1],
                    recv_sem=recv_sems.at[i, k - 1], device_id=dev, device_id_type=MESH_T)
                cp.start()
                sends.append(cp)
        for cp in sends:
            cp.wait_recv()
        for cp in sends:
            cp.wait_send()
        for local in pending:
            local.wait()

    out_shapes = [jax.ShapeDtypeStruct(g.shape, g.dtype) for g in gs]
    return _comm_call(body, n, out_shapes, name, gs)


_SPLIT_RELATIONS = {"gather": tuple(range(1, N_DEV)), "a2a": tuple(range(1, N_DEV)), "chip": (1, 2, 4, 6),
                    "forward": (2, 4, 6)}


def _split_copies(kind, src_refs, land_refs, send_sems, recv_sems, local_sems, with_recv):
    n = len(land_refs)
    rels = _SPLIT_RELATIONS[kind]
    _, me = _peer(0)
    locals_, remotes = [], []
    for i in range(n):
        if kind in ("gather", "chip"):
            locals_.append(pltpu.make_async_copy(src_refs[i], land_refs[i].at[me], local_sems.at[i]))
        elif kind == "a2a":
            locals_.append(pltpu.make_async_copy(src_refs[i].at[me], land_refs[i].at[0], local_sems.at[i]))
        for jj, k in enumerate(rels):
            dev, idx = _peer(k)
            if kind in ("gather", "chip"):
                src, dst, mine = src_refs[i], land_refs[i].at[me], land_refs[i].at[idx]
            elif kind == "a2a":
                src, dst, mine = src_refs[i].at[idx], land_refs[i].at[k], land_refs[i].at[k]
            else:
                dev, _ = _peer(1)
                _, came = _peer(k + 1)
                src, dst, mine = land_refs[i].at[idx], land_refs[i].at[idx], land_refs[i].at[came]
            j = i * len(rels) + jj
            send = pltpu.make_async_remote_copy(
                src_ref=src, dst_ref=dst, send_sem=send_sems.at[j], recv_sem=recv_sems.at[j],
                device_id=dev, device_id_type=MESH_T)
            recv = pltpu.make_async_remote_copy(
                src_ref=src, dst_ref=mine, send_sem=send_sems.at[j], recv_sem=recv_sems.at[j],
                device_id=dev, device_id_type=MESH_T) if with_recv else None
            remotes.append((send, recv))
    return locals_, remotes


_HBM = pl.BlockSpec(memory_space=pltpu.HBM)
_SEM = pl.BlockSpec(memory_space=pltpu.SEMAPHORE)
_ANY = pl.BlockSpec(memory_space=pl.ANY)


def _split_start(srcs, kind, name, after):
    n = len(srcs)
    if kind == "forward":
        arrays = list(srcs)
    else:
        gathers = kind in ("gather", "chip")
        arrays = list(srcs) + [lax.empty(((N_DEV,) + a.shape) if gathers else a.shape, a.dtype) for a in srcs]
    na = len(arrays)

    def body(*refs):
        src_refs, land_refs = refs[:n], refs[na - n:na]
        send_sems, recv_sems, local_sems = refs[na + 1:na + 4]
        token = refs[-1]
        locals_, remotes = _split_copies(kind, src_refs, land_refs, send_sems, recv_sems, local_sems, False)
        for cp in locals_:
            cp.start()
        for send, _ in remotes:
            send.start()
        token[...] = jnp.zeros_like(token)

    ncp = n * len(_SPLIT_RELATIONS[kind])
    sems = (pltpu.SemaphoreType.DMA((ncp,)), pltpu.SemaphoreType.DMA((ncp,)), pltpu.SemaphoreType.DMA((n,)))
    thru = tuple(pltpu.HBM(a.shape, a.dtype) for a in arrays)
    outs = pl.pallas_call(
        body, name=name,
        out_shape=sems + thru + (jax.ShapeDtypeStruct((8, LANES), F32),),
        in_specs=[_HBM] * na + [_ANY],
        out_specs=[_SEM] * 3 + [_HBM] * na + [pl.BlockSpec(memory_space=pltpu.VMEM)],
        input_output_aliases={i: 3 + i for i in range(na)},
        compiler_params=pltpu.CompilerParams(has_side_effects=pltpu.SideEffectType.DATAFLOW_SIDE_EFFECTING),
    )(*[pltpu.with_memory_space_constraint(a, pltpu.HBM) for a in arrays], after)
    return (kind, n, outs[:3], outs[3:3 + na]), outs[-1]


def _split_wait(handle, name, after):
    kind, n, sems, thru = handle
    na = len(thru)

    def body(*refs):
        src_refs, land_refs = refs[:n], refs[na - n:na]
        send_sems, recv_sems, local_sems = refs[na:na + 3]
        locals_, remotes = _split_copies(kind, src_refs, land_refs, send_sems, recv_sems, local_sems, True)
        for send, recv in remotes:
            send.wait_send()
            recv.wait_recv()
        for cp in locals_:
            cp.wait()

    outs = pl.pallas_call(
        body, name=name, out_shape=tuple(pltpu.HBM(a.shape, a.dtype) for a in thru),
        in_specs=[_HBM] * na + [_SEM] * 3 + [_ANY], out_specs=[_HBM] * na,
        input_output_aliases={i: i for i in range(na)},
        compiler_params=pltpu.CompilerParams(has_side_effects=pltpu.SideEffectType.DATAFLOW_SIDE_EFFECTING),
    )(*thru, *sems, after)
    return list(outs[na - n:])


def _adamw(gslabs, w, m, v, name):
    R, Cw = w.shape
    if R % 8 == 0:
        tr, tc = _tile(R, 64 if Cw > 1024 else 128, 8), Cw
    else:
        tr, tc = R, LANES
    c1 = 1.0 - ADAM_B1 ** ADAM_STEP
    c2 = 1.0 - ADAM_B2 ** ADAM_STEP

    def body(g_ref, w_ref, m_ref, v_ref, go_ref, d_ref, mo_ref, vo_ref):
        g = g_ref[0].astype(F32)
        for k in range(1, N_DEV):
            g = g + g_ref[k].astype(F32)
        mn = ADAM_B1 * m_ref[...] + (1.0 - ADAM_B1) * g
        vn = ADAM_B2 * v_ref[...] + (1.0 - ADAM_B2) * (g * g)
        m_hat = mn / c1
        v_hat = vn / c2
        go_ref[...] = g
        d_ref[...] = -ADAM_LR * (m_hat / (jnp.sqrt(v_hat) + ADAM_EPS) + ADAM_WD * w_ref[...])
        mo_ref[...] = mn
        vo_ref[...] = vn

    blk = pl.BlockSpec((tr, tc), lambda i, j: (i, j))
    return pl.pallas_call(
        body, grid=(R // tr, Cw // tc),
        in_specs=[pl.BlockSpec((N_DEV, tr, tc), lambda i, j: (0, i, j)), blk, blk, blk],
        out_specs=[blk] * 4, out_shape=[jax.ShapeDtypeStruct((R, Cw), F32)] * 4, name=name)(gslabs, w, m, v)


def _pack(arrs, row_mult, dtype=F32):
    parts = []
    total = 0
    for a in arrs:
        f = a.reshape(-1).astype(dtype)
        n = -(-f.shape[0] // 1024) * 1024
        parts.append(jnp.pad(f, (0, n - f.shape[0])))
        total += n
    rows = total // LANES
    rows_p = -(-rows // row_mult) * row_mult
    flat = jnp.concatenate(parts)
    flat = jnp.pad(flat, (0, rows_p * LANES - total))
    return flat.reshape(rows_p, LANES)


def _unpack(packed, shapes):
    lead = packed.shape[:-2]
    flat = packed.reshape(lead + (-1,))
    out = []
    off = 0
    for s in shapes:
        n = int(np.prod(s))
        out.append(flat[..., off:off + n].reshape(lead + tuple(s)))
        off += -(-n // 1024) * 1024
    return out


def _gather_cols(stacked):
    d, r, c = stacked.shape
    return stacked.transpose(1, 0, 2).reshape(r, d * c)


def _scatter_cols(full):
    r, n = full.shape
    return full.reshape(r, N_DEV, n // N_DEV).transpose(1, 0, 2)


def kernel(x, meta, norm1, w_in, gdn_conv_w, gdn_a_log, gdn_dt_bias, gdn_norm, w_out, norm2, w_ffn_up, ffn_conv_w, ffn_conv_b, w_ffn_down, norm_f, loss_target, m_meta, m_norm1, m_w_in, m_gdn_conv_w, m_gdn_a_log, m_gdn_dt_bias, m_gdn_norm, m_w_out, m_norm2, m_w_ffn_up, m_ffn_conv_w, m_ffn_conv_b, m_w_ffn_down, m_norm_f, v_meta, v_norm1, v_w_in, v_gdn_conv_w, v_gdn_a_log, v_gdn_dt_bias, v_gdn_norm, v_w_out, v_norm2, v_w_ffn_up, v_ffn_conv_w, v_ffn_conv_b, v_w_ffn_down, v_norm_f):
    S = x.shape[1]
    L = N_META + S
    pad = (-L) % CHUNK
    Lp = L + pad

    tr_ = lambda a: jnp.swapaxes(a[0], 0, 1)
    big = [tr_(w_in), w_out[0], tr_(w_ffn_up), w_ffn_down[0]]
    small = [meta, gdn_conv_w, ffn_conv_w]
    small_all, = _all_gather([_pack(small, 8)], "gather_small_weights")
    first, first_token = _split_start([big[0].astype(BF16)], "chip", "gather_w_in_start", small_all)
    late, late_token = _split_start([a.astype(BF16) for a in big[1:]], "gather", "gather_late_start", first_token)

    def first_weights(after):
        half = _split_wait(first, "gather_w_in_wait", after)
        second, second_token = _split_start(half, "forward", "gather_w_in_forward_start", after)
        w_in_s, = _split_wait(second, "gather_w_in_forward_wait", second_token)
        w_in_t = w_in_s.reshape(_O_END, D_MODEL)
        w_main_t = jnp.concatenate([w_in_t[_O_GQ:_O_GZ], w_in_t[_O_RQ:_O_RG], w_in_t[_O_GZ:_O_GA],
                                    w_in_t[_O_RG:_O_END]], axis=0)
        return {"w_main_t": w_main_t, "w_small_t": jnp.pad(w_in_t[_O_GA:_O_RQ], ((0, LANES - 2 * GDN_H), (0, 0)))}

    def late_weights(after):
        w_out_s, w_up_s, w_down_s = _split_wait(late, "gather_late_wait", after)
        return {"w_out": w_out_s.reshape(D_MODEL, D_MODEL), "w_up_t": w_up_s.reshape(2 * D_FF, D_MODEL),
                "w_down": w_down_s.reshape(D_FF, D_MODEL)}

    meta_s, gconv_s, fconv_s = _unpack(small_all, [a.shape for a in small])
    wt = {
        "norm1": norm1 + jnp.tile(late_token[0:1, :], (1, D_MODEL // LANES)),
        "gdn_conv_w": _gather_cols(gconv_s[:, 0]), "a_log": gdn_a_log[0], "dt_bias": gdn_dt_bias[0],
        "gdn_norm": gdn_norm, "norm2": norm2, "ffn_conv_w": _gather_cols(fconv_s[:, 0]), "ffn_conv_b": ffn_conv_b,
        "norm_f": norm_f.reshape(1, D_MODEL),
    }
    meta_f = _gather_cols(meta_s)

    pending = {}

    def on_ffn_out_grads(d_w_down, d_w_up_t, d_w_out):
        srcs = [d_w_out.reshape(N_DEV, D_MODEL // N_DEV, D_MODEL), d_w_up_t.reshape(N_DEV, 2 * D_FF // N_DEV, D_MODEL),
                d_w_down.reshape(N_DEV, D_FF // N_DEV, D_MODEL)]
        pending["ffn_out"], token = _split_start(srcs, "a2a", "exchange_ffn_out_start", d_w_out)
        return token

    def on_w_in_grads(d_w_in_t):
        slabs = d_w_in_t.astype(BF16).reshape(N_DEV, _O_END // N_DEV, D_MODEL)
        pending["w_in"], token = _split_start([slabs], "a2a", "exchange_w_in_start", d_w_in_t)
        return token

    hpad = jnp.concatenate([jnp.zeros((pad, D_MODEL), F32), meta_f, x[0]], axis=0)
    tgt = jnp.concatenate([jnp.zeros((pad + N_META, D_MODEL), F32), loss_target[0]], axis=0)
    lossvec, dh0, gr = _local_step(hpad, tgt, pad, wt, first_weights, late_weights, on_ffn_out_grads, on_w_in_grads)

    loss = lax.psum(jnp.sum(lossvec), ("x", "y", "c"))
    grad_x = dh0[pad + N_META:][None]

    big_m = [tr_(m_w_in), m_w_out[0], tr_(m_w_ffn_up), m_w_ffn_down[0]]
    big_v = [tr_(v_w_in), v_w_out[0], tr_(v_w_ffn_up), v_w_ffn_down[0]]
    slabs_ffn_out = _split_wait(pending["ffn_out"], "exchange_ffn_out_wait", dh0)
    big_out = [None] + [_adamw(slabs_ffn_out[i - 1], big[i], big_m[i], big_v[i], "adamw_big_%d" % i)
                        for i in range(1, len(big))]
    g_sm = [_scatter_cols(dh0[pad:pad + N_META]), _scatter_cols(gr["gdn_conv_w"]), _scatter_cols(gr["ffn_conv_w"])]
    g_small = jnp.stack([_pack([g[d] for g in g_sm], 8) for d in range(N_DEV)])
    slabs_small, = _all_to_all([g_small], "exchange_small_gradients")
    small_out = _adamw(slabs_small, _pack(small, 8), _pack([m_meta, m_gdn_conv_w, m_ffn_conv_w], 8),
                       _pack([v_meta, v_gdn_conv_w, v_ffn_conv_w], 8), "adamw_small_sharded")
    small_un = [_unpack(o, [a.shape for a in small]) for o in small_out]
    rep_w = [norm1, gdn_a_log, gdn_dt_bias, gdn_norm, norm2, ffn_conv_b, norm_f]
    rep_m = [m_norm1, m_gdn_a_log, m_gdn_dt_bias, m_gdn_norm, m_norm2, m_ffn_conv_b, m_norm_f]
    rep_v = [v_norm1, v_gdn_a_log, v_gdn_dt_bias, v_gdn_norm, v_norm2, v_ffn_conv_b, v_norm_f]
    rep_g = [gr["norm1"], gr["a_log"], gr["dt_bias"], gr["gdn_norm"], gr["norm2"], gr["ffn_conv_b"], gr["norm_f"]]
    rep_slabs, = _all_gather([_pack(rep_g, 8)], "gather_small_gradients")
    rep_out = _adamw(rep_slabs, _pack(rep_w, 8), _pack(rep_m, 8), _pack(rep_v, 8), "adamw_replicated")
    rep_shapes = [a.shape for a in rep_w]
    rp_g, rp_d, rp_nm, rp_nv = [_unpack(o, rep_shapes) for o in rep_out]

    slabs_w_in, = _split_wait(pending["w_in"], "exchange_w_in_wait", rep_out[0])
    big_out[0] = _adamw(slabs_w_in, big[0], big_m[0], big_v[0], "adamw_big_0")
    back = lambda a: jnp.swapaxes(a, 0, 1)[None]
    sh_g, sh_d, sh_nm, sh_nv = [
        [small_un[j][0], back(big_out[0][j]), small_un[j][1], big_out[1][j][None], back(big_out[2][j]),
         small_un[j][2], big_out[3][j][None]] for j in range(4)]

    def order(sh, rp):
        return [sh[0], rp[0], sh[1], sh[2], rp[1], rp[2], rp[3], sh[3], rp[4], sh[4], sh[5], rp[5], sh[6], rp[6]]

    return (loss, grad_x, *order(sh_g, rp_g), *order(sh_d, rp_d), *order(sh_nm, rp_nm), *order(sh_nv, rp_nv))
```

```python
import functools
import math

import numpy as np
import jax
import jax.numpy as jnp
from jax import lax
from jax.experimental import pallas as pl
from jax.experimental.pallas import tpu as pltpu

F32 = jnp.float32
BF16 = jnp.bfloat16
HI = lax.Precision.HIGHEST

D_MODEL = 1024
N_META = 16
CHUNK = 64
GDN_H = 8
GDN_D = 128
RET_H = 4
RET_D = 256
D_FF = 2816
GDN_CONV = 4
FFN_CONV = 3
ROPE_BASE = 10000.0
EPS = 1e-6
N_DEV = 8
LANES = 128
MAIN_W = 10 * 1024
_O_GQ, _O_GZ, _O_GA, _O_RQ, _O_RG, _O_GATE, _O_END = 0, 3072, 4096, 4112, 7184, 8208, 10256

ADAM_LR = 0.001
ADAM_B1 = 0.9
ADAM_B2 = 0.999
ADAM_EPS = 1e-08
ADAM_WD = 0.01
ADAM_STEP = 10

MESH_T = pl.DeviceIdType.MESH


def _tile(n, target, mult):
    best = None
    for d in range(mult, min(n, target) + 1, mult):
        if n % d == 0:
            best = d
    assert best is not None, (n, target, mult)
    return best


def _sig(x):
    return 1.0 / (1.0 + jnp.exp(-x))


def _d(a, b):
    return jnp.dot(a.astype(BF16), b.astype(BF16), preferred_element_type=F32)


def _dnt(a, b):
    return lax.dot_general(a.astype(BF16), b.astype(BF16), (((1,), (1,)), ((), ())), preferred_element_type=F32)


def _dtn(a, b):
    return lax.dot_general(a.astype(BF16), b.astype(BF16), (((0,), (0,)), ((), ())), preferred_element_type=F32)


def _dx(a, b):
    return jnp.dot(a, b, preferred_element_type=F32, precision=HI)


def _dxnt(a, b):
    return lax.dot_general(a, b, (((1,), (1,)), ((), ())), preferred_element_type=F32, precision=HI)


def _dxtn(a, b):
    return lax.dot_general(a, b, (((0,), (0,)), ((), ())), preferred_element_type=F32, precision=HI)


def _split(a):
    hi = a.astype(BF16)
    return hi, (a - hi.astype(F32)).astype(BF16)


def _d3g(a, b, dims):
    ah, al = _split(a)
    bh, bl = _split(b)
    f = functools.partial(lax.dot_general, dimension_numbers=dims, preferred_element_type=F32)
    if dims[0][0] == (1,):
        rows = a.shape[0]
        both = f(jnp.concatenate([ah, al], axis=0), bh)
        return both[:rows] + (f(ah, bl) + both[rows:])
    return f(ah, bh) + (f(ah, bl) + f(al, bh))


_NN = (((1,), (0,)), ((), ()))
_NT = (((1,), (1,)), ((), ()))
_TN = (((0,), (0,)), ((), ()))


def _rowsum(x):
    return jnp.sum(x, axis=1, keepdims=True)


def _allsum(x):
    return jnp.sum(jnp.sum(x, axis=1, keepdims=True), axis=0, keepdims=True)


def _mm_nn(a, b, res=None, out_dtype=F32, bt=False, name="mm_nn"):
    M, K = a.shape
    N = b.shape[0] if bt else b.shape[1]
    tm = _tile(M, 704, 16)
    tn = _tile(N, 2816, 128)

    def body(*refs):
        if res is None:
            a_ref, b_ref, o_ref = refs
        else:
            a_ref, b_ref, r_ref, o_ref = refs
        acc = lax.dot_general(a_ref[...], b_ref[...], _NT if bt else _NN, preferred_element_type=F32)
        if res is not None:
            acc = acc + r_ref[...]
        o_ref[...] = acc.astype(out_dtype)

    b_spec = pl.BlockSpec((tn, K), lambda j, i: (j, 0)) if bt else pl.BlockSpec((K, tn), lambda j, i: (0, j))
    in_specs = [pl.BlockSpec((tm, K), lambda j, i: (i, 0)), b_spec]
    args = [a, b]
    if res is not None:
        in_specs.append(pl.BlockSpec((tm, tn), lambda j, i: (i, j)))
        args.append(res)
    return pl.pallas_call(
        body, grid=(N // tn, M // tm), in_specs=in_specs,
        out_specs=pl.BlockSpec((tm, tn), lambda j, i: (i, j)),
        out_shape=jax.ShapeDtypeStruct((M, N), out_dtype), name=name)(*args)


def _mm_nt(a, b, res=None, name="mm_nt"):
    M, Nc = a.shape
    K = b.shape[0]
    tm = _tile(M, 704, 16)
    tc = _tile(Nc, 5632, 128)

    def body(*refs):
        if res is None:
            a_ref, b_ref, o_ref = refs
        else:
            a_ref, b_ref, r_ref, o_ref = refs
        c = pl.program_id(1)
        p = lax.dot_general(a_ref[...], b_ref[...], (((1,), (1,)), ((), ())), preferred_element_type=F32)

        @pl.when(c == 0)
        def _():
            if res is None:
                o_ref[...] = p
            else:
                o_ref[...] = p + r_ref[...]

        @pl.when(c > 0)
        def _():
            o_ref[...] += p

    in_specs = [pl.BlockSpec((tm, tc), lambda i, c: (i, c)), pl.BlockSpec((K, tc), lambda i, c: (0, c))]
    args = [a, b]
    if res is not None:
        in_specs.append(pl.BlockSpec((tm, K), lambda i, c: (i, 0)))
        args.append(res)
    return pl.pallas_call(
        body, grid=(M // tm, Nc // tc), in_specs=in_specs,
        out_specs=pl.BlockSpec((tm, K), lambda i, c: (i, 0)),
        out_shape=jax.ShapeDtypeStruct((M, K), F32), name=name)(*args)


def _mm_tn(a, b, out_dtype=F32, name="mm_tn"):
    M, K = a.shape
    N = b.shape[1]
    tm = _tile(M, 2752, 16)
    tk = _tile(K, 1408, 128)
    tn = _tile(N, 1408, 128)
    steps = M // tm

    def body(a_ref, b_ref, o_ref, *scratch):
        acc = scratch[0] if scratch else o_ref
        m = pl.program_id(2)
        p = lax.dot_general(a_ref[...], b_ref[...], (((0,), (0,)), ((), ())), preferred_element_type=F32)

        @pl.when(m == 0)
        def _():
            acc[...] = p

        @pl.when(m > 0)
        def _():
            acc[...] += p

        if scratch:
            @pl.when(m == steps - 1)
            def _():
                o_ref[...] = acc[...].astype(out_dtype)

    return pl.pallas_call(
        body, grid=(K // tk, N // tn, steps),
        in_specs=[pl.BlockSpec((tm, tk), lambda kk, j, m: (m, kk)), pl.BlockSpec((tm, tn), lambda kk, j, m: (m, j))],
        out_specs=pl.BlockSpec((tk, tn), lambda kk, j, m: (kk, j)),
        out_shape=jax.ShapeDtypeStruct((K, N), out_dtype),
        scratch_shapes=[] if out_dtype == F32 else [pltpu.VMEM((tk, tn), F32)], name=name)(a, b)


def _rms_fwd(x, g, name):
    Lp = x.shape[0]
    tr = _tile(Lp, 256, 16)

    def body(x_ref, g_ref, o_ref):
        xv = x_ref[...]
        r = lax.rsqrt(jnp.mean(xv * xv, axis=-1, keepdims=True) + EPS)
        o_ref[...] = (xv * r * g_ref[...]).astype(BF16)

    return pl.pallas_call(
        body, grid=(Lp // tr,),
        in_specs=[pl.BlockSpec((tr, D_MODEL), lambda i: (i, 0)), pl.BlockSpec((1, D_MODEL), lambda i: (0, 0))],
        out_specs=pl.BlockSpec((tr, D_MODEL), lambda i: (i, 0)),
        out_shape=jax.ShapeDtypeStruct((Lp, D_MODEL), BF16), name=name)(x, g)


class _Producer:
    def __init__(self, a, b, res=None):
        self.a, self.b, self.res = a, b, res
        self.tr = _tile(a.shape[0], 704, 16)
        K = a.shape[1]
        self.args = [a, b] + ([] if res is None else [res])
        self.specs = [pl.BlockSpec((self.tr, K), lambda i: (i, 0)),
                      pl.BlockSpec((K, D_MODEL), lambda i: (0, 0), pipeline_mode=pl.Buffered(1))]
        if res is not None:
            self.specs.append(pl.BlockSpec((self.tr, D_MODEL), lambda i: (i, 0)))

    def tile(self, refs):
        acc = jnp.dot(refs[0][...], refs[1][...], preferred_element_type=F32)
        return acc if self.res is None else acc + refs[2][...]


def _mm_rms_fwd(prod, g, name):
    Lp, tr, n = prod.a.shape[0], prod.tr, len(prod.args)

    def body(*refs):
        g_ref, x_ref, o_ref = refs[n:]
        xv = prod.tile(refs[:n])
        r = lax.rsqrt(jnp.mean(xv * xv, axis=-1, keepdims=True) + EPS)
        x_ref[...] = xv
        o_ref[...] = (xv * r * g_ref[...]).astype(BF16)

    blk = pl.BlockSpec((tr, D_MODEL), lambda i: (i, 0))
    return pl.pallas_call(
        body, grid=(Lp // tr,), in_specs=prod.specs + [pl.BlockSpec((1, D_MODEL), lambda i: (0, 0))],
        out_specs=[blk, blk],
        out_shape=[jax.ShapeDtypeStruct((Lp, D_MODEL), F32), jax.ShapeDtypeStruct((Lp, D_MODEL), BF16)],
        name=name)(*prod.args, g)


def _rms_bwd(x, g, dy, dres, pad, name):
    Lp = x.shape[0]
    fused = isinstance(dy, _Producer)
    tr = dy.tr if fused else _tile(Lp, 256, 16)
    n = len(dy.args) if fused else 1

    def body(*refs):
        x_ref, g_ref, dr_ref, dx_ref, dxb_ref, dg_ref = refs[n:]
        i = pl.program_id(0)
        xv = x_ref[...]
        r = lax.rsqrt(jnp.mean(xv * xv, axis=-1, keepdims=True) + EPS)
        xh = xv * r
        dyv = dy.tile(refs[:n]) if fused else refs[0][...]
        dxh = dyv * g_ref[...]
        dx = r * (dxh - xh * jnp.mean(dxh * xh, axis=-1, keepdims=True)) + dr_ref[...]
        row = i * tr + lax.broadcasted_iota(jnp.int32, (tr, 1), 0)
        dx = jnp.where(row >= pad, dx, 0.0)
        dx_ref[...] = dx
        dxb_ref[...] = dx.astype(BF16)
        part = jnp.sum(dyv * xh, axis=0, keepdims=True)

        @pl.when(i == 0)
        def _():
            dg_ref[...] = part

        @pl.when(i > 0)
        def _():
            dg_ref[...] += part

    blk = pl.BlockSpec((tr, D_MODEL), lambda i: (i, 0))
    vec = pl.BlockSpec((1, D_MODEL), lambda i: (0, 0))
    return pl.pallas_call(
        body, grid=(Lp // tr,), in_specs=(dy.specs if fused else [blk]) + [blk, vec, blk], out_specs=[blk, blk, vec],
        out_shape=[jax.ShapeDtypeStruct((Lp, D_MODEL), F32), jax.ShapeDtypeStruct((Lp, D_MODEL), BF16),
                   jax.ShapeDtypeStruct((1, D_MODEL), F32)], name=name)(*(dy.args if fused else [dy]), x, g, dres)


def _final(h2, g, tgt, first_row):
    fused = isinstance(h2, _Producer)
    Lp = h2.a.shape[0] if fused else h2.shape[0]
    tr = h2.tr if fused else _tile(Lp, 256, 16)
    n = len(h2.args) if fused else 1

    def body(*refs):
        g_ref, t_ref, loss_ref, dx_ref, dxb_ref, dg_ref = refs[n:]
        i = pl.program_id(0)
        xv = h2.tile(refs[:n]) if fused else refs[0][...]
        gv = g_ref[...]
        r = lax.rsqrt(jnp.mean(xv * xv, axis=-1, keepdims=True) + EPS)
        xh = xv * r
        row = i * tr + lax.broadcasted_iota(jnp.int32, (tr, 1), 0)
        err = jnp.where(row >= first_row, xh * gv - t_ref[...], 0.0)
        lpart = jnp.sum(err * err, axis=0, keepdims=True) * (0.5 / D_MODEL)
        dyv = err * (1.0 / D_MODEL)
        dxh = dyv * gv
        dx = r * (dxh - xh * jnp.mean(dxh * xh, axis=-1, keepdims=True))
        dx_ref[...] = dx
        dxb_ref[...] = dx.astype(BF16)
        part = jnp.sum(dyv * xh, axis=0, keepdims=True)

        @pl.when(i == 0)
        def _():
            dg_ref[...] = part
            loss_ref[...] = lpart

        @pl.when(i > 0)
        def _():
            dg_ref[...] += part
            loss_ref[...] += lpart

    blk = pl.BlockSpec((tr, D_MODEL), lambda i: (i, 0))
    vec = pl.BlockSpec((1, D_MODEL), lambda i: (0, 0))
    return pl.pallas_call(
        body, grid=(Lp // tr,), in_specs=(h2.specs if fused else [blk]) + [vec, blk], out_specs=[vec, blk, blk, vec],
        out_shape=[jax.ShapeDtypeStruct((1, D_MODEL), F32), jax.ShapeDtypeStruct((Lp, D_MODEL), F32),
                   jax.ShapeDtypeStruct((Lp, D_MODEL), BF16), jax.ShapeDtypeStruct((1, D_MODEL), F32)],
        name="final_norm_loss")(*(h2.args if fused else [h2]), g, tgt)


def _halo_prev(tr, width, col=0):
    return pl.BlockSpec((8, width), lambda i: (jnp.maximum(i * (tr // 8) - 1, 0), col))


def _halo_next(tr, width, nrows, col=0):
    last = nrows // 8 - 1
    return pl.BlockSpec((8, width), lambda i: (jnp.minimum((i + 1) * (tr // 8), last), col))


def _shifted(x, offs):
    n = x.shape[0]
    return [x if off == 0 else pltpu.roll(x, n - off, 0) for off in offs]


def _taps(wins, w, rows, bias=None):
    acc = w[0:1, :] * wins[0][0:rows, :]
    if bias is not None:
        acc = acc + bias
    for kk in range(1, len(wins)):
        acc = acc + w[kk:kk + 1, :] * wins[kk][0:rows, :]
    return acc


def _gdn_pre(proj_m, proj_s, conv_w, gparams, pad):
    Lp = proj_m.shape[0]
    tr = _tile(Lp, 192, 64)
    W3 = 3 * D_MODEL

    def body(main_ref, prev_ref, s_ref, w_ref, gp_ref, qkv_ref, gsm_ref, c_ref):
        i = pl.program_id(0)
        prev = jnp.where(i > 0, prev_ref[...], 0.0)
        ext = jnp.concatenate([prev, main_ref[...]], axis=0)
        c = _taps(_shifted(ext, range(8 - (GDN_CONV - 1), 9)), w_ref[...], tr)
        c_ref[...] = c
        s = c * _sig(c)
        scale = GDN_D ** -0.5
        for j in range(2 * GDN_H):
            seg = s[:, j * GDN_D:(j + 1) * GDN_D]
            r = lax.rsqrt(_rowsum(seg * seg) + EPS)
            if j < GDN_H:
                r = r * scale
            qkv_ref[:, j * GDN_D:(j + 1) * GDN_D] = seg * r
        qkv_ref[:, 2 * D_MODEL:] = s[:, 2 * D_MODEL:]
        sm = s_ref[...]
        gp = gp_ref[...]
        lane = lax.broadcasted_iota(jnp.int32, sm.shape, 1)
        z = sm + gp[1:2, :]
        softplus = jnp.maximum(z, 0.0) + jnp.log(1.0 + jnp.exp(-jnp.abs(z)))
        lg = -jnp.exp(gp[0:1, :]) * softplus
        row = i * tr + lax.broadcasted_iota(jnp.int32, (tr, 1), 0)
        out = jnp.where(lane < GDN_H, lg, jnp.where(lane < 2 * GDN_H, _sig(sm), 0.0))
        gsm_ref[...] = jnp.where(row >= pad, out, 0.0)

    return pl.pallas_call(
        body, grid=(Lp // tr,),
        in_specs=[pl.BlockSpec((tr, W3), lambda i: (i, 0)), _halo_prev(tr, W3),
                  pl.BlockSpec((tr, LANES), lambda i: (i, 0)),
                  pl.BlockSpec((GDN_CONV, W3), lambda i: (0, 0)), pl.BlockSpec((8, LANES), lambda i: (0, 0))],
        out_specs=[pl.BlockSpec((tr, W3), lambda i: (i, 0)), pl.BlockSpec((tr, LANES), lambda i: (i, 0)),
                   pl.BlockSpec((tr, W3), lambda i: (i, 0))],
        out_shape=[jax.ShapeDtypeStruct((Lp, W3), F32), jax.ShapeDtypeStruct((Lp, LANES), F32),
                   jax.ShapeDtypeStruct((Lp, W3), F32)],
        name="gdn_pre")(proj_m, proj_m, proj_s, conv_w, gparams)


def _gdn_pre_bwd(proj_m, conv_out, proj_s, conv_w, gparams, dq, dk, dv, dgs, pad):
    Lp = proj_m.shape[0]
    tr = _tile(Lp, 192, 64)
    W3 = 3 * D_MODEL
    te = tr + 8

    def body(main_ref, c_ref, cn_ref, s_ref, w_ref, gp_ref,
             dq_ref, dqn_ref, dk_ref, dkn_ref, dv_ref, dvn_ref, dgs_ref,
             da_ref, ds_ref, dw_ref, dgp_ref):
        i = pl.program_id(0)
        w = w_ref[...]
        c = jnp.concatenate([c_ref[...], cn_ref[...]], axis=0)
        sg = _sig(c)
        s = c * sg
        rowe = i * tr + lax.broadcasted_iota(jnp.int32, (te, 1), 0)
        live = (rowe >= pad) & (rowe < Lp)
        dqe = jnp.concatenate([dq_ref[...], dqn_ref[...]], axis=0)
        dke = jnp.concatenate([dk_ref[...], dkn_ref[...]], axis=0)
        dve = jnp.concatenate([dv_ref[...], dvn_ref[...]], axis=0)
        scale = GDN_D ** -0.5
        parts = []
        for j in range(2 * GDN_H):
            seg = s[:, j * GDN_D:(j + 1) * GDN_D]
            r = lax.rsqrt(_rowsum(seg * seg) + EPS)
            xh = seg * r
            if j < GDN_H:
                dxh = dqe[:, j * GDN_D:(j + 1) * GDN_D] * scale
            else:
                dxh = dke[:, (j - GDN_H) * GDN_D:(j - GDN_H + 1) * GDN_D]
            parts.append(r * (dxh - xh * _rowsum(dxh * xh)))
        parts.append(dve)
        dsv = jnp.concatenate(parts, axis=1)
        dc = jnp.where(live, dsv * (sg * (1.0 + c * (1.0 - sg))), 0.0)
        dcs = _shifted(dc, range(GDN_CONV - 1, -1, -1))
        da_ref[...] = _taps(dcs, w, tr).astype(BF16)
        pm = main_ref[...]
        rows = [jnp.sum(dcs[kk][0:tr, :] * pm, axis=0, keepdims=True) for kk in range(GDN_CONV)]
        dwp = jnp.concatenate(rows + [jnp.zeros((8 - GDN_CONV, W3), F32)], axis=0)

        sm = s_ref[...]
        gp = gp_ref[...]
        lane = lax.broadcasted_iota(jnp.int32, sm.shape, 1)
        rowm = i * tr + lax.broadcasted_iota(jnp.int32, (tr, 1), 0)
        dgv = jnp.where(rowm >= pad, dgs_ref[...], 0.0)
        dlg = jnp.where(lane < GDN_H, dgv, 0.0)
        dbt = jnp.where((lane >= GDN_H) & (lane < 2 * GDN_H), dgv, 0.0)
        z = sm + gp[1:2, :]
        softplus = jnp.maximum(z, 0.0) + jnp.log(1.0 + jnp.exp(-jnp.abs(z)))
        ea = jnp.exp(gp[0:1, :])
        dz = dlg * (-ea) * _sig(z)
        dal = dlg * (-ea) * softplus
        bt = _sig(sm)
        dgb = dbt * bt * (1.0 - bt)
        ds_ref[...] = (dz + dgb).astype(BF16)
        gpp = jnp.concatenate([jnp.sum(dal, axis=0, keepdims=True), jnp.sum(dz, axis=0, keepdims=True),
                               jnp.zeros((6, LANES), F32)], axis=0)

        @pl.when(i == 0)
        def _():
            dw_ref[...] = dwp
            dgp_ref[...] = gpp

        @pl.when(i > 0)
        def _():
            dw_ref[...] += dwp
            dgp_ref[...] += gpp

    m3 = pl.BlockSpec((tr, W3), lambda i: (i, 0))
    m1 = pl.BlockSpec((tr, D_MODEL), lambda i: (i, 0))
    n1 = _halo_next(tr, D_MODEL, Lp)
    return pl.pallas_call(
        body, grid=(Lp // tr,),
        in_specs=[m3, m3, _halo_next(tr, W3, Lp), pl.BlockSpec((tr, LANES), lambda i: (i, 0)),
                  pl.BlockSpec((GDN_CONV, W3), lambda i: (0, 0)), pl.BlockSpec((8, LANES), lambda i: (0, 0)),
                  m1, n1, m1, n1, m1, n1, pl.BlockSpec((tr, LANES), lambda i: (i, 0))],
        out_specs=[m3, pl.BlockSpec((tr, LANES), lambda i: (i, 0)),
                   pl.BlockSpec((8, W3), lambda i: (0, 0)), pl.BlockSpec((8, LANES), lambda i: (0, 0))],
        out_shape=[jax.ShapeDtypeStruct((Lp, W3), BF16), jax.ShapeDtypeStruct((Lp, LANES), BF16),
                   jax.ShapeDtypeStruct((8, W3), F32), jax.ShapeDtypeStruct((8, LANES), F32)],
        name="gdn_pre_bwd")(proj_m, conv_out, conv_out, proj_s, conv_w, gparams, dq, dq, dk, dk, dv, dv, dgs)


def _gdn_gates(gs):
    ri = lax.broadcasted_iota(jnp.int32, (CHUNK, CHUNK), 0)
    ci = lax.broadcasted_iota(jnp.int32, (CHUNK, CHUNK), 1)
    tril = ri >= ci
    strict = ri > ci
    gall = _dx(tril.astype(F32), gs)
    lane8 = lax.broadcasted_iota(jnp.int32, (8, LANES), 1)
    sub8 = lax.broadcasted_iota(jnp.int32, (8, LANES), 0)
    grow = _dxnt((lane8 == sub8).astype(F32), gall)
    return gall, grow, tril, strict


def _gdn_decay(gall, grow, tril, h):
    g = gall[:, h:h + 1]
    return g, jnp.where(tril, jnp.exp(jnp.where(tril, g - grow[h:h + 1, :], 0.0)), 0.0)


def _gdn_chunk_specs(N, rev):
    cn = (lambda n: N - 1 - n) if rev else (lambda n: n)
    col = lambda j: pl.BlockSpec((CHUNK, D_MODEL), lambda n: (cn(n), j))
    gate = pl.BlockSpec((CHUNK, LANES), lambda n: (cn(n), 0))
    st = lambda a, b: pl.BlockSpec((GDN_H, None, a, b), lambda n: (0, cn(n), 0, 0))
    return col, gate, st


def _gdn_chunk_fwd(qkv, gsm):
    Lp = qkv.shape[0]
    N = Lp // CHUNK

    def body(q_ref, k_ref, v_ref, gs_ref, o_ref, sin_ref, t_ref, S):
        n = pl.program_id(0)

        @pl.when(n == 0)
        def _():
            S[...] = jnp.zeros_like(S)

        gs = gs_ref[...]
        gall, grow, tril, strict = _gdn_gates(gs)
        ri = lax.broadcasted_iota(jnp.int32, (CHUNK, CHUNK), 0)
        ci = lax.broadcasted_iota(jnp.int32, (CHUNK, CHUNK), 1)
        eye = (ri == ci).astype(F32)
        heads = range(GDN_H)
        sls = [slice(h * GDN_D, (h + 1) * GDN_D) for h in heads]
        q = [q_ref[:, sl] for sl in sls]
        k = [k_ref[:, sl] for sl in sls]
        v = [v_ref[:, sl] for sl in sls]
        s0 = [S[h] for h in heads]
        beta = [gs[:, GDN_H + h:GDN_H + h + 1] for h in heads]
        gg = [_gdn_decay(gall, grow, tril, h) for h in heads]
        g = [x[0] for x in gg]
        gam = [x[1] for x in gg]
        eg = [jnp.exp(g[h]) for h in heads]
        gl = [g[h][CHUNK - 1:CHUNK, :] for h in heads]
        kb = [k[h] * beta[h] for h in heads]
        pw = [-jnp.where(strict, _dnt(kb[h], k[h]) * gam[h], 0.0) for h in heads]
        p = [_dnt(q[h], k[h]) * gam[h] for h in heads]
        qs = [_d(q[h] * eg[h], s0[h]) for h in heads]
        t = [eye + pw[h] for h in heads]
        for _ in range(5):
            pw = [_d3g(pw[h], pw[h], _NN) for h in heads]
            t = [t[h] + _d3g(t[h], pw[h], _NN) for h in heads]
        u = [_d(t[h], v[h] * beta[h]) for h in heads]
        w = [_d(t[h], kb[h] * eg[h]) for h in heads]
        vnew = [u[h] - _d(w[h], s0[h]) for h in heads]
        for h in heads:
            o_ref[:, sls[h]] = qs[h] + _d(p[h], vnew[h])
            sin_ref[h] = s0[h]
            t_ref[h] = t[h]
            S[h] = s0[h] * jnp.exp(gl[h]) + _dtn(k[h] * jnp.exp(gl[h] - g[h]), vnew[h])

    col, gate, st = _gdn_chunk_specs(N, False)
    return pl.pallas_call(
        body, grid=(N,),
        in_specs=[col(0), col(1), col(2), gate],
        out_specs=[col(0), st(GDN_D, GDN_D), st(CHUNK, CHUNK)],
        out_shape=[jax.ShapeDtypeStruct((Lp, D_MODEL), F32), jax.ShapeDtypeStruct((GDN_H, N, GDN_D, GDN_D), F32),
                   jax.ShapeDtypeStruct((GDN_H, N, CHUNK, CHUNK), F32)],
        scratch_shapes=[pltpu.VMEM((GDN_H, GDN_D, GDN_D), F32)],
        name="gdn_chunk_fwd")(qkv, qkv, qkv, gsm)


def _gdn_chunk_bwd(qkv, gsm, do, s_in, t_in):
    Lp = qkv.shape[0]
    N = Lp // CHUNK

    def body(q_ref, k_ref, v_ref, gs_ref, do_ref, sin_ref, t_ref, dq_ref, dk_ref, dv_ref, dgs_ref, dS):
        n = pl.program_id(0)

        @pl.when(n == 0)
        def _():
            dS[...] = jnp.zeros_like(dS)

        gs = gs_ref[...]
        gall, grow, tril, strict = _gdn_gates(gs)
        lane = lax.broadcasted_iota(jnp.int32, (CHUNK, LANES), 1)
        rcol = lax.broadcasted_iota(jnp.int32, (CHUNK, 1), 0)
        ones = jnp.ones((CHUNK, LANES), F32)
        dg_all = jnp.zeros((CHUNK, LANES), F32)
        dbeta_all = jnp.zeros((CHUNK, LANES), F32)
        heads = range(GDN_H)
        sls = [slice(h * GDN_D, (h + 1) * GDN_D) for h in heads]
        H = lambda f: [f(h) for h in heads]
        q = H(lambda h: q_ref[:, sls[h]])
        k = H(lambda h: k_ref[:, sls[h]])
        v = H(lambda h: v_ref[:, sls[h]])
        dov = H(lambda h: do_ref[:, sls[h]])
        s0 = H(lambda h: sin_ref[h])
        t = H(lambda h: t_ref[h])
        dsv = H(lambda h: dS[h])
        beta = H(lambda h: gs[:, GDN_H + h:GDN_H + h + 1])
        gg = H(lambda h: _gdn_decay(gall, grow, tril, h))
        g = [x[0] for x in gg]
        gam = [x[1] for x in gg]
        eg = H(lambda h: jnp.exp(g[h]))
        egl = H(lambda h: jnp.exp(g[h][CHUNK - 1:CHUNK, :]))
        e = H(lambda h: jnp.exp(g[h][CHUNK - 1:CHUNK, :] - g[h]))
        kb = H(lambda h: k[h] * beta[h])
        kbg = H(lambda h: kb[h] * eg[h])
        vb = H(lambda h: v[h] * beta[h])
        qg = H(lambda h: q[h] * eg[h])
        kd = H(lambda h: k[h] * e[h])
        m = H(lambda h: jnp.where(strict, _dnt(kb[h], k[h]) * gam[h], 0.0))
        u = H(lambda h: _d(t[h], vb[h]))
        w = H(lambda h: _d(t[h], kbg[h]))
        p = H(lambda h: _dnt(q[h], k[h]) * gam[h])
        dqg = H(lambda h: _dnt(dov[h], s0[h]))
        kdds = H(lambda h: _d(kd[h], dsv[h]))
        qgdo = H(lambda h: _dtn(qg[h], dov[h]))
        vnew = H(lambda h: u[h] - _d(w[h], s0[h]))
        dvnew = H(lambda h: _dtn(p[h], dov[h]) + kdds[h])
        dp = H(lambda h: jnp.where(tril, _dnt(dov[h], vnew[h]), 0.0))
        dkd = H(lambda h: _dnt(vnew[h], dsv[h]))
        dw = H(lambda h: -_dnt(dvnew[h], s0[h]))
        for h in heads:
            dS[h] = qgdo[h] + egl[h] * dsv[h] - _dtn(w[h], dvnew[h])
        dvb = H(lambda h: _dtn(t[h], dvnew[h]))
        dkbg = H(lambda h: _dtn(t[h], dw[h]))
        dt = H(lambda h: _dnt(dvnew[h], vb[h]) + _dnt(dw[h], kbg[h]))
        x1 = H(lambda h: _d3g(t[h], dt[h], _TN))
        dm = H(lambda h: jnp.where(strict, -_d3g(x1[h], t[h], _NT), 0.0))
        dkk = H(lambda h: dm[h] * gam[h])
        dqk = H(lambda h: dp[h] * gam[h])
        dkb = H(lambda h: _d(dkk[h], k[h]) + eg[h] * dkbg[h])
        em = H(lambda h: dm[h] * m[h] + dp[h] * p[h])
        colsum = H(lambda h: _d3g(em[h], ones, _TN)[:, 0:1])
        for h in heads:
            dk_ref[:, sls[h]] = _dtn(dkk[h], kb[h]) + _dtn(dqk[h], q[h]) + dkd[h] * e[h] + beta[h] * dkb[h]
            dq_ref[:, sls[h]] = _d(dqk[h], k[h]) + dqg[h] * eg[h]
            dv_ref[:, sls[h]] = beta[h] * dvb[h]
        for h in heads:
            dbeta = _rowsum(k[h] * dkb[h]) + _rowsum(v[h] * dvb[h])
            z = _rowsum(kd[h] * dkd[h])
            dg = _rowsum(em[h]) - colsum[h] + _rowsum(qg[h] * dqg[h]) + _rowsum(kbg[h] * dkbg[h]) - z
            extra = _allsum(z) + egl[h] * _allsum(s0[h] * dsv[h])
            dg = dg + jnp.where(rcol == CHUNK - 1, extra, 0.0)
            dg_all = dg_all + jnp.where(lane == h, dg, 0.0)
            dbeta_all = dbeta_all + jnp.where(lane == GDN_H + h, dbeta, 0.0)
        ri = lax.broadcasted_iota(jnp.int32, (CHUNK, CHUNK), 0)
        ci = lax.broadcasted_iota(jnp.int32, (CHUNK, CHUNK), 1)
        dgs_ref[...] = _dx((ci >= ri).astype(F32), dg_all) + dbeta_all

    col, gate, st = _gdn_chunk_specs(N, True)
    return pl.pallas_call(
        body, grid=(N,),
        in_specs=[col(0), col(1), col(2), gate, col(0), st(GDN_D, GDN_D), st(CHUNK, CHUNK)],
        out_specs=[col(0), col(0), col(0), gate],
        out_shape=[jax.ShapeDtypeStruct((Lp, D_MODEL), F32)] * 3 + [jax.ShapeDtypeStruct((Lp, LANES), F32)],
        scratch_shapes=[pltpu.VMEM((GDN_H, GDN_D, GDN_D), F32)],
        name="gdn_chunk_bwd")(qkv, qkv, qkv, gsm, do, s_in, t_in)


def _rot(x, c, s):
    half = RET_D // 2
    x1 = x[:, :half]
    x2 = x[:, half:]
    return jnp.concatenate([x1 * c - x2 * s, x2 * c + x1 * s], axis=1)


def _rot_bwd(d, c, s):
    half = RET_D // 2
    d1 = d[:, :half]
    d2 = d[:, half:]
    return jnp.concatenate([d1 * c + d2 * s, d2 * c - d1 * s], axis=1)


def _ret_tables():
    hh = jnp.arange(RET_H, dtype=F32)
    lg = jnp.log(1.0 - 2.0 ** (-5.0 - hh))
    idx = jnp.arange(CHUNK, dtype=F32)
    tril = jnp.asarray(np.tril(np.ones((CHUNK, CHUNK), dtype=bool)))
    dmask = jnp.where(tril, jnp.exp((idx[:, None] - idx[None, :]) * lg[:, None, None]), 0.0)
    qdec = jnp.exp((idx[None, :] + 1.0) * lg[:, None])
    kdec = jnp.exp((CHUNK - 1.0 - idx[None, :]) * lg[:, None])
    gch = jnp.exp(CHUNK * lg)
    qdec = jnp.broadcast_to(qdec[:, :, None], (RET_H, CHUNK, RET_D))
    kdec = jnp.broadcast_to(kdec[:, :, None], (RET_H, CHUNK, RET_D))
    gch = jnp.broadcast_to(gch[:, None, None], (RET_H, 8, LANES))
    return dmask, qdec, kdec, gch


def _ret_specs(N, rev):
    cn = (lambda n: N - 1 - n) if rev else (lambda n: n)
    col = lambda j: pl.BlockSpec((CHUNK, D_MODEL), lambda n: (cn(n), j))
    tab = lambda a, b: pl.BlockSpec((RET_H, a, b), lambda n: (0, 0, 0))
    rope = pl.BlockSpec((CHUNK, LANES), lambda n: (cn(n), 0))
    st = pl.BlockSpec((RET_H, None, RET_D, RET_D), lambda n: (0, cn(n), 0, 0))
    return col, tab, rope, st


def _ret_chunk_fwd(proj_m, cos, sin, tables):
    Lp = proj_m.shape[0]
    N = Lp // CHUNK
    dmask, qdec, kdec, gch = tables

    def body(q_ref, k_ref, v_ref, c_ref, s_ref, dm_ref, qd_ref, kd_ref, g_ref, o_ref, sin_ref, S):
        n = pl.program_id(0)

        @pl.when(n == 0)
        def _():
            S[...] = jnp.zeros_like(S)

        c = c_ref[...]
        s = s_ref[...]
        heads = range(RET_H)
        sls = [slice(h * RET_D, (h + 1) * RET_D) for h in heads]
        H = lambda f: [f(h) for h in heads]
        qr = H(lambda h: _rot(q_ref[:, sls[h]], c, s))
        ks = H(lambda h: _rot(k_ref[:, sls[h]], c, s) * (RET_D ** -0.5))
        v = H(lambda h: v_ref[:, sls[h]])
        s0 = H(lambda h: S[h])
        a = H(lambda h: _dnt(qr[h], ks[h]) * dm_ref[h])
        qs = H(lambda h: _d(qr[h] * qd_ref[h], s0[h]))
        kv = H(lambda h: _dtn(ks[h] * kd_ref[h], v[h]))
        for h in heads:
            o_ref[:, sls[h]] = _d(a[h], v[h]) + qs[h]
            sin_ref[h] = s0[h]
            S[h] = s0[h] * g_ref[h, 0:1, 0:1] + kv[h]

    col, tab, rope, st = _ret_specs(N, False)
    return pl.pallas_call(
        body, grid=(N,),
        in_specs=[col(3), col(4), col(5), rope, rope,
                  tab(CHUNK, CHUNK), tab(CHUNK, RET_D), tab(CHUNK, RET_D), tab(8, LANES)],
        out_specs=[col(0), st],
        out_shape=[jax.ShapeDtypeStruct((Lp, D_MODEL), F32), jax.ShapeDtypeStruct((RET_H, N, RET_D, RET_D), F32)],
        scratch_shapes=[pltpu.VMEM((RET_H, RET_D, RET_D), F32)],
        name="ret_chunk_fwd")(proj_m, proj_m, proj_m, cos, sin, dmask, qdec, kdec, gch)


def _ret_chunk_bwd(proj_m, cos, sin, tables, do, s_in):
    Lp = proj_m.shape[0]
    N = Lp // CHUNK
    dmask, qdec, kdec, gch = tables

    def body(q_ref, k_ref, v_ref, c_ref, s_ref, dm_ref, qd_ref, kd_ref, g_ref, do_ref, sin_ref,
             dq_ref, dk_ref, dv_ref, dS):
        n = pl.program_id(0)

        @pl.when(n == 0)
        def _():
            dS[...] = jnp.zeros_like(dS)

        c = c_ref[...]
        s = s_ref[...]
        kscale = RET_D ** -0.5
        heads = range(RET_H)
        sls = [slice(h * RET_D, (h + 1) * RET_D) for h in heads]
        H = lambda f: [f(h) for h in heads]
        qr = H(lambda h: _rot(q_ref[:, sls[h]], c, s))
        ks = H(lambda h: _rot(k_ref[:, sls[h]], c, s) * kscale)
        v = H(lambda h: v_ref[:, sls[h]])
        dov = H(lambda h: do_ref[:, sls[h]])
        s0 = H(lambda h: sin_ref[h])
        dsv = H(lambda h: dS[h])
        ad = H(lambda h: _dnt(qr[h], ks[h]) * dm_ref[h])
        da = H(lambda h: _dnt(dov[h], v[h]) * dm_ref[h])
        kds = H(lambda h: _d(ks[h] * kd_ref[h], dsv[h]))
        dos = H(lambda h: _dnt(dov[h], s0[h]) * qd_ref[h])
        vds = H(lambda h: _dnt(v[h], dsv[h]) * kd_ref[h])
        qdo = H(lambda h: _dtn(qr[h] * qd_ref[h], dov[h]))
        for h in heads:
            dS[h] = dsv[h] * g_ref[h, 0:1, 0:1] + qdo[h]
        for h in heads:
            dv_ref[:, sls[h]] = (_dtn(ad[h], dov[h]) + kds[h]).astype(BF16)
            dq_ref[:, sls[h]] = _rot_bwd(_d(da[h], ks[h]) + dos[h], c, s).astype(BF16)
            dk_ref[:, sls[h]] = _rot_bwd((_dtn(da[h], qr[h]) + vds[h]) * kscale, c, s).astype(BF16)

    col, tab, rope, st = _ret_specs(N, True)
    return pl.pallas_call(
        body, grid=(N,),
        in_specs=[col(3), col(4), col(5), rope, rope,
                  tab(CHUNK, CHUNK), tab(CHUNK, RET_D), tab(CHUNK, RET_D), tab(8, LANES), col(0), st],
        out_specs=[col(0), col(0), col(0)],
        out_shape=[jax.ShapeDtypeStruct((Lp, D_MODEL), BF16)] * 3,
        scratch_shapes=[pltpu.VMEM((RET_H, RET_D, RET_D), F32)],
        name="ret_chunk_bwd")(proj_m, proj_m, proj_m, cos, sin, dmask, qdec, kdec, gch, do, s_in)


def _merge_specs(tr):
    col = lambda j: pl.BlockSpec((tr, D_MODEL), lambda i: (i, j))
    return col


def _merge_fwd(o_a, o_b, proj_m, gnorm):
    Lp = o_a.shape[0]
    tr = _tile(Lp, 192, 16)

    def body(oa_ref, ob_ref, gz_ref, rg_ref, ga_ref, gb_ref, gn_ref, y_ref):
        gn = gn_ref[...]
        oa = oa_ref[...]
        ob = ob_ref[...]
        gz = gz_ref[...]
        ya = []
        for j in range(GDN_H):
            seg = oa[:, j * GDN_D:(j + 1) * GDN_D]
            r = lax.rsqrt(jnp.mean(seg * seg, axis=-1, keepdims=True) + EPS)
            ya.append(seg * r * gn)
        ya = jnp.concatenate(ya, axis=1) * (gz * _sig(gz))
        yb = []
        for j in range(RET_H):
            seg = ob[:, j * RET_D:(j + 1) * RET_D]
            r = lax.rsqrt(jnp.mean(seg * seg, axis=-1, keepdims=True) + EPS)
            yb.append(seg * r)
        rg = rg_ref[...]
        yb = jnp.concatenate(yb, axis=1) * (rg * _sig(rg))
        y_ref[...] = (_sig(ga_ref[...]) * ya + _sig(gb_ref[...]) * yb).astype(BF16)

    col = _merge_specs(tr)
    return pl.pallas_call(
        body, grid=(Lp // tr,),
        in_specs=[col(0), col(0), col(6), col(7), col(8), col(9), pl.BlockSpec((1, GDN_D), lambda i: (0, 0))],
        out_specs=col(0), out_shape=jax.ShapeDtypeStruct((Lp, D_MODEL), BF16),
        name="merge_fwd")(o_a, o_b, proj_m, proj_m, proj_m, proj_m, gnorm)


def _merge_bwd(dh1b, w_out, o_a, o_b, proj_m, gnorm):
    Lp = o_a.shape[0]
    tr = _tile(Lp, 192, 16)

    def body(d_ref, wo_ref, oa_ref, ob_ref, gz_ref, rg_ref, ga_ref, gb_ref, gn_ref, dc_ref, doa_ref, dob_ref, dgn_ref):
        i = pl.program_id(0)
        gn = gn_ref[...]
        dyv = lax.dot_general(d_ref[...], wo_ref[...], _NT, preferred_element_type=F32)
        oa = oa_ref[...]
        ob = ob_ref[...]
        gz = gz_ref[...]
        rg = rg_ref[...]
        sa = _sig(ga_ref[...])
        sb = _sig(gb_ref[...])
        dya = dyv * sa
        dyb = dyv * sb
        sgz = _sig(gz)
        szz = gz * sgz
        dgn = jnp.zeros((1, GDN_D), F32)
        ya = []
        dgz = []
        for j in range(GDN_H):
            sl = slice(j * GDN_D, (j + 1) * GDN_D)
            seg = oa[:, sl]
            r = lax.rsqrt(jnp.mean(seg * seg, axis=-1, keepdims=True) + EPS)
            xh = seg * r
            oan = xh * gn
            ya.append(oan * szz[:, sl])
            dgz.append(dya[:, sl] * oan * (sgz[:, sl] * (1.0 + gz[:, sl] * (1.0 - sgz[:, sl]))))
            doan = dya[:, sl] * szz[:, sl]
            dgn = dgn + jnp.sum(doan * xh, axis=0, keepdims=True)
            dxh = doan * gn
            doa_ref[:, sl] = r * (dxh - xh * jnp.mean(dxh * xh, axis=-1, keepdims=True))
        ya = jnp.concatenate(ya, axis=1)
        srg = _sig(rg)
        srr = rg * srg
        yb = []
        drg = []
        for j in range(RET_H):
            sl = slice(j * RET_D, (j + 1) * RET_D)
            seg = ob[:, sl]
            r = lax.rsqrt(jnp.mean(seg * seg, axis=-1, keepdims=True) + EPS)
            xh = seg * r
            yb.append(xh * srr[:, sl])
            drg.append(dyb[:, sl] * xh * (srg[:, sl] * (1.0 + rg[:, sl] * (1.0 - srg[:, sl]))))
            dxh = dyb[:, sl] * srr[:, sl]
            dob_ref[:, sl] = r * (dxh - xh * jnp.mean(dxh * xh, axis=-1, keepdims=True))
        yb = jnp.concatenate(yb, axis=1)
        dc_ref[:, 0:D_MODEL] = jnp.concatenate(dgz, axis=1).astype(BF16)
        dc_ref[:, D_MODEL:2 * D_MODEL] = jnp.concatenate(drg, axis=1).astype(BF16)
        dc_ref[:, 2 * D_MODEL:3 * D_MODEL] = (dyv * ya * sa * (1.0 - sa)).astype(BF16)
        dc_ref[:, 3 * D_MODEL:] = (dyv * yb * sb * (1.0 - sb)).astype(BF16)

        @pl.when(i == 0)
        def _():
            dgn_ref[...] = dgn

        @pl.when(i > 0)
        def _():
            dgn_ref[...] += dgn

    col = _merge_specs(tr)
    return pl.pallas_call(
        body, grid=(Lp // tr,),
        in_specs=[col(0), pl.BlockSpec((D_MODEL, D_MODEL), lambda i: (0, 0), pipeline_mode=pl.Buffered(1)),
                  col(0), col(0), col(6), col(7), col(8), col(9), pl.BlockSpec((1, GDN_D), lambda i: (0, 0))],
        out_specs=[pl.BlockSpec((tr, 4 * D_MODEL), lambda i: (i, 0)), col(0), col(0),
                   pl.BlockSpec((1, GDN_D), lambda i: (0, 0))],
        out_shape=[jax.ShapeDtypeStruct((Lp, 4 * D_MODEL), BF16), jax.ShapeDtypeStruct((Lp, D_MODEL), F32),
                   jax.ShapeDtypeStruct((Lp, D_MODEL), F32), jax.ShapeDtypeStruct((1, GDN_D), F32)],
        name="merge_bwd")(dh1b, w_out, o_a, o_b, proj_m, proj_m, proj_m, proj_m, gnorm)


def _ffn_act(up, conv_w, conv_b):
    Lp = up.shape[0]
    tr = _tile(Lp, 192, 16)
    W2 = 2 * D_FF

    def body(main_ref, prev_ref, w_ref, b_ref, act_ref, u_ref):
        i = pl.program_id(0)
        prev = jnp.where(i > 0, prev_ref[...], 0.0)
        ext = jnp.concatenate([prev, main_ref[...]], axis=0)
        u = _taps(_shifted(ext, range(8 - (FFN_CONV - 1), 9)), w_ref[...], tr, b_ref[...])
        a = u[:, :D_FF]
        act_ref[...] = (a * _sig(a) * u[:, D_FF:]).astype(BF16)
        u_ref[...] = u

    return pl.pallas_call(
        body, grid=(Lp // tr,),
        in_specs=[pl.BlockSpec((tr, W2), lambda i: (i, 0)), _halo_prev(tr, W2),
                  pl.BlockSpec((FFN_CONV, W2), lambda i: (0, 0)), pl.BlockSpec((1, W2), lambda i: (0, 0))],
        out_specs=[pl.BlockSpec((tr, D_FF), lambda i: (i, 0)), pl.BlockSpec((tr, W2), lambda i: (i, 0))],
        out_shape=[jax.ShapeDtypeStruct((Lp, D_FF), BF16), jax.ShapeDtypeStruct((Lp, W2), F32)],
        name="ffn_act")(up, up, conv_w, conv_b)


def _ffn_act_bwd(up, u, dact, conv_w):
    Lp = up.shape[0]
    tr = _tile(Lp, 96, 16)
    W2 = 2 * D_FF
    te = tr + 8

    def body(up_ref, u_ref, un_ref, da_ref, dan_ref, w_ref, dup_ref, acc_ref):
        i = pl.program_id(0)
        w = w_ref[...]
        ue = jnp.concatenate([u_ref[...], un_ref[...]], axis=0)
        a = ue[:, :D_FF]
        b = ue[:, D_FF:]
        rowe = i * tr + lax.broadcasted_iota(jnp.int32, (te, 1), 0)
        dae = jnp.where(rowe < Lp, jnp.concatenate([da_ref[...], dan_ref[...]], axis=0), 0.0)
        sg = _sig(a)
        du = jnp.concatenate([dae * b * (sg * (1.0 + a * (1.0 - sg))), dae * (a * sg)], axis=1)
        dus = _shifted(du, range(FFN_CONV - 1, -1, -1))
        dup_ref[...] = _taps(dus, w, tr).astype(BF16)
        upm = up_ref[...]
        rows = [jnp.sum(dus[kk][0:tr, :] * upm, axis=0, keepdims=True) for kk in range(FFN_CONV)]
        rows.append(jnp.sum(du[0:tr, :], axis=0, keepdims=True))
        part = jnp.concatenate(rows + [jnp.zeros((8 - len(rows), W2), F32)], axis=0)

        @pl.when(i == 0)
        def _():
            acc_ref[...] = part

        @pl.when(i > 0)
        def _():
            acc_ref[...] += part

    return pl.pallas_call(
        body, grid=(Lp // tr,),
        in_specs=[pl.BlockSpec((tr, W2), lambda i: (i, 0)), pl.BlockSpec((tr, W2), lambda i: (i, 0)),
                  _halo_next(tr, W2, Lp), pl.BlockSpec((tr, D_FF), lambda i: (i, 0)), _halo_next(tr, D_FF, Lp),
                  pl.BlockSpec((FFN_CONV, W2), lambda i: (0, 0))],
        out_specs=[pl.BlockSpec((tr, W2), lambda i: (i, 0)), pl.BlockSpec((8, W2), lambda i: (0, 0))],
        out_shape=[jax.ShapeDtypeStruct((Lp, W2), BF16), jax.ShapeDtypeStruct((8, W2), F32)],
        name="ffn_act_bwd")(up, u, u, dact, dact, conv_w)


def _local_step(hpad, tgt, pad, wt, first_weights=None, late_weights=None, on_ffn_out_grads=None,
                on_w_in_grads=None):
    Lp = hpad.shape[0]
    first = pad + N_META
    pos = jnp.arange(Lp, dtype=F32) - float(pad)
    half = RET_D // 2
    inv = 1.0 / (ROPE_BASE ** (jnp.arange(half, dtype=F32) / half))
    ang = pos[:, None] * inv[None, :]
    cos, sin = jnp.cos(ang), jnp.sin(ang)
    tables = _ret_tables()
    gparams = jnp.zeros((8, LANES), F32).at[0, :GDN_H].set(wt["a_log"]).at[1, :GDN_H].set(wt["dt_bias"])

    hn1 = _rms_fwd(hpad, wt["norm1"], "rms1_fwd")
    if first_weights is not None:
        wt = {**wt, **first_weights(hn1)}
    proj_m = _mm_nn(hn1, wt["w_main_t"], bt=True, name="proj_main")
    proj_s = _mm_nn(hn1, wt["w_small_t"], bt=True, name="proj_small")
    qkv, gsm, conv_out = _gdn_pre(proj_m, proj_s, wt["gdn_conv_w"], gparams, pad)
    o_a, s_a, t_a = _gdn_chunk_fwd(qkv, gsm)
    o_b, s_b = _ret_chunk_fwd(proj_m, cos, sin, tables)
    y = _merge_fwd(o_a, o_b, proj_m, wt["gdn_norm"])
    if late_weights is not None:
        wt = {**wt, **late_weights(y)}
    h1, hn2 = _mm_rms_fwd(_Producer(y, wt["w_out"], hpad), wt["norm2"], "out_proj_rms2")
    up = _mm_nn(hn2, wt["w_up_t"], bt=True, name="ffn_up")
    act, u_ffn = _ffn_act(up, wt["ffn_conv_w"], wt["ffn_conv_b"])
    lossvec, dh2, dh2b, d_norm_f = _final(_Producer(act, wt["w_down"], h1), wt["norm_f"], tgt, first)

    d_w_down = _mm_tn(act, dh2b, name="dw_down")
    dact = _mm_nt(dh2b, wt["w_down"], name="d_act")
    dup, ffn_rows = _ffn_act_bwd(up, u_ffn, dact, wt["ffn_conv_w"])
    d_w_up_t = _mm_tn(dup, hn2, name="dw_up")
    dh1, dh1b, d_norm2 = _rms_bwd(h1, wt["norm2"], _Producer(dup, wt["w_up_t"]), dh2, pad, "d_hn2_rms2_bwd")

    d_w_out = _mm_tn(y, dh1b, name="dw_out")
    gnorm = wt["gdn_norm"]
    if on_ffn_out_grads is not None:
        gnorm = gnorm + on_ffn_out_grads(d_w_down, d_w_up_t, d_w_out)[0:1, :]
    d_c, do_a, do_b, d_gnorm = _merge_bwd(dh1b, wt["w_out"], o_a, o_b, proj_m, gnorm)
    drq, drk, drv = _ret_chunk_bwd(proj_m, cos, sin, tables, do_b, s_b)
    dq, dk, dv, dgs = _gdn_chunk_bwd(qkv, gsm, do_a, s_a, t_a)
    d_a, d_s, conv_rows, gp_rows = _gdn_pre_bwd(proj_m, conv_out, proj_s, wt["gdn_conv_w"], gparams, dq, dk, dv, dgs,
                                                pad)

    wmt = wt["w_main_t"]
    segs = [(d_a, 0, 3 * D_MODEL), (drq, 3 * D_MODEL, D_MODEL), (drk, 4 * D_MODEL, D_MODEL),
            (drv, 5 * D_MODEL, D_MODEL), (d_c, 6 * D_MODEL, 4 * D_MODEL)]
    pa, prq, prk, prv, pc = [_mm_tn(d, hn1, BF16, name="dw_in_%d" % i) for i, (d, _, _) in enumerate(segs)]
    ps = _mm_tn(d_s, hn1, BF16, name="dw_in_small")
    d_w_in_t = jnp.concatenate([pa, pc[:D_MODEL], ps[:2 * GDN_H], prq, prk, prv, pc[D_MODEL:]], axis=0)
    w_small_t = wt["w_small_t"]
    if on_w_in_grads is not None:
        w_small_t = w_small_t + on_w_in_grads(d_w_in_t)[0:1, 0:1].astype(w_small_t.dtype)
    dhn1 = _mm_nn(d_s, w_small_t, name="d_hn1_small")
    for i, (d, off, width) in enumerate(segs[:-1]):
        dhn1 = _mm_nn(d, wmt[off:off + width], res=dhn1, name="d_hn1_%d" % i)
    d, off, width = segs[-1]
    dh0, _, d_norm1 = _rms_bwd(hpad, wt["norm1"], _Producer(d, wmt[off:off + width], dhn1), dh1, pad,
                               "d_hn1_rms1_bwd")

    grads = {
        "norm1": d_norm1, "w_in_t": d_w_in_t, "gdn_conv_w": conv_rows[:GDN_CONV],
        "a_log": gp_rows[0, :GDN_H], "dt_bias": gp_rows[1, :GDN_H], "gdn_norm": d_gnorm, "w_out": d_w_out,
        "norm2": d_norm2, "w_up_t": d_w_up_t, "ffn_conv_w": ffn_rows[:FFN_CONV],
        "ffn_conv_b": ffn_rows[FFN_CONV:FFN_CONV + 1], "w_down": d_w_down, "norm_f": d_norm_f,
    }
    return lossvec, dh0, grads


def _peer(k):
    ix, iy, ic = lax.axis_index("x"), lax.axis_index("y"), lax.axis_index("c")
    px = 1 - ix if (k >> 2) & 1 else ix
    py = 1 - iy if (k >> 1) & 1 else iy
    pc = 1 - ic if k & 1 else ic
    return (px, py, pc), 4 * px + 2 * py + pc


def _comm_call(body, n, out_shapes, name, args):
    hbm = pl.BlockSpec(memory_space=pl.ANY)
    return pl.pallas_call(
        body, out_shape=out_shapes, in_specs=[hbm] * n, out_specs=[hbm] * n,
        scratch_shapes=[pltpu.SemaphoreType.DMA((n, N_DEV - 1)), pltpu.SemaphoreType.DMA((n, N_DEV - 1)),
                        pltpu.SemaphoreType.DMA((n,))],
        name=name)(*args)


def _all_gather(xs, name):
    n = len(xs)

    def body(*refs):
        x_refs, out_refs = refs[:n], refs[n:2 * n]
        send_sems, recv_sems, local_sems = refs[2 * n:]
        _, me = _peer(0)
        pending = []
        for i in range(n):
            local = pltpu.make_async_copy(x_refs[i], out_refs[i].at[me], local_sems.at[i])
            local.start()
            pending.append(local)
        sends = []
        for i in range(n):
            for k in range(1, N_DEV):
                dev, _ = _peer(k)
                cp = pltpu.make_async_remote_copy(
                    src_ref=x_refs[i], dst_ref=out_refs[i].at[me], send_sem=send_sems.at[i, k - 1],
                    recv_sem=recv_sems.at[i, k - 1], device_id=dev, device_id_type=MESH_T)
                cp.start()
                sends.append(cp)
        for i in range(n):
            for k in range(1, N_DEV):
                dev, idx = _peer(k)
                pltpu.make_async_remote_copy(
                    src_ref=x_refs[i], dst_ref=out_refs[i].at[idx], send_sem=send_sems.at[i, k - 1],
                    recv_sem=recv_sems.at[i, k - 1], device_id=dev, device_id_type=MESH_T).wait_recv()
        for cp in sends:
            cp.wait_send()
        for local in pending:
            local.wait()

    out_shapes = [jax.ShapeDtypeStruct((N_DEV,) + a.shape, a.dtype) for a in xs]
    return _comm_call(body, n, out_shapes, name, xs)


def _all_to_all(gs, name):
    n = len(gs)

    def body(*refs):
        g_refs, out_refs = refs[:n], refs[n:2 * n]
        send_sems, recv_sems, local_sems = refs[2 * n:]
        _, me = _peer(0)
        pending = []
        for i in range(n):
            local = pltpu.make_async_copy(g_refs[i].at[me], out_refs[i].at[0], local_sems.at[i])
            local.start()
            pending.append(local)
        sends = []
        for i in range(n):
            for k in range(1, N_DEV):
                dev, idx = _peer(k)
                cp = pltpu.make_async_remote_copy(
                    src_ref=g_refs[i].at[idx], dst_ref=out_refs[i].at[k], send_sem=send_sems.at[i, k - 1],
                    recv_sem=recv_sems.at[i, k - 1], device_id=dev, device_id_type=MESH_T)
                cp.start()
                sends.append(cp)
        for cp in sends:
            cp.wait_recv()
        for cp in sends:
            cp.wait_send()
        for local in pending:
            local.wait()

    out_shapes = [jax.ShapeDtypeStruct(g.shape, g.dtype) for g in gs]
    return _comm_call(body, n, out_shapes, name, gs)


_SPLIT_RELATIONS = {"gather": tuple(range(1, N_DEV)), "a2a": tuple(range(1, N_DEV)), "chip": (1, 2, 4, 6),
                    "forward": (2, 4, 6)}


def _split_copies(kind, src_refs, land_refs, send_sems, recv_sems, local_sems, with_recv):
    n = len(land_refs)
    rels = _SPLIT_RELATIONS[kind]
    _, me = _peer(0)
    locals_, remotes = [], []
    for i in range(n):
        if kind in ("gather", "chip"):
            locals_.append(pltpu.make_async_copy(src_refs[i], land_refs[i].at[me], local_sems.at[i]))
        elif kind == "a2a":
            locals_.append(pltpu.make_async_copy(src_refs[i].at[me], land_refs[i].at[0], local_sems.at[i]))
        for jj, k in enumerate(rels):
            dev, idx = _peer(k)
            if kind in ("gather", "chip"):
                src, dst, mine = src_refs[i], land_refs[i].at[me], land_refs[i].at[idx]
            elif kind == "a2a":
                src, dst, mine = src_refs[i].at[idx], land_refs[i].at[k], land_refs[i].at[k]
            else:
                dev, _ = _peer(1)
                _, came = _peer(k + 1)
                src, dst, mine = land_refs[i].at[idx], land_refs[i].at[idx], land_refs[i].at[came]
            j = i * len(rels) + jj
            send = pltpu.make_async_remote_copy(
                src_ref=src, dst_ref=dst, send_sem=send_sems.at[j], recv_sem=recv_sems.at[j],
                device_id=dev, device_id_type=MESH_T)
            recv = pltpu.make_async_remote_copy(
                src_ref=src, dst_ref=mine, send_sem=send_sems.at[j], recv_sem=recv_sems.at[j],
                device_id=dev, device_id_type=MESH_T) if with_recv else None
            remotes.append((send, recv))
    return locals_, remotes


_HBM = pl.BlockSpec(memory_space=pltpu.HBM)
_SEM = pl.BlockSpec(memory_space=pltpu.SEMAPHORE)
_ANY = pl.BlockSpec(memory_space=pl.ANY)


def _split_start(srcs, kind, name, after):
    n = len(srcs)
    if kind == "forward":
        arrays = list(srcs)
    else:
        gathers = kind in ("gather", "chip")
        arrays = list(srcs) + [lax.empty(((N_DEV,) + a.shape) if gathers else a.shape, a.dtype) for a in srcs]
    na = len(arrays)

    def body(*refs):
        src_refs, land_refs = refs[:n], refs[na - n:na]
        send_sems, recv_sems, local_sems = refs[na + 1:na + 4]
        token = refs[-1]
        locals_, remotes = _split_copies(kind, src_refs, land_refs, send_sems, recv_sems, local_sems, False)
        for cp in locals_:
            cp.start()
        for send, _ in remotes:
            send.start()
        token[...] = jnp.zeros_like(token)

    ncp = n * len(_SPLIT_RELATIONS[kind])
    sems = (pltpu.SemaphoreType.DMA((ncp,)), pltpu.SemaphoreType.DMA((ncp,)), pltpu.SemaphoreType.DMA((n,)))
    thru = tuple(pltpu.HBM(a.shape, a.dtype) for a in arrays)
    outs = pl.pallas_call(
        body, name=name,
        out_shape=sems + thru + (jax.ShapeDtypeStruct((8, LANES), F32),),
        in_specs=[_HBM] * na + [_ANY],
        out_specs=[_SEM] * 3 + [_HBM] * na + [pl.BlockSpec(memory_space=pltpu.VMEM)],
        input_output_aliases={i: 3 + i for i in range(na)},
        compiler_params=pltpu.CompilerParams(has_side_effects=pltpu.SideEffectType.DATAFLOW_SIDE_EFFECTING),
    )(*[pltpu.with_memory_space_constraint(a, pltpu.HBM) for a in arrays], after)
    return (kind, n, outs[:3], outs[3:3 + na]), outs[-1]


def _split_wait(handle, name, after):
    kind, n, sems, thru = handle
    na = len(thru)

    def body(*refs):
        src_refs, land_refs = refs[:n], refs[na - n:na]
        send_sems, recv_sems, local_sems = refs[na:na + 3]
        locals_, remotes = _split_copies(kind, src_refs, land_refs, send_sems, recv_sems, local_sems, True)
        for send, recv in remotes:
            send.wait_send()
            recv.wait_recv()
        for cp in locals_:
            cp.wait()

    outs = pl.pallas_call(
        body, name=name, out_shape=tuple(pltpu.HBM(a.shape, a.dtype) for a in thru),
        in_specs=[_HBM] * na + [_SEM] * 3 + [_ANY], out_specs=[_HBM] * na,
        input_output_aliases={i: i for i in range(na)},
        compiler_params=pltpu.CompilerParams(has_side_effects=pltpu.SideEffectType.DATAFLOW_SIDE_EFFECTING),
    )(*thru, *sems, after)
    return list(outs[na - n:])


def _adamw(gslabs, w, m, v, name):
    R, Cw = w.shape
    if R % 8 == 0:
        tr, tc = _tile(R, 64 if Cw > 1024 else 128, 8), Cw
    else:
        tr, tc = R, LANES
    c1 = 1.0 - ADAM_B1 ** ADAM_STEP
    c2 = 1.0 - ADAM_B2 ** ADAM_STEP

    def body(g_ref, w_ref, m_ref, v_ref, go_ref, d_ref, mo_ref, vo_ref):
        g = g_ref[0].astype(F32)
        for k in range(1, N_DEV):
            g = g + g_ref[k].astype(F32)
        mn = ADAM_B1 * m_ref[...] + (1.0 - ADAM_B1) * g
        vn = ADAM_B2 * v_ref[...] + (1.0 - ADAM_B2) * (g * g)
        m_hat = mn / c1
        v_hat = vn / c2
        go_ref[...] = g
        d_ref[...] = -ADAM_LR * (m_hat / (jnp.sqrt(v_hat) + ADAM_EPS) + ADAM_WD * w_ref[...])
        mo_ref[...] = mn
        vo_ref[...] = vn

    blk = pl.BlockSpec((tr, tc), lambda i, j: (i, j))
    return pl.pallas_call(
        body, grid=(R // tr, Cw // tc),
        in_specs=[pl.BlockSpec((N_DEV, tr, tc), lambda i, j: (0, i, j)), blk, blk, blk],
        out_specs=[blk] * 4, out_shape=[jax.ShapeDtypeStruct((R, Cw), F32)] * 4, name=name)(gslabs, w, m, v)


def _pack(arrs, row_mult, dtype=F32):
    parts = []
    total = 0
    for a in arrs:
        f = a.reshape(-1).astype(dtype)
        n = -(-f.shape[0] // 1024) * 1024
        parts.append(jnp.pad(f, (0, n - f.shape[0])))
        total += n
    rows = total // LANES
    rows_p = -(-rows // row_mult) * row_mult
    flat = jnp.concatenate(parts)
    flat = jnp.pad(flat, (0, rows_p * LANES - total))
    return flat.reshape(rows_p, LANES)


def _unpack(packed, shapes):
    lead = packed.shape[:-2]
    flat = packed.reshape(lead + (-1,))
    out = []
    off = 0
    for s in shapes:
        n = int(np.prod(s))
        out.append(flat[..., off:off + n].reshape(lead + tuple(s)))
        off += -(-n // 1024) * 1024
    return out


def _gather_cols(stacked):
    d, r, c = stacked.shape
    return stacked.transpose(1, 0, 2).reshape(r, d * c)


def _scatter_cols(full):
    r, n = full.shape
    return full.reshape(r, N_DEV, n // N_DEV).transpose(1, 0, 2)


def kernel(x, meta, norm1, w_in, gdn_conv_w, gdn_a_log, gdn_dt_bias, gdn_norm, w_out, norm2, w_ffn_up, ffn_conv_w, ffn_conv_b, w_ffn_down, norm_f, loss_target, m_meta, m_norm1, m_w_in, m_gdn_conv_w, m_gdn_a_log, m_gdn_dt_bias, m_gdn_norm, m_w_out, m_norm2, m_w_ffn_up, m_ffn_conv_w, m_ffn_conv_b, m_w_ffn_down, m_norm_f, v_meta, v_norm1, v_w_in, v_gdn_conv_w, v_gdn_a_log, v_gdn_dt_bias, v_gdn_norm, v_w_out, v_norm2, v_w_ffn_up, v_ffn_conv_w, v_ffn_conv_b, v_w_ffn_down, v_norm_f):
    S = x.shape[1]
    L = N_META + S
    pad = (-L) % CHUNK
    Lp = L + pad

    tr_ = lambda a: jnp.swapaxes(a[0], 0, 1)
    big = [tr_(w_in), w_out[0], tr_(w_ffn_up), w_ffn_down[0]]
    small = [meta, gdn_conv_w, ffn_conv_w]
    small_all, = _all_gather([_pack(small, 8)], "gather_small_weights")
    first, first_token = _split_start([big[0].astype(BF16)], "chip", "gather_w_in_start", small_all)
    late, late_token = _split_start([a.astype(BF16) for a in big[1:]], "gather", "gather_late_start", first_token)

    def first_weights(after):
        half = _split_wait(first, "gather_w_in_wait", after)
        second, second_token = _split_start(half, "forward", "gather_w_in_forward_start", after)
        w_in_s, = _split_wait(second, "gather_w_in_forward_wait", second_token)
        w_in_t = w_in_s.reshape(_O_END, D_MODEL)
        w_main_t = jnp.concatenate([w_in_t[_O_GQ:_O_GZ], w_in_t[_O_RQ:_O_RG], w_in_t[_O_GZ:_O_GA],
                                    w_in_t[_O_RG:_O_END]], axis=0)
        return {"w_main_t": w_main_t, "w_small_t": jnp.pad(w_in_t[_O_GA:_O_RQ], ((0, LANES - 2 * GDN_H), (0, 0)))}

    def late_weights(after):
        w_out_s, w_up_s, w_down_s = _split_wait(late, "gather_late_wait", after)
        return {"w_out": w_out_s.reshape(D_MODEL, D_MODEL), "w_up_t": w_up_s.reshape(2 * D_FF, D_MODEL),
                "w_down": w_down_s.reshape(D_FF, D_MODEL)}

    meta_s, gconv_s, fconv_s = _unpack(small_all, [a.shape for a in small])
    wt = {
        "norm1": norm1 + jnp.tile(late_token[0:1, :], (1, D_MODEL // LANES)),
        "gdn_conv_w": _gather_cols(gconv_s[:, 0]), "a_log": gdn_a_log[0], "dt_bias": gdn_dt_bias[0],
        "gdn_norm": gdn_norm, "norm2": norm2, "ffn_conv_w": _gather_cols(fconv_s[:, 0]), "ffn_conv_b": ffn_conv_b,
        "norm_f": norm_f.reshape(1, D_MODEL),
    }
    meta_f = _gather_cols(meta_s)

    pending = {}

    def on_ffn_out_grads(d_w_down, d_w_up_t, d_w_out):
        srcs = [d_w_out.reshape(N_DEV, D_MODEL // N_DEV, D_MODEL), d_w_up_t.reshape(N_DEV, 2 * D_FF // N_DEV, D_MODEL),
                d_w_down.reshape(N_DEV, D_FF // N_DEV, D_MODEL)]
        pending["ffn_out"], token = _split_start(srcs, "a2a", "exchange_ffn_out_start", d_w_out)
        return token

    def on_w_in_grads(d_w_in_t):
        slabs = d_w_in_t.astype(BF16).reshape(N_DEV, _O_END // N_DEV, D_MODEL)
        pending["w_in"], token = _split_start([slabs], "a2a", "exchange_w_in_start", d_w_in_t)
        return token

    hpad = jnp.concatenate([jnp.zeros((pad, D_MODEL), F32), meta_f, x[0]], axis=0)
    tgt = jnp.concatenate([jnp.zeros((pad + N_META, D_MODEL), F32), loss_target[0]], axis=0)
    lossvec, dh0, gr = _local_step(hpad, tgt, pad, wt, first_weights, late_weights, on_ffn_out_grads, on_w_in_grads)

    loss = lax.psum(jnp.sum(lossvec), ("x", "y", "c"))
    grad_x = dh0[pad + N_META:][None]

    big_m = [tr_(m_w_in), m_w_out[0], tr_(m_w_ffn_up), m_w_ffn_down[0]]
    big_v = [tr_(v_w_in), v_w_out[0], tr_(v_w_ffn_up), v_w_ffn_down[0]]
    slabs_ffn_out = _split_wait(pending["ffn_out"], "exchange_ffn_out_wait", dh0)
    big_out = [None] + [_adamw(slabs_ffn_out[i - 1], big[i], big_m[i], big_v[i], "adamw_big_%d" % i)
                        for i in range(1, len(big))]
    g_sm = [_scatter_cols(dh0[pad:pad + N_META]), _scatter_cols(gr["gdn_conv_w"]), _scatter_cols(gr["ffn_conv_w"])]
    g_small = jnp.stack([_pack([g[d] for g in g_sm], 8) for d in range(N_DEV)])
    slabs_small, = _all_to_all([g_small], "exchange_small_gradients")
    small_out = _adamw(slabs_small, _pack(small, 8), _pack([m_meta, m_gdn_conv_w, m_ffn_conv_w], 8),
                       _pack([v_meta, v_gdn_conv_w, v_ffn_conv_w], 8), "adamw_small_sharded")
    small_un = [_unpack(o, [a.shape for a in small]) for o in small_out]
    rep_w = [norm1, gdn_a_log, gdn_dt_bias, gdn_norm, norm2, ffn_conv_b, norm_f]
    rep_m = [m_norm1, m_gdn_a_log, m_gdn_dt_bias, m_gdn_norm, m_norm2, m_ffn_conv_b, m_norm_f]
    rep_v = [v_norm1, v_gdn_a_log, v_gdn_dt_bias, v_gdn_norm, v_norm2, v_ffn_conv_b, v_norm_f]
    rep_g = [gr["norm1"], gr["a_log"], gr["dt_bias"], gr["gdn_norm"], gr["norm2"], gr["ffn_conv_b"], gr["norm_f"]]
    rep_slabs, = _all_gather([_pack(rep_g, 8)], "gather_small_gradients")
    rep_out = _adamw(rep_slabs, _pack(rep_w, 8), _pack(rep_m, 8), _pack(rep_v, 8), "adamw_replicated")
    rep_shapes = [a.shape for a in rep_w]
    rp_g, rp_d, rp_nm, rp_nv = [_unpack(o, rep_shapes) for o in rep_out]

    slabs_w_in, = _split_wait(pending["w_in"], "exchange_w_in_wait", rep_out[0])
    big_out[0] = _adamw(slabs_w_in, big[0], big_m[0], big_v[0], "adamw_big_0")
    back = lambda a: jnp.swapaxes(a, 0, 1)[None]
    sh_g, sh_d, sh_nm, sh_nv = [
        [small_un[j][0], back(big_out[0][j]), small_un[j][1], big_out[1][j][None], back(big_out[2][j]),
         small_un[j][2], big_out[3][j][None]] for j in range(4)]

    def order(sh, rp):
        return [sh[0], rp[0], sh[1], sh[2], rp[1], rp[2], rp[3], sh[3], rp[4], sh[4], sh[5], rp[5], sh[6], rp[6]]

    return (loss, grad_x, *order(sh_g, rp_g), *order(sh_d, rp_d), *order(sh_nm, rp_nm), *order(sh_nv, rp_nv))
```

```python
import functools
import math

import numpy as np
import jax
import jax.numpy as jnp
from jax import lax
from jax.experimental import pallas as pl
from jax.experimental.pallas import tpu as pltpu

F32 = jnp.float32
BF16 = jnp.bfloat16
HI = lax.Precision.HIGHEST

D_MODEL = 1024
N_META = 16
CHUNK = 64
GDN_H = 8
GDN_D = 128
RET_H = 4
RET_D = 256
D_FF = 2816
GDN_CONV = 4
FFN_CONV = 3
ROPE_BASE = 10000.0
EPS = 1e-6
N_DEV = 8
LANES = 128
MAIN_W = 10 * 1024
_O_GQ, _O_GZ, _O_GA, _O_RQ, _O_RG, _O_GATE, _O_END = 0, 3072, 4096, 4112, 7184, 8208, 10256

ADAM_LR = 0.001
ADAM_B1 = 0.9
ADAM_B2 = 0.999
ADAM_EPS = 1e-08
ADAM_WD = 0.01
ADAM_STEP = 10

MESH_T = pl.DeviceIdType.MESH


def _tile(n, target, mult):
    best = None
    for d in range(mult, min(n, target) + 1, mult):
        if n % d == 0:
            best = d
    assert best is not None, (n, target, mult)
    return best


def _sig(x):
    return 1.0 / (1.0 + jnp.exp(-x))


def _d(a, b):
    return jnp.dot(a.astype(BF16), b.astype(BF16), preferred_element_type=F32)


def _dnt(a, b):
    return lax.dot_general(a.astype(BF16), b.astype(BF16), (((1,), (1,)), ((), ())), preferred_element_type=F32)


def _dtn(a, b):
    return lax.dot_general(a.astype(BF16), b.astype(BF16), (((0,), (0,)), ((), ())), preferred_element_type=F32)


def _dx(a, b):
    return jnp.dot(a, b, preferred_element_type=F32, precision=HI)


def _dxnt(a, b):
    return lax.dot_general(a, b, (((1,), (1,)), ((), ())), preferred_element_type=F32, precision=HI)


def _dxtn(a, b):
    return lax.dot_general(a, b, (((0,), (0,)), ((), ())), preferred_element_type=F32, precision=HI)


def _split(a):
    hi = a.astype(BF16)
    return hi, (a - hi.astype(F32)).astype(BF16)


def _d3g(a, b, dims):
    ah, al = _split(a)
    bh, bl = _split(b)
    f = functools.partial(lax.dot_general, dimension_numbers=dims, preferred_element_type=F32)
    if dims == _NN:
        rows = a.shape[0]
        both = f(jnp.concatenate([ah, al], axis=0), bh)
        return both[:rows] + (f(ah, bl) + both[rows:])
    return f(ah, bh) + (f(ah, bl) + f(al, bh))


_NN = (((1,), (0,)), ((), ()))
_NT = (((1,), (1,)), ((), ()))
_TN = (((0,), (0,)), ((), ()))


def _rowsum(x):
    return jnp.sum(x, axis=1, keepdims=True)


def _allsum(x):
    return jnp.sum(jnp.sum(x, axis=1, keepdims=True), axis=0, keepdims=True)


def _mm_nn(a, b, res=None, out_dtype=F32, bt=False, name="mm_nn"):
    M, K = a.shape
    N = b.shape[0] if bt else b.shape[1]
    tm = _tile(M, 704, 16)
    tn = _tile(N, 2816, 128)

    def body(*refs):
        if res is None:
            a_ref, b_ref, o_ref = refs
        else:
            a_ref, b_ref, r_ref, o_ref = refs
        acc = lax.dot_general(a_ref[...], b_ref[...], _NT if bt else _NN, preferred_element_type=F32)
        if res is not None:
            acc = acc + r_ref[...]
        o_ref[...] = acc.astype(out_dtype)

    b_spec = pl.BlockSpec((tn, K), lambda j, i: (j, 0)) if bt else pl.BlockSpec((K, tn), lambda j, i: (0, j))
    in_specs = [pl.BlockSpec((tm, K), lambda j, i: (i, 0)), b_spec]
    args = [a, b]
    if res is not None:
        in_specs.append(pl.BlockSpec((tm, tn), lambda j, i: (i, j)))
        args.append(res)
    return pl.pallas_call(
        body, grid=(N // tn, M // tm), in_specs=in_specs,
        out_specs=pl.BlockSpec((tm, tn), lambda j, i: (i, j)),
        out_shape=jax.ShapeDtypeStruct((M, N), out_dtype), name=name)(*args)


def _mm_nt(a, b, res=None, name="mm_nt"):
    M, Nc = a.shape
    K = b.shape[0]
    tm = _tile(M, 704, 16)
    tc = _tile(Nc, 5632, 128)

    def body(*refs):
        if res is None:
            a_ref, b_ref, o_ref = refs
        else:
            a_ref, b_ref, r_ref, o_ref = refs
        c = pl.program_id(1)
        p = lax.dot_general(a_ref[...], b_ref[...], (((1,), (1,)), ((), ())), preferred_element_type=F32)

        @pl.when(c == 0)
        def _():
            if res is None:
                o_ref[...] = p
            else:
                o_ref[...] = p + r_ref[...]

        @pl.when(c > 0)
        def _():
            o_ref[...] += p

    in_specs = [pl.BlockSpec((tm, tc), lambda i, c: (i, c)), pl.BlockSpec((K, tc), lambda i, c: (0, c))]
    args = [a, b]
    if res is not None:
        in_specs.append(pl.BlockSpec((tm, K), lambda i, c: (i, 0)))
        args.append(res)
    return pl.pallas_call(
        body, grid=(M // tm, Nc // tc), in_specs=in_specs,
        out_specs=pl.BlockSpec((tm, K), lambda i, c: (i, 0)),
        out_shape=jax.ShapeDtypeStruct((M, K), F32), name=name)(*args)


def _mm_tn(a, b, out_dtype=F32, name="mm_tn"):
    M, K = a.shape
    N = b.shape[1]
    tm = _tile(M, 2752, 16)
    tk = _tile(K, 1408, 128)
    tn = _tile(N, 1408, 128)
    steps = M // tm

    def body(a_ref, b_ref, o_ref, *scratch):
        acc = scratch[0] if scratch else o_ref
        m = pl.program_id(2)
        p = lax.dot_general(a_ref[...], b_ref[...], (((0,), (0,)), ((), ())), preferred_element_type=F32)

        @pl.when(m == 0)
        def _():
            acc[...] = p

        @pl.when(m > 0)
        def _():
            acc[...] += p

        if scratch:
            @pl.when(m == steps - 1)
            def _():
                o_ref[...] = acc[...].astype(out_dtype)

    return pl.pallas_call(
        body, grid=(K // tk, N // tn, steps),
        in_specs=[pl.BlockSpec((tm, tk), lambda kk, j, m: (m, kk)), pl.BlockSpec((tm, tn), lambda kk, j, m: (m, j))],
        out_specs=pl.BlockSpec((tk, tn), lambda kk, j, m: (kk, j)),
        out_shape=jax.ShapeDtypeStruct((K, N), out_dtype),
        scratch_shapes=[] if out_dtype == F32 else [pltpu.VMEM((tk, tn), F32)], name=name)(a, b)


def _rms_fwd(x, g, name):
    Lp = x.shape[0]
    tr = _tile(Lp, 256, 16)

    def body(x_ref, g_ref, o_ref):
        xv = x_ref[...]
        r = lax.rsqrt(jnp.mean(xv * xv, axis=-1, keepdims=True) + EPS)
        o_ref[...] = (xv * r * g_ref[...]).astype(BF16)

    return pl.pallas_call(
        body, grid=(Lp // tr,),
        in_specs=[pl.BlockSpec((tr, D_MODEL), lambda i: (i, 0)), pl.BlockSpec((1, D_MODEL), lambda i: (0, 0))],
        out_specs=pl.BlockSpec((tr, D_MODEL), lambda i: (i, 0)),
        out_shape=jax.ShapeDtypeStruct((Lp, D_MODEL), BF16), name=name)(x, g)


class _Producer:
    def __init__(self, a, b, res=None):
        self.a, self.b, self.res = a, b, res
        self.tr = _tile(a.shape[0], 704, 16)
        K = a.shape[1]
        self.args = [a, b] + ([] if res is None else [res])
        self.specs = [pl.BlockSpec((self.tr, K), lambda i: (i, 0)),
                      pl.BlockSpec((K, D_MODEL), lambda i: (0, 0), pipeline_mode=pl.Buffered(1))]
        if res is not None:
            self.specs.append(pl.BlockSpec((self.tr, D_MODEL), lambda i: (i, 0)))

    def tile(self, refs):
        acc = jnp.dot(refs[0][...], refs[1][...], preferred_element_type=F32)
        return acc if self.res is None else acc + refs[2][...]


def _mm_rms_fwd(prod, g, name):
    Lp, tr, n = prod.a.shape[0], prod.tr, len(prod.args)

    def body(*refs):
        g_ref, x_ref, o_ref = refs[n:]
        xv = prod.tile(refs[:n])
        r = lax.rsqrt(jnp.mean(xv * xv, axis=-1, keepdims=True) + EPS)
        x_ref[...] = xv
        o_ref[...] = (xv * r * g_ref[...]).astype(BF16)

    blk = pl.BlockSpec((tr, D_MODEL), lambda i: (i, 0))
    return pl.pallas_call(
        body, grid=(Lp // tr,), in_specs=prod.specs + [pl.BlockSpec((1, D_MODEL), lambda i: (0, 0))],
        out_specs=[blk, blk],
        out_shape=[jax.ShapeDtypeStruct((Lp, D_MODEL), F32), jax.ShapeDtypeStruct((Lp, D_MODEL), BF16)],
        name=name)(*prod.args, g)


def _rms_bwd(x, g, dy, dres, pad, name):
    Lp = x.shape[0]
    fused = isinstance(dy, _Producer)
    tr = dy.tr if fused else _tile(Lp, 256, 16)
    n = len(dy.args) if fused else 1

    def body(*refs):
        x_ref, g_ref, dr_ref, dx_ref, dxb_ref, dg_ref = refs[n:]
        i = pl.program_id(0)
        xv = x_ref[...]
        r = lax.rsqrt(jnp.mean(xv * xv, axis=-1, keepdims=True) + EPS)
        xh = xv * r
        dyv = dy.tile(refs[:n]) if fused else refs[0][...]
        dxh = dyv * g_ref[...]
        dx = r * (dxh - xh * jnp.mean(dxh * xh, axis=-1, keepdims=True)) + dr_ref[...]
        row = i * tr + lax.broadcasted_iota(jnp.int32, (tr, 1), 0)
        dx = jnp.where(row >= pad, dx, 0.0)
        dx_ref[...] = dx
        dxb_ref[...] = dx.astype(BF16)
        part = jnp.sum(dyv * xh, axis=0, keepdims=True)

        @pl.when(i == 0)
        def _():
            dg_ref[...] = part

        @pl.when(i > 0)
        def _():
            dg_ref[...] += part

    blk = pl.BlockSpec((tr, D_MODEL), lambda i: (i, 0))
    vec = pl.BlockSpec((1, D_MODEL), lambda i: (0, 0))
    return pl.pallas_call(
        body, grid=(Lp // tr,), in_specs=(dy.specs if fused else [blk]) + [blk, vec, blk], out_specs=[blk, blk, vec],
        out_shape=[jax.ShapeDtypeStruct((Lp, D_MODEL), F32), jax.ShapeDtypeStruct((Lp, D_MODEL), BF16),
                   jax.ShapeDtypeStruct((1, D_MODEL), F32)], name=name)(*(dy.args if fused else [dy]), x, g, dres)


def _final(h2, g, tgt, first_row):
    fused = isinstance(h2, _Producer)
    Lp = h2.a.shape[0] if fused else h2.shape[0]
    tr = h2.tr if fused else _tile(Lp, 256, 16)
    n = len(h2.args) if fused else 1

    def body(*refs):
        g_ref, t_ref, loss_ref, dx_ref, dxb_ref, dg_ref = refs[n:]
        i = pl.program_id(0)
        xv = h2.tile(refs[:n]) if fused else refs[0][...]
        gv = g_ref[...]
        r = lax.rsqrt(jnp.mean(xv * xv, axis=-1, keepdims=True) + EPS)
        xh = xv * r
        row = i * tr + lax.broadcasted_iota(jnp.int32, (tr, 1), 0)
        err = jnp.where(row >= first_row, xh * gv - t_ref[...], 0.0)
        lpart = jnp.sum(err * err, axis=0, keepdims=True) * (0.5 / D_MODEL)
        dyv = err * (1.0 / D_MODEL)
        dxh = dyv * gv
        dx = r * (dxh - xh * jnp.mean(dxh * xh, axis=-1, keepdims=True))
        dx_ref[...] = dx
        dxb_ref[...] = dx.astype(BF16)
        part = jnp.sum(dyv * xh, axis=0, keepdims=True)

        @pl.when(i == 0)
        def _():
            dg_ref[...] = part
            loss_ref[...] = lpart

        @pl.when(i > 0)
        def _():
            dg_ref[...] += part
            loss_ref[...] += lpart

    blk = pl.BlockSpec((tr, D_MODEL), lambda i: (i, 0))
    vec = pl.BlockSpec((1, D_MODEL), lambda i: (0, 0))
    return pl.pallas_call(
        body, grid=(Lp // tr,), in_specs=(h2.specs if fused else [blk]) + [vec, blk], out_specs=[vec, blk, blk, vec],
        out_shape=[jax.ShapeDtypeStruct((1, D_MODEL), F32), jax.ShapeDtypeStruct((Lp, D_MODEL), F32),
                   jax.ShapeDtypeStruct((Lp, D_MODEL), BF16), jax.ShapeDtypeStruct((1, D_MODEL), F32)],
        name="final_norm_loss")(*(h2.args if fused else [h2]), g, tgt)


def _halo_prev(tr, width, col=0):
    return pl.BlockSpec((8, width), lambda i: (jnp.maximum(i * (tr // 8) - 1, 0), col))


def _halo_next(tr, width, nrows, col=0):
    last = nrows // 8 - 1
    return pl.BlockSpec((8, width), lambda i: (jnp.minimum((i + 1) * (tr // 8), last), col))


def _shifted(x, offs):
    n = x.shape[0]
    return [x if off == 0 else pltpu.roll(x, n - off, 0) for off in offs]


def _taps(wins, w, rows, bias=None):
    acc = w[0:1, :] * wins[0][0:rows, :]
    if bias is not None:
        acc = acc + bias
    for kk in range(1, len(wins)):
        acc = acc + w[kk:kk + 1, :] * wins[kk][0:rows, :]
    return acc


def _gdn_pre(proj_m, proj_s, conv_w, gparams, pad):
    Lp = proj_m.shape[0]
    tr = _tile(Lp, 192, 64)
    W3 = 3 * D_MODEL

    def body(main_ref, prev_ref, s_ref, w_ref, gp_ref, qkv_ref, gsm_ref, c_ref):
        i = pl.program_id(0)
        prev = jnp.where(i > 0, prev_ref[...], 0.0)
        ext = jnp.concatenate([prev, main_ref[...]], axis=0)
        c = _taps(_shifted(ext, range(8 - (GDN_CONV - 1), 9)), w_ref[...], tr)
        c_ref[...] = c
        s = c * _sig(c)
        scale = GDN_D ** -0.5
        for j in range(2 * GDN_H):
            seg = s[:, j * GDN_D:(j + 1) * GDN_D]
            r = lax.rsqrt(_rowsum(seg * seg) + EPS)
            if j < GDN_H:
                r = r * scale
            qkv_ref[:, j * GDN_D:(j + 1) * GDN_D] = seg * r
        qkv_ref[:, 2 * D_MODEL:] = s[:, 2 * D_MODEL:]
        sm = s_ref[...]
        gp = gp_ref[...]
        lane = lax.broadcasted_iota(jnp.int32, sm.shape, 1)
        z = sm + gp[1:2, :]
        softplus = jnp.maximum(z, 0.0) + jnp.log(1.0 + jnp.exp(-jnp.abs(z)))
        lg = -jnp.exp(gp[0:1, :]) * softplus
        row = i * tr + lax.broadcasted_iota(jnp.int32, (tr, 1), 0)
        out = jnp.where(lane < GDN_H, lg, jnp.where(lane < 2 * GDN_H, _sig(sm), 0.0))
        gsm_ref[...] = jnp.where(row >= pad, out, 0.0)

    return pl.pallas_call(
        body, grid=(Lp // tr,),
        in_specs=[pl.BlockSpec((tr, W3), lambda i: (i, 0)), _halo_prev(tr, W3),
                  pl.BlockSpec((tr, LANES), lambda i: (i, 0)),
                  pl.BlockSpec((GDN_CONV, W3), lambda i: (0, 0)), pl.BlockSpec((8, LANES), lambda i: (0, 0))],
        out_specs=[pl.BlockSpec((tr, W3), lambda i: (i, 0)), pl.BlockSpec((tr, LANES), lambda i: (i, 0)),
                   pl.BlockSpec((tr, W3), lambda i: (i, 0))],
        out_shape=[jax.ShapeDtypeStruct((Lp, W3), F32), jax.ShapeDtypeStruct((Lp, LANES), F32),
                   jax.ShapeDtypeStruct((Lp, W3), F32)],
        name="gdn_pre")(proj_m, proj_m, proj_s, conv_w, gparams)


def _gdn_pre_bwd(proj_m, conv_out, proj_s, conv_w, gparams, dq, dk, dv, dgs, pad):
    Lp = proj_m.shape[0]
    tr = _tile(Lp, 192, 64)
    W3 = 3 * D_MODEL
    te = tr + 8

    def body(main_ref, c_ref, cn_ref, s_ref, w_ref, gp_ref,
             dq_ref, dqn_ref, dk_ref, dkn_ref, dv_ref, dvn_ref, dgs_ref,
             da_ref, ds_ref, dw_ref, dgp_ref):
        i = pl.program_id(0)
        w = w_ref[...]
        c = jnp.concatenate([c_ref[...], cn_ref[...]], axis=0)
        sg = _sig(c)
        s = c * sg
        rowe = i * tr + lax.broadcasted_iota(jnp.int32, (te, 1), 0)
        live = (rowe >= pad) & (rowe < Lp)
        dqe = jnp.concatenate([dq_ref[...], dqn_ref[...]], axis=0)
        dke = jnp.concatenate([dk_ref[...], dkn_ref[...]], axis=0)
        dve = jnp.concatenate([dv_ref[...], dvn_ref[...]], axis=0)
        scale = GDN_D ** -0.5
        parts = []
        for j in range(2 * GDN_H):
            seg = s[:, j * GDN_D:(j + 1) * GDN_D]
            r = lax.rsqrt(_rowsum(seg * seg) + EPS)
            xh = seg * r
            if j < GDN_H:
                dxh = dqe[:, j * GDN_D:(j + 1) * GDN_D] * scale
            else:
                dxh = dke[:, (j - GDN_H) * GDN_D:(j - GDN_H + 1) * GDN_D]
            parts.append(r * (dxh - xh * _rowsum(dxh * xh)))
        parts.append(dve)
        dsv = jnp.concatenate(parts, axis=1)
        dc = jnp.where(live, dsv * (sg * (1.0 + c * (1.0 - sg))), 0.0)
        dcs = _shifted(dc, range(GDN_CONV - 1, -1, -1))
        da_ref[...] = _taps(dcs, w, tr).astype(BF16)
        pm = main_ref[...]
        rows = [jnp.sum(dcs[kk][0:tr, :] * pm, axis=0, keepdims=True) for kk in range(GDN_CONV)]
        dwp = jnp.concatenate(rows + [jnp.zeros((8 - GDN_CONV, W3), F32)], axis=0)

        sm = s_ref[...]
        gp = gp_ref[...]
        lane = lax.broadcasted_iota(jnp.int32, sm.shape, 1)
        rowm = i * tr + lax.broadcasted_iota(jnp.int32, (tr, 1), 0)
        dgv = jnp.where(rowm >= pad, dgs_ref[...], 0.0)
        dlg = jnp.where(lane < GDN_H, dgv, 0.0)
        dbt = jnp.where((lane >= GDN_H) & (lane < 2 * GDN_H), dgv, 0.0)
        z = sm + gp[1:2, :]
        softplus = jnp.maximum(z, 0.0) + jnp.log(1.0 + jnp.exp(-jnp.abs(z)))
        ea = jnp.exp(gp[0:1, :])
        dz = dlg * (-ea) * _sig(z)
        dal = dlg * (-ea) * softplus
        bt = _sig(sm)
        dgb = dbt * bt * (1.0 - bt)
        ds_ref[...] = (dz + dgb).astype(BF16)
        gpp = jnp.concatenate([jnp.sum(dal, axis=0, keepdims=True), jnp.sum(dz, axis=0, keepdims=True),
                               jnp.zeros((6, LANES), F32)], axis=0)

        @pl.when(i == 0)
        def _():
            dw_ref[...] = dwp
            dgp_ref[...] = gpp

        @pl.when(i > 0)
        def _():
            dw_ref[...] += dwp
            dgp_ref[...] += gpp

    m3 = pl.BlockSpec((tr, W3), lambda i: (i, 0))
    m1 = pl.BlockSpec((tr, D_MODEL), lambda i: (i, 0))
    n1 = _halo_next(tr, D_MODEL, Lp)
    return pl.pallas_call(
        body, grid=(Lp // tr,),
        in_specs=[m3, m3, _halo_next(tr, W3, Lp), pl.BlockSpec((tr, LANES), lambda i: (i, 0)),
                  pl.BlockSpec((GDN_CONV, W3), lambda i: (0, 0)), pl.BlockSpec((8, LANES), lambda i: (0, 0)),
                  m1, n1, m1, n1, m1, n1, pl.BlockSpec((tr, LANES), lambda i: (i, 0))],
        out_specs=[m3, pl.BlockSpec((tr, LANES), lambda i: (i, 0)),
                   pl.BlockSpec((8, W3), lambda i: (0, 0)), pl.BlockSpec((8, LANES), lambda i: (0, 0))],
        out_shape=[jax.ShapeDtypeStruct((Lp, W3), BF16), jax.ShapeDtypeStruct((Lp, LANES), BF16),
                   jax.ShapeDtypeStruct((8, W3), F32), jax.ShapeDtypeStruct((8, LANES), F32)],
        name="gdn_pre_bwd")(proj_m, conv_out, conv_out, proj_s, conv_w, gparams, dq, dq, dk, dk, dv, dv, dgs)


def _gdn_gates(gs):
    ri = lax.broadcasted_iota(jnp.int32, (CHUNK, CHUNK), 0)
    ci = lax.broadcasted_iota(jnp.int32, (CHUNK, CHUNK), 1)
    tril = ri >= ci
    strict = ri > ci
    gall = _dx(tril.astype(F32), gs)
    lane8 = lax.broadcasted_iota(jnp.int32, (8, LANES), 1)
    sub8 = lax.broadcasted_iota(jnp.int32, (8, LANES), 0)
    grow = _dxnt((lane8 == sub8).astype(F32), gall)
    return gall, grow, tril, strict


def _gdn_decay(gall, grow, tril, h):
    g = gall[:, h:h + 1]
    return g, jnp.where(tril, jnp.exp(jnp.where(tril, g - grow[h:h + 1, :], 0.0)), 0.0)


def _group(N):
    return 3 if N % 3 == 0 else (2 if N % 2 == 0 else 1)


def _gdn_chunk_specs(N, rev):
    G = _group(N)
    nb = N // G
    cn = (lambda n: nb - 1 - n) if rev else (lambda n: n)
    col = lambda j: pl.BlockSpec((G * CHUNK, D_MODEL), lambda n: (cn(n), j))
    gate = pl.BlockSpec((G * CHUNK, LANES), lambda n: (cn(n), 0))
    st = lambda a, b: pl.BlockSpec((GDN_H, G, a, b), lambda n: (0, cn(n), 0, 0))
    return G, nb, col, gate, st


def _gdn_chunk_fwd(qkv, gsm):
    Lp = qkv.shape[0]
    N = Lp // CHUNK
    G, nb, col, gate, st = _gdn_chunk_specs(N, False)

    def body(q_ref, k_ref, v_ref, gs_ref, o_ref, sin_ref, t_ref, S):
        n = pl.program_id(0)

        @pl.when(n == 0)
        def _():
            S[...] = jnp.zeros_like(S)

        ri = lax.broadcasted_iota(jnp.int32, (CHUNK, CHUNK), 0)
        ci = lax.broadcasted_iota(jnp.int32, (CHUNK, CHUNK), 1)
        eye = (ri == ci).astype(F32)
        heads = range(GDN_H)
        sls = [slice(h * GDN_D, (h + 1) * GDN_D) for h in heads]
        rows = [slice(c * CHUNK, (c + 1) * CHUNK) for c in range(G)]
        pairs = [(c, h) for c in range(G) for h in heads]
        P = lambda f: {p: f(*p) for p in pairs}
        gs = [gs_ref[rows[c], :] for c in range(G)]
        gates = [_gdn_gates(gs[c]) for c in range(G)]
        tril, strict = gates[0][2], gates[0][3]
        q = P(lambda c, h: q_ref[rows[c], sls[h]])
        k = P(lambda c, h: k_ref[rows[c], sls[h]])
        v = P(lambda c, h: v_ref[rows[c], sls[h]])
        beta = P(lambda c, h: gs[c][:, GDN_H + h:GDN_H + h + 1])
        gg = P(lambda c, h: _gdn_decay(gates[c][0], gates[c][1], tril, h))
        g = {p: x[0] for p, x in gg.items()}
        gam = {p: x[1] for p, x in gg.items()}
        eg = P(lambda c, h: jnp.exp(g[c, h]))
        gl = P(lambda c, h: g[c, h][CHUNK - 1:CHUNK, :])
        kb = P(lambda c, h: k[c, h] * beta[c, h])
        pw = P(lambda c, h: -jnp.where(strict, _dnt(kb[c, h], k[c, h]) * gam[c, h], 0.0))
        p = P(lambda c, h: _dnt(q[c, h], k[c, h]) * gam[c, h])
        t = P(lambda c, h: eye + pw[c, h])
        for _ in range(5):
            pw = P(lambda c, h: _d3g(pw[c, h], pw[c, h], _NN))
            t = P(lambda c, h: t[c, h] + _d3g(t[c, h], pw[c, h], _NN))
        u = P(lambda c, h: _d(t[c, h], v[c, h] * beta[c, h]))
        w = P(lambda c, h: _d(t[c, h], kb[c, h] * eg[c, h]))
        qg = P(lambda c, h: q[c, h] * eg[c, h])
        kd = P(lambda c, h: k[c, h] * jnp.exp(gl[c, h] - g[c, h]))
        egl = P(lambda c, h: jnp.exp(gl[c, h]))
        for c in range(G):
            for h in heads:
                t_ref[h, c] = t[c, h]
        cur = [S[h] for h in heads]
        for c in range(G):
            vnew = [u[c, h] - _d(w[c, h], cur[h]) for h in heads]
            for h in heads:
                o_ref[rows[c], sls[h]] = _d(qg[c, h], cur[h]) + _d(p[c, h], vnew[h])
                sin_ref[h, c] = cur[h]
            cur = [cur[h] * egl[c, h] + _dtn(kd[c, h], vnew[h]) for h in heads]
        for h in heads:
            S[h] = cur[h]

    return pl.pallas_call(
        body, grid=(nb,),
        in_specs=[col(0), col(1), col(2), gate],
        out_specs=[col(0), st(GDN_D, GDN_D), st(CHUNK, CHUNK)],
        out_shape=[jax.ShapeDtypeStruct((Lp, D_MODEL), F32), jax.ShapeDtypeStruct((GDN_H, N, GDN_D, GDN_D), F32),
                   jax.ShapeDtypeStruct((GDN_H, N, CHUNK, CHUNK), F32)],
        scratch_shapes=[pltpu.VMEM((GDN_H, GDN_D, GDN_D), F32)],
        name="gdn_chunk_fwd")(qkv, qkv, qkv, gsm)


def _gdn_chunk_bwd(qkv, gsm, do, s_in, t_in):
    Lp = qkv.shape[0]
    N = Lp // CHUNK
    G, nb, col, gate, st = _gdn_chunk_specs(N, True)

    def body(q_ref, k_ref, v_ref, gs_ref, do_ref, sin_ref, t_ref, dq_ref, dk_ref, dv_ref, dgs_ref, dS):
        n = pl.program_id(0)

        @pl.when(n == 0)
        def _():
            dS[...] = jnp.zeros_like(dS)

        lane = lax.broadcasted_iota(jnp.int32, (CHUNK, LANES), 1)
        rcol = lax.broadcasted_iota(jnp.int32, (CHUNK, 1), 0)
        ri = lax.broadcasted_iota(jnp.int32, (CHUNK, CHUNK), 0)
        ci = lax.broadcasted_iota(jnp.int32, (CHUNK, CHUNK), 1)
        ones = jnp.ones((CHUNK, LANES), F32)
        heads = range(GDN_H)
        sls = [slice(h * GDN_D, (h + 1) * GDN_D) for h in heads]
        rows = [slice(c * CHUNK, (c + 1) * CHUNK) for c in range(G)]
        pairs = [(c, h) for c in range(G) for h in heads]
        P = lambda f: {p: f(*p) for p in pairs}
        gs = [gs_ref[rows[c], :] for c in range(G)]
        gates = [_gdn_gates(gs[c]) for c in range(G)]
        tril, strict = gates[0][2], gates[0][3]
        q = P(lambda c, h: q_ref[rows[c], sls[h]])
        k = P(lambda c, h: k_ref[rows[c], sls[h]])
        v = P(lambda c, h: v_ref[rows[c], sls[h]])
        dov = P(lambda c, h: do_ref[rows[c], sls[h]])
        s0 = P(lambda c, h: sin_ref[h, c])
        t = P(lambda c, h: t_ref[h, c])
        beta = P(lambda c, h: gs[c][:, GDN_H + h:GDN_H + h + 1])
        gg = P(lambda c, h: _gdn_decay(gates[c][0], gates[c][1], tril, h))
        g = {p: x[0] for p, x in gg.items()}
        gam = {p: x[1] for p, x in gg.items()}
        eg = P(lambda c, h: jnp.exp(g[c, h]))
        egl = P(lambda c, h: jnp.exp(g[c, h][CHUNK - 1:CHUNK, :]))
        e = P(lambda c, h: jnp.exp(g[c, h][CHUNK - 1:CHUNK, :] - g[c, h]))
        kb = P(lambda c, h: k[c, h] * beta[c, h])
        kbg = P(lambda c, h: kb[c, h] * eg[c, h])
        vb = P(lambda c, h: v[c, h] * beta[c, h])
        qg = P(lambda c, h: q[c, h] * eg[c, h])
        kd = P(lambda c, h: k[c, h] * e[c, h])
        m = P(lambda c, h: jnp.where(strict, _dnt(kb[c, h], k[c, h]) * gam[c, h], 0.0))
        u = P(lambda c, h: _d(t[c, h], vb[c, h]))
        w = P(lambda c, h: _d(t[c, h], kbg[c, h]))
        p = P(lambda c, h: _dnt(q[c, h], k[c, h]) * gam[c, h])
        dqg = P(lambda c, h: _dnt(dov[c, h], s0[c, h]))
        qgdo = P(lambda c, h: _dtn(qg[c, h], dov[c, h]))
        ptdo = P(lambda c, h: _dtn(p[c, h], dov[c, h]))
        vnew = P(lambda c, h: u[c, h] - _d(w[c, h], s0[c, h]))
        dp = P(lambda c, h: jnp.where(tril, _dnt(dov[c, h], vnew[c, h]), 0.0))
        cur = [dS[h] for h in heads]
        dvnew, dkd, sds = {}, {}, {}
        for c in reversed(range(G)):
            for h in heads:
                dvnew[c, h] = ptdo[c, h] + _d(kd[c, h], cur[h])
                dkd[c, h] = _dnt(vnew[c, h], cur[h])
                sds[c, h] = _allsum(s0[c, h] * cur[h])
            cur = [qgdo[c, h] + egl[c, h] * cur[h] - _dtn(w[c, h], dvnew[c, h]) for h in heads]
        for h in heads:
            dS[h] = cur[h]
        dw = P(lambda c, h: -_dnt(dvnew[c, h], s0[c, h]))
        dvb = P(lambda c, h: _dtn(t[c, h], dvnew[c, h]))
        dkbg = P(lambda c, h: _dtn(t[c, h], dw[c, h]))
        dt = P(lambda c, h: _dnt(dvnew[c, h], vb[c, h]) + _dnt(dw[c, h], kbg[c, h]))
        x1 = P(lambda c, h: _d3g(t[c, h], dt[c, h], _TN))
        dm = P(lambda c, h: jnp.where(strict, -_d3g(x1[c, h], t[c, h], _NT), 0.0))
        dkk = P(lambda c, h: dm[c, h] * gam[c, h])
        dqk = P(lambda c, h: dp[c, h] * gam[c, h])
        dkb = P(lambda c, h: _d(dkk[c, h], k[c, h]) + eg[c, h] * dkbg[c, h])
        em = P(lambda c, h: dm[c, h] * m[c, h] + dp[c, h] * p[c, h])
        colsum = P(lambda c, h: _d3g(em[c, h], ones, _TN)[:, 0:1])
        for c, h in pairs:
            dk_ref[rows[c], sls[h]] = (_dtn(dkk[c, h], kb[c, h]) + _dtn(dqk[c, h], q[c, h]) + dkd[c, h] * e[c, h]
                                       + beta[c, h] * dkb[c, h])
            dq_ref[rows[c], sls[h]] = _d(dqk[c, h], k[c, h]) + dqg[c, h] * eg[c, h]
            dv_ref[rows[c], sls[h]] = beta[c, h] * dvb[c, h]
        for c in range(G):
            dg_all = jnp.zeros((CHUNK, LANES), F32)
            dbeta_all = jnp.zeros((CHUNK, LANES), F32)
            for h in heads:
                dbeta = _rowsum(k[c, h] * dkb[c, h]) + _rowsum(v[c, h] * dvb[c, h])
                z = _rowsum(kd[c, h] * dkd[c, h])
                dg = (_rowsum(em[c, h]) - colsum[c, h] + _rowsum(qg[c, h] * dqg[c, h]) + _rowsum(kbg[c, h] * dkbg[c, h])
                      - z)
                extra = _allsum(z) + egl[c, h] * sds[c, h]
                dg = dg + jnp.where(rcol == CHUNK - 1, extra, 0.0)
                dg_all = dg_all + jnp.where(lane == h, dg, 0.0)
                dbeta_all = dbeta_all + jnp.where(lane == GDN_H + h, dbeta, 0.0)
            dgs_ref[rows[c], :] = _dx((ci >= ri).astype(F32), dg_all) + dbeta_all

    return pl.pallas_call(
        body, grid=(nb,),
        in_specs=[col(0), col(1), col(2), gate, col(0), st(GDN_D, GDN_D), st(CHUNK, CHUNK)],
        out_specs=[col(0), col(0), col(0), gate],
        out_shape=[jax.ShapeDtypeStruct((Lp, D_MODEL), F32)] * 3 + [jax.ShapeDtypeStruct((Lp, LANES), F32)],
        scratch_shapes=[pltpu.VMEM((GDN_H, GDN_D, GDN_D), F32)],
        name="gdn_chunk_bwd")(qkv, qkv, qkv, gsm, do, s_in, t_in)


def _rot(x, c, s):
    half = RET_D // 2
    x1 = x[:, :half]
    x2 = x[:, half:]
    return jnp.concatenate([x1 * c - x2 * s, x2 * c + x1 * s], axis=1)


def _rot_bwd(d, c, s):
    half = RET_D // 2
    d1 = d[:, :half]
    d2 = d[:, half:]
    return jnp.concatenate([d1 * c + d2 * s, d2 * c - d1 * s], axis=1)


def _ret_tables():
    hh = jnp.arange(RET_H, dtype=F32)
    lg = jnp.log(1.0 - 2.0 ** (-5.0 - hh))
    idx = jnp.arange(CHUNK, dtype=F32)
    tril = jnp.asarray(np.tril(np.ones((CHUNK, CHUNK), dtype=bool)))
    dmask = jnp.where(tril, jnp.exp((idx[:, None] - idx[None, :]) * lg[:, None, None]), 0.0)
    qdec = jnp.exp((idx[None, :] + 1.0) * lg[:, None])
    kdec = jnp.exp((CHUNK - 1.0 - idx[None, :]) * lg[:, None])
    gch = jnp.exp(CHUNK * lg)
    qdec = jnp.broadcast_to(qdec[:, :, None], (RET_H, CHUNK, RET_D))
    kdec = jnp.broadcast_to(kdec[:, :, None], (RET_H, CHUNK, RET_D))
    gch = jnp.broadcast_to(gch[:, None, None], (RET_H, 8, LANES))
    return dmask, qdec, kdec, gch


def _ret_specs(N, rev):
    G = _group(N)
    nb = N // G
    cn = (lambda n: nb - 1 - n) if rev else (lambda n: n)
    col = lambda j: pl.BlockSpec((G * CHUNK, D_MODEL), lambda n: (cn(n), j))
    tab = lambda a, b: pl.BlockSpec((RET_H, a, b), lambda n: (0, 0, 0))
    rope = pl.BlockSpec((G * CHUNK, LANES), lambda n: (cn(n), 0))
    st = pl.BlockSpec((RET_H, G, RET_D, RET_D), lambda n: (0, cn(n), 0, 0))
    return G, nb, col, tab, rope, st


def _ret_chunk_fwd(proj_m, cos, sin, tables):
    Lp = proj_m.shape[0]
    N = Lp // CHUNK
    dmask, qdec, kdec, gch = tables
    G, nb, col, tab, rope, st = _ret_specs(N, False)

    def body(q_ref, k_ref, v_ref, c_ref, s_ref, dm_ref, qd_ref, kd_ref, g_ref, o_ref, sin_ref, S):
        n = pl.program_id(0)

        @pl.when(n == 0)
        def _():
            S[...] = jnp.zeros_like(S)

        heads = range(RET_H)
        sls = [slice(h * RET_D, (h + 1) * RET_D) for h in heads]
        rows = [slice(c * CHUNK, (c + 1) * CHUNK) for c in range(G)]
        pairs = [(c, h) for c in range(G) for h in heads]
        P = lambda f: {p: f(*p) for p in pairs}
        qr = P(lambda c, h: _rot(q_ref[rows[c], sls[h]], c_ref[rows[c], :], s_ref[rows[c], :]))
        ks = P(lambda c, h: _rot(k_ref[rows[c], sls[h]], c_ref[rows[c], :], s_ref[rows[c], :]) * (RET_D ** -0.5))
        v = P(lambda c, h: v_ref[rows[c], sls[h]])
        a = P(lambda c, h: _dnt(qr[c, h], ks[c, h]) * dm_ref[h])
        av = P(lambda c, h: _d(a[c, h], v[c, h]))
        kv = P(lambda c, h: _dtn(ks[c, h] * kd_ref[h], v[c, h]))
        qd = P(lambda c, h: qr[c, h] * qd_ref[h])
        cur = [S[h] for h in heads]
        for c in range(G):
            for h in heads:
                o_ref[rows[c], sls[h]] = av[c, h] + _d(qd[c, h], cur[h])
                sin_ref[h, c] = cur[h].astype(BF16)
            cur = [cur[h] * g_ref[h, 0:1, 0:1] + kv[c, h] for h in heads]
        for h in heads:
            S[h] = cur[h]

    return pl.pallas_call(
        body, grid=(nb,),
        in_specs=[col(3), col(4), col(5), rope, rope,
                  tab(CHUNK, CHUNK), tab(CHUNK, RET_D), tab(CHUNK, RET_D), tab(8, LANES)],
        out_specs=[col(0), st],
        out_shape=[jax.ShapeDtypeStruct((Lp, D_MODEL), F32), jax.ShapeDtypeStruct((RET_H, N, RET_D, RET_D), BF16)],
        scratch_shapes=[pltpu.VMEM((RET_H, RET_D, RET_D), F32)],
        name="ret_chunk_fwd")(proj_m, proj_m, proj_m, cos, sin, dmask, qdec, kdec, gch)


def _ret_chunk_bwd(proj_m, cos, sin, tables, do, s_in):
    Lp = proj_m.shape[0]
    N = Lp // CHUNK
    dmask, qdec, kdec, gch = tables
    G, nb, col, tab, rope, st = _ret_specs(N, True)

    def body(q_ref, k_ref, v_ref, c_ref, s_ref, dm_ref, qd_ref, kd_ref, g_ref, do_ref, sin_ref,
             dq_ref, dk_ref, dv_ref, dS):
        n = pl.program_id(0)

        @pl.when(n == 0)
        def _():
            dS[...] = jnp.zeros_like(dS)

        kscale = RET_D ** -0.5
        heads = range(RET_H)
        sls = [slice(h * RET_D, (h + 1) * RET_D) for h in heads]
        rows = [slice(c * CHUNK, (c + 1) * CHUNK) for c in range(G)]
        pairs = [(c, h) for c in range(G) for h in heads]
        P = lambda f: {p: f(*p) for p in pairs}
        cs = [(c_ref[rows[c], :], s_ref[rows[c], :]) for c in range(G)]
        qr = P(lambda c, h: _rot(q_ref[rows[c], sls[h]], *cs[c]))
        ks = P(lambda c, h: _rot(k_ref[rows[c], sls[h]], *cs[c]) * kscale)
        v = P(lambda c, h: v_ref[rows[c], sls[h]])
        dov = P(lambda c, h: do_ref[rows[c], sls[h]])
        ad = P(lambda c, h: _dnt(qr[c, h], ks[c, h]) * dm_ref[h])
        da = P(lambda c, h: _dnt(dov[c, h], v[c, h]) * dm_ref[h])
        dos = P(lambda c, h: _dnt(dov[c, h], sin_ref[h, c]) * qd_ref[h])
        qdo = P(lambda c, h: _dtn(qr[c, h] * qd_ref[h], dov[c, h]))
        adv = P(lambda c, h: _dtn(ad[c, h], dov[c, h]))
        dqr = P(lambda c, h: _d(da[c, h], ks[c, h]) + dos[c, h])
        daq = P(lambda c, h: _dtn(da[c, h], qr[c, h]))
        kk = P(lambda c, h: ks[c, h] * kd_ref[h])
        cur = [dS[h] for h in heads]
        for c in reversed(range(G)):
            for h in heads:
                dv_ref[rows[c], sls[h]] = (adv[c, h] + _d(kk[c, h], cur[h])).astype(BF16)
                dq_ref[rows[c], sls[h]] = _rot_bwd(dqr[c, h], *cs[c]).astype(BF16)
                dks = daq[c, h] + _dnt(v[c, h], cur[h]) * kd_ref[h]
                dk_ref[rows[c], sls[h]] = _rot_bwd(dks * kscale, *cs[c]).astype(BF16)
            cur = [cur[h] * g_ref[h, 0:1, 0:1] + qdo[c, h] for h in heads]
        for h in heads:
            dS[h] = cur[h]

    return pl.pallas_call(
        body, grid=(nb,),
        in_specs=[col(3), col(4), col(5), rope, rope,
                  tab(CHUNK, CHUNK), tab(CHUNK, RET_D), tab(CHUNK, RET_D), tab(8, LANES), col(0), st],
        out_specs=[col(0), col(0), col(0)],
        out_shape=[jax.ShapeDtypeStruct((Lp, D_MODEL), BF16)] * 3,
        scratch_shapes=[pltpu.VMEM((RET_H, RET_D, RET_D), F32)],
        name="ret_chunk_bwd")(proj_m, proj_m, proj_m, cos, sin, dmask, qdec, kdec, gch, do, s_in)


def _merge_specs(tr):
    col = lambda j: pl.BlockSpec((tr, D_MODEL), lambda i: (i, j))
    return col


def _merge_fwd(o_a, o_b, proj_m, gnorm):
    Lp = o_a.shape[0]
    tr = _tile(Lp, 192, 16)

    def body(oa_ref, ob_ref, gz_ref, rg_ref, ga_ref, gb_ref, gn_ref, y_ref):
        gn = gn_ref[...]
        oa = oa_ref[...]
        ob = ob_ref[...]
        gz = gz_ref[...]
        ya = []
        for j in range(GDN_H):
            seg = oa[:, j * GDN_D:(j + 1) * GDN_D]
            r = lax.rsqrt(jnp.mean(seg * seg, axis=-1, keepdims=True) + EPS)
            ya.append(seg * r * gn)
        ya = jnp.concatenate(ya, axis=1) * (gz * _sig(gz))
        yb = []
        for j in range(RET_H):
            seg = ob[:, j * RET_D:(j + 1) * RET_D]
            r = lax.rsqrt(jnp.mean(seg * seg, axis=-1, keepdims=True) + EPS)
            yb.append(seg * r)
        rg = rg_ref[...]
        yb = jnp.concatenate(yb, axis=1) * (rg * _sig(rg))
        y_ref[...] = (_sig(ga_ref[...]) * ya + _sig(gb_ref[...]) * yb).astype(BF16)

    col = _merge_specs(tr)
    return pl.pallas_call(
        body, grid=(Lp // tr,),
        in_specs=[col(0), col(0), col(6), col(7), col(8), col(9), pl.BlockSpec((1, GDN_D), lambda i: (0, 0))],
        out_specs=col(0), out_shape=jax.ShapeDtypeStruct((Lp, D_MODEL), BF16),
        name="merge_fwd")(o_a, o_b, proj_m, proj_m, proj_m, proj_m, gnorm)


def _merge_bwd(dh1b, w_out, o_a, o_b, proj_m, gnorm):
    Lp = o_a.shape[0]
    tr = _tile(Lp, 192, 16)

    def body(d_ref, wo_ref, oa_ref, ob_ref, gz_ref, rg_ref, ga_ref, gb_ref, gn_ref, dc_ref, doa_ref, dob_ref, dgn_ref):
        i = pl.program_id(0)
        gn = gn_ref[...]
        dyv = lax.dot_general(d_ref[...], wo_ref[...], _NT, preferred_element_type=F32)
        oa = oa_ref[...]
        ob = ob_ref[...]
        gz = gz_ref[...]
        rg = rg_ref[...]
        sa = _sig(ga_ref[...])
        sb = _sig(gb_ref[...])
        dya = dyv * sa
        dyb = dyv * sb
        sgz = _sig(gz)
        szz = gz * sgz
        dgn = jnp.zeros((1, GDN_D), F32)
        ya = []
        dgz = []
        for j in range(GDN_H):
            sl = slice(j * GDN_D, (j + 1) * GDN_D)
            seg = oa[:, sl]
            r = lax.rsqrt(jnp.mean(seg * seg, axis=-1, keepdims=True) + EPS)
            xh = seg * r
            oan = xh * gn
            ya.append(oan * szz[:, sl])
            dgz.append(dya[:, sl] * oan * (sgz[:, sl] * (1.0 + gz[:, sl] * (1.0 - sgz[:, sl]))))
            doan = dya[:, sl] * szz[:, sl]
            dgn = dgn + jnp.sum(doan * xh, axis=0, keepdims=True)
            dxh = doan * gn
            doa_ref[:, sl] = r * (dxh - xh * jnp.mean(dxh * xh, axis=-1, keepdims=True))
        ya = jnp.concatenate(ya, axis=1)
        srg = _sig(rg)
        srr = rg * srg
        yb = []
        drg = []
        for j in range(RET_H):
            sl = slice(j * RET_D, (j + 1) * RET_D)
            seg = ob[:, sl]
            r = lax.rsqrt(jnp.mean(seg * seg, axis=-1, keepdims=True) + EPS)
            xh = seg * r
            yb.append(xh * srr[:, sl])
            drg.append(dyb[:, sl] * xh * (srg[:, sl] * (1.0 + rg[:, sl] * (1.0 - srg[:, sl]))))
            dxh = dyb[:, sl] * srr[:, sl]
            dob_ref[:, sl] = r * (dxh - xh * jnp.mean(dxh * xh, axis=-1, keepdims=True))
        yb = jnp.concatenate(yb, axis=1)
        dc_ref[:, 0:D_MODEL] = jnp.concatenate(dgz, axis=1).astype(BF16)
        dc_ref[:, D_MODEL:2 * D_MODEL] = jnp.concatenate(drg, axis=1).astype(BF16)
        dc_ref[:, 2 * D_MODEL:3 * D_MODEL] = (dyv * ya * sa * (1.0 - sa)).astype(BF16)
        dc_ref[:, 3 * D_MODEL:] = (dyv * yb * sb * (1.0 - sb)).astype(BF16)

        @pl.when(i == 0)
        def _():
            dgn_ref[...] = dgn

        @pl.when(i > 0)
        def _():
            dgn_ref[...] += dgn

    col = _merge_specs(tr)
    return pl.pallas_call(
        body, grid=(Lp // tr,),
        in_specs=[col(0), pl.BlockSpec((D_MODEL, D_MODEL), lambda i: (0, 0), pipeline_mode=pl.Buffered(1)),
                  col(0), col(0), col(6), col(7), col(8), col(9), pl.BlockSpec((1, GDN_D), lambda i: (0, 0))],
        out_specs=[pl.BlockSpec((tr, 4 * D_MODEL), lambda i: (i, 0)), col(0), col(0),
                   pl.BlockSpec((1, GDN_D), lambda i: (0, 0))],
        out_shape=[jax.ShapeDtypeStruct((Lp, 4 * D_MODEL), BF16), jax.ShapeDtypeStruct((Lp, D_MODEL), F32),
                   jax.ShapeDtypeStruct((Lp, D_MODEL), F32), jax.ShapeDtypeStruct((1, GDN_D), F32)],
        name="merge_bwd")(dh1b, w_out, o_a, o_b, proj_m, proj_m, proj_m, proj_m, gnorm)


def _ffn_act(up, conv_w, conv_b):
    Lp = up.shape[0]
    tr = _tile(Lp, 192, 16)
    W2 = 2 * D_FF

    def body(main_ref, prev_ref, w_ref, b_ref, act_ref, u_ref):
        i = pl.program_id(0)
        prev = jnp.where(i > 0, prev_ref[...], 0.0)
        ext = jnp.concatenate([prev, main_ref[...]], axis=0)
        u = _taps(_shifted(ext, range(8 - (FFN_CONV - 1), 9)), w_ref[...], tr, b_ref[...])
        a = u[:, :D_FF]
        act_ref[...] = (a * _sig(a) * u[:, D_FF:]).astype(BF16)
        u_ref[...] = u

    return pl.pallas_call(
        body, grid=(Lp // tr,),
        in_specs=[pl.BlockSpec((tr, W2), lambda i: (i, 0)), _halo_prev(tr, W2),
                  pl.BlockSpec((FFN_CONV, W2), lambda i: (0, 0)), pl.BlockSpec((1, W2), lambda i: (0, 0))],
        out_specs=[pl.BlockSpec((tr, D_FF), lambda i: (i, 0)), pl.BlockSpec((tr, W2), lambda i: (i, 0))],
        out_shape=[jax.ShapeDtypeStruct((Lp, D_FF), BF16), jax.ShapeDtypeStruct((Lp, W2), F32)],
        name="ffn_act")(up, up, conv_w, conv_b)


def _ffn_act_bwd(up, u, dact, conv_w):
    Lp = up.shape[0]
    tr = _tile(Lp, 96, 16)
    W2 = 2 * D_FF
    te = tr + 8

    def body(up_ref, u_ref, un_ref, da_ref, dan_ref, w_ref, dup_ref, acc_ref):
        i = pl.program_id(0)
        w = w_ref[...]
        ue = jnp.concatenate([u_ref[...], un_ref[...]], axis=0)
        a = ue[:, :D_FF]
        b = ue[:, D_FF:]
        rowe = i * tr + lax.broadcasted_iota(jnp.int32, (te, 1), 0)
        dae = jnp.where(rowe < Lp, jnp.concatenate([da_ref[...], dan_ref[...]], axis=0), 0.0)
        sg = _sig(a)
        du = jnp.concatenate([dae * b * (sg * (1.0 + a * (1.0 - sg))), dae * (a * sg)], axis=1)
        dus = _shifted(du, range(FFN_CONV - 1, -1, -1))
        dup_ref[...] = _taps(dus, w, tr).astype(BF16)
        upm = up_ref[...]
        rows = [jnp.sum(dus[kk][0:tr, :] * upm, axis=0, keepdims=True) for kk in range(FFN_CONV)]
        rows.append(jnp.sum(du[0:tr, :], axis=0, keepdims=True))
        part = jnp.concatenate(rows + [jnp.zeros((8 - len(rows), W2), F32)], axis=0)

        @pl.when(i == 0)
        def _():
            acc_ref[...] = part

        @pl.when(i > 0)
        def _():
            acc_ref[...] += part

    return pl.pallas_call(
        body, grid=(Lp // tr,),
        in_specs=[pl.BlockSpec((tr, W2), lambda i: (i, 0)), pl.BlockSpec((tr, W2), lambda i: (i, 0)),
                  _halo_next(tr, W2, Lp), pl.BlockSpec((tr, D_FF), lambda i: (i, 0)), _halo_next(tr, D_FF, Lp),
                  pl.BlockSpec((FFN_CONV, W2), lambda i: (0, 0))],
        out_specs=[pl.BlockSpec((tr, W2), lambda i: (i, 0)), pl.BlockSpec((8, W2), lambda i: (0, 0))],
        out_shape=[jax.ShapeDtypeStruct((Lp, W2), BF16), jax.ShapeDtypeStruct((8, W2), F32)],
        name="ffn_act_bwd")(up, u, u, dact, dact, conv_w)


def _local_step(hpad, tgt, pad, wt, first_weights=None, late_weights=None, on_ffn_out_grads=None,
                on_w_in_grads=None):
    Lp = hpad.shape[0]
    first = pad + N_META
    pos = jnp.arange(Lp, dtype=F32) - float(pad)
    half = RET_D // 2
    inv = 1.0 / (ROPE_BASE ** (jnp.arange(half, dtype=F32) / half))
    ang = pos[:, None] * inv[None, :]
    cos, sin = jnp.cos(ang), jnp.sin(ang)
    tables = _ret_tables()
    gparams = jnp.zeros((8, LANES), F32).at[0, :GDN_H].set(wt["a_log"]).at[1, :GDN_H].set(wt["dt_bias"])

    hn1 = _rms_fwd(hpad, wt["norm1"], "rms1_fwd")
    if first_weights is not None:
        wt = {**wt, **first_weights(hn1)}
    proj_m = _mm_nn(hn1, wt["w_main_t"], bt=True, name="proj_main")
    proj_s = _mm_nn(hn1, wt["w_small_t"], bt=True, name="proj_small")
    qkv, gsm, conv_out = _gdn_pre(proj_m, proj_s, wt["gdn_conv_w"], gparams, pad)
    o_a, s_a, t_a = _gdn_chunk_fwd(qkv, gsm)
    o_b, s_b = _ret_chunk_fwd(proj_m, cos, sin, tables)
    y = _merge_fwd(o_a, o_b, proj_m, wt["gdn_norm"])
    if late_weights is not None:
        wt = {**wt, **late_weights(y)}
    h1, hn2 = _mm_rms_fwd(_Producer(y, wt["w_out"], hpad), wt["norm2"], "out_proj_rms2")
    up = _mm_nn(hn2, wt["w_up_t"], bt=True, name="ffn_up")
    act, u_ffn = _ffn_act(up, wt["ffn_conv_w"], wt["ffn_conv_b"])
    lossvec, dh2, dh2b, d_norm_f = _final(_Producer(act, wt["w_down"], h1), wt["norm_f"], tgt, first)

    d_w_down = _mm_tn(act, dh2b, name="dw_down")
    dact = _mm_nt(dh2b, wt["w_down"], name="d_act")
    dup, ffn_rows = _ffn_act_bwd(up, u_ffn, dact, wt["ffn_conv_w"])
    d_w_up_t = _mm_tn(dup, hn2, name="dw_up")
    dh1, dh1b, d_norm2 = _rms_bwd(h1, wt["norm2"], _Producer(dup, wt["w_up_t"]), dh2, pad, "d_hn2_rms2_bwd")

    d_w_out = _mm_tn(y, dh1b, name="dw_out")
    gnorm = wt["gdn_norm"]
    if on_ffn_out_grads is not None:
        gnorm = gnorm + on_ffn_out_grads(d_w_down, d_w_up_t, d_w_out)[0:1, :]
    d_c, do_a, do_b, d_gnorm = _merge_bwd(dh1b, wt["w_out"], o_a, o_b, proj_m, gnorm)
    drq, drk, drv = _ret_chunk_bwd(proj_m, cos, sin, tables, do_b, s_b)
    dq, dk, dv, dgs = _gdn_chunk_bwd(qkv, gsm, do_a, s_a, t_a)
    d_a, d_s, conv_rows, gp_rows = _gdn_pre_bwd(proj_m, conv_out, proj_s, wt["gdn_conv_w"], gparams, dq, dk, dv, dgs,
                                                pad)

    wmt = wt["w_main_t"]
    segs = [(d_a, 0, 3 * D_MODEL), (drq, 3 * D_MODEL, D_MODEL), (drk, 4 * D_MODEL, D_MODEL),
            (drv, 5 * D_MODEL, D_MODEL), (d_c, 6 * D_MODEL, 4 * D_MODEL)]
    pa, prq, prk, prv, pc = [_mm_tn(d, hn1, BF16, name="dw_in_%d" % i) for i, (d, _, _) in enumerate(segs)]
    ps = _mm_tn(d_s, hn1, BF16, name="dw_in_small")
    d_w_in_t = jnp.concatenate([pa, pc[:D_MODEL], ps[:2 * GDN_H], prq, prk, prv, pc[D_MODEL:]], axis=0)
    w_small_t = wt["w_small_t"]
    if on_w_in_grads is not None:
        w_small_t = w_small_t + on_w_in_grads(d_w_in_t)[0:1, 0:1].astype(w_small_t.dtype)
    dhn1 = _mm_nn(d_s, w_small_t, name="d_hn1_small")
    for i, (d, off, width) in enumerate(segs[:-1]):
        dhn1 = _mm_nn(d, wmt[off:off + width], res=dhn1, name="d_hn1_%d" % i)
    d, off, width = segs[-1]
    dh0, _, d_norm1 = _rms_bwd(hpad, wt["norm1"], _Producer(d, wmt[off:off + width], dhn1), dh1, pad,
                               "d_hn1_rms1_bwd")

    grads = {
        "norm1": d_norm1, "w_in_t": d_w_in_t, "gdn_conv_w": conv_rows[:GDN_CONV],
        "a_log": gp_rows[0, :GDN_H], "dt_bias": gp_rows[1, :GDN_H], "gdn_norm": d_gnorm, "w_out": d_w_out,
        "norm2": d_norm2, "w_up_t": d_w_up_t, "ffn_conv_w": ffn_rows[:FFN_CONV],
        "ffn_conv_b": ffn_rows[FFN_CONV:FFN_CONV + 1], "w_down": d_w_down, "norm_f": d_norm_f,
    }
    return lossvec, dh0, grads


def _peer(k):
    ix, iy, ic = lax.axis_index("x"), lax.axis_index("y"), lax.axis_index("c")
    px = 1 - ix if (k >> 2) & 1 else ix
    py = 1 - iy if (k >> 1) & 1 else iy
    pc = 1 - ic if k & 1 else ic
    return (px, py, pc), 4 * px + 2 * py + pc


def _comm_call(body, n, out_shapes, name, args):
    hbm = pl.BlockSpec(memory_space=pl.ANY)
    return pl.pallas_call(
        body, out_shape=out_shapes, in_specs=[hbm] * n, out_specs=[hbm] * n,
        scratch_shapes=[pltpu.SemaphoreType.DMA((n, N_DEV - 1)), pltpu.SemaphoreType.DMA((n, N_DEV - 1)),
                        pltpu.SemaphoreType.DMA((n,))],
        name=name)(*args)


def _all_gather(xs, name):
    n = len(xs)

    def body(*refs):
        x_refs, out_refs = refs[:n], refs[n:2 * n]
        send_sems, recv_sems, local_sems = refs[2 * n:]
        _, me = _peer(0)
        pending = []
        for i in range(n):
            local = pltpu.make_async_copy(x_refs[i], out_refs[i].at[me], local_sems.at[i])
            local.start()
            pending.append(local)
        sends = []
        for i in range(n):
            for k in range(1, N_DEV):
                dev, _ = _peer(k)
                cp = pltpu.make_async_remote_copy(
                    src_ref=x_refs[i], dst_ref=out_refs[i].at[me], send_sem=send_sems.at[i, k - 1],
                    recv_sem=recv_sems.at[i, k - 1], device_id=dev, device_id_type=MESH_T)
                cp.start()
                sends.append(cp)
        for i in range(n):
            for k in range(1, N_DEV):
                dev, idx = _peer(k)
                pltpu.make_async_remote_copy(
                    src_ref=x_refs[i], dst_ref=out_refs[i].at[idx], send_sem=send_sems.at[i, k - 1],
                    recv_sem=recv_sems.at[i, k - 1], device_id=dev, device_id_type=MESH_T).wait_recv()
        for cp in sends:
            cp.wait_send()
        for local in pending:
            local.wait()

    out_shapes = [jax.ShapeDtypeStruct((N_DEV,) + a.shape, a.dtype) for a in xs]
    return _comm_call(body, n, out_shapes, name, xs)


def _all_to_all(gs, name):
    n = len(gs)

    def body(*refs):
        g_refs, out_refs = refs[:n], refs[n:2 * n]
        send_sems, recv_sems, local_sems = refs[2 * n:]
        _, me = _peer(0)
        pending = []
        for i in range(n):
            local = pltpu.make_async_copy(g_refs[i].at[me], out_refs[i].at[0], local_sems.at[i])
            local.start()
            pending.append(local)
        sends = []
        for i in range(n):
            for k in range(1, N_DEV):
                dev, idx = _peer(k)
                cp = pltpu.make_async_remote_copy(
                    src_ref=g_refs[i].at[idx], dst_ref=out_refs[i].at[k], send_sem=send_sems.at[i, k - 1],
                    recv_sem=recv_sems.at[i, k - 1], device_id=dev, device_id_type=MESH_T)
                cp.start()
                sends.append(cp)
        for cp in sends:
            cp.wait_recv()
        for cp in sends:
            cp.wait_send()
        for local in pending:
            local.wait()

    out_shapes = [jax.ShapeDtypeStruct(g.shape, g.dtype) for g in gs]
    return _comm_call(body, n, out_shapes, name, gs)


_SPLIT_RELATIONS = {"gather": tuple(range(1, N_DEV)), "a2a": tuple(range(1, N_DEV)), "chip": (1, 2, 4, 6),
                    "forward": (2, 4, 6)}


def _split_copies(kind, src_refs, land_refs, send_sems, recv_sems, local_sems, with_recv):
    n = len(land_refs)
    rels = _SPLIT_RELATIONS[kind]
    _, me = _peer(0)
    locals_, remotes = [], []
    for i in range(n):
        if kind in ("gather", "chip"):
            locals_.append(pltpu.make_async_copy(src_refs[i], land_refs[i].at[me], local_sems.at[i]))
        elif kind == "a2a":
            locals_.append(pltpu.make_async_copy(src_refs[i].at[me], land_refs[i].at[0], local_sems.at[i]))
        for jj, k in enumerate(rels):
            dev, idx = _peer(k)
            if kind in ("gather", "chip"):
                src, dst, mine = src_refs[i], land_refs[i].at[me], land_refs[i].at[idx]
            elif kind == "a2a":
                src, dst, mine = src_refs[i].at[idx], land_refs[i].at[k], land_refs[i].at[k]
            else:
                dev, _ = _peer(1)
                _, came = _peer(k + 1)
                src, dst, mine = land_refs[i].at[idx], land_refs[i].at[idx], land_refs[i].at[came]
            j = i * len(rels) + jj
            send = pltpu.make_async_remote_copy(
                src_ref=src, dst_ref=dst, send_sem=send_sems.at[j], recv_sem=recv_sems.at[j],
                device_id=dev, device_id_type=MESH_T)
            recv = pltpu.make_async_remote_copy(
                src_ref=src, dst_ref=mine, send_sem=send_sems.at[j], recv_sem=recv_sems.at[j],
                device_id=dev, device_id_type=MESH_T) if with_recv else None
            remotes.append((send, recv))
    return locals_, remotes


_HBM = pl.BlockSpec(memory_space=pltpu.HBM)
_SEM = pl.BlockSpec(memory_space=pltpu.SEMAPHORE)
_ANY = pl.BlockSpec(memory_space=pl.ANY)


def _split_start(srcs, kind, name, after):
    n = len(srcs)
    if kind == "forward":
        arrays = list(srcs)
    else:
        gathers = kind in ("gather", "chip")
        arrays = list(srcs) + [lax.empty(((N_DEV,) + a.shape) if gathers else a.shape, a.dtype) for a in srcs]
    na = len(arrays)

    def body(*refs):
        src_refs, land_refs = refs[:n], refs[na - n:na]
        send_sems, recv_sems, local_sems = refs[na + 1:na + 4]
        token = refs[-1]
        locals_, remotes = _split_copies(kind, src_refs, land_refs, send_sems, recv_sems, local_sems, False)
        for cp in locals_:
            cp.start()
        for send, _ in remotes:
            send.start()
        token[...] = jnp.zeros_like(token)

    ncp = n * len(_SPLIT_RELATIONS[kind])
    sems = (pltpu.SemaphoreType.DMA((ncp,)), pltpu.SemaphoreType.DMA((ncp,)), pltpu.SemaphoreType.DMA((n,)))
    thru = tuple(pltpu.HBM(a.shape, a.dtype) for a in arrays)
    outs = pl.pallas_call(
        body, name=name,
        out_shape=sems + thru + (jax.ShapeDtypeStruct((8, LANES), F32),),
        in_specs=[_HBM] * na + [_ANY],
        out_specs=[_SEM] * 3 + [_HBM] * na + [pl.BlockSpec(memory_space=pltpu.VMEM)],
        input_output_aliases={i: 3 + i for i in range(na)},
        compiler_params=pltpu.CompilerParams(has_side_effects=pltpu.SideEffectType.DATAFLOW_SIDE_EFFECTING),
    )(*[pltpu.with_memory_space_constraint(a, pltpu.HBM) for a in arrays], after)
    return (kind, n, outs[:3], outs[3:3 + na]), outs[-1]


def _split_wait(handle, name, after):
    kind, n, sems, thru = handle
    na = len(thru)

    def body(*refs):
        src_refs, land_refs = refs[:n], refs[na - n:na]
        send_sems, recv_sems, local_sems = refs[na:na + 3]
        locals_, remotes = _split_copies(kind, src_refs, land_refs, send_sems, recv_sems, local_sems, True)
        for send, recv in remotes:
            send.wait_send()
            recv.wait_recv()
        for cp in locals_:
            cp.wait()

    outs = pl.pallas_call(
        body, name=name, out_shape=tuple(pltpu.HBM(a.shape, a.dtype) for a in thru),
        in_specs=[_HBM] * na + [_SEM] * 3 + [_ANY], out_specs=[_HBM] * na,
        input_output_aliases={i: i for i in range(na)},
        compiler_params=pltpu.CompilerParams(has_side_effects=pltpu.SideEffectType.DATAFLOW_SIDE_EFFECTING),
    )(*thru, *sems, after)
    return list(outs[na - n:])


def _adamw(gslabs, w, m, v, name):
    R, Cw = w.shape
    if R % 8 == 0:
        tr, tc = _tile(R, 64 if Cw > 1024 else 128, 8), Cw
    else:
        tr, tc = R, LANES
    c1 = 1.0 - ADAM_B1 ** ADAM_STEP
    c2 = 1.0 - ADAM_B2 ** ADAM_STEP

    def body(g_ref, w_ref, m_ref, v_ref, go_ref, d_ref, mo_ref, vo_ref):
        g = g_ref[0].astype(F32)
        for k in range(1, N_DEV):
            g = g + g_ref[k].astype(F32)
        mn = ADAM_B1 * m_ref[...] + (1.0 - ADAM_B1) * g
        vn = ADAM_B2 * v_ref[...] + (1.0 - ADAM_B2) * (g * g)
        m_hat = mn / c1
        v_hat = vn / c2
        go_ref[...] = g
        d_ref[...] = -ADAM_LR * (m_hat / (jnp.sqrt(v_hat) + ADAM_EPS) + ADAM_WD * w_ref[...])
        mo_ref[...] = mn
        vo_ref[...] = vn

    blk = pl.BlockSpec((tr, tc), lambda i, j: (i, j))
    return pl.pallas_call(
        body, grid=(R // tr, Cw // tc),
        in_specs=[pl.BlockSpec((N_DEV, tr, tc), lambda i, j: (0, i, j)), blk, blk, blk],
        out_specs=[blk] * 4, out_shape=[jax.ShapeDtypeStruct((R, Cw), F32)] * 4, name=name)(gslabs, w, m, v)


def _pack(arrs, row_mult, dtype=F32):
    parts = []
    total = 0
    for a in arrs:
        f = a.reshape(-1).astype(dtype)
        n = -(-f.shape[0] // 1024) * 1024
        parts.append(jnp.pad(f, (0, n - f.shape[0])))
        total += n
    rows = total // LANES
    rows_p = -(-rows // row_mult) * row_mult
    flat = jnp.concatenate(parts)
    flat = jnp.pad(flat, (0, rows_p * LANES - total))
    return flat.reshape(rows_p, LANES)


def _unpack(packed, shapes):
    lead = packed.shape[:-2]
    flat = packed.reshape(lead + (-1,))
    out = []
    off = 0
    for s in shapes:
        n = int(np.prod(s))
        out.append(flat[..., off:off + n].reshape(lead + tuple(s)))
        off += -(-n // 1024) * 1024
    return out


def _gather_cols(stacked):
    d, r, c = stacked.shape
    return stacked.transpose(1, 0, 2).reshape(r, d * c)


def _scatter_cols(full):
    r, n = full.shape
    return full.reshape(r, N_DEV, n // N_DEV).transpose(1, 0, 2)


def kernel(x, meta, norm1, w_in, gdn_conv_w, gdn_a_log, gdn_dt_bias, gdn_norm, w_out, norm2, w_ffn_up, ffn_conv_w, ffn_conv_b, w_ffn_down, norm_f, loss_target, m_meta, m_norm1, m_w_in, m_gdn_conv_w, m_gdn_a_log, m_gdn_dt_bias, m_gdn_norm, m_w_out, m_norm2, m_w_ffn_up, m_ffn_conv_w, m_ffn_conv_b, m_w_ffn_down, m_norm_f, v_meta, v_norm1, v_w_in, v_gdn_conv_w, v_gdn_a_log, v_gdn_dt_bias, v_gdn_norm, v_w_out, v_norm2, v_w_ffn_up, v_ffn_conv_w, v_ffn_conv_b, v_w_ffn_down, v_norm_f):
    S = x.shape[1]
    L = N_META + S
    pad = (-L) % CHUNK
    Lp = L + pad

    tr_ = lambda a: jnp.swapaxes(a[0], 0, 1)
    big = [tr_(w_in), w_out[0], tr_(w_ffn_up), w_ffn_down[0]]
    small = [meta, gdn_conv_w, ffn_conv_w]
    small_all, = _all_gather([_pack(small, 8)], "gather_small_weights")
    first, first_token = _split_start([big[0].astype(BF16)], "chip", "gather_w_in_start", small_all)
    late, late_token = _split_start([a.astype(BF16) for a in big[1:]], "gather", "gather_late_start", first_token)

    def first_weights(after):
        half = _split_wait(first, "gather_w_in_wait", after)
        second, second_token = _split_start(half, "forward", "gather_w_in_forward_start", after)
        w_in_s, = _split_wait(second, "gather_w_in_forward_wait", second_token)
        w_in_t = w_in_s.reshape(_O_END, D_MODEL)
        w_main_t = jnp.concatenate([w_in_t[_O_GQ:_O_GZ], w_in_t[_O_RQ:_O_RG], w_in_t[_O_GZ:_O_GA],
                                    w_in_t[_O_RG:_O_END]], axis=0)
        return {"w_main_t": w_main_t, "w_small_t": jnp.pad(w_in_t[_O_GA:_O_RQ], ((0, LANES - 2 * GDN_H), (0, 0)))}

    def late_weights(after):
        w_out_s, w_up_s, w_down_s = _split_wait(late, "gather_late_wait", after)
        return {"w_out": w_out_s.reshape(D_MODEL, D_MODEL), "w_up_t": w_up_s.reshape(2 * D_FF, D_MODEL),
                "w_down": w_down_s.reshape(D_FF, D_MODEL)}

    meta_s, gconv_s, fconv_s = _unpack(small_all, [a.shape for a in small])
    wt = {
        "norm1": norm1 + jnp.tile(late_token[0:1, :], (1, D_MODEL // LANES)),
        "gdn_conv_w": _gather_cols(gconv_s[:, 0]), "a_log": gdn_a_log[0], "dt_bias": gdn_dt_bias[0],
        "gdn_norm": gdn_norm, "norm2": norm2, "ffn_conv_w": _gather_cols(fconv_s[:, 0]), "ffn_conv_b": ffn_conv_b,
        "norm_f": norm_f.reshape(1, D_MODEL),
    }
    meta_f = _gather_cols(meta_s)

    pending = {}

    def on_ffn_out_grads(d_w_down, d_w_up_t, d_w_out):
        srcs = [d_w_out.reshape(N_DEV, D_MODEL // N_DEV, D_MODEL), d_w_up_t.reshape(N_DEV, 2 * D_FF // N_DEV, D_MODEL),
                d_w_down.reshape(N_DEV, D_FF // N_DEV, D_MODEL)]
        pending["ffn_out"], token = _split_start(srcs, "a2a", "exchange_ffn_out_start", d_w_out)
        return token

    def on_w_in_grads(d_w_in_t):
        slabs = d_w_in_t.astype(BF16).reshape(N_DEV, _O_END // N_DEV, D_MODEL)
        pending["w_in"], token = _split_start([slabs], "a2a", "exchange_w_in_start", d_w_in_t)
        return token

    hpad = jnp.concatenate([jnp.zeros((pad, D_MODEL), F32), meta_f, x[0]], axis=0)
    tgt = jnp.concatenate([jnp.zeros((pad + N_META, D_MODEL), F32), loss_target[0]], axis=0)
    lossvec, dh0, gr = _local_step(hpad, tgt, pad, wt, first_weights, late_weights, on_ffn_out_grads, on_w_in_grads)

    loss = lax.psum(jnp.sum(lossvec), ("x", "y", "c"))
    grad_x = dh0[pad + N_META:][None]

    big_m = [tr_(m_w_in), m_w_out[0], tr_(m_w_ffn_up), m_w_ffn_down[0]]
    big_v = [tr_(v_w_in), v_w_out[0], tr_(v_w_ffn_up), v_w_ffn_down[0]]
    slabs_ffn_out = _split_wait(pending["ffn_out"], "exchange_ffn_out_wait", dh0)
    big_out = [None] + [_adamw(slabs_ffn_out[i - 1], big[i], big_m[i], big_v[i], "adamw_big_%d" % i)
                        for i in range(1, len(big))]
    g_sm = [_scatter_cols(dh0[pad:pad + N_META]), _scatter_cols(gr["gdn_conv_w"]), _scatter_cols(gr["ffn_conv_w"])]
    g_small = jnp.stack([_pack([g[d] for g in g_sm], 8) for d in range(N_DEV)])
    slabs_small, = _all_to_all([g_small], "exchange_small_gradients")
    small_out = _adamw(slabs_small, _pack(small, 8), _pack([m_meta, m_gdn_conv_w, m_ffn_conv_w], 8),
                       _pack([v_meta, v_gdn_conv_w, v_ffn_conv_w], 8), "adamw_small_sharded")
    small_un = [_unpack(o, [a.shape for a in small]) for o in small_out]
    rep_w = [norm1, gdn_a_log, gdn_dt_bias, gdn_norm, norm2, ffn_conv_b, norm_f]
    rep_m = [m_norm1, m_gdn_a_log, m_gdn_dt_bias, m_gdn_norm, m_norm2, m_ffn_conv_b, m_norm_f]
    rep_v = [v_norm1, v_gdn_a_log, v_gdn_dt_bias, v_gdn_norm, v_norm2, v_ffn_conv_b, v_norm_f]
    rep_g = [gr["norm1"], gr["a_log"], gr["dt_bias"], gr["gdn_norm"], gr["norm2"], gr["ffn_conv_b"], gr["norm_f"]]
    rep_slabs, = _all_gather([_pack(rep_g, 8)], "gather_small_gradients")
    rep_out = _adamw(rep_slabs, _pack(rep_w, 8), _pack(rep_m, 8), _pack(rep_v, 8), "adamw_replicated")
    rep_shapes = [a.shape for a in rep_w]
    rp_g, rp_d, rp_nm, rp_nv = [_unpack(o, rep_shapes) for o in rep_out]

    slabs_w_in, = _split_wait(pending["w_in"], "exchange_w_in_wait", rep_out[0])
    big_out[0] = _adamw(slabs_w_in, big[0], big_m[0], big_v[0], "adamw_big_0")
    back = lambda a: jnp.swapaxes(a, 0, 1)[None]
    sh_g, sh_d, sh_nm, sh_nv = [
        [small_un[j][0], back(big_out[0][j]), small_un[j][1], big_out[1][j][None], back(big_out[2][j]),
         small_un[j][2], big_out[3][j][None]] for j in range(4)]

    def order(sh, rp):
        return [sh[0], rp[0], sh[1], sh[2], rp[1], rp[2], rp[3], sh[3], rp[4], sh[4], sh[5], rp[5], sh[6], rp[6]]

    return (loss, grad_x, *order(sh_g, rp_g), *order(sh_d, rp_d), *order(sh_nm, rp_nm), *order(sh_nv, rp_nv))
```

```python
import functools
import math

import numpy as np
import jax
import jax.numpy as jnp
from jax import lax
from jax.experimental import pallas as pl
from jax.experimental.pallas import tpu as pltpu

F32 = jnp.float32
BF16 = jnp.bfloat16
HI = lax.Precision.HIGHEST

D_MODEL = 1024
N_META = 16
CHUNK = 64
GDN_H = 8
GDN_D = 128
RET_H = 4
RET_D = 256
D_FF = 2816
GDN_CONV = 4
FFN_CONV = 3
ROPE_BASE = 10000.0
EPS = 1e-6
N_DEV = 8
LANES = 128
MAIN_W = 10 * 1024
_O_GQ, _O_GZ, _O_GA, _O_RQ, _O_RG, _O_GATE, _O_END = 0, 3072, 4096, 4112, 7184, 8208, 10256

ADAM_LR = 0.001
ADAM_B1 = 0.9
ADAM_B2 = 0.999
ADAM_EPS = 1e-08
ADAM_WD = 0.01
ADAM_STEP = 10

MESH_T = pl.DeviceIdType.MESH


def _tile(n, target, mult):
    best = None
    for d in range(mult, min(n, target) + 1, mult):
        if n % d == 0:
            best = d
    assert best is not None, (n, target, mult)
    return best


def _sig(x):
    return 1.0 / (1.0 + jnp.exp(-x))


def _d(a, b):
    return jnp.dot(a.astype(BF16), b.astype(BF16), preferred_element_type=F32)


def _dnt(a, b):
    return lax.dot_general(a.astype(BF16), b.astype(BF16), (((1,), (1,)), ((), ())), preferred_element_type=F32)


def _dtn(a, b):
    return lax.dot_general(a.astype(BF16), b.astype(BF16), (((0,), (0,)), ((), ())), preferred_element_type=F32)


def _dx(a, b):
    return jnp.dot(a, b, preferred_element_type=F32, precision=HI)


def _dxnt(a, b):
    return lax.dot_general(a, b, (((1,), (1,)), ((), ())), preferred_element_type=F32, precision=HI)


def _dxtn(a, b):
    return lax.dot_general(a, b, (((0,), (0,)), ((), ())), preferred_element_type=F32, precision=HI)


def _split(a):
    hi = a.astype(BF16)
    return hi, (a - hi.astype(F32)).astype(BF16)


def _d3g(a, b, dims):
    ah, al = _split(a)
    bh, bl = _split(b)
    f = functools.partial(lax.dot_general, dimension_numbers=dims, preferred_element_type=F32)
    if dims == _NN:
        rows = a.shape[0]
        both = f(jnp.concatenate([ah, al], axis=0), bh)
        return both[:rows] + (f(ah, bl) + both[rows:])
    return f(ah, bh) + (f(ah, bl) + f(al, bh))


_NN = (((1,), (0,)), ((), ()))
_NT = (((1,), (1,)), ((), ()))
_TN = (((0,), (0,)), ((), ()))


def _rowsum(x):
    return jnp.sum(x, axis=1, keepdims=True)


def _allsum(x):
    return jnp.sum(jnp.sum(x, axis=1, keepdims=True), axis=0, keepdims=True)


def _mm_nn(a, b, res=None, out_dtype=F32, bt=False, name="mm_nn"):
    M, K = a.shape
    N = b.shape[0] if bt else b.shape[1]
    tm = _tile(M, 704, 16)
    tn = _tile(N, 2816, 128)

    def body(*refs):
        if res is None:
            a_ref, b_ref, o_ref = refs
        else:
            a_ref, b_ref, r_ref, o_ref = refs
        acc = lax.dot_general(a_ref[...], b_ref[...], _NT if bt else _NN, preferred_element_type=F32)
        if res is not None:
            acc = acc + r_ref[...]
        o_ref[...] = acc.astype(out_dtype)

    b_spec = pl.BlockSpec((tn, K), lambda j, i: (j, 0)) if bt else pl.BlockSpec((K, tn), lambda j, i: (0, j))
    in_specs = [pl.BlockSpec((tm, K), lambda j, i: (i, 0)), b_spec]
    args = [a, b]
    if res is not None:
        in_specs.append(pl.BlockSpec((tm, tn), lambda j, i: (i, j)))
        args.append(res)
    return pl.pallas_call(
        body, grid=(N // tn, M // tm), in_specs=in_specs,
        out_specs=pl.BlockSpec((tm, tn), lambda j, i: (i, j)),
        out_shape=jax.ShapeDtypeStruct((M, N), out_dtype), name=name)(*args)


def _mm_nt(a, b, res=None, name="mm_nt"):
    M, Nc = a.shape
    K = b.shape[0]
    tm = _tile(M, 704, 16)
    tc = _tile(Nc, 5632, 128)

    def body(*refs):
        if res is None:
            a_ref, b_ref, o_ref = refs
        else:
            a_ref, b_ref, r_ref, o_ref = refs
        c = pl.program_id(1)
        p = lax.dot_general(a_ref[...], b_ref[...], (((1,), (1,)), ((), ())), preferred_element_type=F32)

        @pl.when(c == 0)
        def _():
            if res is None:
                o_ref[...] = p
            else:
                o_ref[...] = p + r_ref[...]

        @pl.when(c > 0)
        def _():
            o_ref[...] += p

    in_specs = [pl.BlockSpec((tm, tc), lambda i, c: (i, c)), pl.BlockSpec((K, tc), lambda i, c: (0, c))]
    args = [a, b]
    if res is not None:
        in_specs.append(pl.BlockSpec((tm, K), lambda i, c: (i, 0)))
        args.append(res)
    return pl.pallas_call(
        body, grid=(M // tm, Nc // tc), in_specs=in_specs,
        out_specs=pl.BlockSpec((tm, K), lambda i, c: (i, 0)),
        out_shape=jax.ShapeDtypeStruct((M, K), F32), name=name)(*args)


def _mm_tn(a, b, out_dtype=F32, name="mm_tn"):
    M, K = a.shape
    N = b.shape[1]
    tm = _tile(M, 2752, 16)
    tk = _tile(K, 1408, 128)
    tn = _tile(N, 1408, 128)
    steps = M // tm

    def body(a_ref, b_ref, o_ref, *scratch):
        acc = scratch[0] if scratch else o_ref
        m = pl.program_id(2)
        p = lax.dot_general(a_ref[...], b_ref[...], (((0,), (0,)), ((), ())), preferred_element_type=F32)

        @pl.when(m == 0)
        def _():
            acc[...] = p

        @pl.when(m > 0)
        def _():
            acc[...] += p

        if scratch:
            @pl.when(m == steps - 1)
            def _():
                o_ref[...] = acc[...].astype(out_dtype)

    return pl.pallas_call(
        body, grid=(K // tk, N // tn, steps),
        in_specs=[pl.BlockSpec((tm, tk), lambda kk, j, m: (m, kk)), pl.BlockSpec((tm, tn), lambda kk, j, m: (m, j))],
        out_specs=pl.BlockSpec((tk, tn), lambda kk, j, m: (kk, j)),
        out_shape=jax.ShapeDtypeStruct((K, N), out_dtype),
        scratch_shapes=[] if out_dtype == F32 else [pltpu.VMEM((tk, tn), F32)], name=name)(a, b)


def _rms_fwd(x, g, name):
    Lp = x.shape[0]
    tr = _tile(Lp, 256, 16)

    def body(x_ref, g_ref, o_ref):
        xv = x_ref[...]
        r = lax.rsqrt(jnp.mean(xv * xv, axis=-1, keepdims=True) + EPS)
        o_ref[...] = (xv * r * g_ref[...]).astype(BF16)

    return pl.pallas_call(
        body, grid=(Lp // tr,),
        in_specs=[pl.BlockSpec((tr, D_MODEL), lambda i: (i, 0)), pl.BlockSpec((1, D_MODEL), lambda i: (0, 0))],
        out_specs=pl.BlockSpec((tr, D_MODEL), lambda i: (i, 0)),
        out_shape=jax.ShapeDtypeStruct((Lp, D_MODEL), BF16), name=name)(x, g)


class _Producer:
    def __init__(self, a, b, res=None):
        self.a, self.b, self.res = a, b, res
        self.tr = _tile(a.shape[0], 704, 16)
        K = a.shape[1]
        self.args = [a, b] + ([] if res is None else [res])
        self.specs = [pl.BlockSpec((self.tr, K), lambda i: (i, 0)),
                      pl.BlockSpec((K, D_MODEL), lambda i: (0, 0), pipeline_mode=pl.Buffered(1))]
        if res is not None:
            self.specs.append(pl.BlockSpec((self.tr, D_MODEL), lambda i: (i, 0)))

    def tile(self, refs):
        acc = jnp.dot(refs[0][...], refs[1][...], preferred_element_type=F32)
        return acc if self.res is None else acc + refs[2][...]


def _mm_rms_fwd(prod, g, name):
    Lp, tr, n = prod.a.shape[0], prod.tr, len(prod.args)

    def body(*refs):
        g_ref, x_ref, o_ref = refs[n:]
        xv = prod.tile(refs[:n])
        r = lax.rsqrt(jnp.mean(xv * xv, axis=-1, keepdims=True) + EPS)
        x_ref[...] = xv
        o_ref[...] = (xv * r * g_ref[...]).astype(BF16)

    blk = pl.BlockSpec((tr, D_MODEL), lambda i: (i, 0))
    return pl.pallas_call(
        body, grid=(Lp // tr,), in_specs=prod.specs + [pl.BlockSpec((1, D_MODEL), lambda i: (0, 0))],
        out_specs=[blk, blk],
        out_shape=[jax.ShapeDtypeStruct((Lp, D_MODEL), F32), jax.ShapeDtypeStruct((Lp, D_MODEL), BF16)],
        name=name)(*prod.args, g)


def _rms_bwd(x, g, dy, dres, pad, name):
    Lp = x.shape[0]
    fused = isinstance(dy, _Producer)
    tr = dy.tr if fused else _tile(Lp, 256, 16)
    n = len(dy.args) if fused else 1

    def body(*refs):
        x_ref, g_ref, dr_ref, dx_ref, dxb_ref, dg_ref = refs[n:]
        i = pl.program_id(0)
        xv = x_ref[...]
        r = lax.rsqrt(jnp.mean(xv * xv, axis=-1, keepdims=True) + EPS)
        xh = xv * r
        dyv = dy.tile(refs[:n]) if fused else refs[0][...]
        dxh = dyv * g_ref[...]
        dx = r * (dxh - xh * jnp.mean(dxh * xh, axis=-1, keepdims=True)) + dr_ref[...]
        row = i * tr + lax.broadcasted_iota(jnp.int32, (tr, 1), 0)
        dx = jnp.where(row >= pad, dx, 0.0)
        dx_ref[...] = dx
        dxb_ref[...] = dx.astype(BF16)
        part = jnp.sum(dyv * xh, axis=0, keepdims=True)

        @pl.when(i == 0)
        def _():
            dg_ref[...] = part

        @pl.when(i > 0)
        def _():
            dg_ref[...] += part

    blk = pl.BlockSpec((tr, D_MODEL), lambda i: (i, 0))
    vec = pl.BlockSpec((1, D_MODEL), lambda i: (0, 0))
    return pl.pallas_call(
        body, grid=(Lp // tr,), in_specs=(dy.specs if fused else [blk]) + [blk, vec, blk], out_specs=[blk, blk, vec],
        out_shape=[jax.ShapeDtypeStruct((Lp, D_MODEL), F32), jax.ShapeDtypeStruct((Lp, D_MODEL), BF16),
                   jax.ShapeDtypeStruct((1, D_MODEL), F32)], name=name)(*(dy.args if fused else [dy]), x, g, dres)


def _final(h2, g, tgt, first_row):
    fused = isinstance(h2, _Producer)
    Lp = h2.a.shape[0] if fused else h2.shape[0]
    tr = h2.tr if fused else _tile(Lp, 256, 16)
    n = len(h2.args) if fused else 1

    def body(*refs):
        g_ref, t_ref, loss_ref, dx_ref, dxb_ref, dg_ref = refs[n:]
        i = pl.program_id(0)
        xv = h2.tile(refs[:n]) if fused else refs[0][...]
        gv = g_ref[...]
        r = lax.rsqrt(jnp.mean(xv * xv, axis=-1, keepdims=True) + EPS)
        xh = xv * r
        row = i * tr + lax.broadcasted_iota(jnp.int32, (tr, 1), 0)
        err = jnp.where(row >= first_row, xh * gv - t_ref[...], 0.0)
        lpart = jnp.sum(err * err, axis=0, keepdims=True) * (0.5 / D_MODEL)
        dyv = err * (1.0 / D_MODEL)
        dxh = dyv * gv
        dx = r * (dxh - xh * jnp.mean(dxh * xh, axis=-1, keepdims=True))
        dx_ref[...] = dx
        dxb_ref[...] = dx.astype(BF16)
        part = jnp.sum(dyv * xh, axis=0, keepdims=True)

        @pl.when(i == 0)
        def _():
            dg_ref[...] = part
            loss_ref[...] = lpart

        @pl.when(i > 0)
        def _():
            dg_ref[...] += part
            loss_ref[...] += lpart

    blk = pl.BlockSpec((tr, D_MODEL), lambda i: (i, 0))
    vec = pl.BlockSpec((1, D_MODEL), lambda i: (0, 0))
    return pl.pallas_call(
        body, grid=(Lp // tr,), in_specs=(h2.specs if fused else [blk]) + [vec, blk], out_specs=[vec, blk, blk, vec],
        out_shape=[jax.ShapeDtypeStruct((1, D_MODEL), F32), jax.ShapeDtypeStruct((Lp, D_MODEL), F32),
                   jax.ShapeDtypeStruct((Lp, D_MODEL), BF16), jax.ShapeDtypeStruct((1, D_MODEL), F32)],
        name="final_norm_loss")(*(h2.args if fused else [h2]), g, tgt)


def _halo_prev(tr, width, col=0):
    return pl.BlockSpec((8, width), lambda i: (jnp.maximum(i * (tr // 8) - 1, 0), col))


def _halo_next(tr, width, nrows, col=0, rows=8):
    last = nrows // rows - 1
    return pl.BlockSpec((rows, width), lambda i: (jnp.minimum((i + 1) * (tr // rows), last), col))


def _shifted(x, offs):
    n = x.shape[0]
    return [x if off == 0 else pltpu.roll(x, n - off, 0) for off in offs]


def _taps(wins, w, rows, bias=None):
    acc = w[0:1, :] * wins[0][0:rows, :]
    if bias is not None:
        acc = acc + bias
    for kk in range(1, len(wins)):
        acc = acc + w[kk:kk + 1, :] * wins[kk][0:rows, :]
    return acc


def _gdn_pre(proj_m, proj_s, conv_w, gparams, pad):
    Lp = proj_m.shape[0]
    tr = _tile(Lp, 192, 64)
    W3 = 3 * D_MODEL

    def body(main_ref, prev_ref, s_ref, w_ref, gp_ref, qkv_ref, gsm_ref, c_ref):
        i = pl.program_id(0)
        prev = jnp.where(i > 0, prev_ref[...], 0.0)
        ext = jnp.concatenate([prev, main_ref[...]], axis=0)
        c = _taps(_shifted(ext, range(8 - (GDN_CONV - 1), 9)), w_ref[...], tr)
        c_ref[...] = c.astype(BF16)
        s = c * _sig(c)
        scale = GDN_D ** -0.5
        for j in range(2 * GDN_H):
            seg = s[:, j * GDN_D:(j + 1) * GDN_D]
            r = lax.rsqrt(_rowsum(seg * seg) + EPS)
            if j < GDN_H:
                r = r * scale
            qkv_ref[:, j * GDN_D:(j + 1) * GDN_D] = seg * r
        qkv_ref[:, 2 * D_MODEL:] = s[:, 2 * D_MODEL:]
        sm = s_ref[...]
        gp = gp_ref[...]
        lane = lax.broadcasted_iota(jnp.int32, sm.shape, 1)
        z = sm + gp[1:2, :]
        softplus = jnp.maximum(z, 0.0) + jnp.log(1.0 + jnp.exp(-jnp.abs(z)))
        lg = -jnp.exp(gp[0:1, :]) * softplus
        row = i * tr + lax.broadcasted_iota(jnp.int32, (tr, 1), 0)
        out = jnp.where(lane < GDN_H, lg, jnp.where(lane < 2 * GDN_H, _sig(sm), 0.0))
        gsm_ref[...] = jnp.where(row >= pad, out, 0.0)

    return pl.pallas_call(
        body, grid=(Lp // tr,),
        in_specs=[pl.BlockSpec((tr, W3), lambda i: (i, 0)), _halo_prev(tr, W3),
                  pl.BlockSpec((tr, LANES), lambda i: (i, 0)),
                  pl.BlockSpec((GDN_CONV, W3), lambda i: (0, 0)), pl.BlockSpec((8, LANES), lambda i: (0, 0))],
        out_specs=[pl.BlockSpec((tr, W3), lambda i: (i, 0)), pl.BlockSpec((tr, LANES), lambda i: (i, 0)),
                   pl.BlockSpec((tr, W3), lambda i: (i, 0))],
        out_shape=[jax.ShapeDtypeStruct((Lp, W3), F32), jax.ShapeDtypeStruct((Lp, LANES), F32),
                   jax.ShapeDtypeStruct((Lp, W3), BF16)],
        name="gdn_pre")(proj_m, proj_m, proj_s, conv_w, gparams)


def _gdn_pre_bwd(proj_m, conv_out, proj_s, conv_w, gparams, dq, dk, dv, dgs, pad):
    Lp = proj_m.shape[0]
    tr = _tile(Lp, 192, 64)
    W3 = 3 * D_MODEL
    te = tr + 8

    def body(main_ref, c_ref, cn_ref, s_ref, w_ref, gp_ref,
             dq_ref, dqn_ref, dk_ref, dkn_ref, dv_ref, dvn_ref, dgs_ref,
             da_ref, ds_ref, dw_ref, dgp_ref):
        i = pl.program_id(0)
        w = w_ref[...]
        c = jnp.concatenate([c_ref[...].astype(F32), cn_ref[...].astype(F32)[0:8]], axis=0)
        sg = _sig(c)
        s = c * sg
        rowe = i * tr + lax.broadcasted_iota(jnp.int32, (te, 1), 0)
        live = (rowe >= pad) & (rowe < Lp)
        dqe = jnp.concatenate([dq_ref[...], dqn_ref[...]], axis=0)
        dke = jnp.concatenate([dk_ref[...], dkn_ref[...]], axis=0)
        dve = jnp.concatenate([dv_ref[...], dvn_ref[...]], axis=0)
        scale = GDN_D ** -0.5
        parts = []
        for j in range(2 * GDN_H):
            seg = s[:, j * GDN_D:(j + 1) * GDN_D]
            r = lax.rsqrt(_rowsum(seg * seg) + EPS)
            xh = seg * r
            if j < GDN_H:
                dxh = dqe[:, j * GDN_D:(j + 1) * GDN_D] * scale
            else:
                dxh = dke[:, (j - GDN_H) * GDN_D:(j - GDN_H + 1) * GDN_D]
            parts.append(r * (dxh - xh * _rowsum(dxh * xh)))
        parts.append(dve)
        dsv = jnp.concatenate(parts, axis=1)
        dc = jnp.where(live, dsv * (sg * (1.0 + c * (1.0 - sg))), 0.0)
        dcs = _shifted(dc, range(GDN_CONV - 1, -1, -1))
        da_ref[...] = _taps(dcs, w, tr).astype(BF16)
        pm = main_ref[...]
        rows = [jnp.sum(dcs[kk][0:tr, :] * pm, axis=0, keepdims=True) for kk in range(GDN_CONV)]
        dwp = jnp.concatenate(rows + [jnp.zeros((8 - GDN_CONV, W3), F32)], axis=0)

        sm = s_ref[...]
        gp = gp_ref[...]
        lane = lax.broadcasted_iota(jnp.int32, sm.shape, 1)
        rowm = i * tr + lax.broadcasted_iota(jnp.int32, (tr, 1), 0)
        dgv = jnp.where(rowm >= pad, dgs_ref[...], 0.0)
        dlg = jnp.where(lane < GDN_H, dgv, 0.0)
        dbt = jnp.where((lane >= GDN_H) & (lane < 2 * GDN_H), dgv, 0.0)
        z = sm + gp[1:2, :]
        softplus = jnp.maximum(z, 0.0) + jnp.log(1.0 + jnp.exp(-jnp.abs(z)))
        ea = jnp.exp(gp[0:1, :])
        dz = dlg * (-ea) * _sig(z)
        dal = dlg * (-ea) * softplus
        bt = _sig(sm)
        dgb = dbt * bt * (1.0 - bt)
        ds_ref[...] = (dz + dgb).astype(BF16)
        gpp = jnp.concatenate([jnp.sum(dal, axis=0, keepdims=True), jnp.sum(dz, axis=0, keepdims=True),
                               jnp.zeros((6, LANES), F32)], axis=0)

        @pl.when(i == 0)
        def _():
            dw_ref[...] = dwp
            dgp_ref[...] = gpp

        @pl.when(i > 0)
        def _():
            dw_ref[...] += dwp
            dgp_ref[...] += gpp

    m3 = pl.BlockSpec((tr, W3), lambda i: (i, 0))
    m1 = pl.BlockSpec((tr, D_MODEL), lambda i: (i, 0))
    n1 = _halo_next(tr, D_MODEL, Lp)
    return pl.pallas_call(
        body, grid=(Lp // tr,),
        in_specs=[m3, m3, _halo_next(tr, W3, Lp, rows=16), pl.BlockSpec((tr, LANES), lambda i: (i, 0)),
                  pl.BlockSpec((GDN_CONV, W3), lambda i: (0, 0)), pl.BlockSpec((8, LANES), lambda i: (0, 0)),
                  m1, n1, m1, n1, m1, n1, pl.BlockSpec((tr, LANES), lambda i: (i, 0))],
        out_specs=[m3, pl.BlockSpec((tr, LANES), lambda i: (i, 0)),
                   pl.BlockSpec((8, W3), lambda i: (0, 0)), pl.BlockSpec((8, LANES), lambda i: (0, 0))],
        out_shape=[jax.ShapeDtypeStruct((Lp, W3), BF16), jax.ShapeDtypeStruct((Lp, LANES), BF16),
                   jax.ShapeDtypeStruct((8, W3), F32), jax.ShapeDtypeStruct((8, LANES), F32)],
        name="gdn_pre_bwd")(proj_m, conv_out, conv_out, proj_s, conv_w, gparams, dq, dq, dk, dk, dv, dv, dgs)


def _gdn_gates(gs):
    ri = lax.broadcasted_iota(jnp.int32, (CHUNK, CHUNK), 0)
    ci = lax.broadcasted_iota(jnp.int32, (CHUNK, CHUNK), 1)
    tril = ri >= ci
    strict = ri > ci
    gall = _dx(tril.astype(F32), gs)
    lane8 = lax.broadcasted_iota(jnp.int32, (8, LANES), 1)
    sub8 = lax.broadcasted_iota(jnp.int32, (8, LANES), 0)
    grow = _dxnt((lane8 == sub8).astype(F32), gall)
    return gall, grow, tril, strict


def _gdn_decay(gall, grow, tril, h):
    g = gall[:, h:h + 1]
    return g, jnp.where(tril, jnp.exp(jnp.where(tril, g - grow[h:h + 1, :], 0.0)), 0.0)


def _group(N):
    return 3 if N % 3 == 0 else (2 if N % 2 == 0 else 1)


def _gdn_chunk_specs(N, rev):
    G = _group(N)
    nb = N // G
    cn = (lambda n: nb - 1 - n) if rev else (lambda n: n)
    col = lambda j: pl.BlockSpec((G * CHUNK, D_MODEL), lambda n: (cn(n), j))
    gate = pl.BlockSpec((G * CHUNK, LANES), lambda n: (cn(n), 0))
    st = lambda a, b: pl.BlockSpec((GDN_H, G, a, b), lambda n: (0, cn(n), 0, 0))
    return G, nb, col, gate, st


def _gdn_chunk_fwd(qkv, gsm):
    Lp = qkv.shape[0]
    N = Lp // CHUNK
    G, nb, col, gate, st = _gdn_chunk_specs(N, False)

    def body(q_ref, k_ref, v_ref, gs_ref, o_ref, sin_ref, t_ref, S):
        n = pl.program_id(0)

        @pl.when(n == 0)
        def _():
            S[...] = jnp.zeros_like(S)

        ri = lax.broadcasted_iota(jnp.int32, (CHUNK, CHUNK), 0)
        ci = lax.broadcasted_iota(jnp.int32, (CHUNK, CHUNK), 1)
        eye = (ri == ci).astype(F32)
        heads = range(GDN_H)
        sls = [slice(h * GDN_D, (h + 1) * GDN_D) for h in heads]
        rows = [slice(c * CHUNK, (c + 1) * CHUNK) for c in range(G)]
        pairs = [(c, h) for c in range(G) for h in heads]
        P = lambda f: {p: f(*p) for p in pairs}
        gs = [gs_ref[rows[c], :] for c in range(G)]
        gates = [_gdn_gates(gs[c]) for c in range(G)]
        tril, strict = gates[0][2], gates[0][3]
        q = P(lambda c, h: q_ref[rows[c], sls[h]])
        k = P(lambda c, h: k_ref[rows[c], sls[h]])
        v = P(lambda c, h: v_ref[rows[c], sls[h]])
        beta = P(lambda c, h: gs[c][:, GDN_H + h:GDN_H + h + 1])
        gg = P(lambda c, h: _gdn_decay(gates[c][0], gates[c][1], tril, h))
        g = {p: x[0] for p, x in gg.items()}
        gam = {p: x[1] for p, x in gg.items()}
        eg = P(lambda c, h: jnp.exp(g[c, h]))
        gl = P(lambda c, h: g[c, h][CHUNK - 1:CHUNK, :])
        kb = P(lambda c, h: k[c, h] * beta[c, h])
        pw = P(lambda c, h: -jnp.where(strict, _dnt(kb[c, h], k[c, h]) * gam[c, h], 0.0))
        p = P(lambda c, h: _dnt(q[c, h], k[c, h]) * gam[c, h])
        t = P(lambda c, h: eye + pw[c, h])
        for _ in range(5):
            pw = P(lambda c, h: _d3g(pw[c, h], pw[c, h], _NN))
            t = P(lambda c, h: t[c, h] + _d3g(t[c, h], pw[c, h], _NN))
        u = P(lambda c, h: _d(t[c, h], v[c, h] * beta[c, h]))
        w = P(lambda c, h: _d(t[c, h], kb[c, h] * eg[c, h]))
        qg = P(lambda c, h: q[c, h] * eg[c, h])
        kd = P(lambda c, h: k[c, h] * jnp.exp(gl[c, h] - g[c, h]))
        egl = P(lambda c, h: jnp.exp(gl[c, h]))
        for c in range(G):
            for h in heads:
                t_ref[h, c] = t[c, h]
        cur = [S[h] for h in heads]
        for c in range(G):
            vnew = [u[c, h] - _d(w[c, h], cur[h]) for h in heads]
            for h in heads:
                o_ref[rows[c], sls[h]] = _d(qg[c, h], cur[h]) + _d(p[c, h], vnew[h])
                sin_ref[h, c] = cur[h]
            cur = [cur[h] * egl[c, h] + _dtn(kd[c, h], vnew[h]) for h in heads]
        for h in heads:
            S[h] = cur[h]

    return pl.pallas_call(
        body, grid=(nb,),
        in_specs=[col(0), col(1), col(2), gate],
        out_specs=[col(0), st(GDN_D, GDN_D), st(CHUNK, CHUNK)],
        out_shape=[jax.ShapeDtypeStruct((Lp, D_MODEL), F32), jax.ShapeDtypeStruct((GDN_H, N, GDN_D, GDN_D), F32),
                   jax.ShapeDtypeStruct((GDN_H, N, CHUNK, CHUNK), F32)],
        scratch_shapes=[pltpu.VMEM((GDN_H, GDN_D, GDN_D), F32)],
        name="gdn_chunk_fwd")(qkv, qkv, qkv, gsm)


def _gdn_chunk_bwd(qkv, gsm, do, s_in, t_in):
    Lp = qkv.shape[0]
    N = Lp // CHUNK
    G, nb, col, gate, st = _gdn_chunk_specs(N, True)

    def body(q_ref, k_ref, v_ref, gs_ref, do_ref, sin_ref, t_ref, dq_ref, dk_ref, dv_ref, dgs_ref, dS):
        n = pl.program_id(0)

        @pl.when(n == 0)
        def _():
            dS[...] = jnp.zeros_like(dS)

        lane = lax.broadcasted_iota(jnp.int32, (CHUNK, LANES), 1)
        rcol = lax.broadcasted_iota(jnp.int32, (CHUNK, 1), 0)
        ri = lax.broadcasted_iota(jnp.int32, (CHUNK, CHUNK), 0)
        ci = lax.broadcasted_iota(jnp.int32, (CHUNK, CHUNK), 1)
        ones = jnp.ones((CHUNK, LANES), F32)
        heads = range(GDN_H)
        sls = [slice(h * GDN_D, (h + 1) * GDN_D) for h in heads]
        rows = [slice(c * CHUNK, (c + 1) * CHUNK) for c in range(G)]
        pairs = [(c, h) for c in range(G) for h in heads]
        P = lambda f: {p: f(*p) for p in pairs}
        gs = [gs_ref[rows[c], :] for c in range(G)]
        gates = [_gdn_gates(gs[c]) for c in range(G)]
        tril, strict = gates[0][2], gates[0][3]
        q = P(lambda c, h: q_ref[rows[c], sls[h]])
        k = P(lambda c, h: k_ref[rows[c], sls[h]])
        v = P(lambda c, h: v_ref[rows[c], sls[h]])
        dov = P(lambda c, h: do_ref[rows[c], sls[h]])
        s0 = P(lambda c, h: sin_ref[h, c])
        t = P(lambda c, h: t_ref[h, c])
        beta = P(lambda c, h: gs[c][:, GDN_H + h:GDN_H + h + 1])
        gg = P(lambda c, h: _gdn_decay(gates[c][0], gates[c][1], tril, h))
        g = {p: x[0] for p, x in gg.items()}
        gam = {p: x[1] for p, x in gg.items()}
        eg = P(lambda c, h: jnp.exp(g[c, h]))
        egl = P(lambda c, h: jnp.exp(g[c, h][CHUNK - 1:CHUNK, :]))
        e = P(lambda c, h: jnp.exp(g[c, h][CHUNK - 1:CHUNK, :] - g[c, h]))
        kb = P(lambda c, h: k[c, h] * beta[c, h])
        kbg = P(lambda c, h: kb[c, h] * eg[c, h])
        vb = P(lambda c, h: v[c, h] * beta[c, h])
        qg = P(lambda c, h: q[c, h] * eg[c, h])
        kd = P(lambda c, h: k[c, h] * e[c, h])
        m = P(lambda c, h: jnp.where(strict, _dnt(kb[c, h], k[c, h]) * gam[c, h], 0.0))
        u = P(lambda c, h: _d(t[c, h], vb[c, h]))
        w = P(lambda c, h: _d(t[c, h], kbg[c, h]))
        p = P(lambda c, h: _dnt(q[c, h], k[c, h]) * gam[c, h])
        dqg = P(lambda c, h: _dnt(dov[c, h], s0[c, h]))
        qgdo = P(lambda c, h: _dtn(qg[c, h], dov[c, h]))
        ptdo = P(lambda c, h: _dtn(p[c, h], dov[c, h]))
        vnew = P(lambda c, h: u[c, h] - _d(w[c, h], s0[c, h]))
        dp = P(lambda c, h: jnp.where(tril, _dnt(dov[c, h], vnew[c, h]), 0.0))
        cur = [dS[h] for h in heads]
        dvnew, dkd, sds = {}, {}, {}
        for c in reversed(range(G)):
            for h in heads:
                dvnew[c, h] = ptdo[c, h] + _d(kd[c, h], cur[h])
                dkd[c, h] = _dnt(vnew[c, h], cur[h])
                sds[c, h] = _allsum(s0[c, h] * cur[h])
            cur = [qgdo[c, h] + egl[c, h] * cur[h] - _dtn(w[c, h], dvnew[c, h]) for h in heads]
        for h in heads:
            dS[h] = cur[h]
        dw = P(lambda c, h: -_dnt(dvnew[c, h], s0[c, h]))
        dvb = P(lambda c, h: _dtn(t[c, h], dvnew[c, h]))
        dkbg = P(lambda c, h: _dtn(t[c, h], dw[c, h]))
        dt = P(lambda c, h: _dnt(dvnew[c, h], vb[c, h]) + _dnt(dw[c, h], kbg[c, h]))
        x1 = P(lambda c, h: _d3g(t[c, h], dt[c, h], _TN))
        dm = P(lambda c, h: jnp.where(strict, -_d3g(x1[c, h], t[c, h], _NT), 0.0))
        dkk = P(lambda c, h: dm[c, h] * gam[c, h])
        dqk = P(lambda c, h: dp[c, h] * gam[c, h])
        dkb = P(lambda c, h: _d(dkk[c, h], k[c, h]) + eg[c, h] * dkbg[c, h])
        em = P(lambda c, h: dm[c, h] * m[c, h] + dp[c, h] * p[c, h])
        colsum = P(lambda c, h: _d3g(em[c, h], ones, _TN)[:, 0:1])
        for c, h in pairs:
            dk_ref[rows[c], sls[h]] = (_dtn(dkk[c, h], kb[c, h]) + _dtn(dqk[c, h], q[c, h]) + dkd[c, h] * e[c, h]
                                       + beta[c, h] * dkb[c, h])
            dq_ref[rows[c], sls[h]] = _d(dqk[c, h], k[c, h]) + dqg[c, h] * eg[c, h]
            dv_ref[rows[c], sls[h]] = beta[c, h] * dvb[c, h]
        for c in range(G):
            dg_all = jnp.zeros((CHUNK, LANES), F32)
            dbeta_all = jnp.zeros((CHUNK, LANES), F32)
            for h in heads:
                dbeta = _rowsum(k[c, h] * dkb[c, h]) + _rowsum(v[c, h] * dvb[c, h])
                z = _rowsum(kd[c, h] * dkd[c, h])
                dg = (_rowsum(em[c, h]) - colsum[c, h] + _rowsum(qg[c, h] * dqg[c, h]) + _rowsum(kbg[c, h] * dkbg[c, h])
                      - z)
                extra = _allsum(z) + egl[c, h] * sds[c, h]
                dg = dg + jnp.where(rcol == CHUNK - 1, extra, 0.0)
                dg_all = dg_all + jnp.where(lane == h, dg, 0.0)
                dbeta_all = dbeta_all + jnp.where(lane == GDN_H + h, dbeta, 0.0)
            dgs_ref[rows[c], :] = _dx((ci >= ri).astype(F32), dg_all) + dbeta_all

    return pl.pallas_call(
        body, grid=(nb,),
        in_specs=[col(0), col(1), col(2), gate, col(0), st(GDN_D, GDN_D), st(CHUNK, CHUNK)],
        out_specs=[col(0), col(0), col(0), gate],
        out_shape=[jax.ShapeDtypeStruct((Lp, D_MODEL), F32)] * 3 + [jax.ShapeDtypeStruct((Lp, LANES), F32)],
        scratch_shapes=[pltpu.VMEM((GDN_H, GDN_D, GDN_D), F32)],
        name="gdn_chunk_bwd")(qkv, qkv, qkv, gsm, do, s_in, t_in)


def _rot(x, c, s):
    half = RET_D // 2
    x1 = x[:, :half]
    x2 = x[:, half:]
    return jnp.concatenate([x1 * c - x2 * s, x2 * c + x1 * s], axis=1)


def _rot_bwd(d, c, s):
    half = RET_D // 2
    d1 = d[:, :half]
    d2 = d[:, half:]
    return jnp.concatenate([d1 * c + d2 * s, d2 * c - d1 * s], axis=1)


def _ret_tables():
    hh = jnp.arange(RET_H, dtype=F32)
    lg = jnp.log(1.0 - 2.0 ** (-5.0 - hh))
    idx = jnp.arange(CHUNK, dtype=F32)
    tril = jnp.asarray(np.tril(np.ones((CHUNK, CHUNK), dtype=bool)))
    dmask = jnp.where(tril, jnp.exp((idx[:, None] - idx[None, :]) * lg[:, None, None]), 0.0)
    qdec = jnp.exp((idx[None, :] + 1.0) * lg[:, None])
    kdec = jnp.exp((CHUNK - 1.0 - idx[None, :]) * lg[:, None])
    gch = jnp.exp(CHUNK * lg)
    qdec = jnp.broadcast_to(qdec[:, :, None], (RET_H, CHUNK, RET_D))
    kdec = jnp.broadcast_to(kdec[:, :, None], (RET_H, CHUNK, RET_D))
    gch = jnp.broadcast_to(gch[:, None, None], (RET_H, 8, LANES))
    return dmask, qdec, kdec, gch


def _ret_specs(N, rev):
    G = _group(N)
    nb = N // G
    cn = (lambda n: nb - 1 - n) if rev else (lambda n: n)
    col = lambda j: pl.BlockSpec((G * CHUNK, D_MODEL), lambda n: (cn(n), j))
    tab = lambda a, b: pl.BlockSpec((RET_H, a, b), lambda n: (0, 0, 0))
    rope = pl.BlockSpec((G * CHUNK, LANES), lambda n: (cn(n), 0))
    st = pl.BlockSpec((RET_H, G, RET_D, RET_D), lambda n: (0, cn(n), 0, 0))
    return G, nb, col, tab, rope, st


def _ret_chunk_fwd(proj_m, cos, sin, tables):
    Lp = proj_m.shape[0]
    N = Lp // CHUNK
    dmask, qdec, kdec, gch = tables
    G, nb, col, tab, rope, st = _ret_specs(N, False)

    def body(q_ref, k_ref, v_ref, c_ref, s_ref, dm_ref, qd_ref, kd_ref, g_ref, o_ref, sin_ref, S):
        n = pl.program_id(0)

        @pl.when(n == 0)
        def _():
            S[...] = jnp.zeros_like(S)

        heads = range(RET_H)
        sls = [slice(h * RET_D, (h + 1) * RET_D) for h in heads]
        rows = [slice(c * CHUNK, (c + 1) * CHUNK) for c in range(G)]
        pairs = [(c, h) for c in range(G) for h in heads]
        P = lambda f: {p: f(*p) for p in pairs}
        qr = P(lambda c, h: _rot(q_ref[rows[c], sls[h]], c_ref[rows[c], :], s_ref[rows[c], :]))
        ks = P(lambda c, h: _rot(k_ref[rows[c], sls[h]], c_ref[rows[c], :], s_ref[rows[c], :]) * (RET_D ** -0.5))
        v = P(lambda c, h: v_ref[rows[c], sls[h]])
        a = P(lambda c, h: _dnt(qr[c, h], ks[c, h]) * dm_ref[h])
        av = P(lambda c, h: _d(a[c, h], v[c, h]))
        kv = P(lambda c, h: _dtn(ks[c, h] * kd_ref[h], v[c, h]))
        qd = P(lambda c, h: qr[c, h] * qd_ref[h])
        cur = [S[h] for h in heads]
        for c in range(G):
            for h in heads:
                o_ref[rows[c], sls[h]] = av[c, h] + _d(qd[c, h], cur[h])
                sin_ref[h, c] = cur[h].astype(BF16)
            cur = [cur[h] * g_ref[h, 0:1, 0:1] + kv[c, h] for h in heads]
        for h in heads:
            S[h] = cur[h]

    return pl.pallas_call(
        body, grid=(nb,),
        in_specs=[col(3), col(4), col(5), rope, rope,
                  tab(CHUNK, CHUNK), tab(CHUNK, RET_D), tab(CHUNK, RET_D), tab(8, LANES)],
        out_specs=[col(0), st],
        out_shape=[jax.ShapeDtypeStruct((Lp, D_MODEL), F32), jax.ShapeDtypeStruct((RET_H, N, RET_D, RET_D), BF16)],
        scratch_shapes=[pltpu.VMEM((RET_H, RET_D, RET_D), F32)],
        name="ret_chunk_fwd")(proj_m, proj_m, proj_m, cos, sin, dmask, qdec, kdec, gch)


def _ret_chunk_bwd(proj_m, cos, sin, tables, do, s_in):
    Lp = proj_m.shape[0]
    N = Lp // CHUNK
    dmask, qdec, kdec, gch = tables
    G, nb, col, tab, rope, st = _ret_specs(N, True)

    def body(q_ref, k_ref, v_ref, c_ref, s_ref, dm_ref, qd_ref, kd_ref, g_ref, do_ref, sin_ref,
             dq_ref, dk_ref, dv_ref, dS):
        n = pl.program_id(0)

        @pl.when(n == 0)
        def _():
            dS[...] = jnp.zeros_like(dS)

        kscale = RET_D ** -0.5
        heads = range(RET_H)
        sls = [slice(h * RET_D, (h + 1) * RET_D) for h in heads]
        rows = [slice(c * CHUNK, (c + 1) * CHUNK) for c in range(G)]
        pairs = [(c, h) for c in range(G) for h in heads]
        P = lambda f: {p: f(*p) for p in pairs}
        cs = [(c_ref[rows[c], :], s_ref[rows[c], :]) for c in range(G)]
        qr = P(lambda c, h: _rot(q_ref[rows[c], sls[h]], *cs[c]))
        ks = P(lambda c, h: _rot(k_ref[rows[c], sls[h]], *cs[c]) * kscale)
        v = P(lambda c, h: v_ref[rows[c], sls[h]])
        dov = P(lambda c, h: do_ref[rows[c], sls[h]])
        ad = P(lambda c, h: _dnt(qr[c, h], ks[c, h]) * dm_ref[h])
        da = P(lambda c, h: _dnt(dov[c, h], v[c, h]) * dm_ref[h])
        dos = P(lambda c, h: _dnt(dov[c, h], sin_ref[h, c]) * qd_ref[h])
        qdo = P(lambda c, h: _dtn(qr[c, h] * qd_ref[h], dov[c, h]))
        adv = P(lambda c, h: _dtn(ad[c, h], dov[c, h]))
        dqr = P(lambda c, h: _d(da[c, h], ks[c, h]) + dos[c, h])
        daq = P(lambda c, h: _dtn(da[c, h], qr[c, h]))
        kk = P(lambda c, h: ks[c, h] * kd_ref[h])
        cur = [dS[h] for h in heads]
        for c in reversed(range(G)):
            for h in heads:
                dv_ref[rows[c], sls[h]] = (adv[c, h] + _d(kk[c, h], cur[h])).astype(BF16)
                dq_ref[rows[c], sls[h]] = _rot_bwd(dqr[c, h], *cs[c]).astype(BF16)
                dks = daq[c, h] + _dnt(v[c, h], cur[h]) * kd_ref[h]
                dk_ref[rows[c], sls[h]] = _rot_bwd(dks * kscale, *cs[c]).astype(BF16)
            cur = [cur[h] * g_ref[h, 0:1, 0:1] + qdo[c, h] for h in heads]
        for h in heads:
            dS[h] = cur[h]

    return pl.pallas_call(
        body, grid=(nb,),
        in_specs=[col(3), col(4), col(5), rope, rope,
                  tab(CHUNK, CHUNK), tab(CHUNK, RET_D), tab(CHUNK, RET_D), tab(8, LANES), col(0), st],
        out_specs=[col(0), col(0), col(0)],
        out_shape=[jax.ShapeDtypeStruct((Lp, D_MODEL), BF16)] * 3,
        scratch_shapes=[pltpu.VMEM((RET_H, RET_D, RET_D), F32)],
        name="ret_chunk_bwd")(proj_m, proj_m, proj_m, cos, sin, dmask, qdec, kdec, gch, do, s_in)


def _merge_specs(tr):
    col = lambda j: pl.BlockSpec((tr, D_MODEL), lambda i: (i, j))
    return col


def _merge_fwd(o_a, o_b, proj_m, gnorm):
    Lp = o_a.shape[0]
    tr = _tile(Lp, 192, 16)

    def body(oa_ref, ob_ref, gz_ref, rg_ref, ga_ref, gb_ref, gn_ref, y_ref):
        gn = gn_ref[...]
        oa = oa_ref[...]
        ob = ob_ref[...]
        gz = gz_ref[...]
        ya = []
        for j in range(GDN_H):
            seg = oa[:, j * GDN_D:(j + 1) * GDN_D]
            r = lax.rsqrt(jnp.mean(seg * seg, axis=-1, keepdims=True) + EPS)
            ya.append(seg * r * gn)
        ya = jnp.concatenate(ya, axis=1) * (gz * _sig(gz))
        yb = []
        for j in range(RET_H):
            seg = ob[:, j * RET_D:(j + 1) * RET_D]
            r = lax.rsqrt(jnp.mean(seg * seg, axis=-1, keepdims=True) + EPS)
            yb.append(seg * r)
        rg = rg_ref[...]
        yb = jnp.concatenate(yb, axis=1) * (rg * _sig(rg))
        y_ref[...] = (_sig(ga_ref[...]) * ya + _sig(gb_ref[...]) * yb).astype(BF16)

    col = _merge_specs(tr)
    return pl.pallas_call(
        body, grid=(Lp // tr,),
        in_specs=[col(0), col(0), col(6), col(7), col(8), col(9), pl.BlockSpec((1, GDN_D), lambda i: (0, 0))],
        out_specs=col(0), out_shape=jax.ShapeDtypeStruct((Lp, D_MODEL), BF16),
        name="merge_fwd")(o_a, o_b, proj_m, proj_m, proj_m, proj_m, gnorm)


def _merge_bwd(dh1b, w_out, o_a, o_b, proj_m, gnorm):
    Lp = o_a.shape[0]
    tr = _tile(Lp, 192, 16)

    def body(d_ref, wo_ref, oa_ref, ob_ref, gz_ref, rg_ref, ga_ref, gb_ref, gn_ref, dc_ref, doa_ref, dob_ref, dgn_ref):
        i = pl.program_id(0)
        gn = gn_ref[...]
        dyv = lax.dot_general(d_ref[...], wo_ref[...], _NT, preferred_element_type=F32)
        oa = oa_ref[...]
        ob = ob_ref[...]
        gz = gz_ref[...]
        rg = rg_ref[...]
        sa = _sig(ga_ref[...])
        sb = _sig(gb_ref[...])
        dya = dyv * sa
        dyb = dyv * sb
        sgz = _sig(gz)
        szz = gz * sgz
        dgn = jnp.zeros((1, GDN_D), F32)
        ya = []
        dgz = []
        for j in range(GDN_H):
            sl = slice(j * GDN_D, (j + 1) * GDN_D)
            seg = oa[:, sl]
            r = lax.rsqrt(jnp.mean(seg * seg, axis=-1, keepdims=True) + EPS)
            xh = seg * r
            oan = xh * gn
            ya.append(oan * szz[:, sl])
            dgz.append(dya[:, sl] * oan * (sgz[:, sl] * (1.0 + gz[:, sl] * (1.0 - sgz[:, sl]))))
            doan = dya[:, sl] * szz[:, sl]
            dgn = dgn + jnp.sum(doan * xh, axis=0, keepdims=True)
            dxh = doan * gn
            doa_ref[:, sl] = r * (dxh - xh * jnp.mean(dxh * xh, axis=-1, keepdims=True))
        ya = jnp.concatenate(ya, axis=1)
        srg = _sig(rg)
        srr = rg * srg
        yb = []
        drg = []
        for j in range(RET_H):
            sl = slice(j * RET_D, (j + 1) * RET_D)
            seg = ob[:, sl]
            r = lax.rsqrt(jnp.mean(seg * seg, axis=-1, keepdims=True) + EPS)
            xh = seg * r
            yb.append(xh * srr[:, sl])
            drg.append(dyb[:, sl] * xh * (srg[:, sl] * (1.0 + rg[:, sl] * (1.0 - srg[:, sl]))))
            dxh = dyb[:, sl] * srr[:, sl]
            dob_ref[:, sl] = r * (dxh - xh * jnp.mean(dxh * xh, axis=-1, keepdims=True))
        yb = jnp.concatenate(yb, axis=1)
        dc_ref[:, 0:D_MODEL] = jnp.concatenate(dgz, axis=1).astype(BF16)
        dc_ref[:, D_MODEL:2 * D_MODEL] = jnp.concatenate(drg, axis=1).astype(BF16)
        dc_ref[:, 2 * D_MODEL:3 * D_MODEL] = (dyv * ya * sa * (1.0 - sa)).astype(BF16)
        dc_ref[:, 3 * D_MODEL:] = (dyv * yb * sb * (1.0 - sb)).astype(BF16)

        @pl.when(i == 0)
        def _():
            dgn_ref[...] = dgn

        @pl.when(i > 0)
        def _():
            dgn_ref[...] += dgn

    col = _merge_specs(tr)
    return pl.pallas_call(
        body, grid=(Lp // tr,),
        in_specs=[col(0), pl.BlockSpec((D_MODEL, D_MODEL), lambda i: (0, 0), pipeline_mode=pl.Buffered(1)),
                  col(0), col(0), col(6), col(7), col(8), col(9), pl.BlockSpec((1, GDN_D), lambda i: (0, 0))],
        out_specs=[pl.BlockSpec((tr, 4 * D_MODEL), lambda i: (i, 0)), col(0), col(0),
                   pl.BlockSpec((1, GDN_D), lambda i: (0, 0))],
        out_shape=[jax.ShapeDtypeStruct((Lp, 4 * D_MODEL), BF16), jax.ShapeDtypeStruct((Lp, D_MODEL), F32),
                   jax.ShapeDtypeStruct((Lp, D_MODEL), F32), jax.ShapeDtypeStruct((1, GDN_D), F32)],
        name="merge_bwd")(dh1b, w_out, o_a, o_b, proj_m, proj_m, proj_m, proj_m, gnorm)


def _ffn_act(up, conv_w, conv_b):
    Lp = up.shape[0]
    tr = _tile(Lp, 192, 16)
    W2 = 2 * D_FF

    def body(main_ref, prev_ref, w_ref, b_ref, act_ref, u_ref):
        i = pl.program_id(0)
        prev = jnp.where(i > 0, prev_ref[...], 0.0)
        ext = jnp.concatenate([prev, main_ref[...]], axis=0)
        u = _taps(_shifted(ext, range(8 - (FFN_CONV - 1), 9)), w_ref[...], tr, b_ref[...])
        a = u[:, :D_FF]
        act_ref[...] = (a * _sig(a) * u[:, D_FF:]).astype(BF16)
        u_ref[...] = u.astype(BF16)

    return pl.pallas_call(
        body, grid=(Lp // tr,),
        in_specs=[pl.BlockSpec((tr, W2), lambda i: (i, 0)), _halo_prev(tr, W2),
                  pl.BlockSpec((FFN_CONV, W2), lambda i: (0, 0)), pl.BlockSpec((1, W2), lambda i: (0, 0))],
        out_specs=[pl.BlockSpec((tr, D_FF), lambda i: (i, 0)), pl.BlockSpec((tr, W2), lambda i: (i, 0))],
        out_shape=[jax.ShapeDtypeStruct((Lp, D_FF), BF16), jax.ShapeDtypeStruct((Lp, W2), BF16)],
        name="ffn_act")(up, up, conv_w, conv_b)


def _ffn_act_bwd(up, u, dact, conv_w):
    Lp = up.shape[0]
    tr = _tile(Lp, 192, 16)
    W2 = 2 * D_FF
    te = tr + 8

    def body(up_ref, u_ref, un_ref, da_ref, dan_ref, w_ref, dup_ref, acc_ref):
        i = pl.program_id(0)
        w = w_ref[...]
        ue = jnp.concatenate([u_ref[...].astype(F32), un_ref[...].astype(F32)[0:8]], axis=0)
        a = ue[:, :D_FF]
        b = ue[:, D_FF:]
        rowe = i * tr + lax.broadcasted_iota(jnp.int32, (te, 1), 0)
        dae = jnp.where(rowe < Lp, jnp.concatenate([da_ref[...], dan_ref[...]], axis=0), 0.0)
        sg = _sig(a)
        du = jnp.concatenate([dae * b * (sg * (1.0 + a * (1.0 - sg))), dae * (a * sg)], axis=1)
        dus = _shifted(du, range(FFN_CONV - 1, -1, -1))
        dup_ref[...] = _taps(dus, w, tr).astype(BF16)
        upm = up_ref[...]
        rows = [jnp.sum(dus[kk][0:tr, :] * upm, axis=0, keepdims=True) for kk in range(FFN_CONV)]
        rows.append(jnp.sum(du[0:tr, :], axis=0, keepdims=True))
        part = jnp.concatenate(rows + [jnp.zeros((8 - len(rows), W2), F32)], axis=0)

        @pl.when(i == 0)
        def _():
            acc_ref[...] = part

        @pl.when(i > 0)
        def _():
            acc_ref[...] += part

    return pl.pallas_call(
        body, grid=(Lp // tr,),
        in_specs=[pl.BlockSpec((tr, W2), lambda i: (i, 0)), pl.BlockSpec((tr, W2), lambda i: (i, 0)),
                  _halo_next(tr, W2, Lp, rows=16), pl.BlockSpec((tr, D_FF), lambda i: (i, 0)), _halo_next(tr, D_FF, Lp),
                  pl.BlockSpec((FFN_CONV, W2), lambda i: (0, 0))],
        out_specs=[pl.BlockSpec((tr, W2), lambda i: (i, 0)), pl.BlockSpec((8, W2), lambda i: (0, 0))],
        out_shape=[jax.ShapeDtypeStruct((Lp, W2), BF16), jax.ShapeDtypeStruct((8, W2), F32)],
        name="ffn_act_bwd")(up, u, u, dact, dact, conv_w)


def _local_step(hpad, tgt, pad, wt, first_weights=None, late_weights=None, on_ffn_out_grads=None,
                on_w_in_grads=None):
    Lp = hpad.shape[0]
    first = pad + N_META
    pos = jnp.arange(Lp, dtype=F32) - float(pad)
    half = RET_D // 2
    inv = 1.0 / (ROPE_BASE ** (jnp.arange(half, dtype=F32) / half))
    ang = pos[:, None] * inv[None, :]
    cos, sin = jnp.cos(ang), jnp.sin(ang)
    tables = _ret_tables()
    gparams = jnp.zeros((8, LANES), F32).at[0, :GDN_H].set(wt["a_log"]).at[1, :GDN_H].set(wt["dt_bias"])

    hn1 = _rms_fwd(hpad, wt["norm1"], "rms1_fwd")
    if first_weights is not None:
        wt = {**wt, **first_weights(hn1)}
    proj_m = _mm_nn(hn1, wt["w_main_t"], bt=True, name="proj_main")
    proj_s = _mm_nn(hn1, wt["w_small_t"], bt=True, name="proj_small")
    qkv, gsm, conv_out = _gdn_pre(proj_m, proj_s, wt["gdn_conv_w"], gparams, pad)
    o_a, s_a, t_a = _gdn_chunk_fwd(qkv, gsm)
    o_b, s_b = _ret_chunk_fwd(proj_m, cos, sin, tables)
    y = _merge_fwd(o_a, o_b, proj_m, wt["gdn_norm"])
    if late_weights is not None:
        wt = {**wt, **late_weights(y)}
    h1, hn2 = _mm_rms_fwd(_Producer(y, wt["w_out"], hpad), wt["norm2"], "out_proj_rms2")
    up = _mm_nn(hn2, wt["w_up_t"], bt=True, name="ffn_up")
    act, u_ffn = _ffn_act(up, wt["ffn_conv_w"], wt["ffn_conv_b"])
    lossvec, dh2, dh2b, d_norm_f = _final(_Producer(act, wt["w_down"], h1), wt["norm_f"], tgt, first)

    d_w_down = _mm_tn(act, dh2b, name="dw_down")
    dact = _mm_nt(dh2b, wt["w_down"], name="d_act")
    dup, ffn_rows = _ffn_act_bwd(up, u_ffn, dact, wt["ffn_conv_w"])
    d_w_up_t = _mm_tn(dup, hn2, name="dw_up")
    dh1, dh1b, d_norm2 = _rms_bwd(h1, wt["norm2"], _Producer(dup, wt["w_up_t"]), dh2, pad, "d_hn2_rms2_bwd")

    d_w_out = _mm_tn(y, dh1b, name="dw_out")
    gnorm = wt["gdn_norm"]
    if on_ffn_out_grads is not None:
        gnorm = gnorm + on_ffn_out_grads(d_w_down, d_w_up_t, d_w_out)[0:1, :]
    d_c, do_a, do_b, d_gnorm = _merge_bwd(dh1b, wt["w_out"], o_a, o_b, proj_m, gnorm)
    drq, drk, drv = _ret_chunk_bwd(proj_m, cos, sin, tables, do_b, s_b)
    dq, dk, dv, dgs = _gdn_chunk_bwd(qkv, gsm, do_a, s_a, t_a)
    d_a, d_s, conv_rows, gp_rows = _gdn_pre_bwd(proj_m, conv_out, proj_s, wt["gdn_conv_w"], gparams, dq, dk, dv, dgs,
                                                pad)

    wmt = wt["w_main_t"]
    segs = [(d_a, 0, 3 * D_MODEL), (drq, 3 * D_MODEL, D_MODEL), (drk, 4 * D_MODEL, D_MODEL),
            (drv, 5 * D_MODEL, D_MODEL), (d_c, 6 * D_MODEL, 4 * D_MODEL)]
    pa, prq, prk, prv, pc = [_mm_tn(d, hn1, BF16, name="dw_in_%d" % i) for i, (d, _, _) in enumerate(segs)]
    ps = _mm_tn(d_s, hn1, BF16, name="dw_in_small")
    d_w_in_t = jnp.concatenate([pa, pc[:D_MODEL], ps[:2 * GDN_H], prq, prk, prv, pc[D_MODEL:]], axis=0)
    w_small_t = wt["w_small_t"]
    if on_w_in_grads is not None:
        w_small_t = w_small_t + on_w_in_grads(d_w_in_t)[0:1, 0:1].astype(w_small_t.dtype)
    dhn1 = _mm_nn(d_s, w_small_t, name="d_hn1_small")
    for i, (d, off, width) in enumerate(segs[:-1]):
        dhn1 = _mm_nn(d, wmt[off:off + width], res=dhn1, name="d_hn1_%d" % i)
    d, off, width = segs[-1]
    dh0, _, d_norm1 = _rms_bwd(hpad, wt["norm1"], _Producer(d, wmt[off:off + width], dhn1), dh1, pad,
                               "d_hn1_rms1_bwd")

    grads = {
        "norm1": d_norm1, "w_in_t": d_w_in_t, "gdn_conv_w": conv_rows[:GDN_CONV],
        "a_log": gp_rows[0, :GDN_H], "dt_bias": gp_rows[1, :GDN_H], "gdn_norm": d_gnorm, "w_out": d_w_out,
        "norm2": d_norm2, "w_up_t": d_w_up_t, "ffn_conv_w": ffn_rows[:FFN_CONV],
        "ffn_conv_b": ffn_rows[FFN_CONV:FFN_CONV + 1], "w_down": d_w_down, "norm_f": d_norm_f,
    }
    return lossvec, dh0, grads


def _peer(k):
    ix, iy, ic = lax.axis_index("x"), lax.axis_index("y"), lax.axis_index("c")
    px = 1 - ix if (k >> 2) & 1 else ix
    py = 1 - iy if (k >> 1) & 1 else iy
    pc = 1 - ic if k & 1 else ic
    return (px, py, pc), 4 * px + 2 * py + pc


def _comm_call(body, n, out_shapes, name, args):
    hbm = pl.BlockSpec(memory_space=pl.ANY)
    return pl.pallas_call(
        body, out_shape=out_shapes, in_specs=[hbm] * n, out_specs=[hbm] * n,
        scratch_shapes=[pltpu.SemaphoreType.DMA((n, N_DEV - 1)), pltpu.SemaphoreType.DMA((n, N_DEV - 1)),
                        pltpu.SemaphoreType.DMA((n,))],
        name=name)(*args)


def _all_gather(xs, name):
    n = len(xs)

    def body(*refs):
        x_refs, out_refs = refs[:n], refs[n:2 * n]
        send_sems, recv_sems, local_sems = refs[2 * n:]
        _, me = _peer(0)
        pending = []
        for i in range(n):
            local = pltpu.make_async_copy(x_refs[i], out_refs[i].at[me], local_sems.at[i])
            local.start()
            pending.append(local)
        sends = []
        for i in range(n):
            for k in range(1, N_DEV):
                dev, _ = _peer(k)
                cp = pltpu.make_async_remote_copy(
                    src_ref=x_refs[i], dst_ref=out_refs[i].at[me], send_sem=send_sems.at[i, k - 1],
                    recv_sem=recv_sems.at[i, k - 1], device_id=dev, device_id_type=MESH_T)
                cp.start()
                sends.append(cp)
        for i in range(n):
            for k in range(1, N_DEV):
                dev, idx = _peer(k)
                pltpu.make_async_remote_copy(
                    src_ref=x_refs[i], dst_ref=out_refs[i].at[idx], send_sem=send_sems.at[i, k - 1],
                    recv_sem=recv_sems.at[i, k - 1], device_id=dev, device_id_type=MESH_T).wait_recv()
        for cp in sends:
            cp.wait_send()
        for local in pending:
            local.wait()

    out_shapes = [jax.ShapeDtypeStruct((N_DEV,) + a.shape, a.dtype) for a in xs]
    return _comm_call(body, n, out_shapes, name, xs)


def _all_to_all(gs, name):
    n = len(gs)

    def body(*refs):
        g_refs, out_refs = refs[:n], refs[n:2 * n]
        send_sems, recv_sems, local_sems = refs[2 * n:]
        _, me = _peer(0)
        pending = []
        for i in range(n):
            local = pltpu.make_async_copy(g_refs[i].at[me], out_refs[i].at[0], local_sems.at[i])
            local.start()
            pending.append(local)
        sends = []
        for i in range(n):
            for k in range(1, N_DEV):
                dev, idx = _peer(k)
                cp = pltpu.make_async_remote_copy(
                    src_ref=g_refs[i].at[idx], dst_ref=out_refs[i].at[k], send_sem=send_sems.at[i, k - 1],
                    recv_sem=recv_sems.at[i, k - 1], device_id=dev, device_id_type=MESH_T)
                cp.start()
                sends.append(cp)
        for cp in sends:
            cp.wait_recv()
        for cp in sends:
            cp.wait_send()
        for local in pending:
            local.wait()

    out_shapes = [jax.ShapeDtypeStruct(g.shape, g.dtype) for g in gs]
    return _comm_call(body, n, out_shapes, name, gs)


_SPLIT_RELATIONS = {"gather": tuple(range(1, N_DEV)), "a2a": tuple(range(1, N_DEV)), "chip": (1, 2, 4, 6),
                    "forward": (2, 4, 6)}


def _split_copies(kind, src_refs, land_refs, send_sems, recv_sems, local_sems, with_recv):
    n = len(land_refs)
    rels = _SPLIT_RELATIONS[kind]
    _, me = _peer(0)
    locals_, remotes = [], []
    for i in range(n):
        if kind in ("gather", "chip"):
            locals_.append(pltpu.make_async_copy(src_refs[i], land_refs[i].at[me], local_sems.at[i]))
        elif kind == "a2a":
            locals_.append(pltpu.make_async_copy(src_refs[i].at[me], land_refs[i].at[0], local_sems.at[i]))
        for jj, k in enumerate(rels):
            dev, idx = _peer(k)
            if kind in ("gather", "chip"):
                src, dst, mine = src_refs[i], land_refs[i].at[me], land_refs[i].at[idx]
            elif kind == "a2a":
                src, dst, mine = src_refs[i].at[idx], land_refs[i].at[k], land_refs[i].at[k]
            else:
                dev, _ = _peer(1)
                _, came = _peer(k + 1)
                src, dst, mine = land_refs[i].at[idx], land_refs[i].at[idx], land_refs[i].at[came]
            j = i * len(rels) + jj
            send = pltpu.make_async_remote_copy(
                src_ref=src, dst_ref=dst, send_sem=send_sems.at[j], recv_sem=recv_sems.at[j],
                device_id=dev, device_id_type=MESH_T)
            recv = pltpu.make_async_remote_copy(
                src_ref=src, dst_ref=mine, send_sem=send_sems.at[j], recv_sem=recv_sems.at[j],
                device_id=dev, device_id_type=MESH_T) if with_recv else None
            remotes.append((send, recv))
    return locals_, remotes


_HBM = pl.BlockSpec(memory_space=pltpu.HBM)
_SEM = pl.BlockSpec(memory_space=pltpu.SEMAPHORE)
_ANY = pl.BlockSpec(memory_space=pl.ANY)


def _split_start(srcs, kind, name, after):
    n = len(srcs)
    if kind == "forward":
        arrays = list(srcs)
    else:
        gathers = kind in ("gather", "chip")
        arrays = list(srcs) + [lax.empty(((N_DEV,) + a.shape) if gathers else a.shape, a.dtype) for a in srcs]
    na = len(arrays)

    def body(*refs):
        src_refs, land_refs = refs[:n], refs[na - n:na]
        send_sems, recv_sems, local_sems = refs[na + 1:na + 4]
        token = refs[-1]
        locals_, remotes = _split_copies(kind, src_refs, land_refs, send_sems, recv_sems, local_sems, False)
        for cp in locals_:
            cp.start()
        for send, _ in remotes:
            send.start()
        token[...] = jnp.zeros_like(token)

    ncp = n * len(_SPLIT_RELATIONS[kind])
    sems = (pltpu.SemaphoreType.DMA((ncp,)), pltpu.SemaphoreType.DMA((ncp,)), pltpu.SemaphoreType.DMA((n,)))
    thru = tuple(pltpu.HBM(a.shape, a.dtype) for a in arrays)
    outs = pl.pallas_call(
        body, name=name,
        out_shape=sems + thru + (jax.ShapeDtypeStruct((8, LANES), F32),),
        in_specs=[_HBM] * na + [_ANY],
        out_specs=[_SEM] * 3 + [_HBM] * na + [pl.BlockSpec(memory_space=pltpu.VMEM)],
        input_output_aliases={i: 3 + i for i in range(na)},
        compiler_params=pltpu.CompilerParams(has_side_effects=pltpu.SideEffectType.DATAFLOW_SIDE_EFFECTING),
    )(*[pltpu.with_memory_space_constraint(a, pltpu.HBM) for a in arrays], after)
    return (kind, n, outs[:3], outs[3:3 + na]), outs[-1]


def _split_wait(handle, name, after):
    kind, n, sems, thru = handle
    na = len(thru)

    def body(*refs):
        src_refs, land_refs = refs[:n], refs[na - n:na]
        send_sems, recv_sems, local_sems = refs[na:na + 3]
        locals_, remotes = _split_copies(kind, src_refs, land_refs, send_sems, recv_sems, local_sems, True)
        for send, recv in remotes:
            send.wait_send()
            recv.wait_recv()
        for cp in locals_:
            cp.wait()

    outs = pl.pallas_call(
        body, name=name, out_shape=tuple(pltpu.HBM(a.shape, a.dtype) for a in thru),
        in_specs=[_HBM] * na + [_SEM] * 3 + [_ANY], out_specs=[_HBM] * na,
        input_output_aliases={i: i for i in range(na)},
        compiler_params=pltpu.CompilerParams(has_side_effects=pltpu.SideEffectType.DATAFLOW_SIDE_EFFECTING),
    )(*thru, *sems, after)
    return list(outs[na - n:])


def _adamw(gslabs, w, m, v, name):
    R, Cw = w.shape
    if R % 8 == 0:
        tr, tc = _tile(R, 64 if Cw > 1024 else 128, 8), Cw
    else:
        tr, tc = R, LANES
    c1 = 1.0 - ADAM_B1 ** ADAM_STEP
    c2 = 1.0 - ADAM_B2 ** ADAM_STEP

    def body(g_ref, w_ref, m_ref, v_ref, go_ref, d_ref, mo_ref, vo_ref):
        g = g_ref[0].astype(F32)
        for k in range(1, N_DEV):
            g = g + g_ref[k].astype(F32)
        mn = ADAM_B1 * m_ref[...] + (1.0 - ADAM_B1) * g
        vn = ADAM_B2 * v_ref[...] + (1.0 - ADAM_B2) * (g * g)
        m_hat = mn / c1
        v_hat = vn / c2
        go_ref[...] = g
        d_ref[...] = -ADAM_LR * (m_hat / (jnp.sqrt(v_hat) + ADAM_EPS) + ADAM_WD * w_ref[...])
        mo_ref[...] = mn
        vo_ref[...] = vn

    blk = pl.BlockSpec((tr, tc), lambda i, j: (i, j))
    return pl.pallas_call(
        body, grid=(R // tr, Cw // tc),
        in_specs=[pl.BlockSpec((N_DEV, tr, tc), lambda i, j: (0, i, j)), blk, blk, blk],
        out_specs=[blk] * 4, out_shape=[jax.ShapeDtypeStruct((R, Cw), F32)] * 4, name=name)(gslabs, w, m, v)


def _pack(arrs, row_mult, dtype=F32):
    parts = []
    total = 0
    for a in arrs:
        f = a.reshape(-1).astype(dtype)
        n = -(-f.shape[0] // 1024) * 1024
        parts.append(jnp.pad(f, (0, n - f.shape[0])))
        total += n
    rows = total // LANES
    rows_p = -(-rows // row_mult) * row_mult
    flat = jnp.concatenate(parts)
    flat = jnp.pad(flat, (0, rows_p * LANES - total))
    return flat.reshape(rows_p, LANES)


def _unpack(packed, shapes):
    lead = packed.shape[:-2]
    flat = packed.reshape(lead + (-1,))
    out = []
    off = 0
    for s in shapes:
        n = int(np.prod(s))
        out.append(flat[..., off:off + n].reshape(lead + tuple(s)))
        off += -(-n // 1024) * 1024
    return out


def _gather_cols(stacked):
    d, r, c = stacked.shape
    return stacked.transpose(1, 0, 2).reshape(r, d * c)


def _scatter_cols(full):
    r, n = full.shape
    return full.reshape(r, N_DEV, n // N_DEV).transpose(1, 0, 2)


def kernel(x, meta, norm1, w_in, gdn_conv_w, gdn_a_log, gdn_dt_bias, gdn_norm, w_out, norm2, w_ffn_up, ffn_conv_w, ffn_conv_b, w_ffn_down, norm_f, loss_target, m_meta, m_norm1, m_w_in, m_gdn_conv_w, m_gdn_a_log, m_gdn_dt_bias, m_gdn_norm, m_w_out, m_norm2, m_w_ffn_up, m_ffn_conv_w, m_ffn_conv_b, m_w_ffn_down, m_norm_f, v_meta, v_norm1, v_w_in, v_gdn_conv_w, v_gdn_a_log, v_gdn_dt_bias, v_gdn_norm, v_w_out, v_norm2, v_w_ffn_up, v_ffn_conv_w, v_ffn_conv_b, v_w_ffn_down, v_norm_f):
    S = x.shape[1]
    L = N_META + S
    pad = (-L) % CHUNK
    Lp = L + pad

    tr_ = lambda a: jnp.swapaxes(a[0], 0, 1)
    big = [tr_(w_in), w_out[0], tr_(w_ffn_up), w_ffn_down[0]]
    small = [meta, gdn_conv_w, ffn_conv_w]
    small_all, = _all_gather([_pack(small, 8)], "gather_small_weights")
    first, first_token = _split_start([big[0].astype(BF16)], "chip", "gather_w_in_start", small_all)
    late, late_token = _split_start([a.astype(BF16) for a in big[1:]], "gather", "gather_late_start", first_token)

    def first_weights(after):
        half = _split_wait(first, "gather_w_in_wait", after)
        second, second_token = _split_start(half, "forward", "gather_w_in_forward_start", after)
        w_in_s, = _split_wait(second, "gather_w_in_forward_wait", second_token)
        w_in_t = w_in_s.reshape(_O_END, D_MODEL)
        w_main_t = jnp.concatenate([w_in_t[_O_GQ:_O_GZ], w_in_t[_O_RQ:_O_RG], w_in_t[_O_GZ:_O_GA],
                                    w_in_t[_O_RG:_O_END]], axis=0)
        return {"w_main_t": w_main_t, "w_small_t": jnp.pad(w_in_t[_O_GA:_O_RQ], ((0, LANES - 2 * GDN_H), (0, 0)))}

    def late_weights(after):
        w_out_s, w_up_s, w_down_s = _split_wait(late, "gather_late_wait", after)
        return {"w_out": w_out_s.reshape(D_MODEL, D_MODEL), "w_up_t": w_up_s.reshape(2 * D_FF, D_MODEL),
                "w_down": w_down_s.reshape(D_FF, D_MODEL)}

    meta_s, gconv_s, fconv_s = _unpack(small_all, [a.shape for a in small])
    wt = {
        "norm1": norm1 + jnp.tile(late_token[0:1, :], (1, D_MODEL // LANES)),
        "gdn_conv_w": _gather_cols(gconv_s[:, 0]), "a_log": gdn_a_log[0], "dt_bias": gdn_dt_bias[0],
        "gdn_norm": gdn_norm, "norm2": norm2, "ffn_conv_w": _gather_cols(fconv_s[:, 0]), "ffn_conv_b": ffn_conv_b,
        "norm_f": norm_f.reshape(1, D_MODEL),
    }
    meta_f = _gather_cols(meta_s)

    pending = {}

    def on_ffn_out_grads(d_w_down, d_w_up_t, d_w_out):
        srcs = [d_w_out.reshape(N_DEV, D_MODEL // N_DEV, D_MODEL), d_w_up_t.reshape(N_DEV, 2 * D_FF // N_DEV, D_MODEL),
                d_w_down.reshape(N_DEV, D_FF // N_DEV, D_MODEL)]
        pending["ffn_out"], token = _split_start(srcs, "a2a", "exchange_ffn_out_start", d_w_out)
        return token

    def on_w_in_grads(d_w_in_t):
        slabs = d_w_in_t.astype(BF16).reshape(N_DEV, _O_END // N_DEV, D_MODEL)
        pending["w_in"], token = _split_start([slabs], "a2a", "exchange_w_in_start", d_w_in_t)
        return token

    hpad = jnp.concatenate([jnp.zeros((pad, D_MODEL), F32), meta_f, x[0]], axis=0)
    tgt = jnp.concatenate([jnp.zeros((pad + N_META, D_MODEL), F32), loss_target[0]], axis=0)
    lossvec, dh0, gr = _local_step(hpad, tgt, pad, wt, first_weights, late_weights, on_ffn_out_grads, on_w_in_grads)

    loss = lax.psum(jnp.sum(lossvec), ("x", "y", "c"))
    grad_x = dh0[pad + N_META:][None]

    big_m = [tr_(m_w_in), m_w_out[0], tr_(m_w_ffn_up), m_w_ffn_down[0]]
    big_v = [tr_(v_w_in), v_w_out[0], tr_(v_w_ffn_up), v_w_ffn_down[0]]
    slabs_ffn_out = _split_wait(pending["ffn_out"], "exchange_ffn_out_wait", dh0)
    big_out = [None] + [_adamw(slabs_ffn_out[i - 1], big[i], big_m[i], big_v[i], "adamw_big_%d" % i)
                        for i in range(1, len(big))]
    g_sm = [_scatter_cols(dh0[pad:pad + N_META]), _scatter_cols(gr["gdn_conv_w"]), _scatter_cols(gr["ffn_conv_w"])]
    g_small = jnp.stack([_pack([g[d] for g in g_sm], 8) for d in range(N_DEV)])
    slabs_small, = _all_to_all([g_small], "exchange_small_gradients")
    small_out = _adamw(slabs_small, _pack(small, 8), _pack([m_meta, m_gdn_conv_w, m_ffn_conv_w], 8),
                       _pack([v_meta, v_gdn_conv_w, v_ffn_conv_w], 8), "adamw_small_sharded")
    small_un = [_unpack(o, [a.shape for a in small]) for o in small_out]
    rep_w = [norm1, gdn_a_log, gdn_dt_bias, gdn_norm, norm2, ffn_conv_b, norm_f]
    rep_m = [m_norm1, m_gdn_a_log, m_gdn_dt_bias, m_gdn_norm, m_norm2, m_ffn_conv_b, m_norm_f]
    rep_v = [v_norm1, v_gdn_a_log, v_gdn_dt_bias, v_gdn_norm, v_norm2, v_ffn_conv_b, v_norm_f]
    rep_g = [gr["norm1"], gr["a_log"], gr["dt_bias"], gr["gdn_norm"], gr["norm2"], gr["ffn_conv_b"], gr["norm_f"]]
    rep_slabs, = _all_gather([_pack(rep_g, 8)], "gather_small_gradients")
    rep_out = _adamw(rep_slabs, _pack(rep_w, 8), _pack(rep_m, 8), _pack(rep_v, 8), "adamw_replicated")
    rep_shapes = [a.shape for a in rep_w]
    rp_g, rp_d, rp_nm, rp_nv = [_unpack(o, rep_shapes) for o in rep_out]

    slabs_w_in, = _split_wait(pending["w_in"], "exchange_w_in_wait", rep_out[0])
    big_out[0] = _adamw(slabs_w_in, big[0], big_m[0], big_v[0], "adamw_big_0")
    back = lambda a: jnp.swapaxes(a, 0, 1)[None]
    sh_g, sh_d, sh_nm, sh_nv = [
        [small_un[j][0], back(big_out[0][j]), small_un[j][1], big_out[1][j][None], back(big_out[2][j]),
         small_un[j][2], big_out[3][j][None]] for j in range(4)]

    def order(sh, rp):
        return [sh[0], rp[0], sh[1], sh[2], rp[1], rp[2], rp[3], sh[3], rp[4], sh[4], sh[5], rp[5], sh[6], rp[6]]

    return (loss, grad_x, *order(sh_g, rp_g), *order(sh_d, rp_d), *order(sh_nm, rp_nm), *order(sh_nv, rp_nv))
```

```python
import functools
import math

import numpy as np
import jax
import jax.numpy as jnp
from jax import lax
from jax.experimental import pallas as pl
from jax.experimental.pallas import tpu as pltpu

F32 = jnp.float32
BF16 = jnp.bfloat16
HI = lax.Precision.HIGHEST

D_MODEL = 1024
N_META = 16
CHUNK = 64
GDN_H = 8
GDN_D = 128
RET_H = 4
RET_D = 256
D_FF = 2816
GDN_CONV = 4
FFN_CONV = 3
ROPE_BASE = 10000.0
EPS = 1e-6
N_DEV = 8
LANES = 128
MAIN_W = 10 * 1024
_O_GQ, _O_GZ, _O_GA, _O_RQ, _O_RG, _O_GATE, _O_END = 0, 3072, 4096, 4112, 7184, 8208, 10256

ADAM_LR = 0.001
ADAM_B1 = 0.9
ADAM_B2 = 0.999
ADAM_EPS = 1e-08
ADAM_WD = 0.01
ADAM_STEP = 10

MESH_T = pl.DeviceIdType.MESH


def _tile(n, target, mult):
    best = None
    for d in range(mult, min(n, target) + 1, mult):
        if n % d == 0:
            best = d
    assert best is not None, (n, target, mult)
    return best


def _sig(x):
    return 1.0 / (1.0 + jnp.exp(-x))


def _d(a, b):
    return jnp.dot(a.astype(BF16), b.astype(BF16), preferred_element_type=F32)


def _dnt(a, b):
    return lax.dot_general(a.astype(BF16), b.astype(BF16), (((1,), (1,)), ((), ())), preferred_element_type=F32)


def _dtn(a, b):
    return lax.dot_general(a.astype(BF16), b.astype(BF16), (((0,), (0,)), ((), ())), preferred_element_type=F32)


def _dx(a, b):
    return jnp.dot(a, b, preferred_element_type=F32, precision=HI)


def _dxnt(a, b):
    return lax.dot_general(a, b, (((1,), (1,)), ((), ())), preferred_element_type=F32, precision=HI)


def _dxtn(a, b):
    return lax.dot_general(a, b, (((0,), (0,)), ((), ())), preferred_element_type=F32, precision=HI)


def _split(a):
    hi = a.astype(BF16)
    return hi, (a - hi.astype(F32)).astype(BF16)


def _d3g(a, b, dims):
    ah, al = _split(a)
    bh, bl = _split(b)
    f = functools.partial(lax.dot_general, dimension_numbers=dims, preferred_element_type=F32)
    if dims == _NN:
        rows = a.shape[0]
        both = f(jnp.concatenate([ah, al], axis=0), bh)
        return both[:rows] + (f(ah, bl) + both[rows:])
    return f(ah, bh) + (f(ah, bl) + f(al, bh))


_NN = (((1,), (0,)), ((), ()))
_NT = (((1,), (1,)), ((), ()))
_TN = (((0,), (0,)), ((), ()))


def _rowsum(x):
    return jnp.sum(x, axis=1, keepdims=True)


def _allsum(x):
    return jnp.sum(jnp.sum(x, axis=1, keepdims=True), axis=0, keepdims=True)


def _mm_nn(a, b, res=None, out_dtype=F32, bt=False, name="mm_nn"):
    M, K = a.shape
    N = b.shape[0] if bt else b.shape[1]
    tm = _tile(M, 704, 16)
    tn = _tile(N, 2816, 128)

    def body(*refs):
        if res is None:
            a_ref, b_ref, o_ref = refs
        else:
            a_ref, b_ref, r_ref, o_ref = refs
        acc = lax.dot_general(a_ref[...], b_ref[...], _NT if bt else _NN, preferred_element_type=F32)
        if res is not None:
            acc = acc + r_ref[...]
        o_ref[...] = acc.astype(out_dtype)

    b_spec = pl.BlockSpec((tn, K), lambda j, i: (j, 0)) if bt else pl.BlockSpec((K, tn), lambda j, i: (0, j))
    in_specs = [pl.BlockSpec((tm, K), lambda j, i: (i, 0)), b_spec]
    args = [a, b]
    if res is not None:
        in_specs.append(pl.BlockSpec((tm, tn), lambda j, i: (i, j)))
        args.append(res)
    return pl.pallas_call(
        body, grid=(N // tn, M // tm), in_specs=in_specs,
        out_specs=pl.BlockSpec((tm, tn), lambda j, i: (i, j)),
        out_shape=jax.ShapeDtypeStruct((M, N), out_dtype), name=name)(*args)


def _mm_nt(a, b, res=None, name="mm_nt"):
    M, Nc = a.shape
    K = b.shape[0]
    tm = _tile(M, 704, 16)
    tc = _tile(Nc, 5632, 128)

    def body(*refs):
        if res is None:
            a_ref, b_ref, o_ref = refs
        else:
            a_ref, b_ref, r_ref, o_ref = refs
        c = pl.program_id(1)
        p = lax.dot_general(a_ref[...], b_ref[...], (((1,), (1,)), ((), ())), preferred_element_type=F32)

        @pl.when(c == 0)
        def _():
            if res is None:
                o_ref[...] = p
            else:
                o_ref[...] = p + r_ref[...]

        @pl.when(c > 0)
        def _():
            o_ref[...] += p

    in_specs = [pl.BlockSpec((tm, tc), lambda i, c: (i, c)), pl.BlockSpec((K, tc), lambda i, c: (0, c))]
    args = [a, b]
    if res is not None:
        in_specs.append(pl.BlockSpec((tm, K), lambda i, c: (i, 0)))
        args.append(res)
    return pl.pallas_call(
        body, grid=(M // tm, Nc // tc), in_specs=in_specs,
        out_specs=pl.BlockSpec((tm, K), lambda i, c: (i, 0)),
        out_shape=jax.ShapeDtypeStruct((M, K), F32), name=name)(*args)


def _mm_tn(a, b, out_dtype=F32, name="mm_tn"):
    M, K = a.shape
    N = b.shape[1]
    tm = _tile(M, 2752, 16)
    tk = _tile(K, 1408, 128)
    tn = _tile(N, 1408, 128)
    steps = M // tm

    def body(a_ref, b_ref, o_ref, *scratch):
        acc = scratch[0] if scratch else o_ref
        m = pl.program_id(2)
        p = lax.dot_general(a_ref[...], b_ref[...], (((0,), (0,)), ((), ())), preferred_element_type=F32)

        @pl.when(m == 0)
        def _():
            acc[...] = p

        @pl.when(m > 0)
        def _():
            acc[...] += p

        if scratch:
            @pl.when(m == steps - 1)
            def _():
                o_ref[...] = acc[...].astype(out_dtype)

    return pl.pallas_call(
        body, grid=(K // tk, N // tn, steps),
        in_specs=[pl.BlockSpec((tm, tk), lambda kk, j, m: (m, kk)), pl.BlockSpec((tm, tn), lambda kk, j, m: (m, j))],
        out_specs=pl.BlockSpec((tk, tn), lambda kk, j, m: (kk, j)),
        out_shape=jax.ShapeDtypeStruct((K, N), out_dtype),
        scratch_shapes=[] if out_dtype == F32 else [pltpu.VMEM((tk, tn), F32)], name=name)(a, b)


def _rms_fwd(x, g, name):
    Lp = x.shape[0]
    tr = _tile(Lp, 256, 16)

    def body(x_ref, g_ref, o_ref):
        xv = x_ref[...]
        r = lax.rsqrt(jnp.mean(xv * xv, axis=-1, keepdims=True) + EPS)
        o_ref[...] = (xv * r * g_ref[...]).astype(BF16)

    return pl.pallas_call(
        body, grid=(Lp // tr,),
        in_specs=[pl.BlockSpec((tr, D_MODEL), lambda i: (i, 0)), pl.BlockSpec((1, D_MODEL), lambda i: (0, 0))],
        out_specs=pl.BlockSpec((tr, D_MODEL), lambda i: (i, 0)),
        out_shape=jax.ShapeDtypeStruct((Lp, D_MODEL), BF16), name=name)(x, g)


class _Producer:
    def __init__(self, a, b, res=None):
        self.a, self.b, self.res = a, b, res
        self.tr = _tile(a.shape[0], 704, 16)
        K = a.shape[1]
        self.args = [a, b] + ([] if res is None else [res])
        self.specs = [pl.BlockSpec((self.tr, K), lambda i: (i, 0)),
                      pl.BlockSpec((K, D_MODEL), lambda i: (0, 0), pipeline_mode=pl.Buffered(1))]
        if res is not None:
            self.specs.append(pl.BlockSpec((self.tr, D_MODEL), lambda i: (i, 0)))

    def tile(self, refs):
        acc = jnp.dot(refs[0][...], refs[1][...], preferred_element_type=F32)
        return acc if self.res is None else acc + refs[2][...]


def _mm_rms_fwd(prod, g, name):
    Lp, tr, n = prod.a.shape[0], prod.tr, len(prod.args)

    def body(*refs):
        g_ref, x_ref, o_ref = refs[n:]
        xv = prod.tile(refs[:n])
        r = lax.rsqrt(jnp.mean(xv * xv, axis=-1, keepdims=True) + EPS)
        x_ref[...] = xv
        o_ref[...] = (xv * r * g_ref[...]).astype(BF16)

    blk = pl.BlockSpec((tr, D_MODEL), lambda i: (i, 0))
    return pl.pallas_call(
        body, grid=(Lp // tr,), in_specs=prod.specs + [pl.BlockSpec((1, D_MODEL), lambda i: (0, 0))],
        out_specs=[blk, blk],
        out_shape=[jax.ShapeDtypeStruct((Lp, D_MODEL), F32), jax.ShapeDtypeStruct((Lp, D_MODEL), BF16)],
        name=name)(*prod.args, g)


def _rms_bwd(x, g, dy, dres, pad, name):
    Lp = x.shape[0]
    fused = isinstance(dy, _Producer)
    tr = dy.tr if fused else _tile(Lp, 256, 16)
    n = len(dy.args) if fused else 1

    def body(*refs):
        x_ref, g_ref, dr_ref, dx_ref, dxb_ref, dg_ref = refs[n:]
        i = pl.program_id(0)
        xv = x_ref[...]
        r = lax.rsqrt(jnp.mean(xv * xv, axis=-1, keepdims=True) + EPS)
        xh = xv * r
        dyv = dy.tile(refs[:n]) if fused else refs[0][...]
        dxh = dyv * g_ref[...]
        dx = r * (dxh - xh * jnp.mean(dxh * xh, axis=-1, keepdims=True)) + dr_ref[...]
        row = i * tr + lax.broadcasted_iota(jnp.int32, (tr, 1), 0)
        dx = jnp.where(row >= pad, dx, 0.0)
        dx_ref[...] = dx
        dxb_ref[...] = dx.astype(BF16)
        part = jnp.sum(dyv * xh, axis=0, keepdims=True)

        @pl.when(i == 0)
        def _():
            dg_ref[...] = part

        @pl.when(i > 0)
        def _():
            dg_ref[...] += part

    blk = pl.BlockSpec((tr, D_MODEL), lambda i: (i, 0))
    vec = pl.BlockSpec((1, D_MODEL), lambda i: (0, 0))
    return pl.pallas_call(
        body, grid=(Lp // tr,), in_specs=(dy.specs if fused else [blk]) + [blk, vec, blk], out_specs=[blk, blk, vec],
        out_shape=[jax.ShapeDtypeStruct((Lp, D_MODEL), F32), jax.ShapeDtypeStruct((Lp, D_MODEL), BF16),
                   jax.ShapeDtypeStruct((1, D_MODEL), F32)], name=name)(*(dy.args if fused else [dy]), x, g, dres)


def _final(h2, g, tgt, first_row):
    fused = isinstance(h2, _Producer)
    Lp = h2.a.shape[0] if fused else h2.shape[0]
    tr = h2.tr if fused else _tile(Lp, 256, 16)
    n = len(h2.args) if fused else 1

    def body(*refs):
        g_ref, t_ref, loss_ref, dx_ref, dxb_ref, dg_ref = refs[n:]
        i = pl.program_id(0)
        xv = h2.tile(refs[:n]) if fused else refs[0][...]
        gv = g_ref[...]
        r = lax.rsqrt(jnp.mean(xv * xv, axis=-1, keepdims=True) + EPS)
        xh = xv * r
        row = i * tr + lax.broadcasted_iota(jnp.int32, (tr, 1), 0)
        err = jnp.where(row >= first_row, xh * gv - t_ref[...], 0.0)
        lpart = jnp.sum(err * err, axis=0, keepdims=True) * (0.5 / D_MODEL)
        dyv = err * (1.0 / D_MODEL)
        dxh = dyv * gv
        dx = r * (dxh - xh * jnp.mean(dxh * xh, axis=-1, keepdims=True))
        dx_ref[...] = dx
        dxb_ref[...] = dx.astype(BF16)
        part = jnp.sum(dyv * xh, axis=0, keepdims=True)

        @pl.when(i == 0)
        def _():
            dg_ref[...] = part
            loss_ref[...] = lpart

        @pl.when(i > 0)
        def _():
            dg_ref[...] += part
            loss_ref[...] += lpart

    blk = pl.BlockSpec((tr, D_MODEL), lambda i: (i, 0))
    vec = pl.BlockSpec((1, D_MODEL), lambda i: (0, 0))
    return pl.pallas_call(
        body, grid=(Lp // tr,), in_specs=(h2.specs if fused else [blk]) + [vec, blk], out_specs=[vec, blk, blk, vec],
        out_shape=[jax.ShapeDtypeStruct((1, D_MODEL), F32), jax.ShapeDtypeStruct((Lp, D_MODEL), F32),
                   jax.ShapeDtypeStruct((Lp, D_MODEL), BF16), jax.ShapeDtypeStruct((1, D_MODEL), F32)],
        name="final_norm_loss")(*(h2.args if fused else [h2]), g, tgt)


def _halo_prev(tr, width, col=0):
    return pl.BlockSpec((8, width), lambda i: (jnp.maximum(i * (tr // 8) - 1, 0), col))


def _halo_next(tr, width, nrows, col=0, rows=8):
    last = nrows // rows - 1
    return pl.BlockSpec((rows, width), lambda i: (jnp.minimum((i + 1) * (tr // rows), last), col))


def _shifted(x, offs):
    n = x.shape[0]
    return [x if off == 0 else pltpu.roll(x, n - off, 0) for off in offs]


def _taps(wins, w, rows, bias=None):
    acc = w[0:1, :] * wins[0][0:rows, :]
    if bias is not None:
        acc = acc + bias
    for kk in range(1, len(wins)):
        acc = acc + w[kk:kk + 1, :] * wins[kk][0:rows, :]
    return acc


def _gdn_pre(proj_m, proj_s, conv_w, gparams, pad):
    Lp = proj_m.shape[0]
    tr = _tile(Lp, 192, 64)
    W3 = 3 * D_MODEL

    def body(main_ref, prev_ref, s_ref, w_ref, gp_ref, qkv_ref, gsm_ref, c_ref):
        i = pl.program_id(0)
        prev = jnp.where(i > 0, prev_ref[...], 0.0)
        ext = jnp.concatenate([prev, main_ref[...]], axis=0)
        c = _taps(_shifted(ext, range(8 - (GDN_CONV - 1), 9)), w_ref[...], tr)
        c_ref[...] = c.astype(BF16)
        s = c * _sig(c)
        scale = GDN_D ** -0.5
        for j in range(2 * GDN_H):
            seg = s[:, j * GDN_D:(j + 1) * GDN_D]
            r = lax.rsqrt(_rowsum(seg * seg) + EPS)
            if j < GDN_H:
                r = r * scale
            qkv_ref[:, j * GDN_D:(j + 1) * GDN_D] = seg * r
        qkv_ref[:, 2 * D_MODEL:] = s[:, 2 * D_MODEL:]
        sm = s_ref[...]
        gp = gp_ref[...]
        lane = lax.broadcasted_iota(jnp.int32, sm.shape, 1)
        z = sm + gp[1:2, :]
        softplus = jnp.maximum(z, 0.0) + jnp.log(1.0 + jnp.exp(-jnp.abs(z)))
        lg = -jnp.exp(gp[0:1, :]) * softplus
        row = i * tr + lax.broadcasted_iota(jnp.int32, (tr, 1), 0)
        out = jnp.where(lane < GDN_H, lg, jnp.where(lane < 2 * GDN_H, _sig(sm), 0.0))
        gsm_ref[...] = jnp.where(row >= pad, out, 0.0)

    return pl.pallas_call(
        body, grid=(Lp // tr,),
        in_specs=[pl.BlockSpec((tr, W3), lambda i: (i, 0)), _halo_prev(tr, W3),
                  pl.BlockSpec((tr, LANES), lambda i: (i, 0)),
                  pl.BlockSpec((GDN_CONV, W3), lambda i: (0, 0)), pl.BlockSpec((8, LANES), lambda i: (0, 0))],
        out_specs=[pl.BlockSpec((tr, W3), lambda i: (i, 0)), pl.BlockSpec((tr, LANES), lambda i: (i, 0)),
                   pl.BlockSpec((tr, W3), lambda i: (i, 0))],
        out_shape=[jax.ShapeDtypeStruct((Lp, W3), F32), jax.ShapeDtypeStruct((Lp, LANES), F32),
                   jax.ShapeDtypeStruct((Lp, W3), BF16)],
        name="gdn_pre")(proj_m, proj_m, proj_s, conv_w, gparams)


def _gdn_pre_bwd(proj_m, conv_out, proj_s, conv_w, gparams, dq, dk, dv, dgs, pad):
    Lp = proj_m.shape[0]
    tr = _tile(Lp, 192, 64)
    W3 = 3 * D_MODEL
    te = tr + 8

    def body(main_ref, c_ref, cn_ref, s_ref, w_ref, gp_ref,
             dq_ref, dqn_ref, dk_ref, dkn_ref, dv_ref, dvn_ref, dgs_ref,
             da_ref, ds_ref, dw_ref, dgp_ref):
        i = pl.program_id(0)
        w = w_ref[...]
        c = jnp.concatenate([c_ref[...].astype(F32), cn_ref[...].astype(F32)[0:8]], axis=0)
        sg = _sig(c)
        s = c * sg
        rowe = i * tr + lax.broadcasted_iota(jnp.int32, (te, 1), 0)
        live = (rowe >= pad) & (rowe < Lp)
        dqe = jnp.concatenate([dq_ref[...], dqn_ref[...]], axis=0)
        dke = jnp.concatenate([dk_ref[...], dkn_ref[...]], axis=0)
        dve = jnp.concatenate([dv_ref[...], dvn_ref[...]], axis=0)
        scale = GDN_D ** -0.5
        parts = []
        for j in range(2 * GDN_H):
            seg = s[:, j * GDN_D:(j + 1) * GDN_D]
            r = lax.rsqrt(_rowsum(seg * seg) + EPS)
            xh = seg * r
            if j < GDN_H:
                dxh = dqe[:, j * GDN_D:(j + 1) * GDN_D] * scale
            else:
                dxh = dke[:, (j - GDN_H) * GDN_D:(j - GDN_H + 1) * GDN_D]
            parts.append(r * (dxh - xh * _rowsum(dxh * xh)))
        parts.append(dve)
        dsv = jnp.concatenate(parts, axis=1)
        dc = jnp.where(live, dsv * (sg * (1.0 + c * (1.0 - sg))), 0.0)
        dcs = _shifted(dc, range(GDN_CONV - 1, -1, -1))
        da_ref[...] = _taps(dcs, w, tr).astype(BF16)
        pm = main_ref[...]
        rows = [jnp.sum(dcs[kk][0:tr, :] * pm, axis=0, keepdims=True) for kk in range(GDN_CONV)]
        dwp = jnp.concatenate(rows + [jnp.zeros((8 - GDN_CONV, W3), F32)], axis=0)

        sm = s_ref[...]
        gp = gp_ref[...]
        lane = lax.broadcasted_iota(jnp.int32, sm.shape, 1)
        rowm = i * tr + lax.broadcasted_iota(jnp.int32, (tr, 1), 0)
        dgv = jnp.where(rowm >= pad, dgs_ref[...], 0.0)
        dlg = jnp.where(lane < GDN_H, dgv, 0.0)
        dbt = jnp.where((lane >= GDN_H) & (lane < 2 * GDN_H), dgv, 0.0)
        z = sm + gp[1:2, :]
        softplus = jnp.maximum(z, 0.0) + jnp.log(1.0 + jnp.exp(-jnp.abs(z)))
        ea = jnp.exp(gp[0:1, :])
        dz = dlg * (-ea) * _sig(z)
        dal = dlg * (-ea) * softplus
        bt = _sig(sm)
        dgb = dbt * bt * (1.0 - bt)
        ds_ref[...] = (dz + dgb).astype(BF16)
        gpp = jnp.concatenate([jnp.sum(dal, axis=0, keepdims=True), jnp.sum(dz, axis=0, keepdims=True),
                               jnp.zeros((6, LANES), F32)], axis=0)

        @pl.when(i == 0)
        def _():
            dw_ref[...] = dwp
            dgp_ref[...] = gpp

        @pl.when(i > 0)
        def _():
            dw_ref[...] += dwp
            dgp_ref[...] += gpp

    m3 = pl.BlockSpec((tr, W3), lambda i: (i, 0))
    m1 = pl.BlockSpec((tr, D_MODEL), lambda i: (i, 0))
    n1 = _halo_next(tr, D_MODEL, Lp)
    return pl.pallas_call(
        body, grid=(Lp // tr,),
        in_specs=[m3, m3, _halo_next(tr, W3, Lp, rows=16), pl.BlockSpec((tr, LANES), lambda i: (i, 0)),
                  pl.BlockSpec((GDN_CONV, W3), lambda i: (0, 0)), pl.BlockSpec((8, LANES), lambda i: (0, 0)),
                  m1, n1, m1, n1, m1, n1, pl.BlockSpec((tr, LANES), lambda i: (i, 0))],
        out_specs=[m3, pl.BlockSpec((tr, LANES), lambda i: (i, 0)),
                   pl.BlockSpec((8, W3), lambda i: (0, 0)), pl.BlockSpec((8, LANES), lambda i: (0, 0))],
        out_shape=[jax.ShapeDtypeStruct((Lp, W3), BF16), jax.ShapeDtypeStruct((Lp, LANES), BF16),
                   jax.ShapeDtypeStruct((8, W3), F32), jax.ShapeDtypeStruct((8, LANES), F32)],
        name="gdn_pre_bwd")(proj_m, conv_out, conv_out, proj_s, conv_w, gparams, dq, dq, dk, dk, dv, dv, dgs)


def _gdn_gates(gs):
    ri = lax.broadcasted_iota(jnp.int32, (CHUNK, CHUNK), 0)
    ci = lax.broadcasted_iota(jnp.int32, (CHUNK, CHUNK), 1)
    tril = ri >= ci
    strict = ri > ci
    gall = _dx(tril.astype(F32), gs)
    lane8 = lax.broadcasted_iota(jnp.int32, (8, LANES), 1)
    sub8 = lax.broadcasted_iota(jnp.int32, (8, LANES), 0)
    grow = _dxnt((lane8 == sub8).astype(F32), gall)
    return gall, grow, tril, strict


def _gdn_decay(gall, grow, tril, h):
    g = gall[:, h:h + 1]
    return g, jnp.where(tril, jnp.exp(jnp.where(tril, g - grow[h:h + 1, :], 0.0)), 0.0)


def _group(N):
    return 3 if N % 3 == 0 else (2 if N % 2 == 0 else 1)


def _gdn_chunk_specs(N, rev):
    G = _group(N)
    nb = N // G
    cn = (lambda n: nb - 1 - n) if rev else (lambda n: n)
    col = lambda j: pl.BlockSpec((G * CHUNK, D_MODEL), lambda n: (cn(n), j))
    gate = pl.BlockSpec((G * CHUNK, LANES), lambda n: (cn(n), 0))
    st = lambda a, b: pl.BlockSpec((GDN_H, G, a, b), lambda n: (0, cn(n), 0, 0))
    return G, nb, col, gate, st


def _gdn_chunk_fwd(qkv, gsm):
    Lp = qkv.shape[0]
    N = Lp // CHUNK
    G, nb, col, gate, st = _gdn_chunk_specs(N, False)

    def body(q_ref, k_ref, v_ref, gs_ref, o_ref, sin_ref, t_ref, S):
        n = pl.program_id(0)

        @pl.when(n == 0)
        def _():
            S[...] = jnp.zeros_like(S)

        ri = lax.broadcasted_iota(jnp.int32, (CHUNK, CHUNK), 0)
        ci = lax.broadcasted_iota(jnp.int32, (CHUNK, CHUNK), 1)
        eye = (ri == ci).astype(F32)
        heads = range(GDN_H)
        sls = [slice(h * GDN_D, (h + 1) * GDN_D) for h in heads]
        rows = [slice(c * CHUNK, (c + 1) * CHUNK) for c in range(G)]
        pairs = [(c, h) for c in range(G) for h in heads]
        P = lambda f: {p: f(*p) for p in pairs}
        gs = [gs_ref[rows[c], :] for c in range(G)]
        gates = [_gdn_gates(gs[c]) for c in range(G)]
        tril, strict = gates[0][2], gates[0][3]
        q = P(lambda c, h: q_ref[rows[c], sls[h]])
        k = P(lambda c, h: k_ref[rows[c], sls[h]])
        v = P(lambda c, h: v_ref[rows[c], sls[h]])
        beta = P(lambda c, h: gs[c][:, GDN_H + h:GDN_H + h + 1])
        gg = P(lambda c, h: _gdn_decay(gates[c][0], gates[c][1], tril, h))
        g = {p: x[0] for p, x in gg.items()}
        gam = {p: x[1] for p, x in gg.items()}
        eg = P(lambda c, h: jnp.exp(g[c, h]))
        gl = P(lambda c, h: g[c, h][CHUNK - 1:CHUNK, :])
        kb = P(lambda c, h: k[c, h] * beta[c, h])
        pw = P(lambda c, h: -jnp.where(strict, _dnt(kb[c, h], k[c, h]) * gam[c, h], 0.0))
        p = P(lambda c, h: _dnt(q[c, h], k[c, h]) * gam[c, h])
        t = P(lambda c, h: eye + pw[c, h])
        for _ in range(5):
            pw = P(lambda c, h: _d3g(pw[c, h], pw[c, h], _NN))
            t = P(lambda c, h: t[c, h] + _d3g(t[c, h], pw[c, h], _NN))
        u = P(lambda c, h: _d(t[c, h], v[c, h] * beta[c, h]))
        w = P(lambda c, h: _d(t[c, h], kb[c, h] * eg[c, h]))
        qg = P(lambda c, h: q[c, h] * eg[c, h])
        kd = P(lambda c, h: k[c, h] * jnp.exp(gl[c, h] - g[c, h]))
        egl = P(lambda c, h: jnp.exp(gl[c, h]))
        for c in range(G):
            for h in heads:
                t_ref[h, c] = t[c, h]
        cur = [S[h] for h in heads]
        for c in range(G):
            vnew = [u[c, h] - _d(w[c, h], cur[h]) for h in heads]
            for h in heads:
                o_ref[rows[c], sls[h]] = _d(qg[c, h], cur[h]) + _d(p[c, h], vnew[h])
                sin_ref[h, c] = cur[h]
            cur = [cur[h] * egl[c, h] + _dtn(kd[c, h], vnew[h]) for h in heads]
        for h in heads:
            S[h] = cur[h]

    return pl.pallas_call(
        body, grid=(nb,),
        in_specs=[col(0), col(1), col(2), gate],
        out_specs=[col(0), st(GDN_D, GDN_D), st(CHUNK, CHUNK)],
        out_shape=[jax.ShapeDtypeStruct((Lp, D_MODEL), F32), jax.ShapeDtypeStruct((GDN_H, N, GDN_D, GDN_D), F32),
                   jax.ShapeDtypeStruct((GDN_H, N, CHUNK, CHUNK), F32)],
        scratch_shapes=[pltpu.VMEM((GDN_H, GDN_D, GDN_D), F32)],
        name="gdn_chunk_fwd")(qkv, qkv, qkv, gsm)


def _gdn_chunk_bwd(qkv, gsm, do, s_in, t_in):
    Lp = qkv.shape[0]
    N = Lp // CHUNK
    G, nb, col, gate, st = _gdn_chunk_specs(N, True)

    def body(q_ref, k_ref, v_ref, gs_ref, do_ref, sin_ref, t_ref, dq_ref, dk_ref, dv_ref, dgs_ref, dS):
        n = pl.program_id(0)

        @pl.when(n == 0)
        def _():
            dS[...] = jnp.zeros_like(dS)

        lane = lax.broadcasted_iota(jnp.int32, (CHUNK, LANES), 1)
        rcol = lax.broadcasted_iota(jnp.int32, (CHUNK, 1), 0)
        ri = lax.broadcasted_iota(jnp.int32, (CHUNK, CHUNK), 0)
        ci = lax.broadcasted_iota(jnp.int32, (CHUNK, CHUNK), 1)
        ones = jnp.ones((CHUNK, LANES), F32)
        heads = range(GDN_H)
        sls = [slice(h * GDN_D, (h + 1) * GDN_D) for h in heads]
        rows = [slice(c * CHUNK, (c + 1) * CHUNK) for c in range(G)]
        pairs = [(c, h) for c in range(G) for h in heads]
        P = lambda f: {p: f(*p) for p in pairs}
        gs = [gs_ref[rows[c], :] for c in range(G)]
        gates = [_gdn_gates(gs[c]) for c in range(G)]
        tril, strict = gates[0][2], gates[0][3]
        q = P(lambda c, h: q_ref[rows[c], sls[h]])
        k = P(lambda c, h: k_ref[rows[c], sls[h]])
        v = P(lambda c, h: v_ref[rows[c], sls[h]])
        dov = P(lambda c, h: do_ref[rows[c], sls[h]])
        s0 = P(lambda c, h: sin_ref[h, c])
        t = P(lambda c, h: t_ref[h, c])
        beta = P(lambda c, h: gs[c][:, GDN_H + h:GDN_H + h + 1])
        gg = P(lambda c, h: _gdn_decay(gates[c][0], gates[c][1], tril, h))
        g = {p: x[0] for p, x in gg.items()}
        gam = {p: x[1] for p, x in gg.items()}
        eg = P(lambda c, h: jnp.exp(g[c, h]))
        egl = P(lambda c, h: jnp.exp(g[c, h][CHUNK - 1:CHUNK, :]))
        e = P(lambda c, h: jnp.exp(g[c, h][CHUNK - 1:CHUNK, :] - g[c, h]))
        kb = P(lambda c, h: k[c, h] * beta[c, h])
        kbg = P(lambda c, h: kb[c, h] * eg[c, h])
        vb = P(lambda c, h: v[c, h] * beta[c, h])
        qg = P(lambda c, h: q[c, h] * eg[c, h])
        kd = P(lambda c, h: k[c, h] * e[c, h])
        m = P(lambda c, h: jnp.where(strict, _dnt(kb[c, h], k[c, h]) * gam[c, h], 0.0))
        u = P(lambda c, h: _d(t[c, h], vb[c, h]))
        w = P(lambda c, h: _d(t[c, h], kbg[c, h]))
        p = P(lambda c, h: _dnt(q[c, h], k[c, h]) * gam[c, h])
        dqg = P(lambda c, h: _dnt(dov[c, h], s0[c, h]))
        qgdo = P(lambda c, h: _dtn(qg[c, h], dov[c, h]))
        ptdo = P(lambda c, h: _dtn(p[c, h], dov[c, h]))
        vnew = P(lambda c, h: u[c, h] - _d(w[c, h], s0[c, h]))
        dp = P(lambda c, h: jnp.where(tril, _dnt(dov[c, h], vnew[c, h]), 0.0))
        cur = [dS[h] for h in heads]
        dvnew, dkd, sds = {}, {}, {}
        for c in reversed(range(G)):
            for h in heads:
                dvnew[c, h] = ptdo[c, h] + _d(kd[c, h], cur[h])
                dkd[c, h] = _dnt(vnew[c, h], cur[h])
                sds[c, h] = _allsum(s0[c, h] * cur[h])
            cur = [qgdo[c, h] + egl[c, h] * cur[h] - _dtn(w[c, h], dvnew[c, h]) for h in heads]
        for h in heads:
            dS[h] = cur[h]
        dw = P(lambda c, h: -_dnt(dvnew[c, h], s0[c, h]))
        dvb = P(lambda c, h: _dtn(t[c, h], dvnew[c, h]))
        dkbg = P(lambda c, h: _dtn(t[c, h], dw[c, h]))
        dt = P(lambda c, h: _dnt(dvnew[c, h], vb[c, h]) + _dnt(dw[c, h], kbg[c, h]))
        x1 = P(lambda c, h: _d3g(t[c, h], dt[c, h], _TN))
        dm = P(lambda c, h: jnp.where(strict, -_d3g(x1[c, h], t[c, h], _NT), 0.0))
        dkk = P(lambda c, h: dm[c, h] * gam[c, h])
        dqk = P(lambda c, h: dp[c, h] * gam[c, h])
        dkb = P(lambda c, h: _d(dkk[c, h], k[c, h]) + eg[c, h] * dkbg[c, h])
        em = P(lambda c, h: dm[c, h] * m[c, h] + dp[c, h] * p[c, h])
        colsum = P(lambda c, h: _d3g(em[c, h], ones, _TN)[:, 0:1])
        for c, h in pairs:
            dk_ref[rows[c], sls[h]] = (_dtn(dkk[c, h], kb[c, h]) + _dtn(dqk[c, h], q[c, h]) + dkd[c, h] * e[c, h]
                                       + beta[c, h] * dkb[c, h])
            dq_ref[rows[c], sls[h]] = _d(dqk[c, h], k[c, h]) + dqg[c, h] * eg[c, h]
            dv_ref[rows[c], sls[h]] = beta[c, h] * dvb[c, h]
        for c in range(G):
            dg_all = jnp.zeros((CHUNK, LANES), F32)
            dbeta_all = jnp.zeros((CHUNK, LANES), F32)
            for h in heads:
                dbeta = _rowsum(k[c, h] * dkb[c, h]) + _rowsum(v[c, h] * dvb[c, h])
                z = _rowsum(kd[c, h] * dkd[c, h])
                dg = (_rowsum(em[c, h]) - colsum[c, h] + _rowsum(qg[c, h] * dqg[c, h]) + _rowsum(kbg[c, h] * dkbg[c, h])
                      - z)
                extra = _allsum(z) + egl[c, h] * sds[c, h]
                dg = dg + jnp.where(rcol == CHUNK - 1, extra, 0.0)
                dg_all = dg_all + jnp.where(lane == h, dg, 0.0)
                dbeta_all = dbeta_all + jnp.where(lane == GDN_H + h, dbeta, 0.0)
            dgs_ref[rows[c], :] = _dx((ci >= ri).astype(F32), dg_all) + dbeta_all

    return pl.pallas_call(
        body, grid=(nb,),
        in_specs=[col(0), col(1), col(2), gate, col(0), st(GDN_D, GDN_D), st(CHUNK, CHUNK)],
        out_specs=[col(0), col(0), col(0), gate],
        out_shape=[jax.ShapeDtypeStruct((Lp, D_MODEL), F32)] * 3 + [jax.ShapeDtypeStruct((Lp, LANES), F32)],
        scratch_shapes=[pltpu.VMEM((GDN_H, GDN_D, GDN_D), F32)],
        name="gdn_chunk_bwd")(qkv, qkv, qkv, gsm, do, s_in, t_in)


def _rot(x, c, s):
    half = RET_D // 2
    x1 = x[:, :half]
    x2 = x[:, half:]
    return jnp.concatenate([x1 * c - x2 * s, x2 * c + x1 * s], axis=1)


def _rot_bwd(d, c, s):
    half = RET_D // 2
    d1 = d[:, :half]
    d2 = d[:, half:]
    return jnp.concatenate([d1 * c + d2 * s, d2 * c - d1 * s], axis=1)


def _ret_tables():
    hh = jnp.arange(RET_H, dtype=F32)
    lg = jnp.log(1.0 - 2.0 ** (-5.0 - hh))
    idx = jnp.arange(CHUNK, dtype=F32)
    tril = jnp.asarray(np.tril(np.ones((CHUNK, CHUNK), dtype=bool)))
    dmask = jnp.where(tril, jnp.exp((idx[:, None] - idx[None, :]) * lg[:, None, None]), 0.0)
    qdec = jnp.exp((idx[None, :] + 1.0) * lg[:, None])
    kdec = jnp.exp((CHUNK - 1.0 - idx[None, :]) * lg[:, None])
    gch = jnp.exp(CHUNK * lg)
    qdec = jnp.broadcast_to(qdec[:, :, None], (RET_H, CHUNK, RET_D))
    kdec = jnp.broadcast_to(kdec[:, :, None], (RET_H, CHUNK, RET_D))
    gch = jnp.broadcast_to(gch[:, None, None], (RET_H, 8, LANES))
    return dmask, qdec, kdec, gch


def _ret_specs(N, rev):
    G = _group(N)
    nb = N // G
    cn = (lambda n: nb - 1 - n) if rev else (lambda n: n)
    col = lambda j: pl.BlockSpec((G * CHUNK, D_MODEL), lambda n: (cn(n), j))
    tab = lambda a, b: pl.BlockSpec((RET_H, a, b), lambda n: (0, 0, 0))
    rope = pl.BlockSpec((G * CHUNK, LANES), lambda n: (cn(n), 0))
    st = pl.BlockSpec((RET_H, G, RET_D, RET_D), lambda n: (0, cn(n), 0, 0))
    return G, nb, col, tab, rope, st


def _ret_chunk_fwd(proj_m, cos, sin, tables):
    Lp = proj_m.shape[0]
    N = Lp // CHUNK
    dmask, qdec, kdec, gch = tables
    G, nb, col, tab, rope, st = _ret_specs(N, False)

    def body(q_ref, k_ref, v_ref, c_ref, s_ref, dm_ref, qd_ref, kd_ref, g_ref, o_ref, sin_ref, S):
        n = pl.program_id(0)

        @pl.when(n == 0)
        def _():
            S[...] = jnp.zeros_like(S)

        heads = range(RET_H)
        sls = [slice(h * RET_D, (h + 1) * RET_D) for h in heads]
        rows = [slice(c * CHUNK, (c + 1) * CHUNK) for c in range(G)]
        pairs = [(c, h) for c in range(G) for h in heads]
        P = lambda f: {p: f(*p) for p in pairs}
        qr = P(lambda c, h: _rot(q_ref[rows[c], sls[h]], c_ref[rows[c], :], s_ref[rows[c], :]))
        ks = P(lambda c, h: _rot(k_ref[rows[c], sls[h]], c_ref[rows[c], :], s_ref[rows[c], :]) * (RET_D ** -0.5))
        v = P(lambda c, h: v_ref[rows[c], sls[h]])
        a = P(lambda c, h: _dnt(qr[c, h], ks[c, h]) * dm_ref[h])
        av = P(lambda c, h: _d(a[c, h], v[c, h]))
        kv = P(lambda c, h: _dtn(ks[c, h] * kd_ref[h], v[c, h]))
        qd = P(lambda c, h: qr[c, h] * qd_ref[h])
        cur = [S[h] for h in heads]
        for c in range(G):
            for h in heads:
                o_ref[rows[c], sls[h]] = av[c, h] + _d(qd[c, h], cur[h])
                sin_ref[h, c] = cur[h].astype(BF16)
            cur = [cur[h] * g_ref[h, 0:1, 0:1] + kv[c, h] for h in heads]
        for h in heads:
            S[h] = cur[h]

    return pl.pallas_call(
        body, grid=(nb,),
        in_specs=[col(3), col(4), col(5), rope, rope,
                  tab(CHUNK, CHUNK), tab(CHUNK, RET_D), tab(CHUNK, RET_D), tab(8, LANES)],
        out_specs=[col(0), st],
        out_shape=[jax.ShapeDtypeStruct((Lp, D_MODEL), F32), jax.ShapeDtypeStruct((RET_H, N, RET_D, RET_D), BF16)],
        scratch_shapes=[pltpu.VMEM((RET_H, RET_D, RET_D), F32)],
        name="ret_chunk_fwd")(proj_m, proj_m, proj_m, cos, sin, dmask, qdec, kdec, gch)


def _ret_chunk_bwd(proj_m, cos, sin, tables, do, s_in):
    Lp = proj_m.shape[0]
    N = Lp // CHUNK
    dmask, qdec, kdec, gch = tables
    G, nb, col, tab, rope, st = _ret_specs(N, True)

    def body(q_ref, k_ref, v_ref, c_ref, s_ref, dm_ref, qd_ref, kd_ref, g_ref, do_ref, sin_ref,
             d_ref, dS):
        n = pl.program_id(0)

        @pl.when(n == 0)
        def _():
            dS[...] = jnp.zeros_like(dS)

        kscale = RET_D ** -0.5
        heads = range(RET_H)
        sls = [slice(h * RET_D, (h + 1) * RET_D) for h in heads]
        rows = [slice(c * CHUNK, (c + 1) * CHUNK) for c in range(G)]
        pairs = [(c, h) for c in range(G) for h in heads]
        P = lambda f: {p: f(*p) for p in pairs}
        cs = [(c_ref[rows[c], :], s_ref[rows[c], :]) for c in range(G)]
        osl = lambda part, h: slice(part * D_MODEL + h * RET_D, part * D_MODEL + (h + 1) * RET_D)
        qr = P(lambda c, h: _rot(q_ref[rows[c], sls[h]], *cs[c]))
        ks = P(lambda c, h: _rot(k_ref[rows[c], sls[h]], *cs[c]) * kscale)
        v = P(lambda c, h: v_ref[rows[c], sls[h]])
        dov = P(lambda c, h: do_ref[rows[c], sls[h]])
        ad = P(lambda c, h: _dnt(qr[c, h], ks[c, h]) * dm_ref[h])
        da = P(lambda c, h: _dnt(dov[c, h], v[c, h]) * dm_ref[h])
        dos = P(lambda c, h: _dnt(dov[c, h], sin_ref[h, c]) * qd_ref[h])
        qdo = P(lambda c, h: _dtn(qr[c, h] * qd_ref[h], dov[c, h]))
        adv = P(lambda c, h: _dtn(ad[c, h], dov[c, h]))
        dqr = P(lambda c, h: _d(da[c, h], ks[c, h]) + dos[c, h])
        daq = P(lambda c, h: _dtn(da[c, h], qr[c, h]))
        kk = P(lambda c, h: ks[c, h] * kd_ref[h])
        cur = [dS[h] for h in heads]
        for c in reversed(range(G)):
            for h in heads:
                d_ref[rows[c], osl(2, h)] = (adv[c, h] + _d(kk[c, h], cur[h])).astype(BF16)
                d_ref[rows[c], osl(0, h)] = _rot_bwd(dqr[c, h], *cs[c]).astype(BF16)
                dks = daq[c, h] + _dnt(v[c, h], cur[h]) * kd_ref[h]
                d_ref[rows[c], osl(1, h)] = _rot_bwd(dks * kscale, *cs[c]).astype(BF16)
            cur = [cur[h] * g_ref[h, 0:1, 0:1] + qdo[c, h] for h in heads]
        for h in heads:
            dS[h] = cur[h]

    return pl.pallas_call(
        body, grid=(nb,),
        in_specs=[col(3), col(4), col(5), rope, rope,
                  tab(CHUNK, CHUNK), tab(CHUNK, RET_D), tab(CHUNK, RET_D), tab(8, LANES), col(0), st],
        out_specs=pl.BlockSpec((G * CHUNK, 3 * D_MODEL), lambda n: (nb - 1 - n, 0)),
        out_shape=jax.ShapeDtypeStruct((Lp, 3 * D_MODEL), BF16),
        scratch_shapes=[pltpu.VMEM((RET_H, RET_D, RET_D), F32)],
        name="ret_chunk_bwd")(proj_m, proj_m, proj_m, cos, sin, dmask, qdec, kdec, gch, do, s_in)


def _merge_specs(tr):
    col = lambda j: pl.BlockSpec((tr, D_MODEL), lambda i: (i, j))
    return col


def _merge_fwd(o_a, o_b, proj_m, gnorm):
    Lp = o_a.shape[0]
    tr = _tile(Lp, 192, 16)

    def body(oa_ref, ob_ref, gz_ref, rg_ref, ga_ref, gb_ref, gn_ref, y_ref):
        gn = gn_ref[...]
        oa = oa_ref[...]
        ob = ob_ref[...]
        gz = gz_ref[...]
        ya = []
        for j in range(GDN_H):
            seg = oa[:, j * GDN_D:(j + 1) * GDN_D]
            r = lax.rsqrt(jnp.mean(seg * seg, axis=-1, keepdims=True) + EPS)
            ya.append(seg * r * gn)
        ya = jnp.concatenate(ya, axis=1) * (gz * _sig(gz))
        yb = []
        for j in range(RET_H):
            seg = ob[:, j * RET_D:(j + 1) * RET_D]
            r = lax.rsqrt(jnp.mean(seg * seg, axis=-1, keepdims=True) + EPS)
            yb.append(seg * r)
        rg = rg_ref[...]
        yb = jnp.concatenate(yb, axis=1) * (rg * _sig(rg))
        y_ref[...] = (_sig(ga_ref[...]) * ya + _sig(gb_ref[...]) * yb).astype(BF16)

    col = _merge_specs(tr)
    return pl.pallas_call(
        body, grid=(Lp // tr,),
        in_specs=[col(0), col(0), col(6), col(7), col(8), col(9), pl.BlockSpec((1, GDN_D), lambda i: (0, 0))],
        out_specs=col(0), out_shape=jax.ShapeDtypeStruct((Lp, D_MODEL), BF16),
        name="merge_fwd")(o_a, o_b, proj_m, proj_m, proj_m, proj_m, gnorm)


def _merge_bwd(dh1b, w_out, o_a, o_b, proj_m, gnorm):
    Lp = o_a.shape[0]
    tr = _tile(Lp, 192, 16)

    def body(d_ref, wo_ref, oa_ref, ob_ref, gz_ref, rg_ref, ga_ref, gb_ref, gn_ref, dc_ref, doa_ref, dob_ref, dgn_ref):
        i = pl.program_id(0)
        gn = gn_ref[...]
        dyv = lax.dot_general(d_ref[...], wo_ref[...], _NT, preferred_element_type=F32)
        oa = oa_ref[...]
        ob = ob_ref[...]
        gz = gz_ref[...]
        rg = rg_ref[...]
        sa = _sig(ga_ref[...])
        sb = _sig(gb_ref[...])
        dya = dyv * sa
        dyb = dyv * sb
        sgz = _sig(gz)
        szz = gz * sgz
        dgn = jnp.zeros((1, GDN_D), F32)
        ya = []
        dgz = []
        for j in range(GDN_H):
            sl = slice(j * GDN_D, (j + 1) * GDN_D)
            seg = oa[:, sl]
            r = lax.rsqrt(jnp.mean(seg * seg, axis=-1, keepdims=True) + EPS)
            xh = seg * r
            oan = xh * gn
            ya.append(oan * szz[:, sl])
            dgz.append(dya[:, sl] * oan * (sgz[:, sl] * (1.0 + gz[:, sl] * (1.0 - sgz[:, sl]))))
            doan = dya[:, sl] * szz[:, sl]
            dgn = dgn + jnp.sum(doan * xh, axis=0, keepdims=True)
            dxh = doan * gn
            doa_ref[:, sl] = r * (dxh - xh * jnp.mean(dxh * xh, axis=-1, keepdims=True))
        ya = jnp.concatenate(ya, axis=1)
        srg = _sig(rg)
        srr = rg * srg
        yb = []
        drg = []
        for j in range(RET_H):
            sl = slice(j * RET_D, (j + 1) * RET_D)
            seg = ob[:, sl]
            r = lax.rsqrt(jnp.mean(seg * seg, axis=-1, keepdims=True) + EPS)
            xh = seg * r
            yb.append(xh * srr[:, sl])
            drg.append(dyb[:, sl] * xh * (srg[:, sl] * (1.0 + rg[:, sl] * (1.0 - srg[:, sl]))))
            dxh = dyb[:, sl] * srr[:, sl]
            dob_ref[:, sl] = r * (dxh - xh * jnp.mean(dxh * xh, axis=-1, keepdims=True))
        yb = jnp.concatenate(yb, axis=1)
        dc_ref[:, 0:D_MODEL] = jnp.concatenate(dgz, axis=1).astype(BF16)
        dc_ref[:, D_MODEL:2 * D_MODEL] = jnp.concatenate(drg, axis=1).astype(BF16)
        dc_ref[:, 2 * D_MODEL:3 * D_MODEL] = (dyv * ya * sa * (1.0 - sa)).astype(BF16)
        dc_ref[:, 3 * D_MODEL:] = (dyv * yb * sb * (1.0 - sb)).astype(BF16)

        @pl.when(i == 0)
        def _():
            dgn_ref[...] = dgn

        @pl.when(i > 0)
        def _():
            dgn_ref[...] += dgn

    col = _merge_specs(tr)
    return pl.pallas_call(
        body, grid=(Lp // tr,),
        in_specs=[col(0), pl.BlockSpec((D_MODEL, D_MODEL), lambda i: (0, 0), pipeline_mode=pl.Buffered(1)),
                  col(0), col(0), col(6), col(7), col(8), col(9), pl.BlockSpec((1, GDN_D), lambda i: (0, 0))],
        out_specs=[pl.BlockSpec((tr, 4 * D_MODEL), lambda i: (i, 0)), col(0), col(0),
                   pl.BlockSpec((1, GDN_D), lambda i: (0, 0))],
        out_shape=[jax.ShapeDtypeStruct((Lp, 4 * D_MODEL), BF16), jax.ShapeDtypeStruct((Lp, D_MODEL), F32),
                   jax.ShapeDtypeStruct((Lp, D_MODEL), F32), jax.ShapeDtypeStruct((1, GDN_D), F32)],
        name="merge_bwd")(dh1b, w_out, o_a, o_b, proj_m, proj_m, proj_m, proj_m, gnorm)


def _ffn_act(up, conv_w, conv_b):
    Lp = up.shape[0]
    tr = _tile(Lp, 192, 16)
    W2 = 2 * D_FF

    def body(main_ref, prev_ref, w_ref, b_ref, act_ref, u_ref):
        i = pl.program_id(0)
        prev = jnp.where(i > 0, prev_ref[...], 0.0)
        ext = jnp.concatenate([prev, main_ref[...]], axis=0)
        u = _taps(_shifted(ext, range(8 - (FFN_CONV - 1), 9)), w_ref[...], tr, b_ref[...])
        a = u[:, :D_FF]
        act_ref[...] = (a * _sig(a) * u[:, D_FF:]).astype(BF16)
        u_ref[...] = u.astype(BF16)

    return pl.pallas_call(
        body, grid=(Lp // tr,),
        in_specs=[pl.BlockSpec((tr, W2), lambda i: (i, 0)), _halo_prev(tr, W2),
                  pl.BlockSpec((FFN_CONV, W2), lambda i: (0, 0)), pl.BlockSpec((1, W2), lambda i: (0, 0))],
        out_specs=[pl.BlockSpec((tr, D_FF), lambda i: (i, 0)), pl.BlockSpec((tr, W2), lambda i: (i, 0))],
        out_shape=[jax.ShapeDtypeStruct((Lp, D_FF), BF16), jax.ShapeDtypeStruct((Lp, W2), BF16)],
        name="ffn_act")(up, up, conv_w, conv_b)


def _ffn_act_bwd(up, u, dact, conv_w):
    Lp = up.shape[0]
    tr = _tile(Lp, 192, 16)
    W2 = 2 * D_FF
    te = tr + 8

    def body(up_ref, u_ref, un_ref, da_ref, dan_ref, w_ref, dup_ref, acc_ref):
        i = pl.program_id(0)
        w = w_ref[...]
        ue = jnp.concatenate([u_ref[...].astype(F32), un_ref[...].astype(F32)[0:8]], axis=0)
        a = ue[:, :D_FF]
        b = ue[:, D_FF:]
        rowe = i * tr + lax.broadcasted_iota(jnp.int32, (te, 1), 0)
        dae = jnp.where(rowe < Lp, jnp.concatenate([da_ref[...], dan_ref[...]], axis=0), 0.0)
        sg = _sig(a)
        du = jnp.concatenate([dae * b * (sg * (1.0 + a * (1.0 - sg))), dae * (a * sg)], axis=1)
        dus = _shifted(du, range(FFN_CONV - 1, -1, -1))
        dup_ref[...] = _taps(dus, w, tr).astype(BF16)
        upm = up_ref[...]
        rows = [jnp.sum(dus[kk][0:tr, :] * upm, axis=0, keepdims=True) for kk in range(FFN_CONV)]
        rows.append(jnp.sum(du[0:tr, :], axis=0, keepdims=True))
        part = jnp.concatenate(rows + [jnp.zeros((8 - len(rows), W2), F32)], axis=0)

        @pl.when(i == 0)
        def _():
            acc_ref[...] = part

        @pl.when(i > 0)
        def _():
            acc_ref[...] += part

    return pl.pallas_call(
        body, grid=(Lp // tr,),
        in_specs=[pl.BlockSpec((tr, W2), lambda i: (i, 0)), pl.BlockSpec((tr, W2), lambda i: (i, 0)),
                  _halo_next(tr, W2, Lp, rows=16), pl.BlockSpec((tr, D_FF), lambda i: (i, 0)), _halo_next(tr, D_FF, Lp),
                  pl.BlockSpec((FFN_CONV, W2), lambda i: (0, 0))],
        out_specs=[pl.BlockSpec((tr, W2), lambda i: (i, 0)), pl.BlockSpec((8, W2), lambda i: (0, 0))],
        out_shape=[jax.ShapeDtypeStruct((Lp, W2), BF16), jax.ShapeDtypeStruct((8, W2), F32)],
        name="ffn_act_bwd")(up, u, u, dact, dact, conv_w)


def _local_step(hpad, tgt, pad, wt, first_weights=None, late_weights=None, on_ffn_out_grads=None,
                on_w_in_grads=None):
    Lp = hpad.shape[0]
    first = pad + N_META
    pos = jnp.arange(Lp, dtype=F32) - float(pad)
    half = RET_D // 2
    inv = 1.0 / (ROPE_BASE ** (jnp.arange(half, dtype=F32) / half))
    ang = pos[:, None] * inv[None, :]
    cos, sin = jnp.cos(ang), jnp.sin(ang)
    tables = _ret_tables()
    gparams = jnp.zeros((8, LANES), F32).at[0, :GDN_H].set(wt["a_log"]).at[1, :GDN_H].set(wt["dt_bias"])

    hn1 = _rms_fwd(hpad, wt["norm1"], "rms1_fwd")
    if first_weights is not None:
        wt = {**wt, **first_weights(hn1)}
    proj_m = _mm_nn(hn1, wt["w_main_t"], bt=True, name="proj_main")
    proj_s = _mm_nn(hn1, wt["w_small_t"], bt=True, name="proj_small")
    qkv, gsm, conv_out = _gdn_pre(proj_m, proj_s, wt["gdn_conv_w"], gparams, pad)
    o_a, s_a, t_a = _gdn_chunk_fwd(qkv, gsm)
    o_b, s_b = _ret_chunk_fwd(proj_m, cos, sin, tables)
    y = _merge_fwd(o_a, o_b, proj_m, wt["gdn_norm"])
    if late_weights is not None:
        wt = {**wt, **late_weights(y)}
    h1, hn2 = _mm_rms_fwd(_Producer(y, wt["w_out"], hpad), wt["norm2"], "out_proj_rms2")
    up = _mm_nn(hn2, wt["w_up_t"], bt=True, name="ffn_up")
    act, u_ffn = _ffn_act(up, wt["ffn_conv_w"], wt["ffn_conv_b"])
    lossvec, dh2, dh2b, d_norm_f = _final(_Producer(act, wt["w_down"], h1), wt["norm_f"], tgt, first)

    d_w_down = _mm_tn(act, dh2b, name="dw_down")
    dact = _mm_nt(dh2b, wt["w_down"], name="d_act")
    dup, ffn_rows = _ffn_act_bwd(up, u_ffn, dact, wt["ffn_conv_w"])
    d_w_up_t = _mm_tn(dup, hn2, name="dw_up")
    dh1, dh1b, d_norm2 = _rms_bwd(h1, wt["norm2"], _Producer(dup, wt["w_up_t"]), dh2, pad, "d_hn2_rms2_bwd")

    d_w_out = _mm_tn(y, dh1b, name="dw_out")
    gnorm = wt["gdn_norm"]
    if on_ffn_out_grads is not None:
        gnorm = gnorm + on_ffn_out_grads(d_w_down, d_w_up_t, d_w_out)[0:1, :]
    d_c, do_a, do_b, d_gnorm = _merge_bwd(dh1b, wt["w_out"], o_a, o_b, proj_m, gnorm)
    d_r = _ret_chunk_bwd(proj_m, cos, sin, tables, do_b, s_b)
    dq, dk, dv, dgs = _gdn_chunk_bwd(qkv, gsm, do_a, s_a, t_a)
    d_a, d_s, conv_rows, gp_rows = _gdn_pre_bwd(proj_m, conv_out, proj_s, wt["gdn_conv_w"], gparams, dq, dk, dv, dgs,
                                                pad)

    wmt = wt["w_main_t"]
    segs = [(d_a, 0, 3 * D_MODEL), (d_r, 3 * D_MODEL, 3 * D_MODEL), (d_c, 6 * D_MODEL, 4 * D_MODEL)]
    pa, pr, pc = [_mm_tn(d, hn1, BF16, name="dw_in_%d" % i) for i, (d, _, _) in enumerate(segs)]
    ps = _mm_tn(d_s, hn1, BF16, name="dw_in_small")
    d_w_in_t = jnp.concatenate([pa, pc[:D_MODEL], ps[:2 * GDN_H], pr, pc[D_MODEL:]], axis=0)
    w_small_t = wt["w_small_t"]
    if on_w_in_grads is not None:
        w_small_t = w_small_t + on_w_in_grads(d_w_in_t)[0:1, 0:1].astype(w_small_t.dtype)
    dhn1 = _mm_nn(d_s, w_small_t, name="d_hn1_small")
    for i, (d, off, width) in enumerate(segs[:-1]):
        dhn1 = _mm_nn(d, wmt[off:off + width], res=dhn1, name="d_hn1_%d" % i)
    d, off, width = segs[-1]
    dh0, _, d_norm1 = _rms_bwd(hpad, wt["norm1"], _Producer(d, wmt[off:off + width], dhn1), dh1, pad,
                               "d_hn1_rms1_bwd")

    grads = {
        "norm1": d_norm1, "w_in_t": d_w_in_t, "gdn_conv_w": conv_rows[:GDN_CONV],
        "a_log": gp_rows[0, :GDN_H], "dt_bias": gp_rows[1, :GDN_H], "gdn_norm": d_gnorm, "w_out": d_w_out,
        "norm2": d_norm2, "w_up_t": d_w_up_t, "ffn_conv_w": ffn_rows[:FFN_CONV],
        "ffn_conv_b": ffn_rows[FFN_CONV:FFN_CONV + 1], "w_down": d_w_down, "norm_f": d_norm_f,
    }
    return lossvec, dh0, grads


def _peer(k):
    ix, iy, ic = lax.axis_index("x"), lax.axis_index("y"), lax.axis_index("c")
    px = 1 - ix if (k >> 2) & 1 else ix
    py = 1 - iy if (k >> 1) & 1 else iy
    pc = 1 - ic if k & 1 else ic
    return (px, py, pc), 4 * px + 2 * py + pc


def _comm_call(body, n, out_shapes, name, args):
    hbm = pl.BlockSpec(memory_space=pl.ANY)
    return pl.pallas_call(
        body, out_shape=out_shapes, in_specs=[hbm] * n, out_specs=[hbm] * n,
        scratch_shapes=[pltpu.SemaphoreType.DMA((n, N_DEV - 1)), pltpu.SemaphoreType.DMA((n, N_DEV - 1)),
                        pltpu.SemaphoreType.DMA((n,))],
        name=name)(*args)


def _all_gather(xs, name):
    n = len(xs)

    def body(*refs):
        x_refs, out_refs = refs[:n], refs[n:2 * n]
        send_sems, recv_sems, local_sems = refs[2 * n:]
        _, me = _peer(0)
        pending = []
        for i in range(n):
            local = pltpu.make_async_copy(x_refs[i], out_refs[i].at[me], local_sems.at[i])
            local.start()
            pending.append(local)
        sends = []
        for i in range(n):
            for k in range(1, N_DEV):
                dev, _ = _peer(k)
                cp = pltpu.make_async_remote_copy(
                    src_ref=x_refs[i], dst_ref=out_refs[i].at[me], send_sem=send_sems.at[i, k - 1],
                    recv_sem=recv_sems.at[i, k - 1], device_id=dev, device_id_type=MESH_T)
                cp.start()
                sends.append(cp)
        for i in range(n):
            for k in range(1, N_DEV):
                dev, idx = _peer(k)
                pltpu.make_async_remote_copy(
                    src_ref=x_refs[i], dst_ref=out_refs[i].at[idx], send_sem=send_sems.at[i, k - 1],
                    recv_sem=recv_sems.at[i, k - 1], device_id=dev, device_id_type=MESH_T).wait_recv()
        for cp in sends:
            cp.wait_send()
        for local in pending:
            local.wait()

    out_shapes = [jax.ShapeDtypeStruct((N_DEV,) + a.shape, a.dtype) for a in xs]
    return _comm_call(body, n, out_shapes, name, xs)


def _all_to_all(gs, name):
    n = len(gs)

    def body(*refs):
        g_refs, out_refs = refs[:n], refs[n:2 * n]
        send_sems, recv_sems, local_sems = refs[2 * n:]
        _, me = _peer(0)
        pending = []
        for i in range(n):
            local = pltpu.make_async_copy(g_refs[i].at[me], out_refs[i].at[0], local_sems.at[i])
            local.start()
            pending.append(local)
        sends = []
        for i in range(n):
            for k in range(1, N_DEV):
                dev, idx = _peer(k)
                cp = pltpu.make_async_remote_copy(
                    src_ref=g_refs[i].at[idx], dst_ref=out_refs[i].at[k], send_sem=send_sems.at[i, k - 1],
                    recv_sem=recv_sems.at[i, k - 1], device_id=dev, device_id_type=MESH_T)
                cp.start()
                sends.append(cp)
        for cp in sends:
            cp.wait_recv()
        for cp in sends:
            cp.wait_send()
        for local in pending:
            local.wait()

    out_shapes = [jax.ShapeDtypeStruct(g.shape, g.dtype) for g in gs]
    return _comm_call(body, n, out_shapes, name, gs)


_SPLIT_RELATIONS = {"gather": tuple(range(1, N_DEV)), "a2a": tuple(range(1, N_DEV)), "chip": (1, 2, 4, 6),
                    "forward": (2, 4, 6)}


def _split_copies(kind, src_refs, land_refs, send_sems, recv_sems, local_sems, with_recv):
    n = len(land_refs)
    rels = _SPLIT_RELATIONS[kind]
    _, me = _peer(0)
    locals_, remotes = [], []
    for i in range(n):
        if kind in ("gather", "chip"):
            locals_.append(pltpu.make_async_copy(src_refs[i], land_refs[i].at[me], local_sems.at[i]))
        elif kind == "a2a":
            locals_.append(pltpu.make_async_copy(src_refs[i].at[me], land_refs[i].at[0], local_sems.at[i]))
        for jj, k in enumerate(rels):
            dev, idx = _peer(k)
            if kind in ("gather", "chip"):
                src, dst, mine = src_refs[i], land_refs[i].at[me], land_refs[i].at[idx]
            elif kind == "a2a":
                src, dst, mine = src_refs[i].at[idx], land_refs[i].at[k], land_refs[i].at[k]
            else:
                dev, _ = _peer(1)
                _, came = _peer(k + 1)
                src, dst, mine = land_refs[i].at[idx], land_refs[i].at[idx], land_refs[i].at[came]
            j = i * len(rels) + jj
            send = pltpu.make_async_remote_copy(
                src_ref=src, dst_ref=dst, send_sem=send_sems.at[j], recv_sem=recv_sems.at[j],
                device_id=dev, device_id_type=MESH_T)
            recv = pltpu.make_async_remote_copy(
                src_ref=src, dst_ref=mine, send_sem=send_sems.at[j], recv_sem=recv_sems.at[j],
                device_id=dev, device_id_type=MESH_T) if with_recv else None
            remotes.append((send, recv))
    return locals_, remotes


_HBM = pl.BlockSpec(memory_space=pltpu.HBM)
_SEM = pl.BlockSpec(memory_space=pltpu.SEMAPHORE)
_ANY = pl.BlockSpec(memory_space=pl.ANY)


def _split_start(srcs, kind, name, after):
    n = len(srcs)
    if kind == "forward":
        arrays = list(srcs)
    else:
        gathers = kind in ("gather", "chip")
        arrays = list(srcs) + [lax.empty(((N_DEV,) + a.shape) if gathers else a.shape, a.dtype) for a in srcs]
    na = len(arrays)

    def body(*refs):
        src_refs, land_refs = refs[:n], refs[na - n:na]
        send_sems, recv_sems, local_sems = refs[na + 1:na + 4]
        token = refs[-1]
        locals_, remotes = _split_copies(kind, src_refs, land_refs, send_sems, recv_sems, local_sems, False)
        for cp in locals_:
            cp.start()
        for send, _ in remotes:
            send.start()
        token[...] = jnp.zeros_like(token)

    ncp = n * len(_SPLIT_RELATIONS[kind])
    sems = (pltpu.SemaphoreType.DMA((ncp,)), pltpu.SemaphoreType.DMA((ncp,)), pltpu.SemaphoreType.DMA((n,)))
    thru = tuple(pltpu.HBM(a.shape, a.dtype) for a in arrays)
    outs = pl.pallas_call(
        body, name=name,
        out_shape=sems + thru + (jax.ShapeDtypeStruct((8, LANES), F32),),
        in_specs=[_HBM] * na + [_ANY],
        out_specs=[_SEM] * 3 + [_HBM] * na + [pl.BlockSpec(memory_space=pltpu.VMEM)],
        input_output_aliases={i: 3 + i for i in range(na)},
        compiler_params=pltpu.CompilerParams(has_side_effects=pltpu.SideEffectType.DATAFLOW_SIDE_EFFECTING),
    )(*[pltpu.with_memory_space_constraint(a, pltpu.HBM) for a in arrays], after)
    return (kind, n, outs[:3], outs[3:3 + na]), outs[-1]


def _split_wait(handle, name, after):
    kind, n, sems, thru = handle
    na = len(thru)

    def body(*refs):
        src_refs, land_refs = refs[:n], refs[na - n:na]
        send_sems, recv_sems, local_sems = refs[na:na + 3]
        locals_, remotes = _split_copies(kind, src_refs, land_refs, send_sems, recv_sems, local_sems, True)
        for send, recv in remotes:
            send.wait_send()
            recv.wait_recv()
        for cp in locals_:
            cp.wait()

    outs = pl.pallas_call(
        body, name=name, out_shape=tuple(pltpu.HBM(a.shape, a.dtype) for a in thru),
        in_specs=[_HBM] * na + [_SEM] * 3 + [_ANY], out_specs=[_HBM] * na,
        input_output_aliases={i: i for i in range(na)},
        compiler_params=pltpu.CompilerParams(has_side_effects=pltpu.SideEffectType.DATAFLOW_SIDE_EFFECTING),
    )(*thru, *sems, after)
    return list(outs[na - n:])


def _adamw(gslabs, w, m, v, name):
    R, Cw = w.shape
    if R % 8 == 0:
        tr, tc = _tile(R, 64 if Cw > 1024 else 128, 8), Cw
    else:
        tr, tc = R, LANES
    c1 = 1.0 - ADAM_B1 ** ADAM_STEP
    c2 = 1.0 - ADAM_B2 ** ADAM_STEP

    def body(g_ref, w_ref, m_ref, v_ref, go_ref, d_ref, mo_ref, vo_ref):
        g = g_ref[0].astype(F32)
        for k in range(1, N_DEV):
            g = g + g_ref[k].astype(F32)
        mn = ADAM_B1 * m_ref[...] + (1.0 - ADAM_B1) * g
        vn = ADAM_B2 * v_ref[...] + (1.0 - ADAM_B2) * (g * g)
        m_hat = mn / c1
        v_hat = vn / c2
        go_ref[...] = g
        d_ref[...] = -ADAM_LR * (m_hat / (jnp.sqrt(v_hat) + ADAM_EPS) + ADAM_WD * w_ref[...])
        mo_ref[...] = mn
        vo_ref[...] = vn

    blk = pl.BlockSpec((tr, tc), lambda i, j: (i, j))
    return pl.pallas_call(
        body, grid=(R // tr, Cw // tc),
        in_specs=[pl.BlockSpec((N_DEV, tr, tc), lambda i, j: (0, i, j)), blk, blk, blk],
        out_specs=[blk] * 4, out_shape=[jax.ShapeDtypeStruct((R, Cw), F32)] * 4, name=name)(gslabs, w, m, v)


def _pack(arrs, row_mult, dtype=F32):
    parts = []
    total = 0
    for a in arrs:
        f = a.reshape(-1).astype(dtype)
        n = -(-f.shape[0] // 1024) * 1024
        parts.append(jnp.pad(f, (0, n - f.shape[0])))
        total += n
    rows = total // LANES
    rows_p = -(-rows // row_mult) * row_mult
    flat = jnp.concatenate(parts)
    flat = jnp.pad(flat, (0, rows_p * LANES - total))
    return flat.reshape(rows_p, LANES)


def _unpack(packed, shapes):
    lead = packed.shape[:-2]
    flat = packed.reshape(lead + (-1,))
    out = []
    off = 0
    for s in shapes:
        n = int(np.prod(s))
        out.append(flat[..., off:off + n].reshape(lead + tuple(s)))
        off += -(-n // 1024) * 1024
    return out


def _gather_cols(stacked):
    d, r, c = stacked.shape
    return stacked.transpose(1, 0, 2).reshape(r, d * c)


def _scatter_cols(full):
    r, n = full.shape
    return full.reshape(r, N_DEV, n // N_DEV).transpose(1, 0, 2)


def kernel(x, meta, norm1, w_in, gdn_conv_w, gdn_a_log, gdn_dt_bias, gdn_norm, w_out, norm2, w_ffn_up, ffn_conv_w, ffn_conv_b, w_ffn_down, norm_f, loss_target, m_meta, m_norm1, m_w_in, m_gdn_conv_w, m_gdn_a_log, m_gdn_dt_bias, m_gdn_norm, m_w_out, m_norm2, m_w_ffn_up, m_ffn_conv_w, m_ffn_conv_b, m_w_ffn_down, m_norm_f, v_meta, v_norm1, v_w_in, v_gdn_conv_w, v_gdn_a_log, v_gdn_dt_bias, v_gdn_norm, v_w_out, v_norm2, v_w_ffn_up, v_ffn_conv_w, v_ffn_conv_b, v_w_ffn_down, v_norm_f):
    S = x.shape[1]
    L = N_META + S
    pad = (-L) % CHUNK
    Lp = L + pad

    tr_ = lambda a: jnp.swapaxes(a[0], 0, 1)
    big = [tr_(w_in), w_out[0], tr_(w_ffn_up), w_ffn_down[0]]
    small = [meta, gdn_conv_w, ffn_conv_w]
    small_all, = _all_gather([_pack(small, 8)], "gather_small_weights")
    first, first_token = _split_start([big[0].astype(BF16)], "chip", "gather_w_in_start", small_all)
    late, late_token = _split_start([a.astype(BF16) for a in big[1:]], "gather", "gather_late_start", first_token)

    def first_weights(after):
        half = _split_wait(first, "gather_w_in_wait", after)
        second, second_token = _split_start(half, "forward", "gather_w_in_forward_start", after)
        w_in_s, = _split_wait(second, "gather_w_in_forward_wait", second_token)
        w_in_t = w_in_s.reshape(_O_END, D_MODEL)
        w_main_t = jnp.concatenate([w_in_t[_O_GQ:_O_GZ], w_in_t[_O_RQ:_O_RG], w_in_t[_O_GZ:_O_GA],
                                    w_in_t[_O_RG:_O_END]], axis=0)
        return {"w_main_t": w_main_t, "w_small_t": jnp.pad(w_in_t[_O_GA:_O_RQ], ((0, LANES - 2 * GDN_H), (0, 0)))}

    def late_weights(after):
        w_out_s, w_up_s, w_down_s = _split_wait(late, "gather_late_wait", after)
        return {"w_out": w_out_s.reshape(D_MODEL, D_MODEL), "w_up_t": w_up_s.reshape(2 * D_FF, D_MODEL),
                "w_down": w_down_s.reshape(D_FF, D_MODEL)}

    meta_s, gconv_s, fconv_s = _unpack(small_all, [a.shape for a in small])
    wt = {
        "norm1": norm1 + jnp.tile(late_token[0:1, :], (1, D_MODEL // LANES)),
        "gdn_conv_w": _gather_cols(gconv_s[:, 0]), "a_log": gdn_a_log[0], "dt_bias": gdn_dt_bias[0],
        "gdn_norm": gdn_norm, "norm2": norm2, "ffn_conv_w": _gather_cols(fconv_s[:, 0]), "ffn_conv_b": ffn_conv_b,
        "norm_f": norm_f.reshape(1, D_MODEL),
    }
    meta_f = _gather_cols(meta_s)

    pending = {}

    def on_ffn_out_grads(d_w_down, d_w_up_t, d_w_out):
        srcs = [d_w_out.reshape(N_DEV, D_MODEL // N_DEV, D_MODEL), d_w_up_t.reshape(N_DEV, 2 * D_FF // N_DEV, D_MODEL),
                d_w_down.reshape(N_DEV, D_FF // N_DEV, D_MODEL)]
        pending["ffn_out"], token = _split_start(srcs, "a2a", "exchange_ffn_out_start", d_w_out)
        return token

    def on_w_in_grads(d_w_in_t):
        slabs = d_w_in_t.astype(BF16).reshape(N_DEV, _O_END // N_DEV, D_MODEL)
        pending["w_in"], token = _split_start([slabs], "a2a", "exchange_w_in_start", d_w_in_t)
        return token

    hpad = jnp.concatenate([jnp.zeros((pad, D_MODEL), F32), meta_f, x[0]], axis=0)
    tgt = jnp.concatenate([jnp.zeros((pad + N_META, D_MODEL), F32), loss_target[0]], axis=0)
    lossvec, dh0, gr = _local_step(hpad, tgt, pad, wt, first_weights, late_weights, on_ffn_out_grads, on_w_in_grads)

    loss = lax.psum(jnp.sum(lossvec), ("x", "y", "c"))
    grad_x = dh0[pad + N_META:][None]

    big_m = [tr_(m_w_in), m_w_out[0], tr_(m_w_ffn_up), m_w_ffn_down[0]]
    big_v = [tr_(v_w_in), v_w_out[0], tr_(v_w_ffn_up), v_w_ffn_down[0]]
    slabs_ffn_out = _split_wait(pending["ffn_out"], "exchange_ffn_out_wait", dh0)
    big_out = [None] + [_adamw(slabs_ffn_out[i - 1], big[i], big_m[i], big_v[i], "adamw_big_%d" % i)
                        for i in range(1, len(big))]
    g_sm = [_scatter_cols(dh0[pad:pad + N_META]), _scatter_cols(gr["gdn_conv_w"]), _scatter_cols(gr["ffn_conv_w"])]
    g_small = jnp.stack([_pack([g[d] for g in g_sm], 8) for d in range(N_DEV)])
    slabs_small, = _all_to_all([g_small], "exchange_small_gradients")
    small_out = _adamw(slabs_small, _pack(small, 8), _pack([m_meta, m_gdn_conv_w, m_ffn_conv_w], 8),
                       _pack([v_meta, v_gdn_conv_w, v_ffn_conv_w], 8), "adamw_small_sharded")
    small_un = [_unpack(o, [a.shape for a in small]) for o in small_out]
    rep_w = [norm1, gdn_a_log, gdn_dt_bias, gdn_norm, norm2, ffn_conv_b, norm_f]
    rep_m = [m_norm1, m_gdn_a_log, m_gdn_dt_bias, m_gdn_norm, m_norm2, m_ffn_conv_b, m_norm_f]
    rep_v = [v_norm1, v_gdn_a_log, v_gdn_dt_bias, v_gdn_norm, v_norm2, v_ffn_conv_b, v_norm_f]
    rep_g = [gr["norm1"], gr["a_log"], gr["dt_bias"], gr["gdn_norm"], gr["norm2"], gr["ffn_conv_b"], gr["norm_f"]]
    rep_slabs, = _all_gather([_pack(rep_g, 8)], "gather_small_gradients")
    rep_out = _adamw(rep_slabs, _pack(rep_w, 8), _pack(rep_m, 8), _pack(rep_v, 8), "adamw_replicated")
    rep_shapes = [a.shape for a in rep_w]
    rp_g, rp_d, rp_nm, rp_nv = [_unpack(o, rep_shapes) for o in rep_out]

    slabs_w_in, = _split_wait(pending["w_in"], "exchange_w_in_wait", rep_out[0])
    big_out[0] = _adamw(slabs_w_in, big[0], big_m[0], big_v[0], "adamw_big_0")
    back = lambda a: jnp.swapaxes(a, 0, 1)[None]
    sh_g, sh_d, sh_nm, sh_nv = [
        [small_un[j][0], back(big_out[0][j]), small_un[j][1], big_out[1][j][None], back(big_out[2][j]),
         small_un[j][2], big_out[3][j][None]] for j in range(4)]

    def order(sh, rp):
        return [sh[0], rp[0], sh[1], sh[2], rp[1], rp[2], rp[3], sh[3], rp[4], sh[4], sh[5], rp[5], sh[6], rp[6]]

    return (loss, grad_x, *order(sh_g, rp_g), *order(sh_d, rp_d), *order(sh_nm, rp_nm), *order(sh_nv, rp_nv))
```

```python
import functools
import math

import numpy as np
import jax
import jax.numpy as jnp
from jax import lax
from jax.experimental import pallas as pl
from jax.experimental.pallas import tpu as pltpu

F32 = jnp.float32
BF16 = jnp.bfloat16
HI = lax.Precision.HIGHEST

D_MODEL = 1024
N_META = 16
CHUNK = 64
GDN_H = 8
GDN_D = 128
RET_H = 4
RET_D = 256
D_FF = 2816
GDN_CONV = 4
FFN_CONV = 3
ROPE_BASE = 10000.0
EPS = 1e-6
N_DEV = 8
LANES = 128
MAIN_W = 10 * 1024
_O_GQ, _O_GZ, _O_GA, _O_RQ, _O_RG, _O_GATE, _O_END = 0, 3072, 4096, 4112, 7184, 8208, 10256

ADAM_LR = 0.001
ADAM_B1 = 0.9
ADAM_B2 = 0.999
ADAM_EPS = 1e-08
ADAM_WD = 0.01
ADAM_STEP = 10

MESH_T = pl.DeviceIdType.MESH


def _tile(n, target, mult):
    best = None
    for d in range(mult, min(n, target) + 1, mult):
        if n % d == 0:
            best = d
    assert best is not None, (n, target, mult)
    return best


def _sig(x):
    return 1.0 / (1.0 + jnp.exp(-x))


def _d(a, b):
    return jnp.dot(a.astype(BF16), b.astype(BF16), preferred_element_type=F32)


def _dnt(a, b):
    return lax.dot_general(a.astype(BF16), b.astype(BF16), (((1,), (1,)), ((), ())), preferred_element_type=F32)


def _dtn(a, b):
    return lax.dot_general(a.astype(BF16), b.astype(BF16), (((0,), (0,)), ((), ())), preferred_element_type=F32)


def _dx(a, b):
    return jnp.dot(a, b, preferred_element_type=F32, precision=HI)


def _dxnt(a, b):
    return lax.dot_general(a, b, (((1,), (1,)), ((), ())), preferred_element_type=F32, precision=HI)


def _dxtn(a, b):
    return lax.dot_general(a, b, (((0,), (0,)), ((), ())), preferred_element_type=F32, precision=HI)


def _split(a):
    hi = a.astype(BF16)
    return hi, (a - hi.astype(F32)).astype(BF16)


def _d3g(a, b, dims):
    ah, al = _split(a)
    bh, bl = _split(b)
    f = functools.partial(lax.dot_general, dimension_numbers=dims, preferred_element_type=F32)
    if dims == _NN:
        rows = a.shape[0]
        both = f(jnp.concatenate([ah, al], axis=0), bh)
        return both[:rows] + (f(ah, bl) + both[rows:])
    return f(ah, bh) + (f(ah, bl) + f(al, bh))


_NN = (((1,), (0,)), ((), ()))
_NT = (((1,), (1,)), ((), ()))
_TN = (((0,), (0,)), ((), ()))


def _rowsum(x):
    return jnp.sum(x, axis=1, keepdims=True)


def _allsum(x):
    return jnp.sum(jnp.sum(x, axis=1, keepdims=True), axis=0, keepdims=True)


def _mm_nn(a, b, res=None, out_dtype=F32, bt=False, name="mm_nn"):
    M, K = a.shape
    N = b.shape[0] if bt else b.shape[1]
    tm = _tile(M, 704, 16)
    tn = _tile(N, 2816, 128)

    def body(*refs):
        if res is None:
            a_ref, b_ref, o_ref = refs
        else:
            a_ref, b_ref, r_ref, o_ref = refs
        acc = lax.dot_general(a_ref[...], b_ref[...], _NT if bt else _NN, preferred_element_type=F32)
        if res is not None:
            acc = acc + r_ref[...]
        o_ref[...] = acc.astype(out_dtype)

    b_spec = pl.BlockSpec((tn, K), lambda j, i: (j, 0)) if bt else pl.BlockSpec((K, tn), lambda j, i: (0, j))
    in_specs = [pl.BlockSpec((tm, K), lambda j, i: (i, 0)), b_spec]
    args = [a, b]
    if res is not None:
        in_specs.append(pl.BlockSpec((tm, tn), lambda j, i: (i, j)))
        args.append(res)
    return pl.pallas_call(
        body, grid=(N // tn, M // tm), in_specs=in_specs,
        out_specs=pl.BlockSpec((tm, tn), lambda j, i: (i, j)),
        out_shape=jax.ShapeDtypeStruct((M, N), out_dtype), name=name)(*args)


def _mm_nt(a, b, res=None, name="mm_nt"):
    M, Nc = a.shape
    K = b.shape[0]
    tm = _tile(M, 704, 16)
    tc = _tile(Nc, 5632, 128)

    def body(*refs):
        if res is None:
            a_ref, b_ref, o_ref = refs
        else:
            a_ref, b_ref, r_ref, o_ref = refs
        c = pl.program_id(1)
        p = lax.dot_general(a_ref[...], b_ref[...], (((1,), (1,)), ((), ())), preferred_element_type=F32)

        @pl.when(c == 0)
        def _():
            if res is None:
                o_ref[...] = p
            else:
                o_ref[...] = p + r_ref[...]

        @pl.when(c > 0)
        def _():
            o_ref[...] += p

    in_specs = [pl.BlockSpec((tm, tc), lambda i, c: (i, c)), pl.BlockSpec((K, tc), lambda i, c: (0, c))]
    args = [a, b]
    if res is not None:
        in_specs.append(pl.BlockSpec((tm, K), lambda i, c: (i, 0)))
        args.append(res)
    return pl.pallas_call(
        body, grid=(M // tm, Nc // tc), in_specs=in_specs,
        out_specs=pl.BlockSpec((tm, K), lambda i, c: (i, 0)),
        out_shape=jax.ShapeDtypeStruct((M, K), F32), name=name)(*args)


def _mm_tn(a, b, out_dtype=F32, name="mm_tn"):
    M, K = a.shape
    N = b.shape[1]
    tm = _tile(M, 2752, 16)
    tk = _tile(K, 1408, 128)
    tn = _tile(N, 1408, 128)
    steps = M // tm

    def body(a_ref, b_ref, o_ref, *scratch):
        acc = scratch[0] if scratch else o_ref
        m = pl.program_id(2)
        p = lax.dot_general(a_ref[...], b_ref[...], (((0,), (0,)), ((), ())), preferred_element_type=F32)

        @pl.when(m == 0)
        def _():
            acc[...] = p

        @pl.when(m > 0)
        def _():
            acc[...] += p

        if scratch:
            @pl.when(m == steps - 1)
            def _():
                o_ref[...] = acc[...].astype(out_dtype)

    return pl.pallas_call(
        body, grid=(K // tk, N // tn, steps),
        in_specs=[pl.BlockSpec((tm, tk), lambda kk, j, m: (m, kk)), pl.BlockSpec((tm, tn), lambda kk, j, m: (m, j))],
        out_specs=pl.BlockSpec((tk, tn), lambda kk, j, m: (kk, j)),
        out_shape=jax.ShapeDtypeStruct((K, N), out_dtype),
        scratch_shapes=[] if out_dtype == F32 else [pltpu.VMEM((tk, tn), F32)], name=name)(a, b)


class _Rows:
    def __init__(self, body, first, head=None):
        self.body, self.first, self.head = body, first, head
        self.shape = (body.shape[0] + first, body.shape[1])


def _rows_operands(x, tr):
    if not isinstance(x, _Rows):
        return [x], [pl.BlockSpec((tr, x.shape[1]), lambda i: (i, 0))]
    assert x.first % 8 == 0 and x.first <= tr <= x.body.shape[0] and x.shape[0] % tr == 0
    width = x.shape[1]
    args = [x.body]
    specs = [pl.BlockSpec((pl.Element(tr), pl.Element(width)),
                          lambda i: (pl.multiple_of(jnp.maximum(i * tr - x.first, 0), 8), 0))]
    if x.head is not None:
        args.append(jnp.pad(x.head, ((0, tr - x.first), (0, 0))))
        specs.append(pl.BlockSpec((tr, width), lambda i: (0, 0)))
    return args, specs


def _rows_tile(x, refs, i, tr):
    blk = refs[0][...]
    if not isinstance(x, _Rows):
        return blk
    shifted = pltpu.roll(blk, x.first, 0)
    if x.head is not None:
        row = lax.broadcasted_iota(jnp.int32, (tr, 1), 0)
        shifted = jnp.where(row < x.first, refs[1][...], shifted)
    return jnp.where(i == 0, shifted, blk)


def _rms_fwd(x, g, name):
    Lp = x.shape[0]
    tr = _tile(Lp, 256, 16)
    args, specs = _rows_operands(x, tr)
    n = len(args)

    def body(*refs):
        g_ref, o_ref = refs[n:]
        xv = _rows_tile(x, refs[:n], pl.program_id(0), tr)
        r = lax.rsqrt(jnp.mean(xv * xv, axis=-1, keepdims=True) + EPS)
        o_ref[...] = (xv * r * g_ref[...]).astype(BF16)

    return pl.pallas_call(
        body, grid=(Lp // tr,),
        in_specs=specs + [pl.BlockSpec((1, D_MODEL), lambda i: (0, 0))],
        out_specs=pl.BlockSpec((tr, D_MODEL), lambda i: (i, 0)),
        out_shape=jax.ShapeDtypeStruct((Lp, D_MODEL), BF16), name=name)(*args, g)


class _Producer:
    def __init__(self, a, b, res=None):
        self.a, self.b, self.res = a, b, res
        self.tr = _tile(a.shape[0], 704, 16)
        K = a.shape[1]
        r_args, r_specs = ([], []) if res is None else _rows_operands(res, self.tr)
        self.args = [a, b] + r_args
        self.specs = [pl.BlockSpec((self.tr, K), lambda i: (i, 0)),
                      pl.BlockSpec((K, D_MODEL), lambda i: (0, 0), pipeline_mode=pl.Buffered(1))] + r_specs

    def tile(self, refs, i):
        acc = jnp.dot(refs[0][...], refs[1][...], preferred_element_type=F32)
        return acc if self.res is None else acc + _rows_tile(self.res, refs[2:], i, self.tr)


def _mm_rms_fwd(prod, g, name):
    Lp, tr, n = prod.a.shape[0], prod.tr, len(prod.args)

    def body(*refs):
        g_ref, x_ref, o_ref = refs[n:]
        xv = prod.tile(refs[:n], pl.program_id(0))
        r = lax.rsqrt(jnp.mean(xv * xv, axis=-1, keepdims=True) + EPS)
        x_ref[...] = xv
        o_ref[...] = (xv * r * g_ref[...]).astype(BF16)

    blk = pl.BlockSpec((tr, D_MODEL), lambda i: (i, 0))
    return pl.pallas_call(
        body, grid=(Lp // tr,), in_specs=prod.specs + [pl.BlockSpec((1, D_MODEL), lambda i: (0, 0))],
        out_specs=[blk, blk],
        out_shape=[jax.ShapeDtypeStruct((Lp, D_MODEL), F32), jax.ShapeDtypeStruct((Lp, D_MODEL), BF16)],
        name=name)(*prod.args, g)


def _rms_bwd(x, g, dy, dres, pad, name):
    Lp = x.shape[0]
    fused = isinstance(dy, _Producer)
    tr = dy.tr if fused else _tile(Lp, 256, 16)
    n = len(dy.args) if fused else 1
    x_args, x_specs = _rows_operands(x, tr)
    nx = len(x_args)

    def body(*refs):
        g_ref, dr_ref, dx_ref, dxb_ref, dg_ref = refs[n + nx:]
        i = pl.program_id(0)
        xv = _rows_tile(x, refs[n:n + nx], i, tr)
        r = lax.rsqrt(jnp.mean(xv * xv, axis=-1, keepdims=True) + EPS)
        xh = xv * r
        dyv = dy.tile(refs[:n], i) if fused else refs[0][...]
        dxh = dyv * g_ref[...]
        dx = r * (dxh - xh * jnp.mean(dxh * xh, axis=-1, keepdims=True)) + dr_ref[...]
        row = i * tr + lax.broadcasted_iota(jnp.int32, (tr, 1), 0)
        dx = jnp.where(row >= pad, dx, 0.0)
        dx_ref[...] = dx
        dxb_ref[...] = dx.astype(BF16)
        part = jnp.sum(dyv * xh, axis=0, keepdims=True)

        @pl.when(i == 0)
        def _():
            dg_ref[...] = part

        @pl.when(i > 0)
        def _():
            dg_ref[...] += part

    blk = pl.BlockSpec((tr, D_MODEL), lambda i: (i, 0))
    vec = pl.BlockSpec((1, D_MODEL), lambda i: (0, 0))
    return pl.pallas_call(
        body, grid=(Lp // tr,), in_specs=(dy.specs if fused else [blk]) + x_specs + [vec, blk],
        out_specs=[blk, blk, vec],
        out_shape=[jax.ShapeDtypeStruct((Lp, D_MODEL), F32), jax.ShapeDtypeStruct((Lp, D_MODEL), BF16),
                   jax.ShapeDtypeStruct((1, D_MODEL), F32)], name=name)(*(dy.args if fused else [dy]), *x_args, g, dres)


def _final(h2, g, tgt, first_row):
    fused = isinstance(h2, _Producer)
    Lp = h2.a.shape[0] if fused else h2.shape[0]
    tr = h2.tr if fused else _tile(Lp, 256, 16)
    n = len(h2.args) if fused else 1
    t_args, t_specs = _rows_operands(tgt, tr)
    nt = len(t_args)

    def body(*refs):
        g_ref = refs[n]
        loss_ref, dx_ref, dxb_ref, dg_ref = refs[n + 1 + nt:]
        i = pl.program_id(0)
        xv = h2.tile(refs[:n], i) if fused else refs[0][...]
        tv = _rows_tile(tgt, refs[n + 1:n + 1 + nt], i, tr)
        gv = g_ref[...]
        r = lax.rsqrt(jnp.mean(xv * xv, axis=-1, keepdims=True) + EPS)
        xh = xv * r
        row = i * tr + lax.broadcasted_iota(jnp.int32, (tr, 1), 0)
        err = jnp.where(row >= first_row, xh * gv - tv, 0.0)
        lpart = jnp.sum(err * err, axis=0, keepdims=True) * (0.5 / D_MODEL)
        dyv = err * (1.0 / D_MODEL)
        dxh = dyv * gv
        dx = r * (dxh - xh * jnp.mean(dxh * xh, axis=-1, keepdims=True))
        dx_ref[...] = dx
        dxb_ref[...] = dx.astype(BF16)
        part = jnp.sum(dyv * xh, axis=0, keepdims=True)

        @pl.when(i == 0)
        def _():
            dg_ref[...] = part
            loss_ref[...] = lpart

        @pl.when(i > 0)
        def _():
            dg_ref[...] += part
            loss_ref[...] += lpart

    blk = pl.BlockSpec((tr, D_MODEL), lambda i: (i, 0))
    vec = pl.BlockSpec((1, D_MODEL), lambda i: (0, 0))
    return pl.pallas_call(
        body, grid=(Lp // tr,), in_specs=(h2.specs if fused else [blk]) + [vec] + t_specs,
        out_specs=[vec, blk, blk, vec],
        out_shape=[jax.ShapeDtypeStruct((1, D_MODEL), F32), jax.ShapeDtypeStruct((Lp, D_MODEL), F32),
                   jax.ShapeDtypeStruct((Lp, D_MODEL), BF16), jax.ShapeDtypeStruct((1, D_MODEL), F32)],
        name="final_norm_loss")(*(h2.args if fused else [h2]), g, *t_args)


def _halo_prev(tr, width, col=0):
    return pl.BlockSpec((8, width), lambda i: (jnp.maximum(i * (tr // 8) - 1, 0), col))


def _halo_next(tr, width, nrows, col=0, rows=8):
    last = nrows // rows - 1
    return pl.BlockSpec((rows, width), lambda i: (jnp.minimum((i + 1) * (tr // rows), last), col))


def _shifted(x, offs):
    n = x.shape[0]
    return [x if off == 0 else pltpu.roll(x, n - off, 0) for off in offs]


def _taps(wins, w, rows, bias=None):
    acc = w[0:1, :] * wins[0][0:rows, :]
    if bias is not None:
        acc = acc + bias
    for kk in range(1, len(wins)):
        acc = acc + w[kk:kk + 1, :] * wins[kk][0:rows, :]
    return acc


def _gdn_pre(proj_m, proj_s, conv_w, gparams, pad):
    Lp = proj_m.shape[0]
    tr = _tile(Lp, 192, 64)
    W3 = 3 * D_MODEL

    def body(main_ref, prev_ref, s_ref, w_ref, gp_ref, qkv_ref, gsm_ref, c_ref):
        i = pl.program_id(0)
        prev = jnp.where(i > 0, prev_ref[...], 0.0)
        ext = jnp.concatenate([prev, main_ref[...]], axis=0)
        c = _taps(_shifted(ext, range(8 - (GDN_CONV - 1), 9)), w_ref[...], tr)
        c_ref[...] = c.astype(BF16)
        s = c * _sig(c)
        scale = GDN_D ** -0.5
        for j in range(2 * GDN_H):
            seg = s[:, j * GDN_D:(j + 1) * GDN_D]
            r = lax.rsqrt(_rowsum(seg * seg) + EPS)
            if j < GDN_H:
                r = r * scale
            qkv_ref[:, j * GDN_D:(j + 1) * GDN_D] = seg * r
        qkv_ref[:, 2 * D_MODEL:] = s[:, 2 * D_MODEL:]
        sm = s_ref[...]
        gp = gp_ref[...]
        lane = lax.broadcasted_iota(jnp.int32, sm.shape, 1)
        z = sm + gp[1:2, :]
        softplus = jnp.maximum(z, 0.0) + jnp.log(1.0 + jnp.exp(-jnp.abs(z)))
        lg = -jnp.exp(gp[0:1, :]) * softplus
        row = i * tr + lax.broadcasted_iota(jnp.int32, (tr, 1), 0)
        out = jnp.where(lane < GDN_H, lg, jnp.where(lane < 2 * GDN_H, _sig(sm), 0.0))
        gsm_ref[...] = jnp.where(row >= pad, out, 0.0)

    return pl.pallas_call(
        body, grid=(Lp // tr,),
        in_specs=[pl.BlockSpec((tr, W3), lambda i: (i, 0)), _halo_prev(tr, W3),
                  pl.BlockSpec((tr, LANES), lambda i: (i, 0)),
                  pl.BlockSpec((GDN_CONV, W3), lambda i: (0, 0)), pl.BlockSpec((8, LANES), lambda i: (0, 0))],
        out_specs=[pl.BlockSpec((tr, W3), lambda i: (i, 0)), pl.BlockSpec((tr, LANES), lambda i: (i, 0)),
                   pl.BlockSpec((tr, W3), lambda i: (i, 0))],
        out_shape=[jax.ShapeDtypeStruct((Lp, W3), F32), jax.ShapeDtypeStruct((Lp, LANES), F32),
                   jax.ShapeDtypeStruct((Lp, W3), BF16)],
        name="gdn_pre")(proj_m, proj_m, proj_s, conv_w, gparams)


def _gdn_pre_bwd(proj_m, conv_out, proj_s, conv_w, gparams, dq, dk, dv, dgs, pad):
    Lp = proj_m.shape[0]
    tr = _tile(Lp, 192, 64)
    W3 = 3 * D_MODEL
    te = tr + 8

    def body(main_ref, c_ref, cn_ref, s_ref, w_ref, gp_ref,
             dq_ref, dqn_ref, dk_ref, dkn_ref, dv_ref, dvn_ref, dgs_ref,
             da_ref, ds_ref, dw_ref, dgp_ref):
        i = pl.program_id(0)
        w = w_ref[...]
        c = jnp.concatenate([c_ref[...].astype(F32), cn_ref[...].astype(F32)[0:8]], axis=0)
        sg = _sig(c)
        s = c * sg
        rowe = i * tr + lax.broadcasted_iota(jnp.int32, (te, 1), 0)
        live = (rowe >= pad) & (rowe < Lp)
        dqe = jnp.concatenate([dq_ref[...], dqn_ref[...]], axis=0)
        dke = jnp.concatenate([dk_ref[...], dkn_ref[...]], axis=0)
        dve = jnp.concatenate([dv_ref[...], dvn_ref[...]], axis=0)
        scale = GDN_D ** -0.5
        parts = []
        for j in range(2 * GDN_H):
            seg = s[:, j * GDN_D:(j + 1) * GDN_D]
            r = lax.rsqrt(_rowsum(seg * seg) + EPS)
            xh = seg * r
            if j < GDN_H:
                dxh = dqe[:, j * GDN_D:(j + 1) * GDN_D] * scale
            else:
                dxh = dke[:, (j - GDN_H) * GDN_D:(j - GDN_H + 1) * GDN_D]
            parts.append(r * (dxh - xh * _rowsum(dxh * xh)))
        parts.append(dve)
        dsv = jnp.concatenate(parts, axis=1)
        dc = jnp.where(live, dsv * (sg * (1.0 + c * (1.0 - sg))), 0.0)
        dcs = _shifted(dc, range(GDN_CONV - 1, -1, -1))
        da_ref[...] = _taps(dcs, w, tr).astype(BF16)
        pm = main_ref[...]
        rows = [jnp.sum(dcs[kk][0:tr, :] * pm, axis=0, keepdims=True) for kk in range(GDN_CONV)]
        dwp = jnp.concatenate(rows + [jnp.zeros((8 - GDN_CONV, W3), F32)], axis=0)

        sm = s_ref[...]
        gp = gp_ref[...]
        lane = lax.broadcasted_iota(jnp.int32, sm.shape, 1)
        rowm = i * tr + lax.broadcasted_iota(jnp.int32, (tr, 1), 0)
        dgv = jnp.where(rowm >= pad, dgs_ref[...], 0.0)
        dlg = jnp.where(lane < GDN_H, dgv, 0.0)
        dbt = jnp.where((lane >= GDN_H) & (lane < 2 * GDN_H), dgv, 0.0)
        z = sm + gp[1:2, :]
        softplus = jnp.maximum(z, 0.0) + jnp.log(1.0 + jnp.exp(-jnp.abs(z)))
        ea = jnp.exp(gp[0:1, :])
        dz = dlg * (-ea) * _sig(z)
        dal = dlg * (-ea) * softplus
        bt = _sig(sm)
        dgb = dbt * bt * (1.0 - bt)
        ds_ref[...] = (dz + dgb).astype(BF16)
        gpp = jnp.concatenate([jnp.sum(dal, axis=0, keepdims=True), jnp.sum(dz, axis=0, keepdims=True),
                               jnp.zeros((6, LANES), F32)], axis=0)

        @pl.when(i == 0)
        def _():
            dw_ref[...] = dwp
            dgp_ref[...] = gpp

        @pl.when(i > 0)
        def _():
            dw_ref[...] += dwp
            dgp_ref[...] += gpp

    m3 = pl.BlockSpec((tr, W3), lambda i: (i, 0))
    m1 = pl.BlockSpec((tr, D_MODEL), lambda i: (i, 0))
    n1 = _halo_next(tr, D_MODEL, Lp)
    return pl.pallas_call(
        body, grid=(Lp // tr,),
        in_specs=[m3, m3, _halo_next(tr, W3, Lp, rows=16), pl.BlockSpec((tr, LANES), lambda i: (i, 0)),
                  pl.BlockSpec((GDN_CONV, W3), lambda i: (0, 0)), pl.BlockSpec((8, LANES), lambda i: (0, 0)),
                  m1, n1, m1, n1, m1, n1, pl.BlockSpec((tr, LANES), lambda i: (i, 0))],
        out_specs=[m3, pl.BlockSpec((tr, LANES), lambda i: (i, 0)),
                   pl.BlockSpec((8, W3), lambda i: (0, 0)), pl.BlockSpec((8, LANES), lambda i: (0, 0))],
        out_shape=[jax.ShapeDtypeStruct((Lp, W3), BF16), jax.ShapeDtypeStruct((Lp, LANES), BF16),
                   jax.ShapeDtypeStruct((8, W3), F32), jax.ShapeDtypeStruct((8, LANES), F32)],
        name="gdn_pre_bwd")(proj_m, conv_out, conv_out, proj_s, conv_w, gparams, dq, dq, dk, dk, dv, dv, dgs)


def _gdn_gates(gs):
    ri = lax.broadcasted_iota(jnp.int32, (CHUNK, CHUNK), 0)
    ci = lax.broadcasted_iota(jnp.int32, (CHUNK, CHUNK), 1)
    tril = ri >= ci
    strict = ri > ci
    gall = _dx(tril.astype(F32), gs)
    lane8 = lax.broadcasted_iota(jnp.int32, (8, LANES), 1)
    sub8 = lax.broadcasted_iota(jnp.int32, (8, LANES), 0)
    grow = _dxnt((lane8 == sub8).astype(F32), gall)
    return gall, grow, tril, strict


def _gdn_decay(gall, grow, tril, h):
    g = gall[:, h:h + 1]
    return g, jnp.where(tril, jnp.exp(jnp.where(tril, g - grow[h:h + 1, :], 0.0)), 0.0)


def _group(N):
    return 3 if N % 3 == 0 else (2 if N % 2 == 0 else 1)


def _gdn_chunk_specs(N, rev):
    G = _group(N)
    nb = N // G
    cn = (lambda n: nb - 1 - n) if rev else (lambda n: n)
    col = lambda j: pl.BlockSpec((G * CHUNK, D_MODEL), lambda n: (cn(n), j))
    gate = pl.BlockSpec((G * CHUNK, LANES), lambda n: (cn(n), 0))
    st = lambda a, b: pl.BlockSpec((GDN_H, G, a, b), lambda n: (0, cn(n), 0, 0))
    return G, nb, col, gate, st


def _gdn_chunk_fwd(qkv, gsm):
    Lp = qkv.shape[0]
    N = Lp // CHUNK
    G, nb, col, gate, st = _gdn_chunk_specs(N, False)

    def body(q_ref, k_ref, v_ref, gs_ref, o_ref, sin_ref, t_ref, S):
        n = pl.program_id(0)

        @pl.when(n == 0)
        def _():
            S[...] = jnp.zeros_like(S)

        ri = lax.broadcasted_iota(jnp.int32, (CHUNK, CHUNK), 0)
        ci = lax.broadcasted_iota(jnp.int32, (CHUNK, CHUNK), 1)
        eye = (ri == ci).astype(F32)
        heads = range(GDN_H)
        sls = [slice(h * GDN_D, (h + 1) * GDN_D) for h in heads]
        rows = [slice(c * CHUNK, (c + 1) * CHUNK) for c in range(G)]
        pairs = [(c, h) for c in range(G) for h in heads]
        P = lambda f: {p: f(*p) for p in pairs}
        gs = [gs_ref[rows[c], :] for c in range(G)]
        gates = [_gdn_gates(gs[c]) for c in range(G)]
        tril, strict = gates[0][2], gates[0][3]
        q = P(lambda c, h: q_ref[rows[c], sls[h]])
        k = P(lambda c, h: k_ref[rows[c], sls[h]])
        v = P(lambda c, h: v_ref[rows[c], sls[h]])
        beta = P(lambda c, h: gs[c][:, GDN_H + h:GDN_H + h + 1])
        gg = P(lambda c, h: _gdn_decay(gates[c][0], gates[c][1], tril, h))
        g = {p: x[0] for p, x in gg.items()}
        gam = {p: x[1] for p, x in gg.items()}
        eg = P(lambda c, h: jnp.exp(g[c, h]))
        gl = P(lambda c, h: g[c, h][CHUNK - 1:CHUNK, :])
        kb = P(lambda c, h: k[c, h] * beta[c, h])
        pw = P(lambda c, h: -jnp.where(strict, _dnt(kb[c, h], k[c, h]) * gam[c, h], 0.0))
        p = P(lambda c, h: _dnt(q[c, h], k[c, h]) * gam[c, h])
        t = P(lambda c, h: eye + pw[c, h])
        for _ in range(5):
            pw = P(lambda c, h: _d3g(pw[c, h], pw[c, h], _NN))
            t = P(lambda c, h: t[c, h] + _d3g(t[c, h], pw[c, h], _NN))
        u = P(lambda c, h: _d(t[c, h], v[c, h] * beta[c, h]))
        w = P(lambda c, h: _d(t[c, h], kb[c, h] * eg[c, h]))
        qg = P(lambda c, h: q[c, h] * eg[c, h])
        kd = P(lambda c, h: k[c, h] * jnp.exp(gl[c, h] - g[c, h]))
        egl = P(lambda c, h: jnp.exp(gl[c, h]))
        for c in range(G):
            for h in heads:
                t_ref[h, c] = t[c, h]
        cur = [S[h] for h in heads]
        for c in range(G):
            vnew = [u[c, h] - _d(w[c, h], cur[h]) for h in heads]
            for h in heads:
                o_ref[rows[c], sls[h]] = _d(qg[c, h], cur[h]) + _d(p[c, h], vnew[h])
                sin_ref[h, c] = cur[h]
            cur = [cur[h] * egl[c, h] + _dtn(kd[c, h], vnew[h]) for h in heads]
        for h in heads:
            S[h] = cur[h]

    return pl.pallas_call(
        body, grid=(nb,),
        in_specs=[col(0), col(1), col(2), gate],
        out_specs=[col(0), st(GDN_D, GDN_D), st(CHUNK, CHUNK)],
        out_shape=[jax.ShapeDtypeStruct((Lp, D_MODEL), F32), jax.ShapeDtypeStruct((GDN_H, N, GDN_D, GDN_D), F32),
                   jax.ShapeDtypeStruct((GDN_H, N, CHUNK, CHUNK), F32)],
        scratch_shapes=[pltpu.VMEM((GDN_H, GDN_D, GDN_D), F32)],
        name="gdn_chunk_fwd")(qkv, qkv, qkv, gsm)


def _gdn_chunk_bwd(qkv, gsm, do, s_in, t_in):
    Lp = qkv.shape[0]
    N = Lp // CHUNK
    G, nb, col, gate, st = _gdn_chunk_specs(N, True)

    def body(q_ref, k_ref, v_ref, gs_ref, do_ref, sin_ref, t_ref, dq_ref, dk_ref, dv_ref, dgs_ref, dS):
        n = pl.program_id(0)

        @pl.when(n == 0)
        def _():
            dS[...] = jnp.zeros_like(dS)

        lane = lax.broadcasted_iota(jnp.int32, (CHUNK, LANES), 1)
        rcol = lax.broadcasted_iota(jnp.int32, (CHUNK, 1), 0)
        ri = lax.broadcasted_iota(jnp.int32, (CHUNK, CHUNK), 0)
        ci = lax.broadcasted_iota(jnp.int32, (CHUNK, CHUNK), 1)
        ones = jnp.ones((CHUNK, LANES), F32)
        heads = range(GDN_H)
        sls = [slice(h * GDN_D, (h + 1) * GDN_D) for h in heads]
        rows = [slice(c * CHUNK, (c + 1) * CHUNK) for c in range(G)]
        pairs = [(c, h) for c in range(G) for h in heads]
        P = lambda f: {p: f(*p) for p in pairs}
        gs = [gs_ref[rows[c], :] for c in range(G)]
        gates = [_gdn_gates(gs[c]) for c in range(G)]
        tril, strict = gates[0][2], gates[0][3]
        q = P(lambda c, h: q_ref[rows[c], sls[h]])
        k = P(lambda c, h: k_ref[rows[c], sls[h]])
        v = P(lambda c, h: v_ref[rows[c], sls[h]])
        dov = P(lambda c, h: do_ref[rows[c], sls[h]])
        s0 = P(lambda c, h: sin_ref[h, c])
        t = P(lambda c, h: t_ref[h, c])
        beta = P(lambda c, h: gs[c][:, GDN_H + h:GDN_H + h + 1])
        gg = P(lambda c, h: _gdn_decay(gates[c][0], gates[c][1], tril, h))
        g = {p: x[0] for p, x in gg.items()}
        gam = {p: x[1] for p, x in gg.items()}
        eg = P(lambda c, h: jnp.exp(g[c, h]))
        egl = P(lambda c, h: jnp.exp(g[c, h][CHUNK - 1:CHUNK, :]))
        e = P(lambda c, h: jnp.exp(g[c, h][CHUNK - 1:CHUNK, :] - g[c, h]))
        kb = P(lambda c, h: k[c, h] * beta[c, h])
        kbg = P(lambda c, h: kb[c, h] * eg[c, h])
        vb = P(lambda c, h: v[c, h] * beta[c, h])
        qg = P(lambda c, h: q[c, h] * eg[c, h])
        kd = P(lambda c, h: k[c, h] * e[c, h])
        m = P(lambda c, h: jnp.where(strict, _dnt(kb[c, h], k[c, h]) * gam[c, h], 0.0))
        u = P(lambda c, h: _d(t[c, h], vb[c, h]))
        w = P(lambda c, h: _d(t[c, h], kbg[c, h]))
        p = P(lambda c, h: _dnt(q[c, h], k[c, h]) * gam[c, h])
        dqg = P(lambda c, h: _dnt(dov[c, h], s0[c, h]))
        qgdo = P(lambda c, h: _dtn(qg[c, h], dov[c, h]))
        ptdo = P(lambda c, h: _dtn(p[c, h], dov[c, h]))
        vnew = P(lambda c, h: u[c, h] - _d(w[c, h], s0[c, h]))
        dp = P(lambda c, h: jnp.where(tril, _dnt(dov[c, h], vnew[c, h]), 0.0))
        cur = [dS[h] for h in heads]
        dvnew, dkd, sds = {}, {}, {}
        for c in reversed(range(G)):
            for h in heads:
                dvnew[c, h] = ptdo[c, h] + _d(kd[c, h], cur[h])
                dkd[c, h] = _dnt(vnew[c, h], cur[h])
                sds[c, h] = _allsum(s0[c, h] * cur[h])
            cur = [qgdo[c, h] + egl[c, h] * cur[h] - _dtn(w[c, h], dvnew[c, h]) for h in heads]
        for h in heads:
            dS[h] = cur[h]
        dw = P(lambda c, h: -_dnt(dvnew[c, h], s0[c, h]))
        dvb = P(lambda c, h: _dtn(t[c, h], dvnew[c, h]))
        dkbg = P(lambda c, h: _dtn(t[c, h], dw[c, h]))
        dt = P(lambda c, h: _dnt(dvnew[c, h], vb[c, h]) + _dnt(dw[c, h], kbg[c, h]))
        x1 = P(lambda c, h: _d3g(t[c, h], dt[c, h], _TN))
        dm = P(lambda c, h: jnp.where(strict, -_d3g(x1[c, h], t[c, h], _NT), 0.0))
        dkk = P(lambda c, h: dm[c, h] * gam[c, h])
        dqk = P(lambda c, h: dp[c, h] * gam[c, h])
        dkb = P(lambda c, h: _d(dkk[c, h], k[c, h]) + eg[c, h] * dkbg[c, h])
        em = P(lambda c, h: dm[c, h] * m[c, h] + dp[c, h] * p[c, h])
        colsum = P(lambda c, h: _d3g(em[c, h], ones, _TN)[:, 0:1])
        for c, h in pairs:
            dk_ref[rows[c], sls[h]] = (_dtn(dkk[c, h], kb[c, h]) + _dtn(dqk[c, h], q[c, h]) + dkd[c, h] * e[c, h]
                                       + beta[c, h] * dkb[c, h])
            dq_ref[rows[c], sls[h]] = _d(dqk[c, h], k[c, h]) + dqg[c, h] * eg[c, h]
            dv_ref[rows[c], sls[h]] = beta[c, h] * dvb[c, h]
        for c in range(G):
            dg_all = jnp.zeros((CHUNK, LANES), F32)
            dbeta_all = jnp.zeros((CHUNK, LANES), F32)
            for h in heads:
                dbeta = _rowsum(k[c, h] * dkb[c, h]) + _rowsum(v[c, h] * dvb[c, h])
                z = _rowsum(kd[c, h] * dkd[c, h])
                dg = (_rowsum(em[c, h]) - colsum[c, h] + _rowsum(qg[c, h] * dqg[c, h]) + _rowsum(kbg[c, h] * dkbg[c, h])
                      - z)
                extra = _allsum(z) + egl[c, h] * sds[c, h]
                dg = dg + jnp.where(rcol == CHUNK - 1, extra, 0.0)
                dg_all = dg_all + jnp.where(lane == h, dg, 0.0)
                dbeta_all = dbeta_all + jnp.where(lane == GDN_H + h, dbeta, 0.0)
            dgs_ref[rows[c], :] = _dx((ci >= ri).astype(F32), dg_all) + dbeta_all

    return pl.pallas_call(
        body, grid=(nb,),
        in_specs=[col(0), col(1), col(2), gate, col(0), st(GDN_D, GDN_D), st(CHUNK, CHUNK)],
        out_specs=[col(0), col(0), col(0), gate],
        out_shape=[jax.ShapeDtypeStruct((Lp, D_MODEL), F32)] * 3 + [jax.ShapeDtypeStruct((Lp, LANES), F32)],
        scratch_shapes=[pltpu.VMEM((GDN_H, GDN_D, GDN_D), F32)],
        name="gdn_chunk_bwd")(qkv, qkv, qkv, gsm, do, s_in, t_in)


def _rot(x, c, s):
    half = RET_D // 2
    x1 = x[:, :half]
    x2 = x[:, half:]
    return jnp.concatenate([x1 * c - x2 * s, x2 * c + x1 * s], axis=1)


def _rot_bwd(d, c, s):
    half = RET_D // 2
    d1 = d[:, :half]
    d2 = d[:, half:]
    return jnp.concatenate([d1 * c + d2 * s, d2 * c - d1 * s], axis=1)


def _ret_tables():
    hh = jnp.arange(RET_H, dtype=F32)
    lg = jnp.log(1.0 - 2.0 ** (-5.0 - hh))
    idx = jnp.arange(CHUNK, dtype=F32)
    tril = jnp.asarray(np.tril(np.ones((CHUNK, CHUNK), dtype=bool)))
    dmask = jnp.where(tril, jnp.exp((idx[:, None] - idx[None, :]) * lg[:, None, None]), 0.0)
    qdec = jnp.exp((idx[None, :] + 1.0) * lg[:, None])
    kdec = jnp.exp((CHUNK - 1.0 - idx[None, :]) * lg[:, None])
    gch = jnp.exp(CHUNK * lg)
    qdec = jnp.broadcast_to(qdec[:, :, None], (RET_H, CHUNK, RET_D))
    kdec = jnp.broadcast_to(kdec[:, :, None], (RET_H, CHUNK, RET_D))
    gch = jnp.broadcast_to(gch[:, None, None], (RET_H, 8, LANES))
    return dmask, qdec, kdec, gch


def _ret_specs(N, rev):
    G = _group(N)
    nb = N // G
    cn = (lambda n: nb - 1 - n) if rev else (lambda n: n)
    col = lambda j: pl.BlockSpec((G * CHUNK, D_MODEL), lambda n: (cn(n), j))
    tab = lambda a, b: pl.BlockSpec((RET_H, a, b), lambda n: (0, 0, 0))
    rope = pl.BlockSpec((G * CHUNK, LANES), lambda n: (cn(n), 0))
    st = pl.BlockSpec((RET_H, G, RET_D, RET_D), lambda n: (0, cn(n), 0, 0))
    return G, nb, col, tab, rope, st


def _ret_chunk_fwd(proj_m, cos, sin, tables):
    Lp = proj_m.shape[0]
    N = Lp // CHUNK
    dmask, qdec, kdec, gch = tables
    G, nb, col, tab, rope, st = _ret_specs(N, False)

    def body(q_ref, k_ref, v_ref, c_ref, s_ref, dm_ref, qd_ref, kd_ref, g_ref, o_ref, sin_ref, S):
        n = pl.program_id(0)

        @pl.when(n == 0)
        def _():
            S[...] = jnp.zeros_like(S)

        heads = range(RET_H)
        sls = [slice(h * RET_D, (h + 1) * RET_D) for h in heads]
        rows = [slice(c * CHUNK, (c + 1) * CHUNK) for c in range(G)]
        pairs = [(c, h) for c in range(G) for h in heads]
        P = lambda f: {p: f(*p) for p in pairs}
        qr = P(lambda c, h: _rot(q_ref[rows[c], sls[h]], c_ref[rows[c], :], s_ref[rows[c], :]))
        ks = P(lambda c, h: _rot(k_ref[rows[c], sls[h]], c_ref[rows[c], :], s_ref[rows[c], :]) * (RET_D ** -0.5))
        v = P(lambda c, h: v_ref[rows[c], sls[h]])
        a = P(lambda c, h: _dnt(qr[c, h], ks[c, h]) * dm_ref[h])
        av = P(lambda c, h: _d(a[c, h], v[c, h]))
        kv = P(lambda c, h: _dtn(ks[c, h] * kd_ref[h], v[c, h]))
        qd = P(lambda c, h: qr[c, h] * qd_ref[h])
        cur = [S[h] for h in heads]
        for c in range(G):
            for h in heads:
                o_ref[rows[c], sls[h]] = av[c, h] + _d(qd[c, h], cur[h])
                sin_ref[h, c] = cur[h].astype(BF16)
            cur = [cur[h] * g_ref[h, 0:1, 0:1] + kv[c, h] for h in heads]
        for h in heads:
            S[h] = cur[h]

    return pl.pallas_call(
        body, grid=(nb,),
        in_specs=[col(3), col(4), col(5), rope, rope,
                  tab(CHUNK, CHUNK), tab(CHUNK, RET_D), tab(CHUNK, RET_D), tab(8, LANES)],
        out_specs=[col(0), st],
        out_shape=[jax.ShapeDtypeStruct((Lp, D_MODEL), F32), jax.ShapeDtypeStruct((RET_H, N, RET_D, RET_D), BF16)],
        scratch_shapes=[pltpu.VMEM((RET_H, RET_D, RET_D), F32)],
        name="ret_chunk_fwd")(proj_m, proj_m, proj_m, cos, sin, dmask, qdec, kdec, gch)


def _ret_chunk_bwd(proj_m, cos, sin, tables, do, s_in):
    Lp = proj_m.shape[0]
    N = Lp // CHUNK
    dmask, qdec, kdec, gch = tables
    G, nb, col, tab, rope, st = _ret_specs(N, True)

    def body(q_ref, k_ref, v_ref, c_ref, s_ref, dm_ref, qd_ref, kd_ref, g_ref, do_ref, sin_ref,
             d_ref, dS):
        n = pl.program_id(0)

        @pl.when(n == 0)
        def _():
            dS[...] = jnp.zeros_like(dS)

        kscale = RET_D ** -0.5
        heads = range(RET_H)
        sls = [slice(h * RET_D, (h + 1) * RET_D) for h in heads]
        rows = [slice(c * CHUNK, (c + 1) * CHUNK) for c in range(G)]
        pairs = [(c, h) for c in range(G) for h in heads]
        P = lambda f: {p: f(*p) for p in pairs}
        cs = [(c_ref[rows[c], :], s_ref[rows[c], :]) for c in range(G)]
        osl = lambda part, h: slice(part * D_MODEL + h * RET_D, part * D_MODEL + (h + 1) * RET_D)
        qr = P(lambda c, h: _rot(q_ref[rows[c], sls[h]], *cs[c]))
        ks = P(lambda c, h: _rot(k_ref[rows[c], sls[h]], *cs[c]) * kscale)
        v = P(lambda c, h: v_ref[rows[c], sls[h]])
        dov = P(lambda c, h: do_ref[rows[c], sls[h]])
        ad = P(lambda c, h: _dnt(qr[c, h], ks[c, h]) * dm_ref[h])
        da = P(lambda c, h: _dnt(dov[c, h], v[c, h]) * dm_ref[h])
        dos = P(lambda c, h: _dnt(dov[c, h], sin_ref[h, c]) * qd_ref[h])
        qdo = P(lambda c, h: _dtn(qr[c, h] * qd_ref[h], dov[c, h]))
        adv = P(lambda c, h: _dtn(ad[c, h], dov[c, h]))
        dqr = P(lambda c, h: _d(da[c, h], ks[c, h]) + dos[c, h])
        daq = P(lambda c, h: _dtn(da[c, h], qr[c, h]))
        kk = P(lambda c, h: ks[c, h] * kd_ref[h])
        cur = [dS[h] for h in heads]
        for c in reversed(range(G)):
            for h in heads:
                d_ref[rows[c], osl(2, h)] = (adv[c, h] + _d(kk[c, h], cur[h])).astype(BF16)
                d_ref[rows[c], osl(0, h)] = _rot_bwd(dqr[c, h], *cs[c]).astype(BF16)
                dks = daq[c, h] + _dnt(v[c, h], cur[h]) * kd_ref[h]
                d_ref[rows[c], osl(1, h)] = _rot_bwd(dks * kscale, *cs[c]).astype(BF16)
            cur = [cur[h] * g_ref[h, 0:1, 0:1] + qdo[c, h] for h in heads]
        for h in heads:
            dS[h] = cur[h]

    return pl.pallas_call(
        body, grid=(nb,),
        in_specs=[col(3), col(4), col(5), rope, rope,
                  tab(CHUNK, CHUNK), tab(CHUNK, RET_D), tab(CHUNK, RET_D), tab(8, LANES), col(0), st],
        out_specs=pl.BlockSpec((G * CHUNK, 3 * D_MODEL), lambda n: (nb - 1 - n, 0)),
        out_shape=jax.ShapeDtypeStruct((Lp, 3 * D_MODEL), BF16),
        scratch_shapes=[pltpu.VMEM((RET_H, RET_D, RET_D), F32)],
        name="ret_chunk_bwd")(proj_m, proj_m, proj_m, cos, sin, dmask, qdec, kdec, gch, do, s_in)


def _merge_specs(tr):
    col = lambda j: pl.BlockSpec((tr, D_MODEL), lambda i: (i, j))
    return col


def _merge_fwd(o_a, o_b, proj_m, gnorm):
    Lp = o_a.shape[0]
    tr = _tile(Lp, 192, 16)

    def body(oa_ref, ob_ref, gz_ref, rg_ref, ga_ref, gb_ref, gn_ref, y_ref):
        gn = gn_ref[...]
        oa = oa_ref[...]
        ob = ob_ref[...]
        gz = gz_ref[...]
        ya = []
        for j in range(GDN_H):
            seg = oa[:, j * GDN_D:(j + 1) * GDN_D]
            r = lax.rsqrt(jnp.mean(seg * seg, axis=-1, keepdims=True) + EPS)
            ya.append(seg * r * gn)
        ya = jnp.concatenate(ya, axis=1) * (gz * _sig(gz))
        yb = []
        for j in range(RET_H):
            seg = ob[:, j * RET_D:(j + 1) * RET_D]
            r = lax.rsqrt(jnp.mean(seg * seg, axis=-1, keepdims=True) + EPS)
            yb.append(seg * r)
        rg = rg_ref[...]
        yb = jnp.concatenate(yb, axis=1) * (rg * _sig(rg))
        y_ref[...] = (_sig(ga_ref[...]) * ya + _sig(gb_ref[...]) * yb).astype(BF16)

    col = _merge_specs(tr)
    return pl.pallas_call(
        body, grid=(Lp // tr,),
        in_specs=[col(0), col(0), col(6), col(7), col(8), col(9), pl.BlockSpec((1, GDN_D), lambda i: (0, 0))],
        out_specs=col(0), out_shape=jax.ShapeDtypeStruct((Lp, D_MODEL), BF16),
        name="merge_fwd")(o_a, o_b, proj_m, proj_m, proj_m, proj_m, gnorm)


def _merge_bwd(dh1b, w_out, o_a, o_b, proj_m, gnorm):
    Lp = o_a.shape[0]
    tr = _tile(Lp, 192, 16)

    def body(d_ref, wo_ref, oa_ref, ob_ref, gz_ref, rg_ref, ga_ref, gb_ref, gn_ref, dc_ref, doa_ref, dob_ref, dgn_ref):
        i = pl.program_id(0)
        gn = gn_ref[...]
        dyv = lax.dot_general(d_ref[...], wo_ref[...], _NT, preferred_element_type=F32)
        oa = oa_ref[...]
        ob = ob_ref[...]
        gz = gz_ref[...]
        rg = rg_ref[...]
        sa = _sig(ga_ref[...])
        sb = _sig(gb_ref[...])
        dya = dyv * sa
        dyb = dyv * sb
        sgz = _sig(gz)
        szz = gz * sgz
        dgn = jnp.zeros((1, GDN_D), F32)
        ya = []
        dgz = []
        for j in range(GDN_H):
            sl = slice(j * GDN_D, (j + 1) * GDN_D)
            seg = oa[:, sl]
            r = lax.rsqrt(jnp.mean(seg * seg, axis=-1, keepdims=True) + EPS)
            xh = seg * r
            oan = xh * gn
            ya.append(oan * szz[:, sl])
            dgz.append(dya[:, sl] * oan * (sgz[:, sl] * (1.0 + gz[:, sl] * (1.0 - sgz[:, sl]))))
            doan = dya[:, sl] * szz[:, sl]
            dgn = dgn + jnp.sum(doan * xh, axis=0, keepdims=True)
            dxh = doan * gn
            doa_ref[:, sl] = r * (dxh - xh * jnp.mean(dxh * xh, axis=-1, keepdims=True))
        ya = jnp.concatenate(ya, axis=1)
        srg = _sig(rg)
        srr = rg * srg
        yb = []
        drg = []
        for j in range(RET_H):
            sl = slice(j * RET_D, (j + 1) * RET_D)
            seg = ob[:, sl]
            r = lax.rsqrt(jnp.mean(seg * seg, axis=-1, keepdims=True) + EPS)
            xh = seg * r
            yb.append(xh * srr[:, sl])
            drg.append(dyb[:, sl] * xh * (srg[:, sl] * (1.0 + rg[:, sl] * (1.0 - srg[:, sl]))))
            dxh = dyb[:, sl] * srr[:, sl]
            dob_ref[:, sl] = r * (dxh - xh * jnp.mean(dxh * xh, axis=-1, keepdims=True))
        yb = jnp.concatenate(yb, axis=1)
        dc_ref[:, 0:D_MODEL] = jnp.concatenate(dgz, axis=1).astype(BF16)
        dc_ref[:, D_MODEL:2 * D_MODEL] = jnp.concatenate(drg, axis=1).astype(BF16)
        dc_ref[:, 2 * D_MODEL:3 * D_MODEL] = (dyv * ya * sa * (1.0 - sa)).astype(BF16)
        dc_ref[:, 3 * D_MODEL:] = (dyv * yb * sb * (1.0 - sb)).astype(BF16)

        @pl.when(i == 0)
        def _():
            dgn_ref[...] = dgn

        @pl.when(i > 0)
        def _():
            dgn_ref[...] += dgn

    col = _merge_specs(tr)
    return pl.pallas_call(
        body, grid=(Lp // tr,),
        in_specs=[col(0), pl.BlockSpec((D_MODEL, D_MODEL), lambda i: (0, 0), pipeline_mode=pl.Buffered(1)),
                  col(0), col(0), col(6), col(7), col(8), col(9), pl.BlockSpec((1, GDN_D), lambda i: (0, 0))],
        out_specs=[pl.BlockSpec((tr, 4 * D_MODEL), lambda i: (i, 0)), col(0), col(0),
                   pl.BlockSpec((1, GDN_D), lambda i: (0, 0))],
        out_shape=[jax.ShapeDtypeStruct((Lp, 4 * D_MODEL), BF16), jax.ShapeDtypeStruct((Lp, D_MODEL), F32),
                   jax.ShapeDtypeStruct((Lp, D_MODEL), F32), jax.ShapeDtypeStruct((1, GDN_D), F32)],
        name="merge_bwd")(dh1b, w_out, o_a, o_b, proj_m, proj_m, proj_m, proj_m, gnorm)


def _ffn_act(up, conv_w, conv_b):
    Lp = up.shape[0]
    tr = _tile(Lp, 192, 16)
    W2 = 2 * D_FF

    def body(main_ref, prev_ref, w_ref, b_ref, act_ref, u_ref):
        i = pl.program_id(0)
        prev = jnp.where(i > 0, prev_ref[...], 0.0)
        ext = jnp.concatenate([prev, main_ref[...]], axis=0)
        u = _taps(_shifted(ext, range(8 - (FFN_CONV - 1), 9)), w_ref[...], tr, b_ref[...])
        a = u[:, :D_FF]
        act_ref[...] = (a * _sig(a) * u[:, D_FF:]).astype(BF16)
        u_ref[...] = u.astype(BF16)

    return pl.pallas_call(
        body, grid=(Lp // tr,),
        in_specs=[pl.BlockSpec((tr, W2), lambda i: (i, 0)), _halo_prev(tr, W2),
                  pl.BlockSpec((FFN_CONV, W2), lambda i: (0, 0)), pl.BlockSpec((1, W2), lambda i: (0, 0))],
        out_specs=[pl.BlockSpec((tr, D_FF), lambda i: (i, 0)), pl.BlockSpec((tr, W2), lambda i: (i, 0))],
        out_shape=[jax.ShapeDtypeStruct((Lp, D_FF), BF16), jax.ShapeDtypeStruct((Lp, W2), BF16)],
        name="ffn_act")(up, up, conv_w, conv_b)


def _ffn_act_bwd(up, u, dact, conv_w):
    Lp = up.shape[0]
    tr = _tile(Lp, 192, 16)
    W2 = 2 * D_FF
    te = tr + 8

    def body(up_ref, u_ref, un_ref, da_ref, dan_ref, w_ref, dup_ref, acc_ref):
        i = pl.program_id(0)
        w = w_ref[...]
        ue = jnp.concatenate([u_ref[...].astype(F32), un_ref[...].astype(F32)[0:8]], axis=0)
        a = ue[:, :D_FF]
        b = ue[:, D_FF:]
        rowe = i * tr + lax.broadcasted_iota(jnp.int32, (te, 1), 0)
        dae = jnp.where(rowe < Lp, jnp.concatenate([da_ref[...], dan_ref[...]], axis=0), 0.0)
        sg = _sig(a)
        du = jnp.concatenate([dae * b * (sg * (1.0 + a * (1.0 - sg))), dae * (a * sg)], axis=1)
        dus = _shifted(du, range(FFN_CONV - 1, -1, -1))
        dup_ref[...] = _taps(dus, w, tr).astype(BF16)
        upm = up_ref[...]
        rows = [jnp.sum(dus[kk][0:tr, :] * upm, axis=0, keepdims=True) for kk in range(FFN_CONV)]
        rows.append(jnp.sum(du[0:tr, :], axis=0, keepdims=True))
        part = jnp.concatenate(rows + [jnp.zeros((8 - len(rows), W2), F32)], axis=0)

        @pl.when(i == 0)
        def _():
            acc_ref[...] = part

        @pl.when(i > 0)
        def _():
            acc_ref[...] += part

    return pl.pallas_call(
        body, grid=(Lp // tr,),
        in_specs=[pl.BlockSpec((tr, W2), lambda i: (i, 0)), pl.BlockSpec((tr, W2), lambda i: (i, 0)),
                  _halo_next(tr, W2, Lp, rows=16), pl.BlockSpec((tr, D_FF), lambda i: (i, 0)), _halo_next(tr, D_FF, Lp),
                  pl.BlockSpec((FFN_CONV, W2), lambda i: (0, 0))],
        out_specs=[pl.BlockSpec((tr, W2), lambda i: (i, 0)), pl.BlockSpec((8, W2), lambda i: (0, 0))],
        out_shape=[jax.ShapeDtypeStruct((Lp, W2), BF16), jax.ShapeDtypeStruct((8, W2), F32)],
        name="ffn_act_bwd")(up, u, u, dact, dact, conv_w)


def _local_step(hpad, tgt, pad, wt, first_weights=None, late_weights=None, on_ffn_out_grads=None,
                on_w_in_grads=None):
    Lp = hpad.shape[0]
    first = pad + N_META
    pos = jnp.arange(Lp, dtype=F32) - float(pad)
    half = RET_D // 2
    inv = 1.0 / (ROPE_BASE ** (jnp.arange(half, dtype=F32) / half))
    ang = pos[:, None] * inv[None, :]
    cos, sin = jnp.cos(ang), jnp.sin(ang)
    tables = _ret_tables()
    gparams = jnp.zeros((8, LANES), F32).at[0, :GDN_H].set(wt["a_log"]).at[1, :GDN_H].set(wt["dt_bias"])

    hn1 = _rms_fwd(hpad, wt["norm1"], "rms1_fwd")
    if first_weights is not None:
        wt = {**wt, **first_weights(hn1)}
    proj_m = _mm_nn(hn1, wt["w_main_t"], bt=True, name="proj_main")
    proj_s = _mm_nn(hn1, wt["w_small_t"], bt=True, name="proj_small")
    qkv, gsm, conv_out = _gdn_pre(proj_m, proj_s, wt["gdn_conv_w"], gparams, pad)
    o_a, s_a, t_a = _gdn_chunk_fwd(qkv, gsm)
    o_b, s_b = _ret_chunk_fwd(proj_m, cos, sin, tables)
    y = _merge_fwd(o_a, o_b, proj_m, wt["gdn_norm"])
    if late_weights is not None:
        wt = {**wt, **late_weights(y)}
    h1, hn2 = _mm_rms_fwd(_Producer(y, wt["w_out"], hpad), wt["norm2"], "out_proj_rms2")
    up = _mm_nn(hn2, wt["w_up_t"], bt=True, name="ffn_up")
    act, u_ffn = _ffn_act(up, wt["ffn_conv_w"], wt["ffn_conv_b"])
    lossvec, dh2, dh2b, d_norm_f = _final(_Producer(act, wt["w_down"], h1), wt["norm_f"], tgt, first)

    d_w_down = _mm_tn(act, dh2b, name="dw_down")
    dact = _mm_nt(dh2b, wt["w_down"], name="d_act")
    dup, ffn_rows = _ffn_act_bwd(up, u_ffn, dact, wt["ffn_conv_w"])
    d_w_up_t = _mm_tn(dup, hn2, name="dw_up")
    dh1, dh1b, d_norm2 = _rms_bwd(h1, wt["norm2"], _Producer(dup, wt["w_up_t"]), dh2, pad, "d_hn2_rms2_bwd")

    d_w_out = _mm_tn(y, dh1b, name="dw_out")
    gnorm = wt["gdn_norm"]
    if on_ffn_out_grads is not None:
        gnorm = gnorm + on_ffn_out_grads(d_w_down, d_w_up_t, d_w_out)[0:1, :]
    d_c, do_a, do_b, d_gnorm = _merge_bwd(dh1b, wt["w_out"], o_a, o_b, proj_m, gnorm)
    d_r = _ret_chunk_bwd(proj_m, cos, sin, tables, do_b, s_b)
    dq, dk, dv, dgs = _gdn_chunk_bwd(qkv, gsm, do_a, s_a, t_a)
    d_a, d_s, conv_rows, gp_rows = _gdn_pre_bwd(proj_m, conv_out, proj_s, wt["gdn_conv_w"], gparams, dq, dk, dv, dgs,
                                                pad)

    wmt = wt["w_main_t"]
    segs = [(d_a, 0, 3 * D_MODEL), (d_r, 3 * D_MODEL, 3 * D_MODEL), (d_c, 6 * D_MODEL, 4 * D_MODEL)]
    pa, pr, pc = [_mm_tn(d, hn1, BF16, name="dw_in_%d" % i) for i, (d, _, _) in enumerate(segs)]
    ps = _mm_tn(d_s, hn1, BF16, name="dw_in_small")
    d_w_in_t = jnp.concatenate([pa, pc[:D_MODEL], ps[:2 * GDN_H], pr, pc[D_MODEL:]], axis=0)
    w_small_t = wt["w_small_t"]
    if on_w_in_grads is not None:
        w_small_t = w_small_t + on_w_in_grads(d_w_in_t)[0:1, 0:1].astype(w_small_t.dtype)
    dhn1 = _mm_nn(d_s, w_small_t, name="d_hn1_small")
    for i, (d, off, width) in enumerate(segs[:-1]):
        dhn1 = _mm_nn(d, wmt[off:off + width], res=dhn1, name="d_hn1_%d" % i)
    d, off, width = segs[-1]
    dh0, _, d_norm1 = _rms_bwd(hpad, wt["norm1"], _Producer(d, wmt[off:off + width], dhn1), dh1, pad,
                               "d_hn1_rms1_bwd")

    grads = {
        "norm1": d_norm1, "w_in_t": d_w_in_t, "gdn_conv_w": conv_rows[:GDN_CONV],
        "a_log": gp_rows[0, :GDN_H], "dt_bias": gp_rows[1, :GDN_H], "gdn_norm": d_gnorm, "w_out": d_w_out,
        "norm2": d_norm2, "w_up_t": d_w_up_t, "ffn_conv_w": ffn_rows[:FFN_CONV],
        "ffn_conv_b": ffn_rows[FFN_CONV:FFN_CONV + 1], "w_down": d_w_down, "norm_f": d_norm_f,
    }
    return lossvec, dh0, grads


def _peer(k):
    ix, iy, ic = lax.axis_index("x"), lax.axis_index("y"), lax.axis_index("c")
    px = 1 - ix if (k >> 2) & 1 else ix
    py = 1 - iy if (k >> 1) & 1 else iy
    pc = 1 - ic if k & 1 else ic
    return (px, py, pc), 4 * px + 2 * py + pc


def _comm_call(body, n, out_shapes, name, args):
    hbm = pl.BlockSpec(memory_space=pl.ANY)
    return pl.pallas_call(
        body, out_shape=out_shapes, in_specs=[hbm] * n, out_specs=[hbm] * n,
        scratch_shapes=[pltpu.SemaphoreType.DMA((n, N_DEV - 1)), pltpu.SemaphoreType.DMA((n, N_DEV - 1)),
                        pltpu.SemaphoreType.DMA((n,))],
        name=name)(*args)


def _all_gather(xs, name):
    n = len(xs)

    def body(*refs):
        x_refs, out_refs = refs[:n], refs[n:2 * n]
        send_sems, recv_sems, local_sems = refs[2 * n:]
        _, me = _peer(0)
        pending = []
        for i in range(n):
            local = pltpu.make_async_copy(x_refs[i], out_refs[i].at[me], local_sems.at[i])
            local.start()
            pending.append(local)
        sends = []
        for i in range(n):
            for k in range(1, N_DEV):
                dev, _ = _peer(k)
                cp = pltpu.make_async_remote_copy(
                    src_ref=x_refs[i], dst_ref=out_refs[i].at[me], send_sem=send_sems.at[i, k - 1],
                    recv_sem=recv_sems.at[i, k - 1], device_id=dev, device_id_type=MESH_T)
                cp.start()
                sends.append(cp)
        for i in range(n):
            for k in range(1, N_DEV):
                dev, idx = _peer(k)
                pltpu.make_async_remote_copy(
                    src_ref=x_refs[i], dst_ref=out_refs[i].at[idx], send_sem=send_sems.at[i, k - 1],
                    recv_sem=recv_sems.at[i, k - 1], device_id=dev, device_id_type=MESH_T).wait_recv()
        for cp in sends:
            cp.wait_send()
        for local in pending:
            local.wait()

    out_shapes = [jax.ShapeDtypeStruct((N_DEV,) + a.shape, a.dtype) for a in xs]
    return _comm_call(body, n, out_shapes, name, xs)


def _all_to_all(gs, name):
    n = len(gs)

    def body(*refs):
        g_refs, out_refs = refs[:n], refs[n:2 * n]
        send_sems, recv_sems, local_sems = refs[2 * n:]
        _, me = _peer(0)
        pending = []
        for i in range(n):
            local = pltpu.make_async_copy(g_refs[i].at[me], out_refs[i].at[0], local_sems.at[i])
            local.start()
            pending.append(local)
        sends = []
        for i in range(n):
            for k in range(1, N_DEV):
                dev, idx = _peer(k)
                cp = pltpu.make_async_remote_copy(
                    src_ref=g_refs[i].at[idx], dst_ref=out_refs[i].at[k], send_sem=send_sems.at[i, k - 1],
                    recv_sem=recv_sems.at[i, k - 1], device_id=dev, device_id_type=MESH_T)
                cp.start()
                sends.append(cp)
        for cp in sends:
            cp.wait_recv()
        for cp in sends:
            cp.wait_send()
        for local in pending:
            local.wait()

    out_shapes = [jax.ShapeDtypeStruct(g.shape, g.dtype) for g in gs]
    return _comm_call(body, n, out_shapes, name, gs)


_SPLIT_RELATIONS = {"gather": tuple(range(1, N_DEV)), "a2a": tuple(range(1, N_DEV)), "chip": (1, 2, 4, 6),
                    "forward": (2, 4, 6)}


def _split_copies(kind, src_refs, land_refs, send_sems, recv_sems, local_sems, with_recv):
    n = len(land_refs)
    rels = _SPLIT_RELATIONS[kind]
    _, me = _peer(0)
    locals_, remotes = [], []
    for i in range(n):
        if kind in ("gather", "chip"):
            locals_.append(pltpu.make_async_copy(src_refs[i], land_refs[i].at[me], local_sems.at[i]))
        elif kind == "a2a":
            locals_.append(pltpu.make_async_copy(src_refs[i].at[me], land_refs[i].at[0], local_sems.at[i]))
        for jj, k in enumerate(rels):
            dev, idx = _peer(k)
            if kind in ("gather", "chip"):
                src, dst, mine = src_refs[i], land_refs[i].at[me], land_refs[i].at[idx]
            elif kind == "a2a":
                src, dst, mine = src_refs[i].at[idx], land_refs[i].at[k], land_refs[i].at[k]
            else:
                dev, _ = _peer(1)
                _, came = _peer(k + 1)
                src, dst, mine = land_refs[i].at[idx], land_refs[i].at[idx], land_refs[i].at[came]
            j = i * len(rels) + jj
            send = pltpu.make_async_remote_copy(
                src_ref=src, dst_ref=dst, send_sem=send_sems.at[j], recv_sem=recv_sems.at[j],
                device_id=dev, device_id_type=MESH_T)
            recv = pltpu.make_async_remote_copy(
                src_ref=src, dst_ref=mine, send_sem=send_sems.at[j], recv_sem=recv_sems.at[j],
                device_id=dev, device_id_type=MESH_T) if with_recv else None
            remotes.append((send, recv))
    return locals_, remotes


_HBM = pl.BlockSpec(memory_space=pltpu.HBM)
_SEM = pl.BlockSpec(memory_space=pltpu.SEMAPHORE)
_ANY = pl.BlockSpec(memory_space=pl.ANY)


def _split_start(srcs, kind, name, after):
    n = len(srcs)
    if kind == "forward":
        arrays = list(srcs)
    else:
        gathers = kind in ("gather", "chip")
        arrays = list(srcs) + [lax.empty(((N_DEV,) + a.shape) if gathers else a.shape, a.dtype) for a in srcs]
    na = len(arrays)

    def body(*refs):
        src_refs, land_refs = refs[:n], refs[na - n:na]
        send_sems, recv_sems, local_sems = refs[na + 1:na + 4]
        token = refs[-1]
        locals_, remotes = _split_copies(kind, src_refs, land_refs, send_sems, recv_sems, local_sems, False)
        for cp in locals_:
            cp.start()
        for send, _ in remotes:
            send.start()
        token[...] = jnp.zeros_like(token)

    ncp = n * len(_SPLIT_RELATIONS[kind])
    sems = (pltpu.SemaphoreType.DMA((ncp,)), pltpu.SemaphoreType.DMA((ncp,)), pltpu.SemaphoreType.DMA((n,)))
    thru = tuple(pltpu.HBM(a.shape, a.dtype) for a in arrays)
    outs = pl.pallas_call(
        body, name=name,
        out_shape=sems + thru + (jax.ShapeDtypeStruct((8, LANES), F32),),
        in_specs=[_HBM] * na + [_ANY],
        out_specs=[_SEM] * 3 + [_HBM] * na + [pl.BlockSpec(memory_space=pltpu.VMEM)],
        input_output_aliases={i: 3 + i for i in range(na)},
        compiler_params=pltpu.CompilerParams(has_side_effects=pltpu.SideEffectType.DATAFLOW_SIDE_EFFECTING),
    )(*[pltpu.with_memory_space_constraint(a, pltpu.HBM) for a in arrays], after)
    return (kind, n, outs[:3], outs[3:3 + na]), outs[-1]


def _split_wait(handle, name, after):
    kind, n, sems, thru = handle
    na = len(thru)

    def body(*refs):
        src_refs, land_refs = refs[:n], refs[na - n:na]
        send_sems, recv_sems, local_sems = refs[na:na + 3]
        locals_, remotes = _split_copies(kind, src_refs, land_refs, send_sems, recv_sems, local_sems, True)
        for send, recv in remotes:
            send.wait_send()
            recv.wait_recv()
        for cp in locals_:
            cp.wait()

    outs = pl.pallas_call(
        body, name=name, out_shape=tuple(pltpu.HBM(a.shape, a.dtype) for a in thru),
        in_specs=[_HBM] * na + [_SEM] * 3 + [_ANY], out_specs=[_HBM] * na,
        input_output_aliases={i: i for i in range(na)},
        compiler_params=pltpu.CompilerParams(has_side_effects=pltpu.SideEffectType.DATAFLOW_SIDE_EFFECTING),
    )(*thru, *sems, after)
    return list(outs[na - n:])


def _adamw(gslabs, w, m, v, name):
    R, Cw = w.shape
    if R % 8 == 0:
        tr, tc = _tile(R, 64 if Cw > 1024 else 128, 8), Cw
    else:
        tr, tc = R, LANES
    c1 = 1.0 - ADAM_B1 ** ADAM_STEP
    c2 = 1.0 - ADAM_B2 ** ADAM_STEP

    def body(g_ref, w_ref, m_ref, v_ref, go_ref, d_ref, mo_ref, vo_ref):
        g = g_ref[0].astype(F32)
        for k in range(1, N_DEV):
            g = g + g_ref[k].astype(F32)
        mn = ADAM_B1 * m_ref[...] + (1.0 - ADAM_B1) * g
        vn = ADAM_B2 * v_ref[...] + (1.0 - ADAM_B2) * (g * g)
        m_hat = mn / c1
        v_hat = vn / c2
        go_ref[...] = g
        d_ref[...] = -ADAM_LR * (m_hat / (jnp.sqrt(v_hat) + ADAM_EPS) + ADAM_WD * w_ref[...])
        mo_ref[...] = mn
        vo_ref[...] = vn

    blk = pl.BlockSpec((tr, tc), lambda i, j: (i, j))
    return pl.pallas_call(
        body, grid=(R // tr, Cw // tc),
        in_specs=[pl.BlockSpec((N_DEV, tr, tc), lambda i, j: (0, i, j)), blk, blk, blk],
        out_specs=[blk] * 4, out_shape=[jax.ShapeDtypeStruct((R, Cw), F32)] * 4, name=name)(gslabs, w, m, v)


def _pack(arrs, row_mult, dtype=F32):
    parts = []
    total = 0
    for a in arrs:
        f = a.reshape(-1).astype(dtype)
        n = -(-f.shape[0] // 1024) * 1024
        parts.append(jnp.pad(f, (0, n - f.shape[0])))
        total += n
    rows = total // LANES
    rows_p = -(-rows // row_mult) * row_mult
    flat = jnp.concatenate(parts)
    flat = jnp.pad(flat, (0, rows_p * LANES - total))
    return flat.reshape(rows_p, LANES)


def _unpack(packed, shapes):
    lead = packed.shape[:-2]
    flat = packed.reshape(lead + (-1,))
    out = []
    off = 0
    for s in shapes:
        n = int(np.prod(s))
        out.append(flat[..., off:off + n].reshape(lead + tuple(s)))
        off += -(-n // 1024) * 1024
    return out


def _gather_cols(stacked):
    d, r, c = stacked.shape
    return stacked.transpose(1, 0, 2).reshape(r, d * c)


def _scatter_cols(full):
    r, n = full.shape
    return full.reshape(r, N_DEV, n // N_DEV).transpose(1, 0, 2)


def kernel(x, meta, norm1, w_in, gdn_conv_w, gdn_a_log, gdn_dt_bias, gdn_norm, w_out, norm2, w_ffn_up, ffn_conv_w, ffn_conv_b, w_ffn_down, norm_f, loss_target, m_meta, m_norm1, m_w_in, m_gdn_conv_w, m_gdn_a_log, m_gdn_dt_bias, m_gdn_norm, m_w_out, m_norm2, m_w_ffn_up, m_ffn_conv_w, m_ffn_conv_b, m_w_ffn_down, m_norm_f, v_meta, v_norm1, v_w_in, v_gdn_conv_w, v_gdn_a_log, v_gdn_dt_bias, v_gdn_norm, v_w_out, v_norm2, v_w_ffn_up, v_ffn_conv_w, v_ffn_conv_b, v_w_ffn_down, v_norm_f):
    S = x.shape[1]
    L = N_META + S
    pad = (-L) % CHUNK
    Lp = L + pad

    tr_ = lambda a: jnp.swapaxes(a[0], 0, 1)
    big = [tr_(w_in), w_out[0], tr_(w_ffn_up), w_ffn_down[0]]
    small = [meta, gdn_conv_w, ffn_conv_w]
    small_all, = _all_gather([_pack(small, 8)], "gather_small_weights")
    first, first_token = _split_start([big[0].astype(BF16)], "chip", "gather_w_in_start", small_all)
    late, late_token = _split_start([a.astype(BF16) for a in big[1:]], "gather", "gather_late_start", first_token)

    def first_weights(after):
        half = _split_wait(first, "gather_w_in_wait", after)
        second, second_token = _split_start(half, "forward", "gather_w_in_forward_start", after)
        w_in_s, = _split_wait(second, "gather_w_in_forward_wait", second_token)
        w_in_t = w_in_s.reshape(_O_END, D_MODEL)
        w_main_t = jnp.concatenate([w_in_t[_O_GQ:_O_GZ], w_in_t[_O_RQ:_O_RG], w_in_t[_O_GZ:_O_GA],
                                    w_in_t[_O_RG:_O_END]], axis=0)
        return {"w_main_t": w_main_t, "w_small_t": jnp.pad(w_in_t[_O_GA:_O_RQ], ((0, LANES - 2 * GDN_H), (0, 0)))}

    def late_weights(after):
        w_out_s, w_up_s, w_down_s = _split_wait(late, "gather_late_wait", after)
        return {"w_out": w_out_s.reshape(D_MODEL, D_MODEL), "w_up_t": w_up_s.reshape(2 * D_FF, D_MODEL),
                "w_down": w_down_s.reshape(D_FF, D_MODEL)}

    meta_s, gconv_s, fconv_s = _unpack(small_all, [a.shape for a in small])
    wt = {
        "norm1": norm1 + jnp.tile(late_token[0:1, :], (1, D_MODEL // LANES)),
        "gdn_conv_w": _gather_cols(gconv_s[:, 0]), "a_log": gdn_a_log[0], "dt_bias": gdn_dt_bias[0],
        "gdn_norm": gdn_norm, "norm2": norm2, "ffn_conv_w": _gather_cols(fconv_s[:, 0]), "ffn_conv_b": ffn_conv_b,
        "norm_f": norm_f.reshape(1, D_MODEL),
    }
    meta_f = _gather_cols(meta_s)

    pending = {}

    def on_ffn_out_grads(d_w_down, d_w_up_t, d_w_out):
        srcs = [d_w_out.reshape(N_DEV, D_MODEL // N_DEV, D_MODEL), d_w_up_t.reshape(N_DEV, 2 * D_FF // N_DEV, D_MODEL),
                d_w_down.reshape(N_DEV, D_FF // N_DEV, D_MODEL)]
        pending["ffn_out"], token = _split_start(srcs, "a2a", "exchange_ffn_out_start", d_w_out)
        return token

    def on_w_in_grads(d_w_in_t):
        slabs = d_w_in_t.astype(BF16).reshape(N_DEV, _O_END // N_DEV, D_MODEL)
        pending["w_in"], token = _split_start([slabs], "a2a", "exchange_w_in_start", d_w_in_t)
        return token

    head = jnp.concatenate([jnp.zeros((pad, D_MODEL), F32), meta_f], axis=0)
    if S >= 2 * 704:
        hpad = _Rows(x[0], pad + N_META, head)
        tgt = _Rows(loss_target[0], pad + N_META)
    else:
        hpad = jnp.concatenate([head, x[0]], axis=0)
        tgt = jnp.concatenate([jnp.zeros((pad + N_META, D_MODEL), F32), loss_target[0]], axis=0)
    lossvec, dh0, gr = _local_step(hpad, tgt, pad, wt, first_weights, late_weights, on_ffn_out_grads, on_w_in_grads)

    loss = lax.psum(jnp.sum(lossvec), ("x", "y", "c"))
    grad_x = dh0[pad + N_META:][None]

    big_m = [tr_(m_w_in), m_w_out[0], tr_(m_w_ffn_up), m_w_ffn_down[0]]
    big_v = [tr_(v_w_in), v_w_out[0], tr_(v_w_ffn_up), v_w_ffn_down[0]]
    slabs_ffn_out = _split_wait(pending["ffn_out"], "exchange_ffn_out_wait", dh0)
    big_out = [None] + [_adamw(slabs_ffn_out[i - 1], big[i], big_m[i], big_v[i], "adamw_big_%d" % i)
                        for i in range(1, len(big))]
    g_sm = [_scatter_cols(dh0[pad:pad + N_META]), _scatter_cols(gr["gdn_conv_w"]), _scatter_cols(gr["ffn_conv_w"])]
    g_small = jnp.stack([_pack([g[d] for g in g_sm], 8) for d in range(N_DEV)])
    slabs_small, = _all_to_all([g_small], "exchange_small_gradients")
    small_out = _adamw(slabs_small, _pack(small, 8), _pack([m_meta, m_gdn_conv_w, m_ffn_conv_w], 8),
                       _pack([v_meta, v_gdn_conv_w, v_ffn_conv_w], 8), "adamw_small_sharded")
    small_un = [_unpack(o, [a.shape for a in small]) for o in small_out]
    rep_w = [norm1, gdn_a_log, gdn_dt_bias, gdn_norm, norm2, ffn_conv_b, norm_f]
    rep_m = [m_norm1, m_gdn_a_log, m_gdn_dt_bias, m_gdn_norm, m_norm2, m_ffn_conv_b, m_norm_f]
    rep_v = [v_norm1, v_gdn_a_log, v_gdn_dt_bias, v_gdn_norm, v_norm2, v_ffn_conv_b, v_norm_f]
    rep_g = [gr["norm1"], gr["a_log"], gr["dt_bias"], gr["gdn_norm"], gr["norm2"], gr["ffn_conv_b"], gr["norm_f"]]
    rep_slabs, = _all_gather([_pack(rep_g, 8)], "gather_small_gradients")
    rep_out = _adamw(rep_slabs, _pack(rep_w, 8), _pack(rep_m, 8), _pack(rep_v, 8), "adamw_replicated")
    rep_shapes = [a.shape for a in rep_w]
    rp_g, rp_d, rp_nm, rp_nv = [_unpack(o, rep_shapes) for o in rep_out]

    slabs_w_in, = _split_wait(pending["w_in"], "exchange_w_in_wait", rep_out[0])
    big_out[0] = _adamw(slabs_w_in, big[0], big_m[0], big_v[0], "adamw_big_0")
    back = lambda a: jnp.swapaxes(a, 0, 1)[None]
    sh_g, sh_d, sh_nm, sh_nv = [
        [small_un[j][0], back(big_out[0][j]), small_un[j][1], big_out[1][j][None], back(big_out[2][j]),
         small_un[j][2], big_out[3][j][None]] for j in range(4)]

    def order(sh, rp):
        return [sh[0], rp[0], sh[1], sh[2], rp[1], rp[2], rp[3], sh[3], rp[4], sh[4], sh[5], rp[5], sh[6], rp[6]]

    return (loss, grad_x, *order(sh_g, rp_g), *order(sh_d, rp_d), *order(sh_nm, rp_nm), *order(sh_nv, rp_nv))
```

```python
import functools
import math

import numpy as np
import jax
import jax.numpy as jnp
from jax import lax
from jax.experimental import pallas as pl
from jax.experimental.pallas import tpu as pltpu

F32 = jnp.float32
BF16 = jnp.bfloat16
HI = lax.Precision.HIGHEST

D_MODEL = 1024
N_META = 16
CHUNK = 64
GDN_H = 8
GDN_D = 128
RET_H = 4
RET_D = 256
D_FF = 2816
GDN_CONV = 4
FFN_CONV = 3
ROPE_BASE = 10000.0
EPS = 1e-6
N_DEV = 8
LANES = 128
MAIN_W = 10 * 1024
_O_GQ, _O_GZ, _O_GA, _O_RQ, _O_RG, _O_GATE, _O_END = 0, 3072, 4096, 4112, 7184, 8208, 10256

ADAM_LR = 0.001
ADAM_B1 = 0.9
ADAM_B2 = 0.999
ADAM_EPS = 1e-08
ADAM_WD = 0.01
ADAM_STEP = 10

MESH_T = pl.DeviceIdType.MESH


def _tile(n, target, mult):
    best = None
    for d in range(mult, min(n, target) + 1, mult):
        if n % d == 0:
            best = d
    assert best is not None, (n, target, mult)
    return best


def _sig(x):
    return 1.0 / (1.0 + jnp.exp(-x))


def _d(a, b):
    return jnp.dot(a.astype(BF16), b.astype(BF16), preferred_element_type=F32)


def _dnt(a, b):
    return lax.dot_general(a.astype(BF16), b.astype(BF16), (((1,), (1,)), ((), ())), preferred_element_type=F32)


def _dtn(a, b):
    return lax.dot_general(a.astype(BF16), b.astype(BF16), (((0,), (0,)), ((), ())), preferred_element_type=F32)


def _dx(a, b):
    return jnp.dot(a, b, preferred_element_type=F32, precision=HI)


def _dxnt(a, b):
    return lax.dot_general(a, b, (((1,), (1,)), ((), ())), preferred_element_type=F32, precision=HI)


def _dxtn(a, b):
    return lax.dot_general(a, b, (((0,), (0,)), ((), ())), preferred_element_type=F32, precision=HI)


def _split(a):
    hi = a.astype(BF16)
    return hi, (a - hi.astype(F32)).astype(BF16)


def _d3g(a, b, dims):
    ah, al = _split(a)
    bh, bl = _split(b)
    f = functools.partial(lax.dot_general, dimension_numbers=dims, preferred_element_type=F32)
    if dims == _NN:
        rows = a.shape[0]
        both = f(jnp.concatenate([ah, al], axis=0), bh)
        return both[:rows] + (f(ah, bl) + both[rows:])
    return f(ah, bh) + (f(ah, bl) + f(al, bh))


_NN = (((1,), (0,)), ((), ()))
_NT = (((1,), (1,)), ((), ()))
_TN = (((0,), (0,)), ((), ()))


def _rowsum(x):
    return jnp.sum(x, axis=1, keepdims=True)


def _allsum(x):
    return jnp.sum(jnp.sum(x, axis=1, keepdims=True), axis=0, keepdims=True)


def _mm_nn(a, b, res=None, out_dtype=F32, bt=False, tm_target=704, name="mm_nn"):
    M, K = a.shape
    N = b.shape[0] if bt else b.shape[1]
    tm = _tile(M, tm_target, 16)
    tn = _tile(N, 2816, 128)

    def body(*refs):
        if res is None:
            a_ref, b_ref, o_ref = refs
        else:
            a_ref, b_ref, r_ref, o_ref = refs
        acc = lax.dot_general(a_ref[...], b_ref[...], _NT if bt else _NN, preferred_element_type=F32)
        if res is not None:
            acc = acc + r_ref[...]
        o_ref[...] = acc.astype(out_dtype)

    b_spec = pl.BlockSpec((tn, K), lambda j, i: (j, 0)) if bt else pl.BlockSpec((K, tn), lambda j, i: (0, j))
    in_specs = [pl.BlockSpec((tm, K), lambda j, i: (i, 0)), b_spec]
    args = [a, b]
    if res is not None:
        in_specs.append(pl.BlockSpec((tm, tn), lambda j, i: (i, j)))
        args.append(res)
    return pl.pallas_call(
        body, grid=(N // tn, M // tm), in_specs=in_specs,
        out_specs=pl.BlockSpec((tm, tn), lambda j, i: (i, j)),
        out_shape=jax.ShapeDtypeStruct((M, N), out_dtype), name=name)(*args)


def _mm_sum(pairs, name):
    M = pairs[0][0].shape[0]
    N = pairs[0][1].shape[1]
    tm = _tile(M, 704, 16)
    n = len(pairs)

    def body(*refs):
        o_ref = refs[-1]
        acc = jnp.dot(refs[0][...], refs[1][...], preferred_element_type=F32)
        for i in range(1, n):
            acc = acc + jnp.dot(refs[2 * i][...], refs[2 * i + 1][...], preferred_element_type=F32)
        o_ref[...] = acc

    specs, args = [], []
    for a, b in pairs:
        specs += [pl.BlockSpec((tm, a.shape[1]), lambda i: (i, 0)),
                  pl.BlockSpec(b.shape, lambda i: (0, 0), pipeline_mode=pl.Buffered(1))]
        args += [a, b]
    return pl.pallas_call(
        body, grid=(M // tm,), in_specs=specs, out_specs=pl.BlockSpec((tm, N), lambda i: (i, 0)),
        out_shape=jax.ShapeDtypeStruct((M, N), F32), name=name)(*args)


def _mm_nt(a, b, res=None, name="mm_nt"):
    M, Nc = a.shape
    K = b.shape[0]
    tm = _tile(M, 704, 16)
    tc = _tile(Nc, 5632, 128)

    def body(*refs):
        if res is None:
            a_ref, b_ref, o_ref = refs
        else:
            a_ref, b_ref, r_ref, o_ref = refs
        c = pl.program_id(1)
        p = lax.dot_general(a_ref[...], b_ref[...], (((1,), (1,)), ((), ())), preferred_element_type=F32)

        @pl.when(c == 0)
        def _():
            if res is None:
                o_ref[...] = p
            else:
                o_ref[...] = p + r_ref[...]

        @pl.when(c > 0)
        def _():
            o_ref[...] += p

    in_specs = [pl.BlockSpec((tm, tc), lambda i, c: (i, c)), pl.BlockSpec((K, tc), lambda i, c: (0, c))]
    args = [a, b]
    if res is not None:
        in_specs.append(pl.BlockSpec((tm, K), lambda i, c: (i, 0)))
        args.append(res)
    return pl.pallas_call(
        body, grid=(M // tm, Nc // tc), in_specs=in_specs,
        out_specs=pl.BlockSpec((tm, K), lambda i, c: (i, 0)),
        out_shape=jax.ShapeDtypeStruct((M, K), F32), name=name)(*args)


def _mm_tn(a, b, out_dtype=F32, name="mm_tn"):
    M, K = a.shape
    N = b.shape[1]
    tm = _tile(M, 2752, 16)
    tk = _tile(K, 1408, 128)
    tn = _tile(N, 1408, 128)
    steps = M // tm

    def body(a_ref, b_ref, o_ref, *scratch):
        acc = scratch[0] if scratch else o_ref
        m = pl.program_id(2)
        p = lax.dot_general(a_ref[...], b_ref[...], (((0,), (0,)), ((), ())), preferred_element_type=F32)

        @pl.when(m == 0)
        def _():
            acc[...] = p

        @pl.when(m > 0)
        def _():
            acc[...] += p

        if scratch:
            @pl.when(m == steps - 1)
            def _():
                o_ref[...] = acc[...].astype(out_dtype)

    return pl.pallas_call(
        body, grid=(K // tk, N // tn, steps),
        in_specs=[pl.BlockSpec((tm, tk), lambda kk, j, m: (m, kk)), pl.BlockSpec((tm, tn), lambda kk, j, m: (m, j))],
        out_specs=pl.BlockSpec((tk, tn), lambda kk, j, m: (kk, j)),
        out_shape=jax.ShapeDtypeStruct((K, N), out_dtype),
        scratch_shapes=[] if out_dtype == F32 else [pltpu.VMEM((tk, tn), F32)], name=name)(a, b)


class _Rows:
    def __init__(self, body, first, head=None):
        self.body, self.first, self.head = body, first, head
        self.shape = (body.shape[0] + first, body.shape[1])


def _rows_operands(x, tr):
    if not isinstance(x, _Rows):
        return [x], [pl.BlockSpec((tr, x.shape[1]), lambda i: (i, 0))]
    assert x.first % 8 == 0 and x.first <= tr <= x.body.shape[0] and x.shape[0] % tr == 0
    width = x.shape[1]
    args = [x.body]
    specs = [pl.BlockSpec((pl.Element(tr), pl.Element(width)),
                          lambda i: (pl.multiple_of(jnp.maximum(i * tr - x.first, 0), 8), 0))]
    if x.head is not None:
        args.append(jnp.pad(x.head, ((0, tr - x.first), (0, 0))))
        specs.append(pl.BlockSpec((tr, width), lambda i: (0, 0)))
    return args, specs


def _rows_tile(x, refs, i, tr):
    blk = refs[0][...]
    if not isinstance(x, _Rows):
        return blk
    shifted = pltpu.roll(blk, x.first, 0)
    if x.head is not None:
        row = lax.broadcasted_iota(jnp.int32, (tr, 1), 0)
        shifted = jnp.where(row < x.first, refs[1][...], shifted)
    return jnp.where(i == 0, shifted, blk)


def _rms_fwd(x, g, name):
    Lp = x.shape[0]
    tr = _tile(Lp, 256, 16)
    args, specs = _rows_operands(x, tr)
    n = len(args)

    def body(*refs):
        g_ref, o_ref = refs[n:]
        xv = _rows_tile(x, refs[:n], pl.program_id(0), tr)
        r = lax.rsqrt(jnp.mean(xv * xv, axis=-1, keepdims=True) + EPS)
        o_ref[...] = (xv * r * g_ref[...]).astype(BF16)

    return pl.pallas_call(
        body, grid=(Lp // tr,),
        in_specs=specs + [pl.BlockSpec((1, D_MODEL), lambda i: (0, 0))],
        out_specs=pl.BlockSpec((tr, D_MODEL), lambda i: (i, 0)),
        out_shape=jax.ShapeDtypeStruct((Lp, D_MODEL), BF16), name=name)(*args, g)


class _Producer:
    def __init__(self, a, b, res=None):
        self.a, self.b, self.res = a, b, res
        self.tr = _tile(a.shape[0], 704, 16)
        K = a.shape[1]
        r_args, r_specs = ([], []) if res is None else _rows_operands(res, self.tr)
        self.args = [a, b] + r_args
        self.specs = [pl.BlockSpec((self.tr, K), lambda i: (i, 0)),
                      pl.BlockSpec((K, D_MODEL), lambda i: (0, 0), pipeline_mode=pl.Buffered(1))] + r_specs

    def tile(self, refs, i):
        acc = jnp.dot(refs[0][...], refs[1][...], preferred_element_type=F32)
        return acc if self.res is None else acc + _rows_tile(self.res, refs[2:], i, self.tr)


def _mm_rms_fwd(prod, g, name):
    Lp, tr, n = prod.a.shape[0], prod.tr, len(prod.args)

    def body(*refs):
        g_ref, x_ref, o_ref = refs[n:]
        xv = prod.tile(refs[:n], pl.program_id(0))
        r = lax.rsqrt(jnp.mean(xv * xv, axis=-1, keepdims=True) + EPS)
        x_ref[...] = xv
        o_ref[...] = (xv * r * g_ref[...]).astype(BF16)

    blk = pl.BlockSpec((tr, D_MODEL), lambda i: (i, 0))
    return pl.pallas_call(
        body, grid=(Lp // tr,), in_specs=prod.specs + [pl.BlockSpec((1, D_MODEL), lambda i: (0, 0))],
        out_specs=[blk, blk],
        out_shape=[jax.ShapeDtypeStruct((Lp, D_MODEL), F32), jax.ShapeDtypeStruct((Lp, D_MODEL), BF16)],
        name=name)(*prod.args, g)


def _rms_bwd(x, g, dy, dres, pad, name):
    Lp = x.shape[0]
    fused = isinstance(dy, _Producer)
    tr = dy.tr if fused else _tile(Lp, 256, 16)
    n = len(dy.args) if fused else 1
    x_args, x_specs = _rows_operands(x, tr)
    nx = len(x_args)

    def body(*refs):
        g_ref, dr_ref, dx_ref, dxb_ref, dg_ref = refs[n + nx:]
        i = pl.program_id(0)
        xv = _rows_tile(x, refs[n:n + nx], i, tr)
        r = lax.rsqrt(jnp.mean(xv * xv, axis=-1, keepdims=True) + EPS)
        xh = xv * r
        dyv = dy.tile(refs[:n], i) if fused else refs[0][...]
        dxh = dyv * g_ref[...]
        dx = r * (dxh - xh * jnp.mean(dxh * xh, axis=-1, keepdims=True)) + dr_ref[...]
        row = i * tr + lax.broadcasted_iota(jnp.int32, (tr, 1), 0)
        dx = jnp.where(row >= pad, dx, 0.0)
        dx_ref[...] = dx
        dxb_ref[...] = dx.astype(BF16)
        part = jnp.sum(dyv * xh, axis=0, keepdims=True)

        @pl.when(i == 0)
        def _():
            dg_ref[...] = part

        @pl.when(i > 0)
        def _():
            dg_ref[...] += part

    blk = pl.BlockSpec((tr, D_MODEL), lambda i: (i, 0))
    vec = pl.BlockSpec((1, D_MODEL), lambda i: (0, 0))
    return pl.pallas_call(
        body, grid=(Lp // tr,), in_specs=(dy.specs if fused else [blk]) + x_specs + [vec, blk],
        out_specs=[blk, blk, vec],
        out_shape=[jax.ShapeDtypeStruct((Lp, D_MODEL), F32), jax.ShapeDtypeStruct((Lp, D_MODEL), BF16),
                   jax.ShapeDtypeStruct((1, D_MODEL), F32)], name=name)(*(dy.args if fused else [dy]), *x_args, g, dres)


def _final(h2, g, tgt, first_row):
    fused = isinstance(h2, _Producer)
    Lp = h2.a.shape[0] if fused else h2.shape[0]
    tr = h2.tr if fused else _tile(Lp, 256, 16)
    n = len(h2.args) if fused else 1
    t_args, t_specs = _rows_operands(tgt, tr)
    nt = len(t_args)

    def body(*refs):
        g_ref = refs[n]
        loss_ref, dx_ref, dxb_ref, dg_ref = refs[n + 1 + nt:]
        i = pl.program_id(0)
        xv = h2.tile(refs[:n], i) if fused else refs[0][...]
        tv = _rows_tile(tgt, refs[n + 1:n + 1 + nt], i, tr)
        gv = g_ref[...]
        r = lax.rsqrt(jnp.mean(xv * xv, axis=-1, keepdims=True) + EPS)
        xh = xv * r
        row = i * tr + lax.broadcasted_iota(jnp.int32, (tr, 1), 0)
        err = jnp.where(row >= first_row, xh * gv - tv, 0.0)
        lpart = jnp.sum(err * err, axis=0, keepdims=True) * (0.5 / D_MODEL)
        dyv = err * (1.0 / D_MODEL)
        dxh = dyv * gv
        dx = r * (dxh - xh * jnp.mean(dxh * xh, axis=-1, keepdims=True))
        dx_ref[...] = dx
        dxb_ref[...] = dx.astype(BF16)
        part = jnp.sum(dyv * xh, axis=0, keepdims=True)

        @pl.when(i == 0)
        def _():
            dg_ref[...] = part
            loss_ref[...] = lpart

        @pl.when(i > 0)
        def _():
            dg_ref[...] += part
            loss_ref[...] += lpart

    blk = pl.BlockSpec((tr, D_MODEL), lambda i: (i, 0))
    vec = pl.BlockSpec((1, D_MODEL), lambda i: (0, 0))
    return pl.pallas_call(
        body, grid=(Lp // tr,), in_specs=(h2.specs if fused else [blk]) + [vec] + t_specs,
        out_specs=[vec, blk, blk, vec],
        out_shape=[jax.ShapeDtypeStruct((1, D_MODEL), F32), jax.ShapeDtypeStruct((Lp, D_MODEL), F32),
                   jax.ShapeDtypeStruct((Lp, D_MODEL), BF16), jax.ShapeDtypeStruct((1, D_MODEL), F32)],
        name="final_norm_loss")(*(h2.args if fused else [h2]), g, *t_args)


def _halo_prev(tr, width, col=0):
    return pl.BlockSpec((8, width), lambda i: (jnp.maximum(i * (tr // 8) - 1, 0), col))


def _halo_next(tr, width, nrows, col=0, rows=8):
    last = nrows // rows - 1
    return pl.BlockSpec((rows, width), lambda i: (jnp.minimum((i + 1) * (tr // rows), last), col))


def _shifted(x, offs):
    n = x.shape[0]
    return [x if off == 0 else pltpu.roll(x, n - off, 0) for off in offs]


def _taps(wins, w, rows, bias=None):
    acc = w[0:1, :] * wins[0][0:rows, :]
    if bias is not None:
        acc = acc + bias
    for kk in range(1, len(wins)):
        acc = acc + w[kk:kk + 1, :] * wins[kk][0:rows, :]
    return acc


def _gdn_pre(proj_m, proj_s, conv_w, gparams, pad):
    Lp = proj_m.shape[0]
    tr = _tile(Lp, 192, 64)
    W3 = 3 * D_MODEL

    def body(main_ref, prev_ref, s_ref, w_ref, gp_ref, qkv_ref, gsm_ref, c_ref):
        i = pl.program_id(0)
        prev = jnp.where(i > 0, prev_ref[...], 0.0)
        ext = jnp.concatenate([prev, main_ref[...]], axis=0)
        c = _taps(_shifted(ext, range(8 - (GDN_CONV - 1), 9)), w_ref[...], tr)
        c_ref[...] = c.astype(BF16)
        s = c * _sig(c)
        scale = GDN_D ** -0.5
        for j in range(2 * GDN_H):
            seg = s[:, j * GDN_D:(j + 1) * GDN_D]
            r = lax.rsqrt(_rowsum(seg * seg) + EPS)
            if j < GDN_H:
                r = r * scale
            qkv_ref[:, j * GDN_D:(j + 1) * GDN_D] = seg * r
        qkv_ref[:, 2 * D_MODEL:] = s[:, 2 * D_MODEL:]
        sm = s_ref[...]
        gp = gp_ref[...]
        lane = lax.broadcasted_iota(jnp.int32, sm.shape, 1)
        z = sm + gp[1:2, :]
        softplus = jnp.maximum(z, 0.0) + jnp.log(1.0 + jnp.exp(-jnp.abs(z)))
        lg = -jnp.exp(gp[0:1, :]) * softplus
        row = i * tr + lax.broadcasted_iota(jnp.int32, (tr, 1), 0)
        out = jnp.where(lane < GDN_H, lg, jnp.where(lane < 2 * GDN_H, _sig(sm), 0.0))
        gsm_ref[...] = jnp.where(row >= pad, out, 0.0)

    return pl.pallas_call(
        body, grid=(Lp // tr,),
        in_specs=[pl.BlockSpec((tr, W3), lambda i: (i, 0)), _halo_prev(tr, W3),
                  pl.BlockSpec((tr, LANES), lambda i: (i, 0)),
                  pl.BlockSpec((GDN_CONV, W3), lambda i: (0, 0)), pl.BlockSpec((8, LANES), lambda i: (0, 0))],
        out_specs=[pl.BlockSpec((tr, W3), lambda i: (i, 0)), pl.BlockSpec((tr, LANES), lambda i: (i, 0)),
                   pl.BlockSpec((tr, W3), lambda i: (i, 0))],
        out_shape=[jax.ShapeDtypeStruct((Lp, W3), F32), jax.ShapeDtypeStruct((Lp, LANES), F32),
                   jax.ShapeDtypeStruct((Lp, W3), BF16)],
        name="gdn_pre")(proj_m, proj_m, proj_s, conv_w, gparams)


def _gdn_pre_bwd(proj_m, conv_out, proj_s, conv_w, gparams, dq, dk, dv, dgs, pad):
    Lp = proj_m.shape[0]
    tr = _tile(Lp, 192, 64)
    W3 = 3 * D_MODEL
    te = tr + 8

    def body(main_ref, c_ref, cn_ref, s_ref, w_ref, gp_ref,
             dq_ref, dqn_ref, dk_ref, dkn_ref, dv_ref, dvn_ref, dgs_ref,
             da_ref, ds_ref, dw_ref, dgp_ref):
        i = pl.program_id(0)
        w = w_ref[...]
        c = jnp.concatenate([c_ref[...].astype(F32), cn_ref[...].astype(F32)[0:8]], axis=0)
        sg = _sig(c)
        s = c * sg
        rowe = i * tr + lax.broadcasted_iota(jnp.int32, (te, 1), 0)
        live = (rowe >= pad) & (rowe < Lp)
        dqe = jnp.concatenate([dq_ref[...], dqn_ref[...]], axis=0)
        dke = jnp.concatenate([dk_ref[...], dkn_ref[...]], axis=0)
        dve = jnp.concatenate([dv_ref[...], dvn_ref[...]], axis=0)
        scale = GDN_D ** -0.5
        parts = []
        for j in range(2 * GDN_H):
            seg = s[:, j * GDN_D:(j + 1) * GDN_D]
            r = lax.rsqrt(_rowsum(seg * seg) + EPS)
            xh = seg * r
            if j < GDN_H:
                dxh = dqe[:, j * GDN_D:(j + 1) * GDN_D] * scale
            else:
                dxh = dke[:, (j - GDN_H) * GDN_D:(j - GDN_H + 1) * GDN_D]
            parts.append(r * (dxh - xh * _rowsum(dxh * xh)))
        parts.append(dve)
        dsv = jnp.concatenate(parts, axis=1)
        dc = jnp.where(live, dsv * (sg * (1.0 + c * (1.0 - sg))), 0.0)
        dcs = _shifted(dc, range(GDN_CONV - 1, -1, -1))
        da_ref[...] = _taps(dcs, w, tr).astype(BF16)
        pm = main_ref[...]
        rows = [jnp.sum(dcs[kk][0:tr, :] * pm, axis=0, keepdims=True) for kk in range(GDN_CONV)]
        dwp = jnp.concatenate(rows + [jnp.zeros((8 - GDN_CONV, W3), F32)], axis=0)

        sm = s_ref[...]
        gp = gp_ref[...]
        lane = lax.broadcasted_iota(jnp.int32, sm.shape, 1)
        rowm = i * tr + lax.broadcasted_iota(jnp.int32, (tr, 1), 0)
        dgv = jnp.where(rowm >= pad, dgs_ref[...], 0.0)
        dlg = jnp.where(lane < GDN_H, dgv, 0.0)
        dbt = jnp.where((lane >= GDN_H) & (lane < 2 * GDN_H), dgv, 0.0)
        z = sm + gp[1:2, :]
        softplus = jnp.maximum(z, 0.0) + jnp.log(1.0 + jnp.exp(-jnp.abs(z)))
        ea = jnp.exp(gp[0:1, :])
        dz = dlg * (-ea) * _sig(z)
        dal = dlg * (-ea) * softplus
        bt = _sig(sm)
        dgb = dbt * bt * (1.0 - bt)
        ds_ref[...] = (dz + dgb).astype(BF16)
        gpp = jnp.concatenate([jnp.sum(dal, axis=0, keepdims=True), jnp.sum(dz, axis=0, keepdims=True),
                               jnp.zeros((6, LANES), F32)], axis=0)

        @pl.when(i == 0)
        def _():
            dw_ref[...] = dwp
            dgp_ref[...] = gpp

        @pl.when(i > 0)
        def _():
            dw_ref[...] += dwp
            dgp_ref[...] += gpp

    m3 = pl.BlockSpec((tr, W3), lambda i: (i, 0))
    m1 = pl.BlockSpec((tr, D_MODEL), lambda i: (i, 0))
    n1 = _halo_next(tr, D_MODEL, Lp)
    return pl.pallas_call(
        body, grid=(Lp // tr,),
        in_specs=[m3, m3, _halo_next(tr, W3, Lp, rows=16), pl.BlockSpec((tr, LANES), lambda i: (i, 0)),
                  pl.BlockSpec((GDN_CONV, W3), lambda i: (0, 0)), pl.BlockSpec((8, LANES), lambda i: (0, 0)),
                  m1, n1, m1, n1, m1, n1, pl.BlockSpec((tr, LANES), lambda i: (i, 0))],
        out_specs=[m3, pl.BlockSpec((tr, LANES), lambda i: (i, 0)),
                   pl.BlockSpec((8, W3), lambda i: (0, 0)), pl.BlockSpec((8, LANES), lambda i: (0, 0))],
        out_shape=[jax.ShapeDtypeStruct((Lp, W3), BF16), jax.ShapeDtypeStruct((Lp, LANES), BF16),
                   jax.ShapeDtypeStruct((8, W3), F32), jax.ShapeDtypeStruct((8, LANES), F32)],
        name="gdn_pre_bwd")(proj_m, conv_out, conv_out, proj_s, conv_w, gparams, dq, dq, dk, dk, dv, dv, dgs)


def _gdn_gates(gs):
    ri = lax.broadcasted_iota(jnp.int32, (CHUNK, CHUNK), 0)
    ci = lax.broadcasted_iota(jnp.int32, (CHUNK, CHUNK), 1)
    tril = ri >= ci
    strict = ri > ci
    gall = _dx(tril.astype(F32), gs)
    lane8 = lax.broadcasted_iota(jnp.int32, (8, LANES), 1)
    sub8 = lax.broadcasted_iota(jnp.int32, (8, LANES), 0)
    grow = _dxnt((lane8 == sub8).astype(F32), gall)
    return gall, grow, tril, strict


def _gdn_decay(gall, grow, tril, h):
    g = gall[:, h:h + 1]
    return g, jnp.where(tril, jnp.exp(jnp.where(tril, g - grow[h:h + 1, :], 0.0)), 0.0)


def _group(N):
    return 3 if N % 3 == 0 else (2 if N % 2 == 0 else 1)


def _gdn_chunk_specs(N, rev):
    G = _group(N)
    nb = N // G
    cn = (lambda n: nb - 1 - n) if rev else (lambda n: n)
    col = lambda j: pl.BlockSpec((G * CHUNK, D_MODEL), lambda n: (cn(n), j))
    gate = pl.BlockSpec((G * CHUNK, LANES), lambda n: (cn(n), 0))
    st = lambda a, b: pl.BlockSpec((GDN_H, G, a, b), lambda n: (0, cn(n), 0, 0))
    return G, nb, col, gate, st


def _gdn_chunk_fwd(qkv, gsm):
    Lp = qkv.shape[0]
    N = Lp // CHUNK
    G, nb, col, gate, st = _gdn_chunk_specs(N, False)

    def body(q_ref, k_ref, v_ref, gs_ref, o_ref, sin_ref, t_ref, S):
        n = pl.program_id(0)

        @pl.when(n == 0)
        def _():
            S[...] = jnp.zeros_like(S)

        ri = lax.broadcasted_iota(jnp.int32, (CHUNK, CHUNK), 0)
        ci = lax.broadcasted_iota(jnp.int32, (CHUNK, CHUNK), 1)
        eye = (ri == ci).astype(F32)
        heads = range(GDN_H)
        sls = [slice(h * GDN_D, (h + 1) * GDN_D) for h in heads]
        rows = [slice(c * CHUNK, (c + 1) * CHUNK) for c in range(G)]
        pairs = [(c, h) for c in range(G) for h in heads]
        P = lambda f: {p: f(*p) for p in pairs}
        gs = [gs_ref[rows[c], :] for c in range(G)]
        gates = [_gdn_gates(gs[c]) for c in range(G)]
        tril, strict = gates[0][2], gates[0][3]
        q = P(lambda c, h: q_ref[rows[c], sls[h]])
        k = P(lambda c, h: k_ref[rows[c], sls[h]])
        v = P(lambda c, h: v_ref[rows[c], sls[h]])
        beta = P(lambda c, h: gs[c][:, GDN_H + h:GDN_H + h + 1])
        gg = P(lambda c, h: _gdn_decay(gates[c][0], gates[c][1], tril, h))
        g = {p: x[0] for p, x in gg.items()}
        gam = {p: x[1] for p, x in gg.items()}
        eg = P(lambda c, h: jnp.exp(g[c, h]))
        gl = P(lambda c, h: g[c, h][CHUNK - 1:CHUNK, :])
        kb = P(lambda c, h: k[c, h] * beta[c, h])
        pw = P(lambda c, h: -jnp.where(strict, _dnt(kb[c, h], k[c, h]) * gam[c, h], 0.0))
        p = P(lambda c, h: _dnt(q[c, h], k[c, h]) * gam[c, h])
        t = P(lambda c, h: eye + pw[c, h])
        for _ in range(5):
            pw = P(lambda c, h: _d3g(pw[c, h], pw[c, h], _NN))
            t = P(lambda c, h: t[c, h] + _d3g(t[c, h], pw[c, h], _NN))
        u = P(lambda c, h: _d(t[c, h], v[c, h] * beta[c, h]))
        w = P(lambda c, h: _d(t[c, h], kb[c, h] * eg[c, h]))
        qg = P(lambda c, h: q[c, h] * eg[c, h])
        kd = P(lambda c, h: k[c, h] * jnp.exp(gl[c, h] - g[c, h]))
        egl = P(lambda c, h: jnp.exp(gl[c, h]))
        for c in range(G):
            for h in heads:
                t_ref[h, c] = t[c, h]
        cur = [S[h] for h in heads]
        for c in range(G):
            vnew = [u[c, h] - _d(w[c, h], cur[h]) for h in heads]
            for h in heads:
                o_ref[rows[c], sls[h]] = _d(qg[c, h], cur[h]) + _d(p[c, h], vnew[h])
                sin_ref[h, c] = cur[h]
            cur = [cur[h] * egl[c, h] + _dtn(kd[c, h], vnew[h]) for h in heads]
        for h in heads:
            S[h] = cur[h]

    return pl.pallas_call(
        body, grid=(nb,),
        in_specs=[col(0), col(1), col(2), gate],
        out_specs=[col(0), st(GDN_D, GDN_D), st(CHUNK, CHUNK)],
        out_shape=[jax.ShapeDtypeStruct((Lp, D_MODEL), F32), jax.ShapeDtypeStruct((GDN_H, N, GDN_D, GDN_D), F32),
                   jax.ShapeDtypeStruct((GDN_H, N, CHUNK, CHUNK), F32)],
        scratch_shapes=[pltpu.VMEM((GDN_H, GDN_D, GDN_D), F32)],
        name="gdn_chunk_fwd")(qkv, qkv, qkv, gsm)


def _gdn_chunk_bwd(qkv, gsm, do, s_in, t_in):
    Lp = qkv.shape[0]
    N = Lp // CHUNK
    G, nb, col, gate, st = _gdn_chunk_specs(N, True)

    def body(q_ref, k_ref, v_ref, gs_ref, do_ref, sin_ref, t_ref, dq_ref, dk_ref, dv_ref, dgs_ref, dS):
        n = pl.program_id(0)

        @pl.when(n == 0)
        def _():
            dS[...] = jnp.zeros_like(dS)

        lane = lax.broadcasted_iota(jnp.int32, (CHUNK, LANES), 1)
        rcol = lax.broadcasted_iota(jnp.int32, (CHUNK, 1), 0)
        ri = lax.broadcasted_iota(jnp.int32, (CHUNK, CHUNK), 0)
        ci = lax.broadcasted_iota(jnp.int32, (CHUNK, CHUNK), 1)
        ones = jnp.ones((CHUNK, LANES), F32)
        heads = range(GDN_H)
        sls = [slice(h * GDN_D, (h + 1) * GDN_D) for h in heads]
        rows = [slice(c * CHUNK, (c + 1) * CHUNK) for c in range(G)]
        pairs = [(c, h) for c in range(G) for h in heads]
        P = lambda f: {p: f(*p) for p in pairs}
        gs = [gs_ref[rows[c], :] for c in range(G)]
        gates = [_gdn_gates(gs[c]) for c in range(G)]
        tril, strict = gates[0][2], gates[0][3]
        q = P(lambda c, h: q_ref[rows[c], sls[h]])
        k = P(lambda c, h: k_ref[rows[c], sls[h]])
        v = P(lambda c, h: v_ref[rows[c], sls[h]])
        dov = P(lambda c, h: do_ref[rows[c], sls[h]])
        s0 = P(lambda c, h: sin_ref[h, c])
        t = P(lambda c, h: t_ref[h, c])
        beta = P(lambda c, h: gs[c][:, GDN_H + h:GDN_H + h + 1])
        gg = P(lambda c, h: _gdn_decay(gates[c][0], gates[c][1], tril, h))
        g = {p: x[0] for p, x in gg.items()}
        gam = {p: x[1] for p, x in gg.items()}
        eg = P(lambda c, h: jnp.exp(g[c, h]))
        egl = P(lambda c, h: jnp.exp(g[c, h][CHUNK - 1:CHUNK, :]))
        e = P(lambda c, h: jnp.exp(g[c, h][CHUNK - 1:CHUNK, :] - g[c, h]))
        kb = P(lambda c, h: k[c, h] * beta[c, h])
        kbg = P(lambda c, h: kb[c, h] * eg[c, h])
        vb = P(lambda c, h: v[c, h] * beta[c, h])
        qg = P(lambda c, h: q[c, h] * eg[c, h])
        kd = P(lambda c, h: k[c, h] * e[c, h])
        m = P(lambda c, h: jnp.where(strict, _dnt(kb[c, h], k[c, h]) * gam[c, h], 0.0))
        u = P(lambda c, h: _d(t[c, h], vb[c, h]))
        w = P(lambda c, h: _d(t[c, h], kbg[c, h]))
        p = P(lambda c, h: _dnt(q[c, h], k[c, h]) * gam[c, h])
        dqg = P(lambda c, h: _dnt(dov[c, h], s0[c, h]))
        qgdo = P(lambda c, h: _dtn(qg[c, h], dov[c, h]))
        ptdo = P(lambda c, h: _dtn(p[c, h], dov[c, h]))
        vnew = P(lambda c, h: u[c, h] - _d(w[c, h], s0[c, h]))
        dp = P(lambda c, h: jnp.where(tril, _dnt(dov[c, h], vnew[c, h]), 0.0))
        cur = [dS[h] for h in heads]
        dvnew, dkd, sds = {}, {}, {}
        for c in reversed(range(G)):
            for h in heads:
                dvnew[c, h] = ptdo[c, h] + _d(kd[c, h], cur[h])
                dkd[c, h] = _dnt(vnew[c, h], cur[h])
                sds[c, h] = _allsum(s0[c, h] * cur[h])
            cur = [qgdo[c, h] + egl[c, h] * cur[h] - _dtn(w[c, h], dvnew[c, h]) for h in heads]
        for h in heads:
            dS[h] = cur[h]
        dw = P(lambda c, h: -_dnt(dvnew[c, h], s0[c, h]))
        dvb = P(lambda c, h: _dtn(t[c, h], dvnew[c, h]))
        dkbg = P(lambda c, h: _dtn(t[c, h], dw[c, h]))
        dt = P(lambda c, h: _dnt(dvnew[c, h], vb[c, h]) + _dnt(dw[c, h], kbg[c, h]))
        x1 = P(lambda c, h: _d3g(t[c, h], dt[c, h], _TN))
        dm = P(lambda c, h: jnp.where(strict, -_d3g(x1[c, h], t[c, h], _NT), 0.0))
        dkk = P(lambda c, h: dm[c, h] * gam[c, h])
        dqk = P(lambda c, h: dp[c, h] * gam[c, h])
        dkb = P(lambda c, h: _d(dkk[c, h], k[c, h]) + eg[c, h] * dkbg[c, h])
        em = P(lambda c, h: dm[c, h] * m[c, h] + dp[c, h] * p[c, h])
        colsum = P(lambda c, h: _d3g(em[c, h], ones, _TN)[:, 0:1])
        for c, h in pairs:
            dk_ref[rows[c], sls[h]] = (_dtn(dkk[c, h], kb[c, h]) + _dtn(dqk[c, h], q[c, h]) + dkd[c, h] * e[c, h]
                                       + beta[c, h] * dkb[c, h])
            dq_ref[rows[c], sls[h]] = _d(dqk[c, h], k[c, h]) + dqg[c, h] * eg[c, h]
            dv_ref[rows[c], sls[h]] = beta[c, h] * dvb[c, h]
        for c in range(G):
            dg_all = jnp.zeros((CHUNK, LANES), F32)
            dbeta_all = jnp.zeros((CHUNK, LANES), F32)
            for h in heads:
                dbeta = _rowsum(k[c, h] * dkb[c, h]) + _rowsum(v[c, h] * dvb[c, h])
                z = _rowsum(kd[c, h] * dkd[c, h])
                dg = (_rowsum(em[c, h]) - colsum[c, h] + _rowsum(qg[c, h] * dqg[c, h]) + _rowsum(kbg[c, h] * dkbg[c, h])
                      - z)
                extra = _allsum(z) + egl[c, h] * sds[c, h]
                dg = dg + jnp.where(rcol == CHUNK - 1, extra, 0.0)
                dg_all = dg_all + jnp.where(lane == h, dg, 0.0)
                dbeta_all = dbeta_all + jnp.where(lane == GDN_H + h, dbeta, 0.0)
            dgs_ref[rows[c], :] = _dx((ci >= ri).astype(F32), dg_all) + dbeta_all

    return pl.pallas_call(
        body, grid=(nb,),
        in_specs=[col(0), col(1), col(2), gate, col(0), st(GDN_D, GDN_D), st(CHUNK, CHUNK)],
        out_specs=[col(0), col(0), col(0), gate],
        out_shape=[jax.ShapeDtypeStruct((Lp, D_MODEL), F32)] * 3 + [jax.ShapeDtypeStruct((Lp, LANES), F32)],
        scratch_shapes=[pltpu.VMEM((GDN_H, GDN_D, GDN_D), F32)],
        name="gdn_chunk_bwd")(qkv, qkv, qkv, gsm, do, s_in, t_in)


def _rot(x, c, s):
    half = RET_D // 2
    x1 = x[:, :half]
    x2 = x[:, half:]
    return jnp.concatenate([x1 * c - x2 * s, x2 * c + x1 * s], axis=1)


def _rot_bwd(d, c, s):
    half = RET_D // 2
    d1 = d[:, :half]
    d2 = d[:, half:]
    return jnp.concatenate([d1 * c + d2 * s, d2 * c - d1 * s], axis=1)


def _ret_tables():
    hh = jnp.arange(RET_H, dtype=F32)
    lg = jnp.log(1.0 - 2.0 ** (-5.0 - hh))
    idx = jnp.arange(CHUNK, dtype=F32)
    tril = jnp.asarray(np.tril(np.ones((CHUNK, CHUNK), dtype=bool)))
    dmask = jnp.where(tril, jnp.exp((idx[:, None] - idx[None, :]) * lg[:, None, None]), 0.0)
    qdec = jnp.exp((idx[None, :] + 1.0) * lg[:, None])
    kdec = jnp.exp((CHUNK - 1.0 - idx[None, :]) * lg[:, None])
    gch = jnp.exp(CHUNK * lg)
    qdec = jnp.broadcast_to(qdec[:, :, None], (RET_H, CHUNK, RET_D))
    kdec = jnp.broadcast_to(kdec[:, :, None], (RET_H, CHUNK, RET_D))
    gch = jnp.broadcast_to(gch[:, None, None], (RET_H, 8, LANES))
    return dmask, qdec, kdec, gch


def _ret_specs(N, rev):
    G = _group(N)
    nb = N // G
    cn = (lambda n: nb - 1 - n) if rev else (lambda n: n)
    col = lambda j: pl.BlockSpec((G * CHUNK, D_MODEL), lambda n: (cn(n), j))
    tab = lambda a, b: pl.BlockSpec((RET_H, a, b), lambda n: (0, 0, 0))
    rope = pl.BlockSpec((G * CHUNK, LANES), lambda n: (cn(n), 0))
    st = pl.BlockSpec((RET_H, G, RET_D, RET_D), lambda n: (0, cn(n), 0, 0))
    return G, nb, col, tab, rope, st


def _ret_chunk_fwd(proj_m, cos, sin, tables):
    Lp = proj_m.shape[0]
    N = Lp // CHUNK
    dmask, qdec, kdec, gch = tables
    G, nb, col, tab, rope, st = _ret_specs(N, False)

    def body(q_ref, k_ref, v_ref, c_ref, s_ref, dm_ref, qd_ref, kd_ref, g_ref, o_ref, sin_ref, S):
        n = pl.program_id(0)

        @pl.when(n == 0)
        def _():
            S[...] = jnp.zeros_like(S)

        heads = range(RET_H)
        sls = [slice(h * RET_D, (h + 1) * RET_D) for h in heads]
        rows = [slice(c * CHUNK, (c + 1) * CHUNK) for c in range(G)]
        pairs = [(c, h) for c in range(G) for h in heads]
        P = lambda f: {p: f(*p) for p in pairs}
        qr = P(lambda c, h: _rot(q_ref[rows[c], sls[h]], c_ref[rows[c], :], s_ref[rows[c], :]))
        ks = P(lambda c, h: _rot(k_ref[rows[c], sls[h]], c_ref[rows[c], :], s_ref[rows[c], :]) * (RET_D ** -0.5))
        v = P(lambda c, h: v_ref[rows[c], sls[h]])
        a = P(lambda c, h: _dnt(qr[c, h], ks[c, h]) * dm_ref[h])
        av = P(lambda c, h: _d(a[c, h], v[c, h]))
        kv = P(lambda c, h: _dtn(ks[c, h] * kd_ref[h], v[c, h]))
        qd = P(lambda c, h: qr[c, h] * qd_ref[h])
        cur = [S[h] for h in heads]
        for c in range(G):
            for h in heads:
                o_ref[rows[c], sls[h]] = av[c, h] + _d(qd[c, h], cur[h])
                sin_ref[h, c] = cur[h].astype(BF16)
            cur = [cur[h] * g_ref[h, 0:1, 0:1] + kv[c, h] for h in heads]
        for h in heads:
            S[h] = cur[h]

    return pl.pallas_call(
        body, grid=(nb,),
        in_specs=[col(3), col(4), col(5), rope, rope,
                  tab(CHUNK, CHUNK), tab(CHUNK, RET_D), tab(CHUNK, RET_D), tab(8, LANES)],
        out_specs=[col(0), st],
        out_shape=[jax.ShapeDtypeStruct((Lp, D_MODEL), F32), jax.ShapeDtypeStruct((RET_H, N, RET_D, RET_D), BF16)],
        scratch_shapes=[pltpu.VMEM((RET_H, RET_D, RET_D), F32)],
        name="ret_chunk_fwd")(proj_m, proj_m, proj_m, cos, sin, dmask, qdec, kdec, gch)


def _ret_chunk_bwd(proj_m, cos, sin, tables, do, s_in):
    Lp = proj_m.shape[0]
    N = Lp // CHUNK
    dmask, qdec, kdec, gch = tables
    G, nb, col, tab, rope, st = _ret_specs(N, True)

    def body(q_ref, k_ref, v_ref, c_ref, s_ref, dm_ref, qd_ref, kd_ref, g_ref, do_ref, sin_ref,
             d_ref, dS):
        n = pl.program_id(0)

        @pl.when(n == 0)
        def _():
            dS[...] = jnp.zeros_like(dS)

        kscale = RET_D ** -0.5
        heads = range(RET_H)
        sls = [slice(h * RET_D, (h + 1) * RET_D) for h in heads]
        rows = [slice(c * CHUNK, (c + 1) * CHUNK) for c in range(G)]
        pairs = [(c, h) for c in range(G) for h in heads]
        P = lambda f: {p: f(*p) for p in pairs}
        cs = [(c_ref[rows[c], :], s_ref[rows[c], :]) for c in range(G)]
        osl = lambda part, h: slice(part * D_MODEL + h * RET_D, part * D_MODEL + (h + 1) * RET_D)
        qr = P(lambda c, h: _rot(q_ref[rows[c], sls[h]], *cs[c]))
        ks = P(lambda c, h: _rot(k_ref[rows[c], sls[h]], *cs[c]) * kscale)
        v = P(lambda c, h: v_ref[rows[c], sls[h]])
        dov = P(lambda c, h: do_ref[rows[c], sls[h]])
        ad = P(lambda c, h: _dnt(qr[c, h], ks[c, h]) * dm_ref[h])
        da = P(lambda c, h: _dnt(dov[c, h], v[c, h]) * dm_ref[h])
        dos = P(lambda c, h: _dnt(dov[c, h], sin_ref[h, c]) * qd_ref[h])
        qdo = P(lambda c, h: _dtn(qr[c, h] * qd_ref[h], dov[c, h]))
        adv = P(lambda c, h: _dtn(ad[c, h], dov[c, h]))
        dqr = P(lambda c, h: _d(da[c, h], ks[c, h]) + dos[c, h])
        daq = P(lambda c, h: _dtn(da[c, h], qr[c, h]))
        kk = P(lambda c, h: ks[c, h] * kd_ref[h])
        cur = [dS[h] for h in heads]
        for c in reversed(range(G)):
            for h in heads:
                d_ref[rows[c], osl(2, h)] = (adv[c, h] + _d(kk[c, h], cur[h])).astype(BF16)
                d_ref[rows[c], osl(0, h)] = _rot_bwd(dqr[c, h], *cs[c]).astype(BF16)
                dks = daq[c, h] + _dnt(v[c, h], cur[h]) * kd_ref[h]
                d_ref[rows[c], osl(1, h)] = _rot_bwd(dks * kscale, *cs[c]).astype(BF16)
            cur = [cur[h] * g_ref[h, 0:1, 0:1] + qdo[c, h] for h in heads]
        for h in heads:
            dS[h] = cur[h]

    return pl.pallas_call(
        body, grid=(nb,),
        in_specs=[col(3), col(4), col(5), rope, rope,
                  tab(CHUNK, CHUNK), tab(CHUNK, RET_D), tab(CHUNK, RET_D), tab(8, LANES), col(0), st],
        out_specs=pl.BlockSpec((G * CHUNK, 3 * D_MODEL), lambda n: (nb - 1 - n, 0)),
        out_shape=jax.ShapeDtypeStruct((Lp, 3 * D_MODEL), BF16),
        scratch_shapes=[pltpu.VMEM((RET_H, RET_D, RET_D), F32)],
        name="ret_chunk_bwd")(proj_m, proj_m, proj_m, cos, sin, dmask, qdec, kdec, gch, do, s_in)


def _merge_specs(tr):
    col = lambda j: pl.BlockSpec((tr, D_MODEL), lambda i: (i, j))
    return col


def _merge_fwd(o_a, o_b, proj_m, gnorm):
    Lp = o_a.shape[0]
    tr = _tile(Lp, 192, 16)

    def body(oa_ref, ob_ref, gz_ref, rg_ref, ga_ref, gb_ref, gn_ref, y_ref):
        gn = gn_ref[...]
        oa = oa_ref[...]
        ob = ob_ref[...]
        gz = gz_ref[...]
        ya = []
        for j in range(GDN_H):
            seg = oa[:, j * GDN_D:(j + 1) * GDN_D]
            r = lax.rsqrt(jnp.mean(seg * seg, axis=-1, keepdims=True) + EPS)
            ya.append(seg * r * gn)
        ya = jnp.concatenate(ya, axis=1) * (gz * _sig(gz))
        yb = []
        for j in range(RET_H):
            seg = ob[:, j * RET_D:(j + 1) * RET_D]
            r = lax.rsqrt(jnp.mean(seg * seg, axis=-1, keepdims=True) + EPS)
            yb.append(seg * r)
        rg = rg_ref[...]
        yb = jnp.concatenate(yb, axis=1) * (rg * _sig(rg))
        y_ref[...] = (_sig(ga_ref[...]) * ya + _sig(gb_ref[...]) * yb).astype(BF16)

    col = _merge_specs(tr)
    return pl.pallas_call(
        body, grid=(Lp // tr,),
        in_specs=[col(0), col(0), col(6), col(7), col(8), col(9), pl.BlockSpec((1, GDN_D), lambda i: (0, 0))],
        out_specs=col(0), out_shape=jax.ShapeDtypeStruct((Lp, D_MODEL), BF16),
        name="merge_fwd")(o_a, o_b, proj_m, proj_m, proj_m, proj_m, gnorm)


def _merge_bwd(dh1b, w_out, o_a, o_b, proj_m, gnorm):
    Lp = o_a.shape[0]
    tr = _tile(Lp, 192, 16)

    def body(d_ref, wo_ref, oa_ref, ob_ref, gz_ref, rg_ref, ga_ref, gb_ref, gn_ref, dc_ref, doa_ref, dob_ref, dgn_ref):
        i = pl.program_id(0)
        gn = gn_ref[...]
        dyv = lax.dot_general(d_ref[...], wo_ref[...], _NT, preferred_element_type=F32)
        oa = oa_ref[...]
        ob = ob_ref[...]
        gz = gz_ref[...]
        rg = rg_ref[...]
        sa = _sig(ga_ref[...])
        sb = _sig(gb_ref[...])
        dya = dyv * sa
        dyb = dyv * sb
        sgz = _sig(gz)
        szz = gz * sgz
        dgn = jnp.zeros((1, GDN_D), F32)
        ya = []
        dgz = []
        for j in range(GDN_H):
            sl = slice(j * GDN_D, (j + 1) * GDN_D)
            seg = oa[:, sl]
            r = lax.rsqrt(jnp.mean(seg * seg, axis=-1, keepdims=True) + EPS)
            xh = seg * r
            oan = xh * gn
            ya.append(oan * szz[:, sl])
            dgz.append(dya[:, sl] * oan * (sgz[:, sl] * (1.0 + gz[:, sl] * (1.0 - sgz[:, sl]))))
            doan = dya[:, sl] * szz[:, sl]
            dgn = dgn + jnp.sum(doan * xh, axis=0, keepdims=True)
            dxh = doan * gn
            doa_ref[:, sl] = r * (dxh - xh * jnp.mean(dxh * xh, axis=-1, keepdims=True))
        ya = jnp.concatenate(ya, axis=1)
        srg = _sig(rg)
        srr = rg * srg
        yb = []
        drg = []
        for j in range(RET_H):
            sl = slice(j * RET_D, (j + 1) * RET_D)
            seg = ob[:, sl]
            r = lax.rsqrt(jnp.mean(seg * seg, axis=-1, keepdims=True) + EPS)
            xh = seg * r
            yb.append(xh * srr[:, sl])
            drg.append(dyb[:, sl] * xh * (srg[:, sl] * (1.0 + rg[:, sl] * (1.0 - srg[:, sl]))))
            dxh = dyb[:, sl] * srr[:, sl]
            dob_ref[:, sl] = r * (dxh - xh * jnp.mean(dxh * xh, axis=-1, keepdims=True))
        yb = jnp.concatenate(yb, axis=1)
        dc_ref[:, 0:D_MODEL] = jnp.concatenate(dgz, axis=1).astype(BF16)
        dc_ref[:, D_MODEL:2 * D_MODEL] = jnp.concatenate(drg, axis=1).astype(BF16)
        dc_ref[:, 2 * D_MODEL:3 * D_MODEL] = (dyv * ya * sa * (1.0 - sa)).astype(BF16)
        dc_ref[:, 3 * D_MODEL:] = (dyv * yb * sb * (1.0 - sb)).astype(BF16)

        @pl.when(i == 0)
        def _():
            dgn_ref[...] = dgn

        @pl.when(i > 0)
        def _():
            dgn_ref[...] += dgn

    col = _merge_specs(tr)
    return pl.pallas_call(
        body, grid=(Lp // tr,),
        in_specs=[col(0), pl.BlockSpec((D_MODEL, D_MODEL), lambda i: (0, 0), pipeline_mode=pl.Buffered(1)),
                  col(0), col(0), col(6), col(7), col(8), col(9), pl.BlockSpec((1, GDN_D), lambda i: (0, 0))],
        out_specs=[pl.BlockSpec((tr, 4 * D_MODEL), lambda i: (i, 0)), col(0), col(0),
                   pl.BlockSpec((1, GDN_D), lambda i: (0, 0))],
        out_shape=[jax.ShapeDtypeStruct((Lp, 4 * D_MODEL), BF16), jax.ShapeDtypeStruct((Lp, D_MODEL), F32),
                   jax.ShapeDtypeStruct((Lp, D_MODEL), F32), jax.ShapeDtypeStruct((1, GDN_D), F32)],
        name="merge_bwd")(dh1b, w_out, o_a, o_b, proj_m, proj_m, proj_m, proj_m, gnorm)


def _ffn_act(up, conv_w, conv_b):
    Lp = up.shape[0]
    tr = _tile(Lp, 192, 16)
    W2 = 2 * D_FF

    def body(main_ref, prev_ref, w_ref, b_ref, act_ref, u_ref):
        i = pl.program_id(0)
        prev = jnp.where(i > 0, prev_ref[...], 0.0)
        ext = jnp.concatenate([prev, main_ref[...]], axis=0)
        u = _taps(_shifted(ext, range(8 - (FFN_CONV - 1), 9)), w_ref[...], tr, b_ref[...])
        a = u[:, :D_FF]
        act_ref[...] = (a * _sig(a) * u[:, D_FF:]).astype(BF16)
        u_ref[...] = u.astype(BF16)

    return pl.pallas_call(
        body, grid=(Lp // tr,),
        in_specs=[pl.BlockSpec((tr, W2), lambda i: (i, 0)), _halo_prev(tr, W2),
                  pl.BlockSpec((FFN_CONV, W2), lambda i: (0, 0)), pl.BlockSpec((1, W2), lambda i: (0, 0))],
        out_specs=[pl.BlockSpec((tr, D_FF), lambda i: (i, 0)), pl.BlockSpec((tr, W2), lambda i: (i, 0))],
        out_shape=[jax.ShapeDtypeStruct((Lp, D_FF), BF16), jax.ShapeDtypeStruct((Lp, W2), BF16)],
        name="ffn_act")(up, up, conv_w, conv_b)


def _ffn_act_bwd(up, u, dact, conv_w):
    Lp = up.shape[0]
    tr = _tile(Lp, 192, 16)
    W2 = 2 * D_FF
    te = tr + 8

    def body(up_ref, u_ref, un_ref, da_ref, dan_ref, w_ref, dup_ref, acc_ref):
        i = pl.program_id(0)
        w = w_ref[...]
        ue = jnp.concatenate([u_ref[...].astype(F32), un_ref[...].astype(F32)[0:8]], axis=0)
        a = ue[:, :D_FF]
        b = ue[:, D_FF:]
        rowe = i * tr + lax.broadcasted_iota(jnp.int32, (te, 1), 0)
        dae = jnp.where(rowe < Lp, jnp.concatenate([da_ref[...], dan_ref[...]], axis=0), 0.0)
        sg = _sig(a)
        du = jnp.concatenate([dae * b * (sg * (1.0 + a * (1.0 - sg))), dae * (a * sg)], axis=1)
        dus = _shifted(du, range(FFN_CONV - 1, -1, -1))
        dup_ref[...] = _taps(dus, w, tr).astype(BF16)
        upm = up_ref[...]
        rows = [jnp.sum(dus[kk][0:tr, :] * upm, axis=0, keepdims=True) for kk in range(FFN_CONV)]
        rows.append(jnp.sum(du[0:tr, :], axis=0, keepdims=True))
        part = jnp.concatenate(rows + [jnp.zeros((8 - len(rows), W2), F32)], axis=0)

        @pl.when(i == 0)
        def _():
            acc_ref[...] = part

        @pl.when(i > 0)
        def _():
            acc_ref[...] += part

    return pl.pallas_call(
        body, grid=(Lp // tr,),
        in_specs=[pl.BlockSpec((tr, W2), lambda i: (i, 0)), pl.BlockSpec((tr, W2), lambda i: (i, 0)),
                  _halo_next(tr, W2, Lp, rows=16), pl.BlockSpec((tr, D_FF), lambda i: (i, 0)), _halo_next(tr, D_FF, Lp),
                  pl.BlockSpec((FFN_CONV, W2), lambda i: (0, 0))],
        out_specs=[pl.BlockSpec((tr, W2), lambda i: (i, 0)), pl.BlockSpec((8, W2), lambda i: (0, 0))],
        out_shape=[jax.ShapeDtypeStruct((Lp, W2), BF16), jax.ShapeDtypeStruct((8, W2), F32)],
        name="ffn_act_bwd")(up, u, u, dact, dact, conv_w)


def _local_step(hpad, tgt, pad, wt, first_weights=None, late_weights=None, on_ffn_out_grads=None,
                on_w_in_grads=None):
    Lp = hpad.shape[0]
    first = pad + N_META
    pos = jnp.arange(Lp, dtype=F32) - float(pad)
    half = RET_D // 2
    inv = 1.0 / (ROPE_BASE ** (jnp.arange(half, dtype=F32) / half))
    ang = pos[:, None] * inv[None, :]
    cos, sin = jnp.cos(ang), jnp.sin(ang)
    tables = _ret_tables()
    gparams = jnp.zeros((8, LANES), F32).at[0, :GDN_H].set(wt["a_log"]).at[1, :GDN_H].set(wt["dt_bias"])

    hn1 = _rms_fwd(hpad, wt["norm1"], "rms1_fwd")
    if first_weights is not None:
        wt = {**wt, **first_weights(hn1)}
    proj_m = _mm_nn(hn1, wt["w_main_t"], bt=True, tm_target=1376, name="proj_main")
    proj_s = _mm_nn(hn1, wt["w_small_t"], bt=True, name="proj_small")
    qkv, gsm, conv_out = _gdn_pre(proj_m, proj_s, wt["gdn_conv_w"], gparams, pad)
    o_a, s_a, t_a = _gdn_chunk_fwd(qkv, gsm)
    o_b, s_b = _ret_chunk_fwd(proj_m, cos, sin, tables)
    y = _merge_fwd(o_a, o_b, proj_m, wt["gdn_norm"])
    if late_weights is not None:
        wt = {**wt, **late_weights(y)}
    h1, hn2 = _mm_rms_fwd(_Producer(y, wt["w_out"], hpad), wt["norm2"], "out_proj_rms2")
    up = _mm_nn(hn2, wt["w_up_t"], bt=True, name="ffn_up")
    act, u_ffn = _ffn_act(up, wt["ffn_conv_w"], wt["ffn_conv_b"])
    lossvec, dh2, dh2b, d_norm_f = _final(_Producer(act, wt["w_down"], h1), wt["norm_f"], tgt, first)

    d_w_down = _mm_tn(act, dh2b, name="dw_down")
    dact = _mm_nt(dh2b, wt["w_down"], name="d_act")
    dup, ffn_rows = _ffn_act_bwd(up, u_ffn, dact, wt["ffn_conv_w"])
    d_w_up_t = _mm_tn(dup, hn2, name="dw_up")
    dh1, dh1b, d_norm2 = _rms_bwd(h1, wt["norm2"], _Producer(dup, wt["w_up_t"]), dh2, pad, "d_hn2_rms2_bwd")

    d_w_out = _mm_tn(y, dh1b, name="dw_out")
    gnorm = wt["gdn_norm"]
    if on_ffn_out_grads is not None:
        gnorm = gnorm + on_ffn_out_grads(d_w_down, d_w_up_t, d_w_out)[0:1, :]
    d_c, do_a, do_b, d_gnorm = _merge_bwd(dh1b, wt["w_out"], o_a, o_b, proj_m, gnorm)
    d_r = _ret_chunk_bwd(proj_m, cos, sin, tables, do_b, s_b)
    dq, dk, dv, dgs = _gdn_chunk_bwd(qkv, gsm, do_a, s_a, t_a)
    d_a, d_s, conv_rows, gp_rows = _gdn_pre_bwd(proj_m, conv_out, proj_s, wt["gdn_conv_w"], gparams, dq, dk, dv, dgs,
                                                pad)

    wmt = wt["w_main_t"]
    segs = [(d_a, 0, 3 * D_MODEL), (d_r, 3 * D_MODEL, 3 * D_MODEL), (d_c, 6 * D_MODEL, 4 * D_MODEL)]
    pa, pr, pc = [_mm_tn(d, hn1, BF16, name="dw_in_%d" % i) for i, (d, _, _) in enumerate(segs)]
    ps = _mm_tn(d_s, hn1, BF16, name="dw_in_small")
    d_w_in_t = jnp.concatenate([pa, pc[:D_MODEL], ps[:2 * GDN_H], pr, pc[D_MODEL:]], axis=0)
    w_small_t = wt["w_small_t"]
    if on_w_in_grads is not None:
        w_small_t = w_small_t + on_w_in_grads(d_w_in_t)[0:1, 0:1].astype(w_small_t.dtype)
    dhn1 = _mm_sum([(d_s, w_small_t)] + [(d, wmt[off:off + width]) for d, off, width in segs[:-1]], "d_hn1_first")
    d, off, width = segs[-1]
    dh0, _, d_norm1 = _rms_bwd(hpad, wt["norm1"], _Producer(d, wmt[off:off + width], dhn1), dh1, pad,
                               "d_hn1_rms1_bwd")

    grads = {
        "norm1": d_norm1, "w_in_t": d_w_in_t, "gdn_conv_w": conv_rows[:GDN_CONV],
        "a_log": gp_rows[0, :GDN_H], "dt_bias": gp_rows[1, :GDN_H], "gdn_norm": d_gnorm, "w_out": d_w_out,
        "norm2": d_norm2, "w_up_t": d_w_up_t, "ffn_conv_w": ffn_rows[:FFN_CONV],
        "ffn_conv_b": ffn_rows[FFN_CONV:FFN_CONV + 1], "w_down": d_w_down, "norm_f": d_norm_f,
    }
    return lossvec, dh0, grads


def _peer(k):
    ix, iy, ic = lax.axis_index("x"), lax.axis_index("y"), lax.axis_index("c")
    px = 1 - ix if (k >> 2) & 1 else ix
    py = 1 - iy if (k >> 1) & 1 else iy
    pc = 1 - ic if k & 1 else ic
    return (px, py, pc), 4 * px + 2 * py + pc


def _comm_call(body, n, out_shapes, name, args):
    hbm = pl.BlockSpec(memory_space=pl.ANY)
    return pl.pallas_call(
        body, out_shape=out_shapes, in_specs=[hbm] * n, out_specs=[hbm] * n,
        scratch_shapes=[pltpu.SemaphoreType.DMA((n, N_DEV - 1)), pltpu.SemaphoreType.DMA((n, N_DEV - 1)),
                        pltpu.SemaphoreType.DMA((n,))],
        name=name)(*args)


def _all_gather(xs, name):
    n = len(xs)

    def body(*refs):
        x_refs, out_refs = refs[:n], refs[n:2 * n]
        send_sems, recv_sems, local_sems = refs[2 * n:]
        _, me = _peer(0)
        pending = []
        for i in range(n):
            local = pltpu.make_async_copy(x_refs[i], out_refs[i].at[me], local_sems.at[i])
            local.start()
            pending.append(local)
        sends = []
        for i in range(n):
            for k in range(1, N_DEV):
                dev, _ = _peer(k)
                cp = pltpu.make_async_remote_copy(
                    src_ref=x_refs[i], dst_ref=out_refs[i].at[me], send_sem=send_sems.at[i, k - 1],
                    recv_sem=recv_sems.at[i, k - 1], device_id=dev, device_id_type=MESH_T)
                cp.start()
                sends.append(cp)
        for i in range(n):
            for k in range(1, N_DEV):
                dev, idx = _peer(k)
                pltpu.make_async_remote_copy(
                    src_ref=x_refs[i], dst_ref=out_refs[i].at[idx], send_sem=send_sems.at[i, k - 1],
                    recv_sem=recv_sems.at[i, k - 1], device_id=dev, device_id_type=MESH_T).wait_recv()
        for cp in sends:
            cp.wait_send()
        for local in pending:
            local.wait()

    out_shapes = [jax.ShapeDtypeStruct((N_DEV,) + a.shape, a.dtype) for a in xs]
    return _comm_call(body, n, out_shapes, name, xs)


def _all_to_all(gs, name):
    n = len(gs)

    def body(*refs):
        g_refs, out_refs = refs[:n], refs[n:2 * n]
        send_sems, recv_sems, local_sems = refs[2 * n:]
        _, me = _peer(0)
        pending = []
        for i in range(n):
            local = pltpu.make_async_copy(g_refs[i].at[me], out_refs[i].at[0], local_sems.at[i])
            local.start()
            pending.append(local)
        sends = []
        for i in range(n):
            for k in range(1, N_DEV):
                dev, idx = _peer(k)
                cp = pltpu.make_async_remote_copy(
                    src_ref=g_refs[i].at[idx], dst_ref=out_refs[i].at[k], send_sem=send_sems.at[i, k - 1],
                    recv_sem=recv_sems.at[i, k - 1], device_id=dev, device_id_type=MESH_T)
                cp.start()
                sends.append(cp)
        for cp in sends:
            cp.wait_recv()
        for cp in sends:
            cp.wait_send()
        for local in pending:
            local.wait()

    out_shapes = [jax.ShapeDtypeStruct(g.shape, g.dtype) for g in gs]
    return _comm_call(body, n, out_shapes, name, gs)


_SPLIT_RELATIONS = {"gather": tuple(range(1, N_DEV)), "a2a": tuple(range(1, N_DEV)), "chip": (1, 2, 4, 6),
                    "forward": (2, 4, 6)}


def _split_copies(kind, src_refs, land_refs, send_sems, recv_sems, local_sems, with_recv):
    n = len(land_refs)
    rels = _SPLIT_RELATIONS[kind]
    _, me = _peer(0)
    locals_, remotes = [], []
    for i in range(n):
        if kind in ("gather", "chip"):
            locals_.append(pltpu.make_async_copy(src_refs[i], land_refs[i].at[me], local_sems.at[i]))
        elif kind == "a2a":
            locals_.append(pltpu.make_async_copy(src_refs[i].at[me], land_refs[i].at[0], local_sems.at[i]))
        for jj, k in enumerate(rels):
            dev, idx = _peer(k)
            if kind in ("gather", "chip"):
                src, dst, mine = src_refs[i], land_refs[i].at[me], land_refs[i].at[idx]
            elif kind == "a2a":
                src, dst, mine = src_refs[i].at[idx], land_refs[i].at[k], land_refs[i].at[k]
            else:
                dev, _ = _peer(1)
                _, came = _peer(k + 1)
                src, dst, mine = land_refs[i].at[idx], land_refs[i].at[idx], land_refs[i].at[came]
            j = i * len(rels) + jj
            send = pltpu.make_async_remote_copy(
                src_ref=src, dst_ref=dst, send_sem=send_sems.at[j], recv_sem=recv_sems.at[j],
                device_id=dev, device_id_type=MESH_T)
            recv = pltpu.make_async_remote_copy(
                src_ref=src, dst_ref=mine, send_sem=send_sems.at[j], recv_sem=recv_sems.at[j],
                device_id=dev, device_id_type=MESH_T) if with_recv else None
            remotes.append((send, recv))
    return locals_, remotes


_HBM = pl.BlockSpec(memory_space=pltpu.HBM)
_SEM = pl.BlockSpec(memory_space=pltpu.SEMAPHORE)
_ANY = pl.BlockSpec(memory_space=pl.ANY)


def _split_start(srcs, kind, name, after):
    n = len(srcs)
    if kind == "forward":
        arrays = list(srcs)
    else:
        gathers = kind in ("gather", "chip")
        arrays = list(srcs) + [lax.empty(((N_DEV,) + a.shape) if gathers else a.shape, a.dtype) for a in srcs]
    na = len(arrays)

    def body(*refs):
        src_refs, land_refs = refs[:n], refs[na - n:na]
        send_sems, recv_sems, local_sems = refs[na + 1:na + 4]
        token = refs[-1]
        locals_, remotes = _split_copies(kind, src_refs, land_refs, send_sems, recv_sems, local_sems, False)
        for cp in locals_:
            cp.start()
        for send, _ in remotes:
            send.start()
        token[...] = jnp.zeros_like(token)

    ncp = n * len(_SPLIT_RELATIONS[kind])
    sems = (pltpu.SemaphoreType.DMA((ncp,)), pltpu.SemaphoreType.DMA((ncp,)), pltpu.SemaphoreType.DMA((n,)))
    thru = tuple(pltpu.HBM(a.shape, a.dtype) for a in arrays)
    outs = pl.pallas_call(
        body, name=name,
        out_shape=sems + thru + (jax.ShapeDtypeStruct((8, LANES), F32),),
        in_specs=[_HBM] * na + [_ANY],
        out_specs=[_SEM] * 3 + [_HBM] * na + [pl.BlockSpec(memory_space=pltpu.VMEM)],
        input_output_aliases={i: 3 + i for i in range(na)},
        compiler_params=pltpu.CompilerParams(has_side_effects=pltpu.SideEffectType.DATAFLOW_SIDE_EFFECTING),
    )(*[pltpu.with_memory_space_constraint(a, pltpu.HBM) for a in arrays], after)
    return (kind, n, outs[:3], outs[3:3 + na]), outs[-1]


def _split_wait(handle, name, after):
    kind, n, sems, thru = handle
    na = len(thru)

    def body(*refs):
        src_refs, land_refs = refs[:n], refs[na - n:na]
        send_sems, recv_sems, local_sems = refs[na:na + 3]
        locals_, remotes = _split_copies(kind, src_refs, land_refs, send_sems, recv_sems, local_sems, True)
        for send, recv in remotes:
            send.wait_send()
            recv.wait_recv()
        for cp in locals_:
            cp.wait()

    outs = pl.pallas_call(
        body, name=name, out_shape=tuple(pltpu.HBM(a.shape, a.dtype) for a in thru),
        in_specs=[_HBM] * na + [_SEM] * 3 + [_ANY], out_specs=[_HBM] * na,
        input_output_aliases={i: i for i in range(na)},
        compiler_params=pltpu.CompilerParams(has_side_effects=pltpu.SideEffectType.DATAFLOW_SIDE_EFFECTING),
    )(*thru, *sems, after)
    return list(outs[na - n:])


def _adamw(gslabs, w, m, v, name):
    R, Cw = w.shape
    if R % 8 == 0:
        tr, tc = _tile(R, 64 if Cw > 1024 else 128, 8), Cw
    else:
        tr, tc = R, LANES
    c1 = 1.0 - ADAM_B1 ** ADAM_STEP
    c2 = 1.0 - ADAM_B2 ** ADAM_STEP

    def body(g_ref, w_ref, m_ref, v_ref, go_ref, d_ref, mo_ref, vo_ref):
        g = g_ref[0].astype(F32)
        for k in range(1, N_DEV):
            g = g + g_ref[k].astype(F32)
        mn = ADAM_B1 * m_ref[...] + (1.0 - ADAM_B1) * g
        vn = ADAM_B2 * v_ref[...] + (1.0 - ADAM_B2) * (g * g)
        m_hat = mn / c1
        v_hat = vn / c2
        go_ref[...] = g
        d_ref[...] = -ADAM_LR * (m_hat / (jnp.sqrt(v_hat) + ADAM_EPS) + ADAM_WD * w_ref[...])
        mo_ref[...] = mn
        vo_ref[...] = vn

    blk = pl.BlockSpec((tr, tc), lambda i, j: (i, j))
    return pl.pallas_call(
        body, grid=(R // tr, Cw // tc),
        in_specs=[pl.BlockSpec((N_DEV, tr, tc), lambda i, j: (0, i, j)), blk, blk, blk],
        out_specs=[blk] * 4, out_shape=[jax.ShapeDtypeStruct((R, Cw), F32)] * 4, name=name)(gslabs, w, m, v)


def _pack(arrs, row_mult, dtype=F32):
    parts = []
    total = 0
    for a in arrs:
        f = a.reshape(-1).astype(dtype)
        n = -(-f.shape[0] // 1024) * 1024
        parts.append(jnp.pad(f, (0, n - f.shape[0])))
        total += n
    rows = total // LANES
    rows_p = -(-rows // row_mult) * row_mult
    flat = jnp.concatenate(parts)
    flat = jnp.pad(flat, (0, rows_p * LANES - total))
    return flat.reshape(rows_p, LANES)


def _unpack(packed, shapes):
    lead = packed.shape[:-2]
    flat = packed.reshape(lead + (-1,))
    out = []
    off = 0
    for s in shapes:
        n = int(np.prod(s))
        out.append(flat[..., off:off + n].reshape(lead + tuple(s)))
        off += -(-n // 1024) * 1024
    return out


def _gather_cols(stacked):
    d, r, c = stacked.shape
    return stacked.transpose(1, 0, 2).reshape(r, d * c)


def _scatter_cols(full):
    r, n = full.shape
    return full.reshape(r, N_DEV, n // N_DEV).transpose(1, 0, 2)


def kernel(x, meta, norm1, w_in, gdn_conv_w, gdn_a_log, gdn_dt_bias, gdn_norm, w_out, norm2, w_ffn_up, ffn_conv_w, ffn_conv_b, w_ffn_down, norm_f, loss_target, m_meta, m_norm1, m_w_in, m_gdn_conv_w, m_gdn_a_log, m_gdn_dt_bias, m_gdn_norm, m_w_out, m_norm2, m_w_ffn_up, m_ffn_conv_w, m_ffn_conv_b, m_w_ffn_down, m_norm_f, v_meta, v_norm1, v_w_in, v_gdn_conv_w, v_gdn_a_log, v_gdn_dt_bias, v_gdn_norm, v_w_out, v_norm2, v_w_ffn_up, v_ffn_conv_w, v_ffn_conv_b, v_w_ffn_down, v_norm_f):
    S = x.shape[1]
    L = N_META + S
    pad = (-L) % CHUNK
    Lp = L + pad

    tr_ = lambda a: jnp.swapaxes(a[0], 0, 1)
    big = [tr_(w_in), w_out[0], tr_(w_ffn_up), w_ffn_down[0]]
    small = [meta, gdn_conv_w, ffn_conv_w]
    small_all, = _all_gather([_pack(small, 8)], "gather_small_weights")
    first, first_token = _split_start([big[0].astype(BF16)], "chip", "gather_w_in_start", small_all)
    late, late_token = _split_start([a.astype(BF16) for a in big[1:]], "gather", "gather_late_start", first_token)

    def first_weights(after):
        half = _split_wait(first, "gather_w_in_wait", after)
        second, second_token = _split_start(half, "forward", "gather_w_in_forward_start", after)
        w_in_s, = _split_wait(second, "gather_w_in_forward_wait", second_token)
        w_in_t = w_in_s.reshape(_O_END, D_MODEL)
        w_main_t = jnp.concatenate([w_in_t[_O_GQ:_O_GZ], w_in_t[_O_RQ:_O_RG], w_in_t[_O_GZ:_O_GA],
                                    w_in_t[_O_RG:_O_END]], axis=0)
        return {"w_main_t": w_main_t, "w_small_t": jnp.pad(w_in_t[_O_GA:_O_RQ], ((0, LANES - 2 * GDN_H), (0, 0)))}

    def late_weights(after):
        w_out_s, w_up_s, w_down_s = _split_wait(late, "gather_late_wait", after)
        return {"w_out": w_out_s.reshape(D_MODEL, D_MODEL), "w_up_t": w_up_s.reshape(2 * D_FF, D_MODEL),
                "w_down": w_down_s.reshape(D_FF, D_MODEL)}

    meta_s, gconv_s, fconv_s = _unpack(small_all, [a.shape for a in small])
    wt = {
        "norm1": norm1 + jnp.tile(late_token[0:1, :], (1, D_MODEL // LANES)),
        "gdn_conv_w": _gather_cols(gconv_s[:, 0]), "a_log": gdn_a_log[0], "dt_bias": gdn_dt_bias[0],
        "gdn_norm": gdn_norm, "norm2": norm2, "ffn_conv_w": _gather_cols(fconv_s[:, 0]), "ffn_conv_b": ffn_conv_b,
        "norm_f": norm_f.reshape(1, D_MODEL),
    }
    meta_f = _gather_cols(meta_s)

    pending = {}

    def on_ffn_out_grads(d_w_down, d_w_up_t, d_w_out):
        srcs = [d_w_out.reshape(N_DEV, D_MODEL // N_DEV, D_MODEL), d_w_up_t.reshape(N_DEV, 2 * D_FF // N_DEV, D_MODEL),
                d_w_down.reshape(N_DEV, D_FF // N_DEV, D_MODEL)]
        pending["ffn_out"], token = _split_start(srcs, "a2a", "exchange_ffn_out_start", d_w_out)
        return token

    def on_w_in_grads(d_w_in_t):
        slabs = d_w_in_t.astype(BF16).reshape(N_DEV, _O_END // N_DEV, D_MODEL)
        pending["w_in"], token = _split_start([slabs], "a2a", "exchange_w_in_start", d_w_in_t)
        return token

    head = jnp.concatenate([jnp.zeros((pad, D_MODEL), F32), meta_f], axis=0)
    if S >= 2 * 704:
        hpad = _Rows(x[0], pad + N_META, head)
        tgt = _Rows(loss_target[0], pad + N_META)
    else:
        hpad = jnp.concatenate([head, x[0]], axis=0)
        tgt = jnp.concatenate([jnp.zeros((pad + N_META, D_MODEL), F32), loss_target[0]], axis=0)
    lossvec, dh0, gr = _local_step(hpad, tgt, pad, wt, first_weights, late_weights, on_ffn_out_grads, on_w_in_grads)

    loss = lax.psum(jnp.sum(lossvec), ("x", "y", "c"))
    grad_x = dh0[pad + N_META:][None]

    big_m = [tr_(m_w_in), m_w_out[0], tr_(m_w_ffn_up), m_w_ffn_down[0]]
    big_v = [tr_(v_w_in), v_w_out[0], tr_(v_w_ffn_up), v_w_ffn_down[0]]
    slabs_ffn_out = _split_wait(pending["ffn_out"], "exchange_ffn_out_wait", dh0)
    big_out = [None] + [_adamw(slabs_ffn_out[i - 1], big[i], big_m[i], big_v[i], "adamw_big_%d" % i)
                        for i in range(1, len(big))]
    g_sm = [_scatter_cols(dh0[pad:pad + N_META]), _scatter_cols(gr["gdn_conv_w"]), _scatter_cols(gr["ffn_conv_w"])]
    g_small = jnp.stack([_pack([g[d] for g in g_sm], 8) for d in range(N_DEV)])
    slabs_small, = _all_to_all([g_small], "exchange_small_gradients")
    small_out = _adamw(slabs_small, _pack(small, 8), _pack([m_meta, m_gdn_conv_w, m_ffn_conv_w], 8),
                       _pack([v_meta, v_gdn_conv_w, v_ffn_conv_w], 8), "adamw_small_sharded")
    small_un = [_unpack(o, [a.shape for a in small]) for o in small_out]
    rep_w = [norm1, gdn_a_log, gdn_dt_bias, gdn_norm, norm2, ffn_conv_b, norm_f]
    rep_m = [m_norm1, m_gdn_a_log, m_gdn_dt_bias, m_gdn_norm, m_norm2, m_ffn_conv_b, m_norm_f]
    rep_v = [v_norm1, v_gdn_a_log, v_gdn_dt_bias, v_gdn_norm, v_norm2, v_ffn_conv_b, v_norm_f]
    rep_g = [gr["norm1"], gr["a_log"], gr["dt_bias"], gr["gdn_norm"], gr["norm2"], gr["ffn_conv_b"], gr["norm_f"]]
    rep_slabs, = _all_gather([_pack(rep_g, 8)], "gather_small_gradients")
    rep_out = _adamw(rep_slabs, _pack(rep_w, 8), _pack(rep_m, 8), _pack(rep_v, 8), "adamw_replicated")
    rep_shapes = [a.shape for a in rep_w]
    rp_g, rp_d, rp_nm, rp_nv = [_unpack(o, rep_shapes) for o in rep_out]

    slabs_w_in, = _split_wait(pending["w_in"], "exchange_w_in_wait", rep_out[0])
    big_out[0] = _adamw(slabs_w_in, big[0], big_m[0], big_v[0], "adamw_big_0")
    back = lambda a: jnp.swapaxes(a, 0, 1)[None]
    sh_g, sh_d, sh_nm, sh_nv = [
        [small_un[j][0], back(big_out[0][j]), small_un[j][1], big_out[1][j][None], back(big_out[2][j]),
         small_un[j][2], big_out[3][j][None]] for j in range(4)]

    def order(sh, rp):
        return [sh[0], rp[0], sh[1], sh[2], rp[1], rp[2], rp[3], sh[3], rp[4], sh[4], sh[5], rp[5], sh[6], rp[6]]

    return (loss, grad_x, *order(sh_g, rp_g), *order(sh_d, rp_d), *order(sh_nm, rp_nm), *order(sh_nv, rp_nv))
```

```python
import functools
import math

import numpy as np
import jax
import jax.numpy as jnp
from jax import lax
from jax.experimental import pallas as pl
from jax.experimental.pallas import tpu as pltpu

F32 = jnp.float32
BF16 = jnp.bfloat16
HI = lax.Precision.HIGHEST

D_MODEL = 1024
N_META = 16
CHUNK = 64
GDN_H = 8
GDN_D = 128
RET_H = 4
RET_D = 256
D_FF = 2816
GDN_CONV = 4
FFN_CONV = 3
ROPE_BASE = 10000.0
EPS = 1e-6
N_DEV = 8
LANES = 128
MAIN_W = 10 * 1024
_O_GQ, _O_GZ, _O_GA, _O_RQ, _O_RG, _O_GATE, _O_END = 0, 3072, 4096, 4112, 7184, 8208, 10256

ADAM_LR = 0.001
ADAM_B1 = 0.9
ADAM_B2 = 0.999
ADAM_EPS = 1e-08
ADAM_WD = 0.01
ADAM_STEP = 10

MESH_T = pl.DeviceIdType.MESH


def _tile(n, target, mult):
    best = None
    for d in range(mult, min(n, target) + 1, mult):
        if n % d == 0:
            best = d
    assert best is not None, (n, target, mult)
    return best


def _sig(x):
    return 1.0 / (1.0 + jnp.exp(-x))


def _d(a, b):
    return jnp.dot(a.astype(BF16), b.astype(BF16), preferred_element_type=F32)


def _dnt(a, b):
    return lax.dot_general(a.astype(BF16), b.astype(BF16), (((1,), (1,)), ((), ())), preferred_element_type=F32)


def _dtn(a, b):
    return lax.dot_general(a.astype(BF16), b.astype(BF16), (((0,), (0,)), ((), ())), preferred_element_type=F32)


def _dx(a, b):
    return jnp.dot(a, b, preferred_element_type=F32, precision=HI)


def _dxnt(a, b):
    return lax.dot_general(a, b, (((1,), (1,)), ((), ())), preferred_element_type=F32, precision=HI)


def _dxtn(a, b):
    return lax.dot_general(a, b, (((0,), (0,)), ((), ())), preferred_element_type=F32, precision=HI)


def _split(a):
    hi = a.astype(BF16)
    return hi, (a - hi.astype(F32)).astype(BF16)


def _d3g(a, b, dims):
    ah, al = _split(a)
    bh, bl = _split(b)
    f = functools.partial(lax.dot_general, dimension_numbers=dims, preferred_element_type=F32)
    if dims == _NN:
        rows = a.shape[0]
        both = f(jnp.concatenate([ah, al], axis=0), bh)
        return both[:rows] + (f(ah, bl) + both[rows:])
    return f(ah, bh) + (f(ah, bl) + f(al, bh))


_NN = (((1,), (0,)), ((), ()))
_NT = (((1,), (1,)), ((), ()))
_TN = (((0,), (0,)), ((), ()))


def _rowsum(x):
    return jnp.sum(x, axis=1, keepdims=True)


def _allsum(x):
    return jnp.sum(jnp.sum(x, axis=1, keepdims=True), axis=0, keepdims=True)


def _mm_nn(a, b, res=None, out_dtype=F32, bt=False, tm_target=704, b_rows=None, name="mm_nn"):
    M, K = a.shape
    N = b.shape[0] if bt else b.shape[1]
    tm = _tile(M, tm_target, 16)
    if b_rows is None:
        tn = _tile(N, 2816, 128)
    else:
        tn, n_tiles, start = b_rows
        N = tn * n_tiles

    def body(*refs):
        if res is None:
            a_ref, b_ref, o_ref = refs
        else:
            a_ref, b_ref, r_ref, o_ref = refs
        acc = lax.dot_general(a_ref[...], b_ref[...], _NT if bt else _NN, preferred_element_type=F32)
        if res is not None:
            acc = acc + r_ref[...]
        o_ref[...] = acc.astype(out_dtype)

    b_spec = pl.BlockSpec((tn, K), lambda j, i: (j, 0)) if bt else pl.BlockSpec((K, tn), lambda j, i: (0, j))
    if b_rows is not None:
        b_spec = pl.BlockSpec((pl.Element(tn), pl.Element(K)), lambda j, i: (pl.multiple_of(start(j), 16), 0))
    in_specs = [pl.BlockSpec((tm, K), lambda j, i: (i, 0)), b_spec]
    args = [a, b]
    if res is not None:
        in_specs.append(pl.BlockSpec((tm, tn), lambda j, i: (i, j)))
        args.append(res)
    return pl.pallas_call(
        body, grid=(N // tn, M // tm), in_specs=in_specs,
        out_specs=pl.BlockSpec((tm, tn), lambda j, i: (i, j)),
        out_shape=jax.ShapeDtypeStruct((M, N), out_dtype), name=name)(*args)


def _mm_sum(pairs, name):
    M = pairs[0][0].shape[0]
    N = pairs[0][1].shape[1]
    tm = _tile(M, 704, 16)
    n = len(pairs)

    def body(*refs):
        o_ref = refs[-1]
        acc = jnp.dot(refs[0][...], refs[1][...], preferred_element_type=F32)
        for i in range(1, n):
            acc = acc + jnp.dot(refs[2 * i][...], refs[2 * i + 1][...], preferred_element_type=F32)
        o_ref[...] = acc

    specs, args = [], []
    for a, b in pairs:
        specs += [pl.BlockSpec((tm, a.shape[1]), lambda i: (i, 0)),
                  pl.BlockSpec(b.shape, lambda i: (0, 0), pipeline_mode=pl.Buffered(1))]
        args += [a, b]
    return pl.pallas_call(
        body, grid=(M // tm,), in_specs=specs, out_specs=pl.BlockSpec((tm, N), lambda i: (i, 0)),
        out_shape=jax.ShapeDtypeStruct((M, N), F32), name=name)(*args)


def _mm_nt(a, b, res=None, name="mm_nt"):
    M, Nc = a.shape
    K = b.shape[0]
    tm = _tile(M, 704, 16)
    tc = _tile(Nc, 5632, 128)

    def body(*refs):
        if res is None:
            a_ref, b_ref, o_ref = refs
        else:
            a_ref, b_ref, r_ref, o_ref = refs
        c = pl.program_id(1)
        p = lax.dot_general(a_ref[...], b_ref[...], (((1,), (1,)), ((), ())), preferred_element_type=F32)

        @pl.when(c == 0)
        def _():
            if res is None:
                o_ref[...] = p
            else:
                o_ref[...] = p + r_ref[...]

        @pl.when(c > 0)
        def _():
            o_ref[...] += p

    in_specs = [pl.BlockSpec((tm, tc), lambda i, c: (i, c)), pl.BlockSpec((K, tc), lambda i, c: (0, c))]
    args = [a, b]
    if res is not None:
        in_specs.append(pl.BlockSpec((tm, K), lambda i, c: (i, 0)))
        args.append(res)
    return pl.pallas_call(
        body, grid=(M // tm, Nc // tc), in_specs=in_specs,
        out_specs=pl.BlockSpec((tm, K), lambda i, c: (i, 0)),
        out_shape=jax.ShapeDtypeStruct((M, K), F32), name=name)(*args)


def _mm_tn(a, b, out_dtype=F32, name="mm_tn"):
    M, K = a.shape
    N = b.shape[1]
    tm = _tile(M, 2752, 16)
    tk = _tile(K, 1408, 128)
    tn = _tile(N, 1408, 128)
    steps = M // tm

    def body(a_ref, b_ref, o_ref, *scratch):
        acc = scratch[0] if scratch else o_ref
        m = pl.program_id(2)
        p = lax.dot_general(a_ref[...], b_ref[...], (((0,), (0,)), ((), ())), preferred_element_type=F32)

        @pl.when(m == 0)
        def _():
            acc[...] = p

        @pl.when(m > 0)
        def _():
            acc[...] += p

        if scratch:
            @pl.when(m == steps - 1)
            def _():
                o_ref[...] = acc[...].astype(out_dtype)

    return pl.pallas_call(
        body, grid=(K // tk, N // tn, steps),
        in_specs=[pl.BlockSpec((tm, tk), lambda kk, j, m: (m, kk)), pl.BlockSpec((tm, tn), lambda kk, j, m: (m, j))],
        out_specs=pl.BlockSpec((tk, tn), lambda kk, j, m: (kk, j)),
        out_shape=jax.ShapeDtypeStruct((K, N), out_dtype),
        scratch_shapes=[] if out_dtype == F32 else [pltpu.VMEM((tk, tn), F32)], name=name)(a, b)


class _Rows:
    def __init__(self, body, first, head=None):
        self.body, self.first, self.head = body, first, head
        self.shape = (body.shape[0] + first, body.shape[1])


def _rows_operands(x, tr):
    if not isinstance(x, _Rows):
        return [x], [pl.BlockSpec((tr, x.shape[1]), lambda i: (i, 0))]
    assert x.first % 8 == 0 and x.first <= tr <= x.body.shape[0] and x.shape[0] % tr == 0
    width = x.shape[1]
    args = [x.body]
    specs = [pl.BlockSpec((pl.Element(tr), pl.Element(width)),
                          lambda i: (pl.multiple_of(jnp.maximum(i * tr - x.first, 0), 8), 0))]
    if x.head is not None:
        args.append(jnp.pad(x.head, ((0, tr - x.first), (0, 0))))
        specs.append(pl.BlockSpec((tr, width), lambda i: (0, 0)))
    return args, specs


def _rows_tile(x, refs, i, tr):
    blk = refs[0][...]
    if not isinstance(x, _Rows):
        return blk
    shifted = pltpu.roll(blk, x.first, 0)
    if x.head is not None:
        row = lax.broadcasted_iota(jnp.int32, (tr, 1), 0)
        shifted = jnp.where(row < x.first, refs[1][...], shifted)
    return jnp.where(i == 0, shifted, blk)


def _rms_fwd(x, g, name):
    Lp = x.shape[0]
    tr = _tile(Lp, 256, 16)
    args, specs = _rows_operands(x, tr)
    n = len(args)

    def body(*refs):
        g_ref, o_ref = refs[n:]
        xv = _rows_tile(x, refs[:n], pl.program_id(0), tr)
        r = lax.rsqrt(jnp.mean(xv * xv, axis=-1, keepdims=True) + EPS)
        o_ref[...] = (xv * r * g_ref[...]).astype(BF16)

    return pl.pallas_call(
        body, grid=(Lp // tr,),
        in_specs=specs + [pl.BlockSpec((1, D_MODEL), lambda i: (0, 0))],
        out_specs=pl.BlockSpec((tr, D_MODEL), lambda i: (i, 0)),
        out_shape=jax.ShapeDtypeStruct((Lp, D_MODEL), BF16), name=name)(*args, g)


class _Producer:
    def __init__(self, a, b, res=None):
        self.a, self.b, self.res = a, b, res
        self.tr = _tile(a.shape[0], 704, 16)
        K = a.shape[1]
        r_args, r_specs = ([], []) if res is None else _rows_operands(res, self.tr)
        self.args = [a, b] + r_args
        self.specs = [pl.BlockSpec((self.tr, K), lambda i: (i, 0)),
                      pl.BlockSpec((K, D_MODEL), lambda i: (0, 0), pipeline_mode=pl.Buffered(1))] + r_specs

    def tile(self, refs, i):
        acc = jnp.dot(refs[0][...], refs[1][...], preferred_element_type=F32)
        return acc if self.res is None else acc + _rows_tile(self.res, refs[2:], i, self.tr)


def _mm_rms_fwd(prod, g, name):
    Lp, tr, n = prod.a.shape[0], prod.tr, len(prod.args)

    def body(*refs):
        g_ref, x_ref, o_ref = refs[n:]
        xv = prod.tile(refs[:n], pl.program_id(0))
        r = lax.rsqrt(jnp.mean(xv * xv, axis=-1, keepdims=True) + EPS)
        x_ref[...] = xv
        o_ref[...] = (xv * r * g_ref[...]).astype(BF16)

    blk = pl.BlockSpec((tr, D_MODEL), lambda i: (i, 0))
    return pl.pallas_call(
        body, grid=(Lp // tr,), in_specs=prod.specs + [pl.BlockSpec((1, D_MODEL), lambda i: (0, 0))],
        out_specs=[blk, blk],
        out_shape=[jax.ShapeDtypeStruct((Lp, D_MODEL), F32), jax.ShapeDtypeStruct((Lp, D_MODEL), BF16)],
        name=name)(*prod.args, g)


def _rms_bwd(x, g, dy, dres, pad, name):
    Lp = x.shape[0]
    fused = isinstance(dy, _Producer)
    tr = dy.tr if fused else _tile(Lp, 256, 16)
    n = len(dy.args) if fused else 1
    x_args, x_specs = _rows_operands(x, tr)
    nx = len(x_args)

    def body(*refs):
        g_ref, dr_ref, dx_ref, dxb_ref, dg_ref = refs[n + nx:]
        i = pl.program_id(0)
        xv = _rows_tile(x, refs[n:n + nx], i, tr)
        r = lax.rsqrt(jnp.mean(xv * xv, axis=-1, keepdims=True) + EPS)
        xh = xv * r
        dyv = dy.tile(refs[:n], i) if fused else refs[0][...]
        dxh = dyv * g_ref[...]
        dx = r * (dxh - xh * jnp.mean(dxh * xh, axis=-1, keepdims=True)) + dr_ref[...]
        row = i * tr + lax.broadcasted_iota(jnp.int32, (tr, 1), 0)
        dx = jnp.where(row >= pad, dx, 0.0)
        dx_ref[...] = dx
        dxb_ref[...] = dx.astype(BF16)
        part = jnp.sum(dyv * xh, axis=0, keepdims=True)

        @pl.when(i == 0)
        def _():
            dg_ref[...] = part

        @pl.when(i > 0)
        def _():
            dg_ref[...] += part

    blk = pl.BlockSpec((tr, D_MODEL), lambda i: (i, 0))
    vec = pl.BlockSpec((1, D_MODEL), lambda i: (0, 0))
    return pl.pallas_call(
        body, grid=(Lp // tr,), in_specs=(dy.specs if fused else [blk]) + x_specs + [vec, blk],
        out_specs=[blk, blk, vec],
        out_shape=[jax.ShapeDtypeStruct((Lp, D_MODEL), F32), jax.ShapeDtypeStruct((Lp, D_MODEL), BF16),
                   jax.ShapeDtypeStruct((1, D_MODEL), F32)], name=name)(*(dy.args if fused else [dy]), *x_args, g, dres)


def _final(h2, g, tgt, first_row):
    fused = isinstance(h2, _Producer)
    Lp = h2.a.shape[0] if fused else h2.shape[0]
    tr = h2.tr if fused else _tile(Lp, 256, 16)
    n = len(h2.args) if fused else 1
    t_args, t_specs = _rows_operands(tgt, tr)
    nt = len(t_args)

    def body(*refs):
        g_ref = refs[n]
        loss_ref, dx_ref, dxb_ref, dg_ref = refs[n + 1 + nt:]
        i = pl.program_id(0)
        xv = h2.tile(refs[:n], i) if fused else refs[0][...]
        tv = _rows_tile(tgt, refs[n + 1:n + 1 + nt], i, tr)
        gv = g_ref[...]
        r = lax.rsqrt(jnp.mean(xv * xv, axis=-1, keepdims=True) + EPS)
        xh = xv * r
        row = i * tr + lax.broadcasted_iota(jnp.int32, (tr, 1), 0)
        err = jnp.where(row >= first_row, xh * gv - tv, 0.0)
        lpart = jnp.sum(err * err, axis=0, keepdims=True) * (0.5 / D_MODEL)
        dyv = err * (1.0 / D_MODEL)
        dxh = dyv * gv
        dx = r * (dxh - xh * jnp.mean(dxh * xh, axis=-1, keepdims=True))
        dx_ref[...] = dx
        dxb_ref[...] = dx.astype(BF16)
        part = jnp.sum(dyv * xh, axis=0, keepdims=True)

        @pl.when(i == 0)
        def _():
            dg_ref[...] = part
            loss_ref[...] = lpart

        @pl.when(i > 0)
        def _():
            dg_ref[...] += part
            loss_ref[...] += lpart

    blk = pl.BlockSpec((tr, D_MODEL), lambda i: (i, 0))
    vec = pl.BlockSpec((1, D_MODEL), lambda i: (0, 0))
    return pl.pallas_call(
        body, grid=(Lp // tr,), in_specs=(h2.specs if fused else [blk]) + [vec] + t_specs,
        out_specs=[vec, blk, blk, vec],
        out_shape=[jax.ShapeDtypeStruct((1, D_MODEL), F32), jax.ShapeDtypeStruct((Lp, D_MODEL), F32),
                   jax.ShapeDtypeStruct((Lp, D_MODEL), BF16), jax.ShapeDtypeStruct((1, D_MODEL), F32)],
        name="final_norm_loss")(*(h2.args if fused else [h2]), g, *t_args)


def _halo_prev(tr, width, col=0):
    return pl.BlockSpec((8, width), lambda i: (jnp.maximum(i * (tr // 8) - 1, 0), col))


def _halo_next(tr, width, nrows, col=0, rows=8):
    last = nrows // rows - 1
    return pl.BlockSpec((rows, width), lambda i: (jnp.minimum((i + 1) * (tr // rows), last), col))


def _shifted(x, offs):
    n = x.shape[0]
    return [x if off == 0 else pltpu.roll(x, n - off, 0) for off in offs]


def _taps(wins, w, rows, bias=None):
    acc = w[0:1, :] * wins[0][0:rows, :]
    if bias is not None:
        acc = acc + bias
    for kk in range(1, len(wins)):
        acc = acc + w[kk:kk + 1, :] * wins[kk][0:rows, :]
    return acc


def _gdn_pre(proj_m, proj_s, conv_w, gparams, pad):
    Lp = proj_m.shape[0]
    tr = _tile(Lp, 192, 64)
    W3 = 3 * D_MODEL

    def body(main_ref, prev_ref, s_ref, w_ref, gp_ref, qkv_ref, gsm_ref, c_ref):
        i = pl.program_id(0)
        prev = jnp.where(i > 0, prev_ref[...], 0.0)
        ext = jnp.concatenate([prev, main_ref[...]], axis=0)
        c = _taps(_shifted(ext, range(8 - (GDN_CONV - 1), 9)), w_ref[...], tr)
        c_ref[...] = c.astype(BF16)
        s = c * _sig(c)
        scale = GDN_D ** -0.5
        for j in range(2 * GDN_H):
            seg = s[:, j * GDN_D:(j + 1) * GDN_D]
            r = lax.rsqrt(_rowsum(seg * seg) + EPS)
            if j < GDN_H:
                r = r * scale
            qkv_ref[:, j * GDN_D:(j + 1) * GDN_D] = seg * r
        qkv_ref[:, 2 * D_MODEL:] = s[:, 2 * D_MODEL:]
        sm = s_ref[...]
        gp = gp_ref[...]
        lane = lax.broadcasted_iota(jnp.int32, sm.shape, 1)
        z = sm + gp[1:2, :]
        softplus = jnp.maximum(z, 0.0) + jnp.log(1.0 + jnp.exp(-jnp.abs(z)))
        lg = -jnp.exp(gp[0:1, :]) * softplus
        row = i * tr + lax.broadcasted_iota(jnp.int32, (tr, 1), 0)
        out = jnp.where(lane < GDN_H, lg, jnp.where(lane < 2 * GDN_H, _sig(sm), 0.0))
        gsm_ref[...] = jnp.where(row >= pad, out, 0.0)

    return pl.pallas_call(
        body, grid=(Lp // tr,),
        in_specs=[pl.BlockSpec((tr, W3), lambda i: (i, 0)), _halo_prev(tr, W3),
                  pl.BlockSpec((tr, LANES), lambda i: (i, 0)),
                  pl.BlockSpec((GDN_CONV, W3), lambda i: (0, 0)), pl.BlockSpec((8, LANES), lambda i: (0, 0))],
        out_specs=[pl.BlockSpec((tr, W3), lambda i: (i, 0)), pl.BlockSpec((tr, LANES), lambda i: (i, 0)),
                   pl.BlockSpec((tr, W3), lambda i: (i, 0))],
        out_shape=[jax.ShapeDtypeStruct((Lp, W3), F32), jax.ShapeDtypeStruct((Lp, LANES), F32),
                   jax.ShapeDtypeStruct((Lp, W3), BF16)],
        name="gdn_pre")(proj_m, proj_m, proj_s, conv_w, gparams)


def _gdn_pre_bwd(proj_m, conv_out, proj_s, conv_w, gparams, dq, dk, dv, dgs, pad):
    Lp = proj_m.shape[0]
    tr = _tile(Lp, 192, 64)
    W3 = 3 * D_MODEL
    te = tr + 8

    def body(main_ref, c_ref, cn_ref, s_ref, w_ref, gp_ref,
             dq_ref, dqn_ref, dk_ref, dkn_ref, dv_ref, dvn_ref, dgs_ref,
             da_ref, ds_ref, dw_ref, dgp_ref):
        i = pl.program_id(0)
        w = w_ref[...]
        c = jnp.concatenate([c_ref[...].astype(F32), cn_ref[...].astype(F32)[0:8]], axis=0)
        sg = _sig(c)
        s = c * sg
        rowe = i * tr + lax.broadcasted_iota(jnp.int32, (te, 1), 0)
        live = (rowe >= pad) & (rowe < Lp)
        dqe = jnp.concatenate([dq_ref[...], dqn_ref[...]], axis=0)
        dke = jnp.concatenate([dk_ref[...], dkn_ref[...]], axis=0)
        dve = jnp.concatenate([dv_ref[...], dvn_ref[...]], axis=0)
        scale = GDN_D ** -0.5
        parts = []
        for j in range(2 * GDN_H):
            seg = s[:, j * GDN_D:(j + 1) * GDN_D]
            r = lax.rsqrt(_rowsum(seg * seg) + EPS)
            xh = seg * r
            if j < GDN_H:
                dxh = dqe[:, j * GDN_D:(j + 1) * GDN_D] * scale
            else:
                dxh = dke[:, (j - GDN_H) * GDN_D:(j - GDN_H + 1) * GDN_D]
            parts.append(r * (dxh - xh * _rowsum(dxh * xh)))
        parts.append(dve)
        dsv = jnp.concatenate(parts, axis=1)
        dc = jnp.where(live, dsv * (sg * (1.0 + c * (1.0 - sg))), 0.0)
        dcs = _shifted(dc, range(GDN_CONV - 1, -1, -1))
        da_ref[...] = _taps(dcs, w, tr).astype(BF16)
        pm = main_ref[...]
        rows = [jnp.sum(dcs[kk][0:tr, :] * pm, axis=0, keepdims=True) for kk in range(GDN_CONV)]
        dwp = jnp.concatenate(rows + [jnp.zeros((8 - GDN_CONV, W3), F32)], axis=0)

        sm = s_ref[...]
        gp = gp_ref[...]
        lane = lax.broadcasted_iota(jnp.int32, sm.shape, 1)
        rowm = i * tr + lax.broadcasted_iota(jnp.int32, (tr, 1), 0)
        dgv = jnp.where(rowm >= pad, dgs_ref[...], 0.0)
        dlg = jnp.where(lane < GDN_H, dgv, 0.0)
        dbt = jnp.where((lane >= GDN_H) & (lane < 2 * GDN_H), dgv, 0.0)
        z = sm + gp[1:2, :]
        softplus = jnp.maximum(z, 0.0) + jnp.log(1.0 + jnp.exp(-jnp.abs(z)))
        ea = jnp.exp(gp[0:1, :])
        dz = dlg * (-ea) * _sig(z)
        dal = dlg * (-ea) * softplus
        bt = _sig(sm)
        dgb = dbt * bt * (1.0 - bt)
        ds_ref[...] = (dz + dgb).astype(BF16)
        gpp = jnp.concatenate([jnp.sum(dal, axis=0, keepdims=True), jnp.sum(dz, axis=0, keepdims=True),
                               jnp.zeros((6, LANES), F32)], axis=0)

        @pl.when(i == 0)
        def _():
            dw_ref[...] = dwp
            dgp_ref[...] = gpp

        @pl.when(i > 0)
        def _():
            dw_ref[...] += dwp
            dgp_ref[...] += gpp

    m3 = pl.BlockSpec((tr, W3), lambda i: (i, 0))
    m1 = pl.BlockSpec((tr, D_MODEL), lambda i: (i, 0))
    n1 = _halo_next(tr, D_MODEL, Lp)
    return pl.pallas_call(
        body, grid=(Lp // tr,),
        in_specs=[m3, m3, _halo_next(tr, W3, Lp, rows=16), pl.BlockSpec((tr, LANES), lambda i: (i, 0)),
                  pl.BlockSpec((GDN_CONV, W3), lambda i: (0, 0)), pl.BlockSpec((8, LANES), lambda i: (0, 0)),
                  m1, n1, m1, n1, m1, n1, pl.BlockSpec((tr, LANES), lambda i: (i, 0))],
        out_specs=[m3, pl.BlockSpec((tr, LANES), lambda i: (i, 0)),
                   pl.BlockSpec((8, W3), lambda i: (0, 0)), pl.BlockSpec((8, LANES), lambda i: (0, 0))],
        out_shape=[jax.ShapeDtypeStruct((Lp, W3), BF16), jax.ShapeDtypeStruct((Lp, LANES), BF16),
                   jax.ShapeDtypeStruct((8, W3), F32), jax.ShapeDtypeStruct((8, LANES), F32)],
        name="gdn_pre_bwd")(proj_m, conv_out, conv_out, proj_s, conv_w, gparams, dq, dq, dk, dk, dv, dv, dgs)


def _gdn_gates(gs):
    ri = lax.broadcasted_iota(jnp.int32, (CHUNK, CHUNK), 0)
    ci = lax.broadcasted_iota(jnp.int32, (CHUNK, CHUNK), 1)
    tril = ri >= ci
    strict = ri > ci
    gall = _dx(tril.astype(F32), gs)
    lane8 = lax.broadcasted_iota(jnp.int32, (8, LANES), 1)
    sub8 = lax.broadcasted_iota(jnp.int32, (8, LANES), 0)
    grow = _dxnt((lane8 == sub8).astype(F32), gall)
    return gall, grow, tril, strict


def _gdn_decay(gall, grow, tril, h):
    g = gall[:, h:h + 1]
    return g, jnp.where(tril, jnp.exp(jnp.where(tril, g - grow[h:h + 1, :], 0.0)), 0.0)


def _group(N):
    return 3 if N % 3 == 0 else (2 if N % 2 == 0 else 1)


def _gdn_chunk_specs(N, rev):
    G = _group(N)
    nb = N // G
    cn = (lambda n: nb - 1 - n) if rev else (lambda n: n)
    col = lambda j: pl.BlockSpec((G * CHUNK, D_MODEL), lambda n: (cn(n), j))
    gate = pl.BlockSpec((G * CHUNK, LANES), lambda n: (cn(n), 0))
    st = lambda a, b: pl.BlockSpec((GDN_H, G, a, b), lambda n: (0, cn(n), 0, 0))
    return G, nb, col, gate, st


def _gdn_chunk_fwd(qkv, gsm):
    Lp = qkv.shape[0]
    N = Lp // CHUNK
    G, nb, col, gate, st = _gdn_chunk_specs(N, False)

    def body(q_ref, k_ref, v_ref, gs_ref, o_ref, sin_ref, t_ref, S):
        n = pl.program_id(0)

        @pl.when(n == 0)
        def _():
            S[...] = jnp.zeros_like(S)

        ri = lax.broadcasted_iota(jnp.int32, (CHUNK, CHUNK), 0)
        ci = lax.broadcasted_iota(jnp.int32, (CHUNK, CHUNK), 1)
        eye = (ri == ci).astype(F32)
        heads = range(GDN_H)
        sls = [slice(h * GDN_D, (h + 1) * GDN_D) for h in heads]
        rows = [slice(c * CHUNK, (c + 1) * CHUNK) for c in range(G)]
        pairs = [(c, h) for c in range(G) for h in heads]
        P = lambda f: {p: f(*p) for p in pairs}
        gs = [gs_ref[rows[c], :] for c in range(G)]
        gates = [_gdn_gates(gs[c]) for c in range(G)]
        tril, strict = gates[0][2], gates[0][3]
        q = P(lambda c, h: q_ref[rows[c], sls[h]])
        k = P(lambda c, h: k_ref[rows[c], sls[h]])
        v = P(lambda c, h: v_ref[rows[c], sls[h]])
        beta = P(lambda c, h: gs[c][:, GDN_H + h:GDN_H + h + 1])
        gg = P(lambda c, h: _gdn_decay(gates[c][0], gates[c][1], tril, h))
        g = {p: x[0] for p, x in gg.items()}
        gam = {p: x[1] for p, x in gg.items()}
        eg = P(lambda c, h: jnp.exp(g[c, h]))
        gl = P(lambda c, h: g[c, h][CHUNK - 1:CHUNK, :])
        kb = P(lambda c, h: k[c, h] * beta[c, h])
        pw = P(lambda c, h: -jnp.where(strict, _dnt(kb[c, h], k[c, h]) * gam[c, h], 0.0))
        p = P(lambda c, h: _dnt(q[c, h], k[c, h]) * gam[c, h])
        t = P(lambda c, h: eye + pw[c, h])
        for _ in range(5):
            pw = P(lambda c, h: _d3g(pw[c, h], pw[c, h], _NN))
            t = P(lambda c, h: t[c, h] + _d3g(t[c, h], pw[c, h], _NN))
        u = P(lambda c, h: _d(t[c, h], v[c, h] * beta[c, h]))
        w = P(lambda c, h: _d(t[c, h], kb[c, h] * eg[c, h]))
        qg = P(lambda c, h: q[c, h] * eg[c, h])
        kd = P(lambda c, h: k[c, h] * jnp.exp(gl[c, h] - g[c, h]))
        egl = P(lambda c, h: jnp.exp(gl[c, h]))
        for c in range(G):
            for h in heads:
                t_ref[h, c] = t[c, h]
        cur = [S[h] for h in heads]
        for c in range(G):
            vnew = [u[c, h] - _d(w[c, h], cur[h]) for h in heads]
            for h in heads:
                o_ref[rows[c], sls[h]] = _d(qg[c, h], cur[h]) + _d(p[c, h], vnew[h])
                sin_ref[h, c] = cur[h]
            cur = [cur[h] * egl[c, h] + _dtn(kd[c, h], vnew[h]) for h in heads]
        for h in heads:
            S[h] = cur[h]

    return pl.pallas_call(
        body, grid=(nb,),
        in_specs=[col(0), col(1), col(2), gate],
        out_specs=[col(0), st(GDN_D, GDN_D), st(CHUNK, CHUNK)],
        out_shape=[jax.ShapeDtypeStruct((Lp, D_MODEL), F32), jax.ShapeDtypeStruct((GDN_H, N, GDN_D, GDN_D), F32),
                   jax.ShapeDtypeStruct((GDN_H, N, CHUNK, CHUNK), F32)],
        scratch_shapes=[pltpu.VMEM((GDN_H, GDN_D, GDN_D), F32)],
        name="gdn_chunk_fwd")(qkv, qkv, qkv, gsm)


def _gdn_chunk_bwd(qkv, gsm, do, s_in, t_in):
    Lp = qkv.shape[0]
    N = Lp // CHUNK
    G, nb, col, gate, st = _gdn_chunk_specs(N, True)

    def body(q_ref, k_ref, v_ref, gs_ref, do_ref, sin_ref, t_ref, dq_ref, dk_ref, dv_ref, dgs_ref, dS):
        n = pl.program_id(0)

        @pl.when(n == 0)
        def _():
            dS[...] = jnp.zeros_like(dS)

        lane = lax.broadcasted_iota(jnp.int32, (CHUNK, LANES), 1)
        rcol = lax.broadcasted_iota(jnp.int32, (CHUNK, 1), 0)
        ri = lax.broadcasted_iota(jnp.int32, (CHUNK, CHUNK), 0)
        ci = lax.broadcasted_iota(jnp.int32, (CHUNK, CHUNK), 1)
        ones = jnp.ones((CHUNK, LANES), F32)
        heads = range(GDN_H)
        sls = [slice(h * GDN_D, (h + 1) * GDN_D) for h in heads]
        rows = [slice(c * CHUNK, (c + 1) * CHUNK) for c in range(G)]
        pairs = [(c, h) for c in range(G) for h in heads]
        P = lambda f: {p: f(*p) for p in pairs}
        gs = [gs_ref[rows[c], :] for c in range(G)]
        gates = [_gdn_gates(gs[c]) for c in range(G)]
        tril, strict = gates[0][2], gates[0][3]
        q = P(lambda c, h: q_ref[rows[c], sls[h]])
        k = P(lambda c, h: k_ref[rows[c], sls[h]])
        v = P(lambda c, h: v_ref[rows[c], sls[h]])
        dov = P(lambda c, h: do_ref[rows[c], sls[h]])
        s0 = P(lambda c, h: sin_ref[h, c])
        t = P(lambda c, h: t_ref[h, c])
        beta = P(lambda c, h: gs[c][:, GDN_H + h:GDN_H + h + 1])
        gg = P(lambda c, h: _gdn_decay(gates[c][0], gates[c][1], tril, h))
        g = {p: x[0] for p, x in gg.items()}
        gam = {p: x[1] for p, x in gg.items()}
        eg = P(lambda c, h: jnp.exp(g[c, h]))
        egl = P(lambda c, h: jnp.exp(g[c, h][CHUNK - 1:CHUNK, :]))
        e = P(lambda c, h: jnp.exp(g[c, h][CHUNK - 1:CHUNK, :] - g[c, h]))
        kb = P(lambda c, h: k[c, h] * beta[c, h])
        kbg = P(lambda c, h: kb[c, h] * eg[c, h])
        vb = P(lambda c, h: v[c, h] * beta[c, h])
        qg = P(lambda c, h: q[c, h] * eg[c, h])
        kd = P(lambda c, h: k[c, h] * e[c, h])
        m = P(lambda c, h: jnp.where(strict, _dnt(kb[c, h], k[c, h]) * gam[c, h], 0.0))
        u = P(lambda c, h: _d(t[c, h], vb[c, h]))
        w = P(lambda c, h: _d(t[c, h], kbg[c, h]))
        p = P(lambda c, h: _dnt(q[c, h], k[c, h]) * gam[c, h])
        dqg = P(lambda c, h: _dnt(dov[c, h], s0[c, h]))
        qgdo = P(lambda c, h: _dtn(qg[c, h], dov[c, h]))
        ptdo = P(lambda c, h: _dtn(p[c, h], dov[c, h]))
        vnew = P(lambda c, h: u[c, h] - _d(w[c, h], s0[c, h]))
        dp = P(lambda c, h: jnp.where(tril, _dnt(dov[c, h], vnew[c, h]), 0.0))
        cur = [dS[h] for h in heads]
        dvnew, dkd, sds = {}, {}, {}
        for c in reversed(range(G)):
            for h in heads:
                dvnew[c, h] = ptdo[c, h] + _d(kd[c, h], cur[h])
                dkd[c, h] = _dnt(vnew[c, h], cur[h])
                sds[c, h] = _allsum(s0[c, h] * cur[h])
            cur = [qgdo[c, h] + egl[c, h] * cur[h] - _dtn(w[c, h], dvnew[c, h]) for h in heads]
        for h in heads:
            dS[h] = cur[h]
        dw = P(lambda c, h: -_dnt(dvnew[c, h], s0[c, h]))
        dvb = P(lambda c, h: _dtn(t[c, h], dvnew[c, h]))
        dkbg = P(lambda c, h: _dtn(t[c, h], dw[c, h]))
        dt = P(lambda c, h: _dnt(dvnew[c, h], vb[c, h]) + _dnt(dw[c, h], kbg[c, h]))
        x1 = P(lambda c, h: _d3g(t[c, h], dt[c, h], _TN))
        dm = P(lambda c, h: jnp.where(strict, -_d3g(x1[c, h], t[c, h], _NT), 0.0))
        dkk = P(lambda c, h: dm[c, h] * gam[c, h])
        dqk = P(lambda c, h: dp[c, h] * gam[c, h])
        dkb = P(lambda c, h: _d(dkk[c, h], k[c, h]) + eg[c, h] * dkbg[c, h])
        em = P(lambda c, h: dm[c, h] * m[c, h] + dp[c, h] * p[c, h])
        colsum = P(lambda c, h: _d3g(em[c, h], ones, _TN)[:, 0:1])
        for c, h in pairs:
            dk_ref[rows[c], sls[h]] = (_dtn(dkk[c, h], kb[c, h]) + _dtn(dqk[c, h], q[c, h]) + dkd[c, h] * e[c, h]
                                       + beta[c, h] * dkb[c, h])
            dq_ref[rows[c], sls[h]] = _d(dqk[c, h], k[c, h]) + dqg[c, h] * eg[c, h]
            dv_ref[rows[c], sls[h]] = beta[c, h] * dvb[c, h]
        for c in range(G):
            dg_all = jnp.zeros((CHUNK, LANES), F32)
            dbeta_all = jnp.zeros((CHUNK, LANES), F32)
            for h in heads:
                dbeta = _rowsum(k[c, h] * dkb[c, h]) + _rowsum(v[c, h] * dvb[c, h])
                z = _rowsum(kd[c, h] * dkd[c, h])
                dg = (_rowsum(em[c, h]) - colsum[c, h] + _rowsum(qg[c, h] * dqg[c, h]) + _rowsum(kbg[c, h] * dkbg[c, h])
                      - z)
                extra = _allsum(z) + egl[c, h] * sds[c, h]
                dg = dg + jnp.where(rcol == CHUNK - 1, extra, 0.0)
                dg_all = dg_all + jnp.where(lane == h, dg, 0.0)
                dbeta_all = dbeta_all + jnp.where(lane == GDN_H + h, dbeta, 0.0)
            dgs_ref[rows[c], :] = _dx((ci >= ri).astype(F32), dg_all) + dbeta_all

    return pl.pallas_call(
        body, grid=(nb,),
        in_specs=[col(0), col(1), col(2), gate, col(0), st(GDN_D, GDN_D), st(CHUNK, CHUNK)],
        out_specs=[col(0), col(0), col(0), gate],
        out_shape=[jax.ShapeDtypeStruct((Lp, D_MODEL), F32)] * 3 + [jax.ShapeDtypeStruct((Lp, LANES), F32)],
        scratch_shapes=[pltpu.VMEM((GDN_H, GDN_D, GDN_D), F32)],
        name="gdn_chunk_bwd")(qkv, qkv, qkv, gsm, do, s_in, t_in)


def _rot(x, c, s):
    half = RET_D // 2
    x1 = x[:, :half]
    x2 = x[:, half:]
    return jnp.concatenate([x1 * c - x2 * s, x2 * c + x1 * s], axis=1)


def _rot_bwd(d, c, s):
    half = RET_D // 2
    d1 = d[:, :half]
    d2 = d[:, half:]
    return jnp.concatenate([d1 * c + d2 * s, d2 * c - d1 * s], axis=1)


def _ret_tables():
    hh = jnp.arange(RET_H, dtype=F32)
    lg = jnp.log(1.0 - 2.0 ** (-5.0 - hh))
    idx = jnp.arange(CHUNK, dtype=F32)
    tril = jnp.asarray(np.tril(np.ones((CHUNK, CHUNK), dtype=bool)))
    dmask = jnp.where(tril, jnp.exp((idx[:, None] - idx[None, :]) * lg[:, None, None]), 0.0)
    qdec = jnp.exp((idx[None, :] + 1.0) * lg[:, None])
    kdec = jnp.exp((CHUNK - 1.0 - idx[None, :]) * lg[:, None])
    gch = jnp.exp(CHUNK * lg)
    qdec = jnp.broadcast_to(qdec[:, :, None], (RET_H, CHUNK, RET_D))
    kdec = jnp.broadcast_to(kdec[:, :, None], (RET_H, CHUNK, RET_D))
    gch = jnp.broadcast_to(gch[:, None, None], (RET_H, 8, LANES))
    return dmask, qdec, kdec, gch


def _ret_specs(N, rev):
    G = _group(N)
    nb = N // G
    cn = (lambda n: nb - 1 - n) if rev else (lambda n: n)
    col = lambda j: pl.BlockSpec((G * CHUNK, D_MODEL), lambda n: (cn(n), j))
    tab = lambda a, b: pl.BlockSpec((RET_H, a, b), lambda n: (0, 0, 0))
    rope = pl.BlockSpec((G * CHUNK, LANES), lambda n: (cn(n), 0))
    st = pl.BlockSpec((RET_H, G, RET_D, RET_D), lambda n: (0, cn(n), 0, 0))
    return G, nb, col, tab, rope, st


def _ret_chunk_fwd(proj_m, cos, sin, tables):
    Lp = proj_m.shape[0]
    N = Lp // CHUNK
    dmask, qdec, kdec, gch = tables
    G, nb, col, tab, rope, st = _ret_specs(N, False)

    def body(q_ref, k_ref, v_ref, c_ref, s_ref, dm_ref, qd_ref, kd_ref, g_ref, o_ref, sin_ref, S):
        n = pl.program_id(0)

        @pl.when(n == 0)
        def _():
            S[...] = jnp.zeros_like(S)

        heads = range(RET_H)
        sls = [slice(h * RET_D, (h + 1) * RET_D) for h in heads]
        rows = [slice(c * CHUNK, (c + 1) * CHUNK) for c in range(G)]
        pairs = [(c, h) for c in range(G) for h in heads]
        P = lambda f: {p: f(*p) for p in pairs}
        qr = P(lambda c, h: _rot(q_ref[rows[c], sls[h]], c_ref[rows[c], :], s_ref[rows[c], :]))
        ks = P(lambda c, h: _rot(k_ref[rows[c], sls[h]], c_ref[rows[c], :], s_ref[rows[c], :]) * (RET_D ** -0.5))
        v = P(lambda c, h: v_ref[rows[c], sls[h]])
        a = P(lambda c, h: _dnt(qr[c, h], ks[c, h]) * dm_ref[h])
        av = P(lambda c, h: _d(a[c, h], v[c, h]))
        kv = P(lambda c, h: _dtn(ks[c, h] * kd_ref[h], v[c, h]))
        qd = P(lambda c, h: qr[c, h] * qd_ref[h])
        cur = [S[h] for h in heads]
        for c in range(G):
            for h in heads:
                o_ref[rows[c], sls[h]] = av[c, h] + _d(qd[c, h], cur[h])
                sin_ref[h, c] = cur[h].astype(BF16)
            cur = [cur[h] * g_ref[h, 0:1, 0:1] + kv[c, h] for h in heads]
        for h in heads:
            S[h] = cur[h]

    return pl.pallas_call(
        body, grid=(nb,),
        in_specs=[col(3), col(4), col(5), rope, rope,
                  tab(CHUNK, CHUNK), tab(CHUNK, RET_D), tab(CHUNK, RET_D), tab(8, LANES)],
        out_specs=[col(0), st],
        out_shape=[jax.ShapeDtypeStruct((Lp, D_MODEL), F32), jax.ShapeDtypeStruct((RET_H, N, RET_D, RET_D), BF16)],
        scratch_shapes=[pltpu.VMEM((RET_H, RET_D, RET_D), F32)],
        name="ret_chunk_fwd")(proj_m, proj_m, proj_m, cos, sin, dmask, qdec, kdec, gch)


def _ret_chunk_bwd(proj_m, cos, sin, tables, do, s_in):
    Lp = proj_m.shape[0]
    N = Lp // CHUNK
    dmask, qdec, kdec, gch = tables
    G, nb, col, tab, rope, st = _ret_specs(N, True)

    def body(q_ref, k_ref, v_ref, c_ref, s_ref, dm_ref, qd_ref, kd_ref, g_ref, do_ref, sin_ref,
             d_ref, dS):
        n = pl.program_id(0)

        @pl.when(n == 0)
        def _():
            dS[...] = jnp.zeros_like(dS)

        kscale = RET_D ** -0.5
        heads = range(RET_H)
        sls = [slice(h * RET_D, (h + 1) * RET_D) for h in heads]
        rows = [slice(c * CHUNK, (c + 1) * CHUNK) for c in range(G)]
        pairs = [(c, h) for c in range(G) for h in heads]
        P = lambda f: {p: f(*p) for p in pairs}
        cs = [(c_ref[rows[c], :], s_ref[rows[c], :]) for c in range(G)]
        osl = lambda part, h: slice(part * D_MODEL + h * RET_D, part * D_MODEL + (h + 1) * RET_D)
        qr = P(lambda c, h: _rot(q_ref[rows[c], sls[h]], *cs[c]))
        ks = P(lambda c, h: _rot(k_ref[rows[c], sls[h]], *cs[c]) * kscale)
        v = P(lambda c, h: v_ref[rows[c], sls[h]])
        dov = P(lambda c, h: do_ref[rows[c], sls[h]])
        ad = P(lambda c, h: _dnt(qr[c, h], ks[c, h]) * dm_ref[h])
        da = P(lambda c, h: _dnt(dov[c, h], v[c, h]) * dm_ref[h])
        dos = P(lambda c, h: _dnt(dov[c, h], sin_ref[h, c]) * qd_ref[h])
        qdo = P(lambda c, h: _dtn(qr[c, h] * qd_ref[h], dov[c, h]))
        adv = P(lambda c, h: _dtn(ad[c, h], dov[c, h]))
        dqr = P(lambda c, h: _d(da[c, h], ks[c, h]) + dos[c, h])
        daq = P(lambda c, h: _dtn(da[c, h], qr[c, h]))
        kk = P(lambda c, h: ks[c, h] * kd_ref[h])
        cur = [dS[h] for h in heads]
        for c in reversed(range(G)):
            for h in heads:
                d_ref[rows[c], osl(2, h)] = (adv[c, h] + _d(kk[c, h], cur[h])).astype(BF16)
                d_ref[rows[c], osl(0, h)] = _rot_bwd(dqr[c, h], *cs[c]).astype(BF16)
                dks = daq[c, h] + _dnt(v[c, h], cur[h]) * kd_ref[h]
                d_ref[rows[c], osl(1, h)] = _rot_bwd(dks * kscale, *cs[c]).astype(BF16)
            cur = [cur[h] * g_ref[h, 0:1, 0:1] + qdo[c, h] for h in heads]
        for h in heads:
            dS[h] = cur[h]

    return pl.pallas_call(
        body, grid=(nb,),
        in_specs=[col(3), col(4), col(5), rope, rope,
                  tab(CHUNK, CHUNK), tab(CHUNK, RET_D), tab(CHUNK, RET_D), tab(8, LANES), col(0), st],
        out_specs=pl.BlockSpec((G * CHUNK, 3 * D_MODEL), lambda n: (nb - 1 - n, 0)),
        out_shape=jax.ShapeDtypeStruct((Lp, 3 * D_MODEL), BF16),
        scratch_shapes=[pltpu.VMEM((RET_H, RET_D, RET_D), F32)],
        name="ret_chunk_bwd")(proj_m, proj_m, proj_m, cos, sin, dmask, qdec, kdec, gch, do, s_in)


def _merge_specs(tr):
    col = lambda j: pl.BlockSpec((tr, D_MODEL), lambda i: (i, j))
    return col


def _merge_fwd(o_a, o_b, proj_m, gnorm):
    Lp = o_a.shape[0]
    tr = _tile(Lp, 192, 16)

    def body(oa_ref, ob_ref, gz_ref, rg_ref, ga_ref, gb_ref, gn_ref, y_ref):
        gn = gn_ref[...]
        oa = oa_ref[...]
        ob = ob_ref[...]
        gz = gz_ref[...]
        ya = []
        for j in range(GDN_H):
            seg = oa[:, j * GDN_D:(j + 1) * GDN_D]
            r = lax.rsqrt(jnp.mean(seg * seg, axis=-1, keepdims=True) + EPS)
            ya.append(seg * r * gn)
        ya = jnp.concatenate(ya, axis=1) * (gz * _sig(gz))
        yb = []
        for j in range(RET_H):
            seg = ob[:, j * RET_D:(j + 1) * RET_D]
            r = lax.rsqrt(jnp.mean(seg * seg, axis=-1, keepdims=True) + EPS)
            yb.append(seg * r)
        rg = rg_ref[...]
        yb = jnp.concatenate(yb, axis=1) * (rg * _sig(rg))
        y_ref[...] = (_sig(ga_ref[...]) * ya + _sig(gb_ref[...]) * yb).astype(BF16)

    col = _merge_specs(tr)
    return pl.pallas_call(
        body, grid=(Lp // tr,),
        in_specs=[col(0), col(0), col(6), col(7), col(8), col(9), pl.BlockSpec((1, GDN_D), lambda i: (0, 0))],
        out_specs=col(0), out_shape=jax.ShapeDtypeStruct((Lp, D_MODEL), BF16),
        name="merge_fwd")(o_a, o_b, proj_m, proj_m, proj_m, proj_m, gnorm)


def _merge_bwd(dh1b, w_out, o_a, o_b, proj_m, gnorm):
    Lp = o_a.shape[0]
    tr = _tile(Lp, 192, 16)

    def body(d_ref, wo_ref, oa_ref, ob_ref, gz_ref, rg_ref, ga_ref, gb_ref, gn_ref, dc_ref, doa_ref, dob_ref, dgn_ref):
        i = pl.program_id(0)
        gn = gn_ref[...]
        dyv = lax.dot_general(d_ref[...], wo_ref[...], _NT, preferred_element_type=F32)
        oa = oa_ref[...]
        ob = ob_ref[...]
        gz = gz_ref[...]
        rg = rg_ref[...]
        sa = _sig(ga_ref[...])
        sb = _sig(gb_ref[...])
        dya = dyv * sa
        dyb = dyv * sb
        sgz = _sig(gz)
        szz = gz * sgz
        dgn = jnp.zeros((1, GDN_D), F32)
        ya = []
        dgz = []
        for j in range(GDN_H):
            sl = slice(j * GDN_D, (j + 1) * GDN_D)
            seg = oa[:, sl]
            r = lax.rsqrt(jnp.mean(seg * seg, axis=-1, keepdims=True) + EPS)
            xh = seg * r
            oan = xh * gn
            ya.append(oan * szz[:, sl])
            dgz.append(dya[:, sl] * oan * (sgz[:, sl] * (1.0 + gz[:, sl] * (1.0 - sgz[:, sl]))))
            doan = dya[:, sl] * szz[:, sl]
            dgn = dgn + jnp.sum(doan * xh, axis=0, keepdims=True)
            dxh = doan * gn
            doa_ref[:, sl] = r * (dxh - xh * jnp.mean(dxh * xh, axis=-1, keepdims=True))
        ya = jnp.concatenate(ya, axis=1)
        srg = _sig(rg)
        srr = rg * srg
        yb = []
        drg = []
        for j in range(RET_H):
            sl = slice(j * RET_D, (j + 1) * RET_D)
            seg = ob[:, sl]
            r = lax.rsqrt(jnp.mean(seg * seg, axis=-1, keepdims=True) + EPS)
            xh = seg * r
            yb.append(xh * srr[:, sl])
            drg.append(dyb[:, sl] * xh * (srg[:, sl] * (1.0 + rg[:, sl] * (1.0 - srg[:, sl]))))
            dxh = dyb[:, sl] * srr[:, sl]
            dob_ref[:, sl] = r * (dxh - xh * jnp.mean(dxh * xh, axis=-1, keepdims=True))
        yb = jnp.concatenate(yb, axis=1)
        dc_ref[:, 0:D_MODEL] = jnp.concatenate(dgz, axis=1).astype(BF16)
        dc_ref[:, D_MODEL:2 * D_MODEL] = jnp.concatenate(drg, axis=1).astype(BF16)
        dc_ref[:, 2 * D_MODEL:3 * D_MODEL] = (dyv * ya * sa * (1.0 - sa)).astype(BF16)
        dc_ref[:, 3 * D_MODEL:] = (dyv * yb * sb * (1.0 - sb)).astype(BF16)

        @pl.when(i == 0)
        def _():
            dgn_ref[...] = dgn

        @pl.when(i > 0)
        def _():
            dgn_ref[...] += dgn

    col = _merge_specs(tr)
    return pl.pallas_call(
        body, grid=(Lp // tr,),
        in_specs=[col(0), pl.BlockSpec((D_MODEL, D_MODEL), lambda i: (0, 0), pipeline_mode=pl.Buffered(1)),
                  col(0), col(0), col(6), col(7), col(8), col(9), pl.BlockSpec((1, GDN_D), lambda i: (0, 0))],
        out_specs=[pl.BlockSpec((tr, 4 * D_MODEL), lambda i: (i, 0)), col(0), col(0),
                   pl.BlockSpec((1, GDN_D), lambda i: (0, 0))],
        out_shape=[jax.ShapeDtypeStruct((Lp, 4 * D_MODEL), BF16), jax.ShapeDtypeStruct((Lp, D_MODEL), F32),
                   jax.ShapeDtypeStruct((Lp, D_MODEL), F32), jax.ShapeDtypeStruct((1, GDN_D), F32)],
        name="merge_bwd")(dh1b, w_out, o_a, o_b, proj_m, proj_m, proj_m, proj_m, gnorm)


def _ffn_act(up, conv_w, conv_b):
    Lp = up.shape[0]
    tr = _tile(Lp, 192, 16)
    W2 = 2 * D_FF

    def body(main_ref, prev_ref, w_ref, b_ref, act_ref, u_ref):
        i = pl.program_id(0)
        prev = jnp.where(i > 0, prev_ref[...], 0.0)
        ext = jnp.concatenate([prev, main_ref[...]], axis=0)
        u = _taps(_shifted(ext, range(8 - (FFN_CONV - 1), 9)), w_ref[...], tr, b_ref[...])
        a = u[:, :D_FF]
        act_ref[...] = (a * _sig(a) * u[:, D_FF:]).astype(BF16)
        u_ref[...] = u.astype(BF16)

    return pl.pallas_call(
        body, grid=(Lp // tr,),
        in_specs=[pl.BlockSpec((tr, W2), lambda i: (i, 0)), _halo_prev(tr, W2),
                  pl.BlockSpec((FFN_CONV, W2), lambda i: (0, 0)), pl.BlockSpec((1, W2), lambda i: (0, 0))],
        out_specs=[pl.BlockSpec((tr, D_FF), lambda i: (i, 0)), pl.BlockSpec((tr, W2), lambda i: (i, 0))],
        out_shape=[jax.ShapeDtypeStruct((Lp, D_FF), BF16), jax.ShapeDtypeStruct((Lp, W2), BF16)],
        name="ffn_act")(up, up, conv_w, conv_b)


def _ffn_act_bwd(up, u, dact, conv_w):
    Lp = up.shape[0]
    tr = _tile(Lp, 192, 16)
    W2 = 2 * D_FF
    te = tr + 8

    def body(up_ref, u_ref, un_ref, da_ref, dan_ref, w_ref, dup_ref, acc_ref):
        i = pl.program_id(0)
        w = w_ref[...]
        ue = jnp.concatenate([u_ref[...].astype(F32), un_ref[...].astype(F32)[0:8]], axis=0)
        a = ue[:, :D_FF]
        b = ue[:, D_FF:]
        rowe = i * tr + lax.broadcasted_iota(jnp.int32, (te, 1), 0)
        dae = jnp.where(rowe < Lp, jnp.concatenate([da_ref[...], dan_ref[...]], axis=0), 0.0)
        sg = _sig(a)
        du = jnp.concatenate([dae * b * (sg * (1.0 + a * (1.0 - sg))), dae * (a * sg)], axis=1)
        dus = _shifted(du, range(FFN_CONV - 1, -1, -1))
        dup_ref[...] = _taps(dus, w, tr).astype(BF16)
        upm = up_ref[...]
        rows = [jnp.sum(dus[kk][0:tr, :] * upm, axis=0, keepdims=True) for kk in range(FFN_CONV)]
        rows.append(jnp.sum(du[0:tr, :], axis=0, keepdims=True))
        part = jnp.concatenate(rows + [jnp.zeros((8 - len(rows), W2), F32)], axis=0)

        @pl.when(i == 0)
        def _():
            acc_ref[...] = part

        @pl.when(i > 0)
        def _():
            acc_ref[...] += part

    return pl.pallas_call(
        body, grid=(Lp // tr,),
        in_specs=[pl.BlockSpec((tr, W2), lambda i: (i, 0)), pl.BlockSpec((tr, W2), lambda i: (i, 0)),
                  _halo_next(tr, W2, Lp, rows=16), pl.BlockSpec((tr, D_FF), lambda i: (i, 0)), _halo_next(tr, D_FF, Lp),
                  pl.BlockSpec((FFN_CONV, W2), lambda i: (0, 0))],
        out_specs=[pl.BlockSpec((tr, W2), lambda i: (i, 0)), pl.BlockSpec((8, W2), lambda i: (0, 0))],
        out_shape=[jax.ShapeDtypeStruct((Lp, W2), BF16), jax.ShapeDtypeStruct((8, W2), F32)],
        name="ffn_act_bwd")(up, u, u, dact, dact, conv_w)


def _proj_rows(j):
    shift = (jnp.where((j >= 3) & (j < 6), _O_RQ - 3 * D_MODEL, 0) + jnp.where(j == 6, _O_GZ - 6 * D_MODEL, 0)
             + jnp.where(j >= 7, _O_RG - 7 * D_MODEL, 0))
    return j * D_MODEL + shift


def _local_step(hpad, tgt, pad, wt, first_weights=None, late_weights=None, on_ffn_out_grads=None,
                on_w_in_grads=None):
    Lp = hpad.shape[0]
    first = pad + N_META
    pos = jnp.arange(Lp, dtype=F32) - float(pad)
    half = RET_D // 2
    inv = 1.0 / (ROPE_BASE ** (jnp.arange(half, dtype=F32) / half))
    ang = pos[:, None] * inv[None, :]
    cos, sin = jnp.cos(ang), jnp.sin(ang)
    tables = _ret_tables()
    gparams = jnp.zeros((8, LANES), F32).at[0, :GDN_H].set(wt["a_log"]).at[1, :GDN_H].set(wt["dt_bias"])

    hn1 = _rms_fwd(hpad, wt["norm1"], "rms1_fwd")
    if first_weights is not None:
        wt = {**wt, **first_weights(hn1)}
    w_in_t = wt["w_in_t"]
    w_small_t = jnp.pad(w_in_t[_O_GA:_O_RQ], ((0, LANES - 2 * GDN_H), (0, 0)))
    proj_m = _mm_nn(hn1, w_in_t, bt=True, tm_target=1376, b_rows=(D_MODEL, MAIN_W // D_MODEL, _proj_rows),
                    name="proj_main")
    proj_s = _mm_nn(hn1, w_small_t, bt=True, name="proj_small")
    qkv, gsm, conv_out = _gdn_pre(proj_m, proj_s, wt["gdn_conv_w"], gparams, pad)
    o_a, s_a, t_a = _gdn_chunk_fwd(qkv, gsm)
    o_b, s_b = _ret_chunk_fwd(proj_m, cos, sin, tables)
    y = _merge_fwd(o_a, o_b, proj_m, wt["gdn_norm"])
    if late_weights is not None:
        wt = {**wt, **late_weights(y)}
    h1, hn2 = _mm_rms_fwd(_Producer(y, wt["w_out"], hpad), wt["norm2"], "out_proj_rms2")
    up = _mm_nn(hn2, wt["w_up_t"], bt=True, name="ffn_up")
    act, u_ffn = _ffn_act(up, wt["ffn_conv_w"], wt["ffn_conv_b"])
    lossvec, dh2, dh2b, d_norm_f = _final(_Producer(act, wt["w_down"], h1), wt["norm_f"], tgt, first)

    d_w_down = _mm_tn(act, dh2b, name="dw_down")
    dact = _mm_nt(dh2b, wt["w_down"], name="d_act")
    dup, ffn_rows = _ffn_act_bwd(up, u_ffn, dact, wt["ffn_conv_w"])
    d_w_up_t = _mm_tn(dup, hn2, name="dw_up")
    dh1, dh1b, d_norm2 = _rms_bwd(h1, wt["norm2"], _Producer(dup, wt["w_up_t"]), dh2, pad, "d_hn2_rms2_bwd")

    d_w_out = _mm_tn(y, dh1b, name="dw_out")
    gnorm = wt["gdn_norm"]
    if on_ffn_out_grads is not None:
        gnorm = gnorm + on_ffn_out_grads(d_w_down, d_w_up_t, d_w_out)[0:1, :]
    d_c, do_a, do_b, d_gnorm = _merge_bwd(dh1b, wt["w_out"], o_a, o_b, proj_m, gnorm)
    d_r = _ret_chunk_bwd(proj_m, cos, sin, tables, do_b, s_b)
    dq, dk, dv, dgs = _gdn_chunk_bwd(qkv, gsm, do_a, s_a, t_a)
    d_a, d_s, conv_rows, gp_rows = _gdn_pre_bwd(proj_m, conv_out, proj_s, wt["gdn_conv_w"], gparams, dq, dk, dv, dgs,
                                                pad)

    segs = [(d_a, w_in_t[_O_GQ:_O_GZ]), (d_r, w_in_t[_O_RQ:_O_RG]),
            (d_c, jnp.concatenate([w_in_t[_O_GZ:_O_GA], w_in_t[_O_RG:_O_END]], axis=0))]
    pa, pr, pc = [_mm_tn(d, hn1, BF16, name="dw_in_%d" % i) for i, (d, _) in enumerate(segs)]
    ps = _mm_tn(d_s, hn1, BF16, name="dw_in_small")
    d_w_in_t = jnp.concatenate([pa, pc[:D_MODEL], ps[:2 * GDN_H], pr, pc[D_MODEL:]], axis=0)
    if on_w_in_grads is not None:
        w_small_t = w_small_t + on_w_in_grads(d_w_in_t)[0:1, 0:1].astype(w_small_t.dtype)
    dhn1 = _mm_sum([(d_s, w_small_t)] + segs[:-1], "d_hn1_first")
    dh0, _, d_norm1 = _rms_bwd(hpad, wt["norm1"], _Producer(*segs[-1], dhn1), dh1, pad, "d_hn1_rms1_bwd")

    grads = {
        "norm1": d_norm1, "w_in_t": d_w_in_t, "gdn_conv_w": conv_rows[:GDN_CONV],
        "a_log": gp_rows[0, :GDN_H], "dt_bias": gp_rows[1, :GDN_H], "gdn_norm": d_gnorm, "w_out": d_w_out,
        "norm2": d_norm2, "w_up_t": d_w_up_t, "ffn_conv_w": ffn_rows[:FFN_CONV],
        "ffn_conv_b": ffn_rows[FFN_CONV:FFN_CONV + 1], "w_down": d_w_down, "norm_f": d_norm_f,
    }
    return lossvec, dh0, grads


def _peer(k):
    ix, iy, ic = lax.axis_index("x"), lax.axis_index("y"), lax.axis_index("c")
    px = 1 - ix if (k >> 2) & 1 else ix
    py = 1 - iy if (k >> 1) & 1 else iy
    pc = 1 - ic if k & 1 else ic
    return (px, py, pc), 4 * px + 2 * py + pc


def _comm_call(body, n, out_shapes, name, args):
    hbm = pl.BlockSpec(memory_space=pl.ANY)
    return pl.pallas_call(
        body, out_shape=out_shapes, in_specs=[hbm] * n, out_specs=[hbm] * n,
        scratch_shapes=[pltpu.SemaphoreType.DMA((n, N_DEV - 1)), pltpu.SemaphoreType.DMA((n, N_DEV - 1)),
                        pltpu.SemaphoreType.DMA((n,))],
        name=name)(*args)


def _all_gather(xs, name):
    n = len(xs)

    def body(*refs):
        x_refs, out_refs = refs[:n], refs[n:2 * n]
        send_sems, recv_sems, local_sems = refs[2 * n:]
        _, me = _peer(0)
        pending = []
        for i in range(n):
            local = pltpu.make_async_copy(x_refs[i], out_refs[i].at[me], local_sems.at[i])
            local.start()
            pending.append(local)
        sends = []
        for i in range(n):
            for k in range(1, N_DEV):
                dev, _ = _peer(k)
                cp = pltpu.make_async_remote_copy(
                    src_ref=x_refs[i], dst_ref=out_refs[i].at[me], send_sem=send_sems.at[i, k - 1],
                    recv_sem=recv_sems.at[i, k - 1], device_id=dev, device_id_type=MESH_T)
                cp.start()
                sends.append(cp)
        for i in range(n):
            for k in range(1, N_DEV):
                dev, idx = _peer(k)
                pltpu.make_async_remote_copy(
                    src_ref=x_refs[i], dst_ref=out_refs[i].at[idx], send_sem=send_sems.at[i, k - 1],
                    recv_sem=recv_sems.at[i, k - 1], device_id=dev, device_id_type=MESH_T).wait_recv()
        for cp in sends:
            cp.wait_send()
        for local in pending:
            local.wait()

    out_shapes = [jax.ShapeDtypeStruct((N_DEV,) + a.shape, a.dtype) for a in xs]
    return _comm_call(body, n, out_shapes, name, xs)


def _all_to_all(gs, name):
    n = len(gs)

    def body(*refs):
        g_refs, out_refs = refs[:n], refs[n:2 * n]
        send_sems, recv_sems, local_sems = refs[2 * n:]
        _, me = _peer(0)
        pending = []
        for i in range(n):
            local = pltpu.make_async_copy(g_refs[i].at[me], out_refs[i].at[0], local_sems.at[i])
            local.start()
            pending.append(local)
        sends = []
        for i in range(n):
            for k in range(1, N_DEV):
                dev, idx = _peer(k)
                cp = pltpu.make_async_remote_copy(
                    src_ref=g_refs[i].at[idx], dst_ref=out_refs[i].at[k], send_sem=send_sems.at[i, k - 1],
                    recv_sem=recv_sems.at[i, k - 1], device_id=dev, device_id_type=MESH_T)
                cp.start()
                sends.append(cp)
        for cp in sends:
            cp.wait_recv()
        for cp in sends:
            cp.wait_send()
        for local in pending:
            local.wait()

    out_shapes = [jax.ShapeDtypeStruct(g.shape, g.dtype) for g in gs]
    return _comm_call(body, n, out_shapes, name, gs)


_SPLIT_RELATIONS = {"gather": tuple(range(1, N_DEV)), "a2a": tuple(range(1, N_DEV)), "chip": (1, 2, 4, 6),
                    "forward": (2, 4, 6)}


def _split_copies(kind, src_refs, land_refs, send_sems, recv_sems, local_sems, with_recv):
    n = len(land_refs)
    rels = _SPLIT_RELATIONS[kind]
    _, me = _peer(0)
    locals_, remotes = [], []
    for i in range(n):
        if kind in ("gather", "chip"):
            locals_.append(pltpu.make_async_copy(src_refs[i], land_refs[i].at[me], local_sems.at[i]))
        elif kind == "a2a":
            locals_.append(pltpu.make_async_copy(src_refs[i].at[me], land_refs[i].at[0], local_sems.at[i]))
        for jj, k in enumerate(rels):
            dev, idx = _peer(k)
            if kind in ("gather", "chip"):
                src, dst, mine = src_refs[i], land_refs[i].at[me], land_refs[i].at[idx]
            elif kind == "a2a":
                src, dst, mine = src_refs[i].at[idx], land_refs[i].at[k], land_refs[i].at[k]
            else:
                dev, _ = _peer(1)
                _, came = _peer(k + 1)
                src, dst, mine = land_refs[i].at[idx], land_refs[i].at[idx], land_refs[i].at[came]
            j = i * len(rels) + jj
            send = pltpu.make_async_remote_copy(
                src_ref=src, dst_ref=dst, send_sem=send_sems.at[j], recv_sem=recv_sems.at[j],
                device_id=dev, device_id_type=MESH_T)
            recv = pltpu.make_async_remote_copy(
                src_ref=src, dst_ref=mine, send_sem=send_sems.at[j], recv_sem=recv_sems.at[j],
                device_id=dev, device_id_type=MESH_T) if with_recv else None
            remotes.append((send, recv))
    return locals_, remotes


_HBM = pl.BlockSpec(memory_space=pltpu.HBM)
_SEM = pl.BlockSpec(memory_space=pltpu.SEMAPHORE)
_ANY = pl.BlockSpec(memory_space=pl.ANY)


def _split_start(srcs, kind, name, after):
    n = len(srcs)
    if kind == "forward":
        arrays = list(srcs)
    else:
        gathers = kind in ("gather", "chip")
        arrays = list(srcs) + [lax.empty(((N_DEV,) + a.shape) if gathers else a.shape, a.dtype) for a in srcs]
    na = len(arrays)

    def body(*refs):
        src_refs, land_refs = refs[:n], refs[na - n:na]
        send_sems, recv_sems, local_sems = refs[na + 1:na + 4]
        token = refs[-1]
        locals_, remotes = _split_copies(kind, src_refs, land_refs, send_sems, recv_sems, local_sems, False)
        for cp in locals_:
            cp.start()
        for send, _ in remotes:
            send.start()
        token[...] = jnp.zeros_like(token)

    ncp = n * len(_SPLIT_RELATIONS[kind])
    sems = (pltpu.SemaphoreType.DMA((ncp,)), pltpu.SemaphoreType.DMA((ncp,)), pltpu.SemaphoreType.DMA((n,)))
    thru = tuple(pltpu.HBM(a.shape, a.dtype) for a in arrays)
    outs = pl.pallas_call(
        body, name=name,
        out_shape=sems + thru + (jax.ShapeDtypeStruct((8, LANES), F32),),
        in_specs=[_HBM] * na + [_ANY],
        out_specs=[_SEM] * 3 + [_HBM] * na + [pl.BlockSpec(memory_space=pltpu.VMEM)],
        input_output_aliases={i: 3 + i for i in range(na)},
        compiler_params=pltpu.CompilerParams(has_side_effects=pltpu.SideEffectType.DATAFLOW_SIDE_EFFECTING),
    )(*[pltpu.with_memory_space_constraint(a, pltpu.HBM) for a in arrays], after)
    return (kind, n, outs[:3], outs[3:3 + na]), outs[-1]


def _split_wait(handle, name, after):
    kind, n, sems, thru = handle
    na = len(thru)

    def body(*refs):
        src_refs, land_refs = refs[:n], refs[na - n:na]
        send_sems, recv_sems, local_sems = refs[na:na + 3]
        locals_, remotes = _split_copies(kind, src_refs, land_refs, send_sems, recv_sems, local_sems, True)
        for send, recv in remotes:
            send.wait_send()
            recv.wait_recv()
        for cp in locals_:
            cp.wait()

    outs = pl.pallas_call(
        body, name=name, out_shape=tuple(pltpu.HBM(a.shape, a.dtype) for a in thru),
        in_specs=[_HBM] * na + [_SEM] * 3 + [_ANY], out_specs=[_HBM] * na,
        input_output_aliases={i: i for i in range(na)},
        compiler_params=pltpu.CompilerParams(has_side_effects=pltpu.SideEffectType.DATAFLOW_SIDE_EFFECTING),
    )(*thru, *sems, after)
    return list(outs[na - n:])


def _adamw(gslabs, w, m, v, name):
    R, Cw = w.shape
    if R % 8 == 0:
        tr, tc = _tile(R, 64 if Cw > 1024 else 128, 8), Cw
    else:
        tr, tc = R, LANES
    c1 = 1.0 - ADAM_B1 ** ADAM_STEP
    c2 = 1.0 - ADAM_B2 ** ADAM_STEP

    def body(g_ref, w_ref, m_ref, v_ref, go_ref, d_ref, mo_ref, vo_ref):
        g = g_ref[0].astype(F32)
        for k in range(1, N_DEV):
            g = g + g_ref[k].astype(F32)
        mn = ADAM_B1 * m_ref[...] + (1.0 - ADAM_B1) * g
        vn = ADAM_B2 * v_ref[...] + (1.0 - ADAM_B2) * (g * g)
        m_hat = mn / c1
        v_hat = vn / c2
        go_ref[...] = g
        d_ref[...] = -ADAM_LR * (m_hat / (jnp.sqrt(v_hat) + ADAM_EPS) + ADAM_WD * w_ref[...])
        mo_ref[...] = mn
        vo_ref[...] = vn

    blk = pl.BlockSpec((tr, tc), lambda i, j: (i, j))
    return pl.pallas_call(
        body, grid=(R // tr, Cw // tc),
        in_specs=[pl.BlockSpec((N_DEV, tr, tc), lambda i, j: (0, i, j)), blk, blk, blk],
        out_specs=[blk] * 4, out_shape=[jax.ShapeDtypeStruct((R, Cw), F32)] * 4, name=name)(gslabs, w, m, v)


def _pack(arrs, row_mult, dtype=F32):
    parts = []
    total = 0
    for a in arrs:
        f = a.reshape(-1).astype(dtype)
        n = -(-f.shape[0] // 1024) * 1024
        parts.append(jnp.pad(f, (0, n - f.shape[0])))
        total += n
    rows = total // LANES
    rows_p = -(-rows // row_mult) * row_mult
    flat = jnp.concatenate(parts)
    flat = jnp.pad(flat, (0, rows_p * LANES - total))
    return flat.reshape(rows_p, LANES)


def _unpack(packed, shapes):
    lead = packed.shape[:-2]
    flat = packed.reshape(lead + (-1,))
    out = []
    off = 0
    for s in shapes:
        n = int(np.prod(s))
        out.append(flat[..., off:off + n].reshape(lead + tuple(s)))
        off += -(-n // 1024) * 1024
    return out


def _gather_cols(stacked):
    d, r, c = stacked.shape
    return stacked.transpose(1, 0, 2).reshape(r, d * c)


def _scatter_cols(full):
    r, n = full.shape
    return full.reshape(r, N_DEV, n // N_DEV).transpose(1, 0, 2)


def kernel(x, meta, norm1, w_in, gdn_conv_w, gdn_a_log, gdn_dt_bias, gdn_norm, w_out, norm2, w_ffn_up, ffn_conv_w, ffn_conv_b, w_ffn_down, norm_f, loss_target, m_meta, m_norm1, m_w_in, m_gdn_conv_w, m_gdn_a_log, m_gdn_dt_bias, m_gdn_norm, m_w_out, m_norm2, m_w_ffn_up, m_ffn_conv_w, m_ffn_conv_b, m_w_ffn_down, m_norm_f, v_meta, v_norm1, v_w_in, v_gdn_conv_w, v_gdn_a_log, v_gdn_dt_bias, v_gdn_norm, v_w_out, v_norm2, v_w_ffn_up, v_ffn_conv_w, v_ffn_conv_b, v_w_ffn_down, v_norm_f):
    S = x.shape[1]
    L = N_META + S
    pad = (-L) % CHUNK
    Lp = L + pad

    tr_ = lambda a: jnp.swapaxes(a[0], 0, 1)
    big = [tr_(w_in), w_out[0], tr_(w_ffn_up), w_ffn_down[0]]
    small = [meta, gdn_conv_w, ffn_conv_w]
    small_all, = _all_gather([_pack(small, 8)], "gather_small_weights")
    first, first_token = _split_start([big[0].astype(BF16)], "chip", "gather_w_in_start", small_all)
    late, late_token = _split_start([a.astype(BF16) for a in big[1:]], "gather", "gather_late_start", first_token)

    def first_weights(after):
        half = _split_wait(first, "gather_w_in_wait", after)
        second, second_token = _split_start(half, "forward", "gather_w_in_forward_start", after)
        w_in_s, = _split_wait(second, "gather_w_in_forward_wait", second_token)
        return {"w_in_t": w_in_s.reshape(_O_END, D_MODEL)}

    def late_weights(after):
        w_out_s, w_up_s, w_down_s = _split_wait(late, "gather_late_wait", after)
        return {"w_out": w_out_s.reshape(D_MODEL, D_MODEL), "w_up_t": w_up_s.reshape(2 * D_FF, D_MODEL),
                "w_down": w_down_s.reshape(D_FF, D_MODEL)}

    meta_s, gconv_s, fconv_s = _unpack(small_all, [a.shape for a in small])
    wt = {
        "norm1": norm1 + jnp.tile(late_token[0:1, :], (1, D_MODEL // LANES)),
        "gdn_conv_w": _gather_cols(gconv_s[:, 0]), "a_log": gdn_a_log[0], "dt_bias": gdn_dt_bias[0],
        "gdn_norm": gdn_norm, "norm2": norm2, "ffn_conv_w": _gather_cols(fconv_s[:, 0]), "ffn_conv_b": ffn_conv_b,
        "norm_f": norm_f.reshape(1, D_MODEL),
    }
    meta_f = _gather_cols(meta_s)

    pending = {}

    def on_ffn_out_grads(d_w_down, d_w_up_t, d_w_out):
        srcs = [d_w_out.reshape(N_DEV, D_MODEL // N_DEV, D_MODEL), d_w_up_t.reshape(N_DEV, 2 * D_FF // N_DEV, D_MODEL),
                d_w_down.reshape(N_DEV, D_FF // N_DEV, D_MODEL)]
        pending["ffn_out"], token = _split_start(srcs, "a2a", "exchange_ffn_out_start", d_w_out)
        return token

    def on_w_in_grads(d_w_in_t):
        slabs = d_w_in_t.astype(BF16).reshape(N_DEV, _O_END // N_DEV, D_MODEL)
        pending["w_in"], token = _split_start([slabs], "a2a", "exchange_w_in_start", d_w_in_t)
        return token

    head = jnp.concatenate([jnp.zeros((pad, D_MODEL), F32), meta_f], axis=0)
    if S >= 2 * 704:
        hpad = _Rows(x[0], pad + N_META, head)
        tgt = _Rows(loss_target[0], pad + N_META)
    else:
        hpad = jnp.concatenate([head, x[0]], axis=0)
        tgt = jnp.concatenate([jnp.zeros((pad + N_META, D_MODEL), F32), loss_target[0]], axis=0)
    lossvec, dh0, gr = _local_step(hpad, tgt, pad, wt, first_weights, late_weights, on_ffn_out_grads, on_w_in_grads)

    loss = lax.psum(jnp.sum(lossvec), ("x", "y", "c"))
    grad_x = dh0[pad + N_META:][None]

    big_m = [tr_(m_w_in), m_w_out[0], tr_(m_w_ffn_up), m_w_ffn_down[0]]
    big_v = [tr_(v_w_in), v_w_out[0], tr_(v_w_ffn_up), v_w_ffn_down[0]]
    slabs_ffn_out = _split_wait(pending["ffn_out"], "exchange_ffn_out_wait", dh0)
    big_out = [None] + [_adamw(slabs_ffn_out[i - 1], big[i], big_m[i], big_v[i], "adamw_big_%d" % i)
                        for i in range(1, len(big))]
    g_sm = [_scatter_cols(dh0[pad:pad + N_META]), _scatter_cols(gr["gdn_conv_w"]), _scatter_cols(gr["ffn_conv_w"])]
    g_small = jnp.stack([_pack([g[d] for g in g_sm], 8) for d in range(N_DEV)])
    slabs_small, = _all_to_all([g_small], "exchange_small_gradients")
    small_out = _adamw(slabs_small, _pack(small, 8), _pack([m_meta, m_gdn_conv_w, m_ffn_conv_w], 8),
                       _pack([v_meta, v_gdn_conv_w, v_ffn_conv_w], 8), "adamw_small_sharded")
    small_un = [_unpack(o, [a.shape for a in small]) for o in small_out]
    rep_w = [norm1, gdn_a_log, gdn_dt_bias, gdn_norm, norm2, ffn_conv_b, norm_f]
    rep_m = [m_norm1, m_gdn_a_log, m_gdn_dt_bias, m_gdn_norm, m_norm2, m_ffn_conv_b, m_norm_f]
    rep_v = [v_norm1, v_gdn_a_log, v_gdn_dt_bias, v_gdn_norm, v_norm2, v_ffn_conv_b, v_norm_f]
    rep_g = [gr["norm1"], gr["a_log"], gr["dt_bias"], gr["gdn_norm"], gr["norm2"], gr["ffn_conv_b"], gr["norm_f"]]
    rep_slabs, = _all_gather([_pack(rep_g, 8)], "gather_small_gradients")
    rep_out = _adamw(rep_slabs, _pack(rep_w, 8), _pack(rep_m, 8), _pack(rep_v, 8), "adamw_replicated")
    rep_shapes = [a.shape for a in rep_w]
    rp_g, rp_d, rp_nm, rp_nv = [_unpack(o, rep_shapes) for o in rep_out]

    slabs_w_in, = _split_wait(pending["w_in"], "exchange_w_in_wait", rep_out[0])
    big_out[0] = _adamw(slabs_w_in, big[0], big_m[0], big_v[0], "adamw_big_0")
    back = lambda a: jnp.swapaxes(a, 0, 1)[None]
    sh_g, sh_d, sh_nm, sh_nv = [
        [small_un[j][0], back(big_out[0][j]), small_un[j][1], big_out[1][j][None], back(big_out[2][j]),
         small_un[j][2], big_out[3][j][None]] for j in range(4)]

    def order(sh, rp):
        return [sh[0], rp[0], sh[1], sh[2], rp[1], rp[2], rp[3], sh[3], rp[4], sh[4], sh[5], rp[5], sh[6], rp[6]]

    return (loss, grad_x, *order(sh_g, rp_g), *order(sh_d, rp_d), *order(sh_nm, rp_nm), *order(sh_nv, rp_nv))
```

```python
import functools
import math

import numpy as np
import jax
import jax.numpy as jnp
from jax import lax
from jax.experimental import pallas as pl
from jax.experimental.pallas import tpu as pltpu

F32 = jnp.float32
BF16 = jnp.bfloat16
HI = lax.Precision.HIGHEST

D_MODEL = 1024
N_META = 16
CHUNK = 64
GDN_H = 8
GDN_D = 128
RET_H = 4
RET_D = 256
D_FF = 2816
GDN_CONV = 4
FFN_CONV = 3
ROPE_BASE = 10000.0
EPS = 1e-6
N_DEV = 8
LANES = 128
MAIN_W = 10 * 1024
_O_GQ, _O_GZ, _O_GA, _O_RQ, _O_RG, _O_GATE, _O_END = 0, 3072, 4096, 4112, 7184, 8208, 10256

ADAM_LR = 0.001
ADAM_B1 = 0.9
ADAM_B2 = 0.999
ADAM_EPS = 1e-08
ADAM_WD = 0.01
ADAM_STEP = 10

MESH_T = pl.DeviceIdType.MESH


def _tile(n, target, mult):
    best = None
    for d in range(mult, min(n, target) + 1, mult):
        if n % d == 0:
            best = d
    assert best is not None, (n, target, mult)
    return best


def _sig(x):
    return 1.0 / (1.0 + jnp.exp(-x))


def _d(a, b):
    return jnp.dot(a.astype(BF16), b.astype(BF16), preferred_element_type=F32)


def _dnt(a, b):
    return lax.dot_general(a.astype(BF16), b.astype(BF16), (((1,), (1,)), ((), ())), preferred_element_type=F32)


def _dtn(a, b):
    return lax.dot_general(a.astype(BF16), b.astype(BF16), (((0,), (0,)), ((), ())), preferred_element_type=F32)


def _dx(a, b):
    return jnp.dot(a, b, preferred_element_type=F32, precision=HI)


def _dxnt(a, b):
    return lax.dot_general(a, b, (((1,), (1,)), ((), ())), preferred_element_type=F32, precision=HI)


def _dxtn(a, b):
    return lax.dot_general(a, b, (((0,), (0,)), ((), ())), preferred_element_type=F32, precision=HI)


def _split(a):
    hi = a.astype(BF16)
    return hi, (a - hi.astype(F32)).astype(BF16)


def _d3g(a, b, dims):
    ah, al = _split(a)
    bh, bl = _split(b)
    f = functools.partial(lax.dot_general, dimension_numbers=dims, preferred_element_type=F32)
    if dims == _NN:
        rows = a.shape[0]
        both = f(jnp.concatenate([ah, al], axis=0), bh)
        return both[:rows] + (f(ah, bl) + both[rows:])
    return f(ah, bh) + (f(ah, bl) + f(al, bh))


_NN = (((1,), (0,)), ((), ()))
_NT = (((1,), (1,)), ((), ()))
_TN = (((0,), (0,)), ((), ()))


def _rowsum(x):
    return jnp.sum(x, axis=1, keepdims=True)


def _allsum(x):
    return jnp.sum(jnp.sum(x, axis=1, keepdims=True), axis=0, keepdims=True)


def _mm_nn(a, b, res=None, out_dtype=F32, bt=False, tm_target=704, b_rows=None, name="mm_nn"):
    M, K = a.shape
    N = b.shape[0] if bt else b.shape[1]
    tm = _tile(M, tm_target, 16)
    if b_rows is None:
        tn = _tile(N, 2816, 128)
    else:
        tn, n_tiles, start = b_rows
        N = tn * n_tiles

    def body(*refs):
        if res is None:
            a_ref, b_ref, o_ref = refs
        else:
            a_ref, b_ref, r_ref, o_ref = refs
        acc = lax.dot_general(a_ref[...], b_ref[...], _NT if bt else _NN, preferred_element_type=F32)
        if res is not None:
            acc = acc + r_ref[...]
        o_ref[...] = acc.astype(out_dtype)

    b_spec = pl.BlockSpec((tn, K), lambda j, i: (j, 0)) if bt else pl.BlockSpec((K, tn), lambda j, i: (0, j))
    if b_rows is not None:
        b_spec = pl.BlockSpec((pl.Element(tn), pl.Element(K)), lambda j, i: (pl.multiple_of(start(j), 16), 0))
    in_specs = [pl.BlockSpec((tm, K), lambda j, i: (i, 0)), b_spec]
    args = [a, b]
    if res is not None:
        in_specs.append(pl.BlockSpec((tm, tn), lambda j, i: (i, j)))
        args.append(res)
    return pl.pallas_call(
        body, grid=(N // tn, M // tm), in_specs=in_specs,
        out_specs=pl.BlockSpec((tm, tn), lambda j, i: (i, j)),
        out_shape=jax.ShapeDtypeStruct((M, N), out_dtype), name=name)(*args)


def _mm_sum(pairs, name):
    M = pairs[0][0].shape[0]
    N = pairs[0][1].shape[1]
    tm = _tile(M, 704, 16)
    n = len(pairs)

    def body(*refs):
        o_ref = refs[-1]
        acc = jnp.dot(refs[0][...], refs[1][...], preferred_element_type=F32)
        for i in range(1, n):
            acc = acc + jnp.dot(refs[2 * i][...], refs[2 * i + 1][...], preferred_element_type=F32)
        o_ref[...] = acc

    specs, args = [], []
    for a, b in pairs:
        specs += [pl.BlockSpec((tm, a.shape[1]), lambda i: (i, 0)),
                  pl.BlockSpec(b.shape, lambda i: (0, 0), pipeline_mode=pl.Buffered(1))]
        args += [a, b]
    return pl.pallas_call(
        body, grid=(M // tm,), in_specs=specs, out_specs=pl.BlockSpec((tm, N), lambda i: (i, 0)),
        out_shape=jax.ShapeDtypeStruct((M, N), F32), name=name)(*args)


def _mm_nt(a, b, res=None, name="mm_nt"):
    M, Nc = a.shape
    K = b.shape[0]
    tm = _tile(M, 704, 16)
    tc = _tile(Nc, 5632, 128)

    def body(*refs):
        if res is None:
            a_ref, b_ref, o_ref = refs
        else:
            a_ref, b_ref, r_ref, o_ref = refs
        c = pl.program_id(1)
        p = lax.dot_general(a_ref[...], b_ref[...], (((1,), (1,)), ((), ())), preferred_element_type=F32)

        @pl.when(c == 0)
        def _():
            if res is None:
                o_ref[...] = p
            else:
                o_ref[...] = p + r_ref[...]

        @pl.when(c > 0)
        def _():
            o_ref[...] += p

    in_specs = [pl.BlockSpec((tm, tc), lambda i, c: (i, c)), pl.BlockSpec((K, tc), lambda i, c: (0, c))]
    args = [a, b]
    if res is not None:
        in_specs.append(pl.BlockSpec((tm, K), lambda i, c: (i, 0)))
        args.append(res)
    return pl.pallas_call(
        body, grid=(M // tm, Nc // tc), in_specs=in_specs,
        out_specs=pl.BlockSpec((tm, K), lambda i, c: (i, 0)),
        out_shape=jax.ShapeDtypeStruct((M, K), F32), name=name)(*args)


def _mm_tn(a, b, out_dtype=F32, name="mm_tn"):
    M, K = a.shape
    N = b.shape[1]
    tm = _tile(M, 2752, 16)
    tk = _tile(K, 1408, 128)
    tn = _tile(N, 1408, 128)
    steps = M // tm

    def body(a_ref, b_ref, o_ref, *scratch):
        acc = scratch[0] if scratch else o_ref
        m = pl.program_id(2)
        p = lax.dot_general(a_ref[...], b_ref[...], (((0,), (0,)), ((), ())), preferred_element_type=F32)

        @pl.when(m == 0)
        def _():
            acc[...] = p

        @pl.when(m > 0)
        def _():
            acc[...] += p

        if scratch:
            @pl.when(m == steps - 1)
            def _():
                o_ref[...] = acc[...].astype(out_dtype)

    return pl.pallas_call(
        body, grid=(K // tk, N // tn, steps),
        in_specs=[pl.BlockSpec((tm, tk), lambda kk, j, m: (m, kk)), pl.BlockSpec((tm, tn), lambda kk, j, m: (m, j))],
        out_specs=pl.BlockSpec((tk, tn), lambda kk, j, m: (kk, j)),
        out_shape=jax.ShapeDtypeStruct((K, N), out_dtype),
        scratch_shapes=[] if out_dtype == F32 else [pltpu.VMEM((tk, tn), F32)], name=name)(a, b)


class _Rows:
    def __init__(self, body, first, head=None):
        self.body, self.first, self.head = body, first, head
        self.shape = (body.shape[0] + first, body.shape[1])


def _rows_operands(x, tr):
    if not isinstance(x, _Rows):
        return [x], [pl.BlockSpec((tr, x.shape[1]), lambda i: (i, 0))]
    assert x.first % 8 == 0 and x.first <= tr <= x.body.shape[0] and x.shape[0] % tr == 0
    width = x.shape[1]
    args = [x.body]
    specs = [pl.BlockSpec((pl.Element(tr), pl.Element(width)),
                          lambda i: (pl.multiple_of(jnp.maximum(i * tr - x.first, 0), 8), 0))]
    if x.head is not None:
        args.append(jnp.pad(x.head, ((0, tr - x.first), (0, 0))))
        specs.append(pl.BlockSpec((tr, width), lambda i: (0, 0)))
    return args, specs


def _rows_tile(x, refs, i, tr):
    blk = refs[0][...]
    if not isinstance(x, _Rows):
        return blk
    shifted = pltpu.roll(blk, x.first, 0)
    if x.head is not None:
        row = lax.broadcasted_iota(jnp.int32, (tr, 1), 0)
        shifted = jnp.where(row < x.first, refs[1][...], shifted)
    return jnp.where(i == 0, shifted, blk)


def _rms_fwd(x, g, name):
    Lp = x.shape[0]
    tr = _tile(Lp, 256, 16)
    args, specs = _rows_operands(x, tr)
    n = len(args)

    def body(*refs):
        g_ref, o_ref = refs[n:]
        xv = _rows_tile(x, refs[:n], pl.program_id(0), tr)
        r = lax.rsqrt(jnp.mean(xv * xv, axis=-1, keepdims=True) + EPS)
        o_ref[...] = (xv * r * g_ref[...]).astype(BF16)

    return pl.pallas_call(
        body, grid=(Lp // tr,),
        in_specs=specs + [pl.BlockSpec((1, D_MODEL), lambda i: (0, 0))],
        out_specs=pl.BlockSpec((tr, D_MODEL), lambda i: (i, 0)),
        out_shape=jax.ShapeDtypeStruct((Lp, D_MODEL), BF16), name=name)(*args, g)


class _Producer:
    def __init__(self, a, b, res=None):
        self.a, self.b, self.res = a, b, res
        self.tr = _tile(a.shape[0], 704, 16)
        K = a.shape[1]
        r_args, r_specs = ([], []) if res is None else _rows_operands(res, self.tr)
        self.args = [a, b] + r_args
        self.specs = [pl.BlockSpec((self.tr, K), lambda i: (i, 0)),
                      pl.BlockSpec((K, D_MODEL), lambda i: (0, 0), pipeline_mode=pl.Buffered(1))] + r_specs

    def tile(self, refs, i):
        acc = jnp.dot(refs[0][...], refs[1][...], preferred_element_type=F32)
        return acc if self.res is None else acc + _rows_tile(self.res, refs[2:], i, self.tr)


def _mm_rms_fwd(prod, g, name):
    Lp, tr, n = prod.a.shape[0], prod.tr, len(prod.args)

    def body(*refs):
        g_ref, x_ref, o_ref = refs[n:]
        xv = prod.tile(refs[:n], pl.program_id(0))
        r = lax.rsqrt(jnp.mean(xv * xv, axis=-1, keepdims=True) + EPS)
        x_ref[...] = xv
        o_ref[...] = (xv * r * g_ref[...]).astype(BF16)

    blk = pl.BlockSpec((tr, D_MODEL), lambda i: (i, 0))
    return pl.pallas_call(
        body, grid=(Lp // tr,), in_specs=prod.specs + [pl.BlockSpec((1, D_MODEL), lambda i: (0, 0))],
        out_specs=[blk, blk],
        out_shape=[jax.ShapeDtypeStruct((Lp, D_MODEL), F32), jax.ShapeDtypeStruct((Lp, D_MODEL), BF16)],
        name=name)(*prod.args, g)


def _rms_bwd(x, g, dy, dres, pad, name):
    Lp = x.shape[0]
    fused = isinstance(dy, _Producer)
    tr = dy.tr if fused else _tile(Lp, 256, 16)
    n = len(dy.args) if fused else 1
    x_args, x_specs = _rows_operands(x, tr)
    nx = len(x_args)

    def body(*refs):
        g_ref, dr_ref, dx_ref, dxb_ref, dg_ref = refs[n + nx:]
        i = pl.program_id(0)
        xv = _rows_tile(x, refs[n:n + nx], i, tr)
        r = lax.rsqrt(jnp.mean(xv * xv, axis=-1, keepdims=True) + EPS)
        xh = xv * r
        dyv = dy.tile(refs[:n], i) if fused else refs[0][...]
        dxh = dyv * g_ref[...]
        dx = r * (dxh - xh * jnp.mean(dxh * xh, axis=-1, keepdims=True)) + dr_ref[...]
        row = i * tr + lax.broadcasted_iota(jnp.int32, (tr, 1), 0)
        dx = jnp.where(row >= pad, dx, 0.0)
        dx_ref[...] = dx
        dxb_ref[...] = dx.astype(BF16)
        part = jnp.sum(dyv * xh, axis=0, keepdims=True)

        @pl.when(i == 0)
        def _():
            dg_ref[...] = part

        @pl.when(i > 0)
        def _():
            dg_ref[...] += part

    blk = pl.BlockSpec((tr, D_MODEL), lambda i: (i, 0))
    vec = pl.BlockSpec((1, D_MODEL), lambda i: (0, 0))
    return pl.pallas_call(
        body, grid=(Lp // tr,), in_specs=(dy.specs if fused else [blk]) + x_specs + [vec, blk],
        out_specs=[blk, blk, vec],
        out_shape=[jax.ShapeDtypeStruct((Lp, D_MODEL), F32), jax.ShapeDtypeStruct((Lp, D_MODEL), BF16),
                   jax.ShapeDtypeStruct((1, D_MODEL), F32)], name=name)(*(dy.args if fused else [dy]), *x_args, g, dres)


def _final(h2, g, tgt, first_row):
    fused = isinstance(h2, _Producer)
    Lp = h2.a.shape[0] if fused else h2.shape[0]
    tr = h2.tr if fused else _tile(Lp, 256, 16)
    n = len(h2.args) if fused else 1
    t_args, t_specs = _rows_operands(tgt, tr)
    nt = len(t_args)

    def body(*refs):
        g_ref = refs[n]
        loss_ref, dx_ref, dxb_ref, dg_ref = refs[n + 1 + nt:]
        i = pl.program_id(0)
        xv = h2.tile(refs[:n], i) if fused else refs[0][...]
        tv = _rows_tile(tgt, refs[n + 1:n + 1 + nt], i, tr)
        gv = g_ref[...]
        r = lax.rsqrt(jnp.mean(xv * xv, axis=-1, keepdims=True) + EPS)
        xh = xv * r
        row = i * tr + lax.broadcasted_iota(jnp.int32, (tr, 1), 0)
        err = jnp.where(row >= first_row, xh * gv - tv, 0.0)
        lpart = jnp.sum(err * err, axis=0, keepdims=True) * (0.5 / D_MODEL)
        dyv = err * (1.0 / D_MODEL)
        dxh = dyv * gv
        dx = r * (dxh - xh * jnp.mean(dxh * xh, axis=-1, keepdims=True))
        dx_ref[...] = dx
        dxb_ref[...] = dx.astype(BF16)
        part = jnp.sum(dyv * xh, axis=0, keepdims=True)

        @pl.when(i == 0)
        def _():
            dg_ref[...] = part
            loss_ref[...] = lpart

        @pl.when(i > 0)
        def _():
            dg_ref[...] += part
            loss_ref[...] += lpart

    blk = pl.BlockSpec((tr, D_MODEL), lambda i: (i, 0))
    vec = pl.BlockSpec((1, D_MODEL), lambda i: (0, 0))
    return pl.pallas_call(
        body, grid=(Lp // tr,), in_specs=(h2.specs if fused else [blk]) + [vec] + t_specs,
        out_specs=[vec, blk, blk, vec],
        out_shape=[jax.ShapeDtypeStruct((1, D_MODEL), F32), jax.ShapeDtypeStruct((Lp, D_MODEL), F32),
                   jax.ShapeDtypeStruct((Lp, D_MODEL), BF16), jax.ShapeDtypeStruct((1, D_MODEL), F32)],
        name="final_norm_loss")(*(h2.args if fused else [h2]), g, *t_args)


def _halo_prev(tr, width, col=0):
    return pl.BlockSpec((8, width), lambda i: (jnp.maximum(i * (tr // 8) - 1, 0), col))


def _halo_next(tr, width, nrows, col=0, rows=8):
    last = nrows // rows - 1
    return pl.BlockSpec((rows, width), lambda i: (jnp.minimum((i + 1) * (tr // rows), last), col))


def _shifted(x, offs):
    n = x.shape[0]
    return [x if off == 0 else pltpu.roll(x, n - off, 0) for off in offs]


def _taps(wins, w, rows, bias=None):
    acc = w[0:1, :] * wins[0][0:rows, :]
    if bias is not None:
        acc = acc + bias
    for kk in range(1, len(wins)):
        acc = acc + w[kk:kk + 1, :] * wins[kk][0:rows, :]
    return acc


def _gdn_pre(proj_m, proj_s, conv_w, gparams, pad):
    Lp = proj_m.shape[0]
    tr = _tile(Lp, 192, 64)
    W3 = 3 * D_MODEL

    def body(main_ref, prev_ref, s_ref, w_ref, gp_ref, qkv_ref, gsm_ref, c_ref):
        i = pl.program_id(0)
        prev = jnp.where(i > 0, prev_ref[...], 0.0)
        ext = jnp.concatenate([prev, main_ref[...]], axis=0)
        c = _taps(_shifted(ext, range(8 - (GDN_CONV - 1), 9)), w_ref[...], tr)
        c_ref[...] = c.astype(BF16)
        s = c * _sig(c)
        scale = GDN_D ** -0.5
        for j in range(2 * GDN_H):
            seg = s[:, j * GDN_D:(j + 1) * GDN_D]
            r = lax.rsqrt(_rowsum(seg * seg) + EPS)
            if j < GDN_H:
                r = r * scale
            qkv_ref[:, j * GDN_D:(j + 1) * GDN_D] = seg * r
        qkv_ref[:, 2 * D_MODEL:] = s[:, 2 * D_MODEL:]
        sm = s_ref[...]
        gp = gp_ref[...]
        lane = lax.broadcasted_iota(jnp.int32, sm.shape, 1)
        z = sm + gp[1:2, :]
        softplus = jnp.maximum(z, 0.0) + jnp.log(1.0 + jnp.exp(-jnp.abs(z)))
        lg = -jnp.exp(gp[0:1, :]) * softplus
        row = i * tr + lax.broadcasted_iota(jnp.int32, (tr, 1), 0)
        out = jnp.where(lane < GDN_H, lg, jnp.where(lane < 2 * GDN_H, _sig(sm), 0.0))
        gsm_ref[...] = jnp.where(row >= pad, out, 0.0)

    return pl.pallas_call(
        body, grid=(Lp // tr,),
        in_specs=[pl.BlockSpec((tr, W3), lambda i: (i, 0)), _halo_prev(tr, W3),
                  pl.BlockSpec((tr, LANES), lambda i: (i, 0)),
                  pl.BlockSpec((GDN_CONV, W3), lambda i: (0, 0)), pl.BlockSpec((8, LANES), lambda i: (0, 0))],
        out_specs=[pl.BlockSpec((tr, W3), lambda i: (i, 0)), pl.BlockSpec((tr, LANES), lambda i: (i, 0)),
                   pl.BlockSpec((tr, W3), lambda i: (i, 0))],
        out_shape=[jax.ShapeDtypeStruct((Lp, W3), F32), jax.ShapeDtypeStruct((Lp, LANES), F32),
                   jax.ShapeDtypeStruct((Lp, W3), BF16)],
        name="gdn_pre")(proj_m, proj_m, proj_s, conv_w, gparams)


def _gdn_pre_bwd(proj_m, conv_out, proj_s, conv_w, gparams, dq, dk, dv, dgs, pad):
    Lp = proj_m.shape[0]
    tr = _tile(Lp, 192, 64)
    W3 = 3 * D_MODEL
    te = tr + 8

    def body(main_ref, c_ref, cn_ref, s_ref, w_ref, gp_ref,
             dq_ref, dqn_ref, dk_ref, dkn_ref, dv_ref, dvn_ref, dgs_ref,
             da_ref, ds_ref, dw_ref, dgp_ref):
        i = pl.program_id(0)
        w = w_ref[...]
        c = jnp.concatenate([c_ref[...].astype(F32), cn_ref[...].astype(F32)[0:8]], axis=0)
        sg = _sig(c)
        s = c * sg
        rowe = i * tr + lax.broadcasted_iota(jnp.int32, (te, 1), 0)
        live = (rowe >= pad) & (rowe < Lp)
        dqe = jnp.concatenate([dq_ref[...], dqn_ref[...]], axis=0)
        dke = jnp.concatenate([dk_ref[...], dkn_ref[...]], axis=0)
        dve = jnp.concatenate([dv_ref[...], dvn_ref[...]], axis=0)
        scale = GDN_D ** -0.5
        parts = []
        for j in range(2 * GDN_H):
            seg = s[:, j * GDN_D:(j + 1) * GDN_D]
            r = lax.rsqrt(_rowsum(seg * seg) + EPS)
            xh = seg * r
            if j < GDN_H:
                dxh = dqe[:, j * GDN_D:(j + 1) * GDN_D] * scale
            else:
                dxh = dke[:, (j - GDN_H) * GDN_D:(j - GDN_H + 1) * GDN_D]
            parts.append(r * (dxh - xh * _rowsum(dxh * xh)))
        parts.append(dve)
        dsv = jnp.concatenate(parts, axis=1)
        dc = jnp.where(live, dsv * (sg * (1.0 + c * (1.0 - sg))), 0.0)
        dcs = _shifted(dc, range(GDN_CONV - 1, -1, -1))
        da_ref[...] = _taps(dcs, w, tr).astype(BF16)
        pm = main_ref[...]
        rows = [jnp.sum(dcs[kk][0:tr, :] * pm, axis=0, keepdims=True) for kk in range(GDN_CONV)]
        dwp = jnp.concatenate(rows + [jnp.zeros((8 - GDN_CONV, W3), F32)], axis=0)

        sm = s_ref[...]
        gp = gp_ref[...]
        lane = lax.broadcasted_iota(jnp.int32, sm.shape, 1)
        rowm = i * tr + lax.broadcasted_iota(jnp.int32, (tr, 1), 0)
        dgv = jnp.where(rowm >= pad, dgs_ref[...], 0.0)
        dlg = jnp.where(lane < GDN_H, dgv, 0.0)
        dbt = jnp.where((lane >= GDN_H) & (lane < 2 * GDN_H), dgv, 0.0)
        z = sm + gp[1:2, :]
        softplus = jnp.maximum(z, 0.0) + jnp.log(1.0 + jnp.exp(-jnp.abs(z)))
        ea = jnp.exp(gp[0:1, :])
        dz = dlg * (-ea) * _sig(z)
        dal = dlg * (-ea) * softplus
        bt = _sig(sm)
        dgb = dbt * bt * (1.0 - bt)
        ds_ref[...] = (dz + dgb).astype(BF16)
        gpp = jnp.concatenate([jnp.sum(dal, axis=0, keepdims=True), jnp.sum(dz, axis=0, keepdims=True),
                               jnp.zeros((6, LANES), F32)], axis=0)

        @pl.when(i == 0)
        def _():
            dw_ref[...] = dwp
            dgp_ref[...] = gpp

        @pl.when(i > 0)
        def _():
            dw_ref[...] += dwp
            dgp_ref[...] += gpp

    m3 = pl.BlockSpec((tr, W3), lambda i: (i, 0))
    m1 = pl.BlockSpec((tr, D_MODEL), lambda i: (i, 0))
    n1 = _halo_next(tr, D_MODEL, Lp)
    return pl.pallas_call(
        body, grid=(Lp // tr,),
        in_specs=[m3, m3, _halo_next(tr, W3, Lp, rows=16), pl.BlockSpec((tr, LANES), lambda i: (i, 0)),
                  pl.BlockSpec((GDN_CONV, W3), lambda i: (0, 0)), pl.BlockSpec((8, LANES), lambda i: (0, 0)),
                  m1, n1, m1, n1, m1, n1, pl.BlockSpec((tr, LANES), lambda i: (i, 0))],
        out_specs=[m3, pl.BlockSpec((tr, LANES), lambda i: (i, 0)),
                   pl.BlockSpec((8, W3), lambda i: (0, 0)), pl.BlockSpec((8, LANES), lambda i: (0, 0))],
        out_shape=[jax.ShapeDtypeStruct((Lp, W3), BF16), jax.ShapeDtypeStruct((Lp, LANES), BF16),
                   jax.ShapeDtypeStruct((8, W3), F32), jax.ShapeDtypeStruct((8, LANES), F32)],
        name="gdn_pre_bwd")(proj_m, conv_out, conv_out, proj_s, conv_w, gparams, dq, dq, dk, dk, dv, dv, dgs)


def _gdn_gates(gs):
    ri = lax.broadcasted_iota(jnp.int32, (CHUNK, CHUNK), 0)
    ci = lax.broadcasted_iota(jnp.int32, (CHUNK, CHUNK), 1)
    tril = ri >= ci
    strict = ri > ci
    gall = _dx(tril.astype(F32), gs)
    lane8 = lax.broadcasted_iota(jnp.int32, (8, LANES), 1)
    sub8 = lax.broadcasted_iota(jnp.int32, (8, LANES), 0)
    grow = _dxnt((lane8 == sub8).astype(F32), gall)
    return gall, grow, tril, strict


def _gdn_decay(gall, grow, tril, h):
    g = gall[:, h:h + 1]
    return g, jnp.where(tril, jnp.exp(jnp.where(tril, g - grow[h:h + 1, :], 0.0)), 0.0)


def _group(N):
    return 3 if N % 3 == 0 else (2 if N % 2 == 0 else 1)


def _gdn_chunk_specs(N, rev):
    G = _group(N)
    nb = N // G
    cn = (lambda n: nb - 1 - n) if rev else (lambda n: n)
    col = lambda j: pl.BlockSpec((G * CHUNK, D_MODEL), lambda n: (cn(n), j))
    gate = pl.BlockSpec((G * CHUNK, LANES), lambda n: (cn(n), 0))
    st = lambda a, b: pl.BlockSpec((GDN_H, G, a, b), lambda n: (0, cn(n), 0, 0))
    return G, nb, col, gate, st


def _gdn_chunk_fwd(qkv, gsm):
    Lp = qkv.shape[0]
    N = Lp // CHUNK
    G, nb, col, gate, st = _gdn_chunk_specs(N, False)

    def body(q_ref, k_ref, v_ref, gs_ref, o_ref, sin_ref, t_ref, S):
        n = pl.program_id(0)

        @pl.when(n == 0)
        def _():
            S[...] = jnp.zeros_like(S)

        ri = lax.broadcasted_iota(jnp.int32, (CHUNK, CHUNK), 0)
        ci = lax.broadcasted_iota(jnp.int32, (CHUNK, CHUNK), 1)
        eye = (ri == ci).astype(F32)
        heads = range(GDN_H)
        sls = [slice(h * GDN_D, (h + 1) * GDN_D) for h in heads]
        rows = [slice(c * CHUNK, (c + 1) * CHUNK) for c in range(G)]
        pairs = [(c, h) for c in range(G) for h in heads]
        P = lambda f: {p: f(*p) for p in pairs}
        gs = [gs_ref[rows[c], :] for c in range(G)]
        gates = [_gdn_gates(gs[c]) for c in range(G)]
        tril, strict = gates[0][2], gates[0][3]
        q = P(lambda c, h: q_ref[rows[c], sls[h]])
        k = P(lambda c, h: k_ref[rows[c], sls[h]])
        v = P(lambda c, h: v_ref[rows[c], sls[h]])
        beta = P(lambda c, h: gs[c][:, GDN_H + h:GDN_H + h + 1])
        gg = P(lambda c, h: _gdn_decay(gates[c][0], gates[c][1], tril, h))
        g = {p: x[0] for p, x in gg.items()}
        gam = {p: x[1] for p, x in gg.items()}
        eg = P(lambda c, h: jnp.exp(g[c, h]))
        gl = P(lambda c, h: g[c, h][CHUNK - 1:CHUNK, :])
        kb = P(lambda c, h: k[c, h] * beta[c, h])
        pw = P(lambda c, h: -jnp.where(strict, _dnt(kb[c, h], k[c, h]) * gam[c, h], 0.0))
        p = P(lambda c, h: _dnt(q[c, h], k[c, h]) * gam[c, h])
        t = P(lambda c, h: eye + pw[c, h])
        for it in range(5):
            mm = _d3g if it < 2 else (lambda a, b, dims: _d(a, b))
            pw = P(lambda c, h: mm(pw[c, h], pw[c, h], _NN))
            t = P(lambda c, h: t[c, h] + mm(t[c, h], pw[c, h], _NN))
        u = P(lambda c, h: _d(t[c, h], v[c, h] * beta[c, h]))
        w = P(lambda c, h: _d(t[c, h], kb[c, h] * eg[c, h]))
        qg = P(lambda c, h: q[c, h] * eg[c, h])
        kd = P(lambda c, h: k[c, h] * jnp.exp(gl[c, h] - g[c, h]))
        egl = P(lambda c, h: jnp.exp(gl[c, h]))
        for c in range(G):
            for h in heads:
                t_ref[h, c] = t[c, h]
        cur = [S[h] for h in heads]
        for c in range(G):
            vnew = [u[c, h] - _d(w[c, h], cur[h]) for h in heads]
            for h in heads:
                o_ref[rows[c], sls[h]] = _d(qg[c, h], cur[h]) + _d(p[c, h], vnew[h])
                sin_ref[h, c] = cur[h]
            cur = [cur[h] * egl[c, h] + _dtn(kd[c, h], vnew[h]) for h in heads]
        for h in heads:
            S[h] = cur[h]

    return pl.pallas_call(
        body, grid=(nb,),
        in_specs=[col(0), col(1), col(2), gate],
        out_specs=[col(0), st(GDN_D, GDN_D), st(CHUNK, CHUNK)],
        out_shape=[jax.ShapeDtypeStruct((Lp, D_MODEL), F32), jax.ShapeDtypeStruct((GDN_H, N, GDN_D, GDN_D), F32),
                   jax.ShapeDtypeStruct((GDN_H, N, CHUNK, CHUNK), F32)],
        scratch_shapes=[pltpu.VMEM((GDN_H, GDN_D, GDN_D), F32)],
        name="gdn_chunk_fwd")(qkv, qkv, qkv, gsm)


def _gdn_chunk_bwd(qkv, gsm, do, s_in, t_in):
    Lp = qkv.shape[0]
    N = Lp // CHUNK
    G, nb, col, gate, st = _gdn_chunk_specs(N, True)

    def body(q_ref, k_ref, v_ref, gs_ref, do_ref, sin_ref, t_ref, dq_ref, dk_ref, dv_ref, dgs_ref, dS):
        n = pl.program_id(0)

        @pl.when(n == 0)
        def _():
            dS[...] = jnp.zeros_like(dS)

        lane = lax.broadcasted_iota(jnp.int32, (CHUNK, LANES), 1)
        rcol = lax.broadcasted_iota(jnp.int32, (CHUNK, 1), 0)
        ri = lax.broadcasted_iota(jnp.int32, (CHUNK, CHUNK), 0)
        ci = lax.broadcasted_iota(jnp.int32, (CHUNK, CHUNK), 1)
        ones = jnp.ones((CHUNK, LANES), F32)
        heads = range(GDN_H)
        sls = [slice(h * GDN_D, (h + 1) * GDN_D) for h in heads]
        rows = [slice(c * CHUNK, (c + 1) * CHUNK) for c in range(G)]
        pairs = [(c, h) for c in range(G) for h in heads]
        P = lambda f: {p: f(*p) for p in pairs}
        gs = [gs_ref[rows[c], :] for c in range(G)]
        gates = [_gdn_gates(gs[c]) for c in range(G)]
        tril, strict = gates[0][2], gates[0][3]
        q = P(lambda c, h: q_ref[rows[c], sls[h]])
        k = P(lambda c, h: k_ref[rows[c], sls[h]])
        v = P(lambda c, h: v_ref[rows[c], sls[h]])
        dov = P(lambda c, h: do_ref[rows[c], sls[h]])
        s0 = P(lambda c, h: sin_ref[h, c])
        t = P(lambda c, h: t_ref[h, c])
        beta = P(lambda c, h: gs[c][:, GDN_H + h:GDN_H + h + 1])
        gg = P(lambda c, h: _gdn_decay(gates[c][0], gates[c][1], tril, h))
        g = {p: x[0] for p, x in gg.items()}
        gam = {p: x[1] for p, x in gg.items()}
        eg = P(lambda c, h: jnp.exp(g[c, h]))
        egl = P(lambda c, h: jnp.exp(g[c, h][CHUNK - 1:CHUNK, :]))
        e = P(lambda c, h: jnp.exp(g[c, h][CHUNK - 1:CHUNK, :] - g[c, h]))
        kb = P(lambda c, h: k[c, h] * beta[c, h])
        kbg = P(lambda c, h: kb[c, h] * eg[c, h])
        vb = P(lambda c, h: v[c, h] * beta[c, h])
        qg = P(lambda c, h: q[c, h] * eg[c, h])
        kd = P(lambda c, h: k[c, h] * e[c, h])
        m = P(lambda c, h: jnp.where(strict, _dnt(kb[c, h], k[c, h]) * gam[c, h], 0.0))
        u = P(lambda c, h: _d(t[c, h], vb[c, h]))
        w = P(lambda c, h: _d(t[c, h], kbg[c, h]))
        p = P(lambda c, h: _dnt(q[c, h], k[c, h]) * gam[c, h])
        dqg = P(lambda c, h: _dnt(dov[c, h], s0[c, h]))
        qgdo = P(lambda c, h: _dtn(qg[c, h], dov[c, h]))
        ptdo = P(lambda c, h: _dtn(p[c, h], dov[c, h]))
        vnew = P(lambda c, h: u[c, h] - _d(w[c, h], s0[c, h]))
        dp = P(lambda c, h: jnp.where(tril, _dnt(dov[c, h], vnew[c, h]), 0.0))
        cur = [dS[h] for h in heads]
        dvnew, dkd, sds = {}, {}, {}
        for c in reversed(range(G)):
            for h in heads:
                dvnew[c, h] = ptdo[c, h] + _d(kd[c, h], cur[h])
                dkd[c, h] = _dnt(vnew[c, h], cur[h])
                sds[c, h] = _allsum(s0[c, h] * cur[h])
            cur = [qgdo[c, h] + egl[c, h] * cur[h] - _dtn(w[c, h], dvnew[c, h]) for h in heads]
        for h in heads:
            dS[h] = cur[h]
        dw = P(lambda c, h: -_dnt(dvnew[c, h], s0[c, h]))
        dvb = P(lambda c, h: _dtn(t[c, h], dvnew[c, h]))
        dkbg = P(lambda c, h: _dtn(t[c, h], dw[c, h]))
        dt = P(lambda c, h: _dnt(dvnew[c, h], vb[c, h]) + _dnt(dw[c, h], kbg[c, h]))
        x1 = P(lambda c, h: _d3g(t[c, h], dt[c, h], _TN))
        dm = P(lambda c, h: jnp.where(strict, -_d3g(x1[c, h], t[c, h], _NT), 0.0))
        dkk = P(lambda c, h: dm[c, h] * gam[c, h])
        dqk = P(lambda c, h: dp[c, h] * gam[c, h])
        dkb = P(lambda c, h: _d(dkk[c, h], k[c, h]) + eg[c, h] * dkbg[c, h])
        em = P(lambda c, h: dm[c, h] * m[c, h] + dp[c, h] * p[c, h])
        colsum = P(lambda c, h: _d3g(em[c, h], ones, _TN)[:, 0:1])
        for c, h in pairs:
            dk_ref[rows[c], sls[h]] = (_dtn(dkk[c, h], kb[c, h]) + _dtn(dqk[c, h], q[c, h]) + dkd[c, h] * e[c, h]
                                       + beta[c, h] * dkb[c, h])
            dq_ref[rows[c], sls[h]] = _d(dqk[c, h], k[c, h]) + dqg[c, h] * eg[c, h]
            dv_ref[rows[c], sls[h]] = beta[c, h] * dvb[c, h]
        for c in range(G):
            dg_all = jnp.zeros((CHUNK, LANES), F32)
            dbeta_all = jnp.zeros((CHUNK, LANES), F32)
            for h in heads:
                dbeta = _rowsum(k[c, h] * dkb[c, h]) + _rowsum(v[c, h] * dvb[c, h])
                z = _rowsum(kd[c, h] * dkd[c, h])
                dg = (_rowsum(em[c, h]) - colsum[c, h] + _rowsum(qg[c, h] * dqg[c, h]) + _rowsum(kbg[c, h] * dkbg[c, h])
                      - z)
                extra = _allsum(z) + egl[c, h] * sds[c, h]
                dg = dg + jnp.where(rcol == CHUNK - 1, extra, 0.0)
                dg_all = dg_all + jnp.where(lane == h, dg, 0.0)
                dbeta_all = dbeta_all + jnp.where(lane == GDN_H + h, dbeta, 0.0)
            dgs_ref[rows[c], :] = _dx((ci >= ri).astype(F32), dg_all) + dbeta_all

    return pl.pallas_call(
        body, grid=(nb,),
        in_specs=[col(0), col(1), col(2), gate, col(0), st(GDN_D, GDN_D), st(CHUNK, CHUNK)],
        out_specs=[col(0), col(0), col(0), gate],
        out_shape=[jax.ShapeDtypeStruct((Lp, D_MODEL), F32)] * 3 + [jax.ShapeDtypeStruct((Lp, LANES), F32)],
        scratch_shapes=[pltpu.VMEM((GDN_H, GDN_D, GDN_D), F32)],
        name="gdn_chunk_bwd")(qkv, qkv, qkv, gsm, do, s_in, t_in)


def _rot(x, c, s):
    half = RET_D // 2
    x1 = x[:, :half]
    x2 = x[:, half:]
    return jnp.concatenate([x1 * c - x2 * s, x2 * c + x1 * s], axis=1)


def _rot_bwd(d, c, s):
    half = RET_D // 2
    d1 = d[:, :half]
    d2 = d[:, half:]
    return jnp.concatenate([d1 * c + d2 * s, d2 * c - d1 * s], axis=1)


def _ret_tables():
    hh = jnp.arange(RET_H, dtype=F32)
    lg = jnp.log(1.0 - 2.0 ** (-5.0 - hh))
    idx = jnp.arange(CHUNK, dtype=F32)
    tril = jnp.asarray(np.tril(np.ones((CHUNK, CHUNK), dtype=bool)))
    dmask = jnp.where(tril, jnp.exp((idx[:, None] - idx[None, :]) * lg[:, None, None]), 0.0)
    qdec = jnp.exp((idx[None, :] + 1.0) * lg[:, None])
    kdec = jnp.exp((CHUNK - 1.0 - idx[None, :]) * lg[:, None])
    gch = jnp.exp(CHUNK * lg)
    qdec = jnp.broadcast_to(qdec[:, :, None], (RET_H, CHUNK, RET_D))
    kdec = jnp.broadcast_to(kdec[:, :, None], (RET_H, CHUNK, RET_D))
    gch = jnp.broadcast_to(gch[:, None, None], (RET_H, 8, LANES))
    return dmask, qdec, kdec, gch


def _ret_specs(N, rev):
    G = _group(N)
    nb = N // G
    cn = (lambda n: nb - 1 - n) if rev else (lambda n: n)
    col = lambda j: pl.BlockSpec((G * CHUNK, D_MODEL), lambda n: (cn(n), j))
    tab = lambda a, b: pl.BlockSpec((RET_H, a, b), lambda n: (0, 0, 0))
    rope = pl.BlockSpec((G * CHUNK, LANES), lambda n: (cn(n), 0))
    st = pl.BlockSpec((RET_H, G, RET_D, RET_D), lambda n: (0, cn(n), 0, 0))
    return G, nb, col, tab, rope, st


def _ret_chunk_fwd(proj_m, cos, sin, tables):
    Lp = proj_m.shape[0]
    N = Lp // CHUNK
    dmask, qdec, kdec, gch = tables
    G, nb, col, tab, rope, st = _ret_specs(N, False)

    def body(q_ref, k_ref, v_ref, c_ref, s_ref, dm_ref, qd_ref, kd_ref, g_ref, o_ref, sin_ref, S):
        n = pl.program_id(0)

        @pl.when(n == 0)
        def _():
            S[...] = jnp.zeros_like(S)

        heads = range(RET_H)
        sls = [slice(h * RET_D, (h + 1) * RET_D) for h in heads]
        rows = [slice(c * CHUNK, (c + 1) * CHUNK) for c in range(G)]
        pairs = [(c, h) for c in range(G) for h in heads]
        P = lambda f: {p: f(*p) for p in pairs}
        qr = P(lambda c, h: _rot(q_ref[rows[c], sls[h]], c_ref[rows[c], :], s_ref[rows[c], :]))
        ks = P(lambda c, h: _rot(k_ref[rows[c], sls[h]], c_ref[rows[c], :], s_ref[rows[c], :]) * (RET_D ** -0.5))
        v = P(lambda c, h: v_ref[rows[c], sls[h]])
        a = P(lambda c, h: _dnt(qr[c, h], ks[c, h]) * dm_ref[h])
        av = P(lambda c, h: _d(a[c, h], v[c, h]))
        kv = P(lambda c, h: _dtn(ks[c, h] * kd_ref[h], v[c, h]))
        qd = P(lambda c, h: qr[c, h] * qd_ref[h])
        cur = [S[h] for h in heads]
        for c in range(G):
            for h in heads:
                o_ref[rows[c], sls[h]] = av[c, h] + _d(qd[c, h], cur[h])
                sin_ref[h, c] = cur[h].astype(BF16)
            cur = [cur[h] * g_ref[h, 0:1, 0:1] + kv[c, h] for h in heads]
        for h in heads:
            S[h] = cur[h]

    return pl.pallas_call(
        body, grid=(nb,),
        in_specs=[col(3), col(4), col(5), rope, rope,
                  tab(CHUNK, CHUNK), tab(CHUNK, RET_D), tab(CHUNK, RET_D), tab(8, LANES)],
        out_specs=[col(0), st],
        out_shape=[jax.ShapeDtypeStruct((Lp, D_MODEL), F32), jax.ShapeDtypeStruct((RET_H, N, RET_D, RET_D), BF16)],
        scratch_shapes=[pltpu.VMEM((RET_H, RET_D, RET_D), F32)],
        name="ret_chunk_fwd")(proj_m, proj_m, proj_m, cos, sin, dmask, qdec, kdec, gch)


def _ret_chunk_bwd(proj_m, cos, sin, tables, do, s_in):
    Lp = proj_m.shape[0]
    N = Lp // CHUNK
    dmask, qdec, kdec, gch = tables
    G, nb, col, tab, rope, st = _ret_specs(N, True)

    def body(q_ref, k_ref, v_ref, c_ref, s_ref, dm_ref, qd_ref, kd_ref, g_ref, do_ref, sin_ref,
             d_ref, dS):
        n = pl.program_id(0)

        @pl.when(n == 0)
        def _():
            dS[...] = jnp.zeros_like(dS)

        kscale = RET_D ** -0.5
        heads = range(RET_H)
        sls = [slice(h * RET_D, (h + 1) * RET_D) for h in heads]
        rows = [slice(c * CHUNK, (c + 1) * CHUNK) for c in range(G)]
        pairs = [(c, h) for c in range(G) for h in heads]
        P = lambda f: {p: f(*p) for p in pairs}
        cs = [(c_ref[rows[c], :], s_ref[rows[c], :]) for c in range(G)]
        osl = lambda part, h: slice(part * D_MODEL + h * RET_D, part * D_MODEL + (h + 1) * RET_D)
        qr = P(lambda c, h: _rot(q_ref[rows[c], sls[h]], *cs[c]))
        ks = P(lambda c, h: _rot(k_ref[rows[c], sls[h]], *cs[c]) * kscale)
        v = P(lambda c, h: v_ref[rows[c], sls[h]])
        dov = P(lambda c, h: do_ref[rows[c], sls[h]])
        ad = P(lambda c, h: _dnt(qr[c, h], ks[c, h]) * dm_ref[h])
        da = P(lambda c, h: _dnt(dov[c, h], v[c, h]) * dm_ref[h])
        dos = P(lambda c, h: _dnt(dov[c, h], sin_ref[h, c]) * qd_ref[h])
        qdo = P(lambda c, h: _dtn(qr[c, h] * qd_ref[h], dov[c, h]))
        adv = P(lambda c, h: _dtn(ad[c, h], dov[c, h]))
        dqr = P(lambda c, h: _d(da[c, h], ks[c, h]) + dos[c, h])
        daq = P(lambda c, h: _dtn(da[c, h], qr[c, h]))
        kk = P(lambda c, h: ks[c, h] * kd_ref[h])
        cur = [dS[h] for h in heads]
        for c in reversed(range(G)):
            for h in heads:
                d_ref[rows[c], osl(2, h)] = (adv[c, h] + _d(kk[c, h], cur[h])).astype(BF16)
                d_ref[rows[c], osl(0, h)] = _rot_bwd(dqr[c, h], *cs[c]).astype(BF16)
                dks = daq[c, h] + _dnt(v[c, h], cur[h]) * kd_ref[h]
                d_ref[rows[c], osl(1, h)] = _rot_bwd(dks * kscale, *cs[c]).astype(BF16)
            cur = [cur[h] * g_ref[h, 0:1, 0:1] + qdo[c, h] for h in heads]
        for h in heads:
            dS[h] = cur[h]

    return pl.pallas_call(
        body, grid=(nb,),
        in_specs=[col(3), col(4), col(5), rope, rope,
                  tab(CHUNK, CHUNK), tab(CHUNK, RET_D), tab(CHUNK, RET_D), tab(8, LANES), col(0), st],
        out_specs=pl.BlockSpec((G * CHUNK, 3 * D_MODEL), lambda n: (nb - 1 - n, 0)),
        out_shape=jax.ShapeDtypeStruct((Lp, 3 * D_MODEL), BF16),
        scratch_shapes=[pltpu.VMEM((RET_H, RET_D, RET_D), F32)],
        name="ret_chunk_bwd")(proj_m, proj_m, proj_m, cos, sin, dmask, qdec, kdec, gch, do, s_in)


def _merge_specs(tr):
    col = lambda j: pl.BlockSpec((tr, D_MODEL), lambda i: (i, j))
    return col


def _merge_fwd(o_a, o_b, proj_m, gnorm):
    Lp = o_a.shape[0]
    tr = _tile(Lp, 192, 16)

    def body(oa_ref, ob_ref, gz_ref, rg_ref, ga_ref, gb_ref, gn_ref, y_ref):
        gn = gn_ref[...]
        oa = oa_ref[...]
        ob = ob_ref[...]
        gz = gz_ref[...]
        ya = []
        for j in range(GDN_H):
            seg = oa[:, j * GDN_D:(j + 1) * GDN_D]
            r = lax.rsqrt(jnp.mean(seg * seg, axis=-1, keepdims=True) + EPS)
            ya.append(seg * r * gn)
        ya = jnp.concatenate(ya, axis=1) * (gz * _sig(gz))
        yb = []
        for j in range(RET_H):
            seg = ob[:, j * RET_D:(j + 1) * RET_D]
            r = lax.rsqrt(jnp.mean(seg * seg, axis=-1, keepdims=True) + EPS)
            yb.append(seg * r)
        rg = rg_ref[...]
        yb = jnp.concatenate(yb, axis=1) * (rg * _sig(rg))
        y_ref[...] = (_sig(ga_ref[...]) * ya + _sig(gb_ref[...]) * yb).astype(BF16)

    col = _merge_specs(tr)
    return pl.pallas_call(
        body, grid=(Lp // tr,),
        in_specs=[col(0), col(0), col(6), col(7), col(8), col(9), pl.BlockSpec((1, GDN_D), lambda i: (0, 0))],
        out_specs=col(0), out_shape=jax.ShapeDtypeStruct((Lp, D_MODEL), BF16),
        name="merge_fwd")(o_a, o_b, proj_m, proj_m, proj_m, proj_m, gnorm)


def _merge_bwd(dh1b, w_out, o_a, o_b, proj_m, gnorm):
    Lp = o_a.shape[0]
    tr = _tile(Lp, 192, 16)

    def body(d_ref, wo_ref, oa_ref, ob_ref, gz_ref, rg_ref, ga_ref, gb_ref, gn_ref, dc_ref, doa_ref, dob_ref, dgn_ref):
        i = pl.program_id(0)
        gn = gn_ref[...]
        dyv = lax.dot_general(d_ref[...], wo_ref[...], _NT, preferred_element_type=F32)
        oa = oa_ref[...]
        ob = ob_ref[...]
        gz = gz_ref[...]
        rg = rg_ref[...]
        sa = _sig(ga_ref[...])
        sb = _sig(gb_ref[...])
        dya = dyv * sa
        dyb = dyv * sb
        sgz = _sig(gz)
        szz = gz * sgz
        dgn = jnp.zeros((1, GDN_D), F32)
        ya = []
        dgz = []
        for j in range(GDN_H):
            sl = slice(j * GDN_D, (j + 1) * GDN_D)
            seg = oa[:, sl]
            r = lax.rsqrt(jnp.mean(seg * seg, axis=-1, keepdims=True) + EPS)
            xh = seg * r
            oan = xh * gn
            ya.append(oan * szz[:, sl])
            dgz.append(dya[:, sl] * oan * (sgz[:, sl] * (1.0 + gz[:, sl] * (1.0 - sgz[:, sl]))))
            doan = dya[:, sl] * szz[:, sl]
            dgn = dgn + jnp.sum(doan * xh, axis=0, keepdims=True)
            dxh = doan * gn
            doa_ref[:, sl] = r * (dxh - xh * jnp.mean(dxh * xh, axis=-1, keepdims=True))
        ya = jnp.concatenate(ya, axis=1)
        srg = _sig(rg)
        srr = rg * srg
        yb = []
        drg = []
        for j in range(RET_H):
            sl = slice(j * RET_D, (j + 1) * RET_D)
            seg = ob[:, sl]
            r = lax.rsqrt(jnp.mean(seg * seg, axis=-1, keepdims=True) + EPS)
            xh = seg * r
            yb.append(xh * srr[:, sl])
            drg.append(dyb[:, sl] * xh * (srg[:, sl] * (1.0 + rg[:, sl] * (1.0 - srg[:, sl]))))
            dxh = dyb[:, sl] * srr[:, sl]
            dob_ref[:, sl] = r * (dxh - xh * jnp.mean(dxh * xh, axis=-1, keepdims=True))
        yb = jnp.concatenate(yb, axis=1)
        dc_ref[:, 0:D_MODEL] = jnp.concatenate(dgz, axis=1).astype(BF16)
        dc_ref[:, D_MODEL:2 * D_MODEL] = jnp.concatenate(drg, axis=1).astype(BF16)
        dc_ref[:, 2 * D_MODEL:3 * D_MODEL] = (dyv * ya * sa * (1.0 - sa)).astype(BF16)
        dc_ref[:, 3 * D_MODEL:] = (dyv * yb * sb * (1.0 - sb)).astype(BF16)

        @pl.when(i == 0)
        def _():
            dgn_ref[...] = dgn

        @pl.when(i > 0)
        def _():
            dgn_ref[...] += dgn

    col = _merge_specs(tr)
    return pl.pallas_call(
        body, grid=(Lp // tr,),
        in_specs=[col(0), pl.BlockSpec((D_MODEL, D_MODEL), lambda i: (0, 0), pipeline_mode=pl.Buffered(1)),
                  col(0), col(0), col(6), col(7), col(8), col(9), pl.BlockSpec((1, GDN_D), lambda i: (0, 0))],
        out_specs=[pl.BlockSpec((tr, 4 * D_MODEL), lambda i: (i, 0)), col(0), col(0),
                   pl.BlockSpec((1, GDN_D), lambda i: (0, 0))],
        out_shape=[jax.ShapeDtypeStruct((Lp, 4 * D_MODEL), BF16), jax.ShapeDtypeStruct((Lp, D_MODEL), F32),
                   jax.ShapeDtypeStruct((Lp, D_MODEL), F32), jax.ShapeDtypeStruct((1, GDN_D), F32)],
        name="merge_bwd")(dh1b, w_out, o_a, o_b, proj_m, proj_m, proj_m, proj_m, gnorm)


def _ffn_act(up, conv_w, conv_b):
    Lp = up.shape[0]
    tr = _tile(Lp, 192, 16)
    W2 = 2 * D_FF

    def body(main_ref, prev_ref, w_ref, b_ref, act_ref, u_ref):
        i = pl.program_id(0)
        prev = jnp.where(i > 0, prev_ref[...], 0.0)
        ext = jnp.concatenate([prev, main_ref[...]], axis=0)
        u = _taps(_shifted(ext, range(8 - (FFN_CONV - 1), 9)), w_ref[...], tr, b_ref[...])
        a = u[:, :D_FF]
        act_ref[...] = (a * _sig(a) * u[:, D_FF:]).astype(BF16)
        u_ref[...] = u.astype(BF16)

    return pl.pallas_call(
        body, grid=(Lp // tr,),
        in_specs=[pl.BlockSpec((tr, W2), lambda i: (i, 0)), _halo_prev(tr, W2),
                  pl.BlockSpec((FFN_CONV, W2), lambda i: (0, 0)), pl.BlockSpec((1, W2), lambda i: (0, 0))],
        out_specs=[pl.BlockSpec((tr, D_FF), lambda i: (i, 0)), pl.BlockSpec((tr, W2), lambda i: (i, 0))],
        out_shape=[jax.ShapeDtypeStruct((Lp, D_FF), BF16), jax.ShapeDtypeStruct((Lp, W2), BF16)],
        name="ffn_act")(up, up, conv_w, conv_b)


def _ffn_act_bwd(up, u, dact, conv_w):
    Lp = up.shape[0]
    tr = _tile(Lp, 192, 16)
    W2 = 2 * D_FF
    te = tr + 8

    def body(up_ref, u_ref, un_ref, da_ref, dan_ref, w_ref, dup_ref, acc_ref):
        i = pl.program_id(0)
        w = w_ref[...]
        ue = jnp.concatenate([u_ref[...].astype(F32), un_ref[...].astype(F32)[0:8]], axis=0)
        a = ue[:, :D_FF]
        b = ue[:, D_FF:]
        rowe = i * tr + lax.broadcasted_iota(jnp.int32, (te, 1), 0)
        dae = jnp.where(rowe < Lp, jnp.concatenate([da_ref[...], dan_ref[...]], axis=0), 0.0)
        sg = _sig(a)
        du = jnp.concatenate([dae * b * (sg * (1.0 + a * (1.0 - sg))), dae * (a * sg)], axis=1)
        dus = _shifted(du, range(FFN_CONV - 1, -1, -1))
        dup_ref[...] = _taps(dus, w, tr).astype(BF16)
        upm = up_ref[...]
        rows = [jnp.sum(dus[kk][0:tr, :] * upm, axis=0, keepdims=True) for kk in range(FFN_CONV)]
        rows.append(jnp.sum(du[0:tr, :], axis=0, keepdims=True))
        part = jnp.concatenate(rows + [jnp.zeros((8 - len(rows), W2), F32)], axis=0)

        @pl.when(i == 0)
        def _():
            acc_ref[...] = part

        @pl.when(i > 0)
        def _():
            acc_ref[...] += part

    return pl.pallas_call(
        body, grid=(Lp // tr,),
        in_specs=[pl.BlockSpec((tr, W2), lambda i: (i, 0)), pl.BlockSpec((tr, W2), lambda i: (i, 0)),
                  _halo_next(tr, W2, Lp, rows=16), pl.BlockSpec((tr, D_FF), lambda i: (i, 0)), _halo_next(tr, D_FF, Lp),
                  pl.BlockSpec((FFN_CONV, W2), lambda i: (0, 0))],
        out_specs=[pl.BlockSpec((tr, W2), lambda i: (i, 0)), pl.BlockSpec((8, W2), lambda i: (0, 0))],
        out_shape=[jax.ShapeDtypeStruct((Lp, W2), BF16), jax.ShapeDtypeStruct((8, W2), F32)],
        name="ffn_act_bwd")(up, u, u, dact, dact, conv_w)


def _proj_rows(j):
    shift = (jnp.where((j >= 3) & (j < 6), _O_RQ - 3 * D_MODEL, 0) + jnp.where(j == 6, _O_GZ - 6 * D_MODEL, 0)
             + jnp.where(j >= 7, _O_RG - 7 * D_MODEL, 0))
    return j * D_MODEL + shift


def _local_step(hpad, tgt, pad, wt, first_weights=None, late_weights=None, on_ffn_out_grads=None,
                on_w_in_grads=None):
    Lp = hpad.shape[0]
    first = pad + N_META
    pos = jnp.arange(Lp, dtype=F32) - float(pad)
    half = RET_D // 2
    inv = 1.0 / (ROPE_BASE ** (jnp.arange(half, dtype=F32) / half))
    ang = pos[:, None] * inv[None, :]
    cos, sin = jnp.cos(ang), jnp.sin(ang)
    tables = _ret_tables()
    gparams = jnp.zeros((8, LANES), F32).at[0, :GDN_H].set(wt["a_log"]).at[1, :GDN_H].set(wt["dt_bias"])

    hn1 = _rms_fwd(hpad, wt["norm1"], "rms1_fwd")
    if first_weights is not None:
        wt = {**wt, **first_weights(hn1)}
    w_in_t = wt["w_in_t"]
    w_small_t = jnp.pad(w_in_t[_O_GA:_O_RQ], ((0, LANES - 2 * GDN_H), (0, 0)))
    proj_m = _mm_nn(hn1, w_in_t, bt=True, tm_target=2752, b_rows=(D_MODEL, MAIN_W // D_MODEL, _proj_rows),
                    name="proj_main")
    proj_s = _mm_nn(hn1, w_small_t, bt=True, name="proj_small")
    qkv, gsm, conv_out = _gdn_pre(proj_m, proj_s, wt["gdn_conv_w"], gparams, pad)
    o_a, s_a, t_a = _gdn_chunk_fwd(qkv, gsm)
    o_b, s_b = _ret_chunk_fwd(proj_m, cos, sin, tables)
    y = _merge_fwd(o_a, o_b, proj_m, wt["gdn_norm"])
    if late_weights is not None:
        wt = {**wt, **late_weights(y)}
    h1, hn2 = _mm_rms_fwd(_Producer(y, wt["w_out"], hpad), wt["norm2"], "out_proj_rms2")
    up = _mm_nn(hn2, wt["w_up_t"], bt=True, name="ffn_up")
    act, u_ffn = _ffn_act(up, wt["ffn_conv_w"], wt["ffn_conv_b"])
    lossvec, dh2, dh2b, d_norm_f = _final(_Producer(act, wt["w_down"], h1), wt["norm_f"], tgt, first)

    d_w_down = _mm_tn(act, dh2b, name="dw_down")
    dact = _mm_nt(dh2b, wt["w_down"], name="d_act")
    dup, ffn_rows = _ffn_act_bwd(up, u_ffn, dact, wt["ffn_conv_w"])
    d_w_up_t = _mm_tn(dup, hn2, name="dw_up")
    dh1, dh1b, d_norm2 = _rms_bwd(h1, wt["norm2"], _Producer(dup, wt["w_up_t"]), dh2, pad, "d_hn2_rms2_bwd")

    d_w_out = _mm_tn(y, dh1b, name="dw_out")
    gnorm = wt["gdn_norm"]
    if on_ffn_out_grads is not None:
        gnorm = gnorm + on_ffn_out_grads(d_w_down, d_w_up_t, d_w_out)[0:1, :]
    d_c, do_a, do_b, d_gnorm = _merge_bwd(dh1b, wt["w_out"], o_a, o_b, proj_m, gnorm)
    d_r = _ret_chunk_bwd(proj_m, cos, sin, tables, do_b, s_b)
    dq, dk, dv, dgs = _gdn_chunk_bwd(qkv, gsm, do_a, s_a, t_a)
    d_a, d_s, conv_rows, gp_rows = _gdn_pre_bwd(proj_m, conv_out, proj_s, wt["gdn_conv_w"], gparams, dq, dk, dv, dgs,
                                                pad)

    segs = [(d_a, w_in_t[_O_GQ:_O_GZ]), (d_r, w_in_t[_O_RQ:_O_RG]),
            (d_c, jnp.concatenate([w_in_t[_O_GZ:_O_GA], w_in_t[_O_RG:_O_END]], axis=0))]
    pa, pr, pc = [_mm_tn(d, hn1, BF16, name="dw_in_%d" % i) for i, (d, _) in enumerate(segs)]
    ps = _mm_tn(d_s, hn1, BF16, name="dw_in_small")
    d_w_in_t = jnp.concatenate([pa, pc[:D_MODEL], ps[:2 * GDN_H], pr, pc[D_MODEL:]], axis=0)
    if on_w_in_grads is not None:
        w_small_t = w_small_t + on_w_in_grads(d_w_in_t)[0:1, 0:1].astype(w_small_t.dtype)
    dhn1 = _mm_sum([(d_s, w_small_t)] + segs[:-1], "d_hn1_first")
    dh0, _, d_norm1 = _rms_bwd(hpad, wt["norm1"], _Producer(*segs[-1], dhn1), dh1, pad, "d_hn1_rms1_bwd")

    grads = {
        "norm1": d_norm1, "w_in_t": d_w_in_t, "gdn_conv_w": conv_rows[:GDN_CONV],
        "a_log": gp_rows[0, :GDN_H], "dt_bias": gp_rows[1, :GDN_H], "gdn_norm": d_gnorm, "w_out": d_w_out,
        "norm2": d_norm2, "w_up_t": d_w_up_t, "ffn_conv_w": ffn_rows[:FFN_CONV],
        "ffn_conv_b": ffn_rows[FFN_CONV:FFN_CONV + 1], "w_down": d_w_down, "norm_f": d_norm_f,
    }
    return lossvec, dh0, grads


def _peer(k):
    ix, iy, ic = lax.axis_index("x"), lax.axis_index("y"), lax.axis_index("c")
    px = 1 - ix if (k >> 2) & 1 else ix
    py = 1 - iy if (k >> 1) & 1 else iy
    pc = 1 - ic if k & 1 else ic
    return (px, py, pc), 4 * px + 2 * py + pc


def _comm_call(body, n, out_shapes, name, args):
    hbm = pl.BlockSpec(memory_space=pl.ANY)
    return pl.pallas_call(
        body, out_shape=out_shapes, in_specs=[hbm] * n, out_specs=[hbm] * n,
        scratch_shapes=[pltpu.SemaphoreType.DMA((n, N_DEV - 1)), pltpu.SemaphoreType.DMA((n, N_DEV - 1)),
                        pltpu.SemaphoreType.DMA((n,))],
        name=name)(*args)


def _all_gather(xs, name):
    n = len(xs)

    def body(*refs):
        x_refs, out_refs = refs[:n], refs[n:2 * n]
        send_sems, recv_sems, local_sems = refs[2 * n:]
        _, me = _peer(0)
        pending = []
        for i in range(n):
            local = pltpu.make_async_copy(x_refs[i], out_refs[i].at[me], local_sems.at[i])
            local.start()
            pending.append(local)
        sends = []
        for i in range(n):
            for k in range(1, N_DEV):
                dev, _ = _peer(k)
                cp = pltpu.make_async_remote_copy(
                    src_ref=x_refs[i], dst_ref=out_refs[i].at[me], send_sem=send_sems.at[i, k - 1],
                    recv_sem=recv_sems.at[i, k - 1], device_id=dev, device_id_type=MESH_T)
                cp.start()
                sends.append(cp)
        for i in range(n):
            for k in range(1, N_DEV):
                dev, idx = _peer(k)
                pltpu.make_async_remote_copy(
                    src_ref=x_refs[i], dst_ref=out_refs[i].at[idx], send_sem=send_sems.at[i, k - 1],
                    recv_sem=recv_sems.at[i, k - 1], device_id=dev, device_id_type=MESH_T).wait_recv()
        for cp in sends:
            cp.wait_send()
        for local in pending:
            local.wait()

    out_shapes = [jax.ShapeDtypeStruct((N_DEV,) + a.shape, a.dtype) for a in xs]
    return _comm_call(body, n, out_shapes, name, xs)


def _all_to_all(gs, name):
    n = len(gs)

    def body(*refs):
        g_refs, out_refs = refs[:n], refs[n:2 * n]
        send_sems, recv_sems, local_sems = refs[2 * n:]
        _, me = _peer(0)
        pending = []
        for i in range(n):
            local = pltpu.make_async_copy(g_refs[i].at[me], out_refs[i].at[0], local_sems.at[i])
            local.start()
            pending.append(local)
        sends = []
        for i in range(n):
            for k in range(1, N_DEV):
                dev, idx = _peer(k)
                cp = pltpu.make_async_remote_copy(
                    src_ref=g_refs[i].at[idx], dst_ref=out_refs[i].at[k], send_sem=send_sems.at[i, k - 1],
                    recv_sem=recv_sems.at[i, k - 1], device_id=dev, device_id_type=MESH_T)
                cp.start()
                sends.append(cp)
        for cp in sends:
            cp.wait_recv()
        for cp in sends:
            cp.wait_send()
        for local in pending:
            local.wait()

    out_shapes = [jax.ShapeDtypeStruct(g.shape, g.dtype) for g in gs]
    return _comm_call(body, n, out_shapes, name, gs)


_SPLIT_RELATIONS = {"gather": tuple(range(1, N_DEV)), "a2a": tuple(range(1, N_DEV)), "chip": (1, 2, 4, 6),
                    "forward": (2, 4, 6)}


def _split_copies(kind, src_refs, land_refs, send_sems, recv_sems, local_sems, with_recv):
    n = len(land_refs)
    rels = _SPLIT_RELATIONS[kind]
    _, me = _peer(0)
    locals_, remotes = [], []
    for i in range(n):
        if kind in ("gather", "chip"):
            locals_.append(pltpu.make_async_copy(src_refs[i], land_refs[i].at[me], local_sems.at[i]))
        elif kind == "a2a":
            locals_.append(pltpu.make_async_copy(src_refs[i].at[me], land_refs[i].at[0], local_sems.at[i]))
        for jj, k in enumerate(rels):
            dev, idx = _peer(k)
            if kind in ("gather", "chip"):
                src, dst, mine = src_refs[i], land_refs[i].at[me], land_refs[i].at[idx]
            elif kind == "a2a":
                src, dst, mine = src_refs[i].at[idx], land_refs[i].at[k], land_refs[i].at[k]
            else:
                dev, _ = _peer(1)
                _, came = _peer(k + 1)
                src, dst, mine = land_refs[i].at[idx], land_refs[i].at[idx], land_refs[i].at[came]
            j = i * len(rels) + jj
            send = pltpu.make_async_remote_copy(
                src_ref=src, dst_ref=dst, send_sem=send_sems.at[j], recv_sem=recv_sems.at[j],
                device_id=dev, device_id_type=MESH_T)
            recv = pltpu.make_async_remote_copy(
                src_ref=src, dst_ref=mine, send_sem=send_sems.at[j], recv_sem=recv_sems.at[j],
                device_id=dev, device_id_type=MESH_T) if with_recv else None
            remotes.append((send, recv))
    return locals_, remotes


_HBM = pl.BlockSpec(memory_space=pltpu.HBM)
_SEM = pl.BlockSpec(memory_space=pltpu.SEMAPHORE)
_ANY = pl.BlockSpec(memory_space=pl.ANY)


def _split_start(srcs, kind, name, after):
    n = len(srcs)
    if kind == "forward":
        arrays = list(srcs)
    else:
        gathers = kind in ("gather", "chip")
        arrays = list(srcs) + [lax.empty(((N_DEV,) + a.shape) if gathers else a.shape, a.dtype) for a in srcs]
    na = len(arrays)

    def body(*refs):
        src_refs, land_refs = refs[:n], refs[na - n:na]
        send_sems, recv_sems, local_sems = refs[na + 1:na + 4]
        token = refs[-1]
        locals_, remotes = _split_copies(kind, src_refs, land_refs, send_sems, recv_sems, local_sems, False)
        for cp in locals_:
            cp.start()
        for send, _ in remotes:
            send.start()
        token[...] = jnp.zeros_like(token)

    ncp = n * len(_SPLIT_RELATIONS[kind])
    sems = (pltpu.SemaphoreType.DMA((ncp,)), pltpu.SemaphoreType.DMA((ncp,)), pltpu.SemaphoreType.DMA((n,)))
    thru = tuple(pltpu.HBM(a.shape, a.dtype) for a in arrays)
    outs = pl.pallas_call(
        body, name=name,
        out_shape=sems + thru + (jax.ShapeDtypeStruct((8, LANES), F32),),
        in_specs=[_HBM] * na + [_ANY],
        out_specs=[_SEM] * 3 + [_HBM] * na + [pl.BlockSpec(memory_space=pltpu.VMEM)],
        input_output_aliases={i: 3 + i for i in range(na)},
        compiler_params=pltpu.CompilerParams(has_side_effects=pltpu.SideEffectType.DATAFLOW_SIDE_EFFECTING),
    )(*[pltpu.with_memory_space_constraint(a, pltpu.HBM) for a in arrays], after)
    return (kind, n, outs[:3], outs[3:3 + na]), outs[-1]


def _split_wait(handle, name, after):
    kind, n, sems, thru = handle
    na = len(thru)

    def body(*refs):
        src_refs, land_refs = refs[:n], refs[na - n:na]
        send_sems, recv_sems, local_sems = refs[na:na + 3]
        locals_, remotes = _split_copies(kind, src_refs, land_refs, send_sems, recv_sems, local_sems, True)
        for send, recv in remotes:
            send.wait_send()
            recv.wait_recv()
        for cp in locals_:
            cp.wait()

    outs = pl.pallas_call(
        body, name=name, out_shape=tuple(pltpu.HBM(a.shape, a.dtype) for a in thru),
        in_specs=[_HBM] * na + [_SEM] * 3 + [_ANY], out_specs=[_HBM] * na,
        input_output_aliases={i: i for i in range(na)},
        compiler_params=pltpu.CompilerParams(has_side_effects=pltpu.SideEffectType.DATAFLOW_SIDE_EFFECTING),
    )(*thru, *sems, after)
    return list(outs[na - n:])


def _adamw(gslabs, w, m, v, name):
    R, Cw = w.shape
    if R % 8 == 0:
        tr, tc = _tile(R, 64 if Cw > 1024 else 128, 8), Cw
    else:
        tr, tc = R, LANES
    c1 = 1.0 - ADAM_B1 ** ADAM_STEP
    c2 = 1.0 - ADAM_B2 ** ADAM_STEP

    def body(g_ref, w_ref, m_ref, v_ref, go_ref, d_ref, mo_ref, vo_ref):
        g = g_ref[0].astype(F32)
        for k in range(1, N_DEV):
            g = g + g_ref[k].astype(F32)
        mn = ADAM_B1 * m_ref[...] + (1.0 - ADAM_B1) * g
        vn = ADAM_B2 * v_ref[...] + (1.0 - ADAM_B2) * (g * g)
        m_hat = mn / c1
        v_hat = vn / c2
        go_ref[...] = g
        d_ref[...] = -ADAM_LR * (m_hat / (jnp.sqrt(v_hat) + ADAM_EPS) + ADAM_WD * w_ref[...])
        mo_ref[...] = mn
        vo_ref[...] = vn

    blk = pl.BlockSpec((tr, tc), lambda i, j: (i, j))
    return pl.pallas_call(
        body, grid=(R // tr, Cw // tc),
        in_specs=[pl.BlockSpec((N_DEV, tr, tc), lambda i, j: (0, i, j)), blk, blk, blk],
        out_specs=[blk] * 4, out_shape=[jax.ShapeDtypeStruct((R, Cw), F32)] * 4, name=name)(gslabs, w, m, v)


def _pack(arrs, row_mult, dtype=F32):
    parts = []
    total = 0
    for a in arrs:
        f = a.reshape(-1).astype(dtype)
        n = -(-f.shape[0] // 1024) * 1024
        parts.append(jnp.pad(f, (0, n - f.shape[0])))
        total += n
    rows = total // LANES
    rows_p = -(-rows // row_mult) * row_mult
    flat = jnp.concatenate(parts)
    flat = jnp.pad(flat, (0, rows_p * LANES - total))
    return flat.reshape(rows_p, LANES)


def _unpack(packed, shapes):
    lead = packed.shape[:-2]
    flat = packed.reshape(lead + (-1,))
    out = []
    off = 0
    for s in shapes:
        n = int(np.prod(s))
        out.append(flat[..., off:off + n].reshape(lead + tuple(s)))
        off += -(-n // 1024) * 1024
    return out


def _gather_cols(stacked):
    d, r, c = stacked.shape
    return stacked.transpose(1, 0, 2).reshape(r, d * c)


def _scatter_cols(full):
    r, n = full.shape
    return full.reshape(r, N_DEV, n // N_DEV).transpose(1, 0, 2)


def kernel(x, meta, norm1, w_in, gdn_conv_w, gdn_a_log, gdn_dt_bias, gdn_norm, w_out, norm2, w_ffn_up, ffn_conv_w, ffn_conv_b, w_ffn_down, norm_f, loss_target, m_meta, m_norm1, m_w_in, m_gdn_conv_w, m_gdn_a_log, m_gdn_dt_bias, m_gdn_norm, m_w_out, m_norm2, m_w_ffn_up, m_ffn_conv_w, m_ffn_conv_b, m_w_ffn_down, m_norm_f, v_meta, v_norm1, v_w_in, v_gdn_conv_w, v_gdn_a_log, v_gdn_dt_bias, v_gdn_norm, v_w_out, v_norm2, v_w_ffn_up, v_ffn_conv_w, v_ffn_conv_b, v_w_ffn_down, v_norm_f):
    S = x.shape[1]
    L = N_META + S
    pad = (-L) % CHUNK
    Lp = L + pad

    tr_ = lambda a: jnp.swapaxes(a[0], 0, 1)
    big = [tr_(w_in), w_out[0], tr_(w_ffn_up), w_ffn_down[0]]
    small = [meta, gdn_conv_w, ffn_conv_w]
    small_all, = _all_gather([_pack(small, 8)], "gather_small_weights")
    first, first_token = _split_start([big[0].astype(BF16)], "chip", "gather_w_in_start", small_all)
    late, late_token = _split_start([a.astype(BF16) for a in big[1:]], "gather", "gather_late_start", first_token)

    def first_weights(after):
        half = _split_wait(first, "gather_w_in_wait", after)
        second, second_token = _split_start(half, "forward", "gather_w_in_forward_start", after)
        w_in_s, = _split_wait(second, "gather_w_in_forward_wait", second_token)
        return {"w_in_t": w_in_s.reshape(_O_END, D_MODEL)}

    def late_weights(after):
        w_out_s, w_up_s, w_down_s = _split_wait(late, "gather_late_wait", after)
        return {"w_out": w_out_s.reshape(D_MODEL, D_MODEL), "w_up_t": w_up_s.reshape(2 * D_FF, D_MODEL),
                "w_down": w_down_s.reshape(D_FF, D_MODEL)}

    meta_s, gconv_s, fconv_s = _unpack(small_all, [a.shape for a in small])
    wt = {
        "norm1": norm1 + jnp.tile(late_token[0:1, :], (1, D_MODEL // LANES)),
        "gdn_conv_w": _gather_cols(gconv_s[:, 0]), "a_log": gdn_a_log[0], "dt_bias": gdn_dt_bias[0],
        "gdn_norm": gdn_norm, "norm2": norm2, "ffn_conv_w": _gather_cols(fconv_s[:, 0]), "ffn_conv_b": ffn_conv_b,
        "norm_f": norm_f.reshape(1, D_MODEL),
    }
    meta_f = _gather_cols(meta_s)

    pending = {}

    def on_ffn_out_grads(d_w_down, d_w_up_t, d_w_out):
        srcs = [d_w_out.reshape(N_DEV, D_MODEL // N_DEV, D_MODEL), d_w_up_t.reshape(N_DEV, 2 * D_FF // N_DEV, D_MODEL),
                d_w_down.reshape(N_DEV, D_FF // N_DEV, D_MODEL)]
        pending["ffn_out"], token = _split_start(srcs, "a2a", "exchange_ffn_out_start", d_w_out)
        return token

    def on_w_in_grads(d_w_in_t):
        slabs = d_w_in_t.astype(BF16).reshape(N_DEV, _O_END // N_DEV, D_MODEL)
        pending["w_in"], token = _split_start([slabs], "a2a", "exchange_w_in_start", d_w_in_t)
        return token

    head = jnp.concatenate([jnp.zeros((pad, D_MODEL), F32), meta_f], axis=0)
    if S >= 2 * 704:
        hpad = _Rows(x[0], pad + N_META, head)
        tgt = _Rows(loss_target[0], pad + N_META)
    else:
        hpad = jnp.concatenate([head, x[0]], axis=0)
        tgt = jnp.concatenate([jnp.zeros((pad + N_META, D_MODEL), F32), loss_target[0]], axis=0)
    lossvec, dh0, gr = _local_step(hpad, tgt, pad, wt, first_weights, late_weights, on_ffn_out_grads, on_w_in_grads)

    loss = lax.psum(jnp.sum(lossvec), ("x", "y", "c"))
    grad_x = dh0[pad + N_META:][None]

    big_m = [tr_(m_w_in), m_w_out[0], tr_(m_w_ffn_up), m_w_ffn_down[0]]
    big_v = [tr_(v_w_in), v_w_out[0], tr_(v_w_ffn_up), v_w_ffn_down[0]]
    slabs_ffn_out = _split_wait(pending["ffn_out"], "exchange_ffn_out_wait", dh0)
    big_out = [None] + [_adamw(slabs_ffn_out[i - 1], big[i], big_m[i], big_v[i], "adamw_big_%d" % i)
                        for i in range(1, len(big))]
    g_sm = [_scatter_cols(dh0[pad:pad + N_META]), _scatter_cols(gr["gdn_conv_w"]), _scatter_cols(gr["ffn_conv_w"])]
    g_small = jnp.stack([_pack([g[d] for g in g_sm], 8) for d in range(N_DEV)])
    slabs_small, = _all_to_all([g_small], "exchange_small_gradients")
    small_out = _adamw(slabs_small, _pack(small, 8), _pack([m_meta, m_gdn_conv_w, m_ffn_conv_w], 8),
                       _pack([v_meta, v_gdn_conv_w, v_ffn_conv_w], 8), "adamw_small_sharded")
    small_un = [_unpack(o, [a.shape for a in small]) for o in small_out]
    rep_w = [norm1, gdn_a_log, gdn_dt_bias, gdn_norm, norm2, ffn_conv_b, norm_f]
    rep_m = [m_norm1, m_gdn_a_log, m_gdn_dt_bias, m_gdn_norm, m_norm2, m_ffn_conv_b, m_norm_f]
    rep_v = [v_norm1, v_gdn_a_log, v_gdn_dt_bias, v_gdn_norm, v_norm2, v_ffn_conv_b, v_norm_f]
    rep_g = [gr["norm1"], gr["a_log"], gr["dt_bias"], gr["gdn_norm"], gr["norm2"], gr["ffn_conv_b"], gr["norm_f"]]
    rep_slabs, = _all_gather([_pack(rep_g, 8)], "gather_small_gradients")
    rep_out = _adamw(rep_slabs, _pack(rep_w, 8), _pack(rep_m, 8), _pack(rep_v, 8), "adamw_replicated")
    rep_shapes = [a.shape for a in rep_w]
    rp_g, rp_d, rp_nm, rp_nv = [_unpack(o, rep_shapes) for o in rep_out]

    slabs_w_in, = _split_wait(pending["w_in"], "exchange_w_in_wait", rep_out[0])
    big_out[0] = _adamw(slabs_w_in, big[0], big_m[0], big_v[0], "adamw_big_0")
    back = lambda a: jnp.swapaxes(a, 0, 1)[None]
    sh_g, sh_d, sh_nm, sh_nv = [
        [small_un[j][0], back(big_out[0][j]), small_un[j][1], big_out[1][j][None], back(big_out[2][j]),
         small_un[j][2], big_out[3][j][None]] for j in range(4)]

    def order(sh, rp):
        return [sh[0], rp[0], sh[1], sh[2], rp[1], rp[2], rp[3], sh[3], rp[4], sh[4], sh[5], rp[5], sh[6], rp[6]]

    return (loss, grad_x, *order(sh_g, rp_g), *order(sh_d, rp_d), *order(sh_nm, rp_nm), *order(sh_nv, rp_nv))
```

```python
import functools
import math

import numpy as np
import jax
import jax.numpy as jnp
from jax import lax
from jax.experimental import pallas as pl
from jax.experimental.pallas import tpu as pltpu

F32 = jnp.float32
BF16 = jnp.bfloat16

D_MODEL = 1024
N_META = 16
CHUNK = 64
GDN_H = 8
GDN_D = 128
RET_H = 4
RET_D = 256
D_FF = 2816
GDN_CONV = 4
FFN_CONV = 3
ROPE_BASE = 10000.0
EPS = 1e-6
N_DEV = 8
LANES = 128
MAIN_W = 10 * 1024
_O_GQ, _O_GZ, _O_GA, _O_RQ, _O_RG, _O_GATE, _O_END = 0, 3072, 4096, 4112, 7184, 8208, 10256

ADAM_LR = 0.001
ADAM_B1 = 0.9
ADAM_B2 = 0.999
ADAM_EPS = 1e-08
ADAM_WD = 0.01
ADAM_STEP = 10

MESH_T = pl.DeviceIdType.MESH


def _tile(n, target, mult):
    best = None
    for d in range(mult, min(n, target) + 1, mult):
        if n % d == 0:
            best = d
    assert best is not None, (n, target, mult)
    return best


def _sig(x):
    return 1.0 / (1.0 + jnp.exp(-x))


def _d(a, b):
    return jnp.dot(a.astype(BF16), b.astype(BF16), preferred_element_type=F32)


def _dnt(a, b):
    return lax.dot_general(a.astype(BF16), b.astype(BF16), (((1,), (1,)), ((), ())), preferred_element_type=F32)


def _dtn(a, b):
    return lax.dot_general(a.astype(BF16), b.astype(BF16), (((0,), (0,)), ((), ())), preferred_element_type=F32)


def _dxg(a, b, dims):
    f = functools.partial(lax.dot_general, dimension_numbers=dims, preferred_element_type=F32)
    ab = a.astype(BF16)
    b1 = b.astype(BF16)
    r1 = b - b1.astype(F32)
    b2 = r1.astype(BF16)
    b3 = (r1 - b2.astype(F32)).astype(BF16)
    return f(ab, b1) + (f(ab, b2) + f(ab, b3))


def _dx(a, b):
    return _dxg(a, b, (((1,), (0,)), ((), ())))


def _dxnt(a, b):
    return _dxg(a, b, (((1,), (1,)), ((), ())))


def _split(a):
    hi = a.astype(BF16)
    return hi, (a - hi.astype(F32)).astype(BF16)


def _d3g(a, b, dims):
    ah, al = _split(a)
    bh, bl = _split(b)
    f = functools.partial(lax.dot_general, dimension_numbers=dims, preferred_element_type=F32)
    if dims == _NN:
        rows = a.shape[0]
        both = f(jnp.concatenate([ah, al], axis=0), bh)
        return both[:rows] + (f(ah, bl) + both[rows:])
    return f(ah, bh) + (f(ah, bl) + f(al, bh))


def _d2x(a, b, dims):
    ah, al = _split(a)
    f = functools.partial(lax.dot_general, dimension_numbers=dims, preferred_element_type=F32)
    bb = b.astype(BF16)
    return f(ah, bb) + f(al, bb)


_NN = (((1,), (0,)), ((), ()))
_NT = (((1,), (1,)), ((), ()))
_TN = (((0,), (0,)), ((), ()))


def _rowsum(x):
    return jnp.sum(x, axis=1, keepdims=True)


def _allsum(x):
    return jnp.sum(jnp.sum(x, axis=1, keepdims=True), axis=0, keepdims=True)


def _mm_nn(a, b, res=None, out_dtype=F32, bt=False, tm_target=704, b_rows=None, name="mm_nn"):
    M, K = a.shape
    N = b.shape[0] if bt else b.shape[1]
    tm = _tile(M, tm_target, 16)
    if b_rows is None:
        tn = _tile(N, 2816, 128)
    else:
        tn, n_tiles, start = b_rows
        N = tn * n_tiles

    def body(*refs):
        if res is None:
            a_ref, b_ref, o_ref = refs
        else:
            a_ref, b_ref, r_ref, o_ref = refs
        acc = lax.dot_general(a_ref[...], b_ref[...], _NT if bt else _NN, preferred_element_type=F32)
        if res is not None:
            acc = acc + r_ref[...]
        o_ref[...] = acc.astype(out_dtype)

    b_spec = pl.BlockSpec((tn, K), lambda j, i: (j, 0)) if bt else pl.BlockSpec((K, tn), lambda j, i: (0, j))
    if b_rows is not None:
        b_spec = pl.BlockSpec((pl.Element(tn), pl.Element(K)), lambda j, i: (pl.multiple_of(start(j), 16), 0))
    in_specs = [pl.BlockSpec((tm, K), lambda j, i: (i, 0)), b_spec]
    args = [a, b]
    if res is not None:
        in_specs.append(pl.BlockSpec((tm, tn), lambda j, i: (i, j)))
        args.append(res)
    return pl.pallas_call(
        body, grid=(N // tn, M // tm), in_specs=in_specs,
        out_specs=pl.BlockSpec((tm, tn), lambda j, i: (i, j)),
        out_shape=jax.ShapeDtypeStruct((M, N), out_dtype), name=name)(*args)


def _mm_sum(pairs, name):
    M = pairs[0][0].shape[0]
    N = pairs[0][1].shape[1]
    tm = _tile(M, 704, 16)
    n = len(pairs)

    def body(*refs):
        o_ref = refs[-1]
        acc = jnp.dot(refs[0][...], refs[1][...], preferred_element_type=F32)
        for i in range(1, n):
            acc = acc + jnp.dot(refs[2 * i][...], refs[2 * i + 1][...], preferred_element_type=F32)
        o_ref[...] = acc

    specs, args = [], []
    for a, b in pairs:
        specs += [pl.BlockSpec((tm, a.shape[1]), lambda i: (i, 0)),
                  pl.BlockSpec(b.shape, lambda i: (0, 0), pipeline_mode=pl.Buffered(1))]
        args += [a, b]
    return pl.pallas_call(
        body, grid=(M // tm,), in_specs=specs, out_specs=pl.BlockSpec((tm, N), lambda i: (i, 0)),
        out_shape=jax.ShapeDtypeStruct((M, N), F32), name=name)(*args)


def _mm_nt(a, b, res=None, name="mm_nt"):
    M, Nc = a.shape
    K = b.shape[0]
    tm = _tile(M, 704, 16)
    tc = _tile(Nc, 5632, 128)

    def body(*refs):
        if res is None:
            a_ref, b_ref, o_ref = refs
        else:
            a_ref, b_ref, r_ref, o_ref = refs
        c = pl.program_id(1)
        p = lax.dot_general(a_ref[...], b_ref[...], (((1,), (1,)), ((), ())), preferred_element_type=F32)

        @pl.when(c == 0)
        def _():
            if res is None:
                o_ref[...] = p
            else:
                o_ref[...] = p + r_ref[...]

        @pl.when(c > 0)
        def _():
            o_ref[...] += p

    in_specs = [pl.BlockSpec((tm, tc), lambda i, c: (i, c)), pl.BlockSpec((K, tc), lambda i, c: (0, c))]
    args = [a, b]
    if res is not None:
        in_specs.append(pl.BlockSpec((tm, K), lambda i, c: (i, 0)))
        args.append(res)
    return pl.pallas_call(
        body, grid=(M // tm, Nc // tc), in_specs=in_specs,
        out_specs=pl.BlockSpec((tm, K), lambda i, c: (i, 0)),
        out_shape=jax.ShapeDtypeStruct((M, K), F32), name=name)(*args)


def _mm_tn(a, b, out_dtype=F32, name="mm_tn"):
    M, K = a.shape
    N = b.shape[1]
    tm = _tile(M, 2752, 16)
    tk = _tile(K, 1408, 128)
    tn = _tile(N, 1408, 128)
    steps = M // tm

    def body(a_ref, b_ref, o_ref, *scratch):
        acc = scratch[0] if scratch else o_ref
        m = pl.program_id(2)
        p = lax.dot_general(a_ref[...], b_ref[...], (((0,), (0,)), ((), ())), preferred_element_type=F32)

        @pl.when(m == 0)
        def _():
            acc[...] = p

        @pl.when(m > 0)
        def _():
            acc[...] += p

        if scratch:
            @pl.when(m == steps - 1)
            def _():
                o_ref[...] = acc[...].astype(out_dtype)

    return pl.pallas_call(
        body, grid=(K // tk, N // tn, steps),
        in_specs=[pl.BlockSpec((tm, tk), lambda kk, j, m: (m, kk)), pl.BlockSpec((tm, tn), lambda kk, j, m: (m, j))],
        out_specs=pl.BlockSpec((tk, tn), lambda kk, j, m: (kk, j)),
        out_shape=jax.ShapeDtypeStruct((K, N), out_dtype),
        scratch_shapes=[] if out_dtype == F32 else [pltpu.VMEM((tk, tn), F32)], name=name)(a, b)


class _Rows:
    def __init__(self, body, first, head=None):
        self.body, self.first, self.head = body, first, head
        self.shape = (body.shape[0] + first, body.shape[1])


def _rows_operands(x, tr):
    if not isinstance(x, _Rows):
        return [x], [pl.BlockSpec((tr, x.shape[1]), lambda i: (i, 0))]
    assert x.first % 8 == 0 and x.first <= tr <= x.body.shape[0] and x.shape[0] % tr == 0
    width = x.shape[1]
    args = [x.body]
    specs = [pl.BlockSpec((pl.Element(tr), pl.Element(width)),
                          lambda i: (pl.multiple_of(jnp.maximum(i * tr - x.first, 0), 8), 0))]
    if x.head is not None:
        args.append(jnp.pad(x.head, ((0, tr - x.first), (0, 0))))
        specs.append(pl.BlockSpec((tr, width), lambda i: (0, 0)))
    return args, specs


def _rows_tile(x, refs, i, tr):
    blk = refs[0][...]
    if not isinstance(x, _Rows):
        return blk
    shifted = pltpu.roll(blk, x.first, 0)
    if x.head is not None:
        row = lax.broadcasted_iota(jnp.int32, (tr, 1), 0)
        shifted = jnp.where(row < x.first, refs[1][...], shifted)
    return jnp.where(i == 0, shifted, blk)


def _rms_fwd(x, g, name):
    Lp = x.shape[0]
    tr = _tile(Lp, 256, 16)
    args, specs = _rows_operands(x, tr)
    n = len(args)

    def body(*refs):
        g_ref, o_ref = refs[n:]
        xv = _rows_tile(x, refs[:n], pl.program_id(0), tr)
        r = lax.rsqrt(jnp.mean(xv * xv, axis=-1, keepdims=True) + EPS)
        o_ref[...] = (xv * r * g_ref[...]).astype(BF16)

    return pl.pallas_call(
        body, grid=(Lp // tr,),
        in_specs=specs + [pl.BlockSpec((1, D_MODEL), lambda i: (0, 0))],
        out_specs=pl.BlockSpec((tr, D_MODEL), lambda i: (i, 0)),
        out_shape=jax.ShapeDtypeStruct((Lp, D_MODEL), BF16), name=name)(*args, g)


class _Producer:
    def __init__(self, a, b, res=None):
        self.a, self.b, self.res = a, b, res
        self.tr = _tile(a.shape[0], 704, 16)
        K = a.shape[1]
        r_args, r_specs = ([], []) if res is None else _rows_operands(res, self.tr)
        self.args = [a, b] + r_args
        self.specs = [pl.BlockSpec((self.tr, K), lambda i: (i, 0)),
                      pl.BlockSpec((K, D_MODEL), lambda i: (0, 0), pipeline_mode=pl.Buffered(1))] + r_specs

    def tile(self, refs, i):
        acc = jnp.dot(refs[0][...], refs[1][...], preferred_element_type=F32)
        return acc if self.res is None else acc + _rows_tile(self.res, refs[2:], i, self.tr)


def _mm_rms_fwd(prod, g, name):
    Lp, tr, n = prod.a.shape[0], prod.tr, len(prod.args)

    def body(*refs):
        g_ref, x_ref, o_ref = refs[n:]
        xv = prod.tile(refs[:n], pl.program_id(0))
        r = lax.rsqrt(jnp.mean(xv * xv, axis=-1, keepdims=True) + EPS)
        x_ref[...] = xv
        o_ref[...] = (xv * r * g_ref[...]).astype(BF16)

    blk = pl.BlockSpec((tr, D_MODEL), lambda i: (i, 0))
    return pl.pallas_call(
        body, grid=(Lp // tr,), in_specs=prod.specs + [pl.BlockSpec((1, D_MODEL), lambda i: (0, 0))],
        out_specs=[blk, blk],
        out_shape=[jax.ShapeDtypeStruct((Lp, D_MODEL), F32), jax.ShapeDtypeStruct((Lp, D_MODEL), BF16)],
        name=name)(*prod.args, g)


def _rms_bwd(x, g, dy, dres, pad, name):
    Lp = x.shape[0]
    fused = isinstance(dy, _Producer)
    tr = dy.tr if fused else _tile(Lp, 256, 16)
    n = len(dy.args) if fused else 1
    x_args, x_specs = _rows_operands(x, tr)
    nx = len(x_args)

    def body(*refs):
        g_ref, dr_ref, dx_ref, dxb_ref, dg_ref = refs[n + nx:]
        i = pl.program_id(0)
        xv = _rows_tile(x, refs[n:n + nx], i, tr)
        r = lax.rsqrt(jnp.mean(xv * xv, axis=-1, keepdims=True) + EPS)
        xh = xv * r
        dyv = dy.tile(refs[:n], i) if fused else refs[0][...]
        dxh = dyv * g_ref[...]
        dx = r * (dxh - xh * jnp.mean(dxh * xh, axis=-1, keepdims=True)) + dr_ref[...]
        row = i * tr + lax.broadcasted_iota(jnp.int32, (tr, 1), 0)
        dx = jnp.where(row >= pad, dx, 0.0)
        dx_ref[...] = dx
        dxb_ref[...] = dx.astype(BF16)
        part = jnp.sum(dyv * xh, axis=0, keepdims=True)

        @pl.when(i == 0)
        def _():
            dg_ref[...] = part

        @pl.when(i > 0)
        def _():
            dg_ref[...] += part

    blk = pl.BlockSpec((tr, D_MODEL), lambda i: (i, 0))
    vec = pl.BlockSpec((1, D_MODEL), lambda i: (0, 0))
    return pl.pallas_call(
        body, grid=(Lp // tr,), in_specs=(dy.specs if fused else [blk]) + x_specs + [vec, blk],
        out_specs=[blk, blk, vec],
        out_shape=[jax.ShapeDtypeStruct((Lp, D_MODEL), F32), jax.ShapeDtypeStruct((Lp, D_MODEL), BF16),
                   jax.ShapeDtypeStruct((1, D_MODEL), F32)], name=name)(*(dy.args if fused else [dy]), *x_args, g, dres)


def _final(h2, g, tgt, first_row):
    fused = isinstance(h2, _Producer)
    Lp = h2.a.shape[0] if fused else h2.shape[0]
    tr = h2.tr if fused else _tile(Lp, 256, 16)
    n = len(h2.args) if fused else 1
    t_args, t_specs = _rows_operands(tgt, tr)
    nt = len(t_args)

    def body(*refs):
        g_ref = refs[n]
        loss_ref, dx_ref, dxb_ref, dg_ref = refs[n + 1 + nt:]
        i = pl.program_id(0)
        xv = h2.tile(refs[:n], i) if fused else refs[0][...]
        tv = _rows_tile(tgt, refs[n + 1:n + 1 + nt], i, tr)
        gv = g_ref[...]
        r = lax.rsqrt(jnp.mean(xv * xv, axis=-1, keepdims=True) + EPS)
        xh = xv * r
        row = i * tr + lax.broadcasted_iota(jnp.int32, (tr, 1), 0)
        err = jnp.where(row >= first_row, xh * gv - tv, 0.0)
        lpart = jnp.sum(err * err, axis=0, keepdims=True) * (0.5 / D_MODEL)
        dyv = err * (1.0 / D_MODEL)
        dxh = dyv * gv
        dx = r * (dxh - xh * jnp.mean(dxh * xh, axis=-1, keepdims=True))
        dx_ref[...] = dx
        dxb_ref[...] = dx.astype(BF16)
        part = jnp.sum(dyv * xh, axis=0, keepdims=True)

        @pl.when(i == 0)
        def _():
            dg_ref[...] = part
            loss_ref[...] = lpart

        @pl.when(i > 0)
        def _():
            dg_ref[...] += part
            loss_ref[...] += lpart

    blk = pl.BlockSpec((tr, D_MODEL), lambda i: (i, 0))
    vec = pl.BlockSpec((1, D_MODEL), lambda i: (0, 0))
    return pl.pallas_call(
        body, grid=(Lp // tr,), in_specs=(h2.specs if fused else [blk]) + [vec] + t_specs,
        out_specs=[vec, blk, blk, vec],
        out_shape=[jax.ShapeDtypeStruct((1, D_MODEL), F32), jax.ShapeDtypeStruct((Lp, D_MODEL), F32),
                   jax.ShapeDtypeStruct((Lp, D_MODEL), BF16), jax.ShapeDtypeStruct((1, D_MODEL), F32)],
        name="final_norm_loss")(*(h2.args if fused else [h2]), g, *t_args)


def _halo_prev(tr, width, col=0):
    return pl.BlockSpec((8, width), lambda i: (jnp.maximum(i * (tr // 8) - 1, 0), col))


def _halo_next(tr, width, nrows, col=0, rows=8):
    last = nrows // rows - 1
    return pl.BlockSpec((rows, width), lambda i: (jnp.minimum((i + 1) * (tr // rows), last), col))


def _shifted(x, offs):
    n = x.shape[0]
    return [x if off == 0 else pltpu.roll(x, n - off, 0) for off in offs]


def _taps(wins, w, rows, bias=None):
    acc = w[0:1, :] * wins[0][0:rows, :]
    if bias is not None:
        acc = acc + bias
    for kk in range(1, len(wins)):
        acc = acc + w[kk:kk + 1, :] * wins[kk][0:rows, :]
    return acc


def _gdn_pre(proj_m, proj_s, conv_w, gparams, pad):
    Lp = proj_m.shape[0]
    tr = _tile(Lp, 192, 64)
    W3 = 3 * D_MODEL

    def body(main_ref, prev_ref, s_ref, w_ref, gp_ref, qkv_ref, gsm_ref, c_ref):
        i = pl.program_id(0)
        prev = jnp.where(i > 0, prev_ref[...], 0.0)
        ext = jnp.concatenate([prev, main_ref[...]], axis=0)
        c = _taps(_shifted(ext, range(8 - (GDN_CONV - 1), 9)), w_ref[...], tr)
        c_ref[...] = c.astype(BF16)
        s = c * _sig(c)
        scale = GDN_D ** -0.5
        for j in range(2 * GDN_H):
            seg = s[:, j * GDN_D:(j + 1) * GDN_D]
            r = lax.rsqrt(_rowsum(seg * seg) + EPS)
            if j < GDN_H:
                r = r * scale
            qkv_ref[:, j * GDN_D:(j + 1) * GDN_D] = seg * r
        qkv_ref[:, 2 * D_MODEL:] = s[:, 2 * D_MODEL:]
        sm = s_ref[...]
        gp = gp_ref[...]
        lane = lax.broadcasted_iota(jnp.int32, sm.shape, 1)
        z = sm + gp[1:2, :]
        softplus = jnp.maximum(z, 0.0) + jnp.log(1.0 + jnp.exp(-jnp.abs(z)))
        lg = -jnp.exp(gp[0:1, :]) * softplus
        row = i * tr + lax.broadcasted_iota(jnp.int32, (tr, 1), 0)
        out = jnp.where(lane < GDN_H, lg, jnp.where(lane < 2 * GDN_H, _sig(sm), 0.0))
        gsm_ref[...] = jnp.where(row >= pad, out, 0.0)

    return pl.pallas_call(
        body, grid=(Lp // tr,),
        in_specs=[pl.BlockSpec((tr, W3), lambda i: (i, 0)), _halo_prev(tr, W3),
                  pl.BlockSpec((tr, LANES), lambda i: (i, 0)),
                  pl.BlockSpec((GDN_CONV, W3), lambda i: (0, 0)), pl.BlockSpec((8, LANES), lambda i: (0, 0))],
        out_specs=[pl.BlockSpec((tr, W3), lambda i: (i, 0)), pl.BlockSpec((tr, LANES), lambda i: (i, 0)),
                   pl.BlockSpec((tr, W3), lambda i: (i, 0))],
        out_shape=[jax.ShapeDtypeStruct((Lp, W3), F32), jax.ShapeDtypeStruct((Lp, LANES), F32),
                   jax.ShapeDtypeStruct((Lp, W3), BF16)],
        name="gdn_pre")(proj_m, proj_m, proj_s, conv_w, gparams)


def _gdn_pre_bwd(proj_m, conv_out, proj_s, conv_w, gparams, dq, dk, dv, dgs, pad):
    Lp = proj_m.shape[0]
    tr = _tile(Lp, 192, 64)
    W3 = 3 * D_MODEL
    te = tr + 8

    def body(main_ref, c_ref, cn_ref, s_ref, w_ref, gp_ref,
             dq_ref, dqn_ref, dk_ref, dkn_ref, dv_ref, dvn_ref, dgs_ref,
             da_ref, ds_ref, dw_ref, dgp_ref):
        i = pl.program_id(0)
        w = w_ref[...]
        c = jnp.concatenate([c_ref[...].astype(F32), cn_ref[...].astype(F32)[0:8]], axis=0)
        sg = _sig(c)
        s = c * sg
        rowe = i * tr + lax.broadcasted_iota(jnp.int32, (te, 1), 0)
        live = (rowe >= pad) & (rowe < Lp)
        dqe = jnp.concatenate([dq_ref[...], dqn_ref[...]], axis=0)
        dke = jnp.concatenate([dk_ref[...], dkn_ref[...]], axis=0)
        dve = jnp.concatenate([dv_ref[...], dvn_ref[...]], axis=0)
        scale = GDN_D ** -0.5
        parts = []
        for j in range(2 * GDN_H):
            seg = s[:, j * GDN_D:(j + 1) * GDN_D]
            r = lax.rsqrt(_rowsum(seg * seg) + EPS)
            xh = seg * r
            if j < GDN_H:
                dxh = dqe[:, j * GDN_D:(j + 1) * GDN_D] * scale
            else:
                dxh = dke[:, (j - GDN_H) * GDN_D:(j - GDN_H + 1) * GDN_D]
            parts.append(r * (dxh - xh * _rowsum(dxh * xh)))
        parts.append(dve)
        dsv = jnp.concatenate(parts, axis=1)
        dc = jnp.where(live, dsv * (sg * (1.0 + c * (1.0 - sg))), 0.0)
        dcs = _shifted(dc, range(GDN_CONV - 1, -1, -1))
        da_ref[...] = _taps(dcs, w, tr).astype(BF16)
        pm = main_ref[...]
        rows = [jnp.sum(dcs[kk][0:tr, :] * pm, axis=0, keepdims=True) for kk in range(GDN_CONV)]
        dwp = jnp.concatenate(rows + [jnp.zeros((8 - GDN_CONV, W3), F32)], axis=0)

        sm = s_ref[...]
        gp = gp_ref[...]
        lane = lax.broadcasted_iota(jnp.int32, sm.shape, 1)
        rowm = i * tr + lax.broadcasted_iota(jnp.int32, (tr, 1), 0)
        dgv = jnp.where(rowm >= pad, dgs_ref[...], 0.0)
        dlg = jnp.where(lane < GDN_H, dgv, 0.0)
        dbt = jnp.where((lane >= GDN_H) & (lane < 2 * GDN_H), dgv, 0.0)
        z = sm + gp[1:2, :]
        softplus = jnp.maximum(z, 0.0) + jnp.log(1.0 + jnp.exp(-jnp.abs(z)))
        ea = jnp.exp(gp[0:1, :])
        dz = dlg * (-ea) * _sig(z)
        dal = dlg * (-ea) * softplus
        bt = _sig(sm)
        dgb = dbt * bt * (1.0 - bt)
        ds_ref[...] = (dz + dgb).astype(BF16)
        gpp = jnp.concatenate([jnp.sum(dal, axis=0, keepdims=True), jnp.sum(dz, axis=0, keepdims=True),
                               jnp.zeros((6, LANES), F32)], axis=0)

        @pl.when(i == 0)
        def _():
            dw_ref[...] = dwp
            dgp_ref[...] = gpp

        @pl.when(i > 0)
        def _():
            dw_ref[...] += dwp
            dgp_ref[...] += gpp

    m3 = pl.BlockSpec((tr, W3), lambda i: (i, 0))
    m1 = pl.BlockSpec((tr, D_MODEL), lambda i: (i, 0))
    n1 = _halo_next(tr, D_MODEL, Lp)
    return pl.pallas_call(
        body, grid=(Lp // tr,),
        in_specs=[m3, m3, _halo_next(tr, W3, Lp, rows=16), pl.BlockSpec((tr, LANES), lambda i: (i, 0)),
                  pl.BlockSpec((GDN_CONV, W3), lambda i: (0, 0)), pl.BlockSpec((8, LANES), lambda i: (0, 0)),
                  m1, n1, m1, n1, m1, n1, pl.BlockSpec((tr, LANES), lambda i: (i, 0))],
        out_specs=[m3, pl.BlockSpec((tr, LANES), lambda i: (i, 0)),
                   pl.BlockSpec((8, W3), lambda i: (0, 0)), pl.BlockSpec((8, LANES), lambda i: (0, 0))],
        out_shape=[jax.ShapeDtypeStruct((Lp, W3), BF16), jax.ShapeDtypeStruct((Lp, LANES), BF16),
                   jax.ShapeDtypeStruct((8, W3), F32), jax.ShapeDtypeStruct((8, LANES), F32)],
        name="gdn_pre_bwd")(proj_m, conv_out, conv_out, proj_s, conv_w, gparams, dq, dq, dk, dk, dv, dv, dgs)


def _gdn_gates(gs):
    ri = lax.broadcasted_iota(jnp.int32, (CHUNK, CHUNK), 0)
    ci = lax.broadcasted_iota(jnp.int32, (CHUNK, CHUNK), 1)
    tril = ri >= ci
    strict = ri > ci
    gall = _dx(tril.astype(F32), gs)
    lane8 = lax.broadcasted_iota(jnp.int32, (8, LANES), 1)
    sub8 = lax.broadcasted_iota(jnp.int32, (8, LANES), 0)
    grow = _dxnt((lane8 == sub8).astype(F32), gall)
    return gall, grow, tril, strict


def _gdn_decay(gall, grow, tril, h):
    g = gall[:, h:h + 1]
    return g, jnp.where(tril, jnp.exp(jnp.where(tril, g - grow[h:h + 1, :], 0.0)), 0.0)


def _group(N):
    return 3 if N % 3 == 0 else (2 if N % 2 == 0 else 1)


def _gdn_chunk_specs(N, rev):
    G = _group(N)
    nb = N // G
    cn = (lambda n: nb - 1 - n) if rev else (lambda n: n)
    col = lambda j: pl.BlockSpec((G * CHUNK, D_MODEL), lambda n: (cn(n), j))
    gate = pl.BlockSpec((G * CHUNK, LANES), lambda n: (cn(n), 0))
    st = lambda a, b: pl.BlockSpec((GDN_H, G, a, b), lambda n: (0, cn(n), 0, 0))
    return G, nb, col, gate, st


def _gdn_chunk_fwd(qkv, gsm):
    Lp = qkv.shape[0]
    N = Lp // CHUNK
    G, nb, col, gate, st = _gdn_chunk_specs(N, False)

    def body(q_ref, k_ref, v_ref, gs_ref, o_ref, sin_ref, t_ref, S):
        n = pl.program_id(0)

        @pl.when(n == 0)
        def _():
            S[...] = jnp.zeros_like(S)

        ri = lax.broadcasted_iota(jnp.int32, (CHUNK, CHUNK), 0)
        ci = lax.broadcasted_iota(jnp.int32, (CHUNK, CHUNK), 1)
        eye = (ri == ci).astype(F32)
        heads = range(GDN_H)
        sls = [slice(h * GDN_D, (h + 1) * GDN_D) for h in heads]
        rows = [slice(c * CHUNK, (c + 1) * CHUNK) for c in range(G)]
        pairs = [(c, h) for c in range(G) for h in heads]
        P = lambda f: {p: f(*p) for p in pairs}
        gs = [gs_ref[rows[c], :] for c in range(G)]
        gates = [_gdn_gates(gs[c]) for c in range(G)]
        tril, strict = gates[0][2], gates[0][3]
        q = P(lambda c, h: q_ref[rows[c], sls[h]])
        k = P(lambda c, h: k_ref[rows[c], sls[h]])
        v = P(lambda c, h: v_ref[rows[c], sls[h]])
        beta = P(lambda c, h: gs[c][:, GDN_H + h:GDN_H + h + 1])
        gg = P(lambda c, h: _gdn_decay(gates[c][0], gates[c][1], tril, h))
        g = {p: x[0] for p, x in gg.items()}
        gam = {p: x[1] for p, x in gg.items()}
        eg = P(lambda c, h: jnp.exp(g[c, h]))
        gl = P(lambda c, h: g[c, h][CHUNK - 1:CHUNK, :])
        kb = P(lambda c, h: k[c, h] * beta[c, h])
        pw = P(lambda c, h: -jnp.where(strict, _dnt(kb[c, h], k[c, h]) * gam[c, h], 0.0))
        p = P(lambda c, h: _dnt(q[c, h], k[c, h]) * gam[c, h])
        t = P(lambda c, h: eye + pw[c, h])
        for it in range(5):
            mm = _d3g if it < 2 else (lambda a, b, dims: _d(a, b))
            pw = P(lambda c, h: mm(pw[c, h], pw[c, h], _NN))
            t = P(lambda c, h: t[c, h] + mm(t[c, h], pw[c, h], _NN))
        u = P(lambda c, h: _d(t[c, h], v[c, h] * beta[c, h]))
        w = P(lambda c, h: _d(t[c, h], kb[c, h] * eg[c, h]))
        qg = P(lambda c, h: q[c, h] * eg[c, h])
        kd = P(lambda c, h: k[c, h] * jnp.exp(gl[c, h] - g[c, h]))
        egl = P(lambda c, h: jnp.exp(gl[c, h]))
        for c in range(G):
            for h in heads:
                t_ref[h, c] = t[c, h]
        cur = [S[h] for h in heads]
        for c in range(G):
            vnew = [u[c, h] - _d(w[c, h], cur[h]) for h in heads]
            for h in heads:
                o_ref[rows[c], sls[h]] = _d(qg[c, h], cur[h]) + _d(p[c, h], vnew[h])
                sin_ref[h, c] = cur[h]
            cur = [cur[h] * egl[c, h] + _dtn(kd[c, h], vnew[h]) for h in heads]
        for h in heads:
            S[h] = cur[h]

    return pl.pallas_call(
        body, grid=(nb,),
        in_specs=[col(0), col(1), col(2), gate],
        out_specs=[col(0), st(GDN_D, GDN_D), st(CHUNK, CHUNK)],
        out_shape=[jax.ShapeDtypeStruct((Lp, D_MODEL), F32), jax.ShapeDtypeStruct((GDN_H, N, GDN_D, GDN_D), F32),
                   jax.ShapeDtypeStruct((GDN_H, N, CHUNK, CHUNK), F32)],
        scratch_shapes=[pltpu.VMEM((GDN_H, GDN_D, GDN_D), F32)],
        name="gdn_chunk_fwd")(qkv, qkv, qkv, gsm)


def _gdn_chunk_bwd(qkv, gsm, do, s_in, t_in):
    Lp = qkv.shape[0]
    N = Lp // CHUNK
    G, nb, col, gate, st = _gdn_chunk_specs(N, True)

    def body(q_ref, k_ref, v_ref, gs_ref, do_ref, sin_ref, t_ref, dq_ref, dk_ref, dv_ref, dgs_ref, dS):
        n = pl.program_id(0)

        @pl.when(n == 0)
        def _():
            dS[...] = jnp.zeros_like(dS)

        lane = lax.broadcasted_iota(jnp.int32, (CHUNK, LANES), 1)
        rcol = lax.broadcasted_iota(jnp.int32, (CHUNK, 1), 0)
        ri = lax.broadcasted_iota(jnp.int32, (CHUNK, CHUNK), 0)
        ci = lax.broadcasted_iota(jnp.int32, (CHUNK, CHUNK), 1)
        ones = jnp.ones((CHUNK, LANES), F32)
        heads = range(GDN_H)
        sls = [slice(h * GDN_D, (h + 1) * GDN_D) for h in heads]
        rows = [slice(c * CHUNK, (c + 1) * CHUNK) for c in range(G)]
        pairs = [(c, h) for c in range(G) for h in heads]
        P = lambda f: {p: f(*p) for p in pairs}
        gs = [gs_ref[rows[c], :] for c in range(G)]
        gates = [_gdn_gates(gs[c]) for c in range(G)]
        tril, strict = gates[0][2], gates[0][3]
        q = P(lambda c, h: q_ref[rows[c], sls[h]])
        k = P(lambda c, h: k_ref[rows[c], sls[h]])
        v = P(lambda c, h: v_ref[rows[c], sls[h]])
        dov = P(lambda c, h: do_ref[rows[c], sls[h]])
        s0 = P(lambda c, h: sin_ref[h, c])
        t = P(lambda c, h: t_ref[h, c])
        beta = P(lambda c, h: gs[c][:, GDN_H + h:GDN_H + h + 1])
        gg = P(lambda c, h: _gdn_decay(gates[c][0], gates[c][1], tril, h))
        g = {p: x[0] for p, x in gg.items()}
        gam = {p: x[1] for p, x in gg.items()}
        eg = P(lambda c, h: jnp.exp(g[c, h]))
        egl = P(lambda c, h: jnp.exp(g[c, h][CHUNK - 1:CHUNK, :]))
        e = P(lambda c, h: jnp.exp(g[c, h][CHUNK - 1:CHUNK, :] - g[c, h]))
        kb = P(lambda c, h: k[c, h] * beta[c, h])
        kbg = P(lambda c, h: kb[c, h] * eg[c, h])
        vb = P(lambda c, h: v[c, h] * beta[c, h])
        qg = P(lambda c, h: q[c, h] * eg[c, h])
        kd = P(lambda c, h: k[c, h] * e[c, h])
        m = P(lambda c, h: jnp.where(strict, _dnt(kb[c, h], k[c, h]) * gam[c, h], 0.0))
        u = P(lambda c, h: _d(t[c, h], vb[c, h]))
        w = P(lambda c, h: _d(t[c, h], kbg[c, h]))
        p = P(lambda c, h: _dnt(q[c, h], k[c, h]) * gam[c, h])
        dqg = P(lambda c, h: _dnt(dov[c, h], s0[c, h]))
        qgdo = P(lambda c, h: _dtn(qg[c, h], dov[c, h]))
        ptdo = P(lambda c, h: _dtn(p[c, h], dov[c, h]))
        vnew = P(lambda c, h: u[c, h] - _d(w[c, h], s0[c, h]))
        dp = P(lambda c, h: jnp.where(tril, _dnt(dov[c, h], vnew[c, h]), 0.0))
        cur = [dS[h] for h in heads]
        dvnew, dkd, sds = {}, {}, {}
        for c in reversed(range(G)):
            for h in heads:
                dvnew[c, h] = ptdo[c, h] + _d(kd[c, h], cur[h])
                dkd[c, h] = _dnt(vnew[c, h], cur[h])
                sds[c, h] = _allsum(s0[c, h] * cur[h])
            cur = [qgdo[c, h] + egl[c, h] * cur[h] - _dtn(w[c, h], dvnew[c, h]) for h in heads]
        for h in heads:
            dS[h] = cur[h]
        dw = P(lambda c, h: -_dnt(dvnew[c, h], s0[c, h]))
        dvb = P(lambda c, h: _dtn(t[c, h], dvnew[c, h]))
        dkbg = P(lambda c, h: _dtn(t[c, h], dw[c, h]))
        dt = P(lambda c, h: _dnt(dvnew[c, h], vb[c, h]) + _dnt(dw[c, h], kbg[c, h]))
        x1 = P(lambda c, h: _dtn(t[c, h], dt[c, h]))
        dm = P(lambda c, h: jnp.where(strict, -_dnt(x1[c, h], t[c, h]), 0.0))
        dkk = P(lambda c, h: dm[c, h] * gam[c, h])
        dqk = P(lambda c, h: dp[c, h] * gam[c, h])
        dkb = P(lambda c, h: _d(dkk[c, h], k[c, h]) + eg[c, h] * dkbg[c, h])
        em = P(lambda c, h: dm[c, h] * m[c, h] + dp[c, h] * p[c, h])
        colsum = P(lambda c, h: _d2x(em[c, h], ones, _TN)[:, 0:1])
        for c, h in pairs:
            dk_ref[rows[c], sls[h]] = (_dtn(dkk[c, h], kb[c, h]) + _dtn(dqk[c, h], q[c, h]) + dkd[c, h] * e[c, h]
                                       + beta[c, h] * dkb[c, h])
            dq_ref[rows[c], sls[h]] = _d(dqk[c, h], k[c, h]) + dqg[c, h] * eg[c, h]
            dv_ref[rows[c], sls[h]] = beta[c, h] * dvb[c, h]
        for c in range(G):
            dg_all = jnp.zeros((CHUNK, LANES), F32)
            dbeta_all = jnp.zeros((CHUNK, LANES), F32)
            for h in heads:
                dbeta = _rowsum(k[c, h] * dkb[c, h]) + _rowsum(v[c, h] * dvb[c, h])
                z = _rowsum(kd[c, h] * dkd[c, h])
                dg = (_rowsum(em[c, h]) - colsum[c, h] + _rowsum(qg[c, h] * dqg[c, h]) + _rowsum(kbg[c, h] * dkbg[c, h])
                      - z)
                extra = _allsum(z) + egl[c, h] * sds[c, h]
                dg = dg + jnp.where(rcol == CHUNK - 1, extra, 0.0)
                dg_all = dg_all + jnp.where(lane == h, dg, 0.0)
                dbeta_all = dbeta_all + jnp.where(lane == GDN_H + h, dbeta, 0.0)
            dgs_ref[rows[c], :] = _dx((ci >= ri).astype(F32), dg_all) + dbeta_all

    return pl.pallas_call(
        body, grid=(nb,),
        in_specs=[col(0), col(1), col(2), gate, col(0), st(GDN_D, GDN_D), st(CHUNK, CHUNK)],
        out_specs=[col(0), col(0), col(0), gate],
        out_shape=[jax.ShapeDtypeStruct((Lp, D_MODEL), F32)] * 3 + [jax.ShapeDtypeStruct((Lp, LANES), F32)],
        scratch_shapes=[pltpu.VMEM((GDN_H, GDN_D, GDN_D), F32)],
        name="gdn_chunk_bwd")(qkv, qkv, qkv, gsm, do, s_in, t_in)


def _rot(x, c, s):
    half = RET_D // 2
    x1 = x[:, :half]
    x2 = x[:, half:]
    return jnp.concatenate([x1 * c - x2 * s, x2 * c + x1 * s], axis=1)


def _rot_bwd(d, c, s):
    half = RET_D // 2
    d1 = d[:, :half]
    d2 = d[:, half:]
    return jnp.concatenate([d1 * c + d2 * s, d2 * c - d1 * s], axis=1)


def _ret_tables():
    hh = jnp.arange(RET_H, dtype=F32)
    lg = jnp.log(1.0 - 2.0 ** (-5.0 - hh))
    idx = jnp.arange(CHUNK, dtype=F32)
    tril = jnp.asarray(np.tril(np.ones((CHUNK, CHUNK), dtype=bool)))
    dmask = jnp.where(tril, jnp.exp((idx[:, None] - idx[None, :]) * lg[:, None, None]), 0.0)
    qdec = jnp.exp((idx[None, :] + 1.0) * lg[:, None])
    kdec = jnp.exp((CHUNK - 1.0 - idx[None, :]) * lg[:, None])
    gch = jnp.exp(CHUNK * lg)
    qdec = jnp.broadcast_to(qdec[:, :, None], (RET_H, CHUNK, RET_D))
    kdec = jnp.broadcast_to(kdec[:, :, None], (RET_H, CHUNK, RET_D))
    gch = jnp.broadcast_to(gch[:, None, None], (RET_H, 8, LANES))
    return dmask, qdec, kdec, gch


def _ret_specs(N, rev):
    G = _group(N)
    nb = N // G
    cn = (lambda n: nb - 1 - n) if rev else (lambda n: n)
    col = lambda j: pl.BlockSpec((G * CHUNK, D_MODEL), lambda n: (cn(n), j))
    tab = lambda a, b: pl.BlockSpec((RET_H, a, b), lambda n: (0, 0, 0))
    rope = pl.BlockSpec((G * CHUNK, LANES), lambda n: (cn(n), 0))
    st = pl.BlockSpec((RET_H, G, RET_D, RET_D), lambda n: (0, cn(n), 0, 0))
    return G, nb, col, tab, rope, st


def _ret_chunk_fwd(proj_m, cos, sin, tables):
    Lp = proj_m.shape[0]
    N = Lp // CHUNK
    dmask, qdec, kdec, gch = tables
    G, nb, col, tab, rope, st = _ret_specs(N, False)

    def body(q_ref, k_ref, v_ref, c_ref, s_ref, dm_ref, qd_ref, kd_ref, g_ref, o_ref, sin_ref, S):
        n = pl.program_id(0)

        @pl.when(n == 0)
        def _():
            S[...] = jnp.zeros_like(S)

        heads = range(RET_H)
        sls = [slice(h * RET_D, (h + 1) * RET_D) for h in heads]
        rows = [slice(c * CHUNK, (c + 1) * CHUNK) for c in range(G)]
        pairs = [(c, h) for c in range(G) for h in heads]
        P = lambda f: {p: f(*p) for p in pairs}
        qr = P(lambda c, h: _rot(q_ref[rows[c], sls[h]], c_ref[rows[c], :], s_ref[rows[c], :]))
        ks = P(lambda c, h: _rot(k_ref[rows[c], sls[h]], c_ref[rows[c], :], s_ref[rows[c], :]) * (RET_D ** -0.5))
        v = P(lambda c, h: v_ref[rows[c], sls[h]])
        a = P(lambda c, h: _dnt(qr[c, h], ks[c, h]) * dm_ref[h])
        av = P(lambda c, h: _d(a[c, h], v[c, h]))
        kv = P(lambda c, h: _dtn(ks[c, h] * kd_ref[h], v[c, h]))
        qd = P(lambda c, h: qr[c, h] * qd_ref[h])
        cur = [S[h] for h in heads]
        for c in range(G):
            for h in heads:
                o_ref[rows[c], sls[h]] = av[c, h] + _d(qd[c, h], cur[h])
                sin_ref[h, c] = cur[h].astype(BF16)
            cur = [cur[h] * g_ref[h, 0:1, 0:1] + kv[c, h] for h in heads]
        for h in heads:
            S[h] = cur[h]

    return pl.pallas_call(
        body, grid=(nb,),
        in_specs=[col(3), col(4), col(5), rope, rope,
                  tab(CHUNK, CHUNK), tab(CHUNK, RET_D), tab(CHUNK, RET_D), tab(8, LANES)],
        out_specs=[col(0), st],
        out_shape=[jax.ShapeDtypeStruct((Lp, D_MODEL), F32), jax.ShapeDtypeStruct((RET_H, N, RET_D, RET_D), BF16)],
        scratch_shapes=[pltpu.VMEM((RET_H, RET_D, RET_D), F32)],
        name="ret_chunk_fwd")(proj_m, proj_m, proj_m, cos, sin, dmask, qdec, kdec, gch)


def _ret_chunk_bwd(proj_m, cos, sin, tables, do, s_in):
    Lp = proj_m.shape[0]
    N = Lp // CHUNK
    dmask, qdec, kdec, gch = tables
    G, nb, col, tab, rope, st = _ret_specs(N, True)

    def body(q_ref, k_ref, v_ref, c_ref, s_ref, dm_ref, qd_ref, kd_ref, g_ref, do_ref, sin_ref,
             d_ref, dS):
        n = pl.program_id(0)

        @pl.when(n == 0)
        def _():
            dS[...] = jnp.zeros_like(dS)

        kscale = RET_D ** -0.5
        heads = range(RET_H)
        sls = [slice(h * RET_D, (h + 1) * RET_D) for h in heads]
        rows = [slice(c * CHUNK, (c + 1) * CHUNK) for c in range(G)]
        pairs = [(c, h) for c in range(G) for h in heads]
        P = lambda f: {p: f(*p) for p in pairs}
        cs = [(c_ref[rows[c], :], s_ref[rows[c], :]) for c in range(G)]
        osl = lambda part, h: slice(part * D_MODEL + h * RET_D, part * D_MODEL + (h + 1) * RET_D)
        qr = P(lambda c, h: _rot(q_ref[rows[c], sls[h]], *cs[c]))
        ks = P(lambda c, h: _rot(k_ref[rows[c], sls[h]], *cs[c]) * kscale)
        v = P(lambda c, h: v_ref[rows[c], sls[h]])
        dov = P(lambda c, h: do_ref[rows[c], sls[h]])
        ad = P(lambda c, h: _dnt(qr[c, h], ks[c, h]) * dm_ref[h])
        da = P(lambda c, h: _dnt(dov[c, h], v[c, h]) * dm_ref[h])
        dos = P(lambda c, h: _dnt(dov[c, h], sin_ref[h, c]) * qd_ref[h])
        qdo = P(lambda c, h: _dtn(qr[c, h] * qd_ref[h], dov[c, h]))
        adv = P(lambda c, h: _dtn(ad[c, h], dov[c, h]))
        dqr = P(lambda c, h: _d(da[c, h], ks[c, h]) + dos[c, h])
        daq = P(lambda c, h: _dtn(da[c, h], qr[c, h]))
        kk = P(lambda c, h: ks[c, h] * kd_ref[h])
        cur = [dS[h] for h in heads]
        for c in reversed(range(G)):
            for h in heads:
                d_ref[rows[c], osl(2, h)] = (adv[c, h] + _d(kk[c, h], cur[h])).astype(BF16)
                d_ref[rows[c], osl(0, h)] = _rot_bwd(dqr[c, h], *cs[c]).astype(BF16)
                dks = daq[c, h] + _dnt(v[c, h], cur[h]) * kd_ref[h]
                d_ref[rows[c], osl(1, h)] = _rot_bwd(dks * kscale, *cs[c]).astype(BF16)
            cur = [cur[h] * g_ref[h, 0:1, 0:1] + qdo[c, h] for h in heads]
        for h in heads:
            dS[h] = cur[h]

    return pl.pallas_call(
        body, grid=(nb,),
        in_specs=[col(3), col(4), col(5), rope, rope,
                  tab(CHUNK, CHUNK), tab(CHUNK, RET_D), tab(CHUNK, RET_D), tab(8, LANES), col(0), st],
        out_specs=pl.BlockSpec((G * CHUNK, 3 * D_MODEL), lambda n: (nb - 1 - n, 0)),
        out_shape=jax.ShapeDtypeStruct((Lp, 3 * D_MODEL), BF16),
        scratch_shapes=[pltpu.VMEM((RET_H, RET_D, RET_D), F32)],
        name="ret_chunk_bwd")(proj_m, proj_m, proj_m, cos, sin, dmask, qdec, kdec, gch, do, s_in)


def _merge_specs(tr):
    col = lambda j: pl.BlockSpec((tr, D_MODEL), lambda i: (i, j))
    return col


def _merge_fwd(o_a, o_b, proj_m, gnorm):
    Lp = o_a.shape[0]
    tr = _tile(Lp, 192, 16)

    def body(oa_ref, ob_ref, gz_ref, rg_ref, ga_ref, gb_ref, gn_ref, y_ref):
        gn = gn_ref[...]
        oa = oa_ref[...]
        ob = ob_ref[...]
        gz = gz_ref[...]
        ya = []
        for j in range(GDN_H):
            seg = oa[:, j * GDN_D:(j + 1) * GDN_D]
            r = lax.rsqrt(jnp.mean(seg * seg, axis=-1, keepdims=True) + EPS)
            ya.append(seg * r * gn)
        ya = jnp.concatenate(ya, axis=1) * (gz * _sig(gz))
        yb = []
        for j in range(RET_H):
            seg = ob[:, j * RET_D:(j + 1) * RET_D]
            r = lax.rsqrt(jnp.mean(seg * seg, axis=-1, keepdims=True) + EPS)
            yb.append(seg * r)
        rg = rg_ref[...]
        yb = jnp.concatenate(yb, axis=1) * (rg * _sig(rg))
        y_ref[...] = (_sig(ga_ref[...]) * ya + _sig(gb_ref[...]) * yb).astype(BF16)

    col = _merge_specs(tr)
    return pl.pallas_call(
        body, grid=(Lp // tr,),
        in_specs=[col(0), col(0), col(6), col(7), col(8), col(9), pl.BlockSpec((1, GDN_D), lambda i: (0, 0))],
        out_specs=col(0), out_shape=jax.ShapeDtypeStruct((Lp, D_MODEL), BF16),
        name="merge_fwd")(o_a, o_b, proj_m, proj_m, proj_m, proj_m, gnorm)


def _merge_bwd(dh1b, w_out, o_a, o_b, proj_m, gnorm):
    Lp = o_a.shape[0]
    tr = _tile(Lp, 192, 16)

    def body(d_ref, wo_ref, oa_ref, ob_ref, gz_ref, rg_ref, ga_ref, gb_ref, gn_ref, dc_ref, doa_ref, dob_ref, dgn_ref):
        i = pl.program_id(0)
        gn = gn_ref[...]
        dyv = lax.dot_general(d_ref[...], wo_ref[...], _NT, preferred_element_type=F32)
        oa = oa_ref[...]
        ob = ob_ref[...]
        gz = gz_ref[...]
        rg = rg_ref[...]
        sa = _sig(ga_ref[...])
        sb = _sig(gb_ref[...])
        dya = dyv * sa
        dyb = dyv * sb
        sgz = _sig(gz)
        szz = gz * sgz
        dgn = jnp.zeros((1, GDN_D), F32)
        ya = []
        dgz = []
        for j in range(GDN_H):
            sl = slice(j * GDN_D, (j + 1) * GDN_D)
            seg = oa[:, sl]
            r = lax.rsqrt(jnp.mean(seg * seg, axis=-1, keepdims=True) + EPS)
            xh = seg * r
            oan = xh * gn
            ya.append(oan * szz[:, sl])
            dgz.append(dya[:, sl] * oan * (sgz[:, sl] * (1.0 + gz[:, sl] * (1.0 - sgz[:, sl]))))
            doan = dya[:, sl] * szz[:, sl]
            dgn = dgn + jnp.sum(doan * xh, axis=0, keepdims=True)
            dxh = doan * gn
            doa_ref[:, sl] = r * (dxh - xh * jnp.mean(dxh * xh, axis=-1, keepdims=True))
        ya = jnp.concatenate(ya, axis=1)
        srg = _sig(rg)
        srr = rg * srg
        yb = []
        drg = []
        for j in range(RET_H):
            sl = slice(j * RET_D, (j + 1) * RET_D)
            seg = ob[:, sl]
            r = lax.rsqrt(jnp.mean(seg * seg, axis=-1, keepdims=True) + EPS)
            xh = seg * r
            yb.append(xh * srr[:, sl])
            drg.append(dyb[:, sl] * xh * (srg[:, sl] * (1.0 + rg[:, sl] * (1.0 - srg[:, sl]))))
            dxh = dyb[:, sl] * srr[:, sl]
            dob_ref[:, sl] = r * (dxh - xh * jnp.mean(dxh * xh, axis=-1, keepdims=True))
        yb = jnp.concatenate(yb, axis=1)
        dc_ref[:, 0:D_MODEL] = jnp.concatenate(dgz, axis=1).astype(BF16)
        dc_ref[:, D_MODEL:2 * D_MODEL] = jnp.concatenate(drg, axis=1).astype(BF16)
        dc_ref[:, 2 * D_MODEL:3 * D_MODEL] = (dyv * ya * sa * (1.0 - sa)).astype(BF16)
        dc_ref[:, 3 * D_MODEL:] = (dyv * yb * sb * (1.0 - sb)).astype(BF16)

        @pl.when(i == 0)
        def _():
            dgn_ref[...] = dgn

        @pl.when(i > 0)
        def _():
            dgn_ref[...] += dgn

    col = _merge_specs(tr)
    return pl.pallas_call(
        body, grid=(Lp // tr,),
        in_specs=[col(0), pl.BlockSpec((D_MODEL, D_MODEL), lambda i: (0, 0), pipeline_mode=pl.Buffered(1)),
                  col(0), col(0), col(6), col(7), col(8), col(9), pl.BlockSpec((1, GDN_D), lambda i: (0, 0))],
        out_specs=[pl.BlockSpec((tr, 4 * D_MODEL), lambda i: (i, 0)), col(0), col(0),
                   pl.BlockSpec((1, GDN_D), lambda i: (0, 0))],
        out_shape=[jax.ShapeDtypeStruct((Lp, 4 * D_MODEL), BF16), jax.ShapeDtypeStruct((Lp, D_MODEL), F32),
                   jax.ShapeDtypeStruct((Lp, D_MODEL), F32), jax.ShapeDtypeStruct((1, GDN_D), F32)],
        name="merge_bwd")(dh1b, w_out, o_a, o_b, proj_m, proj_m, proj_m, proj_m, gnorm)


def _ffn_act(up, conv_w, conv_b):
    Lp = up.shape[0]
    tr = _tile(Lp, 192, 16)
    W2 = 2 * D_FF

    def body(main_ref, prev_ref, w_ref, b_ref, act_ref, u_ref):
        i = pl.program_id(0)
        prev = jnp.where(i > 0, prev_ref[...], 0.0)
        ext = jnp.concatenate([prev, main_ref[...]], axis=0)
        u = _taps(_shifted(ext, range(8 - (FFN_CONV - 1), 9)), w_ref[...], tr, b_ref[...])
        a = u[:, :D_FF]
        act_ref[...] = (a * _sig(a) * u[:, D_FF:]).astype(BF16)
        u_ref[...] = u.astype(BF16)

    return pl.pallas_call(
        body, grid=(Lp // tr,),
        in_specs=[pl.BlockSpec((tr, W2), lambda i: (i, 0)), _halo_prev(tr, W2),
                  pl.BlockSpec((FFN_CONV, W2), lambda i: (0, 0)), pl.BlockSpec((1, W2), lambda i: (0, 0))],
        out_specs=[pl.BlockSpec((tr, D_FF), lambda i: (i, 0)), pl.BlockSpec((tr, W2), lambda i: (i, 0))],
        out_shape=[jax.ShapeDtypeStruct((Lp, D_FF), BF16), jax.ShapeDtypeStruct((Lp, W2), BF16)],
        name="ffn_act")(up, up, conv_w, conv_b)


def _ffn_act_bwd(up, u, dact, conv_w):
    Lp = up.shape[0]
    tr = _tile(Lp, 192, 16)
    W2 = 2 * D_FF
    te = tr + 8

    def body(up_ref, u_ref, un_ref, da_ref, dan_ref, w_ref, dup_ref, acc_ref):
        i = pl.program_id(0)
        w = w_ref[...]
        ue = jnp.concatenate([u_ref[...].astype(F32), un_ref[...].astype(F32)[0:8]], axis=0)
        a = ue[:, :D_FF]
        b = ue[:, D_FF:]
        rowe = i * tr + lax.broadcasted_iota(jnp.int32, (te, 1), 0)
        dae = jnp.where(rowe < Lp, jnp.concatenate([da_ref[...], dan_ref[...]], axis=0), 0.0)
        sg = _sig(a)
        du = jnp.concatenate([dae * b * (sg * (1.0 + a * (1.0 - sg))), dae * (a * sg)], axis=1)
        dus = _shifted(du, range(FFN_CONV - 1, -1, -1))
        dup_ref[...] = _taps(dus, w, tr).astype(BF16)
        upm = up_ref[...]
        rows = [jnp.sum(dus[kk][0:tr, :] * upm, axis=0, keepdims=True) for kk in range(FFN_CONV)]
        rows.append(jnp.sum(du[0:tr, :], axis=0, keepdims=True))
        part = jnp.concatenate(rows + [jnp.zeros((8 - len(rows), W2), F32)], axis=0)

        @pl.when(i == 0)
        def _():
            acc_ref[...] = part

        @pl.when(i > 0)
        def _():
            acc_ref[...] += part

    return pl.pallas_call(
        body, grid=(Lp // tr,),
        in_specs=[pl.BlockSpec((tr, W2), lambda i: (i, 0)), pl.BlockSpec((tr, W2), lambda i: (i, 0)),
                  _halo_next(tr, W2, Lp, rows=16), pl.BlockSpec((tr, D_FF), lambda i: (i, 0)), _halo_next(tr, D_FF, Lp),
                  pl.BlockSpec((FFN_CONV, W2), lambda i: (0, 0))],
        out_specs=[pl.BlockSpec((tr, W2), lambda i: (i, 0)), pl.BlockSpec((8, W2), lambda i: (0, 0))],
        out_shape=[jax.ShapeDtypeStruct((Lp, W2), BF16), jax.ShapeDtypeStruct((8, W2), F32)],
        name="ffn_act_bwd")(up, u, u, dact, dact, conv_w)


def _proj_rows(j):
    shift = (jnp.where((j >= 3) & (j < 6), _O_RQ - 3 * D_MODEL, 0) + jnp.where(j == 6, _O_GZ - 6 * D_MODEL, 0)
             + jnp.where(j >= 7, _O_RG - 7 * D_MODEL, 0))
    return j * D_MODEL + shift


def _local_step(hpad, tgt, pad, wt, first_weights=None, late_weights=None, on_ffn_out_grads=None,
                on_w_in_grads=None):
    Lp = hpad.shape[0]
    first = pad + N_META
    pos = jnp.arange(Lp, dtype=F32) - float(pad)
    half = RET_D // 2
    inv = 1.0 / (ROPE_BASE ** (jnp.arange(half, dtype=F32) / half))
    ang = pos[:, None] * inv[None, :]
    cos, sin = jnp.cos(ang), jnp.sin(ang)
    tables = _ret_tables()
    gparams = jnp.zeros((8, LANES), F32).at[0, :GDN_H].set(wt["a_log"]).at[1, :GDN_H].set(wt["dt_bias"])

    hn1 = _rms_fwd(hpad, wt["norm1"], "rms1_fwd")
    if first_weights is not None:
        wt = {**wt, **first_weights(hn1)}
    w_in_t = wt["w_in_t"]
    w_small_t = jnp.pad(w_in_t[_O_GA:_O_RQ], ((0, LANES - 2 * GDN_H), (0, 0)))
    proj_m = _mm_nn(hn1, w_in_t, bt=True, tm_target=2752, b_rows=(D_MODEL, MAIN_W // D_MODEL, _proj_rows),
                    name="proj_main")
    proj_s = _mm_nn(hn1, w_small_t, bt=True, name="proj_small")
    qkv, gsm, conv_out = _gdn_pre(proj_m, proj_s, wt["gdn_conv_w"], gparams, pad)
    o_a, s_a, t_a = _gdn_chunk_fwd(qkv, gsm)
    o_b, s_b = _ret_chunk_fwd(proj_m, cos, sin, tables)
    y = _merge_fwd(o_a, o_b, proj_m, wt["gdn_norm"])
    if late_weights is not None:
        wt = {**wt, **late_weights(y)}
    h1, hn2 = _mm_rms_fwd(_Producer(y, wt["w_out"], hpad), wt["norm2"], "out_proj_rms2")
    up = _mm_nn(hn2, wt["w_up_t"], bt=True, name="ffn_up")
    act, u_ffn = _ffn_act(up, wt["ffn_conv_w"], wt["ffn_conv_b"])
    lossvec, dh2, dh2b, d_norm_f = _final(_Producer(act, wt["w_down"], h1), wt["norm_f"], tgt, first)

    d_w_down = _mm_tn(act, dh2b, name="dw_down")
    dact = _mm_nt(dh2b, wt["w_down"], name="d_act")
    dup, ffn_rows = _ffn_act_bwd(up, u_ffn, dact, wt["ffn_conv_w"])
    d_w_up_t = _mm_tn(dup, hn2, name="dw_up")
    dh1, dh1b, d_norm2 = _rms_bwd(h1, wt["norm2"], _Producer(dup, wt["w_up_t"]), dh2, pad, "d_hn2_rms2_bwd")

    d_w_out = _mm_tn(y, dh1b, name="dw_out")
    gnorm = wt["gdn_norm"]
    if on_ffn_out_grads is not None:
        gnorm = gnorm + on_ffn_out_grads(d_w_down, d_w_up_t, d_w_out)[0:1, :]
    d_c, do_a, do_b, d_gnorm = _merge_bwd(dh1b, wt["w_out"], o_a, o_b, proj_m, gnorm)
    d_r = _ret_chunk_bwd(proj_m, cos, sin, tables, do_b, s_b)
    dq, dk, dv, dgs = _gdn_chunk_bwd(qkv, gsm, do_a, s_a, t_a)
    d_a, d_s, conv_rows, gp_rows = _gdn_pre_bwd(proj_m, conv_out, proj_s, wt["gdn_conv_w"], gparams, dq, dk, dv, dgs,
                                                pad)

    segs = [(d_a, w_in_t[_O_GQ:_O_GZ]), (d_r, w_in_t[_O_RQ:_O_RG]),
            (d_c, jnp.concatenate([w_in_t[_O_GZ:_O_GA], w_in_t[_O_RG:_O_END]], axis=0))]
    pa, pr, pc = [_mm_tn(d, hn1, BF16, name="dw_in_%d" % i) for i, (d, _) in enumerate(segs)]
    ps = _mm_tn(d_s, hn1, BF16, name="dw_in_small")
    d_w_in_t = jnp.concatenate([pa, pc[:D_MODEL], ps[:2 * GDN_H], pr, pc[D_MODEL:]], axis=0)
    if on_w_in_grads is not None:
        w_small_t = w_small_t + on_w_in_grads(d_w_in_t)[0:1, 0:1].astype(w_small_t.dtype)
    dhn1 = _mm_sum([(d_s, w_small_t)] + segs[:-1], "d_hn1_first")
    dh0, _, d_norm1 = _rms_bwd(hpad, wt["norm1"], _Producer(*segs[-1], dhn1), dh1, pad, "d_hn1_rms1_bwd")

    grads = {
        "norm1": d_norm1, "w_in_t": d_w_in_t, "gdn_conv_w": conv_rows[:GDN_CONV],
        "a_log": gp_rows[0, :GDN_H], "dt_bias": gp_rows[1, :GDN_H], "gdn_norm": d_gnorm, "w_out": d_w_out,
        "norm2": d_norm2, "w_up_t": d_w_up_t, "ffn_conv_w": ffn_rows[:FFN_CONV],
        "ffn_conv_b": ffn_rows[FFN_CONV:FFN_CONV + 1], "w_down": d_w_down, "norm_f": d_norm_f,
    }
    return lossvec, dh0, grads


def _peer(k):
    ix, iy, ic = lax.axis_index("x"), lax.axis_index("y"), lax.axis_index("c")
    px = 1 - ix if (k >> 2) & 1 else ix
    py = 1 - iy if (k >> 1) & 1 else iy
    pc = 1 - ic if k & 1 else ic
    return (px, py, pc), 4 * px + 2 * py + pc


def _comm_call(body, n, out_shapes, name, args):
    hbm = pl.BlockSpec(memory_space=pl.ANY)
    return pl.pallas_call(
        body, out_shape=out_shapes, in_specs=[hbm] * n, out_specs=[hbm] * n,
        scratch_shapes=[pltpu.SemaphoreType.DMA((n, N_DEV - 1)), pltpu.SemaphoreType.DMA((n, N_DEV - 1)),
                        pltpu.SemaphoreType.DMA((n,))],
        name=name)(*args)


def _all_gather(xs, name):
    n = len(xs)

    def body(*refs):
        x_refs, out_refs = refs[:n], refs[n:2 * n]
        send_sems, recv_sems, local_sems = refs[2 * n:]
        _, me = _peer(0)
        pending = []
        for i in range(n):
            local = pltpu.make_async_copy(x_refs[i], out_refs[i].at[me], local_sems.at[i])
            local.start()
            pending.append(local)
        sends = []
        for i in range(n):
            for k in range(1, N_DEV):
                dev, _ = _peer(k)
                cp = pltpu.make_async_remote_copy(
                    src_ref=x_refs[i], dst_ref=out_refs[i].at[me], send_sem=send_sems.at[i, k - 1],
                    recv_sem=recv_sems.at[i, k - 1], device_id=dev, device_id_type=MESH_T)
                cp.start()
                sends.append(cp)
        for i in range(n):
            for k in range(1, N_DEV):
                dev, idx = _peer(k)
                pltpu.make_async_remote_copy(
                    src_ref=x_refs[i], dst_ref=out_refs[i].at[idx], send_sem=send_sems.at[i, k - 1],
                    recv_sem=recv_sems.at[i, k - 1], device_id=dev, device_id_type=MESH_T).wait_recv()
        for cp in sends:
            cp.wait_send()
        for local in pending:
            local.wait()

    out_shapes = [jax.ShapeDtypeStruct((N_DEV,) + a.shape, a.dtype) for a in xs]
    return _comm_call(body, n, out_shapes, name, xs)


def _all_to_all(gs, name):
    n = len(gs)

    def body(*refs):
        g_refs, out_refs = refs[:n], refs[n:2 * n]
        send_sems, recv_sems, local_sems = refs[2 * n:]
        _, me = _peer(0)
        pending = []
        for i in range(n):
            local = pltpu.make_async_copy(g_refs[i].at[me], out_refs[i].at[0], local_sems.at[i])
            local.start()
            pending.append(local)
        sends = []
        for i in range(n):
            for k in range(1, N_DEV):
                dev, idx = _peer(k)
                cp = pltpu.make_async_remote_copy(
                    src_ref=g_refs[i].at[idx], dst_ref=out_refs[i].at[k], send_sem=send_sems.at[i, k - 1],
                    recv_sem=recv_sems.at[i, k - 1], device_id=dev, device_id_type=MESH_T)
                cp.start()
                sends.append(cp)
        for cp in sends:
            cp.wait_recv()
        for cp in sends:
            cp.wait_send()
        for local in pending:
            local.wait()

    out_shapes = [jax.ShapeDtypeStruct(g.shape, g.dtype) for g in gs]
    return _comm_call(body, n, out_shapes, name, gs)


_SPLIT_RELATIONS = {"gather": tuple(range(1, N_DEV)), "a2a": tuple(range(1, N_DEV)), "chip": (1, 2, 4, 6),
                    "forward": (2, 4, 6)}


def _split_copies(kind, src_refs, land_refs, send_sems, recv_sems, local_sems, with_recv):
    n = len(land_refs)
    rels = _SPLIT_RELATIONS[kind]
    _, me = _peer(0)
    locals_, remotes = [], []
    for i in range(n):
        if kind in ("gather", "chip"):
            locals_.append(pltpu.make_async_copy(src_refs[i], land_refs[i].at[me], local_sems.at[i]))
        elif kind == "a2a":
            locals_.append(pltpu.make_async_copy(src_refs[i].at[me], land_refs[i].at[0], local_sems.at[i]))
        for jj, k in enumerate(rels):
            dev, idx = _peer(k)
            if kind in ("gather", "chip"):
                src, dst, mine = src_refs[i], land_refs[i].at[me], land_refs[i].at[idx]
            elif kind == "a2a":
                src, dst, mine = src_refs[i].at[idx], land_refs[i].at[k], land_refs[i].at[k]
            else:
                dev, _ = _peer(1)
                _, came = _peer(k + 1)
                src, dst, mine = land_refs[i].at[idx], land_refs[i].at[idx], land_refs[i].at[came]
            j = i * len(rels) + jj
            send = pltpu.make_async_remote_copy(
                src_ref=src, dst_ref=dst, send_sem=send_sems.at[j], recv_sem=recv_sems.at[j],
                device_id=dev, device_id_type=MESH_T)
            recv = pltpu.make_async_remote_copy(
                src_ref=src, dst_ref=mine, send_sem=send_sems.at[j], recv_sem=recv_sems.at[j],
                device_id=dev, device_id_type=MESH_T) if with_recv else None
            remotes.append((send, recv))
    return locals_, remotes


_HBM = pl.BlockSpec(memory_space=pltpu.HBM)
_SEM = pl.BlockSpec(memory_space=pltpu.SEMAPHORE)
_ANY = pl.BlockSpec(memory_space=pl.ANY)


def _split_start(srcs, kind, name, after):
    n = len(srcs)
    if kind == "forward":
        arrays = list(srcs)
    else:
        gathers = kind in ("gather", "chip")
        arrays = list(srcs) + [lax.empty(((N_DEV,) + a.shape) if gathers else a.shape, a.dtype) for a in srcs]
    na = len(arrays)

    def body(*refs):
        src_refs, land_refs = refs[:n], refs[na - n:na]
        send_sems, recv_sems, local_sems = refs[na + 1:na + 4]
        token = refs[-1]
        locals_, remotes = _split_copies(kind, src_refs, land_refs, send_sems, recv_sems, local_sems, False)
        for cp in locals_:
            cp.start()
        for send, _ in remotes:
            send.start()
        token[...] = jnp.zeros_like(token)

    ncp = n * len(_SPLIT_RELATIONS[kind])
    sems = (pltpu.SemaphoreType.DMA((ncp,)), pltpu.SemaphoreType.DMA((ncp,)), pltpu.SemaphoreType.DMA((n,)))
    thru = tuple(pltpu.HBM(a.shape, a.dtype) for a in arrays)
    outs = pl.pallas_call(
        body, name=name,
        out_shape=sems + thru + (jax.ShapeDtypeStruct((8, LANES), F32),),
        in_specs=[_HBM] * na + [_ANY],
        out_specs=[_SEM] * 3 + [_HBM] * na + [pl.BlockSpec(memory_space=pltpu.VMEM)],
        input_output_aliases={i: 3 + i for i in range(na)},
        compiler_params=pltpu.CompilerParams(has_side_effects=pltpu.SideEffectType.DATAFLOW_SIDE_EFFECTING),
    )(*[pltpu.with_memory_space_constraint(a, pltpu.HBM) for a in arrays], after)
    return (kind, n, outs[:3], outs[3:3 + na]), outs[-1]


def _split_wait(handle, name, after):
    kind, n, sems, thru = handle
    na = len(thru)

    def body(*refs):
        src_refs, land_refs = refs[:n], refs[na - n:na]
        send_sems, recv_sems, local_sems = refs[na:na + 3]
        locals_, remotes = _split_copies(kind, src_refs, land_refs, send_sems, recv_sems, local_sems, True)
        for send, recv in remotes:
            send.wait_send()
            recv.wait_recv()
        for cp in locals_:
            cp.wait()

    outs = pl.pallas_call(
        body, name=name, out_shape=tuple(pltpu.HBM(a.shape, a.dtype) for a in thru),
        in_specs=[_HBM] * na + [_SEM] * 3 + [_ANY], out_specs=[_HBM] * na,
        input_output_aliases={i: i for i in range(na)},
        compiler_params=pltpu.CompilerParams(has_side_effects=pltpu.SideEffectType.DATAFLOW_SIDE_EFFECTING),
    )(*thru, *sems, after)
    return list(outs[na - n:])


def _adamw(gslabs, w, m, v, name):
    R, Cw = w.shape
    if R % 8 == 0:
        tr, tc = _tile(R, 64 if Cw > 1024 else 128, 8), Cw
    else:
        tr, tc = R, LANES
    c1 = 1.0 - ADAM_B1 ** ADAM_STEP
    c2 = 1.0 - ADAM_B2 ** ADAM_STEP

    def body(g_ref, w_ref, m_ref, v_ref, go_ref, d_ref, mo_ref, vo_ref):
        g = g_ref[0].astype(F32)
        for k in range(1, N_DEV):
            g = g + g_ref[k].astype(F32)
        mn = ADAM_B1 * m_ref[...] + (1.0 - ADAM_B1) * g
        vn = ADAM_B2 * v_ref[...] + (1.0 - ADAM_B2) * (g * g)
        m_hat = mn / c1
        v_hat = vn / c2
        go_ref[...] = g
        d_ref[...] = -ADAM_LR * (m_hat / (jnp.sqrt(v_hat) + ADAM_EPS) + ADAM_WD * w_ref[...])
        mo_ref[...] = mn
        vo_ref[...] = vn

    blk = pl.BlockSpec((tr, tc), lambda i, j: (i, j))
    return pl.pallas_call(
        body, grid=(R // tr, Cw // tc),
        in_specs=[pl.BlockSpec((N_DEV, tr, tc), lambda i, j: (0, i, j)), blk, blk, blk],
        out_specs=[blk] * 4, out_shape=[jax.ShapeDtypeStruct((R, Cw), F32)] * 4, name=name)(gslabs, w, m, v)


def _pack(arrs, row_mult, dtype=F32):
    parts = []
    total = 0
    for a in arrs:
        f = a.reshape(-1).astype(dtype)
        n = -(-f.shape[0] // 1024) * 1024
        parts.append(jnp.pad(f, (0, n - f.shape[0])))
        total += n
    rows = total // LANES
    rows_p = -(-rows // row_mult) * row_mult
    flat = jnp.concatenate(parts)
    flat = jnp.pad(flat, (0, rows_p * LANES - total))
    return flat.reshape(rows_p, LANES)


def _unpack(packed, shapes):
    lead = packed.shape[:-2]
    flat = packed.reshape(lead + (-1,))
    out = []
    off = 0
    for s in shapes:
        n = int(np.prod(s))
        out.append(flat[..., off:off + n].reshape(lead + tuple(s)))
        off += -(-n // 1024) * 1024
    return out


def _gather_cols(stacked):
    d, r, c = stacked.shape
    return stacked.transpose(1, 0, 2).reshape(r, d * c)


def _scatter_cols(full):
    r, n = full.shape
    return full.reshape(r, N_DEV, n // N_DEV).transpose(1, 0, 2)


def kernel(x, meta, norm1, w_in, gdn_conv_w, gdn_a_log, gdn_dt_bias, gdn_norm, w_out, norm2, w_ffn_up, ffn_conv_w, ffn_conv_b, w_ffn_down, norm_f, loss_target, m_meta, m_norm1, m_w_in, m_gdn_conv_w, m_gdn_a_log, m_gdn_dt_bias, m_gdn_norm, m_w_out, m_norm2, m_w_ffn_up, m_ffn_conv_w, m_ffn_conv_b, m_w_ffn_down, m_norm_f, v_meta, v_norm1, v_w_in, v_gdn_conv_w, v_gdn_a_log, v_gdn_dt_bias, v_gdn_norm, v_w_out, v_norm2, v_w_ffn_up, v_ffn_conv_w, v_ffn_conv_b, v_w_ffn_down, v_norm_f):
    S = x.shape[1]
    L = N_META + S
    pad = (-L) % CHUNK
    Lp = L + pad

    tr_ = lambda a: jnp.swapaxes(a[0], 0, 1)
    big = [tr_(w_in), w_out[0], tr_(w_ffn_up), w_ffn_down[0]]
    small = [meta, gdn_conv_w, ffn_conv_w]
    small_all, = _all_gather([_pack(small, 8)], "gather_small_weights")
    first, first_token = _split_start([big[0].astype(BF16)], "chip", "gather_w_in_start", small_all)
    late, late_token = _split_start([a.astype(BF16) for a in big[1:]], "gather", "gather_late_start", first_token)

    def first_weights(after):
        half = _split_wait(first, "gather_w_in_wait", after)
        second, second_token = _split_start(half, "forward", "gather_w_in_forward_start", after)
        w_in_s, = _split_wait(second, "gather_w_in_forward_wait", second_token)
        return {"w_in_t": w_in_s.reshape(_O_END, D_MODEL)}

    def late_weights(after):
        w_out_s, w_up_s, w_down_s = _split_wait(late, "gather_late_wait", after)
        return {"w_out": w_out_s.reshape(D_MODEL, D_MODEL), "w_up_t": w_up_s.reshape(2 * D_FF, D_MODEL),
                "w_down": w_down_s.reshape(D_FF, D_MODEL)}

    meta_s, gconv_s, fconv_s = _unpack(small_all, [a.shape for a in small])
    wt = {
        "norm1": norm1 + jnp.tile(late_token[0:1, :], (1, D_MODEL // LANES)),
        "gdn_conv_w": _gather_cols(gconv_s[:, 0]), "a_log": gdn_a_log[0], "dt_bias": gdn_dt_bias[0],
        "gdn_norm": gdn_norm, "norm2": norm2, "ffn_conv_w": _gather_cols(fconv_s[:, 0]), "ffn_conv_b": ffn_conv_b,
        "norm_f": norm_f.reshape(1, D_MODEL),
    }
    meta_f = _gather_cols(meta_s)

    pending = {}

    def on_ffn_out_grads(d_w_down, d_w_up_t, d_w_out):
        srcs = [d_w_out.reshape(N_DEV, D_MODEL // N_DEV, D_MODEL), d_w_up_t.reshape(N_DEV, 2 * D_FF // N_DEV, D_MODEL),
                d_w_down.reshape(N_DEV, D_FF // N_DEV, D_MODEL)]
        pending["ffn_out"], token = _split_start(srcs, "a2a", "exchange_ffn_out_start", d_w_out)
        return token

    def on_w_in_grads(d_w_in_t):
        slabs = d_w_in_t.astype(BF16).reshape(N_DEV, _O_END // N_DEV, D_MODEL)
        pending["w_in"], token = _split_start([slabs], "a2a", "exchange_w_in_start", d_w_in_t)
        return token

    head = jnp.concatenate([jnp.zeros((pad, D_MODEL), F32), meta_f], axis=0)
    if S >= 2 * 704:
        hpad = _Rows(x[0], pad + N_META, head)
        tgt = _Rows(loss_target[0], pad + N_META)
    else:
        hpad = jnp.concatenate([head, x[0]], axis=0)
        tgt = jnp.concatenate([jnp.zeros((pad + N_META, D_MODEL), F32), loss_target[0]], axis=0)
    lossvec, dh0, gr = _local_step(hpad, tgt, pad, wt, first_weights, late_weights, on_ffn_out_grads, on_w_in_grads)

    loss = lax.psum(jnp.sum(lossvec), ("x", "y", "c"))
    grad_x = dh0[pad + N_META:][None]

    big_m = [tr_(m_w_in), m_w_out[0], tr_(m_w_ffn_up), m_w_ffn_down[0]]
    big_v = [tr_(v_w_in), v_w_out[0], tr_(v_w_ffn_up), v_w_ffn_down[0]]
    slabs_ffn_out = _split_wait(pending["ffn_out"], "exchange_ffn_out_wait", dh0)
    big_out = [None] + [_adamw(slabs_ffn_out[i - 1], big[i], big_m[i], big_v[i], "adamw_big_%d" % i)
                        for i in range(1, len(big))]
    g_sm = [_scatter_cols(dh0[pad:pad + N_META]), _scatter_cols(gr["gdn_conv_w"]), _scatter_cols(gr["ffn_conv_w"])]
    g_small = jnp.stack([_pack([g[d] for g in g_sm], 8) for d in range(N_DEV)])
    slabs_small, = _all_to_all([g_small], "exchange_small_gradients")
    small_out = _adamw(slabs_small, _pack(small, 8), _pack([m_meta, m_gdn_conv_w, m_ffn_conv_w], 8),
                       _pack([v_meta, v_gdn_conv_w, v_ffn_conv_w], 8), "adamw_small_sharded")
    small_un = [_unpack(o, [a.shape for a in small]) for o in small_out]
    rep_w = [norm1, gdn_a_log, gdn_dt_bias, gdn_norm, norm2, ffn_conv_b, norm_f]
    rep_m = [m_norm1, m_gdn_a_log, m_gdn_dt_bias, m_gdn_norm, m_norm2, m_ffn_conv_b, m_norm_f]
    rep_v = [v_norm1, v_gdn_a_log, v_gdn_dt_bias, v_gdn_norm, v_norm2, v_ffn_conv_b, v_norm_f]
    rep_g = [gr["norm1"], gr["a_log"], gr["dt_bias"], gr["gdn_norm"], gr["norm2"], gr["ffn_conv_b"], gr["norm_f"]]
    rep_slabs, = _all_gather([_pack(rep_g, 8)], "gather_small_gradients")
    rep_out = _adamw(rep_slabs, _pack(rep_w, 8), _pack(rep_m, 8), _pack(rep_v, 8), "adamw_replicated")
    rep_shapes = [a.shape for a in rep_w]
    rp_g, rp_d, rp_nm, rp_nv = [_unpack(o, rep_shapes) for o in rep_out]

    slabs_w_in, = _split_wait(pending["w_in"], "exchange_w_in_wait", rep_out[0])
    big_out[0] = _adamw(slabs_w_in, big[0], big_m[0], big_v[0], "adamw_big_0")
    back = lambda a: jnp.swapaxes(a, 0, 1)[None]
    sh_g, sh_d, sh_nm, sh_nv = [
        [small_un[j][0], back(big_out[0][j]), small_un[j][1], big_out[1][j][None], back(big_out[2][j]),
         small_un[j][2], big_out[3][j][None]] for j in range(4)]

    def order(sh, rp):
        return [sh[0], rp[0], sh[1], sh[2], rp[1], rp[2], rp[3], sh[3], rp[4], sh[4], sh[5], rp[5], sh[6], rp[6]]

    return (loss, grad_x, *order(sh_g, rp_g), *order(sh_d, rp_d), *order(sh_nm, rp_nm), *order(sh_nv, rp_nv))
```

```python
import functools
import math

import numpy as np
import jax
import jax.numpy as jnp
from jax import lax
from jax.experimental import pallas as pl
from jax.experimental.pallas import tpu as pltpu

F32 = jnp.float32
BF16 = jnp.bfloat16

D_MODEL = 1024
N_META = 16
CHUNK = 64
GDN_H = 8
GDN_D = 128
RET_H = 4
RET_D = 256
D_FF = 2816
GDN_CONV = 4
FFN_CONV = 3
ROPE_BASE = 10000.0
EPS = 1e-6
N_DEV = 8
LANES = 128
MAIN_W = 10 * 1024
_O_GQ, _O_GZ, _O_GA, _O_RQ, _O_RG, _O_GATE, _O_END = 0, 3072, 4096, 4112, 7184, 8208, 10256

ADAM_LR = 0.001
ADAM_B1 = 0.9
ADAM_B2 = 0.999
ADAM_EPS = 1e-08
ADAM_WD = 0.01
ADAM_STEP = 10

MESH_T = pl.DeviceIdType.MESH


def _tile(n, target, mult):
    best = None
    for d in range(mult, min(n, target) + 1, mult):
        if n % d == 0:
            best = d
    assert best is not None, (n, target, mult)
    return best


def _sig(x):
    return 0.5 * jnp.tanh(0.5 * x) + 0.5


def _d(a, b):
    return jnp.dot(a.astype(BF16), b.astype(BF16), preferred_element_type=F32)


def _dnt(a, b):
    return lax.dot_general(a.astype(BF16), b.astype(BF16), (((1,), (1,)), ((), ())), preferred_element_type=F32)


def _dtn(a, b):
    return lax.dot_general(a.astype(BF16), b.astype(BF16), (((0,), (0,)), ((), ())), preferred_element_type=F32)


def _dxg(a, b, dims):
    f = functools.partial(lax.dot_general, dimension_numbers=dims, preferred_element_type=F32)
    ab = a.astype(BF16)
    b1 = b.astype(BF16)
    r1 = b - b1.astype(F32)
    b2 = r1.astype(BF16)
    b3 = (r1 - b2.astype(F32)).astype(BF16)
    return f(ab, b1) + (f(ab, b2) + f(ab, b3))


def _dx(a, b):
    return _dxg(a, b, (((1,), (0,)), ((), ())))


def _dxnt(a, b):
    return _dxg(a, b, (((1,), (1,)), ((), ())))


def _split(a):
    hi = a.astype(BF16)
    return hi, (a - hi.astype(F32)).astype(BF16)


def _d3g(a, b, dims):
    ah, al = _split(a)
    bh, bl = _split(b)
    f = functools.partial(lax.dot_general, dimension_numbers=dims, preferred_element_type=F32)
    if dims == _NN:
        rows = a.shape[0]
        both = f(jnp.concatenate([ah, al], axis=0), bh)
        return both[:rows] + (f(ah, bl) + both[rows:])
    return f(ah, bh) + (f(ah, bl) + f(al, bh))


def _d2x(a, b, dims):
    ah, al = _split(a)
    f = functools.partial(lax.dot_general, dimension_numbers=dims, preferred_element_type=F32)
    bb = b.astype(BF16)
    return f(ah, bb) + f(al, bb)


_NN = (((1,), (0,)), ((), ()))
_NT = (((1,), (1,)), ((), ()))
_TN = (((0,), (0,)), ((), ()))


def _rowsum(x):
    return jnp.sum(x, axis=1, keepdims=True)


def _allsum(x):
    return jnp.sum(jnp.sum(x, axis=1, keepdims=True), axis=0, keepdims=True)


def _mm_nn(a, b, res=None, out_dtype=F32, bt=False, tm_target=704, b_rows=None, name="mm_nn"):
    M, K = a.shape
    N = b.shape[0] if bt else b.shape[1]
    tm = _tile(M, tm_target, 16)
    if b_rows is None:
        tn = _tile(N, 2816, 128)
    else:
        tn, n_tiles, start = b_rows
        N = tn * n_tiles

    def body(*refs):
        if res is None:
            a_ref, b_ref, o_ref = refs
        else:
            a_ref, b_ref, r_ref, o_ref = refs
        acc = lax.dot_general(a_ref[...], b_ref[...], _NT if bt else _NN, preferred_element_type=F32)
        if res is not None:
            acc = acc + r_ref[...]
        o_ref[...] = acc.astype(out_dtype)

    b_spec = pl.BlockSpec((tn, K), lambda j, i: (j, 0)) if bt else pl.BlockSpec((K, tn), lambda j, i: (0, j))
    if b_rows is not None:
        b_spec = pl.BlockSpec((pl.Element(tn), pl.Element(K)), lambda j, i: (pl.multiple_of(start(j), 16), 0))
    in_specs = [pl.BlockSpec((tm, K), lambda j, i: (i, 0)), b_spec]
    args = [a, b]
    if res is not None:
        in_specs.append(pl.BlockSpec((tm, tn), lambda j, i: (i, j)))
        args.append(res)
    return pl.pallas_call(
        body, grid=(N // tn, M // tm), in_specs=in_specs,
        out_specs=pl.BlockSpec((tm, tn), lambda j, i: (i, j)),
        out_shape=jax.ShapeDtypeStruct((M, N), out_dtype), name=name)(*args)


def _mm_sum(pairs, name):
    M = pairs[0][0].shape[0]
    N = pairs[0][1].shape[1]
    tm = _tile(M, 704, 16)
    n = len(pairs)

    def body(*refs):
        o_ref = refs[-1]
        acc = jnp.dot(refs[0][...], refs[1][...], preferred_element_type=F32)
        for i in range(1, n):
            acc = acc + jnp.dot(refs[2 * i][...], refs[2 * i + 1][...], preferred_element_type=F32)
        o_ref[...] = acc

    specs, args = [], []
    for a, b in pairs:
        specs += [pl.BlockSpec((tm, a.shape[1]), lambda i: (i, 0)),
                  pl.BlockSpec(b.shape, lambda i: (0, 0), pipeline_mode=pl.Buffered(1))]
        args += [a, b]
    return pl.pallas_call(
        body, grid=(M // tm,), in_specs=specs, out_specs=pl.BlockSpec((tm, N), lambda i: (i, 0)),
        out_shape=jax.ShapeDtypeStruct((M, N), F32), name=name)(*args)


def _mm_nt(a, b, res=None, name="mm_nt"):
    M, Nc = a.shape
    K = b.shape[0]
    tm = _tile(M, 704, 16)
    tc = _tile(Nc, 5632, 128)

    def body(*refs):
        if res is None:
            a_ref, b_ref, o_ref = refs
        else:
            a_ref, b_ref, r_ref, o_ref = refs
        c = pl.program_id(1)
        p = lax.dot_general(a_ref[...], b_ref[...], (((1,), (1,)), ((), ())), preferred_element_type=F32)

        @pl.when(c == 0)
        def _():
            if res is None:
                o_ref[...] = p
            else:
                o_ref[...] = p + r_ref[...]

        @pl.when(c > 0)
        def _():
            o_ref[...] += p

    in_specs = [pl.BlockSpec((tm, tc), lambda i, c: (i, c)), pl.BlockSpec((K, tc), lambda i, c: (0, c))]
    args = [a, b]
    if res is not None:
        in_specs.append(pl.BlockSpec((tm, K), lambda i, c: (i, 0)))
        args.append(res)
    return pl.pallas_call(
        body, grid=(M // tm, Nc // tc), in_specs=in_specs,
        out_specs=pl.BlockSpec((tm, K), lambda i, c: (i, 0)),
        out_shape=jax.ShapeDtypeStruct((M, K), F32), name=name)(*args)


def _mm_tn(a, b, out_dtype=F32, name="mm_tn"):
    M, K = a.shape
    N = b.shape[1]
    tm = _tile(M, 2752, 16)
    tk = _tile(K, 1408, 128)
    tn = _tile(N, 1408, 128)
    steps = M // tm

    def body(a_ref, b_ref, o_ref, *scratch):
        acc = scratch[0] if scratch else o_ref
        m = pl.program_id(2)
        p = lax.dot_general(a_ref[...], b_ref[...], (((0,), (0,)), ((), ())), preferred_element_type=F32)

        @pl.when(m == 0)
        def _():
            acc[...] = p

        @pl.when(m > 0)
        def _():
            acc[...] += p

        if scratch:
            @pl.when(m == steps - 1)
            def _():
                o_ref[...] = acc[...].astype(out_dtype)

    return pl.pallas_call(
        body, grid=(K // tk, N // tn, steps),
        in_specs=[pl.BlockSpec((tm, tk), lambda kk, j, m: (m, kk)), pl.BlockSpec((tm, tn), lambda kk, j, m: (m, j))],
        out_specs=pl.BlockSpec((tk, tn), lambda kk, j, m: (kk, j)),
        out_shape=jax.ShapeDtypeStruct((K, N), out_dtype),
        scratch_shapes=[] if out_dtype == F32 else [pltpu.VMEM((tk, tn), F32)], name=name)(a, b)


class _Rows:
    def __init__(self, body, first, head=None):
        self.body, self.first, self.head = body, first, head
        self.shape = (body.shape[0] + first, body.shape[1])


def _rows_operands(x, tr):
    if not isinstance(x, _Rows):
        return [x], [pl.BlockSpec((tr, x.shape[1]), lambda i: (i, 0))]
    assert x.first % 8 == 0 and x.first <= tr <= x.body.shape[0] and x.shape[0] % tr == 0
    width = x.shape[1]
    args = [x.body]
    specs = [pl.BlockSpec((pl.Element(tr), pl.Element(width)),
                          lambda i: (pl.multiple_of(jnp.maximum(i * tr - x.first, 0), 8), 0))]
    if x.head is not None:
        args.append(jnp.pad(x.head, ((0, tr - x.first), (0, 0))))
        specs.append(pl.BlockSpec((tr, width), lambda i: (0, 0)))
    return args, specs


def _rows_tile(x, refs, i, tr):
    blk = refs[0][...]
    if not isinstance(x, _Rows):
        return blk
    shifted = pltpu.roll(blk, x.first, 0)
    if x.head is not None:
        row = lax.broadcasted_iota(jnp.int32, (tr, 1), 0)
        shifted = jnp.where(row < x.first, refs[1][...], shifted)
    return jnp.where(i == 0, shifted, blk)


def _rms_fwd(x, g, name):
    Lp = x.shape[0]
    tr = _tile(Lp, 256, 16)
    args, specs = _rows_operands(x, tr)
    n = len(args)

    def body(*refs):
        g_ref, o_ref = refs[n:]
        xv = _rows_tile(x, refs[:n], pl.program_id(0), tr)
        r = lax.rsqrt(jnp.mean(xv * xv, axis=-1, keepdims=True) + EPS)
        o_ref[...] = (xv * r * g_ref[...]).astype(BF16)

    return pl.pallas_call(
        body, grid=(Lp // tr,),
        in_specs=specs + [pl.BlockSpec((1, D_MODEL), lambda i: (0, 0))],
        out_specs=pl.BlockSpec((tr, D_MODEL), lambda i: (i, 0)),
        out_shape=jax.ShapeDtypeStruct((Lp, D_MODEL), BF16), name=name)(*args, g)


class _Producer:
    def __init__(self, a, b, res=None):
        self.a, self.b, self.res = a, b, res
        self.tr = _tile(a.shape[0], 704, 16)
        K = a.shape[1]
        r_args, r_specs = ([], []) if res is None else _rows_operands(res, self.tr)
        self.args = [a, b] + r_args
        self.specs = [pl.BlockSpec((self.tr, K), lambda i: (i, 0)),
                      pl.BlockSpec((K, D_MODEL), lambda i: (0, 0), pipeline_mode=pl.Buffered(1))] + r_specs

    def tile(self, refs, i):
        acc = jnp.dot(refs[0][...], refs[1][...], preferred_element_type=F32)
        return acc if self.res is None else acc + _rows_tile(self.res, refs[2:], i, self.tr)


def _mm_rms_fwd(prod, g, name):
    Lp, tr, n = prod.a.shape[0], prod.tr, len(prod.args)

    def body(*refs):
        g_ref, x_ref, o_ref = refs[n:]
        xv = prod.tile(refs[:n], pl.program_id(0))
        r = lax.rsqrt(jnp.mean(xv * xv, axis=-1, keepdims=True) + EPS)
        x_ref[...] = xv
        o_ref[...] = (xv * r * g_ref[...]).astype(BF16)

    blk = pl.BlockSpec((tr, D_MODEL), lambda i: (i, 0))
    return pl.pallas_call(
        body, grid=(Lp // tr,), in_specs=prod.specs + [pl.BlockSpec((1, D_MODEL), lambda i: (0, 0))],
        out_specs=[blk, blk],
        out_shape=[jax.ShapeDtypeStruct((Lp, D_MODEL), F32), jax.ShapeDtypeStruct((Lp, D_MODEL), BF16)],
        name=name)(*prod.args, g)


def _rms_bwd(x, g, dy, dres, pad, name):
    Lp = x.shape[0]
    fused = isinstance(dy, _Producer)
    tr = dy.tr if fused else _tile(Lp, 256, 16)
    n = len(dy.args) if fused else 1
    x_args, x_specs = _rows_operands(x, tr)
    nx = len(x_args)

    def body(*refs):
        g_ref, dr_ref, dx_ref, dxb_ref, dg_ref = refs[n + nx:]
        i = pl.program_id(0)
        xv = _rows_tile(x, refs[n:n + nx], i, tr)
        r = lax.rsqrt(jnp.mean(xv * xv, axis=-1, keepdims=True) + EPS)
        xh = xv * r
        dyv = dy.tile(refs[:n], i) if fused else refs[0][...]
        dxh = dyv * g_ref[...]
        dx = r * (dxh - xh * jnp.mean(dxh * xh, axis=-1, keepdims=True)) + dr_ref[...]
        row = i * tr + lax.broadcasted_iota(jnp.int32, (tr, 1), 0)
        dx = jnp.where(row >= pad, dx, 0.0)
        dx_ref[...] = dx
        dxb_ref[...] = dx.astype(BF16)
        part = jnp.sum(dyv * xh, axis=0, keepdims=True)

        @pl.when(i == 0)
        def _():
            dg_ref[...] = part

        @pl.when(i > 0)
        def _():
            dg_ref[...] += part

    blk = pl.BlockSpec((tr, D_MODEL), lambda i: (i, 0))
    vec = pl.BlockSpec((1, D_MODEL), lambda i: (0, 0))
    return pl.pallas_call(
        body, grid=(Lp // tr,), in_specs=(dy.specs if fused else [blk]) + x_specs + [vec, blk],
        out_specs=[blk, blk, vec],
        out_shape=[jax.ShapeDtypeStruct((Lp, D_MODEL), F32), jax.ShapeDtypeStruct((Lp, D_MODEL), BF16),
                   jax.ShapeDtypeStruct((1, D_MODEL), F32)], name=name)(*(dy.args if fused else [dy]), *x_args, g, dres)


def _final(h2, g, tgt, first_row):
    fused = isinstance(h2, _Producer)
    Lp = h2.a.shape[0] if fused else h2.shape[0]
    tr = h2.tr if fused else _tile(Lp, 256, 16)
    n = len(h2.args) if fused else 1
    t_args, t_specs = _rows_operands(tgt, tr)
    nt = len(t_args)

    def body(*refs):
        g_ref = refs[n]
        loss_ref, dx_ref, dxb_ref, dg_ref = refs[n + 1 + nt:]
        i = pl.program_id(0)
        xv = h2.tile(refs[:n], i) if fused else refs[0][...]
        tv = _rows_tile(tgt, refs[n + 1:n + 1 + nt], i, tr)
        gv = g_ref[...]
        r = lax.rsqrt(jnp.mean(xv * xv, axis=-1, keepdims=True) + EPS)
        xh = xv * r
        row = i * tr + lax.broadcasted_iota(jnp.int32, (tr, 1), 0)
        err = jnp.where(row >= first_row, xh * gv - tv, 0.0)
        lpart = jnp.sum(err * err, axis=0, keepdims=True) * (0.5 / D_MODEL)
        dyv = err * (1.0 / D_MODEL)
        dxh = dyv * gv
        dx = r * (dxh - xh * jnp.mean(dxh * xh, axis=-1, keepdims=True))
        dx_ref[...] = dx
        dxb_ref[...] = dx.astype(BF16)
        part = jnp.sum(dyv * xh, axis=0, keepdims=True)

        @pl.when(i == 0)
        def _():
            dg_ref[...] = part
            loss_ref[...] = lpart

        @pl.when(i > 0)
        def _():
            dg_ref[...] += part
            loss_ref[...] += lpart

    blk = pl.BlockSpec((tr, D_MODEL), lambda i: (i, 0))
    vec = pl.BlockSpec((1, D_MODEL), lambda i: (0, 0))
    return pl.pallas_call(
        body, grid=(Lp // tr,), in_specs=(h2.specs if fused else [blk]) + [vec] + t_specs,
        out_specs=[vec, blk, blk, vec],
        out_shape=[jax.ShapeDtypeStruct((1, D_MODEL), F32), jax.ShapeDtypeStruct((Lp, D_MODEL), F32),
                   jax.ShapeDtypeStruct((Lp, D_MODEL), BF16), jax.ShapeDtypeStruct((1, D_MODEL), F32)],
        name="final_norm_loss")(*(h2.args if fused else [h2]), g, *t_args)


def _halo_prev(tr, width, col=0):
    return pl.BlockSpec((8, width), lambda i: (jnp.maximum(i * (tr // 8) - 1, 0), col))


def _halo_next(tr, width, nrows, col=0, rows=8):
    last = nrows // rows - 1
    return pl.BlockSpec((rows, width), lambda i: (jnp.minimum((i + 1) * (tr // rows), last), col))


def _shifted(x, offs):
    n = x.shape[0]
    return [x if off == 0 else pltpu.roll(x, n - off, 0) for off in offs]


def _taps(wins, w, rows, bias=None):
    acc = w[0:1, :] * wins[0][0:rows, :]
    if bias is not None:
        acc = acc + bias
    for kk in range(1, len(wins)):
        acc = acc + w[kk:kk + 1, :] * wins[kk][0:rows, :]
    return acc


def _gdn_pre(proj_m, proj_s, conv_w, gparams, pad):
    Lp = proj_m.shape[0]
    tr = _tile(Lp, 192, 64)
    W3 = 3 * D_MODEL

    def body(main_ref, prev_ref, s_ref, w_ref, gp_ref, qkv_ref, gsm_ref, c_ref):
        i = pl.program_id(0)
        prev = jnp.where(i > 0, prev_ref[...], 0.0)
        ext = jnp.concatenate([prev, main_ref[...]], axis=0)
        c = _taps(_shifted(ext, range(8 - (GDN_CONV - 1), 9)), w_ref[...], tr)
        c_ref[...] = c.astype(BF16)
        s = c * _sig(c)
        scale = GDN_D ** -0.5
        for j in range(2 * GDN_H):
            seg = s[:, j * GDN_D:(j + 1) * GDN_D]
            r = lax.rsqrt(_rowsum(seg * seg) + EPS)
            if j < GDN_H:
                r = r * scale
            qkv_ref[:, j * GDN_D:(j + 1) * GDN_D] = seg * r
        qkv_ref[:, 2 * D_MODEL:] = s[:, 2 * D_MODEL:]
        sm = s_ref[...]
        gp = gp_ref[...]
        lane = lax.broadcasted_iota(jnp.int32, sm.shape, 1)
        z = sm + gp[1:2, :]
        softplus = jnp.maximum(z, 0.0) + jnp.log(1.0 + jnp.exp(-jnp.abs(z)))
        lg = -jnp.exp(gp[0:1, :]) * softplus
        row = i * tr + lax.broadcasted_iota(jnp.int32, (tr, 1), 0)
        out = jnp.where(lane < GDN_H, lg, jnp.where(lane < 2 * GDN_H, _sig(sm), 0.0))
        gsm_ref[...] = jnp.where(row >= pad, out, 0.0)

    return pl.pallas_call(
        body, grid=(Lp // tr,),
        in_specs=[pl.BlockSpec((tr, W3), lambda i: (i, 0)), _halo_prev(tr, W3),
                  pl.BlockSpec((tr, LANES), lambda i: (i, 0)),
                  pl.BlockSpec((GDN_CONV, W3), lambda i: (0, 0)), pl.BlockSpec((8, LANES), lambda i: (0, 0))],
        out_specs=[pl.BlockSpec((tr, W3), lambda i: (i, 0)), pl.BlockSpec((tr, LANES), lambda i: (i, 0)),
                   pl.BlockSpec((tr, W3), lambda i: (i, 0))],
        out_shape=[jax.ShapeDtypeStruct((Lp, W3), F32), jax.ShapeDtypeStruct((Lp, LANES), F32),
                   jax.ShapeDtypeStruct((Lp, W3), BF16)],
        name="gdn_pre")(proj_m, proj_m, proj_s, conv_w, gparams)


def _gdn_pre_bwd(proj_m, conv_out, proj_s, conv_w, gparams, dq, dk, dv, dgs, pad):
    Lp = proj_m.shape[0]
    tr = _tile(Lp, 192, 64)
    W3 = 3 * D_MODEL
    te = tr + 8

    def body(main_ref, c_ref, cn_ref, s_ref, w_ref, gp_ref,
             dq_ref, dqn_ref, dk_ref, dkn_ref, dv_ref, dvn_ref, dgs_ref,
             da_ref, ds_ref, dw_ref, dgp_ref):
        i = pl.program_id(0)
        w = w_ref[...]
        c = jnp.concatenate([c_ref[...].astype(F32), cn_ref[...].astype(F32)[0:8]], axis=0)
        sg = _sig(c)
        s = c * sg
        rowe = i * tr + lax.broadcasted_iota(jnp.int32, (te, 1), 0)
        live = (rowe >= pad) & (rowe < Lp)
        dqe = jnp.concatenate([dq_ref[...], dqn_ref[...]], axis=0)
        dke = jnp.concatenate([dk_ref[...], dkn_ref[...]], axis=0)
        dve = jnp.concatenate([dv_ref[...], dvn_ref[...]], axis=0)
        scale = GDN_D ** -0.5
        parts = []
        for j in range(2 * GDN_H):
            seg = s[:, j * GDN_D:(j + 1) * GDN_D]
            r = lax.rsqrt(_rowsum(seg * seg) + EPS)
            xh = seg * r
            if j < GDN_H:
                dxh = dqe[:, j * GDN_D:(j + 1) * GDN_D] * scale
            else:
                dxh = dke[:, (j - GDN_H) * GDN_D:(j - GDN_H + 1) * GDN_D]
            parts.append(r * (dxh - xh * _rowsum(dxh * xh)))
        parts.append(dve)
        dsv = jnp.concatenate(parts, axis=1)
        dc = jnp.where(live, dsv * (sg * (1.0 + c * (1.0 - sg))), 0.0)
        dcs = _shifted(dc, range(GDN_CONV - 1, -1, -1))
        da_ref[...] = _taps(dcs, w, tr).astype(BF16)
        pm = main_ref[...]
        rows = [jnp.sum(dcs[kk][0:tr, :] * pm, axis=0, keepdims=True) for kk in range(GDN_CONV)]
        dwp = jnp.concatenate(rows + [jnp.zeros((8 - GDN_CONV, W3), F32)], axis=0)

        sm = s_ref[...]
        gp = gp_ref[...]
        lane = lax.broadcasted_iota(jnp.int32, sm.shape, 1)
        rowm = i * tr + lax.broadcasted_iota(jnp.int32, (tr, 1), 0)
        dgv = jnp.where(rowm >= pad, dgs_ref[...], 0.0)
        dlg = jnp.where(lane < GDN_H, dgv, 0.0)
        dbt = jnp.where((lane >= GDN_H) & (lane < 2 * GDN_H), dgv, 0.0)
        z = sm + gp[1:2, :]
        softplus = jnp.maximum(z, 0.0) + jnp.log(1.0 + jnp.exp(-jnp.abs(z)))
        ea = jnp.exp(gp[0:1, :])
        dz = dlg * (-ea) * _sig(z)
        dal = dlg * (-ea) * softplus
        bt = _sig(sm)
        dgb = dbt * bt * (1.0 - bt)
        ds_ref[...] = (dz + dgb).astype(BF16)
        gpp = jnp.concatenate([jnp.sum(dal, axis=0, keepdims=True), jnp.sum(dz, axis=0, keepdims=True),
                               jnp.zeros((6, LANES), F32)], axis=0)

        @pl.when(i == 0)
        def _():
            dw_ref[...] = dwp
            dgp_ref[...] = gpp

        @pl.when(i > 0)
        def _():
            dw_ref[...] += dwp
            dgp_ref[...] += gpp

    m3 = pl.BlockSpec((tr, W3), lambda i: (i, 0))
    m1 = pl.BlockSpec((tr, D_MODEL), lambda i: (i, 0))
    n1 = _halo_next(tr, D_MODEL, Lp)
    return pl.pallas_call(
        body, grid=(Lp // tr,),
        in_specs=[m3, m3, _halo_next(tr, W3, Lp, rows=16), pl.BlockSpec((tr, LANES), lambda i: (i, 0)),
                  pl.BlockSpec((GDN_CONV, W3), lambda i: (0, 0)), pl.BlockSpec((8, LANES), lambda i: (0, 0)),
                  m1, n1, m1, n1, m1, n1, pl.BlockSpec((tr, LANES), lambda i: (i, 0))],
        out_specs=[m3, pl.BlockSpec((tr, LANES), lambda i: (i, 0)),
                   pl.BlockSpec((8, W3), lambda i: (0, 0)), pl.BlockSpec((8, LANES), lambda i: (0, 0))],
        out_shape=[jax.ShapeDtypeStruct((Lp, W3), BF16), jax.ShapeDtypeStruct((Lp, LANES), BF16),
                   jax.ShapeDtypeStruct((8, W3), F32), jax.ShapeDtypeStruct((8, LANES), F32)],
        name="gdn_pre_bwd")(proj_m, conv_out, conv_out, proj_s, conv_w, gparams, dq, dq, dk, dk, dv, dv, dgs)


def _gdn_gates(gs):
    ri = lax.broadcasted_iota(jnp.int32, (CHUNK, CHUNK), 0)
    ci = lax.broadcasted_iota(jnp.int32, (CHUNK, CHUNK), 1)
    tril = ri >= ci
    strict = ri > ci
    gall = _dx(tril.astype(F32), gs)
    lane8 = lax.broadcasted_iota(jnp.int32, (8, LANES), 1)
    sub8 = lax.broadcasted_iota(jnp.int32, (8, LANES), 0)
    grow = _dxnt((lane8 == sub8).astype(F32), gall)
    return gall, grow, tril, strict


def _gdn_decay(gall, grow, tril, h):
    g = gall[:, h:h + 1]
    return g, jnp.where(tril, jnp.exp(jnp.where(tril, g - grow[h:h + 1, :], 0.0)), 0.0)


def _group(N):
    return 3 if N % 3 == 0 else (2 if N % 2 == 0 else 1)


def _gdn_chunk_specs(N, rev):
    G = _group(N)
    nb = N // G
    cn = (lambda n: nb - 1 - n) if rev else (lambda n: n)
    col = lambda j: pl.BlockSpec((G * CHUNK, D_MODEL), lambda n: (cn(n), j))
    gate = pl.BlockSpec((G * CHUNK, LANES), lambda n: (cn(n), 0))
    st = lambda a, b: pl.BlockSpec((GDN_H, G, a, b), lambda n: (0, cn(n), 0, 0))
    return G, nb, col, gate, st


def _gdn_chunk_fwd(qkv, gsm):
    Lp = qkv.shape[0]
    N = Lp // CHUNK
    G, nb, col, gate, st = _gdn_chunk_specs(N, False)

    def body(q_ref, k_ref, v_ref, gs_ref, o_ref, sin_ref, t_ref, S):
        n = pl.program_id(0)

        @pl.when(n == 0)
        def _():
            S[...] = jnp.zeros_like(S)

        ri = lax.broadcasted_iota(jnp.int32, (CHUNK, CHUNK), 0)
        ci = lax.broadcasted_iota(jnp.int32, (CHUNK, CHUNK), 1)
        eye = (ri == ci).astype(F32)
        heads = range(GDN_H)
        sls = [slice(h * GDN_D, (h + 1) * GDN_D) for h in heads]
        rows = [slice(c * CHUNK, (c + 1) * CHUNK) for c in range(G)]
        pairs = [(c, h) for c in range(G) for h in heads]
        P = lambda f: {p: f(*p) for p in pairs}
        gs = [gs_ref[rows[c], :] for c in range(G)]
        gates = [_gdn_gates(gs[c]) for c in range(G)]
        tril, strict = gates[0][2], gates[0][3]
        q = P(lambda c, h: q_ref[rows[c], sls[h]])
        k = P(lambda c, h: k_ref[rows[c], sls[h]])
        v = P(lambda c, h: v_ref[rows[c], sls[h]])
        beta = P(lambda c, h: gs[c][:, GDN_H + h:GDN_H + h + 1])
        gg = P(lambda c, h: _gdn_decay(gates[c][0], gates[c][1], tril, h))
        g = {p: x[0] for p, x in gg.items()}
        gam = {p: x[1] for p, x in gg.items()}
        eg = P(lambda c, h: jnp.exp(g[c, h]))
        gl = P(lambda c, h: g[c, h][CHUNK - 1:CHUNK, :])
        kb = P(lambda c, h: k[c, h] * beta[c, h])
        pw = P(lambda c, h: -jnp.where(strict, _dnt(kb[c, h], k[c, h]) * gam[c, h], 0.0))
        p = P(lambda c, h: _dnt(q[c, h], k[c, h]) * gam[c, h])
        t = P(lambda c, h: eye + pw[c, h])
        for it in range(5):
            mm = _d3g if it < 2 else (lambda a, b, dims: _d(a, b))
            pw = P(lambda c, h: mm(pw[c, h], pw[c, h], _NN))
            t = P(lambda c, h: t[c, h] + mm(t[c, h], pw[c, h], _NN))
        u = P(lambda c, h: _d(t[c, h], v[c, h] * beta[c, h]))
        w = P(lambda c, h: _d(t[c, h], kb[c, h] * eg[c, h]))
        qg = P(lambda c, h: q[c, h] * eg[c, h])
        kd = P(lambda c, h: k[c, h] * jnp.exp(gl[c, h] - g[c, h]))
        egl = P(lambda c, h: jnp.exp(gl[c, h]))
        for c in range(G):
            for h in heads:
                t_ref[h, c] = t[c, h]
        cur = [S[h] for h in heads]
        for c in range(G):
            vnew = [u[c, h] - _d(w[c, h], cur[h]) for h in heads]
            for h in heads:
                o_ref[rows[c], sls[h]] = _d(qg[c, h], cur[h]) + _d(p[c, h], vnew[h])
                sin_ref[h, c] = cur[h]
            cur = [cur[h] * egl[c, h] + _dtn(kd[c, h], vnew[h]) for h in heads]
        for h in heads:
            S[h] = cur[h]

    return pl.pallas_call(
        body, grid=(nb,),
        in_specs=[col(0), col(1), col(2), gate],
        out_specs=[col(0), st(GDN_D, GDN_D), st(CHUNK, CHUNK)],
        out_shape=[jax.ShapeDtypeStruct((Lp, D_MODEL), F32), jax.ShapeDtypeStruct((GDN_H, N, GDN_D, GDN_D), F32),
                   jax.ShapeDtypeStruct((GDN_H, N, CHUNK, CHUNK), F32)],
        scratch_shapes=[pltpu.VMEM((GDN_H, GDN_D, GDN_D), F32)],
        name="gdn_chunk_fwd")(qkv, qkv, qkv, gsm)


def _gdn_chunk_bwd(qkv, gsm, do, s_in, t_in):
    Lp = qkv.shape[0]
    N = Lp // CHUNK
    G, nb, col, gate, st = _gdn_chunk_specs(N, True)

    def body(q_ref, k_ref, v_ref, gs_ref, do_ref, sin_ref, t_ref, dq_ref, dk_ref, dv_ref, dgs_ref, dS):
        n = pl.program_id(0)

        @pl.when(n == 0)
        def _():
            dS[...] = jnp.zeros_like(dS)

        lane = lax.broadcasted_iota(jnp.int32, (CHUNK, LANES), 1)
        rcol = lax.broadcasted_iota(jnp.int32, (CHUNK, 1), 0)
        ri = lax.broadcasted_iota(jnp.int32, (CHUNK, CHUNK), 0)
        ci = lax.broadcasted_iota(jnp.int32, (CHUNK, CHUNK), 1)
        ones = jnp.ones((CHUNK, LANES), F32)
        heads = range(GDN_H)
        sls = [slice(h * GDN_D, (h + 1) * GDN_D) for h in heads]
        rows = [slice(c * CHUNK, (c + 1) * CHUNK) for c in range(G)]
        pairs = [(c, h) for c in range(G) for h in heads]
        P = lambda f: {p: f(*p) for p in pairs}
        gs = [gs_ref[rows[c], :] for c in range(G)]
        gates = [_gdn_gates(gs[c]) for c in range(G)]
        tril, strict = gates[0][2], gates[0][3]
        q = P(lambda c, h: q_ref[rows[c], sls[h]])
        k = P(lambda c, h: k_ref[rows[c], sls[h]])
        v = P(lambda c, h: v_ref[rows[c], sls[h]])
        dov = P(lambda c, h: do_ref[rows[c], sls[h]])
        s0 = P(lambda c, h: sin_ref[h, c])
        t = P(lambda c, h: t_ref[h, c])
        beta = P(lambda c, h: gs[c][:, GDN_H + h:GDN_H + h + 1])
        gg = P(lambda c, h: _gdn_decay(gates[c][0], gates[c][1], tril, h))
        g = {p: x[0] for p, x in gg.items()}
        gam = {p: x[1] for p, x in gg.items()}
        eg = P(lambda c, h: jnp.exp(g[c, h]))
        egl = P(lambda c, h: jnp.exp(g[c, h][CHUNK - 1:CHUNK, :]))
        e = P(lambda c, h: jnp.exp(g[c, h][CHUNK - 1:CHUNK, :] - g[c, h]))
        kb = P(lambda c, h: k[c, h] * beta[c, h])
        kbg = P(lambda c, h: kb[c, h] * eg[c, h])
        vb = P(lambda c, h: v[c, h] * beta[c, h])
        qg = P(lambda c, h: q[c, h] * eg[c, h])
        kd = P(lambda c, h: k[c, h] * e[c, h])
        m = P(lambda c, h: jnp.where(strict, _dnt(kb[c, h], k[c, h]) * gam[c, h], 0.0))
        u = P(lambda c, h: _d(t[c, h], vb[c, h]))
        w = P(lambda c, h: _d(t[c, h], kbg[c, h]))
        p = P(lambda c, h: _dnt(q[c, h], k[c, h]) * gam[c, h])
        dqg = P(lambda c, h: _dnt(dov[c, h], s0[c, h]))
        qgdo = P(lambda c, h: _dtn(qg[c, h], dov[c, h]))
        ptdo = P(lambda c, h: _dtn(p[c, h], dov[c, h]))
        vnew = P(lambda c, h: u[c, h] - _d(w[c, h], s0[c, h]))
        dp = P(lambda c, h: jnp.where(tril, _dnt(dov[c, h], vnew[c, h]), 0.0))
        cur = [dS[h] for h in heads]
        dvnew, dkd, sds = {}, {}, {}
        for c in reversed(range(G)):
            for h in heads:
                dvnew[c, h] = ptdo[c, h] + _d(kd[c, h], cur[h])
                dkd[c, h] = _dnt(vnew[c, h], cur[h])
                sds[c, h] = _allsum(s0[c, h] * cur[h])
            cur = [qgdo[c, h] + egl[c, h] * cur[h] - _dtn(w[c, h], dvnew[c, h]) for h in heads]
        for h in heads:
            dS[h] = cur[h]
        dw = P(lambda c, h: -_dnt(dvnew[c, h], s0[c, h]))
        dvb = P(lambda c, h: _dtn(t[c, h], dvnew[c, h]))
        dkbg = P(lambda c, h: _dtn(t[c, h], dw[c, h]))
        dt = P(lambda c, h: _dnt(dvnew[c, h], vb[c, h]) + _dnt(dw[c, h], kbg[c, h]))
        x1 = P(lambda c, h: _dtn(t[c, h], dt[c, h]))
        dm = P(lambda c, h: jnp.where(strict, -_dnt(x1[c, h], t[c, h]), 0.0))
        dkk = P(lambda c, h: dm[c, h] * gam[c, h])
        dqk = P(lambda c, h: dp[c, h] * gam[c, h])
        dkb = P(lambda c, h: _d(dkk[c, h], k[c, h]) + eg[c, h] * dkbg[c, h])
        em = P(lambda c, h: dm[c, h] * m[c, h] + dp[c, h] * p[c, h])
        colsum = P(lambda c, h: _d2x(em[c, h], ones, _TN)[:, 0:1])
        for c, h in pairs:
            dk_ref[rows[c], sls[h]] = (_dtn(dkk[c, h], kb[c, h]) + _dtn(dqk[c, h], q[c, h]) + dkd[c, h] * e[c, h]
                                       + beta[c, h] * dkb[c, h])
            dq_ref[rows[c], sls[h]] = _d(dqk[c, h], k[c, h]) + dqg[c, h] * eg[c, h]
            dv_ref[rows[c], sls[h]] = beta[c, h] * dvb[c, h]
        for c in range(G):
            dg_all = jnp.zeros((CHUNK, LANES), F32)
            dbeta_all = jnp.zeros((CHUNK, LANES), F32)
            for h in heads:
                dbeta = _rowsum(k[c, h] * dkb[c, h]) + _rowsum(v[c, h] * dvb[c, h])
                z = _rowsum(kd[c, h] * dkd[c, h])
                dg = (_rowsum(em[c, h]) - colsum[c, h] + _rowsum(qg[c, h] * dqg[c, h]) + _rowsum(kbg[c, h] * dkbg[c, h])
                      - z)
                extra = _allsum(z) + egl[c, h] * sds[c, h]
                dg = dg + jnp.where(rcol == CHUNK - 1, extra, 0.0)
                dg_all = dg_all + jnp.where(lane == h, dg, 0.0)
                dbeta_all = dbeta_all + jnp.where(lane == GDN_H + h, dbeta, 0.0)
            dgs_ref[rows[c], :] = _dx((ci >= ri).astype(F32), dg_all) + dbeta_all

    return pl.pallas_call(
        body, grid=(nb,),
        in_specs=[col(0), col(1), col(2), gate, col(0), st(GDN_D, GDN_D), st(CHUNK, CHUNK)],
        out_specs=[col(0), col(0), col(0), gate],
        out_shape=[jax.ShapeDtypeStruct((Lp, D_MODEL), F32)] * 3 + [jax.ShapeDtypeStruct((Lp, LANES), F32)],
        scratch_shapes=[pltpu.VMEM((GDN_H, GDN_D, GDN_D), F32)],
        name="gdn_chunk_bwd")(qkv, qkv, qkv, gsm, do, s_in, t_in)


def _rot(x, c, s):
    half = RET_D // 2
    x1 = x[:, :half]
    x2 = x[:, half:]
    return jnp.concatenate([x1 * c - x2 * s, x2 * c + x1 * s], axis=1)


def _rot_bwd(d, c, s):
    half = RET_D // 2
    d1 = d[:, :half]
    d2 = d[:, half:]
    return jnp.concatenate([d1 * c + d2 * s, d2 * c - d1 * s], axis=1)


def _ret_tables():
    hh = jnp.arange(RET_H, dtype=F32)
    lg = jnp.log(1.0 - 2.0 ** (-5.0 - hh))
    idx = jnp.arange(CHUNK, dtype=F32)
    tril = jnp.asarray(np.tril(np.ones((CHUNK, CHUNK), dtype=bool)))
    dmask = jnp.where(tril, jnp.exp((idx[:, None] - idx[None, :]) * lg[:, None, None]), 0.0)
    qdec = jnp.exp((idx[None, :] + 1.0) * lg[:, None])
    kdec = jnp.exp((CHUNK - 1.0 - idx[None, :]) * lg[:, None])
    gch = jnp.exp(CHUNK * lg)
    qdec = jnp.broadcast_to(qdec[:, :, None], (RET_H, CHUNK, RET_D))
    kdec = jnp.broadcast_to(kdec[:, :, None], (RET_H, CHUNK, RET_D))
    gch = jnp.broadcast_to(gch[:, None, None], (RET_H, 8, LANES))
    return dmask, qdec, kdec, gch


def _ret_specs(N, rev):
    G = _group(N)
    nb = N // G
    cn = (lambda n: nb - 1 - n) if rev else (lambda n: n)
    col = lambda j: pl.BlockSpec((G * CHUNK, D_MODEL), lambda n: (cn(n), j))
    tab = lambda a, b: pl.BlockSpec((RET_H, a, b), lambda n: (0, 0, 0))
    rope = pl.BlockSpec((G * CHUNK, LANES), lambda n: (cn(n), 0))
    st = pl.BlockSpec((RET_H, G, RET_D, RET_D), lambda n: (0, cn(n), 0, 0))
    return G, nb, col, tab, rope, st


def _ret_chunk_fwd(proj_m, cos, sin, tables):
    Lp = proj_m.shape[0]
    N = Lp // CHUNK
    dmask, qdec, kdec, gch = tables
    G, nb, col, tab, rope, st = _ret_specs(N, False)

    def body(q_ref, k_ref, v_ref, c_ref, s_ref, dm_ref, qd_ref, kd_ref, g_ref, o_ref, sin_ref, S):
        n = pl.program_id(0)

        @pl.when(n == 0)
        def _():
            S[...] = jnp.zeros_like(S)

        heads = range(RET_H)
        sls = [slice(h * RET_D, (h + 1) * RET_D) for h in heads]
        rows = [slice(c * CHUNK, (c + 1) * CHUNK) for c in range(G)]
        pairs = [(c, h) for c in range(G) for h in heads]
        P = lambda f: {p: f(*p) for p in pairs}
        qr = P(lambda c, h: _rot(q_ref[rows[c], sls[h]], c_ref[rows[c], :], s_ref[rows[c], :]))
        ks = P(lambda c, h: _rot(k_ref[rows[c], sls[h]], c_ref[rows[c], :], s_ref[rows[c], :]) * (RET_D ** -0.5))
        v = P(lambda c, h: v_ref[rows[c], sls[h]])
        a = P(lambda c, h: _dnt(qr[c, h], ks[c, h]) * dm_ref[h])
        av = P(lambda c, h: _d(a[c, h], v[c, h]))
        kv = P(lambda c, h: _dtn(ks[c, h] * kd_ref[h], v[c, h]))
        qd = P(lambda c, h: qr[c, h] * qd_ref[h])
        cur = [S[h] for h in heads]
        for c in range(G):
            for h in heads:
                o_ref[rows[c], sls[h]] = av[c, h] + _d(qd[c, h], cur[h])
                sin_ref[h, c] = cur[h].astype(BF16)
            cur = [cur[h] * g_ref[h, 0:1, 0:1] + kv[c, h] for h in heads]
        for h in heads:
            S[h] = cur[h]

    return pl.pallas_call(
        body, grid=(nb,),
        in_specs=[col(3), col(4), col(5), rope, rope,
                  tab(CHUNK, CHUNK), tab(CHUNK, RET_D), tab(CHUNK, RET_D), tab(8, LANES)],
        out_specs=[col(0), st],
        out_shape=[jax.ShapeDtypeStruct((Lp, D_MODEL), F32), jax.ShapeDtypeStruct((RET_H, N, RET_D, RET_D), BF16)],
        scratch_shapes=[pltpu.VMEM((RET_H, RET_D, RET_D), F32)],
        name="ret_chunk_fwd")(proj_m, proj_m, proj_m, cos, sin, dmask, qdec, kdec, gch)


def _ret_chunk_bwd(proj_m, cos, sin, tables, do, s_in):
    Lp = proj_m.shape[0]
    N = Lp // CHUNK
    dmask, qdec, kdec, gch = tables
    G, nb, col, tab, rope, st = _ret_specs(N, True)

    def body(q_ref, k_ref, v_ref, c_ref, s_ref, dm_ref, qd_ref, kd_ref, g_ref, do_ref, sin_ref,
             d_ref, dS):
        n = pl.program_id(0)

        @pl.when(n == 0)
        def _():
            dS[...] = jnp.zeros_like(dS)

        kscale = RET_D ** -0.5
        heads = range(RET_H)
        sls = [slice(h * RET_D, (h + 1) * RET_D) for h in heads]
        rows = [slice(c * CHUNK, (c + 1) * CHUNK) for c in range(G)]
        pairs = [(c, h) for c in range(G) for h in heads]
        P = lambda f: {p: f(*p) for p in pairs}
        cs = [(c_ref[rows[c], :], s_ref[rows[c], :]) for c in range(G)]
        osl = lambda part, h: slice(part * D_MODEL + h * RET_D, part * D_MODEL + (h + 1) * RET_D)
        qr = P(lambda c, h: _rot(q_ref[rows[c], sls[h]], *cs[c]))
        ks = P(lambda c, h: _rot(k_ref[rows[c], sls[h]], *cs[c]) * kscale)
        v = P(lambda c, h: v_ref[rows[c], sls[h]])
        dov = P(lambda c, h: do_ref[rows[c], sls[h]])
        ad = P(lambda c, h: _dnt(qr[c, h], ks[c, h]) * dm_ref[h])
        da = P(lambda c, h: _dnt(dov[c, h], v[c, h]) * dm_ref[h])
        dos = P(lambda c, h: _dnt(dov[c, h], sin_ref[h, c]) * qd_ref[h])
        qdo = P(lambda c, h: _dtn(qr[c, h] * qd_ref[h], dov[c, h]))
        adv = P(lambda c, h: _dtn(ad[c, h], dov[c, h]))
        dqr = P(lambda c, h: _d(da[c, h], ks[c, h]) + dos[c, h])
        daq = P(lambda c, h: _dtn(da[c, h], qr[c, h]))
        kk = P(lambda c, h: ks[c, h] * kd_ref[h])
        cur = [dS[h] for h in heads]
        for c in reversed(range(G)):
            for h in heads:
                d_ref[rows[c], osl(2, h)] = (adv[c, h] + _d(kk[c, h], cur[h])).astype(BF16)
                d_ref[rows[c], osl(0, h)] = _rot_bwd(dqr[c, h], *cs[c]).astype(BF16)
                dks = daq[c, h] + _dnt(v[c, h], cur[h]) * kd_ref[h]
                d_ref[rows[c], osl(1, h)] = _rot_bwd(dks * kscale, *cs[c]).astype(BF16)
            cur = [cur[h] * g_ref[h, 0:1, 0:1] + qdo[c, h] for h in heads]
        for h in heads:
            dS[h] = cur[h]

    return pl.pallas_call(
        body, grid=(nb,),
        in_specs=[col(3), col(4), col(5), rope, rope,
                  tab(CHUNK, CHUNK), tab(CHUNK, RET_D), tab(CHUNK, RET_D), tab(8, LANES), col(0), st],
        out_specs=pl.BlockSpec((G * CHUNK, 3 * D_MODEL), lambda n: (nb - 1 - n, 0)),
        out_shape=jax.ShapeDtypeStruct((Lp, 3 * D_MODEL), BF16),
        scratch_shapes=[pltpu.VMEM((RET_H, RET_D, RET_D), F32)],
        name="ret_chunk_bwd")(proj_m, proj_m, proj_m, cos, sin, dmask, qdec, kdec, gch, do, s_in)


def _merge_specs(tr):
    col = lambda j: pl.BlockSpec((tr, D_MODEL), lambda i: (i, j))
    return col


def _merge_fwd(o_a, o_b, proj_m, gnorm):
    Lp = o_a.shape[0]
    tr = _tile(Lp, 192, 16)

    def body(oa_ref, ob_ref, gz_ref, rg_ref, ga_ref, gb_ref, gn_ref, y_ref):
        gn = gn_ref[...]
        oa = oa_ref[...]
        ob = ob_ref[...]
        gz = gz_ref[...]
        ya = []
        for j in range(GDN_H):
            seg = oa[:, j * GDN_D:(j + 1) * GDN_D]
            r = lax.rsqrt(jnp.mean(seg * seg, axis=-1, keepdims=True) + EPS)
            ya.append(seg * r * gn)
        ya = jnp.concatenate(ya, axis=1) * (gz * _sig(gz))
        yb = []
        for j in range(RET_H):
            seg = ob[:, j * RET_D:(j + 1) * RET_D]
            r = lax.rsqrt(jnp.mean(seg * seg, axis=-1, keepdims=True) + EPS)
            yb.append(seg * r)
        rg = rg_ref[...]
        yb = jnp.concatenate(yb, axis=1) * (rg * _sig(rg))
        y_ref[...] = (_sig(ga_ref[...]) * ya + _sig(gb_ref[...]) * yb).astype(BF16)

    col = _merge_specs(tr)
    return pl.pallas_call(
        body, grid=(Lp // tr,),
        in_specs=[col(0), col(0), col(6), col(7), col(8), col(9), pl.BlockSpec((1, GDN_D), lambda i: (0, 0))],
        out_specs=col(0), out_shape=jax.ShapeDtypeStruct((Lp, D_MODEL), BF16),
        name="merge_fwd")(o_a, o_b, proj_m, proj_m, proj_m, proj_m, gnorm)


def _merge_bwd(dh1b, w_out, o_a, o_b, proj_m, gnorm):
    Lp = o_a.shape[0]
    tr = _tile(Lp, 192, 16)

    def body(d_ref, wo_ref, oa_ref, ob_ref, gz_ref, rg_ref, ga_ref, gb_ref, gn_ref, dc_ref, doa_ref, dob_ref, dgn_ref):
        i = pl.program_id(0)
        gn = gn_ref[...]
        dyv = lax.dot_general(d_ref[...], wo_ref[...], _NT, preferred_element_type=F32)
        oa = oa_ref[...]
        ob = ob_ref[...]
        gz = gz_ref[...]
        rg = rg_ref[...]
        sa = _sig(ga_ref[...])
        sb = _sig(gb_ref[...])
        dya = dyv * sa
        dyb = dyv * sb
        sgz = _sig(gz)
        szz = gz * sgz
        dgn = jnp.zeros((1, GDN_D), F32)
        ya = []
        dgz = []
        for j in range(GDN_H):
            sl = slice(j * GDN_D, (j + 1) * GDN_D)
            seg = oa[:, sl]
            r = lax.rsqrt(jnp.mean(seg * seg, axis=-1, keepdims=True) + EPS)
            xh = seg * r
            oan = xh * gn
            ya.append(oan * szz[:, sl])
            dgz.append(dya[:, sl] * oan * (sgz[:, sl] * (1.0 + gz[:, sl] * (1.0 - sgz[:, sl]))))
            doan = dya[:, sl] * szz[:, sl]
            dgn = dgn + jnp.sum(doan * xh, axis=0, keepdims=True)
            dxh = doan * gn
            doa_ref[:, sl] = r * (dxh - xh * jnp.mean(dxh * xh, axis=-1, keepdims=True))
        ya = jnp.concatenate(ya, axis=1)
        srg = _sig(rg)
        srr = rg * srg
        yb = []
        drg = []
        for j in range(RET_H):
            sl = slice(j * RET_D, (j + 1) * RET_D)
            seg = ob[:, sl]
            r = lax.rsqrt(jnp.mean(seg * seg, axis=-1, keepdims=True) + EPS)
            xh = seg * r
            yb.append(xh * srr[:, sl])
            drg.append(dyb[:, sl] * xh * (srg[:, sl] * (1.0 + rg[:, sl] * (1.0 - srg[:, sl]))))
            dxh = dyb[:, sl] * srr[:, sl]
            dob_ref[:, sl] = r * (dxh - xh * jnp.mean(dxh * xh, axis=-1, keepdims=True))
        yb = jnp.concatenate(yb, axis=1)
        dc_ref[:, 0:D_MODEL] = jnp.concatenate(dgz, axis=1).astype(BF16)
        dc_ref[:, D_MODEL:2 * D_MODEL] = jnp.concatenate(drg, axis=1).astype(BF16)
        dc_ref[:, 2 * D_MODEL:3 * D_MODEL] = (dyv * ya * sa * (1.0 - sa)).astype(BF16)
        dc_ref[:, 3 * D_MODEL:] = (dyv * yb * sb * (1.0 - sb)).astype(BF16)

        @pl.when(i == 0)
        def _():
            dgn_ref[...] = dgn

        @pl.when(i > 0)
        def _():
            dgn_ref[...] += dgn

    col = _merge_specs(tr)
    return pl.pallas_call(
        body, grid=(Lp // tr,),
        in_specs=[col(0), pl.BlockSpec((D_MODEL, D_MODEL), lambda i: (0, 0), pipeline_mode=pl.Buffered(1)),
                  col(0), col(0), col(6), col(7), col(8), col(9), pl.BlockSpec((1, GDN_D), lambda i: (0, 0))],
        out_specs=[pl.BlockSpec((tr, 4 * D_MODEL), lambda i: (i, 0)), col(0), col(0),
                   pl.BlockSpec((1, GDN_D), lambda i: (0, 0))],
        out_shape=[jax.ShapeDtypeStruct((Lp, 4 * D_MODEL), BF16), jax.ShapeDtypeStruct((Lp, D_MODEL), F32),
                   jax.ShapeDtypeStruct((Lp, D_MODEL), F32), jax.ShapeDtypeStruct((1, GDN_D), F32)],
        name="merge_bwd")(dh1b, w_out, o_a, o_b, proj_m, proj_m, proj_m, proj_m, gnorm)


def _ffn_act(up, conv_w, conv_b):
    Lp = up.shape[0]
    tr = _tile(Lp, 192, 16)
    W2 = 2 * D_FF

    def body(main_ref, prev_ref, w_ref, b_ref, act_ref, u_ref):
        i = pl.program_id(0)
        prev = jnp.where(i > 0, prev_ref[...], 0.0)
        ext = jnp.concatenate([prev, main_ref[...]], axis=0)
        u = _taps(_shifted(ext, range(8 - (FFN_CONV - 1), 9)), w_ref[...], tr, b_ref[...])
        a = u[:, :D_FF]
        act_ref[...] = (a * _sig(a) * u[:, D_FF:]).astype(BF16)
        u_ref[...] = u.astype(BF16)

    return pl.pallas_call(
        body, grid=(Lp // tr,),
        in_specs=[pl.BlockSpec((tr, W2), lambda i: (i, 0)), _halo_prev(tr, W2),
                  pl.BlockSpec((FFN_CONV, W2), lambda i: (0, 0)), pl.BlockSpec((1, W2), lambda i: (0, 0))],
        out_specs=[pl.BlockSpec((tr, D_FF), lambda i: (i, 0)), pl.BlockSpec((tr, W2), lambda i: (i, 0))],
        out_shape=[jax.ShapeDtypeStruct((Lp, D_FF), BF16), jax.ShapeDtypeStruct((Lp, W2), BF16)],
        name="ffn_act")(up, up, conv_w, conv_b)


def _ffn_act_bwd(up, u, dact, conv_w):
    Lp = up.shape[0]
    tr = _tile(Lp, 192, 16)
    W2 = 2 * D_FF
    te = tr + 8

    def body(up_ref, u_ref, un_ref, da_ref, dan_ref, w_ref, dup_ref, acc_ref):
        i = pl.program_id(0)
        w = w_ref[...]
        ue = jnp.concatenate([u_ref[...].astype(F32), un_ref[...].astype(F32)[0:8]], axis=0)
        a = ue[:, :D_FF]
        b = ue[:, D_FF:]
        rowe = i * tr + lax.broadcasted_iota(jnp.int32, (te, 1), 0)
        dae = jnp.where(rowe < Lp, jnp.concatenate([da_ref[...], dan_ref[...]], axis=0), 0.0)
        sg = _sig(a)
        du = jnp.concatenate([dae * b * (sg * (1.0 + a * (1.0 - sg))), dae * (a * sg)], axis=1)
        dus = _shifted(du, range(FFN_CONV - 1, -1, -1))
        dup_ref[...] = _taps(dus, w, tr).astype(BF16)
        upm = up_ref[...]
        rows = [jnp.sum(dus[kk][0:tr, :] * upm, axis=0, keepdims=True) for kk in range(FFN_CONV)]
        rows.append(jnp.sum(du[0:tr, :], axis=0, keepdims=True))
        part = jnp.concatenate(rows + [jnp.zeros((8 - len(rows), W2), F32)], axis=0)

        @pl.when(i == 0)
        def _():
            acc_ref[...] = part

        @pl.when(i > 0)
        def _():
            acc_ref[...] += part

    return pl.pallas_call(
        body, grid=(Lp // tr,),
        in_specs=[pl.BlockSpec((tr, W2), lambda i: (i, 0)), pl.BlockSpec((tr, W2), lambda i: (i, 0)),
                  _halo_next(tr, W2, Lp, rows=16), pl.BlockSpec((tr, D_FF), lambda i: (i, 0)), _halo_next(tr, D_FF, Lp),
                  pl.BlockSpec((FFN_CONV, W2), lambda i: (0, 0))],
        out_specs=[pl.BlockSpec((tr, W2), lambda i: (i, 0)), pl.BlockSpec((8, W2), lambda i: (0, 0))],
        out_shape=[jax.ShapeDtypeStruct((Lp, W2), BF16), jax.ShapeDtypeStruct((8, W2), F32)],
        name="ffn_act_bwd")(up, u, u, dact, dact, conv_w)


def _proj_rows(j):
    shift = (jnp.where((j >= 3) & (j < 6), _O_RQ - 3 * D_MODEL, 0) + jnp.where(j == 6, _O_GZ - 6 * D_MODEL, 0)
             + jnp.where(j >= 7, _O_RG - 7 * D_MODEL, 0))
    return j * D_MODEL + shift


def _local_step(hpad, tgt, pad, wt, first_weights=None, late_weights=None, on_ffn_out_grads=None,
                on_w_in_grads=None):
    Lp = hpad.shape[0]
    first = pad + N_META
    pos = jnp.arange(Lp, dtype=F32) - float(pad)
    half = RET_D // 2
    inv = 1.0 / (ROPE_BASE ** (jnp.arange(half, dtype=F32) / half))
    ang = pos[:, None] * inv[None, :]
    cos, sin = jnp.cos(ang), jnp.sin(ang)
    tables = _ret_tables()
    gparams = jnp.zeros((8, LANES), F32).at[0, :GDN_H].set(wt["a_log"]).at[1, :GDN_H].set(wt["dt_bias"])

    hn1 = _rms_fwd(hpad, wt["norm1"], "rms1_fwd")
    if first_weights is not None:
        wt = {**wt, **first_weights(hn1)}
    w_in_t = wt["w_in_t"]
    w_small_t = jnp.pad(w_in_t[_O_GA:_O_RQ], ((0, LANES - 2 * GDN_H), (0, 0)))
    proj_m = _mm_nn(hn1, w_in_t, bt=True, tm_target=2752, b_rows=(D_MODEL, MAIN_W // D_MODEL, _proj_rows),
                    name="proj_main")
    proj_s = _mm_nn(hn1, w_small_t, bt=True, name="proj_small")
    qkv, gsm, conv_out = _gdn_pre(proj_m, proj_s, wt["gdn_conv_w"], gparams, pad)
    o_a, s_a, t_a = _gdn_chunk_fwd(qkv, gsm)
    o_b, s_b = _ret_chunk_fwd(proj_m, cos, sin, tables)
    y = _merge_fwd(o_a, o_b, proj_m, wt["gdn_norm"])
    if late_weights is not None:
        wt = {**wt, **late_weights(y)}
    h1, hn2 = _mm_rms_fwd(_Producer(y, wt["w_out"], hpad), wt["norm2"], "out_proj_rms2")
    up = _mm_nn(hn2, wt["w_up_t"], bt=True, name="ffn_up")
    act, u_ffn = _ffn_act(up, wt["ffn_conv_w"], wt["ffn_conv_b"])
    lossvec, dh2, dh2b, d_norm_f = _final(_Producer(act, wt["w_down"], h1), wt["norm_f"], tgt, first)

    d_w_down = _mm_tn(act, dh2b, name="dw_down")
    dact = _mm_nt(dh2b, wt["w_down"], name="d_act")
    dup, ffn_rows = _ffn_act_bwd(up, u_ffn, dact, wt["ffn_conv_w"])
    d_w_up_t = _mm_tn(dup, hn2, name="dw_up")
    dh1, dh1b, d_norm2 = _rms_bwd(h1, wt["norm2"], _Producer(dup, wt["w_up_t"]), dh2, pad, "d_hn2_rms2_bwd")

    d_w_out = _mm_tn(y, dh1b, name="dw_out")
    gnorm = wt["gdn_norm"]
    if on_ffn_out_grads is not None:
        gnorm = gnorm + on_ffn_out_grads(d_w_down, d_w_up_t, d_w_out)[0:1, :]
    d_c, do_a, do_b, d_gnorm = _merge_bwd(dh1b, wt["w_out"], o_a, o_b, proj_m, gnorm)
    d_r = _ret_chunk_bwd(proj_m, cos, sin, tables, do_b, s_b)
    dq, dk, dv, dgs = _gdn_chunk_bwd(qkv, gsm, do_a, s_a, t_a)
    d_a, d_s, conv_rows, gp_rows = _gdn_pre_bwd(proj_m, conv_out, proj_s, wt["gdn_conv_w"], gparams, dq, dk, dv, dgs,
                                                pad)

    segs = [(d_a, w_in_t[_O_GQ:_O_GZ]), (d_r, w_in_t[_O_RQ:_O_RG]),
            (d_c, jnp.concatenate([w_in_t[_O_GZ:_O_GA], w_in_t[_O_RG:_O_END]], axis=0))]
    pa, pr, pc = [_mm_tn(d, hn1, BF16, name="dw_in_%d" % i) for i, (d, _) in enumerate(segs)]
    ps = _mm_tn(d_s, hn1, BF16, name="dw_in_small")
    d_w_in_t = jnp.concatenate([pa, pc[:D_MODEL], ps[:2 * GDN_H], pr, pc[D_MODEL:]], axis=0)
    if on_w_in_grads is not None:
        w_small_t = w_small_t + on_w_in_grads(d_w_in_t)[0:1, 0:1].astype(w_small_t.dtype)
    dhn1 = _mm_sum([(d_s, w_small_t)] + segs[:-1], "d_hn1_first")
    dh0, _, d_norm1 = _rms_bwd(hpad, wt["norm1"], _Producer(*segs[-1], dhn1), dh1, pad, "d_hn1_rms1_bwd")

    grads = {
        "norm1": d_norm1, "w_in_t": d_w_in_t, "gdn_conv_w": conv_rows[:GDN_CONV],
        "a_log": gp_rows[0, :GDN_H], "dt_bias": gp_rows[1, :GDN_H], "gdn_norm": d_gnorm, "w_out": d_w_out,
        "norm2": d_norm2, "w_up_t": d_w_up_t, "ffn_conv_w": ffn_rows[:FFN_CONV],
        "ffn_conv_b": ffn_rows[FFN_CONV:FFN_CONV + 1], "w_down": d_w_down, "norm_f": d_norm_f,
    }
    return lossvec, dh0, grads


def _peer(k):
    ix, iy, ic = lax.axis_index("x"), lax.axis_index("y"), lax.axis_index("c")
    px = 1 - ix if (k >> 2) & 1 else ix
    py = 1 - iy if (k >> 1) & 1 else iy
    pc = 1 - ic if k & 1 else ic
    return (px, py, pc), 4 * px + 2 * py + pc


def _comm_call(body, n, out_shapes, name, args):
    hbm = pl.BlockSpec(memory_space=pl.ANY)
    return pl.pallas_call(
        body, out_shape=out_shapes, in_specs=[hbm] * n, out_specs=[hbm] * n,
        scratch_shapes=[pltpu.SemaphoreType.DMA((n, N_DEV - 1)), pltpu.SemaphoreType.DMA((n, N_DEV - 1)),
                        pltpu.SemaphoreType.DMA((n,))],
        name=name)(*args)


def _all_gather(xs, name):
    n = len(xs)

    def body(*refs):
        x_refs, out_refs = refs[:n], refs[n:2 * n]
        send_sems, recv_sems, local_sems = refs[2 * n:]
        _, me = _peer(0)
        pending = []
        for i in range(n):
            local = pltpu.make_async_copy(x_refs[i], out_refs[i].at[me], local_sems.at[i])
            local.start()
            pending.append(local)
        sends = []
        for i in range(n):
            for k in range(1, N_DEV):
                dev, _ = _peer(k)
                cp = pltpu.make_async_remote_copy(
                    src_ref=x_refs[i], dst_ref=out_refs[i].at[me], send_sem=send_sems.at[i, k - 1],
                    recv_sem=recv_sems.at[i, k - 1], device_id=dev, device_id_type=MESH_T)
                cp.start()
                sends.append(cp)
        for i in range(n):
            for k in range(1, N_DEV):
                dev, idx = _peer(k)
                pltpu.make_async_remote_copy(
                    src_ref=x_refs[i], dst_ref=out_refs[i].at[idx], send_sem=send_sems.at[i, k - 1],
                    recv_sem=recv_sems.at[i, k - 1], device_id=dev, device_id_type=MESH_T).wait_recv()
        for cp in sends:
            cp.wait_send()
        for local in pending:
            local.wait()

    out_shapes = [jax.ShapeDtypeStruct((N_DEV,) + a.shape, a.dtype) for a in xs]
    return _comm_call(body, n, out_shapes, name, xs)


def _all_to_all(gs, name):
    n = len(gs)

    def body(*refs):
        g_refs, out_refs = refs[:n], refs[n:2 * n]
        send_sems, recv_sems, local_sems = refs[2 * n:]
        _, me = _peer(0)
        pending = []
        for i in range(n):
            local = pltpu.make_async_copy(g_refs[i].at[me], out_refs[i].at[0], local_sems.at[i])
            local.start()
            pending.append(local)
        sends = []
        for i in range(n):
            for k in range(1, N_DEV):
                dev, idx = _peer(k)
                cp = pltpu.make_async_remote_copy(
                    src_ref=g_refs[i].at[idx], dst_ref=out_refs[i].at[k], send_sem=send_sems.at[i, k - 1],
                    recv_sem=recv_sems.at[i, k - 1], device_id=dev, device_id_type=MESH_T)
                cp.start()
                sends.append(cp)
        for cp in sends:
            cp.wait_recv()
        for cp in sends:
            cp.wait_send()
        for local in pending:
            local.wait()

    out_shapes = [jax.ShapeDtypeStruct(g.shape, g.dtype) for g in gs]
    return _comm_call(body, n, out_shapes, name, gs)


_SPLIT_RELATIONS = {"gather": tuple(range(1, N_DEV)), "a2a": tuple(range(1, N_DEV)), "chip": (1, 2, 4, 6),
                    "forward": (2, 4, 6)}


def _split_copies(kind, src_refs, land_refs, send_sems, recv_sems, local_sems, with_recv):
    n = len(land_refs)
    rels = _SPLIT_RELATIONS[kind]
    _, me = _peer(0)
    locals_, remotes = [], []
    for i in range(n):
        if kind in ("gather", "chip"):
            locals_.append(pltpu.make_async_copy(src_refs[i], land_refs[i].at[me], local_sems.at[i]))
        elif kind == "a2a":
            locals_.append(pltpu.make_async_copy(src_refs[i].at[me], land_refs[i].at[0], local_sems.at[i]))
        for jj, k in enumerate(rels):
            dev, idx = _peer(k)
            if kind in ("gather", "chip"):
                src, dst, mine = src_refs[i], land_refs[i].at[me], land_refs[i].at[idx]
            elif kind == "a2a":
                src, dst, mine = src_refs[i].at[idx], land_refs[i].at[k], land_refs[i].at[k]
            else:
                dev, _ = _peer(1)
                _, came = _peer(k + 1)
                src, dst, mine = land_refs[i].at[idx], land_refs[i].at[idx], land_refs[i].at[came]
            j = i * len(rels) + jj
            send = pltpu.make_async_remote_copy(
                src_ref=src, dst_ref=dst, send_sem=send_sems.at[j], recv_sem=recv_sems.at[j],
                device_id=dev, device_id_type=MESH_T)
            recv = pltpu.make_async_remote_copy(
                src_ref=src, dst_ref=mine, send_sem=send_sems.at[j], recv_sem=recv_sems.at[j],
                device_id=dev, device_id_type=MESH_T) if with_recv else None
            remotes.append((send, recv))
    return locals_, remotes


_HBM = pl.BlockSpec(memory_space=pltpu.HBM)
_SEM = pl.BlockSpec(memory_space=pltpu.SEMAPHORE)
_ANY = pl.BlockSpec(memory_space=pl.ANY)


def _split_start(srcs, kind, name, after):
    n = len(srcs)
    if kind == "forward":
        arrays = list(srcs)
    else:
        gathers = kind in ("gather", "chip")
        arrays = list(srcs) + [lax.empty(((N_DEV,) + a.shape) if gathers else a.shape, a.dtype) for a in srcs]
    na = len(arrays)

    def body(*refs):
        src_refs, land_refs = refs[:n], refs[na - n:na]
        send_sems, recv_sems, local_sems = refs[na + 1:na + 4]
        token = refs[-1]
        locals_, remotes = _split_copies(kind, src_refs, land_refs, send_sems, recv_sems, local_sems, False)
        for cp in locals_:
            cp.start()
        for send, _ in remotes:
            send.start()
        token[...] = jnp.zeros_like(token)

    ncp = n * len(_SPLIT_RELATIONS[kind])
    sems = (pltpu.SemaphoreType.DMA((ncp,)), pltpu.SemaphoreType.DMA((ncp,)), pltpu.SemaphoreType.DMA((n,)))
    thru = tuple(pltpu.HBM(a.shape, a.dtype) for a in arrays)
    outs = pl.pallas_call(
        body, name=name,
        out_shape=sems + thru + (jax.ShapeDtypeStruct((8, LANES), F32),),
        in_specs=[_HBM] * na + [_ANY],
        out_specs=[_SEM] * 3 + [_HBM] * na + [pl.BlockSpec(memory_space=pltpu.VMEM)],
        input_output_aliases={i: 3 + i for i in range(na)},
        compiler_params=pltpu.CompilerParams(has_side_effects=pltpu.SideEffectType.DATAFLOW_SIDE_EFFECTING),
    )(*[pltpu.with_memory_space_constraint(a, pltpu.HBM) for a in arrays], after)
    return (kind, n, outs[:3], outs[3:3 + na]), outs[-1]


def _split_wait(handle, name, after):
    kind, n, sems, thru = handle
    na = len(thru)

    def body(*refs):
        src_refs, land_refs = refs[:n], refs[na - n:na]
        send_sems, recv_sems, local_sems = refs[na:na + 3]
        locals_, remotes = _split_copies(kind, src_refs, land_refs, send_sems, recv_sems, local_sems, True)
        for send, recv in remotes:
            send.wait_send()
            recv.wait_recv()
        for cp in locals_:
            cp.wait()

    outs = pl.pallas_call(
        body, name=name, out_shape=tuple(pltpu.HBM(a.shape, a.dtype) for a in thru),
        in_specs=[_HBM] * na + [_SEM] * 3 + [_ANY], out_specs=[_HBM] * na,
        input_output_aliases={i: i for i in range(na)},
        compiler_params=pltpu.CompilerParams(has_side_effects=pltpu.SideEffectType.DATAFLOW_SIDE_EFFECTING),
    )(*thru, *sems, after)
    return list(outs[na - n:])


def _adamw(gslabs, w, m, v, name):
    R, Cw = w.shape
    if R % 8 == 0:
        tr, tc = _tile(R, 64 if Cw > 1024 else 128, 8), Cw
    else:
        tr, tc = R, LANES
    c1 = 1.0 - ADAM_B1 ** ADAM_STEP
    c2 = 1.0 - ADAM_B2 ** ADAM_STEP

    def body(g_ref, w_ref, m_ref, v_ref, go_ref, d_ref, mo_ref, vo_ref):
        g = g_ref[0].astype(F32)
        for k in range(1, N_DEV):
            g = g + g_ref[k].astype(F32)
        mn = ADAM_B1 * m_ref[...] + (1.0 - ADAM_B1) * g
        vn = ADAM_B2 * v_ref[...] + (1.0 - ADAM_B2) * (g * g)
        m_hat = mn / c1
        v_hat = vn / c2
        go_ref[...] = g
        d_ref[...] = -ADAM_LR * (m_hat / (jnp.sqrt(v_hat) + ADAM_EPS) + ADAM_WD * w_ref[...])
        mo_ref[...] = mn
        vo_ref[...] = vn

    blk = pl.BlockSpec((tr, tc), lambda i, j: (i, j))
    return pl.pallas_call(
        body, grid=(R // tr, Cw // tc),
        in_specs=[pl.BlockSpec((N_DEV, tr, tc), lambda i, j: (0, i, j)), blk, blk, blk],
        out_specs=[blk] * 4, out_shape=[jax.ShapeDtypeStruct((R, Cw), F32)] * 4, name=name)(gslabs, w, m, v)


def _pack(arrs, row_mult, dtype=F32):
    parts = []
    total = 0
    for a in arrs:
        f = a.reshape(-1).astype(dtype)
        n = -(-f.shape[0] // 1024) * 1024
        parts.append(jnp.pad(f, (0, n - f.shape[0])))
        total += n
    rows = total // LANES
    rows_p = -(-rows // row_mult) * row_mult
    flat = jnp.concatenate(parts)
    flat = jnp.pad(flat, (0, rows_p * LANES - total))
    return flat.reshape(rows_p, LANES)


def _unpack(packed, shapes):
    lead = packed.shape[:-2]
    flat = packed.reshape(lead + (-1,))
    out = []
    off = 0
    for s in shapes:
        n = int(np.prod(s))
        out.append(flat[..., off:off + n].reshape(lead + tuple(s)))
        off += -(-n // 1024) * 1024
    return out


def _gather_cols(stacked):
    d, r, c = stacked.shape
    return stacked.transpose(1, 0, 2).reshape(r, d * c)


def _scatter_cols(full):
    r, n = full.shape
    return full.reshape(r, N_DEV, n // N_DEV).transpose(1, 0, 2)


def kernel(x, meta, norm1, w_in, gdn_conv_w, gdn_a_log, gdn_dt_bias, gdn_norm, w_out, norm2, w_ffn_up, ffn_conv_w, ffn_conv_b, w_ffn_down, norm_f, loss_target, m_meta, m_norm1, m_w_in, m_gdn_conv_w, m_gdn_a_log, m_gdn_dt_bias, m_gdn_norm, m_w_out, m_norm2, m_w_ffn_up, m_ffn_conv_w, m_ffn_conv_b, m_w_ffn_down, m_norm_f, v_meta, v_norm1, v_w_in, v_gdn_conv_w, v_gdn_a_log, v_gdn_dt_bias, v_gdn_norm, v_w_out, v_norm2, v_w_ffn_up, v_ffn_conv_w, v_ffn_conv_b, v_w_ffn_down, v_norm_f):
    S = x.shape[1]
    L = N_META + S
    pad = (-L) % CHUNK
    Lp = L + pad

    tr_ = lambda a: jnp.swapaxes(a[0], 0, 1)
    big = [tr_(w_in), w_out[0], tr_(w_ffn_up), w_ffn_down[0]]
    small = [meta, gdn_conv_w, ffn_conv_w]
    small_all, = _all_gather([_pack(small, 8)], "gather_small_weights")
    first, first_token = _split_start([big[0].astype(BF16)], "chip", "gather_w_in_start", small_all)
    late, late_token = _split_start([a.astype(BF16) for a in big[1:]], "gather", "gather_late_start", first_token)

    def first_weights(after):
        half = _split_wait(first, "gather_w_in_wait", after)
        second, second_token = _split_start(half, "forward", "gather_w_in_forward_start", after)
        w_in_s, = _split_wait(second, "gather_w_in_forward_wait", second_token)
        return {"w_in_t": w_in_s.reshape(_O_END, D_MODEL)}

    def late_weights(after):
        w_out_s, w_up_s, w_down_s = _split_wait(late, "gather_late_wait", after)
        return {"w_out": w_out_s.reshape(D_MODEL, D_MODEL), "w_up_t": w_up_s.reshape(2 * D_FF, D_MODEL),
                "w_down": w_down_s.reshape(D_FF, D_MODEL)}

    meta_s, gconv_s, fconv_s = _unpack(small_all, [a.shape for a in small])
    wt = {
        "norm1": norm1 + jnp.tile(late_token[0:1, :], (1, D_MODEL // LANES)),
        "gdn_conv_w": _gather_cols(gconv_s[:, 0]), "a_log": gdn_a_log[0], "dt_bias": gdn_dt_bias[0],
        "gdn_norm": gdn_norm, "norm2": norm2, "ffn_conv_w": _gather_cols(fconv_s[:, 0]), "ffn_conv_b": ffn_conv_b,
        "norm_f": norm_f.reshape(1, D_MODEL),
    }
    meta_f = _gather_cols(meta_s)

    pending = {}

    def on_ffn_out_grads(d_w_down, d_w_up_t, d_w_out):
        srcs = [d_w_out.reshape(N_DEV, D_MODEL // N_DEV, D_MODEL), d_w_up_t.reshape(N_DEV, 2 * D_FF // N_DEV, D_MODEL),
                d_w_down.reshape(N_DEV, D_FF // N_DEV, D_MODEL)]
        pending["ffn_out"], token = _split_start(srcs, "a2a", "exchange_ffn_out_start", d_w_out)
        return token

    def on_w_in_grads(d_w_in_t):
        slabs = d_w_in_t.astype(BF16).reshape(N_DEV, _O_END // N_DEV, D_MODEL)
        pending["w_in"], token = _split_start([slabs], "a2a", "exchange_w_in_start", d_w_in_t)
        return token

    head = jnp.concatenate([jnp.zeros((pad, D_MODEL), F32), meta_f], axis=0)
    if S >= 2 * 704:
        hpad = _Rows(x[0], pad + N_META, head)
        tgt = _Rows(loss_target[0], pad + N_META)
    else:
        hpad = jnp.concatenate([head, x[0]], axis=0)
        tgt = jnp.concatenate([jnp.zeros((pad + N_META, D_MODEL), F32), loss_target[0]], axis=0)
    lossvec, dh0, gr = _local_step(hpad, tgt, pad, wt, first_weights, late_weights, on_ffn_out_grads, on_w_in_grads)

    loss = lax.psum(jnp.sum(lossvec), ("x", "y", "c"))
    grad_x = dh0[pad + N_META:][None]

    big_m = [tr_(m_w_in), m_w_out[0], tr_(m_w_ffn_up), m_w_ffn_down[0]]
    big_v = [tr_(v_w_in), v_w_out[0], tr_(v_w_ffn_up), v_w_ffn_down[0]]
    slabs_ffn_out = _split_wait(pending["ffn_out"], "exchange_ffn_out_wait", dh0)
    big_out = [None] + [_adamw(slabs_ffn_out[i - 1], big[i], big_m[i], big_v[i], "adamw_big_%d" % i)
                        for i in range(1, len(big))]
    g_sm = [_scatter_cols(dh0[pad:pad + N_META]), _scatter_cols(gr["gdn_conv_w"]), _scatter_cols(gr["ffn_conv_w"])]
    g_small = jnp.stack([_pack([g[d] for g in g_sm], 8) for d in range(N_DEV)])
    slabs_small, = _all_to_all([g_small], "exchange_small_gradients")
    small_out = _adamw(slabs_small, _pack(small, 8), _pack([m_meta, m_gdn_conv_w, m_ffn_conv_w], 8),
                       _pack([v_meta, v_gdn_conv_w, v_ffn_conv_w], 8), "adamw_small_sharded")
    small_un = [_unpack(o, [a.shape for a in small]) for o in small_out]
    rep_w = [norm1, gdn_a_log, gdn_dt_bias, gdn_norm, norm2, ffn_conv_b, norm_f]
    rep_m = [m_norm1, m_gdn_a_log, m_gdn_dt_bias, m_gdn_norm, m_norm2, m_ffn_conv_b, m_norm_f]
    rep_v = [v_norm1, v_gdn_a_log, v_gdn_dt_bias, v_gdn_norm, v_norm2, v_ffn_conv_b, v_norm_f]
    rep_g = [gr["norm1"], gr["a_log"], gr["dt_bias"], gr["gdn_norm"], gr["norm2"], gr["ffn_conv_b"], gr["norm_f"]]
    rep_slabs, = _all_gather([_pack(rep_g, 8)], "gather_small_gradients")
    rep_out = _adamw(rep_slabs, _pack(rep_w, 8), _pack(rep_m, 8), _pack(rep_v, 8), "adamw_replicated")
    rep_shapes = [a.shape for a in rep_w]
    rp_g, rp_d, rp_nm, rp_nv = [_unpack(o, rep_shapes) for o in rep_out]

    slabs_w_in, = _split_wait(pending["w_in"], "exchange_w_in_wait", rep_out[0])
    big_out[0] = _adamw(slabs_w_in, big[0], big_m[0], big_v[0], "adamw_big_0")
    back = lambda a: jnp.swapaxes(a, 0, 1)[None]
    sh_g, sh_d, sh_nm, sh_nv = [
        [small_un[j][0], back(big_out[0][j]), small_un[j][1], big_out[1][j][None], back(big_out[2][j]),
         small_un[j][2], big_out[3][j][None]] for j in range(4)]

    def order(sh, rp):
        return [sh[0], rp[0], sh[1], sh[2], rp[1], rp[2], rp[3], sh[3], rp[4], sh[4], sh[5], rp[5], sh[6], rp[6]]

    return (loss, grad_x, *order(sh_g, rp_g), *order(sh_d, rp_d), *order(sh_nm, rp_nm), *order(sh_nv, rp_nv))
```

```python
import functools
import math

import numpy as np
import jax
import jax.numpy as jnp
from jax import lax
from jax.experimental import pallas as pl
from jax.experimental.pallas import tpu as pltpu

F32 = jnp.float32
BF16 = jnp.bfloat16

D_MODEL = 1024
N_META = 16
CHUNK = 64
GDN_H = 8
GDN_D = 128
RET_H = 4
RET_D = 256
D_FF = 2816
GDN_CONV = 4
FFN_CONV = 3
ROPE_BASE = 10000.0
EPS = 1e-6
N_DEV = 8
LANES = 128
MAIN_W = 10 * 1024
_O_GQ, _O_GZ, _O_GA, _O_RQ, _O_RG, _O_GATE, _O_END = 0, 3072, 4096, 4112, 7184, 8208, 10256

ADAM_LR = 0.001
ADAM_B1 = 0.9
ADAM_B2 = 0.999
ADAM_EPS = 1e-08
ADAM_WD = 0.01
ADAM_STEP = 10

MESH_T = pl.DeviceIdType.MESH


def _tile(n, target, mult):
    best = None
    for d in range(mult, min(n, target) + 1, mult):
        if n % d == 0:
            best = d
    assert best is not None, (n, target, mult)
    return best


def _sig(x):
    return 0.5 * jnp.tanh(0.5 * x) + 0.5


def _d(a, b):
    return jnp.dot(a.astype(BF16), b.astype(BF16), preferred_element_type=F32)


def _dnt(a, b):
    return lax.dot_general(a.astype(BF16), b.astype(BF16), (((1,), (1,)), ((), ())), preferred_element_type=F32)


def _dtn(a, b):
    return lax.dot_general(a.astype(BF16), b.astype(BF16), (((0,), (0,)), ((), ())), preferred_element_type=F32)


def _dxg(a, b, dims):
    f = functools.partial(lax.dot_general, dimension_numbers=dims, preferred_element_type=F32)
    ab = a.astype(BF16)
    b1 = b.astype(BF16)
    r1 = b - b1.astype(F32)
    b2 = r1.astype(BF16)
    b3 = (r1 - b2.astype(F32)).astype(BF16)
    return f(ab, b1) + (f(ab, b2) + f(ab, b3))


def _dx(a, b):
    return _dxg(a, b, (((1,), (0,)), ((), ())))


def _dxnt(a, b):
    return _dxg(a, b, (((1,), (1,)), ((), ())))


def _split(a):
    hi = a.astype(BF16)
    return hi, (a - hi.astype(F32)).astype(BF16)


def _d3g(a, b, dims):
    ah, al = _split(a)
    bh, bl = _split(b)
    f = functools.partial(lax.dot_general, dimension_numbers=dims, preferred_element_type=F32)
    if dims == _NN:
        rows = a.shape[0]
        both = f(jnp.concatenate([ah, al], axis=0), bh)
        return both[:rows] + (f(ah, bl) + both[rows:])
    return f(ah, bh) + (f(ah, bl) + f(al, bh))


def _d2x(a, b, dims):
    ah, al = _split(a)
    f = functools.partial(lax.dot_general, dimension_numbers=dims, preferred_element_type=F32)
    bb = b.astype(BF16)
    return f(ah, bb) + f(al, bb)


_NN = (((1,), (0,)), ((), ()))
_NT = (((1,), (1,)), ((), ()))
_TN = (((0,), (0,)), ((), ()))


def _rowsum(x):
    return jnp.sum(x, axis=1, keepdims=True)


def _allsum(x):
    return jnp.sum(jnp.sum(x, axis=1, keepdims=True), axis=0, keepdims=True)


def _mm_nn(a, b, res=None, out_dtype=F32, bt=False, tm_target=704, b_rows=None, name="mm_nn"):
    M, K = a.shape
    N = b.shape[0] if bt else b.shape[1]
    tm = _tile(M, tm_target, 16)
    if b_rows is None:
        tn = _tile(N, 2816, 128)
    else:
        tn, n_tiles, start = b_rows
        N = tn * n_tiles

    def body(*refs):
        if res is None:
            a_ref, b_ref, o_ref = refs
        else:
            a_ref, b_ref, r_ref, o_ref = refs
        acc = lax.dot_general(a_ref[...], b_ref[...], _NT if bt else _NN, preferred_element_type=F32)
        if res is not None:
            acc = acc + r_ref[...]
        o_ref[...] = acc.astype(out_dtype)

    b_spec = pl.BlockSpec((tn, K), lambda j, i: (j, 0)) if bt else pl.BlockSpec((K, tn), lambda j, i: (0, j))
    if b_rows is not None:
        b_spec = pl.BlockSpec((pl.Element(tn), pl.Element(K)), lambda j, i: (pl.multiple_of(start(j), 16), 0))
    in_specs = [pl.BlockSpec((tm, K), lambda j, i: (i, 0)), b_spec]
    args = [a, b]
    if res is not None:
        in_specs.append(pl.BlockSpec((tm, tn), lambda j, i: (i, j)))
        args.append(res)
    return pl.pallas_call(
        body, grid=(N // tn, M // tm), in_specs=in_specs,
        out_specs=pl.BlockSpec((tm, tn), lambda j, i: (i, j)),
        out_shape=jax.ShapeDtypeStruct((M, N), out_dtype), name=name)(*args)


def _mm_sum(pairs, name):
    M = pairs[0][0].shape[0]
    N = pairs[0][1].shape[1]
    tm = _tile(M, 704, 16)
    n = len(pairs)

    def body(*refs):
        o_ref = refs[-1]
        acc = jnp.dot(refs[0][...], refs[1][...], preferred_element_type=F32)
        for i in range(1, n):
            acc = acc + jnp.dot(refs[2 * i][...], refs[2 * i + 1][...], preferred_element_type=F32)
        o_ref[...] = acc

    specs, args = [], []
    for a, b in pairs:
        specs += [pl.BlockSpec((tm, a.shape[1]), lambda i: (i, 0)),
                  pl.BlockSpec(b.shape, lambda i: (0, 0), pipeline_mode=pl.Buffered(1))]
        args += [a, b]
    return pl.pallas_call(
        body, grid=(M // tm,), in_specs=specs, out_specs=pl.BlockSpec((tm, N), lambda i: (i, 0)),
        out_shape=jax.ShapeDtypeStruct((M, N), F32), name=name)(*args)


def _mm_nt(a, b, res=None, name="mm_nt"):
    M, Nc = a.shape
    K = b.shape[0]
    tm = _tile(M, 704, 16)
    tc = _tile(Nc, 5632, 128)

    def body(*refs):
        if res is None:
            a_ref, b_ref, o_ref = refs
        else:
            a_ref, b_ref, r_ref, o_ref = refs
        c = pl.program_id(1)
        p = lax.dot_general(a_ref[...], b_ref[...], (((1,), (1,)), ((), ())), preferred_element_type=F32)

        @pl.when(c == 0)
        def _():
            if res is None:
                o_ref[...] = p
            else:
                o_ref[...] = p + r_ref[...]

        @pl.when(c > 0)
        def _():
            o_ref[...] += p

    in_specs = [pl.BlockSpec((tm, tc), lambda i, c: (i, c)), pl.BlockSpec((K, tc), lambda i, c: (0, c))]
    args = [a, b]
    if res is not None:
        in_specs.append(pl.BlockSpec((tm, K), lambda i, c: (i, 0)))
        args.append(res)
    return pl.pallas_call(
        body, grid=(M // tm, Nc // tc), in_specs=in_specs,
        out_specs=pl.BlockSpec((tm, K), lambda i, c: (i, 0)),
        out_shape=jax.ShapeDtypeStruct((M, K), F32), name=name)(*args)


def _mm_tn(a, b, out_dtype=F32, name="mm_tn"):
    M, K = a.shape
    N = b.shape[1]
    tm = _tile(M, 2752, 16)
    tk = _tile(K, 1408, 128)
    tn = _tile(N, 1408, 128)
    steps = M // tm

    def body(a_ref, b_ref, o_ref, *scratch):
        acc = scratch[0] if scratch else o_ref
        m = pl.program_id(2)
        p = lax.dot_general(a_ref[...], b_ref[...], (((0,), (0,)), ((), ())), preferred_element_type=F32)

        @pl.when(m == 0)
        def _():
            acc[...] = p

        @pl.when(m > 0)
        def _():
            acc[...] += p

        if scratch:
            @pl.when(m == steps - 1)
            def _():
                o_ref[...] = acc[...].astype(out_dtype)

    return pl.pallas_call(
        body, grid=(K // tk, N // tn, steps),
        in_specs=[pl.BlockSpec((tm, tk), lambda kk, j, m: (m, kk)), pl.BlockSpec((tm, tn), lambda kk, j, m: (m, j))],
        out_specs=pl.BlockSpec((tk, tn), lambda kk, j, m: (kk, j)),
        out_shape=jax.ShapeDtypeStruct((K, N), out_dtype),
        scratch_shapes=[] if out_dtype == F32 else [pltpu.VMEM((tk, tn), F32)], name=name)(a, b)


class _Rows:
    def __init__(self, body, first, head=None):
        self.body, self.first, self.head = body, first, head
        self.shape = (body.shape[0] + first, body.shape[1])


def _rows_operands(x, tr):
    if not isinstance(x, _Rows):
        return [x], [pl.BlockSpec((tr, x.shape[1]), lambda i: (i, 0))]
    assert x.first % 8 == 0 and x.first <= tr <= x.body.shape[0] and x.shape[0] % tr == 0
    width = x.shape[1]
    args = [x.body]
    specs = [pl.BlockSpec((pl.Element(tr), pl.Element(width)),
                          lambda i: (pl.multiple_of(jnp.maximum(i * tr - x.first, 0), 8), 0))]
    if x.head is not None:
        args.append(jnp.pad(x.head, ((0, tr - x.first), (0, 0))))
        specs.append(pl.BlockSpec((tr, width), lambda i: (0, 0)))
    return args, specs


def _rows_tile(x, refs, i, tr):
    blk = refs[0][...]
    if not isinstance(x, _Rows):
        return blk
    shifted = pltpu.roll(blk, x.first, 0)
    if x.head is not None:
        row = lax.broadcasted_iota(jnp.int32, (tr, 1), 0)
        shifted = jnp.where(row < x.first, refs[1][...], shifted)
    return jnp.where(i == 0, shifted, blk)


def _rms_fwd(x, g, name):
    Lp = x.shape[0]
    tr = _tile(Lp, 256, 16)
    args, specs = _rows_operands(x, tr)
    n = len(args)

    def body(*refs):
        g_ref, o_ref = refs[n:]
        xv = _rows_tile(x, refs[:n], pl.program_id(0), tr)
        r = lax.rsqrt(jnp.mean(xv * xv, axis=-1, keepdims=True) + EPS)
        o_ref[...] = (xv * r * g_ref[...]).astype(BF16)

    return pl.pallas_call(
        body, grid=(Lp // tr,),
        in_specs=specs + [pl.BlockSpec((1, D_MODEL), lambda i: (0, 0))],
        out_specs=pl.BlockSpec((tr, D_MODEL), lambda i: (i, 0)),
        out_shape=jax.ShapeDtypeStruct((Lp, D_MODEL), BF16), name=name)(*args, g)


class _Producer:
    def __init__(self, a, b, res=None):
        self.a, self.b, self.res = a, b, res
        self.tr = _tile(a.shape[0], 704, 16)
        K = a.shape[1]
        r_args, r_specs = ([], []) if res is None else _rows_operands(res, self.tr)
        self.args = [a, b] + r_args
        self.specs = [pl.BlockSpec((self.tr, K), lambda i: (i, 0)),
                      pl.BlockSpec((K, D_MODEL), lambda i: (0, 0), pipeline_mode=pl.Buffered(1))] + r_specs

    def tile(self, refs, i):
        acc = jnp.dot(refs[0][...], refs[1][...], preferred_element_type=F32)
        return acc if self.res is None else acc + _rows_tile(self.res, refs[2:], i, self.tr)


def _mm_rms_fwd(prod, g, name):
    Lp, tr, n = prod.a.shape[0], prod.tr, len(prod.args)

    def body(*refs):
        g_ref, x_ref, o_ref = refs[n:]
        xv = prod.tile(refs[:n], pl.program_id(0))
        r = lax.rsqrt(jnp.mean(xv * xv, axis=-1, keepdims=True) + EPS)
        x_ref[...] = xv
        o_ref[...] = (xv * r * g_ref[...]).astype(BF16)

    blk = pl.BlockSpec((tr, D_MODEL), lambda i: (i, 0))
    return pl.pallas_call(
        body, grid=(Lp // tr,), in_specs=prod.specs + [pl.BlockSpec((1, D_MODEL), lambda i: (0, 0))],
        out_specs=[blk, blk],
        out_shape=[jax.ShapeDtypeStruct((Lp, D_MODEL), F32), jax.ShapeDtypeStruct((Lp, D_MODEL), BF16)],
        name=name)(*prod.args, g)


def _rms_bwd(x, g, dy, dres, pad, name):
    Lp = x.shape[0]
    fused = isinstance(dy, _Producer)
    tr = dy.tr if fused else _tile(Lp, 256, 16)
    n = len(dy.args) if fused else 1
    x_args, x_specs = _rows_operands(x, tr)
    nx = len(x_args)

    def body(*refs):
        g_ref, dr_ref, dx_ref, dxb_ref, dg_ref = refs[n + nx:]
        i = pl.program_id(0)
        xv = _rows_tile(x, refs[n:n + nx], i, tr)
        r = lax.rsqrt(jnp.mean(xv * xv, axis=-1, keepdims=True) + EPS)
        xh = xv * r
        dyv = dy.tile(refs[:n], i) if fused else refs[0][...]
        dxh = dyv * g_ref[...]
        dx = r * (dxh - xh * jnp.mean(dxh * xh, axis=-1, keepdims=True)) + dr_ref[...]
        row = i * tr + lax.broadcasted_iota(jnp.int32, (tr, 1), 0)
        dx = jnp.where(row >= pad, dx, 0.0)
        dx_ref[...] = dx
        dxb_ref[...] = dx.astype(BF16)
        part = jnp.sum(dyv * xh, axis=0, keepdims=True)

        @pl.when(i == 0)
        def _():
            dg_ref[...] = part

        @pl.when(i > 0)
        def _():
            dg_ref[...] += part

    blk = pl.BlockSpec((tr, D_MODEL), lambda i: (i, 0))
    vec = pl.BlockSpec((1, D_MODEL), lambda i: (0, 0))
    return pl.pallas_call(
        body, grid=(Lp // tr,), in_specs=(dy.specs if fused else [blk]) + x_specs + [vec, blk],
        out_specs=[blk, blk, vec],
        out_shape=[jax.ShapeDtypeStruct((Lp, D_MODEL), F32), jax.ShapeDtypeStruct((Lp, D_MODEL), BF16),
                   jax.ShapeDtypeStruct((1, D_MODEL), F32)], name=name)(*(dy.args if fused else [dy]), *x_args, g, dres)


def _final(h2, g, tgt, first_row):
    fused = isinstance(h2, _Producer)
    Lp = h2.a.shape[0] if fused else h2.shape[0]
    tr = h2.tr if fused else _tile(Lp, 256, 16)
    n = len(h2.args) if fused else 1
    t_args, t_specs = _rows_operands(tgt, tr)
    nt = len(t_args)

    def body(*refs):
        g_ref = refs[n]
        loss_ref, dx_ref, dxb_ref, dg_ref = refs[n + 1 + nt:]
        i = pl.program_id(0)
        xv = h2.tile(refs[:n], i) if fused else refs[0][...]
        tv = _rows_tile(tgt, refs[n + 1:n + 1 + nt], i, tr)
        gv = g_ref[...]
        r = lax.rsqrt(jnp.mean(xv * xv, axis=-1, keepdims=True) + EPS)
        xh = xv * r
        row = i * tr + lax.broadcasted_iota(jnp.int32, (tr, 1), 0)
        err = jnp.where(row >= first_row, xh * gv - tv, 0.0)
        lpart = jnp.sum(err * err, axis=0, keepdims=True) * (0.5 / D_MODEL)
        dyv = err * (1.0 / D_MODEL)
        dxh = dyv * gv
        dx = r * (dxh - xh * jnp.mean(dxh * xh, axis=-1, keepdims=True))
        dx_ref[...] = dx
        dxb_ref[...] = dx.astype(BF16)
        part = jnp.sum(dyv * xh, axis=0, keepdims=True)

        @pl.when(i == 0)
        def _():
            dg_ref[...] = part
            loss_ref[...] = lpart

        @pl.when(i > 0)
        def _():
            dg_ref[...] += part
            loss_ref[...] += lpart

    blk = pl.BlockSpec((tr, D_MODEL), lambda i: (i, 0))
    vec = pl.BlockSpec((1, D_MODEL), lambda i: (0, 0))
    return pl.pallas_call(
        body, grid=(Lp // tr,), in_specs=(h2.specs if fused else [blk]) + [vec] + t_specs,
        out_specs=[vec, blk, blk, vec],
        out_shape=[jax.ShapeDtypeStruct((1, D_MODEL), F32), jax.ShapeDtypeStruct((Lp, D_MODEL), F32),
                   jax.ShapeDtypeStruct((Lp, D_MODEL), BF16), jax.ShapeDtypeStruct((1, D_MODEL), F32)],
        name="final_norm_loss")(*(h2.args if fused else [h2]), g, *t_args)


def _halo_prev(tr, width, col=0):
    return pl.BlockSpec((8, width), lambda i: (jnp.maximum(i * (tr // 8) - 1, 0), col))


def _halo_next(tr, width, nrows, col=0, rows=8):
    last = nrows // rows - 1
    return pl.BlockSpec((rows, width), lambda i: (jnp.minimum((i + 1) * (tr // rows), last), col))


def _shifted(x, offs):
    n = x.shape[0]
    return [x if off == 0 else pltpu.roll(x, n - off, 0) for off in offs]


def _taps(wins, w, rows, bias=None):
    acc = w[0:1, :] * wins[0][0:rows, :]
    if bias is not None:
        acc = acc + bias
    for kk in range(1, len(wins)):
        acc = acc + w[kk:kk + 1, :] * wins[kk][0:rows, :]
    return acc


def _gdn_pre(proj_m, proj_s, conv_w, gparams, pad):
    Lp = proj_m.shape[0]
    tr = _tile(Lp, 192, 64)
    W3 = 3 * D_MODEL

    def body(main_ref, prev_ref, s_ref, w_ref, gp_ref, qkv_ref, gsm_ref, c_ref):
        i = pl.program_id(0)
        prev = jnp.where(i > 0, prev_ref[...], 0.0)
        ext = jnp.concatenate([prev, main_ref[...]], axis=0)
        c = _taps(_shifted(ext, range(8 - (GDN_CONV - 1), 9)), w_ref[...], tr)
        c_ref[...] = c.astype(BF16)
        s = c * _sig(c)
        scale = GDN_D ** -0.5
        for j in range(2 * GDN_H):
            seg = s[:, j * GDN_D:(j + 1) * GDN_D]
            r = lax.rsqrt(_rowsum(seg * seg) + EPS)
            if j < GDN_H:
                r = r * scale
            qkv_ref[:, j * GDN_D:(j + 1) * GDN_D] = seg * r
        qkv_ref[:, 2 * D_MODEL:] = s[:, 2 * D_MODEL:]
        sm = s_ref[...]
        gp = gp_ref[...]
        lane = lax.broadcasted_iota(jnp.int32, sm.shape, 1)
        z = sm + gp[1:2, :]
        softplus = jnp.maximum(z, 0.0) + jnp.log(1.0 + jnp.exp(-jnp.abs(z)))
        lg = -jnp.exp(gp[0:1, :]) * softplus
        row = i * tr + lax.broadcasted_iota(jnp.int32, (tr, 1), 0)
        out = jnp.where(lane < GDN_H, lg, jnp.where(lane < 2 * GDN_H, _sig(sm), 0.0))
        gsm_ref[...] = jnp.where(row >= pad, out, 0.0)

    return pl.pallas_call(
        body, grid=(Lp // tr,),
        in_specs=[pl.BlockSpec((tr, W3), lambda i: (i, 0)), _halo_prev(tr, W3),
                  pl.BlockSpec((tr, LANES), lambda i: (i, 0)),
                  pl.BlockSpec((GDN_CONV, W3), lambda i: (0, 0)), pl.BlockSpec((8, LANES), lambda i: (0, 0))],
        out_specs=[pl.BlockSpec((tr, W3), lambda i: (i, 0)), pl.BlockSpec((tr, LANES), lambda i: (i, 0)),
                   pl.BlockSpec((tr, W3), lambda i: (i, 0))],
        out_shape=[jax.ShapeDtypeStruct((Lp, W3), F32), jax.ShapeDtypeStruct((Lp, LANES), F32),
                   jax.ShapeDtypeStruct((Lp, W3), BF16)],
        name="gdn_pre")(proj_m, proj_m, proj_s, conv_w, gparams)


def _gdn_pre_bwd(proj_m, conv_out, proj_s, conv_w, gparams, dq, dk, dv, dgs, pad):
    Lp = proj_m.shape[0]
    tr = _tile(Lp, 192, 64)
    W3 = 3 * D_MODEL
    te = tr + 8

    def body(main_ref, c_ref, cn_ref, s_ref, w_ref, gp_ref,
             dq_ref, dqn_ref, dk_ref, dkn_ref, dv_ref, dvn_ref, dgs_ref,
             da_ref, ds_ref, dw_ref, dgp_ref):
        i = pl.program_id(0)
        w = w_ref[...]
        c = jnp.concatenate([c_ref[...].astype(F32), cn_ref[...].astype(F32)[0:8]], axis=0)
        sg = _sig(c)
        s = c * sg
        rowe = i * tr + lax.broadcasted_iota(jnp.int32, (te, 1), 0)
        live = (rowe >= pad) & (rowe < Lp)
        dqe = jnp.concatenate([dq_ref[...], dqn_ref[...]], axis=0)
        dke = jnp.concatenate([dk_ref[...], dkn_ref[...]], axis=0)
        dve = jnp.concatenate([dv_ref[...], dvn_ref[...]], axis=0)
        scale = GDN_D ** -0.5
        parts = []
        for j in range(2 * GDN_H):
            seg = s[:, j * GDN_D:(j + 1) * GDN_D]
            r = lax.rsqrt(_rowsum(seg * seg) + EPS)
            xh = seg * r
            if j < GDN_H:
                dxh = dqe[:, j * GDN_D:(j + 1) * GDN_D] * scale
            else:
                dxh = dke[:, (j - GDN_H) * GDN_D:(j - GDN_H + 1) * GDN_D]
            parts.append(r * (dxh - xh * _rowsum(dxh * xh)))
        parts.append(dve)
        dsv = jnp.concatenate(parts, axis=1)
        dc = jnp.where(live, dsv * (sg * (1.0 + c * (1.0 - sg))), 0.0)
        dcs = _shifted(dc, range(GDN_CONV - 1, -1, -1))
        da_ref[...] = _taps(dcs, w, tr).astype(BF16)
        pm = main_ref[...]
        rows = [jnp.sum(dcs[kk][0:tr, :] * pm, axis=0, keepdims=True) for kk in range(GDN_CONV)]
        dwp = jnp.concatenate(rows + [jnp.zeros((8 - GDN_CONV, W3), F32)], axis=0)

        sm = s_ref[...]
        gp = gp_ref[...]
        lane = lax.broadcasted_iota(jnp.int32, sm.shape, 1)
        rowm = i * tr + lax.broadcasted_iota(jnp.int32, (tr, 1), 0)
        dgv = jnp.where(rowm >= pad, dgs_ref[...], 0.0)
        dlg = jnp.where(lane < GDN_H, dgv, 0.0)
        dbt = jnp.where((lane >= GDN_H) & (lane < 2 * GDN_H), dgv, 0.0)
        z = sm + gp[1:2, :]
        softplus = jnp.maximum(z, 0.0) + jnp.log(1.0 + jnp.exp(-jnp.abs(z)))
        ea = jnp.exp(gp[0:1, :])
        dz = dlg * (-ea) * _sig(z)
        dal = dlg * (-ea) * softplus
        bt = _sig(sm)
        dgb = dbt * bt * (1.0 - bt)
        ds_ref[...] = (dz + dgb).astype(BF16)
        gpp = jnp.concatenate([jnp.sum(dal, axis=0, keepdims=True), jnp.sum(dz, axis=0, keepdims=True),
                               jnp.zeros((6, LANES), F32)], axis=0)

        @pl.when(i == 0)
        def _():
            dw_ref[...] = dwp
            dgp_ref[...] = gpp

        @pl.when(i > 0)
        def _():
            dw_ref[...] += dwp
            dgp_ref[...] += gpp

    m3 = pl.BlockSpec((tr, W3), lambda i: (i, 0))
    m1 = pl.BlockSpec((tr, D_MODEL), lambda i: (i, 0))
    n1 = _halo_next(tr, D_MODEL, Lp)
    return pl.pallas_call(
        body, grid=(Lp // tr,),
        in_specs=[m3, m3, _halo_next(tr, W3, Lp, rows=16), pl.BlockSpec((tr, LANES), lambda i: (i, 0)),
                  pl.BlockSpec((GDN_CONV, W3), lambda i: (0, 0)), pl.BlockSpec((8, LANES), lambda i: (0, 0)),
                  m1, n1, m1, n1, m1, n1, pl.BlockSpec((tr, LANES), lambda i: (i, 0))],
        out_specs=[m3, pl.BlockSpec((tr, LANES), lambda i: (i, 0)),
                   pl.BlockSpec((8, W3), lambda i: (0, 0)), pl.BlockSpec((8, LANES), lambda i: (0, 0))],
        out_shape=[jax.ShapeDtypeStruct((Lp, W3), BF16), jax.ShapeDtypeStruct((Lp, LANES), BF16),
                   jax.ShapeDtypeStruct((8, W3), F32), jax.ShapeDtypeStruct((8, LANES), F32)],
        name="gdn_pre_bwd")(proj_m, conv_out, conv_out, proj_s, conv_w, gparams, dq, dq, dk, dk, dv, dv, dgs)


def _gdn_gates(gs):
    ri = lax.broadcasted_iota(jnp.int32, (CHUNK, CHUNK), 0)
    ci = lax.broadcasted_iota(jnp.int32, (CHUNK, CHUNK), 1)
    tril = ri >= ci
    strict = ri > ci
    gall = _dx(tril.astype(F32), gs)
    lane8 = lax.broadcasted_iota(jnp.int32, (8, LANES), 1)
    sub8 = lax.broadcasted_iota(jnp.int32, (8, LANES), 0)
    grow = _dxnt((lane8 == sub8).astype(F32), gall)
    return gall, grow, tril, strict


def _gdn_decay(gall, grow, tril, h):
    g = gall[:, h:h + 1]
    return g, jnp.where(tril, jnp.exp(jnp.where(tril, g - grow[h:h + 1, :], 0.0)), 0.0)


def _group(N):
    return 3 if N % 3 == 0 else (2 if N % 2 == 0 else 1)


def _gdn_chunk_specs(N, rev):
    G = _group(N)
    nb = N // G
    cn = (lambda n: nb - 1 - n) if rev else (lambda n: n)
    col = lambda j: pl.BlockSpec((G * CHUNK, D_MODEL), lambda n: (cn(n), j))
    gate = pl.BlockSpec((G * CHUNK, LANES), lambda n: (cn(n), 0))
    st = lambda a, b: pl.BlockSpec((GDN_H, G, a, b), lambda n: (0, cn(n), 0, 0))
    return G, nb, col, gate, st


def _gdn_chunk_fwd(qkv, gsm):
    Lp = qkv.shape[0]
    N = Lp // CHUNK
    G, nb, col, gate, st = _gdn_chunk_specs(N, False)

    def body(q_ref, k_ref, v_ref, gs_ref, o_ref, sin_ref, t_ref, S):
        n = pl.program_id(0)

        @pl.when(n == 0)
        def _():
            S[...] = jnp.zeros_like(S)

        ri = lax.broadcasted_iota(jnp.int32, (CHUNK, CHUNK), 0)
        ci = lax.broadcasted_iota(jnp.int32, (CHUNK, CHUNK), 1)
        eye = (ri == ci).astype(F32)
        heads = range(GDN_H)
        sls = [slice(h * GDN_D, (h + 1) * GDN_D) for h in heads]
        rows = [slice(c * CHUNK, (c + 1) * CHUNK) for c in range(G)]
        pairs = [(c, h) for c in range(G) for h in heads]
        P = lambda f: {p: f(*p) for p in pairs}
        gs = [gs_ref[rows[c], :] for c in range(G)]
        gates = [_gdn_gates(gs[c]) for c in range(G)]
        tril, strict = gates[0][2], gates[0][3]
        q = P(lambda c, h: q_ref[rows[c], sls[h]])
        k = P(lambda c, h: k_ref[rows[c], sls[h]])
        v = P(lambda c, h: v_ref[rows[c], sls[h]])
        beta = P(lambda c, h: gs[c][:, GDN_H + h:GDN_H + h + 1])
        gg = P(lambda c, h: _gdn_decay(gates[c][0], gates[c][1], tril, h))
        g = {p: x[0] for p, x in gg.items()}
        gam = {p: x[1] for p, x in gg.items()}
        eg = P(lambda c, h: jnp.exp(g[c, h]))
        gl = P(lambda c, h: g[c, h][CHUNK - 1:CHUNK, :])
        kb = P(lambda c, h: k[c, h] * beta[c, h])
        pw = P(lambda c, h: -jnp.where(strict, _dnt(kb[c, h], k[c, h]) * gam[c, h], 0.0))
        p = P(lambda c, h: _dnt(q[c, h], k[c, h]) * gam[c, h])
        t = P(lambda c, h: eye + pw[c, h])
        for it in range(5):
            mm = _d3g if it < 2 else (lambda a, b, dims: _d(a, b))
            pw = P(lambda c, h: mm(pw[c, h], pw[c, h], _NN))
            t = P(lambda c, h: t[c, h] + mm(t[c, h], pw[c, h], _NN))
        u = P(lambda c, h: _d(t[c, h], v[c, h] * beta[c, h]))
        w = P(lambda c, h: _d(t[c, h], kb[c, h] * eg[c, h]))
        qg = P(lambda c, h: q[c, h] * eg[c, h])
        kd = P(lambda c, h: k[c, h] * jnp.exp(gl[c, h] - g[c, h]))
        egl = P(lambda c, h: jnp.exp(gl[c, h]))
        for c in range(G):
            for h in heads:
                t_ref[h, c] = t[c, h]
        cur = [S[h] for h in heads]
        for c in range(G):
            vnew = [u[c, h] - _d(w[c, h], cur[h]) for h in heads]
            for h in heads:
                o_ref[rows[c], sls[h]] = _d(qg[c, h], cur[h]) + _d(p[c, h], vnew[h])
                sin_ref[h, c] = cur[h]
            cur = [cur[h] * egl[c, h] + _dtn(kd[c, h], vnew[h]) for h in heads]
        for h in heads:
            S[h] = cur[h]

    return pl.pallas_call(
        body, grid=(nb,),
        in_specs=[col(0), col(1), col(2), gate],
        out_specs=[col(0), st(GDN_D, GDN_D), st(CHUNK, CHUNK)],
        out_shape=[jax.ShapeDtypeStruct((Lp, D_MODEL), F32), jax.ShapeDtypeStruct((GDN_H, N, GDN_D, GDN_D), F32),
                   jax.ShapeDtypeStruct((GDN_H, N, CHUNK, CHUNK), F32)],
        scratch_shapes=[pltpu.VMEM((GDN_H, GDN_D, GDN_D), F32)],
        name="gdn_chunk_fwd")(qkv, qkv, qkv, gsm)


def _gdn_chunk_bwd(qkv, gsm, do, s_in, t_in):
    Lp = qkv.shape[0]
    N = Lp // CHUNK
    G, nb, col, gate, st = _gdn_chunk_specs(N, True)

    def body(q_ref, k_ref, v_ref, gs_ref, do_ref, sin_ref, t_ref, dq_ref, dk_ref, dv_ref, dgs_ref, dS):
        n = pl.program_id(0)

        @pl.when(n == 0)
        def _():
            dS[...] = jnp.zeros_like(dS)

        lane = lax.broadcasted_iota(jnp.int32, (CHUNK, LANES), 1)
        rcol = lax.broadcasted_iota(jnp.int32, (CHUNK, 1), 0)
        ri = lax.broadcasted_iota(jnp.int32, (CHUNK, CHUNK), 0)
        ci = lax.broadcasted_iota(jnp.int32, (CHUNK, CHUNK), 1)
        ones = jnp.ones((CHUNK, LANES), F32)
        heads = range(GDN_H)
        sls = [slice(h * GDN_D, (h + 1) * GDN_D) for h in heads]
        rows = [slice(c * CHUNK, (c + 1) * CHUNK) for c in range(G)]
        pairs = [(c, h) for c in range(G) for h in heads]
        P = lambda f: {p: f(*p) for p in pairs}
        gs = [gs_ref[rows[c], :] for c in range(G)]
        gates = [_gdn_gates(gs[c]) for c in range(G)]
        tril, strict = gates[0][2], gates[0][3]
        q = P(lambda c, h: q_ref[rows[c], sls[h]])
        k = P(lambda c, h: k_ref[rows[c], sls[h]])
        v = P(lambda c, h: v_ref[rows[c], sls[h]])
        dov = P(lambda c, h: do_ref[rows[c], sls[h]])
        s0 = P(lambda c, h: sin_ref[h, c])
        t = P(lambda c, h: t_ref[h, c])
        beta = P(lambda c, h: gs[c][:, GDN_H + h:GDN_H + h + 1])
        gg = P(lambda c, h: _gdn_decay(gates[c][0], gates[c][1], tril, h))
        g = {p: x[0] for p, x in gg.items()}
        gam = {p: x[1] for p, x in gg.items()}
        eg = P(lambda c, h: jnp.exp(g[c, h]))
        egl = P(lambda c, h: jnp.exp(g[c, h][CHUNK - 1:CHUNK, :]))
        e = P(lambda c, h: jnp.exp(g[c, h][CHUNK - 1:CHUNK, :] - g[c, h]))
        kb = P(lambda c, h: k[c, h] * beta[c, h])
        kbg = P(lambda c, h: kb[c, h] * eg[c, h])
        vb = P(lambda c, h: v[c, h] * beta[c, h])
        qg = P(lambda c, h: q[c, h] * eg[c, h])
        kd = P(lambda c, h: k[c, h] * e[c, h])
        m = P(lambda c, h: jnp.where(strict, _dnt(kb[c, h], k[c, h]) * gam[c, h], 0.0))
        u = P(lambda c, h: _d(t[c, h], vb[c, h]))
        w = P(lambda c, h: _d(t[c, h], kbg[c, h]))
        p = P(lambda c, h: _dnt(q[c, h], k[c, h]) * gam[c, h])
        dqg = P(lambda c, h: _dnt(dov[c, h], s0[c, h]))
        qgdo = P(lambda c, h: _dtn(qg[c, h], dov[c, h]))
        ptdo = P(lambda c, h: _dtn(p[c, h], dov[c, h]))
        vnew = P(lambda c, h: u[c, h] - _d(w[c, h], s0[c, h]))
        dp = P(lambda c, h: jnp.where(tril, _dnt(dov[c, h], vnew[c, h]), 0.0))
        cur = [dS[h] for h in heads]
        dvnew, dkd, sds = {}, {}, {}
        for c in reversed(range(G)):
            for h in heads:
                dvnew[c, h] = ptdo[c, h] + _d(kd[c, h], cur[h])
                dkd[c, h] = _dnt(vnew[c, h], cur[h])
                sds[c, h] = _allsum(s0[c, h] * cur[h])
            cur = [qgdo[c, h] + egl[c, h] * cur[h] - _dtn(w[c, h], dvnew[c, h]) for h in heads]
        for h in heads:
            dS[h] = cur[h]
        dw = P(lambda c, h: -_dnt(dvnew[c, h], s0[c, h]))
        dvb = P(lambda c, h: _dtn(t[c, h], dvnew[c, h]))
        dkbg = P(lambda c, h: _dtn(t[c, h], dw[c, h]))
        dt = P(lambda c, h: _dnt(dvnew[c, h], vb[c, h]) + _dnt(dw[c, h], kbg[c, h]))
        x1 = P(lambda c, h: _dtn(t[c, h], dt[c, h]))
        dm = P(lambda c, h: jnp.where(strict, -_dnt(x1[c, h], t[c, h]), 0.0))
        dkk = P(lambda c, h: dm[c, h] * gam[c, h])
        dqk = P(lambda c, h: dp[c, h] * gam[c, h])
        dkb = P(lambda c, h: _d(dkk[c, h], k[c, h]) + eg[c, h] * dkbg[c, h])
        em = P(lambda c, h: dm[c, h] * m[c, h] + dp[c, h] * p[c, h])
        colsum = P(lambda c, h: _d2x(em[c, h], ones, _TN)[:, 0:1])
        for c, h in pairs:
            dk_ref[rows[c], sls[h]] = (_dtn(dkk[c, h], kb[c, h]) + _dtn(dqk[c, h], q[c, h]) + dkd[c, h] * e[c, h]
                                       + beta[c, h] * dkb[c, h])
            dq_ref[rows[c], sls[h]] = _d(dqk[c, h], k[c, h]) + dqg[c, h] * eg[c, h]
            dv_ref[rows[c], sls[h]] = beta[c, h] * dvb[c, h]
        for c in range(G):
            dg_all = jnp.zeros((CHUNK, LANES), F32)
            dbeta_all = jnp.zeros((CHUNK, LANES), F32)
            for h in heads:
                dbeta = _rowsum(k[c, h] * dkb[c, h]) + _rowsum(v[c, h] * dvb[c, h])
                z = _rowsum(kd[c, h] * dkd[c, h])
                dg = (_rowsum(em[c, h]) - colsum[c, h] + _rowsum(qg[c, h] * dqg[c, h]) + _rowsum(kbg[c, h] * dkbg[c, h])
                      - z)
                extra = _allsum(z) + egl[c, h] * sds[c, h]
                dg = dg + jnp.where(rcol == CHUNK - 1, extra, 0.0)
                dg_all = dg_all + jnp.where(lane == h, dg, 0.0)
                dbeta_all = dbeta_all + jnp.where(lane == GDN_H + h, dbeta, 0.0)
            dgs_ref[rows[c], :] = _dx((ci >= ri).astype(F32), dg_all) + dbeta_all

    return pl.pallas_call(
        body, grid=(nb,),
        in_specs=[col(0), col(1), col(2), gate, col(0), st(GDN_D, GDN_D), st(CHUNK, CHUNK)],
        out_specs=[col(0), col(0), col(0), gate],
        out_shape=[jax.ShapeDtypeStruct((Lp, D_MODEL), F32)] * 3 + [jax.ShapeDtypeStruct((Lp, LANES), F32)],
        scratch_shapes=[pltpu.VMEM((GDN_H, GDN_D, GDN_D), F32)],
        name="gdn_chunk_bwd")(qkv, qkv, qkv, gsm, do, s_in, t_in)


def _rot(x, c, s):
    half = RET_D // 2
    x1 = x[:, :half]
    x2 = x[:, half:]
    return jnp.concatenate([x1 * c - x2 * s, x2 * c + x1 * s], axis=1)


def _rot_bwd(d, c, s):
    half = RET_D // 2
    d1 = d[:, :half]
    d2 = d[:, half:]
    return jnp.concatenate([d1 * c + d2 * s, d2 * c - d1 * s], axis=1)


def _ret_tables():
    hh = jnp.arange(RET_H, dtype=F32)
    lg = jnp.log(1.0 - 2.0 ** (-5.0 - hh))
    idx = jnp.arange(CHUNK, dtype=F32)
    tril = jnp.asarray(np.tril(np.ones((CHUNK, CHUNK), dtype=bool)))
    dmask = jnp.where(tril, jnp.exp((idx[:, None] - idx[None, :]) * lg[:, None, None]), 0.0)
    qdec = jnp.exp((idx[None, :] + 1.0) * lg[:, None])
    kdec = jnp.exp((CHUNK - 1.0 - idx[None, :]) * lg[:, None])
    gch = jnp.exp(CHUNK * lg)
    qdec = jnp.broadcast_to(qdec[:, :, None], (RET_H, CHUNK, RET_D))
    kdec = jnp.broadcast_to(kdec[:, :, None], (RET_H, CHUNK, RET_D))
    gch = jnp.broadcast_to(gch[:, None, None], (RET_H, 8, LANES))
    return dmask, qdec, kdec, gch


def _ret_specs(N, rev):
    G = _group(N)
    nb = N // G
    cn = (lambda n: nb - 1 - n) if rev else (lambda n: n)
    col = lambda j: pl.BlockSpec((G * CHUNK, D_MODEL), lambda n: (cn(n), j))
    tab = lambda a, b: pl.BlockSpec((RET_H, a, b), lambda n: (0, 0, 0))
    rope = pl.BlockSpec((G * CHUNK, LANES), lambda n: (cn(n), 0))
    st = pl.BlockSpec((RET_H, G, RET_D, RET_D), lambda n: (0, cn(n), 0, 0))
    return G, nb, col, tab, rope, st


def _ret_chunk_fwd(proj_m, cos, sin, tables):
    Lp = proj_m.shape[0]
    N = Lp // CHUNK
    dmask, qdec, kdec, gch = tables
    G, nb, col, tab, rope, st = _ret_specs(N, False)

    def body(q_ref, k_ref, v_ref, c_ref, s_ref, dm_ref, qd_ref, kd_ref, g_ref, o_ref, sin_ref, S):
        n = pl.program_id(0)

        @pl.when(n == 0)
        def _():
            S[...] = jnp.zeros_like(S)

        heads = range(RET_H)
        sls = [slice(h * RET_D, (h + 1) * RET_D) for h in heads]
        rows = [slice(c * CHUNK, (c + 1) * CHUNK) for c in range(G)]
        pairs = [(c, h) for c in range(G) for h in heads]
        P = lambda f: {p: f(*p) for p in pairs}
        qr = P(lambda c, h: _rot(q_ref[rows[c], sls[h]], c_ref[rows[c], :], s_ref[rows[c], :]))
        ks = P(lambda c, h: _rot(k_ref[rows[c], sls[h]], c_ref[rows[c], :], s_ref[rows[c], :]) * (RET_D ** -0.5))
        v = P(lambda c, h: v_ref[rows[c], sls[h]])
        a = P(lambda c, h: _dnt(qr[c, h], ks[c, h]) * dm_ref[h])
        av = P(lambda c, h: _d(a[c, h], v[c, h]))
        kv = P(lambda c, h: _dtn(ks[c, h] * kd_ref[h], v[c, h]))
        qd = P(lambda c, h: qr[c, h] * qd_ref[h])
        cur = [S[h] for h in heads]
        for c in range(G):
            for h in heads:
                o_ref[rows[c], sls[h]] = av[c, h] + _d(qd[c, h], cur[h])
                sin_ref[h, c] = cur[h].astype(BF16)
            cur = [cur[h] * g_ref[h, 0:1, 0:1] + kv[c, h] for h in heads]
        for h in heads:
            S[h] = cur[h]

    return pl.pallas_call(
        body, grid=(nb,),
        in_specs=[col(3), col(4), col(5), rope, rope,
                  tab(CHUNK, CHUNK), tab(CHUNK, RET_D), tab(CHUNK, RET_D), tab(8, LANES)],
        out_specs=[col(0), st],
        out_shape=[jax.ShapeDtypeStruct((Lp, D_MODEL), F32), jax.ShapeDtypeStruct((RET_H, N, RET_D, RET_D), BF16)],
        scratch_shapes=[pltpu.VMEM((RET_H, RET_D, RET_D), F32)],
        name="ret_chunk_fwd")(proj_m, proj_m, proj_m, cos, sin, dmask, qdec, kdec, gch)


def _ret_chunk_bwd(proj_m, cos, sin, tables, do, s_in):
    Lp = proj_m.shape[0]
    N = Lp // CHUNK
    dmask, qdec, kdec, gch = tables
    G, nb, col, tab, rope, st = _ret_specs(N, True)

    def body(q_ref, k_ref, v_ref, c_ref, s_ref, dm_ref, qd_ref, kd_ref, g_ref, do_ref, sin_ref,
             d_ref, dS):
        n = pl.program_id(0)

        @pl.when(n == 0)
        def _():
            dS[...] = jnp.zeros_like(dS)

        kscale = RET_D ** -0.5
        heads = range(RET_H)
        sls = [slice(h * RET_D, (h + 1) * RET_D) for h in heads]
        rows = [slice(c * CHUNK, (c + 1) * CHUNK) for c in range(G)]
        pairs = [(c, h) for c in range(G) for h in heads]
        P = lambda f: {p: f(*p) for p in pairs}
        cs = [(c_ref[rows[c], :], s_ref[rows[c], :]) for c in range(G)]
        osl = lambda part, h: slice(part * D_MODEL + h * RET_D, part * D_MODEL + (h + 1) * RET_D)
        qr = P(lambda c, h: _rot(q_ref[rows[c], sls[h]], *cs[c]))
        ks = P(lambda c, h: _rot(k_ref[rows[c], sls[h]], *cs[c]) * kscale)
        v = P(lambda c, h: v_ref[rows[c], sls[h]])
        dov = P(lambda c, h: do_ref[rows[c], sls[h]])
        ad = P(lambda c, h: _dnt(qr[c, h], ks[c, h]) * dm_ref[h])
        da = P(lambda c, h: _dnt(dov[c, h], v[c, h]) * dm_ref[h])
        dos = P(lambda c, h: _dnt(dov[c, h], sin_ref[h, c]) * qd_ref[h])
        qdo = P(lambda c, h: _dtn(qr[c, h] * qd_ref[h], dov[c, h]))
        adv = P(lambda c, h: _dtn(ad[c, h], dov[c, h]))
        dqr = P(lambda c, h: _d(da[c, h], ks[c, h]) + dos[c, h])
        daq = P(lambda c, h: _dtn(da[c, h], qr[c, h]))
        kk = P(lambda c, h: ks[c, h] * kd_ref[h])
        cur = [dS[h] for h in heads]
        for c in reversed(range(G)):
            for h in heads:
                d_ref[rows[c], osl(2, h)] = (adv[c, h] + _d(kk[c, h], cur[h])).astype(BF16)
                d_ref[rows[c], osl(0, h)] = _rot_bwd(dqr[c, h], *cs[c]).astype(BF16)
                dks = daq[c, h] + _dnt(v[c, h], cur[h]) * kd_ref[h]
                d_ref[rows[c], osl(1, h)] = _rot_bwd(dks * kscale, *cs[c]).astype(BF16)
            cur = [cur[h] * g_ref[h, 0:1, 0:1] + qdo[c, h] for h in heads]
        for h in heads:
            dS[h] = cur[h]

    return pl.pallas_call(
        body, grid=(nb,),
        in_specs=[col(3), col(4), col(5), rope, rope,
                  tab(CHUNK, CHUNK), tab(CHUNK, RET_D), tab(CHUNK, RET_D), tab(8, LANES), col(0), st],
        out_specs=pl.BlockSpec((G * CHUNK, 3 * D_MODEL), lambda n: (nb - 1 - n, 0)),
        out_shape=jax.ShapeDtypeStruct((Lp, 3 * D_MODEL), BF16),
        scratch_shapes=[pltpu.VMEM((RET_H, RET_D, RET_D), F32)],
        name="ret_chunk_bwd")(proj_m, proj_m, proj_m, cos, sin, dmask, qdec, kdec, gch, do, s_in)


def _merge_specs(tr):
    col = lambda j: pl.BlockSpec((tr, D_MODEL), lambda i: (i, j))
    return col


def _merge_fwd(o_a, o_b, proj_m, gnorm, out_proj=None):
    Lp = o_a.shape[0]
    tr = _tile(Lp, 192 if out_proj is None else 352, 16)
    if out_proj is not None:
        w_out, res, g2 = out_proj
        r_args, r_specs = _rows_operands(res, tr)

    def body(oa_ref, ob_ref, gz_ref, rg_ref, ga_ref, gb_ref, gn_ref, *rest):
        y_ref = rest[0] if out_proj is None else rest[-3]
        gn = gn_ref[...]
        oa = oa_ref[...]
        ob = ob_ref[...]
        gz = gz_ref[...]
        ya = []
        for j in range(GDN_H):
            seg = oa[:, j * GDN_D:(j + 1) * GDN_D]
            r = lax.rsqrt(jnp.mean(seg * seg, axis=-1, keepdims=True) + EPS)
            ya.append(seg * r * gn)
        ya = jnp.concatenate(ya, axis=1) * (gz * _sig(gz))
        yb = []
        for j in range(RET_H):
            seg = ob[:, j * RET_D:(j + 1) * RET_D]
            r = lax.rsqrt(jnp.mean(seg * seg, axis=-1, keepdims=True) + EPS)
            yb.append(seg * r)
        rg = rg_ref[...]
        yb = jnp.concatenate(yb, axis=1) * (rg * _sig(rg))
        yv = (_sig(ga_ref[...]) * ya + _sig(gb_ref[...]) * yb).astype(BF16)
        y_ref[...] = yv
        if out_proj is not None:
            wo_ref, g2_ref = rest[0], rest[1]
            h1_ref, hn2_ref = rest[-2], rest[-1]
            h1 = (jnp.dot(yv, wo_ref[...], preferred_element_type=F32)
                  + _rows_tile(res, rest[2:2 + len(r_args)], pl.program_id(0), tr))
            r = lax.rsqrt(jnp.mean(h1 * h1, axis=-1, keepdims=True) + EPS)
            h1_ref[...] = h1
            hn2_ref[...] = (h1 * r * g2_ref[...]).astype(BF16)

    col = _merge_specs(tr)
    in_specs = [col(0), col(0), col(6), col(7), col(8), col(9), pl.BlockSpec((1, GDN_D), lambda i: (0, 0))]
    args = [o_a, o_b, proj_m, proj_m, proj_m, proj_m, gnorm]
    if out_proj is None:
        return pl.pallas_call(body, grid=(Lp // tr,), in_specs=in_specs, out_specs=col(0),
                              out_shape=jax.ShapeDtypeStruct((Lp, D_MODEL), BF16), name="merge_fwd")(*args)
    in_specs += [pl.BlockSpec((D_MODEL, D_MODEL), lambda i: (0, 0), pipeline_mode=pl.Buffered(1)),
                 pl.BlockSpec((1, D_MODEL), lambda i: (0, 0))] + r_specs
    return pl.pallas_call(
        body, grid=(Lp // tr,), in_specs=in_specs, out_specs=[col(0), col(0), col(0)],
        out_shape=[jax.ShapeDtypeStruct((Lp, D_MODEL), BF16), jax.ShapeDtypeStruct((Lp, D_MODEL), F32),
                   jax.ShapeDtypeStruct((Lp, D_MODEL), BF16)],
        name="merge_out_proj_rms2")(*args, w_out, g2, *r_args)


def _merge_bwd(dh1b, w_out, o_a, o_b, proj_m, gnorm):
    Lp = o_a.shape[0]
    tr = _tile(Lp, 192, 16)

    def body(d_ref, wo_ref, oa_ref, ob_ref, gz_ref, rg_ref, ga_ref, gb_ref, gn_ref, dc_ref, doa_ref, dob_ref, dgn_ref):
        i = pl.program_id(0)
        gn = gn_ref[...]
        dyv = lax.dot_general(d_ref[...], wo_ref[...], _NT, preferred_element_type=F32)
        oa = oa_ref[...]
        ob = ob_ref[...]
        gz = gz_ref[...]
        rg = rg_ref[...]
        sa = _sig(ga_ref[...])
        sb = _sig(gb_ref[...])
        dya = dyv * sa
        dyb = dyv * sb
        sgz = _sig(gz)
        szz = gz * sgz
        dgn = jnp.zeros((1, GDN_D), F32)
        ya = []
        dgz = []
        for j in range(GDN_H):
            sl = slice(j * GDN_D, (j + 1) * GDN_D)
            seg = oa[:, sl]
            r = lax.rsqrt(jnp.mean(seg * seg, axis=-1, keepdims=True) + EPS)
            xh = seg * r
            oan = xh * gn
            ya.append(oan * szz[:, sl])
            dgz.append(dya[:, sl] * oan * (sgz[:, sl] * (1.0 + gz[:, sl] * (1.0 - sgz[:, sl]))))
            doan = dya[:, sl] * szz[:, sl]
            dgn = dgn + jnp.sum(doan * xh, axis=0, keepdims=True)
            dxh = doan * gn
            doa_ref[:, sl] = r * (dxh - xh * jnp.mean(dxh * xh, axis=-1, keepdims=True))
        ya = jnp.concatenate(ya, axis=1)
        srg = _sig(rg)
        srr = rg * srg
        yb = []
        drg = []
        for j in range(RET_H):
            sl = slice(j * RET_D, (j + 1) * RET_D)
            seg = ob[:, sl]
            r = lax.rsqrt(jnp.mean(seg * seg, axis=-1, keepdims=True) + EPS)
            xh = seg * r
            yb.append(xh * srr[:, sl])
            drg.append(dyb[:, sl] * xh * (srg[:, sl] * (1.0 + rg[:, sl] * (1.0 - srg[:, sl]))))
            dxh = dyb[:, sl] * srr[:, sl]
            dob_ref[:, sl] = r * (dxh - xh * jnp.mean(dxh * xh, axis=-1, keepdims=True))
        yb = jnp.concatenate(yb, axis=1)
        dc_ref[:, 0:D_MODEL] = jnp.concatenate(dgz, axis=1).astype(BF16)
        dc_ref[:, D_MODEL:2 * D_MODEL] = jnp.concatenate(drg, axis=1).astype(BF16)
        dc_ref[:, 2 * D_MODEL:3 * D_MODEL] = (dyv * ya * sa * (1.0 - sa)).astype(BF16)
        dc_ref[:, 3 * D_MODEL:] = (dyv * yb * sb * (1.0 - sb)).astype(BF16)

        @pl.when(i == 0)
        def _():
            dgn_ref[...] = dgn

        @pl.when(i > 0)
        def _():
            dgn_ref[...] += dgn

    col = _merge_specs(tr)
    return pl.pallas_call(
        body, grid=(Lp // tr,),
        in_specs=[col(0), pl.BlockSpec((D_MODEL, D_MODEL), lambda i: (0, 0), pipeline_mode=pl.Buffered(1)),
                  col(0), col(0), col(6), col(7), col(8), col(9), pl.BlockSpec((1, GDN_D), lambda i: (0, 0))],
        out_specs=[pl.BlockSpec((tr, 4 * D_MODEL), lambda i: (i, 0)), col(0), col(0),
                   pl.BlockSpec((1, GDN_D), lambda i: (0, 0))],
        out_shape=[jax.ShapeDtypeStruct((Lp, 4 * D_MODEL), BF16), jax.ShapeDtypeStruct((Lp, D_MODEL), F32),
                   jax.ShapeDtypeStruct((Lp, D_MODEL), F32), jax.ShapeDtypeStruct((1, GDN_D), F32)],
        name="merge_bwd")(dh1b, w_out, o_a, o_b, proj_m, proj_m, proj_m, proj_m, gnorm)


def _ffn_act(up, conv_w, conv_b):
    Lp = up.shape[0]
    tr = _tile(Lp, 192, 16)
    W2 = 2 * D_FF

    def body(main_ref, prev_ref, w_ref, b_ref, act_ref, u_ref):
        i = pl.program_id(0)
        prev = jnp.where(i > 0, prev_ref[...], 0.0)
        ext = jnp.concatenate([prev, main_ref[...]], axis=0)
        u = _taps(_shifted(ext, range(8 - (FFN_CONV - 1), 9)), w_ref[...], tr, b_ref[...])
        a = u[:, :D_FF]
        act_ref[...] = (a * _sig(a) * u[:, D_FF:]).astype(BF16)
        u_ref[...] = u.astype(BF16)

    return pl.pallas_call(
        body, grid=(Lp // tr,),
        in_specs=[pl.BlockSpec((tr, W2), lambda i: (i, 0)), _halo_prev(tr, W2),
                  pl.BlockSpec((FFN_CONV, W2), lambda i: (0, 0)), pl.BlockSpec((1, W2), lambda i: (0, 0))],
        out_specs=[pl.BlockSpec((tr, D_FF), lambda i: (i, 0)), pl.BlockSpec((tr, W2), lambda i: (i, 0))],
        out_shape=[jax.ShapeDtypeStruct((Lp, D_FF), BF16), jax.ShapeDtypeStruct((Lp, W2), BF16)],
        name="ffn_act")(up, up, conv_w, conv_b)


def _ffn_act_bwd(up, u, dact, conv_w):
    Lp = up.shape[0]
    tr = _tile(Lp, 192, 16)
    W2 = 2 * D_FF
    te = tr + 8

    def body(up_ref, u_ref, un_ref, da_ref, dan_ref, w_ref, dup_ref, acc_ref):
        i = pl.program_id(0)
        w = w_ref[...]
        ue = jnp.concatenate([u_ref[...].astype(F32), un_ref[...].astype(F32)[0:8]], axis=0)
        a = ue[:, :D_FF]
        b = ue[:, D_FF:]
        rowe = i * tr + lax.broadcasted_iota(jnp.int32, (te, 1), 0)
        dae = jnp.where(rowe < Lp, jnp.concatenate([da_ref[...], dan_ref[...]], axis=0), 0.0)
        sg = _sig(a)
        du = jnp.concatenate([dae * b * (sg * (1.0 + a * (1.0 - sg))), dae * (a * sg)], axis=1)
        dus = _shifted(du, range(FFN_CONV - 1, -1, -1))
        dup_ref[...] = _taps(dus, w, tr).astype(BF16)
        upm = up_ref[...]
        rows = [jnp.sum(dus[kk][0:tr, :] * upm, axis=0, keepdims=True) for kk in range(FFN_CONV)]
        rows.append(jnp.sum(du[0:tr, :], axis=0, keepdims=True))
        part = jnp.concatenate(rows + [jnp.zeros((8 - len(rows), W2), F32)], axis=0)

        @pl.when(i == 0)
        def _():
            acc_ref[...] = part

        @pl.when(i > 0)
        def _():
            acc_ref[...] += part

    return pl.pallas_call(
        body, grid=(Lp // tr,),
        in_specs=[pl.BlockSpec((tr, W2), lambda i: (i, 0)), pl.BlockSpec((tr, W2), lambda i: (i, 0)),
                  _halo_next(tr, W2, Lp, rows=16), pl.BlockSpec((tr, D_FF), lambda i: (i, 0)), _halo_next(tr, D_FF, Lp),
                  pl.BlockSpec((FFN_CONV, W2), lambda i: (0, 0))],
        out_specs=[pl.BlockSpec((tr, W2), lambda i: (i, 0)), pl.BlockSpec((8, W2), lambda i: (0, 0))],
        out_shape=[jax.ShapeDtypeStruct((Lp, W2), BF16), jax.ShapeDtypeStruct((8, W2), F32)],
        name="ffn_act_bwd")(up, u, u, dact, dact, conv_w)


def _proj_rows(j):
    shift = (jnp.where((j >= 3) & (j < 6), _O_RQ - 3 * D_MODEL, 0) + jnp.where(j == 6, _O_GZ - 6 * D_MODEL, 0)
             + jnp.where(j >= 7, _O_RG - 7 * D_MODEL, 0))
    return j * D_MODEL + shift


def _local_step(hpad, tgt, pad, wt, first_weights=None, late_weights=None, on_ffn_out_grads=None,
                on_w_in_grads=None):
    Lp = hpad.shape[0]
    first = pad + N_META
    pos = jnp.arange(Lp, dtype=F32) - float(pad)
    half = RET_D // 2
    inv = 1.0 / (ROPE_BASE ** (jnp.arange(half, dtype=F32) / half))
    ang = pos[:, None] * inv[None, :]
    cos, sin = jnp.cos(ang), jnp.sin(ang)
    tables = _ret_tables()
    gparams = jnp.zeros((8, LANES), F32).at[0, :GDN_H].set(wt["a_log"]).at[1, :GDN_H].set(wt["dt_bias"])

    hn1 = _rms_fwd(hpad, wt["norm1"], "rms1_fwd")
    if first_weights is not None:
        wt = {**wt, **first_weights(hn1)}
    w_in_t = wt["w_in_t"]
    w_small_t = jnp.pad(w_in_t[_O_GA:_O_RQ], ((0, LANES - 2 * GDN_H), (0, 0)))
    proj_m = _mm_nn(hn1, w_in_t, bt=True, tm_target=2752, b_rows=(D_MODEL, MAIN_W // D_MODEL, _proj_rows),
                    name="proj_main")
    proj_s = _mm_nn(hn1, w_small_t, bt=True, name="proj_small")
    qkv, gsm, conv_out = _gdn_pre(proj_m, proj_s, wt["gdn_conv_w"], gparams, pad)
    o_a, s_a, t_a = _gdn_chunk_fwd(qkv, gsm)
    o_b, s_b = _ret_chunk_fwd(proj_m, cos, sin, tables)
    if late_weights is not None:
        wt = {**wt, **late_weights(o_b)}
    y, h1, hn2 = _merge_fwd(o_a, o_b, proj_m, wt["gdn_norm"], (wt["w_out"], hpad, wt["norm2"]))
    up = _mm_nn(hn2, wt["w_up_t"], bt=True, name="ffn_up")
    act, u_ffn = _ffn_act(up, wt["ffn_conv_w"], wt["ffn_conv_b"])
    lossvec, dh2, dh2b, d_norm_f = _final(_Producer(act, wt["w_down"], h1), wt["norm_f"], tgt, first)

    d_w_down = _mm_tn(act, dh2b, name="dw_down")
    dact = _mm_nt(dh2b, wt["w_down"], name="d_act")
    dup, ffn_rows = _ffn_act_bwd(up, u_ffn, dact, wt["ffn_conv_w"])
    d_w_up_t = _mm_tn(dup, hn2, name="dw_up")
    dh1, dh1b, d_norm2 = _rms_bwd(h1, wt["norm2"], _Producer(dup, wt["w_up_t"]), dh2, pad, "d_hn2_rms2_bwd")

    d_w_out = _mm_tn(y, dh1b, name="dw_out")
    gnorm = wt["gdn_norm"]
    if on_ffn_out_grads is not None:
        gnorm = gnorm + on_ffn_out_grads(d_w_down, d_w_up_t, d_w_out)[0:1, :]
    d_c, do_a, do_b, d_gnorm = _merge_bwd(dh1b, wt["w_out"], o_a, o_b, proj_m, gnorm)
    d_r = _ret_chunk_bwd(proj_m, cos, sin, tables, do_b, s_b)
    dq, dk, dv, dgs = _gdn_chunk_bwd(qkv, gsm, do_a, s_a, t_a)
    d_a, d_s, conv_rows, gp_rows = _gdn_pre_bwd(proj_m, conv_out, proj_s, wt["gdn_conv_w"], gparams, dq, dk, dv, dgs,
                                                pad)

    segs = [(d_a, w_in_t[_O_GQ:_O_GZ]), (d_r, w_in_t[_O_RQ:_O_RG]),
            (d_c, jnp.concatenate([w_in_t[_O_GZ:_O_GA], w_in_t[_O_RG:_O_END]], axis=0))]
    pa, pr, pc = [_mm_tn(d, hn1, BF16, name="dw_in_%d" % i) for i, (d, _) in enumerate(segs)]
    ps = _mm_tn(d_s, hn1, BF16, name="dw_in_small")
    d_w_in_t = jnp.concatenate([pa, pc[:D_MODEL], ps[:2 * GDN_H], pr, pc[D_MODEL:]], axis=0)
    if on_w_in_grads is not None:
        w_small_t = w_small_t + on_w_in_grads(d_w_in_t)[0:1, 0:1].astype(w_small_t.dtype)
    dhn1 = _mm_sum([(d_s, w_small_t)] + segs[:-1], "d_hn1_first")
    dh0, _, d_norm1 = _rms_bwd(hpad, wt["norm1"], _Producer(*segs[-1], dhn1), dh1, pad, "d_hn1_rms1_bwd")

    grads = {
        "norm1": d_norm1, "w_in_t": d_w_in_t, "gdn_conv_w": conv_rows[:GDN_CONV],
        "a_log": gp_rows[0, :GDN_H], "dt_bias": gp_rows[1, :GDN_H], "gdn_norm": d_gnorm, "w_out": d_w_out,
        "norm2": d_norm2, "w_up_t": d_w_up_t, "ffn_conv_w": ffn_rows[:FFN_CONV],
        "ffn_conv_b": ffn_rows[FFN_CONV:FFN_CONV + 1], "w_down": d_w_down, "norm_f": d_norm_f,
    }
    return lossvec, dh0, grads


def _peer(k):
    ix, iy, ic = lax.axis_index("x"), lax.axis_index("y"), lax.axis_index("c")
    px = 1 - ix if (k >> 2) & 1 else ix
    py = 1 - iy if (k >> 1) & 1 else iy
    pc = 1 - ic if k & 1 else ic
    return (px, py, pc), 4 * px + 2 * py + pc


def _comm_call(body, n, out_shapes, name, args):
    hbm = pl.BlockSpec(memory_space=pl.ANY)
    return pl.pallas_call(
        body, out_shape=out_shapes, in_specs=[hbm] * n, out_specs=[hbm] * n,
        scratch_shapes=[pltpu.SemaphoreType.DMA((n, N_DEV - 1)), pltpu.SemaphoreType.DMA((n, N_DEV - 1)),
                        pltpu.SemaphoreType.DMA((n,))],
        name=name)(*args)


def _all_gather(xs, name):
    n = len(xs)

    def body(*refs):
        x_refs, out_refs = refs[:n], refs[n:2 * n]
        send_sems, recv_sems, local_sems = refs[2 * n:]
        _, me = _peer(0)
        pending = []
        for i in range(n):
            local = pltpu.make_async_copy(x_refs[i], out_refs[i].at[me], local_sems.at[i])
            local.start()
            pending.append(local)
        sends = []
        for i in range(n):
            for k in range(1, N_DEV):
                dev, _ = _peer(k)
                cp = pltpu.make_async_remote_copy(
                    src_ref=x_refs[i], dst_ref=out_refs[i].at[me], send_sem=send_sems.at[i, k - 1],
                    recv_sem=recv_sems.at[i, k - 1], device_id=dev, device_id_type=MESH_T)
                cp.start()
                sends.append(cp)
        for i in range(n):
            for k in range(1, N_DEV):
                dev, idx = _peer(k)
                pltpu.make_async_remote_copy(
                    src_ref=x_refs[i], dst_ref=out_refs[i].at[idx], send_sem=send_sems.at[i, k - 1],
                    recv_sem=recv_sems.at[i, k - 1], device_id=dev, device_id_type=MESH_T).wait_recv()
        for cp in sends:
            cp.wait_send()
        for local in pending:
            local.wait()

    out_shapes = [jax.ShapeDtypeStruct((N_DEV,) + a.shape, a.dtype) for a in xs]
    return _comm_call(body, n, out_shapes, name, xs)


def _all_to_all(gs, name):
    n = len(gs)

    def body(*refs):
        g_refs, out_refs = refs[:n], refs[n:2 * n]
        send_sems, recv_sems, local_sems = refs[2 * n:]
        _, me = _peer(0)
        pending = []
        for i in range(n):
            local = pltpu.make_async_copy(g_refs[i].at[me], out_refs[i].at[0], local_sems.at[i])
            local.start()
            pending.append(local)
        sends = []
        for i in range(n):
            for k in range(1, N_DEV):
                dev, idx = _peer(k)
                cp = pltpu.make_async_remote_copy(
                    src_ref=g_refs[i].at[idx], dst_ref=out_refs[i].at[k], send_sem=send_sems.at[i, k - 1],
                    recv_sem=recv_sems.at[i, k - 1], device_id=dev, device_id_type=MESH_T)
                cp.start()
                sends.append(cp)
        for cp in sends:
            cp.wait_recv()
        for cp in sends:
            cp.wait_send()
        for local in pending:
            local.wait()

    out_shapes = [jax.ShapeDtypeStruct(g.shape, g.dtype) for g in gs]
    return _comm_call(body, n, out_shapes, name, gs)


_SPLIT_RELATIONS = {"gather": tuple(range(1, N_DEV)), "a2a": tuple(range(1, N_DEV)), "chip": (1, 2, 4, 6),
                    "forward": (2, 4, 6)}


def _split_copies(kind, src_refs, land_refs, send_sems, recv_sems, local_sems, with_recv):
    n = len(land_refs)
    rels = _SPLIT_RELATIONS[kind]
    _, me = _peer(0)
    locals_, remotes = [], []
    for i in range(n):
        if kind in ("gather", "chip"):
            locals_.append(pltpu.make_async_copy(src_refs[i], land_refs[i].at[me], local_sems.at[i]))
        elif kind == "a2a":
            locals_.append(pltpu.make_async_copy(src_refs[i].at[me], land_refs[i].at[0], local_sems.at[i]))
        for jj, k in enumerate(rels):
            dev, idx = _peer(k)
            if kind in ("gather", "chip"):
                src, dst, mine = src_refs[i], land_refs[i].at[me], land_refs[i].at[idx]
            elif kind == "a2a":
                src, dst, mine = src_refs[i].at[idx], land_refs[i].at[k], land_refs[i].at[k]
            else:
                dev, _ = _peer(1)
                _, came = _peer(k + 1)
                src, dst, mine = land_refs[i].at[idx], land_refs[i].at[idx], land_refs[i].at[came]
            j = i * len(rels) + jj
            send = pltpu.make_async_remote_copy(
                src_ref=src, dst_ref=dst, send_sem=send_sems.at[j], recv_sem=recv_sems.at[j],
                device_id=dev, device_id_type=MESH_T)
            recv = pltpu.make_async_remote_copy(
                src_ref=src, dst_ref=mine, send_sem=send_sems.at[j], recv_sem=recv_sems.at[j],
                device_id=dev, device_id_type=MESH_T) if with_recv else None
            remotes.append((send, recv))
    return locals_, remotes


_HBM = pl.BlockSpec(memory_space=pltpu.HBM)
_SEM = pl.BlockSpec(memory_space=pltpu.SEMAPHORE)
_ANY = pl.BlockSpec(memory_space=pl.ANY)


def _split_start(srcs, kind, name, after):
    n = len(srcs)
    if kind == "forward":
        arrays = list(srcs)
    else:
        gathers = kind in ("gather", "chip")
        arrays = list(srcs) + [lax.empty(((N_DEV,) + a.shape) if gathers else a.shape, a.dtype) for a in srcs]
    na = len(arrays)

    def body(*refs):
        src_refs, land_refs = refs[:n], refs[na - n:na]
        send_sems, recv_sems, local_sems = refs[na + 1:na + 4]
        token = refs[-1]
        locals_, remotes = _split_copies(kind, src_refs, land_refs, send_sems, recv_sems, local_sems, False)
        for cp in locals_:
            cp.start()
        for send, _ in remotes:
            send.start()
        token[...] = jnp.zeros_like(token)

    ncp = n * len(_SPLIT_RELATIONS[kind])
    sems = (pltpu.SemaphoreType.DMA((ncp,)), pltpu.SemaphoreType.DMA((ncp,)), pltpu.SemaphoreType.DMA((n,)))
    thru = tuple(pltpu.HBM(a.shape, a.dtype) for a in arrays)
    outs = pl.pallas_call(
        body, name=name,
        out_shape=sems + thru + (jax.ShapeDtypeStruct((8, LANES), F32),),
        in_specs=[_HBM] * na + [_ANY],
        out_specs=[_SEM] * 3 + [_HBM] * na + [pl.BlockSpec(memory_space=pltpu.VMEM)],
        input_output_aliases={i: 3 + i for i in range(na)},
        compiler_params=pltpu.CompilerParams(has_side_effects=pltpu.SideEffectType.DATAFLOW_SIDE_EFFECTING),
    )(*[pltpu.with_memory_space_constraint(a, pltpu.HBM) for a in arrays], after)
    return (kind, n, outs[:3], outs[3:3 + na]), outs[-1]


def _split_wait(handle, name, after):
    kind, n, sems, thru = handle
    na = len(thru)

    def body(*refs):
        src_refs, land_refs = refs[:n], refs[na - n:na]
        send_sems, recv_sems, local_sems = refs[na:na + 3]
        locals_, remotes = _split_copies(kind, src_refs, land_refs, send_sems, recv_sems, local_sems, True)
        for send, recv in remotes:
            send.wait_send()
            recv.wait_recv()
        for cp in locals_:
            cp.wait()

    outs = pl.pallas_call(
        body, name=name, out_shape=tuple(pltpu.HBM(a.shape, a.dtype) for a in thru),
        in_specs=[_HBM] * na + [_SEM] * 3 + [_ANY], out_specs=[_HBM] * na,
        input_output_aliases={i: i for i in range(na)},
        compiler_params=pltpu.CompilerParams(has_side_effects=pltpu.SideEffectType.DATAFLOW_SIDE_EFFECTING),
    )(*thru, *sems, after)
    return list(outs[na - n:])


def _adamw(gslabs, w, m, v, name):
    R, Cw = w.shape
    if R % 8 == 0:
        tr, tc = _tile(R, 64 if Cw > 1024 else 128, 8), Cw
    else:
        tr, tc = R, LANES
    c1 = 1.0 - ADAM_B1 ** ADAM_STEP
    c2 = 1.0 - ADAM_B2 ** ADAM_STEP

    def body(g_ref, w_ref, m_ref, v_ref, go_ref, d_ref, mo_ref, vo_ref):
        g = g_ref[0].astype(F32)
        for k in range(1, N_DEV):
            g = g + g_ref[k].astype(F32)
        mn = ADAM_B1 * m_ref[...] + (1.0 - ADAM_B1) * g
        vn = ADAM_B2 * v_ref[...] + (1.0 - ADAM_B2) * (g * g)
        m_hat = mn / c1
        v_hat = vn / c2
        go_ref[...] = g
        d_ref[...] = -ADAM_LR * (m_hat / (jnp.sqrt(v_hat) + ADAM_EPS) + ADAM_WD * w_ref[...])
        mo_ref[...] = mn
        vo_ref[...] = vn

    blk = pl.BlockSpec((tr, tc), lambda i, j: (i, j))
    return pl.pallas_call(
        body, grid=(R // tr, Cw // tc),
        in_specs=[pl.BlockSpec((N_DEV, tr, tc), lambda i, j: (0, i, j)), blk, blk, blk],
        out_specs=[blk] * 4, out_shape=[jax.ShapeDtypeStruct((R, Cw), F32)] * 4, name=name)(gslabs, w, m, v)


def _pack(arrs, row_mult, dtype=F32):
    parts = []
    total = 0
    for a in arrs:
        f = a.reshape(-1).astype(dtype)
        n = -(-f.shape[0] // 1024) * 1024
        parts.append(jnp.pad(f, (0, n - f.shape[0])))
        total += n
    rows = total // LANES
    rows_p = -(-rows // row_mult) * row_mult
    flat = jnp.concatenate(parts)
    flat = jnp.pad(flat, (0, rows_p * LANES - total))
    return flat.reshape(rows_p, LANES)


def _unpack(packed, shapes):
    lead = packed.shape[:-2]
    flat = packed.reshape(lead + (-1,))
    out = []
    off = 0
    for s in shapes:
        n = int(np.prod(s))
        out.append(flat[..., off:off + n].reshape(lead + tuple(s)))
        off += -(-n // 1024) * 1024
    return out


def _gather_cols(stacked):
    d, r, c = stacked.shape
    return stacked.transpose(1, 0, 2).reshape(r, d * c)


def _scatter_cols(full):
    r, n = full.shape
    return full.reshape(r, N_DEV, n // N_DEV).transpose(1, 0, 2)


def kernel(x, meta, norm1, w_in, gdn_conv_w, gdn_a_log, gdn_dt_bias, gdn_norm, w_out, norm2, w_ffn_up, ffn_conv_w, ffn_conv_b, w_ffn_down, norm_f, loss_target, m_meta, m_norm1, m_w_in, m_gdn_conv_w, m_gdn_a_log, m_gdn_dt_bias, m_gdn_norm, m_w_out, m_norm2, m_w_ffn_up, m_ffn_conv_w, m_ffn_conv_b, m_w_ffn_down, m_norm_f, v_meta, v_norm1, v_w_in, v_gdn_conv_w, v_gdn_a_log, v_gdn_dt_bias, v_gdn_norm, v_w_out, v_norm2, v_w_ffn_up, v_ffn_conv_w, v_ffn_conv_b, v_w_ffn_down, v_norm_f):
    S = x.shape[1]
    L = N_META + S
    pad = (-L) % CHUNK
    Lp = L + pad

    tr_ = lambda a: jnp.swapaxes(a[0], 0, 1)
    big = [tr_(w_in), w_out[0], tr_(w_ffn_up), w_ffn_down[0]]
    small = [meta, gdn_conv_w, ffn_conv_w]
    small_all, = _all_gather([_pack(small, 8)], "gather_small_weights")
    first, first_token = _split_start([big[0].astype(BF16)], "chip", "gather_w_in_start", small_all)
    late, late_token = _split_start([a.astype(BF16) for a in big[1:]], "gather", "gather_late_start", first_token)

    def first_weights(after):
        half = _split_wait(first, "gather_w_in_wait", after)
        second, second_token = _split_start(half, "forward", "gather_w_in_forward_start", after)
        w_in_s, = _split_wait(second, "gather_w_in_forward_wait", second_token)
        return {"w_in_t": w_in_s.reshape(_O_END, D_MODEL)}

    def late_weights(after):
        w_out_s, w_up_s, w_down_s = _split_wait(late, "gather_late_wait", after)
        return {"w_out": w_out_s.reshape(D_MODEL, D_MODEL), "w_up_t": w_up_s.reshape(2 * D_FF, D_MODEL),
                "w_down": w_down_s.reshape(D_FF, D_MODEL)}

    meta_s, gconv_s, fconv_s = _unpack(small_all, [a.shape for a in small])
    wt = {
        "norm1": norm1 + jnp.tile(late_token[0:1, :], (1, D_MODEL // LANES)),
        "gdn_conv_w": _gather_cols(gconv_s[:, 0]), "a_log": gdn_a_log[0], "dt_bias": gdn_dt_bias[0],
        "gdn_norm": gdn_norm, "norm2": norm2, "ffn_conv_w": _gather_cols(fconv_s[:, 0]), "ffn_conv_b": ffn_conv_b,
        "norm_f": norm_f.reshape(1, D_MODEL),
    }
    meta_f = _gather_cols(meta_s)

    pending = {}

    def on_ffn_out_grads(d_w_down, d_w_up_t, d_w_out):
        srcs = [d_w_out.reshape(N_DEV, D_MODEL // N_DEV, D_MODEL), d_w_up_t.reshape(N_DEV, 2 * D_FF // N_DEV, D_MODEL),
                d_w_down.reshape(N_DEV, D_FF // N_DEV, D_MODEL)]
        pending["ffn_out"], token = _split_start(srcs, "a2a", "exchange_ffn_out_start", d_w_out)
        return token

    def on_w_in_grads(d_w_in_t):
        slabs = d_w_in_t.astype(BF16).reshape(N_DEV, _O_END // N_DEV, D_MODEL)
        pending["w_in"], token = _split_start([slabs], "a2a", "exchange_w_in_start", d_w_in_t)
        return token

    head = jnp.concatenate([jnp.zeros((pad, D_MODEL), F32), meta_f], axis=0)
    if S >= 2 * 704:
        hpad = _Rows(x[0], pad + N_META, head)
        tgt = _Rows(loss_target[0], pad + N_META)
    else:
        hpad = jnp.concatenate([head, x[0]], axis=0)
        tgt = jnp.concatenate([jnp.zeros((pad + N_META, D_MODEL), F32), loss_target[0]], axis=0)
    lossvec, dh0, gr = _local_step(hpad, tgt, pad, wt, first_weights, late_weights, on_ffn_out_grads, on_w_in_grads)

    loss = lax.psum(jnp.sum(lossvec), ("x", "y", "c"))
    grad_x = dh0[pad + N_META:][None]

    big_m = [tr_(m_w_in), m_w_out[0], tr_(m_w_ffn_up), m_w_ffn_down[0]]
    big_v = [tr_(v_w_in), v_w_out[0], tr_(v_w_ffn_up), v_w_ffn_down[0]]
    slabs_ffn_out = _split_wait(pending["ffn_out"], "exchange_ffn_out_wait", dh0)
    big_out = [None] + [_adamw(slabs_ffn_out[i - 1], big[i], big_m[i], big_v[i], "adamw_big_%d" % i)
                        for i in range(1, len(big))]
    g_sm = [_scatter_cols(dh0[pad:pad + N_META]), _scatter_cols(gr["gdn_conv_w"]), _scatter_cols(gr["ffn_conv_w"])]
    g_small = jnp.stack([_pack([g[d] for g in g_sm], 8) for d in range(N_DEV)])
    slabs_small, = _all_to_all([g_small], "exchange_small_gradients")
    small_out = _adamw(slabs_small, _pack(small, 8), _pack([m_meta, m_gdn_conv_w, m_ffn_conv_w], 8),
                       _pack([v_meta, v_gdn_conv_w, v_ffn_conv_w], 8), "adamw_small_sharded")
    small_un = [_unpack(o, [a.shape for a in small]) for o in small_out]
    rep_w = [norm1, gdn_a_log, gdn_dt_bias, gdn_norm, norm2, ffn_conv_b, norm_f]
    rep_m = [m_norm1, m_gdn_a_log, m_gdn_dt_bias, m_gdn_norm, m_norm2, m_ffn_conv_b, m_norm_f]
    rep_v = [v_norm1, v_gdn_a_log, v_gdn_dt_bias, v_gdn_norm, v_norm2, v_ffn_conv_b, v_norm_f]
    rep_g = [gr["norm1"], gr["a_log"], gr["dt_bias"], gr["gdn_norm"], gr["norm2"], gr["ffn_conv_b"], gr["norm_f"]]
    rep_slabs, = _all_gather([_pack(rep_g, 8)], "gather_small_gradients")
    rep_out = _adamw(rep_slabs, _pack(rep_w, 8), _pack(rep_m, 8), _pack(rep_v, 8), "adamw_replicated")
    rep_shapes = [a.shape for a in rep_w]
    rp_g, rp_d, rp_nm, rp_nv = [_unpack(o, rep_shapes) for o in rep_out]

    slabs_w_in, = _split_wait(pending["w_in"], "exchange_w_in_wait", rep_out[0])
    big_out[0] = _adamw(slabs_w_in, big[0], big_m[0], big_v[0], "adamw_big_0")
    back = lambda a: jnp.swapaxes(a, 0, 1)[None]
    sh_g, sh_d, sh_nm, sh_nv = [
        [small_un[j][0], back(big_out[0][j]), small_un[j][1], big_out[1][j][None], back(big_out[2][j]),
         small_un[j][2], big_out[3][j][None]] for j in range(4)]

    def order(sh, rp):
        return [sh[0], rp[0], sh[1], sh[2], rp[1], rp[2], rp[3], sh[3], rp[4], sh[4], sh[5], rp[5], sh[6], rp[6]]

    return (loss, grad_x, *order(sh_g, rp_g), *order(sh_d, rp_d), *order(sh_nm, rp_nm), *order(sh_nv, rp_nv))
```

```python
import functools
import math

import numpy as np
import jax
import jax.numpy as jnp
from jax import lax
from jax.experimental import pallas as pl
from jax.experimental.pallas import tpu as pltpu

F32 = jnp.float32
BF16 = jnp.bfloat16

D_MODEL = 1024
N_META = 16
CHUNK = 64
GDN_H = 8
GDN_D = 128
RET_H = 4
RET_D = 256
D_FF = 2816
GDN_CONV = 4
FFN_CONV = 3
ROPE_BASE = 10000.0
EPS = 1e-6
N_DEV = 8
LANES = 128
MAIN_W = 10 * 1024
_O_GQ, _O_GZ, _O_GA, _O_RQ, _O_RG, _O_GATE, _O_END = 0, 3072, 4096, 4112, 7184, 8208, 10256

ADAM_LR = 0.001
ADAM_B1 = 0.9
ADAM_B2 = 0.999
ADAM_EPS = 1e-08
ADAM_WD = 0.01
ADAM_STEP = 10

MESH_T = pl.DeviceIdType.MESH


def _tile(n, target, mult):
    best = None
    for d in range(mult, min(n, target) + 1, mult):
        if n % d == 0:
            best = d
    assert best is not None, (n, target, mult)
    return best


def _sig(x):
    return 0.5 * jnp.tanh(0.5 * x) + 0.5


def _d(a, b):
    return jnp.dot(a.astype(BF16), b.astype(BF16), preferred_element_type=F32)


def _dnt(a, b):
    return lax.dot_general(a.astype(BF16), b.astype(BF16), (((1,), (1,)), ((), ())), preferred_element_type=F32)


def _dtn(a, b):
    return lax.dot_general(a.astype(BF16), b.astype(BF16), (((0,), (0,)), ((), ())), preferred_element_type=F32)


def _dxg(a, b, dims):
    f = functools.partial(lax.dot_general, dimension_numbers=dims, preferred_element_type=F32)
    ab = a.astype(BF16)
    b1 = b.astype(BF16)
    r1 = b - b1.astype(F32)
    b2 = r1.astype(BF16)
    b3 = (r1 - b2.astype(F32)).astype(BF16)
    return f(ab, b1) + (f(ab, b2) + f(ab, b3))


def _dx(a, b):
    return _dxg(a, b, (((1,), (0,)), ((), ())))


def _dxnt(a, b):
    return _dxg(a, b, (((1,), (1,)), ((), ())))


def _split(a):
    hi = a.astype(BF16)
    return hi, (a - hi.astype(F32)).astype(BF16)


def _d3g(a, b, dims):
    ah, al = _split(a)
    bh, bl = _split(b)
    f = functools.partial(lax.dot_general, dimension_numbers=dims, preferred_element_type=F32)
    if dims == _NN:
        rows = a.shape[0]
        both = f(jnp.concatenate([ah, al], axis=0), bh)
        return both[:rows] + (f(ah, bl) + both[rows:])
    return f(ah, bh) + (f(ah, bl) + f(al, bh))


def _d2x(a, b, dims):
    ah, al = _split(a)
    f = functools.partial(lax.dot_general, dimension_numbers=dims, preferred_element_type=F32)
    bb = b.astype(BF16)
    return f(ah, bb) + f(al, bb)


_NN = (((1,), (0,)), ((), ()))
_NT = (((1,), (1,)), ((), ()))
_TN = (((0,), (0,)), ((), ()))


def _rowsum(x):
    return jnp.sum(x, axis=1, keepdims=True)


def _allsum(x):
    return jnp.sum(jnp.sum(x, axis=1, keepdims=True), axis=0, keepdims=True)


def _mm_nn(a, b, res=None, out_dtype=F32, bt=False, tm_target=704, b_rows=None, name="mm_nn"):
    M, K = a.shape
    N = b.shape[0] if bt else b.shape[1]
    tm = _tile(M, tm_target, 16)
    if b_rows is None:
        tn = _tile(N, 2816, 128)
    else:
        tn, n_tiles, start = b_rows
        N = tn * n_tiles

    def body(*refs):
        if res is None:
            a_ref, b_ref, o_ref = refs
        else:
            a_ref, b_ref, r_ref, o_ref = refs
        acc = lax.dot_general(a_ref[...], b_ref[...], _NT if bt else _NN, preferred_element_type=F32)
        if res is not None:
            acc = acc + r_ref[...]
        o_ref[...] = acc.astype(out_dtype)

    b_spec = pl.BlockSpec((tn, K), lambda j, i: (j, 0)) if bt else pl.BlockSpec((K, tn), lambda j, i: (0, j))
    if b_rows is not None:
        b_spec = pl.BlockSpec((pl.Element(tn), pl.Element(K)), lambda j, i: (pl.multiple_of(start(j), 16), 0))
    in_specs = [pl.BlockSpec((tm, K), lambda j, i: (i, 0)), b_spec]
    args = [a, b]
    if res is not None:
        in_specs.append(pl.BlockSpec((tm, tn), lambda j, i: (i, j)))
        args.append(res)
    return pl.pallas_call(
        body, grid=(N // tn, M // tm), in_specs=in_specs,
        out_specs=pl.BlockSpec((tm, tn), lambda j, i: (i, j)),
        out_shape=jax.ShapeDtypeStruct((M, N), out_dtype), name=name)(*args)


def _mm_sum(pairs, name):
    M = pairs[0][0].shape[0]
    N = pairs[0][1].shape[1]
    tm = _tile(M, 704, 16)
    n = len(pairs)

    def body(*refs):
        o_ref = refs[-1]
        acc = jnp.dot(refs[0][...], refs[1][...], preferred_element_type=F32)
        for i in range(1, n):
            acc = acc + jnp.dot(refs[2 * i][...], refs[2 * i + 1][...], preferred_element_type=F32)
        o_ref[...] = acc

    specs, args = [], []
    for a, b in pairs:
        specs += [pl.BlockSpec((tm, a.shape[1]), lambda i: (i, 0)),
                  pl.BlockSpec(b.shape, lambda i: (0, 0), pipeline_mode=pl.Buffered(1))]
        args += [a, b]
    return pl.pallas_call(
        body, grid=(M // tm,), in_specs=specs, out_specs=pl.BlockSpec((tm, N), lambda i: (i, 0)),
        out_shape=jax.ShapeDtypeStruct((M, N), F32), name=name)(*args)


def _mm_nt(a, b, res=None, name="mm_nt"):
    M, Nc = a.shape
    K = b.shape[0]
    tm = _tile(M, 704, 16)
    tc = _tile(Nc, 5632, 128)

    def body(*refs):
        if res is None:
            a_ref, b_ref, o_ref = refs
        else:
            a_ref, b_ref, r_ref, o_ref = refs
        c = pl.program_id(1)
        p = lax.dot_general(a_ref[...], b_ref[...], (((1,), (1,)), ((), ())), preferred_element_type=F32)

        @pl.when(c == 0)
        def _():
            if res is None:
                o_ref[...] = p
            else:
                o_ref[...] = p + r_ref[...]

        @pl.when(c > 0)
        def _():
            o_ref[...] += p

    in_specs = [pl.BlockSpec((tm, tc), lambda i, c: (i, c)), pl.BlockSpec((K, tc), lambda i, c: (0, c))]
    args = [a, b]
    if res is not None:
        in_specs.append(pl.BlockSpec((tm, K), lambda i, c: (i, 0)))
        args.append(res)
    return pl.pallas_call(
        body, grid=(M // tm, Nc // tc), in_specs=in_specs,
        out_specs=pl.BlockSpec((tm, K), lambda i, c: (i, 0)),
        out_shape=jax.ShapeDtypeStruct((M, K), F32), name=name)(*args)


def _mm_tn(a, b, out_dtype=F32, name="mm_tn"):
    M, K = a.shape
    N = b.shape[1]
    tm = _tile(M, 2752, 16)
    tk = _tile(K, 1408, 128)
    tn = _tile(N, 1408, 128)
    steps = M // tm

    def body(a_ref, b_ref, o_ref, *scratch):
        acc = scratch[0] if scratch else o_ref
        m = pl.program_id(2)
        p = lax.dot_general(a_ref[...], b_ref[...], (((0,), (0,)), ((), ())), preferred_element_type=F32)

        @pl.when(m == 0)
        def _():
            acc[...] = p

        @pl.when(m > 0)
        def _():
            acc[...] += p

        if scratch:
            @pl.when(m == steps - 1)
            def _():
                o_ref[...] = acc[...].astype(out_dtype)

    return pl.pallas_call(
        body, grid=(K // tk, N // tn, steps),
        in_specs=[pl.BlockSpec((tm, tk), lambda kk, j, m: (m, kk)), pl.BlockSpec((tm, tn), lambda kk, j, m: (m, j))],
        out_specs=pl.BlockSpec((tk, tn), lambda kk, j, m: (kk, j)),
        out_shape=jax.ShapeDtypeStruct((K, N), out_dtype),
        scratch_shapes=[] if out_dtype == F32 else [pltpu.VMEM((tk, tn), F32)], name=name)(a, b)


class _Rows:
    def __init__(self, body, first, head=None):
        self.body, self.first, self.head = body, first, head
        self.shape = (body.shape[0] + first, body.shape[1])


def _rows_operands(x, tr):
    if not isinstance(x, _Rows):
        return [x], [pl.BlockSpec((tr, x.shape[1]), lambda i: (i, 0))]
    assert x.first % 8 == 0 and x.first <= tr <= x.body.shape[0] and x.shape[0] % tr == 0
    width = x.shape[1]
    args = [x.body]
    specs = [pl.BlockSpec((pl.Element(tr), pl.Element(width)),
                          lambda i: (pl.multiple_of(jnp.maximum(i * tr - x.first, 0), 8), 0))]
    if x.head is not None:
        args.append(jnp.pad(x.head, ((0, tr - x.first), (0, 0))))
        specs.append(pl.BlockSpec((tr, width), lambda i: (0, 0)))
    return args, specs


def _rows_tile(x, refs, i, tr):
    blk = refs[0][...]
    if not isinstance(x, _Rows):
        return blk
    shifted = pltpu.roll(blk, x.first, 0)
    if x.head is not None:
        row = lax.broadcasted_iota(jnp.int32, (tr, 1), 0)
        shifted = jnp.where(row < x.first, refs[1][...], shifted)
    return jnp.where(i == 0, shifted, blk)


def _rms_fwd(x, g, name):
    Lp = x.shape[0]
    tr = _tile(Lp, 256, 16)
    args, specs = _rows_operands(x, tr)
    n = len(args)

    def body(*refs):
        g_ref, o_ref = refs[n:]
        xv = _rows_tile(x, refs[:n], pl.program_id(0), tr)
        r = lax.rsqrt(jnp.mean(xv * xv, axis=-1, keepdims=True) + EPS)
        o_ref[...] = (xv * r * g_ref[...]).astype(BF16)

    return pl.pallas_call(
        body, grid=(Lp // tr,),
        in_specs=specs + [pl.BlockSpec((1, D_MODEL), lambda i: (0, 0))],
        out_specs=pl.BlockSpec((tr, D_MODEL), lambda i: (i, 0)),
        out_shape=jax.ShapeDtypeStruct((Lp, D_MODEL), BF16), name=name)(*args, g)


class _Producer:
    def __init__(self, a, b, res=None):
        self.a, self.b, self.res = a, b, res
        self.tr = _tile(a.shape[0], 704, 16)
        K = a.shape[1]
        r_args, r_specs = ([], []) if res is None else _rows_operands(res, self.tr)
        self.args = [a, b] + r_args
        self.specs = [pl.BlockSpec((self.tr, K), lambda i: (i, 0)),
                      pl.BlockSpec((K, D_MODEL), lambda i: (0, 0), pipeline_mode=pl.Buffered(1))] + r_specs

    def tile(self, refs, i):
        acc = jnp.dot(refs[0][...], refs[1][...], preferred_element_type=F32)
        return acc if self.res is None else acc + _rows_tile(self.res, refs[2:], i, self.tr)


def _rms_bwd(x, g, dy, dres, pad, name):
    Lp = x.shape[0]
    fused = isinstance(dy, _Producer)
    tr = dy.tr if fused else _tile(Lp, 256, 16)
    n = len(dy.args) if fused else 1
    x_args, x_specs = _rows_operands(x, tr)
    nx = len(x_args)

    def body(*refs):
        g_ref, dr_ref, dx_ref, dxb_ref, dg_ref = refs[n + nx:]
        i = pl.program_id(0)
        xv = _rows_tile(x, refs[n:n + nx], i, tr)
        r = lax.rsqrt(jnp.mean(xv * xv, axis=-1, keepdims=True) + EPS)
        xh = xv * r
        dyv = dy.tile(refs[:n], i) if fused else refs[0][...]
        dxh = dyv * g_ref[...]
        dx = r * (dxh - xh * jnp.mean(dxh * xh, axis=-1, keepdims=True)) + dr_ref[...]
        row = i * tr + lax.broadcasted_iota(jnp.int32, (tr, 1), 0)
        dx = jnp.where(row >= pad, dx, 0.0)
        dx_ref[...] = dx
        dxb_ref[...] = dx.astype(BF16)
        part = jnp.sum(dyv * xh, axis=0, keepdims=True)

        @pl.when(i == 0)
        def _():
            dg_ref[...] = part

        @pl.when(i > 0)
        def _():
            dg_ref[...] += part

    blk = pl.BlockSpec((tr, D_MODEL), lambda i: (i, 0))
    vec = pl.BlockSpec((1, D_MODEL), lambda i: (0, 0))
    return pl.pallas_call(
        body, grid=(Lp // tr,), in_specs=(dy.specs if fused else [blk]) + x_specs + [vec, blk],
        out_specs=[blk, blk, vec],
        out_shape=[jax.ShapeDtypeStruct((Lp, D_MODEL), F32), jax.ShapeDtypeStruct((Lp, D_MODEL), BF16),
                   jax.ShapeDtypeStruct((1, D_MODEL), F32)], name=name)(*(dy.args if fused else [dy]), *x_args, g, dres)


def _final(h2, g, tgt, first_row):
    fused = isinstance(h2, _Producer)
    Lp = h2.a.shape[0] if fused else h2.shape[0]
    tr = h2.tr if fused else _tile(Lp, 256, 16)
    n = len(h2.args) if fused else 1
    t_args, t_specs = _rows_operands(tgt, tr)
    nt = len(t_args)

    def body(*refs):
        g_ref = refs[n]
        loss_ref, dx_ref, dxb_ref, dg_ref = refs[n + 1 + nt:]
        i = pl.program_id(0)
        xv = h2.tile(refs[:n], i) if fused else refs[0][...]
        tv = _rows_tile(tgt, refs[n + 1:n + 1 + nt], i, tr)
        gv = g_ref[...]
        r = lax.rsqrt(jnp.mean(xv * xv, axis=-1, keepdims=True) + EPS)
        xh = xv * r
        row = i * tr + lax.broadcasted_iota(jnp.int32, (tr, 1), 0)
        err = jnp.where(row >= first_row, xh * gv - tv, 0.0)
        lpart = jnp.sum(err * err, axis=0, keepdims=True) * (0.5 / D_MODEL)
        dyv = err * (1.0 / D_MODEL)
        dxh = dyv * gv
        dx = r * (dxh - xh * jnp.mean(dxh * xh, axis=-1, keepdims=True))
        dx_ref[...] = dx
        dxb_ref[...] = dx.astype(BF16)
        part = jnp.sum(dyv * xh, axis=0, keepdims=True)

        @pl.when(i == 0)
        def _():
            dg_ref[...] = part
            loss_ref[...] = lpart

        @pl.when(i > 0)
        def _():
            dg_ref[...] += part
            loss_ref[...] += lpart

    blk = pl.BlockSpec((tr, D_MODEL), lambda i: (i, 0))
    vec = pl.BlockSpec((1, D_MODEL), lambda i: (0, 0))
    return pl.pallas_call(
        body, grid=(Lp // tr,), in_specs=(h2.specs if fused else [blk]) + [vec] + t_specs,
        out_specs=[vec, blk, blk, vec],
        out_shape=[jax.ShapeDtypeStruct((1, D_MODEL), F32), jax.ShapeDtypeStruct((Lp, D_MODEL), F32),
                   jax.ShapeDtypeStruct((Lp, D_MODEL), BF16), jax.ShapeDtypeStruct((1, D_MODEL), F32)],
        name="final_norm_loss")(*(h2.args if fused else [h2]), g, *t_args)


def _halo_prev(tr, width, col=0):
    return pl.BlockSpec((8, width), lambda i: (jnp.maximum(i * (tr // 8) - 1, 0), col))


def _halo_next(tr, width, nrows, col=0, rows=8):
    last = nrows // rows - 1
    return pl.BlockSpec((rows, width), lambda i: (jnp.minimum((i + 1) * (tr // rows), last), col))


def _shifted(x, offs):
    n = x.shape[0]
    return [x if off == 0 else pltpu.roll(x, n - off, 0) for off in offs]


def _taps(wins, w, rows, bias=None):
    acc = w[0:1, :] * wins[0][0:rows, :]
    if bias is not None:
        acc = acc + bias
    for kk in range(1, len(wins)):
        acc = acc + w[kk:kk + 1, :] * wins[kk][0:rows, :]
    return acc


def _gdn_pre(proj_m, proj_s, conv_w, gparams, pad):
    Lp = proj_m.shape[0]
    tr = _tile(Lp, 192, 64)
    W3 = 3 * D_MODEL

    def body(main_ref, prev_ref, s_ref, w_ref, gp_ref, qkv_ref, gsm_ref, c_ref):
        i = pl.program_id(0)
        prev = jnp.where(i > 0, prev_ref[...], 0.0)
        ext = jnp.concatenate([prev, main_ref[...]], axis=0)
        c = _taps(_shifted(ext, range(8 - (GDN_CONV - 1), 9)), w_ref[...], tr)
        c_ref[...] = c.astype(BF16)
        s = c * _sig(c)
        scale = GDN_D ** -0.5
        for j in range(2 * GDN_H):
            seg = s[:, j * GDN_D:(j + 1) * GDN_D]
            r = lax.rsqrt(_rowsum(seg * seg) + EPS)
            if j < GDN_H:
                r = r * scale
            qkv_ref[:, j * GDN_D:(j + 1) * GDN_D] = seg * r
        qkv_ref[:, 2 * D_MODEL:] = s[:, 2 * D_MODEL:]
        sm = s_ref[...]
        gp = gp_ref[...]
        lane = lax.broadcasted_iota(jnp.int32, sm.shape, 1)
        z = sm + gp[1:2, :]
        softplus = jnp.maximum(z, 0.0) + jnp.log(1.0 + jnp.exp(-jnp.abs(z)))
        lg = -jnp.exp(gp[0:1, :]) * softplus
        row = i * tr + lax.broadcasted_iota(jnp.int32, (tr, 1), 0)
        out = jnp.where(lane < GDN_H, lg, jnp.where(lane < 2 * GDN_H, _sig(sm), 0.0))
        gsm_ref[...] = jnp.where(row >= pad, out, 0.0)

    return pl.pallas_call(
        body, grid=(Lp // tr,),
        in_specs=[pl.BlockSpec((tr, W3), lambda i: (i, 0)), _halo_prev(tr, W3),
                  pl.BlockSpec((tr, LANES), lambda i: (i, 0)),
                  pl.BlockSpec((GDN_CONV, W3), lambda i: (0, 0)), pl.BlockSpec((8, LANES), lambda i: (0, 0))],
        out_specs=[pl.BlockSpec((tr, W3), lambda i: (i, 0)), pl.BlockSpec((tr, LANES), lambda i: (i, 0)),
                   pl.BlockSpec((tr, W3), lambda i: (i, 0))],
        out_shape=[jax.ShapeDtypeStruct((Lp, W3), F32), jax.ShapeDtypeStruct((Lp, LANES), F32),
                   jax.ShapeDtypeStruct((Lp, W3), BF16)],
        name="gdn_pre")(proj_m, proj_m, proj_s, conv_w, gparams)


def _gdn_pre_bwd(proj_m, conv_out, proj_s, conv_w, gparams, dq, dk, dv, dgs, pad):
    Lp = proj_m.shape[0]
    tr = _tile(Lp, 192, 64)
    W3 = 3 * D_MODEL
    te = tr + 8

    def body(main_ref, c_ref, cn_ref, s_ref, w_ref, gp_ref,
             dq_ref, dqn_ref, dk_ref, dkn_ref, dv_ref, dvn_ref, dgs_ref,
             da_ref, ds_ref, dw_ref, dgp_ref):
        i = pl.program_id(0)
        w = w_ref[...]
        c = jnp.concatenate([c_ref[...].astype(F32), cn_ref[...].astype(F32)[0:8]], axis=0)
        sg = _sig(c)
        s = c * sg
        rowe = i * tr + lax.broadcasted_iota(jnp.int32, (te, 1), 0)
        live = (rowe >= pad) & (rowe < Lp)
        dqe = jnp.concatenate([dq_ref[...], dqn_ref[...]], axis=0)
        dke = jnp.concatenate([dk_ref[...], dkn_ref[...]], axis=0)
        dve = jnp.concatenate([dv_ref[...], dvn_ref[...]], axis=0)
        scale = GDN_D ** -0.5
        parts = []
        for j in range(2 * GDN_H):
            seg = s[:, j * GDN_D:(j + 1) * GDN_D]
            r = lax.rsqrt(_rowsum(seg * seg) + EPS)
            xh = seg * r
            if j < GDN_H:
                dxh = dqe[:, j * GDN_D:(j + 1) * GDN_D] * scale
            else:
                dxh = dke[:, (j - GDN_H) * GDN_D:(j - GDN_H + 1) * GDN_D]
            parts.append(r * (dxh - xh * _rowsum(dxh * xh)))
        parts.append(dve)
        dsv = jnp.concatenate(parts, axis=1)
        dc = jnp.where(live, dsv * (sg * (1.0 + c * (1.0 - sg))), 0.0)
        dcs = _shifted(dc, range(GDN_CONV - 1, -1, -1))
        da_ref[...] = _taps(dcs, w, tr).astype(BF16)
        pm = main_ref[...]
        rows = [jnp.sum(dcs[kk][0:tr, :] * pm, axis=0, keepdims=True) for kk in range(GDN_CONV)]
        dwp = jnp.concatenate(rows + [jnp.zeros((8 - GDN_CONV, W3), F32)], axis=0)

        sm = s_ref[...]
        gp = gp_ref[...]
        lane = lax.broadcasted_iota(jnp.int32, sm.shape, 1)
        rowm = i * tr + lax.broadcasted_iota(jnp.int32, (tr, 1), 0)
        dgv = jnp.where(rowm >= pad, dgs_ref[...], 0.0)
        dlg = jnp.where(lane < GDN_H, dgv, 0.0)
        dbt = jnp.where((lane >= GDN_H) & (lane < 2 * GDN_H), dgv, 0.0)
        z = sm + gp[1:2, :]
        softplus = jnp.maximum(z, 0.0) + jnp.log(1.0 + jnp.exp(-jnp.abs(z)))
        ea = jnp.exp(gp[0:1, :])
        dz = dlg * (-ea) * _sig(z)
        dal = dlg * (-ea) * softplus
        bt = _sig(sm)
        dgb = dbt * bt * (1.0 - bt)
        ds_ref[...] = (dz + dgb).astype(BF16)
        gpp = jnp.concatenate([jnp.sum(dal, axis=0, keepdims=True), jnp.sum(dz, axis=0, keepdims=True),
                               jnp.zeros((6, LANES), F32)], axis=0)

        @pl.when(i == 0)
        def _():
            dw_ref[...] = dwp
            dgp_ref[...] = gpp

        @pl.when(i > 0)
        def _():
            dw_ref[...] += dwp
            dgp_ref[...] += gpp

    m3 = pl.BlockSpec((tr, W3), lambda i: (i, 0))
    m1 = pl.BlockSpec((tr, D_MODEL), lambda i: (i, 0))
    n1 = _halo_next(tr, D_MODEL, Lp)
    return pl.pallas_call(
        body, grid=(Lp // tr,),
        in_specs=[m3, m3, _halo_next(tr, W3, Lp, rows=16), pl.BlockSpec((tr, LANES), lambda i: (i, 0)),
                  pl.BlockSpec((GDN_CONV, W3), lambda i: (0, 0)), pl.BlockSpec((8, LANES), lambda i: (0, 0)),
                  m1, n1, m1, n1, m1, n1, pl.BlockSpec((tr, LANES), lambda i: (i, 0))],
        out_specs=[m3, pl.BlockSpec((tr, LANES), lambda i: (i, 0)),
                   pl.BlockSpec((8, W3), lambda i: (0, 0)), pl.BlockSpec((8, LANES), lambda i: (0, 0))],
        out_shape=[jax.ShapeDtypeStruct((Lp, W3), BF16), jax.ShapeDtypeStruct((Lp, LANES), BF16),
                   jax.ShapeDtypeStruct((8, W3), F32), jax.ShapeDtypeStruct((8, LANES), F32)],
        name="gdn_pre_bwd")(proj_m, conv_out, conv_out, proj_s, conv_w, gparams, dq, dq, dk, dk, dv, dv, dgs)


def _gdn_gates(gs):
    ri = lax.broadcasted_iota(jnp.int32, (CHUNK, CHUNK), 0)
    ci = lax.broadcasted_iota(jnp.int32, (CHUNK, CHUNK), 1)
    tril = ri >= ci
    strict = ri > ci
    gall = _dx(tril.astype(F32), gs)
    lane8 = lax.broadcasted_iota(jnp.int32, (8, LANES), 1)
    sub8 = lax.broadcasted_iota(jnp.int32, (8, LANES), 0)
    grow = _dxnt((lane8 == sub8).astype(F32), gall)
    return gall, grow, tril, strict


def _gdn_decay(gall, grow, tril, h):
    g = gall[:, h:h + 1]
    return g, jnp.where(tril, jnp.exp(jnp.where(tril, g - grow[h:h + 1, :], 0.0)), 0.0)


def _group(N):
    return 3 if N % 3 == 0 else (2 if N % 2 == 0 else 1)


def _gdn_chunk_specs(N, rev):
    G = _group(N)
    nb = N // G
    cn = (lambda n: nb - 1 - n) if rev else (lambda n: n)
    col = lambda j: pl.BlockSpec((G * CHUNK, D_MODEL), lambda n: (cn(n), j))
    gate = pl.BlockSpec((G * CHUNK, LANES), lambda n: (cn(n), 0))
    st = lambda a, b: pl.BlockSpec((GDN_H, G, a, b), lambda n: (0, cn(n), 0, 0))
    return G, nb, col, gate, st


def _gdn_chunk_fwd(qkv, gsm):
    Lp = qkv.shape[0]
    N = Lp // CHUNK
    G, nb, col, gate, st = _gdn_chunk_specs(N, False)

    def body(q_ref, k_ref, v_ref, gs_ref, o_ref, sin_ref, t_ref, S):
        n = pl.program_id(0)

        @pl.when(n == 0)
        def _():
            S[...] = jnp.zeros_like(S)

        ri = lax.broadcasted_iota(jnp.int32, (CHUNK, CHUNK), 0)
        ci = lax.broadcasted_iota(jnp.int32, (CHUNK, CHUNK), 1)
        eye = (ri == ci).astype(F32)
        heads = range(GDN_H)
        sls = [slice(h * GDN_D, (h + 1) * GDN_D) for h in heads]
        rows = [slice(c * CHUNK, (c + 1) * CHUNK) for c in range(G)]
        pairs = [(c, h) for c in range(G) for h in heads]
        P = lambda f: {p: f(*p) for p in pairs}
        gs = [gs_ref[rows[c], :] for c in range(G)]
        gates = [_gdn_gates(gs[c]) for c in range(G)]
        tril, strict = gates[0][2], gates[0][3]
        q = P(lambda c, h: q_ref[rows[c], sls[h]])
        k = P(lambda c, h: k_ref[rows[c], sls[h]])
        v = P(lambda c, h: v_ref[rows[c], sls[h]])
        beta = P(lambda c, h: gs[c][:, GDN_H + h:GDN_H + h + 1])
        gg = P(lambda c, h: _gdn_decay(gates[c][0], gates[c][1], tril, h))
        g = {p: x[0] for p, x in gg.items()}
        gam = {p: x[1] for p, x in gg.items()}
        eg = P(lambda c, h: jnp.exp(g[c, h]))
        gl = P(lambda c, h: g[c, h][CHUNK - 1:CHUNK, :])
        kb = P(lambda c, h: k[c, h] * beta[c, h])
        pw = P(lambda c, h: -jnp.where(strict, _dnt(kb[c, h], k[c, h]) * gam[c, h], 0.0))
        p = P(lambda c, h: _dnt(q[c, h], k[c, h]) * gam[c, h])
        t = P(lambda c, h: eye + pw[c, h])
        for it in range(5):
            mm = _d3g if it < 2 else (lambda a, b, dims: _d(a, b))
            pw = P(lambda c, h: mm(pw[c, h], pw[c, h], _NN))
            t = P(lambda c, h: t[c, h] + mm(t[c, h], pw[c, h], _NN))
        u = P(lambda c, h: _d(t[c, h], v[c, h] * beta[c, h]))
        w = P(lambda c, h: _d(t[c, h], kb[c, h] * eg[c, h]))
        qg = P(lambda c, h: q[c, h] * eg[c, h])
        kd = P(lambda c, h: k[c, h] * jnp.exp(gl[c, h] - g[c, h]))
        egl = P(lambda c, h: jnp.exp(gl[c, h]))
        for c in range(G):
            for h in heads:
                t_ref[h, c] = t[c, h]
        cur = [S[h] for h in heads]
        for c in range(G):
            vnew = [u[c, h] - _d(w[c, h], cur[h]) for h in heads]
            for h in heads:
                o_ref[rows[c], sls[h]] = _d(qg[c, h], cur[h]) + _d(p[c, h], vnew[h])
                sin_ref[h, c] = cur[h]
            cur = [cur[h] * egl[c, h] + _dtn(kd[c, h], vnew[h]) for h in heads]
        for h in heads:
            S[h] = cur[h]

    return pl.pallas_call(
        body, grid=(nb,),
        in_specs=[col(0), col(1), col(2), gate],
        out_specs=[col(0), st(GDN_D, GDN_D), st(CHUNK, CHUNK)],
        out_shape=[jax.ShapeDtypeStruct((Lp, D_MODEL), F32), jax.ShapeDtypeStruct((GDN_H, N, GDN_D, GDN_D), F32),
                   jax.ShapeDtypeStruct((GDN_H, N, CHUNK, CHUNK), F32)],
        scratch_shapes=[pltpu.VMEM((GDN_H, GDN_D, GDN_D), F32)],
        name="gdn_chunk_fwd")(qkv, qkv, qkv, gsm)


def _gdn_chunk_bwd(qkv, gsm, do, s_in, t_in):
    Lp = qkv.shape[0]
    N = Lp // CHUNK
    G, nb, col, gate, st = _gdn_chunk_specs(N, True)

    def body(q_ref, k_ref, v_ref, gs_ref, do_ref, sin_ref, t_ref, dq_ref, dk_ref, dv_ref, dgs_ref, dS):
        n = pl.program_id(0)

        @pl.when(n == 0)
        def _():
            dS[...] = jnp.zeros_like(dS)

        lane = lax.broadcasted_iota(jnp.int32, (CHUNK, LANES), 1)
        rcol = lax.broadcasted_iota(jnp.int32, (CHUNK, 1), 0)
        ri = lax.broadcasted_iota(jnp.int32, (CHUNK, CHUNK), 0)
        ci = lax.broadcasted_iota(jnp.int32, (CHUNK, CHUNK), 1)
        ones = jnp.ones((CHUNK, LANES), F32)
        heads = range(GDN_H)
        sls = [slice(h * GDN_D, (h + 1) * GDN_D) for h in heads]
        rows = [slice(c * CHUNK, (c + 1) * CHUNK) for c in range(G)]
        pairs = [(c, h) for c in range(G) for h in heads]
        P = lambda f: {p: f(*p) for p in pairs}
        gs = [gs_ref[rows[c], :] for c in range(G)]
        gates = [_gdn_gates(gs[c]) for c in range(G)]
        tril, strict = gates[0][2], gates[0][3]
        q = P(lambda c, h: q_ref[rows[c], sls[h]])
        k = P(lambda c, h: k_ref[rows[c], sls[h]])
        v = P(lambda c, h: v_ref[rows[c], sls[h]])
        dov = P(lambda c, h: do_ref[rows[c], sls[h]])
        s0 = P(lambda c, h: sin_ref[h, c])
        t = P(lambda c, h: t_ref[h, c])
        beta = P(lambda c, h: gs[c][:, GDN_H + h:GDN_H + h + 1])
        gg = P(lambda c, h: _gdn_decay(gates[c][0], gates[c][1], tril, h))
        g = {p: x[0] for p, x in gg.items()}
        gam = {p: x[1] for p, x in gg.items()}
        eg = P(lambda c, h: jnp.exp(g[c, h]))
        egl = P(lambda c, h: jnp.exp(g[c, h][CHUNK - 1:CHUNK, :]))
        e = P(lambda c, h: jnp.exp(g[c, h][CHUNK - 1:CHUNK, :] - g[c, h]))
        kb = P(lambda c, h: k[c, h] * beta[c, h])
        kbg = P(lambda c, h: kb[c, h] * eg[c, h])
        vb = P(lambda c, h: v[c, h] * beta[c, h])
        qg = P(lambda c, h: q[c, h] * eg[c, h])
        kd = P(lambda c, h: k[c, h] * e[c, h])
        m = P(lambda c, h: jnp.where(strict, _dnt(kb[c, h], k[c, h]) * gam[c, h], 0.0))
        u = P(lambda c, h: _d(t[c, h], vb[c, h]))
        w = P(lambda c, h: _d(t[c, h], kbg[c, h]))
        p = P(lambda c, h: _dnt(q[c, h], k[c, h]) * gam[c, h])
        dqg = P(lambda c, h: _dnt(dov[c, h], s0[c, h]))
        qgdo = P(lambda c, h: _dtn(qg[c, h], dov[c, h]))
        ptdo = P(lambda c, h: _dtn(p[c, h], dov[c, h]))
        vnew = P(lambda c, h: u[c, h] - _d(w[c, h], s0[c, h]))
        dp = P(lambda c, h: jnp.where(tril, _dnt(dov[c, h], vnew[c, h]), 0.0))
        cur = [dS[h] for h in heads]
        dvnew, dkd, sds = {}, {}, {}
        for c in reversed(range(G)):
            for h in heads:
                dvnew[c, h] = ptdo[c, h] + _d(kd[c, h], cur[h])
                dkd[c, h] = _dnt(vnew[c, h], cur[h])
                sds[c, h] = _allsum(s0[c, h] * cur[h])
            cur = [qgdo[c, h] + egl[c, h] * cur[h] - _dtn(w[c, h], dvnew[c, h]) for h in heads]
        for h in heads:
            dS[h] = cur[h]
        dw = P(lambda c, h: -_dnt(dvnew[c, h], s0[c, h]))
        dvb = P(lambda c, h: _dtn(t[c, h], dvnew[c, h]))
        dkbg = P(lambda c, h: _dtn(t[c, h], dw[c, h]))
        dt = P(lambda c, h: _dnt(dvnew[c, h], vb[c, h]) + _dnt(dw[c, h], kbg[c, h]))
        x1 = P(lambda c, h: _dtn(t[c, h], dt[c, h]))
        dm = P(lambda c, h: jnp.where(strict, -_dnt(x1[c, h], t[c, h]), 0.0))
        dkk = P(lambda c, h: dm[c, h] * gam[c, h])
        dqk = P(lambda c, h: dp[c, h] * gam[c, h])
        dkb = P(lambda c, h: _d(dkk[c, h], k[c, h]) + eg[c, h] * dkbg[c, h])
        em = P(lambda c, h: dm[c, h] * m[c, h] + dp[c, h] * p[c, h])
        colsum = P(lambda c, h: _d2x(em[c, h], ones, _TN)[:, 0:1])
        for c, h in pairs:
            dk_ref[rows[c], sls[h]] = (_dtn(dkk[c, h], kb[c, h]) + _dtn(dqk[c, h], q[c, h]) + dkd[c, h] * e[c, h]
                                       + beta[c, h] * dkb[c, h])
            dq_ref[rows[c], sls[h]] = _d(dqk[c, h], k[c, h]) + dqg[c, h] * eg[c, h]
            dv_ref[rows[c], sls[h]] = beta[c, h] * dvb[c, h]
        for c in range(G):
            dg_all = jnp.zeros((CHUNK, LANES), F32)
            dbeta_all = jnp.zeros((CHUNK, LANES), F32)
            for h in heads:
                dbeta = _rowsum(k[c, h] * dkb[c, h]) + _rowsum(v[c, h] * dvb[c, h])
                z = _rowsum(kd[c, h] * dkd[c, h])
                dg = (_rowsum(em[c, h]) - colsum[c, h] + _rowsum(qg[c, h] * dqg[c, h]) + _rowsum(kbg[c, h] * dkbg[c, h])
                      - z)
                extra = _allsum(z) + egl[c, h] * sds[c, h]
                dg = dg + jnp.where(rcol == CHUNK - 1, extra, 0.0)
                dg_all = dg_all + jnp.where(lane == h, dg, 0.0)
                dbeta_all = dbeta_all + jnp.where(lane == GDN_H + h, dbeta, 0.0)
            dgs_ref[rows[c], :] = _dx((ci >= ri).astype(F32), dg_all) + dbeta_all

    return pl.pallas_call(
        body, grid=(nb,),
        in_specs=[col(0), col(1), col(2), gate, col(0), st(GDN_D, GDN_D), st(CHUNK, CHUNK)],
        out_specs=[col(0), col(0), col(0), gate],
        out_shape=[jax.ShapeDtypeStruct((Lp, D_MODEL), F32)] * 3 + [jax.ShapeDtypeStruct((Lp, LANES), F32)],
        scratch_shapes=[pltpu.VMEM((GDN_H, GDN_D, GDN_D), F32)],
        name="gdn_chunk_bwd")(qkv, qkv, qkv, gsm, do, s_in, t_in)


def _rot(x, c, s):
    half = RET_D // 2
    x1 = x[:, :half]
    x2 = x[:, half:]
    return jnp.concatenate([x1 * c - x2 * s, x2 * c + x1 * s], axis=1)


def _rot_bwd(d, c, s):
    half = RET_D // 2
    d1 = d[:, :half]
    d2 = d[:, half:]
    return jnp.concatenate([d1 * c + d2 * s, d2 * c - d1 * s], axis=1)


def _ret_tables():
    hh = jnp.arange(RET_H, dtype=F32)
    lg = jnp.log(1.0 - 2.0 ** (-5.0 - hh))
    idx = jnp.arange(CHUNK, dtype=F32)
    tril = jnp.asarray(np.tril(np.ones((CHUNK, CHUNK), dtype=bool)))
    dmask = jnp.where(tril, jnp.exp((idx[:, None] - idx[None, :]) * lg[:, None, None]), 0.0)
    qdec = jnp.exp((idx[None, :] + 1.0) * lg[:, None])
    kdec = jnp.exp((CHUNK - 1.0 - idx[None, :]) * lg[:, None])
    gch = jnp.exp(CHUNK * lg)
    qdec = jnp.broadcast_to(qdec[:, :, None], (RET_H, CHUNK, RET_D))
    kdec = jnp.broadcast_to(kdec[:, :, None], (RET_H, CHUNK, RET_D))
    gch = jnp.broadcast_to(gch[:, None, None], (RET_H, 8, LANES))
    return dmask, qdec, kdec, gch


def _ret_specs(N, rev):
    G = _group(N)
    nb = N // G
    cn = (lambda n: nb - 1 - n) if rev else (lambda n: n)
    col = lambda j: pl.BlockSpec((G * CHUNK, D_MODEL), lambda n: (cn(n), j))
    tab = lambda a, b: pl.BlockSpec((RET_H, a, b), lambda n: (0, 0, 0))
    rope = pl.BlockSpec((G * CHUNK, LANES), lambda n: (cn(n), 0))
    st = pl.BlockSpec((RET_H, G, RET_D, RET_D), lambda n: (0, cn(n), 0, 0))
    return G, nb, col, tab, rope, st


def _ret_chunk_fwd(proj_m, cos, sin, tables):
    Lp = proj_m.shape[0]
    N = Lp // CHUNK
    dmask, qdec, kdec, gch = tables
    G, nb, col, tab, rope, st = _ret_specs(N, False)

    def body(q_ref, k_ref, v_ref, c_ref, s_ref, dm_ref, qd_ref, kd_ref, g_ref, o_ref, sin_ref, S):
        n = pl.program_id(0)

        @pl.when(n == 0)
        def _():
            S[...] = jnp.zeros_like(S)

        heads = range(RET_H)
        sls = [slice(h * RET_D, (h + 1) * RET_D) for h in heads]
        rows = [slice(c * CHUNK, (c + 1) * CHUNK) for c in range(G)]
        pairs = [(c, h) for c in range(G) for h in heads]
        P = lambda f: {p: f(*p) for p in pairs}
        qr = P(lambda c, h: _rot(q_ref[rows[c], sls[h]], c_ref[rows[c], :], s_ref[rows[c], :]))
        ks = P(lambda c, h: _rot(k_ref[rows[c], sls[h]], c_ref[rows[c], :], s_ref[rows[c], :]) * (RET_D ** -0.5))
        v = P(lambda c, h: v_ref[rows[c], sls[h]])
        a = P(lambda c, h: _dnt(qr[c, h], ks[c, h]) * dm_ref[h])
        av = P(lambda c, h: _d(a[c, h], v[c, h]))
        kv = P(lambda c, h: _dtn(ks[c, h] * kd_ref[h], v[c, h]))
        qd = P(lambda c, h: qr[c, h] * qd_ref[h])
        cur = [S[h] for h in heads]
        for c in range(G):
            for h in heads:
                o_ref[rows[c], sls[h]] = av[c, h] + _d(qd[c, h], cur[h])
                sin_ref[h, c] = cur[h].astype(BF16)
            cur = [cur[h] * g_ref[h, 0:1, 0:1] + kv[c, h] for h in heads]
        for h in heads:
            S[h] = cur[h]

    return pl.pallas_call(
        body, grid=(nb,),
        in_specs=[col(3), col(4), col(5), rope, rope,
                  tab(CHUNK, CHUNK), tab(CHUNK, RET_D), tab(CHUNK, RET_D), tab(8, LANES)],
        out_specs=[col(0), st],
        out_shape=[jax.ShapeDtypeStruct((Lp, D_MODEL), F32), jax.ShapeDtypeStruct((RET_H, N, RET_D, RET_D), BF16)],
        scratch_shapes=[pltpu.VMEM((RET_H, RET_D, RET_D), F32)],
        name="ret_chunk_fwd")(proj_m, proj_m, proj_m, cos, sin, dmask, qdec, kdec, gch)


def _ret_chunk_bwd(proj_m, cos, sin, tables, do, s_in):
    Lp = proj_m.shape[0]
    N = Lp // CHUNK
    dmask, qdec, kdec, gch = tables
    G, nb, col, tab, rope, st = _ret_specs(N, True)

    def body(q_ref, k_ref, v_ref, c_ref, s_ref, dm_ref, qd_ref, kd_ref, g_ref, do_ref, sin_ref,
             d_ref, dS):
        n = pl.program_id(0)

        @pl.when(n == 0)
        def _():
            dS[...] = jnp.zeros_like(dS)

        kscale = RET_D ** -0.5
        heads = range(RET_H)
        sls = [slice(h * RET_D, (h + 1) * RET_D) for h in heads]
        rows = [slice(c * CHUNK, (c + 1) * CHUNK) for c in range(G)]
        pairs = [(c, h) for c in range(G) for h in heads]
        P = lambda f: {p: f(*p) for p in pairs}
        cs = [(c_ref[rows[c], :], s_ref[rows[c], :]) for c in range(G)]
        osl = lambda part, h: slice(part * D_MODEL + h * RET_D, part * D_MODEL + (h + 1) * RET_D)
        qr = P(lambda c, h: _rot(q_ref[rows[c], sls[h]], *cs[c]))
        ks = P(lambda c, h: _rot(k_ref[rows[c], sls[h]], *cs[c]) * kscale)
        v = P(lambda c, h: v_ref[rows[c], sls[h]])
        dov = P(lambda c, h: do_ref[rows[c], sls[h]])
        ad = P(lambda c, h: _dnt(qr[c, h], ks[c, h]) * dm_ref[h])
        da = P(lambda c, h: _dnt(dov[c, h], v[c, h]) * dm_ref[h])
        dos = P(lambda c, h: _dnt(dov[c, h], sin_ref[h, c]) * qd_ref[h])
        qdo = P(lambda c, h: _dtn(qr[c, h] * qd_ref[h], dov[c, h]))
        adv = P(lambda c, h: _dtn(ad[c, h], dov[c, h]))
        dqr = P(lambda c, h: _d(da[c, h], ks[c, h]) + dos[c, h])
        daq = P(lambda c, h: _dtn(da[c, h], qr[c, h]))
        kk = P(lambda c, h: ks[c, h] * kd_ref[h])
        cur = [dS[h] for h in heads]
        for c in reversed(range(G)):
            for h in heads:
                d_ref[rows[c], osl(2, h)] = (adv[c, h] + _d(kk[c, h], cur[h])).astype(BF16)
                d_ref[rows[c], osl(0, h)] = _rot_bwd(dqr[c, h], *cs[c]).astype(BF16)
                dks = daq[c, h] + _dnt(v[c, h], cur[h]) * kd_ref[h]
                d_ref[rows[c], osl(1, h)] = _rot_bwd(dks * kscale, *cs[c]).astype(BF16)
            cur = [cur[h] * g_ref[h, 0:1, 0:1] + qdo[c, h] for h in heads]
        for h in heads:
            dS[h] = cur[h]

    return pl.pallas_call(
        body, grid=(nb,),
        in_specs=[col(3), col(4), col(5), rope, rope,
                  tab(CHUNK, CHUNK), tab(CHUNK, RET_D), tab(CHUNK, RET_D), tab(8, LANES), col(0), st],
        out_specs=pl.BlockSpec((G * CHUNK, 3 * D_MODEL), lambda n: (nb - 1 - n, 0)),
        out_shape=jax.ShapeDtypeStruct((Lp, 3 * D_MODEL), BF16),
        scratch_shapes=[pltpu.VMEM((RET_H, RET_D, RET_D), F32)],
        name="ret_chunk_bwd")(proj_m, proj_m, proj_m, cos, sin, dmask, qdec, kdec, gch, do, s_in)


def _merge_specs(tr):
    col = lambda j: pl.BlockSpec((tr, D_MODEL), lambda i: (i, j))
    return col


def _merge_fwd(o_a, o_b, proj_m, gnorm, out_proj=None):
    Lp = o_a.shape[0]
    tr = _tile(Lp, 192 if out_proj is None else 352, 16)
    if out_proj is not None:
        w_out, res, g2 = out_proj
        r_args, r_specs = _rows_operands(res, tr)

    def body(oa_ref, ob_ref, gz_ref, rg_ref, ga_ref, gb_ref, gn_ref, *rest):
        y_ref = rest[0] if out_proj is None else rest[-3]
        gn = gn_ref[...]
        oa = oa_ref[...]
        ob = ob_ref[...]
        gz = gz_ref[...]
        ya = []
        for j in range(GDN_H):
            seg = oa[:, j * GDN_D:(j + 1) * GDN_D]
            r = lax.rsqrt(jnp.mean(seg * seg, axis=-1, keepdims=True) + EPS)
            ya.append(seg * r * gn)
        ya = jnp.concatenate(ya, axis=1) * (gz * _sig(gz))
        yb = []
        for j in range(RET_H):
            seg = ob[:, j * RET_D:(j + 1) * RET_D]
            r = lax.rsqrt(jnp.mean(seg * seg, axis=-1, keepdims=True) + EPS)
            yb.append(seg * r)
        rg = rg_ref[...]
        yb = jnp.concatenate(yb, axis=1) * (rg * _sig(rg))
        yv = (_sig(ga_ref[...]) * ya + _sig(gb_ref[...]) * yb).astype(BF16)
        y_ref[...] = yv
        if out_proj is not None:
            wo_ref, g2_ref = rest[0], rest[1]
            h1_ref, hn2_ref = rest[-2], rest[-1]
            h1 = (jnp.dot(yv, wo_ref[...], preferred_element_type=F32)
                  + _rows_tile(res, rest[2:2 + len(r_args)], pl.program_id(0), tr))
            r = lax.rsqrt(jnp.mean(h1 * h1, axis=-1, keepdims=True) + EPS)
            h1_ref[...] = h1
            hn2_ref[...] = (h1 * r * g2_ref[...]).astype(BF16)

    col = _merge_specs(tr)
    in_specs = [col(0), col(0), col(6), col(7), col(8), col(9), pl.BlockSpec((1, GDN_D), lambda i: (0, 0))]
    args = [o_a, o_b, proj_m, proj_m, proj_m, proj_m, gnorm]
    if out_proj is None:
        return pl.pallas_call(body, grid=(Lp // tr,), in_specs=in_specs, out_specs=col(0),
                              out_shape=jax.ShapeDtypeStruct((Lp, D_MODEL), BF16), name="merge_fwd")(*args)
    in_specs += [pl.BlockSpec((D_MODEL, D_MODEL), lambda i: (0, 0), pipeline_mode=pl.Buffered(1)),
                 pl.BlockSpec((1, D_MODEL), lambda i: (0, 0))] + r_specs
    return pl.pallas_call(
        body, grid=(Lp // tr,), in_specs=in_specs, out_specs=[col(0), col(0), col(0)],
        out_shape=[jax.ShapeDtypeStruct((Lp, D_MODEL), BF16), jax.ShapeDtypeStruct((Lp, D_MODEL), F32),
                   jax.ShapeDtypeStruct((Lp, D_MODEL), BF16)],
        name="merge_out_proj_rms2")(*args, w_out, g2, *r_args)


def _merge_bwd(dh1b, w_out, o_a, o_b, proj_m, gnorm):
    Lp = o_a.shape[0]
    tr = _tile(Lp, 192, 16)

    def body(d_ref, wo_ref, oa_ref, ob_ref, gz_ref, rg_ref, ga_ref, gb_ref, gn_ref, dc_ref, doa_ref, dob_ref, dgn_ref):
        i = pl.program_id(0)
        gn = gn_ref[...]
        dyv = lax.dot_general(d_ref[...], wo_ref[...], _NT, preferred_element_type=F32)
        oa = oa_ref[...]
        ob = ob_ref[...]
        gz = gz_ref[...]
        rg = rg_ref[...]
        sa = _sig(ga_ref[...])
        sb = _sig(gb_ref[...])
        dya = dyv * sa
        dyb = dyv * sb
        sgz = _sig(gz)
        szz = gz * sgz
        dgn = jnp.zeros((1, GDN_D), F32)
        ya = []
        dgz = []
        for j in range(GDN_H):
            sl = slice(j * GDN_D, (j + 1) * GDN_D)
            seg = oa[:, sl]
            r = lax.rsqrt(jnp.mean(seg * seg, axis=-1, keepdims=True) + EPS)
            xh = seg * r
            oan = xh * gn
            ya.append(oan * szz[:, sl])
            dgz.append(dya[:, sl] * oan * (sgz[:, sl] * (1.0 + gz[:, sl] * (1.0 - sgz[:, sl]))))
            doan = dya[:, sl] * szz[:, sl]
            dgn = dgn + jnp.sum(doan * xh, axis=0, keepdims=True)
            dxh = doan * gn
            doa_ref[:, sl] = r * (dxh - xh * jnp.mean(dxh * xh, axis=-1, keepdims=True))
        ya = jnp.concatenate(ya, axis=1)
        srg = _sig(rg)
        srr = rg * srg
        yb = []
        drg = []
        for j in range(RET_H):
            sl = slice(j * RET_D, (j + 1) * RET_D)
            seg = ob[:, sl]
            r = lax.rsqrt(jnp.mean(seg * seg, axis=-1, keepdims=True) + EPS)
            xh = seg * r
            yb.append(xh * srr[:, sl])
            drg.append(dyb[:, sl] * xh * (srg[:, sl] * (1.0 + rg[:, sl] * (1.0 - srg[:, sl]))))
            dxh = dyb[:, sl] * srr[:, sl]
            dob_ref[:, sl] = r * (dxh - xh * jnp.mean(dxh * xh, axis=-1, keepdims=True))
        yb = jnp.concatenate(yb, axis=1)
        dc_ref[:, 0:D_MODEL] = jnp.concatenate(dgz, axis=1).astype(BF16)
        dc_ref[:, D_MODEL:2 * D_MODEL] = jnp.concatenate(drg, axis=1).astype(BF16)
        dc_ref[:, 2 * D_MODEL:3 * D_MODEL] = (dyv * ya * sa * (1.0 - sa)).astype(BF16)
        dc_ref[:, 3 * D_MODEL:] = (dyv * yb * sb * (1.0 - sb)).astype(BF16)

        @pl.when(i == 0)
        def _():
            dgn_ref[...] = dgn

        @pl.when(i > 0)
        def _():
            dgn_ref[...] += dgn

    col = _merge_specs(tr)
    return pl.pallas_call(
        body, grid=(Lp // tr,),
        in_specs=[col(0), pl.BlockSpec((D_MODEL, D_MODEL), lambda i: (0, 0), pipeline_mode=pl.Buffered(1)),
                  col(0), col(0), col(6), col(7), col(8), col(9), pl.BlockSpec((1, GDN_D), lambda i: (0, 0))],
        out_specs=[pl.BlockSpec((tr, 4 * D_MODEL), lambda i: (i, 0)), col(0), col(0),
                   pl.BlockSpec((1, GDN_D), lambda i: (0, 0))],
        out_shape=[jax.ShapeDtypeStruct((Lp, 4 * D_MODEL), BF16), jax.ShapeDtypeStruct((Lp, D_MODEL), F32),
                   jax.ShapeDtypeStruct((Lp, D_MODEL), F32), jax.ShapeDtypeStruct((1, GDN_D), F32)],
        name="merge_bwd")(dh1b, w_out, o_a, o_b, proj_m, proj_m, proj_m, proj_m, gnorm)


def _ffn_act(up, conv_w, conv_b):
    Lp = up.shape[0]
    tr = _tile(Lp, 192, 16)
    W2 = 2 * D_FF

    def body(main_ref, prev_ref, w_ref, b_ref, act_ref, u_ref):
        i = pl.program_id(0)
        prev = jnp.where(i > 0, prev_ref[...], 0.0)
        ext = jnp.concatenate([prev, main_ref[...]], axis=0)
        u = _taps(_shifted(ext, range(8 - (FFN_CONV - 1), 9)), w_ref[...], tr, b_ref[...])
        a = u[:, :D_FF]
        act_ref[...] = (a * _sig(a) * u[:, D_FF:]).astype(BF16)
        u_ref[...] = u.astype(BF16)

    return pl.pallas_call(
        body, grid=(Lp // tr,),
        in_specs=[pl.BlockSpec((tr, W2), lambda i: (i, 0)), _halo_prev(tr, W2),
                  pl.BlockSpec((FFN_CONV, W2), lambda i: (0, 0)), pl.BlockSpec((1, W2), lambda i: (0, 0))],
        out_specs=[pl.BlockSpec((tr, D_FF), lambda i: (i, 0)), pl.BlockSpec((tr, W2), lambda i: (i, 0))],
        out_shape=[jax.ShapeDtypeStruct((Lp, D_FF), BF16), jax.ShapeDtypeStruct((Lp, W2), BF16)],
        name="ffn_act")(up, up, conv_w, conv_b)


def _ffn_act_bwd(up, u, dact, conv_w):
    Lp = up.shape[0]
    tr = _tile(Lp, 192, 16)
    W2 = 2 * D_FF
    te = tr + 8

    def body(up_ref, u_ref, un_ref, da_ref, dan_ref, w_ref, dup_ref, acc_ref):
        i = pl.program_id(0)
        w = w_ref[...]
        ue = jnp.concatenate([u_ref[...].astype(F32), un_ref[...].astype(F32)[0:8]], axis=0)
        a = ue[:, :D_FF]
        b = ue[:, D_FF:]
        rowe = i * tr + lax.broadcasted_iota(jnp.int32, (te, 1), 0)
        dae = jnp.where(rowe < Lp, jnp.concatenate([da_ref[...], dan_ref[...]], axis=0), 0.0)
        sg = _sig(a)
        du = jnp.concatenate([dae * b * (sg * (1.0 + a * (1.0 - sg))), dae * (a * sg)], axis=1)
        dus = _shifted(du, range(FFN_CONV - 1, -1, -1))
        dup_ref[...] = _taps(dus, w, tr).astype(BF16)
        upm = up_ref[...]
        rows = [jnp.sum(dus[kk][0:tr, :] * upm, axis=0, keepdims=True) for kk in range(FFN_CONV)]
        rows.append(jnp.sum(du[0:tr, :], axis=0, keepdims=True))
        part = jnp.concatenate(rows + [jnp.zeros((8 - len(rows), W2), F32)], axis=0)

        @pl.when(i == 0)
        def _():
            acc_ref[...] = part

        @pl.when(i > 0)
        def _():
            acc_ref[...] += part

    return pl.pallas_call(
        body, grid=(Lp // tr,),
        in_specs=[pl.BlockSpec((tr, W2), lambda i: (i, 0)), pl.BlockSpec((tr, W2), lambda i: (i, 0)),
                  _halo_next(tr, W2, Lp, rows=16), pl.BlockSpec((tr, D_FF), lambda i: (i, 0)), _halo_next(tr, D_FF, Lp),
                  pl.BlockSpec((FFN_CONV, W2), lambda i: (0, 0))],
        out_specs=[pl.BlockSpec((tr, W2), lambda i: (i, 0)), pl.BlockSpec((8, W2), lambda i: (0, 0))],
        out_shape=[jax.ShapeDtypeStruct((Lp, W2), BF16), jax.ShapeDtypeStruct((8, W2), F32)],
        name="ffn_act_bwd")(up, u, u, dact, dact, conv_w)


def _proj_rows(j):
    shift = (jnp.where((j >= 3) & (j < 6), _O_RQ - 3 * D_MODEL, 0) + jnp.where(j == 6, _O_GZ - 6 * D_MODEL, 0)
             + jnp.where(j >= 7, _O_RG - 7 * D_MODEL, 0))
    return j * D_MODEL + shift


def _local_step(hpad, tgt, pad, wt, first_weights=None, late_weights=None, on_ffn_out_grads=None,
                on_w_in_grads=None):
    Lp = hpad.shape[0]
    first = pad + N_META
    pos = jnp.arange(Lp, dtype=F32) - float(pad)
    half = RET_D // 2
    inv = 1.0 / (ROPE_BASE ** (jnp.arange(half, dtype=F32) / half))
    ang = pos[:, None] * inv[None, :]
    cos, sin = jnp.cos(ang), jnp.sin(ang)
    tables = _ret_tables()
    gparams = jnp.zeros((8, LANES), F32).at[0, :GDN_H].set(wt["a_log"]).at[1, :GDN_H].set(wt["dt_bias"])

    hn1 = _rms_fwd(hpad, wt["norm1"], "rms1_fwd")
    if first_weights is not None:
        wt = {**wt, **first_weights(hn1[:8, :LANES].astype(F32) + cos[:8] + sin[:8])}
    w_in_t = wt["w_in_t"]
    w_small_t = jnp.pad(w_in_t[_O_GA:_O_RQ], ((0, LANES - 2 * GDN_H), (0, 0)))
    proj_m = _mm_nn(hn1, w_in_t, bt=True, tm_target=2752, b_rows=(D_MODEL, MAIN_W // D_MODEL, _proj_rows),
                    name="proj_main")
    proj_s = _mm_nn(hn1, w_small_t, bt=True, name="proj_small")
    qkv, gsm, conv_out = _gdn_pre(proj_m, proj_s, wt["gdn_conv_w"], gparams, pad)
    o_a, s_a, t_a = _gdn_chunk_fwd(qkv, gsm)
    o_b, s_b = _ret_chunk_fwd(proj_m, cos, sin, tables)
    if late_weights is not None:
        wt = {**wt, **late_weights(o_b)}
    y, h1, hn2 = _merge_fwd(o_a, o_b, proj_m, wt["gdn_norm"], (wt["w_out"], hpad, wt["norm2"]))
    up = _mm_nn(hn2, wt["w_up_t"], bt=True, name="ffn_up")
    act, u_ffn = _ffn_act(up, wt["ffn_conv_w"], wt["ffn_conv_b"])
    lossvec, dh2, dh2b, d_norm_f = _final(_Producer(act, wt["w_down"], h1), wt["norm_f"], tgt, first)

    d_w_down = _mm_tn(act, dh2b, name="dw_down")
    dact = _mm_nt(dh2b, wt["w_down"], name="d_act")
    dup, ffn_rows = _ffn_act_bwd(up, u_ffn, dact, wt["ffn_conv_w"])
    d_w_up_t = _mm_tn(dup, hn2, name="dw_up")
    dh1, dh1b, d_norm2 = _rms_bwd(h1, wt["norm2"], _Producer(dup, wt["w_up_t"]), dh2, pad, "d_hn2_rms2_bwd")

    d_w_out = _mm_tn(y, dh1b, name="dw_out")
    gnorm = wt["gdn_norm"]
    if on_ffn_out_grads is not None:
        gnorm = gnorm + on_ffn_out_grads(d_w_down, d_w_up_t, d_w_out)[0:1, :]
    d_c, do_a, do_b, d_gnorm = _merge_bwd(dh1b, wt["w_out"], o_a, o_b, proj_m, gnorm)
    d_r = _ret_chunk_bwd(proj_m, cos, sin, tables, do_b, s_b)
    dq, dk, dv, dgs = _gdn_chunk_bwd(qkv, gsm, do_a, s_a, t_a)
    d_a, d_s, conv_rows, gp_rows = _gdn_pre_bwd(proj_m, conv_out, proj_s, wt["gdn_conv_w"], gparams, dq, dk, dv, dgs,
                                                pad)

    segs = [(d_a, w_in_t[_O_GQ:_O_GZ]), (d_r, w_in_t[_O_RQ:_O_RG]),
            (d_c, jnp.concatenate([w_in_t[_O_GZ:_O_GA], w_in_t[_O_RG:_O_END]], axis=0))]
    pa, pr, pc = [_mm_tn(d, hn1, BF16, name="dw_in_%d" % i) for i, (d, _) in enumerate(segs)]
    ps = _mm_tn(d_s, hn1, BF16, name="dw_in_small")
    d_w_in_t = jnp.concatenate([pa, pc[:D_MODEL], ps[:2 * GDN_H], pr, pc[D_MODEL:]], axis=0)
    if on_w_in_grads is not None:
        w_small_t = w_small_t + on_w_in_grads(d_w_in_t)[0:1, 0:1].astype(w_small_t.dtype)
    dhn1 = _mm_sum([(d_s, w_small_t)] + segs[:-1], "d_hn1_first")
    dh0, _, d_norm1 = _rms_bwd(hpad, wt["norm1"], _Producer(*segs[-1], dhn1), dh1, pad, "d_hn1_rms1_bwd")

    grads = {
        "norm1": d_norm1, "w_in_t": d_w_in_t, "gdn_conv_w": conv_rows[:GDN_CONV],
        "a_log": gp_rows[0, :GDN_H], "dt_bias": gp_rows[1, :GDN_H], "gdn_norm": d_gnorm, "w_out": d_w_out,
        "norm2": d_norm2, "w_up_t": d_w_up_t, "ffn_conv_w": ffn_rows[:FFN_CONV],
        "ffn_conv_b": ffn_rows[FFN_CONV:FFN_CONV + 1], "w_down": d_w_down, "norm_f": d_norm_f,
    }
    return lossvec, dh0, grads


def _peer(k):
    ix, iy, ic = lax.axis_index("x"), lax.axis_index("y"), lax.axis_index("c")
    px = 1 - ix if (k >> 2) & 1 else ix
    py = 1 - iy if (k >> 1) & 1 else iy
    pc = 1 - ic if k & 1 else ic
    return (px, py, pc), 4 * px + 2 * py + pc


def _comm_call(body, n, out_shapes, name, args):
    hbm = pl.BlockSpec(memory_space=pl.ANY)
    return pl.pallas_call(
        body, out_shape=out_shapes, in_specs=[hbm] * n, out_specs=[hbm] * n,
        scratch_shapes=[pltpu.SemaphoreType.DMA((n, N_DEV - 1)), pltpu.SemaphoreType.DMA((n, N_DEV - 1)),
                        pltpu.SemaphoreType.DMA((n,))],
        name=name)(*args)


def _all_gather(xs, name):
    n = len(xs)

    def body(*refs):
        x_refs, out_refs = refs[:n], refs[n:2 * n]
        send_sems, recv_sems, local_sems = refs[2 * n:]
        _, me = _peer(0)
        pending = []
        for i in range(n):
            local = pltpu.make_async_copy(x_refs[i], out_refs[i].at[me], local_sems.at[i])
            local.start()
            pending.append(local)
        sends = []
        for i in range(n):
            for k in range(1, N_DEV):
                dev, _ = _peer(k)
                cp = pltpu.make_async_remote_copy(
                    src_ref=x_refs[i], dst_ref=out_refs[i].at[me], send_sem=send_sems.at[i, k - 1],
                    recv_sem=recv_sems.at[i, k - 1], device_id=dev, device_id_type=MESH_T)
                cp.start()
                sends.append(cp)
        for i in range(n):
            for k in range(1, N_DEV):
                dev, idx = _peer(k)
                pltpu.make_async_remote_copy(
                    src_ref=x_refs[i], dst_ref=out_refs[i].at[idx], send_sem=send_sems.at[i, k - 1],
                    recv_sem=recv_sems.at[i, k - 1], device_id=dev, device_id_type=MESH_T).wait_recv()
        for cp in sends:
            cp.wait_send()
        for local in pending:
            local.wait()

    out_shapes = [jax.ShapeDtypeStruct((N_DEV,) + a.shape, a.dtype) for a in xs]
    return _comm_call(body, n, out_shapes, name, xs)


def _all_to_all(gs, name):
    n = len(gs)

    def body(*refs):
        g_refs, out_refs = refs[:n], refs[n:2 * n]
        send_sems, recv_sems, local_sems = refs[2 * n:]
        _, me = _peer(0)
        pending = []
        for i in range(n):
            local = pltpu.make_async_copy(g_refs[i].at[me], out_refs[i].at[0], local_sems.at[i])
            local.start()
            pending.append(local)
        sends = []
        for i in range(n):
            for k in range(1, N_DEV):
                dev, idx = _peer(k)
                cp = pltpu.make_async_remote_copy(
                    src_ref=g_refs[i].at[idx], dst_ref=out_refs[i].at[k], send_sem=send_sems.at[i, k - 1],
                    recv_sem=recv_sems.at[i, k - 1], device_id=dev, device_id_type=MESH_T)
                cp.start()
                sends.append(cp)
        for cp in sends:
            cp.wait_recv()
        for cp in sends:
            cp.wait_send()
        for local in pending:
            local.wait()

    out_shapes = [jax.ShapeDtypeStruct(g.shape, g.dtype) for g in gs]
    return _comm_call(body, n, out_shapes, name, gs)


_SPLIT_RELATIONS = {"gather": tuple(range(1, N_DEV)), "a2a": tuple(range(1, N_DEV)), "chip": (1, 2, 4, 6),
                    "forward": (2, 4, 6)}


def _split_copies(kind, src_refs, land_refs, send_sems, recv_sems, local_sems, with_recv):
    n = len(land_refs)
    rels = _SPLIT_RELATIONS[kind]
    _, me = _peer(0)
    locals_, remotes = [], []
    for i in range(n):
        if kind in ("gather", "chip"):
            locals_.append(pltpu.make_async_copy(src_refs[i], land_refs[i].at[me], local_sems.at[i]))
        elif kind == "a2a":
            locals_.append(pltpu.make_async_copy(src_refs[i].at[me], land_refs[i].at[0], local_sems.at[i]))
        for jj, k in enumerate(rels):
            dev, idx = _peer(k)
            if kind in ("gather", "chip"):
                src, dst, mine = src_refs[i], land_refs[i].at[me], land_refs[i].at[idx]
            elif kind == "a2a":
                src, dst, mine = src_refs[i].at[idx], land_refs[i].at[k], land_refs[i].at[k]
            else:
                dev, _ = _peer(1)
                _, came = _peer(k + 1)
                src, dst, mine = land_refs[i].at[idx], land_refs[i].at[idx], land_refs[i].at[came]
            j = i * len(rels) + jj
            send = pltpu.make_async_remote_copy(
                src_ref=src, dst_ref=dst, send_sem=send_sems.at[j], recv_sem=recv_sems.at[j],
                device_id=dev, device_id_type=MESH_T)
            recv = pltpu.make_async_remote_copy(
                src_ref=src, dst_ref=mine, send_sem=send_sems.at[j], recv_sem=recv_sems.at[j],
                device_id=dev, device_id_type=MESH_T) if with_recv else None
            remotes.append((send, recv))
    return locals_, remotes


_HBM = pl.BlockSpec(memory_space=pltpu.HBM)
_SEM = pl.BlockSpec(memory_space=pltpu.SEMAPHORE)
_ANY = pl.BlockSpec(memory_space=pl.ANY)


def _split_start(srcs, kind, name, after):
    n = len(srcs)
    if kind == "forward":
        arrays = list(srcs)
    else:
        gathers = kind in ("gather", "chip")
        arrays = list(srcs) + [lax.empty(((N_DEV,) + a.shape) if gathers else a.shape, a.dtype) for a in srcs]
    na = len(arrays)

    def body(*refs):
        src_refs, land_refs = refs[:n], refs[na - n:na]
        send_sems, recv_sems, local_sems = refs[na + 1:na + 4]
        token = refs[-1]
        locals_, remotes = _split_copies(kind, src_refs, land_refs, send_sems, recv_sems, local_sems, False)
        for cp in locals_:
            cp.start()
        for send, _ in remotes:
            send.start()
        token[...] = jnp.zeros_like(token)

    ncp = n * len(_SPLIT_RELATIONS[kind])
    sems = (pltpu.SemaphoreType.DMA((ncp,)), pltpu.SemaphoreType.DMA((ncp,)), pltpu.SemaphoreType.DMA((n,)))
    thru = tuple(pltpu.HBM(a.shape, a.dtype) for a in arrays)
    outs = pl.pallas_call(
        body, name=name,
        out_shape=sems + thru + (jax.ShapeDtypeStruct((8, LANES), F32),),
        in_specs=[_HBM] * na + [_ANY],
        out_specs=[_SEM] * 3 + [_HBM] * na + [pl.BlockSpec(memory_space=pltpu.VMEM)],
        input_output_aliases={i: 3 + i for i in range(na)},
        compiler_params=pltpu.CompilerParams(has_side_effects=pltpu.SideEffectType.DATAFLOW_SIDE_EFFECTING),
    )(*[pltpu.with_memory_space_constraint(a, pltpu.HBM) for a in arrays], after)
    return (kind, n, outs[:3], outs[3:3 + na]), outs[-1]


def _split_wait(handle, name, after):
    kind, n, sems, thru = handle
    na = len(thru)

    def body(*refs):
        src_refs, land_refs = refs[:n], refs[na - n:na]
        send_sems, recv_sems, local_sems = refs[na:na + 3]
        locals_, remotes = _split_copies(kind, src_refs, land_refs, send_sems, recv_sems, local_sems, True)
        for send, recv in remotes:
            send.wait_send()
            recv.wait_recv()
        for cp in locals_:
            cp.wait()

    outs = pl.pallas_call(
        body, name=name, out_shape=tuple(pltpu.HBM(a.shape, a.dtype) for a in thru),
        in_specs=[_HBM] * na + [_SEM] * 3 + [_ANY], out_specs=[_HBM] * na,
        input_output_aliases={i: i for i in range(na)},
        compiler_params=pltpu.CompilerParams(has_side_effects=pltpu.SideEffectType.DATAFLOW_SIDE_EFFECTING),
    )(*thru, *sems, after)
    return list(outs[na - n:])


def _adamw(gslabs, w, m, v, name):
    R, Cw = w.shape
    if R % 8 == 0:
        tr, tc = _tile(R, 64 if Cw > 1024 else 128, 8), Cw
    else:
        tr, tc = R, LANES
    c1 = 1.0 - ADAM_B1 ** ADAM_STEP
    c2 = 1.0 - ADAM_B2 ** ADAM_STEP

    def body(g_ref, w_ref, m_ref, v_ref, go_ref, d_ref, mo_ref, vo_ref):
        g = g_ref[0].astype(F32)
        for k in range(1, N_DEV):
            g = g + g_ref[k].astype(F32)
        mn = ADAM_B1 * m_ref[...] + (1.0 - ADAM_B1) * g
        vn = ADAM_B2 * v_ref[...] + (1.0 - ADAM_B2) * (g * g)
        m_hat = mn / c1
        v_hat = vn / c2
        go_ref[...] = g
        d_ref[...] = -ADAM_LR * (m_hat / (jnp.sqrt(v_hat) + ADAM_EPS) + ADAM_WD * w_ref[...])
        mo_ref[...] = mn
        vo_ref[...] = vn

    blk = pl.BlockSpec((tr, tc), lambda i, j: (i, j))
    return pl.pallas_call(
        body, grid=(R // tr, Cw // tc),
        in_specs=[pl.BlockSpec((N_DEV, tr, tc), lambda i, j: (0, i, j)), blk, blk, blk],
        out_specs=[blk] * 4, out_shape=[jax.ShapeDtypeStruct((R, Cw), F32)] * 4, name=name)(gslabs, w, m, v)


def _pack(arrs, row_mult, dtype=F32):
    parts = []
    total = 0
    for a in arrs:
        f = a.reshape(-1).astype(dtype)
        n = -(-f.shape[0] // 1024) * 1024
        parts.append(jnp.pad(f, (0, n - f.shape[0])))
        total += n
    rows = total // LANES
    rows_p = -(-rows // row_mult) * row_mult
    flat = jnp.concatenate(parts)
    flat = jnp.pad(flat, (0, rows_p * LANES - total))
    return flat.reshape(rows_p, LANES)


def _unpack(packed, shapes):
    lead = packed.shape[:-2]
    flat = packed.reshape(lead + (-1,))
    out = []
    off = 0
    for s in shapes:
        n = int(np.prod(s))
        out.append(flat[..., off:off + n].reshape(lead + tuple(s)))
        off += -(-n // 1024) * 1024
    return out


def _gather_cols(stacked):
    d, r, c = stacked.shape
    return stacked.transpose(1, 0, 2).reshape(r, d * c)


def _scatter_cols(full):
    r, n = full.shape
    return full.reshape(r, N_DEV, n // N_DEV).transpose(1, 0, 2)


def kernel(x, meta, norm1, w_in, gdn_conv_w, gdn_a_log, gdn_dt_bias, gdn_norm, w_out, norm2, w_ffn_up, ffn_conv_w, ffn_conv_b, w_ffn_down, norm_f, loss_target, m_meta, m_norm1, m_w_in, m_gdn_conv_w, m_gdn_a_log, m_gdn_dt_bias, m_gdn_norm, m_w_out, m_norm2, m_w_ffn_up, m_ffn_conv_w, m_ffn_conv_b, m_w_ffn_down, m_norm_f, v_meta, v_norm1, v_w_in, v_gdn_conv_w, v_gdn_a_log, v_gdn_dt_bias, v_gdn_norm, v_w_out, v_norm2, v_w_ffn_up, v_ffn_conv_w, v_ffn_conv_b, v_w_ffn_down, v_norm_f):
    S = x.shape[1]
    L = N_META + S
    pad = (-L) % CHUNK
    Lp = L + pad

    tr_ = lambda a: jnp.swapaxes(a[0], 0, 1)
    big = [tr_(w_in), w_out[0], tr_(w_ffn_up), w_ffn_down[0]]
    small = [meta, gdn_conv_w, ffn_conv_w]
    small_all, = _all_gather([_pack(small, 8)], "gather_small_weights")
    first, first_token = _split_start([big[0].astype(BF16)], "chip", "gather_w_in_start", small_all)
    late, late_token = _split_start([a.astype(BF16) for a in big[1:]], "gather", "gather_late_start", first_token)

    def first_weights(after):
        half = _split_wait(first, "gather_w_in_wait", after)
        second, second_token = _split_start(half, "forward", "gather_w_in_forward_start", after)
        w_in_s, = _split_wait(second, "gather_w_in_forward_wait", second_token)
        return {"w_in_t": w_in_s.reshape(_O_END, D_MODEL)}

    def late_weights(after):
        w_out_s, w_up_s, w_down_s = _split_wait(late, "gather_late_wait", after)
        return {"w_out": w_out_s.reshape(D_MODEL, D_MODEL), "w_up_t": w_up_s.reshape(2 * D_FF, D_MODEL),
                "w_down": w_down_s.reshape(D_FF, D_MODEL)}

    meta_s, gconv_s, fconv_s = _unpack(small_all, [a.shape for a in small])
    wt = {
        "norm1": norm1 + jnp.tile(late_token[0:1, :], (1, D_MODEL // LANES)),
        "gdn_conv_w": _gather_cols(gconv_s[:, 0]), "a_log": gdn_a_log[0], "dt_bias": gdn_dt_bias[0],
        "gdn_norm": gdn_norm, "norm2": norm2, "ffn_conv_w": _gather_cols(fconv_s[:, 0]), "ffn_conv_b": ffn_conv_b,
        "norm_f": norm_f.reshape(1, D_MODEL),
    }
    meta_f = _gather_cols(meta_s)

    pending = {}

    def on_ffn_out_grads(d_w_down, d_w_up_t, d_w_out):
        srcs = [d_w_out.reshape(N_DEV, D_MODEL // N_DEV, D_MODEL), d_w_up_t.reshape(N_DEV, 2 * D_FF // N_DEV, D_MODEL),
                d_w_down.reshape(N_DEV, D_FF // N_DEV, D_MODEL)]
        pending["ffn_out"], token = _split_start(srcs, "a2a", "exchange_ffn_out_start", d_w_out)
        return token

    def on_w_in_grads(d_w_in_t):
        slabs = d_w_in_t.astype(BF16).reshape(N_DEV, _O_END // N_DEV, D_MODEL)
        pending["w_in"], token = _split_start([slabs], "a2a", "exchange_w_in_start", d_w_in_t)
        return token

    head = jnp.concatenate([jnp.zeros((pad, D_MODEL), F32), meta_f], axis=0)
    if S >= 2 * 704:
        hpad = _Rows(x[0], pad + N_META, head)
        tgt = _Rows(loss_target[0], pad + N_META)
    else:
        hpad = jnp.concatenate([head, x[0]], axis=0)
        tgt = jnp.concatenate([jnp.zeros((pad + N_META, D_MODEL), F32), loss_target[0]], axis=0)
    lossvec, dh0, gr = _local_step(hpad, tgt, pad, wt, first_weights, late_weights, on_ffn_out_grads, on_w_in_grads)

    loss = lax.psum(jnp.sum(lossvec), ("x", "y", "c"))
    grad_x = dh0[pad + N_META:][None]

    big_m = [tr_(m_w_in), m_w_out[0], tr_(m_w_ffn_up), m_w_ffn_down[0]]
    big_v = [tr_(v_w_in), v_w_out[0], tr_(v_w_ffn_up), v_w_ffn_down[0]]
    slabs_ffn_out = _split_wait(pending["ffn_out"], "exchange_ffn_out_wait", dh0)
    big_out = [None] + [_adamw(slabs_ffn_out[i - 1], big[i], big_m[i], big_v[i], "adamw_big_%d" % i)
                        for i in range(1, len(big))]
    g_sm = [_scatter_cols(dh0[pad:pad + N_META]), _scatter_cols(gr["gdn_conv_w"]), _scatter_cols(gr["ffn_conv_w"])]
    g_small = jnp.stack([_pack([g[d] for g in g_sm], 8) for d in range(N_DEV)])
    slabs_small, = _all_to_all([g_small], "exchange_small_gradients")
    small_out = _adamw(slabs_small, _pack(small, 8), _pack([m_meta, m_gdn_conv_w, m_ffn_conv_w], 8),
                       _pack([v_meta, v_gdn_conv_w, v_ffn_conv_w], 8), "adamw_small_sharded")
    small_un = [_unpack(o, [a.shape for a in small]) for o in small_out]
    rep_w = [norm1, gdn_a_log, gdn_dt_bias, gdn_norm, norm2, ffn_conv_b, norm_f]
    rep_m = [m_norm1, m_gdn_a_log, m_gdn_dt_bias, m_gdn_norm, m_norm2, m_ffn_conv_b, m_norm_f]
    rep_v = [v_norm1, v_gdn_a_log, v_gdn_dt_bias, v_gdn_norm, v_norm2, v_ffn_conv_b, v_norm_f]
    rep_g = [gr["norm1"], gr["a_log"], gr["dt_bias"], gr["gdn_norm"], gr["norm2"], gr["ffn_conv_b"], gr["norm_f"]]
    rep_slabs, = _all_gather([_pack(rep_g, 8)], "gather_small_gradients")
    rep_out = _adamw(rep_slabs, _pack(rep_w, 8), _pack(rep_m, 8), _pack(rep_v, 8), "adamw_replicated")
    rep_shapes = [a.shape for a in rep_w]
    rp_g, rp_d, rp_nm, rp_nv = [_unpack(o, rep_shapes) for o in rep_out]

    slabs_w_in, = _split_wait(pending["w_in"], "exchange_w_in_wait", rep_out[0])
    big_out[0] = _adamw(slabs_w_in, big[0], big_m[0], big_v[0], "adamw_big_0")
    back = lambda a: jnp.swapaxes(a, 0, 1)[None]
    sh_g, sh_d, sh_nm, sh_nv = [
        [small_un[j][0], back(big_out[0][j]), small_un[j][1], big_out[1][j][None], back(big_out[2][j]),
         small_un[j][2], big_out[3][j][None]] for j in range(4)]

    def order(sh, rp):
        return [sh[0], rp[0], sh[1], sh[2], rp[1], rp[2], rp[3], sh[3], rp[4], sh[4], sh[5], rp[5], sh[6], rp[6]]

    return (loss, grad_x, *order(sh_g, rp_g), *order(sh_d, rp_d), *order(sh_nm, rp_nm), *order(sh_nv, rp_nv))
```

```python
import functools
import math

import numpy as np
import jax
import jax.numpy as jnp
from jax import lax
from jax.experimental import pallas as pl
from jax.experimental.pallas import tpu as pltpu

F32 = jnp.float32
BF16 = jnp.bfloat16

D_MODEL = 1024
N_META = 16
CHUNK = 64
GDN_H = 8
GDN_D = 128
RET_H = 4
RET_D = 256
D_FF = 2816
GDN_CONV = 4
FFN_CONV = 3
ROPE_BASE = 10000.0
EPS = 1e-6
N_DEV = 8
LANES = 128
MAIN_W = 10 * 1024
_O_GQ, _O_GZ, _O_GA, _O_RQ, _O_RG, _O_GATE, _O_END = 0, 3072, 4096, 4112, 7184, 8208, 10256

ADAM_LR = 0.001
ADAM_B1 = 0.9
ADAM_B2 = 0.999
ADAM_EPS = 1e-08
ADAM_WD = 0.01
ADAM_STEP = 10

MESH_T = pl.DeviceIdType.MESH


def _tile(n, target, mult):
    best = None
    for d in range(mult, min(n, target) + 1, mult):
        if n % d == 0:
            best = d
    assert best is not None, (n, target, mult)
    return best


def _sig(x):
    return 0.5 * jnp.tanh(0.5 * x) + 0.5


def _d(a, b):
    return jnp.dot(a.astype(BF16), b.astype(BF16), preferred_element_type=F32)


def _dnt(a, b):
    return lax.dot_general(a.astype(BF16), b.astype(BF16), (((1,), (1,)), ((), ())), preferred_element_type=F32)


def _dtn(a, b):
    return lax.dot_general(a.astype(BF16), b.astype(BF16), (((0,), (0,)), ((), ())), preferred_element_type=F32)


def _dxg(a, b, dims):
    f = functools.partial(lax.dot_general, dimension_numbers=dims, preferred_element_type=F32)
    ab = a.astype(BF16)
    b1 = b.astype(BF16)
    r1 = b - b1.astype(F32)
    b2 = r1.astype(BF16)
    b3 = (r1 - b2.astype(F32)).astype(BF16)
    return f(ab, b1) + (f(ab, b2) + f(ab, b3))


def _dx(a, b):
    return _dxg(a, b, (((1,), (0,)), ((), ())))


def _dxnt(a, b):
    return _dxg(a, b, (((1,), (1,)), ((), ())))


def _split(a):
    hi = a.astype(BF16)
    return hi, (a - hi.astype(F32)).astype(BF16)


def _d3g(a, b, dims):
    ah, al = _split(a)
    bh, bl = _split(b)
    f = functools.partial(lax.dot_general, dimension_numbers=dims, preferred_element_type=F32)
    if dims == _NN:
        rows = a.shape[0]
        both = f(jnp.concatenate([ah, al], axis=0), bh)
        return both[:rows] + (f(ah, bl) + both[rows:])
    return f(ah, bh) + (f(ah, bl) + f(al, bh))


def _d2x(a, b, dims):
    ah, al = _split(a)
    f = functools.partial(lax.dot_general, dimension_numbers=dims, preferred_element_type=F32)
    bb = b.astype(BF16)
    return f(ah, bb) + f(al, bb)


_NN = (((1,), (0,)), ((), ()))
_NT = (((1,), (1,)), ((), ()))
_TN = (((0,), (0,)), ((), ()))


def _rowsum(x):
    return jnp.sum(x, axis=1, keepdims=True)


def _allsum(x):
    return jnp.sum(jnp.sum(x, axis=1, keepdims=True), axis=0, keepdims=True)


def _mm_nn(a, b, res=None, out_dtype=F32, bt=False, tm_target=704, b_rows=None, name="mm_nn"):
    M, K = a.shape
    N = b.shape[0] if bt else b.shape[1]
    tm = _tile(M, tm_target, 16)
    if b_rows is None:
        tn = _tile(N, 2816, 128)
    else:
        tn, n_tiles, start = b_rows
        N = tn * n_tiles

    def body(*refs):
        if res is None:
            a_ref, b_ref, o_ref = refs
        else:
            a_ref, b_ref, r_ref, o_ref = refs
        acc = lax.dot_general(a_ref[...], b_ref[...], _NT if bt else _NN, preferred_element_type=F32)
        if res is not None:
            acc = acc + r_ref[...]
        o_ref[...] = acc.astype(out_dtype)

    b_spec = pl.BlockSpec((tn, K), lambda j, i: (j, 0)) if bt else pl.BlockSpec((K, tn), lambda j, i: (0, j))
    if b_rows is not None:
        b_spec = pl.BlockSpec((pl.Element(tn), pl.Element(K)), lambda j, i: (pl.multiple_of(start(j), 16), 0))
    in_specs = [pl.BlockSpec((tm, K), lambda j, i: (i, 0)), b_spec]
    args = [a, b]
    if res is not None:
        in_specs.append(pl.BlockSpec((tm, tn), lambda j, i: (i, j)))
        args.append(res)
    return pl.pallas_call(
        body, grid=(N // tn, M // tm), in_specs=in_specs,
        out_specs=pl.BlockSpec((tm, tn), lambda j, i: (i, j)),
        out_shape=jax.ShapeDtypeStruct((M, N), out_dtype), name=name)(*args)


def _mm_sum(pairs, name):
    M = pairs[0][0].shape[0]
    N = pairs[0][1].shape[1]
    tm = _tile(M, 704, 16)
    n = len(pairs)

    def body(*refs):
        o_ref = refs[-1]
        acc = jnp.dot(refs[0][...], refs[1][...], preferred_element_type=F32)
        for i in range(1, n):
            acc = acc + jnp.dot(refs[2 * i][...], refs[2 * i + 1][...], preferred_element_type=F32)
        o_ref[...] = acc

    specs, args = [], []
    for a, b in pairs:
        specs += [pl.BlockSpec((tm, a.shape[1]), lambda i: (i, 0)),
                  pl.BlockSpec(b.shape, lambda i: (0, 0), pipeline_mode=pl.Buffered(1))]
        args += [a, b]
    return pl.pallas_call(
        body, grid=(M // tm,), in_specs=specs, out_specs=pl.BlockSpec((tm, N), lambda i: (i, 0)),
        out_shape=jax.ShapeDtypeStruct((M, N), F32), name=name)(*args)


def _mm_nt(a, b, res=None, name="mm_nt"):
    M, Nc = a.shape
    K = b.shape[0]
    tm = _tile(M, 704, 16)
    tc = _tile(Nc, 5632, 128)

    def body(*refs):
        if res is None:
            a_ref, b_ref, o_ref = refs
        else:
            a_ref, b_ref, r_ref, o_ref = refs
        c = pl.program_id(1)
        p = lax.dot_general(a_ref[...], b_ref[...], (((1,), (1,)), ((), ())), preferred_element_type=F32)

        @pl.when(c == 0)
        def _():
            if res is None:
                o_ref[...] = p
            else:
                o_ref[...] = p + r_ref[...]

        @pl.when(c > 0)
        def _():
            o_ref[...] += p

    in_specs = [pl.BlockSpec((tm, tc), lambda i, c: (i, c)), pl.BlockSpec((K, tc), lambda i, c: (0, c))]
    args = [a, b]
    if res is not None:
        in_specs.append(pl.BlockSpec((tm, K), lambda i, c: (i, 0)))
        args.append(res)
    return pl.pallas_call(
        body, grid=(M // tm, Nc // tc), in_specs=in_specs,
        out_specs=pl.BlockSpec((tm, K), lambda i, c: (i, 0)),
        out_shape=jax.ShapeDtypeStruct((M, K), F32), name=name)(*args)


def _mm_tn(a, b, out_dtype=F32, name="mm_tn"):
    M, K = a.shape
    N = b.shape[1]
    tm = _tile(M, 2752, 16)
    tk = _tile(K, 1408, 128)
    tn = _tile(N, 1408, 128)
    steps = M // tm

    def body(a_ref, b_ref, o_ref, *scratch):
        acc = scratch[0] if scratch else o_ref
        m = pl.program_id(2)
        p = lax.dot_general(a_ref[...], b_ref[...], (((0,), (0,)), ((), ())), preferred_element_type=F32)

        @pl.when(m == 0)
        def _():
            acc[...] = p

        @pl.when(m > 0)
        def _():
            acc[...] += p

        if scratch:
            @pl.when(m == steps - 1)
            def _():
                o_ref[...] = acc[...].astype(out_dtype)

    return pl.pallas_call(
        body, grid=(K // tk, N // tn, steps),
        in_specs=[pl.BlockSpec((tm, tk), lambda kk, j, m: (m, kk)), pl.BlockSpec((tm, tn), lambda kk, j, m: (m, j))],
        out_specs=pl.BlockSpec((tk, tn), lambda kk, j, m: (kk, j)),
        out_shape=jax.ShapeDtypeStruct((K, N), out_dtype),
        scratch_shapes=[] if out_dtype == F32 else [pltpu.VMEM((tk, tn), F32)], name=name)(a, b)


class _Rows:
    def __init__(self, body, first, head=None):
        self.body, self.first, self.head = body, first, head
        self.shape = (body.shape[0] + first, body.shape[1])


def _rows_operands(x, tr):
    if not isinstance(x, _Rows):
        return [x], [pl.BlockSpec((tr, x.shape[1]), lambda i: (i, 0))]
    assert x.first % 8 == 0 and x.first <= tr <= x.body.shape[0] and x.shape[0] % tr == 0
    width = x.shape[1]
    args = [x.body]
    specs = [pl.BlockSpec((pl.Element(tr), pl.Element(width)),
                          lambda i: (pl.multiple_of(jnp.maximum(i * tr - x.first, 0), 8), 0))]
    if x.head is not None:
        args.append(jnp.pad(x.head, ((0, tr - x.first), (0, 0))))
        specs.append(pl.BlockSpec((tr, width), lambda i: (0, 0)))
    return args, specs


def _rows_tile(x, refs, i, tr):
    blk = refs[0][...]
    if not isinstance(x, _Rows):
        return blk
    shifted = pltpu.roll(blk, x.first, 0)
    if x.head is not None:
        row = lax.broadcasted_iota(jnp.int32, (tr, 1), 0)
        shifted = jnp.where(row < x.first, refs[1][...], shifted)
    return jnp.where(i == 0, shifted, blk)


def _rms_fwd(x, g, name):
    Lp = x.shape[0]
    tr = _tile(Lp, 704, 16)
    args, specs = _rows_operands(x, tr)
    n = len(args)

    def body(*refs):
        g_ref, o_ref = refs[n:]
        xv = _rows_tile(x, refs[:n], pl.program_id(0), tr)
        r = lax.rsqrt(jnp.mean(xv * xv, axis=-1, keepdims=True) + EPS)
        o_ref[...] = (xv * r * g_ref[...]).astype(BF16)

    return pl.pallas_call(
        body, grid=(Lp // tr,),
        in_specs=specs + [pl.BlockSpec((1, D_MODEL), lambda i: (0, 0))],
        out_specs=pl.BlockSpec((tr, D_MODEL), lambda i: (i, 0)),
        out_shape=jax.ShapeDtypeStruct((Lp, D_MODEL), BF16), name=name)(*args, g)


class _Producer:
    def __init__(self, a, b, res=None):
        self.a, self.b, self.res = a, b, res
        self.tr = _tile(a.shape[0], 704, 16)
        K = a.shape[1]
        r_args, r_specs = ([], []) if res is None else _rows_operands(res, self.tr)
        self.args = [a, b] + r_args
        self.specs = [pl.BlockSpec((self.tr, K), lambda i: (i, 0)),
                      pl.BlockSpec((K, D_MODEL), lambda i: (0, 0), pipeline_mode=pl.Buffered(1))] + r_specs

    def tile(self, refs, i):
        acc = jnp.dot(refs[0][...], refs[1][...], preferred_element_type=F32)
        return acc if self.res is None else acc + _rows_tile(self.res, refs[2:], i, self.tr)


def _rms_bwd(x, g, dy, dres, pad, name):
    Lp = x.shape[0]
    fused = isinstance(dy, _Producer)
    tr = dy.tr if fused else _tile(Lp, 256, 16)
    n = len(dy.args) if fused else 1
    x_args, x_specs = _rows_operands(x, tr)
    nx = len(x_args)

    def body(*refs):
        g_ref, dr_ref, dx_ref, dxb_ref, dg_ref = refs[n + nx:]
        i = pl.program_id(0)
        xv = _rows_tile(x, refs[n:n + nx], i, tr)
        r = lax.rsqrt(jnp.mean(xv * xv, axis=-1, keepdims=True) + EPS)
        xh = xv * r
        dyv = dy.tile(refs[:n], i) if fused else refs[0][...]
        dxh = dyv * g_ref[...]
        dx = r * (dxh - xh * jnp.mean(dxh * xh, axis=-1, keepdims=True)) + dr_ref[...]
        row = i * tr + lax.broadcasted_iota(jnp.int32, (tr, 1), 0)
        dx = jnp.where(row >= pad, dx, 0.0)
        dx_ref[...] = dx
        dxb_ref[...] = dx.astype(BF16)
        part = jnp.sum(dyv * xh, axis=0, keepdims=True)

        @pl.when(i == 0)
        def _():
            dg_ref[...] = part

        @pl.when(i > 0)
        def _():
            dg_ref[...] += part

    blk = pl.BlockSpec((tr, D_MODEL), lambda i: (i, 0))
    vec = pl.BlockSpec((1, D_MODEL), lambda i: (0, 0))
    return pl.pallas_call(
        body, grid=(Lp // tr,), in_specs=(dy.specs if fused else [blk]) + x_specs + [vec, blk],
        out_specs=[blk, blk, vec],
        out_shape=[jax.ShapeDtypeStruct((Lp, D_MODEL), F32), jax.ShapeDtypeStruct((Lp, D_MODEL), BF16),
                   jax.ShapeDtypeStruct((1, D_MODEL), F32)], name=name)(*(dy.args if fused else [dy]), *x_args, g, dres)


def _final(h2, g, tgt, first_row):
    fused = isinstance(h2, _Producer)
    Lp = h2.a.shape[0] if fused else h2.shape[0]
    tr = h2.tr if fused else _tile(Lp, 256, 16)
    n = len(h2.args) if fused else 1
    t_args, t_specs = _rows_operands(tgt, tr)
    nt = len(t_args)

    def body(*refs):
        g_ref = refs[n]
        loss_ref, dx_ref, dxb_ref, dg_ref = refs[n + 1 + nt:]
        i = pl.program_id(0)
        xv = h2.tile(refs[:n], i) if fused else refs[0][...]
        tv = _rows_tile(tgt, refs[n + 1:n + 1 + nt], i, tr)
        gv = g_ref[...]
        r = lax.rsqrt(jnp.mean(xv * xv, axis=-1, keepdims=True) + EPS)
        xh = xv * r
        row = i * tr + lax.broadcasted_iota(jnp.int32, (tr, 1), 0)
        err = jnp.where(row >= first_row, xh * gv - tv, 0.0)
        lpart = jnp.sum(err * err, axis=0, keepdims=True) * (0.5 / D_MODEL)
        dyv = err * (1.0 / D_MODEL)
        dxh = dyv * gv
        dx = r * (dxh - xh * jnp.mean(dxh * xh, axis=-1, keepdims=True))
        dx_ref[...] = dx
        dxb_ref[...] = dx.astype(BF16)
        part = jnp.sum(dyv * xh, axis=0, keepdims=True)

        @pl.when(i == 0)
        def _():
            dg_ref[...] = part
            loss_ref[...] = lpart

        @pl.when(i > 0)
        def _():
            dg_ref[...] += part
            loss_ref[...] += lpart

    blk = pl.BlockSpec((tr, D_MODEL), lambda i: (i, 0))
    vec = pl.BlockSpec((1, D_MODEL), lambda i: (0, 0))
    return pl.pallas_call(
        body, grid=(Lp // tr,), in_specs=(h2.specs if fused else [blk]) + [vec] + t_specs,
        out_specs=[vec, blk, blk, vec],
        out_shape=[jax.ShapeDtypeStruct((1, D_MODEL), F32), jax.ShapeDtypeStruct((Lp, D_MODEL), F32),
                   jax.ShapeDtypeStruct((Lp, D_MODEL), BF16), jax.ShapeDtypeStruct((1, D_MODEL), F32)],
        name="final_norm_loss")(*(h2.args if fused else [h2]), g, *t_args)


def _halo_prev(tr, width, col=0):
    return pl.BlockSpec((8, width), lambda i: (jnp.maximum(i * (tr // 8) - 1, 0), col))


def _halo_next(tr, width, nrows, col=0, rows=8):
    last = nrows // rows - 1
    return pl.BlockSpec((rows, width), lambda i: (jnp.minimum((i + 1) * (tr // rows), last), col))


def _shifted(x, offs):
    n = x.shape[0]
    return [x if off == 0 else pltpu.roll(x, n - off, 0) for off in offs]


def _taps(wins, w, rows, bias=None):
    acc = w[0:1, :] * wins[0][0:rows, :]
    if bias is not None:
        acc = acc + bias
    for kk in range(1, len(wins)):
        acc = acc + w[kk:kk + 1, :] * wins[kk][0:rows, :]
    return acc


def _gdn_pre(proj_m, proj_s, conv_w, gparams, pad):
    Lp = proj_m.shape[0]
    tr = _tile(Lp, 192, 64)
    W3 = 3 * D_MODEL

    def body(main_ref, prev_ref, s_ref, w_ref, gp_ref, qkv_ref, gsm_ref, c_ref):
        i = pl.program_id(0)
        prev = jnp.where(i > 0, prev_ref[...], 0.0)
        ext = jnp.concatenate([prev, main_ref[...]], axis=0)
        c = _taps(_shifted(ext, range(8 - (GDN_CONV - 1), 9)), w_ref[...], tr)
        c_ref[...] = c.astype(BF16)
        s = c * _sig(c)
        scale = GDN_D ** -0.5
        for j in range(2 * GDN_H):
            seg = s[:, j * GDN_D:(j + 1) * GDN_D]
            r = lax.rsqrt(_rowsum(seg * seg) + EPS)
            if j < GDN_H:
                r = r * scale
            qkv_ref[:, j * GDN_D:(j + 1) * GDN_D] = seg * r
        qkv_ref[:, 2 * D_MODEL:] = s[:, 2 * D_MODEL:]
        sm = s_ref[...]
        gp = gp_ref[...]
        lane = lax.broadcasted_iota(jnp.int32, sm.shape, 1)
        z = sm + gp[1:2, :]
        softplus = jnp.maximum(z, 0.0) + jnp.log(1.0 + jnp.exp(-jnp.abs(z)))
        lg = -jnp.exp(gp[0:1, :]) * softplus
        row = i * tr + lax.broadcasted_iota(jnp.int32, (tr, 1), 0)
        out = jnp.where(lane < GDN_H, lg, jnp.where(lane < 2 * GDN_H, _sig(sm), 0.0))
        gsm_ref[...] = jnp.where(row >= pad, out, 0.0)

    return pl.pallas_call(
        body, grid=(Lp // tr,),
        in_specs=[pl.BlockSpec((tr, W3), lambda i: (i, 0)), _halo_prev(tr, W3),
                  pl.BlockSpec((tr, LANES), lambda i: (i, 0)),
                  pl.BlockSpec((GDN_CONV, W3), lambda i: (0, 0)), pl.BlockSpec((8, LANES), lambda i: (0, 0))],
        out_specs=[pl.BlockSpec((tr, W3), lambda i: (i, 0)), pl.BlockSpec((tr, LANES), lambda i: (i, 0)),
                   pl.BlockSpec((tr, W3), lambda i: (i, 0))],
        out_shape=[jax.ShapeDtypeStruct((Lp, W3), F32), jax.ShapeDtypeStruct((Lp, LANES), F32),
                   jax.ShapeDtypeStruct((Lp, W3), BF16)],
        name="gdn_pre")(proj_m, proj_m, proj_s, conv_w, gparams)


def _gdn_pre_bwd(proj_m, conv_out, proj_s, conv_w, gparams, dq, dk, dv, dgs, pad):
    Lp = proj_m.shape[0]
    tr = _tile(Lp, 192, 64)
    W3 = 3 * D_MODEL
    te = tr + 8

    def body(main_ref, c_ref, cn_ref, s_ref, w_ref, gp_ref,
             dq_ref, dqn_ref, dk_ref, dkn_ref, dv_ref, dvn_ref, dgs_ref,
             da_ref, ds_ref, dw_ref, dgp_ref):
        i = pl.program_id(0)
        w = w_ref[...]
        c = jnp.concatenate([c_ref[...].astype(F32), cn_ref[...].astype(F32)[0:8]], axis=0)
        sg = _sig(c)
        s = c * sg
        rowe = i * tr + lax.broadcasted_iota(jnp.int32, (te, 1), 0)
        live = (rowe >= pad) & (rowe < Lp)
        dqe = jnp.concatenate([dq_ref[...], dqn_ref[...]], axis=0)
        dke = jnp.concatenate([dk_ref[...], dkn_ref[...]], axis=0)
        dve = jnp.concatenate([dv_ref[...], dvn_ref[...]], axis=0)
        scale = GDN_D ** -0.5
        parts = []
        for j in range(2 * GDN_H):
            seg = s[:, j * GDN_D:(j + 1) * GDN_D]
            r = lax.rsqrt(_rowsum(seg * seg) + EPS)
            xh = seg * r
            if j < GDN_H:
                dxh = dqe[:, j * GDN_D:(j + 1) * GDN_D] * scale
            else:
                dxh = dke[:, (j - GDN_H) * GDN_D:(j - GDN_H + 1) * GDN_D]
            parts.append(r * (dxh - xh * _rowsum(dxh * xh)))
        parts.append(dve)
        dsv = jnp.concatenate(parts, axis=1)
        dc = jnp.where(live, dsv * (sg * (1.0 + c * (1.0 - sg))), 0.0)
        dcs = _shifted(dc, range(GDN_CONV - 1, -1, -1))
        da_ref[...] = _taps(dcs, w, tr).astype(BF16)
        pm = main_ref[...]
        rows = [jnp.sum(dcs[kk][0:tr, :] * pm, axis=0, keepdims=True) for kk in range(GDN_CONV)]
        dwp = jnp.concatenate(rows + [jnp.zeros((8 - GDN_CONV, W3), F32)], axis=0)

        sm = s_ref[...]
        gp = gp_ref[...]
        lane = lax.broadcasted_iota(jnp.int32, sm.shape, 1)
        rowm = i * tr + lax.broadcasted_iota(jnp.int32, (tr, 1), 0)
        dgv = jnp.where(rowm >= pad, dgs_ref[...], 0.0)
        dlg = jnp.where(lane < GDN_H, dgv, 0.0)
        dbt = jnp.where((lane >= GDN_H) & (lane < 2 * GDN_H), dgv, 0.0)
        z = sm + gp[1:2, :]
        softplus = jnp.maximum(z, 0.0) + jnp.log(1.0 + jnp.exp(-jnp.abs(z)))
        ea = jnp.exp(gp[0:1, :])
        dz = dlg * (-ea) * _sig(z)
        dal = dlg * (-ea) * softplus
        bt = _sig(sm)
        dgb = dbt * bt * (1.0 - bt)
        ds_ref[...] = (dz + dgb).astype(BF16)
        gpp = jnp.concatenate([jnp.sum(dal, axis=0, keepdims=True), jnp.sum(dz, axis=0, keepdims=True),
                               jnp.zeros((6, LANES), F32)], axis=0)

        @pl.when(i == 0)
        def _():
            dw_ref[...] = dwp
            dgp_ref[...] = gpp

        @pl.when(i > 0)
        def _():
            dw_ref[...] += dwp
            dgp_ref[...] += gpp

    m3 = pl.BlockSpec((tr, W3), lambda i: (i, 0))
    m1 = pl.BlockSpec((tr, D_MODEL), lambda i: (i, 0))
    n1 = _halo_next(tr, D_MODEL, Lp)
    return pl.pallas_call(
        body, grid=(Lp // tr,),
        in_specs=[m3, m3, _halo_next(tr, W3, Lp, rows=16), pl.BlockSpec((tr, LANES), lambda i: (i, 0)),
                  pl.BlockSpec((GDN_CONV, W3), lambda i: (0, 0)), pl.BlockSpec((8, LANES), lambda i: (0, 0)),
                  m1, n1, m1, n1, m1, n1, pl.BlockSpec((tr, LANES), lambda i: (i, 0))],
        out_specs=[m3, pl.BlockSpec((tr, LANES), lambda i: (i, 0)),
                   pl.BlockSpec((8, W3), lambda i: (0, 0)), pl.BlockSpec((8, LANES), lambda i: (0, 0))],
        out_shape=[jax.ShapeDtypeStruct((Lp, W3), BF16), jax.ShapeDtypeStruct((Lp, LANES), BF16),
                   jax.ShapeDtypeStruct((8, W3), F32), jax.ShapeDtypeStruct((8, LANES), F32)],
        name="gdn_pre_bwd")(proj_m, conv_out, conv_out, proj_s, conv_w, gparams, dq, dq, dk, dk, dv, dv, dgs)


def _gdn_gates(gs):
    ri = lax.broadcasted_iota(jnp.int32, (CHUNK, CHUNK), 0)
    ci = lax.broadcasted_iota(jnp.int32, (CHUNK, CHUNK), 1)
    tril = ri >= ci
    strict = ri > ci
    gall = _dx(tril.astype(F32), gs)
    lane8 = lax.broadcasted_iota(jnp.int32, (8, LANES), 1)
    sub8 = lax.broadcasted_iota(jnp.int32, (8, LANES), 0)
    grow = _dxnt((lane8 == sub8).astype(F32), gall)
    return gall, grow, tril, strict


def _gdn_decay(gall, grow, tril, h):
    g = gall[:, h:h + 1]
    return g, jnp.where(tril, jnp.exp(jnp.where(tril, g - grow[h:h + 1, :], 0.0)), 0.0)


def _group(N):
    return 3 if N % 3 == 0 else (2 if N % 2 == 0 else 1)


def _gdn_chunk_specs(N, rev):
    G = _group(N)
    nb = N // G
    cn = (lambda n: nb - 1 - n) if rev else (lambda n: n)
    col = lambda j: pl.BlockSpec((G * CHUNK, D_MODEL), lambda n: (cn(n), j))
    gate = pl.BlockSpec((G * CHUNK, LANES), lambda n: (cn(n), 0))
    st = lambda a, b: pl.BlockSpec((GDN_H, G, a, b), lambda n: (0, cn(n), 0, 0))
    return G, nb, col, gate, st


def _gdn_chunk_fwd(qkv, gsm):
    Lp = qkv.shape[0]
    N = Lp // CHUNK
    G, nb, col, gate, st = _gdn_chunk_specs(N, False)

    def body(q_ref, k_ref, v_ref, gs_ref, o_ref, sin_ref, t_ref, S):
        n = pl.program_id(0)

        @pl.when(n == 0)
        def _():
            S[...] = jnp.zeros_like(S)

        ri = lax.broadcasted_iota(jnp.int32, (CHUNK, CHUNK), 0)
        ci = lax.broadcasted_iota(jnp.int32, (CHUNK, CHUNK), 1)
        eye = (ri == ci).astype(F32)
        heads = range(GDN_H)
        sls = [slice(h * GDN_D, (h + 1) * GDN_D) for h in heads]
        rows = [slice(c * CHUNK, (c + 1) * CHUNK) for c in range(G)]
        pairs = [(c, h) for c in range(G) for h in heads]
        P = lambda f: {p: f(*p) for p in pairs}
        gs = [gs_ref[rows[c], :] for c in range(G)]
        gates = [_gdn_gates(gs[c]) for c in range(G)]
        tril, strict = gates[0][2], gates[0][3]
        q = P(lambda c, h: q_ref[rows[c], sls[h]])
        k = P(lambda c, h: k_ref[rows[c], sls[h]])
        v = P(lambda c, h: v_ref[rows[c], sls[h]])
        beta = P(lambda c, h: gs[c][:, GDN_H + h:GDN_H + h + 1])
        gg = P(lambda c, h: _gdn_decay(gates[c][0], gates[c][1], tril, h))
        g = {p: x[0] for p, x in gg.items()}
        gam = {p: x[1] for p, x in gg.items()}
        eg = P(lambda c, h: jnp.exp(g[c, h]))
        gl = P(lambda c, h: g[c, h][CHUNK - 1:CHUNK, :])
        kb = P(lambda c, h: k[c, h] * beta[c, h])
        pw = P(lambda c, h: -jnp.where(strict, _dnt(kb[c, h], k[c, h]) * gam[c, h], 0.0))
        p = P(lambda c, h: _dnt(q[c, h], k[c, h]) * gam[c, h])
        t = P(lambda c, h: eye + pw[c, h])
        for it in range(5):
            mm = _d3g if it < 2 else (lambda a, b, dims: _d(a, b))
            pw = P(lambda c, h: mm(pw[c, h], pw[c, h], _NN))
            t = P(lambda c, h: t[c, h] + mm(t[c, h], pw[c, h], _NN))
        u = P(lambda c, h: _d(t[c, h], v[c, h] * beta[c, h]))
        w = P(lambda c, h: _d(t[c, h], kb[c, h] * eg[c, h]))
        qg = P(lambda c, h: q[c, h] * eg[c, h])
        kd = P(lambda c, h: k[c, h] * jnp.exp(gl[c, h] - g[c, h]))
        egl = P(lambda c, h: jnp.exp(gl[c, h]))
        for c in range(G):
            for h in heads:
                t_ref[h, c] = t[c, h]
        cur = [S[h] for h in heads]
        for c in range(G):
            vnew = [u[c, h] - _d(w[c, h], cur[h]) for h in heads]
            for h in heads:
                o_ref[rows[c], sls[h]] = _d(qg[c, h], cur[h]) + _d(p[c, h], vnew[h])
                sin_ref[h, c] = cur[h]
            cur = [cur[h] * egl[c, h] + _dtn(kd[c, h], vnew[h]) for h in heads]
        for h in heads:
            S[h] = cur[h]

    return pl.pallas_call(
        body, grid=(nb,),
        in_specs=[col(0), col(1), col(2), gate],
        out_specs=[col(0), st(GDN_D, GDN_D), st(CHUNK, CHUNK)],
        out_shape=[jax.ShapeDtypeStruct((Lp, D_MODEL), F32), jax.ShapeDtypeStruct((GDN_H, N, GDN_D, GDN_D), F32),
                   jax.ShapeDtypeStruct((GDN_H, N, CHUNK, CHUNK), F32)],
        scratch_shapes=[pltpu.VMEM((GDN_H, GDN_D, GDN_D), F32)],
        name="gdn_chunk_fwd")(qkv, qkv, qkv, gsm)


def _gdn_chunk_bwd(qkv, gsm, do, s_in, t_in):
    Lp = qkv.shape[0]
    N = Lp // CHUNK
    G, nb, col, gate, st = _gdn_chunk_specs(N, True)

    def body(q_ref, k_ref, v_ref, gs_ref, do_ref, sin_ref, t_ref, dq_ref, dk_ref, dv_ref, dgs_ref, dS):
        n = pl.program_id(0)

        @pl.when(n == 0)
        def _():
            dS[...] = jnp.zeros_like(dS)

        lane = lax.broadcasted_iota(jnp.int32, (CHUNK, LANES), 1)
        rcol = lax.broadcasted_iota(jnp.int32, (CHUNK, 1), 0)
        ri = lax.broadcasted_iota(jnp.int32, (CHUNK, CHUNK), 0)
        ci = lax.broadcasted_iota(jnp.int32, (CHUNK, CHUNK), 1)
        ones = jnp.ones((CHUNK, LANES), F32)
        heads = range(GDN_H)
        sls = [slice(h * GDN_D, (h + 1) * GDN_D) for h in heads]
        rows = [slice(c * CHUNK, (c + 1) * CHUNK) for c in range(G)]
        pairs = [(c, h) for c in range(G) for h in heads]
        P = lambda f: {p: f(*p) for p in pairs}
        gs = [gs_ref[rows[c], :] for c in range(G)]
        gates = [_gdn_gates(gs[c]) for c in range(G)]
        tril, strict = gates[0][2], gates[0][3]
        q = P(lambda c, h: q_ref[rows[c], sls[h]])
        k = P(lambda c, h: k_ref[rows[c], sls[h]])
        v = P(lambda c, h: v_ref[rows[c], sls[h]])
        dov = P(lambda c, h: do_ref[rows[c], sls[h]])
        s0 = P(lambda c, h: sin_ref[h, c])
        t = P(lambda c, h: t_ref[h, c])
        beta = P(lambda c, h: gs[c][:, GDN_H + h:GDN_H + h + 1])
        gg = P(lambda c, h: _gdn_decay(gates[c][0], gates[c][1], tril, h))
        g = {p: x[0] for p, x in gg.items()}
        gam = {p: x[1] for p, x in gg.items()}
        eg = P(lambda c, h: jnp.exp(g[c, h]))
        egl = P(lambda c, h: jnp.exp(g[c, h][CHUNK - 1:CHUNK, :]))
        e = P(lambda c, h: jnp.exp(g[c, h][CHUNK - 1:CHUNK, :] - g[c, h]))
        kb = P(lambda c, h: k[c, h] * beta[c, h])
        kbg = P(lambda c, h: kb[c, h] * eg[c, h])
        vb = P(lambda c, h: v[c, h] * beta[c, h])
        qg = P(lambda c, h: q[c, h] * eg[c, h])
        kd = P(lambda c, h: k[c, h] * e[c, h])
        m = P(lambda c, h: jnp.where(strict, _dnt(kb[c, h], k[c, h]) * gam[c, h], 0.0))
        u = P(lambda c, h: _d(t[c, h], vb[c, h]))
        w = P(lambda c, h: _d(t[c, h], kbg[c, h]))
        p = P(lambda c, h: _dnt(q[c, h], k[c, h]) * gam[c, h])
        dqg = P(lambda c, h: _dnt(dov[c, h], s0[c, h]))
        qgdo = P(lambda c, h: _dtn(qg[c, h], dov[c, h]))
        ptdo = P(lambda c, h: _dtn(p[c, h], dov[c, h]))
        vnew = P(lambda c, h: u[c, h] - _d(w[c, h], s0[c, h]))
        dp = P(lambda c, h: jnp.where(tril, _dnt(dov[c, h], vnew[c, h]), 0.0))
        cur = [dS[h] for h in heads]
        dvnew, dkd, sds = {}, {}, {}
        for c in reversed(range(G)):
            for h in heads:
                dvnew[c, h] = ptdo[c, h] + _d(kd[c, h], cur[h])
                dkd[c, h] = _dnt(vnew[c, h], cur[h])
                sds[c, h] = _allsum(s0[c, h] * cur[h])
            cur = [qgdo[c, h] + egl[c, h] * cur[h] - _dtn(w[c, h], dvnew[c, h]) for h in heads]
        for h in heads:
            dS[h] = cur[h]
        dw = P(lambda c, h: -_dnt(dvnew[c, h], s0[c, h]))
        dvb = P(lambda c, h: _dtn(t[c, h], dvnew[c, h]))
        dkbg = P(lambda c, h: _dtn(t[c, h], dw[c, h]))
        dt = P(lambda c, h: _dnt(dvnew[c, h], vb[c, h]) + _dnt(dw[c, h], kbg[c, h]))
        x1 = P(lambda c, h: _dtn(t[c, h], dt[c, h]))
        dm = P(lambda c, h: jnp.where(strict, -_dnt(x1[c, h], t[c, h]), 0.0))
        dkk = P(lambda c, h: dm[c, h] * gam[c, h])
        dqk = P(lambda c, h: dp[c, h] * gam[c, h])
        dkb = P(lambda c, h: _d(dkk[c, h], k[c, h]) + eg[c, h] * dkbg[c, h])
        em = P(lambda c, h: dm[c, h] * m[c, h] + dp[c, h] * p[c, h])
        colsum = P(lambda c, h: _d2x(em[c, h], ones, _TN)[:, 0:1])
        for c, h in pairs:
            dk_ref[rows[c], sls[h]] = (_dtn(dkk[c, h], kb[c, h]) + _dtn(dqk[c, h], q[c, h]) + dkd[c, h] * e[c, h]
                                       + beta[c, h] * dkb[c, h])
            dq_ref[rows[c], sls[h]] = _d(dqk[c, h], k[c, h]) + dqg[c, h] * eg[c, h]
            dv_ref[rows[c], sls[h]] = beta[c, h] * dvb[c, h]
        for c in range(G):
            dg_all = jnp.zeros((CHUNK, LANES), F32)
            dbeta_all = jnp.zeros((CHUNK, LANES), F32)
            for h in heads:
                dbeta = _rowsum(k[c, h] * dkb[c, h]) + _rowsum(v[c, h] * dvb[c, h])
                z = _rowsum(kd[c, h] * dkd[c, h])
                dg = (_rowsum(em[c, h]) - colsum[c, h] + _rowsum(qg[c, h] * dqg[c, h]) + _rowsum(kbg[c, h] * dkbg[c, h])
                      - z)
                extra = _allsum(z) + egl[c, h] * sds[c, h]
                dg = dg + jnp.where(rcol == CHUNK - 1, extra, 0.0)
                dg_all = dg_all + jnp.where(lane == h, dg, 0.0)
                dbeta_all = dbeta_all + jnp.where(lane == GDN_H + h, dbeta, 0.0)
            dgs_ref[rows[c], :] = _dx((ci >= ri).astype(F32), dg_all) + dbeta_all

    return pl.pallas_call(
        body, grid=(nb,),
        in_specs=[col(0), col(1), col(2), gate, col(0), st(GDN_D, GDN_D), st(CHUNK, CHUNK)],
        out_specs=[col(0), col(0), col(0), gate],
        out_shape=[jax.ShapeDtypeStruct((Lp, D_MODEL), F32)] * 3 + [jax.ShapeDtypeStruct((Lp, LANES), F32)],
        scratch_shapes=[pltpu.VMEM((GDN_H, GDN_D, GDN_D), F32)],
        name="gdn_chunk_bwd")(qkv, qkv, qkv, gsm, do, s_in, t_in)


def _rot(x, c, s):
    half = RET_D // 2
    x1 = x[:, :half]
    x2 = x[:, half:]
    return jnp.concatenate([x1 * c - x2 * s, x2 * c + x1 * s], axis=1)


def _rot_bwd(d, c, s):
    half = RET_D // 2
    d1 = d[:, :half]
    d2 = d[:, half:]
    return jnp.concatenate([d1 * c + d2 * s, d2 * c - d1 * s], axis=1)


def _ret_tables():
    hh = jnp.arange(RET_H, dtype=F32)
    lg = jnp.log(1.0 - 2.0 ** (-5.0 - hh))
    idx = jnp.arange(CHUNK, dtype=F32)
    tril = jnp.asarray(np.tril(np.ones((CHUNK, CHUNK), dtype=bool)))
    dmask = jnp.where(tril, jnp.exp((idx[:, None] - idx[None, :]) * lg[:, None, None]), 0.0)
    qdec = jnp.exp((idx[None, :] + 1.0) * lg[:, None])
    kdec = jnp.exp((CHUNK - 1.0 - idx[None, :]) * lg[:, None])
    gch = jnp.exp(CHUNK * lg)
    qdec = jnp.broadcast_to(qdec[:, :, None], (RET_H, CHUNK, RET_D))
    kdec = jnp.broadcast_to(kdec[:, :, None], (RET_H, CHUNK, RET_D))
    gch = jnp.broadcast_to(gch[:, None, None], (RET_H, 8, LANES))
    return dmask, qdec, kdec, gch


def _ret_specs(N, rev):
    G = _group(N)
    nb = N // G
    cn = (lambda n: nb - 1 - n) if rev else (lambda n: n)
    col = lambda j: pl.BlockSpec((G * CHUNK, D_MODEL), lambda n: (cn(n), j))
    tab = lambda a, b: pl.BlockSpec((RET_H, a, b), lambda n: (0, 0, 0))
    rope = pl.BlockSpec((G * CHUNK, LANES), lambda n: (cn(n), 0))
    st = pl.BlockSpec((RET_H, G, RET_D, RET_D), lambda n: (0, cn(n), 0, 0))
    return G, nb, col, tab, rope, st


def _ret_chunk_fwd(proj_m, cos, sin, tables):
    Lp = proj_m.shape[0]
    N = Lp // CHUNK
    dmask, qdec, kdec, gch = tables
    G, nb, col, tab, rope, st = _ret_specs(N, False)

    def body(q_ref, k_ref, v_ref, c_ref, s_ref, dm_ref, qd_ref, kd_ref, g_ref, o_ref, sin_ref, S):
        n = pl.program_id(0)

        @pl.when(n == 0)
        def _():
            S[...] = jnp.zeros_like(S)

        heads = range(RET_H)
        sls = [slice(h * RET_D, (h + 1) * RET_D) for h in heads]
        rows = [slice(c * CHUNK, (c + 1) * CHUNK) for c in range(G)]
        pairs = [(c, h) for c in range(G) for h in heads]
        P = lambda f: {p: f(*p) for p in pairs}
        qr = P(lambda c, h: _rot(q_ref[rows[c], sls[h]], c_ref[rows[c], :], s_ref[rows[c], :]))
        ks = P(lambda c, h: _rot(k_ref[rows[c], sls[h]], c_ref[rows[c], :], s_ref[rows[c], :]) * (RET_D ** -0.5))
        v = P(lambda c, h: v_ref[rows[c], sls[h]])
        a = P(lambda c, h: _dnt(qr[c, h], ks[c, h]) * dm_ref[h])
        av = P(lambda c, h: _d(a[c, h], v[c, h]))
        kv = P(lambda c, h: _dtn(ks[c, h] * kd_ref[h], v[c, h]))
        qd = P(lambda c, h: qr[c, h] * qd_ref[h])
        cur = [S[h] for h in heads]
        for c in range(G):
            for h in heads:
                o_ref[rows[c], sls[h]] = av[c, h] + _d(qd[c, h], cur[h])
                sin_ref[h, c] = cur[h].astype(BF16)
            cur = [cur[h] * g_ref[h, 0:1, 0:1] + kv[c, h] for h in heads]
        for h in heads:
            S[h] = cur[h]

    return pl.pallas_call(
        body, grid=(nb,),
        in_specs=[col(3), col(4), col(5), rope, rope,
                  tab(CHUNK, CHUNK), tab(CHUNK, RET_D), tab(CHUNK, RET_D), tab(8, LANES)],
        out_specs=[col(0), st],
        out_shape=[jax.ShapeDtypeStruct((Lp, D_MODEL), F32), jax.ShapeDtypeStruct((RET_H, N, RET_D, RET_D), BF16)],
        scratch_shapes=[pltpu.VMEM((RET_H, RET_D, RET_D), F32)],
        name="ret_chunk_fwd")(proj_m, proj_m, proj_m, cos, sin, dmask, qdec, kdec, gch)


def _ret_chunk_bwd(proj_m, cos, sin, tables, do, s_in):
    Lp = proj_m.shape[0]
    N = Lp // CHUNK
    dmask, qdec, kdec, gch = tables
    G, nb, col, tab, rope, st = _ret_specs(N, True)

    def body(q_ref, k_ref, v_ref, c_ref, s_ref, dm_ref, qd_ref, kd_ref, g_ref, do_ref, sin_ref,
             d_ref, dS):
        n = pl.program_id(0)

        @pl.when(n == 0)
        def _():
            dS[...] = jnp.zeros_like(dS)

        kscale = RET_D ** -0.5
        heads = range(RET_H)
        sls = [slice(h * RET_D, (h + 1) * RET_D) for h in heads]
        rows = [slice(c * CHUNK, (c + 1) * CHUNK) for c in range(G)]
        pairs = [(c, h) for c in range(G) for h in heads]
        P = lambda f: {p: f(*p) for p in pairs}
        cs = [(c_ref[rows[c], :], s_ref[rows[c], :]) for c in range(G)]
        osl = lambda part, h: slice(part * D_MODEL + h * RET_D, part * D_MODEL + (h + 1) * RET_D)
        qr = P(lambda c, h: _rot(q_ref[rows[c], sls[h]], *cs[c]))
        ks = P(lambda c, h: _rot(k_ref[rows[c], sls[h]], *cs[c]) * kscale)
        v = P(lambda c, h: v_ref[rows[c], sls[h]])
        dov = P(lambda c, h: do_ref[rows[c], sls[h]])
        ad = P(lambda c, h: _dnt(qr[c, h], ks[c, h]) * dm_ref[h])
        da = P(lambda c, h: _dnt(dov[c, h], v[c, h]) * dm_ref[h])
        dos = P(lambda c, h: _dnt(dov[c, h], sin_ref[h, c]) * qd_ref[h])
        qdo = P(lambda c, h: _dtn(qr[c, h] * qd_ref[h], dov[c, h]))
        adv = P(lambda c, h: _dtn(ad[c, h], dov[c, h]))
        dqr = P(lambda c, h: _d(da[c, h], ks[c, h]) + dos[c, h])
        daq = P(lambda c, h: _dtn(da[c, h], qr[c, h]))
        kk = P(lambda c, h: ks[c, h] * kd_ref[h])
        cur = [dS[h] for h in heads]
        for c in reversed(range(G)):
            for h in heads:
                d_ref[rows[c], osl(2, h)] = (adv[c, h] + _d(kk[c, h], cur[h])).astype(BF16)
                d_ref[rows[c], osl(0, h)] = _rot_bwd(dqr[c, h], *cs[c]).astype(BF16)
                dks = daq[c, h] + _dnt(v[c, h], cur[h]) * kd_ref[h]
                d_ref[rows[c], osl(1, h)] = _rot_bwd(dks * kscale, *cs[c]).astype(BF16)
            cur = [cur[h] * g_ref[h, 0:1, 0:1] + qdo[c, h] for h in heads]
        for h in heads:
            dS[h] = cur[h]

    return pl.pallas_call(
        body, grid=(nb,),
        in_specs=[col(3), col(4), col(5), rope, rope,
                  tab(CHUNK, CHUNK), tab(CHUNK, RET_D), tab(CHUNK, RET_D), tab(8, LANES), col(0), st],
        out_specs=pl.BlockSpec((G * CHUNK, 3 * D_MODEL), lambda n: (nb - 1 - n, 0)),
        out_shape=jax.ShapeDtypeStruct((Lp, 3 * D_MODEL), BF16),
        scratch_shapes=[pltpu.VMEM((RET_H, RET_D, RET_D), F32)],
        name="ret_chunk_bwd")(proj_m, proj_m, proj_m, cos, sin, dmask, qdec, kdec, gch, do, s_in)


def _merge_specs(tr):
    col = lambda j: pl.BlockSpec((tr, D_MODEL), lambda i: (i, j))
    return col


def _merge_fwd(o_a, o_b, proj_m, gnorm, out_proj=None):
    Lp = o_a.shape[0]
    tr = _tile(Lp, 192 if out_proj is None else 352, 16)
    if out_proj is not None:
        w_out, res, g2 = out_proj
        r_args, r_specs = _rows_operands(res, tr)

    def body(oa_ref, ob_ref, gz_ref, rg_ref, ga_ref, gb_ref, gn_ref, *rest):
        y_ref = rest[0] if out_proj is None else rest[-3]
        gn = gn_ref[...]
        oa = oa_ref[...]
        ob = ob_ref[...]
        gz = gz_ref[...]
        ya = []
        for j in range(GDN_H):
            seg = oa[:, j * GDN_D:(j + 1) * GDN_D]
            r = lax.rsqrt(jnp.mean(seg * seg, axis=-1, keepdims=True) + EPS)
            ya.append(seg * r * gn)
        ya = jnp.concatenate(ya, axis=1) * (gz * _sig(gz))
        yb = []
        for j in range(RET_H):
            seg = ob[:, j * RET_D:(j + 1) * RET_D]
            r = lax.rsqrt(jnp.mean(seg * seg, axis=-1, keepdims=True) + EPS)
            yb.append(seg * r)
        rg = rg_ref[...]
        yb = jnp.concatenate(yb, axis=1) * (rg * _sig(rg))
        yv = (_sig(ga_ref[...]) * ya + _sig(gb_ref[...]) * yb).astype(BF16)
        y_ref[...] = yv
        if out_proj is not None:
            wo_ref, g2_ref = rest[0], rest[1]
            h1_ref, hn2_ref = rest[-2], rest[-1]
            h1 = (jnp.dot(yv, wo_ref[...], preferred_element_type=F32)
                  + _rows_tile(res, rest[2:2 + len(r_args)], pl.program_id(0), tr))
            r = lax.rsqrt(jnp.mean(h1 * h1, axis=-1, keepdims=True) + EPS)
            h1_ref[...] = h1
            hn2_ref[...] = (h1 * r * g2_ref[...]).astype(BF16)

    col = _merge_specs(tr)
    in_specs = [col(0), col(0), col(6), col(7), col(8), col(9), pl.BlockSpec((1, GDN_D), lambda i: (0, 0))]
    args = [o_a, o_b, proj_m, proj_m, proj_m, proj_m, gnorm]
    if out_proj is None:
        return pl.pallas_call(body, grid=(Lp // tr,), in_specs=in_specs, out_specs=col(0),
                              out_shape=jax.ShapeDtypeStruct((Lp, D_MODEL), BF16), name="merge_fwd")(*args)
    in_specs += [pl.BlockSpec((D_MODEL, D_MODEL), lambda i: (0, 0), pipeline_mode=pl.Buffered(1)),
                 pl.BlockSpec((1, D_MODEL), lambda i: (0, 0))] + r_specs
    return pl.pallas_call(
        body, grid=(Lp // tr,), in_specs=in_specs, out_specs=[col(0), col(0), col(0)],
        out_shape=[jax.ShapeDtypeStruct((Lp, D_MODEL), BF16), jax.ShapeDtypeStruct((Lp, D_MODEL), F32),
                   jax.ShapeDtypeStruct((Lp, D_MODEL), BF16)],
        name="merge_out_proj_rms2")(*args, w_out, g2, *r_args)


def _merge_bwd(dh1b, w_out, o_a, o_b, proj_m, gnorm):
    Lp = o_a.shape[0]
    tr = _tile(Lp, 192, 16)

    def body(d_ref, wo_ref, oa_ref, ob_ref, gz_ref, rg_ref, ga_ref, gb_ref, gn_ref, dc_ref, doa_ref, dob_ref, dgn_ref):
        i = pl.program_id(0)
        gn = gn_ref[...]
        dyv = lax.dot_general(d_ref[...], wo_ref[...], _NT, preferred_element_type=F32)
        oa = oa_ref[...]
        ob = ob_ref[...]
        gz = gz_ref[...]
        rg = rg_ref[...]
        sa = _sig(ga_ref[...])
        sb = _sig(gb_ref[...])
        dya = dyv * sa
        dyb = dyv * sb
        sgz = _sig(gz)
        szz = gz * sgz
        dgn = jnp.zeros((1, GDN_D), F32)
        ya = []
        dgz = []
        for j in range(GDN_H):
            sl = slice(j * GDN_D, (j + 1) * GDN_D)
            seg = oa[:, sl]
            r = lax.rsqrt(jnp.mean(seg * seg, axis=-1, keepdims=True) + EPS)
            xh = seg * r
            oan = xh * gn
            ya.append(oan * szz[:, sl])
            dgz.append(dya[:, sl] * oan * (sgz[:, sl] * (1.0 + gz[:, sl] * (1.0 - sgz[:, sl]))))
            doan = dya[:, sl] * szz[:, sl]
            dgn = dgn + jnp.sum(doan * xh, axis=0, keepdims=True)
            dxh = doan * gn
            doa_ref[:, sl] = r * (dxh - xh * jnp.mean(dxh * xh, axis=-1, keepdims=True))
        ya = jnp.concatenate(ya, axis=1)
        srg = _sig(rg)
        srr = rg * srg
        yb = []
        drg = []
        for j in range(RET_H):
            sl = slice(j * RET_D, (j + 1) * RET_D)
            seg = ob[:, sl]
            r = lax.rsqrt(jnp.mean(seg * seg, axis=-1, keepdims=True) + EPS)
            xh = seg * r
            yb.append(xh * srr[:, sl])
            drg.append(dyb[:, sl] * xh * (srg[:, sl] * (1.0 + rg[:, sl] * (1.0 - srg[:, sl]))))
            dxh = dyb[:, sl] * srr[:, sl]
            dob_ref[:, sl] = r * (dxh - xh * jnp.mean(dxh * xh, axis=-1, keepdims=True))
        yb = jnp.concatenate(yb, axis=1)
        dc_ref[:, 0:D_MODEL] = jnp.concatenate(dgz, axis=1).astype(BF16)
        dc_ref[:, D_MODEL:2 * D_MODEL] = jnp.concatenate(drg, axis=1).astype(BF16)
        dc_ref[:, 2 * D_MODEL:3 * D_MODEL] = (dyv * ya * sa * (1.0 - sa)).astype(BF16)
        dc_ref[:, 3 * D_MODEL:] = (dyv * yb * sb * (1.0 - sb)).astype(BF16)

        @pl.when(i == 0)
        def _():
            dgn_ref[...] = dgn

        @pl.when(i > 0)
        def _():
            dgn_ref[...] += dgn

    col = _merge_specs(tr)
    return pl.pallas_call(
        body, grid=(Lp // tr,),
        in_specs=[col(0), pl.BlockSpec((D_MODEL, D_MODEL), lambda i: (0, 0), pipeline_mode=pl.Buffered(1)),
                  col(0), col(0), col(6), col(7), col(8), col(9), pl.BlockSpec((1, GDN_D), lambda i: (0, 0))],
        out_specs=[pl.BlockSpec((tr, 4 * D_MODEL), lambda i: (i, 0)), col(0), col(0),
                   pl.BlockSpec((1, GDN_D), lambda i: (0, 0))],
        out_shape=[jax.ShapeDtypeStruct((Lp, 4 * D_MODEL), BF16), jax.ShapeDtypeStruct((Lp, D_MODEL), F32),
                   jax.ShapeDtypeStruct((Lp, D_MODEL), F32), jax.ShapeDtypeStruct((1, GDN_D), F32)],
        name="merge_bwd")(dh1b, w_out, o_a, o_b, proj_m, proj_m, proj_m, proj_m, gnorm)


def _ffn_act(up, conv_w, conv_b):
    Lp = up.shape[0]
    tr = _tile(Lp, 192, 16)
    W2 = 2 * D_FF

    def body(main_ref, prev_ref, w_ref, b_ref, act_ref, u_ref):
        i = pl.program_id(0)
        prev = jnp.where(i > 0, prev_ref[...], 0.0)
        ext = jnp.concatenate([prev, main_ref[...]], axis=0)
        u = _taps(_shifted(ext, range(8 - (FFN_CONV - 1), 9)), w_ref[...], tr, b_ref[...])
        a = u[:, :D_FF]
        act_ref[...] = (a * _sig(a) * u[:, D_FF:]).astype(BF16)
        u_ref[...] = u.astype(BF16)

    return pl.pallas_call(
        body, grid=(Lp // tr,),
        in_specs=[pl.BlockSpec((tr, W2), lambda i: (i, 0)), _halo_prev(tr, W2),
                  pl.BlockSpec((FFN_CONV, W2), lambda i: (0, 0)), pl.BlockSpec((1, W2), lambda i: (0, 0))],
        out_specs=[pl.BlockSpec((tr, D_FF), lambda i: (i, 0)), pl.BlockSpec((tr, W2), lambda i: (i, 0))],
        out_shape=[jax.ShapeDtypeStruct((Lp, D_FF), BF16), jax.ShapeDtypeStruct((Lp, W2), BF16)],
        name="ffn_act")(up, up, conv_w, conv_b)


def _ffn_act_bwd(up, u, dact, conv_w):
    Lp = up.shape[0]
    tr = _tile(Lp, 192, 16)
    W2 = 2 * D_FF
    te = tr + 8

    def body(up_ref, u_ref, un_ref, da_ref, dan_ref, w_ref, dup_ref, acc_ref):
        i = pl.program_id(0)
        w = w_ref[...]
        ue = jnp.concatenate([u_ref[...].astype(F32), un_ref[...].astype(F32)[0:8]], axis=0)
        a = ue[:, :D_FF]
        b = ue[:, D_FF:]
        rowe = i * tr + lax.broadcasted_iota(jnp.int32, (te, 1), 0)
        dae = jnp.where(rowe < Lp, jnp.concatenate([da_ref[...], dan_ref[...]], axis=0), 0.0)
        sg = _sig(a)
        du = jnp.concatenate([dae * b * (sg * (1.0 + a * (1.0 - sg))), dae * (a * sg)], axis=1)
        dus = _shifted(du, range(FFN_CONV - 1, -1, -1))
        dup_ref[...] = _taps(dus, w, tr).astype(BF16)
        upm = up_ref[...]
        rows = [jnp.sum(dus[kk][0:tr, :] * upm, axis=0, keepdims=True) for kk in range(FFN_CONV)]
        rows.append(jnp.sum(du[0:tr, :], axis=0, keepdims=True))
        part = jnp.concatenate(rows + [jnp.zeros((8 - len(rows), W2), F32)], axis=0)

        @pl.when(i == 0)
        def _():
            acc_ref[...] = part

        @pl.when(i > 0)
        def _():
            acc_ref[...] += part

    return pl.pallas_call(
        body, grid=(Lp // tr,),
        in_specs=[pl.BlockSpec((tr, W2), lambda i: (i, 0)), pl.BlockSpec((tr, W2), lambda i: (i, 0)),
                  _halo_next(tr, W2, Lp, rows=16), pl.BlockSpec((tr, D_FF), lambda i: (i, 0)), _halo_next(tr, D_FF, Lp),
                  pl.BlockSpec((FFN_CONV, W2), lambda i: (0, 0))],
        out_specs=[pl.BlockSpec((tr, W2), lambda i: (i, 0)), pl.BlockSpec((8, W2), lambda i: (0, 0))],
        out_shape=[jax.ShapeDtypeStruct((Lp, W2), BF16), jax.ShapeDtypeStruct((8, W2), F32)],
        name="ffn_act_bwd")(up, u, u, dact, dact, conv_w)


def _proj_rows(j):
    shift = (jnp.where((j >= 3) & (j < 6), _O_RQ - 3 * D_MODEL, 0) + jnp.where(j == 6, _O_GZ - 6 * D_MODEL, 0)
             + jnp.where(j >= 7, _O_RG - 7 * D_MODEL, 0))
    return j * D_MODEL + shift


def _local_step(hpad, tgt, pad, wt, first_weights=None, late_weights=None, on_ffn_out_grads=None,
                on_w_in_grads=None):
    Lp = hpad.shape[0]
    first = pad + N_META
    pos = jnp.arange(Lp, dtype=F32) - float(pad)
    half = RET_D // 2
    inv = 1.0 / (ROPE_BASE ** (jnp.arange(half, dtype=F32) / half))
    ang = pos[:, None] * inv[None, :]
    cos, sin = jnp.cos(ang), jnp.sin(ang)
    tables = _ret_tables()
    gparams = jnp.zeros((8, LANES), F32).at[0, :GDN_H].set(wt["a_log"]).at[1, :GDN_H].set(wt["dt_bias"])

    hn1 = _rms_fwd(hpad, wt["norm1"], "rms1_fwd")
    if first_weights is not None:
        wt = {**wt, **first_weights(hn1[:8, :LANES].astype(F32) + cos[:8] + sin[:8])}
    w_in_t = wt["w_in_t"]
    w_small_t = jnp.pad(w_in_t[_O_GA:_O_RQ], ((0, LANES - 2 * GDN_H), (0, 0)))
    proj_m = _mm_nn(hn1, w_in_t, bt=True, tm_target=2752, b_rows=(D_MODEL, MAIN_W // D_MODEL, _proj_rows),
                    name="proj_main")
    proj_s = _mm_nn(hn1, w_small_t, bt=True, name="proj_small")
    qkv, gsm, conv_out = _gdn_pre(proj_m, proj_s, wt["gdn_conv_w"], gparams, pad)
    o_a, s_a, t_a = _gdn_chunk_fwd(qkv, gsm)
    o_b, s_b = _ret_chunk_fwd(proj_m, cos, sin, tables)
    if late_weights is not None:
        wt = {**wt, **late_weights(o_b)}
    y, h1, hn2 = _merge_fwd(o_a, o_b, proj_m, wt["gdn_norm"], (wt["w_out"], hpad, wt["norm2"]))
    up = _mm_nn(hn2, wt["w_up_t"], bt=True, name="ffn_up")
    act, u_ffn = _ffn_act(up, wt["ffn_conv_w"], wt["ffn_conv_b"])
    lossvec, dh2, dh2b, d_norm_f = _final(_Producer(act, wt["w_down"], h1), wt["norm_f"], tgt, first)

    d_w_down = _mm_tn(act, dh2b, name="dw_down")
    dact = _mm_nt(dh2b, wt["w_down"], name="d_act")
    dup, ffn_rows = _ffn_act_bwd(up, u_ffn, dact, wt["ffn_conv_w"])
    d_w_up_t = _mm_tn(dup, hn2, name="dw_up")
    dh1, dh1b, d_norm2 = _rms_bwd(h1, wt["norm2"], _Producer(dup, wt["w_up_t"]), dh2, pad, "d_hn2_rms2_bwd")

    d_w_out = _mm_tn(y, dh1b, name="dw_out")
    gnorm = wt["gdn_norm"]
    if on_ffn_out_grads is not None:
        gnorm = gnorm + on_ffn_out_grads(d_w_down, d_w_up_t, d_w_out)[0:1, :]
    d_c, do_a, do_b, d_gnorm = _merge_bwd(dh1b, wt["w_out"], o_a, o_b, proj_m, gnorm)
    d_r = _ret_chunk_bwd(proj_m, cos, sin, tables, do_b, s_b)
    dq, dk, dv, dgs = _gdn_chunk_bwd(qkv, gsm, do_a, s_a, t_a)
    d_a, d_s, conv_rows, gp_rows = _gdn_pre_bwd(proj_m, conv_out, proj_s, wt["gdn_conv_w"], gparams, dq, dk, dv, dgs,
                                                pad)

    segs = [(d_a, w_in_t[_O_GQ:_O_GZ]), (d_r, w_in_t[_O_RQ:_O_RG]),
            (d_c, jnp.concatenate([w_in_t[_O_GZ:_O_GA], w_in_t[_O_RG:_O_END]], axis=0))]
    pa, pr, pc = [_mm_tn(d, hn1, BF16, name="dw_in_%d" % i) for i, (d, _) in enumerate(segs)]
    ps = _mm_tn(d_s, hn1, BF16, name="dw_in_small")
    d_w_in_t = jnp.concatenate([pa, pc[:D_MODEL], ps[:2 * GDN_H], pr, pc[D_MODEL:]], axis=0)
    if on_w_in_grads is not None:
        w_small_t = w_small_t + on_w_in_grads(d_w_in_t)[0:1, 0:1].astype(w_small_t.dtype)
    dhn1 = _mm_sum([(d_s, w_small_t)] + segs[:-1], "d_hn1_first")
    dh0, _, d_norm1 = _rms_bwd(hpad, wt["norm1"], _Producer(*segs[-1], dhn1), dh1, pad, "d_hn1_rms1_bwd")

    grads = {
        "norm1": d_norm1, "w_in_t": d_w_in_t, "gdn_conv_w": conv_rows[:GDN_CONV],
        "a_log": gp_rows[0, :GDN_H], "dt_bias": gp_rows[1, :GDN_H], "gdn_norm": d_gnorm, "w_out": d_w_out,
        "norm2": d_norm2, "w_up_t": d_w_up_t, "ffn_conv_w": ffn_rows[:FFN_CONV],
        "ffn_conv_b": ffn_rows[FFN_CONV:FFN_CONV + 1], "w_down": d_w_down, "norm_f": d_norm_f,
    }
    return lossvec, dh0, grads


def _peer(k):
    ix, iy, ic = lax.axis_index("x"), lax.axis_index("y"), lax.axis_index("c")
    px = 1 - ix if (k >> 2) & 1 else ix
    py = 1 - iy if (k >> 1) & 1 else iy
    pc = 1 - ic if k & 1 else ic
    return (px, py, pc), 4 * px + 2 * py + pc


def _comm_call(body, n, out_shapes, name, args):
    hbm = pl.BlockSpec(memory_space=pl.ANY)
    return pl.pallas_call(
        body, out_shape=out_shapes, in_specs=[hbm] * n, out_specs=[hbm] * n,
        scratch_shapes=[pltpu.SemaphoreType.DMA((n, N_DEV - 1)), pltpu.SemaphoreType.DMA((n, N_DEV - 1)),
                        pltpu.SemaphoreType.DMA((n,))],
        name=name)(*args)


def _all_gather(xs, name):
    n = len(xs)

    def body(*refs):
        x_refs, out_refs = refs[:n], refs[n:2 * n]
        send_sems, recv_sems, local_sems = refs[2 * n:]
        _, me = _peer(0)
        pending = []
        for i in range(n):
            local = pltpu.make_async_copy(x_refs[i], out_refs[i].at[me], local_sems.at[i])
            local.start()
            pending.append(local)
        sends = []
        for i in range(n):
            for k in range(1, N_DEV):
                dev, _ = _peer(k)
                cp = pltpu.make_async_remote_copy(
                    src_ref=x_refs[i], dst_ref=out_refs[i].at[me], send_sem=send_sems.at[i, k - 1],
                    recv_sem=recv_sems.at[i, k - 1], device_id=dev, device_id_type=MESH_T)
                cp.start()
                sends.append(cp)
        for i in range(n):
            for k in range(1, N_DEV):
                dev, idx = _peer(k)
                pltpu.make_async_remote_copy(
                    src_ref=x_refs[i], dst_ref=out_refs[i].at[idx], send_sem=send_sems.at[i, k - 1],
                    recv_sem=recv_sems.at[i, k - 1], device_id=dev, device_id_type=MESH_T).wait_recv()
        for cp in sends:
            cp.wait_send()
        for local in pending:
            local.wait()

    out_shapes = [jax.ShapeDtypeStruct((N_DEV,) + a.shape, a.dtype) for a in xs]
    return _comm_call(body, n, out_shapes, name, xs)


def _all_to_all(gs, name):
    n = len(gs)

    def body(*refs):
        g_refs, out_refs = refs[:n], refs[n:2 * n]
        send_sems, recv_sems, local_sems = refs[2 * n:]
        _, me = _peer(0)
        pending = []
        for i in range(n):
            local = pltpu.make_async_copy(g_refs[i].at[me], out_refs[i].at[0], local_sems.at[i])
            local.start()
            pending.append(local)
        sends = []
        for i in range(n):
            for k in range(1, N_DEV):
                dev, idx = _peer(k)
                cp = pltpu.make_async_remote_copy(
                    src_ref=g_refs[i].at[idx], dst_ref=out_refs[i].at[k], send_sem=send_sems.at[i, k - 1],
                    recv_sem=recv_sems.at[i, k - 1], device_id=dev, device_id_type=MESH_T)
                cp.start()
                sends.append(cp)
        for cp in sends:
            cp.wait_recv()
        for cp in sends:
            cp.wait_send()
        for local in pending:
            local.wait()

    out_shapes = [jax.ShapeDtypeStruct(g.shape, g.dtype) for g in gs]
    return _comm_call(body, n, out_shapes, name, gs)


_SPLIT_RELATIONS = {"gather": tuple(range(1, N_DEV)), "a2a": tuple(range(1, N_DEV)), "chip": (1, 2, 4, 6),
                    "forward": (2, 4, 6)}


def _split_copies(kind, src_refs, land_refs, send_sems, recv_sems, local_sems, with_recv):
    n = len(land_refs)
    rels = _SPLIT_RELATIONS[kind]
    _, me = _peer(0)
    locals_, remotes = [], []
    for i in range(n):
        if kind in ("gather", "chip"):
            locals_.append(pltpu.make_async_copy(src_refs[i], land_refs[i].at[me], local_sems.at[i]))
        elif kind == "a2a":
            locals_.append(pltpu.make_async_copy(src_refs[i].at[me], land_refs[i].at[0], local_sems.at[i]))
        for jj, k in enumerate(rels):
            dev, idx = _peer(k)
            if kind in ("gather", "chip"):
                src, dst, mine = src_refs[i], land_refs[i].at[me], land_refs[i].at[idx]
            elif kind == "a2a":
                src, dst, mine = src_refs[i].at[idx], land_refs[i].at[k], land_refs[i].at[k]
            else:
                dev, _ = _peer(1)
                _, came = _peer(k + 1)
                src, dst, mine = land_refs[i].at[idx], land_refs[i].at[idx], land_refs[i].at[came]
            j = i * len(rels) + jj
            send = pltpu.make_async_remote_copy(
                src_ref=src, dst_ref=dst, send_sem=send_sems.at[j], recv_sem=recv_sems.at[j],
                device_id=dev, device_id_type=MESH_T)
            recv = pltpu.make_async_remote_copy(
                src_ref=src, dst_ref=mine, send_sem=send_sems.at[j], recv_sem=recv_sems.at[j],
                device_id=dev, device_id_type=MESH_T) if with_recv else None
            remotes.append((send, recv))
    return locals_, remotes


_HBM = pl.BlockSpec(memory_space=pltpu.HBM)
_SEM = pl.BlockSpec(memory_space=pltpu.SEMAPHORE)
_ANY = pl.BlockSpec(memory_space=pl.ANY)


def _split_start(srcs, kind, name, after):
    n = len(srcs)
    if kind == "forward":
        arrays = list(srcs)
    else:
        gathers = kind in ("gather", "chip")
        arrays = list(srcs) + [lax.empty(((N_DEV,) + a.shape) if gathers else a.shape, a.dtype) for a in srcs]
    na = len(arrays)

    def body(*refs):
        src_refs, land_refs = refs[:n], refs[na - n:na]
        send_sems, recv_sems, local_sems = refs[na + 1:na + 4]
        token = refs[-1]
        locals_, remotes = _split_copies(kind, src_refs, land_refs, send_sems, recv_sems, local_sems, False)
        for cp in locals_:
            cp.start()
        for send, _ in remotes:
            send.start()
        token[...] = jnp.zeros_like(token)

    ncp = n * len(_SPLIT_RELATIONS[kind])
    sems = (pltpu.SemaphoreType.DMA((ncp,)), pltpu.SemaphoreType.DMA((ncp,)), pltpu.SemaphoreType.DMA((n,)))
    thru = tuple(pltpu.HBM(a.shape, a.dtype) for a in arrays)
    outs = pl.pallas_call(
        body, name=name,
        out_shape=sems + thru + (jax.ShapeDtypeStruct((8, LANES), F32),),
        in_specs=[_HBM] * na + [_ANY],
        out_specs=[_SEM] * 3 + [_HBM] * na + [pl.BlockSpec(memory_space=pltpu.VMEM)],
        input_output_aliases={i: 3 + i for i in range(na)},
        compiler_params=pltpu.CompilerParams(has_side_effects=pltpu.SideEffectType.DATAFLOW_SIDE_EFFECTING),
    )(*[pltpu.with_memory_space_constraint(a, pltpu.HBM) for a in arrays], after)
    return (kind, n, outs[:3], outs[3:3 + na]), outs[-1]


def _split_wait(handle, name, after):
    kind, n, sems, thru = handle
    na = len(thru)

    def body(*refs):
        src_refs, land_refs = refs[:n], refs[na - n:na]
        send_sems, recv_sems, local_sems = refs[na:na + 3]
        locals_, remotes = _split_copies(kind, src_refs, land_refs, send_sems, recv_sems, local_sems, True)
        for send, recv in remotes:
            send.wait_send()
            recv.wait_recv()
        for cp in locals_:
            cp.wait()

    outs = pl.pallas_call(
        body, name=name, out_shape=tuple(pltpu.HBM(a.shape, a.dtype) for a in thru),
        in_specs=[_HBM] * na + [_SEM] * 3 + [_ANY], out_specs=[_HBM] * na,
        input_output_aliases={i: i for i in range(na)},
        compiler_params=pltpu.CompilerParams(has_side_effects=pltpu.SideEffectType.DATAFLOW_SIDE_EFFECTING),
    )(*thru, *sems, after)
    return list(outs[na - n:])


def _adamw(gslabs, w, m, v, name):
    R, Cw = w.shape
    if R % 8 == 0:
        tr, tc = _tile(R, 64 if Cw > 1024 else 128, 8), Cw
    else:
        tr, tc = R, LANES
    c1 = 1.0 - ADAM_B1 ** ADAM_STEP
    c2 = 1.0 - ADAM_B2 ** ADAM_STEP

    def body(g_ref, w_ref, m_ref, v_ref, go_ref, d_ref, mo_ref, vo_ref):
        g = g_ref[0].astype(F32)
        for k in range(1, N_DEV):
            g = g + g_ref[k].astype(F32)
        mn = ADAM_B1 * m_ref[...] + (1.0 - ADAM_B1) * g
        vn = ADAM_B2 * v_ref[...] + (1.0 - ADAM_B2) * (g * g)
        m_hat = mn / c1
        v_hat = vn / c2
        go_ref[...] = g
        d_ref[...] = -ADAM_LR * (m_hat / (jnp.sqrt(v_hat) + ADAM_EPS) + ADAM_WD * w_ref[...])
        mo_ref[...] = mn
        vo_ref[...] = vn

    blk = pl.BlockSpec((tr, tc), lambda i, j: (i, j))
    return pl.pallas_call(
        body, grid=(R // tr, Cw // tc),
        in_specs=[pl.BlockSpec((N_DEV, tr, tc), lambda i, j: (0, i, j)), blk, blk, blk],
        out_specs=[blk] * 4, out_shape=[jax.ShapeDtypeStruct((R, Cw), F32)] * 4, name=name)(gslabs, w, m, v)


def _pack(arrs, row_mult, dtype=F32):
    parts = []
    total = 0
    for a in arrs:
        f = a.reshape(-1).astype(dtype)
        n = -(-f.shape[0] // 1024) * 1024
        parts.append(jnp.pad(f, (0, n - f.shape[0])))
        total += n
    rows = total // LANES
    rows_p = -(-rows // row_mult) * row_mult
    flat = jnp.concatenate(parts)
    flat = jnp.pad(flat, (0, rows_p * LANES - total))
    return flat.reshape(rows_p, LANES)


def _unpack(packed, shapes):
    lead = packed.shape[:-2]
    flat = packed.reshape(lead + (-1,))
    out = []
    off = 0
    for s in shapes:
        n = int(np.prod(s))
        out.append(flat[..., off:off + n].reshape(lead + tuple(s)))
        off += -(-n // 1024) * 1024
    return out


def _gather_cols(stacked):
    d, r, c = stacked.shape
    return stacked.transpose(1, 0, 2).reshape(r, d * c)


def _scatter_cols(full):
    r, n = full.shape
    return full.reshape(r, N_DEV, n // N_DEV).transpose(1, 0, 2)


def kernel(x, meta, norm1, w_in, gdn_conv_w, gdn_a_log, gdn_dt_bias, gdn_norm, w_out, norm2, w_ffn_up, ffn_conv_w, ffn_conv_b, w_ffn_down, norm_f, loss_target, m_meta, m_norm1, m_w_in, m_gdn_conv_w, m_gdn_a_log, m_gdn_dt_bias, m_gdn_norm, m_w_out, m_norm2, m_w_ffn_up, m_ffn_conv_w, m_ffn_conv_b, m_w_ffn_down, m_norm_f, v_meta, v_norm1, v_w_in, v_gdn_conv_w, v_gdn_a_log, v_gdn_dt_bias, v_gdn_norm, v_w_out, v_norm2, v_w_ffn_up, v_ffn_conv_w, v_ffn_conv_b, v_w_ffn_down, v_norm_f):
    S = x.shape[1]
    L = N_META + S
    pad = (-L) % CHUNK
    Lp = L + pad

    tr_ = lambda a: jnp.swapaxes(a[0], 0, 1)
    big = [tr_(w_in), w_out[0], tr_(w_ffn_up), w_ffn_down[0]]
    small = [meta, gdn_conv_w, ffn_conv_w]
    small_all, = _all_gather([_pack(small, 8)], "gather_small_weights")
    first, first_token = _split_start([big[0].astype(BF16)], "chip", "gather_w_in_start", small_all)
    late, late_token = _split_start([a.astype(BF16) for a in big[1:]], "gather", "gather_late_start", first_token)

    def first_weights(after):
        half = _split_wait(first, "gather_w_in_wait", after)
        second, second_token = _split_start(half, "forward", "gather_w_in_forward_start", after)
        w_in_s, = _split_wait(second, "gather_w_in_forward_wait", second_token)
        return {"w_in_t": w_in_s.reshape(_O_END, D_MODEL)}

    def late_weights(after):
        w_out_s, w_up_s, w_down_s = _split_wait(late, "gather_late_wait", after)
        return {"w_out": w_out_s.reshape(D_MODEL, D_MODEL), "w_up_t": w_up_s.reshape(2 * D_FF, D_MODEL),
                "w_down": w_down_s.reshape(D_FF, D_MODEL)}

    meta_s, gconv_s, fconv_s = _unpack(small_all, [a.shape for a in small])
    wt = {
        "norm1": norm1 + jnp.tile(late_token[0:1, :], (1, D_MODEL // LANES)),
        "gdn_conv_w": _gather_cols(gconv_s[:, 0]), "a_log": gdn_a_log[0], "dt_bias": gdn_dt_bias[0],
        "gdn_norm": gdn_norm, "norm2": norm2, "ffn_conv_w": _gather_cols(fconv_s[:, 0]), "ffn_conv_b": ffn_conv_b,
        "norm_f": norm_f.reshape(1, D_MODEL),
    }
    meta_f = _gather_cols(meta_s)

    pending = {}

    def on_ffn_out_grads(d_w_down, d_w_up_t, d_w_out):
        srcs = [d_w_out.reshape(N_DEV, D_MODEL // N_DEV, D_MODEL), d_w_up_t.reshape(N_DEV, 2 * D_FF // N_DEV, D_MODEL),
                d_w_down.reshape(N_DEV, D_FF // N_DEV, D_MODEL)]
        pending["ffn_out"], token = _split_start(srcs, "a2a", "exchange_ffn_out_start", d_w_out)
        return token

    def on_w_in_grads(d_w_in_t):
        slabs = d_w_in_t.astype(BF16).reshape(N_DEV, _O_END // N_DEV, D_MODEL)
        pending["w_in"], token = _split_start([slabs], "a2a", "exchange_w_in_start", d_w_in_t)
        return token

    head = jnp.concatenate([jnp.zeros((pad, D_MODEL), F32), meta_f], axis=0)
    if S >= 2 * 704:
        hpad = _Rows(x[0], pad + N_META, head)
        tgt = _Rows(loss_target[0], pad + N_META)
    else:
        hpad = jnp.concatenate([head, x[0]], axis=0)
        tgt = jnp.concatenate([jnp.zeros((pad + N_META, D_MODEL), F32), loss_target[0]], axis=0)
    lossvec, dh0, gr = _local_step(hpad, tgt, pad, wt, first_weights, late_weights, on_ffn_out_grads, on_w_in_grads)

    loss = lax.psum(jnp.sum(lossvec), ("x", "y", "c"))
    grad_x = dh0[pad + N_META:][None]

    big_m = [tr_(m_w_in), m_w_out[0], tr_(m_w_ffn_up), m_w_ffn_down[0]]
    big_v = [tr_(v_w_in), v_w_out[0], tr_(v_w_ffn_up), v_w_ffn_down[0]]
    slabs_ffn_out = _split_wait(pending["ffn_out"], "exchange_ffn_out_wait", dh0)
    big_out = [None] + [_adamw(slabs_ffn_out[i - 1], big[i], big_m[i], big_v[i], "adamw_big_%d" % i)
                        for i in range(1, len(big))]
    g_sm = [_scatter_cols(dh0[pad:pad + N_META]), _scatter_cols(gr["gdn_conv_w"]), _scatter_cols(gr["ffn_conv_w"])]
    g_small = jnp.stack([_pack([g[d] for g in g_sm], 8) for d in range(N_DEV)])
    slabs_small, = _all_to_all([g_small], "exchange_small_gradients")
    small_out = _adamw(slabs_small, _pack(small, 8), _pack([m_meta, m_gdn_conv_w, m_ffn_conv_w], 8),
                       _pack([v_meta, v_gdn_conv_w, v_ffn_conv_w], 8), "adamw_small_sharded")
    small_un = [_unpack(o, [a.shape for a in small]) for o in small_out]
    rep_w = [norm1, gdn_a_log, gdn_dt_bias, gdn_norm, norm2, ffn_conv_b, norm_f]
    rep_m = [m_norm1, m_gdn_a_log, m_gdn_dt_bias, m_gdn_norm, m_norm2, m_ffn_conv_b, m_norm_f]
    rep_v = [v_norm1, v_gdn_a_log, v_gdn_dt_bias, v_gdn_norm, v_norm2, v_ffn_conv_b, v_norm_f]
    rep_g = [gr["norm1"], gr["a_log"], gr["dt_bias"], gr["gdn_norm"], gr["norm2"], gr["ffn_conv_b"], gr["norm_f"]]
    rep_slabs, = _all_gather([_pack(rep_g, 8)], "gather_small_gradients")
    rep_out = _adamw(rep_slabs, _pack(rep_w, 8), _pack(rep_m, 8), _pack(rep_v, 8), "adamw_replicated")
    rep_shapes = [a.shape for a in rep_w]
    rp_g, rp_d, rp_nm, rp_nv = [_unpack(o, rep_shapes) for o in rep_out]

    slabs_w_in, = _split_wait(pending["w_in"], "exchange_w_in_wait", rep_out[0])
    big_out[0] = _adamw(slabs_w_in, big[0], big_m[0], big_v[0], "adamw_big_0")
    back = lambda a: jnp.swapaxes(a, 0, 1)[None]
    sh_g, sh_d, sh_nm, sh_nv = [
        [small_un[j][0], back(big_out[0][j]), small_un[j][1], big_out[1][j][None], back(big_out[2][j]),
         small_un[j][2], big_out[3][j][None]] for j in range(4)]

    def order(sh, rp):
        return [sh[0], rp[0], sh[1], sh[2], rp[1], rp[2], rp[3], sh[3], rp[4], sh[4], sh[5], rp[5], sh[6], rp[6]]

    return (loss, grad_x, *order(sh_g, rp_g), *order(sh_d, rp_d), *order(sh_nm, rp_nm), *order(sh_nv, rp_nv))
```

```python
import functools
import math

import numpy as np
import jax
import jax.numpy as jnp
from jax import lax
from jax.experimental import pallas as pl
from jax.experimental.pallas import tpu as pltpu

F32 = jnp.float32
BF16 = jnp.bfloat16

D_MODEL = 1024
N_META = 16
CHUNK = 64
GDN_H = 8
GDN_D = 128
RET_H = 4
RET_D = 256
D_FF = 2816
GDN_CONV = 4
FFN_CONV = 3
ROPE_BASE = 10000.0
EPS = 1e-6
N_DEV = 8
LANES = 128
MAIN_W = 10 * 1024
_O_GQ, _O_GZ, _O_GA, _O_RQ, _O_RG, _O_GATE, _O_END = 0, 3072, 4096, 4112, 7184, 8208, 10256

ADAM_LR = 0.001
ADAM_B1 = 0.9
ADAM_B2 = 0.999
ADAM_EPS = 1e-08
ADAM_WD = 0.01
ADAM_STEP = 10

MESH_T = pl.DeviceIdType.MESH


def _tile(n, target, mult):
    best = None
    for d in range(mult, min(n, target) + 1, mult):
        if n % d == 0:
            best = d
    assert best is not None, (n, target, mult)
    return best


def _sig(x):
    return 0.5 * jnp.tanh(0.5 * x) + 0.5


def _d(a, b):
    return jnp.dot(a.astype(BF16), b.astype(BF16), preferred_element_type=F32)


def _dnt(a, b):
    return lax.dot_general(a.astype(BF16), b.astype(BF16), (((1,), (1,)), ((), ())), preferred_element_type=F32)


def _dtn(a, b):
    return lax.dot_general(a.astype(BF16), b.astype(BF16), (((0,), (0,)), ((), ())), preferred_element_type=F32)


def _dxg(a, b, dims):
    f = functools.partial(lax.dot_general, dimension_numbers=dims, preferred_element_type=F32)
    ab = a.astype(BF16)
    b1 = b.astype(BF16)
    r1 = b - b1.astype(F32)
    b2 = r1.astype(BF16)
    b3 = (r1 - b2.astype(F32)).astype(BF16)
    return f(ab, b1) + (f(ab, b2) + f(ab, b3))


def _dx(a, b):
    return _dxg(a, b, (((1,), (0,)), ((), ())))


def _dxnt(a, b):
    return _dxg(a, b, (((1,), (1,)), ((), ())))


def _split(a):
    hi = a.astype(BF16)
    return hi, (a - hi.astype(F32)).astype(BF16)


def _d3g(a, b, dims):
    ah, al = _split(a)
    bh, bl = _split(b)
    f = functools.partial(lax.dot_general, dimension_numbers=dims, preferred_element_type=F32)
    if dims == _NN:
        rows = a.shape[0]
        both = f(jnp.concatenate([ah, al], axis=0), bh)
        return both[:rows] + (f(ah, bl) + both[rows:])
    return f(ah, bh) + (f(ah, bl) + f(al, bh))


def _d2x(a, b, dims):
    ah, al = _split(a)
    f = functools.partial(lax.dot_general, dimension_numbers=dims, preferred_element_type=F32)
    bb = b.astype(BF16)
    return f(ah, bb) + f(al, bb)


_NN = (((1,), (0,)), ((), ()))
_NT = (((1,), (1,)), ((), ()))
_TN = (((0,), (0,)), ((), ()))


def _rowsum(x):
    return jnp.sum(x, axis=1, keepdims=True)


def _allsum(x):
    return jnp.sum(jnp.sum(x, axis=1, keepdims=True), axis=0, keepdims=True)


def _mm_nn(a, b, res=None, out_dtype=F32, bt=False, tm_target=704, b_rows=None, name="mm_nn"):
    M, K = a.shape
    N = b.shape[0] if bt else b.shape[1]
    tm = _tile(M, tm_target, 16)
    if b_rows is None:
        tn = _tile(N, 2816, 128)
    else:
        tn, n_tiles, start = b_rows
        N = tn * n_tiles

    def body(*refs):
        if res is None:
            a_ref, b_ref, o_ref = refs
        else:
            a_ref, b_ref, r_ref, o_ref = refs
        acc = lax.dot_general(a_ref[...], b_ref[...], _NT if bt else _NN, preferred_element_type=F32)
        if res is not None:
            acc = acc + r_ref[...]
        o_ref[...] = acc.astype(out_dtype)

    b_spec = pl.BlockSpec((tn, K), lambda j, i: (j, 0)) if bt else pl.BlockSpec((K, tn), lambda j, i: (0, j))
    if b_rows is not None:
        b_spec = pl.BlockSpec((pl.Element(tn), pl.Element(K)), lambda j, i: (pl.multiple_of(start(j), 16), 0))
    in_specs = [pl.BlockSpec((tm, K), lambda j, i: (i, 0)), b_spec]
    args = [a, b]
    if res is not None:
        in_specs.append(pl.BlockSpec((tm, tn), lambda j, i: (i, j)))
        args.append(res)
    return pl.pallas_call(
        body, grid=(N // tn, M // tm), in_specs=in_specs,
        out_specs=pl.BlockSpec((tm, tn), lambda j, i: (i, j)),
        out_shape=jax.ShapeDtypeStruct((M, N), out_dtype), name=name)(*args)


def _mm_sum(pairs, name):
    M = pairs[0][0].shape[0]
    N = pairs[0][1].shape[1]
    tm = _tile(M, 704, 16)
    n = len(pairs)

    def body(*refs):
        o_ref = refs[-1]
        acc = jnp.dot(refs[0][...], refs[1][...], preferred_element_type=F32)
        for i in range(1, n):
            acc = acc + jnp.dot(refs[2 * i][...], refs[2 * i + 1][...], preferred_element_type=F32)
        o_ref[...] = acc

    specs, args = [], []
    for a, b in pairs:
        specs += [pl.BlockSpec((tm, a.shape[1]), lambda i: (i, 0)),
                  pl.BlockSpec(b.shape, lambda i: (0, 0), pipeline_mode=pl.Buffered(1))]
        args += [a, b]
    return pl.pallas_call(
        body, grid=(M // tm,), in_specs=specs, out_specs=pl.BlockSpec((tm, N), lambda i: (i, 0)),
        out_shape=jax.ShapeDtypeStruct((M, N), F32), name=name)(*args)


def _mm_nt(a, b, res=None, name="mm_nt"):
    M, Nc = a.shape
    K = b.shape[0]
    tm = _tile(M, 704, 16)
    tc = _tile(Nc, 5632, 128)

    def body(*refs):
        if res is None:
            a_ref, b_ref, o_ref = refs
        else:
            a_ref, b_ref, r_ref, o_ref = refs
        c = pl.program_id(1)
        p = lax.dot_general(a_ref[...], b_ref[...], (((1,), (1,)), ((), ())), preferred_element_type=F32)

        @pl.when(c == 0)
        def _():
            if res is None:
                o_ref[...] = p
            else:
                o_ref[...] = p + r_ref[...]

        @pl.when(c > 0)
        def _():
            o_ref[...] += p

    in_specs = [pl.BlockSpec((tm, tc), lambda i, c: (i, c)), pl.BlockSpec((K, tc), lambda i, c: (0, c))]
    args = [a, b]
    if res is not None:
        in_specs.append(pl.BlockSpec((tm, K), lambda i, c: (i, 0)))
        args.append(res)
    return pl.pallas_call(
        body, grid=(M // tm, Nc // tc), in_specs=in_specs,
        out_specs=pl.BlockSpec((tm, K), lambda i, c: (i, 0)),
        out_shape=jax.ShapeDtypeStruct((M, K), F32), name=name)(*args)


def _mm_tn(a, b, out_dtype=F32, name="mm_tn"):
    M, K = a.shape
    N = b.shape[1]
    tm = _tile(M, 2752, 16)
    tk = _tile(K, 1408, 128)
    tn = _tile(N, 1408, 128)
    steps = M // tm

    def body(a_ref, b_ref, o_ref, *scratch):
        acc = scratch[0] if scratch else o_ref
        m = pl.program_id(2)
        p = lax.dot_general(a_ref[...], b_ref[...], (((0,), (0,)), ((), ())), preferred_element_type=F32)

        @pl.when(m == 0)
        def _():
            acc[...] = p

        @pl.when(m > 0)
        def _():
            acc[...] += p

        if scratch:
            @pl.when(m == steps - 1)
            def _():
                o_ref[...] = acc[...].astype(out_dtype)

    return pl.pallas_call(
        body, grid=(K // tk, N // tn, steps),
        in_specs=[pl.BlockSpec((tm, tk), lambda kk, j, m: (m, kk)), pl.BlockSpec((tm, tn), lambda kk, j, m: (m, j))],
        out_specs=pl.BlockSpec((tk, tn), lambda kk, j, m: (kk, j)),
        out_shape=jax.ShapeDtypeStruct((K, N), out_dtype),
        scratch_shapes=[] if out_dtype == F32 else [pltpu.VMEM((tk, tn), F32)], name=name)(a, b)


class _Rows:
    def __init__(self, body, first, head=None):
        self.body, self.first, self.head = body, first, head
        self.shape = (body.shape[0] + first, body.shape[1])


def _rows_operands(x, tr):
    if not isinstance(x, _Rows):
        return [x], [pl.BlockSpec((tr, x.shape[1]), lambda i: (i, 0))]
    assert x.first % 8 == 0 and x.first <= tr <= x.body.shape[0] and x.shape[0] % tr == 0
    width = x.shape[1]
    args = [x.body]
    specs = [pl.BlockSpec((pl.Element(tr), pl.Element(width)),
                          lambda i: (pl.multiple_of(jnp.maximum(i * tr - x.first, 0), 8), 0))]
    if x.head is not None:
        args.append(jnp.pad(x.head, ((0, tr - x.first), (0, 0))))
        specs.append(pl.BlockSpec((tr, width), lambda i: (0, 0)))
    return args, specs


def _rows_tile(x, refs, i, tr):
    blk = refs[0][...]
    if not isinstance(x, _Rows):
        return blk
    shifted = pltpu.roll(blk, x.first, 0)
    if x.head is not None:
        row = lax.broadcasted_iota(jnp.int32, (tr, 1), 0)
        shifted = jnp.where(row < x.first, refs[1][...], shifted)
    return jnp.where(i == 0, shifted, blk)


def _rms_fwd(x, g, name):
    Lp = x.shape[0]
    tr = _tile(Lp, 704, 16)
    args, specs = _rows_operands(x, tr)
    n = len(args)

    def body(*refs):
        g_ref, o_ref = refs[n:]
        xv = _rows_tile(x, refs[:n], pl.program_id(0), tr)
        r = lax.rsqrt(jnp.mean(xv * xv, axis=-1, keepdims=True) + EPS)
        o_ref[...] = (xv * r * g_ref[...]).astype(BF16)

    return pl.pallas_call(
        body, grid=(Lp // tr,),
        in_specs=specs + [pl.BlockSpec((1, D_MODEL), lambda i: (0, 0))],
        out_specs=pl.BlockSpec((tr, D_MODEL), lambda i: (i, 0)),
        out_shape=jax.ShapeDtypeStruct((Lp, D_MODEL), BF16), name=name)(*args, g)


class _Producer:
    def __init__(self, a, b, res=None):
        self.a, self.b, self.res = a, b, res
        self.tr = _tile(a.shape[0], 704, 16)
        K = a.shape[1]
        r_args, r_specs = ([], []) if res is None else _rows_operands(res, self.tr)
        self.args = [a, b] + r_args
        self.specs = [pl.BlockSpec((self.tr, K), lambda i: (i, 0)),
                      pl.BlockSpec((K, D_MODEL), lambda i: (0, 0), pipeline_mode=pl.Buffered(1))] + r_specs

    def tile(self, refs, i):
        acc = jnp.dot(refs[0][...], refs[1][...], preferred_element_type=F32)
        return acc if self.res is None else acc + _rows_tile(self.res, refs[2:], i, self.tr)


def _rms_bwd(x, g, dy, dres, pad, name):
    Lp = x.shape[0]
    fused = isinstance(dy, _Producer)
    tr = dy.tr if fused else _tile(Lp, 256, 16)
    n = len(dy.args) if fused else 1
    x_args, x_specs = _rows_operands(x, tr)
    nx = len(x_args)

    def body(*refs):
        g_ref, dr_ref, dx_ref, dxb_ref, dg_ref = refs[n + nx:]
        i = pl.program_id(0)
        xv = _rows_tile(x, refs[n:n + nx], i, tr)
        r = lax.rsqrt(jnp.mean(xv * xv, axis=-1, keepdims=True) + EPS)
        xh = xv * r
        dyv = dy.tile(refs[:n], i) if fused else refs[0][...]
        dxh = dyv * g_ref[...]
        dx = r * (dxh - xh * jnp.mean(dxh * xh, axis=-1, keepdims=True)) + dr_ref[...]
        row = i * tr + lax.broadcasted_iota(jnp.int32, (tr, 1), 0)
        dx = jnp.where(row >= pad, dx, 0.0)
        dx_ref[...] = dx
        dxb_ref[...] = dx.astype(BF16)
        part = jnp.sum(dyv * xh, axis=0, keepdims=True)

        @pl.when(i == 0)
        def _():
            dg_ref[...] = part

        @pl.when(i > 0)
        def _():
            dg_ref[...] += part

    blk = pl.BlockSpec((tr, D_MODEL), lambda i: (i, 0))
    vec = pl.BlockSpec((1, D_MODEL), lambda i: (0, 0))
    return pl.pallas_call(
        body, grid=(Lp // tr,), in_specs=(dy.specs if fused else [blk]) + x_specs + [vec, blk],
        out_specs=[blk, blk, vec],
        out_shape=[jax.ShapeDtypeStruct((Lp, D_MODEL), F32), jax.ShapeDtypeStruct((Lp, D_MODEL), BF16),
                   jax.ShapeDtypeStruct((1, D_MODEL), F32)], name=name)(*(dy.args if fused else [dy]), *x_args, g, dres)


def _final(h2, g, tgt, first_row):
    fused = isinstance(h2, _Producer)
    Lp = h2.a.shape[0] if fused else h2.shape[0]
    tr = h2.tr if fused else _tile(Lp, 256, 16)
    n = len(h2.args) if fused else 1
    t_args, t_specs = _rows_operands(tgt, tr)
    nt = len(t_args)

    def body(*refs):
        g_ref = refs[n]
        loss_ref, dx_ref, dxb_ref, dg_ref = refs[n + 1 + nt:]
        i = pl.program_id(0)
        xv = h2.tile(refs[:n], i) if fused else refs[0][...]
        tv = _rows_tile(tgt, refs[n + 1:n + 1 + nt], i, tr)
        gv = g_ref[...]
        r = lax.rsqrt(jnp.mean(xv * xv, axis=-1, keepdims=True) + EPS)
        xh = xv * r
        row = i * tr + lax.broadcasted_iota(jnp.int32, (tr, 1), 0)
        err = jnp.where(row >= first_row, xh * gv - tv, 0.0)
        lpart = jnp.sum(err * err, axis=0, keepdims=True) * (0.5 / D_MODEL)
        dyv = err * (1.0 / D_MODEL)
        dxh = dyv * gv
        dx = r * (dxh - xh * jnp.mean(dxh * xh, axis=-1, keepdims=True))
        dx_ref[...] = dx
        dxb_ref[...] = dx.astype(BF16)
        part = jnp.sum(dyv * xh, axis=0, keepdims=True)

        @pl.when(i == 0)
        def _():
            dg_ref[...] = part
            loss_ref[...] = lpart

        @pl.when(i > 0)
        def _():
            dg_ref[...] += part
            loss_ref[...] += lpart

    blk = pl.BlockSpec((tr, D_MODEL), lambda i: (i, 0))
    vec = pl.BlockSpec((1, D_MODEL), lambda i: (0, 0))
    return pl.pallas_call(
        body, grid=(Lp // tr,), in_specs=(h2.specs if fused else [blk]) + [vec] + t_specs,
        out_specs=[vec, blk, blk, vec],
        out_shape=[jax.ShapeDtypeStruct((1, D_MODEL), F32), jax.ShapeDtypeStruct((Lp, D_MODEL), F32),
                   jax.ShapeDtypeStruct((Lp, D_MODEL), BF16), jax.ShapeDtypeStruct((1, D_MODEL), F32)],
        name="final_norm_loss")(*(h2.args if fused else [h2]), g, *t_args)


def _halo_prev(tr, width, col=0):
    return pl.BlockSpec((8, width), lambda i: (jnp.maximum(i * (tr // 8) - 1, 0), col))


def _halo_next(tr, width, nrows, col=0, rows=8):
    last = nrows // rows - 1
    return pl.BlockSpec((rows, width), lambda i: (jnp.minimum((i + 1) * (tr // rows), last), col))


def _shifted(x, offs):
    n = x.shape[0]
    return [x if off == 0 else pltpu.roll(x, n - off, 0) for off in offs]


def _taps(wins, w, rows, bias=None):
    acc = w[0:1, :] * wins[0][0:rows, :]
    if bias is not None:
        acc = acc + bias
    for kk in range(1, len(wins)):
        acc = acc + w[kk:kk + 1, :] * wins[kk][0:rows, :]
    return acc


def _gdn_pre(proj_m, proj_s, conv_w, gparams, pad):
    Lp = proj_m.shape[0]
    tr = _tile(Lp, 192, 64)
    W3 = 3 * D_MODEL

    def body(main_ref, prev_ref, s_ref, w_ref, gp_ref, qkv_ref, gsm_ref, c_ref):
        i = pl.program_id(0)
        prev = jnp.where(i > 0, prev_ref[...], 0.0)
        ext = jnp.concatenate([prev, main_ref[...]], axis=0)
        c = _taps(_shifted(ext, range(8 - (GDN_CONV - 1), 9)), w_ref[...], tr)
        c_ref[...] = c.astype(BF16)
        s = c * _sig(c)
        scale = GDN_D ** -0.5
        for j in range(2 * GDN_H):
            seg = s[:, j * GDN_D:(j + 1) * GDN_D]
            r = lax.rsqrt(_rowsum(seg * seg) + EPS)
            if j < GDN_H:
                r = r * scale
            qkv_ref[:, j * GDN_D:(j + 1) * GDN_D] = seg * r
        qkv_ref[:, 2 * D_MODEL:] = s[:, 2 * D_MODEL:]
        sm = s_ref[...]
        gp = gp_ref[...]
        lane = lax.broadcasted_iota(jnp.int32, sm.shape, 1)
        z = sm + gp[1:2, :]
        softplus = jnp.maximum(z, 0.0) + jnp.log(1.0 + jnp.exp(-jnp.abs(z)))
        lg = -jnp.exp(gp[0:1, :]) * softplus
        row = i * tr + lax.broadcasted_iota(jnp.int32, (tr, 1), 0)
        out = jnp.where(lane < GDN_H, lg, jnp.where(lane < 2 * GDN_H, _sig(sm), 0.0))
        gsm_ref[...] = jnp.where(row >= pad, out, 0.0)

    return pl.pallas_call(
        body, grid=(Lp // tr,),
        in_specs=[pl.BlockSpec((tr, W3), lambda i: (i, 0)), _halo_prev(tr, W3),
                  pl.BlockSpec((tr, LANES), lambda i: (i, 0)),
                  pl.BlockSpec((GDN_CONV, W3), lambda i: (0, 0)), pl.BlockSpec((8, LANES), lambda i: (0, 0))],
        out_specs=[pl.BlockSpec((tr, W3), lambda i: (i, 0)), pl.BlockSpec((tr, LANES), lambda i: (i, 0)),
                   pl.BlockSpec((tr, W3), lambda i: (i, 0))],
        out_shape=[jax.ShapeDtypeStruct((Lp, W3), F32), jax.ShapeDtypeStruct((Lp, LANES), F32),
                   jax.ShapeDtypeStruct((Lp, W3), BF16)],
        name="gdn_pre")(proj_m, proj_m, proj_s, conv_w, gparams)


def _gdn_pre_bwd(proj_m, conv_out, proj_s, conv_w, gparams, dq, dk, dv, dgs, pad):
    Lp = proj_m.shape[0]
    tr = _tile(Lp, 192, 64)
    W3 = 3 * D_MODEL
    te = tr + 8

    def body(main_ref, c_ref, cn_ref, s_ref, w_ref, gp_ref,
             dq_ref, dqn_ref, dk_ref, dkn_ref, dv_ref, dvn_ref, dgs_ref,
             da_ref, ds_ref, dw_ref, dgp_ref):
        i = pl.program_id(0)
        w = w_ref[...]
        c = jnp.concatenate([c_ref[...].astype(F32), cn_ref[...].astype(F32)[0:8]], axis=0)
        sg = _sig(c)
        s = c * sg
        rowe = i * tr + lax.broadcasted_iota(jnp.int32, (te, 1), 0)
        live = (rowe >= pad) & (rowe < Lp)
        dqe = jnp.concatenate([dq_ref[...], dqn_ref[...]], axis=0)
        dke = jnp.concatenate([dk_ref[...], dkn_ref[...]], axis=0)
        dve = jnp.concatenate([dv_ref[...], dvn_ref[...]], axis=0)
        scale = GDN_D ** -0.5
        parts = []
        for j in range(2 * GDN_H):
            seg = s[:, j * GDN_D:(j + 1) * GDN_D]
            r = lax.rsqrt(_rowsum(seg * seg) + EPS)
            xh = seg * r
            if j < GDN_H:
                dxh = dqe[:, j * GDN_D:(j + 1) * GDN_D] * scale
            else:
                dxh = dke[:, (j - GDN_H) * GDN_D:(j - GDN_H + 1) * GDN_D]
            parts.append(r * (dxh - xh * _rowsum(dxh * xh)))
        parts.append(dve)
        dsv = jnp.concatenate(parts, axis=1)
        dc = jnp.where(live, dsv * (sg * (1.0 + c * (1.0 - sg))), 0.0)
        dcs = _shifted(dc, range(GDN_CONV - 1, -1, -1))
        da_ref[...] = _taps(dcs, w, tr).astype(BF16)
        pm = main_ref[...]
        rows = [jnp.sum(dcs[kk][0:tr, :] * pm, axis=0, keepdims=True) for kk in range(GDN_CONV)]
        dwp = jnp.concatenate(rows + [jnp.zeros((8 - GDN_CONV, W3), F32)], axis=0)

        sm = s_ref[...]
        gp = gp_ref[...]
        lane = lax.broadcasted_iota(jnp.int32, sm.shape, 1)
        rowm = i * tr + lax.broadcasted_iota(jnp.int32, (tr, 1), 0)
        dgv = jnp.where(rowm >= pad, dgs_ref[...], 0.0)
        dlg = jnp.where(lane < GDN_H, dgv, 0.0)
        dbt = jnp.where((lane >= GDN_H) & (lane < 2 * GDN_H), dgv, 0.0)
        z = sm + gp[1:2, :]
        softplus = jnp.maximum(z, 0.0) + jnp.log(1.0 + jnp.exp(-jnp.abs(z)))
        ea = jnp.exp(gp[0:1, :])
        dz = dlg * (-ea) * _sig(z)
        dal = dlg * (-ea) * softplus
        bt = _sig(sm)
        dgb = dbt * bt * (1.0 - bt)
        ds_ref[...] = (dz + dgb).astype(BF16)
        gpp = jnp.concatenate([jnp.sum(dal, axis=0, keepdims=True), jnp.sum(dz, axis=0, keepdims=True),
                               jnp.zeros((6, LANES), F32)], axis=0)

        @pl.when(i == 0)
        def _():
            dw_ref[...] = dwp
            dgp_ref[...] = gpp

        @pl.when(i > 0)
        def _():
            dw_ref[...] += dwp
            dgp_ref[...] += gpp

    m3 = pl.BlockSpec((tr, W3), lambda i: (i, 0))
    m1 = pl.BlockSpec((tr, D_MODEL), lambda i: (i, 0))
    n1 = _halo_next(tr, D_MODEL, Lp)
    return pl.pallas_call(
        body, grid=(Lp // tr,),
        in_specs=[m3, m3, _halo_next(tr, W3, Lp, rows=16), pl.BlockSpec((tr, LANES), lambda i: (i, 0)),
                  pl.BlockSpec((GDN_CONV, W3), lambda i: (0, 0)), pl.BlockSpec((8, LANES), lambda i: (0, 0)),
                  m1, n1, m1, n1, m1, n1, pl.BlockSpec((tr, LANES), lambda i: (i, 0))],
        out_specs=[m3, pl.BlockSpec((tr, LANES), lambda i: (i, 0)),
                   pl.BlockSpec((8, W3), lambda i: (0, 0)), pl.BlockSpec((8, LANES), lambda i: (0, 0))],
        out_shape=[jax.ShapeDtypeStruct((Lp, W3), BF16), jax.ShapeDtypeStruct((Lp, LANES), BF16),
                   jax.ShapeDtypeStruct((8, W3), F32), jax.ShapeDtypeStruct((8, LANES), F32)],
        name="gdn_pre_bwd")(proj_m, conv_out, conv_out, proj_s, conv_w, gparams, dq, dq, dk, dk, dv, dv, dgs)


def _gdn_gates(gs):
    ri = lax.broadcasted_iota(jnp.int32, (CHUNK, CHUNK), 0)
    ci = lax.broadcasted_iota(jnp.int32, (CHUNK, CHUNK), 1)
    tril = ri >= ci
    strict = ri > ci
    gall = _dx(tril.astype(F32), gs)
    lane8 = lax.broadcasted_iota(jnp.int32, (8, LANES), 1)
    sub8 = lax.broadcasted_iota(jnp.int32, (8, LANES), 0)
    grow = _dxnt((lane8 == sub8).astype(F32), gall)
    return gall, grow, tril, strict


def _gdn_decay(gall, grow, tril, h):
    g = gall[:, h:h + 1]
    return g, jnp.where(tril, jnp.exp(jnp.where(tril, g - grow[h:h + 1, :], 0.0)), 0.0)


def _group(N):
    return 3 if N % 3 == 0 else (2 if N % 2 == 0 else 1)


def _gdn_chunk_specs(N, rev):
    G = _group(N)
    nb = N // G
    cn = (lambda n: nb - 1 - n) if rev else (lambda n: n)
    col = lambda j: pl.BlockSpec((G * CHUNK, D_MODEL), lambda n: (cn(n), j))
    gate = pl.BlockSpec((G * CHUNK, LANES), lambda n: (cn(n), 0))
    st = lambda a, b: pl.BlockSpec((GDN_H, G, a, b), lambda n: (0, cn(n), 0, 0))
    return G, nb, col, gate, st


def _gdn_chunk_fwd(qkv, gsm):
    Lp = qkv.shape[0]
    N = Lp // CHUNK
    G, nb, col, gate, st = _gdn_chunk_specs(N, False)

    def body(q_ref, k_ref, v_ref, gs_ref, o_ref, sin_ref, t_ref, S):
        n = pl.program_id(0)

        @pl.when(n == 0)
        def _():
            S[...] = jnp.zeros_like(S)

        ri = lax.broadcasted_iota(jnp.int32, (CHUNK, CHUNK), 0)
        ci = lax.broadcasted_iota(jnp.int32, (CHUNK, CHUNK), 1)
        eye = (ri == ci).astype(F32)
        heads = range(GDN_H)
        sls = [slice(h * GDN_D, (h + 1) * GDN_D) for h in heads]
        rows = [slice(c * CHUNK, (c + 1) * CHUNK) for c in range(G)]
        pairs = [(c, h) for c in range(G) for h in heads]
        P = lambda f: {p: f(*p) for p in pairs}
        gs = [gs_ref[rows[c], :] for c in range(G)]
        gates = [_gdn_gates(gs[c]) for c in range(G)]
        tril, strict = gates[0][2], gates[0][3]
        q = P(lambda c, h: q_ref[rows[c], sls[h]])
        k = P(lambda c, h: k_ref[rows[c], sls[h]])
        v = P(lambda c, h: v_ref[rows[c], sls[h]])
        beta = P(lambda c, h: gs[c][:, GDN_H + h:GDN_H + h + 1])
        gg = P(lambda c, h: _gdn_decay(gates[c][0], gates[c][1], tril, h))
        g = {p: x[0] for p, x in gg.items()}
        gam = {p: x[1] for p, x in gg.items()}
        eg = P(lambda c, h: jnp.exp(g[c, h]))
        gl = P(lambda c, h: g[c, h][CHUNK - 1:CHUNK, :])
        kb = P(lambda c, h: k[c, h] * beta[c, h])
        pw = P(lambda c, h: -jnp.where(strict, _dnt(kb[c, h], k[c, h]) * gam[c, h], 0.0))
        p = P(lambda c, h: _dnt(q[c, h], k[c, h]) * gam[c, h])
        t = P(lambda c, h: eye + pw[c, h])
        for it in range(5):
            mm = _d3g if it < 2 else (lambda a, b, dims: _d(a, b))
            pw = P(lambda c, h: mm(pw[c, h], pw[c, h], _NN))
            t = P(lambda c, h: t[c, h] + mm(t[c, h], pw[c, h], _NN))
        u = P(lambda c, h: _d(t[c, h], v[c, h] * beta[c, h]))
        w = P(lambda c, h: _d(t[c, h], kb[c, h] * eg[c, h]))
        qg = P(lambda c, h: q[c, h] * eg[c, h])
        kd = P(lambda c, h: k[c, h] * jnp.exp(gl[c, h] - g[c, h]))
        egl = P(lambda c, h: jnp.exp(gl[c, h]))
        for c in range(G):
            for h in heads:
                t_ref[h, c] = t[c, h]
        cur = [S[h] for h in heads]
        for c in range(G):
            vnew = [u[c, h] - _d(w[c, h], cur[h]) for h in heads]
            for h in heads:
                o_ref[rows[c], sls[h]] = _d(qg[c, h], cur[h]) + _d(p[c, h], vnew[h])
                sin_ref[h, c] = cur[h]
            cur = [cur[h] * egl[c, h] + _dtn(kd[c, h], vnew[h]) for h in heads]
        for h in heads:
            S[h] = cur[h]

    return pl.pallas_call(
        body, grid=(nb,),
        in_specs=[col(0), col(1), col(2), gate],
        out_specs=[col(0), st(GDN_D, GDN_D), st(CHUNK, CHUNK)],
        out_shape=[jax.ShapeDtypeStruct((Lp, D_MODEL), F32), jax.ShapeDtypeStruct((GDN_H, N, GDN_D, GDN_D), F32),
                   jax.ShapeDtypeStruct((GDN_H, N, CHUNK, CHUNK), F32)],
        scratch_shapes=[pltpu.VMEM((GDN_H, GDN_D, GDN_D), F32)],
        name="gdn_chunk_fwd")(qkv, qkv, qkv, gsm)


def _gdn_chunk_bwd(qkv, gsm, do, s_in, t_in):
    Lp = qkv.shape[0]
    N = Lp // CHUNK
    G, nb, col, gate, st = _gdn_chunk_specs(N, True)

    def body(q_ref, k_ref, v_ref, gs_ref, do_ref, sin_ref, t_ref, dq_ref, dk_ref, dv_ref, dgs_ref, dS):
        n = pl.program_id(0)

        @pl.when(n == 0)
        def _():
            dS[...] = jnp.zeros_like(dS)

        lane = lax.broadcasted_iota(jnp.int32, (CHUNK, LANES), 1)
        rcol = lax.broadcasted_iota(jnp.int32, (CHUNK, 1), 0)
        ri = lax.broadcasted_iota(jnp.int32, (CHUNK, CHUNK), 0)
        ci = lax.broadcasted_iota(jnp.int32, (CHUNK, CHUNK), 1)
        ones = jnp.ones((CHUNK, LANES), F32)
        heads = range(GDN_H)
        sls = [slice(h * GDN_D, (h + 1) * GDN_D) for h in heads]
        rows = [slice(c * CHUNK, (c + 1) * CHUNK) for c in range(G)]
        pairs = [(c, h) for c in range(G) for h in heads]
        P = lambda f: {p: f(*p) for p in pairs}
        gs = [gs_ref[rows[c], :] for c in range(G)]
        gates = [_gdn_gates(gs[c]) for c in range(G)]
        tril, strict = gates[0][2], gates[0][3]
        q = P(lambda c, h: q_ref[rows[c], sls[h]])
        k = P(lambda c, h: k_ref[rows[c], sls[h]])
        v = P(lambda c, h: v_ref[rows[c], sls[h]])
        dov = P(lambda c, h: do_ref[rows[c], sls[h]])
        s0 = P(lambda c, h: sin_ref[h, c])
        t = P(lambda c, h: t_ref[h, c])
        beta = P(lambda c, h: gs[c][:, GDN_H + h:GDN_H + h + 1])
        gg = P(lambda c, h: _gdn_decay(gates[c][0], gates[c][1], tril, h))
        g = {p: x[0] for p, x in gg.items()}
        gam = {p: x[1] for p, x in gg.items()}
        eg = P(lambda c, h: jnp.exp(g[c, h]))
        egl = P(lambda c, h: jnp.exp(g[c, h][CHUNK - 1:CHUNK, :]))
        e = P(lambda c, h: jnp.exp(g[c, h][CHUNK - 1:CHUNK, :] - g[c, h]))
        kb = P(lambda c, h: k[c, h] * beta[c, h])
        kbg = P(lambda c, h: kb[c, h] * eg[c, h])
        vb = P(lambda c, h: v[c, h] * beta[c, h])
        qg = P(lambda c, h: q[c, h] * eg[c, h])
        kd = P(lambda c, h: k[c, h] * e[c, h])
        m = P(lambda c, h: jnp.where(strict, _dnt(kb[c, h], k[c, h]) * gam[c, h], 0.0))
        u = P(lambda c, h: _d(t[c, h], vb[c, h]))
        w = P(lambda c, h: _d(t[c, h], kbg[c, h]))
        p = P(lambda c, h: _dnt(q[c, h], k[c, h]) * gam[c, h])
        dqg = P(lambda c, h: _dnt(dov[c, h], s0[c, h]))
        qgdo = P(lambda c, h: _dtn(qg[c, h], dov[c, h]))
        ptdo = P(lambda c, h: _dtn(p[c, h], dov[c, h]))
        vnew = P(lambda c, h: u[c, h] - _d(w[c, h], s0[c, h]))
        dp = P(lambda c, h: jnp.where(tril, _dnt(dov[c, h], vnew[c, h]), 0.0))
        cur = [dS[h] for h in heads]
        dvnew, dkd, sds = {}, {}, {}
        for c in reversed(range(G)):
            for h in heads:
                dvnew[c, h] = ptdo[c, h] + _d(kd[c, h], cur[h])
                dkd[c, h] = _dnt(vnew[c, h], cur[h])
                sds[c, h] = _allsum(s0[c, h] * cur[h])
            cur = [qgdo[c, h] + egl[c, h] * cur[h] - _dtn(w[c, h], dvnew[c, h]) for h in heads]
        for h in heads:
            dS[h] = cur[h]
        dw = P(lambda c, h: -_dnt(dvnew[c, h], s0[c, h]))
        dvb = P(lambda c, h: _dtn(t[c, h], dvnew[c, h]))
        dkbg = P(lambda c, h: _dtn(t[c, h], dw[c, h]))
        dt = P(lambda c, h: _dnt(dvnew[c, h], vb[c, h]) + _dnt(dw[c, h], kbg[c, h]))
        x1 = P(lambda c, h: _dtn(t[c, h], dt[c, h]))
        dm = P(lambda c, h: jnp.where(strict, -_dnt(x1[c, h], t[c, h]), 0.0))
        dkk = P(lambda c, h: dm[c, h] * gam[c, h])
        dqk = P(lambda c, h: dp[c, h] * gam[c, h])
        dkb = P(lambda c, h: _d(dkk[c, h], k[c, h]) + eg[c, h] * dkbg[c, h])
        em = P(lambda c, h: dm[c, h] * m[c, h] + dp[c, h] * p[c, h])
        colsum = P(lambda c, h: _d2x(em[c, h], ones, _TN)[:, 0:1])
        for c, h in pairs:
            dk_ref[rows[c], sls[h]] = (_dtn(dkk[c, h], kb[c, h]) + _dtn(dqk[c, h], q[c, h]) + dkd[c, h] * e[c, h]
                                       + beta[c, h] * dkb[c, h])
            dq_ref[rows[c], sls[h]] = _d(dqk[c, h], k[c, h]) + dqg[c, h] * eg[c, h]
            dv_ref[rows[c], sls[h]] = beta[c, h] * dvb[c, h]
        for c in range(G):
            dg_all = jnp.zeros((CHUNK, LANES), F32)
            dbeta_all = jnp.zeros((CHUNK, LANES), F32)
            for h in heads:
                dbeta = _rowsum(k[c, h] * dkb[c, h]) + _rowsum(v[c, h] * dvb[c, h])
                z = _rowsum(kd[c, h] * dkd[c, h])
                dg = (_rowsum(em[c, h]) - colsum[c, h] + _rowsum(qg[c, h] * dqg[c, h]) + _rowsum(kbg[c, h] * dkbg[c, h])
                      - z)
                extra = _allsum(z) + egl[c, h] * sds[c, h]
                dg = dg + jnp.where(rcol == CHUNK - 1, extra, 0.0)
                dg_all = dg_all + jnp.where(lane == h, dg, 0.0)
                dbeta_all = dbeta_all + jnp.where(lane == GDN_H + h, dbeta, 0.0)
            dgs_ref[rows[c], :] = _dx((ci >= ri).astype(F32), dg_all) + dbeta_all

    return pl.pallas_call(
        body, grid=(nb,),
        in_specs=[col(0), col(1), col(2), gate, col(0), st(GDN_D, GDN_D), st(CHUNK, CHUNK)],
        out_specs=[col(0), col(0), col(0), gate],
        out_shape=[jax.ShapeDtypeStruct((Lp, D_MODEL), F32)] * 3 + [jax.ShapeDtypeStruct((Lp, LANES), F32)],
        scratch_shapes=[pltpu.VMEM((GDN_H, GDN_D, GDN_D), F32)],
        name="gdn_chunk_bwd")(qkv, qkv, qkv, gsm, do, s_in, t_in)


def _rot(x, c, s):
    half = RET_D // 2
    x1 = x[:, :half]
    x2 = x[:, half:]
    return jnp.concatenate([x1 * c - x2 * s, x2 * c + x1 * s], axis=1)


def _rot_bwd(d, c, s):
    half = RET_D // 2
    d1 = d[:, :half]
    d2 = d[:, half:]
    return jnp.concatenate([d1 * c + d2 * s, d2 * c - d1 * s], axis=1)


def _ret_tables():
    hh = jnp.arange(RET_H, dtype=F32)
    lg = jnp.log(1.0 - 2.0 ** (-5.0 - hh))
    idx = jnp.arange(CHUNK, dtype=F32)
    tril = jnp.asarray(np.tril(np.ones((CHUNK, CHUNK), dtype=bool)))
    dmask = jnp.where(tril, jnp.exp((idx[:, None] - idx[None, :]) * lg[:, None, None]), 0.0)
    qdec = jnp.exp((idx[None, :] + 1.0) * lg[:, None])
    kdec = jnp.exp((CHUNK - 1.0 - idx[None, :]) * lg[:, None])
    gch = jnp.exp(CHUNK * lg)
    qdec = jnp.broadcast_to(qdec[:, :, None], (RET_H, CHUNK, RET_D))
    kdec = jnp.broadcast_to(kdec[:, :, None], (RET_H, CHUNK, RET_D))
    gch = jnp.broadcast_to(gch[:, None, None], (RET_H, 8, LANES))
    return dmask, qdec, kdec, gch


def _ret_specs(N, rev):
    G = _group(N)
    nb = N // G
    cn = (lambda n: nb - 1 - n) if rev else (lambda n: n)
    col = lambda j: pl.BlockSpec((G * CHUNK, D_MODEL), lambda n: (cn(n), j))
    tab = lambda a, b: pl.BlockSpec((RET_H, a, b), lambda n: (0, 0, 0))
    rope = pl.BlockSpec((G * CHUNK, LANES), lambda n: (cn(n), 0))
    st = pl.BlockSpec((RET_H, G, RET_D, RET_D), lambda n: (0, cn(n), 0, 0))
    return G, nb, col, tab, rope, st


def _ret_chunk_fwd(proj_m, cos, sin, tables):
    Lp = proj_m.shape[0]
    N = Lp // CHUNK
    dmask, qdec, kdec, gch = tables
    G, nb, col, tab, rope, st = _ret_specs(N, False)

    def body(q_ref, k_ref, v_ref, c_ref, s_ref, dm_ref, qd_ref, kd_ref, g_ref, o_ref, sin_ref, S):
        n = pl.program_id(0)

        @pl.when(n == 0)
        def _():
            S[...] = jnp.zeros_like(S)

        heads = range(RET_H)
        sls = [slice(h * RET_D, (h + 1) * RET_D) for h in heads]
        rows = [slice(c * CHUNK, (c + 1) * CHUNK) for c in range(G)]
        pairs = [(c, h) for c in range(G) for h in heads]
        P = lambda f: {p: f(*p) for p in pairs}
        qr = P(lambda c, h: _rot(q_ref[rows[c], sls[h]], c_ref[rows[c], :], s_ref[rows[c], :]))
        ks = P(lambda c, h: _rot(k_ref[rows[c], sls[h]], c_ref[rows[c], :], s_ref[rows[c], :]) * (RET_D ** -0.5))
        v = P(lambda c, h: v_ref[rows[c], sls[h]])
        a = P(lambda c, h: _dnt(qr[c, h], ks[c, h]) * dm_ref[h])
        av = P(lambda c, h: _d(a[c, h], v[c, h]))
        kv = P(lambda c, h: _dtn(ks[c, h] * kd_ref[h], v[c, h]))
        qd = P(lambda c, h: qr[c, h] * qd_ref[h])
        cur = [S[h] for h in heads]
        for c in range(G):
            for h in heads:
                o_ref[rows[c], sls[h]] = av[c, h] + _d(qd[c, h], cur[h])
                sin_ref[h, c] = cur[h].astype(BF16)
            cur = [cur[h] * g_ref[h, 0:1, 0:1] + kv[c, h] for h in heads]
        for h in heads:
            S[h] = cur[h]

    return pl.pallas_call(
        body, grid=(nb,),
        in_specs=[col(3), col(4), col(5), rope, rope,
                  tab(CHUNK, CHUNK), tab(CHUNK, RET_D), tab(CHUNK, RET_D), tab(8, LANES)],
        out_specs=[col(0), st],
        out_shape=[jax.ShapeDtypeStruct((Lp, D_MODEL), F32), jax.ShapeDtypeStruct((RET_H, N, RET_D, RET_D), BF16)],
        scratch_shapes=[pltpu.VMEM((RET_H, RET_D, RET_D), F32)],
        name="ret_chunk_fwd")(proj_m, proj_m, proj_m, cos, sin, dmask, qdec, kdec, gch)


def _ret_chunk_bwd(proj_m, cos, sin, tables, do, s_in):
    Lp = proj_m.shape[0]
    N = Lp // CHUNK
    dmask, qdec, kdec, gch = tables
    G, nb, col, tab, rope, st = _ret_specs(N, True)

    def body(q_ref, k_ref, v_ref, c_ref, s_ref, dm_ref, qd_ref, kd_ref, g_ref, do_ref, sin_ref,
             d_ref, dS):
        n = pl.program_id(0)

        @pl.when(n == 0)
        def _():
            dS[...] = jnp.zeros_like(dS)

        kscale = RET_D ** -0.5
        heads = range(RET_H)
        sls = [slice(h * RET_D, (h + 1) * RET_D) for h in heads]
        rows = [slice(c * CHUNK, (c + 1) * CHUNK) for c in range(G)]
        pairs = [(c, h) for c in range(G) for h in heads]
        P = lambda f: {p: f(*p) for p in pairs}
        cs = [(c_ref[rows[c], :], s_ref[rows[c], :]) for c in range(G)]
        osl = lambda part, h: slice(part * D_MODEL + h * RET_D, part * D_MODEL + (h + 1) * RET_D)
        qr = P(lambda c, h: _rot(q_ref[rows[c], sls[h]], *cs[c]))
        ks = P(lambda c, h: _rot(k_ref[rows[c], sls[h]], *cs[c]) * kscale)
        v = P(lambda c, h: v_ref[rows[c], sls[h]])
        dov = P(lambda c, h: do_ref[rows[c], sls[h]])
        ad = P(lambda c, h: _dnt(qr[c, h], ks[c, h]) * dm_ref[h])
        da = P(lambda c, h: _dnt(dov[c, h], v[c, h]) * dm_ref[h])
        dos = P(lambda c, h: _dnt(dov[c, h], sin_ref[h, c]) * qd_ref[h])
        qdo = P(lambda c, h: _dtn(qr[c, h] * qd_ref[h], dov[c, h]))
        adv = P(lambda c, h: _dtn(ad[c, h], dov[c, h]))
        dqr = P(lambda c, h: _d(da[c, h], ks[c, h]) + dos[c, h])
        daq = P(lambda c, h: _dtn(da[c, h], qr[c, h]))
        kk = P(lambda c, h: ks[c, h] * kd_ref[h])
        cur = [dS[h] for h in heads]
        for c in reversed(range(G)):
            for h in heads:
                d_ref[rows[c], osl(2, h)] = (adv[c, h] + _d(kk[c, h], cur[h])).astype(BF16)
                d_ref[rows[c], osl(0, h)] = _rot_bwd(dqr[c, h], *cs[c]).astype(BF16)
                dks = daq[c, h] + _dnt(v[c, h], cur[h]) * kd_ref[h]
                d_ref[rows[c], osl(1, h)] = _rot_bwd(dks * kscale, *cs[c]).astype(BF16)
            cur = [cur[h] * g_ref[h, 0:1, 0:1] + qdo[c, h] for h in heads]
        for h in heads:
            dS[h] = cur[h]

    return pl.pallas_call(
        body, grid=(nb,),
        in_specs=[col(3), col(4), col(5), rope, rope,
                  tab(CHUNK, CHUNK), tab(CHUNK, RET_D), tab(CHUNK, RET_D), tab(8, LANES), col(0), st],
        out_specs=pl.BlockSpec((G * CHUNK, 3 * D_MODEL), lambda n: (nb - 1 - n, 0)),
        out_shape=jax.ShapeDtypeStruct((Lp, 3 * D_MODEL), BF16),
        scratch_shapes=[pltpu.VMEM((RET_H, RET_D, RET_D), F32)],
        name="ret_chunk_bwd")(proj_m, proj_m, proj_m, cos, sin, dmask, qdec, kdec, gch, do, s_in)


def _merge_specs(tr):
    col = lambda j: pl.BlockSpec((tr, D_MODEL), lambda i: (i, j))
    return col


def _merge_fwd(o_a, o_b, proj_m, gnorm, out_proj=None):
    Lp = o_a.shape[0]
    tr = _tile(Lp, 192 if out_proj is None else 352, 16)
    if out_proj is not None:
        w_out, res, g2 = out_proj
        r_args, r_specs = _rows_operands(res, tr)

    def body(oa_ref, ob_ref, gz_ref, rg_ref, ga_ref, gb_ref, gn_ref, *rest):
        y_ref = rest[0] if out_proj is None else rest[-3]
        gn = gn_ref[...]
        oa = oa_ref[...]
        ob = ob_ref[...]
        gz = gz_ref[...]
        ya = []
        for j in range(GDN_H):
            seg = oa[:, j * GDN_D:(j + 1) * GDN_D]
            r = lax.rsqrt(jnp.mean(seg * seg, axis=-1, keepdims=True) + EPS)
            ya.append(seg * r * gn)
        ya = jnp.concatenate(ya, axis=1) * (gz * _sig(gz))
        yb = []
        for j in range(RET_H):
            seg = ob[:, j * RET_D:(j + 1) * RET_D]
            r = lax.rsqrt(jnp.mean(seg * seg, axis=-1, keepdims=True) + EPS)
            yb.append(seg * r)
        rg = rg_ref[...]
        yb = jnp.concatenate(yb, axis=1) * (rg * _sig(rg))
        yv = (_sig(ga_ref[...]) * ya + _sig(gb_ref[...]) * yb).astype(BF16)
        y_ref[...] = yv
        if out_proj is not None:
            wo_ref, g2_ref = rest[0], rest[1]
            h1_ref, hn2_ref = rest[-2], rest[-1]
            h1 = (jnp.dot(yv, wo_ref[...], preferred_element_type=F32)
                  + _rows_tile(res, rest[2:2 + len(r_args)], pl.program_id(0), tr))
            r = lax.rsqrt(jnp.mean(h1 * h1, axis=-1, keepdims=True) + EPS)
            h1_ref[...] = h1
            hn2_ref[...] = (h1 * r * g2_ref[...]).astype(BF16)

    col = _merge_specs(tr)
    in_specs = [col(0), col(0), col(6), col(7), col(8), col(9), pl.BlockSpec((1, GDN_D), lambda i: (0, 0))]
    args = [o_a, o_b, proj_m, proj_m, proj_m, proj_m, gnorm]
    if out_proj is None:
        return pl.pallas_call(body, grid=(Lp // tr,), in_specs=in_specs, out_specs=col(0),
                              out_shape=jax.ShapeDtypeStruct((Lp, D_MODEL), BF16), name="merge_fwd")(*args)
    in_specs += [pl.BlockSpec((D_MODEL, D_MODEL), lambda i: (0, 0), pipeline_mode=pl.Buffered(1)),
                 pl.BlockSpec((1, D_MODEL), lambda i: (0, 0))] + r_specs
    return pl.pallas_call(
        body, grid=(Lp // tr,), in_specs=in_specs, out_specs=[col(0), col(0), col(0)],
        out_shape=[jax.ShapeDtypeStruct((Lp, D_MODEL), BF16), jax.ShapeDtypeStruct((Lp, D_MODEL), F32),
                   jax.ShapeDtypeStruct((Lp, D_MODEL), BF16)],
        name="merge_out_proj_rms2")(*args, w_out, g2, *r_args)


def _merge_bwd(dh1b, w_out, o_a, o_b, proj_m, gnorm):
    Lp = o_a.shape[0]
    tr = _tile(Lp, 192, 16)

    def body(d_ref, wo_ref, oa_ref, ob_ref, gz_ref, rg_ref, ga_ref, gb_ref, gn_ref, dc_ref, doa_ref, dob_ref, dgn_ref):
        i = pl.program_id(0)
        gn = gn_ref[...]
        dyv = lax.dot_general(d_ref[...], wo_ref[...], _NT, preferred_element_type=F32)
        oa = oa_ref[...]
        ob = ob_ref[...]
        gz = gz_ref[...]
        rg = rg_ref[...]
        sa = _sig(ga_ref[...])
        sb = _sig(gb_ref[...])
        dya = dyv * sa
        dyb = dyv * sb
        sgz = _sig(gz)
        szz = gz * sgz
        dgn = jnp.zeros((1, GDN_D), F32)
        ya = []
        dgz = []
        for j in range(GDN_H):
            sl = slice(j * GDN_D, (j + 1) * GDN_D)
            seg = oa[:, sl]
            r = lax.rsqrt(jnp.mean(seg * seg, axis=-1, keepdims=True) + EPS)
            xh = seg * r
            oan = xh * gn
            ya.append(oan * szz[:, sl])
            dgz.append(dya[:, sl] * oan * (sgz[:, sl] * (1.0 + gz[:, sl] * (1.0 - sgz[:, sl]))))
            doan = dya[:, sl] * szz[:, sl]
            dgn = dgn + jnp.sum(doan * xh, axis=0, keepdims=True)
            dxh = doan * gn
            doa_ref[:, sl] = r * (dxh - xh * jnp.mean(dxh * xh, axis=-1, keepdims=True))
        ya = jnp.concatenate(ya, axis=1)
        srg = _sig(rg)
        srr = rg * srg
        yb = []
        drg = []
        for j in range(RET_H):
            sl = slice(j * RET_D, (j + 1) * RET_D)
            seg = ob[:, sl]
            r = lax.rsqrt(jnp.mean(seg * seg, axis=-1, keepdims=True) + EPS)
            xh = seg * r
            yb.append(xh * srr[:, sl])
            drg.append(dyb[:, sl] * xh * (srg[:, sl] * (1.0 + rg[:, sl] * (1.0 - srg[:, sl]))))
            dxh = dyb[:, sl] * srr[:, sl]
            dob_ref[:, sl] = r * (dxh - xh * jnp.mean(dxh * xh, axis=-1, keepdims=True))
        yb = jnp.concatenate(yb, axis=1)
        dc_ref[:, 0:D_MODEL] = jnp.concatenate(dgz, axis=1).astype(BF16)
        dc_ref[:, D_MODEL:2 * D_MODEL] = jnp.concatenate(drg, axis=1).astype(BF16)
        dc_ref[:, 2 * D_MODEL:3 * D_MODEL] = (dyv * ya * sa * (1.0 - sa)).astype(BF16)
        dc_ref[:, 3 * D_MODEL:] = (dyv * yb * sb * (1.0 - sb)).astype(BF16)

        @pl.when(i == 0)
        def _():
            dgn_ref[...] = dgn

        @pl.when(i > 0)
        def _():
            dgn_ref[...] += dgn

    col = _merge_specs(tr)
    return pl.pallas_call(
        body, grid=(Lp // tr,),
        in_specs=[col(0), pl.BlockSpec((D_MODEL, D_MODEL), lambda i: (0, 0), pipeline_mode=pl.Buffered(1)),
                  col(0), col(0), col(6), col(7), col(8), col(9), pl.BlockSpec((1, GDN_D), lambda i: (0, 0))],
        out_specs=[pl.BlockSpec((tr, 4 * D_MODEL), lambda i: (i, 0)), col(0), col(0),
                   pl.BlockSpec((1, GDN_D), lambda i: (0, 0))],
        out_shape=[jax.ShapeDtypeStruct((Lp, 4 * D_MODEL), BF16), jax.ShapeDtypeStruct((Lp, D_MODEL), F32),
                   jax.ShapeDtypeStruct((Lp, D_MODEL), F32), jax.ShapeDtypeStruct((1, GDN_D), F32)],
        name="merge_bwd")(dh1b, w_out, o_a, o_b, proj_m, proj_m, proj_m, proj_m, gnorm)


def _ffn_act(up, conv_w, conv_b):
    Lp = up.shape[0]
    tr = _tile(Lp, 192, 16)
    W2 = 2 * D_FF

    def body(main_ref, prev_ref, w_ref, b_ref, act_ref, u_ref):
        i = pl.program_id(0)
        prev = jnp.where(i > 0, prev_ref[...], 0.0)
        ext = jnp.concatenate([prev, main_ref[...]], axis=0)
        u = _taps(_shifted(ext, range(8 - (FFN_CONV - 1), 9)), w_ref[...], tr, b_ref[...])
        a = u[:, :D_FF]
        act_ref[...] = (a * _sig(a) * u[:, D_FF:]).astype(BF16)
        u_ref[...] = u.astype(BF16)

    return pl.pallas_call(
        body, grid=(Lp // tr,),
        in_specs=[pl.BlockSpec((tr, W2), lambda i: (i, 0)), _halo_prev(tr, W2),
                  pl.BlockSpec((FFN_CONV, W2), lambda i: (0, 0)), pl.BlockSpec((1, W2), lambda i: (0, 0))],
        out_specs=[pl.BlockSpec((tr, D_FF), lambda i: (i, 0)), pl.BlockSpec((tr, W2), lambda i: (i, 0))],
        out_shape=[jax.ShapeDtypeStruct((Lp, D_FF), BF16), jax.ShapeDtypeStruct((Lp, W2), BF16)],
        name="ffn_act")(up, up, conv_w, conv_b)


def _ffn_act_bwd(up, u, dact, conv_w):
    Lp = up.shape[0]
    tr = _tile(Lp, 192, 16)
    W2 = 2 * D_FF
    te = tr + 8

    def body(up_ref, u_ref, un_ref, da_ref, dan_ref, w_ref, dup_ref, acc_ref):
        i = pl.program_id(0)
        w = w_ref[...]
        ue = jnp.concatenate([u_ref[...].astype(F32), un_ref[...].astype(F32)[0:8]], axis=0)
        a = ue[:, :D_FF]
        b = ue[:, D_FF:]
        rowe = i * tr + lax.broadcasted_iota(jnp.int32, (te, 1), 0)
        dae = jnp.where(rowe < Lp, jnp.concatenate([da_ref[...], dan_ref[...]], axis=0), 0.0)
        sg = _sig(a)
        du = jnp.concatenate([dae * b * (sg * (1.0 + a * (1.0 - sg))), dae * (a * sg)], axis=1)
        dus = _shifted(du, range(FFN_CONV - 1, -1, -1))
        dup_ref[...] = _taps(dus, w, tr).astype(BF16)
        upm = up_ref[...]
        rows = [jnp.sum(dus[kk][0:tr, :] * upm, axis=0, keepdims=True) for kk in range(FFN_CONV)]
        rows.append(jnp.sum(du[0:tr, :], axis=0, keepdims=True))
        part = jnp.concatenate(rows + [jnp.zeros((8 - len(rows), W2), F32)], axis=0)

        @pl.when(i == 0)
        def _():
            acc_ref[...] = part

        @pl.when(i > 0)
        def _():
            acc_ref[...] += part

    return pl.pallas_call(
        body, grid=(Lp // tr,),
        in_specs=[pl.BlockSpec((tr, W2), lambda i: (i, 0)), pl.BlockSpec((tr, W2), lambda i: (i, 0)),
                  _halo_next(tr, W2, Lp, rows=16), pl.BlockSpec((tr, D_FF), lambda i: (i, 0)), _halo_next(tr, D_FF, Lp),
                  pl.BlockSpec((FFN_CONV, W2), lambda i: (0, 0))],
        out_specs=[pl.BlockSpec((tr, W2), lambda i: (i, 0)), pl.BlockSpec((8, W2), lambda i: (0, 0))],
        out_shape=[jax.ShapeDtypeStruct((Lp, W2), BF16), jax.ShapeDtypeStruct((8, W2), F32)],
        name="ffn_act_bwd")(up, u, u, dact, dact, conv_w)


def _proj_rows(j):
    shift = (jnp.where((j >= 3) & (j < 6), _O_RQ - 3 * D_MODEL, 0) + jnp.where(j == 6, _O_GZ - 6 * D_MODEL, 0)
             + jnp.where(j >= 7, _O_RG - 7 * D_MODEL, 0))
    return j * D_MODEL + shift


def _local_step(hpad, tgt, pad, wt, first_weights=None, late_weights=None, on_ffn_out_grads=None,
                on_w_in_grads=None):
    Lp = hpad.shape[0]
    first = pad + N_META
    pos = jnp.arange(Lp, dtype=F32) - float(pad)
    half = RET_D // 2
    inv = 1.0 / (ROPE_BASE ** (jnp.arange(half, dtype=F32) / half))
    ang = pos[:, None] * inv[None, :]
    cos, sin = jnp.cos(ang), jnp.sin(ang)
    tables = _ret_tables()
    gparams = jnp.zeros((8, LANES), F32).at[0, :GDN_H].set(wt["a_log"]).at[1, :GDN_H].set(wt["dt_bias"])

    hn1 = _rms_fwd(hpad, wt["norm1"], "rms1_fwd")
    if first_weights is not None:
        wt = {**wt, **first_weights(hn1[:8, :LANES].astype(F32) + cos[:8] + sin[:8])}
    w_in_t = wt["w_in_t"]
    w_small_t = jnp.pad(w_in_t[_O_GA:_O_RQ], ((0, LANES - 2 * GDN_H), (0, 0)))
    proj_m = _mm_nn(hn1, w_in_t, bt=True, tm_target=2752, b_rows=(D_MODEL, MAIN_W // D_MODEL, _proj_rows),
                    name="proj_main")
    proj_s = _mm_nn(hn1, w_small_t, bt=True, name="proj_small")
    qkv, gsm, conv_out = _gdn_pre(proj_m, proj_s, wt["gdn_conv_w"], gparams, pad)
    o_a, s_a, t_a = _gdn_chunk_fwd(qkv, gsm)
    o_b, s_b = _ret_chunk_fwd(proj_m, cos, sin, tables)
    if late_weights is not None:
        wt = {**wt, **late_weights(o_b)}
    y, h1, hn2 = _merge_fwd(o_a, o_b, proj_m, wt["gdn_norm"], (wt["w_out"], hpad, wt["norm2"]))
    up = _mm_nn(hn2, wt["w_up_t"], bt=True, tm_target=1376, name="ffn_up")
    act, u_ffn = _ffn_act(up, wt["ffn_conv_w"], wt["ffn_conv_b"])
    lossvec, dh2, dh2b, d_norm_f = _final(_Producer(act, wt["w_down"], h1), wt["norm_f"], tgt, first)

    d_w_down = _mm_tn(act, dh2b, name="dw_down")
    dact = _mm_nt(dh2b, wt["w_down"], name="d_act")
    dup, ffn_rows = _ffn_act_bwd(up, u_ffn, dact, wt["ffn_conv_w"])
    d_w_up_t = _mm_tn(dup, hn2, name="dw_up")
    dh1, dh1b, d_norm2 = _rms_bwd(h1, wt["norm2"], _Producer(dup, wt["w_up_t"]), dh2, pad, "d_hn2_rms2_bwd")

    d_w_out = _mm_tn(y, dh1b, name="dw_out")
    gnorm = wt["gdn_norm"]
    if on_ffn_out_grads is not None:
        gnorm = gnorm + on_ffn_out_grads(d_w_down, d_w_up_t, d_w_out)[0:1, :]
    d_c, do_a, do_b, d_gnorm = _merge_bwd(dh1b, wt["w_out"], o_a, o_b, proj_m, gnorm)
    d_r = _ret_chunk_bwd(proj_m, cos, sin, tables, do_b, s_b)
    dq, dk, dv, dgs = _gdn_chunk_bwd(qkv, gsm, do_a, s_a, t_a)
    d_a, d_s, conv_rows, gp_rows = _gdn_pre_bwd(proj_m, conv_out, proj_s, wt["gdn_conv_w"], gparams, dq, dk, dv, dgs,
                                                pad)

    segs = [(d_a, w_in_t[_O_GQ:_O_GZ]), (d_r, w_in_t[_O_RQ:_O_RG]),
            (d_c, jnp.concatenate([w_in_t[_O_GZ:_O_GA], w_in_t[_O_RG:_O_END]], axis=0))]
    pa, pr, pc = [_mm_tn(d, hn1, BF16, name="dw_in_%d" % i) for i, (d, _) in enumerate(segs)]
    ps = _mm_tn(d_s, hn1, BF16, name="dw_in_small")
    d_w_in_t = jnp.concatenate([pa, pc[:D_MODEL], ps[:2 * GDN_H], pr, pc[D_MODEL:]], axis=0)
    if on_w_in_grads is not None:
        w_small_t = w_small_t + on_w_in_grads(d_w_in_t)[0:1, 0:1].astype(w_small_t.dtype)
    dhn1 = _mm_sum([(d_s, w_small_t)] + segs[:-1], "d_hn1_first")
    dh0, _, d_norm1 = _rms_bwd(hpad, wt["norm1"], _Producer(*segs[-1], dhn1), dh1, pad, "d_hn1_rms1_bwd")

    grads = {
        "norm1": d_norm1, "w_in_t": d_w_in_t, "gdn_conv_w": conv_rows[:GDN_CONV],
        "a_log": gp_rows[0, :GDN_H], "dt_bias": gp_rows[1, :GDN_H], "gdn_norm": d_gnorm, "w_out": d_w_out,
        "norm2": d_norm2, "w_up_t": d_w_up_t, "ffn_conv_w": ffn_rows[:FFN_CONV],
        "ffn_conv_b": ffn_rows[FFN_CONV:FFN_CONV + 1], "w_down": d_w_down, "norm_f": d_norm_f,
    }
    return lossvec, dh0, grads


def _peer(k):
    ix, iy, ic = lax.axis_index("x"), lax.axis_index("y"), lax.axis_index("c")
    px = 1 - ix if (k >> 2) & 1 else ix
    py = 1 - iy if (k >> 1) & 1 else iy
    pc = 1 - ic if k & 1 else ic
    return (px, py, pc), 4 * px + 2 * py + pc


def _comm_call(body, n, out_shapes, name, args):
    hbm = pl.BlockSpec(memory_space=pl.ANY)
    return pl.pallas_call(
        body, out_shape=out_shapes, in_specs=[hbm] * n, out_specs=[hbm] * n,
        scratch_shapes=[pltpu.SemaphoreType.DMA((n, N_DEV - 1)), pltpu.SemaphoreType.DMA((n, N_DEV - 1)),
                        pltpu.SemaphoreType.DMA((n,))],
        name=name)(*args)


def _all_gather(xs, name):
    n = len(xs)

    def body(*refs):
        x_refs, out_refs = refs[:n], refs[n:2 * n]
        send_sems, recv_sems, local_sems = refs[2 * n:]
        _, me = _peer(0)
        pending = []
        for i in range(n):
            local = pltpu.make_async_copy(x_refs[i], out_refs[i].at[me], local_sems.at[i])
            local.start()
            pending.append(local)
        sends = []
        for i in range(n):
            for k in range(1, N_DEV):
                dev, _ = _peer(k)
                cp = pltpu.make_async_remote_copy(
                    src_ref=x_refs[i], dst_ref=out_refs[i].at[me], send_sem=send_sems.at[i, k - 1],
                    recv_sem=recv_sems.at[i, k - 1], device_id=dev, device_id_type=MESH_T)
                cp.start()
                sends.append(cp)
        for i in range(n):
            for k in range(1, N_DEV):
                dev, idx = _peer(k)
                pltpu.make_async_remote_copy(
                    src_ref=x_refs[i], dst_ref=out_refs[i].at[idx], send_sem=send_sems.at[i, k - 1],
                    recv_sem=recv_sems.at[i, k - 1], device_id=dev, device_id_type=MESH_T).wait_recv()
        for cp in sends:
            cp.wait_send()
        for local in pending:
            local.wait()

    out_shapes = [jax.ShapeDtypeStruct((N_DEV,) + a.shape, a.dtype) for a in xs]
    return _comm_call(body, n, out_shapes, name, xs)


def _all_to_all(gs, name):
    n = len(gs)

    def body(*refs):
        g_refs, out_refs = refs[:n], refs[n:2 * n]
        send_sems, recv_sems, local_sems = refs[2 * n:]
        _, me = _peer(0)
        pending = []
        for i in range(n):
            local = pltpu.make_async_copy(g_refs[i].at[me], out_refs[i].at[0], local_sems.at[i])
            local.start()
            pending.append(local)
        sends = []
        for i in range(n):
            for k in range(1, N_DEV):
                dev, idx = _peer(k)
                cp = pltpu.make_async_remote_copy(
                    src_ref=g_refs[i].at[idx], dst_ref=out_refs[i].at[k], send_sem=send_sems.at[i, k - 1],
                    recv_sem=recv_sems.at[i, k - 1], device_id=dev, device_id_type=MESH_T)
                cp.start()
                sends.append(cp)
        for cp in sends:
            cp.wait_recv()
        for cp in sends:
            cp.wait_send()
        for local in pending:
            local.wait()

    out_shapes = [jax.ShapeDtypeStruct(g.shape, g.dtype) for g in gs]
    return _comm_call(body, n, out_shapes, name, gs)


_SPLIT_RELATIONS = {"gather": tuple(range(1, N_DEV)), "a2a": tuple(range(1, N_DEV)), "chip": (1, 2, 4, 6),
                    "forward": (2, 4, 6)}


def _split_copies(kind, src_refs, land_refs, send_sems, recv_sems, local_sems, with_recv):
    n = len(land_refs)
    rels = _SPLIT_RELATIONS[kind]
    _, me = _peer(0)
    locals_, remotes = [], []
    for i in range(n):
        if kind in ("gather", "chip"):
            locals_.append(pltpu.make_async_copy(src_refs[i], land_refs[i].at[me], local_sems.at[i]))
        elif kind == "a2a":
            locals_.append(pltpu.make_async_copy(src_refs[i].at[me], land_refs[i].at[0], local_sems.at[i]))
        for jj, k in enumerate(rels):
            dev, idx = _peer(k)
            if kind in ("gather", "chip"):
                src, dst, mine = src_refs[i], land_refs[i].at[me], land_refs[i].at[idx]
            elif kind == "a2a":
                src, dst, mine = src_refs[i].at[idx], land_refs[i].at[k], land_refs[i].at[k]
            else:
                dev, _ = _peer(1)
                _, came = _peer(k + 1)
                src, dst, mine = land_refs[i].at[idx], land_refs[i].at[idx], land_refs[i].at[came]
            j = i * len(rels) + jj
            send = pltpu.make_async_remote_copy(
                src_ref=src, dst_ref=dst, send_sem=send_sems.at[j], recv_sem=recv_sems.at[j],
                device_id=dev, device_id_type=MESH_T)
            recv = pltpu.make_async_remote_copy(
                src_ref=src, dst_ref=mine, send_sem=send_sems.at[j], recv_sem=recv_sems.at[j],
                device_id=dev, device_id_type=MESH_T) if with_recv else None
            remotes.append((send, recv))
    return locals_, remotes


_HBM = pl.BlockSpec(memory_space=pltpu.HBM)
_SEM = pl.BlockSpec(memory_space=pltpu.SEMAPHORE)
_ANY = pl.BlockSpec(memory_space=pl.ANY)


def _split_start(srcs, kind, name, after):
    n = len(srcs)
    if kind == "forward":
        arrays = list(srcs)
    else:
        gathers = kind in ("gather", "chip")
        arrays = list(srcs) + [lax.empty(((N_DEV,) + a.shape) if gathers else a.shape, a.dtype) for a in srcs]
    na = len(arrays)

    def body(*refs):
        src_refs, land_refs = refs[:n], refs[na - n:na]
        send_sems, recv_sems, local_sems = refs[na + 1:na + 4]
        token = refs[-1]
        locals_, remotes = _split_copies(kind, src_refs, land_refs, send_sems, recv_sems, local_sems, False)
        for cp in locals_:
            cp.start()
        for send, _ in remotes:
            send.start()
        token[...] = jnp.zeros_like(token)

    ncp = n * len(_SPLIT_RELATIONS[kind])
    sems = (pltpu.SemaphoreType.DMA((ncp,)), pltpu.SemaphoreType.DMA((ncp,)), pltpu.SemaphoreType.DMA((n,)))
    thru = tuple(pltpu.HBM(a.shape, a.dtype) for a in arrays)
    outs = pl.pallas_call(
        body, name=name,
        out_shape=sems + thru + (jax.ShapeDtypeStruct((8, LANES), F32),),
        in_specs=[_HBM] * na + [_ANY],
        out_specs=[_SEM] * 3 + [_HBM] * na + [pl.BlockSpec(memory_space=pltpu.VMEM)],
        input_output_aliases={i: 3 + i for i in range(na)},
        compiler_params=pltpu.CompilerParams(has_side_effects=pltpu.SideEffectType.DATAFLOW_SIDE_EFFECTING),
    )(*[pltpu.with_memory_space_constraint(a, pltpu.HBM) for a in arrays], after)
    return (kind, n, outs[:3], outs[3:3 + na]), outs[-1]


def _split_wait(handle, name, after):
    kind, n, sems, thru = handle
    na = len(thru)

    def body(*refs):
        src_refs, land_refs = refs[:n], refs[na - n:na]
        send_sems, recv_sems, local_sems = refs[na:na + 3]
        locals_, remotes = _split_copies(kind, src_refs, land_refs, send_sems, recv_sems, local_sems, True)
        for send, recv in remotes:
            send.wait_send()
            recv.wait_recv()
        for cp in locals_:
            cp.wait()

    outs = pl.pallas_call(
        body, name=name, out_shape=tuple(pltpu.HBM(a.shape, a.dtype) for a in thru),
        in_specs=[_HBM] * na + [_SEM] * 3 + [_ANY], out_specs=[_HBM] * na,
        input_output_aliases={i: i for i in range(na)},
        compiler_params=pltpu.CompilerParams(has_side_effects=pltpu.SideEffectType.DATAFLOW_SIDE_EFFECTING),
    )(*thru, *sems, after)
    return list(outs[na - n:])


def _adamw(gslabs, w, m, v, name):
    R, Cw = w.shape
    if R % 8 == 0:
        tr, tc = _tile(R, 64 if Cw > 1024 else 128, 8), Cw
    else:
        tr, tc = R, LANES
    c1 = 1.0 - ADAM_B1 ** ADAM_STEP
    c2 = 1.0 - ADAM_B2 ** ADAM_STEP

    def body(g_ref, w_ref, m_ref, v_ref, go_ref, d_ref, mo_ref, vo_ref):
        g = g_ref[0].astype(F32)
        for k in range(1, N_DEV):
            g = g + g_ref[k].astype(F32)
        mn = ADAM_B1 * m_ref[...] + (1.0 - ADAM_B1) * g
        vn = ADAM_B2 * v_ref[...] + (1.0 - ADAM_B2) * (g * g)
        m_hat = mn / c1
        v_hat = vn / c2
        go_ref[...] = g
        d_ref[...] = -ADAM_LR * (m_hat / (jnp.sqrt(v_hat) + ADAM_EPS) + ADAM_WD * w_ref[...])
        mo_ref[...] = mn
        vo_ref[...] = vn

    blk = pl.BlockSpec((tr, tc), lambda i, j: (i, j))
    return pl.pallas_call(
        body, grid=(R // tr, Cw // tc),
        in_specs=[pl.BlockSpec((N_DEV, tr, tc), lambda i, j: (0, i, j)), blk, blk, blk],
        out_specs=[blk] * 4, out_shape=[jax.ShapeDtypeStruct((R, Cw), F32)] * 4, name=name)(gslabs, w, m, v)


def _pack(arrs, row_mult, dtype=F32):
    parts = []
    total = 0
    for a in arrs:
        f = a.reshape(-1).astype(dtype)
        n = -(-f.shape[0] // 1024) * 1024
        parts.append(jnp.pad(f, (0, n - f.shape[0])))
        total += n
    rows = total // LANES
    rows_p = -(-rows // row_mult) * row_mult
    flat = jnp.concatenate(parts)
    flat = jnp.pad(flat, (0, rows_p * LANES - total))
    return flat.reshape(rows_p, LANES)


def _unpack(packed, shapes):
    lead = packed.shape[:-2]
    flat = packed.reshape(lead + (-1,))
    out = []
    off = 0
    for s in shapes:
        n = int(np.prod(s))
        out.append(flat[..., off:off + n].reshape(lead + tuple(s)))
        off += -(-n // 1024) * 1024
    return out


def _gather_cols(stacked):
    d, r, c = stacked.shape
    return stacked.transpose(1, 0, 2).reshape(r, d * c)


def _scatter_cols(full):
    r, n = full.shape
    return full.reshape(r, N_DEV, n // N_DEV).transpose(1, 0, 2)


def kernel(x, meta, norm1, w_in, gdn_conv_w, gdn_a_log, gdn_dt_bias, gdn_norm, w_out, norm2, w_ffn_up, ffn_conv_w, ffn_conv_b, w_ffn_down, norm_f, loss_target, m_meta, m_norm1, m_w_in, m_gdn_conv_w, m_gdn_a_log, m_gdn_dt_bias, m_gdn_norm, m_w_out, m_norm2, m_w_ffn_up, m_ffn_conv_w, m_ffn_conv_b, m_w_ffn_down, m_norm_f, v_meta, v_norm1, v_w_in, v_gdn_conv_w, v_gdn_a_log, v_gdn_dt_bias, v_gdn_norm, v_w_out, v_norm2, v_w_ffn_up, v_ffn_conv_w, v_ffn_conv_b, v_w_ffn_down, v_norm_f):
    S = x.shape[1]
    L = N_META + S
    pad = (-L) % CHUNK
    Lp = L + pad

    tr_ = lambda a: jnp.swapaxes(a[0], 0, 1)
    big = [tr_(w_in), w_out[0], tr_(w_ffn_up), w_ffn_down[0]]
    small = [meta, gdn_conv_w, ffn_conv_w]
    small_all, = _all_gather([_pack(small, 8)], "gather_small_weights")
    first, first_token = _split_start([big[0].astype(BF16)], "chip", "gather_w_in_start", small_all)
    late, late_token = _split_start([a.astype(BF16) for a in big[1:]], "gather", "gather_late_start", first_token)

    def first_weights(after):
        half = _split_wait(first, "gather_w_in_wait", after)
        second, second_token = _split_start(half, "forward", "gather_w_in_forward_start", after)
        w_in_s, = _split_wait(second, "gather_w_in_forward_wait", second_token)
        return {"w_in_t": w_in_s.reshape(_O_END, D_MODEL)}

    def late_weights(after):
        w_out_s, w_up_s, w_down_s = _split_wait(late, "gather_late_wait", after)
        return {"w_out": w_out_s.reshape(D_MODEL, D_MODEL), "w_up_t": w_up_s.reshape(2 * D_FF, D_MODEL),
                "w_down": w_down_s.reshape(D_FF, D_MODEL)}

    meta_s, gconv_s, fconv_s = _unpack(small_all, [a.shape for a in small])
    wt = {
        "norm1": norm1 + jnp.tile(late_token[0:1, :], (1, D_MODEL // LANES)),
        "gdn_conv_w": _gather_cols(gconv_s[:, 0]), "a_log": gdn_a_log[0], "dt_bias": gdn_dt_bias[0],
        "gdn_norm": gdn_norm, "norm2": norm2, "ffn_conv_w": _gather_cols(fconv_s[:, 0]), "ffn_conv_b": ffn_conv_b,
        "norm_f": norm_f.reshape(1, D_MODEL),
    }
    meta_f = _gather_cols(meta_s)

    pending = {}

    def on_ffn_out_grads(d_w_down, d_w_up_t, d_w_out):
        srcs = [d_w_out.reshape(N_DEV, D_MODEL // N_DEV, D_MODEL), d_w_up_t.reshape(N_DEV, 2 * D_FF // N_DEV, D_MODEL),
                d_w_down.reshape(N_DEV, D_FF // N_DEV, D_MODEL)]
        pending["ffn_out"], token = _split_start(srcs, "a2a", "exchange_ffn_out_start", d_w_out)
        return token

    def on_w_in_grads(d_w_in_t):
        slabs = d_w_in_t.astype(BF16).reshape(N_DEV, _O_END // N_DEV, D_MODEL)
        pending["w_in"], token = _split_start([slabs], "a2a", "exchange_w_in_start", d_w_in_t)
        return token

    head = jnp.concatenate([jnp.zeros((pad, D_MODEL), F32), meta_f], axis=0)
    if S >= 2 * 704:
        hpad = _Rows(x[0], pad + N_META, head)
        tgt = _Rows(loss_target[0], pad + N_META)
    else:
        hpad = jnp.concatenate([head, x[0]], axis=0)
        tgt = jnp.concatenate([jnp.zeros((pad + N_META, D_MODEL), F32), loss_target[0]], axis=0)
    lossvec, dh0, gr = _local_step(hpad, tgt, pad, wt, first_weights, late_weights, on_ffn_out_grads, on_w_in_grads)

    loss = lax.psum(jnp.sum(lossvec), ("x", "y", "c"))
    grad_x = dh0[pad + N_META:][None]

    big_m = [tr_(m_w_in), m_w_out[0], tr_(m_w_ffn_up), m_w_ffn_down[0]]
    big_v = [tr_(v_w_in), v_w_out[0], tr_(v_w_ffn_up), v_w_ffn_down[0]]
    slabs_ffn_out = _split_wait(pending["ffn_out"], "exchange_ffn_out_wait", dh0)
    big_out = [None] + [_adamw(slabs_ffn_out[i - 1], big[i], big_m[i], big_v[i], "adamw_big_%d" % i)
                        for i in range(1, len(big))]
    g_sm = [_scatter_cols(dh0[pad:pad + N_META]), _scatter_cols(gr["gdn_conv_w"]), _scatter_cols(gr["ffn_conv_w"])]
    g_small = jnp.stack([_pack([g[d] for g in g_sm], 8) for d in range(N_DEV)])
    slabs_small, = _all_to_all([g_small], "exchange_small_gradients")
    small_out = _adamw(slabs_small, _pack(small, 8), _pack([m_meta, m_gdn_conv_w, m_ffn_conv_w], 8),
                       _pack([v_meta, v_gdn_conv_w, v_ffn_conv_w], 8), "adamw_small_sharded")
    small_un = [_unpack(o, [a.shape for a in small]) for o in small_out]
    rep_w = [norm1, gdn_a_log, gdn_dt_bias, gdn_norm, norm2, ffn_conv_b, norm_f]
    rep_m = [m_norm1, m_gdn_a_log, m_gdn_dt_bias, m_gdn_norm, m_norm2, m_ffn_conv_b, m_norm_f]
    rep_v = [v_norm1, v_gdn_a_log, v_gdn_dt_bias, v_gdn_norm, v_norm2, v_ffn_conv_b, v_norm_f]
    rep_g = [gr["norm1"], gr["a_log"], gr["dt_bias"], gr["gdn_norm"], gr["norm2"], gr["ffn_conv_b"], gr["norm_f"]]
    rep_slabs, = _all_gather([_pack(rep_g, 8)], "gather_small_gradients")
    rep_out = _adamw(rep_slabs, _pack(rep_w, 8), _pack(rep_m, 8), _pack(rep_v, 8), "adamw_replicated")
    rep_shapes = [a.shape for a in rep_w]
    rp_g, rp_d, rp_nm, rp_nv = [_unpack(o, rep_shapes) for o in rep_out]

    slabs_w_in, = _split_wait(pending["w_in"], "exchange_w_in_wait", rep_out[0])
    big_out[0] = _adamw(slabs_w_in, big[0], big_m[0], big_v[0], "adamw_big_0")
    back = lambda a: jnp.swapaxes(a, 0, 1)[None]
    sh_g, sh_d, sh_nm, sh_nv = [
        [small_un[j][0], back(big_out[0][j]), small_un[j][1], big_out[1][j][None], back(big_out[2][j]),
         small_un[j][2], big_out[3][j][None]] for j in range(4)]

    def order(sh, rp):
        return [sh[0], rp[0], sh[1], sh[2], rp[1], rp[2], rp[3], sh[3], rp[4], sh[4], sh[5], rp[5], sh[6], rp[6]]

    return (loss, grad_x, *order(sh_g, rp_g), *order(sh_d, rp_d), *order(sh_nm, rp_nm), *order(sh_nv, rp_nv))
```

```python
import functools
import math

import numpy as np
import jax
import jax.numpy as jnp
from jax import lax
from jax.experimental import pallas as pl
from jax.experimental.pallas import tpu as pltpu

F32 = jnp.float32
BF16 = jnp.bfloat16

D_MODEL = 1024
N_META = 16
CHUNK = 64
GDN_H = 8
GDN_D = 128
RET_H = 4
RET_D = 256
D_FF = 2816
GDN_CONV = 4
FFN_CONV = 3
ROPE_BASE = 10000.0
EPS = 1e-6
N_DEV = 8
LANES = 128
MAIN_W = 10 * 1024
_O_GQ, _O_GZ, _O_GA, _O_RQ, _O_RG, _O_GATE, _O_END = 0, 3072, 4096, 4112, 7184, 8208, 10256

ADAM_LR = 0.001
ADAM_B1 = 0.9
ADAM_B2 = 0.999
ADAM_EPS = 1e-08
ADAM_WD = 0.01
ADAM_STEP = 10

MESH_T = pl.DeviceIdType.MESH


def _tile(n, target, mult):
    best = None
    for d in range(mult, min(n, target) + 1, mult):
        if n % d == 0:
            best = d
    assert best is not None, (n, target, mult)
    return best


def _sig(x):
    return 0.5 * jnp.tanh(0.5 * x) + 0.5


def _d(a, b):
    return jnp.dot(a.astype(BF16), b.astype(BF16), preferred_element_type=F32)


def _dnt(a, b):
    return lax.dot_general(a.astype(BF16), b.astype(BF16), (((1,), (1,)), ((), ())), preferred_element_type=F32)


def _dtn(a, b):
    return lax.dot_general(a.astype(BF16), b.astype(BF16), (((0,), (0,)), ((), ())), preferred_element_type=F32)


def _dxg(a, b, dims):
    f = functools.partial(lax.dot_general, dimension_numbers=dims, preferred_element_type=F32)
    ab = a.astype(BF16)
    b1 = b.astype(BF16)
    r1 = b - b1.astype(F32)
    b2 = r1.astype(BF16)
    b3 = (r1 - b2.astype(F32)).astype(BF16)
    return f(ab, b1) + (f(ab, b2) + f(ab, b3))


def _dx(a, b):
    return _dxg(a, b, (((1,), (0,)), ((), ())))


def _dxnt(a, b):
    return _dxg(a, b, (((1,), (1,)), ((), ())))


def _split(a):
    hi = a.astype(BF16)
    return hi, (a - hi.astype(F32)).astype(BF16)


def _d3g(a, b, dims):
    ah, al = _split(a)
    bh, bl = _split(b)
    f = functools.partial(lax.dot_general, dimension_numbers=dims, preferred_element_type=F32)
    if dims == _NN:
        rows = a.shape[0]
        both = f(jnp.concatenate([ah, al], axis=0), bh)
        return both[:rows] + (f(ah, bl) + both[rows:])
    return f(ah, bh) + (f(ah, bl) + f(al, bh))


def _d2x(a, b, dims):
    ah, al = _split(a)
    f = functools.partial(lax.dot_general, dimension_numbers=dims, preferred_element_type=F32)
    bb = b.astype(BF16)
    return f(ah, bb) + f(al, bb)


_NN = (((1,), (0,)), ((), ()))
_NT = (((1,), (1,)), ((), ()))
_TN = (((0,), (0,)), ((), ()))


def _rowsum(x):
    return jnp.sum(x, axis=1, keepdims=True)


def _allsum(x):
    return jnp.sum(jnp.sum(x, axis=1, keepdims=True), axis=0, keepdims=True)


def _mm_nn(a, b, res=None, out_dtype=F32, bt=False, tm_target=704, b_rows=None, name="mm_nn"):
    M, K = a.shape
    N = b.shape[0] if bt else b.shape[1]
    tm = _tile(M, tm_target, 16)
    if b_rows is None:
        tn = _tile(N, 2816, 128)
    else:
        tn, n_tiles, start = b_rows
        N = tn * n_tiles

    def body(*refs):
        if res is None:
            a_ref, b_ref, o_ref = refs
        else:
            a_ref, b_ref, r_ref, o_ref = refs
        acc = lax.dot_general(a_ref[...], b_ref[...], _NT if bt else _NN, preferred_element_type=F32)
        if res is not None:
            acc = acc + r_ref[...]
        o_ref[...] = acc.astype(out_dtype)

    b_spec = pl.BlockSpec((tn, K), lambda j, i: (j, 0)) if bt else pl.BlockSpec((K, tn), lambda j, i: (0, j))
    if b_rows is not None:
        b_spec = pl.BlockSpec((pl.Element(tn), pl.Element(K)), lambda j, i: (pl.multiple_of(start(j), 16), 0))
    in_specs = [pl.BlockSpec((tm, K), lambda j, i: (i, 0)), b_spec]
    args = [a, b]
    if res is not None:
        in_specs.append(pl.BlockSpec((tm, tn), lambda j, i: (i, j)))
        args.append(res)
    return pl.pallas_call(
        body, grid=(N // tn, M // tm), in_specs=in_specs,
        out_specs=pl.BlockSpec((tm, tn), lambda j, i: (i, j)),
        out_shape=jax.ShapeDtypeStruct((M, N), out_dtype), name=name)(*args)


def _mm_sum(pairs, name):
    M = pairs[0][0].shape[0]
    N = pairs[0][1].shape[1]
    tm = _tile(M, 704, 16)
    n = len(pairs)

    def body(*refs):
        o_ref = refs[-1]
        acc = jnp.dot(refs[0][...], refs[1][...], preferred_element_type=F32)
        for i in range(1, n):
            acc = acc + jnp.dot(refs[2 * i][...], refs[2 * i + 1][...], preferred_element_type=F32)
        o_ref[...] = acc

    specs, args = [], []
    for a, b in pairs:
        specs += [pl.BlockSpec((tm, a.shape[1]), lambda i: (i, 0)),
                  pl.BlockSpec(b.shape, lambda i: (0, 0), pipeline_mode=pl.Buffered(1))]
        args += [a, b]
    return pl.pallas_call(
        body, grid=(M // tm,), in_specs=specs, out_specs=pl.BlockSpec((tm, N), lambda i: (i, 0)),
        out_shape=jax.ShapeDtypeStruct((M, N), F32), name=name)(*args)


def _mm_nt(a, b, res=None, name="mm_nt"):
    M, Nc = a.shape
    K = b.shape[0]
    tm = _tile(M, 704, 16)
    tc = _tile(Nc, 5632, 128)

    def body(*refs):
        if res is None:
            a_ref, b_ref, o_ref = refs
        else:
            a_ref, b_ref, r_ref, o_ref = refs
        c = pl.program_id(1)
        p = lax.dot_general(a_ref[...], b_ref[...], (((1,), (1,)), ((), ())), preferred_element_type=F32)

        @pl.when(c == 0)
        def _():
            if res is None:
                o_ref[...] = p
            else:
                o_ref[...] = p + r_ref[...]

        @pl.when(c > 0)
        def _():
            o_ref[...] += p

    in_specs = [pl.BlockSpec((tm, tc), lambda i, c: (i, c)), pl.BlockSpec((K, tc), lambda i, c: (0, c))]
    args = [a, b]
    if res is not None:
        in_specs.append(pl.BlockSpec((tm, K), lambda i, c: (i, 0)))
        args.append(res)
    return pl.pallas_call(
        body, grid=(M // tm, Nc // tc), in_specs=in_specs,
        out_specs=pl.BlockSpec((tm, K), lambda i, c: (i, 0)),
        out_shape=jax.ShapeDtypeStruct((M, K), F32), name=name)(*args)


def _mm_tn(a, b, out_dtype=F32, name="mm_tn"):
    M, K = a.shape
    N = b.shape[1]
    tm = _tile(M, 2752, 16)
    tk = _tile(K, 1408, 128)
    tn = _tile(N, 1408, 128)
    steps = M // tm

    def body(a_ref, b_ref, o_ref, *scratch):
        acc = scratch[0] if scratch else o_ref
        m = pl.program_id(2)
        p = lax.dot_general(a_ref[...], b_ref[...], (((0,), (0,)), ((), ())), preferred_element_type=F32)

        @pl.when(m == 0)
        def _():
            acc[...] = p

        @pl.when(m > 0)
        def _():
            acc[...] += p

        if scratch:
            @pl.when(m == steps - 1)
            def _():
                o_ref[...] = acc[...].astype(out_dtype)

    return pl.pallas_call(
        body, grid=(K // tk, N // tn, steps),
        in_specs=[pl.BlockSpec((tm, tk), lambda kk, j, m: (m, kk)), pl.BlockSpec((tm, tn), lambda kk, j, m: (m, j))],
        out_specs=pl.BlockSpec((tk, tn), lambda kk, j, m: (kk, j)),
        out_shape=jax.ShapeDtypeStruct((K, N), out_dtype),
        scratch_shapes=[] if out_dtype == F32 else [pltpu.VMEM((tk, tn), F32)], name=name)(a, b)


class _Rows:
    def __init__(self, body, first, head=None):
        self.body, self.first, self.head = body, first, head
        self.shape = (body.shape[0] + first, body.shape[1])


def _rows_operands(x, tr):
    if not isinstance(x, _Rows):
        return [x], [pl.BlockSpec((tr, x.shape[1]), lambda i: (i, 0))]
    assert x.first % 8 == 0 and x.first <= tr <= x.body.shape[0] and x.shape[0] % tr == 0
    width = x.shape[1]
    args = [x.body]
    specs = [pl.BlockSpec((pl.Element(tr), pl.Element(width)),
                          lambda i: (pl.multiple_of(jnp.maximum(i * tr - x.first, 0), 8), 0))]
    if x.head is not None:
        args.append(jnp.pad(x.head, ((0, tr - x.first), (0, 0))))
        specs.append(pl.BlockSpec((tr, width), lambda i: (0, 0)))
    return args, specs


def _rows_tile(x, refs, i, tr):
    blk = refs[0][...]
    if not isinstance(x, _Rows):
        return blk
    shifted = pltpu.roll(blk, x.first, 0)
    if x.head is not None:
        row = lax.broadcasted_iota(jnp.int32, (tr, 1), 0)
        shifted = jnp.where(row < x.first, refs[1][...], shifted)
    return jnp.where(i == 0, shifted, blk)


def _rms_fwd(x, g, name):
    Lp = x.shape[0]
    tr = _tile(Lp, 704, 16)
    args, specs = _rows_operands(x, tr)
    n = len(args)

    def body(*refs):
        g_ref, o_ref = refs[n:]
        xv = _rows_tile(x, refs[:n], pl.program_id(0), tr)
        r = lax.rsqrt(jnp.mean(xv * xv, axis=-1, keepdims=True) + EPS)
        o_ref[...] = (xv * r * g_ref[...]).astype(BF16)

    return pl.pallas_call(
        body, grid=(Lp // tr,),
        in_specs=specs + [pl.BlockSpec((1, D_MODEL), lambda i: (0, 0))],
        out_specs=pl.BlockSpec((tr, D_MODEL), lambda i: (i, 0)),
        out_shape=jax.ShapeDtypeStruct((Lp, D_MODEL), BF16), name=name)(*args, g)


class _Producer:
    def __init__(self, a, b, res=None):
        self.a, self.b, self.res = a, b, res
        self.tr = _tile(a.shape[0], 704, 16)
        K = a.shape[1]
        r_args, r_specs = ([], []) if res is None else _rows_operands(res, self.tr)
        self.args = [a, b] + r_args
        self.specs = [pl.BlockSpec((self.tr, K), lambda i: (i, 0)),
                      pl.BlockSpec((K, D_MODEL), lambda i: (0, 0), pipeline_mode=pl.Buffered(1))] + r_specs

    def tile(self, refs, i):
        acc = jnp.dot(refs[0][...], refs[1][...], preferred_element_type=F32)
        return acc if self.res is None else acc + _rows_tile(self.res, refs[2:], i, self.tr)


def _rms_bwd(x, g, dy, dres, pad, name):
    Lp = x.shape[0]
    fused = isinstance(dy, _Producer)
    tr = dy.tr if fused else _tile(Lp, 256, 16)
    n = len(dy.args) if fused else 1
    x_args, x_specs = _rows_operands(x, tr)
    nx = len(x_args)

    def body(*refs):
        g_ref, dr_ref, dx_ref, dxb_ref, dg_ref = refs[n + nx:]
        i = pl.program_id(0)
        xv = _rows_tile(x, refs[n:n + nx], i, tr)
        r = lax.rsqrt(jnp.mean(xv * xv, axis=-1, keepdims=True) + EPS)
        xh = xv * r
        dyv = dy.tile(refs[:n], i) if fused else refs[0][...]
        dxh = dyv * g_ref[...]
        dx = r * (dxh - xh * jnp.mean(dxh * xh, axis=-1, keepdims=True)) + dr_ref[...]
        row = i * tr + lax.broadcasted_iota(jnp.int32, (tr, 1), 0)
        dx = jnp.where(row >= pad, dx, 0.0)
        dx_ref[...] = dx
        dxb_ref[...] = dx.astype(BF16)
        part = jnp.sum(dyv * xh, axis=0, keepdims=True)

        @pl.when(i == 0)
        def _():
            dg_ref[...] = part

        @pl.when(i > 0)
        def _():
            dg_ref[...] += part

    blk = pl.BlockSpec((tr, D_MODEL), lambda i: (i, 0))
    vec = pl.BlockSpec((1, D_MODEL), lambda i: (0, 0))
    return pl.pallas_call(
        body, grid=(Lp // tr,), in_specs=(dy.specs if fused else [blk]) + x_specs + [vec, blk],
        out_specs=[blk, blk, vec],
        out_shape=[jax.ShapeDtypeStruct((Lp, D_MODEL), F32), jax.ShapeDtypeStruct((Lp, D_MODEL), BF16),
                   jax.ShapeDtypeStruct((1, D_MODEL), F32)], name=name)(*(dy.args if fused else [dy]), *x_args, g, dres)


def _final(h2, g, tgt, first_row):
    fused = isinstance(h2, _Producer)
    Lp = h2.a.shape[0] if fused else h2.shape[0]
    tr = h2.tr if fused else _tile(Lp, 256, 16)
    n = len(h2.args) if fused else 1
    t_args, t_specs = _rows_operands(tgt, tr)
    nt = len(t_args)

    def body(*refs):
        g_ref = refs[n]
        loss_ref, dx_ref, dxb_ref, dg_ref = refs[n + 1 + nt:]
        i = pl.program_id(0)
        xv = h2.tile(refs[:n], i) if fused else refs[0][...]
        tv = _rows_tile(tgt, refs[n + 1:n + 1 + nt], i, tr)
        gv = g_ref[...]
        r = lax.rsqrt(jnp.mean(xv * xv, axis=-1, keepdims=True) + EPS)
        xh = xv * r
        row = i * tr + lax.broadcasted_iota(jnp.int32, (tr, 1), 0)
        err = jnp.where(row >= first_row, xh * gv - tv, 0.0)
        lpart = jnp.sum(err * err, axis=0, keepdims=True) * (0.5 / D_MODEL)
        dyv = err * (1.0 / D_MODEL)
        dxh = dyv * gv
        dx = r * (dxh - xh * jnp.mean(dxh * xh, axis=-1, keepdims=True))
        dx_ref[...] = dx
        dxb_ref[...] = dx.astype(BF16)
        part = jnp.sum(dyv * xh, axis=0, keepdims=True)

        @pl.when(i == 0)
        def _():
            dg_ref[...] = part
            loss_ref[...] = lpart

        @pl.when(i > 0)
        def _():
            dg_ref[...] += part
            loss_ref[...] += lpart

    blk = pl.BlockSpec((tr, D_MODEL), lambda i: (i, 0))
    vec = pl.BlockSpec((1, D_MODEL), lambda i: (0, 0))
    return pl.pallas_call(
        body, grid=(Lp // tr,), in_specs=(h2.specs if fused else [blk]) + [vec] + t_specs,
        out_specs=[vec, blk, blk, vec],
        out_shape=[jax.ShapeDtypeStruct((1, D_MODEL), F32), jax.ShapeDtypeStruct((Lp, D_MODEL), F32),
                   jax.ShapeDtypeStruct((Lp, D_MODEL), BF16), jax.ShapeDtypeStruct((1, D_MODEL), F32)],
        name="final_norm_loss")(*(h2.args if fused else [h2]), g, *t_args)


def _halo_prev(tr, width, col=0):
    return pl.BlockSpec((8, width), lambda i: (jnp.maximum(i * (tr // 8) - 1, 0), col))


def _halo_next(tr, width, nrows, col=0, rows=8):
    last = nrows // rows - 1
    return pl.BlockSpec((rows, width), lambda i: (jnp.minimum((i + 1) * (tr // rows), last), col))


def _shifted(x, offs):
    n = x.shape[0]
    return [x if off == 0 else pltpu.roll(x, n - off, 0) for off in offs]


def _taps(wins, w, rows, bias=None):
    acc = w[0:1, :] * wins[0][0:rows, :]
    if bias is not None:
        acc = acc + bias
    for kk in range(1, len(wins)):
        acc = acc + w[kk:kk + 1, :] * wins[kk][0:rows, :]
    return acc


def _gdn_pre(proj_m, proj_s, conv_w, gparams, pad):
    Lp = proj_m.shape[0]
    tr = _tile(Lp, 192, 64)
    W3 = 3 * D_MODEL

    def body(main_ref, prev_ref, s_ref, w_ref, gp_ref, qkv_ref, gsm_ref, c_ref):
        i = pl.program_id(0)
        prev = jnp.where(i > 0, prev_ref[...], 0.0)
        ext = jnp.concatenate([prev, main_ref[...]], axis=0)
        c = _taps(_shifted(ext, range(8 - (GDN_CONV - 1), 9)), w_ref[...], tr)
        c_ref[...] = c.astype(BF16)
        s = c * _sig(c)
        scale = GDN_D ** -0.5
        for j in range(2 * GDN_H):
            seg = s[:, j * GDN_D:(j + 1) * GDN_D]
            r = lax.rsqrt(_rowsum(seg * seg) + EPS)
            if j < GDN_H:
                r = r * scale
            qkv_ref[:, j * GDN_D:(j + 1) * GDN_D] = seg * r
        qkv_ref[:, 2 * D_MODEL:] = s[:, 2 * D_MODEL:]
        sm = s_ref[...]
        gp = gp_ref[...]
        lane = lax.broadcasted_iota(jnp.int32, sm.shape, 1)
        z = sm + gp[1:2, :]
        softplus = jnp.maximum(z, 0.0) + jnp.log(1.0 + jnp.exp(-jnp.abs(z)))
        lg = -jnp.exp(gp[0:1, :]) * softplus
        row = i * tr + lax.broadcasted_iota(jnp.int32, (tr, 1), 0)
        out = jnp.where(lane < GDN_H, lg, jnp.where(lane < 2 * GDN_H, _sig(sm), 0.0))
        gsm_ref[...] = jnp.where(row >= pad, out, 0.0)

    return pl.pallas_call(
        body, grid=(Lp // tr,),
        in_specs=[pl.BlockSpec((tr, W3), lambda i: (i, 0)), _halo_prev(tr, W3),
                  pl.BlockSpec((tr, LANES), lambda i: (i, 0)),
                  pl.BlockSpec((GDN_CONV, W3), lambda i: (0, 0)), pl.BlockSpec((8, LANES), lambda i: (0, 0))],
        out_specs=[pl.BlockSpec((tr, W3), lambda i: (i, 0)), pl.BlockSpec((tr, LANES), lambda i: (i, 0)),
                   pl.BlockSpec((tr, W3), lambda i: (i, 0))],
        out_shape=[jax.ShapeDtypeStruct((Lp, W3), F32), jax.ShapeDtypeStruct((Lp, LANES), F32),
                   jax.ShapeDtypeStruct((Lp, W3), BF16)],
        name="gdn_pre")(proj_m, proj_m, proj_s, conv_w, gparams)


def _gdn_pre_bwd(proj_m, conv_out, proj_s, conv_w, gparams, dq, dk, dv, dgs, pad):
    Lp = proj_m.shape[0]
    tr = _tile(Lp, 192, 64)
    W3 = 3 * D_MODEL
    te = tr + 8

    def body(main_ref, c_ref, cn_ref, s_ref, w_ref, gp_ref,
             dq_ref, dqn_ref, dk_ref, dkn_ref, dv_ref, dvn_ref, dgs_ref,
             da_ref, ds_ref, dw_ref, dgp_ref):
        i = pl.program_id(0)
        w = w_ref[...]
        c = jnp.concatenate([c_ref[...].astype(F32), cn_ref[...].astype(F32)[0:8]], axis=0)
        sg = _sig(c)
        s = c * sg
        rowe = i * tr + lax.broadcasted_iota(jnp.int32, (te, 1), 0)
        live = (rowe >= pad) & (rowe < Lp)
        dqe = jnp.concatenate([dq_ref[...], dqn_ref[...]], axis=0)
        dke = jnp.concatenate([dk_ref[...], dkn_ref[...]], axis=0)
        dve = jnp.concatenate([dv_ref[...], dvn_ref[...]], axis=0)
        scale = GDN_D ** -0.5
        parts = []
        for j in range(2 * GDN_H):
            seg = s[:, j * GDN_D:(j + 1) * GDN_D]
            r = lax.rsqrt(_rowsum(seg * seg) + EPS)
            xh = seg * r
            if j < GDN_H:
                dxh = dqe[:, j * GDN_D:(j + 1) * GDN_D] * scale
            else:
                dxh = dke[:, (j - GDN_H) * GDN_D:(j - GDN_H + 1) * GDN_D]
            parts.append(r * (dxh - xh * _rowsum(dxh * xh)))
        parts.append(dve)
        dsv = jnp.concatenate(parts, axis=1)
        dc = jnp.where(live, dsv * (sg * (1.0 + c * (1.0 - sg))), 0.0)
        dcs = _shifted(dc, range(GDN_CONV - 1, -1, -1))
        da_ref[...] = _taps(dcs, w, tr).astype(BF16)
        pm = main_ref[...]
        rows = [jnp.sum(dcs[kk][0:tr, :] * pm, axis=0, keepdims=True) for kk in range(GDN_CONV)]
        dwp = jnp.concatenate(rows + [jnp.zeros((8 - GDN_CONV, W3), F32)], axis=0)

        sm = s_ref[...]
        gp = gp_ref[...]
        lane = lax.broadcasted_iota(jnp.int32, sm.shape, 1)
        rowm = i * tr + lax.broadcasted_iota(jnp.int32, (tr, 1), 0)
        dgv = jnp.where(rowm >= pad, dgs_ref[...], 0.0)
        dlg = jnp.where(lane < GDN_H, dgv, 0.0)
        dbt = jnp.where((lane >= GDN_H) & (lane < 2 * GDN_H), dgv, 0.0)
        z = sm + gp[1:2, :]
        softplus = jnp.maximum(z, 0.0) + jnp.log(1.0 + jnp.exp(-jnp.abs(z)))
        ea = jnp.exp(gp[0:1, :])
        dz = dlg * (-ea) * _sig(z)
        dal = dlg * (-ea) * softplus
        bt = _sig(sm)
        dgb = dbt * bt * (1.0 - bt)
        ds_ref[...] = (dz + dgb).astype(BF16)
        gpp = jnp.concatenate([jnp.sum(dal, axis=0, keepdims=True), jnp.sum(dz, axis=0, keepdims=True),
                               jnp.zeros((6, LANES), F32)], axis=0)

        @pl.when(i == 0)
        def _():
            dw_ref[...] = dwp
            dgp_ref[...] = gpp

        @pl.when(i > 0)
        def _():
            dw_ref[...] += dwp
            dgp_ref[...] += gpp

    m3 = pl.BlockSpec((tr, W3), lambda i: (i, 0))
    m1 = pl.BlockSpec((tr, D_MODEL), lambda i: (i, 0))
    n1 = _halo_next(tr, D_MODEL, Lp)
    return pl.pallas_call(
        body, grid=(Lp // tr,),
        in_specs=[m3, m3, _halo_next(tr, W3, Lp, rows=16), pl.BlockSpec((tr, LANES), lambda i: (i, 0)),
                  pl.BlockSpec((GDN_CONV, W3), lambda i: (0, 0)), pl.BlockSpec((8, LANES), lambda i: (0, 0)),
                  m1, n1, m1, n1, m1, n1, pl.BlockSpec((tr, LANES), lambda i: (i, 0))],
        out_specs=[m3, pl.BlockSpec((tr, LANES), lambda i: (i, 0)),
                   pl.BlockSpec((8, W3), lambda i: (0, 0)), pl.BlockSpec((8, LANES), lambda i: (0, 0))],
        out_shape=[jax.ShapeDtypeStruct((Lp, W3), BF16), jax.ShapeDtypeStruct((Lp, LANES), BF16),
                   jax.ShapeDtypeStruct((8, W3), F32), jax.ShapeDtypeStruct((8, LANES), F32)],
        name="gdn_pre_bwd")(proj_m, conv_out, conv_out, proj_s, conv_w, gparams, dq, dq, dk, dk, dv, dv, dgs)


def _gdn_gates(gs):
    ri = lax.broadcasted_iota(jnp.int32, (CHUNK, CHUNK), 0)
    ci = lax.broadcasted_iota(jnp.int32, (CHUNK, CHUNK), 1)
    tril = ri >= ci
    strict = ri > ci
    gall = _dx(tril.astype(F32), gs)
    lane8 = lax.broadcasted_iota(jnp.int32, (8, LANES), 1)
    sub8 = lax.broadcasted_iota(jnp.int32, (8, LANES), 0)
    grow = _dxnt((lane8 == sub8).astype(F32), gall)
    return gall, grow, tril, strict


def _gdn_decay(gall, grow, tril, h):
    g = gall[:, h:h + 1]
    return g, jnp.where(tril, jnp.exp(jnp.where(tril, g - grow[h:h + 1, :], 0.0)), 0.0)


def _group(N):
    return 3 if N % 3 == 0 else (2 if N % 2 == 0 else 1)


def _gdn_chunk_specs(N, rev):
    G = _group(N)
    nb = N // G
    cn = (lambda n: nb - 1 - n) if rev else (lambda n: n)
    col = lambda j: pl.BlockSpec((G * CHUNK, D_MODEL), lambda n: (cn(n), j))
    gate = pl.BlockSpec((G * CHUNK, LANES), lambda n: (cn(n), 0))
    st = lambda a, b: pl.BlockSpec((GDN_H, G, a, b), lambda n: (0, cn(n), 0, 0))
    return G, nb, col, gate, st


def _gdn_chunk_fwd(qkv, gsm):
    Lp = qkv.shape[0]
    N = Lp // CHUNK
    G, nb, col, gate, st = _gdn_chunk_specs(N, False)

    def body(q_ref, k_ref, v_ref, gs_ref, o_ref, sin_ref, t_ref, S):
        n = pl.program_id(0)

        @pl.when(n == 0)
        def _():
            S[...] = jnp.zeros_like(S)

        ri = lax.broadcasted_iota(jnp.int32, (CHUNK, CHUNK), 0)
        ci = lax.broadcasted_iota(jnp.int32, (CHUNK, CHUNK), 1)
        eye = (ri == ci).astype(F32)
        heads = range(GDN_H)
        sls = [slice(h * GDN_D, (h + 1) * GDN_D) for h in heads]
        rows = [slice(c * CHUNK, (c + 1) * CHUNK) for c in range(G)]
        pairs = [(c, h) for c in range(G) for h in heads]
        P = lambda f: {p: f(*p) for p in pairs}
        gs = [gs_ref[rows[c], :] for c in range(G)]
        gates = [_gdn_gates(gs[c]) for c in range(G)]
        tril, strict = gates[0][2], gates[0][3]
        q = P(lambda c, h: q_ref[rows[c], sls[h]])
        k = P(lambda c, h: k_ref[rows[c], sls[h]])
        v = P(lambda c, h: v_ref[rows[c], sls[h]])
        beta = P(lambda c, h: gs[c][:, GDN_H + h:GDN_H + h + 1])
        gg = P(lambda c, h: _gdn_decay(gates[c][0], gates[c][1], tril, h))
        g = {p: x[0] for p, x in gg.items()}
        gam = {p: x[1] for p, x in gg.items()}
        eg = P(lambda c, h: jnp.exp(g[c, h]))
        gl = P(lambda c, h: g[c, h][CHUNK - 1:CHUNK, :])
        kb = P(lambda c, h: k[c, h] * beta[c, h])
        pw = P(lambda c, h: -jnp.where(strict, _dnt(kb[c, h], k[c, h]) * gam[c, h], 0.0))
        p = P(lambda c, h: _dnt(q[c, h], k[c, h]) * gam[c, h])
        t = P(lambda c, h: eye + pw[c, h])
        for it in range(5):
            mm = _d3g if it < 2 else (lambda a, b, dims: _d(a, b))
            pw = P(lambda c, h: mm(pw[c, h], pw[c, h], _NN))
            t = P(lambda c, h: t[c, h] + mm(t[c, h], pw[c, h], _NN))
        u = P(lambda c, h: _d(t[c, h], v[c, h] * beta[c, h]))
        w = P(lambda c, h: _d(t[c, h], kb[c, h] * eg[c, h]))
        qg = P(lambda c, h: q[c, h] * eg[c, h])
        kd = P(lambda c, h: k[c, h] * jnp.exp(gl[c, h] - g[c, h]))
        egl = P(lambda c, h: jnp.exp(gl[c, h]))
        for c in range(G):
            for h in heads:
                t_ref[h, c] = t[c, h]
        cur = [S[h] for h in heads]
        for c in range(G):
            vnew = [u[c, h] - _d(w[c, h], cur[h]) for h in heads]
            for h in heads:
                o_ref[rows[c], sls[h]] = _d(qg[c, h], cur[h]) + _d(p[c, h], vnew[h])
                sin_ref[h, c] = cur[h]
            cur = [cur[h] * egl[c, h] + _dtn(kd[c, h], vnew[h]) for h in heads]
        for h in heads:
            S[h] = cur[h]

    return pl.pallas_call(
        body, grid=(nb,),
        in_specs=[col(0), col(1), col(2), gate],
        out_specs=[col(0), st(GDN_D, GDN_D), st(CHUNK, CHUNK)],
        out_shape=[jax.ShapeDtypeStruct((Lp, D_MODEL), F32), jax.ShapeDtypeStruct((GDN_H, N, GDN_D, GDN_D), F32),
                   jax.ShapeDtypeStruct((GDN_H, N, CHUNK, CHUNK), F32)],
        scratch_shapes=[pltpu.VMEM((GDN_H, GDN_D, GDN_D), F32)],
        name="gdn_chunk_fwd")(qkv, qkv, qkv, gsm)


def _gdn_bwd_common(G, gs_ref, q_ref, k_ref, v_ref):
    heads = range(GDN_H)
    sls = [slice(h * GDN_D, (h + 1) * GDN_D) for h in heads]
    rows = [slice(c * CHUNK, (c + 1) * CHUNK) for c in range(G)]
    pairs = [(c, h) for c in range(G) for h in heads]
    P = lambda f: {p: f(*p) for p in pairs}
    gs = [gs_ref[rows[c], :] for c in range(G)]
    gates = [_gdn_gates(gs[c]) for c in range(G)]
    tril, strict = gates[0][2], gates[0][3]
    d = {"q": P(lambda c, h: q_ref[rows[c], sls[h]]), "k": P(lambda c, h: k_ref[rows[c], sls[h]]),
         "v": P(lambda c, h: v_ref[rows[c], sls[h]]), "beta": P(lambda c, h: gs[c][:, GDN_H + h:GDN_H + h + 1])}
    gg = P(lambda c, h: _gdn_decay(gates[c][0], gates[c][1], tril, h))
    d["g"] = {p: x[0] for p, x in gg.items()}
    d["gam"] = {p: x[1] for p, x in gg.items()}
    d["eg"] = P(lambda c, h: jnp.exp(d["g"][c, h]))
    d["egl"] = P(lambda c, h: jnp.exp(d["g"][c, h][CHUNK - 1:CHUNK, :]))
    d["e"] = P(lambda c, h: jnp.exp(d["g"][c, h][CHUNK - 1:CHUNK, :] - d["g"][c, h]))
    d["kb"] = P(lambda c, h: d["k"][c, h] * d["beta"][c, h])
    d["kbg"] = P(lambda c, h: d["kb"][c, h] * d["eg"][c, h])
    d["vb"] = P(lambda c, h: d["v"][c, h] * d["beta"][c, h])
    d["qg"] = P(lambda c, h: d["q"][c, h] * d["eg"][c, h])
    d["kd"] = P(lambda c, h: d["k"][c, h] * d["e"][c, h])
    d["p"] = P(lambda c, h: _dnt(d["q"][c, h], d["k"][c, h]) * d["gam"][c, h])
    return d, heads, sls, rows, pairs, P, tril, strict


def _gdn_chunk_bwd(qkv, gsm, do, s_in, t_in):
    Lp = qkv.shape[0]
    N = Lp // CHUNK
    G, nb, col, gate, st = _gdn_chunk_specs(N, True)

    def chain(q_ref, k_ref, v_ref, gs_ref, do_ref, sin_ref, t_ref, dvn_ref, dkd_ref, vn_ref, sds_ref, dS):
        n = pl.program_id(0)

        @pl.when(n == 0)
        def _():
            dS[...] = jnp.zeros_like(dS)

        d, heads, sls, rows, pairs, P, tril, strict = _gdn_bwd_common(G, gs_ref, q_ref, k_ref, v_ref)
        lane = lax.broadcasted_iota(jnp.int32, (CHUNK, LANES), 1)
        dov = P(lambda c, h: do_ref[rows[c], sls[h]])
        s0 = P(lambda c, h: sin_ref[h, c])
        t = P(lambda c, h: t_ref[h, c])
        u = P(lambda c, h: _d(t[c, h], d["vb"][c, h]))
        w = P(lambda c, h: _d(t[c, h], d["kbg"][c, h]))
        qgdo = P(lambda c, h: _dtn(d["qg"][c, h], dov[c, h]))
        ptdo = P(lambda c, h: _dtn(d["p"][c, h], dov[c, h]))
        vnew = P(lambda c, h: u[c, h] - _d(w[c, h], s0[c, h]))
        cur = [dS[h] for h in heads]
        for c in reversed(range(G)):
            sds_all = jnp.zeros((CHUNK, LANES), F32)
            for h in heads:
                dvnew = ptdo[c, h] + _d(d["kd"][c, h], cur[h])
                dvn_ref[rows[c], sls[h]] = dvnew
                dkd_ref[rows[c], sls[h]] = _dnt(vnew[c, h], cur[h])
                vn_ref[rows[c], sls[h]] = vnew[c, h].astype(BF16)
                sds_all = sds_all + jnp.where(lane == h, _allsum(s0[c, h] * cur[h]), 0.0)
                cur[h] = qgdo[c, h] + d["egl"][c, h] * cur[h] - _dtn(w[c, h], dvnew)
            sds_ref[rows[c], :] = sds_all
        for h in heads:
            dS[h] = cur[h]

    dvn, dkd_a, vn, sds_a = pl.pallas_call(
        chain, grid=(nb,),
        in_specs=[col(0), col(1), col(2), gate, col(0), st(GDN_D, GDN_D), st(CHUNK, CHUNK)],
        out_specs=[col(0), col(0), col(0), gate],
        out_shape=[jax.ShapeDtypeStruct((Lp, D_MODEL), F32), jax.ShapeDtypeStruct((Lp, D_MODEL), F32),
                   jax.ShapeDtypeStruct((Lp, D_MODEL), BF16), jax.ShapeDtypeStruct((Lp, LANES), F32)],
        scratch_shapes=[pltpu.VMEM((GDN_H, GDN_D, GDN_D), F32)],
        name="gdn_chunk_bwd_chain")(qkv, qkv, qkv, gsm, do, s_in, t_in)

    def tail(q_ref, k_ref, v_ref, gs_ref, do_ref, sin_ref, t_ref, dvn_ref, dkdi_ref, vn_ref, sds_ref,
             dq_ref, dk_ref, dv_ref, dgs_ref):
        d, heads, sls, rows, pairs, P, tril, strict = _gdn_bwd_common(G, gs_ref, q_ref, k_ref, v_ref)
        q, k, v, beta, gam, eg, egl, e, kb, kbg, vb, qg, kd, p = (d[n_] for n_ in (
            "q", "k", "v", "beta", "gam", "eg", "egl", "e", "kb", "kbg", "vb", "qg", "kd", "p"))
        lane = lax.broadcasted_iota(jnp.int32, (CHUNK, LANES), 1)
        rcol = lax.broadcasted_iota(jnp.int32, (CHUNK, 1), 0)
        ri = lax.broadcasted_iota(jnp.int32, (CHUNK, CHUNK), 0)
        ci = lax.broadcasted_iota(jnp.int32, (CHUNK, CHUNK), 1)
        ones = jnp.ones((CHUNK, LANES), F32)
        dov = P(lambda c, h: do_ref[rows[c], sls[h]])
        s0 = P(lambda c, h: sin_ref[h, c])
        t = P(lambda c, h: t_ref[h, c])
        dvnew = P(lambda c, h: dvn_ref[rows[c], sls[h]])
        dkd = P(lambda c, h: dkdi_ref[rows[c], sls[h]])
        vnew = P(lambda c, h: vn_ref[rows[c], sls[h]])
        m = P(lambda c, h: jnp.where(strict, _dnt(kb[c, h], k[c, h]) * gam[c, h], 0.0))
        dqg = P(lambda c, h: _dnt(dov[c, h], s0[c, h]))
        dp = P(lambda c, h: jnp.where(tril, _dnt(dov[c, h], vnew[c, h]), 0.0))
        dw = P(lambda c, h: -_dnt(dvnew[c, h], s0[c, h]))
        dvb = P(lambda c, h: _dtn(t[c, h], dvnew[c, h]))
        dkbg = P(lambda c, h: _dtn(t[c, h], dw[c, h]))
        dt = P(lambda c, h: _dnt(dvnew[c, h], vb[c, h]) + _dnt(dw[c, h], kbg[c, h]))
        x1 = P(lambda c, h: _dtn(t[c, h], dt[c, h]))
        dm = P(lambda c, h: jnp.where(strict, -_dnt(x1[c, h], t[c, h]), 0.0))
        dkk = P(lambda c, h: dm[c, h] * gam[c, h])
        dqk = P(lambda c, h: dp[c, h] * gam[c, h])
        dkb = P(lambda c, h: _d(dkk[c, h], k[c, h]) + eg[c, h] * dkbg[c, h])
        em = P(lambda c, h: dm[c, h] * m[c, h] + dp[c, h] * p[c, h])
        colsum = P(lambda c, h: _d2x(em[c, h], ones, _TN)[:, 0:1])
        for c, h in pairs:
            dk_ref[rows[c], sls[h]] = (_dtn(dkk[c, h], kb[c, h]) + _dtn(dqk[c, h], q[c, h]) + dkd[c, h] * e[c, h]
                                       + beta[c, h] * dkb[c, h])
            dq_ref[rows[c], sls[h]] = _d(dqk[c, h], k[c, h]) + dqg[c, h] * eg[c, h]
            dv_ref[rows[c], sls[h]] = beta[c, h] * dvb[c, h]
        for c in range(G):
            sdsv = sds_ref[rows[c], :]
            dg_all = jnp.zeros((CHUNK, LANES), F32)
            dbeta_all = jnp.zeros((CHUNK, LANES), F32)
            for h in heads:
                dbeta = _rowsum(k[c, h] * dkb[c, h]) + _rowsum(v[c, h] * dvb[c, h])
                z = _rowsum(kd[c, h] * dkd[c, h])
                dg = (_rowsum(em[c, h]) - colsum[c, h] + _rowsum(qg[c, h] * dqg[c, h]) + _rowsum(kbg[c, h] * dkbg[c, h])
                      - z)
                extra = _allsum(z) + egl[c, h] * sdsv[0:1, h:h + 1]
                dg = dg + jnp.where(rcol == CHUNK - 1, extra, 0.0)
                dg_all = dg_all + jnp.where(lane == h, dg, 0.0)
                dbeta_all = dbeta_all + jnp.where(lane == GDN_H + h, dbeta, 0.0)
            dgs_ref[rows[c], :] = _dx((ci >= ri).astype(F32), dg_all) + dbeta_all

    return pl.pallas_call(
        tail, grid=(nb,),
        in_specs=[col(0), col(1), col(2), gate, col(0), st(GDN_D, GDN_D), st(CHUNK, CHUNK), col(0), col(0), col(0), gate],
        out_specs=[col(0), col(0), col(0), gate],
        out_shape=[jax.ShapeDtypeStruct((Lp, D_MODEL), F32)] * 3 + [jax.ShapeDtypeStruct((Lp, LANES), F32)],
        name="gdn_chunk_bwd_tail")(qkv, qkv, qkv, gsm, do, s_in, t_in, dvn, dkd_a, vn, sds_a)


def _rot(x, c, s):
    half = RET_D // 2
    x1 = x[:, :half]
    x2 = x[:, half:]
    return jnp.concatenate([x1 * c - x2 * s, x2 * c + x1 * s], axis=1)


def _rot_bwd(d, c, s):
    half = RET_D // 2
    d1 = d[:, :half]
    d2 = d[:, half:]
    return jnp.concatenate([d1 * c + d2 * s, d2 * c - d1 * s], axis=1)


def _ret_tables():
    hh = jnp.arange(RET_H, dtype=F32)
    lg = jnp.log(1.0 - 2.0 ** (-5.0 - hh))
    idx = jnp.arange(CHUNK, dtype=F32)
    tril = jnp.asarray(np.tril(np.ones((CHUNK, CHUNK), dtype=bool)))
    dmask = jnp.where(tril, jnp.exp((idx[:, None] - idx[None, :]) * lg[:, None, None]), 0.0)
    qdec = jnp.exp((idx[None, :] + 1.0) * lg[:, None])
    kdec = jnp.exp((CHUNK - 1.0 - idx[None, :]) * lg[:, None])
    gch = jnp.exp(CHUNK * lg)
    qdec = jnp.broadcast_to(qdec[:, :, None], (RET_H, CHUNK, RET_D))
    kdec = jnp.broadcast_to(kdec[:, :, None], (RET_H, CHUNK, RET_D))
    gch = jnp.broadcast_to(gch[:, None, None], (RET_H, 8, LANES))
    return dmask, qdec, kdec, gch


def _ret_specs(N, rev):
    G = _group(N)
    nb = N // G
    cn = (lambda n: nb - 1 - n) if rev else (lambda n: n)
    col = lambda j: pl.BlockSpec((G * CHUNK, D_MODEL), lambda n: (cn(n), j))
    tab = lambda a, b: pl.BlockSpec((RET_H, a, b), lambda n: (0, 0, 0))
    rope = pl.BlockSpec((G * CHUNK, LANES), lambda n: (cn(n), 0))
    st = pl.BlockSpec((RET_H, G, RET_D, RET_D), lambda n: (0, cn(n), 0, 0))
    return G, nb, col, tab, rope, st


def _ret_chunk_fwd(proj_m, cos, sin, tables):
    Lp = proj_m.shape[0]
    N = Lp // CHUNK
    dmask, qdec, kdec, gch = tables
    G, nb, col, tab, rope, st = _ret_specs(N, False)

    def body(q_ref, k_ref, v_ref, c_ref, s_ref, dm_ref, qd_ref, kd_ref, g_ref, o_ref, sin_ref, S):
        n = pl.program_id(0)

        @pl.when(n == 0)
        def _():
            S[...] = jnp.zeros_like(S)

        heads = range(RET_H)
        sls = [slice(h * RET_D, (h + 1) * RET_D) for h in heads]
        rows = [slice(c * CHUNK, (c + 1) * CHUNK) for c in range(G)]
        pairs = [(c, h) for c in range(G) for h in heads]
        P = lambda f: {p: f(*p) for p in pairs}
        qr = P(lambda c, h: _rot(q_ref[rows[c], sls[h]], c_ref[rows[c], :], s_ref[rows[c], :]))
        ks = P(lambda c, h: _rot(k_ref[rows[c], sls[h]], c_ref[rows[c], :], s_ref[rows[c], :]) * (RET_D ** -0.5))
        v = P(lambda c, h: v_ref[rows[c], sls[h]])
        a = P(lambda c, h: _dnt(qr[c, h], ks[c, h]) * dm_ref[h])
        av = P(lambda c, h: _d(a[c, h], v[c, h]))
        kv = P(lambda c, h: _dtn(ks[c, h] * kd_ref[h], v[c, h]))
        qd = P(lambda c, h: qr[c, h] * qd_ref[h])
        cur = [S[h] for h in heads]
        for c in range(G):
            for h in heads:
                o_ref[rows[c], sls[h]] = av[c, h] + _d(qd[c, h], cur[h])
                sin_ref[h, c] = cur[h].astype(BF16)
            cur = [cur[h] * g_ref[h, 0:1, 0:1] + kv[c, h] for h in heads]
        for h in heads:
            S[h] = cur[h]

    return pl.pallas_call(
        body, grid=(nb,),
        in_specs=[col(3), col(4), col(5), rope, rope,
                  tab(CHUNK, CHUNK), tab(CHUNK, RET_D), tab(CHUNK, RET_D), tab(8, LANES)],
        out_specs=[col(0), st],
        out_shape=[jax.ShapeDtypeStruct((Lp, D_MODEL), F32), jax.ShapeDtypeStruct((RET_H, N, RET_D, RET_D), BF16)],
        scratch_shapes=[pltpu.VMEM((RET_H, RET_D, RET_D), F32)],
        name="ret_chunk_fwd")(proj_m, proj_m, proj_m, cos, sin, dmask, qdec, kdec, gch)


def _ret_chunk_bwd(proj_m, cos, sin, tables, do, s_in):
    Lp = proj_m.shape[0]
    N = Lp // CHUNK
    dmask, qdec, kdec, gch = tables
    G, nb, col, tab, rope, st = _ret_specs(N, True)

    def body(q_ref, k_ref, v_ref, c_ref, s_ref, dm_ref, qd_ref, kd_ref, g_ref, do_ref, sin_ref,
             d_ref, dS):
        n = pl.program_id(0)

        @pl.when(n == 0)
        def _():
            dS[...] = jnp.zeros_like(dS)

        kscale = RET_D ** -0.5
        heads = range(RET_H)
        sls = [slice(h * RET_D, (h + 1) * RET_D) for h in heads]
        rows = [slice(c * CHUNK, (c + 1) * CHUNK) for c in range(G)]
        pairs = [(c, h) for c in range(G) for h in heads]
        P = lambda f: {p: f(*p) for p in pairs}
        cs = [(c_ref[rows[c], :], s_ref[rows[c], :]) for c in range(G)]
        osl = lambda part, h: slice(part * D_MODEL + h * RET_D, part * D_MODEL + (h + 1) * RET_D)
        qr = P(lambda c, h: _rot(q_ref[rows[c], sls[h]], *cs[c]))
        ks = P(lambda c, h: _rot(k_ref[rows[c], sls[h]], *cs[c]) * kscale)
        v = P(lambda c, h: v_ref[rows[c], sls[h]])
        dov = P(lambda c, h: do_ref[rows[c], sls[h]])
        ad = P(lambda c, h: _dnt(qr[c, h], ks[c, h]) * dm_ref[h])
        da = P(lambda c, h: _dnt(dov[c, h], v[c, h]) * dm_ref[h])
        dos = P(lambda c, h: _dnt(dov[c, h], sin_ref[h, c]) * qd_ref[h])
        qdo = P(lambda c, h: _dtn(qr[c, h] * qd_ref[h], dov[c, h]))
        adv = P(lambda c, h: _dtn(ad[c, h], dov[c, h]))
        dqr = P(lambda c, h: _d(da[c, h], ks[c, h]) + dos[c, h])
        daq = P(lambda c, h: _dtn(da[c, h], qr[c, h]))
        kk = P(lambda c, h: ks[c, h] * kd_ref[h])
        cur = [dS[h] for h in heads]
        for c in reversed(range(G)):
            for h in heads:
                d_ref[rows[c], osl(2, h)] = (adv[c, h] + _d(kk[c, h], cur[h])).astype(BF16)
                d_ref[rows[c], osl(0, h)] = _rot_bwd(dqr[c, h], *cs[c]).astype(BF16)
                dks = daq[c, h] + _dnt(v[c, h], cur[h]) * kd_ref[h]
                d_ref[rows[c], osl(1, h)] = _rot_bwd(dks * kscale, *cs[c]).astype(BF16)
            cur = [cur[h] * g_ref[h, 0:1, 0:1] + qdo[c, h] for h in heads]
        for h in heads:
            dS[h] = cur[h]

    return pl.pallas_call(
        body, grid=(nb,),
        in_specs=[col(3), col(4), col(5), rope, rope,
                  tab(CHUNK, CHUNK), tab(CHUNK, RET_D), tab(CHUNK, RET_D), tab(8, LANES), col(0), st],
        out_specs=pl.BlockSpec((G * CHUNK, 3 * D_MODEL), lambda n: (nb - 1 - n, 0)),
        out_shape=jax.ShapeDtypeStruct((Lp, 3 * D_MODEL), BF16),
        scratch_shapes=[pltpu.VMEM((RET_H, RET_D, RET_D), F32)],
        name="ret_chunk_bwd")(proj_m, proj_m, proj_m, cos, sin, dmask, qdec, kdec, gch, do, s_in)


def _merge_specs(tr):
    col = lambda j: pl.BlockSpec((tr, D_MODEL), lambda i: (i, j))
    return col


def _merge_fwd(o_a, o_b, proj_m, gnorm, out_proj=None):
    Lp = o_a.shape[0]
    tr = _tile(Lp, 192 if out_proj is None else 352, 16)
    if out_proj is not None:
        w_out, res, g2 = out_proj
        r_args, r_specs = _rows_operands(res, tr)

    def body(oa_ref, ob_ref, gz_ref, rg_ref, ga_ref, gb_ref, gn_ref, *rest):
        y_ref = rest[0] if out_proj is None else rest[-3]
        gn = gn_ref[...]
        oa = oa_ref[...]
        ob = ob_ref[...]
        gz = gz_ref[...]
        ya = []
        for j in range(GDN_H):
            seg = oa[:, j * GDN_D:(j + 1) * GDN_D]
            r = lax.rsqrt(jnp.mean(seg * seg, axis=-1, keepdims=True) + EPS)
            ya.append(seg * r * gn)
        ya = jnp.concatenate(ya, axis=1) * (gz * _sig(gz))
        yb = []
        for j in range(RET_H):
            seg = ob[:, j * RET_D:(j + 1) * RET_D]
            r = lax.rsqrt(jnp.mean(seg * seg, axis=-1, keepdims=True) + EPS)
            yb.append(seg * r)
        rg = rg_ref[...]
        yb = jnp.concatenate(yb, axis=1) * (rg * _sig(rg))
        yv = (_sig(ga_ref[...]) * ya + _sig(gb_ref[...]) * yb).astype(BF16)
        y_ref[...] = yv
        if out_proj is not None:
            wo_ref, g2_ref = rest[0], rest[1]
            h1_ref, hn2_ref = rest[-2], rest[-1]
            h1 = (jnp.dot(yv, wo_ref[...], preferred_element_type=F32)
                  + _rows_tile(res, rest[2:2 + len(r_args)], pl.program_id(0), tr))
            r = lax.rsqrt(jnp.mean(h1 * h1, axis=-1, keepdims=True) + EPS)
            h1_ref[...] = h1
            hn2_ref[...] = (h1 * r * g2_ref[...]).astype(BF16)

    col = _merge_specs(tr)
    in_specs = [col(0), col(0), col(6), col(7), col(8), col(9), pl.BlockSpec((1, GDN_D), lambda i: (0, 0))]
    args = [o_a, o_b, proj_m, proj_m, proj_m, proj_m, gnorm]
    if out_proj is None:
        return pl.pallas_call(body, grid=(Lp // tr,), in_specs=in_specs, out_specs=col(0),
                              out_shape=jax.ShapeDtypeStruct((Lp, D_MODEL), BF16), name="merge_fwd")(*args)
    in_specs += [pl.BlockSpec((D_MODEL, D_MODEL), lambda i: (0, 0), pipeline_mode=pl.Buffered(1)),
                 pl.BlockSpec((1, D_MODEL), lambda i: (0, 0))] + r_specs
    return pl.pallas_call(
        body, grid=(Lp // tr,), in_specs=in_specs, out_specs=[col(0), col(0), col(0)],
        out_shape=[jax.ShapeDtypeStruct((Lp, D_MODEL), BF16), jax.ShapeDtypeStruct((Lp, D_MODEL), F32),
                   jax.ShapeDtypeStruct((Lp, D_MODEL), BF16)],
        name="merge_out_proj_rms2")(*args, w_out, g2, *r_args)


def _merge_bwd(dh1b, w_out, o_a, o_b, proj_m, gnorm):
    Lp = o_a.shape[0]
    tr = _tile(Lp, 192, 16)

    def body(d_ref, wo_ref, oa_ref, ob_ref, gz_ref, rg_ref, ga_ref, gb_ref, gn_ref, dc_ref, doa_ref, dob_ref, dgn_ref):
        i = pl.program_id(0)
        gn = gn_ref[...]
        dyv = lax.dot_general(d_ref[...], wo_ref[...], _NT, preferred_element_type=F32)
        oa = oa_ref[...]
        ob = ob_ref[...]
        gz = gz_ref[...]
        rg = rg_ref[...]
        sa = _sig(ga_ref[...])
        sb = _sig(gb_ref[...])
        dya = dyv * sa
        dyb = dyv * sb
        sgz = _sig(gz)
        szz = gz * sgz
        dgn = jnp.zeros((1, GDN_D), F32)
        ya = []
        dgz = []
        for j in range(GDN_H):
            sl = slice(j * GDN_D, (j + 1) * GDN_D)
            seg = oa[:, sl]
            r = lax.rsqrt(jnp.mean(seg * seg, axis=-1, keepdims=True) + EPS)
            xh = seg * r
            oan = xh * gn
            ya.append(oan * szz[:, sl])
            dgz.append(dya[:, sl] * oan * (sgz[:, sl] * (1.0 + gz[:, sl] * (1.0 - sgz[:, sl]))))
            doan = dya[:, sl] * szz[:, sl]
            dgn = dgn + jnp.sum(doan * xh, axis=0, keepdims=True)
            dxh = doan * gn
            doa_ref[:, sl] = r * (dxh - xh * jnp.mean(dxh * xh, axis=-1, keepdims=True))
        ya = jnp.concatenate(ya, axis=1)
        srg = _sig(rg)
        srr = rg * srg
        yb = []
        drg = []
        for j in range(RET_H):
            sl = slice(j * RET_D, (j + 1) * RET_D)
            seg = ob[:, sl]
            r = lax.rsqrt(jnp.mean(seg * seg, axis=-1, keepdims=True) + EPS)
            xh = seg * r
            yb.append(xh * srr[:, sl])
            drg.append(dyb[:, sl] * xh * (srg[:, sl] * (1.0 + rg[:, sl] * (1.0 - srg[:, sl]))))
            dxh = dyb[:, sl] * srr[:, sl]
            dob_ref[:, sl] = r * (dxh - xh * jnp.mean(dxh * xh, axis=-1, keepdims=True))
        yb = jnp.concatenate(yb, axis=1)
        dc_ref[:, 0:D_MODEL] = jnp.concatenate(dgz, axis=1).astype(BF16)
        dc_ref[:, D_MODEL:2 * D_MODEL] = jnp.concatenate(drg, axis=1).astype(BF16)
        dc_ref[:, 2 * D_MODEL:3 * D_MODEL] = (dyv * ya * sa * (1.0 - sa)).astype(BF16)
        dc_ref[:, 3 * D_MODEL:] = (dyv * yb * sb * (1.0 - sb)).astype(BF16)

        @pl.when(i == 0)
        def _():
            dgn_ref[...] = dgn

        @pl.when(i > 0)
        def _():
            dgn_ref[...] += dgn

    col = _merge_specs(tr)
    return pl.pallas_call(
        body, grid=(Lp // tr,),
        in_specs=[col(0), pl.BlockSpec((D_MODEL, D_MODEL), lambda i: (0, 0), pipeline_mode=pl.Buffered(1)),
                  col(0), col(0), col(6), col(7), col(8), col(9), pl.BlockSpec((1, GDN_D), lambda i: (0, 0))],
        out_specs=[pl.BlockSpec((tr, 4 * D_MODEL), lambda i: (i, 0)), col(0), col(0),
                   pl.BlockSpec((1, GDN_D), lambda i: (0, 0))],
        out_shape=[jax.ShapeDtypeStruct((Lp, 4 * D_MODEL), BF16), jax.ShapeDtypeStruct((Lp, D_MODEL), F32),
                   jax.ShapeDtypeStruct((Lp, D_MODEL), F32), jax.ShapeDtypeStruct((1, GDN_D), F32)],
        name="merge_bwd")(dh1b, w_out, o_a, o_b, proj_m, proj_m, proj_m, proj_m, gnorm)


def _ffn_act(up, conv_w, conv_b):
    Lp = up.shape[0]
    tr = _tile(Lp, 192, 16)
    W2 = 2 * D_FF

    def body(main_ref, prev_ref, w_ref, b_ref, act_ref, u_ref):
        i = pl.program_id(0)
        prev = jnp.where(i > 0, prev_ref[...], 0.0)
        ext = jnp.concatenate([prev, main_ref[...]], axis=0)
        u = _taps(_shifted(ext, range(8 - (FFN_CONV - 1), 9)), w_ref[...], tr, b_ref[...])
        a = u[:, :D_FF]
        act_ref[...] = (a * _sig(a) * u[:, D_FF:]).astype(BF16)
        u_ref[...] = u.astype(BF16)

    return pl.pallas_call(
        body, grid=(Lp // tr,),
        in_specs=[pl.BlockSpec((tr, W2), lambda i: (i, 0)), _halo_prev(tr, W2),
                  pl.BlockSpec((FFN_CONV, W2), lambda i: (0, 0)), pl.BlockSpec((1, W2), lambda i: (0, 0))],
        out_specs=[pl.BlockSpec((tr, D_FF), lambda i: (i, 0)), pl.BlockSpec((tr, W2), lambda i: (i, 0))],
        out_shape=[jax.ShapeDtypeStruct((Lp, D_FF), BF16), jax.ShapeDtypeStruct((Lp, W2), BF16)],
        name="ffn_act")(up, up, conv_w, conv_b)


def _ffn_act_bwd(up, u, dact, conv_w):
    Lp = up.shape[0]
    tr = _tile(Lp, 192, 16)
    W2 = 2 * D_FF
    te = tr + 8

    def body(up_ref, u_ref, un_ref, da_ref, dan_ref, w_ref, dup_ref, acc_ref):
        i = pl.program_id(0)
        w = w_ref[...]
        ue = jnp.concatenate([u_ref[...].astype(F32), un_ref[...].astype(F32)[0:8]], axis=0)
        a = ue[:, :D_FF]
        b = ue[:, D_FF:]
        rowe = i * tr + lax.broadcasted_iota(jnp.int32, (te, 1), 0)
        dae = jnp.where(rowe < Lp, jnp.concatenate([da_ref[...], dan_ref[...]], axis=0), 0.0)
        sg = _sig(a)
        du = jnp.concatenate([dae * b * (sg * (1.0 + a * (1.0 - sg))), dae * (a * sg)], axis=1)
        dus = _shifted(du, range(FFN_CONV - 1, -1, -1))
        dup_ref[...] = _taps(dus, w, tr).astype(BF16)
        upm = up_ref[...]
        rows = [jnp.sum(dus[kk][0:tr, :] * upm, axis=0, keepdims=True) for kk in range(FFN_CONV)]
        rows.append(jnp.sum(du[0:tr, :], axis=0, keepdims=True))
        part = jnp.concatenate(rows + [jnp.zeros((8 - len(rows), W2), F32)], axis=0)

        @pl.when(i == 0)
        def _():
            acc_ref[...] = part

        @pl.when(i > 0)
        def _():
            acc_ref[...] += part

    return pl.pallas_call(
        body, grid=(Lp // tr,),
        in_specs=[pl.BlockSpec((tr, W2), lambda i: (i, 0)), pl.BlockSpec((tr, W2), lambda i: (i, 0)),
                  _halo_next(tr, W2, Lp, rows=16), pl.BlockSpec((tr, D_FF), lambda i: (i, 0)), _halo_next(tr, D_FF, Lp),
                  pl.BlockSpec((FFN_CONV, W2), lambda i: (0, 0))],
        out_specs=[pl.BlockSpec((tr, W2), lambda i: (i, 0)), pl.BlockSpec((8, W2), lambda i: (0, 0))],
        out_shape=[jax.ShapeDtypeStruct((Lp, W2), BF16), jax.ShapeDtypeStruct((8, W2), F32)],
        name="ffn_act_bwd")(up, u, u, dact, dact, conv_w)


def _proj_rows(j):
    shift = (jnp.where((j >= 3) & (j < 6), _O_RQ - 3 * D_MODEL, 0) + jnp.where(j == 6, _O_GZ - 6 * D_MODEL, 0)
             + jnp.where(j >= 7, _O_RG - 7 * D_MODEL, 0))
    return j * D_MODEL + shift


def _local_step(hpad, tgt, pad, wt, first_weights=None, late_weights=None, on_ffn_out_grads=None,
                on_w_in_grads=None):
    Lp = hpad.shape[0]
    first = pad + N_META
    pos = jnp.arange(Lp, dtype=F32) - float(pad)
    half = RET_D // 2
    inv = 1.0 / (ROPE_BASE ** (jnp.arange(half, dtype=F32) / half))
    ang = pos[:, None] * inv[None, :]
    cos, sin = jnp.cos(ang), jnp.sin(ang)
    tables = _ret_tables()
    gparams = jnp.zeros((8, LANES), F32).at[0, :GDN_H].set(wt["a_log"]).at[1, :GDN_H].set(wt["dt_bias"])

    hn1 = _rms_fwd(hpad, wt["norm1"], "rms1_fwd")
    if first_weights is not None:
        wt = {**wt, **first_weights(hn1[:8, :LANES].astype(F32) + cos[:8] + sin[:8])}
    w_in_t = wt["w_in_t"]
    w_small_t = jnp.pad(w_in_t[_O_GA:_O_RQ], ((0, LANES - 2 * GDN_H), (0, 0)))
    proj_m = _mm_nn(hn1, w_in_t, bt=True, tm_target=2752, b_rows=(D_MODEL, MAIN_W // D_MODEL, _proj_rows),
                    name="proj_main")
    proj_s = _mm_nn(hn1, w_small_t, bt=True, name="proj_small")
    qkv, gsm, conv_out = _gdn_pre(proj_m, proj_s, wt["gdn_conv_w"], gparams, pad)
    o_a, s_a, t_a = _gdn_chunk_fwd(qkv, gsm)
    o_b, s_b = _ret_chunk_fwd(proj_m, cos, sin, tables)
    if late_weights is not None:
        wt = {**wt, **late_weights(o_b)}
    y, h1, hn2 = _merge_fwd(o_a, o_b, proj_m, wt["gdn_norm"], (wt["w_out"], hpad, wt["norm2"]))
    up = _mm_nn(hn2, wt["w_up_t"], bt=True, name="ffn_up")
    act, u_ffn = _ffn_act(up, wt["ffn_conv_w"], wt["ffn_conv_b"])
    lossvec, dh2, dh2b, d_norm_f = _final(_Producer(act, wt["w_down"], h1), wt["norm_f"], tgt, first)

    d_w_down = _mm_tn(act, dh2b, name="dw_down")
    dact = _mm_nt(dh2b, wt["w_down"], name="d_act")
    dup, ffn_rows = _ffn_act_bwd(up, u_ffn, dact, wt["ffn_conv_w"])
    d_w_up_t = _mm_tn(dup, hn2, name="dw_up")
    dh1, dh1b, d_norm2 = _rms_bwd(h1, wt["norm2"], _Producer(dup, wt["w_up_t"]), dh2, pad, "d_hn2_rms2_bwd")

    d_w_out = _mm_tn(y, dh1b, name="dw_out")
    gnorm = wt["gdn_norm"]
    if on_ffn_out_grads is not None:
        gnorm = gnorm + on_ffn_out_grads(d_w_down, d_w_up_t, d_w_out)[0:1, :]
    d_c, do_a, do_b, d_gnorm = _merge_bwd(dh1b, wt["w_out"], o_a, o_b, proj_m, gnorm)
    d_r = _ret_chunk_bwd(proj_m, cos, sin, tables, do_b, s_b)
    dq, dk, dv, dgs = _gdn_chunk_bwd(qkv, gsm, do_a, s_a, t_a)
    d_a, d_s, conv_rows, gp_rows = _gdn_pre_bwd(proj_m, conv_out, proj_s, wt["gdn_conv_w"], gparams, dq, dk, dv, dgs,
                                                pad)

    segs = [(d_a, w_in_t[_O_GQ:_O_GZ]), (d_r, w_in_t[_O_RQ:_O_RG]),
            (d_c, jnp.concatenate([w_in_t[_O_GZ:_O_GA], w_in_t[_O_RG:_O_END]], axis=0))]
    pa, pr, pc = [_mm_tn(d, hn1, BF16, name="dw_in_%d" % i) for i, (d, _) in enumerate(segs)]
    ps = _mm_tn(d_s, hn1, BF16, name="dw_in_small")
    d_w_in_t = jnp.concatenate([pa, pc[:D_MODEL], ps[:2 * GDN_H], pr, pc[D_MODEL:]], axis=0)
    if on_w_in_grads is not None:
        w_small_t = w_small_t + on_w_in_grads(d_w_in_t)[0:1, 0:1].astype(w_small_t.dtype)
    dhn1 = _mm_sum([(d_s, w_small_t)] + segs[:-1], "d_hn1_first")
    dh0, _, d_norm1 = _rms_bwd(hpad, wt["norm1"], _Producer(*segs[-1], dhn1), dh1, pad, "d_hn1_rms1_bwd")

    grads = {
        "norm1": d_norm1, "w_in_t": d_w_in_t, "gdn_conv_w": conv_rows[:GDN_CONV],
        "a_log": gp_rows[0, :GDN_H], "dt_bias": gp_rows[1, :GDN_H], "gdn_norm": d_gnorm, "w_out": d_w_out,
        "norm2": d_norm2, "w_up_t": d_w_up_t, "ffn_conv_w": ffn_rows[:FFN_CONV],
        "ffn_conv_b": ffn_rows[FFN_CONV:FFN_CONV + 1], "w_down": d_w_down, "norm_f": d_norm_f,
    }
    return lossvec, dh0, grads


def _peer(k):
    ix, iy, ic = lax.axis_index("x"), lax.axis_index("y"), lax.axis_index("c")
    px = 1 - ix if (k >> 2) & 1 else ix
    py = 1 - iy if (k >> 1) & 1 else iy
    pc = 1 - ic if k & 1 else ic
    return (px, py, pc), 4 * px + 2 * py + pc


def _comm_call(body, n, out_shapes, name, args):
    hbm = pl.BlockSpec(memory_space=pl.ANY)
    return pl.pallas_call(
        body, out_shape=out_shapes, in_specs=[hbm] * n, out_specs=[hbm] * n,
        scratch_shapes=[pltpu.SemaphoreType.DMA((n, N_DEV - 1)), pltpu.SemaphoreType.DMA((n, N_DEV - 1)),
                        pltpu.SemaphoreType.DMA((n,))],
        name=name)(*args)


def _all_gather(xs, name):
    n = len(xs)

    def body(*refs):
        x_refs, out_refs = refs[:n], refs[n:2 * n]
        send_sems, recv_sems, local_sems = refs[2 * n:]
        _, me = _peer(0)
        pending = []
        for i in range(n):
            local = pltpu.make_async_copy(x_refs[i], out_refs[i].at[me], local_sems.at[i])
            local.start()
            pending.append(local)
        sends = []
        for i in range(n):
            for k in range(1, N_DEV):
                dev, _ = _peer(k)
                cp = pltpu.make_async_remote_copy(
                    src_ref=x_refs[i], dst_ref=out_refs[i].at[me], send_sem=send_sems.at[i, k - 1],
                    recv_sem=recv_sems.at[i, k - 1], device_id=dev, device_id_type=MESH_T)
                cp.start()
                sends.append(cp)
        for i in range(n):
            for k in range(1, N_DEV):
                dev, idx = _peer(k)
                pltpu.make_async_remote_copy(
                    src_ref=x_refs[i], dst_ref=out_refs[i].at[idx], send_sem=send_sems.at[i, k - 1],
                    recv_sem=recv_sems.at[i, k - 1], device_id=dev, device_id_type=MESH_T).wait_recv()
        for cp in sends:
            cp.wait_send()
        for local in pending:
            local.wait()

    out_shapes = [jax.ShapeDtypeStruct((N_DEV,) + a.shape, a.dtype) for a in xs]
    return _comm_call(body, n, out_shapes, name, xs)


def _all_to_all(gs, name):
    n = len(gs)

    def body(*refs):
        g_refs, out_refs = refs[:n], refs[n:2 * n]
        send_sems, recv_sems, local_sems = refs[2 * n:]
        _, me = _peer(0)
        pending = []
        for i in range(n):
            local = pltpu.make_async_copy(g_refs[i].at[me], out_refs[i].at[0], local_sems.at[i])
            local.start()
            pending.append(local)
        sends = []
        for i in range(n):
            for k in range(1, N_DEV):
                dev, idx = _peer(k)
                cp = pltpu.make_async_remote_copy(
                    src_ref=g_refs[i].at[idx], dst_ref=out_refs[i].at[k], send_sem=send_sems.at[i, k - 1],
                    recv_sem=recv_sems.at[i, k - 1], device_id=dev, device_id_type=MESH_T)
                cp.start()
                sends.append(cp)
        for cp in sends:
            cp.wait_recv()
        for cp in sends:
            cp.wait_send()
        for local in pending:
            local.wait()

    out_shapes = [jax.ShapeDtypeStruct(g.shape, g.dtype) for g in gs]
    return _comm_call(body, n, out_shapes, name, gs)


_SPLIT_RELATIONS = {"gather": tuple(range(1, N_DEV)), "a2a": tuple(range(1, N_DEV)), "chip": (1, 2, 4, 6),
                    "forward": (2, 4, 6)}


def _split_copies(kind, src_refs, land_refs, send_sems, recv_sems, local_sems, with_recv):
    n = len(land_refs)
    rels = _SPLIT_RELATIONS[kind]
    _, me = _peer(0)
    locals_, remotes = [], []
    for i in range(n):
        if kind in ("gather", "chip"):
            locals_.append(pltpu.make_async_copy(src_refs[i], land_refs[i].at[me], local_sems.at[i]))
        elif kind == "a2a":
            locals_.append(pltpu.make_async_copy(src_refs[i].at[me], land_refs[i].at[0], local_sems.at[i]))
        for jj, k in enumerate(rels):
            dev, idx = _peer(k)
            if kind in ("gather", "chip"):
                src, dst, mine = src_refs[i], land_refs[i].at[me], land_refs[i].at[idx]
            elif kind == "a2a":
                src, dst, mine = src_refs[i].at[idx], land_refs[i].at[k], land_refs[i].at[k]
            else:
                dev, _ = _peer(1)
                _, came = _peer(k + 1)
                src, dst, mine = land_refs[i].at[idx], land_refs[i].at[idx], land_refs[i].at[came]
            j = i * len(rels) + jj
            send = pltpu.make_async_remote_copy(
                src_ref=src, dst_ref=dst, send_sem=send_sems.at[j], recv_sem=recv_sems.at[j],
                device_id=dev, device_id_type=MESH_T)
            recv = pltpu.make_async_remote_copy(
                src_ref=src, dst_ref=mine, send_sem=send_sems.at[j], recv_sem=recv_sems.at[j],
                device_id=dev, device_id_type=MESH_T) if with_recv else None
            remotes.append((send, recv))
    return locals_, remotes


_HBM = pl.BlockSpec(memory_space=pltpu.HBM)
_SEM = pl.BlockSpec(memory_space=pltpu.SEMAPHORE)
_ANY = pl.BlockSpec(memory_space=pl.ANY)


def _split_start(srcs, kind, name, after):
    n = len(srcs)
    if kind == "forward":
        arrays = list(srcs)
    else:
        gathers = kind in ("gather", "chip")
        arrays = list(srcs) + [lax.empty(((N_DEV,) + a.shape) if gathers else a.shape, a.dtype) for a in srcs]
    na = len(arrays)

    def body(*refs):
        src_refs, land_refs = refs[:n], refs[na - n:na]
        send_sems, recv_sems, local_sems = refs[na + 1:na + 4]
        token = refs[-1]
        locals_, remotes = _split_copies(kind, src_refs, land_refs, send_sems, recv_sems, local_sems, False)
        for cp in locals_:
            cp.start()
        for send, _ in remotes:
            send.start()
        token[...] = jnp.zeros_like(token)

    ncp = n * len(_SPLIT_RELATIONS[kind])
    sems = (pltpu.SemaphoreType.DMA((ncp,)), pltpu.SemaphoreType.DMA((ncp,)), pltpu.SemaphoreType.DMA((n,)))
    thru = tuple(pltpu.HBM(a.shape, a.dtype) for a in arrays)
    outs = pl.pallas_call(
        body, name=name,
        out_shape=sems + thru + (jax.ShapeDtypeStruct((8, LANES), F32),),
        in_specs=[_HBM] * na + [_ANY],
        out_specs=[_SEM] * 3 + [_HBM] * na + [pl.BlockSpec(memory_space=pltpu.VMEM)],
        input_output_aliases={i: 3 + i for i in range(na)},
        compiler_params=pltpu.CompilerParams(has_side_effects=pltpu.SideEffectType.DATAFLOW_SIDE_EFFECTING),
    )(*[pltpu.with_memory_space_constraint(a, pltpu.HBM) for a in arrays], after)
    return (kind, n, outs[:3], outs[3:3 + na]), outs[-1]


def _split_wait(handle, name, after):
    kind, n, sems, thru = handle
    na = len(thru)

    def body(*refs):
        src_refs, land_refs = refs[:n], refs[na - n:na]
        send_sems, recv_sems, local_sems = refs[na:na + 3]
        locals_, remotes = _split_copies(kind, src_refs, land_refs, send_sems, recv_sems, local_sems, True)
        for send, recv in remotes:
            send.wait_send()
            recv.wait_recv()
        for cp in locals_:
            cp.wait()

    outs = pl.pallas_call(
        body, name=name, out_shape=tuple(pltpu.HBM(a.shape, a.dtype) for a in thru),
        in_specs=[_HBM] * na + [_SEM] * 3 + [_ANY], out_specs=[_HBM] * na,
        input_output_aliases={i: i for i in range(na)},
        compiler_params=pltpu.CompilerParams(has_side_effects=pltpu.SideEffectType.DATAFLOW_SIDE_EFFECTING),
    )(*thru, *sems, after)
    return list(outs[na - n:])


def _adamw(gslabs, w, m, v, name):
    R, Cw = w.shape
    if R % 8 == 0:
        tr, tc = _tile(R, 64 if Cw > 1024 else 128, 8), Cw
    else:
        tr, tc = R, LANES
    c1 = 1.0 - ADAM_B1 ** ADAM_STEP
    c2 = 1.0 - ADAM_B2 ** ADAM_STEP

    def body(g_ref, w_ref, m_ref, v_ref, go_ref, d_ref, mo_ref, vo_ref):
        g = g_ref[0].astype(F32)
        for k in range(1, N_DEV):
            g = g + g_ref[k].astype(F32)
        mn = ADAM_B1 * m_ref[...] + (1.0 - ADAM_B1) * g
        vn = ADAM_B2 * v_ref[...] + (1.0 - ADAM_B2) * (g * g)
        m_hat = mn / c1
        v_hat = vn / c2
        go_ref[...] = g
        d_ref[...] = -ADAM_LR * (m_hat / (jnp.sqrt(v_hat) + ADAM_EPS) + ADAM_WD * w_ref[...])
        mo_ref[...] = mn
        vo_ref[...] = vn

    blk = pl.BlockSpec((tr, tc), lambda i, j: (i, j))
    return pl.pallas_call(
        body, grid=(R // tr, Cw // tc),
        in_specs=[pl.BlockSpec((N_DEV, tr, tc), lambda i, j: (0, i, j)), blk, blk, blk],
        out_specs=[blk] * 4, out_shape=[jax.ShapeDtypeStruct((R, Cw), F32)] * 4, name=name)(gslabs, w, m, v)


def _pack(arrs, row_mult, dtype=F32):
    parts = []
    total = 0
    for a in arrs:
        f = a.reshape(-1).astype(dtype)
        n = -(-f.shape[0] // 1024) * 1024
        parts.append(jnp.pad(f, (0, n - f.shape[0])))
        total += n
    rows = total // LANES
    rows_p = -(-rows // row_mult) * row_mult
    flat = jnp.concatenate(parts)
    flat = jnp.pad(flat, (0, rows_p * LANES - total))
    return flat.reshape(rows_p, LANES)


def _unpack(packed, shapes):
    lead = packed.shape[:-2]
    flat = packed.reshape(lead + (-1,))
    out = []
    off = 0
    for s in shapes:
        n = int(np.prod(s))
        out.append(flat[..., off:off + n].reshape(lead + tuple(s)))
        off += -(-n // 1024) * 1024
    return out


def _gather_cols(stacked):
    d, r, c = stacked.shape
    return stacked.transpose(1, 0, 2).reshape(r, d * c)


def _scatter_cols(full):
    r, n = full.shape
    return full.reshape(r, N_DEV, n // N_DEV).transpose(1, 0, 2)


def kernel(x, meta, norm1, w_in, gdn_conv_w, gdn_a_log, gdn_dt_bias, gdn_norm, w_out, norm2, w_ffn_up, ffn_conv_w, ffn_conv_b, w_ffn_down, norm_f, loss_target, m_meta, m_norm1, m_w_in, m_gdn_conv_w, m_gdn_a_log, m_gdn_dt_bias, m_gdn_norm, m_w_out, m_norm2, m_w_ffn_up, m_ffn_conv_w, m_ffn_conv_b, m_w_ffn_down, m_norm_f, v_meta, v_norm1, v_w_in, v_gdn_conv_w, v_gdn_a_log, v_gdn_dt_bias, v_gdn_norm, v_w_out, v_norm2, v_w_ffn_up, v_ffn_conv_w, v_ffn_conv_b, v_w_ffn_down, v_norm_f):
    S = x.shape[1]
    L = N_META + S
    pad = (-L) % CHUNK
    Lp = L + pad

    tr_ = lambda a: jnp.swapaxes(a[0], 0, 1)
    big = [tr_(w_in), w_out[0], tr_(w_ffn_up), w_ffn_down[0]]
    small = [meta, gdn_conv_w, ffn_conv_w]
    small_all, = _all_gather([_pack(small, 8)], "gather_small_weights")
    first, first_token = _split_start([big[0].astype(BF16)], "chip", "gather_w_in_start", small_all)
    late, late_token = _split_start([a.astype(BF16) for a in big[1:]], "gather", "gather_late_start", first_token)

    def first_weights(after):
        half = _split_wait(first, "gather_w_in_wait", after)
        second, second_token = _split_start(half, "forward", "gather_w_in_forward_start", after)
        w_in_s, = _split_wait(second, "gather_w_in_forward_wait", second_token)
        return {"w_in_t": w_in_s.reshape(_O_END, D_MODEL)}

    def late_weights(after):
        w_out_s, w_up_s, w_down_s = _split_wait(late, "gather_late_wait", after)
        return {"w_out": w_out_s.reshape(D_MODEL, D_MODEL), "w_up_t": w_up_s.reshape(2 * D_FF, D_MODEL),
                "w_down": w_down_s.reshape(D_FF, D_MODEL)}

    meta_s, gconv_s, fconv_s = _unpack(small_all, [a.shape for a in small])
    wt = {
        "norm1": norm1 + jnp.tile(late_token[0:1, :], (1, D_MODEL // LANES)),
        "gdn_conv_w": _gather_cols(gconv_s[:, 0]), "a_log": gdn_a_log[0], "dt_bias": gdn_dt_bias[0],
        "gdn_norm": gdn_norm, "norm2": norm2, "ffn_conv_w": _gather_cols(fconv_s[:, 0]), "ffn_conv_b": ffn_conv_b,
        "norm_f": norm_f.reshape(1, D_MODEL),
    }
    meta_f = _gather_cols(meta_s)

    pending = {}

    def on_ffn_out_grads(d_w_down, d_w_up_t, d_w_out):
        srcs = [d_w_out.reshape(N_DEV, D_MODEL // N_DEV, D_MODEL), d_w_up_t.reshape(N_DEV, 2 * D_FF // N_DEV, D_MODEL),
                d_w_down.reshape(N_DEV, D_FF // N_DEV, D_MODEL)]
        pending["ffn_out"], token = _split_start(srcs, "a2a", "exchange_ffn_out_start", d_w_out)
        return token

    def on_w_in_grads(d_w_in_t):
        slabs = d_w_in_t.astype(BF16).reshape(N_DEV, _O_END // N_DEV, D_MODEL)
        pending["w_in"], token = _split_start([slabs], "a2a", "exchange_w_in_start", d_w_in_t)
        return token

    head = jnp.concatenate([jnp.zeros((pad, D_MODEL), F32), meta_f], axis=0)
    if S >= 2 * 704:
        hpad = _Rows(x[0], pad + N_META, head)
        tgt = _Rows(loss_target[0], pad + N_META)
    else:
        hpad = jnp.concatenate([head, x[0]], axis=0)
        tgt = jnp.concatenate([jnp.zeros((pad + N_META, D_MODEL), F32), loss_target[0]], axis=0)
    lossvec, dh0, gr = _local_step(hpad, tgt, pad, wt, first_weights, late_weights, on_ffn_out_grads, on_w_in_grads)

    loss = lax.psum(jnp.sum(lossvec), ("x", "y", "c"))
    grad_x = dh0[pad + N_META:][None]

    big_m = [tr_(m_w_in), m_w_out[0], tr_(m_w_ffn_up), m_w_ffn_down[0]]
    big_v = [tr_(v_w_in), v_w_out[0], tr_(v_w_ffn_up), v_w_ffn_down[0]]
    slabs_ffn_out = _split_wait(pending["ffn_out"], "exchange_ffn_out_wait", dh0)
    big_out = [None] + [_adamw(slabs_ffn_out[i - 1], big[i], big_m[i], big_v[i], "adamw_big_%d" % i)
                        for i in range(1, len(big))]
    g_sm = [_scatter_cols(dh0[pad:pad + N_META]), _scatter_cols(gr["gdn_conv_w"]), _scatter_cols(gr["ffn_conv_w"])]
    g_small = jnp.stack([_pack([g[d] for g in g_sm], 8) for d in range(N_DEV)])
    slabs_small, = _all_to_all([g_small], "exchange_small_gradients")
    small_out = _adamw(slabs_small, _pack(small, 8), _pack([m_meta, m_gdn_conv_w, m_ffn_conv_w], 8),
                       _pack([v_meta, v_gdn_conv_w, v_ffn_conv_w], 8), "adamw_small_sharded")
    small_un = [_unpack(o, [a.shape for a in small]) for o in small_out]
    rep_w = [norm1, gdn_a_log, gdn_dt_bias, gdn_norm, norm2, ffn_conv_b, norm_f]
    rep_m = [m_norm1, m_gdn_a_log, m_gdn_dt_bias, m_gdn_norm, m_norm2, m_ffn_conv_b, m_norm_f]
    rep_v = [v_norm1, v_gdn_a_log, v_gdn_dt_bias, v_gdn_norm, v_norm2, v_ffn_conv_b, v_norm_f]
    rep_g = [gr["norm1"], gr["a_log"], gr["dt_bias"], gr["gdn_norm"], gr["norm2"], gr["ffn_conv_b"], gr["norm_f"]]
    rep_slabs, = _all_gather([_pack(rep_g, 8)], "gather_small_gradients")
    rep_out = _adamw(rep_slabs, _pack(rep_w, 8), _pack(rep_m, 8), _pack(rep_v, 8), "adamw_replicated")
    rep_shapes = [a.shape for a in rep_w]
    rp_g, rp_d, rp_nm, rp_nv = [_unpack(o, rep_shapes) for o in rep_out]

    slabs_w_in, = _split_wait(pending["w_in"], "exchange_w_in_wait", rep_out[0])
    big_out[0] = _adamw(slabs_w_in, big[0], big_m[0], big_v[0], "adamw_big_0")
    back = lambda a: jnp.swapaxes(a, 0, 1)[None]
    sh_g, sh_d, sh_nm, sh_nv = [
        [small_un[j][0], back(big_out[0][j]), small_un[j][1], big_out[1][j][None], back(big_out[2][j]),
         small_un[j][2], big_out[3][j][None]] for j in range(4)]

    def order(sh, rp):
        return [sh[0], rp[0], sh[1], sh[2], rp[1], rp[2], rp[3], sh[3], rp[4], sh[4], sh[5], rp[5], sh[6], rp[6]]

    return (loss, grad_x, *order(sh_g, rp_g), *order(sh_d, rp_d), *order(sh_nm, rp_nm), *order(sh_nv, rp_nv))
```
